```python
import jax, jax.numpy as jnp
from jax import lax
import numpy as np

D_MODEL = 1024
BATCH = 8
SEQ = 4096
DEPTH = 1

D_FF = 2816
POOL_WIDTH = D_MODEL // 2
POOL_WINDOWS = (2, 4, 8, 16)
N_POOL_GROUPS = len(POOL_WINDOWS)
POOL_GROUP = POOL_WIDTH // N_POOL_GROUPS
HEAD_DIM = 64
N_HEADS = D_MODEL // HEAD_DIM
N_KV_HEADS = N_HEADS // 8
GQA_GROUP = N_HEADS // N_KV_HEADS
WINDOW = 128
BLOCK = 128
ATTN_WIDTH = N_HEADS * HEAD_DIM
KV_WIDTH = N_KV_HEADS * HEAD_DIM
N_BRANCHES = 2
IN_WIDTH = POOL_WIDTH + ATTN_WIDTH + 2 * KV_WIDTH + N_BRANCHES * D_MODEL
RMS_EPS = 1e-6

kernel_name = "hybrid_pool_swa_sink_macaron_layer"


def rmsnorm(x, g):
    xf = x.astype(jnp.float32)
    y = xf * lax.rsqrt(jnp.mean(xf * xf, axis=-1, keepdims=True) + RMS_EPS)
    return (y * g.astype(jnp.float32)).astype(x.dtype)


def swiglu(x, w_gate, w_up, w_down):
    return (jax.nn.silu(x @ w_gate) * (x @ w_up)) @ w_down


def causal_pool_mixer(xp, pool_w, pool_scale):
    B, S, P = xp.shape
    xf = xp.astype(jnp.float32)
    csum = jnp.concatenate([jnp.zeros((B, 1, P), jnp.float32), jnp.cumsum(xf, axis=1)], axis=1)
    t = jnp.arange(S)
    pooled = []
    for gi, w in enumerate(POOL_WINDOWS):
        cg = csum[..., gi * POOL_GROUP:(gi + 1) * POOL_GROUP]
        start = jnp.maximum(t + 1 - w, 0)
        window_sum = cg[:, 1:, :] - cg[:, start, :]
        count = jnp.minimum(t + 1, w).astype(jnp.float32)
        pooled.append(window_sum / count[None, :, None])
    pooled = (jnp.concatenate(pooled, axis=-1) - xf).astype(xp.dtype)
    pooled = pooled.reshape(B, S, N_POOL_GROUPS, POOL_GROUP)
    mixed = jnp.einsum('bsgc,gcd->bsgd', pooled, pool_w).reshape(B, S, P)
    return mixed * pool_scale


def sliding_window_sink_attention(q, k, v, q_norm, k_norm, sinks):
    B, S = q.shape[0], q.shape[1]
    nb = S // BLOCK
    q = rmsnorm(q, q_norm)
    k = rmsnorm(k, k_norm)
    qb = q.reshape(B, nb, BLOCK, N_KV_HEADS, GQA_GROUP, HEAD_DIM)
    kb = k.reshape(B, nb, BLOCK, N_KV_HEADS, HEAD_DIM)
    vb = v.reshape(B, nb, BLOCK, N_KV_HEADS, HEAD_DIM)
    pad = ((0, 0), (1, 0), (0, 0), (0, 0), (0, 0))
    kk = jnp.concatenate([jnp.pad(kb, pad)[:, :-1], kb], axis=2)
    vv = jnp.concatenate([jnp.pad(vb, pad)[:, :-1], vb], axis=2)
    scores = jnp.einsum('bnqhgd,bnkhd->bnhgqk', qb, kk).astype(jnp.float32) * (HEAD_DIM ** -0.5)
    qi = jnp.arange(BLOCK)[:, None]
    kj = jnp.arange(2 * BLOCK)[None, :] - BLOCK
    rel = qi - kj
    band = (rel >= 0) & (rel < WINDOW)
    in_seq = (jnp.arange(nb)[:, None, None] > 0) | (kj[None] >= 0)
    mask = band[None] & in_seq
    scores = jnp.where(mask[None, :, None, None], scores, jnp.finfo(jnp.float32).min)
    sink = jnp.broadcast_to(sinks.astype(jnp.float32).reshape(1, 1, N_KV_HEADS, GQA_GROUP, 1, 1),
                            scores.shape[:-1] + (1,))
    probs = jax.nn.softmax(jnp.concatenate([scores, sink], axis=-1), axis=-1)[..., :-1]
    out = jnp.einsum('bnhgqk,bnkhd->bnqhgd', probs.astype(vv.dtype), vv)
    return out.reshape(B, S, ATTN_WIDTH)


def _fwd_setup_inputs(seed: int = 0) -> dict:
    key = jax.random.key(seed)
    ks = jax.random.split(key, 24)
    f32 = jnp.float32

    def nrm(k, shape, fan_in):
        return jax.random.normal(k, shape, f32) * (fan_in ** -0.5)

    def gain(k, shape):
        return jnp.ones(shape, f32) + 0.02 * jax.random.normal(k, shape, f32)

    return {
        "x": jax.random.normal(ks[0], (BATCH, SEQ, D_MODEL), f32),
        "ffn1_norm": gain(ks[1], (D_MODEL,)),
        "ffn1_w_gate": nrm(ks[2], (D_MODEL, D_FF), D_MODEL),
        "ffn1_w_up": nrm(ks[3], (D_MODEL, D_FF), D_MODEL),
        "ffn1_w_down": nrm(ks[4], (D_FF, D_MODEL), D_FF),
        "mix_norm": gain(ks[5], (D_MODEL,)),
        "w_in": nrm(ks[6], (D_MODEL, IN_WIDTH), D_MODEL),
        "pool_w": nrm(ks[7], (N_POOL_GROUPS, POOL_GROUP, POOL_GROUP), POOL_GROUP),
        "pool_scale": gain(ks[8], (POOL_WIDTH,)),
        "w_pool_out": nrm(ks[9], (POOL_WIDTH, D_MODEL), POOL_WIDTH),
        "q_norm": gain(ks[10], (HEAD_DIM,)),
        "k_norm": gain(ks[11], (HEAD_DIM,)),
        "sinks": 0.5 * jax.random.normal(ks[12], (N_HEADS,), f32),
        "w_attn_out": nrm(ks[13], (ATTN_WIDTH, D_MODEL), ATTN_WIDTH),
        "gate_bias": 0.01 * jax.random.normal(ks[14], (N_BRANCHES * D_MODEL,), f32),
        "w_out": nrm(ks[15], (D_MODEL, D_MODEL), D_MODEL),
        "ffn2_norm": gain(ks[16], (D_MODEL,)),
        "ffn2_w_gate": nrm(ks[17], (D_MODEL, D_FF), D_MODEL),
        "ffn2_w_up": nrm(ks[18], (D_MODEL, D_FF), D_MODEL),
        "ffn2_w_down": nrm(ks[19], (D_FF, D_MODEL), D_FF),
    }


def _fwd_reference(x, ffn1_norm, ffn1_w_gate, ffn1_w_up, ffn1_w_down, mix_norm, w_in, pool_w,
              pool_scale, w_pool_out, q_norm, k_norm, sinks, w_attn_out, gate_bias, w_out,
              ffn2_norm, ffn2_w_gate, ffn2_w_up, ffn2_w_down):
    B, S, _ = x.shape
    h = x
    for _layer in range(DEPTH):
        h = h + 0.5 * swiglu(rmsnorm(h, ffn1_norm), ffn1_w_gate, ffn1_w_up, ffn1_w_down)
        u = rmsnorm(h, mix_norm)
        proj = u @ w_in
        o = 0
        xp = proj[..., o:o + POOL_WIDTH]; o += POOL_WIDTH
        q = proj[..., o:o + ATTN_WIDTH].reshape(B, S, N_HEADS, HEAD_DIM); o += ATTN_WIDTH
        k = proj[..., o:o + KV_WIDTH].reshape(B, S, N_KV_HEADS, HEAD_DIM); o += KV_WIDTH
        v = proj[..., o:o + KV_WIDTH].reshape(B, S, N_KV_HEADS, HEAD_DIM); o += KV_WIDTH
        gates = jax.nn.sigmoid(proj[..., o:o + N_BRANCHES * D_MODEL] + gate_bias)
        gate_pool = gates[..., :D_MODEL]
        gate_attn = gates[..., D_MODEL:]
        branch_pool = causal_pool_mixer(xp, pool_w, pool_scale) @ w_pool_out
        branch_attn = sliding_window_sink_attention(q, k, v, q_norm, k_norm, sinks) @ w_attn_out
        merged = gate_pool * branch_pool + gate_attn * branch_attn
        h = h + merged @ w_out
        h = h + 0.5 * swiglu(rmsnorm(h, ffn2_norm), ffn2_w_gate, ffn2_w_up, ffn2_w_down)
    return h


import jax as _jax
import jax.numpy as _jnp

TWIN_FORMAT = 'train_step'
FWD_PARAMS = ['x', 'ffn1_norm', 'ffn1_w_gate', 'ffn1_w_up', 'ffn1_w_down', 'mix_norm', 'w_in', 'pool_w', 'pool_scale', 'w_pool_out', 'q_norm', 'k_norm', 'sinks', 'w_attn_out', 'gate_bias', 'w_out', 'ffn2_norm', 'ffn2_w_gate', 'ffn2_w_up', 'ffn2_w_down']
TWIN_WEIGHTS = ['ffn1_norm', 'ffn1_w_gate', 'ffn1_w_up', 'ffn1_w_down', 'mix_norm', 'w_in', 'pool_w', 'pool_scale', 'w_pool_out', 'q_norm', 'k_norm', 'sinks', 'w_attn_out', 'gate_bias', 'w_out', 'ffn2_norm', 'ffn2_w_gate', 'ffn2_w_up', 'ffn2_w_down']
TWIN_DIFF_INPUT = 'x'
TWIN_INPUTS = ['x', 'ffn1_norm', 'ffn1_w_gate', 'ffn1_w_up', 'ffn1_w_down', 'mix_norm', 'w_in', 'pool_w', 'pool_scale', 'w_pool_out', 'q_norm', 'k_norm', 'sinks', 'w_attn_out', 'gate_bias', 'w_out', 'ffn2_norm', 'ffn2_w_gate', 'ffn2_w_up', 'ffn2_w_down', 'loss_target', 'm_ffn1_norm', 'm_ffn1_w_gate', 'm_ffn1_w_up', 'm_ffn1_w_down', 'm_mix_norm', 'm_w_in', 'm_pool_w', 'm_pool_scale', 'm_w_pool_out', 'm_q_norm', 'm_k_norm', 'm_sinks', 'm_w_attn_out', 'm_gate_bias', 'm_w_out', 'm_ffn2_norm', 'm_ffn2_w_gate', 'm_ffn2_w_up', 'm_ffn2_w_down', 'v_ffn1_norm', 'v_ffn1_w_gate', 'v_ffn1_w_up', 'v_ffn1_w_down', 'v_mix_norm', 'v_w_in', 'v_pool_w', 'v_pool_scale', 'v_w_pool_out', 'v_q_norm', 'v_k_norm', 'v_sinks', 'v_w_attn_out', 'v_gate_bias', 'v_w_out', 'v_ffn2_norm', 'v_ffn2_w_gate', 'v_ffn2_w_up', 'v_ffn2_w_down']
TWIN_OUTPUTS = ['loss', 'grad_x', 'grad_ffn1_norm', 'grad_ffn1_w_gate', 'grad_ffn1_w_up', 'grad_ffn1_w_down', 'grad_mix_norm', 'grad_w_in', 'grad_pool_w', 'grad_pool_scale', 'grad_w_pool_out', 'grad_q_norm', 'grad_k_norm', 'grad_sinks', 'grad_w_attn_out', 'grad_gate_bias', 'grad_w_out', 'grad_ffn2_norm', 'grad_ffn2_w_gate', 'grad_ffn2_w_up', 'grad_ffn2_w_down', 'delta_ffn1_norm', 'delta_ffn1_w_gate', 'delta_ffn1_w_up', 'delta_ffn1_w_down', 'delta_mix_norm', 'delta_w_in', 'delta_pool_w', 'delta_pool_scale', 'delta_w_pool_out', 'delta_q_norm', 'delta_k_norm', 'delta_sinks', 'delta_w_attn_out', 'delta_gate_bias', 'delta_w_out', 'delta_ffn2_norm', 'delta_ffn2_w_gate', 'delta_ffn2_w_up', 'delta_ffn2_w_down', 'new_m_ffn1_norm', 'new_m_ffn1_w_gate', 'new_m_ffn1_w_up', 'new_m_ffn1_w_down', 'new_m_mix_norm', 'new_m_w_in', 'new_m_pool_w', 'new_m_pool_scale', 'new_m_w_pool_out', 'new_m_q_norm', 'new_m_k_norm', 'new_m_sinks', 'new_m_w_attn_out', 'new_m_gate_bias', 'new_m_w_out', 'new_m_ffn2_norm', 'new_m_ffn2_w_gate', 'new_m_ffn2_w_up', 'new_m_ffn2_w_down', 'new_v_ffn1_norm', 'new_v_ffn1_w_gate', 'new_v_ffn1_w_up', 'new_v_ffn1_w_down', 'new_v_mix_norm', 'new_v_w_in', 'new_v_pool_w', 'new_v_pool_scale', 'new_v_w_pool_out', 'new_v_q_norm', 'new_v_k_norm', 'new_v_sinks', 'new_v_w_attn_out', 'new_v_gate_bias', 'new_v_w_out', 'new_v_ffn2_norm', 'new_v_ffn2_w_gate', 'new_v_ffn2_w_up', 'new_v_ffn2_w_down']
TWIN_LEAF_KINDS = {'loss': 'loss', 'grad_x': 'grad_x', 'grad_ffn1_norm': 'grad_w', 'grad_ffn1_w_gate': 'grad_w', 'grad_ffn1_w_up': 'grad_w', 'grad_ffn1_w_down': 'grad_w', 'grad_mix_norm': 'grad_w', 'grad_w_in': 'grad_w', 'grad_pool_w': 'grad_w', 'grad_pool_scale': 'grad_w', 'grad_w_pool_out': 'grad_w', 'grad_q_norm': 'grad_w', 'grad_k_norm': 'grad_w', 'grad_sinks': 'grad_w', 'grad_w_attn_out': 'grad_w', 'grad_gate_bias': 'grad_w', 'grad_w_out': 'grad_w', 'grad_ffn2_norm': 'grad_w', 'grad_ffn2_w_gate': 'grad_w', 'grad_ffn2_w_up': 'grad_w', 'grad_ffn2_w_down': 'grad_w', 'delta_ffn1_norm': 'delta_w', 'delta_ffn1_w_gate': 'delta_w', 'delta_ffn1_w_up': 'delta_w', 'delta_ffn1_w_down': 'delta_w', 'delta_mix_norm': 'delta_w', 'delta_w_in': 'delta_w', 'delta_pool_w': 'delta_w', 'delta_pool_scale': 'delta_w', 'delta_w_pool_out': 'delta_w', 'delta_q_norm': 'delta_w', 'delta_k_norm': 'delta_w', 'delta_sinks': 'delta_w', 'delta_w_attn_out': 'delta_w', 'delta_gate_bias': 'delta_w', 'delta_w_out': 'delta_w', 'delta_ffn2_norm': 'delta_w', 'delta_ffn2_w_gate': 'delta_w', 'delta_ffn2_w_up': 'delta_w', 'delta_ffn2_w_down': 'delta_w', 'new_m_ffn1_norm': 'new_m', 'new_m_ffn1_w_gate': 'new_m', 'new_m_ffn1_w_up': 'new_m', 'new_m_ffn1_w_down': 'new_m', 'new_m_mix_norm': 'new_m', 'new_m_w_in': 'new_m', 'new_m_pool_w': 'new_m', 'new_m_pool_scale': 'new_m', 'new_m_w_pool_out': 'new_m', 'new_m_q_norm': 'new_m', 'new_m_k_norm': 'new_m', 'new_m_sinks': 'new_m', 'new_m_w_attn_out': 'new_m', 'new_m_gate_bias': 'new_m', 'new_m_w_out': 'new_m', 'new_m_ffn2_norm': 'new_m', 'new_m_ffn2_w_gate': 'new_m', 'new_m_ffn2_w_up': 'new_m', 'new_m_ffn2_w_down': 'new_m', 'new_v_ffn1_norm': 'new_v', 'new_v_ffn1_w_gate': 'new_v', 'new_v_ffn1_w_up': 'new_v', 'new_v_ffn1_w_down': 'new_v', 'new_v_mix_norm': 'new_v', 'new_v_w_in': 'new_v', 'new_v_pool_w': 'new_v', 'new_v_pool_scale': 'new_v', 'new_v_w_pool_out': 'new_v', 'new_v_q_norm': 'new_v', 'new_v_k_norm': 'new_v', 'new_v_sinks': 'new_v', 'new_v_w_attn_out': 'new_v', 'new_v_gate_bias': 'new_v', 'new_v_w_out': 'new_v', 'new_v_ffn2_norm': 'new_v', 'new_v_ffn2_w_gate': 'new_v', 'new_v_ffn2_w_up': 'new_v', 'new_v_ffn2_w_down': 'new_v'}


def _forward(args):
    return _fwd_reference(*[args[k] for k in FWD_PARAMS])


def _output_shape():
    def fwd():
        inp = _fwd_setup_inputs(0)
        return _fwd_reference(*[inp[k] for k in FWD_PARAMS])
    out = _jax.eval_shape(fwd)
    return out.shape, out.dtype

N_MICROBATCH = 1
ADAM_LR = 0.001
ADAM_B1 = 0.9
ADAM_B2 = 0.999
ADAM_EPS = 1e-08
ADAM_WD = 0.01
ADAM_STEP = 10
PER_EXAMPLE_BATCH_AXIS = {'x': 0, 'loss_target': 0}
SHARED_INPUTS = []
_WEIGHT_DTYPES = {'ffn1_norm': _jnp.float32, 'ffn1_w_gate': _jnp.float32, 'ffn1_w_up': _jnp.float32, 'ffn1_w_down': _jnp.float32, 'mix_norm': _jnp.float32, 'w_in': _jnp.float32, 'pool_w': _jnp.float32, 'pool_scale': _jnp.float32, 'w_pool_out': _jnp.float32, 'q_norm': _jnp.float32, 'k_norm': _jnp.float32, 'sinks': _jnp.float32, 'w_attn_out': _jnp.float32, 'gate_bias': _jnp.float32, 'w_out': _jnp.float32, 'ffn2_norm': _jnp.float32, 'ffn2_w_gate': _jnp.float32, 'ffn2_w_up': _jnp.float32, 'ffn2_w_down': _jnp.float32}
MOMENT_SCALE = {'ffn1_norm': 6.180956e+00, 'ffn1_w_gate': 7.160844e-02, 'ffn1_w_up': 8.755621e-02, 'ffn1_w_down': 1.458714e-01, 'mix_norm': 8.123698e+00, 'w_in': 3.076749e-01, 'pool_w': 1.586896e+00, 'pool_scale': 1.473375e+01, 'w_pool_out': 7.231841e-01, 'q_norm': 2.048663e+00, 'k_norm': 2.053583e+00, 'sinks': 3.165738e-01, 'w_attn_out': 3.753521e-02, 'gate_bias': 1.800034e+00, 'w_out': 5.784961e-01, 'ffn2_norm': 6.151655e+00, 'ffn2_w_gate': 6.210794e-02, 'ffn2_w_up': 8.482397e-02, 'ffn2_w_down': 1.362981e-01}


def _to_microbatches(a, axis):
    t = _jnp.moveaxis(a, axis, 0)
    t = t.reshape((N_MICROBATCH, t.shape[0] // N_MICROBATCH) + t.shape[1:])
    return _jnp.moveaxis(t, 1, axis + 1)


def setup_inputs(seed: int = 0) -> dict:
    inp = _fwd_setup_inputs(seed)
    key = _jax.random.fold_in(_jax.random.key(seed), 7919)
    shape, _ = _output_shape()
    out = dict(inp)
    out["loss_target"] = _jax.random.normal(_jax.random.fold_in(key, 0), shape, _jnp.float32)
    for i, name in enumerate(TWIN_WEIGHTS):
        w = inp[name].astype(_jnp.float32)
        if MOMENT_SCALE is None:
            s = _jnp.sqrt(_jnp.mean(_jnp.square(w)) + 1e-30)
        else:
            s = MOMENT_SCALE[name]
        km, kv = _jax.random.split(_jax.random.fold_in(key, i + 1))
        out[name] = w
        out["m_" + name] = s * _jax.random.normal(km, w.shape, _jnp.float32)
        out["v_" + name] = (s * s) * _jax.random.uniform(kv, w.shape, _jnp.float32, 0.5, 1.5)
    if N_MICROBATCH > 1:
        for name, axis in PER_EXAMPLE_BATCH_AXIS.items():
            out[name] = _to_microbatches(out[name], axis)
    return {'x': out['x'], 'ffn1_norm': out['ffn1_norm'], 'ffn1_w_gate': out['ffn1_w_gate'], 'ffn1_w_up': out['ffn1_w_up'], 'ffn1_w_down': out['ffn1_w_down'], 'mix_norm': out['mix_norm'], 'w_in': out['w_in'], 'pool_w': out['pool_w'], 'pool_scale': out['pool_scale'], 'w_pool_out': out['w_pool_out'], 'q_norm': out['q_norm'], 'k_norm': out['k_norm'], 'sinks': out['sinks'], 'w_attn_out': out['w_attn_out'], 'gate_bias': out['gate_bias'], 'w_out': out['w_out'], 'ffn2_norm': out['ffn2_norm'], 'ffn2_w_gate': out['ffn2_w_gate'], 'ffn2_w_up': out['ffn2_w_up'], 'ffn2_w_down': out['ffn2_w_down'], 'loss_target': out['loss_target'], 'm_ffn1_norm': out['m_ffn1_norm'], 'm_ffn1_w_gate': out['m_ffn1_w_gate'], 'm_ffn1_w_up': out['m_ffn1_w_up'], 'm_ffn1_w_down': out['m_ffn1_w_down'], 'm_mix_norm': out['m_mix_norm'], 'm_w_in': out['m_w_in'], 'm_pool_w': out['m_pool_w'], 'm_pool_scale': out['m_pool_scale'], 'm_w_pool_out': out['m_w_pool_out'], 'm_q_norm': out['m_q_norm'], 'm_k_norm': out['m_k_norm'], 'm_sinks': out['m_sinks'], 'm_w_attn_out': out['m_w_attn_out'], 'm_gate_bias': out['m_gate_bias'], 'm_w_out': out['m_w_out'], 'm_ffn2_norm': out['m_ffn2_norm'], 'm_ffn2_w_gate': out['m_ffn2_w_gate'], 'm_ffn2_w_up': out['m_ffn2_w_up'], 'm_ffn2_w_down': out['m_ffn2_w_down'], 'v_ffn1_norm': out['v_ffn1_norm'], 'v_ffn1_w_gate': out['v_ffn1_w_gate'], 'v_ffn1_w_up': out['v_ffn1_w_up'], 'v_ffn1_w_down': out['v_ffn1_w_down'], 'v_mix_norm': out['v_mix_norm'], 'v_w_in': out['v_w_in'], 'v_pool_w': out['v_pool_w'], 'v_pool_scale': out['v_pool_scale'], 'v_w_pool_out': out['v_w_pool_out'], 'v_q_norm': out['v_q_norm'], 'v_k_norm': out['v_k_norm'], 'v_sinks': out['v_sinks'], 'v_w_attn_out': out['v_w_attn_out'], 'v_gate_bias': out['v_gate_bias'], 'v_w_out': out['v_w_out'], 'v_ffn2_norm': out['v_ffn2_norm'], 'v_ffn2_w_gate': out['v_ffn2_w_gate'], 'v_ffn2_w_up': out['v_ffn2_w_up'], 'v_ffn2_w_down': out['v_ffn2_w_down']}


def _loss(weights, diff, rest, loss_target):
    with _jax.named_scope("forward"):
        args = {**rest, TWIN_DIFF_INPUT: diff, **{k: w.astype(_WEIGHT_DTYPES[k]) for k, w in weights.items()}}
        y = _forward(args)
    with _jax.named_scope("loss_head"):
        err = _jnp.square(y.astype(_jnp.float32) - loss_target)
        return 0.5 * _jnp.sum(_jnp.mean(err, axis=-1)) if err.ndim else 0.5 * err


def _adamw(w, g, m, v):
    m = ADAM_B1 * m + (1.0 - ADAM_B1) * g
    v = ADAM_B2 * v + (1.0 - ADAM_B2) * _jnp.square(g)
    m_hat = m / (1.0 - ADAM_B1 ** ADAM_STEP)
    v_hat = v / (1.0 - ADAM_B2 ** ADAM_STEP)
    delta = -ADAM_LR * (m_hat / (_jnp.sqrt(v_hat) + ADAM_EPS) + ADAM_WD * w)
    return delta, m, v


def reference(x, ffn1_norm, ffn1_w_gate, ffn1_w_up, ffn1_w_down, mix_norm, w_in, pool_w, pool_scale, w_pool_out, q_norm, k_norm, sinks, w_attn_out, gate_bias, w_out, ffn2_norm, ffn2_w_gate, ffn2_w_up, ffn2_w_down, loss_target, m_ffn1_norm, m_ffn1_w_gate, m_ffn1_w_up, m_ffn1_w_down, m_mix_norm, m_w_in, m_pool_w, m_pool_scale, m_w_pool_out, m_q_norm, m_k_norm, m_sinks, m_w_attn_out, m_gate_bias, m_w_out, m_ffn2_norm, m_ffn2_w_gate, m_ffn2_w_up, m_ffn2_w_down, v_ffn1_norm, v_ffn1_w_gate, v_ffn1_w_up, v_ffn1_w_down, v_mix_norm, v_w_in, v_pool_w, v_pool_scale, v_w_pool_out, v_q_norm, v_k_norm, v_sinks, v_w_attn_out, v_gate_bias, v_w_out, v_ffn2_norm, v_ffn2_w_gate, v_ffn2_w_up, v_ffn2_w_down):
    given = dict(x=x, ffn1_norm=ffn1_norm, ffn1_w_gate=ffn1_w_gate, ffn1_w_up=ffn1_w_up, ffn1_w_down=ffn1_w_down, mix_norm=mix_norm, w_in=w_in, pool_w=pool_w, pool_scale=pool_scale, w_pool_out=w_pool_out, q_norm=q_norm, k_norm=k_norm, sinks=sinks, w_attn_out=w_attn_out, gate_bias=gate_bias, w_out=w_out, ffn2_norm=ffn2_norm, ffn2_w_gate=ffn2_w_gate, ffn2_w_up=ffn2_w_up, ffn2_w_down=ffn2_w_down, loss_target=loss_target, m_ffn1_norm=m_ffn1_norm, m_ffn1_w_gate=m_ffn1_w_gate, m_ffn1_w_up=m_ffn1_w_up, m_ffn1_w_down=m_ffn1_w_down, m_mix_norm=m_mix_norm, m_w_in=m_w_in, m_pool_w=m_pool_w, m_pool_scale=m_pool_scale, m_w_pool_out=m_w_pool_out, m_q_norm=m_q_norm, m_k_norm=m_k_norm, m_sinks=m_sinks, m_w_attn_out=m_w_attn_out, m_gate_bias=m_gate_bias, m_w_out=m_w_out, m_ffn2_norm=m_ffn2_norm, m_ffn2_w_gate=m_ffn2_w_gate, m_ffn2_w_up=m_ffn2_w_up, m_ffn2_w_down=m_ffn2_w_down, v_ffn1_norm=v_ffn1_norm, v_ffn1_w_gate=v_ffn1_w_gate, v_ffn1_w_up=v_ffn1_w_up, v_ffn1_w_down=v_ffn1_w_down, v_mix_norm=v_mix_norm, v_w_in=v_w_in, v_pool_w=v_pool_w, v_pool_scale=v_pool_scale, v_w_pool_out=v_w_pool_out, v_q_norm=v_q_norm, v_k_norm=v_k_norm, v_sinks=v_sinks, v_w_attn_out=v_w_attn_out, v_gate_bias=v_gate_bias, v_w_out=v_w_out, v_ffn2_norm=v_ffn2_norm, v_ffn2_w_gate=v_ffn2_w_gate, v_ffn2_w_up=v_ffn2_w_up, v_ffn2_w_down=v_ffn2_w_down)
    weights = {n: given[n] for n in TWIN_WEIGHTS}
    shared = {n: given[n] for n in SHARED_INPUTS}
    per_example = {n: given[n] for n in ['x']}
    grad_fn = _jax.value_and_grad(_loss, argnums=(0, 1))

    def one_microbatch(ex, loss_target):
        ex = dict(ex)
        diff = ex.pop(TWIN_DIFF_INPUT)
        return grad_fn(weights, diff, {**shared, **ex}, loss_target)

    if N_MICROBATCH == 1:
        loss, (grad_w, grad_x) = one_microbatch(per_example, given["loss_target"])
    else:
        def body(carry, xs):
            loss_sum, grad_sum = carry
            l_k, (gw_k, gx_k) = one_microbatch(xs[0], xs[1])
            with _jax.named_scope("update"):
                return (loss_sum + l_k, _jax.tree.map(_jnp.add, grad_sum, gw_k)), gx_k

        init = (_jnp.zeros((), _jnp.float32), _jax.tree.map(_jnp.zeros_like, weights))
        (loss, grad_w), grad_x = _jax.lax.scan(body, init, (per_example, given["loss_target"]))
    with _jax.named_scope("update"):
        delta_w, new_m, new_v = {}, {}, {}
        for n in TWIN_WEIGHTS:
            delta_w[n], new_m[n], new_v[n] = _adamw(weights[n], grad_w[n], given["m_" + n], given["v_" + n])
    return (loss, grad_x, *[grad_w[n] for n in TWIN_WEIGHTS], *[delta_w[n] for n in TWIN_WEIGHTS],
            *[new_m[n] for n in TWIN_WEIGHTS], *[new_v[n] for n in TWIN_WEIGHTS])
```

```python
import functools

import jax
import jax.numpy as jnp
from jax import lax
from jax.experimental import pallas as pl
from jax.experimental.pallas import tpu as pltpu

F32 = jnp.float32
BF16 = jnp.bfloat16
RMS_EPS = 1e-6
POOL_WINDOWS = (2, 4, 8, 16)
POOL_GROUP = 128
POOL_HALO = 16
HEAD_DIM = 64
GQA_GROUP = 8
N_KV_HEADS = 2
ATTN_BLOCK = 128
SCALE = HEAD_DIM ** -0.5
NEG = -1e30
N_SHARDS = 4
ADAM_LR, ADAM_B1, ADAM_B2, ADAM_EPS, ADAM_WD, ADAM_STEP = 0.001, 0.9, 0.999, 1e-08, 0.01, 10
VMEM_LIMIT = 56 * 1024 * 1024
MESH = pl.DeviceIdType.MESH
SEG_POOL, SEG_Q, SEG_KV, SEG_GATE = (0, 512), (512, 1024), (1536, 256), (1792, 2048)


def _params(**kw):
    return pltpu.CompilerParams(vmem_limit_bytes=VMEM_LIMIT, **kw)


def _dot(a, b):
    return jnp.dot(a, b, preferred_element_type=F32)


def _dot_nt(a, b):
    return lax.dot_general(a, b, (((1,), (1,)), ((), ())), preferred_element_type=F32)


def _dot_tn(a, b):
    return lax.dot_general(a, b, (((0,), (0,)), ((), ())), preferred_element_type=F32)


def _rinv(x):
    return lax.rsqrt(jnp.mean(x * x, axis=-1, keepdims=True) + RMS_EPS)


def _rms_bwd(dn, x, g):
    r = _rinv(x)
    xr = x * r
    z = dn * g
    dx = r * (z - xr * jnp.mean(z * xr, axis=-1, keepdims=True))
    return dx, dn * xr


def _acc(ref, val, first):
    @pl.when(first)
    def _():
        ref[...] = val

    @pl.when(jnp.logical_not(first))
    def _():
        ref[...] += val


def _ffn_fwd(h, gnorm, wg, wu, wd, target, name, tm=512):
    S, D = h.shape
    NS, _, FS = wg.shape
    with_loss = target is not None
    nt = S // tm

    def body(*refs):
        if with_loss:
            h_ref, g_ref, wg_ref, wu_ref, wd_ref, t_ref, n_ref, a_ref, b_ref, dy_ref, dyh_ref, loss_ref, acc_ref = refs
        else:
            h_ref, g_ref, wg_ref, wu_ref, wd_ref, n_ref, a_ref, b_ref, ho_ref, acc_ref = refs
        i = pl.program_id(0)
        j = pl.program_id(1)

        @pl.when(j == 0)
        def _():
            x = h_ref[...]
            n_ref[...] = (x * _rinv(x) * g_ref[...]).astype(BF16)

        n = n_ref[...]
        a = _dot(n, wg_ref[0])
        b = _dot(n, wu_ref[0])
        a_ref[0] = a.astype(BF16)
        b_ref[0] = b.astype(BF16)
        s = (a * jax.nn.sigmoid(a) * b).astype(BF16)
        _acc(acc_ref, _dot(s, wd_ref[0]), j == 0)

        @pl.when(j == NS - 1)
        def _():
            y = h_ref[...] + 0.5 * acc_ref[...]
            if with_loss:
                e = y - t_ref[...]
                dy = e * (1.0 / D)
                dy_ref[...] = dy
                dyh_ref[...] = (0.5 * dy).astype(BF16)
                tot = jnp.sum(jnp.sum(e * e, axis=1, keepdims=True), axis=0, keepdims=True) * (0.5 / D)
                _acc(loss_ref, jnp.broadcast_to(tot, loss_ref.shape), i == 0)
            else:
                ho_ref[...] = y

    row = pl.BlockSpec((tm, D), lambda i, j: (i, 0))
    wcol = pl.BlockSpec((1, D, FS), lambda i, j: (j, 0, 0))
    wrow = pl.BlockSpec((1, FS, D), lambda i, j: (j, 0, 0))
    act = pl.BlockSpec((1, tm, FS), lambda i, j: (j, i, 0))
    in_specs = [row, pl.BlockSpec((1, D), lambda i, j: (0, 0)), wcol, wcol, wrow]
    out_specs = [row, act, act]
    out_shape = [jax.ShapeDtypeStruct((S, D), BF16), jax.ShapeDtypeStruct((NS, S, FS), BF16),
                 jax.ShapeDtypeStruct((NS, S, FS), BF16)]
    args = [h, gnorm, wg, wu, wd]
    if with_loss:
        in_specs.append(row)
        args.append(target)
        out_specs += [row, row, pl.BlockSpec((8, 128), lambda i, j: (0, 0))]
        out_shape += [jax.ShapeDtypeStruct((S, D), F32), jax.ShapeDtypeStruct((S, D), BF16),
                      jax.ShapeDtypeStruct((8, 128), F32)]
    else:
        out_specs.append(row)
        out_shape.append(jax.ShapeDtypeStruct((S, D), F32))
    return pl.pallas_call(
        body, name=name, grid=(nt, NS), in_specs=in_specs, out_specs=out_specs, out_shape=out_shape,
        scratch_shapes=[pltpu.VMEM((tm, D), F32)],
        compiler_params=_params(dimension_semantics=("arbitrary", "arbitrary")),
    )(*args)


def _ffn_bwd_w(dyh, n, a, b, wd, name, tm=256):
    S, D = n.shape
    NS, _, FS = a.shape
    nt = S // tm

    def body(dyh_ref, n_ref, a_ref, b_ref, wd_ref, da_ref, db_ref, dwg_ref, dwu_ref, dwd_ref):
        i = pl.program_id(1)
        dyh_t = dyh_ref[...]
        n_t = n_ref[...]
        av = a_ref[0].astype(F32)
        bv = b_ref[0].astype(F32)
        ds = _dot_nt(dyh_t, wd_ref[0])
        sig = jax.nn.sigmoid(av)
        silu = av * sig
        da = (ds * bv * (sig * (1.0 + av * (1.0 - sig)))).astype(BF16)
        db = (ds * silu).astype(BF16)
        s = (silu * bv).astype(BF16)
        da_ref[0] = da
        db_ref[0] = db
        _acc(dwg_ref.at[0], _dot_tn(n_t, da), i == 0)
        _acc(dwu_ref.at[0], _dot_tn(n_t, db), i == 0)
        _acc(dwd_ref.at[0], _dot_tn(s, dyh_t), i == 0)

    row = pl.BlockSpec((tm, D), lambda j, i: (i, 0))
    act = pl.BlockSpec((1, tm, FS), lambda j, i: (j, i, 0))
    wcol = pl.BlockSpec((1, D, FS), lambda j, i: (j, 0, 0))
    wrow = pl.BlockSpec((1, FS, D), lambda j, i: (j, 0, 0))
    return pl.pallas_call(
        body, name=name, grid=(NS, nt), in_specs=[row, row, act, act, wrow],
        out_specs=[act, act, wcol, wcol, wrow],
        out_shape=[jax.ShapeDtypeStruct((NS, S, FS), BF16), jax.ShapeDtypeStruct((NS, S, FS), BF16),
                   jax.ShapeDtypeStruct((NS, D, FS), F32), jax.ShapeDtypeStruct((NS, D, FS), F32),
                   jax.ShapeDtypeStruct((NS, FS, D), F32)],
        compiler_params=_params(dimension_semantics=("arbitrary", "arbitrary")),
    )(dyh, n, a, b, wd)


def _ffn_bwd_x(da, db, wg, wu, x, dh, gnorm, name, tm=512):
    S, D = x.shape
    NS, _, FS = da.shape
    nt = S // tm

    def body(da_ref, db_ref, wg_ref, wu_ref, x_ref, dh_ref, g_ref, dx_ref, dxh_ref, dg_ref, acc_ref):
        i = pl.program_id(0)
        j = pl.program_id(1)
        _acc(acc_ref, _dot_nt(da_ref[0], wg_ref[0]) + _dot_nt(db_ref[0], wu_ref[0]), j == 0)

        @pl.when(j == NS - 1)
        def _():
            dx, dg_rows = _rms_bwd(acc_ref[...], x_ref[...], g_ref[...])
            out = dh_ref[...] + dx
            dx_ref[...] = out
            dxh_ref[...] = (0.5 * out).astype(BF16)
            _acc(dg_ref, jnp.sum(dg_rows, axis=0, keepdims=True), i == 0)

    row = pl.BlockSpec((tm, D), lambda i, j: (i, 0))
    act = pl.BlockSpec((1, tm, FS), lambda i, j: (j, i, 0))
    wcol = pl.BlockSpec((1, D, FS), lambda i, j: (j, 0, 0))
    vec = pl.BlockSpec((1, D), lambda i, j: (0, 0))
    return pl.pallas_call(
        body, name=name, grid=(nt, NS), in_specs=[act, act, wcol, wcol, row, row, vec],
        out_specs=[row, row, vec],
        out_shape=[jax.ShapeDtypeStruct((S, D), F32), jax.ShapeDtypeStruct((S, D), BF16),
                   jax.ShapeDtypeStruct((1, D), F32)],
        scratch_shapes=[pltpu.VMEM((tm, D), F32)],
        compiler_params=_params(dimension_semantics=("arbitrary", "arbitrary")),
    )(da, db, wg, wu, x, dh, gnorm)


def _mix_proj(h, gnorm, win, tm=512):
    S, D = h.shape
    nt = S // tm

    def body(h_ref, g_ref, w_ref, u_ref, xp_ref, q_ref, kv_ref, gl_ref):
        x = h_ref[...]
        u = (x * _rinv(x) * g_ref[...]).astype(BF16)
        u_ref[...] = u
        for (off, width), ref in ((SEG_POOL, xp_ref), (SEG_Q, q_ref), (SEG_KV, kv_ref), (SEG_GATE, gl_ref)):
            ref[...] = _dot(u, w_ref[:, off:off + width]).astype(ref.dtype)

    def row(width):
        return pl.BlockSpec((tm, width), lambda i: (i, 0))

    return pl.pallas_call(
        body, name="mix_proj", grid=(nt,),
        in_specs=[row(D), pl.BlockSpec((1, D), lambda i: (0, 0)), pl.BlockSpec(win.shape, lambda i: (0, 0))],
        out_specs=[row(D), row(SEG_POOL[1]), row(SEG_Q[1]), row(SEG_KV[1]), row(SEG_GATE[1])],
        out_shape=[jax.ShapeDtypeStruct((S, D), BF16), jax.ShapeDtypeStruct((S, SEG_POOL[1]), F32),
                   jax.ShapeDtypeStruct((S, SEG_Q[1]), BF16), jax.ShapeDtypeStruct((S, SEG_KV[1]), BF16),
                   jax.ShapeDtypeStruct((S, SEG_GATE[1]), BF16)],
        compiler_params=_params(dimension_semantics=("arbitrary",)),
    )(h, gnorm, win)


def _stack_heads(x, g):
    return jnp.concatenate([x[:, (GQA_GROUP * g + hh) * HEAD_DIM:(GQA_GROUP * g + hh + 1) * HEAD_DIM]
                            for hh in range(GQA_GROUP)], axis=0)


def _unstack_heads(ref, val, g):
    for hh in range(GQA_GROUP):
        lo = (GQA_GROUP * g + hh) * HEAD_DIM
        ref[:, lo:lo + HEAD_DIM] = val[hh * ATTN_BLOCK:(hh + 1) * ATTN_BLOCK, :]


def _attn_probs(qn, kc, kp, sink, n):
    rows = GQA_GROUP * ATTN_BLOCK
    qi = lax.broadcasted_iota(jnp.int32, (rows, ATTN_BLOCK), 0) % ATTN_BLOCK
    kj = lax.broadcasted_iota(jnp.int32, (rows, ATTN_BLOCK), 1)
    s_c = jnp.where(kj <= qi, _dot_nt(qn, kc), NEG)
    s_p = jnp.where(jnp.logical_and(kj > qi, n > 0), _dot_nt(qn, kp), NEG)
    m = jnp.maximum(jnp.maximum(jnp.max(s_c, axis=-1, keepdims=True), jnp.max(s_p, axis=-1, keepdims=True)), sink)
    e_c = jnp.exp(s_c - m)
    e_p = jnp.exp(s_p - m)
    e_s = jnp.exp(sink - m)
    inv = 1.0 / (jnp.sum(e_c, axis=-1, keepdims=True) + jnp.sum(e_p, axis=-1, keepdims=True) + e_s)
    return e_c * inv, e_p * inv, e_s * inv


def _attn_blocks(kv_ref, n):
    cur = pl.multiple_of(n * ATTN_BLOCK, ATTN_BLOCK)
    prev = pl.multiple_of(jnp.maximum(n - 1, 0) * ATTN_BLOCK, ATTN_BLOCK)
    return cur, prev


def _kv_split(kv, g):
    k = kv[:, g * HEAD_DIM:(g + 1) * HEAD_DIM]
    v = kv[:, (N_KV_HEADS + g) * HEAD_DIM:(N_KV_HEADS + g + 1) * HEAD_DIM]
    return k, v


def _attn_fwd(q, kv, qw, kw, sink_rows):
    S, W = q.shape
    nb = S // ATTN_BLOCK

    def body(q_ref, kv_ref, qw_ref, kw_ref, sk_ref, o_ref, o_scr):
        n = pl.program_id(0)
        qf = q_ref[...].astype(F32)
        cur, prev = _attn_blocks(kv_ref, n)
        kvc = kv_ref[pl.ds(cur, ATTN_BLOCK), :].astype(F32)
        kvp = kv_ref[pl.ds(prev, ATTN_BLOCK), :].astype(F32)
        for g in range(N_KV_HEADS):
            qs = _stack_heads(qf, g)
            qn = (qs * _rinv(qs) * qw_ref[...] * SCALE).astype(BF16)
            kc, vc = _kv_split(kvc, g)
            kp, vp = _kv_split(kvp, g)
            kc = (kc * _rinv(kc) * kw_ref[...]).astype(BF16)
            kp = (kp * _rinv(kp) * kw_ref[...]).astype(BF16)
            p_c, p_p, _ = _attn_probs(qn, kc, kp, sk_ref[g], n)
            o = _dot(p_c.astype(BF16), vc.astype(BF16)) + _dot(p_p.astype(BF16), vp.astype(BF16))
            _unstack_heads(o_scr, o, g)
        o_ref[...] = o_scr[...].astype(BF16)

    blk = pl.BlockSpec((ATTN_BLOCK, W), lambda n: (n, 0))
    return pl.pallas_call(
        body, name="attn_fwd", grid=(nb,),
        in_specs=[blk, pl.BlockSpec(kv.shape, lambda n: (0, 0)), pl.BlockSpec((1, HEAD_DIM), lambda n: (0, 0)),
                  pl.BlockSpec((1, HEAD_DIM), lambda n: (0, 0)), pl.BlockSpec(sink_rows.shape, lambda n: (0, 0, 0))],
        out_specs=blk, out_shape=jax.ShapeDtypeStruct((S, W), BF16),
        scratch_shapes=[pltpu.VMEM((ATTN_BLOCK, W), F32)],
        compiler_params=_params(dimension_semantics=("arbitrary",)),
    )(q, kv, qw, kw, sink_rows)


def _attn_bwd(q, kv, do, qw, kw, sink_rows):
    S, W = q.shape
    KW = kv.shape[1]
    nb = S // ATTN_BLOCK
    chunk = 512

    def body(q_ref, kv_ref, do_ref, qw_ref, kw_ref, sk_ref, dq_ref, dkv_ref, dqw_ref, dkw_ref, dsk_ref, dq_scr):
        n = pl.program_id(0)

        @pl.when(n == 0)
        def _():
            dkv_ref[...] = jnp.zeros_like(dkv_ref)
            dqw_ref[...] = jnp.zeros_like(dqw_ref)
            dsk_ref[...] = jnp.zeros_like(dsk_ref)

        qf = q_ref[...].astype(F32)
        dof = do_ref[...].astype(F32)
        cur, prev = _attn_blocks(kv_ref, n)
        kvc = kv_ref[pl.ds(cur, ATTN_BLOCK), :].astype(F32)
        kvp = kv_ref[pl.ds(prev, ATTN_BLOCK), :].astype(F32)
        qw_v = qw_ref[...]
        for g in range(N_KV_HEADS):
            qs = _stack_heads(qf, g)
            qhat = qs * _rinv(qs)
            qn = (qhat * qw_v * SCALE).astype(BF16)
            kc, vc = _kv_split(kvc, g)
            kp, vp = _kv_split(kvp, g)
            kc = (kc * _rinv(kc) * kw_ref[...]).astype(BF16)
            kp = (kp * _rinv(kp) * kw_ref[...]).astype(BF16)
            vc = vc.astype(BF16)
            vp = vp.astype(BF16)
            dos = _stack_heads(dof, g).astype(BF16)
            p_c, p_p, p_s = _attn_probs(qn, kc, kp, sk_ref[g], n)
            dp_c = _dot_nt(dos, vc)
            dp_p = _dot_nt(dos, vp)
            drow = jnp.sum(p_c * dp_c, axis=-1, keepdims=True) + jnp.sum(p_p * dp_p, axis=-1, keepdims=True)
            ds_c = (p_c * (dp_c - drow)).astype(BF16)
            ds_p = (p_p * (dp_p - drow)).astype(BF16)
            dsink = -(p_s * drow)
            for hh in range(GQA_GROUP):
                tot = jnp.sum(dsink[hh * ATTN_BLOCK:(hh + 1) * ATTN_BLOCK, :], axis=0, keepdims=True)
                dsk_ref[g, hh:hh + 1, :] += jnp.broadcast_to(tot, (1, 128))
            dqn = (_dot(ds_c, kc) + _dot(ds_p, kp)) * SCALE
            klo, vlo = g * HEAD_DIM, (N_KV_HEADS + g) * HEAD_DIM
            dkv_ref[pl.ds(cur, ATTN_BLOCK), klo:klo + HEAD_DIM] += _dot_tn(ds_c, qn)
            dkv_ref[pl.ds(prev, ATTN_BLOCK), klo:klo + HEAD_DIM] += _dot_tn(ds_p, qn)
            dkv_ref[pl.ds(cur, ATTN_BLOCK), vlo:vlo + HEAD_DIM] += _dot_tn(p_c.astype(BF16), dos)
            dkv_ref[pl.ds(prev, ATTN_BLOCK), vlo:vlo + HEAD_DIM] += _dot_tn(p_p.astype(BF16), dos)
            dqw_ref[...] += jnp.sum(dqn * qhat, axis=0, keepdims=True)
            z = dqn * qw_v
            dqs = _rinv(qs) * (z - qhat * jnp.mean(z * qhat, axis=-1, keepdims=True))
            _unstack_heads(dq_scr, dqs, g)
        dq_ref[...] = dq_scr[...].astype(BF16)

        @pl.when(n == nb - 1)
        def _():
            def one(c, dkw):
                rows = pl.ds(pl.multiple_of(c * chunk, chunk), chunk)
                for g in range(N_KV_HEADS):
                    lo = g * HEAD_DIM
                    k = kv_ref[rows, lo:lo + HEAD_DIM].astype(F32)
                    dx, dg_rows = _rms_bwd(dkv_ref[rows, lo:lo + HEAD_DIM], k, kw_ref[...])
                    dkv_ref[rows, lo:lo + HEAD_DIM] = dx
                    dkw = dkw + jnp.sum(dg_rows, axis=0, keepdims=True)
                return dkw

            dkw_ref[...] = lax.fori_loop(0, S // chunk, one, jnp.zeros((1, HEAD_DIM), F32))

    blk = pl.BlockSpec((ATTN_BLOCK, W), lambda n: (n, 0))
    whole_kv = pl.BlockSpec((S, KW), lambda n: (0, 0))
    vec = pl.BlockSpec((1, HEAD_DIM), lambda n: (0, 0))
    sk = pl.BlockSpec(sink_rows.shape, lambda n: (0, 0, 0))
    dsk = pl.BlockSpec((N_KV_HEADS, GQA_GROUP, 128), lambda n: (0, 0, 0))
    return pl.pallas_call(
        body, name="attn_bwd", grid=(nb,), in_specs=[blk, whole_kv, blk, vec, vec, sk],
        out_specs=[blk, whole_kv, vec, vec, dsk],
        out_shape=[jax.ShapeDtypeStruct((S, W), BF16), jax.ShapeDtypeStruct((S, KW), F32),
                   jax.ShapeDtypeStruct((1, HEAD_DIM), F32), jax.ShapeDtypeStruct((1, HEAD_DIM), F32),
                   jax.ShapeDtypeStruct((N_KV_HEADS, GQA_GROUP, 128), F32)],
        scratch_shapes=[pltpu.VMEM((ATTN_BLOCK, W), F32)],
        compiler_params=_params(dimension_semantics=("arbitrary",)),
    )(q, kv, do, qw, kw, sink_rows)


def _pooled(xc, xprev, i):
    tm = xc.shape[0]
    xh = jnp.concatenate([jnp.where(i > 0, xprev, 0.0), xc], axis=0)
    t = lax.broadcasted_iota(jnp.int32, (tm, 1), 0) + i * tm
    out = []
    for gi, w in enumerate(POOL_WINDOWS):
        acc = xh[:, gi * POOL_GROUP:(gi + 1) * POOL_GROUP]
        sh = 1
        while sh < w:
            acc = acc + pltpu.roll(acc, sh, 0)
            sh *= 2
        cnt = jnp.minimum(t + 1, w).astype(F32)
        out.append(acc[POOL_HALO:, :] / cnt - xc[:, gi * POOL_GROUP:(gi + 1) * POOL_GROUP])
    return jnp.concatenate(out, axis=1)


def _pool_mix(pooled_b, pw_ref):
    return jnp.concatenate([_dot(pooled_b[:, gi * POOL_GROUP:(gi + 1) * POOL_GROUP], pw_ref[gi])
                            for gi in range(len(POOL_WINDOWS))], axis=1)


def _halo_specs(tm, width, S, after):
    per = tm // POOL_HALO
    last = S // POOL_HALO - 1
    if after:
        return pl.BlockSpec((POOL_HALO, width), lambda i: (jnp.minimum((i + 1) * per, last), 0))
    return pl.BlockSpec((POOL_HALO, width), lambda i: (jnp.maximum(i * per - 1, 0), 0))


def _mix_out(xp, attn, gl, bias, pw, pscale, wpo, wao, wo, h, tm=512):
    S, D = h.shape
    nt = S // tm
    PW = xp.shape[1]

    def body(xc_ref, xprev_ref, at_ref, gl_ref, bias_ref, pw_ref, ps_ref, wpo_ref, wao_ref, wo_ref, h_ref,
             ho_ref, bp_ref, ba_ref):
        i = pl.program_id(0)
        pooled = _pooled(xc_ref[...], xprev_ref[...], i).astype(BF16)
        ms = (_pool_mix(pooled, pw_ref) * ps_ref[...]).astype(BF16)
        bp = _dot(ms, wpo_ref[...])
        ba = _dot(at_ref[...], wao_ref[...])
        bp_ref[...] = bp.astype(BF16)
        ba_ref[...] = ba.astype(BF16)
        gates = jax.nn.sigmoid(gl_ref[...].astype(F32) + bias_ref[...])
        merged = (gates[:, :D] * bp + gates[:, D:] * ba).astype(BF16)
        ho_ref[...] = h_ref[...] + _dot(merged, wo_ref[...])

    def row(width):
        return pl.BlockSpec((tm, width), lambda i: (i, 0))

    def whole(x):
        nd = x.ndim
        return pl.BlockSpec(x.shape, lambda i: (0,) * nd)

    return pl.pallas_call(
        body, name="mix_out", grid=(nt,),
        in_specs=[row(PW), _halo_specs(tm, PW, S, False), row(D), row(2 * D), whole(bias), whole(pw), whole(pscale),
                  whole(wpo), whole(wao), whole(wo), row(D)],
        out_specs=[row(D), row(D), row(D)],
        out_shape=[jax.ShapeDtypeStruct((S, D), F32), jax.ShapeDtypeStruct((S, D), BF16),
                   jax.ShapeDtypeStruct((S, D), BF16)],
        compiler_params=_params(dimension_semantics=("arbitrary",)),
    )(xp, xp, attn, gl, bias, pw, pscale, wpo, wao, wo, h)


def _mix_bwd_gate(dh, wo, bp, ba, gl, bias, tm=512):
    S, D = dh.shape
    nt = S // tm

    def body(dh_ref, wo_ref, bp_ref, ba_ref, gl_ref, bias_ref, dgl_ref, dbp_ref, dba_ref, dwo_ref, dbias_ref):
        i = pl.program_id(0)
        dhb = dh_ref[...].astype(BF16)
        dm = _dot_nt(dhb, wo_ref[...])
        gates = jax.nn.sigmoid(gl_ref[...].astype(F32) + bias_ref[...])
        gp, ga = gates[:, :D], gates[:, D:]
        bp_v = bp_ref[...].astype(F32)
        ba_v = ba_ref[...].astype(F32)
        merged = (gp * bp_v + ga * ba_v).astype(BF16)
        _acc(dwo_ref, _dot_tn(merged, dhb), i == 0)
        dbp_ref[...] = (dm * gp).astype(BF16)
        dba_ref[...] = (dm * ga).astype(BF16)
        dgl = jnp.concatenate([dm * bp_v * gp * (1.0 - gp), dm * ba_v * ga * (1.0 - ga)], axis=1)
        dgl_ref[...] = dgl.astype(BF16)
        _acc(dbias_ref, jnp.sum(dgl, axis=0, keepdims=True), i == 0)

    def row(width):
        return pl.BlockSpec((tm, width), lambda i: (i, 0))

    def whole(shape):
        return pl.BlockSpec(shape, lambda i: (0, 0))

    return pl.pallas_call(
        body, name="mix_bwd_gate", grid=(nt,),
        in_specs=[row(D), whole(wo.shape), row(D), row(D), row(2 * D), whole(bias.shape)],
        out_specs=[row(2 * D), row(D), row(D), whole((D, D)), whole((1, 2 * D))],
        out_shape=[jax.ShapeDtypeStruct((S, 2 * D), BF16), jax.ShapeDtypeStruct((S, D), BF16),
                   jax.ShapeDtypeStruct((S, D), BF16), jax.ShapeDtypeStruct((D, D), F32),
                   jax.ShapeDtypeStruct((1, 2 * D), F32)],
        compiler_params=_params(dimension_semantics=("arbitrary",)),
    )(dh, wo, bp, ba, gl, bias)


def _mix_bwd_branch(dbp, dba, attn, xp, pw, pscale, wpo, wao, tm=512):
    S, D = dbp.shape
    nt = S // tm
    PW = xp.shape[1]
    NG = len(POOL_WINDOWS)

    def body(dbp_ref, dba_ref, at_ref, xc_ref, xprev_ref, pw_ref, ps_ref, wpo_ref, wao_ref,
             dat_ref, dpl_ref, dwao_ref, dwpo_ref, dpw_ref, dps_ref):
        i = pl.program_id(0)
        dba_v = dba_ref[...]
        dbp_v = dbp_ref[...]
        _acc(dwao_ref, _dot_tn(at_ref[...], dba_v), i == 0)
        dat_ref[...] = _dot_nt(dba_v, wao_ref[...]).astype(BF16)
        pooled = _pooled(xc_ref[...], xprev_ref[...], i).astype(BF16)
        mixed = _pool_mix(pooled, pw_ref)
        ps = ps_ref[...]
        _acc(dwpo_ref, _dot_tn((mixed * ps).astype(BF16), dbp_v), i == 0)
        dms = _dot_nt(dbp_v, wpo_ref[...])
        _acc(dps_ref, jnp.sum(dms * mixed, axis=0, keepdims=True), i == 0)
        dmixed = (dms * ps).astype(BF16)
        dpooled = []
        for gi in range(NG):
            cols = slice(gi * POOL_GROUP, (gi + 1) * POOL_GROUP)
            _acc(dpw_ref.at[gi], _dot_tn(pooled[:, cols], dmixed[:, cols]), i == 0)
            dpooled.append(_dot_nt(dmixed[:, cols], pw_ref[gi]))
        dpl_ref[...] = jnp.concatenate(dpooled, axis=1)

    def row(width):
        return pl.BlockSpec((tm, width), lambda i: (i, 0))

    def whole(shape):
        nd = len(shape)
        return pl.BlockSpec(shape, lambda i: (0,) * nd)

    return pl.pallas_call(
        body, name="mix_bwd_branch", grid=(nt,),
        in_specs=[row(D), row(D), row(D), row(PW), _halo_specs(tm, PW, S, False), whole(pw.shape), whole(pscale.shape),
                  whole(wpo.shape), whole(wao.shape)],
        out_specs=[row(D), row(PW), whole((D, D)), whole((PW, D)), whole(pw.shape), whole((1, PW))],
        out_shape=[jax.ShapeDtypeStruct((S, D), BF16), jax.ShapeDtypeStruct((S, PW), F32),
                   jax.ShapeDtypeStruct((D, D), F32), jax.ShapeDtypeStruct((PW, D), F32),
                   jax.ShapeDtypeStruct(pw.shape, F32), jax.ShapeDtypeStruct((1, PW), F32)],
        compiler_params=_params(dimension_semantics=("arbitrary",)),
    )(dbp, dba, attn, xp, xp, pw, pscale, wpo, wao)


def _pool_bwd(dpooled, tm=512):
    S, PW = dpooled.shape
    nt = S // tm

    def body(dc_ref, dnext_ref, dxp_ref):
        i = pl.program_id(0)
        dc = dc_ref[...]
        dh = jnp.concatenate([dc, jnp.where(i < nt - 1, dnext_ref[...], 0.0)], axis=0)
        rows = tm + POOL_HALO
        t = lax.broadcasted_iota(jnp.int32, (rows, 1), 0) + i * tm
        out = []
        for gi, w in enumerate(POOL_WINDOWS):
            cols = slice(gi * POOL_GROUP, (gi + 1) * POOL_GROUP)
            acc = dh[:, cols] / jnp.minimum(t + 1, w).astype(F32)
            sh = 1
            while sh < w:
                acc = acc + pltpu.roll(acc, rows - sh, 0)
                sh *= 2
            out.append(acc[:tm, :] - dc[:, cols])
        dxp_ref[...] = jnp.concatenate(out, axis=1).astype(BF16)

    return pl.pallas_call(
        body, name="pool_bwd", grid=(nt,),
        in_specs=[pl.BlockSpec((tm, PW), lambda i: (i, 0)), _halo_specs(tm, PW, S, True)],
        out_specs=pl.BlockSpec((tm, PW), lambda i: (i, 0)), out_shape=jax.ShapeDtypeStruct((S, PW), BF16),
        compiler_params=_params(dimension_semantics=("arbitrary",)),
    )(dpooled, dpooled)


def _in_bwd(dxp, dq, dkv, dgl, u, win, h, dh, gnorm, tm=256):
    S, D = h.shape
    nt = S // tm
    NW = win.shape[1]
    segs = (SEG_POOL, SEG_Q, SEG_KV, SEG_GATE)

    def body(dxp_ref, dq_ref, dkv_ref, dgl_ref, u_ref, w_ref, h_ref, dh_ref, g_ref, dw_hbm, dx_ref, dxh_ref, dg_ref,
             dw_acc, sem):
        i = pl.program_id(0)
        u_t = u_ref[...]
        du = jnp.zeros((tm, D), F32)
        for (off, width), ref in zip(segs, (dxp_ref, dq_ref, dkv_ref, dgl_ref)):
            piece = ref[...].astype(BF16)
            _acc(dw_acc.at[:, off:off + width], _dot_tn(u_t, piece), i == 0)
            du = du + _dot_nt(piece, w_ref[:, off:off + width])
        dx, dg_rows = _rms_bwd(du, h_ref[...], g_ref[...])
        out = dh_ref[...] + dx
        dx_ref[...] = out
        dxh_ref[...] = (0.5 * out).astype(BF16)
        _acc(dg_ref, jnp.sum(dg_rows, axis=0, keepdims=True), i == 0)

        @pl.when(i == nt - 1)
        def _():
            cp = pltpu.make_async_copy(dw_acc, dw_hbm, sem)
            cp.start()
            cp.wait()

    def row(width):
        return pl.BlockSpec((tm, width), lambda i: (i, 0))

    vec = pl.BlockSpec((1, D), lambda i: (0, 0))
    return pl.pallas_call(
        body, name="in_bwd", grid=(nt,),
        in_specs=[row(SEG_POOL[1]), row(SEG_Q[1]), row(SEG_KV[1]), row(SEG_GATE[1]), row(D),
                  pl.BlockSpec(win.shape, lambda i: (0, 0)), row(D), row(D), vec],
        out_specs=[pl.BlockSpec(memory_space=pl.ANY), row(D), row(D), vec],
        out_shape=[jax.ShapeDtypeStruct((D, NW), F32), jax.ShapeDtypeStruct((S, D), F32),
                   jax.ShapeDtypeStruct((S, D), BF16), jax.ShapeDtypeStruct((1, D), F32)],
        scratch_shapes=[pltpu.VMEM((D, NW), F32), pltpu.SemaphoreType.DMA],
        compiler_params=_params(dimension_semantics=("arbitrary",)),
    )(dxp, dq, dkv, dgl, u, win, h, dh, gnorm)


def _row_tile(rows):
    for t in (512, 352, 256, 128, 64, 32, 16, 8):
        if rows % t == 0:
            return t
    return rows


def _adamw(w, g, m, v, name):
    R, C = w.shape
    tr = _row_tile(R)

    def body(w_ref, g_ref, m_ref, v_ref, d_ref, mo_ref, vo_ref):
        gv = g_ref[...]
        mn = ADAM_B1 * m_ref[...] + (1.0 - ADAM_B1) * gv
        vn = ADAM_B2 * v_ref[...] + (1.0 - ADAM_B2) * (gv * gv)
        m_hat = mn / (1.0 - ADAM_B1 ** ADAM_STEP)
        v_hat = vn / (1.0 - ADAM_B2 ** ADAM_STEP)
        d_ref[...] = -ADAM_LR * (m_hat / (jnp.sqrt(v_hat) + ADAM_EPS) + ADAM_WD * w_ref[...])
        mo_ref[...] = mn
        vo_ref[...] = vn

    blk = pl.BlockSpec((tr, C), lambda i: (i, 0))
    sh = jax.ShapeDtypeStruct((R, C), F32)
    return pl.pallas_call(
        body, name=name, grid=(R // tr,), in_specs=[blk] * 4, out_specs=[blk] * 3, out_shape=[sh] * 3,
        compiler_params=_params(dimension_semantics=("arbitrary",)),
    )(w, g, m, v)


def _sum_halves(g4, recv, core, name):
    NS, R, C = g4.shape
    hr = R // 2
    tr = _row_tile(hr)
    per = hr // tr

    def body(c_ref, g_ref, r_ref, o_ref):
        o_ref[...] = (g_ref[...] + r_ref[...]).astype(BF16)

    grid_spec = pltpu.PrefetchScalarGridSpec(
        num_scalar_prefetch=1, grid=(NS, per),
        in_specs=[pl.BlockSpec((1, tr, C), lambda s, i, c: (s, c[0] * per + i, 0)),
                  pl.BlockSpec((1, tr, C), lambda s, i, c: (s, i, 0))],
        out_specs=pl.BlockSpec((1, tr, C), lambda s, i, c: (s, i, 0)))
    return pl.pallas_call(
        body, name=name, grid_spec=grid_spec, out_shape=jax.ShapeDtypeStruct((NS, hr, C), BF16),
        compiler_params=_params(dimension_semantics=("arbitrary", "arbitrary")),
    )(core, g4, recv)


def _sum_quarters(h4, recv3, shard, name):
    NS, hr, C = h4.shape
    tr = _row_tile(hr)

    def body(s_ref, h_ref, r_ref, o_ref):
        acc = h_ref[0].astype(F32)
        for k in range(N_SHARDS - 1):
            acc = acc + r_ref[k].astype(F32)
        o_ref[...] = acc

    grid_spec = pltpu.PrefetchScalarGridSpec(
        num_scalar_prefetch=1, grid=(hr // tr,),
        in_specs=[pl.BlockSpec((1, tr, C), lambda i, s: (s[0], i, 0)),
                  pl.BlockSpec((N_SHARDS - 1, tr, C), lambda i, s: (0, i, 0))],
        out_specs=pl.BlockSpec((tr, C), lambda i, s: (i, 0)))
    return pl.pallas_call(
        body, name=name, grid_spec=grid_spec, out_shape=jax.ShapeDtypeStruct((hr, C), F32),
        compiler_params=_params(dimension_semantics=("arbitrary",)),
    )(shard, h4, recv3)


def _place():
    x, y, c = lax.axis_index("x"), lax.axis_index("y"), lax.axis_index("c")
    chips = [(1 - x, y), (x, 1 - y), (1 - x, 1 - y)]
    return x, y, c, chips


def _any_specs(n):
    return [pl.BlockSpec(memory_space=pl.ANY)] * n


def _gather_weights(shards):
    nw = len(shards)

    def body(*refs):
        ins, outs = refs[:nw], refs[nw:2 * nw]
        local_sem, ici_send, ici_recv, d2d_send, d2d_recv = refs[2 * nw:]
        x, y, c, chips = _place()
        me = 2 * x + y
        sibling = (x, y, 1 - c)
        local, sends = [], []
        for w in range(nw):
            cp = pltpu.make_async_copy(ins[w], outs[w].at[me], local_sem.at[w])
            cp.start()
            local.append(cp)
        for w in range(nw):
            hr = ins[w].shape[0] // 2
            mine = pl.ds(pl.multiple_of(c * hr, 16), hr)
            for k, (px, py) in enumerate(chips):
                cp = pltpu.make_async_remote_copy(
                    src_ref=ins[w].at[mine], dst_ref=outs[w].at[me, mine], send_sem=ici_send.at[w, k],
                    recv_sem=ici_recv.at[w, k], device_id=(px, py, c), device_id_type=MESH)
                cp.start()
                sends.append(cp)
        for w in range(nw):
            hr = ins[w].shape[0] // 2
            mine = pl.ds(pl.multiple_of(c * hr, 16), hr)
            for k, (px, py) in enumerate(chips):
                landed = outs[w].at[2 * px + py, mine]
                pltpu.make_async_remote_copy(
                    src_ref=landed, dst_ref=landed, send_sem=ici_send.at[w, k], recv_sem=ici_recv.at[w, k],
                    device_id=(px, py, c), device_id_type=MESH).wait_recv()
                cp = pltpu.make_async_remote_copy(
                    src_ref=landed, dst_ref=landed, send_sem=d2d_send.at[w, k], recv_sem=d2d_recv.at[w, k],
                    device_id=sibling, device_id_type=MESH)
                cp.start()
                sends.append(cp)
        for w in range(nw):
            hr = ins[w].shape[0] // 2
            other = pl.ds(pl.multiple_of((1 - c) * hr, 16), hr)
            for k, (px, py) in enumerate(chips):
                landed = outs[w].at[2 * px + py, other]
                pltpu.make_async_remote_copy(
                    src_ref=landed, dst_ref=landed, send_sem=d2d_send.at[w, k], recv_sem=d2d_recv.at[w, k],
                    device_id=sibling, device_id_type=MESH).wait_recv()
        for cp in sends:
            cp.wait_send()
        for cp in local:
            cp.wait()

    sem = pltpu.SemaphoreType.DMA((nw, N_SHARDS - 1))
    return pl.pallas_call(
        body, name="gather_weights", in_specs=_any_specs(nw), out_specs=_any_specs(nw),
        out_shape=[jax.ShapeDtypeStruct((N_SHARDS,) + s.shape, s.dtype) for s in shards],
        scratch_shapes=[pltpu.SemaphoreType.DMA((nw,)), sem, sem, sem, sem],
        compiler_params=pltpu.CompilerParams(has_side_effects=True),
    )(*shards)


def _swap_halves(grads):
    nw = len(grads)

    def body(*refs):
        ins, outs = refs[:nw], refs[nw:2 * nw]
        send_sem, recv_sem = refs[2 * nw:]
        x, y, c, _ = _place()
        sibling = (x, y, 1 - c)
        cps = []
        for w in range(nw):
            hr = ins[w].shape[1] // 2
            theirs = pl.ds(pl.multiple_of((1 - c) * hr, 8), hr)
            cp = pltpu.make_async_remote_copy(
                src_ref=ins[w].at[:, theirs], dst_ref=outs[w], send_sem=send_sem.at[w], recv_sem=recv_sem.at[w],
                device_id=sibling, device_id_type=MESH)
            cp.start()
            cps.append(cp)
        for cp in cps:
            cp.wait()

    return pl.pallas_call(
        body, name="swap_halves", in_specs=_any_specs(nw), out_specs=_any_specs(nw),
        out_shape=[jax.ShapeDtypeStruct((g.shape[0], g.shape[1] // 2, g.shape[2]), g.dtype) for g in grads],
        scratch_shapes=[pltpu.SemaphoreType.DMA((nw,)), pltpu.SemaphoreType.DMA((nw,))],
        compiler_params=pltpu.CompilerParams(has_side_effects=True),
    )(*grads)


def _exchange_quarters(halves):
    nw = len(halves)

    def body(*refs):
        ins, outs = refs[:nw], refs[nw:2 * nw]
        send_sem, recv_sem = refs[2 * nw:]
        x, y, c, chips = _place()
        cps = []
        for w in range(nw):
            for k, (px, py) in enumerate(chips):
                cp = pltpu.make_async_remote_copy(
                    src_ref=ins[w].at[2 * px + py], dst_ref=outs[w].at[k], send_sem=send_sem.at[w, k],
                    recv_sem=recv_sem.at[w, k], device_id=(px, py, c), device_id_type=MESH)
                cp.start()
                cps.append(cp)
        for cp in cps:
            cp.wait()

    sem = pltpu.SemaphoreType.DMA((nw, N_SHARDS - 1))
    return pl.pallas_call(
        body, name="exchange_quarters", in_specs=_any_specs(nw), out_specs=_any_specs(nw),
        out_shape=[jax.ShapeDtypeStruct((N_SHARDS - 1,) + h.shape[1:], h.dtype) for h in halves],
        scratch_shapes=[sem, sem],
        compiler_params=pltpu.CompilerParams(has_side_effects=True),
    )(*halves)


def _share_halves(reduced):
    nw = len(reduced)

    def body(*refs):
        ins, outs = refs[:nw], refs[nw:2 * nw]
        local_sem, send_sem, recv_sem = refs[2 * nw:]
        x, y, c, _ = _place()
        sibling = (x, y, 1 - c)
        cps = []
        for w in range(nw):
            hr = ins[w].shape[0]
            mine = pl.ds(pl.multiple_of(c * hr, 8), hr)
            lc = pltpu.make_async_copy(ins[w], outs[w].at[mine], local_sem.at[w])
            lc.start()
            cp = pltpu.make_async_remote_copy(
                src_ref=ins[w], dst_ref=outs[w].at[mine], send_sem=send_sem.at[w], recv_sem=recv_sem.at[w],
                device_id=sibling, device_id_type=MESH)
            cp.start()
            cps += [lc, cp]
        for cp in cps:
            cp.wait()

    return pl.pallas_call(
        body, name="share_halves", in_specs=_any_specs(nw), out_specs=_any_specs(nw),
        out_shape=[jax.ShapeDtypeStruct((2 * r.shape[0], r.shape[1]), r.dtype) for r in reduced],
        scratch_shapes=[pltpu.SemaphoreType.DMA((nw,)), pltpu.SemaphoreType.DMA((nw,)), pltpu.SemaphoreType.DMA((nw,))],
        compiler_params=pltpu.CompilerParams(has_side_effects=True),
    )(*reduced)


def _sum_small(vec):
    R, C = vec.shape
    ndev = 8

    def body(v_ref, o_ref, buf, send_sem, recv_sem):
        x, y, c, _ = _place()
        me = 4 * x + 2 * y + c
        buf[me] = v_ref[...]
        cps = []
        for r in range(1, ndev):
            peer = (x ^ (r >> 2), y ^ ((r >> 1) & 1), c ^ (r & 1))
            cp = pltpu.make_async_remote_copy(
                src_ref=v_ref, dst_ref=buf.at[me], send_sem=send_sem.at[r - 1], recv_sem=recv_sem.at[r - 1],
                device_id=peer, device_id_type=MESH)
            cp.start()
            cps.append(cp)
        for r in range(1, ndev):
            src = (4 * (x ^ (r >> 2)) + 2 * (y ^ ((r >> 1) & 1)) + (c ^ (r & 1)))
            pltpu.make_async_remote_copy(
                src_ref=v_ref, dst_ref=buf.at[src], send_sem=send_sem.at[r - 1], recv_sem=recv_sem.at[r - 1],
                device_id=(x, y, c), device_id_type=MESH).wait_recv()
        acc = buf[0]
        for d in range(1, ndev):
            acc = acc + buf[d]
        o_ref[...] = acc
        for cp in cps:
            cp.wait_send()

    return pl.pallas_call(
        body, name="sum_small", in_specs=[pl.BlockSpec(memory_space=pltpu.VMEM)],
        out_specs=pl.BlockSpec(memory_space=pltpu.VMEM), out_shape=jax.ShapeDtypeStruct((R, C), F32),
        scratch_shapes=[pltpu.VMEM((ndev, R, C), F32), pltpu.SemaphoreType.DMA((ndev - 1,)),
                        pltpu.SemaphoreType.DMA((ndev - 1,))],
        compiler_params=pltpu.CompilerParams(has_side_effects=True),
    )(vec)


SMALL = ("ffn1_norm", "mix_norm", "pool_w", "pool_scale", "q_norm", "k_norm", "sinks", "gate_bias", "ffn2_norm")
SMALL_COLS = 1024


def _pack_small(parts):
    flat = jnp.concatenate([p.reshape(-1) for p in parts])
    rows = -(-flat.shape[0] // (8 * SMALL_COLS)) * 8
    return jnp.pad(flat, (0, rows * SMALL_COLS - flat.shape[0])).reshape(rows, SMALL_COLS)


def _unpack_small(packed, like):
    flat = packed.reshape(-1)
    out, off = [], 0
    for p in like:
        out.append(flat[off:off + p.size].reshape(p.shape))
        off += p.size
    return out, flat[off]


def _local_step(x, target, w, small):
    D = x.shape[1]
    g1 = small["ffn1_norm"].reshape(1, D)
    gm = small["mix_norm"].reshape(1, D)
    g2 = small["ffn2_norm"].reshape(1, D)
    qw = small["q_norm"].reshape(1, HEAD_DIM)
    kw = small["k_norm"].reshape(1, HEAD_DIM)
    bias = small["gate_bias"].reshape(1, 2 * D)
    pscale = small["pool_scale"].reshape(1, -1)
    pw = small["pool_w"].astype(BF16)
    sink_rows = jnp.repeat(small["sinks"], ATTN_BLOCK).reshape(N_KV_HEADS, GQA_GROUP * ATTN_BLOCK, 1)

    n1, a1, b1, h1 = _ffn_fwd(x, g1, w["ffn1_w_gate"], w["ffn1_w_up"], w["ffn1_w_down"], None, "ffn1_fwd")
    u, xp, q, kv, gl = _mix_proj(h1, gm, w["w_in"])
    attn = _attn_fwd(q, kv, qw, kw, sink_rows)
    h2, bp, ba = _mix_out(xp, attn, gl, bias, pw, pscale, w["w_pool_out"], w["w_attn_out"], w["w_out"], h1)
    n2, a2, b2, dy, dyh, loss = _ffn_fwd(h2, g2, w["ffn2_w_gate"], w["ffn2_w_up"], w["ffn2_w_down"], target, "ffn2_fwd")

    gw, gs = {}, {}
    da, db, gw["ffn2_w_gate"], gw["ffn2_w_up"], gw["ffn2_w_down"] = _ffn_bwd_w(dyh, n2, a2, b2, w["ffn2_w_down"], "ffn2_bwd_w")
    dh2, _, gs["ffn2_norm"] = _ffn_bwd_x(da, db, w["ffn2_w_gate"], w["ffn2_w_up"], h2, dy, g2, "ffn2_bwd_x")
    dgl, dbp, dba, gw["w_out"], gs["gate_bias"] = _mix_bwd_gate(dh2, w["w_out"], bp, ba, gl, bias)
    dattn, dpooled, gw["w_attn_out"], gw["w_pool_out"], gs["pool_w"], gs["pool_scale"] = _mix_bwd_branch(
        dbp, dba, attn, xp, pw, pscale, w["w_pool_out"], w["w_attn_out"])
    dq, dkv, gs["q_norm"], gs["k_norm"], dsk = _attn_bwd(q, kv, dattn, qw, kw, sink_rows)
    gs["sinks"] = dsk[:, :, 0]
    dxp = _pool_bwd(dpooled)
    gw["w_in"], dh1, dh1h, gs["mix_norm"] = _in_bwd(dxp, dq, dkv, dgl, u, w["w_in"], h1, dh2, gm)
    da, db, gw["ffn1_w_gate"], gw["ffn1_w_up"], gw["ffn1_w_down"] = _ffn_bwd_w(dh1h, n1, a1, b1, w["ffn1_w_down"], "ffn1_bwd_w")
    gx, _, gs["ffn1_norm"] = _ffn_bwd_x(da, db, w["ffn1_w_gate"], w["ffn1_w_up"], x, dh1, g1, "ffn1_bwd_x")
    return loss[0, 0], gx, gw, gs


LARGE = ("ffn1_w_gate", "ffn1_w_up", "ffn1_w_down", "w_in", "w_pool_out", "w_attn_out", "w_out",
         "ffn2_w_gate", "ffn2_w_up", "ffn2_w_down")
COLUMN_SHARDED = ("w_in", "w_pool_out")
WEIGHTS = ("ffn1_norm", "ffn1_w_gate", "ffn1_w_up", "ffn1_w_down", "mix_norm", "w_in", "pool_w", "pool_scale",
           "w_pool_out", "q_norm", "k_norm", "sinks", "w_attn_out", "gate_bias", "w_out", "ffn2_norm",
           "ffn2_w_gate", "ffn2_w_up", "ffn2_w_down")


def _columns_to_natural(g4):
    ns, r, c = g4.shape
    return jnp.transpose(g4, (1, 0, 2)).reshape(r, ns * c)


def _natural_to_columns(g):
    r, c = g.shape
    return jnp.transpose(g.reshape(r, N_SHARDS, c // N_SHARDS), (1, 0, 2))


def kernel(x, ffn1_norm, ffn1_w_gate, ffn1_w_up, ffn1_w_down, mix_norm, w_in, pool_w, pool_scale, w_pool_out, q_norm, k_norm, sinks, w_attn_out, gate_bias, w_out, ffn2_norm, ffn2_w_gate, ffn2_w_up, ffn2_w_down, loss_target, m_ffn1_norm, m_ffn1_w_gate, m_ffn1_w_up, m_ffn1_w_down, m_mix_norm, m_w_in, m_pool_w, m_pool_scale, m_w_pool_out, m_q_norm, m_k_norm, m_sinks, m_w_attn_out, m_gate_bias, m_w_out, m_ffn2_norm, m_ffn2_w_gate, m_ffn2_w_up, m_ffn2_w_down, v_ffn1_norm, v_ffn1_w_gate, v_ffn1_w_up, v_ffn1_w_down, v_mix_norm, v_w_in, v_pool_w, v_pool_scale, v_w_pool_out, v_q_norm, v_k_norm, v_sinks, v_w_attn_out, v_gate_bias, v_w_out, v_ffn2_norm, v_ffn2_w_gate, v_ffn2_w_up, v_ffn2_w_down):
    args = dict(locals())
    wts = {n: args[n] for n in WEIGHTS}
    mom = {n: args["m_" + n] for n in WEIGHTS}
    var = {n: args["v_" + n] for n in WEIGHTS}
    core = lax.axis_index("c").astype(jnp.int32).reshape(1)
    shard = (2 * lax.axis_index("x") + lax.axis_index("y")).astype(jnp.int32).reshape(1)

    gathered = _gather_weights([wts[n].astype(BF16) for n in LARGE])
    full = dict(zip(LARGE, gathered))
    for n in COLUMN_SHARDED:
        full[n] = _columns_to_natural(full[n])
    for n in ("ffn1_w_down", "ffn2_w_down", "w_attn_out", "w_out"):
        if n in ("w_attn_out", "w_out"):
            full[n] = full[n].reshape(-1, full[n].shape[-1])

    small = {n: wts[n] for n in SMALL}
    loss, grad_x, gw, gs = _local_step(x[0], loss_target[0], full, small)

    for n in COLUMN_SHARDED:
        gw[n] = _natural_to_columns(gw[n])
    for n in ("w_attn_out", "w_out"):
        gw[n] = gw[n].reshape(N_SHARDS, -1, gw[n].shape[-1])
    g4 = [gw[n] for n in LARGE]
    recv = _swap_halves(g4)
    halves = [_sum_halves(g, r, core, "sum_halves_" + n) for n, g, r in zip(LARGE, g4, recv)]
    recv3 = _exchange_quarters(halves)
    reduced = [_sum_quarters(h, r, shard, "sum_quarters_" + n) for n, h, r in zip(LARGE, halves, recv3)]
    grads = dict(zip(LARGE, _share_halves(reduced)))

    small_parts = [gs[n] for n in SMALL] + [loss.reshape(1)]
    summed = _sum_small(_pack_small(small_parts))
    small_grads, loss_sum = _unpack_small(summed, [wts[n] for n in SMALL])
    grads.update(dict(zip(SMALL, small_grads)))

    delta, new_m, new_v = {}, {}, {}
    for n in LARGE:
        delta[n], new_m[n], new_v[n] = _adamw(wts[n], grads[n], mom[n], var[n], "adamw_" + n)
    zero = jnp.zeros((1,), F32)
    packed = [_pack_small([d[n] for n in SMALL] + [zero]) for d in (wts, mom, var)]
    ds, ms, vs = _adamw(packed[0], summed, packed[1], packed[2], "adamw_small")
    like = [wts[n] for n in SMALL]
    for out, packed_out in ((delta, ds), (new_m, ms), (new_v, vs)):
        out.update(dict(zip(SMALL, _unpack_small(packed_out, like)[0])))

    return (loss_sum, grad_x[None], *[grads[n] for n in WEIGHTS], *[delta[n] for n in WEIGHTS],
            *[new_m[n] for n in WEIGHTS], *[new_v[n] for n in WEIGHTS])
```

```python
import jax
import jax.numpy as jnp
from jax import lax
from jax.experimental import pallas as pl
from jax.experimental.pallas import tpu as pltpu

F32 = jnp.float32
BF16 = jnp.bfloat16
RMS_EPS = 1e-6
POOL_WINDOWS = (2, 4, 8, 16)
POOL_GROUP = 128
POOL_HALO = 16
HEAD_DIM = 64
GQA_GROUP = 8
N_KV_HEADS = 2
ATTN_BLOCK = 128
SCALE = HEAD_DIM ** -0.5
NEG = -1e30
N_SHARDS = 4
ADAM_LR, ADAM_B1, ADAM_B2, ADAM_EPS, ADAM_WD, ADAM_STEP = 0.001, 0.9, 0.999, 1e-08, 0.01, 10
VMEM_LIMIT = 56 * 1024 * 1024
MESH = pl.DeviceIdType.MESH
SEG_POOL, SEG_Q, SEG_KV, SEG_GATE = (0, 512), (512, 1024), (1536, 256), (1792, 2048)
SEGMENTS = (SEG_POOL, SEG_Q, SEG_KV, SEG_GATE)


def _params(**kw):
    return pltpu.CompilerParams(vmem_limit_bytes=VMEM_LIMIT, **kw)


def _dot(a, b):
    return jnp.dot(a, b, preferred_element_type=F32)


def _dot_nt(a, b):
    return lax.dot_general(a, b, (((1,), (1,)), ((), ())), preferred_element_type=F32)


def _dot_tn(a, b):
    return lax.dot_general(a, b, (((0,), (0,)), ((), ())), preferred_element_type=F32)


def _rinv(x):
    return lax.rsqrt(jnp.mean(x * x, axis=-1, keepdims=True) + RMS_EPS)


def _rms_bwd(dn, x, g):
    r = _rinv(x)
    xr = x * r
    z = dn * g
    dx = r * (z - xr * jnp.mean(z * xr, axis=-1, keepdims=True))
    return dx, dn * xr


def _acc(ref, val, first):
    @pl.when(first)
    def _():
        ref[...] = val

    @pl.when(jnp.logical_not(first))
    def _():
        ref[...] += val


def _ffn_fwd(h, gnorm, wgt, wut, wd, target, name, tm=512, tf=1408):
    S, D = h.shape
    F = wd.shape[0]
    with_loss = target is not None
    nt, nf = S // tm, F // tf

    def body(*refs):
        if with_loss:
            h_ref, g_ref, wg_ref, wu_ref, wd_ref, t_ref, n_ref, a_ref, b_ref, acc_ref, dyh_ref, loss_ref = refs
        else:
            h_ref, g_ref, wg_ref, wu_ref, wd_ref, n_ref, a_ref, b_ref, acc_ref = refs
        i = pl.program_id(0)
        j = pl.program_id(1)

        @pl.when(j == 0)
        def _():
            x = h_ref[...]
            n_ref[...] = (x * _rinv(x) * g_ref[...]).astype(BF16)

        n = n_ref[...]
        a = _dot_nt(n, wg_ref[...])
        b = _dot_nt(n, wu_ref[...])
        a_ref[...] = a.astype(BF16)
        b_ref[...] = b.astype(BF16)
        s = (a * jax.nn.sigmoid(a) * b).astype(BF16)
        _acc(acc_ref, _dot(s, wd_ref[...]), j == 0)

        @pl.when(j == nf - 1)
        def _():
            y = h_ref[...] + 0.5 * acc_ref[...]
            if with_loss:
                e = y - t_ref[...]
                dy = e * (1.0 / D)
                acc_ref[...] = dy
                dyh_ref[...] = (0.5 * dy).astype(BF16)
                tot = jnp.sum(jnp.sum(e * e, axis=1, keepdims=True), axis=0, keepdims=True) * (0.5 / D)
                _acc(loss_ref, jnp.broadcast_to(tot, loss_ref.shape), i == 0)
            else:
                acc_ref[...] = y

    row = pl.BlockSpec((tm, D), lambda i, j: (i, 0))
    wblk = pl.BlockSpec((tf, D), lambda i, j: (j, 0))
    act = pl.BlockSpec((tm, tf), lambda i, j: (i, j))
    in_specs = [row, pl.BlockSpec((1, D), lambda i, j: (0, 0)), wblk, wblk, wblk]
    out_specs = [row, act, act, row]
    out_shape = [jax.ShapeDtypeStruct((S, D), BF16), jax.ShapeDtypeStruct((S, F), BF16),
                 jax.ShapeDtypeStruct((S, F), BF16), jax.ShapeDtypeStruct((S, D), F32)]
    args = [h, gnorm, wgt, wut, wd]
    if with_loss:
        in_specs.append(row)
        args.append(target)
        out_specs += [row, pl.BlockSpec((8, 128), lambda i, j: (0, 0))]
        out_shape += [jax.ShapeDtypeStruct((S, D), BF16), jax.ShapeDtypeStruct((8, 128), F32)]
    return pl.pallas_call(
        body, name=name, grid=(nt, nf), in_specs=in_specs, out_specs=out_specs, out_shape=out_shape,
        compiler_params=_params(dimension_semantics=("arbitrary", "arbitrary")),
    )(*args)


def _ffn_bwd_w(dyh, n, a, b, wd, name, tm=1024, tf=256):
    S, D = n.shape
    F = wd.shape[0]
    nt, nf = S // tm, F // tf

    def body(dyh_ref, n_ref, a_ref, b_ref, wd_ref, da_ref, db_ref, dwg_ref, dwu_ref, dwd_ref):
        i = pl.program_id(1)
        dyh_t = dyh_ref[...]
        n_t = n_ref[...]
        av = a_ref[...].astype(F32)
        bv = b_ref[...].astype(F32)
        ds = _dot_nt(dyh_t, wd_ref[...])
        sig = jax.nn.sigmoid(av)
        silu = av * sig
        da = (ds * bv * (sig * (1.0 + av * (1.0 - sig)))).astype(BF16)
        db = (ds * silu).astype(BF16)
        s = (silu * bv).astype(BF16)
        da_ref[...] = da
        db_ref[...] = db
        _acc(dwg_ref, _dot_tn(da, n_t), i == 0)
        _acc(dwu_ref, _dot_tn(db, n_t), i == 0)
        _acc(dwd_ref, _dot_tn(s, dyh_t), i == 0)

    row = pl.BlockSpec((tm, D), lambda j, i: (i, 0))
    act = pl.BlockSpec((tm, tf), lambda j, i: (i, j))
    wblk = pl.BlockSpec((tf, D), lambda j, i: (j, 0))
    return pl.pallas_call(
        body, name=name, grid=(nf, nt), in_specs=[row, row, act, act, wblk],
        out_specs=[act, act, wblk, wblk, wblk],
        out_shape=[jax.ShapeDtypeStruct((S, F), BF16), jax.ShapeDtypeStruct((S, F), BF16),
                   jax.ShapeDtypeStruct((F, D), F32), jax.ShapeDtypeStruct((F, D), F32),
                   jax.ShapeDtypeStruct((F, D), F32)],
        compiler_params=_params(dimension_semantics=("arbitrary", "arbitrary")),
    )(dyh, n, a, b, wd)


def _ffn_bwd_x(da, db, wgt, wut, x, dh, gnorm, name, tm=512, tf=1408):
    S, D = x.shape
    F = wgt.shape[0]
    nt, nf = S // tm, F // tf

    def body(da_ref, db_ref, wg_ref, wu_ref, x_ref, dh_ref, g_ref, dx_ref, dxh_ref, dg_ref):
        i = pl.program_id(0)
        j = pl.program_id(1)
        _acc(dx_ref, _dot(da_ref[...], wg_ref[...]) + _dot(db_ref[...], wu_ref[...]), j == 0)

        @pl.when(j == nf - 1)
        def _():
            dx, dg_rows = _rms_bwd(dx_ref[...], x_ref[...], g_ref[...])
            out = dh_ref[...] + dx
            dx_ref[...] = out
            dxh_ref[...] = (0.5 * out).astype(BF16)
            _acc(dg_ref, jnp.sum(dg_rows, axis=0, keepdims=True), i == 0)

    row = pl.BlockSpec((tm, D), lambda i, j: (i, 0))
    act = pl.BlockSpec((tm, tf), lambda i, j: (i, j))
    wblk = pl.BlockSpec((tf, D), lambda i, j: (j, 0))
    vec = pl.BlockSpec((1, D), lambda i, j: (0, 0))
    return pl.pallas_call(
        body, name=name, grid=(nt, nf), in_specs=[act, act, wblk, wblk, row, row, vec],
        out_specs=[row, row, vec],
        out_shape=[jax.ShapeDtypeStruct((S, D), F32), jax.ShapeDtypeStruct((S, D), BF16),
                   jax.ShapeDtypeStruct((1, D), F32)],
        compiler_params=_params(dimension_semantics=("arbitrary", "arbitrary")),
    )(da, db, wgt, wut, x, dh, gnorm)


def _mix_proj(h, gnorm, wint, tm=512):
    S, D = h.shape
    nt = S // tm

    def body(h_ref, g_ref, w_ref, u_ref, xp_ref, q_ref, kv_ref, gl_ref):
        x = h_ref[...]
        u = (x * _rinv(x) * g_ref[...]).astype(BF16)
        u_ref[...] = u
        for (off, width), ref in zip(SEGMENTS, (xp_ref, q_ref, kv_ref, gl_ref)):
            ref[...] = _dot_nt(u, w_ref[off:off + width, :]).astype(ref.dtype)

    def row(width):
        return pl.BlockSpec((tm, width), lambda i: (i, 0))

    return pl.pallas_call(
        body, name="mix_proj", grid=(nt,),
        in_specs=[row(D), pl.BlockSpec((1, D), lambda i: (0, 0)), pl.BlockSpec(wint.shape, lambda i: (0, 0))],
        out_specs=[row(D), row(SEG_POOL[1]), row(SEG_Q[1]), row(SEG_KV[1]), row(SEG_GATE[1])],
        out_shape=[jax.ShapeDtypeStruct((S, D), BF16), jax.ShapeDtypeStruct((S, SEG_POOL[1]), F32),
                   jax.ShapeDtypeStruct((S, SEG_Q[1]), BF16), jax.ShapeDtypeStruct((S, SEG_KV[1]), BF16),
                   jax.ShapeDtypeStruct((S, SEG_GATE[1]), BF16)],
        compiler_params=_params(dimension_semantics=("arbitrary",)),
    )(h, gnorm, wint)


def _stack_heads(x, g):
    return jnp.concatenate([x[:, (GQA_GROUP * g + hh) * HEAD_DIM:(GQA_GROUP * g + hh + 1) * HEAD_DIM]
                            for hh in range(GQA_GROUP)], axis=0)


def _unstack_heads(ref, val, g):
    for hh in range(GQA_GROUP):
        lo = (GQA_GROUP * g + hh) * HEAD_DIM
        ref[:, lo:lo + HEAD_DIM] = val[hh * ATTN_BLOCK:(hh + 1) * ATTN_BLOCK, :]


def _attn_probs(qn, kc, kp, sink, n):
    rows = GQA_GROUP * ATTN_BLOCK
    qi = lax.broadcasted_iota(jnp.int32, (rows, ATTN_BLOCK), 0) % ATTN_BLOCK
    kj = lax.broadcasted_iota(jnp.int32, (rows, ATTN_BLOCK), 1)
    s_c = jnp.where(kj <= qi, _dot_nt(qn, kc), NEG)
    s_p = jnp.where(jnp.logical_and(kj > qi, n > 0), _dot_nt(qn, kp), NEG)
    m = jnp.maximum(jnp.maximum(jnp.max(s_c, axis=-1, keepdims=True), jnp.max(s_p, axis=-1, keepdims=True)), sink)
    e_c = jnp.exp(s_c - m)
    e_p = jnp.exp(s_p - m)
    e_s = jnp.exp(sink - m)
    inv = 1.0 / (jnp.sum(e_c, axis=-1, keepdims=True) + jnp.sum(e_p, axis=-1, keepdims=True) + e_s)
    return e_c * inv, e_p * inv, e_s * inv


def _attn_blocks(n):
    cur = pl.multiple_of(n * ATTN_BLOCK, ATTN_BLOCK)
    prev = pl.multiple_of(jnp.maximum(n - 1, 0) * ATTN_BLOCK, ATTN_BLOCK)
    return cur, prev


def _kv_split(kv, g):
    k = kv[:, g * HEAD_DIM:(g + 1) * HEAD_DIM]
    v = kv[:, (N_KV_HEADS + g) * HEAD_DIM:(N_KV_HEADS + g + 1) * HEAD_DIM]
    return k, v


def _attn_fwd(q, kv, qw, kw, sink_rows):
    S, W = q.shape
    nb = S // ATTN_BLOCK

    def body(q_ref, kv_ref, qw_ref, kw_ref, sk_ref, o_ref, o_scr):
        n = pl.program_id(0)
        qf = q_ref[...].astype(F32)
        cur, prev = _attn_blocks(n)
        kvc = kv_ref[pl.ds(cur, ATTN_BLOCK), :].astype(F32)
        kvp = kv_ref[pl.ds(prev, ATTN_BLOCK), :].astype(F32)
        for g in range(N_KV_HEADS):
            qs = _stack_heads(qf, g)
            qn = (qs * _rinv(qs) * qw_ref[...] * SCALE).astype(BF16)
            kc, vc = _kv_split(kvc, g)
            kp, vp = _kv_split(kvp, g)
            kc = (kc * _rinv(kc) * kw_ref[...]).astype(BF16)
            kp = (kp * _rinv(kp) * kw_ref[...]).astype(BF16)
            p_c, p_p, _ = _attn_probs(qn, kc, kp, sk_ref[g], n)
            o = _dot(p_c.astype(BF16), vc.astype(BF16)) + _dot(p_p.astype(BF16), vp.astype(BF16))
            _unstack_heads(o_scr, o, g)
        o_ref[...] = o_scr[...].astype(BF16)

    blk = pl.BlockSpec((ATTN_BLOCK, W), lambda n: (n, 0))
    return pl.pallas_call(
        body, name="attn_fwd", grid=(nb,),
        in_specs=[blk, pl.BlockSpec(kv.shape, lambda n: (0, 0)), pl.BlockSpec((1, HEAD_DIM), lambda n: (0, 0)),
                  pl.BlockSpec((1, HEAD_DIM), lambda n: (0, 0)), pl.BlockSpec(sink_rows.shape, lambda n: (0, 0, 0))],
        out_specs=blk, out_shape=jax.ShapeDtypeStruct((S, W), BF16),
        scratch_shapes=[pltpu.VMEM((ATTN_BLOCK, W), F32)],
        compiler_params=_params(dimension_semantics=("arbitrary",)),
    )(q, kv, qw, kw, sink_rows)


def _attn_bwd(q, kv, do, qw, kw, sink_rows):
    S, W = q.shape
    KW = kv.shape[1]
    nb = S // ATTN_BLOCK
    chunk = 512

    def body(q_ref, kv_ref, do_ref, qw_ref, kw_ref, sk_ref, dq_ref, dkv_ref, dqw_ref, dkw_ref, dsk_ref, dq_scr):
        n = pl.program_id(0)

        @pl.when(n == 0)
        def _():
            dkv_ref[...] = jnp.zeros_like(dkv_ref)
            dqw_ref[...] = jnp.zeros_like(dqw_ref)
            dsk_ref[...] = jnp.zeros_like(dsk_ref)

        qf = q_ref[...].astype(F32)
        dof = do_ref[...].astype(F32)
        cur, prev = _attn_blocks(n)
        kvc = kv_ref[pl.ds(cur, ATTN_BLOCK), :].astype(F32)
        kvp = kv_ref[pl.ds(prev, ATTN_BLOCK), :].astype(F32)
        qw_v = qw_ref[...]
        for g in range(N_KV_HEADS):
            qs = _stack_heads(qf, g)
            qhat = qs * _rinv(qs)
            qn = (qhat * qw_v * SCALE).astype(BF16)
            kc, vc = _kv_split(kvc, g)
            kp, vp = _kv_split(kvp, g)
            kc = (kc * _rinv(kc) * kw_ref[...]).astype(BF16)
            kp = (kp * _rinv(kp) * kw_ref[...]).astype(BF16)
            vc = vc.astype(BF16)
            vp = vp.astype(BF16)
            dos = _stack_heads(dof, g).astype(BF16)
            p_c, p_p, p_s = _attn_probs(qn, kc, kp, sk_ref[g], n)
            dp_c = _dot_nt(dos, vc)
            dp_p = _dot_nt(dos, vp)
            drow = jnp.sum(p_c * dp_c, axis=-1, keepdims=True) + jnp.sum(p_p * dp_p, axis=-1, keepdims=True)
            ds_c = (p_c * (dp_c - drow)).astype(BF16)
            ds_p = (p_p * (dp_p - drow)).astype(BF16)
            dsink = -(p_s * drow)
            for hh in range(GQA_GROUP):
                tot = jnp.sum(dsink[hh * ATTN_BLOCK:(hh + 1) * ATTN_BLOCK, :], axis=0, keepdims=True)
                dsk_ref[g, hh:hh + 1, :] += jnp.broadcast_to(tot, (1, 128))
            dqn = (_dot(ds_c, kc) + _dot(ds_p, kp)) * SCALE
            klo, vlo = g * HEAD_DIM, (N_KV_HEADS + g) * HEAD_DIM
            dkv_ref[pl.ds(cur, ATTN_BLOCK), klo:klo + HEAD_DIM] += _dot_tn(ds_c, qn)
            dkv_ref[pl.ds(prev, ATTN_BLOCK), klo:klo + HEAD_DIM] += _dot_tn(ds_p, qn)
            dkv_ref[pl.ds(cur, ATTN_BLOCK), vlo:vlo + HEAD_DIM] += _dot_tn(p_c.astype(BF16), dos)
            dkv_ref[pl.ds(prev, ATTN_BLOCK), vlo:vlo + HEAD_DIM] += _dot_tn(p_p.astype(BF16), dos)
            dqw_ref[...] += jnp.sum(dqn * qhat, axis=0, keepdims=True)
            z = dqn * qw_v
            dqs = _rinv(qs) * (z - qhat * jnp.mean(z * qhat, axis=-1, keepdims=True))
            _unstack_heads(dq_scr, dqs, g)
        dq_ref[...] = dq_scr[...].astype(BF16)

        @pl.when(n == nb - 1)
        def _():
            def one(c, dkw):
                rows = pl.ds(pl.multiple_of(c * chunk, chunk), chunk)
                for g in range(N_KV_HEADS):
                    lo = g * HEAD_DIM
                    k = kv_ref[rows, lo:lo + HEAD_DIM].astype(F32)
                    dx, dg_rows = _rms_bwd(dkv_ref[rows, lo:lo + HEAD_DIM], k, kw_ref[...])
                    dkv_ref[rows, lo:lo + HEAD_DIM] = dx
                    dkw = dkw + jnp.sum(dg_rows, axis=0, keepdims=True)
                return dkw

            dkw_ref[...] = lax.fori_loop(0, S // chunk, one, jnp.zeros((1, HEAD_DIM), F32))

    blk = pl.BlockSpec((ATTN_BLOCK, W), lambda n: (n, 0))
    whole_kv = pl.BlockSpec((S, KW), lambda n: (0, 0))
    vec = pl.BlockSpec((1, HEAD_DIM), lambda n: (0, 0))
    sk = pl.BlockSpec(sink_rows.shape, lambda n: (0, 0, 0))
    dsk = pl.BlockSpec((N_KV_HEADS, GQA_GROUP, 128), lambda n: (0, 0, 0))
    return pl.pallas_call(
        body, name="attn_bwd", grid=(nb,), in_specs=[blk, whole_kv, blk, vec, vec, sk],
        out_specs=[blk, whole_kv, vec, vec, dsk],
        out_shape=[jax.ShapeDtypeStruct((S, W), BF16), jax.ShapeDtypeStruct((S, KW), F32),
                   jax.ShapeDtypeStruct((1, HEAD_DIM), F32), jax.ShapeDtypeStruct((1, HEAD_DIM), F32),
                   jax.ShapeDtypeStruct((N_KV_HEADS, GQA_GROUP, 128), F32)],
        scratch_shapes=[pltpu.VMEM((ATTN_BLOCK, W), F32)],
        compiler_params=_params(dimension_semantics=("arbitrary",)),
    )(q, kv, do, qw, kw, sink_rows)


def _pooled(xc, xprev, i):
    tm = xc.shape[0]
    xh = jnp.concatenate([jnp.where(i > 0, xprev, 0.0), xc], axis=0)
    t = lax.broadcasted_iota(jnp.int32, (tm, 1), 0) + i * tm
    out = []
    for gi, w in enumerate(POOL_WINDOWS):
        acc = xh[:, gi * POOL_GROUP:(gi + 1) * POOL_GROUP]
        sh = 1
        while sh < w:
            acc = acc + pltpu.roll(acc, sh, 0)
            sh *= 2
        cnt = jnp.minimum(t + 1, w).astype(F32)
        out.append(acc[POOL_HALO:, :] / cnt - xc[:, gi * POOL_GROUP:(gi + 1) * POOL_GROUP])
    return jnp.concatenate(out, axis=1)


def _pool_mix(pooled_b, pw_ref):
    return jnp.concatenate([_dot(pooled_b[:, gi * POOL_GROUP:(gi + 1) * POOL_GROUP], pw_ref[gi])
                            for gi in range(len(POOL_WINDOWS))], axis=1)


def _halo_specs(tm, width, S, after):
    per = tm // POOL_HALO
    last = S // POOL_HALO - 1
    if after:
        return pl.BlockSpec((POOL_HALO, width), lambda i: (jnp.minimum((i + 1) * per, last), 0))
    return pl.BlockSpec((POOL_HALO, width), lambda i: (jnp.maximum(i * per - 1, 0), 0))


def _mix_out(xp, attn, gl, bias, pw, pscale, wpot, wao, wo, h, tm=512):
    S, D = h.shape
    nt = S // tm
    PW = xp.shape[1]

    def body(xc_ref, xprev_ref, at_ref, gl_ref, bias_ref, pw_ref, ps_ref, wpo_ref, wao_ref, wo_ref, h_ref,
             ho_ref, bp_ref, ba_ref):
        i = pl.program_id(0)
        pooled = _pooled(xc_ref[...], xprev_ref[...], i).astype(BF16)
        ms = (_pool_mix(pooled, pw_ref) * ps_ref[...]).astype(BF16)
        bp = _dot_nt(ms, wpo_ref[...])
        ba = _dot(at_ref[...], wao_ref[...])
        bp_ref[...] = bp.astype(BF16)
        ba_ref[...] = ba.astype(BF16)
        gates = jax.nn.sigmoid(gl_ref[...].astype(F32) + bias_ref[...])
        merged = (gates[:, :D] * bp + gates[:, D:] * ba).astype(BF16)
        ho_ref[...] = h_ref[...] + _dot(merged, wo_ref[...])

    def row(width):
        return pl.BlockSpec((tm, width), lambda i: (i, 0))

    def whole(x):
        nd = x.ndim
        return pl.BlockSpec(x.shape, lambda i: (0,) * nd)

    return pl.pallas_call(
        body, name="mix_out", grid=(nt,),
        in_specs=[row(PW), _halo_specs(tm, PW, S, False), row(D), row(2 * D), whole(bias), whole(pw), whole(pscale),
                  whole(wpot), whole(wao), whole(wo), row(D)],
        out_specs=[row(D), row(D), row(D)],
        out_shape=[jax.ShapeDtypeStruct((S, D), F32), jax.ShapeDtypeStruct((S, D), BF16),
                   jax.ShapeDtypeStruct((S, D), BF16)],
        compiler_params=_params(dimension_semantics=("arbitrary",)),
    )(xp, xp, attn, gl, bias, pw, pscale, wpot, wao, wo, h)


def _mix_bwd_gate(dh, wo, bp, ba, gl, bias, tm=512):
    S, D = dh.shape
    nt = S // tm

    def body(dh_ref, wo_ref, bp_ref, ba_ref, gl_ref, bias_ref, dgl_ref, dbp_ref, dba_ref, dwo_ref, dbias_ref):
        i = pl.program_id(0)
        dhb = dh_ref[...].astype(BF16)
        dm = _dot_nt(dhb, wo_ref[...])
        gates = jax.nn.sigmoid(gl_ref[...].astype(F32) + bias_ref[...])
        gp, ga = gates[:, :D], gates[:, D:]
        bp_v = bp_ref[...].astype(F32)
        ba_v = ba_ref[...].astype(F32)
        merged = (gp * bp_v + ga * ba_v).astype(BF16)
        _acc(dwo_ref, _dot_tn(merged, dhb), i == 0)
        dbp_ref[...] = (dm * gp).astype(BF16)
        dba_ref[...] = (dm * ga).astype(BF16)
        dgl = jnp.concatenate([dm * bp_v * gp * (1.0 - gp), dm * ba_v * ga * (1.0 - ga)], axis=1)
        dgl_ref[...] = dgl.astype(BF16)
        _acc(dbias_ref, jnp.sum(dgl, axis=0, keepdims=True), i == 0)

    def row(width):
        return pl.BlockSpec((tm, width), lambda i: (i, 0))

    def whole(shape):
        return pl.BlockSpec(shape, lambda i: (0, 0))

    return pl.pallas_call(
        body, name="mix_bwd_gate", grid=(nt,),
        in_specs=[row(D), whole(wo.shape), row(D), row(D), row(2 * D), whole(bias.shape)],
        out_specs=[row(2 * D), row(D), row(D), whole((D, D)), whole((1, 2 * D))],
        out_shape=[jax.ShapeDtypeStruct((S, 2 * D), BF16), jax.ShapeDtypeStruct((S, D), BF16),
                   jax.ShapeDtypeStruct((S, D), BF16), jax.ShapeDtypeStruct((D, D), F32),
                   jax.ShapeDtypeStruct((1, 2 * D), F32)],
        compiler_params=_params(dimension_semantics=("arbitrary",)),
    )(dh, wo, bp, ba, gl, bias)


def _mix_bwd_branch(dbp, dba, attn, xp, pw, pscale, wpot, wao, tm=512):
    S, D = dbp.shape
    nt = S // tm
    PW = xp.shape[1]
    NG = len(POOL_WINDOWS)

    def body(dbp_ref, dba_ref, at_ref, xc_ref, xprev_ref, pw_ref, ps_ref, wpo_ref, wao_ref,
             dat_ref, dpl_ref, dwao_ref, dwpo_ref, dpw_ref, dps_ref):
        i = pl.program_id(0)
        dba_v = dba_ref[...]
        dbp_v = dbp_ref[...]
        _acc(dwao_ref, _dot_tn(at_ref[...], dba_v), i == 0)
        dat_ref[...] = _dot_nt(dba_v, wao_ref[...]).astype(BF16)
        pooled = _pooled(xc_ref[...], xprev_ref[...], i).astype(BF16)
        mixed = _pool_mix(pooled, pw_ref)
        ps = ps_ref[...]
        _acc(dwpo_ref, _dot_tn(dbp_v, (mixed * ps).astype(BF16)), i == 0)
        dms = _dot(dbp_v, wpo_ref[...])
        _acc(dps_ref, jnp.sum(dms * mixed, axis=0, keepdims=True), i == 0)
        dmixed = (dms * ps).astype(BF16)
        dpooled = []
        for gi in range(NG):
            cols = slice(gi * POOL_GROUP, (gi + 1) * POOL_GROUP)
            _acc(dpw_ref.at[gi], _dot_tn(pooled[:, cols], dmixed[:, cols]), i == 0)
            dpooled.append(_dot_nt(dmixed[:, cols], pw_ref[gi]))
        dpl_ref[...] = jnp.concatenate(dpooled, axis=1)

    def row(width):
        return pl.BlockSpec((tm, width), lambda i: (i, 0))

    def whole(shape):
        nd = len(shape)
        return pl.BlockSpec(shape, lambda i: (0,) * nd)

    return pl.pallas_call(
        body, name="mix_bwd_branch", grid=(nt,),
        in_specs=[row(D), row(D), row(D), row(PW), _halo_specs(tm, PW, S, False), whole(pw.shape), whole(pscale.shape),
                  whole(wpot.shape), whole(wao.shape)],
        out_specs=[row(D), row(PW), whole((D, D)), whole((D, PW)), whole(pw.shape), whole((1, PW))],
        out_shape=[jax.ShapeDtypeStruct((S, D), BF16), jax.ShapeDtypeStruct((S, PW), F32),
                   jax.ShapeDtypeStruct((D, D), F32), jax.ShapeDtypeStruct((D, PW), F32),
                   jax.ShapeDtypeStruct(pw.shape, F32), jax.ShapeDtypeStruct((1, PW), F32)],
        compiler_params=_params(dimension_semantics=("arbitrary",)),
    )(dbp, dba, attn, xp, xp, pw, pscale, wpot, wao)


def _pool_bwd(dpooled, tm=512):
    S, PW = dpooled.shape
    nt = S // tm

    def body(dc_ref, dnext_ref, dxp_ref):
        i = pl.program_id(0)
        dc = dc_ref[...]
        dh = jnp.concatenate([dc, jnp.where(i < nt - 1, dnext_ref[...], 0.0)], axis=0)
        rows = tm + POOL_HALO
        t = lax.broadcasted_iota(jnp.int32, (rows, 1), 0) + i * tm
        out = []
        for gi, w in enumerate(POOL_WINDOWS):
            cols = slice(gi * POOL_GROUP, (gi + 1) * POOL_GROUP)
            acc = dh[:, cols] / jnp.minimum(t + 1, w).astype(F32)
            sh = 1
            while sh < w:
                acc = acc + pltpu.roll(acc, rows - sh, 0)
                sh *= 2
            out.append(acc[:tm, :] - dc[:, cols])
        dxp_ref[...] = jnp.concatenate(out, axis=1).astype(BF16)

    return pl.pallas_call(
        body, name="pool_bwd", grid=(nt,),
        in_specs=[pl.BlockSpec((tm, PW), lambda i: (i, 0)), _halo_specs(tm, PW, S, True)],
        out_specs=pl.BlockSpec((tm, PW), lambda i: (i, 0)), out_shape=jax.ShapeDtypeStruct((S, PW), BF16),
        compiler_params=_params(dimension_semantics=("arbitrary",)),
    )(dpooled, dpooled)


def _in_bwd_w(pieces, u, tm=1024, tf=256):
    S, D = u.shape
    nt = S // tm
    starts = [off // tf for off, _ in SEGMENTS]
    counts = [width // tf for _, width in SEGMENTS]
    nf = sum(counts)

    def body(*refs):
        piece_refs, u_ref, dw_ref = refs[:len(SEGMENTS)], refs[len(SEGMENTS)], refs[len(SEGMENTS) + 1]
        j = pl.program_id(0)
        i = pl.program_id(1)
        for ref, st, cnt in zip(piece_refs, starts, counts):
            @pl.when(jnp.logical_and(j >= st, j < st + cnt))
            def _(ref=ref):
                _acc(dw_ref, _dot_tn(ref[...].astype(BF16), u_ref[...]), i == 0)

    def piece_spec(st, cnt):
        return pl.BlockSpec((tm, tf), lambda j, i: (i, jnp.clip(j - st, 0, cnt - 1)))

    return pl.pallas_call(
        body, name="in_bwd_w", grid=(nf, nt),
        in_specs=[piece_spec(st, cnt) for st, cnt in zip(starts, counts)] + [pl.BlockSpec((tm, D), lambda j, i: (i, 0))],
        out_specs=pl.BlockSpec((tf, D), lambda j, i: (j, 0)),
        out_shape=jax.ShapeDtypeStruct((nf * tf, D), F32),
        compiler_params=_params(dimension_semantics=("arbitrary", "arbitrary")),
    )(*pieces, u)


def _in_bwd_x(pieces, wint, h, dh, gnorm, tm=512):
    S, D = h.shape
    nt = S // tm

    def body(*refs):
        piece_refs = refs[:len(SEGMENTS)]
        w_ref, h_ref, dh_ref, g_ref, dx_ref, dxh_ref, dg_ref = refs[len(SEGMENTS):]
        i = pl.program_id(0)
        du = jnp.zeros((tm, D), F32)
        for (off, width), ref in zip(SEGMENTS, piece_refs):
            du = du + _dot(ref[...].astype(BF16), w_ref[off:off + width, :])
        dx, dg_rows = _rms_bwd(du, h_ref[...], g_ref[...])
        out = dh_ref[...] + dx
        dx_ref[...] = out
        dxh_ref[...] = (0.5 * out).astype(BF16)
        _acc(dg_ref, jnp.sum(dg_rows, axis=0, keepdims=True), i == 0)

    def row(width):
        return pl.BlockSpec((tm, width), lambda i: (i, 0))

    vec = pl.BlockSpec((1, D), lambda i: (0, 0))
    return pl.pallas_call(
        body, name="in_bwd_x", grid=(nt,),
        in_specs=[row(width) for _, width in SEGMENTS] + [pl.BlockSpec(wint.shape, lambda i: (0, 0)), row(D), row(D), vec],
        out_specs=[row(D), row(D), vec],
        out_shape=[jax.ShapeDtypeStruct((S, D), F32), jax.ShapeDtypeStruct((S, D), BF16),
                   jax.ShapeDtypeStruct((1, D), F32)],
        compiler_params=_params(dimension_semantics=("arbitrary",)),
    )(*pieces, wint, h, dh, gnorm)


def _row_tile(rows):
    for t in (512, 480, 352, 256, 128, 64, 32, 16, 8):
        if rows % t == 0:
            return t
    return rows


def _adamw(w, g, m, v, name):
    R, C = w.shape
    tr = _row_tile(R)

    def body(w_ref, g_ref, m_ref, v_ref, d_ref, mo_ref, vo_ref):
        gv = g_ref[...]
        mn = ADAM_B1 * m_ref[...] + (1.0 - ADAM_B1) * gv
        vn = ADAM_B2 * v_ref[...] + (1.0 - ADAM_B2) * (gv * gv)
        m_hat = mn / (1.0 - ADAM_B1 ** ADAM_STEP)
        v_hat = vn / (1.0 - ADAM_B2 ** ADAM_STEP)
        d_ref[...] = -ADAM_LR * (m_hat / (jnp.sqrt(v_hat) + ADAM_EPS) + ADAM_WD * w_ref[...])
        mo_ref[...] = mn
        vo_ref[...] = vn

    blk = pl.BlockSpec((tr, C), lambda i: (i, 0))
    sh = jax.ShapeDtypeStruct((R, C), F32)
    return pl.pallas_call(
        body, name=name, grid=(R // tr,), in_specs=[blk] * 4, out_specs=[blk] * 3, out_shape=[sh] * 3,
        compiler_params=_params(dimension_semantics=("arbitrary",)),
    )(w, g, m, v)


def _cast_place(w, place, name):
    R, C = w.shape
    tr = _row_tile(R)
    per = R // tr

    def body(p_ref, w_ref, o_ref):
        o_ref[...] = w_ref[...].astype(BF16)

    grid_spec = pltpu.PrefetchScalarGridSpec(
        num_scalar_prefetch=1, grid=(per,),
        in_specs=[pl.BlockSpec((tr, C), lambda i, p: (i, 0))],
        out_specs=pl.BlockSpec((tr, C), lambda i, p: (p[0] * per + i, 0)))
    return pl.pallas_call(
        body, name=name, grid_spec=grid_spec, out_shape=jax.ShapeDtypeStruct((N_SHARDS * R, C), BF16),
        compiler_params=_params(dimension_semantics=("arbitrary",)),
    )(place, w)


def _sum_halves(g4, recv, place, name):
    NS, R, C = g4.shape
    hr = R // 2
    tr = _row_tile(hr)
    per = hr // tr

    def body(p_ref, g_ref, r_ref, o_ref):
        o_ref[...] = (g_ref[...] + r_ref[...]).astype(BF16)

    grid_spec = pltpu.PrefetchScalarGridSpec(
        num_scalar_prefetch=1, grid=(NS, per),
        in_specs=[pl.BlockSpec((1, tr, C), lambda s, i, p: (s, p[1] * per + i, 0)),
                  pl.BlockSpec((1, tr, C), lambda s, i, p: (s, i, 0))],
        out_specs=pl.BlockSpec((1, tr, C), lambda s, i, p: (s, i, 0)))
    return pl.pallas_call(
        body, name=name, grid_spec=grid_spec, out_shape=jax.ShapeDtypeStruct((NS, hr, C), BF16),
        compiler_params=_params(dimension_semantics=("arbitrary", "arbitrary")),
    )(place, g4, recv)


def _sum_quarters(h4, recv3, place, name):
    NS, hr, C = h4.shape
    tr = _row_tile(hr)
    per = hr // tr

    def body(p_ref, h_ref, r_ref, o_ref):
        acc = h_ref[0].astype(F32)
        for k in range(N_SHARDS - 1):
            acc = acc + r_ref[k].astype(F32)
        o_ref[...] = acc

    grid_spec = pltpu.PrefetchScalarGridSpec(
        num_scalar_prefetch=1, grid=(per,),
        in_specs=[pl.BlockSpec((1, tr, C), lambda i, p: (p[0], i, 0)),
                  pl.BlockSpec((N_SHARDS - 1, tr, C), lambda i, p: (0, i, 0))],
        out_specs=pl.BlockSpec((tr, C), lambda i, p: (p[1] * per + i, 0)))
    return pl.pallas_call(
        body, name=name, grid_spec=grid_spec, out_shape=jax.ShapeDtypeStruct((2 * hr, C), F32),
        compiler_params=_params(dimension_semantics=("arbitrary",)),
    )(place, h4, recv3)


def _place():
    x, y, c = lax.axis_index("x"), lax.axis_index("y"), lax.axis_index("c")
    chips = [(1 - x, y), (x, 1 - y), (1 - x, 1 - y)]
    return x, y, c, chips


def _any_specs(n):
    return [pl.BlockSpec(memory_space=pl.ANY)] * n


def _gather_weights(wholes):
    nw = len(wholes)

    def body(*refs):
        bufs = refs[nw:2 * nw]
        ici_send, ici_recv, d2d_send, d2d_recv = refs[2 * nw:]
        x, y, c, chips = _place()
        me = 2 * x + y
        sibling = (x, y, 1 - c)

        def half(w, chip, core):
            hr = bufs[w].shape[0] // (2 * N_SHARDS)
            return bufs[w].at[pl.ds(pl.multiple_of((2 * chip + core) * hr, 16), hr)]

        def copy(w, k, rows, to, sems):
            return pltpu.make_async_remote_copy(src_ref=rows, dst_ref=rows, send_sem=sems[0].at[w, k],
                                                recv_sem=sems[1].at[w, k], device_id=to, device_id_type=MESH)

        ici, d2d = (ici_send, ici_recv), (d2d_send, d2d_recv)
        sends = []
        for w in range(nw):
            for k, (px, py) in enumerate(chips):
                sends.append(copy(w, k, half(w, me, c), (px, py, c), ici))
                sends[-1].start()
        for w in range(nw):
            for k, (px, py) in enumerate(chips):
                landed = half(w, 2 * px + py, c)
                copy(w, k, landed, (px, py, c), ici).wait_recv()
                sends.append(copy(w, k, landed, sibling, d2d))
                sends[-1].start()
        for w in range(nw):
            for k, (px, py) in enumerate(chips):
                copy(w, k, half(w, 2 * px + py, 1 - c), sibling, d2d).wait_recv()
        for cp in sends:
            cp.wait_send()

    sem = pltpu.SemaphoreType.DMA((nw, N_SHARDS - 1))
    return pl.pallas_call(
        body, name="gather_weights", in_specs=_any_specs(nw), out_specs=_any_specs(nw),
        out_shape=[jax.ShapeDtypeStruct(s.shape, s.dtype) for s in wholes],
        input_output_aliases={i: i for i in range(nw)}, scratch_shapes=[sem, sem, sem, sem],
        compiler_params=pltpu.CompilerParams(has_side_effects=True),
    )(*wholes)


def _swap_halves(grads):
    nw = len(grads)

    def body(*refs):
        ins, outs = refs[:nw], refs[nw:2 * nw]
        send_sem, recv_sem = refs[2 * nw:]
        x, y, c, _ = _place()
        sibling = (x, y, 1 - c)
        cps = []
        for w in range(nw):
            hr = ins[w].shape[1] // 2
            theirs = pl.ds(pl.multiple_of((1 - c) * hr, 8), hr)
            cp = pltpu.make_async_remote_copy(
                src_ref=ins[w].at[:, theirs], dst_ref=outs[w], send_sem=send_sem.at[w], recv_sem=recv_sem.at[w],
                device_id=sibling, device_id_type=MESH)
            cp.start()
            cps.append(cp)
        for cp in cps:
            cp.wait()

    return pl.pallas_call(
        body, name="swap_halves", in_specs=_any_specs(nw), out_specs=_any_specs(nw),
        out_shape=[jax.ShapeDtypeStruct((g.shape[0], g.shape[1] // 2, g.shape[2]), g.dtype) for g in grads],
        scratch_shapes=[pltpu.SemaphoreType.DMA((nw,)), pltpu.SemaphoreType.DMA((nw,))],
        compiler_params=pltpu.CompilerParams(has_side_effects=True),
    )(*grads)


def _exchange_quarters(halves):
    nw = len(halves)

    def body(*refs):
        ins, outs = refs[:nw], refs[nw:2 * nw]
        send_sem, recv_sem = refs[2 * nw:]
        x, y, c, chips = _place()
        cps = []
        for w in range(nw):
            for k, (px, py) in enumerate(chips):
                cp = pltpu.make_async_remote_copy(
                    src_ref=ins[w].at[2 * px + py], dst_ref=outs[w].at[k], send_sem=send_sem.at[w, k],
                    recv_sem=recv_sem.at[w, k], device_id=(px, py, c), device_id_type=MESH)
                cp.start()
                cps.append(cp)
        for cp in cps:
            cp.wait()

    sem = pltpu.SemaphoreType.DMA((nw, N_SHARDS - 1))
    return pl.pallas_call(
        body, name="exchange_quarters", in_specs=_any_specs(nw), out_specs=_any_specs(nw),
        out_shape=[jax.ShapeDtypeStruct((N_SHARDS - 1,) + h.shape[1:], h.dtype) for h in halves],
        scratch_shapes=[sem, sem],
        compiler_params=pltpu.CompilerParams(has_side_effects=True),
    )(*halves)


def _share_halves(grads):
    nw = len(grads)

    def body(*refs):
        bufs = refs[nw:2 * nw]
        send_sem, recv_sem = refs[2 * nw:]
        x, y, c, _ = _place()
        sibling = (x, y, 1 - c)
        cps = []
        for w in range(nw):
            hr = bufs[w].shape[0] // 2
            mine = bufs[w].at[pl.ds(pl.multiple_of(c * hr, 8), hr)]
            cp = pltpu.make_async_remote_copy(src_ref=mine, dst_ref=mine, send_sem=send_sem.at[w], recv_sem=recv_sem.at[w],
                                              device_id=sibling, device_id_type=MESH)
            cp.start()
            cps.append(cp)
        for w in range(nw):
            hr = bufs[w].shape[0] // 2
            theirs = bufs[w].at[pl.ds(pl.multiple_of((1 - c) * hr, 8), hr)]
            pltpu.make_async_remote_copy(src_ref=theirs, dst_ref=theirs, send_sem=send_sem.at[w], recv_sem=recv_sem.at[w],
                                         device_id=sibling, device_id_type=MESH).wait_recv()
        for cp in cps:
            cp.wait_send()

    return pl.pallas_call(
        body, name="share_halves", in_specs=_any_specs(nw), out_specs=_any_specs(nw),
        out_shape=[jax.ShapeDtypeStruct(g.shape, g.dtype) for g in grads],
        input_output_aliases={i: i for i in range(nw)},
        scratch_shapes=[pltpu.SemaphoreType.DMA((nw,)), pltpu.SemaphoreType.DMA((nw,))],
        compiler_params=pltpu.CompilerParams(has_side_effects=True),
    )(*grads)


def _sum_small(vec):
    R, C = vec.shape
    ndev = 8

    def body(v_ref, o_ref, buf, send_sem, recv_sem):
        x, y, c, _ = _place()
        me = 4 * x + 2 * y + c
        buf[me] = v_ref[...]
        cps = []
        for r in range(1, ndev):
            peer = (x ^ (r >> 2), y ^ ((r >> 1) & 1), c ^ (r & 1))
            cp = pltpu.make_async_remote_copy(
                src_ref=v_ref, dst_ref=buf.at[me], send_sem=send_sem.at[r - 1], recv_sem=recv_sem.at[r - 1],
                device_id=peer, device_id_type=MESH)
            cp.start()
            cps.append(cp)
        for r in range(1, ndev):
            src = (4 * (x ^ (r >> 2)) + 2 * (y ^ ((r >> 1) & 1)) + (c ^ (r & 1)))
            pltpu.make_async_remote_copy(
                src_ref=v_ref, dst_ref=buf.at[src], send_sem=send_sem.at[r - 1], recv_sem=recv_sem.at[r - 1],
                device_id=(x, y, c), device_id_type=MESH).wait_recv()
        acc = buf[0]
        for d in range(1, ndev):
            acc = acc + buf[d]
        o_ref[...] = acc
        for cp in cps:
            cp.wait_send()

    return pl.pallas_call(
        body, name="sum_small", in_specs=[pl.BlockSpec(memory_space=pltpu.VMEM)],
        out_specs=pl.BlockSpec(memory_space=pltpu.VMEM), out_shape=jax.ShapeDtypeStruct((R, C), F32),
        scratch_shapes=[pltpu.VMEM((ndev, R, C), F32), pltpu.SemaphoreType.DMA((ndev - 1,)),
                        pltpu.SemaphoreType.DMA((ndev - 1,))],
        compiler_params=pltpu.CompilerParams(has_side_effects=True),
    )(vec)


SMALL = ("ffn1_norm", "mix_norm", "pool_w", "pool_scale", "q_norm", "k_norm", "sinks", "gate_bias", "ffn2_norm")
SMALL_COLS = 1024
LARGE = ("ffn1_w_gate", "ffn1_w_up", "ffn1_w_down", "w_in", "w_pool_out", "w_attn_out", "w_out",
         "ffn2_w_gate", "ffn2_w_up", "ffn2_w_down")
TRANSPOSED = ("ffn1_w_gate", "ffn1_w_up", "w_in", "w_pool_out", "ffn2_w_gate", "ffn2_w_up")
WEIGHTS = ("ffn1_norm", "ffn1_w_gate", "ffn1_w_up", "ffn1_w_down", "mix_norm", "w_in", "pool_w", "pool_scale",
           "w_pool_out", "q_norm", "k_norm", "sinks", "w_attn_out", "gate_bias", "w_out", "ffn2_norm",
           "ffn2_w_gate", "ffn2_w_up", "ffn2_w_down")


def _pack_small(parts):
    flat = jnp.concatenate([p.reshape(-1) for p in parts])
    rows = -(-flat.shape[0] // (8 * SMALL_COLS)) * 8
    return jnp.pad(flat, (0, rows * SMALL_COLS - flat.shape[0])).reshape(rows, SMALL_COLS)


def _unpack_small(packed, like):
    flat = packed.reshape(-1)
    out, off = [], 0
    for p in like:
        out.append(flat[off:off + p.size].reshape(p.shape))
        off += p.size
    return out, flat[off]


def _local_step(x, target, w, small):
    D = x.shape[1]
    g1 = small["ffn1_norm"].reshape(1, D)
    gm = small["mix_norm"].reshape(1, D)
    g2 = small["ffn2_norm"].reshape(1, D)
    qw = small["q_norm"].reshape(1, HEAD_DIM)
    kw = small["k_norm"].reshape(1, HEAD_DIM)
    bias = small["gate_bias"].reshape(1, 2 * D)
    pscale = small["pool_scale"].reshape(1, -1)
    pw = small["pool_w"].astype(BF16)
    sink_rows = jnp.repeat(small["sinks"], ATTN_BLOCK).reshape(N_KV_HEADS, GQA_GROUP * ATTN_BLOCK, 1)

    n1, a1, b1, h1 = _ffn_fwd(x, g1, w["ffn1_w_gate"], w["ffn1_w_up"], w["ffn1_w_down"], None, "ffn1_fwd")
    u, xp, q, kv, gl = _mix_proj(h1, gm, w["w_in"])
    attn = _attn_fwd(q, kv, qw, kw, sink_rows)
    h2, bp, ba = _mix_out(xp, attn, gl, bias, pw, pscale, w["w_pool_out"], w["w_attn_out"], w["w_out"], h1)
    n2, a2, b2, dy, dyh, loss = _ffn_fwd(h2, g2, w["ffn2_w_gate"], w["ffn2_w_up"], w["ffn2_w_down"], target, "ffn2_fwd")

    gw, gs = {}, {}
    da, db, gw["ffn2_w_gate"], gw["ffn2_w_up"], gw["ffn2_w_down"] = _ffn_bwd_w(dyh, n2, a2, b2, w["ffn2_w_down"], "ffn2_bwd_w")
    dh2, _, gs["ffn2_norm"] = _ffn_bwd_x(da, db, w["ffn2_w_gate"], w["ffn2_w_up"], h2, dy, g2, "ffn2_bwd_x")
    dgl, dbp, dba, gw["w_out"], gs["gate_bias"] = _mix_bwd_gate(dh2, w["w_out"], bp, ba, gl, bias)
    dattn, dpooled, gw["w_attn_out"], gw["w_pool_out"], gs["pool_w"], gs["pool_scale"] = _mix_bwd_branch(
        dbp, dba, attn, xp, pw, pscale, w["w_pool_out"], w["w_attn_out"])
    dq, dkv, gs["q_norm"], gs["k_norm"], dsk = _attn_bwd(q, kv, dattn, qw, kw, sink_rows)
    gs["sinks"] = dsk[:, :, 0]
    pieces = (_pool_bwd(dpooled), dq, dkv, dgl)
    gw["w_in"] = _in_bwd_w(pieces, u)
    dh1, dh1h, gs["mix_norm"] = _in_bwd_x(pieces, w["w_in"], h1, dh2, gm)
    da, db, gw["ffn1_w_gate"], gw["ffn1_w_up"], gw["ffn1_w_down"] = _ffn_bwd_w(dh1h, n1, a1, b1, w["ffn1_w_down"], "ffn1_bwd_w")
    gx, _, gs["ffn1_norm"] = _ffn_bwd_x(da, db, w["ffn1_w_gate"], w["ffn1_w_up"], x, dh1, g1, "ffn1_bwd_x")
    return loss[0, 0], gx, gw, gs


def _row_form(name, a):
    return a.T if name in TRANSPOSED else a


def kernel(x, ffn1_norm, ffn1_w_gate, ffn1_w_up, ffn1_w_down, mix_norm, w_in, pool_w, pool_scale, w_pool_out, q_norm, k_norm, sinks, w_attn_out, gate_bias, w_out, ffn2_norm, ffn2_w_gate, ffn2_w_up, ffn2_w_down, loss_target, m_ffn1_norm, m_ffn1_w_gate, m_ffn1_w_up, m_ffn1_w_down, m_mix_norm, m_w_in, m_pool_w, m_pool_scale, m_w_pool_out, m_q_norm, m_k_norm, m_sinks, m_w_attn_out, m_gate_bias, m_w_out, m_ffn2_norm, m_ffn2_w_gate, m_ffn2_w_up, m_ffn2_w_down, v_ffn1_norm, v_ffn1_w_gate, v_ffn1_w_up, v_ffn1_w_down, v_mix_norm, v_w_in, v_pool_w, v_pool_scale, v_w_pool_out, v_q_norm, v_k_norm, v_sinks, v_w_attn_out, v_gate_bias, v_w_out, v_ffn2_norm, v_ffn2_w_gate, v_ffn2_w_up, v_ffn2_w_down):
    args = dict(locals())
    wts = {n: _row_form(n, args[n]) for n in WEIGHTS}
    mom = {n: _row_form(n, args["m_" + n]) for n in WEIGHTS}
    var = {n: _row_form(n, args["v_" + n]) for n in WEIGHTS}
    shard = 2 * lax.axis_index("x") + lax.axis_index("y")
    place = jnp.stack([shard, lax.axis_index("c")]).astype(jnp.int32)

    placed = [_cast_place(wts[n], place, "cast_" + n) for n in LARGE]
    full = dict(zip(LARGE, _gather_weights(placed)))

    small = {n: wts[n] for n in SMALL}
    loss, grad_x, gw, gs = _local_step(x[0], loss_target[0], full, small)

    g4 = [gw[n].reshape(N_SHARDS, -1, gw[n].shape[-1]) for n in LARGE]
    recv = _swap_halves(g4)
    halves = [_sum_halves(g, r, place, "sum_halves_" + n) for n, g, r in zip(LARGE, g4, recv)]
    recv3 = _exchange_quarters(halves)
    reduced = [_sum_quarters(h, r, place, "sum_quarters_" + n) for n, h, r in zip(LARGE, halves, recv3)]
    grads = dict(zip(LARGE, _share_halves(reduced)))

    small_parts = [gs[n] for n in SMALL] + [loss.reshape(1)]
    summed = _sum_small(_pack_small(small_parts))
    small_grads, loss_sum = _unpack_small(summed, [wts[n] for n in SMALL])
    grads.update(dict(zip(SMALL, small_grads)))

    delta, new_m, new_v = {}, {}, {}
    for n in LARGE:
        delta[n], new_m[n], new_v[n] = _adamw(wts[n], grads[n], mom[n], var[n], "adamw_" + n)
    zero = jnp.zeros((1,), F32)
    packed = [_pack_small([d[n] for n in SMALL] + [zero]) for d in (wts, mom, var)]
    ds, ms, vs = _adamw(packed[0], summed, packed[1], packed[2], "adamw_small")
    like = [wts[n] for n in SMALL]
    for out, packed_out in ((delta, ds), (new_m, ms), (new_v, vs)):
        out.update(dict(zip(SMALL, _unpack_small(packed_out, like)[0])))

    outs = []
    for d in (grads, delta, new_m, new_v):
        outs += [_row_form(n, d[n]) for n in WEIGHTS]
    return (loss_sum, grad_x[None], *outs)
```

```python
from typing import Callable, NamedTuple

import jax
import jax.numpy as jnp
from jax import lax
from jax.experimental import pallas as pl
from jax.experimental.pallas import tpu as pltpu

F32 = jnp.float32
BF16 = jnp.bfloat16
RMS_EPS = 1e-6
POOL_WINDOWS = (2, 4, 8, 16)
POOL_GROUP = 128
POOL_HALO = 16
HEAD_DIM = 64
GQA_GROUP = 8
N_KV_HEADS = 2
ATTN_BLOCK = 128
SCALE = HEAD_DIM ** -0.5
NEG = -1e30
N_SHARDS = 4
ADAM_LR, ADAM_B1, ADAM_B2, ADAM_EPS, ADAM_WD, ADAM_STEP = 0.001, 0.9, 0.999, 1e-08, 0.01, 10
VMEM_LIMIT = 56 * 1024 * 1024
MESH = pl.DeviceIdType.MESH
SEG_POOL, SEG_Q, SEG_KV, SEG_GATE = (0, 512), (512, 1024), (1536, 256), (1792, 2048)
SEGMENTS = (SEG_POOL, SEG_Q, SEG_KV, SEG_GATE)


def _params(**kw):
    return pltpu.CompilerParams(vmem_limit_bytes=VMEM_LIMIT, **kw)


def _dot(a, b):
    return jnp.dot(a, b, preferred_element_type=F32)


def _dot_nt(a, b):
    return lax.dot_general(a, b, (((1,), (1,)), ((), ())), preferred_element_type=F32)


def _dot_tn(a, b):
    return lax.dot_general(a, b, (((0,), (0,)), ((), ())), preferred_element_type=F32)


def _rinv(x):
    return lax.rsqrt(jnp.mean(x * x, axis=-1, keepdims=True) + RMS_EPS)


def _rms_bwd(dn, x, g):
    r = _rinv(x)
    xr = x * r
    z = dn * g
    dx = r * (z - xr * jnp.mean(z * xr, axis=-1, keepdims=True))
    return dx, dn * xr


def _acc(ref, val, first):
    @pl.when(first)
    def _():
        ref[...] = val

    @pl.when(jnp.logical_not(first))
    def _():
        ref[...] += val


def _ffn_fwd(h, gnorm, wgt, wut, wd, target, name, tm=512, tf=1408):
    S, D = h.shape
    F = wd.shape[0]
    with_loss = target is not None
    nt, nf = S // tm, F // tf

    def body(*refs):
        if with_loss:
            h_ref, g_ref, wg_ref, wu_ref, wd_ref, t_ref, n_ref, a_ref, b_ref, acc_ref, dyh_ref, loss_ref = refs
        else:
            h_ref, g_ref, wg_ref, wu_ref, wd_ref, n_ref, a_ref, b_ref, acc_ref = refs
        i = pl.program_id(0)
        j = pl.program_id(1)

        @pl.when(j == 0)
        def _():
            x = h_ref[...]
            n_ref[...] = (x * _rinv(x) * g_ref[...]).astype(BF16)

        n = n_ref[...]
        a = _dot_nt(n, wg_ref[...])
        b = _dot_nt(n, wu_ref[...])
        a_ref[...] = a.astype(BF16)
        b_ref[...] = b.astype(BF16)
        s = (a * jax.nn.sigmoid(a) * b).astype(BF16)
        _acc(acc_ref, _dot(s, wd_ref[...]), j == 0)

        @pl.when(j == nf - 1)
        def _():
            y = h_ref[...] + 0.5 * acc_ref[...]
            if with_loss:
                e = y - t_ref[...]
                dy = e * (1.0 / D)
                acc_ref[...] = dy
                dyh_ref[...] = (0.5 * dy).astype(BF16)
                tot = jnp.sum(jnp.sum(e * e, axis=1, keepdims=True), axis=0, keepdims=True) * (0.5 / D)
                _acc(loss_ref, jnp.broadcast_to(tot, loss_ref.shape), i == 0)
            else:
                acc_ref[...] = y

    row = pl.BlockSpec((tm, D), lambda i, j: (i, 0))
    wblk = pl.BlockSpec((tf, D), lambda i, j: (j, 0))
    act = pl.BlockSpec((tm, tf), lambda i, j: (i, j))
    in_specs = [row, pl.BlockSpec((1, D), lambda i, j: (0, 0)), wblk, wblk, wblk]
    out_specs = [row, act, act, row]
    out_shape = [jax.ShapeDtypeStruct((S, D), BF16), jax.ShapeDtypeStruct((S, F), BF16),
                 jax.ShapeDtypeStruct((S, F), BF16), jax.ShapeDtypeStruct((S, D), F32)]
    args = [h, gnorm, wgt, wut, wd]
    if with_loss:
        in_specs.append(row)
        args.append(target)
        out_specs += [row, pl.BlockSpec((8, 128), lambda i, j: (0, 0))]
        out_shape += [jax.ShapeDtypeStruct((S, D), BF16), jax.ShapeDtypeStruct((8, 128), F32)]
    return pl.pallas_call(
        body, name=name, grid=(nt, nf), in_specs=in_specs, out_specs=out_specs, out_shape=out_shape,
        compiler_params=_params(dimension_semantics=("arbitrary", "arbitrary")),
    )(*args)


TOKEN_SPEC = pl.BlockSpec(memory_space=pl.ANY)


def _ffn_bwd_w(dyh, n, a, b, wd, token, name, tm=1024, tf=256):
    S, D = n.shape
    F = wd.shape[0]
    nt, nf = S // tm, F // tf

    def body(dyh_ref, n_ref, a_ref, b_ref, wd_ref, _, da_ref, db_ref, dwg_ref, dwu_ref, dwd_ref):
        i = pl.program_id(1)
        dyh_t = dyh_ref[...]
        n_t = n_ref[...]
        av = a_ref[...].astype(F32)
        bv = b_ref[...].astype(F32)
        ds = _dot_nt(dyh_t, wd_ref[...])
        sig = jax.nn.sigmoid(av)
        silu = av * sig
        da = (ds * bv * (sig * (1.0 + av * (1.0 - sig)))).astype(BF16)
        db = (ds * silu).astype(BF16)
        s = (silu * bv).astype(BF16)
        da_ref[...] = da
        db_ref[...] = db
        _acc(dwg_ref, _dot_tn(da, n_t), i == 0)
        _acc(dwu_ref, _dot_tn(db, n_t), i == 0)
        _acc(dwd_ref, _dot_tn(s, dyh_t), i == 0)

    row = pl.BlockSpec((tm, D), lambda j, i: (i, 0))
    act = pl.BlockSpec((tm, tf), lambda j, i: (i, j))
    wblk = pl.BlockSpec((tf, D), lambda j, i: (j, 0))
    return pl.pallas_call(
        body, name=name, grid=(nf, nt), in_specs=[row, row, act, act, wblk, TOKEN_SPEC],
        out_specs=[act, act, wblk, wblk, wblk],
        out_shape=[jax.ShapeDtypeStruct((S, F), BF16), jax.ShapeDtypeStruct((S, F), BF16),
                   jax.ShapeDtypeStruct((F, D), F32), jax.ShapeDtypeStruct((F, D), F32),
                   jax.ShapeDtypeStruct((F, D), F32)],
        compiler_params=_params(dimension_semantics=("arbitrary", "arbitrary")),
    )(dyh, n, a, b, wd, token)


def _ffn_bwd_x(da, db, wgt, wut, x, dh, gnorm, name, tm=512, tf=1408):
    S, D = x.shape
    F = wgt.shape[0]
    nt, nf = S // tm, F // tf

    def body(da_ref, db_ref, wg_ref, wu_ref, x_ref, dh_ref, g_ref, dx_ref, dxh_ref, dg_ref):
        i = pl.program_id(0)
        j = pl.program_id(1)
        _acc(dx_ref, _dot(da_ref[...], wg_ref[...]) + _dot(db_ref[...], wu_ref[...]), j == 0)

        @pl.when(j == nf - 1)
        def _():
            dx, dg_rows = _rms_bwd(dx_ref[...], x_ref[...], g_ref[...])
            out = dh_ref[...] + dx
            dx_ref[...] = out
            dxh_ref[...] = (0.5 * out).astype(BF16)
            _acc(dg_ref, jnp.sum(dg_rows, axis=0, keepdims=True), i == 0)

    row = pl.BlockSpec((tm, D), lambda i, j: (i, 0))
    act = pl.BlockSpec((tm, tf), lambda i, j: (i, j))
    wblk = pl.BlockSpec((tf, D), lambda i, j: (j, 0))
    vec = pl.BlockSpec((1, D), lambda i, j: (0, 0))
    return pl.pallas_call(
        body, name=name, grid=(nt, nf), in_specs=[act, act, wblk, wblk, row, row, vec],
        out_specs=[row, row, vec],
        out_shape=[jax.ShapeDtypeStruct((S, D), F32), jax.ShapeDtypeStruct((S, D), BF16),
                   jax.ShapeDtypeStruct((1, D), F32)],
        compiler_params=_params(dimension_semantics=("arbitrary", "arbitrary")),
    )(da, db, wgt, wut, x, dh, gnorm)


def _mix_proj(h, gnorm, wint, tm=512):
    S, D = h.shape
    nt = S // tm

    def body(h_ref, g_ref, w_ref, u_ref, xp_ref, q_ref, kv_ref, gl_ref):
        x = h_ref[...]
        u = (x * _rinv(x) * g_ref[...]).astype(BF16)
        u_ref[...] = u
        for (off, width), ref in zip(SEGMENTS, (xp_ref, q_ref, kv_ref, gl_ref)):
            ref[...] = _dot_nt(u, w_ref[off:off + width, :]).astype(ref.dtype)

    def row(width):
        return pl.BlockSpec((tm, width), lambda i: (i, 0))

    return pl.pallas_call(
        body, name="mix_proj", grid=(nt,),
        in_specs=[row(D), pl.BlockSpec((1, D), lambda i: (0, 0)), pl.BlockSpec(wint.shape, lambda i: (0, 0))],
        out_specs=[row(D), row(SEG_POOL[1]), row(SEG_Q[1]), row(SEG_KV[1]), row(SEG_GATE[1])],
        out_shape=[jax.ShapeDtypeStruct((S, D), BF16), jax.ShapeDtypeStruct((S, SEG_POOL[1]), F32),
                   jax.ShapeDtypeStruct((S, SEG_Q[1]), BF16), jax.ShapeDtypeStruct((S, SEG_KV[1]), BF16),
                   jax.ShapeDtypeStruct((S, SEG_GATE[1]), BF16)],
        compiler_params=_params(dimension_semantics=("arbitrary",)),
    )(h, gnorm, wint)


def _stack_heads(x, g):
    return jnp.concatenate([x[:, (GQA_GROUP * g + hh) * HEAD_DIM:(GQA_GROUP * g + hh + 1) * HEAD_DIM]
                            for hh in range(GQA_GROUP)], axis=0)


def _unstack_heads(ref, val, g):
    for hh in range(GQA_GROUP):
        lo = (GQA_GROUP * g + hh) * HEAD_DIM
        ref[:, lo:lo + HEAD_DIM] = val[hh * ATTN_BLOCK:(hh + 1) * ATTN_BLOCK, :]


def _attn_probs(qn, kc, kp, sink, n):
    rows = GQA_GROUP * ATTN_BLOCK
    qi = lax.broadcasted_iota(jnp.int32, (rows, ATTN_BLOCK), 0) % ATTN_BLOCK
    kj = lax.broadcasted_iota(jnp.int32, (rows, ATTN_BLOCK), 1)
    s_c = jnp.where(kj <= qi, _dot_nt(qn, kc), NEG)
    s_p = jnp.where(jnp.logical_and(kj > qi, n > 0), _dot_nt(qn, kp), NEG)
    m = jnp.maximum(jnp.maximum(jnp.max(s_c, axis=-1, keepdims=True), jnp.max(s_p, axis=-1, keepdims=True)), sink)
    e_c = jnp.exp(s_c - m)
    e_p = jnp.exp(s_p - m)
    e_s = jnp.exp(sink - m)
    inv = 1.0 / (jnp.sum(e_c, axis=-1, keepdims=True) + jnp.sum(e_p, axis=-1, keepdims=True) + e_s)
    return e_c * inv, e_p * inv, e_s * inv


def _attn_blocks(n):
    cur = pl.multiple_of(n * ATTN_BLOCK, ATTN_BLOCK)
    prev = pl.multiple_of(jnp.maximum(n - 1, 0) * ATTN_BLOCK, ATTN_BLOCK)
    return cur, prev


def _kv_split(kv, g):
    k = kv[:, g * HEAD_DIM:(g + 1) * HEAD_DIM]
    v = kv[:, (N_KV_HEADS + g) * HEAD_DIM:(N_KV_HEADS + g + 1) * HEAD_DIM]
    return k, v


def _attn_fwd(q, kv, qw, kw, sink_rows):
    S, W = q.shape
    nb = S // ATTN_BLOCK

    def body(q_ref, kv_ref, qw_ref, kw_ref, sk_ref, o_ref, o_scr):
        n = pl.program_id(0)
        qf = q_ref[...].astype(F32)
        cur, prev = _attn_blocks(n)
        kvc = kv_ref[pl.ds(cur, ATTN_BLOCK), :].astype(F32)
        kvp = kv_ref[pl.ds(prev, ATTN_BLOCK), :].astype(F32)
        for g in range(N_KV_HEADS):
            qs = _stack_heads(qf, g)
            qn = (qs * _rinv(qs) * qw_ref[...] * SCALE).astype(BF16)
            kc, vc = _kv_split(kvc, g)
            kp, vp = _kv_split(kvp, g)
            kc = (kc * _rinv(kc) * kw_ref[...]).astype(BF16)
            kp = (kp * _rinv(kp) * kw_ref[...]).astype(BF16)
            p_c, p_p, _ = _attn_probs(qn, kc, kp, sk_ref[g], n)
            o = _dot(p_c.astype(BF16), vc.astype(BF16)) + _dot(p_p.astype(BF16), vp.astype(BF16))
            _unstack_heads(o_scr, o, g)
        o_ref[...] = o_scr[...].astype(BF16)

    blk = pl.BlockSpec((ATTN_BLOCK, W), lambda n: (n, 0))
    return pl.pallas_call(
        body, name="attn_fwd", grid=(nb,),
        in_specs=[blk, pl.BlockSpec(kv.shape, lambda n: (0, 0)), pl.BlockSpec((1, HEAD_DIM), lambda n: (0, 0)),
                  pl.BlockSpec((1, HEAD_DIM), lambda n: (0, 0)), pl.BlockSpec(sink_rows.shape, lambda n: (0, 0, 0))],
        out_specs=blk, out_shape=jax.ShapeDtypeStruct((S, W), BF16),
        scratch_shapes=[pltpu.VMEM((ATTN_BLOCK, W), F32)],
        compiler_params=_params(dimension_semantics=("arbitrary",)),
    )(q, kv, qw, kw, sink_rows)


def _attn_bwd(q, kv, do, qw, kw, sink_rows):
    S, W = q.shape
    KW = kv.shape[1]
    nb = S // ATTN_BLOCK
    chunk = 512

    def body(q_ref, kv_ref, do_ref, qw_ref, kw_ref, sk_ref, dq_ref, dkv_ref, dqw_ref, dkw_ref, dsk_ref, dq_scr):
        n = pl.program_id(0)

        @pl.when(n == 0)
        def _():
            dkv_ref[...] = jnp.zeros_like(dkv_ref)
            dqw_ref[...] = jnp.zeros_like(dqw_ref)
            dsk_ref[...] = jnp.zeros_like(dsk_ref)

        qf = q_ref[...].astype(F32)
        dof = do_ref[...].astype(F32)
        cur, prev = _attn_blocks(n)
        kvc = kv_ref[pl.ds(cur, ATTN_BLOCK), :].astype(F32)
        kvp = kv_ref[pl.ds(prev, ATTN_BLOCK), :].astype(F32)
        qw_v = qw_ref[...]
        for g in range(N_KV_HEADS):
            qs = _stack_heads(qf, g)
            qhat = qs * _rinv(qs)
            qn = (qhat * qw_v * SCALE).astype(BF16)
            kc, vc = _kv_split(kvc, g)
            kp, vp = _kv_split(kvp, g)
            kc = (kc * _rinv(kc) * kw_ref[...]).astype(BF16)
            kp = (kp * _rinv(kp) * kw_ref[...]).astype(BF16)
            vc = vc.astype(BF16)
            vp = vp.astype(BF16)
            dos = _stack_heads(dof, g).astype(BF16)
            p_c, p_p, p_s = _attn_probs(qn, kc, kp, sk_ref[g], n)
            dp_c = _dot_nt(dos, vc)
            dp_p = _dot_nt(dos, vp)
            drow = jnp.sum(p_c * dp_c, axis=-1, keepdims=True) + jnp.sum(p_p * dp_p, axis=-1, keepdims=True)
            ds_c = (p_c * (dp_c - drow)).astype(BF16)
            ds_p = (p_p * (dp_p - drow)).astype(BF16)
            dsink = -(p_s * drow)
            for hh in range(GQA_GROUP):
                tot = jnp.sum(dsink[hh * ATTN_BLOCK:(hh + 1) * ATTN_BLOCK, :], axis=0, keepdims=True)
                dsk_ref[g, hh:hh + 1, :] += jnp.broadcast_to(tot, (1, 128))
            dqn = (_dot(ds_c, kc) + _dot(ds_p, kp)) * SCALE
            klo, vlo = g * HEAD_DIM, (N_KV_HEADS + g) * HEAD_DIM
            dkv_ref[pl.ds(cur, ATTN_BLOCK), klo:klo + HEAD_DIM] += _dot_tn(ds_c, qn)
            dkv_ref[pl.ds(prev, ATTN_BLOCK), klo:klo + HEAD_DIM] += _dot_tn(ds_p, qn)
            dkv_ref[pl.ds(cur, ATTN_BLOCK), vlo:vlo + HEAD_DIM] += _dot_tn(p_c.astype(BF16), dos)
            dkv_ref[pl.ds(prev, ATTN_BLOCK), vlo:vlo + HEAD_DIM] += _dot_tn(p_p.astype(BF16), dos)
            dqw_ref[...] += jnp.sum(dqn * qhat, axis=0, keepdims=True)
            z = dqn * qw_v
            dqs = _rinv(qs) * (z - qhat * jnp.mean(z * qhat, axis=-1, keepdims=True))
            _unstack_heads(dq_scr, dqs, g)
        dq_ref[...] = dq_scr[...].astype(BF16)

        @pl.when(n == nb - 1)
        def _():
            def one(c, dkw):
                rows = pl.ds(pl.multiple_of(c * chunk, chunk), chunk)
                for g in range(N_KV_HEADS):
                    lo = g * HEAD_DIM
                    k = kv_ref[rows, lo:lo + HEAD_DIM].astype(F32)
                    dx, dg_rows = _rms_bwd(dkv_ref[rows, lo:lo + HEAD_DIM], k, kw_ref[...])
                    dkv_ref[rows, lo:lo + HEAD_DIM] = dx
                    dkw = dkw + jnp.sum(dg_rows, axis=0, keepdims=True)
                return dkw

            dkw_ref[...] = lax.fori_loop(0, S // chunk, one, jnp.zeros((1, HEAD_DIM), F32))

    blk = pl.BlockSpec((ATTN_BLOCK, W), lambda n: (n, 0))
    whole_kv = pl.BlockSpec((S, KW), lambda n: (0, 0))
    vec = pl.BlockSpec((1, HEAD_DIM), lambda n: (0, 0))
    sk = pl.BlockSpec(sink_rows.shape, lambda n: (0, 0, 0))
    dsk = pl.BlockSpec((N_KV_HEADS, GQA_GROUP, 128), lambda n: (0, 0, 0))
    return pl.pallas_call(
        body, name="attn_bwd", grid=(nb,), in_specs=[blk, whole_kv, blk, vec, vec, sk],
        out_specs=[blk, whole_kv, vec, vec, dsk],
        out_shape=[jax.ShapeDtypeStruct((S, W), BF16), jax.ShapeDtypeStruct((S, KW), F32),
                   jax.ShapeDtypeStruct((1, HEAD_DIM), F32), jax.ShapeDtypeStruct((1, HEAD_DIM), F32),
                   jax.ShapeDtypeStruct((N_KV_HEADS, GQA_GROUP, 128), F32)],
        scratch_shapes=[pltpu.VMEM((ATTN_BLOCK, W), F32)],
        compiler_params=_params(dimension_semantics=("arbitrary",)),
    )(q, kv, do, qw, kw, sink_rows)


def _pooled(xc, xprev, i):
    tm = xc.shape[0]
    xh = jnp.concatenate([jnp.where(i > 0, xprev, 0.0), xc], axis=0)
    t = lax.broadcasted_iota(jnp.int32, (tm, 1), 0) + i * tm
    out = []
    for gi, w in enumerate(POOL_WINDOWS):
        acc = xh[:, gi * POOL_GROUP:(gi + 1) * POOL_GROUP]
        sh = 1
        while sh < w:
            acc = acc + pltpu.roll(acc, sh, 0)
            sh *= 2
        cnt = jnp.minimum(t + 1, w).astype(F32)
        out.append(acc[POOL_HALO:, :] / cnt - xc[:, gi * POOL_GROUP:(gi + 1) * POOL_GROUP])
    return jnp.concatenate(out, axis=1)


def _pool_mix(pooled_b, pw_ref):
    return jnp.concatenate([_dot(pooled_b[:, gi * POOL_GROUP:(gi + 1) * POOL_GROUP], pw_ref[gi])
                            for gi in range(len(POOL_WINDOWS))], axis=1)


def _halo_specs(tm, width, S, after):
    per = tm // POOL_HALO
    last = S // POOL_HALO - 1
    if after:
        return pl.BlockSpec((POOL_HALO, width), lambda i: (jnp.minimum((i + 1) * per, last), 0))
    return pl.BlockSpec((POOL_HALO, width), lambda i: (jnp.maximum(i * per - 1, 0), 0))


def _mix_out(xp, attn, gl, bias, pw, pscale, wpot, wao, wo, h, tm=512):
    S, D = h.shape
    nt = S // tm
    PW = xp.shape[1]

    def body(xc_ref, xprev_ref, at_ref, gl_ref, bias_ref, pw_ref, ps_ref, wpo_ref, wao_ref, wo_ref, h_ref,
             ho_ref, bp_ref, ba_ref):
        i = pl.program_id(0)
        pooled = _pooled(xc_ref[...], xprev_ref[...], i).astype(BF16)
        ms = (_pool_mix(pooled, pw_ref) * ps_ref[...]).astype(BF16)
        bp = _dot_nt(ms, wpo_ref[...])
        ba = _dot(at_ref[...], wao_ref[...])
        bp_ref[...] = bp.astype(BF16)
        ba_ref[...] = ba.astype(BF16)
        gates = jax.nn.sigmoid(gl_ref[...].astype(F32) + bias_ref[...])
        merged = (gates[:, :D] * bp + gates[:, D:] * ba).astype(BF16)
        ho_ref[...] = h_ref[...] + _dot(merged, wo_ref[...])

    def row(width):
        return pl.BlockSpec((tm, width), lambda i: (i, 0))

    def whole(x):
        nd = x.ndim
        return pl.BlockSpec(x.shape, lambda i: (0,) * nd)

    return pl.pallas_call(
        body, name="mix_out", grid=(nt,),
        in_specs=[row(PW), _halo_specs(tm, PW, S, False), row(D), row(2 * D), whole(bias), whole(pw), whole(pscale),
                  whole(wpot), whole(wao), whole(wo), row(D)],
        out_specs=[row(D), row(D), row(D)],
        out_shape=[jax.ShapeDtypeStruct((S, D), F32), jax.ShapeDtypeStruct((S, D), BF16),
                   jax.ShapeDtypeStruct((S, D), BF16)],
        compiler_params=_params(dimension_semantics=("arbitrary",)),
    )(xp, xp, attn, gl, bias, pw, pscale, wpot, wao, wo, h)


def _mix_bwd_gate(dh, wo, bp, ba, gl, bias, tm=512):
    S, D = dh.shape
    nt = S // tm

    def body(dh_ref, wo_ref, bp_ref, ba_ref, gl_ref, bias_ref, dgl_ref, dbp_ref, dba_ref, dwo_ref, dbias_ref):
        i = pl.program_id(0)
        dhb = dh_ref[...].astype(BF16)
        dm = _dot_nt(dhb, wo_ref[...])
        gates = jax.nn.sigmoid(gl_ref[...].astype(F32) + bias_ref[...])
        gp, ga = gates[:, :D], gates[:, D:]
        bp_v = bp_ref[...].astype(F32)
        ba_v = ba_ref[...].astype(F32)
        merged = (gp * bp_v + ga * ba_v).astype(BF16)
        _acc(dwo_ref, _dot_tn(merged, dhb), i == 0)
        dbp_ref[...] = (dm * gp).astype(BF16)
        dba_ref[...] = (dm * ga).astype(BF16)
        dgl = jnp.concatenate([dm * bp_v * gp * (1.0 - gp), dm * ba_v * ga * (1.0 - ga)], axis=1)
        dgl_ref[...] = dgl.astype(BF16)
        _acc(dbias_ref, jnp.sum(dgl, axis=0, keepdims=True), i == 0)

    def row(width):
        return pl.BlockSpec((tm, width), lambda i: (i, 0))

    def whole(shape):
        return pl.BlockSpec(shape, lambda i: (0, 0))

    return pl.pallas_call(
        body, name="mix_bwd_gate", grid=(nt,),
        in_specs=[row(D), whole(wo.shape), row(D), row(D), row(2 * D), whole(bias.shape)],
        out_specs=[row(2 * D), row(D), row(D), whole((D, D)), whole((1, 2 * D))],
        out_shape=[jax.ShapeDtypeStruct((S, 2 * D), BF16), jax.ShapeDtypeStruct((S, D), BF16),
                   jax.ShapeDtypeStruct((S, D), BF16), jax.ShapeDtypeStruct((D, D), F32),
                   jax.ShapeDtypeStruct((1, 2 * D), F32)],
        compiler_params=_params(dimension_semantics=("arbitrary",)),
    )(dh, wo, bp, ba, gl, bias)


def _mix_bwd_branch(dbp, dba, attn, xp, pw, pscale, wpot, wao, tm=512):
    S, D = dbp.shape
    nt = S // tm
    PW = xp.shape[1]
    NG = len(POOL_WINDOWS)

    def body(dbp_ref, dba_ref, at_ref, xc_ref, xprev_ref, pw_ref, ps_ref, wpo_ref, wao_ref,
             dat_ref, dpl_ref, dwao_ref, dwpo_ref, dpw_ref, dps_ref):
        i = pl.program_id(0)
        dba_v = dba_ref[...]
        dbp_v = dbp_ref[...]
        _acc(dwao_ref, _dot_tn(at_ref[...], dba_v), i == 0)
        dat_ref[...] = _dot_nt(dba_v, wao_ref[...]).astype(BF16)
        pooled = _pooled(xc_ref[...], xprev_ref[...], i).astype(BF16)
        mixed = _pool_mix(pooled, pw_ref)
        ps = ps_ref[...]
        _acc(dwpo_ref, _dot_tn(dbp_v, (mixed * ps).astype(BF16)), i == 0)
        dms = _dot(dbp_v, wpo_ref[...])
        _acc(dps_ref, jnp.sum(dms * mixed, axis=0, keepdims=True), i == 0)
        dmixed = (dms * ps).astype(BF16)
        dpooled = []
        for gi in range(NG):
            cols = slice(gi * POOL_GROUP, (gi + 1) * POOL_GROUP)
            _acc(dpw_ref.at[gi], _dot_tn(pooled[:, cols], dmixed[:, cols]), i == 0)
            dpooled.append(_dot_nt(dmixed[:, cols], pw_ref[gi]))
        dpl_ref[...] = jnp.concatenate(dpooled, axis=1)

    def row(width):
        return pl.BlockSpec((tm, width), lambda i: (i, 0))

    def whole(shape):
        nd = len(shape)
        return pl.BlockSpec(shape, lambda i: (0,) * nd)

    return pl.pallas_call(
        body, name="mix_bwd_branch", grid=(nt,),
        in_specs=[row(D), row(D), row(D), row(PW), _halo_specs(tm, PW, S, False), whole(pw.shape), whole(pscale.shape),
                  whole(wpot.shape), whole(wao.shape)],
        out_specs=[row(D), row(PW), whole((D, D)), whole((D, PW)), whole(pw.shape), whole((1, PW))],
        out_shape=[jax.ShapeDtypeStruct((S, D), BF16), jax.ShapeDtypeStruct((S, PW), F32),
                   jax.ShapeDtypeStruct((D, D), F32), jax.ShapeDtypeStruct((D, PW), F32),
                   jax.ShapeDtypeStruct(pw.shape, F32), jax.ShapeDtypeStruct((1, PW), F32)],
        compiler_params=_params(dimension_semantics=("arbitrary",)),
    )(dbp, dba, attn, xp, xp, pw, pscale, wpot, wao)


def _pool_bwd(dpooled, tm=512):
    S, PW = dpooled.shape
    nt = S // tm

    def body(dc_ref, dnext_ref, dxp_ref):
        i = pl.program_id(0)
        dc = dc_ref[...]
        dh = jnp.concatenate([dc, jnp.where(i < nt - 1, dnext_ref[...], 0.0)], axis=0)
        rows = tm + POOL_HALO
        t = lax.broadcasted_iota(jnp.int32, (rows, 1), 0) + i * tm
        out = []
        for gi, w in enumerate(POOL_WINDOWS):
            cols = slice(gi * POOL_GROUP, (gi + 1) * POOL_GROUP)
            acc = dh[:, cols] / jnp.minimum(t + 1, w).astype(F32)
            sh = 1
            while sh < w:
                acc = acc + pltpu.roll(acc, rows - sh, 0)
                sh *= 2
            out.append(acc[:tm, :] - dc[:, cols])
        dxp_ref[...] = jnp.concatenate(out, axis=1).astype(BF16)

    return pl.pallas_call(
        body, name="pool_bwd", grid=(nt,),
        in_specs=[pl.BlockSpec((tm, PW), lambda i: (i, 0)), _halo_specs(tm, PW, S, True)],
        out_specs=pl.BlockSpec((tm, PW), lambda i: (i, 0)), out_shape=jax.ShapeDtypeStruct((S, PW), BF16),
        compiler_params=_params(dimension_semantics=("arbitrary",)),
    )(dpooled, dpooled)


def _in_bwd_w(pieces, u, tm=1024, tf=256):
    S, D = u.shape
    nt = S // tm
    starts = [off // tf for off, _ in SEGMENTS]
    counts = [width // tf for _, width in SEGMENTS]
    nf = sum(counts)

    def body(*refs):
        piece_refs, u_ref, dw_ref = refs[:len(SEGMENTS)], refs[len(SEGMENTS)], refs[len(SEGMENTS) + 1]
        j = pl.program_id(0)
        i = pl.program_id(1)
        for ref, st, cnt in zip(piece_refs, starts, counts):
            @pl.when(jnp.logical_and(j >= st, j < st + cnt))
            def _(ref=ref):
                _acc(dw_ref, _dot_tn(ref[...].astype(BF16), u_ref[...]), i == 0)

    def piece_spec(st, cnt):
        return pl.BlockSpec((tm, tf), lambda j, i: (i, jnp.clip(j - st, 0, cnt - 1)))

    return pl.pallas_call(
        body, name="in_bwd_w", grid=(nf, nt),
        in_specs=[piece_spec(st, cnt) for st, cnt in zip(starts, counts)] + [pl.BlockSpec((tm, D), lambda j, i: (i, 0))],
        out_specs=pl.BlockSpec((tf, D), lambda j, i: (j, 0)),
        out_shape=jax.ShapeDtypeStruct((nf * tf, D), F32),
        compiler_params=_params(dimension_semantics=("arbitrary", "arbitrary")),
    )(*pieces, u)


def _in_bwd_x(pieces, wint, h, dh, gnorm, tm=512):
    S, D = h.shape
    nt = S // tm

    def body(*refs):
        piece_refs = refs[:len(SEGMENTS)]
        w_ref, h_ref, dh_ref, g_ref, dx_ref, dxh_ref, dg_ref = refs[len(SEGMENTS):]
        i = pl.program_id(0)
        du = jnp.zeros((tm, D), F32)
        for (off, width), ref in zip(SEGMENTS, piece_refs):
            du = du + _dot(ref[...].astype(BF16), w_ref[off:off + width, :])
        dx, dg_rows = _rms_bwd(du, h_ref[...], g_ref[...])
        out = dh_ref[...] + dx
        dx_ref[...] = out
        dxh_ref[...] = (0.5 * out).astype(BF16)
        _acc(dg_ref, jnp.sum(dg_rows, axis=0, keepdims=True), i == 0)

    def row(width):
        return pl.BlockSpec((tm, width), lambda i: (i, 0))

    vec = pl.BlockSpec((1, D), lambda i: (0, 0))
    return pl.pallas_call(
        body, name="in_bwd_x", grid=(nt,),
        in_specs=[row(width) for _, width in SEGMENTS] + [pl.BlockSpec(wint.shape, lambda i: (0, 0)), row(D), row(D), vec],
        out_specs=[row(D), row(D), vec],
        out_shape=[jax.ShapeDtypeStruct((S, D), F32), jax.ShapeDtypeStruct((S, D), BF16),
                   jax.ShapeDtypeStruct((1, D), F32)],
        compiler_params=_params(dimension_semantics=("arbitrary",)),
    )(*pieces, wint, h, dh, gnorm)


def _row_tile(rows):
    for t in (512, 480, 352, 256, 128, 64, 32, 16, 8):
        if rows % t == 0:
            return t
    return rows


def _adamw(w, g, m, v, token, name):
    R, C = w.shape
    tr = _row_tile(R)

    def body(w_ref, g_ref, m_ref, v_ref, _, d_ref, mo_ref, vo_ref):
        gv = g_ref[...]
        mn = ADAM_B1 * m_ref[...] + (1.0 - ADAM_B1) * gv
        vn = ADAM_B2 * v_ref[...] + (1.0 - ADAM_B2) * (gv * gv)
        m_hat = mn / (1.0 - ADAM_B1 ** ADAM_STEP)
        v_hat = vn / (1.0 - ADAM_B2 ** ADAM_STEP)
        d_ref[...] = -ADAM_LR * (m_hat / (jnp.sqrt(v_hat) + ADAM_EPS) + ADAM_WD * w_ref[...])
        mo_ref[...] = mn
        vo_ref[...] = vn

    blk = pl.BlockSpec((tr, C), lambda i: (i, 0))
    sh = jax.ShapeDtypeStruct((R, C), F32)
    return pl.pallas_call(
        body, name=name, grid=(R // tr,), in_specs=[blk] * 4 + [TOKEN_SPEC], out_specs=[blk] * 3, out_shape=[sh] * 3,
        compiler_params=_params(dimension_semantics=("arbitrary",)),
    )(w, g, m, v, token)


def _cast_place(w, place, name):
    R, C = w.shape
    tr = _row_tile(R)
    per = R // tr

    def body(p_ref, w_ref, o_ref):
        o_ref[...] = w_ref[...].astype(BF16)

    grid_spec = pltpu.PrefetchScalarGridSpec(
        num_scalar_prefetch=1, grid=(per,),
        in_specs=[pl.BlockSpec((tr, C), lambda i, p: (i, 0))],
        out_specs=pl.BlockSpec((tr, C), lambda i, p: (p[0] * per + i, 0)))
    return pl.pallas_call(
        body, name=name, grid_spec=grid_spec, out_shape=jax.ShapeDtypeStruct((N_SHARDS * R, C), BF16),
        compiler_params=_params(dimension_semantics=("arbitrary",)),
    )(place, w)


def _sum_halves(g4, recv, place, name):
    NS, R, C = g4.shape
    hr = R // 2
    tr = _row_tile(hr)
    per = hr // tr

    def body(p_ref, g_ref, r_ref, o_ref):
        o_ref[...] = (g_ref[...] + r_ref[...]).astype(BF16)

    grid_spec = pltpu.PrefetchScalarGridSpec(
        num_scalar_prefetch=1, grid=(NS, per),
        in_specs=[pl.BlockSpec((1, tr, C), lambda s, i, p: (s, p[1] * per + i, 0)),
                  pl.BlockSpec((1, tr, C), lambda s, i, p: (s, i, 0))],
        out_specs=pl.BlockSpec((1, tr, C), lambda s, i, p: (s, i, 0)))
    return pl.pallas_call(
        body, name=name, grid_spec=grid_spec, out_shape=jax.ShapeDtypeStruct((NS, hr, C), BF16),
        compiler_params=_params(dimension_semantics=("arbitrary", "arbitrary")),
    )(place, g4, recv)


def _sum_quarters(h4, recv3, place, name):
    NS, hr, C = h4.shape
    tr = _row_tile(hr)
    per = hr // tr

    def body(p_ref, h_ref, r_ref, o_ref):
        acc = h_ref[0].astype(F32)
        for k in range(N_SHARDS - 1):
            acc = acc + r_ref[k].astype(F32)
        o_ref[...] = acc

    grid_spec = pltpu.PrefetchScalarGridSpec(
        num_scalar_prefetch=1, grid=(per,),
        in_specs=[pl.BlockSpec((1, tr, C), lambda i, p: (p[0], i, 0)),
                  pl.BlockSpec((N_SHARDS - 1, tr, C), lambda i, p: (0, i, 0))],
        out_specs=pl.BlockSpec((tr, C), lambda i, p: (p[1] * per + i, 0)))
    return pl.pallas_call(
        body, name=name, grid_spec=grid_spec, out_shape=jax.ShapeDtypeStruct((2 * hr, C), F32),
        compiler_params=_params(dimension_semantics=("arbitrary",)),
    )(place, h4, recv3)


def _place():
    x, y, c = lax.axis_index("x"), lax.axis_index("y"), lax.axis_index("c")
    chips = [(1 - x, y), (x, 1 - y), (1 - x, 1 - y)]
    return x, y, c, chips


def _any_specs(n):
    return [pl.BlockSpec(memory_space=pl.ANY)] * n


def _gather_weights(wholes):
    nw = len(wholes)

    def body(*refs):
        bufs = refs[nw:2 * nw]
        ici_send, ici_recv, d2d_send, d2d_recv = refs[2 * nw:]
        x, y, c, chips = _place()
        me = 2 * x + y
        sibling = (x, y, 1 - c)

        def half(w, chip, core):
            hr = bufs[w].shape[0] // (2 * N_SHARDS)
            return bufs[w].at[pl.ds(pl.multiple_of((2 * chip + core) * hr, 16), hr)]

        def copy(w, k, rows, to, sems):
            return pltpu.make_async_remote_copy(src_ref=rows, dst_ref=rows, send_sem=sems[0].at[w, k],
                                                recv_sem=sems[1].at[w, k], device_id=to, device_id_type=MESH)

        ici, d2d = (ici_send, ici_recv), (d2d_send, d2d_recv)
        sends = []
        for w in range(nw):
            for k, (px, py) in enumerate(chips):
                sends.append(copy(w, k, half(w, me, c), (px, py, c), ici))
                sends[-1].start()
        for w in range(nw):
            for k, (px, py) in enumerate(chips):
                landed = half(w, 2 * px + py, c)
                copy(w, k, landed, (px, py, c), ici).wait_recv()
                sends.append(copy(w, k, landed, sibling, d2d))
                sends[-1].start()
        for w in range(nw):
            for k, (px, py) in enumerate(chips):
                copy(w, k, half(w, 2 * px + py, 1 - c), sibling, d2d).wait_recv()
        for cp in sends:
            cp.wait_send()

    sem = pltpu.SemaphoreType.DMA((nw, N_SHARDS - 1))
    return pl.pallas_call(
        body, name="gather_weights", in_specs=_any_specs(nw), out_specs=_any_specs(nw),
        out_shape=[jax.ShapeDtypeStruct(s.shape, s.dtype) for s in wholes],
        input_output_aliases={i: i for i in range(nw)}, scratch_shapes=[sem, sem, sem, sem],
        compiler_params=pltpu.CompilerParams(has_side_effects=True),
    )(*wholes)


HBM_SPEC = pl.BlockSpec(memory_space=pltpu.HBM)
SEM_SPEC = pl.BlockSpec(memory_space=pltpu.SEMAPHORE)
DATAFLOW = pltpu.SideEffectType.DATAFLOW_SIDE_EFFECTING


def _hbm(a):
    return pltpu.with_memory_space_constraint(a, pltpu.HBM)


class InFlight(NamedTuple):
    send_sem: jax.Array
    recv_sem: jax.Array
    bufs: list
    plan: Callable
    token: jax.Array


def _wait_all(plan, refs, send_ref, recv_ref):
    for k, (src, dst, dev) in enumerate(plan(refs)):
        cp = pltpu.make_async_remote_copy(src_ref=src, dst_ref=dst, send_sem=send_ref.at[k], recv_sem=recv_ref.at[k],
                                          device_id=dev, device_id_type=MESH)
        cp.wait_send()
        cp.wait_recv()


def _start_all(plan, refs, send_ref, recv_ref):
    for k, (src, dst, dev) in enumerate(plan(refs)):
        pltpu.make_async_remote_copy(src_ref=src, dst_ref=dst, send_sem=send_ref.at[k], recv_sem=recv_ref.at[k],
                                     device_id=dev, device_id_type=MESH).start()


def _split_start(name, bufs, plan, n_copies, after, waits=None):
    nb = len(bufs)

    def body(*refs):
        ins = refs[:nb]
        if waits is not None:
            _wait_all(waits.plan, ins, refs[nb], refs[nb + 1])
        send_sem, recv_sem, token = refs[-nb - 3], refs[-nb - 2], refs[-1]
        _start_all(plan, ins, send_sem, recv_sem)
        token[...] = jnp.zeros_like(token)

    sems = pltpu.SemaphoreType.DMA((n_copies,))
    earlier = [] if waits is None else [waits.send_sem, waits.recv_sem]
    out = pl.pallas_call(
        body, name=name, in_specs=[HBM_SPEC] * nb + [SEM_SPEC] * len(earlier) + [pl.BlockSpec(memory_space=pl.ANY)],
        out_shape=(sems, sems, *[pltpu.HBM(b.shape, b.dtype) for b in bufs], jax.ShapeDtypeStruct((8, 128), F32)),
        out_specs=(SEM_SPEC, SEM_SPEC, *[HBM_SPEC] * nb, pl.BlockSpec(memory_space=pltpu.VMEM)),
        input_output_aliases={i: i + 2 for i in range(nb)},
        compiler_params=pltpu.CompilerParams(has_side_effects=DATAFLOW),
    )(*[_hbm(b) for b in bufs], *earlier, after)
    return InFlight(out[0], out[1], list(out[2:2 + nb]), plan, out[-1])


def _split_wait(name, flight, after):
    nb = len(flight.bufs)

    def body(*refs):
        _wait_all(flight.plan, refs[:nb], refs[nb], refs[nb + 1])

    out = pl.pallas_call(
        body, name=name, in_specs=[HBM_SPEC] * nb + [SEM_SPEC, SEM_SPEC, pl.BlockSpec(memory_space=pl.ANY)],
        out_shape=[pltpu.HBM(b.shape, b.dtype) for b in flight.bufs], out_specs=[HBM_SPEC] * nb,
        input_output_aliases={i: i for i in range(nb)},
        compiler_params=pltpu.CompilerParams(has_side_effects=DATAFLOW),
    )(*flight.bufs, flight.send_sem, flight.recv_sem, after)
    return list(out)


def _half_rows(buf, chip, core):
    hr = buf.shape[0] // (2 * N_SHARDS)
    return buf.at[pl.ds(pl.multiple_of((2 * chip + core) * hr, 16), hr)]


def _gather_ici_plan(bufs):
    x, y, c, chips = _place()
    return [(_half_rows(b, 2 * x + y, c), _half_rows(b, 2 * x + y, c), (px, py, c)) for b in bufs for px, py in chips]


def _gather_d2d_plan(bufs):
    x, y, c, chips = _place()
    return [(_half_rows(b, 2 * px + py, c), _half_rows(b, 2 * px + py, c), (x, y, 1 - c)) for b in bufs for px, py in chips]


def _swap_plan(bufs):
    x, y, c, _ = _place()
    n = len(bufs) // 2
    copies = []
    for g, land in zip(bufs[:n], bufs[n:]):
        hr = g.shape[1] // 2
        copies.append((g.at[:, pl.ds(pl.multiple_of((1 - c) * hr, 8), hr)], land, (x, y, 1 - c)))
    return copies


def _exchange_plan(bufs):
    x, y, c, chips = _place()
    n = len(bufs) // 2
    return [(h.at[2 * px + py], land.at[k], (px, py, c))
            for h, land in zip(bufs[:n], bufs[n:]) for k, (px, py) in enumerate(chips)]


def _share_plan(bufs):
    x, y, c, _ = _place()
    copies = []
    for buf in bufs:
        hr = buf.shape[0] // 2
        mine = buf.at[pl.ds(pl.multiple_of(c * hr, 8), hr)]
        copies.append((mine, mine, (x, y, 1 - c)))
    return copies


def _sum_small(vec):
    R, C = vec.shape
    ndev = 8

    def body(v_ref, o_ref, buf, send_sem, recv_sem):
        x, y, c, _ = _place()
        me = 4 * x + 2 * y + c
        buf[me] = v_ref[...]
        cps = []
        for r in range(1, ndev):
            peer = (x ^ (r >> 2), y ^ ((r >> 1) & 1), c ^ (r & 1))
            cp = pltpu.make_async_remote_copy(
                src_ref=v_ref, dst_ref=buf.at[me], send_sem=send_sem.at[r - 1], recv_sem=recv_sem.at[r - 1],
                device_id=peer, device_id_type=MESH)
            cp.start()
            cps.append(cp)
        for r in range(1, ndev):
            src = (4 * (x ^ (r >> 2)) + 2 * (y ^ ((r >> 1) & 1)) + (c ^ (r & 1)))
            pltpu.make_async_remote_copy(
                src_ref=v_ref, dst_ref=buf.at[src], send_sem=send_sem.at[r - 1], recv_sem=recv_sem.at[r - 1],
                device_id=(x, y, c), device_id_type=MESH).wait_recv()
        acc = buf[0]
        for d in range(1, ndev):
            acc = acc + buf[d]
        o_ref[...] = acc
        for cp in cps:
            cp.wait_send()

    return pl.pallas_call(
        body, name="sum_small", in_specs=[pl.BlockSpec(memory_space=pltpu.VMEM)],
        out_specs=pl.BlockSpec(memory_space=pltpu.VMEM), out_shape=jax.ShapeDtypeStruct((R, C), F32),
        scratch_shapes=[pltpu.VMEM((ndev, R, C), F32), pltpu.SemaphoreType.DMA((ndev - 1,)),
                        pltpu.SemaphoreType.DMA((ndev - 1,))],
        compiler_params=pltpu.CompilerParams(has_side_effects=True),
    )(vec)


SMALL = ("ffn1_norm", "mix_norm", "pool_w", "pool_scale", "q_norm", "k_norm", "sinks", "gate_bias", "ffn2_norm")
SMALL_COLS = 1024
FFN1 = ("ffn1_w_gate", "ffn1_w_up", "ffn1_w_down")
MIXER = ("w_in", "w_pool_out", "w_attn_out", "w_out")
FFN2 = ("ffn2_w_gate", "ffn2_w_up", "ffn2_w_down")
LARGE = FFN1 + MIXER + FFN2
TRANSPOSED = ("ffn1_w_gate", "ffn1_w_up", "w_in", "w_pool_out", "ffn2_w_gate", "ffn2_w_up")
WEIGHTS = ("ffn1_norm", "ffn1_w_gate", "ffn1_w_up", "ffn1_w_down", "mix_norm", "w_in", "pool_w", "pool_scale",
           "w_pool_out", "q_norm", "k_norm", "sinks", "w_attn_out", "gate_bias", "w_out", "ffn2_norm",
           "ffn2_w_gate", "ffn2_w_up", "ffn2_w_down")


def _pack_small(parts):
    flat = jnp.concatenate([p.reshape(-1) for p in parts])
    rows = -(-flat.shape[0] // (8 * SMALL_COLS)) * 8
    return jnp.pad(flat, (0, rows * SMALL_COLS - flat.shape[0])).reshape(rows, SMALL_COLS)


def _unpack_small(packed, like):
    flat = packed.reshape(-1)
    out, off = [], 0
    for p in like:
        out.append(flat[off:off + p.size].reshape(p.shape))
        off += p.size
    return out, flat[off]


def _tie(small, token):
    return small + token[0, 0]


class Reduction:
    def __init__(self, group, names, grads, place):
        self.group, self.names, self.place = group, names, place
        self.bufs = [g.reshape(N_SHARDS, -1, g.shape[-1]) for g in grads]
        self.flight = None

    def _start(self, stage, bufs, plan, n_copies, after):
        self.flight = _split_start(f"{self.group}_{stage}", bufs, plan, n_copies, after)
        return self.flight.token

    def _landed(self, stage, after):
        n = len(self.names)
        bufs = _split_wait(f"{self.group}_{stage}_wait", self.flight, after)
        return bufs[:n], bufs[n:]

    def start_swap(self, after):
        lands = [lax.empty((g.shape[0], g.shape[1] // 2, g.shape[2]), g.dtype) for g in self.bufs]
        return self._start("swap", self.bufs + lands, _swap_plan, len(self.bufs), after)

    def start_exchange(self, after):
        g4, recv = self._landed("swap", after)
        halves = [_sum_halves(g, r, self.place, f"sum_halves_{n}") for n, g, r in zip(self.names, g4, recv)]
        lands = [lax.empty((N_SHARDS - 1,) + h.shape[1:], h.dtype) for h in halves]
        return self._start("exchange", halves + lands, _exchange_plan, (N_SHARDS - 1) * len(halves), halves[0])

    def start_share(self, after):
        halves, recv3 = self._landed("exchange", after)
        reduced = [_sum_quarters(h, r, self.place, f"sum_quarters_{n}") for n, h, r in zip(self.names, halves, recv3)]
        return self._start("share", reduced, _share_plan, len(reduced), reduced[0])

    def finish(self, after):
        return dict(zip(self.names, _split_wait(f"{self.group}_share_wait", self.flight, after)))


def _row_form(name, a):
    return a.T if name in TRANSPOSED else a


def kernel(x, ffn1_norm, ffn1_w_gate, ffn1_w_up, ffn1_w_down, mix_norm, w_in, pool_w, pool_scale, w_pool_out, q_norm, k_norm, sinks, w_attn_out, gate_bias, w_out, ffn2_norm, ffn2_w_gate, ffn2_w_up, ffn2_w_down, loss_target, m_ffn1_norm, m_ffn1_w_gate, m_ffn1_w_up, m_ffn1_w_down, m_mix_norm, m_w_in, m_pool_w, m_pool_scale, m_w_pool_out, m_q_norm, m_k_norm, m_sinks, m_w_attn_out, m_gate_bias, m_w_out, m_ffn2_norm, m_ffn2_w_gate, m_ffn2_w_up, m_ffn2_w_down, v_ffn1_norm, v_ffn1_w_gate, v_ffn1_w_up, v_ffn1_w_down, v_mix_norm, v_w_in, v_pool_w, v_pool_scale, v_w_pool_out, v_q_norm, v_k_norm, v_sinks, v_w_attn_out, v_gate_bias, v_w_out, v_ffn2_norm, v_ffn2_w_gate, v_ffn2_w_up, v_ffn2_w_down):
    args = dict(locals())
    wts = {n: _row_form(n, args[n]) for n in WEIGHTS}
    mom = {n: _row_form(n, args["m_" + n]) for n in WEIGHTS}
    var = {n: _row_form(n, args["v_" + n]) for n in WEIGHTS}
    shard = 2 * lax.axis_index("x") + lax.axis_index("y")
    place = jnp.stack([shard, lax.axis_index("c")]).astype(jnp.int32)

    xs, target = x[0], loss_target[0]
    D = xs.shape[1]
    g1 = wts["ffn1_norm"].reshape(1, D)
    gm = wts["mix_norm"].reshape(1, D)
    g2 = wts["ffn2_norm"].reshape(1, D)
    qw = wts["q_norm"].reshape(1, HEAD_DIM)
    kw = wts["k_norm"].reshape(1, HEAD_DIM)
    bias = wts["gate_bias"].reshape(1, 2 * D)
    pscale = wts["pool_scale"].reshape(1, -1)
    pw = wts["pool_w"].astype(BF16)
    sink_rows = jnp.repeat(wts["sinks"], ATTN_BLOCK).reshape(N_KV_HEADS, GQA_GROUP * ATTN_BLOCK, 1)

    placed = {n: _cast_place(wts[n], place, "cast_" + n) for n in LARGE}
    w = dict(zip(FFN1, _gather_weights([placed[n] for n in FFN1])))
    mix_ici = _split_start("gather_mix_ici", [placed[n] for n in MIXER], _gather_ici_plan, 3 * len(MIXER), w[FFN1[0]])
    ffn2_ici = _split_start("gather_ffn2_ici", [placed[n] for n in FFN2], _gather_ici_plan, 3 * len(FFN2), mix_ici.token)

    n1, a1, b1, h1 = _ffn_fwd(xs, _tie(g1, ffn2_ici.token), w["ffn1_w_gate"], w["ffn1_w_up"], w["ffn1_w_down"], None, "ffn1_fwd")
    mix_d2d = _split_start("gather_mix_d2d", mix_ici.bufs, _gather_d2d_plan, 3 * len(MIXER), h1, waits=mix_ici)
    w.update(zip(MIXER, _split_wait("gather_mix_wait", mix_d2d, mix_d2d.token)))
    u, xp, q, kv, gl = _mix_proj(h1, gm, w["w_in"])
    attn = _attn_fwd(q, kv, qw, kw, sink_rows)
    ffn2_d2d = _split_start("gather_ffn2_d2d", ffn2_ici.bufs, _gather_d2d_plan, 3 * len(FFN2), attn, waits=ffn2_ici)
    h2, bp, ba = _mix_out(xp, attn, gl, _tie(bias, ffn2_d2d.token), pw, pscale, w["w_pool_out"], w["w_attn_out"], w["w_out"], h1)
    w.update(zip(FFN2, _split_wait("gather_ffn2_wait", ffn2_d2d, h2)))
    n2, a2, b2, dy, dyh, loss = _ffn_fwd(h2, g2, w["ffn2_w_gate"], w["ffn2_w_up"], w["ffn2_w_down"], target, "ffn2_fwd")

    gw, gs = {}, {}
    da, db, gw["ffn2_w_gate"], gw["ffn2_w_up"], gw["ffn2_w_down"] = _ffn_bwd_w(dyh, n2, a2, b2, w["ffn2_w_down"], ffn2_d2d.token, "ffn2_bwd_w")
    red2 = Reduction("reduce_ffn2", FFN2, [gw[n] for n in FFN2], place)
    token = red2.start_swap(da)
    dh2, _, gs["ffn2_norm"] = _ffn_bwd_x(da, db, w["ffn2_w_gate"], w["ffn2_w_up"], h2, dy, _tie(g2, token), "ffn2_bwd_x")
    token = red2.start_exchange(dh2)
    dgl, dbp, dba, gw["w_out"], gs["gate_bias"] = _mix_bwd_gate(dh2, w["w_out"], bp, ba, gl, _tie(bias, token))
    dattn, dpooled, gw["w_attn_out"], gw["w_pool_out"], gs["pool_w"], gs["pool_scale"] = _mix_bwd_branch(
        dbp, dba, attn, xp, pw, pscale, w["w_pool_out"], w["w_attn_out"])
    dq, dkv, gs["q_norm"], gs["k_norm"], dsk = _attn_bwd(q, kv, dattn, qw, kw, sink_rows)
    gs["sinks"] = dsk[:, :, 0]
    red2.start_share(dq)
    pieces = (_pool_bwd(dpooled), dq, dkv, dgl)
    gw["w_in"] = _in_bwd_w(pieces, u)
    grads = red2.finish(gw["w_in"])
    redm = Reduction("reduce_mix", MIXER, [gw[n] for n in MIXER], place)
    token = redm.start_swap(grads[FFN2[0]])
    dh1, dh1h, gs["mix_norm"] = _in_bwd_x(pieces, w["w_in"], h1, dh2, _tie(gm, token))
    token = redm.start_exchange(dh1)
    da, db, gw["ffn1_w_gate"], gw["ffn1_w_up"], gw["ffn1_w_down"] = _ffn_bwd_w(dh1h, n1, a1, b1, w["ffn1_w_down"], token, "ffn1_bwd_w")
    token = redm.start_share(da)
    red1 = Reduction("reduce_ffn1", FFN1, [gw[n] for n in FFN1], place)
    token = red1.start_swap(token)
    grad_x, _, gs["ffn1_norm"] = _ffn_bwd_x(da, db, w["ffn1_w_gate"], w["ffn1_w_up"], xs, dh1, _tie(g1, token), "ffn1_bwd_x")
    grads.update(redm.finish(grad_x))
    token = red1.start_exchange(grads[MIXER[0]])

    delta, new_m, new_v = {}, {}, {}
    for n in FFN2 + MIXER:
        delta[n], new_m[n], new_v[n] = _adamw(wts[n], grads[n], mom[n], var[n], token, "adamw_" + n)
    small_parts = [gs[n] for n in SMALL] + [loss[0, 0].reshape(1)]
    summed = _sum_small(_tie(_pack_small(small_parts), token))
    small_grads, loss_sum = _unpack_small(summed, [wts[n] for n in SMALL])
    grads.update(dict(zip(SMALL, small_grads)))
    zero = jnp.zeros((1,), F32)
    packed = [_pack_small([d[n] for n in SMALL] + [zero]) for d in (wts, mom, var)]
    ds, ms, vs = _adamw(packed[0], summed, packed[1], packed[2], token, "adamw_small")
    like = [wts[n] for n in SMALL]
    for out, packed_out in ((delta, ds), (new_m, ms), (new_v, vs)):
        out.update(dict(zip(SMALL, _unpack_small(packed_out, like)[0])))
    token = red1.start_share(ds)
    grads.update(red1.finish(token))
    for n in FFN1:
        delta[n], new_m[n], new_v[n] = _adamw(wts[n], grads[n], mom[n], var[n], token, "adamw_" + n)

    outs = []
    for d in (grads, delta, new_m, new_v):
        outs += [_row_form(n, d[n]) for n in WEIGHTS]
    return (loss_sum, grad_x[None], *outs)
```

```python
from typing import Callable, NamedTuple

import jax
import jax.numpy as jnp
from jax import lax
from jax.experimental import pallas as pl
from jax.experimental.pallas import tpu as pltpu

F32 = jnp.float32
BF16 = jnp.bfloat16
RMS_EPS = 1e-6
POOL_WINDOWS = (2, 4, 8, 16)
POOL_GROUP = 128
POOL_HALO = 16
HEAD_DIM = 64
GQA_GROUP = 8
N_KV_HEADS = 2
ATTN_BLOCK = 128
SCALE = HEAD_DIM ** -0.5
NEG = -1e30
N_SHARDS = 4
ADAM_LR, ADAM_B1, ADAM_B2, ADAM_EPS, ADAM_WD, ADAM_STEP = 0.001, 0.9, 0.999, 1e-08, 0.01, 10
VMEM_LIMIT = 56 * 1024 * 1024
MESH = pl.DeviceIdType.MESH
SEG_POOL, SEG_Q, SEG_KV, SEG_GATE = (0, 512), (512, 1024), (1536, 256), (1792, 2048)
SEGMENTS = (SEG_POOL, SEG_Q, SEG_KV, SEG_GATE)


def _params(**kw):
    return pltpu.CompilerParams(vmem_limit_bytes=VMEM_LIMIT, **kw)


def _dot(a, b):
    return jnp.dot(a, b, preferred_element_type=F32)


def _dot_nt(a, b):
    return lax.dot_general(a, b, (((1,), (1,)), ((), ())), preferred_element_type=F32)


def _dot_tn(a, b):
    return lax.dot_general(a, b, (((0,), (0,)), ((), ())), preferred_element_type=F32)


def _rinv(x):
    return lax.rsqrt(jnp.mean(x * x, axis=-1, keepdims=True) + RMS_EPS)


def _rms_bwd(dn, x, g):
    r = _rinv(x)
    xr = x * r
    z = dn * g
    dx = r * (z - xr * jnp.mean(z * xr, axis=-1, keepdims=True))
    return dx, dn * xr


def _acc(ref, val, first):
    @pl.when(first)
    def _():
        ref[...] = val

    @pl.when(jnp.logical_not(first))
    def _():
        ref[...] += val


def _ffn_fwd(h, gnorm, wgt, wut, wd, target, name, tm=512, tf=1408):
    S, D = h.shape
    F = wd.shape[0]
    with_loss = target is not None
    nt, nf = S // tm, F // tf

    def body(*refs):
        if with_loss:
            (h_ref, g_ref, wg_ref, wu_ref, wd_ref, t_ref, nt_ref, a_ref, b_ref, acc_ref, dyh_ref, dyht_ref, loss_ref,
             n_ref) = refs
        else:
            h_ref, g_ref, wg_ref, wu_ref, wd_ref, nt_ref, a_ref, b_ref, acc_ref, n_ref = refs
        i = pl.program_id(0)
        j = pl.program_id(1)

        @pl.when(j == 0)
        def _():
            x = h_ref[...]
            nv = x * _rinv(x) * g_ref[...]
            n_ref[...] = nv.astype(BF16)
            nt_ref[...] = nv.T.astype(BF16)

        n = n_ref[...]
        a = _dot_nt(n, wg_ref[...])
        b = _dot_nt(n, wu_ref[...])
        a_ref[...] = a.astype(BF16)
        b_ref[...] = b.astype(BF16)
        s = (a * jax.nn.sigmoid(a) * b).astype(BF16)
        _acc(acc_ref, _dot(s, wd_ref[...]), j == 0)

        @pl.when(j == nf - 1)
        def _():
            y = h_ref[...] + 0.5 * acc_ref[...]
            if with_loss:
                e = y - t_ref[...]
                dy = e * (1.0 / D)
                acc_ref[...] = dy
                dyh_ref[...] = (0.5 * dy).astype(BF16)
                dyht_ref[...] = (0.5 * dy).T.astype(BF16)
                tot = jnp.sum(jnp.sum(e * e, axis=1, keepdims=True), axis=0, keepdims=True) * (0.5 / D)
                _acc(loss_ref, jnp.broadcast_to(tot, loss_ref.shape), i == 0)
            else:
                acc_ref[...] = y

    row = pl.BlockSpec((tm, D), lambda i, j: (i, 0))
    col = pl.BlockSpec((D, tm), lambda i, j: (0, i))
    wblk = pl.BlockSpec((tf, D), lambda i, j: (j, 0))
    act = pl.BlockSpec((tm, tf), lambda i, j: (i, j))
    in_specs = [row, pl.BlockSpec((1, D), lambda i, j: (0, 0)), wblk, wblk, wblk]
    out_specs = [col, act, act, row]
    out_shape = [jax.ShapeDtypeStruct((D, S), BF16), jax.ShapeDtypeStruct((S, F), BF16),
                 jax.ShapeDtypeStruct((S, F), BF16), jax.ShapeDtypeStruct((S, D), F32)]
    args = [h, gnorm, wgt, wut, wd]
    if with_loss:
        in_specs.append(row)
        args.append(target)
        out_specs += [row, col, pl.BlockSpec((8, 128), lambda i, j: (0, 0))]
        out_shape += [jax.ShapeDtypeStruct((S, D), BF16), jax.ShapeDtypeStruct((D, S), BF16),
                      jax.ShapeDtypeStruct((8, 128), F32)]
    return pl.pallas_call(
        body, name=name, grid=(nt, nf), in_specs=in_specs, out_specs=out_specs, out_shape=out_shape,
        scratch_shapes=[pltpu.VMEM((tm, D), BF16)],
        compiler_params=_params(dimension_semantics=("arbitrary", "arbitrary")),
    )(*args)


TOKEN_SPEC = pl.BlockSpec(memory_space=pl.ANY)


def _ffn_bwd_w(dyh, dyht, nt_, a, b, wd, token, name, tm=1024, tf=256):
    D, S = nt_.shape
    F = wd.shape[0]
    nt, nf = S // tm, F // tf

    def body(dyh_ref, dyht_ref, nt_ref, a_ref, b_ref, wd_ref, _, da_ref, db_ref, dwg_ref, dwu_ref, dwd_ref,
             accg, accu, accd):
        i = pl.program_id(1)
        av = a_ref[...].astype(F32)
        bv = b_ref[...].astype(F32)
        ds = _dot_nt(dyh_ref[...], wd_ref[...])
        sig = jax.nn.sigmoid(av)
        silu = av * sig
        da = (ds * bv * (sig * (1.0 + av * (1.0 - sig)))).astype(BF16)
        db = (ds * silu).astype(BF16)
        s = (silu * bv).astype(BF16)
        da_ref[...] = da
        db_ref[...] = db
        n_t = nt_ref[...]
        _acc(accg, _dot(n_t, da), i == 0)
        _acc(accu, _dot(n_t, db), i == 0)
        _acc(accd, _dot(dyht_ref[...], s), i == 0)

        @pl.when(i == nt - 1)
        def _():
            dwg_ref[...] = accg[...].T
            dwu_ref[...] = accu[...].T
            dwd_ref[...] = accd[...].T

    row = pl.BlockSpec((tm, D), lambda j, i: (i, 0))
    col = pl.BlockSpec((D, tm), lambda j, i: (0, i))
    act = pl.BlockSpec((tm, tf), lambda j, i: (i, j))
    wblk = pl.BlockSpec((tf, D), lambda j, i: (j, 0))
    return pl.pallas_call(
        body, name=name, grid=(nf, nt), in_specs=[row, col, col, act, act, wblk, TOKEN_SPEC],
        out_specs=[act, act, wblk, wblk, wblk],
        out_shape=[jax.ShapeDtypeStruct((S, F), BF16), jax.ShapeDtypeStruct((S, F), BF16),
                   jax.ShapeDtypeStruct((F, D), F32), jax.ShapeDtypeStruct((F, D), F32),
                   jax.ShapeDtypeStruct((F, D), F32)],
        scratch_shapes=[pltpu.VMEM((D, tf), F32)] * 3,
        compiler_params=_params(dimension_semantics=("arbitrary", "arbitrary")),
    )(dyh, dyht, nt_, a, b, wd, token)


def _ffn_bwd_x(da, db, wgt, wut, x, dh, gnorm, name, tm=512, tf=1408):
    S, D = x.shape
    F = wgt.shape[0]
    nt, nf = S // tm, F // tf

    def body(da_ref, db_ref, wg_ref, wu_ref, x_ref, dh_ref, g_ref, dx_ref, dg_ref):
        i = pl.program_id(0)
        j = pl.program_id(1)
        _acc(dx_ref, _dot(da_ref[...], wg_ref[...]) + _dot(db_ref[...], wu_ref[...]), j == 0)

        @pl.when(j == nf - 1)
        def _():
            dx, dg_rows = _rms_bwd(dx_ref[...], x_ref[...], g_ref[...])
            dx_ref[...] = dh_ref[...] + dx
            _acc(dg_ref, jnp.sum(dg_rows, axis=0, keepdims=True), i == 0)

    row = pl.BlockSpec((tm, D), lambda i, j: (i, 0))
    act = pl.BlockSpec((tm, tf), lambda i, j: (i, j))
    wblk = pl.BlockSpec((tf, D), lambda i, j: (j, 0))
    vec = pl.BlockSpec((1, D), lambda i, j: (0, 0))
    return pl.pallas_call(
        body, name=name, grid=(nt, nf), in_specs=[act, act, wblk, wblk, row, row, vec],
        out_specs=[row, vec],
        out_shape=[jax.ShapeDtypeStruct((S, D), F32), jax.ShapeDtypeStruct((1, D), F32)],
        compiler_params=_params(dimension_semantics=("arbitrary", "arbitrary")),
    )(da, db, wgt, wut, x, dh, gnorm)


def _mix_proj(h, gnorm, wint, tm=512):
    S, D = h.shape
    nt = S // tm

    def body(h_ref, g_ref, w_ref, ut_ref, xp_ref, q_ref, kv_ref, gl_ref):
        x = h_ref[...]
        uv = x * _rinv(x) * g_ref[...]
        ut_ref[...] = uv.T.astype(BF16)
        u = uv.astype(BF16)
        for (off, width), ref in zip(SEGMENTS, (xp_ref, q_ref, kv_ref, gl_ref)):
            ref[...] = _dot_nt(u, w_ref[off:off + width, :]).astype(ref.dtype)

    def row(width):
        return pl.BlockSpec((tm, width), lambda i: (i, 0))

    return pl.pallas_call(
        body, name="mix_proj", grid=(nt,),
        in_specs=[row(D), pl.BlockSpec((1, D), lambda i: (0, 0)), pl.BlockSpec(wint.shape, lambda i: (0, 0))],
        out_specs=[pl.BlockSpec((D, tm), lambda i: (0, i)), row(SEG_POOL[1]), row(SEG_Q[1]), row(SEG_KV[1]), row(SEG_GATE[1])],
        out_shape=[jax.ShapeDtypeStruct((D, S), BF16), jax.ShapeDtypeStruct((S, SEG_POOL[1]), F32),
                   jax.ShapeDtypeStruct((S, SEG_Q[1]), BF16), jax.ShapeDtypeStruct((S, SEG_KV[1]), BF16),
                   jax.ShapeDtypeStruct((S, SEG_GATE[1]), BF16)],
        compiler_params=_params(dimension_semantics=("arbitrary",)),
    )(h, gnorm, wint)


def _stack_heads(x, g):
    return jnp.concatenate([x[:, (GQA_GROUP * g + hh) * HEAD_DIM:(GQA_GROUP * g + hh + 1) * HEAD_DIM]
                            for hh in range(GQA_GROUP)], axis=0)


def _unstack_heads(ref, val, g):
    for hh in range(GQA_GROUP):
        lo = (GQA_GROUP * g + hh) * HEAD_DIM
        ref[:, lo:lo + HEAD_DIM] = val[hh * ATTN_BLOCK:(hh + 1) * ATTN_BLOCK, :]


def _attn_probs(qn, kc, kp, sink, n):
    rows = GQA_GROUP * ATTN_BLOCK
    qi = lax.broadcasted_iota(jnp.int32, (rows, ATTN_BLOCK), 0) % ATTN_BLOCK
    kj = lax.broadcasted_iota(jnp.int32, (rows, ATTN_BLOCK), 1)
    s_c = jnp.where(kj <= qi, _dot_nt(qn, kc), NEG)
    s_p = jnp.where(jnp.logical_and(kj > qi, n > 0), _dot_nt(qn, kp), NEG)
    m = jnp.maximum(jnp.maximum(jnp.max(s_c, axis=-1, keepdims=True), jnp.max(s_p, axis=-1, keepdims=True)), sink)
    e_c = jnp.exp(s_c - m)
    e_p = jnp.exp(s_p - m)
    e_s = jnp.exp(sink - m)
    inv = 1.0 / (jnp.sum(e_c, axis=-1, keepdims=True) + jnp.sum(e_p, axis=-1, keepdims=True) + e_s)
    return e_c * inv, e_p * inv, e_s * inv


def _attn_blocks(n):
    cur = pl.multiple_of(n * ATTN_BLOCK, ATTN_BLOCK)
    prev = pl.multiple_of(jnp.maximum(n - 1, 0) * ATTN_BLOCK, ATTN_BLOCK)
    return cur, prev


def _kv_split(kv, g):
    k = kv[:, g * HEAD_DIM:(g + 1) * HEAD_DIM]
    v = kv[:, (N_KV_HEADS + g) * HEAD_DIM:(N_KV_HEADS + g + 1) * HEAD_DIM]
    return k, v


def _attn_fwd(q, kv, qw, kw, sink_rows):
    S, W = q.shape
    nb = S // ATTN_BLOCK

    def body(q_ref, kv_ref, qw_ref, kw_ref, sk_ref, o_ref, o_scr):
        n = pl.program_id(0)
        qf = q_ref[...].astype(F32)
        cur, prev = _attn_blocks(n)
        kvc = kv_ref[pl.ds(cur, ATTN_BLOCK), :].astype(F32)
        kvp = kv_ref[pl.ds(prev, ATTN_BLOCK), :].astype(F32)
        for g in range(N_KV_HEADS):
            qs = _stack_heads(qf, g)
            qn = (qs * _rinv(qs) * qw_ref[...] * SCALE).astype(BF16)
            kc, vc = _kv_split(kvc, g)
            kp, vp = _kv_split(kvp, g)
            kc = (kc * _rinv(kc) * kw_ref[...]).astype(BF16)
            kp = (kp * _rinv(kp) * kw_ref[...]).astype(BF16)
            p_c, p_p, _ = _attn_probs(qn, kc, kp, sk_ref[g], n)
            o = _dot(p_c.astype(BF16), vc.astype(BF16)) + _dot(p_p.astype(BF16), vp.astype(BF16))
            _unstack_heads(o_scr, o, g)
        o_ref[...] = o_scr[...].astype(BF16)

    blk = pl.BlockSpec((ATTN_BLOCK, W), lambda n: (n, 0))
    return pl.pallas_call(
        body, name="attn_fwd", grid=(nb,),
        in_specs=[blk, pl.BlockSpec(kv.shape, lambda n: (0, 0)), pl.BlockSpec((1, HEAD_DIM), lambda n: (0, 0)),
                  pl.BlockSpec((1, HEAD_DIM), lambda n: (0, 0)), pl.BlockSpec(sink_rows.shape, lambda n: (0, 0, 0))],
        out_specs=blk, out_shape=jax.ShapeDtypeStruct((S, W), BF16),
        scratch_shapes=[pltpu.VMEM((ATTN_BLOCK, W), F32)],
        compiler_params=_params(dimension_semantics=("arbitrary",)),
    )(q, kv, qw, kw, sink_rows)


def _attn_bwd(q, kv, do, qw, kw, sink_rows):
    S, W = q.shape
    KW = kv.shape[1]
    nb = S // ATTN_BLOCK
    chunk = 512

    def body(q_ref, kv_ref, do_ref, qw_ref, kw_ref, sk_ref, dq_ref, dkv_ref, dqw_ref, dkw_ref, dsk_ref, dq_scr):
        n = pl.program_id(0)

        @pl.when(n == 0)
        def _():
            dkv_ref[...] = jnp.zeros_like(dkv_ref)
            dqw_ref[...] = jnp.zeros_like(dqw_ref)
            dsk_ref[...] = jnp.zeros_like(dsk_ref)

        qf = q_ref[...].astype(F32)
        dof = do_ref[...].astype(F32)
        cur, prev = _attn_blocks(n)
        kvc = kv_ref[pl.ds(cur, ATTN_BLOCK), :].astype(F32)
        kvp = kv_ref[pl.ds(prev, ATTN_BLOCK), :].astype(F32)
        qw_v = qw_ref[...]
        for g in range(N_KV_HEADS):
            qs = _stack_heads(qf, g)
            qhat = qs * _rinv(qs)
            qn = (qhat * qw_v * SCALE).astype(BF16)
            kc, vc = _kv_split(kvc, g)
            kp, vp = _kv_split(kvp, g)
            kc = (kc * _rinv(kc) * kw_ref[...]).astype(BF16)
            kp = (kp * _rinv(kp) * kw_ref[...]).astype(BF16)
            vc = vc.astype(BF16)
            vp = vp.astype(BF16)
            dos = _stack_heads(dof, g).astype(BF16)
            p_c, p_p, p_s = _attn_probs(qn, kc, kp, sk_ref[g], n)
            dp_c = _dot_nt(dos, vc)
            dp_p = _dot_nt(dos, vp)
            drow = jnp.sum(p_c * dp_c, axis=-1, keepdims=True) + jnp.sum(p_p * dp_p, axis=-1, keepdims=True)
            ds_c = (p_c * (dp_c - drow)).astype(BF16)
            ds_p = (p_p * (dp_p - drow)).astype(BF16)
            dsink = -(p_s * drow)
            for hh in range(GQA_GROUP):
                tot = jnp.sum(dsink[hh * ATTN_BLOCK:(hh + 1) * ATTN_BLOCK, :], axis=0, keepdims=True)
                dsk_ref[g, hh:hh + 1, :] += jnp.broadcast_to(tot, (1, 128))
            dqn = (_dot(ds_c, kc) + _dot(ds_p, kp)) * SCALE
            klo, vlo = g * HEAD_DIM, (N_KV_HEADS + g) * HEAD_DIM
            dkv_ref[pl.ds(cur, ATTN_BLOCK), klo:klo + HEAD_DIM] += _dot_tn(ds_c, qn)
            dkv_ref[pl.ds(prev, ATTN_BLOCK), klo:klo + HEAD_DIM] += _dot_tn(ds_p, qn)
            dkv_ref[pl.ds(cur, ATTN_BLOCK), vlo:vlo + HEAD_DIM] += _dot_tn(p_c.astype(BF16), dos)
            dkv_ref[pl.ds(prev, ATTN_BLOCK), vlo:vlo + HEAD_DIM] += _dot_tn(p_p.astype(BF16), dos)
            dqw_ref[...] += jnp.sum(dqn * qhat, axis=0, keepdims=True)
            z = dqn * qw_v
            dqs = _rinv(qs) * (z - qhat * jnp.mean(z * qhat, axis=-1, keepdims=True))
            _unstack_heads(dq_scr, dqs, g)
        dq_ref[...] = dq_scr[...].astype(BF16)

        @pl.when(n == nb - 1)
        def _():
            def one(c, dkw):
                rows = pl.ds(pl.multiple_of(c * chunk, chunk), chunk)
                for g in range(N_KV_HEADS):
                    lo = g * HEAD_DIM
                    k = kv_ref[rows, lo:lo + HEAD_DIM].astype(F32)
                    dx, dg_rows = _rms_bwd(dkv_ref[rows, lo:lo + HEAD_DIM], k, kw_ref[...])
                    dkv_ref[rows, lo:lo + HEAD_DIM] = dx
                    dkw = dkw + jnp.sum(dg_rows, axis=0, keepdims=True)
                return dkw

            dkw_ref[...] = lax.fori_loop(0, S // chunk, one, jnp.zeros((1, HEAD_DIM), F32))

    blk = pl.BlockSpec((ATTN_BLOCK, W), lambda n: (n, 0))
    whole_kv = pl.BlockSpec((S, KW), lambda n: (0, 0))
    vec = pl.BlockSpec((1, HEAD_DIM), lambda n: (0, 0))
    sk = pl.BlockSpec(sink_rows.shape, lambda n: (0, 0, 0))
    dsk = pl.BlockSpec((N_KV_HEADS, GQA_GROUP, 128), lambda n: (0, 0, 0))
    return pl.pallas_call(
        body, name="attn_bwd", grid=(nb,), in_specs=[blk, whole_kv, blk, vec, vec, sk],
        out_specs=[blk, whole_kv, vec, vec, dsk],
        out_shape=[jax.ShapeDtypeStruct((S, W), BF16), jax.ShapeDtypeStruct((S, KW), F32),
                   jax.ShapeDtypeStruct((1, HEAD_DIM), F32), jax.ShapeDtypeStruct((1, HEAD_DIM), F32),
                   jax.ShapeDtypeStruct((N_KV_HEADS, GQA_GROUP, 128), F32)],
        scratch_shapes=[pltpu.VMEM((ATTN_BLOCK, W), F32)],
        compiler_params=_params(dimension_semantics=("arbitrary",)),
    )(q, kv, do, qw, kw, sink_rows)


def _pooled(xc, xprev, i):
    tm = xc.shape[0]
    xh = jnp.concatenate([jnp.where(i > 0, xprev, 0.0), xc], axis=0)
    t = lax.broadcasted_iota(jnp.int32, (tm, 1), 0) + i * tm
    out = []
    for gi, w in enumerate(POOL_WINDOWS):
        acc = xh[:, gi * POOL_GROUP:(gi + 1) * POOL_GROUP]
        sh = 1
        while sh < w:
            acc = acc + pltpu.roll(acc, sh, 0)
            sh *= 2
        cnt = jnp.minimum(t + 1, w).astype(F32)
        out.append(acc[POOL_HALO:, :] / cnt - xc[:, gi * POOL_GROUP:(gi + 1) * POOL_GROUP])
    return jnp.concatenate(out, axis=1)


def _pool_mix(pooled_b, pw_ref):
    return jnp.concatenate([_dot(pooled_b[:, gi * POOL_GROUP:(gi + 1) * POOL_GROUP], pw_ref[gi])
                            for gi in range(len(POOL_WINDOWS))], axis=1)


def _halo_specs(tm, width, S, after):
    per = tm // POOL_HALO
    last = S // POOL_HALO - 1
    if after:
        return pl.BlockSpec((POOL_HALO, width), lambda i: (jnp.minimum((i + 1) * per, last), 0))
    return pl.BlockSpec((POOL_HALO, width), lambda i: (jnp.maximum(i * per - 1, 0), 0))


def _mix_out(xp, attn, gl, bias, pw, pscale, wpot, wao, wo, h, tm=512):
    S, D = h.shape
    nt = S // tm
    PW = xp.shape[1]

    def body(xc_ref, xprev_ref, at_ref, gl_ref, bias_ref, pw_ref, ps_ref, wpo_ref, wao_ref, wo_ref, h_ref,
             ho_ref, bp_ref, ba_ref):
        i = pl.program_id(0)
        pooled = _pooled(xc_ref[...], xprev_ref[...], i).astype(BF16)
        ms = (_pool_mix(pooled, pw_ref) * ps_ref[...]).astype(BF16)
        bp = _dot_nt(ms, wpo_ref[...])
        ba = _dot(at_ref[...], wao_ref[...])
        bp_ref[...] = bp.astype(BF16)
        ba_ref[...] = ba.astype(BF16)
        gates = jax.nn.sigmoid(gl_ref[...].astype(F32) + bias_ref[...])
        merged = (gates[:, :D] * bp + gates[:, D:] * ba).astype(BF16)
        ho_ref[...] = h_ref[...] + _dot(merged, wo_ref[...])

    def row(width):
        return pl.BlockSpec((tm, width), lambda i: (i, 0))

    def whole(x):
        nd = x.ndim
        return pl.BlockSpec(x.shape, lambda i: (0,) * nd)

    return pl.pallas_call(
        body, name="mix_out", grid=(nt,),
        in_specs=[row(PW), _halo_specs(tm, PW, S, False), row(D), row(2 * D), whole(bias), whole(pw), whole(pscale),
                  whole(wpot), whole(wao), whole(wo), row(D)],
        out_specs=[row(D), row(D), row(D)],
        out_shape=[jax.ShapeDtypeStruct((S, D), F32), jax.ShapeDtypeStruct((S, D), BF16),
                   jax.ShapeDtypeStruct((S, D), BF16)],
        compiler_params=_params(dimension_semantics=("arbitrary",)),
    )(xp, xp, attn, gl, bias, pw, pscale, wpot, wao, wo, h)


def _mix_bwd_gate(dh, wo, bp, ba, gl, bias, tm=512):
    S, D = dh.shape
    nt = S // tm

    def body(dh_ref, wo_ref, bp_ref, ba_ref, gl_ref, bias_ref, dgl_ref, dbp_ref, dba_ref, dwo_ref, dbias_ref):
        i = pl.program_id(0)
        dhb = dh_ref[...].astype(BF16)
        dm = _dot_nt(dhb, wo_ref[...])
        gates = jax.nn.sigmoid(gl_ref[...].astype(F32) + bias_ref[...])
        gp, ga = gates[:, :D], gates[:, D:]
        bp_v = bp_ref[...].astype(F32)
        ba_v = ba_ref[...].astype(F32)
        merged = (gp * bp_v + ga * ba_v).astype(BF16)
        _acc(dwo_ref, _dot_tn(merged, dhb), i == 0)
        dbp_ref[...] = (dm * gp).astype(BF16)
        dba_ref[...] = (dm * ga).astype(BF16)
        dgl = jnp.concatenate([dm * bp_v * gp * (1.0 - gp), dm * ba_v * ga * (1.0 - ga)], axis=1)
        dgl_ref[...] = dgl.astype(BF16)
        _acc(dbias_ref, jnp.sum(dgl, axis=0, keepdims=True), i == 0)

    def row(width):
        return pl.BlockSpec((tm, width), lambda i: (i, 0))

    def whole(shape):
        return pl.BlockSpec(shape, lambda i: (0, 0))

    return pl.pallas_call(
        body, name="mix_bwd_gate", grid=(nt,),
        in_specs=[row(D), whole(wo.shape), row(D), row(D), row(2 * D), whole(bias.shape)],
        out_specs=[row(2 * D), row(D), row(D), whole((D, D)), whole((1, 2 * D))],
        out_shape=[jax.ShapeDtypeStruct((S, 2 * D), BF16), jax.ShapeDtypeStruct((S, D), BF16),
                   jax.ShapeDtypeStruct((S, D), BF16), jax.ShapeDtypeStruct((D, D), F32),
                   jax.ShapeDtypeStruct((1, 2 * D), F32)],
        compiler_params=_params(dimension_semantics=("arbitrary",)),
    )(dh, wo, bp, ba, gl, bias)


def _mix_bwd_branch(dbp, dba, attn, xp, pw, pscale, wpot, wao, tm=512):
    S, D = dbp.shape
    nt = S // tm
    PW = xp.shape[1]
    NG = len(POOL_WINDOWS)

    def body(dbp_ref, dba_ref, at_ref, xc_ref, xprev_ref, pw_ref, ps_ref, wpo_ref, wao_ref,
             dat_ref, dpl_ref, dwao_ref, dwpo_ref, dpw_ref, dps_ref):
        i = pl.program_id(0)
        dba_v = dba_ref[...]
        dbp_v = dbp_ref[...]
        _acc(dwao_ref, _dot_tn(at_ref[...], dba_v), i == 0)
        dat_ref[...] = _dot_nt(dba_v, wao_ref[...]).astype(BF16)
        pooled = _pooled(xc_ref[...], xprev_ref[...], i).astype(BF16)
        mixed = _pool_mix(pooled, pw_ref)
        ps = ps_ref[...]
        _acc(dwpo_ref, _dot_tn(dbp_v, (mixed * ps).astype(BF16)), i == 0)
        dms = _dot(dbp_v, wpo_ref[...])
        _acc(dps_ref, jnp.sum(dms * mixed, axis=0, keepdims=True), i == 0)
        dmixed = (dms * ps).astype(BF16)
        dpooled = []
        for gi in range(NG):
            cols = slice(gi * POOL_GROUP, (gi + 1) * POOL_GROUP)
            _acc(dpw_ref.at[gi], _dot_tn(pooled[:, cols], dmixed[:, cols]), i == 0)
            dpooled.append(_dot_nt(dmixed[:, cols], pw_ref[gi]))
        dpl_ref[...] = jnp.concatenate(dpooled, axis=1)

    def row(width):
        return pl.BlockSpec((tm, width), lambda i: (i, 0))

    def whole(shape):
        nd = len(shape)
        return pl.BlockSpec(shape, lambda i: (0,) * nd)

    return pl.pallas_call(
        body, name="mix_bwd_branch", grid=(nt,),
        in_specs=[row(D), row(D), row(D), row(PW), _halo_specs(tm, PW, S, False), whole(pw.shape), whole(pscale.shape),
                  whole(wpot.shape), whole(wao.shape)],
        out_specs=[row(D), row(PW), whole((D, D)), whole((D, PW)), whole(pw.shape), whole((1, PW))],
        out_shape=[jax.ShapeDtypeStruct((S, D), BF16), jax.ShapeDtypeStruct((S, PW), F32),
                   jax.ShapeDtypeStruct((D, D), F32), jax.ShapeDtypeStruct((D, PW), F32),
                   jax.ShapeDtypeStruct(pw.shape, F32), jax.ShapeDtypeStruct((1, PW), F32)],
        compiler_params=_params(dimension_semantics=("arbitrary",)),
    )(dbp, dba, attn, xp, xp, pw, pscale, wpot, wao)


def _pool_bwd(dpooled, tm=512):
    S, PW = dpooled.shape
    nt = S // tm

    def body(dc_ref, dnext_ref, dxp_ref):
        i = pl.program_id(0)
        dc = dc_ref[...]
        dh = jnp.concatenate([dc, jnp.where(i < nt - 1, dnext_ref[...], 0.0)], axis=0)
        rows = tm + POOL_HALO
        t = lax.broadcasted_iota(jnp.int32, (rows, 1), 0) + i * tm
        out = []
        for gi, w in enumerate(POOL_WINDOWS):
            cols = slice(gi * POOL_GROUP, (gi + 1) * POOL_GROUP)
            acc = dh[:, cols] / jnp.minimum(t + 1, w).astype(F32)
            sh = 1
            while sh < w:
                acc = acc + pltpu.roll(acc, rows - sh, 0)
                sh *= 2
            out.append(acc[:tm, :] - dc[:, cols])
        dxp_ref[...] = jnp.concatenate(out, axis=1).astype(BF16)

    return pl.pallas_call(
        body, name="pool_bwd", grid=(nt,),
        in_specs=[pl.BlockSpec((tm, PW), lambda i: (i, 0)), _halo_specs(tm, PW, S, True)],
        out_specs=pl.BlockSpec((tm, PW), lambda i: (i, 0)), out_shape=jax.ShapeDtypeStruct((S, PW), BF16),
        compiler_params=_params(dimension_semantics=("arbitrary",)),
    )(dpooled, dpooled)


def _in_bwd_w(pieces, ut, tm=1024, chunk=256):
    D, S = ut.shape
    nt = S // tm
    NW = sum(width for _, width in SEGMENTS)
    nchunk = NW // chunk

    def body(*refs):
        piece_refs, ut_ref, dw_hbm, acc, stage, sem = (refs[:len(SEGMENTS)], refs[len(SEGMENTS)], refs[len(SEGMENTS) + 1],
                                                      refs[-3], refs[-2], refs[-1])
        i = pl.program_id(0)
        u_t = ut_ref[...]
        for (off, width), ref in zip(SEGMENTS, piece_refs):
            for lo in range(0, width, D):
                hi = min(lo + D, width)
                _acc(acc.at[:, off + lo:off + hi], _dot(u_t, ref[:, lo:hi].astype(BF16)), i == 0)

        @pl.when(i == nt - 1)
        def _():
            def out_copy(c, slot):
                return pltpu.make_async_copy(stage.at[slot], dw_hbm.at[pl.ds(c * chunk, chunk)], sem.at[slot])

            for c in range(nchunk):
                slot = c % 2
                if c >= 2:
                    out_copy(c - 2, slot).wait()
                stage[slot] = acc[:, c * chunk:(c + 1) * chunk].T
                out_copy(c, slot).start()
            for c in range(nchunk - 2, nchunk):
                out_copy(c, c % 2).wait()

    return pl.pallas_call(
        body, name="in_bwd_w", grid=(nt,),
        in_specs=[pl.BlockSpec((tm, width), lambda i: (i, 0)) for _, width in SEGMENTS] + [pl.BlockSpec((D, tm), lambda i: (0, i))],
        out_specs=pl.BlockSpec(memory_space=pl.ANY), out_shape=jax.ShapeDtypeStruct((NW, D), F32),
        scratch_shapes=[pltpu.VMEM((D, NW), F32), pltpu.VMEM((2, chunk, D), F32), pltpu.SemaphoreType.DMA((2,))],
        compiler_params=_params(dimension_semantics=("arbitrary",)),
    )(*pieces, ut)


def _in_bwd_x(pieces, wint, h, dh, gnorm, tm=512):
    S, D = h.shape
    nt = S // tm

    def body(*refs):
        piece_refs = refs[:len(SEGMENTS)]
        w_ref, h_ref, dh_ref, g_ref, dx_ref, dxh_ref, dxht_ref, dg_ref = refs[len(SEGMENTS):]
        i = pl.program_id(0)
        du = jnp.zeros((tm, D), F32)
        for (off, width), ref in zip(SEGMENTS, piece_refs):
            du = du + _dot(ref[...].astype(BF16), w_ref[off:off + width, :])
        dx, dg_rows = _rms_bwd(du, h_ref[...], g_ref[...])
        out = dh_ref[...] + dx
        dx_ref[...] = out
        dxh_ref[...] = (0.5 * out).astype(BF16)
        dxht_ref[...] = (0.5 * out).T.astype(BF16)
        _acc(dg_ref, jnp.sum(dg_rows, axis=0, keepdims=True), i == 0)

    def row(width):
        return pl.BlockSpec((tm, width), lambda i: (i, 0))

    vec = pl.BlockSpec((1, D), lambda i: (0, 0))
    return pl.pallas_call(
        body, name="in_bwd_x", grid=(nt,),
        in_specs=[row(width) for _, width in SEGMENTS] + [pl.BlockSpec(wint.shape, lambda i: (0, 0)), row(D), row(D), vec],
        out_specs=[row(D), row(D), pl.BlockSpec((D, tm), lambda i: (0, i)), vec],
        out_shape=[jax.ShapeDtypeStruct((S, D), F32), jax.ShapeDtypeStruct((S, D), BF16),
                   jax.ShapeDtypeStruct((D, S), BF16), jax.ShapeDtypeStruct((1, D), F32)],
        compiler_params=_params(dimension_semantics=("arbitrary",)),
    )(*pieces, wint, h, dh, gnorm)


def _row_tile(rows):
    for t in (512, 480, 352, 256, 128, 64, 32, 16, 8):
        if rows % t == 0:
            return t
    return rows


def _adamw(w, g, m, v, token, name):
    R, C = w.shape
    tr = _row_tile(R)

    def body(w_ref, g_ref, m_ref, v_ref, _, d_ref, mo_ref, vo_ref, go_ref):
        gv = g_ref[...]
        go_ref[...] = gv
        mn = ADAM_B1 * m_ref[...] + (1.0 - ADAM_B1) * gv
        vn = ADAM_B2 * v_ref[...] + (1.0 - ADAM_B2) * (gv * gv)
        m_hat = mn / (1.0 - ADAM_B1 ** ADAM_STEP)
        v_hat = vn / (1.0 - ADAM_B2 ** ADAM_STEP)
        d_ref[...] = -ADAM_LR * (m_hat / (jnp.sqrt(v_hat) + ADAM_EPS) + ADAM_WD * w_ref[...])
        mo_ref[...] = mn
        vo_ref[...] = vn

    blk = pl.BlockSpec((tr, C), lambda i: (i, 0))
    sh = jax.ShapeDtypeStruct((R, C), F32)
    return pl.pallas_call(
        body, name=name, grid=(R // tr,), in_specs=[blk] * 4 + [TOKEN_SPEC], out_specs=[blk] * 4, out_shape=[sh] * 4,
        compiler_params=_params(dimension_semantics=("arbitrary",)),
    )(w, g, m, v, token)


def _cast_place(w, place, name):
    R, C = w.shape
    tr = _row_tile(R)
    per = R // tr

    def body(p_ref, w_ref, o_ref):
        o_ref[...] = w_ref[...].astype(BF16)

    grid_spec = pltpu.PrefetchScalarGridSpec(
        num_scalar_prefetch=1, grid=(per,),
        in_specs=[pl.BlockSpec((tr, C), lambda i, p: (i, 0))],
        out_specs=pl.BlockSpec((tr, C), lambda i, p: (p[0] * per + i, 0)))
    return pl.pallas_call(
        body, name=name, grid_spec=grid_spec, out_shape=jax.ShapeDtypeStruct((N_SHARDS * R, C), BF16),
        compiler_params=_params(dimension_semantics=("arbitrary",)),
    )(place, w)


def _sum_halves(g4, recv, place, name):
    NS, R, C = g4.shape
    hr = R // 2
    tr = _row_tile(hr)
    per = hr // tr

    def body(p_ref, g_ref, r_ref, o_ref):
        o_ref[...] = (g_ref[...] + r_ref[...]).astype(BF16)

    grid_spec = pltpu.PrefetchScalarGridSpec(
        num_scalar_prefetch=1, grid=(NS, per),
        in_specs=[pl.BlockSpec((1, tr, C), lambda s, i, p: (s, p[1] * per + i, 0)),
                  pl.BlockSpec((1, tr, C), lambda s, i, p: (s, i, 0))],
        out_specs=pl.BlockSpec((1, tr, C), lambda s, i, p: (s, i, 0)))
    return pl.pallas_call(
        body, name=name, grid_spec=grid_spec, out_shape=jax.ShapeDtypeStruct((NS, hr, C), BF16),
        compiler_params=_params(dimension_semantics=("arbitrary", "arbitrary")),
    )(place, g4, recv)


def _sum_quarters(h4, recv3, place, name):
    NS, hr, C = h4.shape
    tr = _row_tile(hr)
    per = hr // tr

    def body(p_ref, h_ref, r_ref, o_ref):
        acc = h_ref[0].astype(F32)
        for k in range(N_SHARDS - 1):
            acc = acc + r_ref[k].astype(F32)
        o_ref[...] = acc

    grid_spec = pltpu.PrefetchScalarGridSpec(
        num_scalar_prefetch=1, grid=(per,),
        in_specs=[pl.BlockSpec((1, tr, C), lambda i, p: (p[0], i, 0)),
                  pl.BlockSpec((N_SHARDS - 1, tr, C), lambda i, p: (0, i, 0))],
        out_specs=pl.BlockSpec((tr, C), lambda i, p: (p[1] * per + i, 0)))
    return pl.pallas_call(
        body, name=name, grid_spec=grid_spec, out_shape=jax.ShapeDtypeStruct((2 * hr, C), F32),
        compiler_params=_params(dimension_semantics=("arbitrary",)),
    )(place, h4, recv3)


def _place():
    x, y, c = lax.axis_index("x"), lax.axis_index("y"), lax.axis_index("c")
    chips = [(1 - x, y), (x, 1 - y), (1 - x, 1 - y)]
    return x, y, c, chips


def _any_specs(n):
    return [pl.BlockSpec(memory_space=pl.ANY)] * n


def _gather_weights(wholes):
    nw = len(wholes)

    def body(*refs):
        bufs = refs[nw:2 * nw]
        ici_send, ici_recv, d2d_send, d2d_recv = refs[2 * nw:]
        x, y, c, chips = _place()
        me = 2 * x + y
        sibling = (x, y, 1 - c)

        def half(w, chip, core):
            hr = bufs[w].shape[0] // (2 * N_SHARDS)
            return bufs[w].at[pl.ds(pl.multiple_of((2 * chip + core) * hr, 16), hr)]

        def copy(w, k, rows, to, sems):
            return pltpu.make_async_remote_copy(src_ref=rows, dst_ref=rows, send_sem=sems[0].at[w, k],
                                                recv_sem=sems[1].at[w, k], device_id=to, device_id_type=MESH)

        ici, d2d = (ici_send, ici_recv), (d2d_send, d2d_recv)
        sends = []
        for w in range(nw):
            for k, (px, py) in enumerate(chips):
                sends.append(copy(w, k, half(w, me, c), (px, py, c), ici))
                sends[-1].start()
        for w in range(nw):
            for k, (px, py) in enumerate(chips):
                landed = half(w, 2 * px + py, c)
                copy(w, k, landed, (px, py, c), ici).wait_recv()
                sends.append(copy(w, k, landed, sibling, d2d))
                sends[-1].start()
        for w in range(nw):
            for k, (px, py) in enumerate(chips):
                copy(w, k, half(w, 2 * px + py, 1 - c), sibling, d2d).wait_recv()
        for cp in sends:
            cp.wait_send()

    sem = pltpu.SemaphoreType.DMA((nw, N_SHARDS - 1))
    return pl.pallas_call(
        body, name="gather_weights", in_specs=_any_specs(nw), out_specs=_any_specs(nw),
        out_shape=[jax.ShapeDtypeStruct(s.shape, s.dtype) for s in wholes],
        input_output_aliases={i: i for i in range(nw)}, scratch_shapes=[sem, sem, sem, sem],
        compiler_params=pltpu.CompilerParams(has_side_effects=True),
    )(*wholes)


HBM_SPEC = pl.BlockSpec(memory_space=pltpu.HBM)
SEM_SPEC = pl.BlockSpec(memory_space=pltpu.SEMAPHORE)
DATAFLOW = pltpu.SideEffectType.DATAFLOW_SIDE_EFFECTING


def _hbm(a):
    return pltpu.with_memory_space_constraint(a, pltpu.HBM)


class InFlight(NamedTuple):
    send_sem: jax.Array
    recv_sem: jax.Array
    bufs: list
    plan: Callable
    token: jax.Array


def _wait_all(plan, refs, send_ref, recv_ref):
    for k, (src, dst, dev) in enumerate(plan(refs)):
        cp = pltpu.make_async_remote_copy(src_ref=src, dst_ref=dst, send_sem=send_ref.at[k], recv_sem=recv_ref.at[k],
                                          device_id=dev, device_id_type=MESH)
        cp.wait_send()
        cp.wait_recv()


def _start_all(plan, refs, send_ref, recv_ref):
    for k, (src, dst, dev) in enumerate(plan(refs)):
        pltpu.make_async_remote_copy(src_ref=src, dst_ref=dst, send_sem=send_ref.at[k], recv_sem=recv_ref.at[k],
                                     device_id=dev, device_id_type=MESH).start()


def _split_start(name, bufs, plan, n_copies, after, waits=None):
    nb = len(bufs)

    def body(*refs):
        ins = refs[:nb]
        if waits is not None:
            _wait_all(waits.plan, ins, refs[nb], refs[nb + 1])
        send_sem, recv_sem, token = refs[-nb - 3], refs[-nb - 2], refs[-1]
        _start_all(plan, ins, send_sem, recv_sem)
        token[...] = jnp.zeros_like(token)

    sems = pltpu.SemaphoreType.DMA((n_copies,))
    earlier = [] if waits is None else [waits.send_sem, waits.recv_sem]
    out = pl.pallas_call(
        body, name=name, in_specs=[HBM_SPEC] * nb + [SEM_SPEC] * len(earlier) + [pl.BlockSpec(memory_space=pl.ANY)],
        out_shape=(sems, sems, *[pltpu.HBM(b.shape, b.dtype) for b in bufs], jax.ShapeDtypeStruct((8, 128), F32)),
        out_specs=(SEM_SPEC, SEM_SPEC, *[HBM_SPEC] * nb, pl.BlockSpec(memory_space=pltpu.VMEM)),
        input_output_aliases={i: i + 2 for i in range(nb)},
        compiler_params=pltpu.CompilerParams(has_side_effects=DATAFLOW),
    )(*[_hbm(b) for b in bufs], *earlier, after)
    return InFlight(out[0], out[1], list(out[2:2 + nb]), plan, out[-1])


def _split_wait(name, flight, after):
    nb = len(flight.bufs)

    def body(*refs):
        _wait_all(flight.plan, refs[:nb], refs[nb], refs[nb + 1])

    out = pl.pallas_call(
        body, name=name, in_specs=[HBM_SPEC] * nb + [SEM_SPEC, SEM_SPEC, pl.BlockSpec(memory_space=pl.ANY)],
        out_shape=[pltpu.HBM(b.shape, b.dtype) for b in flight.bufs], out_specs=[HBM_SPEC] * nb,
        input_output_aliases={i: i for i in range(nb)},
        compiler_params=pltpu.CompilerParams(has_side_effects=DATAFLOW),
    )(*flight.bufs, flight.send_sem, flight.recv_sem, after)
    return list(out)


def _half_rows(buf, chip, core):
    hr = buf.shape[0] // (2 * N_SHARDS)
    return buf.at[pl.ds(pl.multiple_of((2 * chip + core) * hr, 16), hr)]


def _gather_ici_plan(bufs):
    x, y, c, chips = _place()
    return [(_half_rows(b, 2 * x + y, c), _half_rows(b, 2 * x + y, c), (px, py, c)) for b in bufs for px, py in chips]


def _gather_d2d_plan(bufs):
    x, y, c, chips = _place()
    return [(_half_rows(b, 2 * px + py, c), _half_rows(b, 2 * px + py, c), (x, y, 1 - c)) for b in bufs for px, py in chips]


def _swap_plan(bufs):
    x, y, c, _ = _place()
    n = len(bufs) // 2
    copies = []
    for g, land in zip(bufs[:n], bufs[n:]):
        hr = g.shape[1] // 2
        copies.append((g.at[:, pl.ds(pl.multiple_of((1 - c) * hr, 8), hr)], land, (x, y, 1 - c)))
    return copies


def _exchange_plan(bufs):
    x, y, c, chips = _place()
    n = len(bufs) // 2
    return [(h.at[2 * px + py], land.at[k], (px, py, c))
            for h, land in zip(bufs[:n], bufs[n:]) for k, (px, py) in enumerate(chips)]


def _share_plan(bufs):
    x, y, c, _ = _place()
    copies = []
    for buf in bufs:
        hr = buf.shape[0] // 2
        mine = buf.at[pl.ds(pl.multiple_of(c * hr, 8), hr)]
        copies.append((mine, mine, (x, y, 1 - c)))
    return copies


N_DEVICES = 8


def _slot_place(vec, place):
    R, C = vec.shape

    def body(p_ref, v_ref, o_ref):
        o_ref[0] = v_ref[...]

    grid_spec = pltpu.PrefetchScalarGridSpec(
        num_scalar_prefetch=1, grid=(1,), in_specs=[pl.BlockSpec((R, C), lambda i, p: (0, 0))],
        out_specs=pl.BlockSpec((1, R, C), lambda i, p: (2 * p[0] + p[1], 0, 0)))
    return pl.pallas_call(
        body, name="slot_place", grid_spec=grid_spec, out_shape=jax.ShapeDtypeStruct((N_DEVICES, R, C), F32),
        compiler_params=_params(dimension_semantics=("arbitrary",)),
    )(place, vec)


def _slots_plan(bufs):
    x, y, c, _ = _place()
    mine = bufs[0].at[4 * x + 2 * y + c]
    return [(mine, mine, (x ^ (r >> 2), y ^ ((r >> 1) & 1), c ^ (r & 1))) for r in range(1, N_DEVICES)]


def _sum_slots(slots):
    _, R, C = slots.shape

    def body(s_ref, o_ref):
        acc = s_ref[0]
        for d in range(1, N_DEVICES):
            acc = acc + s_ref[d]
        o_ref[...] = acc

    return pl.pallas_call(
        body, name="sum_slots", in_specs=[pl.BlockSpec(memory_space=pltpu.VMEM)],
        out_specs=pl.BlockSpec(memory_space=pltpu.VMEM), out_shape=jax.ShapeDtypeStruct((R, C), F32),
        compiler_params=_params(),
    )(slots)


SMALL = ("ffn1_norm", "mix_norm", "pool_w", "pool_scale", "q_norm", "k_norm", "sinks", "gate_bias", "ffn2_norm")
SMALL_COLS = 1024
FFN1 = ("ffn1_w_gate", "ffn1_w_up", "ffn1_w_down")
MIXER = ("w_in", "w_pool_out", "w_attn_out", "w_out")
FFN2 = ("ffn2_w_gate", "ffn2_w_up", "ffn2_w_down")
LARGE = FFN1 + MIXER + FFN2
TRANSPOSED = ("ffn1_w_gate", "ffn1_w_up", "w_in", "w_pool_out", "ffn2_w_gate", "ffn2_w_up")
WEIGHTS = ("ffn1_norm", "ffn1_w_gate", "ffn1_w_up", "ffn1_w_down", "mix_norm", "w_in", "pool_w", "pool_scale",
           "w_pool_out", "q_norm", "k_norm", "sinks", "w_attn_out", "gate_bias", "w_out", "ffn2_norm",
           "ffn2_w_gate", "ffn2_w_up", "ffn2_w_down")


def _pack_small(parts):
    flat = jnp.concatenate([p.reshape(-1) for p in parts])
    rows = -(-flat.shape[0] // (8 * SMALL_COLS)) * 8
    return jnp.pad(flat, (0, rows * SMALL_COLS - flat.shape[0])).reshape(rows, SMALL_COLS)


def _unpack_small(packed, like):
    flat = packed.reshape(-1)
    out, off = [], 0
    for p in like:
        out.append(flat[off:off + p.size].reshape(p.shape))
        off += p.size
    return out, flat[off]


def _tie(small, token):
    return small + token[0, 0]


class Reduction:
    def __init__(self, group, names, grads, place):
        self.group, self.names, self.place = group, names, place
        self.bufs = [g.reshape(N_SHARDS, -1, g.shape[-1]) for g in grads]
        self.flight = None

    def _start(self, stage, bufs, plan, n_copies, after):
        self.flight = _split_start(f"{self.group}_{stage}", bufs, plan, n_copies, after)
        return self.flight.token

    def _landed(self, stage, after):
        n = len(self.names)
        bufs = _split_wait(f"{self.group}_{stage}_wait", self.flight, after)
        return bufs[:n], bufs[n:]

    def start_swap(self, after):
        lands = [lax.empty((g.shape[0], g.shape[1] // 2, g.shape[2]), g.dtype) for g in self.bufs]
        return self._start("swap", self.bufs + lands, _swap_plan, len(self.bufs), after)

    def start_exchange(self, after):
        g4, recv = self._landed("swap", after)
        halves = [_sum_halves(g, r, self.place, f"sum_halves_{n}") for n, g, r in zip(self.names, g4, recv)]
        lands = [lax.empty((N_SHARDS - 1,) + h.shape[1:], h.dtype) for h in halves]
        return self._start("exchange", halves + lands, _exchange_plan, (N_SHARDS - 1) * len(halves), self.place)

    def start_share(self, after):
        halves, recv3 = self._landed("exchange", after)
        reduced = [_sum_quarters(h, r, self.place, f"sum_quarters_{n}") for n, h, r in zip(self.names, halves, recv3)]
        return self._start("share", reduced, _share_plan, len(reduced), self.place)

    def finish(self, after):
        return dict(zip(self.names, _split_wait(f"{self.group}_share_wait", self.flight, after)))


def _row_form(name, a):
    return a.T if name in TRANSPOSED else a


def kernel(x, ffn1_norm, ffn1_w_gate, ffn1_w_up, ffn1_w_down, mix_norm, w_in, pool_w, pool_scale, w_pool_out, q_norm, k_norm, sinks, w_attn_out, gate_bias, w_out, ffn2_norm, ffn2_w_gate, ffn2_w_up, ffn2_w_down, loss_target, m_ffn1_norm, m_ffn1_w_gate, m_ffn1_w_up, m_ffn1_w_down, m_mix_norm, m_w_in, m_pool_w, m_pool_scale, m_w_pool_out, m_q_norm, m_k_norm, m_sinks, m_w_attn_out, m_gate_bias, m_w_out, m_ffn2_norm, m_ffn2_w_gate, m_ffn2_w_up, m_ffn2_w_down, v_ffn1_norm, v_ffn1_w_gate, v_ffn1_w_up, v_ffn1_w_down, v_mix_norm, v_w_in, v_pool_w, v_pool_scale, v_w_pool_out, v_q_norm, v_k_norm, v_sinks, v_w_attn_out, v_gate_bias, v_w_out, v_ffn2_norm, v_ffn2_w_gate, v_ffn2_w_up, v_ffn2_w_down):
    args = dict(locals())
    wts = {n: _row_form(n, args[n]) for n in WEIGHTS}
    mom = {n: _row_form(n, args["m_" + n]) for n in WEIGHTS}
    var = {n: _row_form(n, args["v_" + n]) for n in WEIGHTS}
    shard = 2 * lax.axis_index("x") + lax.axis_index("y")
    place = jnp.stack([shard, lax.axis_index("c")]).astype(jnp.int32)

    xs, target = x[0], loss_target[0]
    D = xs.shape[1]
    g1 = wts["ffn1_norm"].reshape(1, D)
    gm = wts["mix_norm"].reshape(1, D)
    g2 = wts["ffn2_norm"].reshape(1, D)
    qw = wts["q_norm"].reshape(1, HEAD_DIM)
    kw = wts["k_norm"].reshape(1, HEAD_DIM)
    bias = wts["gate_bias"].reshape(1, 2 * D)
    pscale = wts["pool_scale"].reshape(1, -1)
    pw = wts["pool_w"].astype(BF16)
    sink_rows = jnp.repeat(wts["sinks"], ATTN_BLOCK).reshape(N_KV_HEADS, GQA_GROUP * ATTN_BLOCK, 1)

    placed = {n: _cast_place(wts[n], place, "cast_" + n) for n in LARGE}
    w = dict(zip(FFN1, _gather_weights([placed[n] for n in FFN1])))
    mix_ici = _split_start("gather_mix_ici", [placed[n] for n in MIXER], _gather_ici_plan, 3 * len(MIXER), w[FFN1[0]])
    ffn2_ici = _split_start("gather_ffn2_ici", [placed[n] for n in FFN2], _gather_ici_plan, 3 * len(FFN2), mix_ici.token)

    n1t, a1, b1, h1 = _ffn_fwd(xs, _tie(g1, ffn2_ici.token), w["ffn1_w_gate"], w["ffn1_w_up"], w["ffn1_w_down"], None, "ffn1_fwd")
    mix_d2d = _split_start("gather_mix_d2d", mix_ici.bufs, _gather_d2d_plan, 3 * len(MIXER), h1, waits=mix_ici)
    w.update(zip(MIXER, _split_wait("gather_mix_wait", mix_d2d, mix_d2d.token)))
    ut, xp, q, kv, gl = _mix_proj(h1, gm, w["w_in"])
    attn = _attn_fwd(q, kv, qw, kw, sink_rows)
    ffn2_d2d = _split_start("gather_ffn2_d2d", ffn2_ici.bufs, _gather_d2d_plan, 3 * len(FFN2), attn, waits=ffn2_ici)
    h2, bp, ba = _mix_out(xp, attn, gl, _tie(bias, ffn2_d2d.token), pw, pscale, w["w_pool_out"], w["w_attn_out"], w["w_out"], h1)
    w.update(zip(FFN2, _split_wait("gather_ffn2_wait", ffn2_d2d, h2)))
    n2t, a2, b2, dy, dyh, dyht, loss = _ffn_fwd(h2, g2, w["ffn2_w_gate"], w["ffn2_w_up"], w["ffn2_w_down"], target, "ffn2_fwd")

    gw, gs = {}, {}
    da, db, gw["ffn2_w_gate"], gw["ffn2_w_up"], gw["ffn2_w_down"] = _ffn_bwd_w(
        dyh, dyht, n2t, a2, b2, w["ffn2_w_down"], ffn2_d2d.token, "ffn2_bwd_w")
    red2 = Reduction("reduce_ffn2", FFN2, [gw[n] for n in FFN2], place)
    token = red2.start_swap(da)
    dh2, gs["ffn2_norm"] = _ffn_bwd_x(da, db, w["ffn2_w_gate"], w["ffn2_w_up"], h2, dy, _tie(g2, token), "ffn2_bwd_x")
    token = red2.start_exchange(dh2)
    dgl, dbp, dba, gw["w_out"], gs["gate_bias"] = _mix_bwd_gate(dh2, w["w_out"], bp, ba, gl, _tie(bias, token))
    dattn, dpooled, gw["w_attn_out"], gw["w_pool_out"], gs["pool_w"], gs["pool_scale"] = _mix_bwd_branch(
        dbp, dba, attn, xp, pw, pscale, w["w_pool_out"], w["w_attn_out"])
    dq, dkv, gs["q_norm"], gs["k_norm"], dsk = _attn_bwd(q, kv, dattn, qw, kw, sink_rows)
    gs["sinks"] = dsk[:, :, 0]
    red2.start_share(dq)
    pieces = (_pool_bwd(dpooled), dq, dkv, dgl)
    gw["w_in"] = _in_bwd_w(pieces, ut)
    grads = red2.finish(gw["w_in"])
    redm = Reduction("reduce_mix", MIXER, [gw[n] for n in MIXER], place)
    token = redm.start_swap(grads[FFN2[0]])
    dh1, dh1h, dh1ht, gs["mix_norm"] = _in_bwd_x(pieces, w["w_in"], h1, dh2, _tie(gm, token))
    token = redm.start_exchange(dh1)
    da, db, gw["ffn1_w_gate"], gw["ffn1_w_up"], gw["ffn1_w_down"] = _ffn_bwd_w(
        dh1h, dh1ht, n1t, a1, b1, w["ffn1_w_down"], token, "ffn1_bwd_w")
    token = redm.start_share(da)
    red1 = Reduction("reduce_ffn1", FFN1, [gw[n] for n in FFN1], place)
    token = red1.start_swap(token)
    grad_x, gs["ffn1_norm"] = _ffn_bwd_x(da, db, w["ffn1_w_gate"], w["ffn1_w_up"], xs, dh1, _tie(g1, token), "ffn1_bwd_x")
    grads.update(redm.finish(grad_x))

    small_parts = [gs[n] for n in SMALL] + [loss[0, 0].reshape(1)]
    slots = _split_start("gather_small", [_slot_place(_pack_small(small_parts), place)], _slots_plan, N_DEVICES - 1,
                         grads[MIXER[0]])
    token = red1.start_exchange(slots.token)
    delta, new_m, new_v = {}, {}, {}
    for n in FFN2 + MIXER:
        delta[n], new_m[n], new_v[n], grads[n] = _adamw(wts[n], grads[n], mom[n], var[n], token, "adamw_" + n)
    summed = _sum_slots(_split_wait("gather_small_wait", slots, delta[MIXER[-1]])[0])
    small_grads, loss_sum = _unpack_small(summed, [wts[n] for n in SMALL])
    grads.update(dict(zip(SMALL, small_grads)))
    zero = jnp.zeros((1,), F32)
    packed = [_pack_small([d[n] for n in SMALL] + [zero]) for d in (wts, mom, var)]
    ds, ms, vs, _ = _adamw(packed[0], summed, packed[1], packed[2], token, "adamw_small")
    like = [wts[n] for n in SMALL]
    for out, packed_out in ((delta, ds), (new_m, ms), (new_v, vs)):
        out.update(dict(zip(SMALL, _unpack_small(packed_out, like)[0])))
    token = red1.start_share(ds)
    grads.update(red1.finish(token))
    for n in FFN1:
        delta[n], new_m[n], new_v[n], grads[n] = _adamw(wts[n], grads[n], mom[n], var[n], token, "adamw_" + n)

    outs = []
    for d in (grads, delta, new_m, new_v):
        outs += [_row_form(n, d[n]) for n in WEIGHTS]
    return (loss_sum, grad_x[None], *outs)
```

```python
from typing import Callable, NamedTuple

import jax
import jax.numpy as jnp
from jax import lax
from jax.experimental import pallas as pl
from jax.experimental.pallas import tpu as pltpu

F32 = jnp.float32
BF16 = jnp.bfloat16
RMS_EPS = 1e-6
POOL_WINDOWS = (2, 4, 8, 16)
POOL_GROUP = 128
POOL_HALO = 16
HEAD_DIM = 64
GQA_GROUP = 8
N_KV_HEADS = 2
ATTN_BLOCK = 128
SCALE = HEAD_DIM ** -0.5
NEG = -1e30
N_SHARDS = 4
ADAM_LR, ADAM_B1, ADAM_B2, ADAM_EPS, ADAM_WD, ADAM_STEP = 0.001, 0.9, 0.999, 1e-08, 0.01, 10
VMEM_LIMIT = 56 * 1024 * 1024
MESH = pl.DeviceIdType.MESH
SEG_POOL, SEG_Q, SEG_KV, SEG_GATE = (0, 512), (512, 1024), (1536, 256), (1792, 2048)
SEGMENTS = (SEG_POOL, SEG_Q, SEG_KV, SEG_GATE)


def _params(**kw):
    return pltpu.CompilerParams(vmem_limit_bytes=VMEM_LIMIT, **kw)


def _dot(a, b):
    return jnp.dot(a, b, preferred_element_type=F32)


def _dot_nt(a, b):
    return lax.dot_general(a, b, (((1,), (1,)), ((), ())), preferred_element_type=F32)


def _dot_tn(a, b):
    return lax.dot_general(a, b, (((0,), (0,)), ((), ())), preferred_element_type=F32)


def _rinv(x):
    return lax.rsqrt(jnp.mean(x * x, axis=-1, keepdims=True) + RMS_EPS)


def _rms_bwd(dn, x, g):
    r = _rinv(x)
    xr = x * r
    z = dn * g
    dx = r * (z - xr * jnp.mean(z * xr, axis=-1, keepdims=True))
    return dx, dn * xr


def _acc(ref, val, first):
    @pl.when(first)
    def _():
        ref[...] = val

    @pl.when(jnp.logical_not(first))
    def _():
        ref[...] += val


def _ffn_fwd(h, gnorm, wgt, wut, wd, target, name, tm=512, tf=1408):
    S, D = h.shape
    F = wd.shape[0]
    with_loss = target is not None
    nt, nf = S // tm, F // tf

    def body(*refs):
        if with_loss:
            (h_ref, g_ref, wg_ref, wu_ref, wd_ref, t_ref, nt_ref, a_ref, b_ref, acc_ref, dyh_ref, dyht_ref, loss_ref,
             n_ref) = refs
        else:
            h_ref, g_ref, wg_ref, wu_ref, wd_ref, nt_ref, a_ref, b_ref, acc_ref, n_ref = refs
        i = pl.program_id(0)
        j = pl.program_id(1)

        @pl.when(j == 0)
        def _():
            x = h_ref[...]
            nv = x * _rinv(x) * g_ref[...]
            n_ref[...] = nv.astype(BF16)
            nt_ref[...] = nv.T.astype(BF16)

        n = n_ref[...]
        a = _dot_nt(n, wg_ref[...])
        b = _dot_nt(n, wu_ref[...])
        a_ref[...] = a.astype(BF16)
        b_ref[...] = b.astype(BF16)
        s = (a * jax.nn.sigmoid(a) * b).astype(BF16)
        _acc(acc_ref, _dot(s, wd_ref[...]), j == 0)

        @pl.when(j == nf - 1)
        def _():
            y = h_ref[...] + 0.5 * acc_ref[...]
            if with_loss:
                e = y - t_ref[...]
                dy = e * (1.0 / D)
                acc_ref[...] = dy
                dyh_ref[...] = (0.5 * dy).astype(BF16)
                dyht_ref[...] = (0.5 * dy).T.astype(BF16)
                tot = jnp.sum(jnp.sum(e * e, axis=1, keepdims=True), axis=0, keepdims=True) * (0.5 / D)
                _acc(loss_ref, jnp.broadcast_to(tot, loss_ref.shape), i == 0)
            else:
                acc_ref[...] = y

    row = pl.BlockSpec((tm, D), lambda i, j: (i, 0))
    col = pl.BlockSpec((D, tm), lambda i, j: (0, i))
    wblk = pl.BlockSpec((tf, D), lambda i, j: (j, 0))
    act = pl.BlockSpec((tm, tf), lambda i, j: (i, j))
    in_specs = [row, pl.BlockSpec((1, D), lambda i, j: (0, 0)), wblk, wblk, wblk]
    out_specs = [col, act, act, row]
    out_shape = [jax.ShapeDtypeStruct((D, S), BF16), jax.ShapeDtypeStruct((S, F), BF16),
                 jax.ShapeDtypeStruct((S, F), BF16), jax.ShapeDtypeStruct((S, D), F32)]
    args = [h, gnorm, wgt, wut, wd]
    if with_loss:
        in_specs.append(row)
        args.append(target)
        out_specs += [row, col, pl.BlockSpec((8, 128), lambda i, j: (0, 0))]
        out_shape += [jax.ShapeDtypeStruct((S, D), BF16), jax.ShapeDtypeStruct((D, S), BF16),
                      jax.ShapeDtypeStruct((8, 128), F32)]
    return pl.pallas_call(
        body, name=name, grid=(nt, nf), in_specs=in_specs, out_specs=out_specs, out_shape=out_shape,
        scratch_shapes=[pltpu.VMEM((tm, D), BF16)],
        compiler_params=_params(dimension_semantics=("arbitrary", "arbitrary")),
    )(*args)


TOKEN_SPEC = pl.BlockSpec(memory_space=pl.ANY)


def _ffn_bwd_h(dyh, a, b, wd, token, name, tm=512, tf=1408):
    S, D = dyh.shape
    F = wd.shape[0]
    nt, nf = S // tm, F // tf

    def body(dyh_ref, a_ref, b_ref, wd_ref, _, da_ref, db_ref, s_ref):
        av = a_ref[...].astype(F32)
        bv = b_ref[...].astype(F32)
        ds = _dot_nt(dyh_ref[...], wd_ref[...])
        sig = jax.nn.sigmoid(av)
        silu = av * sig
        da_ref[...] = (ds * bv * (sig * (1.0 + av * (1.0 - sig)))).astype(BF16)
        db_ref[...] = (ds * silu).astype(BF16)
        s_ref[...] = (silu * bv).astype(BF16)

    act = pl.BlockSpec((tm, tf), lambda i, j: (i, j))
    hidden = jax.ShapeDtypeStruct((S, F), BF16)
    return pl.pallas_call(
        body, name=name, grid=(nt, nf),
        in_specs=[pl.BlockSpec((tm, D), lambda i, j: (i, 0)), act, act, pl.BlockSpec((tf, D), lambda i, j: (j, 0)), TOKEN_SPEC],
        out_specs=[act, act, act], out_shape=[hidden, hidden, hidden],
        compiler_params=_params(dimension_semantics=("arbitrary", "arbitrary")),
    )(dyh, a, b, wd, token)


def _xty(yt, x, name, tk=1024, tf=1408, chunk=128):
    D, S = yt.shape
    F = x.shape[1]
    nk, nf = S // tk, F // tf
    nchunk = tf // chunk

    def body(yt_ref, x_ref, out_hbm, acc, stage, sem):
        j = pl.program_id(0)
        k = pl.program_id(1)
        _acc(acc, _dot(yt_ref[...], x_ref[...]), k == 0)

        @pl.when(k == nk - 1)
        def _():
            def out_copy(c, slot):
                rows = pl.ds(pl.multiple_of(j * tf + c * chunk, chunk), chunk)
                return pltpu.make_async_copy(stage.at[slot], out_hbm.at[rows], sem.at[slot])

            for c in range(nchunk):
                slot = c % 2
                if c >= 2:
                    out_copy(c - 2, slot).wait()
                stage[slot] = acc[:, c * chunk:(c + 1) * chunk].T
                out_copy(c, slot).start()
            for c in range(nchunk - 2, nchunk):
                out_copy(c, c % 2).wait()

    return pl.pallas_call(
        body, name=name, grid=(nf, nk),
        in_specs=[pl.BlockSpec((D, tk), lambda j, k: (0, k)), pl.BlockSpec((tk, tf), lambda j, k: (k, j))],
        out_specs=pl.BlockSpec(memory_space=pl.ANY), out_shape=jax.ShapeDtypeStruct((F, D), F32),
        scratch_shapes=[pltpu.VMEM((D, tf), F32), pltpu.VMEM((2, chunk, D), F32), pltpu.SemaphoreType.DMA((2,))],
        compiler_params=_params(dimension_semantics=("arbitrary", "arbitrary")),
    )(yt, x)


def _ffn_bwd_x(da, db, wgt, wut, x, dh, gnorm, name, tm=512, tf=1408):
    S, D = x.shape
    F = wgt.shape[0]
    nt, nf = S // tm, F // tf

    def body(da_ref, db_ref, wg_ref, wu_ref, x_ref, dh_ref, g_ref, dx_ref, dg_ref):
        i = pl.program_id(0)
        j = pl.program_id(1)
        _acc(dx_ref, _dot(da_ref[...], wg_ref[...]) + _dot(db_ref[...], wu_ref[...]), j == 0)

        @pl.when(j == nf - 1)
        def _():
            dx, dg_rows = _rms_bwd(dx_ref[...], x_ref[...], g_ref[...])
            dx_ref[...] = dh_ref[...] + dx
            _acc(dg_ref, jnp.sum(dg_rows, axis=0, keepdims=True), i == 0)

    row = pl.BlockSpec((tm, D), lambda i, j: (i, 0))
    act = pl.BlockSpec((tm, tf), lambda i, j: (i, j))
    wblk = pl.BlockSpec((tf, D), lambda i, j: (j, 0))
    vec = pl.BlockSpec((1, D), lambda i, j: (0, 0))
    return pl.pallas_call(
        body, name=name, grid=(nt, nf), in_specs=[act, act, wblk, wblk, row, row, vec],
        out_specs=[row, vec],
        out_shape=[jax.ShapeDtypeStruct((S, D), F32), jax.ShapeDtypeStruct((1, D), F32)],
        compiler_params=_params(dimension_semantics=("arbitrary", "arbitrary")),
    )(da, db, wgt, wut, x, dh, gnorm)


def _mix_proj(h, gnorm, wint, tm=512):
    S, D = h.shape
    nt = S // tm

    def body(h_ref, g_ref, w_ref, ut_ref, xp_ref, q_ref, kv_ref, gl_ref):
        x = h_ref[...]
        uv = x * _rinv(x) * g_ref[...]
        ut_ref[...] = uv.T.astype(BF16)
        u = uv.astype(BF16)
        for (off, width), ref in zip(SEGMENTS, (xp_ref, q_ref, kv_ref, gl_ref)):
            ref[...] = _dot_nt(u, w_ref[off:off + width, :]).astype(ref.dtype)

    def row(width):
        return pl.BlockSpec((tm, width), lambda i: (i, 0))

    return pl.pallas_call(
        body, name="mix_proj", grid=(nt,),
        in_specs=[row(D), pl.BlockSpec((1, D), lambda i: (0, 0)), pl.BlockSpec(wint.shape, lambda i: (0, 0))],
        out_specs=[pl.BlockSpec((D, tm), lambda i: (0, i)), row(SEG_POOL[1]), row(SEG_Q[1]), row(SEG_KV[1]), row(SEG_GATE[1])],
        out_shape=[jax.ShapeDtypeStruct((D, S), BF16), jax.ShapeDtypeStruct((S, SEG_POOL[1]), F32),
                   jax.ShapeDtypeStruct((S, SEG_Q[1]), BF16), jax.ShapeDtypeStruct((S, SEG_KV[1]), BF16),
                   jax.ShapeDtypeStruct((S, SEG_GATE[1]), BF16)],
        compiler_params=_params(dimension_semantics=("arbitrary",)),
    )(h, gnorm, wint)


def _stack_heads(x, g):
    return jnp.concatenate([x[:, (GQA_GROUP * g + hh) * HEAD_DIM:(GQA_GROUP * g + hh + 1) * HEAD_DIM]
                            for hh in range(GQA_GROUP)], axis=0)


def _unstack_heads(ref, val, g):
    for hh in range(GQA_GROUP):
        lo = (GQA_GROUP * g + hh) * HEAD_DIM
        ref[:, lo:lo + HEAD_DIM] = val[hh * ATTN_BLOCK:(hh + 1) * ATTN_BLOCK, :]


def _attn_probs(qn, kc, kp, sink, n):
    rows = GQA_GROUP * ATTN_BLOCK
    qi = lax.broadcasted_iota(jnp.int32, (rows, ATTN_BLOCK), 0) % ATTN_BLOCK
    kj = lax.broadcasted_iota(jnp.int32, (rows, ATTN_BLOCK), 1)
    s_c = jnp.where(kj <= qi, _dot_nt(qn, kc), NEG)
    s_p = jnp.where(jnp.logical_and(kj > qi, n > 0), _dot_nt(qn, kp), NEG)
    m = jnp.maximum(jnp.maximum(jnp.max(s_c, axis=-1, keepdims=True), jnp.max(s_p, axis=-1, keepdims=True)), sink)
    e_c = jnp.exp(s_c - m)
    e_p = jnp.exp(s_p - m)
    e_s = jnp.exp(sink - m)
    inv = 1.0 / (jnp.sum(e_c, axis=-1, keepdims=True) + jnp.sum(e_p, axis=-1, keepdims=True) + e_s)
    return e_c * inv, e_p * inv, e_s * inv


def _attn_blocks(n):
    cur = pl.multiple_of(n * ATTN_BLOCK, ATTN_BLOCK)
    prev = pl.multiple_of(jnp.maximum(n - 1, 0) * ATTN_BLOCK, ATTN_BLOCK)
    return cur, prev


def _kv_split(kv, g):
    k = kv[:, g * HEAD_DIM:(g + 1) * HEAD_DIM]
    v = kv[:, (N_KV_HEADS + g) * HEAD_DIM:(N_KV_HEADS + g + 1) * HEAD_DIM]
    return k, v


def _attn_fwd(q, kv, qw, kw, sink_rows):
    S, W = q.shape
    nb = S // ATTN_BLOCK

    def body(q_ref, kv_ref, qw_ref, kw_ref, sk_ref, o_ref, o_scr):
        n = pl.program_id(0)
        qf = q_ref[...].astype(F32)
        cur, prev = _attn_blocks(n)
        kvc = kv_ref[pl.ds(cur, ATTN_BLOCK), :].astype(F32)
        kvp = kv_ref[pl.ds(prev, ATTN_BLOCK), :].astype(F32)
        for g in range(N_KV_HEADS):
            qs = _stack_heads(qf, g)
            qn = (qs * _rinv(qs) * qw_ref[...] * SCALE).astype(BF16)
            kc, vc = _kv_split(kvc, g)
            kp, vp = _kv_split(kvp, g)
            kc = (kc * _rinv(kc) * kw_ref[...]).astype(BF16)
            kp = (kp * _rinv(kp) * kw_ref[...]).astype(BF16)
            p_c, p_p, _ = _attn_probs(qn, kc, kp, sk_ref[g], n)
            o = _dot(p_c.astype(BF16), vc.astype(BF16)) + _dot(p_p.astype(BF16), vp.astype(BF16))
            _unstack_heads(o_scr, o, g)
        o_ref[...] = o_scr[...].astype(BF16)

    blk = pl.BlockSpec((ATTN_BLOCK, W), lambda n: (n, 0))
    return pl.pallas_call(
        body, name="attn_fwd", grid=(nb,),
        in_specs=[blk, pl.BlockSpec(kv.shape, lambda n: (0, 0)), pl.BlockSpec((1, HEAD_DIM), lambda n: (0, 0)),
                  pl.BlockSpec((1, HEAD_DIM), lambda n: (0, 0)), pl.BlockSpec(sink_rows.shape, lambda n: (0, 0, 0))],
        out_specs=blk, out_shape=jax.ShapeDtypeStruct((S, W), BF16),
        scratch_shapes=[pltpu.VMEM((ATTN_BLOCK, W), F32)],
        compiler_params=_params(dimension_semantics=("arbitrary",)),
    )(q, kv, qw, kw, sink_rows)


def _attn_bwd(q, kv, do, qw, kw, sink_rows):
    S, W = q.shape
    KW = kv.shape[1]
    nb = S // ATTN_BLOCK
    chunk = 512

    def body(q_ref, kv_ref, do_ref, qw_ref, kw_ref, sk_ref, dq_ref, dkv_ref, dqw_ref, dkw_ref, dsk_ref, dq_scr):
        n = pl.program_id(0)

        @pl.when(n == 0)
        def _():
            dkv_ref[...] = jnp.zeros_like(dkv_ref)
            dqw_ref[...] = jnp.zeros_like(dqw_ref)
            dsk_ref[...] = jnp.zeros_like(dsk_ref)

        qf = q_ref[...].astype(F32)
        dof = do_ref[...].astype(F32)
        cur, prev = _attn_blocks(n)
        kvc = kv_ref[pl.ds(cur, ATTN_BLOCK), :].astype(F32)
        kvp = kv_ref[pl.ds(prev, ATTN_BLOCK), :].astype(F32)
        qw_v = qw_ref[...]
        for g in range(N_KV_HEADS):
            qs = _stack_heads(qf, g)
            qhat = qs * _rinv(qs)
            qn = (qhat * qw_v * SCALE).astype(BF16)
            kc, vc = _kv_split(kvc, g)
            kp, vp = _kv_split(kvp, g)
            kc = (kc * _rinv(kc) * kw_ref[...]).astype(BF16)
            kp = (kp * _rinv(kp) * kw_ref[...]).astype(BF16)
            vc = vc.astype(BF16)
            vp = vp.astype(BF16)
            dos = _stack_heads(dof, g).astype(BF16)
            p_c, p_p, p_s = _attn_probs(qn, kc, kp, sk_ref[g], n)
            dp_c = _dot_nt(dos, vc)
            dp_p = _dot_nt(dos, vp)
            drow = jnp.sum(p_c * dp_c, axis=-1, keepdims=True) + jnp.sum(p_p * dp_p, axis=-1, keepdims=True)
            ds_c = (p_c * (dp_c - drow)).astype(BF16)
            ds_p = (p_p * (dp_p - drow)).astype(BF16)
            dsink = -(p_s * drow)
            for hh in range(GQA_GROUP):
                tot = jnp.sum(dsink[hh * ATTN_BLOCK:(hh + 1) * ATTN_BLOCK, :], axis=0, keepdims=True)
                dsk_ref[g, hh:hh + 1, :] += jnp.broadcast_to(tot, (1, 128))
            dqn = (_dot(ds_c, kc) + _dot(ds_p, kp)) * SCALE
            klo, vlo = g * HEAD_DIM, (N_KV_HEADS + g) * HEAD_DIM
            dkv_ref[pl.ds(cur, ATTN_BLOCK), klo:klo + HEAD_DIM] += _dot_tn(ds_c, qn)
            dkv_ref[pl.ds(prev, ATTN_BLOCK), klo:klo + HEAD_DIM] += _dot_tn(ds_p, qn)
            dkv_ref[pl.ds(cur, ATTN_BLOCK), vlo:vlo + HEAD_DIM] += _dot_tn(p_c.astype(BF16), dos)
            dkv_ref[pl.ds(prev, ATTN_BLOCK), vlo:vlo + HEAD_DIM] += _dot_tn(p_p.astype(BF16), dos)
            dqw_ref[...] += jnp.sum(dqn * qhat, axis=0, keepdims=True)
            z = dqn * qw_v
            dqs = _rinv(qs) * (z - qhat * jnp.mean(z * qhat, axis=-1, keepdims=True))
            _unstack_heads(dq_scr, dqs, g)
        dq_ref[...] = dq_scr[...].astype(BF16)

        @pl.when(n == nb - 1)
        def _():
            def one(c, dkw):
                rows = pl.ds(pl.multiple_of(c * chunk, chunk), chunk)
                for g in range(N_KV_HEADS):
                    lo = g * HEAD_DIM
                    k = kv_ref[rows, lo:lo + HEAD_DIM].astype(F32)
                    dx, dg_rows = _rms_bwd(dkv_ref[rows, lo:lo + HEAD_DIM], k, kw_ref[...])
                    dkv_ref[rows, lo:lo + HEAD_DIM] = dx
                    dkw = dkw + jnp.sum(dg_rows, axis=0, keepdims=True)
                return dkw

            dkw_ref[...] = lax.fori_loop(0, S // chunk, one, jnp.zeros((1, HEAD_DIM), F32))

    blk = pl.BlockSpec((ATTN_BLOCK, W), lambda n: (n, 0))
    whole_kv = pl.BlockSpec((S, KW), lambda n: (0, 0))
    vec = pl.BlockSpec((1, HEAD_DIM), lambda n: (0, 0))
    sk = pl.BlockSpec(sink_rows.shape, lambda n: (0, 0, 0))
    dsk = pl.BlockSpec((N_KV_HEADS, GQA_GROUP, 128), lambda n: (0, 0, 0))
    return pl.pallas_call(
        body, name="attn_bwd", grid=(nb,), in_specs=[blk, whole_kv, blk, vec, vec, sk],
        out_specs=[blk, whole_kv, vec, vec, dsk],
        out_shape=[jax.ShapeDtypeStruct((S, W), BF16), jax.ShapeDtypeStruct((S, KW), F32),
                   jax.ShapeDtypeStruct((1, HEAD_DIM), F32), jax.ShapeDtypeStruct((1, HEAD_DIM), F32),
                   jax.ShapeDtypeStruct((N_KV_HEADS, GQA_GROUP, 128), F32)],
        scratch_shapes=[pltpu.VMEM((ATTN_BLOCK, W), F32)],
        compiler_params=_params(dimension_semantics=("arbitrary",)),
    )(q, kv, do, qw, kw, sink_rows)


def _pooled(xc, xprev, i):
    tm = xc.shape[0]
    xh = jnp.concatenate([jnp.where(i > 0, xprev, 0.0), xc], axis=0)
    t = lax.broadcasted_iota(jnp.int32, (tm, 1), 0) + i * tm
    out = []
    for gi, w in enumerate(POOL_WINDOWS):
        acc = xh[:, gi * POOL_GROUP:(gi + 1) * POOL_GROUP]
        sh = 1
        while sh < w:
            acc = acc + pltpu.roll(acc, sh, 0)
            sh *= 2
        cnt = jnp.minimum(t + 1, w).astype(F32)
        out.append(acc[POOL_HALO:, :] / cnt - xc[:, gi * POOL_GROUP:(gi + 1) * POOL_GROUP])
    return jnp.concatenate(out, axis=1)


def _pool_mix(pooled_b, pw_ref):
    return jnp.concatenate([_dot(pooled_b[:, gi * POOL_GROUP:(gi + 1) * POOL_GROUP], pw_ref[gi])
                            for gi in range(len(POOL_WINDOWS))], axis=1)


def _halo_specs(tm, width, S, after):
    per = tm // POOL_HALO
    last = S // POOL_HALO - 1
    if after:
        return pl.BlockSpec((POOL_HALO, width), lambda i: (jnp.minimum((i + 1) * per, last), 0))
    return pl.BlockSpec((POOL_HALO, width), lambda i: (jnp.maximum(i * per - 1, 0), 0))


def _mix_out(xp, attn, gl, bias, pw, pscale, wpot, wao, wo, h, tm=512):
    S, D = h.shape
    nt = S // tm
    PW = xp.shape[1]

    def body(xc_ref, xprev_ref, at_ref, gl_ref, bias_ref, pw_ref, ps_ref, wpo_ref, wao_ref, wo_ref, h_ref,
             ho_ref, bp_ref, ba_ref):
        i = pl.program_id(0)
        pooled = _pooled(xc_ref[...], xprev_ref[...], i).astype(BF16)
        ms = (_pool_mix(pooled, pw_ref) * ps_ref[...]).astype(BF16)
        bp = _dot_nt(ms, wpo_ref[...])
        ba = _dot(at_ref[...], wao_ref[...])
        bp_ref[...] = bp.astype(BF16)
        ba_ref[...] = ba.astype(BF16)
        gates = jax.nn.sigmoid(gl_ref[...].astype(F32) + bias_ref[...])
        merged = (gates[:, :D] * bp + gates[:, D:] * ba).astype(BF16)
        ho_ref[...] = h_ref[...] + _dot(merged, wo_ref[...])

    def row(width):
        return pl.BlockSpec((tm, width), lambda i: (i, 0))

    def whole(x):
        nd = x.ndim
        return pl.BlockSpec(x.shape, lambda i: (0,) * nd)

    return pl.pallas_call(
        body, name="mix_out", grid=(nt,),
        in_specs=[row(PW), _halo_specs(tm, PW, S, False), row(D), row(2 * D), whole(bias), whole(pw), whole(pscale),
                  whole(wpot), whole(wao), whole(wo), row(D)],
        out_specs=[row(D), row(D), row(D)],
        out_shape=[jax.ShapeDtypeStruct((S, D), F32), jax.ShapeDtypeStruct((S, D), BF16),
                   jax.ShapeDtypeStruct((S, D), BF16)],
        compiler_params=_params(dimension_semantics=("arbitrary",)),
    )(xp, xp, attn, gl, bias, pw, pscale, wpot, wao, wo, h)


def _mix_bwd_gate(dh, wo, bp, ba, gl, bias, tm=512):
    S, D = dh.shape
    nt = S // tm

    def body(dh_ref, wo_ref, bp_ref, ba_ref, gl_ref, bias_ref, dgl_ref, dbp_ref, dba_ref, dwo_ref, dbias_ref):
        i = pl.program_id(0)
        dhb = dh_ref[...].astype(BF16)
        dm = _dot_nt(dhb, wo_ref[...])
        gates = jax.nn.sigmoid(gl_ref[...].astype(F32) + bias_ref[...])
        gp, ga = gates[:, :D], gates[:, D:]
        bp_v = bp_ref[...].astype(F32)
        ba_v = ba_ref[...].astype(F32)
        merged = (gp * bp_v + ga * ba_v).astype(BF16)
        _acc(dwo_ref, _dot_tn(merged, dhb), i == 0)
        dbp_ref[...] = (dm * gp).astype(BF16)
        dba_ref[...] = (dm * ga).astype(BF16)
        dgl = jnp.concatenate([dm * bp_v * gp * (1.0 - gp), dm * ba_v * ga * (1.0 - ga)], axis=1)
        dgl_ref[...] = dgl.astype(BF16)
        _acc(dbias_ref, jnp.sum(dgl, axis=0, keepdims=True), i == 0)

    def row(width):
        return pl.BlockSpec((tm, width), lambda i: (i, 0))

    def whole(shape):
        return pl.BlockSpec(shape, lambda i: (0, 0))

    return pl.pallas_call(
        body, name="mix_bwd_gate", grid=(nt,),
        in_specs=[row(D), whole(wo.shape), row(D), row(D), row(2 * D), whole(bias.shape)],
        out_specs=[row(2 * D), row(D), row(D), whole((D, D)), whole((1, 2 * D))],
        out_shape=[jax.ShapeDtypeStruct((S, 2 * D), BF16), jax.ShapeDtypeStruct((S, D), BF16),
                   jax.ShapeDtypeStruct((S, D), BF16), jax.ShapeDtypeStruct((D, D), F32),
                   jax.ShapeDtypeStruct((1, 2 * D), F32)],
        compiler_params=_params(dimension_semantics=("arbitrary",)),
    )(dh, wo, bp, ba, gl, bias)


def _mix_bwd_branch(dbp, dba, attn, xp, pw, pscale, wpot, wao, tm=512):
    S, D = dbp.shape
    nt = S // tm
    PW = xp.shape[1]
    NG = len(POOL_WINDOWS)

    def body(dbp_ref, dba_ref, at_ref, xc_ref, xprev_ref, pw_ref, ps_ref, wpo_ref, wao_ref,
             dat_ref, dpl_ref, dwao_ref, dwpo_ref, dpw_ref, dps_ref):
        i = pl.program_id(0)
        dba_v = dba_ref[...]
        dbp_v = dbp_ref[...]
        _acc(dwao_ref, _dot_tn(at_ref[...], dba_v), i == 0)
        dat_ref[...] = _dot_nt(dba_v, wao_ref[...]).astype(BF16)
        pooled = _pooled(xc_ref[...], xprev_ref[...], i).astype(BF16)
        mixed = _pool_mix(pooled, pw_ref)
        ps = ps_ref[...]
        _acc(dwpo_ref, _dot_tn(dbp_v, (mixed * ps).astype(BF16)), i == 0)
        dms = _dot(dbp_v, wpo_ref[...])
        _acc(dps_ref, jnp.sum(dms * mixed, axis=0, keepdims=True), i == 0)
        dmixed = (dms * ps).astype(BF16)
        dpooled = []
        for gi in range(NG):
            cols = slice(gi * POOL_GROUP, (gi + 1) * POOL_GROUP)
            _acc(dpw_ref.at[gi], _dot_tn(pooled[:, cols], dmixed[:, cols]), i == 0)
            dpooled.append(_dot_nt(dmixed[:, cols], pw_ref[gi]))
        dpl_ref[...] = jnp.concatenate(dpooled, axis=1)

    def row(width):
        return pl.BlockSpec((tm, width), lambda i: (i, 0))

    def whole(shape):
        nd = len(shape)
        return pl.BlockSpec(shape, lambda i: (0,) * nd)

    return pl.pallas_call(
        body, name="mix_bwd_branch", grid=(nt,),
        in_specs=[row(D), row(D), row(D), row(PW), _halo_specs(tm, PW, S, False), whole(pw.shape), whole(pscale.shape),
                  whole(wpot.shape), whole(wao.shape)],
        out_specs=[row(D), row(PW), whole((D, D)), whole((D, PW)), whole(pw.shape), whole((1, PW))],
        out_shape=[jax.ShapeDtypeStruct((S, D), BF16), jax.ShapeDtypeStruct((S, PW), F32),
                   jax.ShapeDtypeStruct((D, D), F32), jax.ShapeDtypeStruct((D, PW), F32),
                   jax.ShapeDtypeStruct(pw.shape, F32), jax.ShapeDtypeStruct((1, PW), F32)],
        compiler_params=_params(dimension_semantics=("arbitrary",)),
    )(dbp, dba, attn, xp, xp, pw, pscale, wpot, wao)


def _pool_bwd(dpooled, tm=512):
    S, PW = dpooled.shape
    nt = S // tm

    def body(dc_ref, dnext_ref, dxp_ref):
        i = pl.program_id(0)
        dc = dc_ref[...]
        dh = jnp.concatenate([dc, jnp.where(i < nt - 1, dnext_ref[...], 0.0)], axis=0)
        rows = tm + POOL_HALO
        t = lax.broadcasted_iota(jnp.int32, (rows, 1), 0) + i * tm
        out = []
        for gi, w in enumerate(POOL_WINDOWS):
            cols = slice(gi * POOL_GROUP, (gi + 1) * POOL_GROUP)
            acc = dh[:, cols] / jnp.minimum(t + 1, w).astype(F32)
            sh = 1
            while sh < w:
                acc = acc + pltpu.roll(acc, rows - sh, 0)
                sh *= 2
            out.append(acc[:tm, :] - dc[:, cols])
        dxp_ref[...] = jnp.concatenate(out, axis=1).astype(BF16)

    return pl.pallas_call(
        body, name="pool_bwd", grid=(nt,),
        in_specs=[pl.BlockSpec((tm, PW), lambda i: (i, 0)), _halo_specs(tm, PW, S, True)],
        out_specs=pl.BlockSpec((tm, PW), lambda i: (i, 0)), out_shape=jax.ShapeDtypeStruct((S, PW), BF16),
        compiler_params=_params(dimension_semantics=("arbitrary",)),
    )(dpooled, dpooled)


def _in_bwd_w(pieces, ut, tm=1024, chunk=256):
    D, S = ut.shape
    nt = S // tm
    NW = sum(width for _, width in SEGMENTS)
    nchunk = NW // chunk

    def body(*refs):
        piece_refs, ut_ref, dw_hbm, acc, stage, sem = (refs[:len(SEGMENTS)], refs[len(SEGMENTS)], refs[len(SEGMENTS) + 1],
                                                      refs[-3], refs[-2], refs[-1])
        i = pl.program_id(0)
        u_t = ut_ref[...]
        for (off, width), ref in zip(SEGMENTS, piece_refs):
            for lo in range(0, width, D):
                hi = min(lo + D, width)
                _acc(acc.at[:, off + lo:off + hi], _dot(u_t, ref[:, lo:hi].astype(BF16)), i == 0)

        @pl.when(i == nt - 1)
        def _():
            def out_copy(c, slot):
                return pltpu.make_async_copy(stage.at[slot], dw_hbm.at[pl.ds(c * chunk, chunk)], sem.at[slot])

            for c in range(nchunk):
                slot = c % 2
                if c >= 2:
                    out_copy(c - 2, slot).wait()
                stage[slot] = acc[:, c * chunk:(c + 1) * chunk].T
                out_copy(c, slot).start()
            for c in range(nchunk - 2, nchunk):
                out_copy(c, c % 2).wait()

    return pl.pallas_call(
        body, name="in_bwd_w", grid=(nt,),
        in_specs=[pl.BlockSpec((tm, width), lambda i: (i, 0)) for _, width in SEGMENTS] + [pl.BlockSpec((D, tm), lambda i: (0, i))],
        out_specs=pl.BlockSpec(memory_space=pl.ANY), out_shape=jax.ShapeDtypeStruct((NW, D), F32),
        scratch_shapes=[pltpu.VMEM((D, NW), F32), pltpu.VMEM((2, chunk, D), F32), pltpu.SemaphoreType.DMA((2,))],
        compiler_params=_params(dimension_semantics=("arbitrary",)),
    )(*pieces, ut)


def _in_bwd_x(pieces, wint, h, dh, gnorm, tm=512):
    S, D = h.shape
    nt = S // tm

    def body(*refs):
        piece_refs = refs[:len(SEGMENTS)]
        w_ref, h_ref, dh_ref, g_ref, dx_ref, dxh_ref, dxht_ref, dg_ref = refs[len(SEGMENTS):]
        i = pl.program_id(0)
        du = jnp.zeros((tm, D), F32)
        for (off, width), ref in zip(SEGMENTS, piece_refs):
            du = du + _dot(ref[...].astype(BF16), w_ref[off:off + width, :])
        dx, dg_rows = _rms_bwd(du, h_ref[...], g_ref[...])
        out = dh_ref[...] + dx
        dx_ref[...] = out
        dxh_ref[...] = (0.5 * out).astype(BF16)
        dxht_ref[...] = (0.5 * out).T.astype(BF16)
        _acc(dg_ref, jnp.sum(dg_rows, axis=0, keepdims=True), i == 0)

    def row(width):
        return pl.BlockSpec((tm, width), lambda i: (i, 0))

    vec = pl.BlockSpec((1, D), lambda i: (0, 0))
    return pl.pallas_call(
        body, name="in_bwd_x", grid=(nt,),
        in_specs=[row(width) for _, width in SEGMENTS] + [pl.BlockSpec(wint.shape, lambda i: (0, 0)), row(D), row(D), vec],
        out_specs=[row(D), row(D), pl.BlockSpec((D, tm), lambda i: (0, i)), vec],
        out_shape=[jax.ShapeDtypeStruct((S, D), F32), jax.ShapeDtypeStruct((S, D), BF16),
                   jax.ShapeDtypeStruct((D, S), BF16), jax.ShapeDtypeStruct((1, D), F32)],
        compiler_params=_params(dimension_semantics=("arbitrary",)),
    )(*pieces, wint, h, dh, gnorm)


def _row_tile(rows):
    for t in (512, 480, 352, 256, 128, 64, 32, 16, 8):
        if rows % t == 0:
            return t
    return rows


def _adamw(w, g, m, v, token, name):
    R, C = w.shape
    tr = _row_tile(R)

    def body(w_ref, g_ref, m_ref, v_ref, _, d_ref, mo_ref, vo_ref, go_ref):
        gv = g_ref[...]
        go_ref[...] = gv
        mn = ADAM_B1 * m_ref[...] + (1.0 - ADAM_B1) * gv
        vn = ADAM_B2 * v_ref[...] + (1.0 - ADAM_B2) * (gv * gv)
        m_hat = mn / (1.0 - ADAM_B1 ** ADAM_STEP)
        v_hat = vn / (1.0 - ADAM_B2 ** ADAM_STEP)
        d_ref[...] = -ADAM_LR * (m_hat / (jnp.sqrt(v_hat) + ADAM_EPS) + ADAM_WD * w_ref[...])
        mo_ref[...] = mn
        vo_ref[...] = vn

    blk = pl.BlockSpec((tr, C), lambda i: (i, 0))
    sh = jax.ShapeDtypeStruct((R, C), F32)
    return pl.pallas_call(
        body, name=name, grid=(R // tr,), in_specs=[blk] * 4 + [TOKEN_SPEC], out_specs=[blk] * 4, out_shape=[sh] * 4,
        compiler_params=_params(dimension_semantics=("arbitrary",)),
    )(w, g, m, v, token)


def _cast_place(w, place, name):
    R, C = w.shape
    tr = _row_tile(R)
    per = R // tr

    def body(p_ref, w_ref, o_ref):
        o_ref[...] = w_ref[...].astype(BF16)

    grid_spec = pltpu.PrefetchScalarGridSpec(
        num_scalar_prefetch=1, grid=(per,),
        in_specs=[pl.BlockSpec((tr, C), lambda i, p: (i, 0))],
        out_specs=pl.BlockSpec((tr, C), lambda i, p: (p[0] * per + i, 0)))
    return pl.pallas_call(
        body, name=name, grid_spec=grid_spec, out_shape=jax.ShapeDtypeStruct((N_SHARDS * R, C), BF16),
        compiler_params=_params(dimension_semantics=("arbitrary",)),
    )(place, w)


def _sum_halves(g4, recv, place, name):
    NS, R, C = g4.shape
    hr = R // 2
    tr = _row_tile(hr)
    per = hr // tr

    def body(p_ref, g_ref, r_ref, o_ref):
        o_ref[...] = (g_ref[...] + r_ref[...]).astype(BF16)

    grid_spec = pltpu.PrefetchScalarGridSpec(
        num_scalar_prefetch=1, grid=(NS, per),
        in_specs=[pl.BlockSpec((1, tr, C), lambda s, i, p: (s, p[1] * per + i, 0)),
                  pl.BlockSpec((1, tr, C), lambda s, i, p: (s, i, 0))],
        out_specs=pl.BlockSpec((1, tr, C), lambda s, i, p: (s, i, 0)))
    return pl.pallas_call(
        body, name=name, grid_spec=grid_spec, out_shape=jax.ShapeDtypeStruct((NS, hr, C), BF16),
        compiler_params=_params(dimension_semantics=("arbitrary", "arbitrary")),
    )(place, g4, recv)


def _sum_quarters(h4, recv3, place, name):
    NS, hr, C = h4.shape
    tr = _row_tile(hr)
    per = hr // tr

    def body(p_ref, h_ref, r_ref, o_ref):
        acc = h_ref[0].astype(F32)
        for k in range(N_SHARDS - 1):
            acc = acc + r_ref[k].astype(F32)
        o_ref[...] = acc

    grid_spec = pltpu.PrefetchScalarGridSpec(
        num_scalar_prefetch=1, grid=(per,),
        in_specs=[pl.BlockSpec((1, tr, C), lambda i, p: (p[0], i, 0)),
                  pl.BlockSpec((N_SHARDS - 1, tr, C), lambda i, p: (0, i, 0))],
        out_specs=pl.BlockSpec((tr, C), lambda i, p: (p[1] * per + i, 0)))
    return pl.pallas_call(
        body, name=name, grid_spec=grid_spec, out_shape=jax.ShapeDtypeStruct((2 * hr, C), F32),
        compiler_params=_params(dimension_semantics=("arbitrary",)),
    )(place, h4, recv3)


def _place():
    x, y, c = lax.axis_index("x"), lax.axis_index("y"), lax.axis_index("c")
    chips = [(1 - x, y), (x, 1 - y), (1 - x, 1 - y)]
    return x, y, c, chips


HBM_SPEC = pl.BlockSpec(memory_space=pltpu.HBM)
SEM_SPEC = pl.BlockSpec(memory_space=pltpu.SEMAPHORE)
DATAFLOW = pltpu.SideEffectType.DATAFLOW_SIDE_EFFECTING


def _hbm(a):
    return pltpu.with_memory_space_constraint(a, pltpu.HBM)


class InFlight(NamedTuple):
    send_sem: jax.Array
    recv_sem: jax.Array
    bufs: list
    plan: Callable
    token: jax.Array


def _wait_all(plan, refs, send_ref, recv_ref):
    for k, (src, dst, dev) in enumerate(plan(refs)):
        cp = pltpu.make_async_remote_copy(src_ref=src, dst_ref=dst, send_sem=send_ref.at[k], recv_sem=recv_ref.at[k],
                                          device_id=dev, device_id_type=MESH)
        cp.wait_send()
        cp.wait_recv()


def _start_all(plan, refs, send_ref, recv_ref):
    for k, (src, dst, dev) in enumerate(plan(refs)):
        pltpu.make_async_remote_copy(src_ref=src, dst_ref=dst, send_sem=send_ref.at[k], recv_sem=recv_ref.at[k],
                                     device_id=dev, device_id_type=MESH).start()


def _split_start(name, bufs, plan, n_copies, after, waits=None):
    nb = len(bufs)

    def body(*refs):
        ins = refs[:nb]
        if waits is not None:
            _wait_all(waits.plan, ins, refs[nb], refs[nb + 1])
        send_sem, recv_sem, token = refs[-nb - 3], refs[-nb - 2], refs[-1]
        _start_all(plan, ins, send_sem, recv_sem)
        token[...] = jnp.zeros_like(token)

    sems = pltpu.SemaphoreType.DMA((n_copies,))
    earlier = [] if waits is None else [waits.send_sem, waits.recv_sem]
    out = pl.pallas_call(
        body, name=name, in_specs=[HBM_SPEC] * nb + [SEM_SPEC] * len(earlier) + [pl.BlockSpec(memory_space=pl.ANY)],
        out_shape=(sems, sems, *[pltpu.HBM(b.shape, b.dtype) for b in bufs], jax.ShapeDtypeStruct((8, 128), F32)),
        out_specs=(SEM_SPEC, SEM_SPEC, *[HBM_SPEC] * nb, pl.BlockSpec(memory_space=pltpu.VMEM)),
        input_output_aliases={i: i + 2 for i in range(nb)},
        compiler_params=pltpu.CompilerParams(has_side_effects=DATAFLOW),
    )(*[_hbm(b) for b in bufs], *earlier, after)
    return InFlight(out[0], out[1], list(out[2:2 + nb]), plan, out[-1])


def _split_wait(name, flight, after):
    nb = len(flight.bufs)

    def body(*refs):
        _wait_all(flight.plan, refs[:nb], refs[nb], refs[nb + 1])

    out = pl.pallas_call(
        body, name=name, in_specs=[HBM_SPEC] * nb + [SEM_SPEC, SEM_SPEC, pl.BlockSpec(memory_space=pl.ANY)],
        out_shape=[pltpu.HBM(b.shape, b.dtype) for b in flight.bufs], out_specs=[HBM_SPEC] * nb,
        input_output_aliases={i: i for i in range(nb)},
        compiler_params=pltpu.CompilerParams(has_side_effects=DATAFLOW),
    )(*flight.bufs, flight.send_sem, flight.recv_sem, after)
    return list(out)


def _half_rows(buf, chip, core):
    hr = buf.shape[0] // (2 * N_SHARDS)
    return buf.at[pl.ds(pl.multiple_of((2 * chip + core) * hr, 16), hr)]


def _gather_ici_plan(bufs):
    x, y, c, chips = _place()
    return [(_half_rows(b, 2 * x + y, c), _half_rows(b, 2 * x + y, c), (px, py, c)) for b in bufs for px, py in chips]


def _gather_d2d_plan(bufs):
    x, y, c, chips = _place()
    return [(_half_rows(b, 2 * px + py, c), _half_rows(b, 2 * px + py, c), (x, y, 1 - c)) for b in bufs for px, py in chips]


def _swap_plan(bufs):
    x, y, c, _ = _place()
    n = len(bufs) // 2
    copies = []
    for g, land in zip(bufs[:n], bufs[n:]):
        hr = g.shape[1] // 2
        copies.append((g.at[:, pl.ds(pl.multiple_of((1 - c) * hr, 8), hr)], land, (x, y, 1 - c)))
    return copies


def _exchange_plan(bufs):
    x, y, c, chips = _place()
    n = len(bufs) // 2
    return [(h.at[2 * px + py], land.at[k], (px, py, c))
            for h, land in zip(bufs[:n], bufs[n:]) for k, (px, py) in enumerate(chips)]


def _share_plan(bufs):
    x, y, c, _ = _place()
    copies = []
    for buf in bufs:
        hr = buf.shape[0] // 2
        mine = buf.at[pl.ds(pl.multiple_of(c * hr, 8), hr)]
        copies.append((mine, mine, (x, y, 1 - c)))
    return copies


N_DEVICES = 8


def _slot_place(vec, place):
    R, C = vec.shape

    def body(p_ref, v_ref, o_ref):
        o_ref[0] = v_ref[...]

    grid_spec = pltpu.PrefetchScalarGridSpec(
        num_scalar_prefetch=1, grid=(1,), in_specs=[pl.BlockSpec((R, C), lambda i, p: (0, 0))],
        out_specs=pl.BlockSpec((1, R, C), lambda i, p: (2 * p[0] + p[1], 0, 0)))
    return pl.pallas_call(
        body, name="slot_place", grid_spec=grid_spec, out_shape=jax.ShapeDtypeStruct((N_DEVICES, R, C), F32),
        compiler_params=_params(dimension_semantics=("arbitrary",)),
    )(place, vec)


def _slots_plan(bufs):
    x, y, c, _ = _place()
    mine = bufs[0].at[4 * x + 2 * y + c]
    return [(mine, mine, (x ^ (r >> 2), y ^ ((r >> 1) & 1), c ^ (r & 1))) for r in range(1, N_DEVICES)]


def _sum_slots(slots):
    _, R, C = slots.shape

    def body(s_ref, o_ref):
        acc = s_ref[0]
        for d in range(1, N_DEVICES):
            acc = acc + s_ref[d]
        o_ref[...] = acc

    return pl.pallas_call(
        body, name="sum_slots", in_specs=[pl.BlockSpec(memory_space=pltpu.VMEM)],
        out_specs=pl.BlockSpec(memory_space=pltpu.VMEM), out_shape=jax.ShapeDtypeStruct((R, C), F32),
        compiler_params=_params(),
    )(slots)


SMALL = ("ffn1_norm", "mix_norm", "pool_w", "pool_scale", "q_norm", "k_norm", "sinks", "gate_bias", "ffn2_norm")
SMALL_COLS = 1024
FFN1 = ("ffn1_w_gate", "ffn1_w_up", "ffn1_w_down")
MIXER = ("w_in", "w_pool_out", "w_attn_out", "w_out")
FFN2 = ("ffn2_w_gate", "ffn2_w_up", "ffn2_w_down")
LARGE = FFN1 + MIXER + FFN2
TRANSPOSED = ("ffn1_w_gate", "ffn1_w_up", "w_in", "w_pool_out", "ffn2_w_gate", "ffn2_w_up")
WEIGHTS = ("ffn1_norm", "ffn1_w_gate", "ffn1_w_up", "ffn1_w_down", "mix_norm", "w_in", "pool_w", "pool_scale",
           "w_pool_out", "q_norm", "k_norm", "sinks", "w_attn_out", "gate_bias", "w_out", "ffn2_norm",
           "ffn2_w_gate", "ffn2_w_up", "ffn2_w_down")


def _pack_small(parts):
    flat = jnp.concatenate([p.reshape(-1) for p in parts])
    rows = -(-flat.shape[0] // (8 * SMALL_COLS)) * 8
    return jnp.pad(flat, (0, rows * SMALL_COLS - flat.shape[0])).reshape(rows, SMALL_COLS)


def _unpack_small(packed, like):
    flat = packed.reshape(-1)
    out, off = [], 0
    for p in like:
        out.append(flat[off:off + p.size].reshape(p.shape))
        off += p.size
    return out, flat[off]


def _tie(small, token):
    return small + token[0, 0]


class Reduction:
    def __init__(self, group, names, grads, place):
        self.group, self.names, self.place = group, names, place
        self.bufs = [g.reshape(N_SHARDS, -1, g.shape[-1]) for g in grads]
        self.flight = None

    def _start(self, stage, bufs, plan, n_copies, after):
        self.flight = _split_start(f"{self.group}_{stage}", bufs, plan, n_copies, after)
        return self.flight.token

    def _landed(self, stage, after):
        n = len(self.names)
        bufs = _split_wait(f"{self.group}_{stage}_wait", self.flight, after)
        return bufs[:n], bufs[n:]

    def start_swap(self, after):
        lands = [lax.empty((g.shape[0], g.shape[1] // 2, g.shape[2]), g.dtype) for g in self.bufs]
        return self._start("swap", self.bufs + lands, _swap_plan, len(self.bufs), after)

    def start_exchange(self, after):
        g4, recv = self._landed("swap", after)
        halves = [_sum_halves(g, r, self.place, f"sum_halves_{n}") for n, g, r in zip(self.names, g4, recv)]
        lands = [lax.empty((N_SHARDS - 1,) + h.shape[1:], h.dtype) for h in halves]
        return self._start("exchange", halves + lands, _exchange_plan, (N_SHARDS - 1) * len(halves), self.place)

    def start_share(self, after):
        halves, recv3 = self._landed("exchange", after)
        reduced = [_sum_quarters(h, r, self.place, f"sum_quarters_{n}") for n, h, r in zip(self.names, halves, recv3)]
        return self._start("share", reduced, _share_plan, len(reduced), self.place)

    def finish(self, after):
        return dict(zip(self.names, _split_wait(f"{self.group}_share_wait", self.flight, after)))


def _row_form(name, a):
    return a.T if name in TRANSPOSED else a


def kernel(x, ffn1_norm, ffn1_w_gate, ffn1_w_up, ffn1_w_down, mix_norm, w_in, pool_w, pool_scale, w_pool_out, q_norm, k_norm, sinks, w_attn_out, gate_bias, w_out, ffn2_norm, ffn2_w_gate, ffn2_w_up, ffn2_w_down, loss_target, m_ffn1_norm, m_ffn1_w_gate, m_ffn1_w_up, m_ffn1_w_down, m_mix_norm, m_w_in, m_pool_w, m_pool_scale, m_w_pool_out, m_q_norm, m_k_norm, m_sinks, m_w_attn_out, m_gate_bias, m_w_out, m_ffn2_norm, m_ffn2_w_gate, m_ffn2_w_up, m_ffn2_w_down, v_ffn1_norm, v_ffn1_w_gate, v_ffn1_w_up, v_ffn1_w_down, v_mix_norm, v_w_in, v_pool_w, v_pool_scale, v_w_pool_out, v_q_norm, v_k_norm, v_sinks, v_w_attn_out, v_gate_bias, v_w_out, v_ffn2_norm, v_ffn2_w_gate, v_ffn2_w_up, v_ffn2_w_down):
    args = dict(locals())
    wts = {n: _row_form(n, args[n]) for n in WEIGHTS}
    mom = {n: _row_form(n, args["m_" + n]) for n in WEIGHTS}
    var = {n: _row_form(n, args["v_" + n]) for n in WEIGHTS}
    shard = 2 * lax.axis_index("x") + lax.axis_index("y")
    place = jnp.stack([shard, lax.axis_index("c")]).astype(jnp.int32)

    xs, target = x[0], loss_target[0]
    D = xs.shape[1]
    g1 = wts["ffn1_norm"].reshape(1, D)
    gm = wts["mix_norm"].reshape(1, D)
    g2 = wts["ffn2_norm"].reshape(1, D)
    qw = wts["q_norm"].reshape(1, HEAD_DIM)
    kw = wts["k_norm"].reshape(1, HEAD_DIM)
    bias = wts["gate_bias"].reshape(1, 2 * D)
    pscale = wts["pool_scale"].reshape(1, -1)
    pw = wts["pool_w"].astype(BF16)
    sink_rows = jnp.repeat(wts["sinks"], ATTN_BLOCK).reshape(N_KV_HEADS, GQA_GROUP * ATTN_BLOCK, 1)

    placed = {n: _cast_place(wts[n], place, "cast_" + n) for n in FFN1}
    ffn1_ici = _split_start("gather_ffn1_ici", [placed[n] for n in FFN1], _gather_ici_plan, 3 * len(FFN1), place)
    placed.update({n: _cast_place(wts[n], place, "cast_" + n) for n in MIXER + FFN2})
    ffn1_d2d = _split_start("gather_ffn1_d2d", ffn1_ici.bufs, _gather_d2d_plan, 3 * len(FFN1), placed[FFN2[-1]],
                            waits=ffn1_ici)
    w = dict(zip(FFN1, _split_wait("gather_ffn1_wait", ffn1_d2d, ffn1_d2d.token)))
    mix_ici = _split_start("gather_mix_ici", [placed[n] for n in MIXER], _gather_ici_plan, 3 * len(MIXER), w[FFN1[0]])
    ffn2_ici = _split_start("gather_ffn2_ici", [placed[n] for n in FFN2], _gather_ici_plan, 3 * len(FFN2), mix_ici.token)

    n1t, a1, b1, h1 = _ffn_fwd(xs, _tie(g1, ffn2_ici.token), w["ffn1_w_gate"], w["ffn1_w_up"], w["ffn1_w_down"], None, "ffn1_fwd")
    mix_d2d = _split_start("gather_mix_d2d", mix_ici.bufs, _gather_d2d_plan, 3 * len(MIXER), h1, waits=mix_ici)
    w.update(zip(MIXER, _split_wait("gather_mix_wait", mix_d2d, mix_d2d.token)))
    ut, xp, q, kv, gl = _mix_proj(h1, gm, w["w_in"])
    attn = _attn_fwd(q, kv, qw, kw, sink_rows)
    ffn2_d2d = _split_start("gather_ffn2_d2d", ffn2_ici.bufs, _gather_d2d_plan, 3 * len(FFN2), attn, waits=ffn2_ici)
    h2, bp, ba = _mix_out(xp, attn, gl, _tie(bias, ffn2_d2d.token), pw, pscale, w["w_pool_out"], w["w_attn_out"], w["w_out"], h1)
    w.update(zip(FFN2, _split_wait("gather_ffn2_wait", ffn2_d2d, h2)))
    n2t, a2, b2, dy, dyh, dyht, loss = _ffn_fwd(h2, g2, w["ffn2_w_gate"], w["ffn2_w_up"], w["ffn2_w_down"], target, "ffn2_fwd")

    gw, gs = {}, {}
    da, db, s2 = _ffn_bwd_h(dyh, a2, b2, w["ffn2_w_down"], ffn2_d2d.token, "ffn2_bwd_h")
    gw["ffn2_w_gate"], gw["ffn2_w_up"] = _xty(n2t, da, "ffn2_dw_gate"), _xty(n2t, db, "ffn2_dw_up")
    gw["ffn2_w_down"] = _xty(dyht, s2, "ffn2_dw_down")
    red2 = Reduction("reduce_ffn2", FFN2, [gw[n] for n in FFN2], place)
    token = red2.start_swap(da)
    dh2, gs["ffn2_norm"] = _ffn_bwd_x(da, db, w["ffn2_w_gate"], w["ffn2_w_up"], h2, dy, _tie(g2, token), "ffn2_bwd_x")
    token = red2.start_exchange(dh2)
    dgl, dbp, dba, gw["w_out"], gs["gate_bias"] = _mix_bwd_gate(dh2, w["w_out"], bp, ba, gl, _tie(bias, token))
    dattn, dpooled, gw["w_attn_out"], gw["w_pool_out"], gs["pool_w"], gs["pool_scale"] = _mix_bwd_branch(
        dbp, dba, attn, xp, pw, pscale, w["w_pool_out"], w["w_attn_out"])
    dq, dkv, gs["q_norm"], gs["k_norm"], dsk = _attn_bwd(q, kv, dattn, qw, kw, sink_rows)
    gs["sinks"] = dsk[:, :, 0]
    red2.start_share(dq)
    pieces = (_pool_bwd(dpooled), dq, dkv, dgl)
    gw["w_in"] = _in_bwd_w(pieces, ut)
    grads = red2.finish(gw["w_in"])
    redm = Reduction("reduce_mix", MIXER, [gw[n] for n in MIXER], place)
    token = redm.start_swap(grads[FFN2[0]])
    dh1, dh1h, dh1ht, gs["mix_norm"] = _in_bwd_x(pieces, w["w_in"], h1, dh2, _tie(gm, token))
    token = redm.start_exchange(dh1)
    da, db, s1 = _ffn_bwd_h(dh1h, a1, b1, w["ffn1_w_down"], token, "ffn1_bwd_h")
    gw["ffn1_w_gate"], gw["ffn1_w_up"] = _xty(n1t, da, "ffn1_dw_gate"), _xty(n1t, db, "ffn1_dw_up")
    gw["ffn1_w_down"] = _xty(dh1ht, s1, "ffn1_dw_down")
    token = redm.start_share(da)
    red1 = Reduction("reduce_ffn1", FFN1, [gw[n] for n in FFN1], place)
    token = red1.start_swap(token)
    grad_x, gs["ffn1_norm"] = _ffn_bwd_x(da, db, w["ffn1_w_gate"], w["ffn1_w_up"], xs, dh1, _tie(g1, token), "ffn1_bwd_x")
    grads.update(redm.finish(grad_x))

    small_parts = [gs[n] for n in SMALL] + [loss[0, 0].reshape(1)]
    slots = _split_start("gather_small", [_slot_place(_pack_small(small_parts), place)], _slots_plan, N_DEVICES - 1,
                         grads[MIXER[0]])
    token = red1.start_exchange(slots.token)
    delta, new_m, new_v = {}, {}, {}
    for n in FFN2 + MIXER:
        delta[n], new_m[n], new_v[n], grads[n] = _adamw(wts[n], grads[n], mom[n], var[n], token, "adamw_" + n)
    summed = _sum_slots(_split_wait("gather_small_wait", slots, delta[MIXER[-1]])[0])
    small_grads, loss_sum = _unpack_small(summed, [wts[n] for n in SMALL])
    grads.update(dict(zip(SMALL, small_grads)))
    zero = jnp.zeros((1,), F32)
    packed = [_pack_small([d[n] for n in SMALL] + [zero]) for d in (wts, mom, var)]
    ds, ms, vs, _ = _adamw(packed[0], summed, packed[1], packed[2], token, "adamw_small")
    like = [wts[n] for n in SMALL]
    for out, packed_out in ((delta, ds), (new_m, ms), (new_v, vs)):
        out.update(dict(zip(SMALL, _unpack_small(packed_out, like)[0])))
    token = red1.start_share(ds)
    grads.update(red1.finish(token))
    for n in FFN1:
        delta[n], new_m[n], new_v[n], grads[n] = _adamw(wts[n], grads[n], mom[n], var[n], token, "adamw_" + n)

    outs = []
    for d in (grads, delta, new_m, new_v):
        outs += [_row_form(n, d[n]) for n in WEIGHTS]
    return (loss_sum, grad_x[None], *outs)
```

```python
from typing import Callable, NamedTuple

import jax
import jax.numpy as jnp
from jax import lax
from jax.experimental import pallas as pl
from jax.experimental.pallas import tpu as pltpu

F32 = jnp.float32
BF16 = jnp.bfloat16
RMS_EPS = 1e-6
POOL_WINDOWS = (2, 4, 8, 16)
POOL_GROUP = 128
POOL_HALO = 16
HEAD_DIM = 64
GQA_GROUP = 8
N_KV_HEADS = 2
ATTN_BLOCK = 128
SCALE = HEAD_DIM ** -0.5
NEG = -1e30
N_SHARDS = 4
ADAM_LR, ADAM_B1, ADAM_B2, ADAM_EPS, ADAM_WD, ADAM_STEP = 0.001, 0.9, 0.999, 1e-08, 0.01, 10
VMEM_LIMIT = 56 * 1024 * 1024
MESH = pl.DeviceIdType.MESH
SEG_POOL, SEG_Q, SEG_KV, SEG_GATE = (0, 512), (512, 1024), (1536, 256), (1792, 2048)
SEGMENTS = (SEG_POOL, SEG_Q, SEG_KV, SEG_GATE)


def _params(**kw):
    return pltpu.CompilerParams(vmem_limit_bytes=VMEM_LIMIT, **kw)


def _dot(a, b):
    return jnp.dot(a, b, preferred_element_type=F32)


def _dot_nt(a, b):
    return lax.dot_general(a, b, (((1,), (1,)), ((), ())), preferred_element_type=F32)


def _dot_tn(a, b):
    return lax.dot_general(a, b, (((0,), (0,)), ((), ())), preferred_element_type=F32)


def _rinv(x):
    return lax.rsqrt(jnp.mean(x * x, axis=-1, keepdims=True) + RMS_EPS)


def _rms_bwd(dn, x, g):
    r = _rinv(x)
    xr = x * r
    z = dn * g
    dx = r * (z - xr * jnp.mean(z * xr, axis=-1, keepdims=True))
    return dx, dn * xr


def _acc(ref, val, first):
    @pl.when(first)
    def _():
        ref[...] = val

    @pl.when(jnp.logical_not(first))
    def _():
        ref[...] += val


def _ffn_fwd(h, gnorm, wgt, wut, wd, target, name, tm=512, tf=1408):
    S, D = h.shape
    F = wd.shape[0]
    with_loss = target is not None
    nt, nf = S // tm, F // tf

    def body(*refs):
        if with_loss:
            (h_ref, g_ref, wg_ref, wu_ref, wd_ref, t_ref, nt_ref, a_ref, b_ref, acc_ref, dyh_ref, dyht_ref, loss_ref,
             n_ref) = refs
        else:
            h_ref, g_ref, wg_ref, wu_ref, wd_ref, nt_ref, a_ref, b_ref, acc_ref, n_ref = refs
        i = pl.program_id(0)
        j = pl.program_id(1)

        @pl.when(j == 0)
        def _():
            x = h_ref[...]
            nv = x * _rinv(x) * g_ref[...]
            n_ref[...] = nv.astype(BF16)
            nt_ref[...] = nv.T.astype(BF16)

        n = n_ref[...]
        a = _dot_nt(n, wg_ref[...])
        b = _dot_nt(n, wu_ref[...])
        a_ref[...] = a.astype(BF16)
        b_ref[...] = b.astype(BF16)
        s = (a * jax.nn.sigmoid(a) * b).astype(BF16)
        _acc(acc_ref, _dot(s, wd_ref[...]), j == 0)

        @pl.when(j == nf - 1)
        def _():
            y = h_ref[...] + 0.5 * acc_ref[...]
            if with_loss:
                e = y - t_ref[...]
                dy = e * (1.0 / D)
                acc_ref[...] = dy
                dyh_ref[...] = (0.5 * dy).astype(BF16)
                dyht_ref[...] = (0.5 * dy).T.astype(BF16)
                tot = jnp.sum(jnp.sum(e * e, axis=1, keepdims=True), axis=0, keepdims=True) * (0.5 / D)
                _acc(loss_ref, jnp.broadcast_to(tot, loss_ref.shape), i == 0)
            else:
                acc_ref[...] = y

    row = pl.BlockSpec((tm, D), lambda i, j: (i, 0))
    col = pl.BlockSpec((D, tm), lambda i, j: (0, i))
    wblk = pl.BlockSpec((tf, D), lambda i, j: (j, 0))
    act = pl.BlockSpec((tm, tf), lambda i, j: (i, j))
    in_specs = [row, pl.BlockSpec((1, D), lambda i, j: (0, 0)), wblk, wblk, wblk]
    out_specs = [col, act, act, row]
    out_shape = [jax.ShapeDtypeStruct((D, S), BF16), jax.ShapeDtypeStruct((S, F), BF16),
                 jax.ShapeDtypeStruct((S, F), BF16), jax.ShapeDtypeStruct((S, D), F32)]
    args = [h, gnorm, wgt, wut, wd]
    if with_loss:
        in_specs.append(row)
        args.append(target)
        out_specs += [row, col, pl.BlockSpec((8, 128), lambda i, j: (0, 0))]
        out_shape += [jax.ShapeDtypeStruct((S, D), BF16), jax.ShapeDtypeStruct((D, S), BF16),
                      jax.ShapeDtypeStruct((8, 128), F32)]
    return pl.pallas_call(
        body, name=name, grid=(nt, nf), in_specs=in_specs, out_specs=out_specs, out_shape=out_shape,
        scratch_shapes=[pltpu.VMEM((tm, D), BF16)],
        compiler_params=_params(dimension_semantics=("arbitrary", "arbitrary")),
    )(*args)


TOKEN_SPEC = pl.BlockSpec(memory_space=pl.ANY)


def _ffn_bwd_h(dyh, a, b, wd, token, name, tm=512, tf=1408):
    S, D = dyh.shape
    F = wd.shape[0]
    nt, nf = S // tm, F // tf

    def body(dyh_ref, a_ref, b_ref, wd_ref, _, da_ref, db_ref, s_ref):
        av = a_ref[...].astype(F32)
        bv = b_ref[...].astype(F32)
        ds = _dot_nt(dyh_ref[...], wd_ref[...])
        sig = jax.nn.sigmoid(av)
        silu = av * sig
        da_ref[...] = (ds * bv * (sig * (1.0 + av * (1.0 - sig)))).astype(BF16)
        db_ref[...] = (ds * silu).astype(BF16)
        s_ref[...] = (silu * bv).astype(BF16)

    act = pl.BlockSpec((tm, tf), lambda i, j: (i, j))
    hidden = jax.ShapeDtypeStruct((S, F), BF16)
    return pl.pallas_call(
        body, name=name, grid=(nt, nf),
        in_specs=[pl.BlockSpec((tm, D), lambda i, j: (i, 0)), act, act, pl.BlockSpec((tf, D), lambda i, j: (j, 0)), TOKEN_SPEC],
        out_specs=[act, act, act], out_shape=[hidden, hidden, hidden],
        compiler_params=_params(dimension_semantics=("arbitrary", "arbitrary")),
    )(dyh, a, b, wd, token)


def _xty(yt, x, name, tk=1024, tf=1408, chunk=128):
    D, S = yt.shape
    F = x.shape[1]
    nk, nf = S // tk, F // tf
    nchunk = tf // chunk

    def body(yt_ref, x_ref, out_hbm, acc, stage, sem):
        j = pl.program_id(0)
        k = pl.program_id(1)
        _acc(acc, _dot(yt_ref[...], x_ref[...]), k == 0)

        @pl.when(k == nk - 1)
        def _():
            def out_copy(c, slot):
                rows = pl.ds(pl.multiple_of(j * tf + c * chunk, chunk), chunk)
                return pltpu.make_async_copy(stage.at[slot], out_hbm.at[rows], sem.at[slot])

            for c in range(nchunk):
                slot = c % 2
                if c >= 2:
                    out_copy(c - 2, slot).wait()
                stage[slot] = acc[:, c * chunk:(c + 1) * chunk].T
                out_copy(c, slot).start()
            for c in range(nchunk - 2, nchunk):
                out_copy(c, c % 2).wait()

    return pl.pallas_call(
        body, name=name, grid=(nf, nk),
        in_specs=[pl.BlockSpec((D, tk), lambda j, k: (0, k)), pl.BlockSpec((tk, tf), lambda j, k: (k, j))],
        out_specs=pl.BlockSpec(memory_space=pl.ANY), out_shape=jax.ShapeDtypeStruct((F, D), F32),
        scratch_shapes=[pltpu.VMEM((D, tf), F32), pltpu.VMEM((2, chunk, D), F32), pltpu.SemaphoreType.DMA((2,))],
        compiler_params=_params(dimension_semantics=("arbitrary", "arbitrary")),
    )(yt, x)


def _ffn_bwd_x(da, db, wgt, wut, x, dh, gnorm, name, tm=512, tf=1408):
    S, D = x.shape
    F = wgt.shape[0]
    nt, nf = S // tm, F // tf

    def body(da_ref, db_ref, wg_ref, wu_ref, x_ref, dh_ref, g_ref, dx_ref, dg_ref):
        i = pl.program_id(0)
        j = pl.program_id(1)
        _acc(dx_ref, _dot(da_ref[...], wg_ref[...]) + _dot(db_ref[...], wu_ref[...]), j == 0)

        @pl.when(j == nf - 1)
        def _():
            dx, dg_rows = _rms_bwd(dx_ref[...], x_ref[...], g_ref[...])
            dx_ref[...] = dh_ref[...] + dx
            _acc(dg_ref, jnp.sum(dg_rows, axis=0, keepdims=True), i == 0)

    row = pl.BlockSpec((tm, D), lambda i, j: (i, 0))
    act = pl.BlockSpec((tm, tf), lambda i, j: (i, j))
    wblk = pl.BlockSpec((tf, D), lambda i, j: (j, 0))
    vec = pl.BlockSpec((1, D), lambda i, j: (0, 0))
    return pl.pallas_call(
        body, name=name, grid=(nt, nf), in_specs=[act, act, wblk, wblk, row, row, vec],
        out_specs=[row, vec],
        out_shape=[jax.ShapeDtypeStruct((S, D), F32), jax.ShapeDtypeStruct((1, D), F32)],
        compiler_params=_params(dimension_semantics=("arbitrary", "arbitrary")),
    )(da, db, wgt, wut, x, dh, gnorm)


def _mix_proj(h, gnorm, wint, tm=512):
    S, D = h.shape
    nt = S // tm

    def body(h_ref, g_ref, w_ref, ut_ref, xp_ref, q_ref, kv_ref, gl_ref):
        x = h_ref[...]
        uv = x * _rinv(x) * g_ref[...]
        ut_ref[...] = uv.T.astype(BF16)
        u = uv.astype(BF16)
        for (off, width), ref in zip(SEGMENTS, (xp_ref, q_ref, kv_ref, gl_ref)):
            ref[...] = _dot_nt(u, w_ref[off:off + width, :]).astype(ref.dtype)

    def row(width):
        return pl.BlockSpec((tm, width), lambda i: (i, 0))

    return pl.pallas_call(
        body, name="mix_proj", grid=(nt,),
        in_specs=[row(D), pl.BlockSpec((1, D), lambda i: (0, 0)), pl.BlockSpec(wint.shape, lambda i: (0, 0))],
        out_specs=[pl.BlockSpec((D, tm), lambda i: (0, i)), row(SEG_POOL[1]), row(SEG_Q[1]), row(SEG_KV[1]), row(SEG_GATE[1])],
        out_shape=[jax.ShapeDtypeStruct((D, S), BF16), jax.ShapeDtypeStruct((S, SEG_POOL[1]), F32),
                   jax.ShapeDtypeStruct((S, SEG_Q[1]), BF16), jax.ShapeDtypeStruct((S, SEG_KV[1]), BF16),
                   jax.ShapeDtypeStruct((S, SEG_GATE[1]), BF16)],
        compiler_params=_params(dimension_semantics=("arbitrary",)),
    )(h, gnorm, wint)


def _stack_heads(x, g):
    return jnp.concatenate([x[:, (GQA_GROUP * g + hh) * HEAD_DIM:(GQA_GROUP * g + hh + 1) * HEAD_DIM]
                            for hh in range(GQA_GROUP)], axis=0)


def _unstack_heads(ref, val, g):
    for hh in range(GQA_GROUP):
        lo = (GQA_GROUP * g + hh) * HEAD_DIM
        ref[:, lo:lo + HEAD_DIM] = val[hh * ATTN_BLOCK:(hh + 1) * ATTN_BLOCK, :]


def _rowsum(xb, width):
    return _dot(xb, jnp.ones((xb.shape[1], width), BF16))


def _rinv_lanes(x):
    return lax.rsqrt(_rowsum((x * x).astype(BF16), x.shape[1]) * (1.0 / x.shape[1]) + RMS_EPS)


def _attn_exp(qn, kc, kp, sink, n):
    rows = GQA_GROUP * ATTN_BLOCK
    qi = lax.broadcasted_iota(jnp.int32, (rows, ATTN_BLOCK), 0) % ATTN_BLOCK
    kj = lax.broadcasted_iota(jnp.int32, (rows, ATTN_BLOCK), 1)
    s_c = jnp.where(kj <= qi, _dot_nt(qn, kc), NEG)
    s_p = jnp.where(jnp.logical_and(kj > qi, n > 0), _dot_nt(qn, kp), NEG)
    m = jnp.maximum(jnp.max(s_c, axis=-1, keepdims=True), jnp.max(s_p, axis=-1, keepdims=True))
    m = jnp.maximum(jnp.broadcast_to(m, sink.shape), sink)
    e_c = jnp.exp(s_c - m)
    e_p = jnp.exp(s_p - m)
    e_s = jnp.exp(sink - m)
    e_cb, e_pb = e_c.astype(BF16), e_p.astype(BF16)
    inv = 1.0 / (_rowsum(jnp.concatenate([e_cb, e_pb], axis=1), ATTN_BLOCK) + e_s)
    return e_cb, e_pb, e_c, e_p, e_s, inv


def _attn_blocks(n):
    cur = pl.multiple_of(n * ATTN_BLOCK, ATTN_BLOCK)
    prev = pl.multiple_of(jnp.maximum(n - 1, 0) * ATTN_BLOCK, ATTN_BLOCK)
    return cur, prev


def _kv_split(kv, g):
    k = kv[:, g * HEAD_DIM:(g + 1) * HEAD_DIM]
    v = kv[:, (N_KV_HEADS + g) * HEAD_DIM:(N_KV_HEADS + g + 1) * HEAD_DIM]
    return k, v


def _attn_fwd(q, kv, qw, kw, sink_rows):
    S, W = q.shape
    nb = S // ATTN_BLOCK

    def body(q_ref, kv_ref, qw_ref, kw_ref, sk_ref, o_ref, o_scr):
        n = pl.program_id(0)
        qf = q_ref[...].astype(F32)
        cur, prev = _attn_blocks(n)
        kvc = kv_ref[pl.ds(cur, ATTN_BLOCK), :].astype(F32)
        kvp = kv_ref[pl.ds(prev, ATTN_BLOCK), :].astype(F32)
        for g in range(N_KV_HEADS):
            qs = _stack_heads(qf, g)
            qn = (qs * _rinv_lanes(qs) * qw_ref[...] * SCALE).astype(BF16)
            kc, vc = _kv_split(kvc, g)
            kp, vp = _kv_split(kvp, g)
            kc = (kc * _rinv_lanes(kc) * kw_ref[...]).astype(BF16)
            kp = (kp * _rinv_lanes(kp) * kw_ref[...]).astype(BF16)
            e_cb, e_pb, _, _, _, inv = _attn_exp(qn, kc, kp, sk_ref[g], n)
            o = (_dot(e_cb, vc.astype(BF16)) + _dot(e_pb, vp.astype(BF16))) * inv[:, :HEAD_DIM]
            _unstack_heads(o_scr, o, g)
        o_ref[...] = o_scr[...].astype(BF16)

    blk = pl.BlockSpec((ATTN_BLOCK, W), lambda n: (n, 0))
    return pl.pallas_call(
        body, name="attn_fwd", grid=(nb,),
        in_specs=[blk, pl.BlockSpec(kv.shape, lambda n: (0, 0)), pl.BlockSpec((1, HEAD_DIM), lambda n: (0, 0)),
                  pl.BlockSpec((1, HEAD_DIM), lambda n: (0, 0)), pl.BlockSpec(sink_rows.shape, lambda n: (0, 0, 0))],
        out_specs=blk, out_shape=jax.ShapeDtypeStruct((S, W), BF16),
        scratch_shapes=[pltpu.VMEM((ATTN_BLOCK, W), F32)],
        compiler_params=_params(dimension_semantics=("arbitrary",)),
    )(q, kv, qw, kw, sink_rows)


def _attn_bwd(q, kv, do, qw, kw, sink_rows):
    S, W = q.shape
    KW = kv.shape[1]
    nb = S // ATTN_BLOCK
    chunk = 512

    def body(q_ref, kv_ref, do_ref, qw_ref, kw_ref, sk_ref, dq_ref, dkv_ref, dqw_ref, dkw_ref, dsk_ref, dq_scr):
        n = pl.program_id(0)

        @pl.when(n == 0)
        def _():
            dkv_ref[...] = jnp.zeros_like(dkv_ref)
            dqw_ref[...] = jnp.zeros_like(dqw_ref)
            dsk_ref[...] = jnp.zeros_like(dsk_ref)

        qf = q_ref[...].astype(F32)
        dof = do_ref[...].astype(F32)
        cur, prev = _attn_blocks(n)
        kvc = kv_ref[pl.ds(cur, ATTN_BLOCK), :].astype(F32)
        kvp = kv_ref[pl.ds(prev, ATTN_BLOCK), :].astype(F32)
        qw_v = qw_ref[...]
        for g in range(N_KV_HEADS):
            qs = _stack_heads(qf, g)
            rq = _rinv_lanes(qs)
            qhat = qs * rq
            qn = (qhat * qw_v * SCALE).astype(BF16)
            kc, vc = _kv_split(kvc, g)
            kp, vp = _kv_split(kvp, g)
            kc = (kc * _rinv_lanes(kc) * kw_ref[...]).astype(BF16)
            kp = (kp * _rinv_lanes(kp) * kw_ref[...]).astype(BF16)
            vc = vc.astype(BF16)
            vp = vp.astype(BF16)
            dos = _stack_heads(dof, g).astype(BF16)
            _, _, e_c, e_p, e_s, inv = _attn_exp(qn, kc, kp, sk_ref[g], n)
            p_c, p_p, p_s = e_c * inv, e_p * inv, e_s * inv
            dp_c = _dot_nt(dos, vc)
            dp_p = _dot_nt(dos, vp)
            drow = _rowsum(jnp.concatenate([(p_c * dp_c).astype(BF16), (p_p * dp_p).astype(BF16)], axis=1), ATTN_BLOCK)
            ds_c = (p_c * (dp_c - drow)).astype(BF16)
            ds_p = (p_p * (dp_p - drow)).astype(BF16)
            dsink = -(p_s * drow)
            for hh in range(GQA_GROUP):
                tot = jnp.sum(dsink[hh * ATTN_BLOCK:(hh + 1) * ATTN_BLOCK, :], axis=0, keepdims=True)
                dsk_ref[g, hh:hh + 1, :] += tot
            dqn = (_dot(ds_c, kc) + _dot(ds_p, kp)) * SCALE
            klo, vlo = g * HEAD_DIM, (N_KV_HEADS + g) * HEAD_DIM
            dkv_ref[pl.ds(cur, ATTN_BLOCK), klo:klo + HEAD_DIM] += _dot_tn(ds_c, qn)
            dkv_ref[pl.ds(prev, ATTN_BLOCK), klo:klo + HEAD_DIM] += _dot_tn(ds_p, qn)
            dkv_ref[pl.ds(cur, ATTN_BLOCK), vlo:vlo + HEAD_DIM] += _dot_tn(p_c.astype(BF16), dos)
            dkv_ref[pl.ds(prev, ATTN_BLOCK), vlo:vlo + HEAD_DIM] += _dot_tn(p_p.astype(BF16), dos)
            dqw_ref[...] += jnp.sum(dqn * qhat, axis=0, keepdims=True)
            z = dqn * qw_v
            dqs = rq * (z - qhat * (_rowsum((z * qhat).astype(BF16), HEAD_DIM) * (1.0 / HEAD_DIM)))
            _unstack_heads(dq_scr, dqs, g)
        dq_ref[...] = dq_scr[...].astype(BF16)

        @pl.when(n == nb - 1)
        def _():
            def one(c, dkw):
                rows = pl.ds(pl.multiple_of(c * chunk, chunk), chunk)
                for g in range(N_KV_HEADS):
                    lo = g * HEAD_DIM
                    k = kv_ref[rows, lo:lo + HEAD_DIM].astype(F32)
                    dx, dg_rows = _rms_bwd(dkv_ref[rows, lo:lo + HEAD_DIM], k, kw_ref[...])
                    dkv_ref[rows, lo:lo + HEAD_DIM] = dx
                    dkw = dkw + jnp.sum(dg_rows, axis=0, keepdims=True)
                return dkw

            dkw_ref[...] = lax.fori_loop(0, S // chunk, one, jnp.zeros((1, HEAD_DIM), F32))

    blk = pl.BlockSpec((ATTN_BLOCK, W), lambda n: (n, 0))
    whole_kv = pl.BlockSpec((S, KW), lambda n: (0, 0))
    vec = pl.BlockSpec((1, HEAD_DIM), lambda n: (0, 0))
    sk = pl.BlockSpec(sink_rows.shape, lambda n: (0, 0, 0))
    dsk = pl.BlockSpec((N_KV_HEADS, GQA_GROUP, 128), lambda n: (0, 0, 0))
    return pl.pallas_call(
        body, name="attn_bwd", grid=(nb,), in_specs=[blk, whole_kv, blk, vec, vec, sk],
        out_specs=[blk, whole_kv, vec, vec, dsk],
        out_shape=[jax.ShapeDtypeStruct((S, W), BF16), jax.ShapeDtypeStruct((S, KW), F32),
                   jax.ShapeDtypeStruct((1, HEAD_DIM), F32), jax.ShapeDtypeStruct((1, HEAD_DIM), F32),
                   jax.ShapeDtypeStruct((N_KV_HEADS, GQA_GROUP, 128), F32)],
        scratch_shapes=[pltpu.VMEM((ATTN_BLOCK, W), F32)],
        compiler_params=_params(dimension_semantics=("arbitrary",)),
    )(q, kv, do, qw, kw, sink_rows)


def _pooled(xc, xprev, i):
    tm = xc.shape[0]
    xh = jnp.concatenate([jnp.where(i > 0, xprev, 0.0), xc], axis=0)
    t = lax.broadcasted_iota(jnp.int32, (tm, 1), 0) + i * tm
    out = []
    for gi, w in enumerate(POOL_WINDOWS):
        acc = xh[:, gi * POOL_GROUP:(gi + 1) * POOL_GROUP]
        sh = 1
        while sh < w:
            acc = acc + pltpu.roll(acc, sh, 0)
            sh *= 2
        cnt = jnp.minimum(t + 1, w).astype(F32)
        out.append(acc[POOL_HALO:, :] / cnt - xc[:, gi * POOL_GROUP:(gi + 1) * POOL_GROUP])
    return jnp.concatenate(out, axis=1)


def _pool_mix(pooled_b, pw_ref):
    return jnp.concatenate([_dot(pooled_b[:, gi * POOL_GROUP:(gi + 1) * POOL_GROUP], pw_ref[gi])
                            for gi in range(len(POOL_WINDOWS))], axis=1)


def _halo_specs(tm, width, S, after):
    per = tm // POOL_HALO
    last = S // POOL_HALO - 1
    if after:
        return pl.BlockSpec((POOL_HALO, width), lambda i: (jnp.minimum((i + 1) * per, last), 0))
    return pl.BlockSpec((POOL_HALO, width), lambda i: (jnp.maximum(i * per - 1, 0), 0))


def _mix_out(xp, attn, gl, bias, pw, pscale, wpot, wao, wo, h, tm=512):
    S, D = h.shape
    nt = S // tm
    PW = xp.shape[1]

    def body(xc_ref, xprev_ref, at_ref, gl_ref, bias_ref, pw_ref, ps_ref, wpo_ref, wao_ref, wo_ref, h_ref,
             ho_ref, bp_ref, ba_ref):
        i = pl.program_id(0)
        pooled = _pooled(xc_ref[...], xprev_ref[...], i).astype(BF16)
        ms = (_pool_mix(pooled, pw_ref) * ps_ref[...]).astype(BF16)
        bp = _dot_nt(ms, wpo_ref[...])
        ba = _dot(at_ref[...], wao_ref[...])
        bp_ref[...] = bp.astype(BF16)
        ba_ref[...] = ba.astype(BF16)
        gates = jax.nn.sigmoid(gl_ref[...].astype(F32) + bias_ref[...])
        merged = (gates[:, :D] * bp + gates[:, D:] * ba).astype(BF16)
        ho_ref[...] = h_ref[...] + _dot(merged, wo_ref[...])

    def row(width):
        return pl.BlockSpec((tm, width), lambda i: (i, 0))

    def whole(x):
        nd = x.ndim
        return pl.BlockSpec(x.shape, lambda i: (0,) * nd)

    return pl.pallas_call(
        body, name="mix_out", grid=(nt,),
        in_specs=[row(PW), _halo_specs(tm, PW, S, False), row(D), row(2 * D), whole(bias), whole(pw), whole(pscale),
                  whole(wpot), whole(wao), whole(wo), row(D)],
        out_specs=[row(D), row(D), row(D)],
        out_shape=[jax.ShapeDtypeStruct((S, D), F32), jax.ShapeDtypeStruct((S, D), BF16),
                   jax.ShapeDtypeStruct((S, D), BF16)],
        compiler_params=_params(dimension_semantics=("arbitrary",)),
    )(xp, xp, attn, gl, bias, pw, pscale, wpot, wao, wo, h)


def _mix_bwd_gate(dh, wo, bp, ba, gl, bias, tm=512):
    S, D = dh.shape
    nt = S // tm

    def body(dh_ref, wo_ref, bp_ref, ba_ref, gl_ref, bias_ref, dgl_ref, dbp_ref, dba_ref, dwo_ref, dbias_ref):
        i = pl.program_id(0)
        dhb = dh_ref[...].astype(BF16)
        dm = _dot_nt(dhb, wo_ref[...])
        gates = jax.nn.sigmoid(gl_ref[...].astype(F32) + bias_ref[...])
        gp, ga = gates[:, :D], gates[:, D:]
        bp_v = bp_ref[...].astype(F32)
        ba_v = ba_ref[...].astype(F32)
        merged = (gp * bp_v + ga * ba_v).astype(BF16)
        _acc(dwo_ref, _dot_tn(merged, dhb), i == 0)
        dbp_ref[...] = (dm * gp).astype(BF16)
        dba_ref[...] = (dm * ga).astype(BF16)
        dgl = jnp.concatenate([dm * bp_v * gp * (1.0 - gp), dm * ba_v * ga * (1.0 - ga)], axis=1)
        dgl_ref[...] = dgl.astype(BF16)
        _acc(dbias_ref, jnp.sum(dgl, axis=0, keepdims=True), i == 0)

    def row(width):
        return pl.BlockSpec((tm, width), lambda i: (i, 0))

    def whole(shape):
        return pl.BlockSpec(shape, lambda i: (0, 0))

    return pl.pallas_call(
        body, name="mix_bwd_gate", grid=(nt,),
        in_specs=[row(D), whole(wo.shape), row(D), row(D), row(2 * D), whole(bias.shape)],
        out_specs=[row(2 * D), row(D), row(D), whole((D, D)), whole((1, 2 * D))],
        out_shape=[jax.ShapeDtypeStruct((S, 2 * D), BF16), jax.ShapeDtypeStruct((S, D), BF16),
                   jax.ShapeDtypeStruct((S, D), BF16), jax.ShapeDtypeStruct((D, D), F32),
                   jax.ShapeDtypeStruct((1, 2 * D), F32)],
        compiler_params=_params(dimension_semantics=("arbitrary",)),
    )(dh, wo, bp, ba, gl, bias)


def _mix_bwd_branch(dbp, dba, attn, xp, pw, pscale, wpot, wao, tm=512):
    S, D = dbp.shape
    nt = S // tm
    PW = xp.shape[1]
    NG = len(POOL_WINDOWS)

    def body(dbp_ref, dba_ref, at_ref, xc_ref, xprev_ref, pw_ref, ps_ref, wpo_ref, wao_ref,
             dat_ref, dpl_ref, dwao_ref, dwpo_ref, dpw_ref, dps_ref):
        i = pl.program_id(0)
        dba_v = dba_ref[...]
        dbp_v = dbp_ref[...]
        _acc(dwao_ref, _dot_tn(at_ref[...], dba_v), i == 0)
        dat_ref[...] = _dot_nt(dba_v, wao_ref[...]).astype(BF16)
        pooled = _pooled(xc_ref[...], xprev_ref[...], i).astype(BF16)
        mixed = _pool_mix(pooled, pw_ref)
        ps = ps_ref[...]
        _acc(dwpo_ref, _dot_tn(dbp_v, (mixed * ps).astype(BF16)), i == 0)
        dms = _dot(dbp_v, wpo_ref[...])
        _acc(dps_ref, jnp.sum(dms * mixed, axis=0, keepdims=True), i == 0)
        dmixed = (dms * ps).astype(BF16)
        dpooled = []
        for gi in range(NG):
            cols = slice(gi * POOL_GROUP, (gi + 1) * POOL_GROUP)
            _acc(dpw_ref.at[gi], _dot_tn(pooled[:, cols], dmixed[:, cols]), i == 0)
            dpooled.append(_dot_nt(dmixed[:, cols], pw_ref[gi]))
        dpl_ref[...] = jnp.concatenate(dpooled, axis=1)

    def row(width):
        return pl.BlockSpec((tm, width), lambda i: (i, 0))

    def whole(shape):
        nd = len(shape)
        return pl.BlockSpec(shape, lambda i: (0,) * nd)

    return pl.pallas_call(
        body, name="mix_bwd_branch", grid=(nt,),
        in_specs=[row(D), row(D), row(D), row(PW), _halo_specs(tm, PW, S, False), whole(pw.shape), whole(pscale.shape),
                  whole(wpot.shape), whole(wao.shape)],
        out_specs=[row(D), row(PW), whole((D, D)), whole((D, PW)), whole(pw.shape), whole((1, PW))],
        out_shape=[jax.ShapeDtypeStruct((S, D), BF16), jax.ShapeDtypeStruct((S, PW), F32),
                   jax.ShapeDtypeStruct((D, D), F32), jax.ShapeDtypeStruct((D, PW), F32),
                   jax.ShapeDtypeStruct(pw.shape, F32), jax.ShapeDtypeStruct((1, PW), F32)],
        compiler_params=_params(dimension_semantics=("arbitrary",)),
    )(dbp, dba, attn, xp, xp, pw, pscale, wpot, wao)


def _pool_bwd(dpooled, tm=512):
    S, PW = dpooled.shape
    nt = S // tm

    def body(dc_ref, dnext_ref, dxp_ref):
        i = pl.program_id(0)
        dc = dc_ref[...]
        dh = jnp.concatenate([dc, jnp.where(i < nt - 1, dnext_ref[...], 0.0)], axis=0)
        rows = tm + POOL_HALO
        t = lax.broadcasted_iota(jnp.int32, (rows, 1), 0) + i * tm
        out = []
        for gi, w in enumerate(POOL_WINDOWS):
            cols = slice(gi * POOL_GROUP, (gi + 1) * POOL_GROUP)
            acc = dh[:, cols] / jnp.minimum(t + 1, w).astype(F32)
            sh = 1
            while sh < w:
                acc = acc + pltpu.roll(acc, rows - sh, 0)
                sh *= 2
            out.append(acc[:tm, :] - dc[:, cols])
        dxp_ref[...] = jnp.concatenate(out, axis=1).astype(BF16)

    return pl.pallas_call(
        body, name="pool_bwd", grid=(nt,),
        in_specs=[pl.BlockSpec((tm, PW), lambda i: (i, 0)), _halo_specs(tm, PW, S, True)],
        out_specs=pl.BlockSpec((tm, PW), lambda i: (i, 0)), out_shape=jax.ShapeDtypeStruct((S, PW), BF16),
        compiler_params=_params(dimension_semantics=("arbitrary",)),
    )(dpooled, dpooled)


def _in_bwd_w(pieces, ut, tm=1024, chunk=256):
    D, S = ut.shape
    nt = S // tm
    NW = sum(width for _, width in SEGMENTS)
    nchunk = NW // chunk

    def body(*refs):
        piece_refs, ut_ref, dw_hbm, acc, stage, sem = (refs[:len(SEGMENTS)], refs[len(SEGMENTS)], refs[len(SEGMENTS) + 1],
                                                      refs[-3], refs[-2], refs[-1])
        i = pl.program_id(0)
        u_t = ut_ref[...]
        for (off, width), ref in zip(SEGMENTS, piece_refs):
            for lo in range(0, width, D):
                hi = min(lo + D, width)
                _acc(acc.at[:, off + lo:off + hi], _dot(u_t, ref[:, lo:hi].astype(BF16)), i == 0)

        @pl.when(i == nt - 1)
        def _():
            def out_copy(c, slot):
                return pltpu.make_async_copy(stage.at[slot], dw_hbm.at[pl.ds(c * chunk, chunk)], sem.at[slot])

            for c in range(nchunk):
                slot = c % 2
                if c >= 2:
                    out_copy(c - 2, slot).wait()
                stage[slot] = acc[:, c * chunk:(c + 1) * chunk].T
                out_copy(c, slot).start()
            for c in range(nchunk - 2, nchunk):
                out_copy(c, c % 2).wait()

    return pl.pallas_call(
        body, name="in_bwd_w", grid=(nt,),
        in_specs=[pl.BlockSpec((tm, width), lambda i: (i, 0)) for _, width in SEGMENTS] + [pl.BlockSpec((D, tm), lambda i: (0, i))],
        out_specs=pl.BlockSpec(memory_space=pl.ANY), out_shape=jax.ShapeDtypeStruct((NW, D), F32),
        scratch_shapes=[pltpu.VMEM((D, NW), F32), pltpu.VMEM((2, chunk, D), F32), pltpu.SemaphoreType.DMA((2,))],
        compiler_params=_params(dimension_semantics=("arbitrary",)),
    )(*pieces, ut)


def _in_bwd_x(pieces, wint, h, dh, gnorm, tm=512):
    S, D = h.shape
    nt = S // tm

    def body(*refs):
        piece_refs = refs[:len(SEGMENTS)]
        w_ref, h_ref, dh_ref, g_ref, dx_ref, dxh_ref, dxht_ref, dg_ref = refs[len(SEGMENTS):]
        i = pl.program_id(0)
        du = jnp.zeros((tm, D), F32)
        for (off, width), ref in zip(SEGMENTS, piece_refs):
            du = du + _dot(ref[...].astype(BF16), w_ref[off:off + width, :])
        dx, dg_rows = _rms_bwd(du, h_ref[...], g_ref[...])
        out = dh_ref[...] + dx
        dx_ref[...] = out
        dxh_ref[...] = (0.5 * out).astype(BF16)
        dxht_ref[...] = (0.5 * out).T.astype(BF16)
        _acc(dg_ref, jnp.sum(dg_rows, axis=0, keepdims=True), i == 0)

    def row(width):
        return pl.BlockSpec((tm, width), lambda i: (i, 0))

    vec = pl.BlockSpec((1, D), lambda i: (0, 0))
    return pl.pallas_call(
        body, name="in_bwd_x", grid=(nt,),
        in_specs=[row(width) for _, width in SEGMENTS] + [pl.BlockSpec(wint.shape, lambda i: (0, 0)), row(D), row(D), vec],
        out_specs=[row(D), row(D), pl.BlockSpec((D, tm), lambda i: (0, i)), vec],
        out_shape=[jax.ShapeDtypeStruct((S, D), F32), jax.ShapeDtypeStruct((S, D), BF16),
                   jax.ShapeDtypeStruct((D, S), BF16), jax.ShapeDtypeStruct((1, D), F32)],
        compiler_params=_params(dimension_semantics=("arbitrary",)),
    )(*pieces, wint, h, dh, gnorm)


def _row_tile(rows):
    for t in (512, 480, 352, 256, 128, 64, 32, 16, 8):
        if rows % t == 0:
            return t
    return rows


def _adamw(w, g, m, v, token, name):
    R, C = w.shape
    tr = _row_tile(R)

    def body(w_ref, g_ref, m_ref, v_ref, _, d_ref, mo_ref, vo_ref, go_ref):
        gv = g_ref[...]
        go_ref[...] = gv
        mn = ADAM_B1 * m_ref[...] + (1.0 - ADAM_B1) * gv
        vn = ADAM_B2 * v_ref[...] + (1.0 - ADAM_B2) * (gv * gv)
        m_hat = mn / (1.0 - ADAM_B1 ** ADAM_STEP)
        v_hat = vn / (1.0 - ADAM_B2 ** ADAM_STEP)
        d_ref[...] = -ADAM_LR * (m_hat / (jnp.sqrt(v_hat) + ADAM_EPS) + ADAM_WD * w_ref[...])
        mo_ref[...] = mn
        vo_ref[...] = vn

    blk = pl.BlockSpec((tr, C), lambda i: (i, 0))
    sh = jax.ShapeDtypeStruct((R, C), F32)
    return pl.pallas_call(
        body, name=name, grid=(R // tr,), in_specs=[blk] * 4 + [TOKEN_SPEC], out_specs=[blk] * 4, out_shape=[sh] * 4,
        compiler_params=_params(dimension_semantics=("arbitrary",)),
    )(w, g, m, v, token)


def _cast_place(w, place, name):
    R, C = w.shape
    tr = _row_tile(R)
    per = R // tr

    def body(p_ref, w_ref, o_ref):
        o_ref[...] = w_ref[...].astype(BF16)

    grid_spec = pltpu.PrefetchScalarGridSpec(
        num_scalar_prefetch=1, grid=(per,),
        in_specs=[pl.BlockSpec((tr, C), lambda i, p: (i, 0))],
        out_specs=pl.BlockSpec((tr, C), lambda i, p: (p[0] * per + i, 0)))
    return pl.pallas_call(
        body, name=name, grid_spec=grid_spec, out_shape=jax.ShapeDtypeStruct((N_SHARDS * R, C), BF16),
        compiler_params=_params(dimension_semantics=("arbitrary",)),
    )(place, w)


def _sum_halves(g4, recv, place, name):
    NS, R, C = g4.shape
    hr = R // 2
    tr = _row_tile(hr)
    per = hr // tr

    def body(p_ref, g_ref, r_ref, o_ref):
        o_ref[...] = (g_ref[...] + r_ref[...]).astype(BF16)

    grid_spec = pltpu.PrefetchScalarGridSpec(
        num_scalar_prefetch=1, grid=(NS, per),
        in_specs=[pl.BlockSpec((1, tr, C), lambda s, i, p: (s, p[1] * per + i, 0)),
                  pl.BlockSpec((1, tr, C), lambda s, i, p: (s, i, 0))],
        out_specs=pl.BlockSpec((1, tr, C), lambda s, i, p: (s, i, 0)))
    return pl.pallas_call(
        body, name=name, grid_spec=grid_spec, out_shape=jax.ShapeDtypeStruct((NS, hr, C), BF16),
        compiler_params=_params(dimension_semantics=("arbitrary", "arbitrary")),
    )(place, g4, recv)


def _sum_quarters(h4, recv3, place, name):
    NS, hr, C = h4.shape
    tr = _row_tile(hr)
    per = hr // tr

    def body(p_ref, h_ref, r_ref, o_ref):
        acc = h_ref[0].astype(F32)
        for k in range(N_SHARDS - 1):
            acc = acc + r_ref[k].astype(F32)
        o_ref[...] = acc

    grid_spec = pltpu.PrefetchScalarGridSpec(
        num_scalar_prefetch=1, grid=(per,),
        in_specs=[pl.BlockSpec((1, tr, C), lambda i, p: (p[0], i, 0)),
                  pl.BlockSpec((N_SHARDS - 1, tr, C), lambda i, p: (0, i, 0))],
        out_specs=pl.BlockSpec((tr, C), lambda i, p: (p[1] * per + i, 0)))
    return pl.pallas_call(
        body, name=name, grid_spec=grid_spec, out_shape=jax.ShapeDtypeStruct((2 * hr, C), F32),
        compiler_params=_params(dimension_semantics=("arbitrary",)),
    )(place, h4, recv3)


def _place():
    x, y, c = lax.axis_index("x"), lax.axis_index("y"), lax.axis_index("c")
    chips = [(1 - x, y), (x, 1 - y), (1 - x, 1 - y)]
    return x, y, c, chips


HBM_SPEC = pl.BlockSpec(memory_space=pltpu.HBM)
SEM_SPEC = pl.BlockSpec(memory_space=pltpu.SEMAPHORE)
DATAFLOW = pltpu.SideEffectType.DATAFLOW_SIDE_EFFECTING


def _hbm(a):
    return pltpu.with_memory_space_constraint(a, pltpu.HBM)


class InFlight(NamedTuple):
    send_sem: jax.Array
    recv_sem: jax.Array
    bufs: list
    plan: Callable
    token: jax.Array


def _wait_all(plan, refs, send_ref, recv_ref):
    for k, (src, dst, dev) in enumerate(plan(refs)):
        cp = pltpu.make_async_remote_copy(src_ref=src, dst_ref=dst, send_sem=send_ref.at[k], recv_sem=recv_ref.at[k],
                                          device_id=dev, device_id_type=MESH)
        cp.wait_send()
        cp.wait_recv()


def _start_all(plan, refs, send_ref, recv_ref):
    for k, (src, dst, dev) in enumerate(plan(refs)):
        pltpu.make_async_remote_copy(src_ref=src, dst_ref=dst, send_sem=send_ref.at[k], recv_sem=recv_ref.at[k],
                                     device_id=dev, device_id_type=MESH).start()


def _split_start(name, bufs, plan, n_copies, after, waits=None):
    nb = len(bufs)

    def body(*refs):
        ins = refs[:nb]
        if waits is not None:
            _wait_all(waits.plan, ins, refs[nb], refs[nb + 1])
        send_sem, recv_sem, token = refs[-nb - 3], refs[-nb - 2], refs[-1]
        _start_all(plan, ins, send_sem, recv_sem)
        token[...] = jnp.zeros_like(token)

    sems = pltpu.SemaphoreType.DMA((n_copies,))
    earlier = [] if waits is None else [waits.send_sem, waits.recv_sem]
    out = pl.pallas_call(
        body, name=name, in_specs=[HBM_SPEC] * nb + [SEM_SPEC] * len(earlier) + [pl.BlockSpec(memory_space=pl.ANY)],
        out_shape=(sems, sems, *[pltpu.HBM(b.shape, b.dtype) for b in bufs], jax.ShapeDtypeStruct((8, 128), F32)),
        out_specs=(SEM_SPEC, SEM_SPEC, *[HBM_SPEC] * nb, pl.BlockSpec(memory_space=pltpu.VMEM)),
        input_output_aliases={i: i + 2 for i in range(nb)},
        compiler_params=pltpu.CompilerParams(has_side_effects=DATAFLOW),
    )(*[_hbm(b) for b in bufs], *earlier, after)
    return InFlight(out[0], out[1], list(out[2:2 + nb]), plan, out[-1])


def _split_wait(name, flight, after):
    nb = len(flight.bufs)

    def body(*refs):
        _wait_all(flight.plan, refs[:nb], refs[nb], refs[nb + 1])

    out = pl.pallas_call(
        body, name=name, in_specs=[HBM_SPEC] * nb + [SEM_SPEC, SEM_SPEC, pl.BlockSpec(memory_space=pl.ANY)],
        out_shape=[pltpu.HBM(b.shape, b.dtype) for b in flight.bufs], out_specs=[HBM_SPEC] * nb,
        input_output_aliases={i: i for i in range(nb)},
        compiler_params=pltpu.CompilerParams(has_side_effects=DATAFLOW),
    )(*flight.bufs, flight.send_sem, flight.recv_sem, after)
    return list(out)


def _half_rows(buf, chip, core):
    hr = buf.shape[0] // (2 * N_SHARDS)
    return buf.at[pl.ds(pl.multiple_of((2 * chip + core) * hr, 16), hr)]


def _gather_ici_plan(bufs):
    x, y, c, chips = _place()
    return [(_half_rows(b, 2 * x + y, c), _half_rows(b, 2 * x + y, c), (px, py, c)) for b in bufs for px, py in chips]


def _gather_d2d_plan(bufs):
    x, y, c, chips = _place()
    return [(_half_rows(b, 2 * px + py, c), _half_rows(b, 2 * px + py, c), (x, y, 1 - c)) for b in bufs for px, py in chips]


def _swap_plan(bufs):
    x, y, c, _ = _place()
    n = len(bufs) // 2
    copies = []
    for g, land in zip(bufs[:n], bufs[n:]):
        hr = g.shape[1] // 2
        copies.append((g.at[:, pl.ds(pl.multiple_of((1 - c) * hr, 8), hr)], land, (x, y, 1 - c)))
    return copies


def _exchange_plan(bufs):
    x, y, c, chips = _place()
    n = len(bufs) // 2
    return [(h.at[2 * px + py], land.at[k], (px, py, c))
            for h, land in zip(bufs[:n], bufs[n:]) for k, (px, py) in enumerate(chips)]


def _share_plan(bufs):
    x, y, c, _ = _place()
    copies = []
    for buf in bufs:
        hr = buf.shape[0] // 2
        mine = buf.at[pl.ds(pl.multiple_of(c * hr, 8), hr)]
        copies.append((mine, mine, (x, y, 1 - c)))
    return copies


N_DEVICES = 8


def _slot_place(vec, place):
    R, C = vec.shape

    def body(p_ref, v_ref, o_ref):
        o_ref[0] = v_ref[...]

    grid_spec = pltpu.PrefetchScalarGridSpec(
        num_scalar_prefetch=1, grid=(1,), in_specs=[pl.BlockSpec((R, C), lambda i, p: (0, 0))],
        out_specs=pl.BlockSpec((1, R, C), lambda i, p: (2 * p[0] + p[1], 0, 0)))
    return pl.pallas_call(
        body, name="slot_place", grid_spec=grid_spec, out_shape=jax.ShapeDtypeStruct((N_DEVICES, R, C), F32),
        compiler_params=_params(dimension_semantics=("arbitrary",)),
    )(place, vec)


def _slots_plan(bufs):
    x, y, c, _ = _place()
    mine = bufs[0].at[4 * x + 2 * y + c]
    return [(mine, mine, (x ^ (r >> 2), y ^ ((r >> 1) & 1), c ^ (r & 1))) for r in range(1, N_DEVICES)]


def _sum_slots(slots):
    _, R, C = slots.shape

    def body(s_ref, o_ref):
        acc = s_ref[0]
        for d in range(1, N_DEVICES):
            acc = acc + s_ref[d]
        o_ref[...] = acc

    return pl.pallas_call(
        body, name="sum_slots", in_specs=[pl.BlockSpec(memory_space=pltpu.VMEM)],
        out_specs=pl.BlockSpec(memory_space=pltpu.VMEM), out_shape=jax.ShapeDtypeStruct((R, C), F32),
        compiler_params=_params(),
    )(slots)


SMALL = ("ffn1_norm", "mix_norm", "pool_w", "pool_scale", "q_norm", "k_norm", "sinks", "gate_bias", "ffn2_norm")
SMALL_COLS = 1024
FFN1 = ("ffn1_w_gate", "ffn1_w_up", "ffn1_w_down")
MIXER = ("w_in", "w_pool_out", "w_attn_out", "w_out")
FFN2 = ("ffn2_w_gate", "ffn2_w_up", "ffn2_w_down")
LARGE = FFN1 + MIXER + FFN2
TRANSPOSED = ("ffn1_w_gate", "ffn1_w_up", "w_in", "w_pool_out", "ffn2_w_gate", "ffn2_w_up")
WEIGHTS = ("ffn1_norm", "ffn1_w_gate", "ffn1_w_up", "ffn1_w_down", "mix_norm", "w_in", "pool_w", "pool_scale",
           "w_pool_out", "q_norm", "k_norm", "sinks", "w_attn_out", "gate_bias", "w_out", "ffn2_norm",
           "ffn2_w_gate", "ffn2_w_up", "ffn2_w_down")


def _pack_small(parts):
    flat = jnp.concatenate([p.reshape(-1) for p in parts])
    rows = -(-flat.shape[0] // (8 * SMALL_COLS)) * 8
    return jnp.pad(flat, (0, rows * SMALL_COLS - flat.shape[0])).reshape(rows, SMALL_COLS)


def _unpack_small(packed, like):
    flat = packed.reshape(-1)
    out, off = [], 0
    for p in like:
        out.append(flat[off:off + p.size].reshape(p.shape))
        off += p.size
    return out, flat[off]


def _tie(small, token):
    return small + token[0, 0]


class Reduction:
    def __init__(self, group, names, grads, place):
        self.group, self.names, self.place = group, names, place
        self.bufs = [g.reshape(N_SHARDS, -1, g.shape[-1]) for g in grads]
        self.flight = None

    def _start(self, stage, bufs, plan, n_copies, after):
        self.flight = _split_start(f"{self.group}_{stage}", bufs, plan, n_copies, after)
        return self.flight.token

    def _landed(self, stage, after):
        n = len(self.names)
        bufs = _split_wait(f"{self.group}_{stage}_wait", self.flight, after)
        return bufs[:n], bufs[n:]

    def start_swap(self, after):
        lands = [lax.empty((g.shape[0], g.shape[1] // 2, g.shape[2]), g.dtype) for g in self.bufs]
        return self._start("swap", self.bufs + lands, _swap_plan, len(self.bufs), after)

    def start_exchange(self, after):
        g4, recv = self._landed("swap", after)
        halves = [_sum_halves(g, r, self.place, f"sum_halves_{n}") for n, g, r in zip(self.names, g4, recv)]
        lands = [lax.empty((N_SHARDS - 1,) + h.shape[1:], h.dtype) for h in halves]
        return self._start("exchange", halves + lands, _exchange_plan, (N_SHARDS - 1) * len(halves), self.place)

    def start_share(self, after):
        halves, recv3 = self._landed("exchange", after)
        reduced = [_sum_quarters(h, r, self.place, f"sum_quarters_{n}") for n, h, r in zip(self.names, halves, recv3)]
        return self._start("share", reduced, _share_plan, len(reduced), self.place)

    def finish(self, after):
        return dict(zip(self.names, _split_wait(f"{self.group}_share_wait", self.flight, after)))


def _row_form(name, a):
    return a.T if name in TRANSPOSED else a


def kernel(x, ffn1_norm, ffn1_w_gate, ffn1_w_up, ffn1_w_down, mix_norm, w_in, pool_w, pool_scale, w_pool_out, q_norm, k_norm, sinks, w_attn_out, gate_bias, w_out, ffn2_norm, ffn2_w_gate, ffn2_w_up, ffn2_w_down, loss_target, m_ffn1_norm, m_ffn1_w_gate, m_ffn1_w_up, m_ffn1_w_down, m_mix_norm, m_w_in, m_pool_w, m_pool_scale, m_w_pool_out, m_q_norm, m_k_norm, m_sinks, m_w_attn_out, m_gate_bias, m_w_out, m_ffn2_norm, m_ffn2_w_gate, m_ffn2_w_up, m_ffn2_w_down, v_ffn1_norm, v_ffn1_w_gate, v_ffn1_w_up, v_ffn1_w_down, v_mix_norm, v_w_in, v_pool_w, v_pool_scale, v_w_pool_out, v_q_norm, v_k_norm, v_sinks, v_w_attn_out, v_gate_bias, v_w_out, v_ffn2_norm, v_ffn2_w_gate, v_ffn2_w_up, v_ffn2_w_down):
    args = dict(locals())
    wts = {n: _row_form(n, args[n]) for n in WEIGHTS}
    mom = {n: _row_form(n, args["m_" + n]) for n in WEIGHTS}
    var = {n: _row_form(n, args["v_" + n]) for n in WEIGHTS}
    shard = 2 * lax.axis_index("x") + lax.axis_index("y")
    place = jnp.stack([shard, lax.axis_index("c")]).astype(jnp.int32)

    xs, target = x[0], loss_target[0]
    D = xs.shape[1]
    g1 = wts["ffn1_norm"].reshape(1, D)
    gm = wts["mix_norm"].reshape(1, D)
    g2 = wts["ffn2_norm"].reshape(1, D)
    qw = wts["q_norm"].reshape(1, HEAD_DIM)
    kw = wts["k_norm"].reshape(1, HEAD_DIM)
    bias = wts["gate_bias"].reshape(1, 2 * D)
    pscale = wts["pool_scale"].reshape(1, -1)
    pw = wts["pool_w"].astype(BF16)
    sink_rows = jnp.broadcast_to(jnp.repeat(wts["sinks"], ATTN_BLOCK).reshape(N_KV_HEADS, GQA_GROUP * ATTN_BLOCK, 1),
                                 (N_KV_HEADS, GQA_GROUP * ATTN_BLOCK, 128))

    placed = {n: _cast_place(wts[n], place, "cast_" + n) for n in FFN1}
    ffn1_ici = _split_start("gather_ffn1_ici", [placed[n] for n in FFN1], _gather_ici_plan, 3 * len(FFN1), place)
    placed.update({n: _cast_place(wts[n], place, "cast_" + n) for n in MIXER + FFN2})
    ffn1_d2d = _split_start("gather_ffn1_d2d", ffn1_ici.bufs, _gather_d2d_plan, 3 * len(FFN1), placed[FFN2[-1]],
                            waits=ffn1_ici)
    w = dict(zip(FFN1, _split_wait("gather_ffn1_wait", ffn1_d2d, ffn1_d2d.token)))
    mix_ici = _split_start("gather_mix_ici", [placed[n] for n in MIXER], _gather_ici_plan, 3 * len(MIXER), w[FFN1[0]])
    ffn2_ici = _split_start("gather_ffn2_ici", [placed[n] for n in FFN2], _gather_ici_plan, 3 * len(FFN2), mix_ici.token)

    n1t, a1, b1, h1 = _ffn_fwd(xs, _tie(g1, ffn2_ici.token), w["ffn1_w_gate"], w["ffn1_w_up"], w["ffn1_w_down"], None, "ffn1_fwd")
    mix_d2d = _split_start("gather_mix_d2d", mix_ici.bufs, _gather_d2d_plan, 3 * len(MIXER), h1, waits=mix_ici)
    w.update(zip(MIXER, _split_wait("gather_mix_wait", mix_d2d, mix_d2d.token)))
    ut, xp, q, kv, gl = _mix_proj(h1, gm, w["w_in"])
    attn = _attn_fwd(q, kv, qw, kw, sink_rows)
    ffn2_d2d = _split_start("gather_ffn2_d2d", ffn2_ici.bufs, _gather_d2d_plan, 3 * len(FFN2), attn, waits=ffn2_ici)
    h2, bp, ba = _mix_out(xp, attn, gl, _tie(bias, ffn2_d2d.token), pw, pscale, w["w_pool_out"], w["w_attn_out"], w["w_out"], h1)
    w.update(zip(FFN2, _split_wait("gather_ffn2_wait", ffn2_d2d, h2)))
    n2t, a2, b2, dy, dyh, dyht, loss = _ffn_fwd(h2, g2, w["ffn2_w_gate"], w["ffn2_w_up"], w["ffn2_w_down"], target, "ffn2_fwd")

    gw, gs = {}, {}
    da, db, s2 = _ffn_bwd_h(dyh, a2, b2, w["ffn2_w_down"], ffn2_d2d.token, "ffn2_bwd_h")
    gw["ffn2_w_gate"], gw["ffn2_w_up"] = _xty(n2t, da, "ffn2_dw_gate"), _xty(n2t, db, "ffn2_dw_up")
    gw["ffn2_w_down"] = _xty(dyht, s2, "ffn2_dw_down")
    red2 = Reduction("reduce_ffn2", FFN2, [gw[n] for n in FFN2], place)
    token = red2.start_swap(da)
    dh2, gs["ffn2_norm"] = _ffn_bwd_x(da, db, w["ffn2_w_gate"], w["ffn2_w_up"], h2, dy, _tie(g2, token), "ffn2_bwd_x")
    token = red2.start_exchange(dh2)
    dgl, dbp, dba, gw["w_out"], gs["gate_bias"] = _mix_bwd_gate(dh2, w["w_out"], bp, ba, gl, _tie(bias, token))
    dattn, dpooled, gw["w_attn_out"], gw["w_pool_out"], gs["pool_w"], gs["pool_scale"] = _mix_bwd_branch(
        dbp, dba, attn, xp, pw, pscale, w["w_pool_out"], w["w_attn_out"])
    dq, dkv, gs["q_norm"], gs["k_norm"], dsk = _attn_bwd(q, kv, dattn, qw, kw, sink_rows)
    gs["sinks"] = dsk[:, :, 0]
    red2.start_share(dq)
    pieces = (_pool_bwd(dpooled), dq, dkv, dgl)
    gw["w_in"] = _in_bwd_w(pieces, ut)
    grads = red2.finish(gw["w_in"])
    redm = Reduction("reduce_mix", MIXER, [gw[n] for n in MIXER], place)
    token = redm.start_swap(grads[FFN2[0]])
    dh1, dh1h, dh1ht, gs["mix_norm"] = _in_bwd_x(pieces, w["w_in"], h1, dh2, _tie(gm, token))
    token = redm.start_exchange(dh1)
    da, db, s1 = _ffn_bwd_h(dh1h, a1, b1, w["ffn1_w_down"], token, "ffn1_bwd_h")
    gw["ffn1_w_gate"], gw["ffn1_w_up"] = _xty(n1t, da, "ffn1_dw_gate"), _xty(n1t, db, "ffn1_dw_up")
    gw["ffn1_w_down"] = _xty(dh1ht, s1, "ffn1_dw_down")
    token = redm.start_share(da)
    red1 = Reduction("reduce_ffn1", FFN1, [gw[n] for n in FFN1], place)
    token = red1.start_swap(token)
    grad_x, gs["ffn1_norm"] = _ffn_bwd_x(da, db, w["ffn1_w_gate"], w["ffn1_w_up"], xs, dh1, _tie(g1, token), "ffn1_bwd_x")
    grads.update(redm.finish(grad_x))

    small_parts = [gs[n] for n in SMALL] + [loss[0, 0].reshape(1)]
    slots = _split_start("gather_small", [_slot_place(_pack_small(small_parts), place)], _slots_plan, N_DEVICES - 1,
                         grads[MIXER[0]])
    token = red1.start_exchange(slots.token)
    delta, new_m, new_v = {}, {}, {}
    for n in FFN2 + MIXER:
        delta[n], new_m[n], new_v[n], grads[n] = _adamw(wts[n], grads[n], mom[n], var[n], token, "adamw_" + n)
    summed = _sum_slots(_split_wait("gather_small_wait", slots, delta[MIXER[-1]])[0])
    small_grads, loss_sum = _unpack_small(summed, [wts[n] for n in SMALL])
    grads.update(dict(zip(SMALL, small_grads)))
    zero = jnp.zeros((1,), F32)
    packed = [_pack_small([d[n] for n in SMALL] + [zero]) for d in (wts, mom, var)]
    ds, ms, vs, _ = _adamw(packed[0], summed, packed[1], packed[2], token, "adamw_small")
    like = [wts[n] for n in SMALL]
    for out, packed_out in ((delta, ds), (new_m, ms), (new_v, vs)):
        out.update(dict(zip(SMALL, _unpack_small(packed_out, like)[0])))
    token = red1.start_share(ds)
    grads.update(red1.finish(token))
    for n in FFN1:
        delta[n], new_m[n], new_v[n], grads[n] = _adamw(wts[n], grads[n], mom[n], var[n], token, "adamw_" + n)

    outs = []
    for d in (grads, delta, new_m, new_v):
        outs += [_row_form(n, d[n]) for n in WEIGHTS]
    return (loss_sum, grad_x[None], *outs)
```

```python
from typing import Callable, NamedTuple

import jax
import jax.numpy as jnp
from jax import lax
from jax.experimental import pallas as pl
from jax.experimental.pallas import tpu as pltpu

F32 = jnp.float32
BF16 = jnp.bfloat16
RMS_EPS = 1e-6
POOL_WINDOWS = (2, 4, 8, 16)
POOL_GROUP = 128
POOL_HALO = 16
HEAD_DIM = 64
GQA_GROUP = 8
N_KV_HEADS = 2
ATTN_BLOCK = 128
SCALE = HEAD_DIM ** -0.5
NEG = -1e30
N_SHARDS = 4
ADAM_LR, ADAM_B1, ADAM_B2, ADAM_EPS, ADAM_WD, ADAM_STEP = 0.001, 0.9, 0.999, 1e-08, 0.01, 10
VMEM_LIMIT = 56 * 1024 * 1024
MESH = pl.DeviceIdType.MESH
SEG_POOL, SEG_Q, SEG_KV, SEG_GATE = (0, 512), (512, 1024), (1536, 256), (1792, 2048)
SEGMENTS = (SEG_POOL, SEG_Q, SEG_KV, SEG_GATE)


def _params(**kw):
    return pltpu.CompilerParams(vmem_limit_bytes=VMEM_LIMIT, **kw)


def _dot(a, b):
    return jnp.dot(a, b, preferred_element_type=F32)


def _dot_nt(a, b):
    return lax.dot_general(a, b, (((1,), (1,)), ((), ())), preferred_element_type=F32)


def _dot_tn(a, b):
    return lax.dot_general(a, b, (((0,), (0,)), ((), ())), preferred_element_type=F32)


def _rinv(x):
    return lax.rsqrt(jnp.mean(x * x, axis=-1, keepdims=True) + RMS_EPS)


def _rms_bwd(dn, x, g):
    r = _rinv(x)
    xr = x * r
    z = dn * g
    dx = r * (z - xr * jnp.mean(z * xr, axis=-1, keepdims=True))
    return dx, dn * xr


def _acc(ref, val, first):
    @pl.when(first)
    def _():
        ref[...] = val

    @pl.when(jnp.logical_not(first))
    def _():
        ref[...] += val


def _ffn_up(h, gnorm, wgt, wut, name, tm=512, tf=1408):
    S, D = h.shape
    F = wgt.shape[0]
    nt, nf = S // tm, F // tf

    def body(h_ref, g_ref, wg_ref, wu_ref, nt_ref, a_ref, b_ref, s_ref, n_ref):
        @pl.when(pl.program_id(1) == 0)
        def _():
            x = h_ref[...]
            nv = x * _rinv(x) * g_ref[...]
            n_ref[...] = nv.astype(BF16)
            nt_ref[...] = nv.T.astype(BF16)

        n = n_ref[...]
        a = _dot_nt(n, wg_ref[...])
        b = _dot_nt(n, wu_ref[...])
        a_ref[...] = a.astype(BF16)
        b_ref[...] = b.astype(BF16)
        s_ref[...] = (a * jax.nn.sigmoid(a) * b).astype(BF16)

    wblk = pl.BlockSpec((tf, D), lambda i, j: (j, 0))
    act = pl.BlockSpec((tm, tf), lambda i, j: (i, j))
    hidden = jax.ShapeDtypeStruct((S, F), BF16)
    return pl.pallas_call(
        body, name=name, grid=(nt, nf),
        in_specs=[pl.BlockSpec((tm, D), lambda i, j: (i, 0)), pl.BlockSpec((1, D), lambda i, j: (0, 0)), wblk, wblk],
        out_specs=[pl.BlockSpec((D, tm), lambda i, j: (0, i)), act, act, act],
        out_shape=[jax.ShapeDtypeStruct((D, S), BF16), hidden, hidden, hidden],
        scratch_shapes=[pltpu.VMEM((tm, D), BF16)],
        compiler_params=_params(dimension_semantics=("arbitrary", "arbitrary")),
    )(h, gnorm, wgt, wut)


def _ffn_down(h, s, wd, name, tm=1024):
    S, D = h.shape
    F = wd.shape[0]

    def body(h_ref, s_ref, wd_ref, o_ref):
        o_ref[...] = h_ref[...] + 0.5 * _dot(s_ref[...], wd_ref[...])

    row = pl.BlockSpec((tm, D), lambda i: (i, 0))
    return pl.pallas_call(
        body, name=name, grid=(S // tm,),
        in_specs=[row, pl.BlockSpec((tm, F), lambda i: (i, 0)), pl.BlockSpec((F, D), lambda i: (0, 0))],
        out_specs=row, out_shape=jax.ShapeDtypeStruct((S, D), F32),
        compiler_params=_params(dimension_semantics=("arbitrary",)),
    )(h, s, wd)


def _ffn_fwd_loss(h, gnorm, wgt, wut, wd, target, name, tm=512, tf=1408):
    S, D = h.shape
    F = wd.shape[0]
    nt, nf = S // tm, F // tf

    def body(h_ref, g_ref, wg_ref, wu_ref, wd_ref, t_ref, nt_ref, a_ref, b_ref, acc_ref, dyh_ref, dyht_ref, loss_ref,
             n_ref):
        i = pl.program_id(0)
        j = pl.program_id(1)

        @pl.when(j == 0)
        def _():
            x = h_ref[...]
            nv = x * _rinv(x) * g_ref[...]
            n_ref[...] = nv.astype(BF16)
            nt_ref[...] = nv.T.astype(BF16)

        n = n_ref[...]
        a = _dot_nt(n, wg_ref[...])
        b = _dot_nt(n, wu_ref[...])
        a_ref[...] = a.astype(BF16)
        b_ref[...] = b.astype(BF16)
        s = (a * jax.nn.sigmoid(a) * b).astype(BF16)
        _acc(acc_ref, _dot(s, wd_ref[...]), j == 0)

        @pl.when(j == nf - 1)
        def _():
            e = h_ref[...] + 0.5 * acc_ref[...] - t_ref[...]
            dy = e * (1.0 / D)
            acc_ref[...] = dy
            dyh_ref[...] = (0.5 * dy).astype(BF16)
            dyht_ref[...] = (0.5 * dy).T.astype(BF16)
            tot = jnp.sum(jnp.sum(e * e, axis=1, keepdims=True), axis=0, keepdims=True) * (0.5 / D)
            _acc(loss_ref, jnp.broadcast_to(tot, loss_ref.shape), i == 0)

    row = pl.BlockSpec((tm, D), lambda i, j: (i, 0))
    col = pl.BlockSpec((D, tm), lambda i, j: (0, i))
    wblk = pl.BlockSpec((tf, D), lambda i, j: (j, 0))
    act = pl.BlockSpec((tm, tf), lambda i, j: (i, j))
    return pl.pallas_call(
        body, name=name, grid=(nt, nf),
        in_specs=[row, pl.BlockSpec((1, D), lambda i, j: (0, 0)), wblk, wblk, wblk, row],
        out_specs=[col, act, act, row, row, col, pl.BlockSpec((8, 128), lambda i, j: (0, 0))],
        out_shape=[jax.ShapeDtypeStruct((D, S), BF16), jax.ShapeDtypeStruct((S, F), BF16), jax.ShapeDtypeStruct((S, F), BF16),
                   jax.ShapeDtypeStruct((S, D), F32), jax.ShapeDtypeStruct((S, D), BF16), jax.ShapeDtypeStruct((D, S), BF16),
                   jax.ShapeDtypeStruct((8, 128), F32)],
        scratch_shapes=[pltpu.VMEM((tm, D), BF16)],
        compiler_params=_params(dimension_semantics=("arbitrary", "arbitrary")),
    )(h, gnorm, wgt, wut, wd, target)


TOKEN_SPEC = pl.BlockSpec(memory_space=pl.ANY)


def _ffn_bwd_h(dyh, a, b, wd, token, name, tm=512, tf=1408):
    S, D = dyh.shape
    F = wd.shape[0]
    nt, nf = S // tm, F // tf

    def body(dyh_ref, a_ref, b_ref, wd_ref, _, da_ref, db_ref, s_ref):
        av = a_ref[...].astype(F32)
        bv = b_ref[...].astype(F32)
        ds = _dot_nt(dyh_ref[...], wd_ref[...])
        sig = jax.nn.sigmoid(av)
        silu = av * sig
        da_ref[...] = (ds * bv * (sig * (1.0 + av * (1.0 - sig)))).astype(BF16)
        db_ref[...] = (ds * silu).astype(BF16)
        s_ref[...] = (silu * bv).astype(BF16)

    act = pl.BlockSpec((tm, tf), lambda i, j: (i, j))
    hidden = jax.ShapeDtypeStruct((S, F), BF16)
    return pl.pallas_call(
        body, name=name, grid=(nt, nf),
        in_specs=[pl.BlockSpec((tm, D), lambda i, j: (i, 0)), act, act, pl.BlockSpec((tf, D), lambda i, j: (j, 0)), TOKEN_SPEC],
        out_specs=[act, act, act], out_shape=[hidden, hidden, hidden],
        compiler_params=_params(dimension_semantics=("arbitrary", "arbitrary")),
    )(dyh, a, b, wd, token)


def _xty(yt, x, name, tk=1024, tf=1408, chunk=128):
    D, S = yt.shape
    F = x.shape[1]
    nk, nf = S // tk, F // tf
    nchunk = tf // chunk

    def body(yt_ref, x_ref, out_hbm, acc, stage, sem):
        j = pl.program_id(0)
        k = pl.program_id(1)
        _acc(acc, _dot(yt_ref[...], x_ref[...]), k == 0)

        @pl.when(k == nk - 1)
        def _():
            def out_copy(c, slot):
                rows = pl.ds(pl.multiple_of(j * tf + c * chunk, chunk), chunk)
                return pltpu.make_async_copy(stage.at[slot], out_hbm.at[rows], sem.at[slot])

            for c in range(nchunk):
                slot = c % 2
                if c >= 2:
                    out_copy(c - 2, slot).wait()
                stage[slot] = acc[:, c * chunk:(c + 1) * chunk].T
                out_copy(c, slot).start()
            for c in range(nchunk - 2, nchunk):
                out_copy(c, c % 2).wait()

    return pl.pallas_call(
        body, name=name, grid=(nf, nk),
        in_specs=[pl.BlockSpec((D, tk), lambda j, k: (0, k)), pl.BlockSpec((tk, tf), lambda j, k: (k, j))],
        out_specs=pl.BlockSpec(memory_space=pl.ANY), out_shape=jax.ShapeDtypeStruct((F, D), F32),
        scratch_shapes=[pltpu.VMEM((D, tf), F32), pltpu.VMEM((2, chunk, D), F32), pltpu.SemaphoreType.DMA((2,))],
        compiler_params=_params(dimension_semantics=("arbitrary", "arbitrary")),
    )(yt, x)


def _ffn_bwd_x(da, db, wgt, wut, x, dh, gnorm, name, tm=512, tf=1408):
    S, D = x.shape
    F = wgt.shape[0]
    nt, nf = S // tm, F // tf

    def body(da_ref, db_ref, wg_ref, wu_ref, x_ref, dh_ref, g_ref, dx_ref, dg_ref):
        i = pl.program_id(0)
        j = pl.program_id(1)
        _acc(dx_ref, _dot(da_ref[...], wg_ref[...]) + _dot(db_ref[...], wu_ref[...]), j == 0)

        @pl.when(j == nf - 1)
        def _():
            dx, dg_rows = _rms_bwd(dx_ref[...], x_ref[...], g_ref[...])
            dx_ref[...] = dh_ref[...] + dx
            _acc(dg_ref, jnp.sum(dg_rows, axis=0, keepdims=True), i == 0)

    row = pl.BlockSpec((tm, D), lambda i, j: (i, 0))
    act = pl.BlockSpec((tm, tf), lambda i, j: (i, j))
    wblk = pl.BlockSpec((tf, D), lambda i, j: (j, 0))
    vec = pl.BlockSpec((1, D), lambda i, j: (0, 0))
    return pl.pallas_call(
        body, name=name, grid=(nt, nf), in_specs=[act, act, wblk, wblk, row, row, vec],
        out_specs=[row, vec],
        out_shape=[jax.ShapeDtypeStruct((S, D), F32), jax.ShapeDtypeStruct((1, D), F32)],
        compiler_params=_params(dimension_semantics=("arbitrary", "arbitrary")),
    )(da, db, wgt, wut, x, dh, gnorm)


def _mix_proj(h, gnorm, wint, tm=512):
    S, D = h.shape
    nt = S // tm

    def body(h_ref, g_ref, w_ref, ut_ref, xp_ref, q_ref, kv_ref, gl_ref):
        x = h_ref[...]
        uv = x * _rinv(x) * g_ref[...]
        ut_ref[...] = uv.T.astype(BF16)
        u = uv.astype(BF16)
        for (off, width), ref in zip(SEGMENTS, (xp_ref, q_ref, kv_ref, gl_ref)):
            ref[...] = _dot_nt(u, w_ref[off:off + width, :]).astype(ref.dtype)

    def row(width):
        return pl.BlockSpec((tm, width), lambda i: (i, 0))

    return pl.pallas_call(
        body, name="mix_proj", grid=(nt,),
        in_specs=[row(D), pl.BlockSpec((1, D), lambda i: (0, 0)), pl.BlockSpec(wint.shape, lambda i: (0, 0))],
        out_specs=[pl.BlockSpec((D, tm), lambda i: (0, i)), row(SEG_POOL[1]), row(SEG_Q[1]), row(SEG_KV[1]), row(SEG_GATE[1])],
        out_shape=[jax.ShapeDtypeStruct((D, S), BF16), jax.ShapeDtypeStruct((S, SEG_POOL[1]), F32),
                   jax.ShapeDtypeStruct((S, SEG_Q[1]), BF16), jax.ShapeDtypeStruct((S, SEG_KV[1]), BF16),
                   jax.ShapeDtypeStruct((S, SEG_GATE[1]), BF16)],
        compiler_params=_params(dimension_semantics=("arbitrary",)),
    )(h, gnorm, wint)


def _stack_heads(x, g):
    return jnp.concatenate([x[:, (GQA_GROUP * g + hh) * HEAD_DIM:(GQA_GROUP * g + hh + 1) * HEAD_DIM]
                            for hh in range(GQA_GROUP)], axis=0)


def _unstack_heads(ref, val, g):
    for hh in range(GQA_GROUP):
        lo = (GQA_GROUP * g + hh) * HEAD_DIM
        ref[:, lo:lo + HEAD_DIM] = val[hh * ATTN_BLOCK:(hh + 1) * ATTN_BLOCK, :]


def _rowsum(xb, width):
    return _dot(xb, jnp.ones((xb.shape[1], width), BF16))


def _rinv_lanes(x):
    return lax.rsqrt(_rowsum((x * x).astype(BF16), x.shape[1]) * (1.0 / x.shape[1]) + RMS_EPS)


def _attn_exp(qn, kc, kp, sink, n):
    rows = GQA_GROUP * ATTN_BLOCK
    qi = lax.broadcasted_iota(jnp.int32, (rows, ATTN_BLOCK), 0) % ATTN_BLOCK
    kj = lax.broadcasted_iota(jnp.int32, (rows, ATTN_BLOCK), 1)
    s_c = jnp.where(kj <= qi, _dot_nt(qn, kc), NEG)
    s_p = jnp.where(jnp.logical_and(kj > qi, n > 0), _dot_nt(qn, kp), NEG)
    m = jnp.maximum(jnp.max(s_c, axis=-1, keepdims=True), jnp.max(s_p, axis=-1, keepdims=True))
    m = jnp.maximum(jnp.broadcast_to(m, sink.shape), sink)
    e_c = jnp.exp(s_c - m)
    e_p = jnp.exp(s_p - m)
    e_s = jnp.exp(sink - m)
    e_cb, e_pb = e_c.astype(BF16), e_p.astype(BF16)
    inv = 1.0 / (_rowsum(jnp.concatenate([e_cb, e_pb], axis=1), ATTN_BLOCK) + e_s)
    return e_cb, e_pb, e_c, e_p, e_s, inv


def _attn_blocks(n):
    cur = pl.multiple_of(n * ATTN_BLOCK, ATTN_BLOCK)
    prev = pl.multiple_of(jnp.maximum(n - 1, 0) * ATTN_BLOCK, ATTN_BLOCK)
    return cur, prev


def _kv_split(kv, g):
    k = kv[:, g * HEAD_DIM:(g + 1) * HEAD_DIM]
    v = kv[:, (N_KV_HEADS + g) * HEAD_DIM:(N_KV_HEADS + g + 1) * HEAD_DIM]
    return k, v


def _attn_fwd(q, kv, qw, kw, sink_rows):
    S, W = q.shape
    nb = S // ATTN_BLOCK

    def body(q_ref, kv_ref, qw_ref, kw_ref, sk_ref, o_ref, o_scr):
        n = pl.program_id(0)
        qf = q_ref[...].astype(F32)
        cur, prev = _attn_blocks(n)
        kvc = kv_ref[pl.ds(cur, ATTN_BLOCK), :].astype(F32)
        kvp = kv_ref[pl.ds(prev, ATTN_BLOCK), :].astype(F32)
        for g in range(N_KV_HEADS):
            qs = _stack_heads(qf, g)
            qn = (qs * _rinv_lanes(qs) * qw_ref[...] * SCALE).astype(BF16)
            kc, vc = _kv_split(kvc, g)
            kp, vp = _kv_split(kvp, g)
            kc = (kc * _rinv_lanes(kc) * kw_ref[...]).astype(BF16)
            kp = (kp * _rinv_lanes(kp) * kw_ref[...]).astype(BF16)
            e_cb, e_pb, _, _, _, inv = _attn_exp(qn, kc, kp, sk_ref[g], n)
            o = (_dot(e_cb, vc.astype(BF16)) + _dot(e_pb, vp.astype(BF16))) * inv[:, :HEAD_DIM]
            _unstack_heads(o_scr, o, g)
        o_ref[...] = o_scr[...].astype(BF16)

    blk = pl.BlockSpec((ATTN_BLOCK, W), lambda n: (n, 0))
    return pl.pallas_call(
        body, name="attn_fwd", grid=(nb,),
        in_specs=[blk, pl.BlockSpec(kv.shape, lambda n: (0, 0)), pl.BlockSpec((1, HEAD_DIM), lambda n: (0, 0)),
                  pl.BlockSpec((1, HEAD_DIM), lambda n: (0, 0)), pl.BlockSpec(sink_rows.shape, lambda n: (0, 0, 0))],
        out_specs=blk, out_shape=jax.ShapeDtypeStruct((S, W), BF16),
        scratch_shapes=[pltpu.VMEM((ATTN_BLOCK, W), F32)],
        compiler_params=_params(dimension_semantics=("arbitrary",)),
    )(q, kv, qw, kw, sink_rows)


def _attn_bwd(q, kv, do, qw, kw, sink_rows):
    S, W = q.shape
    KW = kv.shape[1]
    nb = S // ATTN_BLOCK
    chunk = 512

    def body(q_ref, kv_ref, do_ref, qw_ref, kw_ref, sk_ref, dq_ref, dkv_ref, dqw_ref, dkw_ref, dsk_ref, dq_scr):
        n = pl.program_id(0)

        @pl.when(n == 0)
        def _():
            dkv_ref[...] = jnp.zeros_like(dkv_ref)
            dqw_ref[...] = jnp.zeros_like(dqw_ref)
            dsk_ref[...] = jnp.zeros_like(dsk_ref)

        qf = q_ref[...].astype(F32)
        dof = do_ref[...].astype(F32)
        cur, prev = _attn_blocks(n)
        kvc = kv_ref[pl.ds(cur, ATTN_BLOCK), :].astype(F32)
        kvp = kv_ref[pl.ds(prev, ATTN_BLOCK), :].astype(F32)
        qw_v = qw_ref[...]
        for g in range(N_KV_HEADS):
            qs = _stack_heads(qf, g)
            rq = _rinv_lanes(qs)
            qhat = qs * rq
            qn = (qhat * qw_v * SCALE).astype(BF16)
            kc, vc = _kv_split(kvc, g)
            kp, vp = _kv_split(kvp, g)
            kc = (kc * _rinv_lanes(kc) * kw_ref[...]).astype(BF16)
            kp = (kp * _rinv_lanes(kp) * kw_ref[...]).astype(BF16)
            vc = vc.astype(BF16)
            vp = vp.astype(BF16)
            dos = _stack_heads(dof, g).astype(BF16)
            _, _, e_c, e_p, e_s, inv = _attn_exp(qn, kc, kp, sk_ref[g], n)
            p_c, p_p, p_s = e_c * inv, e_p * inv, e_s * inv
            dp_c = _dot_nt(dos, vc)
            dp_p = _dot_nt(dos, vp)
            drow = _rowsum(jnp.concatenate([(p_c * dp_c).astype(BF16), (p_p * dp_p).astype(BF16)], axis=1), ATTN_BLOCK)
            ds_c = (p_c * (dp_c - drow)).astype(BF16)
            ds_p = (p_p * (dp_p - drow)).astype(BF16)
            dsink = -(p_s * drow)
            for hh in range(GQA_GROUP):
                tot = jnp.sum(dsink[hh * ATTN_BLOCK:(hh + 1) * ATTN_BLOCK, :], axis=0, keepdims=True)
                dsk_ref[g, hh:hh + 1, :] += tot
            dqn = (_dot(ds_c, kc) + _dot(ds_p, kp)) * SCALE
            klo, vlo = g * HEAD_DIM, (N_KV_HEADS + g) * HEAD_DIM
            dkv_ref[pl.ds(cur, ATTN_BLOCK), klo:klo + HEAD_DIM] += _dot_tn(ds_c, qn)
            dkv_ref[pl.ds(prev, ATTN_BLOCK), klo:klo + HEAD_DIM] += _dot_tn(ds_p, qn)
            dkv_ref[pl.ds(cur, ATTN_BLOCK), vlo:vlo + HEAD_DIM] += _dot_tn(p_c.astype(BF16), dos)
            dkv_ref[pl.ds(prev, ATTN_BLOCK), vlo:vlo + HEAD_DIM] += _dot_tn(p_p.astype(BF16), dos)
            dqw_ref[...] += jnp.sum(dqn * qhat, axis=0, keepdims=True)
            z = dqn * qw_v
            dqs = rq * (z - qhat * (_rowsum((z * qhat).astype(BF16), HEAD_DIM) * (1.0 / HEAD_DIM)))
            _unstack_heads(dq_scr, dqs, g)
        dq_ref[...] = dq_scr[...].astype(BF16)

        @pl.when(n == nb - 1)
        def _():
            def one(c, dkw):
                rows = pl.ds(pl.multiple_of(c * chunk, chunk), chunk)
                for g in range(N_KV_HEADS):
                    lo = g * HEAD_DIM
                    k = kv_ref[rows, lo:lo + HEAD_DIM].astype(F32)
                    dx, dg_rows = _rms_bwd(dkv_ref[rows, lo:lo + HEAD_DIM], k, kw_ref[...])
                    dkv_ref[rows, lo:lo + HEAD_DIM] = dx
                    dkw = dkw + jnp.sum(dg_rows, axis=0, keepdims=True)
                return dkw

            dkw_ref[...] = lax.fori_loop(0, S // chunk, one, jnp.zeros((1, HEAD_DIM), F32))

    blk = pl.BlockSpec((ATTN_BLOCK, W), lambda n: (n, 0))
    whole_kv = pl.BlockSpec((S, KW), lambda n: (0, 0))
    vec = pl.BlockSpec((1, HEAD_DIM), lambda n: (0, 0))
    sk = pl.BlockSpec(sink_rows.shape, lambda n: (0, 0, 0))
    dsk = pl.BlockSpec((N_KV_HEADS, GQA_GROUP, 128), lambda n: (0, 0, 0))
    return pl.pallas_call(
        body, name="attn_bwd", grid=(nb,), in_specs=[blk, whole_kv, blk, vec, vec, sk],
        out_specs=[blk, whole_kv, vec, vec, dsk],
        out_shape=[jax.ShapeDtypeStruct((S, W), BF16), jax.ShapeDtypeStruct((S, KW), F32),
                   jax.ShapeDtypeStruct((1, HEAD_DIM), F32), jax.ShapeDtypeStruct((1, HEAD_DIM), F32),
                   jax.ShapeDtypeStruct((N_KV_HEADS, GQA_GROUP, 128), F32)],
        scratch_shapes=[pltpu.VMEM((ATTN_BLOCK, W), F32)],
        compiler_params=_params(dimension_semantics=("arbitrary",)),
    )(q, kv, do, qw, kw, sink_rows)


def _pooled(xc, xprev, i):
    tm = xc.shape[0]
    xh = jnp.concatenate([jnp.where(i > 0, xprev, 0.0), xc], axis=0)
    t = lax.broadcasted_iota(jnp.int32, (tm, 1), 0) + i * tm
    out = []
    for gi, w in enumerate(POOL_WINDOWS):
        acc = xh[:, gi * POOL_GROUP:(gi + 1) * POOL_GROUP]
        sh = 1
        while sh < w:
            acc = acc + pltpu.roll(acc, sh, 0)
            sh *= 2
        cnt = jnp.minimum(t + 1, w).astype(F32)
        out.append(acc[POOL_HALO:, :] / cnt - xc[:, gi * POOL_GROUP:(gi + 1) * POOL_GROUP])
    return jnp.concatenate(out, axis=1)


def _pool_mix(pooled_b, pw_ref):
    return jnp.concatenate([_dot(pooled_b[:, gi * POOL_GROUP:(gi + 1) * POOL_GROUP], pw_ref[gi])
                            for gi in range(len(POOL_WINDOWS))], axis=1)


def _halo_specs(tm, width, S, after):
    per = tm // POOL_HALO
    last = S // POOL_HALO - 1
    if after:
        return pl.BlockSpec((POOL_HALO, width), lambda i: (jnp.minimum((i + 1) * per, last), 0))
    return pl.BlockSpec((POOL_HALO, width), lambda i: (jnp.maximum(i * per - 1, 0), 0))


def _mix_out(xp, attn, gl, bias, pw, pscale, wpot, wao, wo, h, tm=512):
    S, D = h.shape
    nt = S // tm
    PW = xp.shape[1]

    def body(xc_ref, xprev_ref, at_ref, gl_ref, bias_ref, pw_ref, ps_ref, wpo_ref, wao_ref, wo_ref, h_ref,
             ho_ref, bp_ref, ba_ref):
        i = pl.program_id(0)
        pooled = _pooled(xc_ref[...], xprev_ref[...], i).astype(BF16)
        ms = (_pool_mix(pooled, pw_ref) * ps_ref[...]).astype(BF16)
        bp = _dot_nt(ms, wpo_ref[...])
        ba = _dot(at_ref[...], wao_ref[...])
        bp_ref[...] = bp.astype(BF16)
        ba_ref[...] = ba.astype(BF16)
        gates = jax.nn.sigmoid(gl_ref[...].astype(F32) + bias_ref[...])
        merged = (gates[:, :D] * bp + gates[:, D:] * ba).astype(BF16)
        ho_ref[...] = h_ref[...] + _dot(merged, wo_ref[...])

    def row(width):
        return pl.BlockSpec((tm, width), lambda i: (i, 0))

    def whole(x):
        nd = x.ndim
        return pl.BlockSpec(x.shape, lambda i: (0,) * nd)

    return pl.pallas_call(
        body, name="mix_out", grid=(nt,),
        in_specs=[row(PW), _halo_specs(tm, PW, S, False), row(D), row(2 * D), whole(bias), whole(pw), whole(pscale),
                  whole(wpot), whole(wao), whole(wo), row(D)],
        out_specs=[row(D), row(D), row(D)],
        out_shape=[jax.ShapeDtypeStruct((S, D), F32), jax.ShapeDtypeStruct((S, D), BF16),
                   jax.ShapeDtypeStruct((S, D), BF16)],
        compiler_params=_params(dimension_semantics=("arbitrary",)),
    )(xp, xp, attn, gl, bias, pw, pscale, wpot, wao, wo, h)


def _mix_bwd_gate(dh, wo, bp, ba, gl, bias, tm=512):
    S, D = dh.shape
    nt = S // tm

    def body(dh_ref, wo_ref, bp_ref, ba_ref, gl_ref, bias_ref, dgl_ref, dbp_ref, dba_ref, dwo_ref, dbias_ref):
        i = pl.program_id(0)
        dhb = dh_ref[...].astype(BF16)
        dm = _dot_nt(dhb, wo_ref[...])
        gates = jax.nn.sigmoid(gl_ref[...].astype(F32) + bias_ref[...])
        gp, ga = gates[:, :D], gates[:, D:]
        bp_v = bp_ref[...].astype(F32)
        ba_v = ba_ref[...].astype(F32)
        merged = (gp * bp_v + ga * ba_v).astype(BF16)
        _acc(dwo_ref, _dot_tn(merged, dhb), i == 0)
        dbp_ref[...] = (dm * gp).astype(BF16)
        dba_ref[...] = (dm * ga).astype(BF16)
        dgl = jnp.concatenate([dm * bp_v * gp * (1.0 - gp), dm * ba_v * ga * (1.0 - ga)], axis=1)
        dgl_ref[...] = dgl.astype(BF16)
        _acc(dbias_ref, jnp.sum(dgl, axis=0, keepdims=True), i == 0)

    def row(width):
        return pl.BlockSpec((tm, width), lambda i: (i, 0))

    def whole(shape):
        return pl.BlockSpec(shape, lambda i: (0, 0))

    return pl.pallas_call(
        body, name="mix_bwd_gate", grid=(nt,),
        in_specs=[row(D), whole(wo.shape), row(D), row(D), row(2 * D), whole(bias.shape)],
        out_specs=[row(2 * D), row(D), row(D), whole((D, D)), whole((1, 2 * D))],
        out_shape=[jax.ShapeDtypeStruct((S, 2 * D), BF16), jax.ShapeDtypeStruct((S, D), BF16),
                   jax.ShapeDtypeStruct((S, D), BF16), jax.ShapeDtypeStruct((D, D), F32),
                   jax.ShapeDtypeStruct((1, 2 * D), F32)],
        compiler_params=_params(dimension_semantics=("arbitrary",)),
    )(dh, wo, bp, ba, gl, bias)


def _mix_bwd_branch(dbp, dba, attn, xp, pw, pscale, wpot, wao, tm=512):
    S, D = dbp.shape
    nt = S // tm
    PW = xp.shape[1]
    NG = len(POOL_WINDOWS)

    def body(dbp_ref, dba_ref, at_ref, xc_ref, xprev_ref, pw_ref, ps_ref, wpo_ref, wao_ref,
             dat_ref, dpl_ref, dwao_ref, dwpo_ref, dpw_ref, dps_ref):
        i = pl.program_id(0)
        dba_v = dba_ref[...]
        dbp_v = dbp_ref[...]
        _acc(dwao_ref, _dot_tn(at_ref[...], dba_v), i == 0)
        dat_ref[...] = _dot_nt(dba_v, wao_ref[...]).astype(BF16)
        pooled = _pooled(xc_ref[...], xprev_ref[...], i).astype(BF16)
        mixed = _pool_mix(pooled, pw_ref)
        ps = ps_ref[...]
        _acc(dwpo_ref, _dot_tn(dbp_v, (mixed * ps).astype(BF16)), i == 0)
        dms = _dot(dbp_v, wpo_ref[...])
        _acc(dps_ref, jnp.sum(dms * mixed, axis=0, keepdims=True), i == 0)
        dmixed = (dms * ps).astype(BF16)
        dpooled = []
        for gi in range(NG):
            cols = slice(gi * POOL_GROUP, (gi + 1) * POOL_GROUP)
            _acc(dpw_ref.at[gi], _dot_tn(pooled[:, cols], dmixed[:, cols]), i == 0)
            dpooled.append(_dot_nt(dmixed[:, cols], pw_ref[gi]))
        dpl_ref[...] = jnp.concatenate(dpooled, axis=1)

    def row(width):
        return pl.BlockSpec((tm, width), lambda i: (i, 0))

    def whole(shape):
        nd = len(shape)
        return pl.BlockSpec(shape, lambda i: (0,) * nd)

    return pl.pallas_call(
        body, name="mix_bwd_branch", grid=(nt,),
        in_specs=[row(D), row(D), row(D), row(PW), _halo_specs(tm, PW, S, False), whole(pw.shape), whole(pscale.shape),
                  whole(wpot.shape), whole(wao.shape)],
        out_specs=[row(D), row(PW), whole((D, D)), whole((D, PW)), whole(pw.shape), whole((1, PW))],
        out_shape=[jax.ShapeDtypeStruct((S, D), BF16), jax.ShapeDtypeStruct((S, PW), F32),
                   jax.ShapeDtypeStruct((D, D), F32), jax.ShapeDtypeStruct((D, PW), F32),
                   jax.ShapeDtypeStruct(pw.shape, F32), jax.ShapeDtypeStruct((1, PW), F32)],
        compiler_params=_params(dimension_semantics=("arbitrary",)),
    )(dbp, dba, attn, xp, xp, pw, pscale, wpot, wao)


def _pool_bwd(dpooled, tm=512):
    S, PW = dpooled.shape
    nt = S // tm

    def body(dc_ref, dnext_ref, dxp_ref):
        i = pl.program_id(0)
        dc = dc_ref[...]
        dh = jnp.concatenate([dc, jnp.where(i < nt - 1, dnext_ref[...], 0.0)], axis=0)
        rows = tm + POOL_HALO
        t = lax.broadcasted_iota(jnp.int32, (rows, 1), 0) + i * tm
        out = []
        for gi, w in enumerate(POOL_WINDOWS):
            cols = slice(gi * POOL_GROUP, (gi + 1) * POOL_GROUP)
            acc = dh[:, cols] / jnp.minimum(t + 1, w).astype(F32)
            sh = 1
            while sh < w:
                acc = acc + pltpu.roll(acc, rows - sh, 0)
                sh *= 2
            out.append(acc[:tm, :] - dc[:, cols])
        dxp_ref[...] = jnp.concatenate(out, axis=1).astype(BF16)

    return pl.pallas_call(
        body, name="pool_bwd", grid=(nt,),
        in_specs=[pl.BlockSpec((tm, PW), lambda i: (i, 0)), _halo_specs(tm, PW, S, True)],
        out_specs=pl.BlockSpec((tm, PW), lambda i: (i, 0)), out_shape=jax.ShapeDtypeStruct((S, PW), BF16),
        compiler_params=_params(dimension_semantics=("arbitrary",)),
    )(dpooled, dpooled)


def _in_bwd_w(pieces, ut, tm=1024, chunk=256):
    D, S = ut.shape
    nt = S // tm
    NW = sum(width for _, width in SEGMENTS)
    nchunk = NW // chunk

    def body(*refs):
        piece_refs, ut_ref, dw_hbm, acc, stage, sem = (refs[:len(SEGMENTS)], refs[len(SEGMENTS)], refs[len(SEGMENTS) + 1],
                                                      refs[-3], refs[-2], refs[-1])
        i = pl.program_id(0)
        u_t = ut_ref[...]
        for (off, width), ref in zip(SEGMENTS, piece_refs):
            for lo in range(0, width, D):
                hi = min(lo + D, width)
                _acc(acc.at[:, off + lo:off + hi], _dot(u_t, ref[:, lo:hi].astype(BF16)), i == 0)

        @pl.when(i == nt - 1)
        def _():
            def out_copy(c, slot):
                return pltpu.make_async_copy(stage.at[slot], dw_hbm.at[pl.ds(c * chunk, chunk)], sem.at[slot])

            for c in range(nchunk):
                slot = c % 2
                if c >= 2:
                    out_copy(c - 2, slot).wait()
                stage[slot] = acc[:, c * chunk:(c + 1) * chunk].T
                out_copy(c, slot).start()
            for c in range(nchunk - 2, nchunk):
                out_copy(c, c % 2).wait()

    return pl.pallas_call(
        body, name="in_bwd_w", grid=(nt,),
        in_specs=[pl.BlockSpec((tm, width), lambda i: (i, 0)) for _, width in SEGMENTS] + [pl.BlockSpec((D, tm), lambda i: (0, i))],
        out_specs=pl.BlockSpec(memory_space=pl.ANY), out_shape=jax.ShapeDtypeStruct((NW, D), F32),
        scratch_shapes=[pltpu.VMEM((D, NW), F32), pltpu.VMEM((2, chunk, D), F32), pltpu.SemaphoreType.DMA((2,))],
        compiler_params=_params(dimension_semantics=("arbitrary",)),
    )(*pieces, ut)


def _in_bwd_x(pieces, wint, h, dh, gnorm, tm=512):
    S, D = h.shape
    nt = S // tm

    def body(*refs):
        piece_refs = refs[:len(SEGMENTS)]
        w_ref, h_ref, dh_ref, g_ref, dx_ref, dxh_ref, dxht_ref, dg_ref = refs[len(SEGMENTS):]
        i = pl.program_id(0)
        du = jnp.zeros((tm, D), F32)
        for (off, width), ref in zip(SEGMENTS, piece_refs):
            du = du + _dot(ref[...].astype(BF16), w_ref[off:off + width, :])
        dx, dg_rows = _rms_bwd(du, h_ref[...], g_ref[...])
        out = dh_ref[...] + dx
        dx_ref[...] = out
        dxh_ref[...] = (0.5 * out).astype(BF16)
        dxht_ref[...] = (0.5 * out).T.astype(BF16)
        _acc(dg_ref, jnp.sum(dg_rows, axis=0, keepdims=True), i == 0)

    def row(width):
        return pl.BlockSpec((tm, width), lambda i: (i, 0))

    vec = pl.BlockSpec((1, D), lambda i: (0, 0))
    return pl.pallas_call(
        body, name="in_bwd_x", grid=(nt,),
        in_specs=[row(width) for _, width in SEGMENTS] + [pl.BlockSpec(wint.shape, lambda i: (0, 0)), row(D), row(D), vec],
        out_specs=[row(D), row(D), pl.BlockSpec((D, tm), lambda i: (0, i)), vec],
        out_shape=[jax.ShapeDtypeStruct((S, D), F32), jax.ShapeDtypeStruct((S, D), BF16),
                   jax.ShapeDtypeStruct((D, S), BF16), jax.ShapeDtypeStruct((1, D), F32)],
        compiler_params=_params(dimension_semantics=("arbitrary",)),
    )(*pieces, wint, h, dh, gnorm)


def _row_tile(rows):
    for t in (512, 480, 352, 256, 128, 64, 32, 16, 8):
        if rows % t == 0:
            return t
    return rows


def _adamw(w, g, m, v, token, name):
    R, C = w.shape
    tr = _row_tile(R)

    def body(w_ref, g_ref, m_ref, v_ref, _, d_ref, mo_ref, vo_ref, go_ref):
        gv = g_ref[...]
        go_ref[...] = gv
        mn = ADAM_B1 * m_ref[...] + (1.0 - ADAM_B1) * gv
        vn = ADAM_B2 * v_ref[...] + (1.0 - ADAM_B2) * (gv * gv)
        m_hat = mn / (1.0 - ADAM_B1 ** ADAM_STEP)
        v_hat = vn / (1.0 - ADAM_B2 ** ADAM_STEP)
        d_ref[...] = -ADAM_LR * (m_hat / (jnp.sqrt(v_hat) + ADAM_EPS) + ADAM_WD * w_ref[...])
        mo_ref[...] = mn
        vo_ref[...] = vn

    blk = pl.BlockSpec((tr, C), lambda i: (i, 0))
    sh = jax.ShapeDtypeStruct((R, C), F32)
    return pl.pallas_call(
        body, name=name, grid=(R // tr,), in_specs=[blk] * 4 + [TOKEN_SPEC], out_specs=[blk] * 4, out_shape=[sh] * 4,
        compiler_params=_params(dimension_semantics=("arbitrary",)),
    )(w, g, m, v, token)


def _cast_place(w, place, name):
    R, C = w.shape
    tr = _row_tile(R)
    per = R // tr

    def body(p_ref, w_ref, o_ref):
        o_ref[...] = w_ref[...].astype(BF16)

    grid_spec = pltpu.PrefetchScalarGridSpec(
        num_scalar_prefetch=1, grid=(per,),
        in_specs=[pl.BlockSpec((tr, C), lambda i, p: (i, 0))],
        out_specs=pl.BlockSpec((tr, C), lambda i, p: (p[0] * per + i, 0)))
    return pl.pallas_call(
        body, name=name, grid_spec=grid_spec, out_shape=jax.ShapeDtypeStruct((N_SHARDS * R, C), BF16),
        compiler_params=_params(dimension_semantics=("arbitrary",)),
    )(place, w)


def _sum_halves(g4, recv, place, name):
    NS, R, C = g4.shape
    hr = R // 2
    tr = _row_tile(hr)
    per = hr // tr

    def body(p_ref, g_ref, r_ref, o_ref):
        o_ref[...] = (g_ref[...] + r_ref[...]).astype(BF16)

    grid_spec = pltpu.PrefetchScalarGridSpec(
        num_scalar_prefetch=1, grid=(NS, per),
        in_specs=[pl.BlockSpec((1, tr, C), lambda s, i, p: (s, p[1] * per + i, 0)),
                  pl.BlockSpec((1, tr, C), lambda s, i, p: (s, i, 0))],
        out_specs=pl.BlockSpec((1, tr, C), lambda s, i, p: (s, i, 0)))
    return pl.pallas_call(
        body, name=name, grid_spec=grid_spec, out_shape=jax.ShapeDtypeStruct((NS, hr, C), BF16),
        compiler_params=_params(dimension_semantics=("arbitrary", "arbitrary")),
    )(place, g4, recv)


def _sum_quarters(h4, recv3, place, name):
    NS, hr, C = h4.shape
    tr = _row_tile(hr)
    per = hr // tr

    def body(p_ref, h_ref, r_ref, o_ref):
        acc = h_ref[0].astype(F32)
        for k in range(N_SHARDS - 1):
            acc = acc + r_ref[k].astype(F32)
        o_ref[...] = acc

    grid_spec = pltpu.PrefetchScalarGridSpec(
        num_scalar_prefetch=1, grid=(per,),
        in_specs=[pl.BlockSpec((1, tr, C), lambda i, p: (p[0], i, 0)),
                  pl.BlockSpec((N_SHARDS - 1, tr, C), lambda i, p: (0, i, 0))],
        out_specs=pl.BlockSpec((tr, C), lambda i, p: (p[1] * per + i, 0)))
    return pl.pallas_call(
        body, name=name, grid_spec=grid_spec, out_shape=jax.ShapeDtypeStruct((2 * hr, C), F32),
        compiler_params=_params(dimension_semantics=("arbitrary",)),
    )(place, h4, recv3)


def _place():
    x, y, c = lax.axis_index("x"), lax.axis_index("y"), lax.axis_index("c")
    chips = [(1 - x, y), (x, 1 - y), (1 - x, 1 - y)]
    return x, y, c, chips


HBM_SPEC = pl.BlockSpec(memory_space=pltpu.HBM)
SEM_SPEC = pl.BlockSpec(memory_space=pltpu.SEMAPHORE)
DATAFLOW = pltpu.SideEffectType.DATAFLOW_SIDE_EFFECTING


def _hbm(a):
    return pltpu.with_memory_space_constraint(a, pltpu.HBM)


class InFlight(NamedTuple):
    send_sem: jax.Array
    recv_sem: jax.Array
    bufs: list
    plan: Callable
    token: jax.Array


def _wait_all(plan, refs, send_ref, recv_ref):
    for k, (src, dst, dev) in enumerate(plan(refs)):
        cp = pltpu.make_async_remote_copy(src_ref=src, dst_ref=dst, send_sem=send_ref.at[k], recv_sem=recv_ref.at[k],
                                          device_id=dev, device_id_type=MESH)
        cp.wait_send()
        cp.wait_recv()


def _start_all(plan, refs, send_ref, recv_ref):
    for k, (src, dst, dev) in enumerate(plan(refs)):
        pltpu.make_async_remote_copy(src_ref=src, dst_ref=dst, send_sem=send_ref.at[k], recv_sem=recv_ref.at[k],
                                     device_id=dev, device_id_type=MESH).start()


def _split_start(name, bufs, plan, n_copies, after, waits=None):
    nb = len(bufs)
    after = list(after) if isinstance(after, (list, tuple)) else [after]

    def body(*refs):
        ins = refs[:nb]
        if waits is not None:
            _wait_all(waits.plan, ins, refs[nb], refs[nb + 1])
        send_sem, recv_sem, token = refs[-nb - 3], refs[-nb - 2], refs[-1]
        _start_all(plan, ins, send_sem, recv_sem)
        token[...] = jnp.zeros_like(token)

    sems = pltpu.SemaphoreType.DMA((n_copies,))
    earlier = [] if waits is None else [waits.send_sem, waits.recv_sem]
    out = pl.pallas_call(
        body, name=name, in_specs=[HBM_SPEC] * nb + [SEM_SPEC] * len(earlier) + [TOKEN_SPEC] * len(after),
        out_shape=(sems, sems, *[pltpu.HBM(b.shape, b.dtype) for b in bufs], jax.ShapeDtypeStruct((8, 128), F32)),
        out_specs=(SEM_SPEC, SEM_SPEC, *[HBM_SPEC] * nb, pl.BlockSpec(memory_space=pltpu.VMEM)),
        input_output_aliases={i: i + 2 for i in range(nb)},
        compiler_params=pltpu.CompilerParams(has_side_effects=DATAFLOW),
    )(*[_hbm(b) for b in bufs], *earlier, *after)
    return InFlight(out[0], out[1], list(out[2:2 + nb]), plan, out[-1])


def _split_wait(name, flight, after):
    nb = len(flight.bufs)

    def body(*refs):
        _wait_all(flight.plan, refs[:nb], refs[nb], refs[nb + 1])

    out = pl.pallas_call(
        body, name=name, in_specs=[HBM_SPEC] * nb + [SEM_SPEC, SEM_SPEC, pl.BlockSpec(memory_space=pl.ANY)],
        out_shape=[pltpu.HBM(b.shape, b.dtype) for b in flight.bufs], out_specs=[HBM_SPEC] * nb,
        input_output_aliases={i: i for i in range(nb)},
        compiler_params=pltpu.CompilerParams(has_side_effects=DATAFLOW),
    )(*flight.bufs, flight.send_sem, flight.recv_sem, after)
    return list(out)


def _half_rows(buf, chip, core):
    hr = buf.shape[0] // (2 * N_SHARDS)
    return buf.at[pl.ds(pl.multiple_of((2 * chip + core) * hr, 16), hr)]


def _gather_ici_plan(bufs):
    x, y, c, chips = _place()
    return [(_half_rows(b, 2 * x + y, c), _half_rows(b, 2 * x + y, c), (px, py, c)) for b in bufs for px, py in chips]


def _gather_d2d_plan(bufs):
    x, y, c, chips = _place()
    return [(_half_rows(b, 2 * px + py, c), _half_rows(b, 2 * px + py, c), (x, y, 1 - c)) for b in bufs for px, py in chips]


def _swap_plan(bufs):
    x, y, c, _ = _place()
    n = len(bufs) // 2
    copies = []
    for g, land in zip(bufs[:n], bufs[n:]):
        hr = g.shape[1] // 2
        copies.append((g.at[:, pl.ds(pl.multiple_of((1 - c) * hr, 8), hr)], land, (x, y, 1 - c)))
    return copies


def _exchange_plan(bufs):
    x, y, c, chips = _place()
    n = len(bufs) // 2
    return [(h.at[2 * px + py], land.at[k], (px, py, c))
            for h, land in zip(bufs[:n], bufs[n:]) for k, (px, py) in enumerate(chips)]


def _share_plan(bufs):
    x, y, c, _ = _place()
    copies = []
    for buf in bufs:
        hr = buf.shape[0] // 2
        mine = buf.at[pl.ds(pl.multiple_of(c * hr, 8), hr)]
        copies.append((mine, mine, (x, y, 1 - c)))
    return copies


N_DEVICES = 8


def _slot_place(vec, place):
    R, C = vec.shape

    def body(p_ref, v_ref, o_ref):
        o_ref[0] = v_ref[...]

    grid_spec = pltpu.PrefetchScalarGridSpec(
        num_scalar_prefetch=1, grid=(1,), in_specs=[pl.BlockSpec((R, C), lambda i, p: (0, 0))],
        out_specs=pl.BlockSpec((1, R, C), lambda i, p: (2 * p[0] + p[1], 0, 0)))
    return pl.pallas_call(
        body, name="slot_place", grid_spec=grid_spec, out_shape=jax.ShapeDtypeStruct((N_DEVICES, R, C), F32),
        compiler_params=_params(dimension_semantics=("arbitrary",)),
    )(place, vec)


def _slots_plan(bufs):
    x, y, c, _ = _place()
    mine = bufs[0].at[4 * x + 2 * y + c]
    return [(mine, mine, (x ^ (r >> 2), y ^ ((r >> 1) & 1), c ^ (r & 1))) for r in range(1, N_DEVICES)]


def _sum_slots(slots):
    _, R, C = slots.shape

    def body(s_ref, o_ref):
        acc = s_ref[0]
        for d in range(1, N_DEVICES):
            acc = acc + s_ref[d]
        o_ref[...] = acc

    return pl.pallas_call(
        body, name="sum_slots", in_specs=[pl.BlockSpec(memory_space=pltpu.VMEM)],
        out_specs=pl.BlockSpec(memory_space=pltpu.VMEM), out_shape=jax.ShapeDtypeStruct((R, C), F32),
        compiler_params=_params(),
    )(slots)


SMALL = ("ffn1_norm", "mix_norm", "pool_w", "pool_scale", "q_norm", "k_norm", "sinks", "gate_bias", "ffn2_norm")
SMALL_COLS = 1024
FFN1 = ("ffn1_w_gate", "ffn1_w_up", "ffn1_w_down")
MIXER = ("w_in", "w_pool_out", "w_attn_out", "w_out")
FFN2 = ("ffn2_w_gate", "ffn2_w_up", "ffn2_w_down")
LARGE = FFN1 + MIXER + FFN2
TRANSPOSED = ("ffn1_w_gate", "ffn1_w_up", "w_in", "w_pool_out", "ffn2_w_gate", "ffn2_w_up")
WEIGHTS = ("ffn1_norm", "ffn1_w_gate", "ffn1_w_up", "ffn1_w_down", "mix_norm", "w_in", "pool_w", "pool_scale",
           "w_pool_out", "q_norm", "k_norm", "sinks", "w_attn_out", "gate_bias", "w_out", "ffn2_norm",
           "ffn2_w_gate", "ffn2_w_up", "ffn2_w_down")


def _pack_small(parts):
    flat = jnp.concatenate([p.reshape(-1) for p in parts])
    rows = -(-flat.shape[0] // (8 * SMALL_COLS)) * 8
    return jnp.pad(flat, (0, rows * SMALL_COLS - flat.shape[0])).reshape(rows, SMALL_COLS)


def _unpack_small(packed, like):
    flat = packed.reshape(-1)
    out, off = [], 0
    for p in like:
        out.append(flat[off:off + p.size].reshape(p.shape))
        off += p.size
    return out, flat[off]


def _tie(small, token):
    return small + token[0, 0]


class Reduction:
    def __init__(self, group, names, grads, place):
        self.group, self.names, self.place = group, names, place
        self.bufs = [g.reshape(N_SHARDS, -1, g.shape[-1]) for g in grads]
        self.flight = None

    def _start(self, stage, bufs, plan, n_copies, after):
        self.flight = _split_start(f"{self.group}_{stage}", bufs, plan, n_copies, after)
        return self.flight.token

    def _landed(self, stage, after):
        n = len(self.names)
        bufs = _split_wait(f"{self.group}_{stage}_wait", self.flight, after)
        return bufs[:n], bufs[n:]

    def start_swap(self, after):
        lands = [lax.empty((g.shape[0], g.shape[1] // 2, g.shape[2]), g.dtype) for g in self.bufs]
        return self._start("swap", self.bufs + lands, _swap_plan, len(self.bufs), after)

    def start_exchange(self, after):
        g4, recv = self._landed("swap", after)
        halves = [_sum_halves(g, r, self.place, f"sum_halves_{n}") for n, g, r in zip(self.names, g4, recv)]
        lands = [lax.empty((N_SHARDS - 1,) + h.shape[1:], h.dtype) for h in halves]
        return self._start("exchange", halves + lands, _exchange_plan, (N_SHARDS - 1) * len(halves), self.place)

    def start_share(self, after):
        halves, recv3 = self._landed("exchange", after)
        reduced = [_sum_quarters(h, r, self.place, f"sum_quarters_{n}") for n, h, r in zip(self.names, halves, recv3)]
        return self._start("share", reduced, _share_plan, len(reduced), self.place)

    def finish(self, after):
        return dict(zip(self.names, _split_wait(f"{self.group}_share_wait", self.flight, after)))


def _row_form(name, a):
    return a.T if name in TRANSPOSED else a


def kernel(x, ffn1_norm, ffn1_w_gate, ffn1_w_up, ffn1_w_down, mix_norm, w_in, pool_w, pool_scale, w_pool_out, q_norm, k_norm, sinks, w_attn_out, gate_bias, w_out, ffn2_norm, ffn2_w_gate, ffn2_w_up, ffn2_w_down, loss_target, m_ffn1_norm, m_ffn1_w_gate, m_ffn1_w_up, m_ffn1_w_down, m_mix_norm, m_w_in, m_pool_w, m_pool_scale, m_w_pool_out, m_q_norm, m_k_norm, m_sinks, m_w_attn_out, m_gate_bias, m_w_out, m_ffn2_norm, m_ffn2_w_gate, m_ffn2_w_up, m_ffn2_w_down, v_ffn1_norm, v_ffn1_w_gate, v_ffn1_w_up, v_ffn1_w_down, v_mix_norm, v_w_in, v_pool_w, v_pool_scale, v_w_pool_out, v_q_norm, v_k_norm, v_sinks, v_w_attn_out, v_gate_bias, v_w_out, v_ffn2_norm, v_ffn2_w_gate, v_ffn2_w_up, v_ffn2_w_down):
    args = dict(locals())
    wts = {n: _row_form(n, args[n]) for n in WEIGHTS}
    mom = {n: _row_form(n, args["m_" + n]) for n in WEIGHTS}
    var = {n: _row_form(n, args["v_" + n]) for n in WEIGHTS}
    shard = 2 * lax.axis_index("x") + lax.axis_index("y")
    place = jnp.stack([shard, lax.axis_index("c")]).astype(jnp.int32)

    xs, target = x[0], loss_target[0]
    D = xs.shape[1]
    g1 = wts["ffn1_norm"].reshape(1, D)
    gm = wts["mix_norm"].reshape(1, D)
    g2 = wts["ffn2_norm"].reshape(1, D)
    qw = wts["q_norm"].reshape(1, HEAD_DIM)
    kw = wts["k_norm"].reshape(1, HEAD_DIM)
    bias = wts["gate_bias"].reshape(1, 2 * D)
    pscale = wts["pool_scale"].reshape(1, -1)
    pw = wts["pool_w"].astype(BF16)
    sink_rows = jnp.broadcast_to(jnp.repeat(wts["sinks"], ATTN_BLOCK).reshape(N_KV_HEADS, GQA_GROUP * ATTN_BLOCK, 1),
                                 (N_KV_HEADS, GQA_GROUP * ATTN_BLOCK, 128))

    first, down1 = FFN1[:2], FFN1[2:]
    placed = {n: _cast_place(wts[n], place, "cast_" + n) for n in FFN1}
    first_ici = _split_start("gather_first_ici", [placed[n] for n in first], _gather_ici_plan, 3 * len(first), place)
    placed.update({n: _cast_place(wts[n], place, "cast_" + n) for n in MIXER + FFN2})
    first_d2d = _split_start("gather_first_d2d", first_ici.bufs, _gather_d2d_plan, 3 * len(first),
                             [placed[n] for n in MIXER + FFN2], waits=first_ici)
    down1_ici = _split_start("gather_down1_ici", [placed[n] for n in down1], _gather_ici_plan, 3 * len(down1), first_d2d.token)
    w = dict(zip(first, _split_wait("gather_first_wait", first_d2d, down1_ici.token)))
    mix_ici = _split_start("gather_mix_ici", [placed[n] for n in MIXER], _gather_ici_plan, 3 * len(MIXER), w[first[0]])
    ffn2_ici = _split_start("gather_ffn2_ici", [placed[n] for n in FFN2], _gather_ici_plan, 3 * len(FFN2), mix_ici.token)

    n1t, a1, b1, s1 = _ffn_up(xs, _tie(g1, ffn2_ici.token), w["ffn1_w_gate"], w["ffn1_w_up"], "ffn1_up")
    down1_d2d = _split_start("gather_down1_d2d", down1_ici.bufs, _gather_d2d_plan, 3 * len(down1), s1, waits=down1_ici)
    w.update(zip(down1, _split_wait("gather_down1_wait", down1_d2d, down1_d2d.token)))
    h1 = _ffn_down(xs, s1, w["ffn1_w_down"], "ffn1_down")
    mix_d2d = _split_start("gather_mix_d2d", mix_ici.bufs, _gather_d2d_plan, 3 * len(MIXER), h1, waits=mix_ici)
    w.update(zip(MIXER, _split_wait("gather_mix_wait", mix_d2d, mix_d2d.token)))
    ut, xp, q, kv, gl = _mix_proj(h1, gm, w["w_in"])
    attn = _attn_fwd(q, kv, qw, kw, sink_rows)
    ffn2_d2d = _split_start("gather_ffn2_d2d", ffn2_ici.bufs, _gather_d2d_plan, 3 * len(FFN2), attn, waits=ffn2_ici)
    h2, bp, ba = _mix_out(xp, attn, gl, _tie(bias, ffn2_d2d.token), pw, pscale, w["w_pool_out"], w["w_attn_out"], w["w_out"], h1)
    w.update(zip(FFN2, _split_wait("gather_ffn2_wait", ffn2_d2d, h2)))
    n2t, a2, b2, dy, dyh, dyht, loss = _ffn_fwd_loss(h2, g2, w["ffn2_w_gate"], w["ffn2_w_up"], w["ffn2_w_down"], target, "ffn2_fwd")

    gw, gs = {}, {}
    da, db, s2 = _ffn_bwd_h(dyh, a2, b2, w["ffn2_w_down"], ffn2_d2d.token, "ffn2_bwd_h")
    gw["ffn2_w_gate"], gw["ffn2_w_up"] = _xty(n2t, da, "ffn2_dw_gate"), _xty(n2t, db, "ffn2_dw_up")
    gw["ffn2_w_down"] = _xty(dyht, s2, "ffn2_dw_down")
    red2 = Reduction("reduce_ffn2", FFN2, [gw[n] for n in FFN2], place)
    token = red2.start_swap(da)
    dh2, gs["ffn2_norm"] = _ffn_bwd_x(da, db, w["ffn2_w_gate"], w["ffn2_w_up"], h2, dy, _tie(g2, token), "ffn2_bwd_x")
    token = red2.start_exchange(dh2)
    dgl, dbp, dba, gw["w_out"], gs["gate_bias"] = _mix_bwd_gate(dh2, w["w_out"], bp, ba, gl, _tie(bias, token))
    dattn, dpooled, gw["w_attn_out"], gw["w_pool_out"], gs["pool_w"], gs["pool_scale"] = _mix_bwd_branch(
        dbp, dba, attn, xp, pw, pscale, w["w_pool_out"], w["w_attn_out"])
    dq, dkv, gs["q_norm"], gs["k_norm"], dsk = _attn_bwd(q, kv, dattn, qw, kw, sink_rows)
    gs["sinks"] = dsk[:, :, 0]
    red2.start_share(dq)
    pieces = (_pool_bwd(dpooled), dq, dkv, dgl)
    gw["w_in"] = _in_bwd_w(pieces, ut)
    grads = red2.finish(gw["w_in"])
    redm = Reduction("reduce_mix", MIXER, [gw[n] for n in MIXER], place)
    token = redm.start_swap(grads[FFN2[0]])
    dh1, dh1h, dh1ht, gs["mix_norm"] = _in_bwd_x(pieces, w["w_in"], h1, dh2, _tie(gm, token))
    token = redm.start_exchange(dh1)
    da, db, s1 = _ffn_bwd_h(dh1h, a1, b1, w["ffn1_w_down"], token, "ffn1_bwd_h")
    gw["ffn1_w_gate"], gw["ffn1_w_up"] = _xty(n1t, da, "ffn1_dw_gate"), _xty(n1t, db, "ffn1_dw_up")
    gw["ffn1_w_down"] = _xty(dh1ht, s1, "ffn1_dw_down")
    token = redm.start_share(da)
    red1 = Reduction("reduce_ffn1", FFN1, [gw[n] for n in FFN1], place)
    token = red1.start_swap(token)
    grad_x, gs["ffn1_norm"] = _ffn_bwd_x(da, db, w["ffn1_w_gate"], w["ffn1_w_up"], xs, dh1, _tie(g1, token), "ffn1_bwd_x")
    grads.update(redm.finish(grad_x))

    small_parts = [gs[n] for n in SMALL] + [loss[0, 0].reshape(1)]
    slots = _split_start("gather_small", [_slot_place(_pack_small(small_parts), place)], _slots_plan, N_DEVICES - 1,
                         grads[MIXER[0]])
    token = red1.start_exchange(slots.token)
    delta, new_m, new_v = {}, {}, {}
    for n in FFN2 + MIXER:
        delta[n], new_m[n], new_v[n], grads[n] = _adamw(wts[n], grads[n], mom[n], var[n], token, "adamw_" + n)
    summed = _sum_slots(_split_wait("gather_small_wait", slots, delta[MIXER[-1]])[0])
    small_grads, loss_sum = _unpack_small(summed, [wts[n] for n in SMALL])
    grads.update(dict(zip(SMALL, small_grads)))
    zero = jnp.zeros((1,), F32)
    packed = [_pack_small([d[n] for n in SMALL] + [zero]) for d in (wts, mom, var)]
    ds, ms, vs, _ = _adamw(packed[0], summed, packed[1], packed[2], token, "adamw_small")
    like = [wts[n] for n in SMALL]
    for out, packed_out in ((delta, ds), (new_m, ms), (new_v, vs)):
        out.update(dict(zip(SMALL, _unpack_small(packed_out, like)[0])))
    token = red1.start_share(ds)
    grads.update(red1.finish(token))
    for n in FFN1:
        delta[n], new_m[n], new_v[n], grads[n] = _adamw(wts[n], grads[n], mom[n], var[n], token, "adamw_" + n)

    outs = []
    for d in (grads, delta, new_m, new_v):
        outs += [_row_form(n, d[n]) for n in WEIGHTS]
    return (loss_sum, grad_x[None], *outs)
```

```python
from typing import Callable, NamedTuple

import jax
import jax.numpy as jnp
from jax import lax
from jax.experimental import pallas as pl
from jax.experimental.pallas import tpu as pltpu

F32 = jnp.float32
BF16 = jnp.bfloat16
RMS_EPS = 1e-6
POOL_WINDOWS = (2, 4, 8, 16)
POOL_GROUP = 128
POOL_HALO = 16
HEAD_DIM = 64
GQA_GROUP = 8
N_KV_HEADS = 2
ATTN_BLOCK = 128
SCALE = HEAD_DIM ** -0.5
NEG = -1e30
N_SHARDS = 4
ADAM_LR, ADAM_B1, ADAM_B2, ADAM_EPS, ADAM_WD, ADAM_STEP = 0.001, 0.9, 0.999, 1e-08, 0.01, 10
VMEM_LIMIT = 56 * 1024 * 1024
MESH = pl.DeviceIdType.MESH
SEG_POOL, SEG_Q, SEG_KV, SEG_GATE = (0, 512), (512, 1024), (1536, 256), (1792, 2048)
SEGMENTS = (SEG_POOL, SEG_Q, SEG_KV, SEG_GATE)


def _params(**kw):
    return pltpu.CompilerParams(vmem_limit_bytes=VMEM_LIMIT, **kw)


def _dot(a, b):
    return jnp.dot(a, b, preferred_element_type=F32)


def _dot_nt(a, b):
    return lax.dot_general(a, b, (((1,), (1,)), ((), ())), preferred_element_type=F32)


def _dot_tn(a, b):
    return lax.dot_general(a, b, (((0,), (0,)), ((), ())), preferred_element_type=F32)


def _rinv(x):
    return lax.rsqrt(jnp.mean(x * x, axis=-1, keepdims=True) + RMS_EPS)


def _rms_bwd(dn, x, g):
    r = _rinv(x)
    xr = x * r
    z = dn * g
    dx = r * (z - xr * jnp.mean(z * xr, axis=-1, keepdims=True))
    return dx, dn * xr


def _acc(ref, val, first):
    @pl.when(first)
    def _():
        ref[...] = val

    @pl.when(jnp.logical_not(first))
    def _():
        ref[...] += val


TOKEN_SPEC = pl.BlockSpec(memory_space=pl.ANY)
F_HALF = 1408


def _resident(w):
    return pl.BlockSpec(w.shape, lambda i: (0, 0), pipeline_mode=pl.Buffered(1))


def _ffn_up(h, gnorm, wgt, wut, name, tm=512):
    S, D = h.shape
    F = wgt.shape[0]

    def body(h_ref, g_ref, wg_ref, wu_ref, nt_ref, a_ref, b_ref, s_ref):
        x = h_ref[...]
        nv = x * _rinv(x) * g_ref[...]
        nt_ref[...] = nv.T.astype(BF16)
        n = nv.astype(BF16)
        for lo in range(0, F, F_HALF):
            cols = slice(lo, lo + F_HALF)
            a = _dot_nt(n, wg_ref[cols, :])
            b = _dot_nt(n, wu_ref[cols, :])
            a_ref[:, cols] = a.astype(BF16)
            b_ref[:, cols] = b.astype(BF16)
            s_ref[:, cols] = (a * jax.nn.sigmoid(a) * b).astype(BF16)

    act = pl.BlockSpec((tm, F), lambda i: (i, 0))
    hidden = jax.ShapeDtypeStruct((S, F), BF16)
    return pl.pallas_call(
        body, name=name, grid=(S // tm,),
        in_specs=[pl.BlockSpec((tm, D), lambda i: (i, 0)), pl.BlockSpec((1, D), lambda i: (0, 0)), _resident(wgt), _resident(wut)],
        out_specs=[pl.BlockSpec((D, tm), lambda i: (0, i)), act, act, act],
        out_shape=[jax.ShapeDtypeStruct((D, S), BF16), hidden, hidden, hidden],
        compiler_params=_params(dimension_semantics=("arbitrary",)),
    )(h, gnorm, wgt, wut)


def _ffn_down(h, s, wd, name, tm=1024):
    S, D = h.shape
    F = wd.shape[0]

    def body(h_ref, s_ref, wd_ref, o_ref):
        o_ref[...] = h_ref[...] + 0.5 * _dot(s_ref[...], wd_ref[...])

    row = pl.BlockSpec((tm, D), lambda i: (i, 0))
    return pl.pallas_call(
        body, name=name, grid=(S // tm,), in_specs=[row, pl.BlockSpec((tm, F), lambda i: (i, 0)), _resident(wd)],
        out_specs=row, out_shape=jax.ShapeDtypeStruct((S, D), F32),
        compiler_params=_params(dimension_semantics=("arbitrary",)),
    )(h, s, wd)


def _ffn_down_loss(h, s, wd, target, name, tm=512):
    S, D = h.shape
    F = wd.shape[0]

    def body(h_ref, s_ref, wd_ref, t_ref, dy_ref, dyh_ref, dyht_ref, loss_ref):
        e = h_ref[...] + 0.5 * _dot(s_ref[...], wd_ref[...]) - t_ref[...]
        dy = e * (1.0 / D)
        dy_ref[...] = dy
        dyh_ref[...] = (0.5 * dy).astype(BF16)
        dyht_ref[...] = (0.5 * dy).T.astype(BF16)
        tot = jnp.sum(jnp.sum(e * e, axis=1, keepdims=True), axis=0, keepdims=True) * (0.5 / D)
        _acc(loss_ref, jnp.broadcast_to(tot, loss_ref.shape), pl.program_id(0) == 0)

    row = pl.BlockSpec((tm, D), lambda i: (i, 0))
    col = pl.BlockSpec((D, tm), lambda i: (0, i))
    return pl.pallas_call(
        body, name=name, grid=(S // tm,), in_specs=[row, pl.BlockSpec((tm, F), lambda i: (i, 0)), _resident(wd), row],
        out_specs=[row, row, col, pl.BlockSpec((8, 128), lambda i: (0, 0))],
        out_shape=[jax.ShapeDtypeStruct((S, D), F32), jax.ShapeDtypeStruct((S, D), BF16), jax.ShapeDtypeStruct((D, S), BF16),
                   jax.ShapeDtypeStruct((8, 128), F32)],
        compiler_params=_params(dimension_semantics=("arbitrary",)),
    )(h, s, wd, target)


def _ffn_bwd_h(dyh, a, b, wd, token, name, tm=512):
    S, D = dyh.shape
    F = wd.shape[0]

    def body(dyh_ref, a_ref, b_ref, wd_ref, _, da_ref, db_ref):
        dyh_t = dyh_ref[...]
        for lo in range(0, F, F_HALF):
            cols = slice(lo, lo + F_HALF)
            av = a_ref[:, cols].astype(F32)
            bv = b_ref[:, cols].astype(F32)
            ds = _dot_nt(dyh_t, wd_ref[cols, :])
            sig = jax.nn.sigmoid(av)
            silu = av * sig
            da_ref[:, cols] = (ds * bv * (sig * (1.0 + av * (1.0 - sig)))).astype(BF16)
            db_ref[:, cols] = (ds * silu).astype(BF16)

    act = pl.BlockSpec((tm, F), lambda i: (i, 0))
    hidden = jax.ShapeDtypeStruct((S, F), BF16)
    return pl.pallas_call(
        body, name=name, grid=(S // tm,),
        in_specs=[pl.BlockSpec((tm, D), lambda i: (i, 0)), act, act, _resident(wd), TOKEN_SPEC],
        out_specs=[act, act], out_shape=[hidden, hidden],
        compiler_params=_params(dimension_semantics=("arbitrary",)),
    )(dyh, a, b, wd, token)


def _xty(yt, x, name, tk=1024, tf=1408, chunk=128):
    D, S = yt.shape
    F = x.shape[1]
    nk, nf = S // tk, F // tf
    nchunk = tf // chunk

    def body(yt_ref, x_ref, out_hbm, acc, stage, sem):
        j = pl.program_id(0)
        k = pl.program_id(1)
        _acc(acc, _dot(yt_ref[...], x_ref[...]), k == 0)

        @pl.when(k == nk - 1)
        def _():
            def out_copy(c, slot):
                rows = pl.ds(pl.multiple_of(j * tf + c * chunk, chunk), chunk)
                return pltpu.make_async_copy(stage.at[slot], out_hbm.at[rows], sem.at[slot])

            for c in range(nchunk):
                slot = c % 2
                if c >= 2:
                    out_copy(c - 2, slot).wait()
                stage[slot] = acc[:, c * chunk:(c + 1) * chunk].T
                out_copy(c, slot).start()
            for c in range(nchunk - 2, nchunk):
                out_copy(c, c % 2).wait()

    return pl.pallas_call(
        body, name=name, grid=(nf, nk),
        in_specs=[pl.BlockSpec((D, tk), lambda j, k: (0, k)), pl.BlockSpec((tk, tf), lambda j, k: (k, j))],
        out_specs=pl.BlockSpec(memory_space=pl.ANY), out_shape=jax.ShapeDtypeStruct((F, D), F32),
        scratch_shapes=[pltpu.VMEM((D, tf), F32), pltpu.VMEM((2, chunk, D), F32), pltpu.SemaphoreType.DMA((2,))],
        compiler_params=_params(dimension_semantics=("arbitrary", "arbitrary")),
    )(yt, x)


def _ffn_bwd_x(da, db, wgt, wut, x, dh, gnorm, name, tm=512):
    S, D = x.shape
    F = wgt.shape[0]

    def body(da_ref, db_ref, wg_ref, wu_ref, x_ref, dh_ref, g_ref, dx_ref, dg_ref):
        dn = _dot(da_ref[...], wg_ref[...]) + _dot(db_ref[...], wu_ref[...])
        dx, dg_rows = _rms_bwd(dn, x_ref[...], g_ref[...])
        dx_ref[...] = dh_ref[...] + dx
        _acc(dg_ref, jnp.sum(dg_rows, axis=0, keepdims=True), pl.program_id(0) == 0)

    row = pl.BlockSpec((tm, D), lambda i: (i, 0))
    act = pl.BlockSpec((tm, F), lambda i: (i, 0))
    vec = pl.BlockSpec((1, D), lambda i: (0, 0))
    return pl.pallas_call(
        body, name=name, grid=(S // tm,), in_specs=[act, act, _resident(wgt), _resident(wut), row, row, vec],
        out_specs=[row, vec],
        out_shape=[jax.ShapeDtypeStruct((S, D), F32), jax.ShapeDtypeStruct((1, D), F32)],
        compiler_params=_params(dimension_semantics=("arbitrary",)),
    )(da, db, wgt, wut, x, dh, gnorm)


def _mix_proj(h, gnorm, wint, tm=512):
    S, D = h.shape
    nt = S // tm

    def body(h_ref, g_ref, w_ref, ut_ref, xp_ref, q_ref, kv_ref, gl_ref):
        x = h_ref[...]
        uv = x * _rinv(x) * g_ref[...]
        ut_ref[...] = uv.T.astype(BF16)
        u = uv.astype(BF16)
        for (off, width), ref in zip(SEGMENTS, (xp_ref, q_ref, kv_ref, gl_ref)):
            ref[...] = _dot_nt(u, w_ref[off:off + width, :]).astype(ref.dtype)

    def row(width):
        return pl.BlockSpec((tm, width), lambda i: (i, 0))

    return pl.pallas_call(
        body, name="mix_proj", grid=(nt,),
        in_specs=[row(D), pl.BlockSpec((1, D), lambda i: (0, 0)), pl.BlockSpec(wint.shape, lambda i: (0, 0))],
        out_specs=[pl.BlockSpec((D, tm), lambda i: (0, i)), row(SEG_POOL[1]), row(SEG_Q[1]), row(SEG_KV[1]), row(SEG_GATE[1])],
        out_shape=[jax.ShapeDtypeStruct((D, S), BF16), jax.ShapeDtypeStruct((S, SEG_POOL[1]), F32),
                   jax.ShapeDtypeStruct((S, SEG_Q[1]), BF16), jax.ShapeDtypeStruct((S, SEG_KV[1]), BF16),
                   jax.ShapeDtypeStruct((S, SEG_GATE[1]), BF16)],
        compiler_params=_params(dimension_semantics=("arbitrary",)),
    )(h, gnorm, wint)


def _stack_heads(x, g):
    return jnp.concatenate([x[:, (GQA_GROUP * g + hh) * HEAD_DIM:(GQA_GROUP * g + hh + 1) * HEAD_DIM]
                            for hh in range(GQA_GROUP)], axis=0)


def _unstack_heads(ref, val, g):
    for hh in range(GQA_GROUP):
        lo = (GQA_GROUP * g + hh) * HEAD_DIM
        ref[:, lo:lo + HEAD_DIM] = val[hh * ATTN_BLOCK:(hh + 1) * ATTN_BLOCK, :]


def _rowsum(xb, width):
    return _dot(xb, jnp.ones((xb.shape[1], width), BF16))


def _rinv_lanes(x):
    return lax.rsqrt(_rowsum((x * x).astype(BF16), x.shape[1]) * (1.0 / x.shape[1]) + RMS_EPS)


def _attn_exp(qn, kc, kp, sink, n):
    rows = GQA_GROUP * ATTN_BLOCK
    qi = lax.broadcasted_iota(jnp.int32, (rows, ATTN_BLOCK), 0) % ATTN_BLOCK
    kj = lax.broadcasted_iota(jnp.int32, (rows, ATTN_BLOCK), 1)
    s_c = jnp.where(kj <= qi, _dot_nt(qn, kc), NEG)
    s_p = jnp.where(jnp.logical_and(kj > qi, n > 0), _dot_nt(qn, kp), NEG)
    m = jnp.maximum(jnp.max(s_c, axis=-1, keepdims=True), jnp.max(s_p, axis=-1, keepdims=True))
    m = jnp.maximum(jnp.broadcast_to(m, sink.shape), sink)
    e_c = jnp.exp(s_c - m)
    e_p = jnp.exp(s_p - m)
    e_s = jnp.exp(sink - m)
    e_cb, e_pb = e_c.astype(BF16), e_p.astype(BF16)
    inv = 1.0 / (_rowsum(jnp.concatenate([e_cb, e_pb], axis=1), ATTN_BLOCK) + e_s)
    return e_cb, e_pb, e_c, e_p, e_s, inv


def _attn_blocks(n):
    cur = pl.multiple_of(n * ATTN_BLOCK, ATTN_BLOCK)
    prev = pl.multiple_of(jnp.maximum(n - 1, 0) * ATTN_BLOCK, ATTN_BLOCK)
    return cur, prev


def _kv_split(kv, g):
    k = kv[:, g * HEAD_DIM:(g + 1) * HEAD_DIM]
    v = kv[:, (N_KV_HEADS + g) * HEAD_DIM:(N_KV_HEADS + g + 1) * HEAD_DIM]
    return k, v


def _attn_fwd(q, kv, qw, kw, sink_rows):
    S, W = q.shape
    nb = S // ATTN_BLOCK

    def body(q_ref, kv_ref, qw_ref, kw_ref, sk_ref, o_ref, o_scr):
        n = pl.program_id(0)
        qf = q_ref[...].astype(F32)
        cur, prev = _attn_blocks(n)
        kvc = kv_ref[pl.ds(cur, ATTN_BLOCK), :].astype(F32)
        kvp = kv_ref[pl.ds(prev, ATTN_BLOCK), :].astype(F32)
        for g in range(N_KV_HEADS):
            qs = _stack_heads(qf, g)
            qn = (qs * _rinv_lanes(qs) * qw_ref[...] * SCALE).astype(BF16)
            kc, vc = _kv_split(kvc, g)
            kp, vp = _kv_split(kvp, g)
            kc = (kc * _rinv_lanes(kc) * kw_ref[...]).astype(BF16)
            kp = (kp * _rinv_lanes(kp) * kw_ref[...]).astype(BF16)
            e_cb, e_pb, _, _, _, inv = _attn_exp(qn, kc, kp, sk_ref[g], n)
            o = (_dot(e_cb, vc.astype(BF16)) + _dot(e_pb, vp.astype(BF16))) * inv[:, :HEAD_DIM]
            _unstack_heads(o_scr, o, g)
        o_ref[...] = o_scr[...].astype(BF16)

    blk = pl.BlockSpec((ATTN_BLOCK, W), lambda n: (n, 0))
    return pl.pallas_call(
        body, name="attn_fwd", grid=(nb,),
        in_specs=[blk, pl.BlockSpec(kv.shape, lambda n: (0, 0)), pl.BlockSpec((1, HEAD_DIM), lambda n: (0, 0)),
                  pl.BlockSpec((1, HEAD_DIM), lambda n: (0, 0)), pl.BlockSpec(sink_rows.shape, lambda n: (0, 0, 0))],
        out_specs=blk, out_shape=jax.ShapeDtypeStruct((S, W), BF16),
        scratch_shapes=[pltpu.VMEM((ATTN_BLOCK, W), F32)],
        compiler_params=_params(dimension_semantics=("arbitrary",)),
    )(q, kv, qw, kw, sink_rows)


def _attn_bwd(q, kv, do, qw, kw, sink_rows):
    S, W = q.shape
    KW = kv.shape[1]
    nb = S // ATTN_BLOCK
    chunk = 512

    def body(q_ref, kv_ref, do_ref, qw_ref, kw_ref, sk_ref, dq_ref, dkv_ref, dqw_ref, dkw_ref, dsk_ref, dq_scr):
        n = pl.program_id(0)

        @pl.when(n == 0)
        def _():
            dkv_ref[...] = jnp.zeros_like(dkv_ref)
            dqw_ref[...] = jnp.zeros_like(dqw_ref)
            dsk_ref[...] = jnp.zeros_like(dsk_ref)

        qf = q_ref[...].astype(F32)
        dof = do_ref[...].astype(F32)
        cur, prev = _attn_blocks(n)
        kvc = kv_ref[pl.ds(cur, ATTN_BLOCK), :].astype(F32)
        kvp = kv_ref[pl.ds(prev, ATTN_BLOCK), :].astype(F32)
        qw_v = qw_ref[...]
        for g in range(N_KV_HEADS):
            qs = _stack_heads(qf, g)
            rq = _rinv_lanes(qs)
            qhat = qs * rq
            qn = (qhat * qw_v * SCALE).astype(BF16)
            kc, vc = _kv_split(kvc, g)
            kp, vp = _kv_split(kvp, g)
            kc = (kc * _rinv_lanes(kc) * kw_ref[...]).astype(BF16)
            kp = (kp * _rinv_lanes(kp) * kw_ref[...]).astype(BF16)
            vc = vc.astype(BF16)
            vp = vp.astype(BF16)
            dos = _stack_heads(dof, g).astype(BF16)
            _, _, e_c, e_p, e_s, inv = _attn_exp(qn, kc, kp, sk_ref[g], n)
            p_c, p_p, p_s = e_c * inv, e_p * inv, e_s * inv
            dp_c = _dot_nt(dos, vc)
            dp_p = _dot_nt(dos, vp)
            drow = _rowsum(jnp.concatenate([(p_c * dp_c).astype(BF16), (p_p * dp_p).astype(BF16)], axis=1), ATTN_BLOCK)
            ds_c = (p_c * (dp_c - drow)).astype(BF16)
            ds_p = (p_p * (dp_p - drow)).astype(BF16)
            dsink = -(p_s * drow)
            for hh in range(GQA_GROUP):
                tot = jnp.sum(dsink[hh * ATTN_BLOCK:(hh + 1) * ATTN_BLOCK, :], axis=0, keepdims=True)
                dsk_ref[g, hh:hh + 1, :] += tot
            dqn = (_dot(ds_c, kc) + _dot(ds_p, kp)) * SCALE
            klo, vlo = g * HEAD_DIM, (N_KV_HEADS + g) * HEAD_DIM
            dkv_ref[pl.ds(cur, ATTN_BLOCK), klo:klo + HEAD_DIM] += _dot_tn(ds_c, qn)
            dkv_ref[pl.ds(prev, ATTN_BLOCK), klo:klo + HEAD_DIM] += _dot_tn(ds_p, qn)
            dkv_ref[pl.ds(cur, ATTN_BLOCK), vlo:vlo + HEAD_DIM] += _dot_tn(p_c.astype(BF16), dos)
            dkv_ref[pl.ds(prev, ATTN_BLOCK), vlo:vlo + HEAD_DIM] += _dot_tn(p_p.astype(BF16), dos)
            dqw_ref[...] += jnp.sum(dqn * qhat, axis=0, keepdims=True)
            z = dqn * qw_v
            dqs = rq * (z - qhat * (_rowsum((z * qhat).astype(BF16), HEAD_DIM) * (1.0 / HEAD_DIM)))
            _unstack_heads(dq_scr, dqs, g)
        dq_ref[...] = dq_scr[...].astype(BF16)

        @pl.when(n == nb - 1)
        def _():
            def one(c, dkw):
                rows = pl.ds(pl.multiple_of(c * chunk, chunk), chunk)
                for g in range(N_KV_HEADS):
                    lo = g * HEAD_DIM
                    k = kv_ref[rows, lo:lo + HEAD_DIM].astype(F32)
                    dx, dg_rows = _rms_bwd(dkv_ref[rows, lo:lo + HEAD_DIM], k, kw_ref[...])
                    dkv_ref[rows, lo:lo + HEAD_DIM] = dx
                    dkw = dkw + jnp.sum(dg_rows, axis=0, keepdims=True)
                return dkw

            dkw_ref[...] = lax.fori_loop(0, S // chunk, one, jnp.zeros((1, HEAD_DIM), F32))

    blk = pl.BlockSpec((ATTN_BLOCK, W), lambda n: (n, 0))
    whole_kv = pl.BlockSpec((S, KW), lambda n: (0, 0))
    vec = pl.BlockSpec((1, HEAD_DIM), lambda n: (0, 0))
    sk = pl.BlockSpec(sink_rows.shape, lambda n: (0, 0, 0))
    dsk = pl.BlockSpec((N_KV_HEADS, GQA_GROUP, 128), lambda n: (0, 0, 0))
    return pl.pallas_call(
        body, name="attn_bwd", grid=(nb,), in_specs=[blk, whole_kv, blk, vec, vec, sk],
        out_specs=[blk, whole_kv, vec, vec, dsk],
        out_shape=[jax.ShapeDtypeStruct((S, W), BF16), jax.ShapeDtypeStruct((S, KW), F32),
                   jax.ShapeDtypeStruct((1, HEAD_DIM), F32), jax.ShapeDtypeStruct((1, HEAD_DIM), F32),
                   jax.ShapeDtypeStruct((N_KV_HEADS, GQA_GROUP, 128), F32)],
        scratch_shapes=[pltpu.VMEM((ATTN_BLOCK, W), F32)],
        compiler_params=_params(dimension_semantics=("arbitrary",)),
    )(q, kv, do, qw, kw, sink_rows)


def _pooled(xc, xprev, i):
    tm = xc.shape[0]
    xh = jnp.concatenate([jnp.where(i > 0, xprev, 0.0), xc], axis=0)
    t = lax.broadcasted_iota(jnp.int32, (tm, 1), 0) + i * tm
    out = []
    for gi, w in enumerate(POOL_WINDOWS):
        acc = xh[:, gi * POOL_GROUP:(gi + 1) * POOL_GROUP]
        sh = 1
        while sh < w:
            acc = acc + pltpu.roll(acc, sh, 0)
            sh *= 2
        cnt = jnp.minimum(t + 1, w).astype(F32)
        out.append(acc[POOL_HALO:, :] / cnt - xc[:, gi * POOL_GROUP:(gi + 1) * POOL_GROUP])
    return jnp.concatenate(out, axis=1)


def _pool_mix(pooled_b, pw_ref):
    return jnp.concatenate([_dot(pooled_b[:, gi * POOL_GROUP:(gi + 1) * POOL_GROUP], pw_ref[gi])
                            for gi in range(len(POOL_WINDOWS))], axis=1)


def _halo_specs(tm, width, S, after):
    per = tm // POOL_HALO
    last = S // POOL_HALO - 1
    if after:
        return pl.BlockSpec((POOL_HALO, width), lambda i: (jnp.minimum((i + 1) * per, last), 0))
    return pl.BlockSpec((POOL_HALO, width), lambda i: (jnp.maximum(i * per - 1, 0), 0))


def _mix_out(xp, attn, gl, bias, pw, pscale, wpot, wao, wo, h, tm=512):
    S, D = h.shape
    nt = S // tm
    PW = xp.shape[1]

    def body(xc_ref, xprev_ref, at_ref, gl_ref, bias_ref, pw_ref, ps_ref, wpo_ref, wao_ref, wo_ref, h_ref,
             ho_ref, bp_ref, ba_ref):
        i = pl.program_id(0)
        pooled = _pooled(xc_ref[...], xprev_ref[...], i).astype(BF16)
        ms = (_pool_mix(pooled, pw_ref) * ps_ref[...]).astype(BF16)
        bp = _dot_nt(ms, wpo_ref[...])
        ba = _dot(at_ref[...], wao_ref[...])
        bp_ref[...] = bp.astype(BF16)
        ba_ref[...] = ba.astype(BF16)
        gates = jax.nn.sigmoid(gl_ref[...].astype(F32) + bias_ref[...])
        merged = (gates[:, :D] * bp + gates[:, D:] * ba).astype(BF16)
        ho_ref[...] = h_ref[...] + _dot(merged, wo_ref[...])

    def row(width):
        return pl.BlockSpec((tm, width), lambda i: (i, 0))

    def whole(x):
        nd = x.ndim
        return pl.BlockSpec(x.shape, lambda i: (0,) * nd)

    return pl.pallas_call(
        body, name="mix_out", grid=(nt,),
        in_specs=[row(PW), _halo_specs(tm, PW, S, False), row(D), row(2 * D), whole(bias), whole(pw), whole(pscale),
                  whole(wpot), whole(wao), whole(wo), row(D)],
        out_specs=[row(D), row(D), row(D)],
        out_shape=[jax.ShapeDtypeStruct((S, D), F32), jax.ShapeDtypeStruct((S, D), BF16),
                   jax.ShapeDtypeStruct((S, D), BF16)],
        compiler_params=_params(dimension_semantics=("arbitrary",)),
    )(xp, xp, attn, gl, bias, pw, pscale, wpot, wao, wo, h)


def _mix_bwd_gate(dh, wo, bp, ba, gl, bias, tm=512):
    S, D = dh.shape
    nt = S // tm

    def body(dh_ref, wo_ref, bp_ref, ba_ref, gl_ref, bias_ref, dgl_ref, dbp_ref, dba_ref, dwo_ref, dbias_ref):
        i = pl.program_id(0)
        dhb = dh_ref[...].astype(BF16)
        dm = _dot_nt(dhb, wo_ref[...])
        gates = jax.nn.sigmoid(gl_ref[...].astype(F32) + bias_ref[...])
        gp, ga = gates[:, :D], gates[:, D:]
        bp_v = bp_ref[...].astype(F32)
        ba_v = ba_ref[...].astype(F32)
        merged = (gp * bp_v + ga * ba_v).astype(BF16)
        _acc(dwo_ref, _dot_tn(merged, dhb), i == 0)
        dbp_ref[...] = (dm * gp).astype(BF16)
        dba_ref[...] = (dm * ga).astype(BF16)
        dgl = jnp.concatenate([dm * bp_v * gp * (1.0 - gp), dm * ba_v * ga * (1.0 - ga)], axis=1)
        dgl_ref[...] = dgl.astype(BF16)
        _acc(dbias_ref, jnp.sum(dgl, axis=0, keepdims=True), i == 0)

    def row(width):
        return pl.BlockSpec((tm, width), lambda i: (i, 0))

    def whole(shape):
        return pl.BlockSpec(shape, lambda i: (0, 0))

    return pl.pallas_call(
        body, name="mix_bwd_gate", grid=(nt,),
        in_specs=[row(D), whole(wo.shape), row(D), row(D), row(2 * D), whole(bias.shape)],
        out_specs=[row(2 * D), row(D), row(D), whole((D, D)), whole((1, 2 * D))],
        out_shape=[jax.ShapeDtypeStruct((S, 2 * D), BF16), jax.ShapeDtypeStruct((S, D), BF16),
                   jax.ShapeDtypeStruct((S, D), BF16), jax.ShapeDtypeStruct((D, D), F32),
                   jax.ShapeDtypeStruct((1, 2 * D), F32)],
        compiler_params=_params(dimension_semantics=("arbitrary",)),
    )(dh, wo, bp, ba, gl, bias)


def _mix_bwd_branch(dbp, dba, attn, xp, pw, pscale, wpot, wao, tm=512):
    S, D = dbp.shape
    nt = S // tm
    PW = xp.shape[1]
    NG = len(POOL_WINDOWS)

    def body(dbp_ref, dba_ref, at_ref, xc_ref, xprev_ref, pw_ref, ps_ref, wpo_ref, wao_ref,
             dat_ref, dpl_ref, dwao_ref, dwpo_ref, dpw_ref, dps_ref):
        i = pl.program_id(0)
        dba_v = dba_ref[...]
        dbp_v = dbp_ref[...]
        _acc(dwao_ref, _dot_tn(at_ref[...], dba_v), i == 0)
        dat_ref[...] = _dot_nt(dba_v, wao_ref[...]).astype(BF16)
        pooled = _pooled(xc_ref[...], xprev_ref[...], i).astype(BF16)
        mixed = _pool_mix(pooled, pw_ref)
        ps = ps_ref[...]
        _acc(dwpo_ref, _dot_tn(dbp_v, (mixed * ps).astype(BF16)), i == 0)
        dms = _dot(dbp_v, wpo_ref[...])
        _acc(dps_ref, jnp.sum(dms * mixed, axis=0, keepdims=True), i == 0)
        dmixed = (dms * ps).astype(BF16)
        dpooled = []
        for gi in range(NG):
            cols = slice(gi * POOL_GROUP, (gi + 1) * POOL_GROUP)
            _acc(dpw_ref.at[gi], _dot_tn(pooled[:, cols], dmixed[:, cols]), i == 0)
            dpooled.append(_dot_nt(dmixed[:, cols], pw_ref[gi]))
        dpl_ref[...] = jnp.concatenate(dpooled, axis=1)

    def row(width):
        return pl.BlockSpec((tm, width), lambda i: (i, 0))

    def whole(shape):
        nd = len(shape)
        return pl.BlockSpec(shape, lambda i: (0,) * nd)

    return pl.pallas_call(
        body, name="mix_bwd_branch", grid=(nt,),
        in_specs=[row(D), row(D), row(D), row(PW), _halo_specs(tm, PW, S, False), whole(pw.shape), whole(pscale.shape),
                  whole(wpot.shape), whole(wao.shape)],
        out_specs=[row(D), row(PW), whole((D, D)), whole((D, PW)), whole(pw.shape), whole((1, PW))],
        out_shape=[jax.ShapeDtypeStruct((S, D), BF16), jax.ShapeDtypeStruct((S, PW), F32),
                   jax.ShapeDtypeStruct((D, D), F32), jax.ShapeDtypeStruct((D, PW), F32),
                   jax.ShapeDtypeStruct(pw.shape, F32), jax.ShapeDtypeStruct((1, PW), F32)],
        compiler_params=_params(dimension_semantics=("arbitrary",)),
    )(dbp, dba, attn, xp, xp, pw, pscale, wpot, wao)


def _pool_bwd(dpooled, tm=512):
    S, PW = dpooled.shape
    nt = S // tm

    def body(dc_ref, dnext_ref, dxp_ref):
        i = pl.program_id(0)
        dc = dc_ref[...]
        dh = jnp.concatenate([dc, jnp.where(i < nt - 1, dnext_ref[...], 0.0)], axis=0)
        rows = tm + POOL_HALO
        t = lax.broadcasted_iota(jnp.int32, (rows, 1), 0) + i * tm
        out = []
        for gi, w in enumerate(POOL_WINDOWS):
            cols = slice(gi * POOL_GROUP, (gi + 1) * POOL_GROUP)
            acc = dh[:, cols] / jnp.minimum(t + 1, w).astype(F32)
            sh = 1
            while sh < w:
                acc = acc + pltpu.roll(acc, rows - sh, 0)
                sh *= 2
            out.append(acc[:tm, :] - dc[:, cols])
        dxp_ref[...] = jnp.concatenate(out, axis=1).astype(BF16)

    return pl.pallas_call(
        body, name="pool_bwd", grid=(nt,),
        in_specs=[pl.BlockSpec((tm, PW), lambda i: (i, 0)), _halo_specs(tm, PW, S, True)],
        out_specs=pl.BlockSpec((tm, PW), lambda i: (i, 0)), out_shape=jax.ShapeDtypeStruct((S, PW), BF16),
        compiler_params=_params(dimension_semantics=("arbitrary",)),
    )(dpooled, dpooled)


def _in_bwd_w(pieces, ut, tm=1024, chunk=256):
    D, S = ut.shape
    nt = S // tm
    NW = sum(width for _, width in SEGMENTS)
    nchunk = NW // chunk

    def body(*refs):
        piece_refs, ut_ref, dw_hbm, acc, stage, sem = (refs[:len(SEGMENTS)], refs[len(SEGMENTS)], refs[len(SEGMENTS) + 1],
                                                      refs[-3], refs[-2], refs[-1])
        i = pl.program_id(0)
        u_t = ut_ref[...]
        for (off, width), ref in zip(SEGMENTS, piece_refs):
            for lo in range(0, width, D):
                hi = min(lo + D, width)
                _acc(acc.at[:, off + lo:off + hi], _dot(u_t, ref[:, lo:hi].astype(BF16)), i == 0)

        @pl.when(i == nt - 1)
        def _():
            def out_copy(c, slot):
                return pltpu.make_async_copy(stage.at[slot], dw_hbm.at[pl.ds(c * chunk, chunk)], sem.at[slot])

            for c in range(nchunk):
                slot = c % 2
                if c >= 2:
                    out_copy(c - 2, slot).wait()
                stage[slot] = acc[:, c * chunk:(c + 1) * chunk].T
                out_copy(c, slot).start()
            for c in range(nchunk - 2, nchunk):
                out_copy(c, c % 2).wait()

    return pl.pallas_call(
        body, name="in_bwd_w", grid=(nt,),
        in_specs=[pl.BlockSpec((tm, width), lambda i: (i, 0)) for _, width in SEGMENTS] + [pl.BlockSpec((D, tm), lambda i: (0, i))],
        out_specs=pl.BlockSpec(memory_space=pl.ANY), out_shape=jax.ShapeDtypeStruct((NW, D), F32),
        scratch_shapes=[pltpu.VMEM((D, NW), F32), pltpu.VMEM((2, chunk, D), F32), pltpu.SemaphoreType.DMA((2,))],
        compiler_params=_params(dimension_semantics=("arbitrary",)),
    )(*pieces, ut)


def _in_bwd_x(pieces, wint, h, dh, gnorm, tm=512):
    S, D = h.shape
    nt = S // tm

    def body(*refs):
        piece_refs = refs[:len(SEGMENTS)]
        w_ref, h_ref, dh_ref, g_ref, dx_ref, dxh_ref, dxht_ref, dg_ref = refs[len(SEGMENTS):]
        i = pl.program_id(0)
        du = jnp.zeros((tm, D), F32)
        for (off, width), ref in zip(SEGMENTS, piece_refs):
            du = du + _dot(ref[...].astype(BF16), w_ref[off:off + width, :])
        dx, dg_rows = _rms_bwd(du, h_ref[...], g_ref[...])
        out = dh_ref[...] + dx
        dx_ref[...] = out
        dxh_ref[...] = (0.5 * out).astype(BF16)
        dxht_ref[...] = (0.5 * out).T.astype(BF16)
        _acc(dg_ref, jnp.sum(dg_rows, axis=0, keepdims=True), i == 0)

    def row(width):
        return pl.BlockSpec((tm, width), lambda i: (i, 0))

    vec = pl.BlockSpec((1, D), lambda i: (0, 0))
    return pl.pallas_call(
        body, name="in_bwd_x", grid=(nt,),
        in_specs=[row(width) for _, width in SEGMENTS] + [pl.BlockSpec(wint.shape, lambda i: (0, 0)), row(D), row(D), vec],
        out_specs=[row(D), row(D), pl.BlockSpec((D, tm), lambda i: (0, i)), vec],
        out_shape=[jax.ShapeDtypeStruct((S, D), F32), jax.ShapeDtypeStruct((S, D), BF16),
                   jax.ShapeDtypeStruct((D, S), BF16), jax.ShapeDtypeStruct((1, D), F32)],
        compiler_params=_params(dimension_semantics=("arbitrary",)),
    )(*pieces, wint, h, dh, gnorm)


def _row_tile(rows):
    for t in (512, 480, 352, 256, 128, 64, 32, 16, 8):
        if rows % t == 0:
            return t
    return rows


def _adamw(w, g, m, v, token, name):
    R, C = w.shape
    tr = _row_tile(R)

    def body(w_ref, g_ref, m_ref, v_ref, _, d_ref, mo_ref, vo_ref, go_ref):
        gv = g_ref[...]
        go_ref[...] = gv
        mn = ADAM_B1 * m_ref[...] + (1.0 - ADAM_B1) * gv
        vn = ADAM_B2 * v_ref[...] + (1.0 - ADAM_B2) * (gv * gv)
        m_hat = mn / (1.0 - ADAM_B1 ** ADAM_STEP)
        v_hat = vn / (1.0 - ADAM_B2 ** ADAM_STEP)
        d_ref[...] = -ADAM_LR * (m_hat / (jnp.sqrt(v_hat) + ADAM_EPS) + ADAM_WD * w_ref[...])
        mo_ref[...] = mn
        vo_ref[...] = vn

    blk = pl.BlockSpec((tr, C), lambda i: (i, 0))
    sh = jax.ShapeDtypeStruct((R, C), F32)
    return pl.pallas_call(
        body, name=name, grid=(R // tr,), in_specs=[blk] * 4 + [TOKEN_SPEC], out_specs=[blk] * 4, out_shape=[sh] * 4,
        compiler_params=_params(dimension_semantics=("arbitrary",)),
    )(w, g, m, v, token)


def _cast_place(w, place, name):
    R, C = w.shape
    tr = _row_tile(R)
    per = R // tr

    def body(p_ref, w_ref, o_ref):
        o_ref[...] = w_ref[...].astype(BF16)

    grid_spec = pltpu.PrefetchScalarGridSpec(
        num_scalar_prefetch=1, grid=(per,),
        in_specs=[pl.BlockSpec((tr, C), lambda i, p: (i, 0))],
        out_specs=pl.BlockSpec((tr, C), lambda i, p: (p[0] * per + i, 0)))
    return pl.pallas_call(
        body, name=name, grid_spec=grid_spec, out_shape=jax.ShapeDtypeStruct((N_SHARDS * R, C), BF16),
        compiler_params=_params(dimension_semantics=("arbitrary",)),
    )(place, w)


def _sum_halves(g4, recv, place, name):
    NS, R, C = g4.shape
    hr = R // 2
    tr = _row_tile(hr)
    per = hr // tr

    def body(p_ref, g_ref, r_ref, o_ref):
        o_ref[...] = (g_ref[...] + r_ref[...]).astype(BF16)

    grid_spec = pltpu.PrefetchScalarGridSpec(
        num_scalar_prefetch=1, grid=(NS, per),
        in_specs=[pl.BlockSpec((1, tr, C), lambda s, i, p: (s, p[1] * per + i, 0)),
                  pl.BlockSpec((1, tr, C), lambda s, i, p: (s, i, 0))],
        out_specs=pl.BlockSpec((1, tr, C), lambda s, i, p: (s, i, 0)))
    return pl.pallas_call(
        body, name=name, grid_spec=grid_spec, out_shape=jax.ShapeDtypeStruct((NS, hr, C), BF16),
        compiler_params=_params(dimension_semantics=("arbitrary", "arbitrary")),
    )(place, g4, recv)


def _sum_quarters(h4, recv3, place, name):
    NS, hr, C = h4.shape
    tr = _row_tile(hr)
    per = hr // tr

    def body(p_ref, h_ref, r_ref, o_ref):
        acc = h_ref[0].astype(F32)
        for k in range(N_SHARDS - 1):
            acc = acc + r_ref[k].astype(F32)
        o_ref[...] = acc

    grid_spec = pltpu.PrefetchScalarGridSpec(
        num_scalar_prefetch=1, grid=(per,),
        in_specs=[pl.BlockSpec((1, tr, C), lambda i, p: (p[0], i, 0)),
                  pl.BlockSpec((N_SHARDS - 1, tr, C), lambda i, p: (0, i, 0))],
        out_specs=pl.BlockSpec((tr, C), lambda i, p: (p[1] * per + i, 0)))
    return pl.pallas_call(
        body, name=name, grid_spec=grid_spec, out_shape=jax.ShapeDtypeStruct((2 * hr, C), F32),
        compiler_params=_params(dimension_semantics=("arbitrary",)),
    )(place, h4, recv3)


def _place():
    x, y, c = lax.axis_index("x"), lax.axis_index("y"), lax.axis_index("c")
    chips = [(1 - x, y), (x, 1 - y), (1 - x, 1 - y)]
    return x, y, c, chips


HBM_SPEC = pl.BlockSpec(memory_space=pltpu.HBM)
SEM_SPEC = pl.BlockSpec(memory_space=pltpu.SEMAPHORE)
DATAFLOW = pltpu.SideEffectType.DATAFLOW_SIDE_EFFECTING


def _hbm(a):
    return pltpu.with_memory_space_constraint(a, pltpu.HBM)


class InFlight(NamedTuple):
    send_sem: jax.Array
    recv_sem: jax.Array
    bufs: list
    plan: Callable
    token: jax.Array


def _wait_all(plan, refs, send_ref, recv_ref):
    for k, (src, dst, dev) in enumerate(plan(refs)):
        cp = pltpu.make_async_remote_copy(src_ref=src, dst_ref=dst, send_sem=send_ref.at[k], recv_sem=recv_ref.at[k],
                                          device_id=dev, device_id_type=MESH)
        cp.wait_send()
        cp.wait_recv()


def _start_all(plan, refs, send_ref, recv_ref):
    for k, (src, dst, dev) in enumerate(plan(refs)):
        pltpu.make_async_remote_copy(src_ref=src, dst_ref=dst, send_sem=send_ref.at[k], recv_sem=recv_ref.at[k],
                                     device_id=dev, device_id_type=MESH).start()


def _split_start(name, bufs, plan, n_copies, after, waits=None):
    nb = len(bufs)
    after = list(after) if isinstance(after, (list, tuple)) else [after]

    def body(*refs):
        ins = refs[:nb]
        if waits is not None:
            _wait_all(waits.plan, ins, refs[nb], refs[nb + 1])
        send_sem, recv_sem, token = refs[-nb - 3], refs[-nb - 2], refs[-1]
        _start_all(plan, ins, send_sem, recv_sem)
        token[...] = jnp.zeros_like(token)

    sems = pltpu.SemaphoreType.DMA((n_copies,))
    earlier = [] if waits is None else [waits.send_sem, waits.recv_sem]
    out = pl.pallas_call(
        body, name=name, in_specs=[HBM_SPEC] * nb + [SEM_SPEC] * len(earlier) + [TOKEN_SPEC] * len(after),
        out_shape=(sems, sems, *[pltpu.HBM(b.shape, b.dtype) for b in bufs], jax.ShapeDtypeStruct((8, 128), F32)),
        out_specs=(SEM_SPEC, SEM_SPEC, *[HBM_SPEC] * nb, pl.BlockSpec(memory_space=pltpu.VMEM)),
        input_output_aliases={i: i + 2 for i in range(nb)},
        compiler_params=pltpu.CompilerParams(has_side_effects=DATAFLOW),
    )(*[_hbm(b) for b in bufs], *earlier, *after)
    return InFlight(out[0], out[1], list(out[2:2 + nb]), plan, out[-1])


def _split_wait(name, flight, after):
    nb = len(flight.bufs)

    def body(*refs):
        _wait_all(flight.plan, refs[:nb], refs[nb], refs[nb + 1])

    out = pl.pallas_call(
        body, name=name, in_specs=[HBM_SPEC] * nb + [SEM_SPEC, SEM_SPEC, pl.BlockSpec(memory_space=pl.ANY)],
        out_shape=[pltpu.HBM(b.shape, b.dtype) for b in flight.bufs], out_specs=[HBM_SPEC] * nb,
        input_output_aliases={i: i for i in range(nb)},
        compiler_params=pltpu.CompilerParams(has_side_effects=DATAFLOW),
    )(*flight.bufs, flight.send_sem, flight.recv_sem, after)
    return list(out)


def _half_rows(buf, chip, core):
    hr = buf.shape[0] // (2 * N_SHARDS)
    return buf.at[pl.ds(pl.multiple_of((2 * chip + core) * hr, 16), hr)]


def _gather_ici_plan(bufs):
    x, y, c, chips = _place()
    return [(_half_rows(b, 2 * x + y, c), _half_rows(b, 2 * x + y, c), (px, py, c)) for b in bufs for px, py in chips]


def _gather_d2d_plan(bufs):
    x, y, c, chips = _place()
    return [(_half_rows(b, 2 * px + py, c), _half_rows(b, 2 * px + py, c), (x, y, 1 - c)) for b in bufs for px, py in chips]


def _swap_plan(bufs):
    x, y, c, _ = _place()
    n = len(bufs) // 2
    copies = []
    for g, land in zip(bufs[:n], bufs[n:]):
        hr = g.shape[1] // 2
        copies.append((g.at[:, pl.ds(pl.multiple_of((1 - c) * hr, 8), hr)], land, (x, y, 1 - c)))
    return copies


def _exchange_plan(bufs):
    x, y, c, chips = _place()
    n = len(bufs) // 2
    return [(h.at[2 * px + py], land.at[k], (px, py, c))
            for h, land in zip(bufs[:n], bufs[n:]) for k, (px, py) in enumerate(chips)]


def _share_plan(bufs):
    x, y, c, _ = _place()
    copies = []
    for buf in bufs:
        hr = buf.shape[0] // 2
        mine = buf.at[pl.ds(pl.multiple_of(c * hr, 8), hr)]
        copies.append((mine, mine, (x, y, 1 - c)))
    return copies


N_DEVICES = 8


def _slot_place(vec, place):
    R, C = vec.shape

    def body(p_ref, v_ref, o_ref):
        o_ref[0] = v_ref[...]

    grid_spec = pltpu.PrefetchScalarGridSpec(
        num_scalar_prefetch=1, grid=(1,), in_specs=[pl.BlockSpec((R, C), lambda i, p: (0, 0))],
        out_specs=pl.BlockSpec((1, R, C), lambda i, p: (2 * p[0] + p[1], 0, 0)))
    return pl.pallas_call(
        body, name="slot_place", grid_spec=grid_spec, out_shape=jax.ShapeDtypeStruct((N_DEVICES, R, C), F32),
        compiler_params=_params(dimension_semantics=("arbitrary",)),
    )(place, vec)


def _slots_plan(bufs):
    x, y, c, _ = _place()
    mine = bufs[0].at[4 * x + 2 * y + c]
    return [(mine, mine, (x ^ (r >> 2), y ^ ((r >> 1) & 1), c ^ (r & 1))) for r in range(1, N_DEVICES)]


def _sum_slots(slots):
    _, R, C = slots.shape

    def body(s_ref, o_ref):
        acc = s_ref[0]
        for d in range(1, N_DEVICES):
            acc = acc + s_ref[d]
        o_ref[...] = acc

    return pl.pallas_call(
        body, name="sum_slots", in_specs=[pl.BlockSpec(memory_space=pltpu.VMEM)],
        out_specs=pl.BlockSpec(memory_space=pltpu.VMEM), out_shape=jax.ShapeDtypeStruct((R, C), F32),
        compiler_params=_params(),
    )(slots)


SMALL = ("ffn1_norm", "mix_norm", "pool_w", "pool_scale", "q_norm", "k_norm", "sinks", "gate_bias", "ffn2_norm")
SMALL_COLS = 1024
FFN1 = ("ffn1_w_gate", "ffn1_w_up", "ffn1_w_down")
MIXER = ("w_in", "w_pool_out", "w_attn_out", "w_out")
FFN2 = ("ffn2_w_gate", "ffn2_w_up", "ffn2_w_down")
LARGE = FFN1 + MIXER + FFN2
TRANSPOSED = ("ffn1_w_gate", "ffn1_w_up", "w_in", "w_pool_out", "ffn2_w_gate", "ffn2_w_up")
WEIGHTS = ("ffn1_norm", "ffn1_w_gate", "ffn1_w_up", "ffn1_w_down", "mix_norm", "w_in", "pool_w", "pool_scale",
           "w_pool_out", "q_norm", "k_norm", "sinks", "w_attn_out", "gate_bias", "w_out", "ffn2_norm",
           "ffn2_w_gate", "ffn2_w_up", "ffn2_w_down")


def _pack_small(parts):
    flat = jnp.concatenate([p.reshape(-1) for p in parts])
    rows = -(-flat.shape[0] // (8 * SMALL_COLS)) * 8
    return jnp.pad(flat, (0, rows * SMALL_COLS - flat.shape[0])).reshape(rows, SMALL_COLS)


def _unpack_small(packed, like):
    flat = packed.reshape(-1)
    out, off = [], 0
    for p in like:
        out.append(flat[off:off + p.size].reshape(p.shape))
        off += p.size
    return out, flat[off]


def _tie(small, token):
    return small + token[0, 0]


class Reduction:
    def __init__(self, group, names, grads, place):
        self.group, self.names, self.place = group, names, place
        self.bufs = [g.reshape(N_SHARDS, -1, g.shape[-1]) for g in grads]
        self.flight = None

    def _start(self, stage, bufs, plan, n_copies, after):
        self.flight = _split_start(f"{self.group}_{stage}", bufs, plan, n_copies, after)
        return self.flight.token

    def _landed(self, stage, after):
        n = len(self.names)
        bufs = _split_wait(f"{self.group}_{stage}_wait", self.flight, after)
        return bufs[:n], bufs[n:]

    def start_swap(self, after):
        lands = [lax.empty((g.shape[0], g.shape[1] // 2, g.shape[2]), g.dtype) for g in self.bufs]
        return self._start("swap", self.bufs + lands, _swap_plan, len(self.bufs), after)

    def start_exchange(self, after):
        g4, recv = self._landed("swap", after)
        halves = [_sum_halves(g, r, self.place, f"sum_halves_{n}") for n, g, r in zip(self.names, g4, recv)]
        lands = [lax.empty((N_SHARDS - 1,) + h.shape[1:], h.dtype) for h in halves]
        return self._start("exchange", halves + lands, _exchange_plan, (N_SHARDS - 1) * len(halves), self.place)

    def start_share(self, after):
        halves, recv3 = self._landed("exchange", after)
        reduced = [_sum_quarters(h, r, self.place, f"sum_quarters_{n}") for n, h, r in zip(self.names, halves, recv3)]
        return self._start("share", reduced, _share_plan, len(reduced), self.place)

    def finish(self, after):
        return dict(zip(self.names, _split_wait(f"{self.group}_share_wait", self.flight, after)))


def _row_form(name, a):
    return a.T if name in TRANSPOSED else a


def kernel(x, ffn1_norm, ffn1_w_gate, ffn1_w_up, ffn1_w_down, mix_norm, w_in, pool_w, pool_scale, w_pool_out, q_norm, k_norm, sinks, w_attn_out, gate_bias, w_out, ffn2_norm, ffn2_w_gate, ffn2_w_up, ffn2_w_down, loss_target, m_ffn1_norm, m_ffn1_w_gate, m_ffn1_w_up, m_ffn1_w_down, m_mix_norm, m_w_in, m_pool_w, m_pool_scale, m_w_pool_out, m_q_norm, m_k_norm, m_sinks, m_w_attn_out, m_gate_bias, m_w_out, m_ffn2_norm, m_ffn2_w_gate, m_ffn2_w_up, m_ffn2_w_down, v_ffn1_norm, v_ffn1_w_gate, v_ffn1_w_up, v_ffn1_w_down, v_mix_norm, v_w_in, v_pool_w, v_pool_scale, v_w_pool_out, v_q_norm, v_k_norm, v_sinks, v_w_attn_out, v_gate_bias, v_w_out, v_ffn2_norm, v_ffn2_w_gate, v_ffn2_w_up, v_ffn2_w_down):
    args = dict(locals())
    wts = {n: _row_form(n, args[n]) for n in WEIGHTS}
    mom = {n: _row_form(n, args["m_" + n]) for n in WEIGHTS}
    var = {n: _row_form(n, args["v_" + n]) for n in WEIGHTS}
    shard = 2 * lax.axis_index("x") + lax.axis_index("y")
    place = jnp.stack([shard, lax.axis_index("c")]).astype(jnp.int32)

    xs, target = x[0], loss_target[0]
    D = xs.shape[1]
    g1 = wts["ffn1_norm"].reshape(1, D)
    gm = wts["mix_norm"].reshape(1, D)
    g2 = wts["ffn2_norm"].reshape(1, D)
    qw = wts["q_norm"].reshape(1, HEAD_DIM)
    kw = wts["k_norm"].reshape(1, HEAD_DIM)
    bias = wts["gate_bias"].reshape(1, 2 * D)
    pscale = wts["pool_scale"].reshape(1, -1)
    pw = wts["pool_w"].astype(BF16)
    sink_rows = jnp.broadcast_to(jnp.repeat(wts["sinks"], ATTN_BLOCK).reshape(N_KV_HEADS, GQA_GROUP * ATTN_BLOCK, 1),
                                 (N_KV_HEADS, GQA_GROUP * ATTN_BLOCK, 128))

    first, down1 = FFN1[:2], FFN1[2:]
    placed = {n: _cast_place(wts[n], place, "cast_" + n) for n in FFN1}
    first_ici = _split_start("gather_first_ici", [placed[n] for n in first], _gather_ici_plan, 3 * len(first), place)
    placed.update({n: _cast_place(wts[n], place, "cast_" + n) for n in MIXER + FFN2})
    first_d2d = _split_start("gather_first_d2d", first_ici.bufs, _gather_d2d_plan, 3 * len(first),
                             [placed[n] for n in MIXER + FFN2], waits=first_ici)
    down1_ici = _split_start("gather_down1_ici", [placed[n] for n in down1], _gather_ici_plan, 3 * len(down1), first_d2d.token)
    w = dict(zip(first, _split_wait("gather_first_wait", first_d2d, down1_ici.token)))
    mix_ici = _split_start("gather_mix_ici", [placed[n] for n in MIXER], _gather_ici_plan, 3 * len(MIXER), w[first[0]])
    ffn2_ici = _split_start("gather_ffn2_ici", [placed[n] for n in FFN2], _gather_ici_plan, 3 * len(FFN2), mix_ici.token)

    n1t, a1, b1, s1 = _ffn_up(xs, _tie(g1, ffn2_ici.token), w["ffn1_w_gate"], w["ffn1_w_up"], "ffn1_up")
    down1_d2d = _split_start("gather_down1_d2d", down1_ici.bufs, _gather_d2d_plan, 3 * len(down1), s1, waits=down1_ici)
    w.update(zip(down1, _split_wait("gather_down1_wait", down1_d2d, down1_d2d.token)))
    h1 = _ffn_down(xs, s1, w["ffn1_w_down"], "ffn1_down")
    mix_d2d = _split_start("gather_mix_d2d", mix_ici.bufs, _gather_d2d_plan, 3 * len(MIXER), h1, waits=mix_ici)
    w.update(zip(MIXER, _split_wait("gather_mix_wait", mix_d2d, mix_d2d.token)))
    ut, xp, q, kv, gl = _mix_proj(h1, gm, w["w_in"])
    attn = _attn_fwd(q, kv, qw, kw, sink_rows)
    ffn2_d2d = _split_start("gather_ffn2_d2d", ffn2_ici.bufs, _gather_d2d_plan, 3 * len(FFN2), attn, waits=ffn2_ici)
    h2, bp, ba = _mix_out(xp, attn, gl, _tie(bias, ffn2_d2d.token), pw, pscale, w["w_pool_out"], w["w_attn_out"], w["w_out"], h1)
    w.update(zip(FFN2, _split_wait("gather_ffn2_wait", ffn2_d2d, h2)))
    n2t, a2, b2, s2 = _ffn_up(h2, g2, w["ffn2_w_gate"], w["ffn2_w_up"], "ffn2_up")
    dy, dyh, dyht, loss = _ffn_down_loss(h2, s2, w["ffn2_w_down"], target, "ffn2_down_loss")

    gw, gs = {}, {}
    da, db = _ffn_bwd_h(dyh, a2, b2, w["ffn2_w_down"], ffn2_d2d.token, "ffn2_bwd_h")
    gw["ffn2_w_gate"], gw["ffn2_w_up"] = _xty(n2t, da, "ffn2_dw_gate"), _xty(n2t, db, "ffn2_dw_up")
    gw["ffn2_w_down"] = _xty(dyht, s2, "ffn2_dw_down")
    red2 = Reduction("reduce_ffn2", FFN2, [gw[n] for n in FFN2], place)
    token = red2.start_swap(da)
    dh2, gs["ffn2_norm"] = _ffn_bwd_x(da, db, w["ffn2_w_gate"], w["ffn2_w_up"], h2, dy, _tie(g2, token), "ffn2_bwd_x")
    token = red2.start_exchange(dh2)
    dgl, dbp, dba, gw["w_out"], gs["gate_bias"] = _mix_bwd_gate(dh2, w["w_out"], bp, ba, gl, _tie(bias, token))
    dattn, dpooled, gw["w_attn_out"], gw["w_pool_out"], gs["pool_w"], gs["pool_scale"] = _mix_bwd_branch(
        dbp, dba, attn, xp, pw, pscale, w["w_pool_out"], w["w_attn_out"])
    dq, dkv, gs["q_norm"], gs["k_norm"], dsk = _attn_bwd(q, kv, dattn, qw, kw, sink_rows)
    gs["sinks"] = dsk[:, :, 0]
    red2.start_share(dq)
    pieces = (_pool_bwd(dpooled), dq, dkv, dgl)
    gw["w_in"] = _in_bwd_w(pieces, ut)
    grads = red2.finish(gw["w_in"])
    redm = Reduction("reduce_mix", MIXER, [gw[n] for n in MIXER], place)
    token = redm.start_swap(grads[FFN2[0]])
    dh1, dh1h, dh1ht, gs["mix_norm"] = _in_bwd_x(pieces, w["w_in"], h1, dh2, _tie(gm, token))
    token = redm.start_exchange(dh1)
    da, db = _ffn_bwd_h(dh1h, a1, b1, w["ffn1_w_down"], token, "ffn1_bwd_h")
    gw["ffn1_w_gate"], gw["ffn1_w_up"] = _xty(n1t, da, "ffn1_dw_gate"), _xty(n1t, db, "ffn1_dw_up")
    gw["ffn1_w_down"] = _xty(dh1ht, s1, "ffn1_dw_down")
    token = redm.start_share(da)
    red1 = Reduction("reduce_ffn1", FFN1, [gw[n] for n in FFN1], place)
    token = red1.start_swap(token)
    grad_x, gs["ffn1_norm"] = _ffn_bwd_x(da, db, w["ffn1_w_gate"], w["ffn1_w_up"], xs, dh1, _tie(g1, token), "ffn1_bwd_x")
    grads.update(redm.finish(grad_x))

    small_parts = [gs[n] for n in SMALL] + [loss[0, 0].reshape(1)]
    slots = _split_start("gather_small", [_slot_place(_pack_small(small_parts), place)], _slots_plan, N_DEVICES - 1,
                         grads[MIXER[0]])
    token = red1.start_exchange(slots.token)
    delta, new_m, new_v = {}, {}, {}
    for n in FFN2 + MIXER:
        delta[n], new_m[n], new_v[n], grads[n] = _adamw(wts[n], grads[n], mom[n], var[n], token, "adamw_" + n)
    summed = _sum_slots(_split_wait("gather_small_wait", slots, delta[MIXER[-1]])[0])
    small_grads, loss_sum = _unpack_small(summed, [wts[n] for n in SMALL])
    grads.update(dict(zip(SMALL, small_grads)))
    zero = jnp.zeros((1,), F32)
    packed = [_pack_small([d[n] for n in SMALL] + [zero]) for d in (wts, mom, var)]
    ds, ms, vs, _ = _adamw(packed[0], summed, packed[1], packed[2], token, "adamw_small")
    like = [wts[n] for n in SMALL]
    for out, packed_out in ((delta, ds), (new_m, ms), (new_v, vs)):
        out.update(dict(zip(SMALL, _unpack_small(packed_out, like)[0])))
    token = red1.start_share(ds)
    grads.update(red1.finish(token))
    for n in FFN1:
        delta[n], new_m[n], new_v[n], grads[n] = _adamw(wts[n], grads[n], mom[n], var[n], token, "adamw_" + n)

    outs = []
    for d in (grads, delta, new_m, new_v):
        outs += [_row_form(n, d[n]) for n in WEIGHTS]
    return (loss_sum, grad_x[None], *outs)
```

```python
from typing import Callable, NamedTuple

import jax
import jax.numpy as jnp
from jax import lax
from jax.experimental import pallas as pl
from jax.experimental.pallas import tpu as pltpu

F32 = jnp.float32
BF16 = jnp.bfloat16
RMS_EPS = 1e-6
POOL_WINDOWS = (2, 4, 8, 16)
POOL_GROUP = 128
POOL_HALO = 16
HEAD_DIM = 64
GQA_GROUP = 8
N_KV_HEADS = 2
ATTN_BLOCK = 128
SCALE = HEAD_DIM ** -0.5
NEG = -1e30
N_SHARDS = 4
ADAM_LR, ADAM_B1, ADAM_B2, ADAM_EPS, ADAM_WD, ADAM_STEP = 0.001, 0.9, 0.999, 1e-08, 0.01, 10
VMEM_LIMIT = 56 * 1024 * 1024
MESH = pl.DeviceIdType.MESH
SEG_POOL, SEG_Q, SEG_KV, SEG_GATE = (0, 512), (512, 1024), (1536, 256), (1792, 2048)
SEGMENTS = (SEG_POOL, SEG_Q, SEG_KV, SEG_GATE)


def _params(**kw):
    return pltpu.CompilerParams(vmem_limit_bytes=VMEM_LIMIT, **kw)


def _dot(a, b):
    return jnp.dot(a, b, preferred_element_type=F32)


def _dot_nt(a, b):
    return lax.dot_general(a, b, (((1,), (1,)), ((), ())), preferred_element_type=F32)


def _dot_tn(a, b):
    return lax.dot_general(a, b, (((0,), (0,)), ((), ())), preferred_element_type=F32)


def _rinv(x):
    return lax.rsqrt(jnp.mean(x * x, axis=-1, keepdims=True) + RMS_EPS)


def _rms_bwd(dn, x, g):
    r = _rinv(x)
    xr = x * r
    z = dn * g
    dx = r * (z - xr * jnp.mean(z * xr, axis=-1, keepdims=True))
    return dx, dn * xr


def _acc(ref, val, first):
    @pl.when(first)
    def _():
        ref[...] = val

    @pl.when(jnp.logical_not(first))
    def _():
        ref[...] += val


TOKEN_SPEC = pl.BlockSpec(memory_space=pl.ANY)
F_HALF = 1408


def _resident(w):
    return pl.BlockSpec(w.shape, lambda i: (0, 0), pipeline_mode=pl.Buffered(1))


def _ffn_up(h, gnorm, wgt, wut, name, tm=512):
    S, D = h.shape
    F = wgt.shape[0]

    def body(h_ref, g_ref, wg_ref, wu_ref, nt_ref, a_ref, b_ref, s_ref):
        x = h_ref[...]
        nv = x * _rinv(x) * g_ref[...]
        nt_ref[...] = nv.T.astype(BF16)
        n = nv.astype(BF16)
        for lo in range(0, F, F_HALF):
            cols = slice(lo, lo + F_HALF)
            a = _dot_nt(n, wg_ref[cols, :])
            b = _dot_nt(n, wu_ref[cols, :])
            a_ref[:, cols] = a.astype(BF16)
            b_ref[:, cols] = b.astype(BF16)
            s_ref[:, cols] = (a * jax.nn.sigmoid(a) * b).astype(BF16)

    act = pl.BlockSpec((tm, F), lambda i: (i, 0))
    hidden = jax.ShapeDtypeStruct((S, F), BF16)
    return pl.pallas_call(
        body, name=name, grid=(S // tm,),
        in_specs=[pl.BlockSpec((tm, D), lambda i: (i, 0)), pl.BlockSpec((1, D), lambda i: (0, 0)), _resident(wgt), _resident(wut)],
        out_specs=[pl.BlockSpec((D, tm), lambda i: (0, i)), act, act, act],
        out_shape=[jax.ShapeDtypeStruct((D, S), BF16), hidden, hidden, hidden],
        compiler_params=_params(dimension_semantics=("arbitrary",)),
    )(h, gnorm, wgt, wut)


def _ffn_down(h, s, wd, name, tm=1024):
    S, D = h.shape
    F = wd.shape[0]

    def body(h_ref, s_ref, wd_ref, o_ref):
        o_ref[...] = h_ref[...] + 0.5 * _dot(s_ref[...], wd_ref[...])

    row = pl.BlockSpec((tm, D), lambda i: (i, 0))
    return pl.pallas_call(
        body, name=name, grid=(S // tm,), in_specs=[row, pl.BlockSpec((tm, F), lambda i: (i, 0)), _resident(wd)],
        out_specs=row, out_shape=jax.ShapeDtypeStruct((S, D), F32),
        compiler_params=_params(dimension_semantics=("arbitrary",)),
    )(h, s, wd)


def _ffn_down_loss(h, s, wd, target, name, tm=512):
    S, D = h.shape
    F = wd.shape[0]

    def body(h_ref, s_ref, wd_ref, t_ref, dy_ref, dyh_ref, dyht_ref, loss_ref):
        e = h_ref[...] + 0.5 * _dot(s_ref[...], wd_ref[...]) - t_ref[...]
        dy = e * (1.0 / D)
        dy_ref[...] = dy
        dyh_ref[...] = (0.5 * dy).astype(BF16)
        dyht_ref[...] = (0.5 * dy).T.astype(BF16)
        tot = jnp.sum(jnp.sum(e * e, axis=1, keepdims=True), axis=0, keepdims=True) * (0.5 / D)
        _acc(loss_ref, jnp.broadcast_to(tot, loss_ref.shape), pl.program_id(0) == 0)

    row = pl.BlockSpec((tm, D), lambda i: (i, 0))
    col = pl.BlockSpec((D, tm), lambda i: (0, i))
    return pl.pallas_call(
        body, name=name, grid=(S // tm,), in_specs=[row, pl.BlockSpec((tm, F), lambda i: (i, 0)), _resident(wd), row],
        out_specs=[row, row, col, pl.BlockSpec((8, 128), lambda i: (0, 0))],
        out_shape=[jax.ShapeDtypeStruct((S, D), F32), jax.ShapeDtypeStruct((S, D), BF16), jax.ShapeDtypeStruct((D, S), BF16),
                   jax.ShapeDtypeStruct((8, 128), F32)],
        compiler_params=_params(dimension_semantics=("arbitrary",)),
    )(h, s, wd, target)


def _ffn_bwd_h(dyh, a, b, wd, token, name, tm=512):
    S, D = dyh.shape
    F = wd.shape[0]

    def body(dyh_ref, a_ref, b_ref, wd_ref, _, da_ref, db_ref):
        dyh_t = dyh_ref[...]
        for lo in range(0, F, F_HALF):
            cols = slice(lo, lo + F_HALF)
            av = a_ref[:, cols].astype(F32)
            bv = b_ref[:, cols].astype(F32)
            ds = _dot_nt(dyh_t, wd_ref[cols, :])
            sig = jax.nn.sigmoid(av)
            silu = av * sig
            da_ref[:, cols] = (ds * bv * (sig * (1.0 + av * (1.0 - sig)))).astype(BF16)
            db_ref[:, cols] = (ds * silu).astype(BF16)

    act = pl.BlockSpec((tm, F), lambda i: (i, 0))
    hidden = jax.ShapeDtypeStruct((S, F), BF16)
    return pl.pallas_call(
        body, name=name, grid=(S // tm,),
        in_specs=[pl.BlockSpec((tm, D), lambda i: (i, 0)), act, act, _resident(wd), TOKEN_SPEC],
        out_specs=[act, act], out_shape=[hidden, hidden],
        compiler_params=_params(dimension_semantics=("arbitrary",)),
    )(dyh, a, b, wd, token)


def _xty(yt, x, name, tk=2048, tf=1408, chunk=128):
    D, S = yt.shape
    F = x.shape[1]
    nk, nf = S // tk, F // tf
    nchunk = tf // chunk

    def body(yt_ref, x_ref, out_hbm, acc, stage, sem):
        j = pl.program_id(0)
        k = pl.program_id(1)
        _acc(acc, _dot(yt_ref[...], x_ref[...]), k == 0)

        @pl.when(k == nk - 1)
        def _():
            def out_copy(c, slot):
                rows = pl.ds(pl.multiple_of(j * tf + c * chunk, chunk), chunk)
                return pltpu.make_async_copy(stage.at[slot], out_hbm.at[rows], sem.at[slot])

            for c in range(nchunk):
                slot = c % 2
                if c >= 2:
                    out_copy(c - 2, slot).wait()
                stage[slot] = acc[:, c * chunk:(c + 1) * chunk].T
                out_copy(c, slot).start()
            for c in range(nchunk - 2, nchunk):
                out_copy(c, c % 2).wait()

    return pl.pallas_call(
        body, name=name, grid=(nf, nk),
        in_specs=[pl.BlockSpec((D, tk), lambda j, k: (0, k)), pl.BlockSpec((tk, tf), lambda j, k: (k, j))],
        out_specs=pl.BlockSpec(memory_space=pl.ANY), out_shape=jax.ShapeDtypeStruct((F, D), F32),
        scratch_shapes=[pltpu.VMEM((D, tf), F32), pltpu.VMEM((2, chunk, D), F32), pltpu.SemaphoreType.DMA((2,))],
        compiler_params=_params(dimension_semantics=("arbitrary", "arbitrary")),
    )(yt, x)


def _ffn_bwd_x(da, db, wgt, wut, x, dh, gnorm, name, tm=512):
    S, D = x.shape
    F = wgt.shape[0]

    def body(da_ref, db_ref, wg_ref, wu_ref, x_ref, dh_ref, g_ref, dx_ref, dg_ref):
        dn = _dot(da_ref[...], wg_ref[...]) + _dot(db_ref[...], wu_ref[...])
        dx, dg_rows = _rms_bwd(dn, x_ref[...], g_ref[...])
        dx_ref[...] = dh_ref[...] + dx
        _acc(dg_ref, jnp.sum(dg_rows, axis=0, keepdims=True), pl.program_id(0) == 0)

    row = pl.BlockSpec((tm, D), lambda i: (i, 0))
    act = pl.BlockSpec((tm, F), lambda i: (i, 0))
    vec = pl.BlockSpec((1, D), lambda i: (0, 0))
    return pl.pallas_call(
        body, name=name, grid=(S // tm,), in_specs=[act, act, _resident(wgt), _resident(wut), row, row, vec],
        out_specs=[row, vec],
        out_shape=[jax.ShapeDtypeStruct((S, D), F32), jax.ShapeDtypeStruct((1, D), F32)],
        compiler_params=_params(dimension_semantics=("arbitrary",)),
    )(da, db, wgt, wut, x, dh, gnorm)


def _mix_proj(h, gnorm, wint, tm=512):
    S, D = h.shape
    nt = S // tm

    def body(h_ref, g_ref, w_ref, ut_ref, xp_ref, q_ref, kv_ref, gl_ref):
        x = h_ref[...]
        uv = x * _rinv(x) * g_ref[...]
        ut_ref[...] = uv.T.astype(BF16)
        u = uv.astype(BF16)
        for (off, width), ref in zip(SEGMENTS, (xp_ref, q_ref, kv_ref, gl_ref)):
            ref[...] = _dot_nt(u, w_ref[off:off + width, :]).astype(ref.dtype)

    def row(width):
        return pl.BlockSpec((tm, width), lambda i: (i, 0))

    return pl.pallas_call(
        body, name="mix_proj", grid=(nt,),
        in_specs=[row(D), pl.BlockSpec((1, D), lambda i: (0, 0)), pl.BlockSpec(wint.shape, lambda i: (0, 0))],
        out_specs=[pl.BlockSpec((D, tm), lambda i: (0, i)), row(SEG_POOL[1]), row(SEG_Q[1]), row(SEG_KV[1]), row(SEG_GATE[1])],
        out_shape=[jax.ShapeDtypeStruct((D, S), BF16), jax.ShapeDtypeStruct((S, SEG_POOL[1]), F32),
                   jax.ShapeDtypeStruct((S, SEG_Q[1]), BF16), jax.ShapeDtypeStruct((S, SEG_KV[1]), BF16),
                   jax.ShapeDtypeStruct((S, SEG_GATE[1]), BF16)],
        compiler_params=_params(dimension_semantics=("arbitrary",)),
    )(h, gnorm, wint)


def _stack_heads(x, g):
    return jnp.concatenate([x[:, (GQA_GROUP * g + hh) * HEAD_DIM:(GQA_GROUP * g + hh + 1) * HEAD_DIM]
                            for hh in range(GQA_GROUP)], axis=0)


def _unstack_heads(ref, val, g):
    for hh in range(GQA_GROUP):
        lo = (GQA_GROUP * g + hh) * HEAD_DIM
        ref[:, lo:lo + HEAD_DIM] = val[hh * ATTN_BLOCK:(hh + 1) * ATTN_BLOCK, :]


def _rowsum(xb, width):
    return _dot(xb, jnp.ones((xb.shape[1], width), BF16))


def _rinv_lanes(x):
    return lax.rsqrt(_rowsum((x * x).astype(BF16), x.shape[1]) * (1.0 / x.shape[1]) + RMS_EPS)


def _attn_exp(qn, kc, kp, sink, n):
    rows = GQA_GROUP * ATTN_BLOCK
    qi = lax.broadcasted_iota(jnp.int32, (rows, ATTN_BLOCK), 0) % ATTN_BLOCK
    kj = lax.broadcasted_iota(jnp.int32, (rows, ATTN_BLOCK), 1)
    s_c = jnp.where(kj <= qi, _dot_nt(qn, kc), NEG)
    s_p = jnp.where(jnp.logical_and(kj > qi, n > 0), _dot_nt(qn, kp), NEG)
    m = jnp.maximum(jnp.max(s_c, axis=-1, keepdims=True), jnp.max(s_p, axis=-1, keepdims=True))
    m = jnp.maximum(jnp.broadcast_to(m, sink.shape), sink)
    e_c = jnp.exp(s_c - m)
    e_p = jnp.exp(s_p - m)
    e_s = jnp.exp(sink - m)
    e_cb, e_pb = e_c.astype(BF16), e_p.astype(BF16)
    inv = 1.0 / (_rowsum(jnp.concatenate([e_cb, e_pb], axis=1), ATTN_BLOCK) + e_s)
    return e_cb, e_pb, e_c, e_p, e_s, inv


def _attn_blocks(n):
    cur = pl.multiple_of(n * ATTN_BLOCK, ATTN_BLOCK)
    prev = pl.multiple_of(jnp.maximum(n - 1, 0) * ATTN_BLOCK, ATTN_BLOCK)
    return cur, prev


def _kv_split(kv, g):
    k = kv[:, g * HEAD_DIM:(g + 1) * HEAD_DIM]
    v = kv[:, (N_KV_HEADS + g) * HEAD_DIM:(N_KV_HEADS + g + 1) * HEAD_DIM]
    return k, v


def _attn_fwd(q, kv, qw, kw, sink_rows):
    S, W = q.shape
    nb = S // ATTN_BLOCK

    def body(q_ref, kv_ref, qw_ref, kw_ref, sk_ref, o_ref, o_scr):
        n = pl.program_id(0)
        qf = q_ref[...].astype(F32)
        cur, prev = _attn_blocks(n)
        kvc = kv_ref[pl.ds(cur, ATTN_BLOCK), :].astype(F32)
        kvp = kv_ref[pl.ds(prev, ATTN_BLOCK), :].astype(F32)
        for g in range(N_KV_HEADS):
            qs = _stack_heads(qf, g)
            qn = (qs * _rinv_lanes(qs) * qw_ref[...] * SCALE).astype(BF16)
            kc, vc = _kv_split(kvc, g)
            kp, vp = _kv_split(kvp, g)
            kc = (kc * _rinv_lanes(kc) * kw_ref[...]).astype(BF16)
            kp = (kp * _rinv_lanes(kp) * kw_ref[...]).astype(BF16)
            e_cb, e_pb, _, _, _, inv = _attn_exp(qn, kc, kp, sk_ref[g], n)
            o = (_dot(e_cb, vc.astype(BF16)) + _dot(e_pb, vp.astype(BF16))) * inv[:, :HEAD_DIM]
            _unstack_heads(o_scr, o, g)
        o_ref[...] = o_scr[...].astype(BF16)

    blk = pl.BlockSpec((ATTN_BLOCK, W), lambda n: (n, 0))
    return pl.pallas_call(
        body, name="attn_fwd", grid=(nb,),
        in_specs=[blk, pl.BlockSpec(kv.shape, lambda n: (0, 0)), pl.BlockSpec((1, HEAD_DIM), lambda n: (0, 0)),
                  pl.BlockSpec((1, HEAD_DIM), lambda n: (0, 0)), pl.BlockSpec(sink_rows.shape, lambda n: (0, 0, 0))],
        out_specs=blk, out_shape=jax.ShapeDtypeStruct((S, W), BF16),
        scratch_shapes=[pltpu.VMEM((ATTN_BLOCK, W), F32)],
        compiler_params=_params(dimension_semantics=("arbitrary",)),
    )(q, kv, qw, kw, sink_rows)


def _attn_bwd(q, kv, do, qw, kw, sink_rows):
    S, W = q.shape
    KW = kv.shape[1]
    nb = S // ATTN_BLOCK
    chunk = 512

    def body(q_ref, kv_ref, do_ref, qw_ref, kw_ref, sk_ref, dq_ref, dkv_ref, dqw_ref, dkw_ref, dsk_ref, dq_scr):
        n = pl.program_id(0)

        @pl.when(n == 0)
        def _():
            dkv_ref[...] = jnp.zeros_like(dkv_ref)
            dqw_ref[...] = jnp.zeros_like(dqw_ref)
            dsk_ref[...] = jnp.zeros_like(dsk_ref)

        qf = q_ref[...].astype(F32)
        dof = do_ref[...].astype(F32)
        cur, prev = _attn_blocks(n)
        kvc = kv_ref[pl.ds(cur, ATTN_BLOCK), :].astype(F32)
        kvp = kv_ref[pl.ds(prev, ATTN_BLOCK), :].astype(F32)
        qw_v = qw_ref[...]
        for g in range(N_KV_HEADS):
            qs = _stack_heads(qf, g)
            rq = _rinv_lanes(qs)
            qhat = qs * rq
            qn = (qhat * qw_v * SCALE).astype(BF16)
            kc, vc = _kv_split(kvc, g)
            kp, vp = _kv_split(kvp, g)
            kc = (kc * _rinv_lanes(kc) * kw_ref[...]).astype(BF16)
            kp = (kp * _rinv_lanes(kp) * kw_ref[...]).astype(BF16)
            vc = vc.astype(BF16)
            vp = vp.astype(BF16)
            dos = _stack_heads(dof, g).astype(BF16)
            _, _, e_c, e_p, e_s, inv = _attn_exp(qn, kc, kp, sk_ref[g], n)
            p_c, p_p, p_s = e_c * inv, e_p * inv, e_s * inv
            dp_c = _dot_nt(dos, vc)
            dp_p = _dot_nt(dos, vp)
            drow = _rowsum(jnp.concatenate([(p_c * dp_c).astype(BF16), (p_p * dp_p).astype(BF16)], axis=1), ATTN_BLOCK)
            ds_c = (p_c * (dp_c - drow)).astype(BF16)
            ds_p = (p_p * (dp_p - drow)).astype(BF16)
            dsink = -(p_s * drow)
            for hh in range(GQA_GROUP):
                tot = jnp.sum(dsink[hh * ATTN_BLOCK:(hh + 1) * ATTN_BLOCK, :], axis=0, keepdims=True)
                dsk_ref[g, hh:hh + 1, :] += tot
            dqn = (_dot(ds_c, kc) + _dot(ds_p, kp)) * SCALE
            klo, vlo = g * HEAD_DIM, (N_KV_HEADS + g) * HEAD_DIM
            dkv_ref[pl.ds(cur, ATTN_BLOCK), klo:klo + HEAD_DIM] += _dot_tn(ds_c, qn)
            dkv_ref[pl.ds(prev, ATTN_BLOCK), klo:klo + HEAD_DIM] += _dot_tn(ds_p, qn)
            dkv_ref[pl.ds(cur, ATTN_BLOCK), vlo:vlo + HEAD_DIM] += _dot_tn(p_c.astype(BF16), dos)
            dkv_ref[pl.ds(prev, ATTN_BLOCK), vlo:vlo + HEAD_DIM] += _dot_tn(p_p.astype(BF16), dos)
            dqw_ref[...] += jnp.sum(dqn * qhat, axis=0, keepdims=True)
            z = dqn * qw_v
            dqs = rq * (z - qhat * (_rowsum((z * qhat).astype(BF16), HEAD_DIM) * (1.0 / HEAD_DIM)))
            _unstack_heads(dq_scr, dqs, g)
        dq_ref[...] = dq_scr[...].astype(BF16)

        @pl.when(n == nb - 1)
        def _():
            def one(c, dkw):
                rows = pl.ds(pl.multiple_of(c * chunk, chunk), chunk)
                for g in range(N_KV_HEADS):
                    lo = g * HEAD_DIM
                    k = kv_ref[rows, lo:lo + HEAD_DIM].astype(F32)
                    dx, dg_rows = _rms_bwd(dkv_ref[rows, lo:lo + HEAD_DIM], k, kw_ref[...])
                    dkv_ref[rows, lo:lo + HEAD_DIM] = dx
                    dkw = dkw + jnp.sum(dg_rows, axis=0, keepdims=True)
                return dkw

            dkw_ref[...] = lax.fori_loop(0, S // chunk, one, jnp.zeros((1, HEAD_DIM), F32))

    blk = pl.BlockSpec((ATTN_BLOCK, W), lambda n: (n, 0))
    whole_kv = pl.BlockSpec((S, KW), lambda n: (0, 0))
    vec = pl.BlockSpec((1, HEAD_DIM), lambda n: (0, 0))
    sk = pl.BlockSpec(sink_rows.shape, lambda n: (0, 0, 0))
    dsk = pl.BlockSpec((N_KV_HEADS, GQA_GROUP, 128), lambda n: (0, 0, 0))
    return pl.pallas_call(
        body, name="attn_bwd", grid=(nb,), in_specs=[blk, whole_kv, blk, vec, vec, sk],
        out_specs=[blk, whole_kv, vec, vec, dsk],
        out_shape=[jax.ShapeDtypeStruct((S, W), BF16), jax.ShapeDtypeStruct((S, KW), F32),
                   jax.ShapeDtypeStruct((1, HEAD_DIM), F32), jax.ShapeDtypeStruct((1, HEAD_DIM), F32),
                   jax.ShapeDtypeStruct((N_KV_HEADS, GQA_GROUP, 128), F32)],
        scratch_shapes=[pltpu.VMEM((ATTN_BLOCK, W), F32)],
        compiler_params=_params(dimension_semantics=("arbitrary",)),
    )(q, kv, do, qw, kw, sink_rows)


def _pooled(xc, xprev, i):
    tm = xc.shape[0]
    xh = jnp.concatenate([jnp.where(i > 0, xprev, 0.0), xc], axis=0)
    t = lax.broadcasted_iota(jnp.int32, (tm, 1), 0) + i * tm
    out = []
    for gi, w in enumerate(POOL_WINDOWS):
        acc = xh[:, gi * POOL_GROUP:(gi + 1) * POOL_GROUP]
        sh = 1
        while sh < w:
            acc = acc + pltpu.roll(acc, sh, 0)
            sh *= 2
        cnt = jnp.minimum(t + 1, w).astype(F32)
        out.append(acc[POOL_HALO:, :] / cnt - xc[:, gi * POOL_GROUP:(gi + 1) * POOL_GROUP])
    return jnp.concatenate(out, axis=1)


def _pool_mix(pooled_b, pw_ref):
    return jnp.concatenate([_dot(pooled_b[:, gi * POOL_GROUP:(gi + 1) * POOL_GROUP], pw_ref[gi])
                            for gi in range(len(POOL_WINDOWS))], axis=1)


def _halo_specs(tm, width, S, after):
    per = tm // POOL_HALO
    last = S // POOL_HALO - 1
    if after:
        return pl.BlockSpec((POOL_HALO, width), lambda i: (jnp.minimum((i + 1) * per, last), 0))
    return pl.BlockSpec((POOL_HALO, width), lambda i: (jnp.maximum(i * per - 1, 0), 0))


def _mix_out(xp, attn, gl, bias, pw, pscale, wpot, wao, wo, h, tm=512):
    S, D = h.shape
    nt = S // tm
    PW = xp.shape[1]

    def body(xc_ref, xprev_ref, at_ref, gl_ref, bias_ref, pw_ref, ps_ref, wpo_ref, wao_ref, wo_ref, h_ref,
             ho_ref, bp_ref, ba_ref):
        i = pl.program_id(0)
        pooled = _pooled(xc_ref[...], xprev_ref[...], i).astype(BF16)
        ms = (_pool_mix(pooled, pw_ref) * ps_ref[...]).astype(BF16)
        bp = _dot_nt(ms, wpo_ref[...])
        ba = _dot(at_ref[...], wao_ref[...])
        bp_ref[...] = bp.astype(BF16)
        ba_ref[...] = ba.astype(BF16)
        gates = jax.nn.sigmoid(gl_ref[...].astype(F32) + bias_ref[...])
        merged = (gates[:, :D] * bp + gates[:, D:] * ba).astype(BF16)
        ho_ref[...] = h_ref[...] + _dot(merged, wo_ref[...])

    def row(width):
        return pl.BlockSpec((tm, width), lambda i: (i, 0))

    def whole(x):
        nd = x.ndim
        return pl.BlockSpec(x.shape, lambda i: (0,) * nd)

    return pl.pallas_call(
        body, name="mix_out", grid=(nt,),
        in_specs=[row(PW), _halo_specs(tm, PW, S, False), row(D), row(2 * D), whole(bias), whole(pw), whole(pscale),
                  whole(wpot), whole(wao), whole(wo), row(D)],
        out_specs=[row(D), row(D), row(D)],
        out_shape=[jax.ShapeDtypeStruct((S, D), F32), jax.ShapeDtypeStruct((S, D), BF16),
                   jax.ShapeDtypeStruct((S, D), BF16)],
        compiler_params=_params(dimension_semantics=("arbitrary",)),
    )(xp, xp, attn, gl, bias, pw, pscale, wpot, wao, wo, h)


def _mix_bwd_gate(dh, wo, bp, ba, gl, bias, tm=512):
    S, D = dh.shape
    nt = S // tm

    def body(dh_ref, wo_ref, bp_ref, ba_ref, gl_ref, bias_ref, dgl_ref, dbp_ref, dba_ref, dwo_ref, dbias_ref):
        i = pl.program_id(0)
        dhb = dh_ref[...].astype(BF16)
        dm = _dot_nt(dhb, wo_ref[...])
        gates = jax.nn.sigmoid(gl_ref[...].astype(F32) + bias_ref[...])
        gp, ga = gates[:, :D], gates[:, D:]
        bp_v = bp_ref[...].astype(F32)
        ba_v = ba_ref[...].astype(F32)
        merged = (gp * bp_v + ga * ba_v).astype(BF16)
        _acc(dwo_ref, _dot_tn(merged, dhb), i == 0)
        dbp_ref[...] = (dm * gp).astype(BF16)
        dba_ref[...] = (dm * ga).astype(BF16)
        dgl = jnp.concatenate([dm * bp_v * gp * (1.0 - gp), dm * ba_v * ga * (1.0 - ga)], axis=1)
        dgl_ref[...] = dgl.astype(BF16)
        _acc(dbias_ref, jnp.sum(dgl, axis=0, keepdims=True), i == 0)

    def row(width):
        return pl.BlockSpec((tm, width), lambda i: (i, 0))

    def whole(shape):
        return pl.BlockSpec(shape, lambda i: (0, 0))

    return pl.pallas_call(
        body, name="mix_bwd_gate", grid=(nt,),
        in_specs=[row(D), whole(wo.shape), row(D), row(D), row(2 * D), whole(bias.shape)],
        out_specs=[row(2 * D), row(D), row(D), whole((D, D)), whole((1, 2 * D))],
        out_shape=[jax.ShapeDtypeStruct((S, 2 * D), BF16), jax.ShapeDtypeStruct((S, D), BF16),
                   jax.ShapeDtypeStruct((S, D), BF16), jax.ShapeDtypeStruct((D, D), F32),
                   jax.ShapeDtypeStruct((1, 2 * D), F32)],
        compiler_params=_params(dimension_semantics=("arbitrary",)),
    )(dh, wo, bp, ba, gl, bias)


def _mix_bwd_branch(dbp, dba, attn, xp, pw, pscale, wpot, wao, tm=512):
    S, D = dbp.shape
    nt = S // tm
    PW = xp.shape[1]
    NG = len(POOL_WINDOWS)

    def body(dbp_ref, dba_ref, at_ref, xc_ref, xprev_ref, pw_ref, ps_ref, wpo_ref, wao_ref,
             dat_ref, dpl_ref, dwao_ref, dwpo_ref, dpw_ref, dps_ref):
        i = pl.program_id(0)
        dba_v = dba_ref[...]
        dbp_v = dbp_ref[...]
        _acc(dwao_ref, _dot_tn(at_ref[...], dba_v), i == 0)
        dat_ref[...] = _dot_nt(dba_v, wao_ref[...]).astype(BF16)
        pooled = _pooled(xc_ref[...], xprev_ref[...], i).astype(BF16)
        mixed = _pool_mix(pooled, pw_ref)
        ps = ps_ref[...]
        _acc(dwpo_ref, _dot_tn(dbp_v, (mixed * ps).astype(BF16)), i == 0)
        dms = _dot(dbp_v, wpo_ref[...])
        _acc(dps_ref, jnp.sum(dms * mixed, axis=0, keepdims=True), i == 0)
        dmixed = (dms * ps).astype(BF16)
        dpooled = []
        for gi in range(NG):
            cols = slice(gi * POOL_GROUP, (gi + 1) * POOL_GROUP)
            _acc(dpw_ref.at[gi], _dot_tn(pooled[:, cols], dmixed[:, cols]), i == 0)
            dpooled.append(_dot_nt(dmixed[:, cols], pw_ref[gi]))
        dpl_ref[...] = jnp.concatenate(dpooled, axis=1)

    def row(width):
        return pl.BlockSpec((tm, width), lambda i: (i, 0))

    def whole(shape):
        nd = len(shape)
        return pl.BlockSpec(shape, lambda i: (0,) * nd)

    return pl.pallas_call(
        body, name="mix_bwd_branch", grid=(nt,),
        in_specs=[row(D), row(D), row(D), row(PW), _halo_specs(tm, PW, S, False), whole(pw.shape), whole(pscale.shape),
                  whole(wpot.shape), whole(wao.shape)],
        out_specs=[row(D), row(PW), whole((D, D)), whole((D, PW)), whole(pw.shape), whole((1, PW))],
        out_shape=[jax.ShapeDtypeStruct((S, D), BF16), jax.ShapeDtypeStruct((S, PW), F32),
                   jax.ShapeDtypeStruct((D, D), F32), jax.ShapeDtypeStruct((D, PW), F32),
                   jax.ShapeDtypeStruct(pw.shape, F32), jax.ShapeDtypeStruct((1, PW), F32)],
        compiler_params=_params(dimension_semantics=("arbitrary",)),
    )(dbp, dba, attn, xp, xp, pw, pscale, wpot, wao)


def _pool_bwd(dpooled, tm=512):
    S, PW = dpooled.shape
    nt = S // tm

    def body(dc_ref, dnext_ref, dxp_ref):
        i = pl.program_id(0)
        dc = dc_ref[...]
        dh = jnp.concatenate([dc, jnp.where(i < nt - 1, dnext_ref[...], 0.0)], axis=0)
        rows = tm + POOL_HALO
        t = lax.broadcasted_iota(jnp.int32, (rows, 1), 0) + i * tm
        out = []
        for gi, w in enumerate(POOL_WINDOWS):
            cols = slice(gi * POOL_GROUP, (gi + 1) * POOL_GROUP)
            acc = dh[:, cols] / jnp.minimum(t + 1, w).astype(F32)
            sh = 1
            while sh < w:
                acc = acc + pltpu.roll(acc, rows - sh, 0)
                sh *= 2
            out.append(acc[:tm, :] - dc[:, cols])
        dxp_ref[...] = jnp.concatenate(out, axis=1).astype(BF16)

    return pl.pallas_call(
        body, name="pool_bwd", grid=(nt,),
        in_specs=[pl.BlockSpec((tm, PW), lambda i: (i, 0)), _halo_specs(tm, PW, S, True)],
        out_specs=pl.BlockSpec((tm, PW), lambda i: (i, 0)), out_shape=jax.ShapeDtypeStruct((S, PW), BF16),
        compiler_params=_params(dimension_semantics=("arbitrary",)),
    )(dpooled, dpooled)


def _in_bwd_w(pieces, ut, tm=1024, chunk=256):
    D, S = ut.shape
    nt = S // tm
    NW = sum(width for _, width in SEGMENTS)
    nchunk = NW // chunk

    def body(*refs):
        piece_refs, ut_ref, dw_hbm, acc, stage, sem = (refs[:len(SEGMENTS)], refs[len(SEGMENTS)], refs[len(SEGMENTS) + 1],
                                                      refs[-3], refs[-2], refs[-1])
        i = pl.program_id(0)
        u_t = ut_ref[...]
        for (off, width), ref in zip(SEGMENTS, piece_refs):
            for lo in range(0, width, D):
                hi = min(lo + D, width)
                _acc(acc.at[:, off + lo:off + hi], _dot(u_t, ref[:, lo:hi].astype(BF16)), i == 0)

        @pl.when(i == nt - 1)
        def _():
            def out_copy(c, slot):
                return pltpu.make_async_copy(stage.at[slot], dw_hbm.at[pl.ds(c * chunk, chunk)], sem.at[slot])

            for c in range(nchunk):
                slot = c % 2
                if c >= 2:
                    out_copy(c - 2, slot).wait()
                stage[slot] = acc[:, c * chunk:(c + 1) * chunk].T
                out_copy(c, slot).start()
            for c in range(nchunk - 2, nchunk):
                out_copy(c, c % 2).wait()

    return pl.pallas_call(
        body, name="in_bwd_w", grid=(nt,),
        in_specs=[pl.BlockSpec((tm, width), lambda i: (i, 0)) for _, width in SEGMENTS] + [pl.BlockSpec((D, tm), lambda i: (0, i))],
        out_specs=pl.BlockSpec(memory_space=pl.ANY), out_shape=jax.ShapeDtypeStruct((NW, D), F32),
        scratch_shapes=[pltpu.VMEM((D, NW), F32), pltpu.VMEM((2, chunk, D), F32), pltpu.SemaphoreType.DMA((2,))],
        compiler_params=_params(dimension_semantics=("arbitrary",)),
    )(*pieces, ut)


def _in_bwd_x(pieces, wint, h, dh, gnorm, tm=512):
    S, D = h.shape
    nt = S // tm

    def body(*refs):
        piece_refs = refs[:len(SEGMENTS)]
        w_ref, h_ref, dh_ref, g_ref, dx_ref, dxh_ref, dxht_ref, dg_ref = refs[len(SEGMENTS):]
        i = pl.program_id(0)
        du = jnp.zeros((tm, D), F32)
        for (off, width), ref in zip(SEGMENTS, piece_refs):
            du = du + _dot(ref[...].astype(BF16), w_ref[off:off + width, :])
        dx, dg_rows = _rms_bwd(du, h_ref[...], g_ref[...])
        out = dh_ref[...] + dx
        dx_ref[...] = out
        dxh_ref[...] = (0.5 * out).astype(BF16)
        dxht_ref[...] = (0.5 * out).T.astype(BF16)
        _acc(dg_ref, jnp.sum(dg_rows, axis=0, keepdims=True), i == 0)

    def row(width):
        return pl.BlockSpec((tm, width), lambda i: (i, 0))

    vec = pl.BlockSpec((1, D), lambda i: (0, 0))
    return pl.pallas_call(
        body, name="in_bwd_x", grid=(nt,),
        in_specs=[row(width) for _, width in SEGMENTS] + [pl.BlockSpec(wint.shape, lambda i: (0, 0)), row(D), row(D), vec],
        out_specs=[row(D), row(D), pl.BlockSpec((D, tm), lambda i: (0, i)), vec],
        out_shape=[jax.ShapeDtypeStruct((S, D), F32), jax.ShapeDtypeStruct((S, D), BF16),
                   jax.ShapeDtypeStruct((D, S), BF16), jax.ShapeDtypeStruct((1, D), F32)],
        compiler_params=_params(dimension_semantics=("arbitrary",)),
    )(*pieces, wint, h, dh, gnorm)


def _row_tile(rows):
    for t in (512, 480, 352, 256, 128, 64, 32, 16, 8):
        if rows % t == 0:
            return t
    return rows


def _adamw(w, g, m, v, token, name):
    R, C = w.shape
    tr = _row_tile(R)

    def body(w_ref, g_ref, m_ref, v_ref, _, d_ref, mo_ref, vo_ref, go_ref):
        gv = g_ref[...]
        go_ref[...] = gv
        mn = ADAM_B1 * m_ref[...] + (1.0 - ADAM_B1) * gv
        vn = ADAM_B2 * v_ref[...] + (1.0 - ADAM_B2) * (gv * gv)
        m_hat = mn / (1.0 - ADAM_B1 ** ADAM_STEP)
        v_hat = vn / (1.0 - ADAM_B2 ** ADAM_STEP)
        d_ref[...] = -ADAM_LR * (m_hat / (jnp.sqrt(v_hat) + ADAM_EPS) + ADAM_WD * w_ref[...])
        mo_ref[...] = mn
        vo_ref[...] = vn

    blk = pl.BlockSpec((tr, C), lambda i: (i, 0))
    sh = jax.ShapeDtypeStruct((R, C), F32)
    return pl.pallas_call(
        body, name=name, grid=(R // tr,), in_specs=[blk] * 4 + [TOKEN_SPEC], out_specs=[blk] * 4, out_shape=[sh] * 4,
        compiler_params=_params(dimension_semantics=("arbitrary",)),
    )(w, g, m, v, token)


def _cast_place(w, place, name):
    R, C = w.shape
    tr = _row_tile(R)
    per = R // tr

    def body(p_ref, w_ref, o_ref):
        o_ref[...] = w_ref[...].astype(BF16)

    grid_spec = pltpu.PrefetchScalarGridSpec(
        num_scalar_prefetch=1, grid=(per,),
        in_specs=[pl.BlockSpec((tr, C), lambda i, p: (i, 0))],
        out_specs=pl.BlockSpec((tr, C), lambda i, p: (p[0] * per + i, 0)))
    return pl.pallas_call(
        body, name=name, grid_spec=grid_spec, out_shape=jax.ShapeDtypeStruct((N_SHARDS * R, C), BF16),
        compiler_params=_params(dimension_semantics=("arbitrary",)),
    )(place, w)


def _sum_halves(g4, recv, place, name):
    NS, R, C = g4.shape
    hr = R // 2
    tr = _row_tile(hr)
    per = hr // tr

    def body(p_ref, g_ref, r_ref, o_ref):
        o_ref[...] = (g_ref[...] + r_ref[...]).astype(BF16)

    grid_spec = pltpu.PrefetchScalarGridSpec(
        num_scalar_prefetch=1, grid=(NS, per),
        in_specs=[pl.BlockSpec((1, tr, C), lambda s, i, p: (s, p[1] * per + i, 0)),
                  pl.BlockSpec((1, tr, C), lambda s, i, p: (s, i, 0))],
        out_specs=pl.BlockSpec((1, tr, C), lambda s, i, p: (s, i, 0)))
    return pl.pallas_call(
        body, name=name, grid_spec=grid_spec, out_shape=jax.ShapeDtypeStruct((NS, hr, C), BF16),
        compiler_params=_params(dimension_semantics=("arbitrary", "arbitrary")),
    )(place, g4, recv)


def _sum_quarters(h4, recv3, place, name):
    NS, hr, C = h4.shape
    tr = _row_tile(hr)
    per = hr // tr

    def body(p_ref, h_ref, r_ref, o_ref):
        acc = h_ref[0].astype(F32)
        for k in range(N_SHARDS - 1):
            acc = acc + r_ref[k].astype(F32)
        o_ref[...] = acc

    grid_spec = pltpu.PrefetchScalarGridSpec(
        num_scalar_prefetch=1, grid=(per,),
        in_specs=[pl.BlockSpec((1, tr, C), lambda i, p: (p[0], i, 0)),
                  pl.BlockSpec((N_SHARDS - 1, tr, C), lambda i, p: (0, i, 0))],
        out_specs=pl.BlockSpec((tr, C), lambda i, p: (p[1] * per + i, 0)))
    return pl.pallas_call(
        body, name=name, grid_spec=grid_spec, out_shape=jax.ShapeDtypeStruct((2 * hr, C), F32),
        compiler_params=_params(dimension_semantics=("arbitrary",)),
    )(place, h4, recv3)


def _place():
    x, y, c = lax.axis_index("x"), lax.axis_index("y"), lax.axis_index("c")
    chips = [(1 - x, y), (x, 1 - y), (1 - x, 1 - y)]
    return x, y, c, chips


HBM_SPEC = pl.BlockSpec(memory_space=pltpu.HBM)
SEM_SPEC = pl.BlockSpec(memory_space=pltpu.SEMAPHORE)
DATAFLOW = pltpu.SideEffectType.DATAFLOW_SIDE_EFFECTING


def _hbm(a):
    return pltpu.with_memory_space_constraint(a, pltpu.HBM)


class InFlight(NamedTuple):
    send_sem: jax.Array
    recv_sem: jax.Array
    bufs: list
    plan: Callable
    token: jax.Array


def _wait_all(plan, refs, send_ref, recv_ref):
    for k, (src, dst, dev) in enumerate(plan(refs)):
        cp = pltpu.make_async_remote_copy(src_ref=src, dst_ref=dst, send_sem=send_ref.at[k], recv_sem=recv_ref.at[k],
                                          device_id=dev, device_id_type=MESH)
        cp.wait_send()
        cp.wait_recv()


def _start_all(plan, refs, send_ref, recv_ref):
    for k, (src, dst, dev) in enumerate(plan(refs)):
        pltpu.make_async_remote_copy(src_ref=src, dst_ref=dst, send_sem=send_ref.at[k], recv_sem=recv_ref.at[k],
                                     device_id=dev, device_id_type=MESH).start()


def _split_start(name, bufs, plan, n_copies, after, waits=None):
    nb = len(bufs)
    after = list(after) if isinstance(after, (list, tuple)) else [after]

    def body(*refs):
        ins = refs[:nb]
        if waits is not None:
            _wait_all(waits.plan, ins, refs[nb], refs[nb + 1])
        send_sem, recv_sem, token = refs[-nb - 3], refs[-nb - 2], refs[-1]
        _start_all(plan, ins, send_sem, recv_sem)
        token[...] = jnp.zeros_like(token)

    sems = pltpu.SemaphoreType.DMA((n_copies,))
    earlier = [] if waits is None else [waits.send_sem, waits.recv_sem]
    out = pl.pallas_call(
        body, name=name, in_specs=[HBM_SPEC] * nb + [SEM_SPEC] * len(earlier) + [TOKEN_SPEC] * len(after),
        out_shape=(sems, sems, *[pltpu.HBM(b.shape, b.dtype) for b in bufs], jax.ShapeDtypeStruct((8, 128), F32)),
        out_specs=(SEM_SPEC, SEM_SPEC, *[HBM_SPEC] * nb, pl.BlockSpec(memory_space=pltpu.VMEM)),
        input_output_aliases={i: i + 2 for i in range(nb)},
        compiler_params=pltpu.CompilerParams(has_side_effects=DATAFLOW),
    )(*[_hbm(b) for b in bufs], *earlier, *after)
    return InFlight(out[0], out[1], list(out[2:2 + nb]), plan, out[-1])


def _split_wait(name, flight, after):
    nb = len(flight.bufs)
    after = list(after) if isinstance(after, (list, tuple)) else [after]

    def body(*refs):
        _wait_all(flight.plan, refs[:nb], refs[nb], refs[nb + 1])

    out = pl.pallas_call(
        body, name=name, in_specs=[HBM_SPEC] * nb + [SEM_SPEC, SEM_SPEC] + [TOKEN_SPEC] * len(after),
        out_shape=[pltpu.HBM(b.shape, b.dtype) for b in flight.bufs], out_specs=[HBM_SPEC] * nb,
        input_output_aliases={i: i for i in range(nb)},
        compiler_params=pltpu.CompilerParams(has_side_effects=DATAFLOW),
    )(*flight.bufs, flight.send_sem, flight.recv_sem, *after)
    return list(out)


def _half_rows(buf, chip, core):
    hr = buf.shape[0] // (2 * N_SHARDS)
    return buf.at[pl.ds(pl.multiple_of((2 * chip + core) * hr, 16), hr)]


def _gather_ici_plan(bufs):
    x, y, c, chips = _place()
    return [(_half_rows(b, 2 * x + y, c), _half_rows(b, 2 * x + y, c), (px, py, c)) for b in bufs for px, py in chips]


def _gather_d2d_plan(bufs):
    x, y, c, chips = _place()
    return [(_half_rows(b, 2 * px + py, c), _half_rows(b, 2 * px + py, c), (x, y, 1 - c)) for b in bufs for px, py in chips]


def _swap_plan(bufs):
    x, y, c, _ = _place()
    n = len(bufs) // 2
    copies = []
    for g, land in zip(bufs[:n], bufs[n:]):
        hr = g.shape[1] // 2
        copies.append((g.at[:, pl.ds(pl.multiple_of((1 - c) * hr, 8), hr)], land, (x, y, 1 - c)))
    return copies


def _exchange_plan(bufs):
    x, y, c, chips = _place()
    n = len(bufs) // 2
    return [(h.at[2 * px + py], land.at[k], (px, py, c))
            for h, land in zip(bufs[:n], bufs[n:]) for k, (px, py) in enumerate(chips)]


def _share_plan(bufs):
    x, y, c, _ = _place()
    copies = []
    for buf in bufs:
        hr = buf.shape[0] // 2
        mine = buf.at[pl.ds(pl.multiple_of(c * hr, 8), hr)]
        copies.append((mine, mine, (x, y, 1 - c)))
    return copies


N_DEVICES = 8


def _slot_place(vec, place):
    R, C = vec.shape

    def body(p_ref, v_ref, o_ref):
        o_ref[0] = v_ref[...]

    grid_spec = pltpu.PrefetchScalarGridSpec(
        num_scalar_prefetch=1, grid=(1,), in_specs=[pl.BlockSpec((R, C), lambda i, p: (0, 0))],
        out_specs=pl.BlockSpec((1, R, C), lambda i, p: (2 * p[0] + p[1], 0, 0)))
    return pl.pallas_call(
        body, name="slot_place", grid_spec=grid_spec, out_shape=jax.ShapeDtypeStruct((N_DEVICES, R, C), F32),
        compiler_params=_params(dimension_semantics=("arbitrary",)),
    )(place, vec)


def _slots_plan(bufs):
    x, y, c, _ = _place()
    mine = bufs[0].at[4 * x + 2 * y + c]
    return [(mine, mine, (x ^ (r >> 2), y ^ ((r >> 1) & 1), c ^ (r & 1))) for r in range(1, N_DEVICES)]


def _sum_slots(slots):
    _, R, C = slots.shape

    def body(s_ref, o_ref):
        acc = s_ref[0]
        for d in range(1, N_DEVICES):
            acc = acc + s_ref[d]
        o_ref[...] = acc

    return pl.pallas_call(
        body, name="sum_slots", in_specs=[pl.BlockSpec(memory_space=pltpu.VMEM)],
        out_specs=pl.BlockSpec(memory_space=pltpu.VMEM), out_shape=jax.ShapeDtypeStruct((R, C), F32),
        compiler_params=_params(),
    )(slots)


SMALL = ("ffn1_norm", "mix_norm", "pool_w", "pool_scale", "q_norm", "k_norm", "sinks", "gate_bias", "ffn2_norm")
SMALL_COLS = 1024
FFN1 = ("ffn1_w_gate", "ffn1_w_up", "ffn1_w_down")
MIXER = ("w_in", "w_pool_out", "w_attn_out", "w_out")
FFN2 = ("ffn2_w_gate", "ffn2_w_up", "ffn2_w_down")
LARGE = FFN1 + MIXER + FFN2
TRANSPOSED = ("ffn1_w_gate", "ffn1_w_up", "w_in", "w_pool_out", "ffn2_w_gate", "ffn2_w_up")
WEIGHTS = ("ffn1_norm", "ffn1_w_gate", "ffn1_w_up", "ffn1_w_down", "mix_norm", "w_in", "pool_w", "pool_scale",
           "w_pool_out", "q_norm", "k_norm", "sinks", "w_attn_out", "gate_bias", "w_out", "ffn2_norm",
           "ffn2_w_gate", "ffn2_w_up", "ffn2_w_down")


def _pack_small(parts):
    flat = jnp.concatenate([p.reshape(-1) for p in parts])
    rows = -(-flat.shape[0] // (8 * SMALL_COLS)) * 8
    return jnp.pad(flat, (0, rows * SMALL_COLS - flat.shape[0])).reshape(rows, SMALL_COLS)


def _unpack_small(packed, like):
    flat = packed.reshape(-1)
    out, off = [], 0
    for p in like:
        out.append(flat[off:off + p.size].reshape(p.shape))
        off += p.size
    return out, flat[off]


def _tie(small, token):
    return small + token[0, 0]


class Reduction:
    def __init__(self, group, names, grads, place):
        self.group, self.names, self.place = group, names, place
        self.bufs = [g.reshape(N_SHARDS, -1, g.shape[-1]) for g in grads]
        self.flight = None

    def _start(self, stage, bufs, plan, n_copies, after):
        self.flight = _split_start(f"{self.group}_{stage}", bufs, plan, n_copies, after)
        return self.flight.token

    def _landed(self, stage, after):
        n = len(self.names)
        bufs = _split_wait(f"{self.group}_{stage}_wait", self.flight, after)
        return bufs[:n], bufs[n:]

    def start_swap(self, after):
        lands = [lax.empty((g.shape[0], g.shape[1] // 2, g.shape[2]), g.dtype) for g in self.bufs]
        return self._start("swap", self.bufs + lands, _swap_plan, len(self.bufs), after)

    def start_exchange(self, after):
        g4, recv = self._landed("swap", after)
        halves = [_sum_halves(g, r, self.place, f"sum_halves_{n}") for n, g, r in zip(self.names, g4, recv)]
        lands = [lax.empty((N_SHARDS - 1,) + h.shape[1:], h.dtype) for h in halves]
        return self._start("exchange", halves + lands, _exchange_plan, (N_SHARDS - 1) * len(halves), self.place)

    def start_share(self, after):
        halves, recv3 = self._landed("exchange", after)
        reduced = [_sum_quarters(h, r, self.place, f"sum_quarters_{n}") for n, h, r in zip(self.names, halves, recv3)]
        return self._start("share", reduced, _share_plan, len(reduced), self.place)

    def finish(self, after):
        return dict(zip(self.names, _split_wait(f"{self.group}_share_wait", self.flight, after)))


def _row_form(name, a):
    return a.T if name in TRANSPOSED else a


def kernel(x, ffn1_norm, ffn1_w_gate, ffn1_w_up, ffn1_w_down, mix_norm, w_in, pool_w, pool_scale, w_pool_out, q_norm, k_norm, sinks, w_attn_out, gate_bias, w_out, ffn2_norm, ffn2_w_gate, ffn2_w_up, ffn2_w_down, loss_target, m_ffn1_norm, m_ffn1_w_gate, m_ffn1_w_up, m_ffn1_w_down, m_mix_norm, m_w_in, m_pool_w, m_pool_scale, m_w_pool_out, m_q_norm, m_k_norm, m_sinks, m_w_attn_out, m_gate_bias, m_w_out, m_ffn2_norm, m_ffn2_w_gate, m_ffn2_w_up, m_ffn2_w_down, v_ffn1_norm, v_ffn1_w_gate, v_ffn1_w_up, v_ffn1_w_down, v_mix_norm, v_w_in, v_pool_w, v_pool_scale, v_w_pool_out, v_q_norm, v_k_norm, v_sinks, v_w_attn_out, v_gate_bias, v_w_out, v_ffn2_norm, v_ffn2_w_gate, v_ffn2_w_up, v_ffn2_w_down):
    args = dict(locals())
    wts = {n: _row_form(n, args[n]) for n in WEIGHTS}
    mom = {n: _row_form(n, args["m_" + n]) for n in WEIGHTS}
    var = {n: _row_form(n, args["v_" + n]) for n in WEIGHTS}
    shard = 2 * lax.axis_index("x") + lax.axis_index("y")
    place = jnp.stack([shard, lax.axis_index("c")]).astype(jnp.int32)

    xs, target = x[0], loss_target[0]
    D = xs.shape[1]
    g1 = wts["ffn1_norm"].reshape(1, D)
    gm = wts["mix_norm"].reshape(1, D)
    g2 = wts["ffn2_norm"].reshape(1, D)
    qw = wts["q_norm"].reshape(1, HEAD_DIM)
    kw = wts["k_norm"].reshape(1, HEAD_DIM)
    bias = wts["gate_bias"].reshape(1, 2 * D)
    pscale = wts["pool_scale"].reshape(1, -1)
    pw = wts["pool_w"].astype(BF16)
    sink_rows = jnp.broadcast_to(jnp.repeat(wts["sinks"], ATTN_BLOCK).reshape(N_KV_HEADS, GQA_GROUP * ATTN_BLOCK, 1),
                                 (N_KV_HEADS, GQA_GROUP * ATTN_BLOCK, 128))

    first, down1 = FFN1[:2], FFN1[2:]
    placed = {n: _cast_place(wts[n], place, "cast_" + n) for n in FFN1}
    first_ici = _split_start("gather_first_ici", [placed[n] for n in first], _gather_ici_plan, 3 * len(first), place)
    placed.update({n: _cast_place(wts[n], place, "cast_" + n) for n in MIXER + FFN2})
    first_d2d = _split_start("gather_first_d2d", first_ici.bufs, _gather_d2d_plan, 3 * len(first),
                             [placed[n] for n in MIXER + FFN2], waits=first_ici)
    down1_ici = _split_start("gather_down1_ici", [placed[n] for n in down1], _gather_ici_plan, 3 * len(down1), first_d2d.token)
    w = dict(zip(first, _split_wait("gather_first_wait", first_d2d, down1_ici.token)))
    mix_ici = _split_start("gather_mix_ici", [placed[n] for n in MIXER], _gather_ici_plan, 3 * len(MIXER), w[first[0]])
    ffn2_ici = _split_start("gather_ffn2_ici", [placed[n] for n in FFN2], _gather_ici_plan, 3 * len(FFN2), mix_ici.token)

    n1t, a1, b1, s1 = _ffn_up(xs, _tie(g1, ffn2_ici.token), w["ffn1_w_gate"], w["ffn1_w_up"], "ffn1_up")
    down1_d2d = _split_start("gather_down1_d2d", down1_ici.bufs, _gather_d2d_plan, 3 * len(down1), s1, waits=down1_ici)
    w.update(zip(down1, _split_wait("gather_down1_wait", down1_d2d, down1_d2d.token)))
    h1 = _ffn_down(xs, s1, w["ffn1_w_down"], "ffn1_down")
    mix_d2d = _split_start("gather_mix_d2d", mix_ici.bufs, _gather_d2d_plan, 3 * len(MIXER), h1, waits=mix_ici)
    w.update(zip(MIXER, _split_wait("gather_mix_wait", mix_d2d, mix_d2d.token)))
    ut, xp, q, kv, gl = _mix_proj(h1, gm, w["w_in"])
    attn = _attn_fwd(q, kv, qw, kw, sink_rows)
    ffn2_d2d = _split_start("gather_ffn2_d2d", ffn2_ici.bufs, _gather_d2d_plan, 3 * len(FFN2), attn, waits=ffn2_ici)
    h2, bp, ba = _mix_out(xp, attn, gl, _tie(bias, ffn2_d2d.token), pw, pscale, w["w_pool_out"], w["w_attn_out"], w["w_out"], h1)
    w.update(zip(FFN2, _split_wait("gather_ffn2_wait", ffn2_d2d, h2)))
    n2t, a2, b2, s2 = _ffn_up(h2, g2, w["ffn2_w_gate"], w["ffn2_w_up"], "ffn2_up")
    dy, dyh, dyht, loss = _ffn_down_loss(h2, s2, w["ffn2_w_down"], target, "ffn2_down_loss")

    gw, gs = {}, {}
    da, db = _ffn_bwd_h(dyh, a2, b2, w["ffn2_w_down"], ffn2_d2d.token, "ffn2_bwd_h")
    gw["ffn2_w_gate"], gw["ffn2_w_up"] = _xty(n2t, da, "ffn2_dw_gate"), _xty(n2t, db, "ffn2_dw_up")
    gw["ffn2_w_down"] = _xty(dyht, s2, "ffn2_dw_down")
    red2 = Reduction("reduce_ffn2", FFN2, [gw[n] for n in FFN2], place)
    token = red2.start_swap(da)
    dh2, gs["ffn2_norm"] = _ffn_bwd_x(da, db, w["ffn2_w_gate"], w["ffn2_w_up"], h2, dy, _tie(g2, token), "ffn2_bwd_x")
    token = red2.start_exchange(dh2)
    dgl, dbp, dba, gw["w_out"], gs["gate_bias"] = _mix_bwd_gate(dh2, w["w_out"], bp, ba, gl, _tie(bias, token))
    dattn, dpooled, gw["w_attn_out"], gw["w_pool_out"], gs["pool_w"], gs["pool_scale"] = _mix_bwd_branch(
        dbp, dba, attn, xp, pw, pscale, w["w_pool_out"], w["w_attn_out"])
    dq, dkv, gs["q_norm"], gs["k_norm"], dsk = _attn_bwd(q, kv, dattn, qw, kw, sink_rows)
    gs["sinks"] = dsk[:, :, 0]
    red2.start_share(dq)
    pieces = (_pool_bwd(dpooled), dq, dkv, dgl)
    gw["w_in"] = _in_bwd_w(pieces, ut)
    grads = red2.finish(gw["w_in"])
    redm = Reduction("reduce_mix", MIXER, [gw[n] for n in MIXER], place)
    token = redm.start_swap(grads[FFN2[0]])
    dh1, dh1h, dh1ht, gs["mix_norm"] = _in_bwd_x(pieces, w["w_in"], h1, dh2, _tie(gm, token))
    token = redm.start_exchange(dh1)
    da, db = _ffn_bwd_h(dh1h, a1, b1, w["ffn1_w_down"], token, "ffn1_bwd_h")
    gw["ffn1_w_gate"], gw["ffn1_w_up"] = _xty(n1t, da, "ffn1_dw_gate"), _xty(n1t, db, "ffn1_dw_up")
    gw["ffn1_w_down"] = _xty(dh1ht, s1, "ffn1_dw_down")
    token = redm.start_share([gw[n] for n in FFN1])
    red1 = Reduction("reduce_ffn1", FFN1, [gw[n] for n in FFN1], place)
    token = red1.start_swap(token)
    grad_x, gs["ffn1_norm"] = _ffn_bwd_x(da, db, w["ffn1_w_gate"], w["ffn1_w_up"], xs, dh1, _tie(g1, token), "ffn1_bwd_x")
    grads.update(redm.finish(grad_x))

    small_parts = [gs[n] for n in SMALL] + [loss[0, 0].reshape(1)]
    slots = _split_start("gather_small", [_slot_place(_pack_small(small_parts), place)], _slots_plan, N_DEVICES - 1,
                         grads[MIXER[0]])
    token = red1.start_exchange(slots.token)
    delta, new_m, new_v = {}, {}, {}
    for n in FFN2 + MIXER:
        delta[n], new_m[n], new_v[n], grads[n] = _adamw(wts[n], grads[n], mom[n], var[n], token, "adamw_" + n)
    summed = _sum_slots(_split_wait("gather_small_wait", slots, delta[MIXER[-1]])[0])
    small_grads, loss_sum = _unpack_small(summed, [wts[n] for n in SMALL])
    grads.update(dict(zip(SMALL, small_grads)))
    zero = jnp.zeros((1,), F32)
    packed = [_pack_small([d[n] for n in SMALL] + [zero]) for d in (wts, mom, var)]
    ds, ms, vs, _ = _adamw(packed[0], summed, packed[1], packed[2], token, "adamw_small")
    like = [wts[n] for n in SMALL]
    for out, packed_out in ((delta, ds), (new_m, ms), (new_v, vs)):
        out.update(dict(zip(SMALL, _unpack_small(packed_out, like)[0])))
    token = red1.start_share(ds)
    grads.update(red1.finish(token))
    for n in FFN1:
        delta[n], new_m[n], new_v[n], grads[n] = _adamw(wts[n], grads[n], mom[n], var[n], token, "adamw_" + n)

    outs = []
    for d in (grads, delta, new_m, new_v):
        outs += [_row_form(n, d[n]) for n in WEIGHTS]
    return (loss_sum, grad_x[None], *outs)
```

```python
from typing import Callable, NamedTuple

import jax
import jax.numpy as jnp
from jax import lax
from jax.experimental import pallas as pl
from jax.experimental.pallas import tpu as pltpu

F32 = jnp.float32
BF16 = jnp.bfloat16
RMS_EPS = 1e-6
POOL_WINDOWS = (2, 4, 8, 16)
POOL_GROUP = 128
POOL_HALO = 16
HEAD_DIM = 64
GQA_GROUP = 8
N_KV_HEADS = 2
ATTN_BLOCK = 128
SCALE = HEAD_DIM ** -0.5
NEG = -1e30
N_SHARDS = 4
ADAM_LR, ADAM_B1, ADAM_B2, ADAM_EPS, ADAM_WD, ADAM_STEP = 0.001, 0.9, 0.999, 1e-08, 0.01, 10
VMEM_LIMIT = 56 * 1024 * 1024
MESH = pl.DeviceIdType.MESH
SEG_POOL, SEG_Q, SEG_KV, SEG_GATE = (0, 512), (512, 1024), (1536, 256), (1792, 2048)
SEGMENTS = (SEG_POOL, SEG_Q, SEG_KV, SEG_GATE)


def _params(**kw):
    return pltpu.CompilerParams(vmem_limit_bytes=VMEM_LIMIT, **kw)


def _dot(a, b):
    return jnp.dot(a, b, preferred_element_type=F32)


def _dot_nt(a, b):
    return lax.dot_general(a, b, (((1,), (1,)), ((), ())), preferred_element_type=F32)


def _dot_tn(a, b):
    return lax.dot_general(a, b, (((0,), (0,)), ((), ())), preferred_element_type=F32)


def _rinv(x):
    return lax.rsqrt(jnp.mean(x * x, axis=-1, keepdims=True) + RMS_EPS)


def _rms_bwd(dn, x, g):
    r = _rinv(x)
    xr = x * r
    z = dn * g
    dx = r * (z - xr * jnp.mean(z * xr, axis=-1, keepdims=True))
    return dx, dn * xr


def _acc(ref, val, first):
    @pl.when(first)
    def _():
        ref[...] = val

    @pl.when(jnp.logical_not(first))
    def _():
        ref[...] += val


TOKEN_SPEC = pl.BlockSpec(memory_space=pl.ANY)
F_HALF = 1408


def _resident(w):
    return pl.BlockSpec(w.shape, lambda i: (0, 0), pipeline_mode=pl.Buffered(1))


def _ffn_up(h, gnorm, wgt, wut, name, tm=512):
    S, D = h.shape
    F = wgt.shape[0]

    def body(h_ref, g_ref, wg_ref, wu_ref, n_ref, a_ref, b_ref, s_ref):
        x = h_ref[...]
        n = (x * _rinv(x) * g_ref[...]).astype(BF16)
        n_ref[...] = n
        for lo in range(0, F, F_HALF):
            cols = slice(lo, lo + F_HALF)
            a = _dot_nt(n, wg_ref[cols, :])
            b = _dot_nt(n, wu_ref[cols, :])
            a_ref[:, cols] = a.astype(BF16)
            b_ref[:, cols] = b.astype(BF16)
            s_ref[:, cols] = (a * jax.nn.sigmoid(a) * b).astype(BF16)

    act = pl.BlockSpec((tm, F), lambda i: (i, 0))
    hidden = jax.ShapeDtypeStruct((S, F), BF16)
    return pl.pallas_call(
        body, name=name, grid=(S // tm,),
        in_specs=[pl.BlockSpec((tm, D), lambda i: (i, 0)), pl.BlockSpec((1, D), lambda i: (0, 0)), _resident(wgt), _resident(wut)],
        out_specs=[pl.BlockSpec((tm, D), lambda i: (i, 0)), act, act, act],
        out_shape=[jax.ShapeDtypeStruct((S, D), BF16), hidden, hidden, hidden],
        compiler_params=_params(dimension_semantics=("arbitrary",)),
    )(h, gnorm, wgt, wut)


def _ffn_down(h, s, wd, name, tm=1024):
    S, D = h.shape
    F = wd.shape[0]

    def body(h_ref, s_ref, wd_ref, o_ref):
        o_ref[...] = h_ref[...] + 0.5 * _dot(s_ref[...], wd_ref[...])

    row = pl.BlockSpec((tm, D), lambda i: (i, 0))
    return pl.pallas_call(
        body, name=name, grid=(S // tm,), in_specs=[row, pl.BlockSpec((tm, F), lambda i: (i, 0)), _resident(wd)],
        out_specs=row, out_shape=jax.ShapeDtypeStruct((S, D), F32),
        compiler_params=_params(dimension_semantics=("arbitrary",)),
    )(h, s, wd)


def _ffn_down_loss(h, s, wd, target, name, tm=512):
    S, D = h.shape
    F = wd.shape[0]

    def body(h_ref, s_ref, wd_ref, t_ref, dy_ref, dyh_ref, loss_ref):
        e = h_ref[...] + 0.5 * _dot(s_ref[...], wd_ref[...]) - t_ref[...]
        dy = e * (1.0 / D)
        dy_ref[...] = dy
        dyh_ref[...] = (0.5 * dy).astype(BF16)
        tot = jnp.sum(jnp.sum(e * e, axis=1, keepdims=True), axis=0, keepdims=True) * (0.5 / D)
        _acc(loss_ref, jnp.broadcast_to(tot, loss_ref.shape), pl.program_id(0) == 0)

    row = pl.BlockSpec((tm, D), lambda i: (i, 0))
    return pl.pallas_call(
        body, name=name, grid=(S // tm,), in_specs=[row, pl.BlockSpec((tm, F), lambda i: (i, 0)), _resident(wd), row],
        out_specs=[row, row, pl.BlockSpec((8, 128), lambda i: (0, 0))],
        out_shape=[jax.ShapeDtypeStruct((S, D), F32), jax.ShapeDtypeStruct((S, D), BF16), jax.ShapeDtypeStruct((8, 128), F32)],
        compiler_params=_params(dimension_semantics=("arbitrary",)),
    )(h, s, wd, target)


def _ffn_bwd_h(dyh, a, b, wd, token, name, tm=512):
    S, D = dyh.shape
    F = wd.shape[0]

    def body(dyh_ref, a_ref, b_ref, wd_ref, _, da_ref, db_ref):
        dyh_t = dyh_ref[...]
        for lo in range(0, F, F_HALF):
            cols = slice(lo, lo + F_HALF)
            av = a_ref[:, cols].astype(F32)
            bv = b_ref[:, cols].astype(F32)
            ds = _dot_nt(dyh_t, wd_ref[cols, :])
            sig = jax.nn.sigmoid(av)
            silu = av * sig
            da_ref[:, cols] = (ds * bv * (sig * (1.0 + av * (1.0 - sig)))).astype(BF16)
            db_ref[:, cols] = (ds * silu).astype(BF16)

    act = pl.BlockSpec((tm, F), lambda i: (i, 0))
    hidden = jax.ShapeDtypeStruct((S, F), BF16)
    return pl.pallas_call(
        body, name=name, grid=(S // tm,),
        in_specs=[pl.BlockSpec((tm, D), lambda i: (i, 0)), act, act, _resident(wd), TOKEN_SPEC],
        out_specs=[act, act], out_shape=[hidden, hidden],
        compiler_params=_params(dimension_semantics=("arbitrary",)),
    )(dyh, a, b, wd, token)


def _xty(x, y, name, tk=2048, tf=1408):
    S, F = x.shape
    D = y.shape[1]

    def body(x_ref, y_ref, o_ref):
        _acc(o_ref, _dot_tn(x_ref[...], y_ref[...]), pl.program_id(1) == 0)

    return pl.pallas_call(
        body, name=name, grid=(F // tf, S // tk),
        in_specs=[pl.BlockSpec((tk, tf), lambda j, k: (k, j)), pl.BlockSpec((tk, D), lambda j, k: (k, 0))],
        out_specs=pl.BlockSpec((tf, D), lambda j, k: (j, 0)), out_shape=jax.ShapeDtypeStruct((F, D), F32),
        compiler_params=_params(dimension_semantics=("arbitrary", "arbitrary")),
    )(x, y)


def _ffn_bwd_x(da, db, wgt, wut, x, dh, gnorm, name, tm=512):
    S, D = x.shape
    F = wgt.shape[0]

    def body(da_ref, db_ref, wg_ref, wu_ref, x_ref, dh_ref, g_ref, dx_ref, dg_ref):
        dn = _dot(da_ref[...], wg_ref[...]) + _dot(db_ref[...], wu_ref[...])
        dx, dg_rows = _rms_bwd(dn, x_ref[...], g_ref[...])
        dx_ref[...] = dh_ref[...] + dx
        _acc(dg_ref, jnp.sum(dg_rows, axis=0, keepdims=True), pl.program_id(0) == 0)

    row = pl.BlockSpec((tm, D), lambda i: (i, 0))
    act = pl.BlockSpec((tm, F), lambda i: (i, 0))
    vec = pl.BlockSpec((1, D), lambda i: (0, 0))
    return pl.pallas_call(
        body, name=name, grid=(S // tm,), in_specs=[act, act, _resident(wgt), _resident(wut), row, row, vec],
        out_specs=[row, vec],
        out_shape=[jax.ShapeDtypeStruct((S, D), F32), jax.ShapeDtypeStruct((1, D), F32)],
        compiler_params=_params(dimension_semantics=("arbitrary",)),
    )(da, db, wgt, wut, x, dh, gnorm)


def _mix_proj(h, gnorm, wint, tm=512):
    S, D = h.shape
    nt = S // tm

    def body(h_ref, g_ref, w_ref, u_ref, xp_ref, q_ref, kv_ref, gl_ref):
        x = h_ref[...]
        u = (x * _rinv(x) * g_ref[...]).astype(BF16)
        u_ref[...] = u
        for (off, width), ref in zip(SEGMENTS, (xp_ref, q_ref, kv_ref, gl_ref)):
            ref[...] = _dot_nt(u, w_ref[off:off + width, :]).astype(ref.dtype)

    def row(width):
        return pl.BlockSpec((tm, width), lambda i: (i, 0))

    return pl.pallas_call(
        body, name="mix_proj", grid=(nt,),
        in_specs=[row(D), pl.BlockSpec((1, D), lambda i: (0, 0)), pl.BlockSpec(wint.shape, lambda i: (0, 0))],
        out_specs=[row(D), row(SEG_POOL[1]), row(SEG_Q[1]), row(SEG_KV[1]), row(SEG_GATE[1])],
        out_shape=[jax.ShapeDtypeStruct((S, D), BF16), jax.ShapeDtypeStruct((S, SEG_POOL[1]), F32),
                   jax.ShapeDtypeStruct((S, SEG_Q[1]), BF16), jax.ShapeDtypeStruct((S, SEG_KV[1]), BF16),
                   jax.ShapeDtypeStruct((S, SEG_GATE[1]), BF16)],
        compiler_params=_params(dimension_semantics=("arbitrary",)),
    )(h, gnorm, wint)


def _stack_heads(x, g):
    return jnp.concatenate([x[:, (GQA_GROUP * g + hh) * HEAD_DIM:(GQA_GROUP * g + hh + 1) * HEAD_DIM]
                            for hh in range(GQA_GROUP)], axis=0)


def _unstack_heads(ref, val, g):
    for hh in range(GQA_GROUP):
        lo = (GQA_GROUP * g + hh) * HEAD_DIM
        ref[:, lo:lo + HEAD_DIM] = val[hh * ATTN_BLOCK:(hh + 1) * ATTN_BLOCK, :]


def _rowsum(xb, width):
    return _dot(xb, jnp.ones((xb.shape[1], width), BF16))


def _rinv_lanes(x):
    return lax.rsqrt(_rowsum((x * x).astype(BF16), x.shape[1]) * (1.0 / x.shape[1]) + RMS_EPS)


def _attn_exp(qn, kc, kp, sink, n):
    rows = GQA_GROUP * ATTN_BLOCK
    qi = lax.broadcasted_iota(jnp.int32, (rows, ATTN_BLOCK), 0) % ATTN_BLOCK
    kj = lax.broadcasted_iota(jnp.int32, (rows, ATTN_BLOCK), 1)
    s_c = jnp.where(kj <= qi, _dot_nt(qn, kc), NEG)
    s_p = jnp.where(jnp.logical_and(kj > qi, n > 0), _dot_nt(qn, kp), NEG)
    m = jnp.maximum(jnp.max(s_c, axis=-1, keepdims=True), jnp.max(s_p, axis=-1, keepdims=True))
    m = jnp.maximum(jnp.broadcast_to(m, sink.shape), sink)
    e_c = jnp.exp(s_c - m)
    e_p = jnp.exp(s_p - m)
    e_s = jnp.exp(sink - m)
    e_cb, e_pb = e_c.astype(BF16), e_p.astype(BF16)
    inv = 1.0 / (_rowsum(jnp.concatenate([e_cb, e_pb], axis=1), ATTN_BLOCK) + e_s)
    return e_cb, e_pb, e_c, e_p, e_s, inv


def _attn_blocks(n):
    cur = pl.multiple_of(n * ATTN_BLOCK, ATTN_BLOCK)
    prev = pl.multiple_of(jnp.maximum(n - 1, 0) * ATTN_BLOCK, ATTN_BLOCK)
    return cur, prev


def _kv_split(kv, g):
    k = kv[:, g * HEAD_DIM:(g + 1) * HEAD_DIM]
    v = kv[:, (N_KV_HEADS + g) * HEAD_DIM:(N_KV_HEADS + g + 1) * HEAD_DIM]
    return k, v


def _attn_fwd(q, kv, qw, kw, sink_rows):
    S, W = q.shape
    nb = S // ATTN_BLOCK

    def body(q_ref, kv_ref, qw_ref, kw_ref, sk_ref, o_ref, o_scr):
        n = pl.program_id(0)
        qf = q_ref[...].astype(F32)
        cur, prev = _attn_blocks(n)
        kvc = kv_ref[pl.ds(cur, ATTN_BLOCK), :].astype(F32)
        kvp = kv_ref[pl.ds(prev, ATTN_BLOCK), :].astype(F32)
        for g in range(N_KV_HEADS):
            qs = _stack_heads(qf, g)
            qn = (qs * _rinv_lanes(qs) * qw_ref[...] * SCALE).astype(BF16)
            kc, vc = _kv_split(kvc, g)
            kp, vp = _kv_split(kvp, g)
            kc = (kc * _rinv_lanes(kc) * kw_ref[...]).astype(BF16)
            kp = (kp * _rinv_lanes(kp) * kw_ref[...]).astype(BF16)
            e_cb, e_pb, _, _, _, inv = _attn_exp(qn, kc, kp, sk_ref[g], n)
            o = (_dot(e_cb, vc.astype(BF16)) + _dot(e_pb, vp.astype(BF16))) * inv[:, :HEAD_DIM]
            _unstack_heads(o_scr, o, g)
        o_ref[...] = o_scr[...].astype(BF16)

    blk = pl.BlockSpec((ATTN_BLOCK, W), lambda n: (n, 0))
    return pl.pallas_call(
        body, name="attn_fwd", grid=(nb,),
        in_specs=[blk, pl.BlockSpec(kv.shape, lambda n: (0, 0)), pl.BlockSpec((1, HEAD_DIM), lambda n: (0, 0)),
                  pl.BlockSpec((1, HEAD_DIM), lambda n: (0, 0)), pl.BlockSpec(sink_rows.shape, lambda n: (0, 0, 0))],
        out_specs=blk, out_shape=jax.ShapeDtypeStruct((S, W), BF16),
        scratch_shapes=[pltpu.VMEM((ATTN_BLOCK, W), F32)],
        compiler_params=_params(dimension_semantics=("arbitrary",)),
    )(q, kv, qw, kw, sink_rows)


def _attn_bwd(q, kv, do, qw, kw, sink_rows):
    S, W = q.shape
    KW = kv.shape[1]
    nb = S // ATTN_BLOCK
    chunk = 512

    def body(q_ref, kv_ref, do_ref, qw_ref, kw_ref, sk_ref, dq_ref, dkv_ref, dqw_ref, dkw_ref, dsk_ref, dq_scr):
        n = pl.program_id(0)

        @pl.when(n == 0)
        def _():
            dkv_ref[...] = jnp.zeros_like(dkv_ref)
            dqw_ref[...] = jnp.zeros_like(dqw_ref)
            dsk_ref[...] = jnp.zeros_like(dsk_ref)

        qf = q_ref[...].astype(F32)
        dof = do_ref[...].astype(F32)
        cur, prev = _attn_blocks(n)
        kvc = kv_ref[pl.ds(cur, ATTN_BLOCK), :].astype(F32)
        kvp = kv_ref[pl.ds(prev, ATTN_BLOCK), :].astype(F32)
        qw_v = qw_ref[...]
        for g in range(N_KV_HEADS):
            qs = _stack_heads(qf, g)
            rq = _rinv_lanes(qs)
            qhat = qs * rq
            qn = (qhat * qw_v * SCALE).astype(BF16)
            kc, vc = _kv_split(kvc, g)
            kp, vp = _kv_split(kvp, g)
            kc = (kc * _rinv_lanes(kc) * kw_ref[...]).astype(BF16)
            kp = (kp * _rinv_lanes(kp) * kw_ref[...]).astype(BF16)
            vc = vc.astype(BF16)
            vp = vp.astype(BF16)
            dos = _stack_heads(dof, g).astype(BF16)
            _, _, e_c, e_p, e_s, inv = _attn_exp(qn, kc, kp, sk_ref[g], n)
            p_c, p_p, p_s = e_c * inv, e_p * inv, e_s * inv
            dp_c = _dot_nt(dos, vc)
            dp_p = _dot_nt(dos, vp)
            drow = _rowsum(jnp.concatenate([(p_c * dp_c).astype(BF16), (p_p * dp_p).astype(BF16)], axis=1), ATTN_BLOCK)
            ds_c = (p_c * (dp_c - drow)).astype(BF16)
            ds_p = (p_p * (dp_p - drow)).astype(BF16)
            dsink = -(p_s * drow)
            for hh in range(GQA_GROUP):
                tot = jnp.sum(dsink[hh * ATTN_BLOCK:(hh + 1) * ATTN_BLOCK, :], axis=0, keepdims=True)
                dsk_ref[g, hh:hh + 1, :] += tot
            dqn = (_dot(ds_c, kc) + _dot(ds_p, kp)) * SCALE
            klo, vlo = g * HEAD_DIM, (N_KV_HEADS + g) * HEAD_DIM
            dkv_ref[pl.ds(cur, ATTN_BLOCK), klo:klo + HEAD_DIM] += _dot_tn(ds_c, qn)
            dkv_ref[pl.ds(prev, ATTN_BLOCK), klo:klo + HEAD_DIM] += _dot_tn(ds_p, qn)
            dkv_ref[pl.ds(cur, ATTN_BLOCK), vlo:vlo + HEAD_DIM] += _dot_tn(p_c.astype(BF16), dos)
            dkv_ref[pl.ds(prev, ATTN_BLOCK), vlo:vlo + HEAD_DIM] += _dot_tn(p_p.astype(BF16), dos)
            dqw_ref[...] += jnp.sum(dqn * qhat, axis=0, keepdims=True)
            z = dqn * qw_v
            dqs = rq * (z - qhat * (_rowsum((z * qhat).astype(BF16), HEAD_DIM) * (1.0 / HEAD_DIM)))
            _unstack_heads(dq_scr, dqs, g)
        dq_ref[...] = dq_scr[...].astype(BF16)

        @pl.when(n == nb - 1)
        def _():
            def one(c, dkw):
                rows = pl.ds(pl.multiple_of(c * chunk, chunk), chunk)
                for g in range(N_KV_HEADS):
                    lo = g * HEAD_DIM
                    k = kv_ref[rows, lo:lo + HEAD_DIM].astype(F32)
                    dx, dg_rows = _rms_bwd(dkv_ref[rows, lo:lo + HEAD_DIM], k, kw_ref[...])
                    dkv_ref[rows, lo:lo + HEAD_DIM] = dx
                    dkw = dkw + jnp.sum(dg_rows, axis=0, keepdims=True)
                return dkw

            dkw_ref[...] = lax.fori_loop(0, S // chunk, one, jnp.zeros((1, HEAD_DIM), F32))

    blk = pl.BlockSpec((ATTN_BLOCK, W), lambda n: (n, 0))
    whole_kv = pl.BlockSpec((S, KW), lambda n: (0, 0))
    vec = pl.BlockSpec((1, HEAD_DIM), lambda n: (0, 0))
    sk = pl.BlockSpec(sink_rows.shape, lambda n: (0, 0, 0))
    dsk = pl.BlockSpec((N_KV_HEADS, GQA_GROUP, 128), lambda n: (0, 0, 0))
    return pl.pallas_call(
        body, name="attn_bwd", grid=(nb,), in_specs=[blk, whole_kv, blk, vec, vec, sk],
        out_specs=[blk, whole_kv, vec, vec, dsk],
        out_shape=[jax.ShapeDtypeStruct((S, W), BF16), jax.ShapeDtypeStruct((S, KW), F32),
                   jax.ShapeDtypeStruct((1, HEAD_DIM), F32), jax.ShapeDtypeStruct((1, HEAD_DIM), F32),
                   jax.ShapeDtypeStruct((N_KV_HEADS, GQA_GROUP, 128), F32)],
        scratch_shapes=[pltpu.VMEM((ATTN_BLOCK, W), F32)],
        compiler_params=_params(dimension_semantics=("arbitrary",)),
    )(q, kv, do, qw, kw, sink_rows)


def _pooled(xc, xprev, i):
    tm = xc.shape[0]
    xh = jnp.concatenate([jnp.where(i > 0, xprev, 0.0), xc], axis=0)
    t = lax.broadcasted_iota(jnp.int32, (tm, 1), 0) + i * tm
    out = []
    for gi, w in enumerate(POOL_WINDOWS):
        acc = xh[:, gi * POOL_GROUP:(gi + 1) * POOL_GROUP]
        sh = 1
        while sh < w:
            acc = acc + pltpu.roll(acc, sh, 0)
            sh *= 2
        cnt = jnp.minimum(t + 1, w).astype(F32)
        out.append(acc[POOL_HALO:, :] / cnt - xc[:, gi * POOL_GROUP:(gi + 1) * POOL_GROUP])
    return jnp.concatenate(out, axis=1)


def _pool_mix(pooled_b, pw_ref):
    return jnp.concatenate([_dot(pooled_b[:, gi * POOL_GROUP:(gi + 1) * POOL_GROUP], pw_ref[gi])
                            for gi in range(len(POOL_WINDOWS))], axis=1)


def _halo_specs(tm, width, S, after):
    per = tm // POOL_HALO
    last = S // POOL_HALO - 1
    if after:
        return pl.BlockSpec((POOL_HALO, width), lambda i: (jnp.minimum((i + 1) * per, last), 0))
    return pl.BlockSpec((POOL_HALO, width), lambda i: (jnp.maximum(i * per - 1, 0), 0))


def _mix_out(xp, attn, gl, bias, pw, pscale, wpot, wao, wo, h, tm=512):
    S, D = h.shape
    nt = S // tm
    PW = xp.shape[1]

    def body(xc_ref, xprev_ref, at_ref, gl_ref, bias_ref, pw_ref, ps_ref, wpo_ref, wao_ref, wo_ref, h_ref,
             ho_ref, bp_ref, ba_ref):
        i = pl.program_id(0)
        pooled = _pooled(xc_ref[...], xprev_ref[...], i).astype(BF16)
        ms = (_pool_mix(pooled, pw_ref) * ps_ref[...]).astype(BF16)
        bp = _dot_nt(ms, wpo_ref[...])
        ba = _dot(at_ref[...], wao_ref[...])
        bp_ref[...] = bp.astype(BF16)
        ba_ref[...] = ba.astype(BF16)
        gates = jax.nn.sigmoid(gl_ref[...].astype(F32) + bias_ref[...])
        merged = (gates[:, :D] * bp + gates[:, D:] * ba).astype(BF16)
        ho_ref[...] = h_ref[...] + _dot(merged, wo_ref[...])

    def row(width):
        return pl.BlockSpec((tm, width), lambda i: (i, 0))

    def whole(x):
        nd = x.ndim
        return pl.BlockSpec(x.shape, lambda i: (0,) * nd)

    return pl.pallas_call(
        body, name="mix_out", grid=(nt,),
        in_specs=[row(PW), _halo_specs(tm, PW, S, False), row(D), row(2 * D), whole(bias), whole(pw), whole(pscale),
                  whole(wpot), whole(wao), whole(wo), row(D)],
        out_specs=[row(D), row(D), row(D)],
        out_shape=[jax.ShapeDtypeStruct((S, D), F32), jax.ShapeDtypeStruct((S, D), BF16),
                   jax.ShapeDtypeStruct((S, D), BF16)],
        compiler_params=_params(dimension_semantics=("arbitrary",)),
    )(xp, xp, attn, gl, bias, pw, pscale, wpot, wao, wo, h)


def _mix_bwd_gate(dh, wo, bp, ba, gl, bias, tm=512):
    S, D = dh.shape
    nt = S // tm

    def body(dh_ref, wo_ref, bp_ref, ba_ref, gl_ref, bias_ref, dgl_ref, dbp_ref, dba_ref, dwo_ref, dbias_ref):
        i = pl.program_id(0)
        dhb = dh_ref[...].astype(BF16)
        dm = _dot_nt(dhb, wo_ref[...])
        gates = jax.nn.sigmoid(gl_ref[...].astype(F32) + bias_ref[...])
        gp, ga = gates[:, :D], gates[:, D:]
        bp_v = bp_ref[...].astype(F32)
        ba_v = ba_ref[...].astype(F32)
        merged = (gp * bp_v + ga * ba_v).astype(BF16)
        _acc(dwo_ref, _dot_tn(merged, dhb), i == 0)
        dbp_ref[...] = (dm * gp).astype(BF16)
        dba_ref[...] = (dm * ga).astype(BF16)
        dgl = jnp.concatenate([dm * bp_v * gp * (1.0 - gp), dm * ba_v * ga * (1.0 - ga)], axis=1)
        dgl_ref[...] = dgl.astype(BF16)
        _acc(dbias_ref, jnp.sum(dgl, axis=0, keepdims=True), i == 0)

    def row(width):
        return pl.BlockSpec((tm, width), lambda i: (i, 0))

    def whole(shape):
        return pl.BlockSpec(shape, lambda i: (0, 0))

    return pl.pallas_call(
        body, name="mix_bwd_gate", grid=(nt,),
        in_specs=[row(D), whole(wo.shape), row(D), row(D), row(2 * D), whole(bias.shape)],
        out_specs=[row(2 * D), row(D), row(D), whole((D, D)), whole((1, 2 * D))],
        out_shape=[jax.ShapeDtypeStruct((S, 2 * D), BF16), jax.ShapeDtypeStruct((S, D), BF16),
                   jax.ShapeDtypeStruct((S, D), BF16), jax.ShapeDtypeStruct((D, D), F32),
                   jax.ShapeDtypeStruct((1, 2 * D), F32)],
        compiler_params=_params(dimension_semantics=("arbitrary",)),
    )(dh, wo, bp, ba, gl, bias)


def _mix_bwd_branch(dbp, dba, attn, xp, pw, pscale, wpot, wao, tm=512):
    S, D = dbp.shape
    nt = S // tm
    PW = xp.shape[1]
    NG = len(POOL_WINDOWS)

    def body(dbp_ref, dba_ref, at_ref, xc_ref, xprev_ref, pw_ref, ps_ref, wpo_ref, wao_ref,
             dat_ref, dpl_ref, dwao_ref, dwpo_ref, dpw_ref, dps_ref):
        i = pl.program_id(0)
        dba_v = dba_ref[...]
        dbp_v = dbp_ref[...]
        _acc(dwao_ref, _dot_tn(at_ref[...], dba_v), i == 0)
        dat_ref[...] = _dot_nt(dba_v, wao_ref[...]).astype(BF16)
        pooled = _pooled(xc_ref[...], xprev_ref[...], i).astype(BF16)
        mixed = _pool_mix(pooled, pw_ref)
        ps = ps_ref[...]
        _acc(dwpo_ref, _dot_tn(dbp_v, (mixed * ps).astype(BF16)), i == 0)
        dms = _dot(dbp_v, wpo_ref[...])
        _acc(dps_ref, jnp.sum(dms * mixed, axis=0, keepdims=True), i == 0)
        dmixed = (dms * ps).astype(BF16)
        dpooled = []
        for gi in range(NG):
            cols = slice(gi * POOL_GROUP, (gi + 1) * POOL_GROUP)
            _acc(dpw_ref.at[gi], _dot_tn(pooled[:, cols], dmixed[:, cols]), i == 0)
            dpooled.append(_dot_nt(dmixed[:, cols], pw_ref[gi]))
        dpl_ref[...] = jnp.concatenate(dpooled, axis=1)

    def row(width):
        return pl.BlockSpec((tm, width), lambda i: (i, 0))

    def whole(shape):
        nd = len(shape)
        return pl.BlockSpec(shape, lambda i: (0,) * nd)

    return pl.pallas_call(
        body, name="mix_bwd_branch", grid=(nt,),
        in_specs=[row(D), row(D), row(D), row(PW), _halo_specs(tm, PW, S, False), whole(pw.shape), whole(pscale.shape),
                  whole(wpot.shape), whole(wao.shape)],
        out_specs=[row(D), row(PW), whole((D, D)), whole((D, PW)), whole(pw.shape), whole((1, PW))],
        out_shape=[jax.ShapeDtypeStruct((S, D), BF16), jax.ShapeDtypeStruct((S, PW), F32),
                   jax.ShapeDtypeStruct((D, D), F32), jax.ShapeDtypeStruct((D, PW), F32),
                   jax.ShapeDtypeStruct(pw.shape, F32), jax.ShapeDtypeStruct((1, PW), F32)],
        compiler_params=_params(dimension_semantics=("arbitrary",)),
    )(dbp, dba, attn, xp, xp, pw, pscale, wpot, wao)


def _pool_bwd(dpooled, tm=512):
    S, PW = dpooled.shape
    nt = S // tm

    def body(dc_ref, dnext_ref, dxp_ref):
        i = pl.program_id(0)
        dc = dc_ref[...]
        dh = jnp.concatenate([dc, jnp.where(i < nt - 1, dnext_ref[...], 0.0)], axis=0)
        rows = tm + POOL_HALO
        t = lax.broadcasted_iota(jnp.int32, (rows, 1), 0) + i * tm
        out = []
        for gi, w in enumerate(POOL_WINDOWS):
            cols = slice(gi * POOL_GROUP, (gi + 1) * POOL_GROUP)
            acc = dh[:, cols] / jnp.minimum(t + 1, w).astype(F32)
            sh = 1
            while sh < w:
                acc = acc + pltpu.roll(acc, rows - sh, 0)
                sh *= 2
            out.append(acc[:tm, :] - dc[:, cols])
        dxp_ref[...] = jnp.concatenate(out, axis=1).astype(BF16)

    return pl.pallas_call(
        body, name="pool_bwd", grid=(nt,),
        in_specs=[pl.BlockSpec((tm, PW), lambda i: (i, 0)), _halo_specs(tm, PW, S, True)],
        out_specs=pl.BlockSpec((tm, PW), lambda i: (i, 0)), out_shape=jax.ShapeDtypeStruct((S, PW), BF16),
        compiler_params=_params(dimension_semantics=("arbitrary",)),
    )(dpooled, dpooled)


def _in_bwd_w(pieces, u, tm=1024):
    S, D = u.shape
    nt = S // tm
    NW = sum(width for _, width in SEGMENTS)

    def body(*refs):
        piece_refs, u_ref, dw_hbm, acc, sem = refs[:len(SEGMENTS)], refs[len(SEGMENTS)], refs[len(SEGMENTS) + 1], refs[-2], refs[-1]
        i = pl.program_id(0)
        u_t = u_ref[...]
        for (off, width), ref in zip(SEGMENTS, piece_refs):
            for lo in range(0, width, D):
                hi = min(lo + D, width)
                _acc(acc.at[off + lo:off + hi, :], _dot_tn(ref[:, lo:hi].astype(BF16), u_t), i == 0)

        @pl.when(i == nt - 1)
        def _():
            cp = pltpu.make_async_copy(acc, dw_hbm, sem)
            cp.start()
            cp.wait()

    return pl.pallas_call(
        body, name="in_bwd_w", grid=(nt,),
        in_specs=[pl.BlockSpec((tm, width), lambda i: (i, 0)) for _, width in SEGMENTS] + [pl.BlockSpec((tm, D), lambda i: (i, 0))],
        out_specs=pl.BlockSpec(memory_space=pl.ANY), out_shape=jax.ShapeDtypeStruct((NW, D), F32),
        scratch_shapes=[pltpu.VMEM((NW, D), F32), pltpu.SemaphoreType.DMA],
        compiler_params=_params(dimension_semantics=("arbitrary",)),
    )(*pieces, u)


def _in_bwd_x(pieces, wint, h, dh, gnorm, tm=512):
    S, D = h.shape
    nt = S // tm

    def body(*refs):
        piece_refs = refs[:len(SEGMENTS)]
        w_ref, h_ref, dh_ref, g_ref, dx_ref, dxh_ref, dg_ref = refs[len(SEGMENTS):]
        i = pl.program_id(0)
        du = jnp.zeros((tm, D), F32)
        for (off, width), ref in zip(SEGMENTS, piece_refs):
            du = du + _dot(ref[...].astype(BF16), w_ref[off:off + width, :])
        dx, dg_rows = _rms_bwd(du, h_ref[...], g_ref[...])
        out = dh_ref[...] + dx
        dx_ref[...] = out
        dxh_ref[...] = (0.5 * out).astype(BF16)
        _acc(dg_ref, jnp.sum(dg_rows, axis=0, keepdims=True), i == 0)

    def row(width):
        return pl.BlockSpec((tm, width), lambda i: (i, 0))

    vec = pl.BlockSpec((1, D), lambda i: (0, 0))
    return pl.pallas_call(
        body, name="in_bwd_x", grid=(nt,),
        in_specs=[row(width) for _, width in SEGMENTS] + [pl.BlockSpec(wint.shape, lambda i: (0, 0)), row(D), row(D), vec],
        out_specs=[row(D), row(D), vec],
        out_shape=[jax.ShapeDtypeStruct((S, D), F32), jax.ShapeDtypeStruct((S, D), BF16), jax.ShapeDtypeStruct((1, D), F32)],
        compiler_params=_params(dimension_semantics=("arbitrary",)),
    )(*pieces, wint, h, dh, gnorm)


def _row_tile(rows):
    for t in (512, 480, 352, 256, 128, 64, 32, 16, 8):
        if rows % t == 0:
            return t
    return rows


def _adamw(w, g, m, v, token, name):
    R, C = w.shape
    tr = _row_tile(R)

    def body(w_ref, g_ref, m_ref, v_ref, _, d_ref, mo_ref, vo_ref, go_ref):
        gv = g_ref[...]
        go_ref[...] = gv
        mn = ADAM_B1 * m_ref[...] + (1.0 - ADAM_B1) * gv
        vn = ADAM_B2 * v_ref[...] + (1.0 - ADAM_B2) * (gv * gv)
        m_hat = mn / (1.0 - ADAM_B1 ** ADAM_STEP)
        v_hat = vn / (1.0 - ADAM_B2 ** ADAM_STEP)
        d_ref[...] = -ADAM_LR * (m_hat / (jnp.sqrt(v_hat) + ADAM_EPS) + ADAM_WD * w_ref[...])
        mo_ref[...] = mn
        vo_ref[...] = vn

    blk = pl.BlockSpec((tr, C), lambda i: (i, 0))
    sh = jax.ShapeDtypeStruct((R, C), F32)
    return pl.pallas_call(
        body, name=name, grid=(R // tr,), in_specs=[blk] * 4 + [TOKEN_SPEC], out_specs=[blk] * 4, out_shape=[sh] * 4,
        compiler_params=_params(dimension_semantics=("arbitrary",)),
    )(w, g, m, v, token)


def _cast_place(w, place, name):
    R, C = w.shape
    tr = _row_tile(R)
    per = R // tr

    def body(p_ref, w_ref, o_ref):
        o_ref[...] = w_ref[...].astype(BF16)

    grid_spec = pltpu.PrefetchScalarGridSpec(
        num_scalar_prefetch=1, grid=(per,),
        in_specs=[pl.BlockSpec((tr, C), lambda i, p: (i, 0))],
        out_specs=pl.BlockSpec((tr, C), lambda i, p: (p[0] * per + i, 0)))
    return pl.pallas_call(
        body, name=name, grid_spec=grid_spec, out_shape=jax.ShapeDtypeStruct((N_SHARDS * R, C), BF16),
        compiler_params=_params(dimension_semantics=("arbitrary",)),
    )(place, w)


def _sum_halves(g4, recv, place, name):
    NS, R, C = g4.shape
    hr = R // 2
    tr = _row_tile(hr)
    per = hr // tr

    def body(p_ref, g_ref, r_ref, o_ref):
        o_ref[...] = (g_ref[...] + r_ref[...]).astype(BF16)

    grid_spec = pltpu.PrefetchScalarGridSpec(
        num_scalar_prefetch=1, grid=(NS, per),
        in_specs=[pl.BlockSpec((1, tr, C), lambda s, i, p: (s, p[1] * per + i, 0)),
                  pl.BlockSpec((1, tr, C), lambda s, i, p: (s, i, 0))],
        out_specs=pl.BlockSpec((1, tr, C), lambda s, i, p: (s, i, 0)))
    return pl.pallas_call(
        body, name=name, grid_spec=grid_spec, out_shape=jax.ShapeDtypeStruct((NS, hr, C), BF16),
        compiler_params=_params(dimension_semantics=("arbitrary", "arbitrary")),
    )(place, g4, recv)


def _sum_quarters(h4, recv3, place, name):
    NS, hr, C = h4.shape
    tr = _row_tile(hr)
    per = hr // tr

    def body(p_ref, h_ref, r_ref, o_ref):
        acc = h_ref[0].astype(F32)
        for k in range(N_SHARDS - 1):
            acc = acc + r_ref[k].astype(F32)
        o_ref[...] = acc

    grid_spec = pltpu.PrefetchScalarGridSpec(
        num_scalar_prefetch=1, grid=(per,),
        in_specs=[pl.BlockSpec((1, tr, C), lambda i, p: (p[0], i, 0)),
                  pl.BlockSpec((N_SHARDS - 1, tr, C), lambda i, p: (0, i, 0))],
        out_specs=pl.BlockSpec((tr, C), lambda i, p: (p[1] * per + i, 0)))
    return pl.pallas_call(
        body, name=name, grid_spec=grid_spec, out_shape=jax.ShapeDtypeStruct((2 * hr, C), F32),
        compiler_params=_params(dimension_semantics=("arbitrary",)),
    )(place, h4, recv3)


def _place():
    x, y, c = lax.axis_index("x"), lax.axis_index("y"), lax.axis_index("c")
    chips = [(1 - x, y), (x, 1 - y), (1 - x, 1 - y)]
    return x, y, c, chips


HBM_SPEC = pl.BlockSpec(memory_space=pltpu.HBM)
SEM_SPEC = pl.BlockSpec(memory_space=pltpu.SEMAPHORE)
DATAFLOW = pltpu.SideEffectType.DATAFLOW_SIDE_EFFECTING


def _hbm(a):
    return pltpu.with_memory_space_constraint(a, pltpu.HBM)


class InFlight(NamedTuple):
    send_sem: jax.Array
    recv_sem: jax.Array
    bufs: list
    plan: Callable
    token: jax.Array


def _wait_all(plan, refs, send_ref, recv_ref):
    for k, (src, dst, dev) in enumerate(plan(refs)):
        cp = pltpu.make_async_remote_copy(src_ref=src, dst_ref=dst, send_sem=send_ref.at[k], recv_sem=recv_ref.at[k],
                                          device_id=dev, device_id_type=MESH)
        cp.wait_send()
        cp.wait_recv()


def _start_all(plan, refs, send_ref, recv_ref):
    for k, (src, dst, dev) in enumerate(plan(refs)):
        pltpu.make_async_remote_copy(src_ref=src, dst_ref=dst, send_sem=send_ref.at[k], recv_sem=recv_ref.at[k],
                                     device_id=dev, device_id_type=MESH).start()


def _split_start(name, bufs, plan, n_copies, after, waits=None):
    nb = len(bufs)
    after = list(after) if isinstance(after, (list, tuple)) else [after]

    def body(*refs):
        ins = refs[:nb]
        if waits is not None:
            _wait_all(waits.plan, ins, refs[nb], refs[nb + 1])
        send_sem, recv_sem, token = refs[-nb - 3], refs[-nb - 2], refs[-1]
        _start_all(plan, ins, send_sem, recv_sem)
        token[...] = jnp.zeros_like(token)

    sems = pltpu.SemaphoreType.DMA((n_copies,))
    earlier = [] if waits is None else [waits.send_sem, waits.recv_sem]
    out = pl.pallas_call(
        body, name=name, in_specs=[HBM_SPEC] * nb + [SEM_SPEC] * len(earlier) + [TOKEN_SPEC] * len(after),
        out_shape=(sems, sems, *[pltpu.HBM(b.shape, b.dtype) for b in bufs], jax.ShapeDtypeStruct((8, 128), F32)),
        out_specs=(SEM_SPEC, SEM_SPEC, *[HBM_SPEC] * nb, pl.BlockSpec(memory_space=pltpu.VMEM)),
        input_output_aliases={i: i + 2 for i in range(nb)},
        compiler_params=pltpu.CompilerParams(has_side_effects=DATAFLOW),
    )(*[_hbm(b) for b in bufs], *earlier, *after)
    return InFlight(out[0], out[1], list(out[2:2 + nb]), plan, out[-1])


def _split_wait(name, flight, after):
    nb = len(flight.bufs)
    after = list(after) if isinstance(after, (list, tuple)) else [after]

    def body(*refs):
        _wait_all(flight.plan, refs[:nb], refs[nb], refs[nb + 1])

    out = pl.pallas_call(
        body, name=name, in_specs=[HBM_SPEC] * nb + [SEM_SPEC, SEM_SPEC] + [TOKEN_SPEC] * len(after),
        out_shape=[pltpu.HBM(b.shape, b.dtype) for b in flight.bufs], out_specs=[HBM_SPEC] * nb,
        input_output_aliases={i: i for i in range(nb)},
        compiler_params=pltpu.CompilerParams(has_side_effects=DATAFLOW),
    )(*flight.bufs, flight.send_sem, flight.recv_sem, *after)
    return list(out)


def _half_rows(buf, chip, core):
    hr = buf.shape[0] // (2 * N_SHARDS)
    return buf.at[pl.ds(pl.multiple_of((2 * chip + core) * hr, 16), hr)]


def _gather_ici_plan(bufs):
    x, y, c, chips = _place()
    return [(_half_rows(b, 2 * x + y, c), _half_rows(b, 2 * x + y, c), (px, py, c)) for b in bufs for px, py in chips]


def _gather_d2d_plan(bufs):
    x, y, c, chips = _place()
    return [(_half_rows(b, 2 * px + py, c), _half_rows(b, 2 * px + py, c), (x, y, 1 - c)) for b in bufs for px, py in chips]


def _swap_plan(bufs):
    x, y, c, _ = _place()
    n = len(bufs) // 2
    copies = []
    for g, land in zip(bufs[:n], bufs[n:]):
        hr = g.shape[1] // 2
        copies.append((g.at[:, pl.ds(pl.multiple_of((1 - c) * hr, 8), hr)], land, (x, y, 1 - c)))
    return copies


def _exchange_plan(bufs):
    x, y, c, chips = _place()
    n = len(bufs) // 2
    return [(h.at[2 * px + py], land.at[k], (px, py, c))
            for h, land in zip(bufs[:n], bufs[n:]) for k, (px, py) in enumerate(chips)]


def _share_plan(bufs):
    x, y, c, _ = _place()
    copies = []
    for buf in bufs:
        hr = buf.shape[0] // 2
        mine = buf.at[pl.ds(pl.multiple_of(c * hr, 8), hr)]
        copies.append((mine, mine, (x, y, 1 - c)))
    return copies


N_DEVICES = 8


def _slot_place(vec, place):
    R, C = vec.shape

    def body(p_ref, v_ref, o_ref):
        o_ref[0] = v_ref[...]

    grid_spec = pltpu.PrefetchScalarGridSpec(
        num_scalar_prefetch=1, grid=(1,), in_specs=[pl.BlockSpec((R, C), lambda i, p: (0, 0))],
        out_specs=pl.BlockSpec((1, R, C), lambda i, p: (2 * p[0] + p[1], 0, 0)))
    return pl.pallas_call(
        body, name="slot_place", grid_spec=grid_spec, out_shape=jax.ShapeDtypeStruct((N_DEVICES, R, C), F32),
        compiler_params=_params(dimension_semantics=("arbitrary",)),
    )(place, vec)


def _slots_plan(bufs):
    x, y, c, _ = _place()
    mine = bufs[0].at[4 * x + 2 * y + c]
    return [(mine, mine, (x ^ (r >> 2), y ^ ((r >> 1) & 1), c ^ (r & 1))) for r in range(1, N_DEVICES)]


def _sum_slots(slots):
    _, R, C = slots.shape

    def body(s_ref, o_ref):
        acc = s_ref[0]
        for d in range(1, N_DEVICES):
            acc = acc + s_ref[d]
        o_ref[...] = acc

    return pl.pallas_call(
        body, name="sum_slots", in_specs=[pl.BlockSpec(memory_space=pltpu.VMEM)],
        out_specs=pl.BlockSpec(memory_space=pltpu.VMEM), out_shape=jax.ShapeDtypeStruct((R, C), F32),
        compiler_params=_params(),
    )(slots)


SMALL = ("ffn1_norm", "mix_norm", "pool_w", "pool_scale", "q_norm", "k_norm", "sinks", "gate_bias", "ffn2_norm")
SMALL_COLS = 1024
FFN1 = ("ffn1_w_gate", "ffn1_w_up", "ffn1_w_down")
MIXER = ("w_in", "w_pool_out", "w_attn_out", "w_out")
FFN2 = ("ffn2_w_gate", "ffn2_w_up", "ffn2_w_down")
LARGE = FFN1 + MIXER + FFN2
TRANSPOSED = ("ffn1_w_gate", "ffn1_w_up", "w_in", "w_pool_out", "ffn2_w_gate", "ffn2_w_up")
WEIGHTS = ("ffn1_norm", "ffn1_w_gate", "ffn1_w_up", "ffn1_w_down", "mix_norm", "w_in", "pool_w", "pool_scale",
           "w_pool_out", "q_norm", "k_norm", "sinks", "w_attn_out", "gate_bias", "w_out", "ffn2_norm",
           "ffn2_w_gate", "ffn2_w_up", "ffn2_w_down")


def _pack_small(parts):
    flat = jnp.concatenate([p.reshape(-1) for p in parts])
    rows = -(-flat.shape[0] // (8 * SMALL_COLS)) * 8
    return jnp.pad(flat, (0, rows * SMALL_COLS - flat.shape[0])).reshape(rows, SMALL_COLS)


def _unpack_small(packed, like):
    flat = packed.reshape(-1)
    out, off = [], 0
    for p in like:
        out.append(flat[off:off + p.size].reshape(p.shape))
        off += p.size
    return out, flat[off]


def _tie(small, token):
    return small + token[0, 0]


class Reduction:
    def __init__(self, group, names, grads, place):
        self.group, self.names, self.place = group, names, place
        self.bufs = [g.reshape(N_SHARDS, -1, g.shape[-1]) for g in grads]
        self.flight = None

    def _start(self, stage, bufs, plan, n_copies, after):
        self.flight = _split_start(f"{self.group}_{stage}", bufs, plan, n_copies, after)
        return self.flight.token

    def _landed(self, stage, after):
        n = len(self.names)
        bufs = _split_wait(f"{self.group}_{stage}_wait", self.flight, after)
        return bufs[:n], bufs[n:]

    def start_swap(self, after):
        lands = [lax.empty((g.shape[0], g.shape[1] // 2, g.shape[2]), g.dtype) for g in self.bufs]
        return self._start("swap", self.bufs + lands, _swap_plan, len(self.bufs), after)

    def start_exchange(self, after):
        g4, recv = self._landed("swap", after)
        halves = [_sum_halves(g, r, self.place, f"sum_halves_{n}") for n, g, r in zip(self.names, g4, recv)]
        lands = [lax.empty((N_SHARDS - 1,) + h.shape[1:], h.dtype) for h in halves]
        return self._start("exchange", halves + lands, _exchange_plan, (N_SHARDS - 1) * len(halves), self.place)

    def start_share(self, after):
        halves, recv3 = self._landed("exchange", after)
        reduced = [_sum_quarters(h, r, self.place, f"sum_quarters_{n}") for n, h, r in zip(self.names, halves, recv3)]
        return self._start("share", reduced, _share_plan, len(reduced), self.place)

    def finish(self, after):
        return dict(zip(self.names, _split_wait(f"{self.group}_share_wait", self.flight, after)))


def _row_form(name, a):
    return a.T if name in TRANSPOSED else a


def kernel(x, ffn1_norm, ffn1_w_gate, ffn1_w_up, ffn1_w_down, mix_norm, w_in, pool_w, pool_scale, w_pool_out, q_norm, k_norm, sinks, w_attn_out, gate_bias, w_out, ffn2_norm, ffn2_w_gate, ffn2_w_up, ffn2_w_down, loss_target, m_ffn1_norm, m_ffn1_w_gate, m_ffn1_w_up, m_ffn1_w_down, m_mix_norm, m_w_in, m_pool_w, m_pool_scale, m_w_pool_out, m_q_norm, m_k_norm, m_sinks, m_w_attn_out, m_gate_bias, m_w_out, m_ffn2_norm, m_ffn2_w_gate, m_ffn2_w_up, m_ffn2_w_down, v_ffn1_norm, v_ffn1_w_gate, v_ffn1_w_up, v_ffn1_w_down, v_mix_norm, v_w_in, v_pool_w, v_pool_scale, v_w_pool_out, v_q_norm, v_k_norm, v_sinks, v_w_attn_out, v_gate_bias, v_w_out, v_ffn2_norm, v_ffn2_w_gate, v_ffn2_w_up, v_ffn2_w_down):
    args = dict(locals())
    wts = {n: _row_form(n, args[n]) for n in WEIGHTS}
    mom = {n: _row_form(n, args["m_" + n]) for n in WEIGHTS}
    var = {n: _row_form(n, args["v_" + n]) for n in WEIGHTS}
    shard = 2 * lax.axis_index("x") + lax.axis_index("y")
    place = jnp.stack([shard, lax.axis_index("c")]).astype(jnp.int32)

    xs, target = x[0], loss_target[0]
    D = xs.shape[1]
    g1 = wts["ffn1_norm"].reshape(1, D)
    gm = wts["mix_norm"].reshape(1, D)
    g2 = wts["ffn2_norm"].reshape(1, D)
    qw = wts["q_norm"].reshape(1, HEAD_DIM)
    kw = wts["k_norm"].reshape(1, HEAD_DIM)
    bias = wts["gate_bias"].reshape(1, 2 * D)
    pscale = wts["pool_scale"].reshape(1, -1)
    pw = wts["pool_w"].astype(BF16)
    sink_rows = jnp.broadcast_to(jnp.repeat(wts["sinks"], ATTN_BLOCK).reshape(N_KV_HEADS, GQA_GROUP * ATTN_BLOCK, 1),
                                 (N_KV_HEADS, GQA_GROUP * ATTN_BLOCK, 128))

    first, down1 = FFN1[:2], FFN1[2:]
    placed = {n: _cast_place(wts[n], place, "cast_" + n) for n in FFN1}
    first_ici = _split_start("gather_first_ici", [placed[n] for n in first], _gather_ici_plan, 3 * len(first), place)
    placed.update({n: _cast_place(wts[n], place, "cast_" + n) for n in MIXER + FFN2})
    zero = jnp.zeros((1,), F32)
    packed = [_pack_small([d[n] for n in SMALL] + [zero]) for d in (wts, mom, var)]
    shadow = [placed[n] for n in down1 + MIXER + FFN2] + packed + [sink_rows, pw]
    first_d2d = _split_start("gather_first_d2d", first_ici.bufs, _gather_d2d_plan, 3 * len(first), shadow, waits=first_ici)
    down1_ici = _split_start("gather_down1_ici", [placed[n] for n in down1], _gather_ici_plan, 3 * len(down1), first_d2d.token)
    w = dict(zip(first, _split_wait("gather_first_wait", first_d2d, down1_ici.token)))
    mix_ici = _split_start("gather_mix_ici", [placed[n] for n in MIXER], _gather_ici_plan, 3 * len(MIXER), w[first[0]])
    ffn2_ici = _split_start("gather_ffn2_ici", [placed[n] for n in FFN2], _gather_ici_plan, 3 * len(FFN2), mix_ici.token)

    n1, a1, b1, s1 = _ffn_up(xs, _tie(g1, ffn2_ici.token), w["ffn1_w_gate"], w["ffn1_w_up"], "ffn1_up")
    down1_d2d = _split_start("gather_down1_d2d", down1_ici.bufs, _gather_d2d_plan, 3 * len(down1), s1, waits=down1_ici)
    w.update(zip(down1, _split_wait("gather_down1_wait", down1_d2d, down1_d2d.token)))
    h1 = _ffn_down(xs, s1, w["ffn1_w_down"], "ffn1_down")
    mix_d2d = _split_start("gather_mix_d2d", mix_ici.bufs, _gather_d2d_plan, 3 * len(MIXER), h1, waits=mix_ici)
    w.update(zip(MIXER, _split_wait("gather_mix_wait", mix_d2d, mix_d2d.token)))
    u, xp, q, kv, gl = _mix_proj(h1, gm, w["w_in"])
    attn = _attn_fwd(q, kv, qw, kw, sink_rows)
    ffn2_d2d = _split_start("gather_ffn2_d2d", ffn2_ici.bufs, _gather_d2d_plan, 3 * len(FFN2), attn, waits=ffn2_ici)
    h2, bp, ba = _mix_out(xp, attn, gl, _tie(bias, ffn2_d2d.token), pw, pscale, w["w_pool_out"], w["w_attn_out"], w["w_out"], h1)
    w.update(zip(FFN2, _split_wait("gather_ffn2_wait", ffn2_d2d, h2)))
    n2, a2, b2, s2 = _ffn_up(h2, g2, w["ffn2_w_gate"], w["ffn2_w_up"], "ffn2_up")
    dy, dyh, loss = _ffn_down_loss(h2, s2, w["ffn2_w_down"], target, "ffn2_down_loss")

    gw, gs = {}, {}
    da, db = _ffn_bwd_h(dyh, a2, b2, w["ffn2_w_down"], ffn2_d2d.token, "ffn2_bwd_h")
    gw["ffn2_w_gate"], gw["ffn2_w_up"] = _xty(da, n2, "ffn2_dw_gate"), _xty(db, n2, "ffn2_dw_up")
    gw["ffn2_w_down"] = _xty(s2, dyh, "ffn2_dw_down")
    red2 = Reduction("reduce_ffn2", FFN2, [gw[n] for n in FFN2], place)
    token = red2.start_swap(da)
    dh2, gs["ffn2_norm"] = _ffn_bwd_x(da, db, w["ffn2_w_gate"], w["ffn2_w_up"], h2, dy, _tie(g2, token), "ffn2_bwd_x")
    token = red2.start_exchange(dh2)
    dgl, dbp, dba, gw["w_out"], gs["gate_bias"] = _mix_bwd_gate(dh2, w["w_out"], bp, ba, gl, _tie(bias, token))
    dattn, dpooled, gw["w_attn_out"], gw["w_pool_out"], gs["pool_w"], gs["pool_scale"] = _mix_bwd_branch(
        dbp, dba, attn, xp, pw, pscale, w["w_pool_out"], w["w_attn_out"])
    dq, dkv, gs["q_norm"], gs["k_norm"], dsk = _attn_bwd(q, kv, dattn, qw, kw, sink_rows)
    gs["sinks"] = dsk[:, :, 0]
    red2.start_share(dq)
    pieces = (_pool_bwd(dpooled), dq, dkv, dgl)
    gw["w_in"] = _in_bwd_w(pieces, u)
    grads = red2.finish(gw["w_in"])
    redm = Reduction("reduce_mix", MIXER, [gw[n] for n in MIXER], place)
    token = redm.start_swap(grads[FFN2[0]])
    dh1, dh1h, gs["mix_norm"] = _in_bwd_x(pieces, w["w_in"], h1, dh2, _tie(gm, token))
    token = redm.start_exchange(dh1)
    da, db = _ffn_bwd_h(dh1h, a1, b1, w["ffn1_w_down"], token, "ffn1_bwd_h")
    gw["ffn1_w_gate"], gw["ffn1_w_up"] = _xty(da, n1, "ffn1_dw_gate"), _xty(db, n1, "ffn1_dw_up")
    gw["ffn1_w_down"] = _xty(s1, dh1h, "ffn1_dw_down")
    token = redm.start_share([gw[n] for n in FFN1])
    red1 = Reduction("reduce_ffn1", FFN1, [gw[n] for n in FFN1], place)
    token = red1.start_swap(token)
    grad_x, gs["ffn1_norm"] = _ffn_bwd_x(da, db, w["ffn1_w_gate"], w["ffn1_w_up"], xs, dh1, _tie(g1, token), "ffn1_bwd_x")
    grads.update(redm.finish(grad_x))

    small_parts = [gs[n] for n in SMALL] + [loss[0, 0].reshape(1)]
    slots = _split_start("gather_small", [_slot_place(_pack_small(small_parts), place)], _slots_plan, N_DEVICES - 1,
                         grads[MIXER[0]])
    token = red1.start_exchange(slots.token)
    delta, new_m, new_v = {}, {}, {}
    for n in FFN2 + MIXER:
        delta[n], new_m[n], new_v[n], grads[n] = _adamw(wts[n], grads[n], mom[n], var[n], token, "adamw_" + n)
    summed = _sum_slots(_split_wait("gather_small_wait", slots, delta[MIXER[-1]])[0])
    small_grads, loss_sum = _unpack_small(summed, [wts[n] for n in SMALL])
    grads.update(dict(zip(SMALL, small_grads)))
    ds, ms, vs, _ = _adamw(packed[0], summed, packed[1], packed[2], token, "adamw_small")
    like = [wts[n] for n in SMALL]
    for out, packed_out in ((delta, ds), (new_m, ms), (new_v, vs)):
        out.update(dict(zip(SMALL, _unpack_small(packed_out, like)[0])))
    token = red1.start_share(ds)
    grads.update(red1.finish(token))
    for n in FFN1:
        delta[n], new_m[n], new_v[n], grads[n] = _adamw(wts[n], grads[n], mom[n], var[n], token, "adamw_" + n)

    outs = []
    for d in (grads, delta, new_m, new_v):
        outs += [_row_form(n, d[n]) for n in WEIGHTS]
    return (loss_sum, grad_x[None], *outs)
```

```python
from typing import Callable, NamedTuple

import jax
import jax.numpy as jnp
from jax import lax
from jax.experimental import pallas as pl
from jax.experimental.pallas import tpu as pltpu

F32 = jnp.float32
BF16 = jnp.bfloat16
RMS_EPS = 1e-6
POOL_WINDOWS = (2, 4, 8, 16)
POOL_GROUP = 128
POOL_HALO = 16
HEAD_DIM = 64
GQA_GROUP = 8
N_KV_HEADS = 2
ATTN_BLOCK = 128
SCALE = HEAD_DIM ** -0.5
NEG = -1e30
N_SHARDS = 4
ADAM_LR, ADAM_B1, ADAM_B2, ADAM_EPS, ADAM_WD, ADAM_STEP = 0.001, 0.9, 0.999, 1e-08, 0.01, 10
VMEM_LIMIT = 56 * 1024 * 1024
MESH = pl.DeviceIdType.MESH
SEG_POOL, SEG_Q, SEG_KV, SEG_GATE = (0, 512), (512, 1024), (1536, 256), (1792, 2048)
SEGMENTS = (SEG_POOL, SEG_Q, SEG_KV, SEG_GATE)


def _params(**kw):
    return pltpu.CompilerParams(vmem_limit_bytes=VMEM_LIMIT, **kw)


def _dot(a, b):
    return jnp.dot(a, b, preferred_element_type=F32)


def _dot_nt(a, b):
    return lax.dot_general(a, b, (((1,), (1,)), ((), ())), preferred_element_type=F32)


def _dot_tn(a, b):
    return lax.dot_general(a, b, (((0,), (0,)), ((), ())), preferred_element_type=F32)


def _rinv(x):
    return lax.rsqrt(jnp.mean(x * x, axis=-1, keepdims=True) + RMS_EPS)


def _rms_bwd(dn, x, g):
    r = _rinv(x)
    xr = x * r
    z = dn * g
    dx = r * (z - xr * jnp.mean(z * xr, axis=-1, keepdims=True))
    return dx, dn * xr


def _acc(ref, val, first):
    @pl.when(first)
    def _():
        ref[...] = val

    @pl.when(jnp.logical_not(first))
    def _():
        ref[...] += val


TOKEN_SPEC = pl.BlockSpec(memory_space=pl.ANY)
F_HALF = 1408


def _resident(w):
    return pl.BlockSpec(w.shape, lambda i: (0, 0), pipeline_mode=pl.Buffered(1))


def _ffn_up(h, gnorm, wgt, wut, name, tm=512):
    S, D = h.shape
    F = wgt.shape[0]

    def body(h_ref, g_ref, wg_ref, wu_ref, n_ref, a_ref, b_ref, s_ref):
        x = h_ref[...]
        n = (x * _rinv(x) * g_ref[...]).astype(BF16)
        n_ref[...] = n
        for lo in range(0, F, F_HALF):
            cols = slice(lo, lo + F_HALF)
            a = _dot_nt(n, wg_ref[cols, :])
            b = _dot_nt(n, wu_ref[cols, :])
            a_ref[:, cols] = a.astype(BF16)
            b_ref[:, cols] = b.astype(BF16)
            s_ref[:, cols] = (a * jax.nn.sigmoid(a) * b).astype(BF16)

    act = pl.BlockSpec((tm, F), lambda i: (i, 0))
    hidden = jax.ShapeDtypeStruct((S, F), BF16)
    return pl.pallas_call(
        body, name=name, grid=(S // tm,),
        in_specs=[pl.BlockSpec((tm, D), lambda i: (i, 0)), pl.BlockSpec((1, D), lambda i: (0, 0)), _resident(wgt), _resident(wut)],
        out_specs=[pl.BlockSpec((tm, D), lambda i: (i, 0)), act, act, act],
        out_shape=[jax.ShapeDtypeStruct((S, D), BF16), hidden, hidden, hidden],
        compiler_params=_params(dimension_semantics=("arbitrary",)),
    )(h, gnorm, wgt, wut)


def _ffn_down(h, s, wd, name, tm=1024):
    S, D = h.shape
    F = wd.shape[0]

    def body(h_ref, s_ref, wd_ref, o_ref):
        o_ref[...] = h_ref[...] + 0.5 * _dot(s_ref[...], wd_ref[...])

    row = pl.BlockSpec((tm, D), lambda i: (i, 0))
    return pl.pallas_call(
        body, name=name, grid=(S // tm,), in_specs=[row, pl.BlockSpec((tm, F), lambda i: (i, 0)), _resident(wd)],
        out_specs=row, out_shape=jax.ShapeDtypeStruct((S, D), F32),
        compiler_params=_params(dimension_semantics=("arbitrary",)),
    )(h, s, wd)


def _ffn_down_loss(h, s, wd, target, name, tm=512):
    S, D = h.shape
    F = wd.shape[0]

    def body(h_ref, s_ref, wd_ref, t_ref, dy_ref, dyh_ref, loss_ref):
        e = h_ref[...] + 0.5 * _dot(s_ref[...], wd_ref[...]) - t_ref[...]
        dy = e * (1.0 / D)
        dy_ref[...] = dy
        dyh_ref[...] = (0.5 * dy).astype(BF16)
        tot = jnp.sum(jnp.sum(e * e, axis=1, keepdims=True), axis=0, keepdims=True) * (0.5 / D)
        _acc(loss_ref, jnp.broadcast_to(tot, loss_ref.shape), pl.program_id(0) == 0)

    row = pl.BlockSpec((tm, D), lambda i: (i, 0))
    return pl.pallas_call(
        body, name=name, grid=(S // tm,), in_specs=[row, pl.BlockSpec((tm, F), lambda i: (i, 0)), _resident(wd), row],
        out_specs=[row, row, pl.BlockSpec((8, 128), lambda i: (0, 0))],
        out_shape=[jax.ShapeDtypeStruct((S, D), F32), jax.ShapeDtypeStruct((S, D), BF16), jax.ShapeDtypeStruct((8, 128), F32)],
        compiler_params=_params(dimension_semantics=("arbitrary",)),
    )(h, s, wd, target)


def _ffn_bwd_h(dyh, a, b, wd, token, name, tm=512):
    S, D = dyh.shape
    F = wd.shape[0]

    def body(dyh_ref, a_ref, b_ref, wd_ref, _, da_ref, db_ref):
        dyh_t = dyh_ref[...]
        for lo in range(0, F, F_HALF):
            cols = slice(lo, lo + F_HALF)
            av = a_ref[:, cols].astype(F32)
            bv = b_ref[:, cols].astype(F32)
            ds = _dot_nt(dyh_t, wd_ref[cols, :])
            sig = jax.nn.sigmoid(av)
            silu = av * sig
            da_ref[:, cols] = (ds * bv * (sig * (1.0 + av * (1.0 - sig)))).astype(BF16)
            db_ref[:, cols] = (ds * silu).astype(BF16)

    act = pl.BlockSpec((tm, F), lambda i: (i, 0))
    hidden = jax.ShapeDtypeStruct((S, F), BF16)
    return pl.pallas_call(
        body, name=name, grid=(S // tm,),
        in_specs=[pl.BlockSpec((tm, D), lambda i: (i, 0)), act, act, _resident(wd), TOKEN_SPEC],
        out_specs=[act, act], out_shape=[hidden, hidden],
        compiler_params=_params(dimension_semantics=("arbitrary",)),
    )(dyh, a, b, wd, token)


def _xty(x, y, name, tk=2048, tf=1408):
    S, F = x.shape
    D = y.shape[1]

    def body(x_ref, y_ref, o_ref):
        _acc(o_ref, _dot_tn(x_ref[...], y_ref[...]), pl.program_id(1) == 0)

    return pl.pallas_call(
        body, name=name, grid=(F // tf, S // tk),
        in_specs=[pl.BlockSpec((tk, tf), lambda j, k: (k, j)), pl.BlockSpec((tk, D), lambda j, k: (k, 0))],
        out_specs=pl.BlockSpec((tf, D), lambda j, k: (j, 0)), out_shape=jax.ShapeDtypeStruct((F, D), F32),
        compiler_params=_params(dimension_semantics=("arbitrary", "arbitrary")),
    )(x, y)


def _ffn_bwd_x(da, db, wgt, wut, x, dh, gnorm, name, tm=512):
    S, D = x.shape
    F = wgt.shape[0]

    def body(da_ref, db_ref, wg_ref, wu_ref, x_ref, dh_ref, g_ref, dx_ref, dg_ref):
        dn = _dot(da_ref[...], wg_ref[...]) + _dot(db_ref[...], wu_ref[...])
        dx, dg_rows = _rms_bwd(dn, x_ref[...], g_ref[...])
        dx_ref[...] = dh_ref[...] + dx
        _acc(dg_ref, jnp.sum(dg_rows, axis=0, keepdims=True), pl.program_id(0) == 0)

    row = pl.BlockSpec((tm, D), lambda i: (i, 0))
    act = pl.BlockSpec((tm, F), lambda i: (i, 0))
    vec = pl.BlockSpec((1, D), lambda i: (0, 0))
    return pl.pallas_call(
        body, name=name, grid=(S // tm,), in_specs=[act, act, _resident(wgt), _resident(wut), row, row, vec],
        out_specs=[row, vec],
        out_shape=[jax.ShapeDtypeStruct((S, D), F32), jax.ShapeDtypeStruct((1, D), F32)],
        compiler_params=_params(dimension_semantics=("arbitrary",)),
    )(da, db, wgt, wut, x, dh, gnorm)


def _mix_proj(h, gnorm, wint, tm=512):
    S, D = h.shape
    nt = S // tm

    def body(h_ref, g_ref, w_ref, u_ref, xp_ref, q_ref, kv_ref, gl_ref):
        x = h_ref[...]
        u = (x * _rinv(x) * g_ref[...]).astype(BF16)
        u_ref[...] = u
        for (off, width), ref in zip(SEGMENTS, (xp_ref, q_ref, kv_ref, gl_ref)):
            ref[...] = _dot_nt(u, w_ref[off:off + width, :]).astype(ref.dtype)

    def row(width):
        return pl.BlockSpec((tm, width), lambda i: (i, 0))

    return pl.pallas_call(
        body, name="mix_proj", grid=(nt,),
        in_specs=[row(D), pl.BlockSpec((1, D), lambda i: (0, 0)), pl.BlockSpec(wint.shape, lambda i: (0, 0))],
        out_specs=[row(D), row(SEG_POOL[1]), row(SEG_Q[1]), row(SEG_KV[1]), row(SEG_GATE[1])],
        out_shape=[jax.ShapeDtypeStruct((S, D), BF16), jax.ShapeDtypeStruct((S, SEG_POOL[1]), F32),
                   jax.ShapeDtypeStruct((S, SEG_Q[1]), BF16), jax.ShapeDtypeStruct((S, SEG_KV[1]), BF16),
                   jax.ShapeDtypeStruct((S, SEG_GATE[1]), BF16)],
        compiler_params=_params(dimension_semantics=("arbitrary",)),
    )(h, gnorm, wint)


def _stack_heads(x, g):
    return jnp.concatenate([x[:, (GQA_GROUP * g + hh) * HEAD_DIM:(GQA_GROUP * g + hh + 1) * HEAD_DIM]
                            for hh in range(GQA_GROUP)], axis=0)


def _unstack_heads(ref, val, g):
    for hh in range(GQA_GROUP):
        lo = (GQA_GROUP * g + hh) * HEAD_DIM
        ref[:, lo:lo + HEAD_DIM] = val[hh * ATTN_BLOCK:(hh + 1) * ATTN_BLOCK, :]


def _rowsum(xb, width):
    return _dot(xb, jnp.ones((xb.shape[1], width), BF16))


def _rinv_lanes(x):
    return lax.rsqrt(_rowsum((x * x).astype(BF16), x.shape[1]) * (1.0 / x.shape[1]) + RMS_EPS)


def _attn_exp(qn, kc, kp, sink, n):
    rows = GQA_GROUP * ATTN_BLOCK
    qi = lax.broadcasted_iota(jnp.int32, (rows, ATTN_BLOCK), 0) % ATTN_BLOCK
    kj = lax.broadcasted_iota(jnp.int32, (rows, ATTN_BLOCK), 1)
    s_c = jnp.where(kj <= qi, _dot_nt(qn, kc), NEG)
    s_p = jnp.where(jnp.logical_and(kj > qi, n > 0), _dot_nt(qn, kp), NEG)
    m = jnp.maximum(jnp.max(s_c, axis=-1, keepdims=True), jnp.max(s_p, axis=-1, keepdims=True))
    m = jnp.maximum(jnp.broadcast_to(m, sink.shape), sink)
    e_c = jnp.exp(s_c - m)
    e_p = jnp.exp(s_p - m)
    e_s = jnp.exp(sink - m)
    e_cb, e_pb = e_c.astype(BF16), e_p.astype(BF16)
    inv = 1.0 / (_rowsum(jnp.concatenate([e_cb, e_pb], axis=1), ATTN_BLOCK) + e_s)
    return e_cb, e_pb, e_c, e_p, e_s, inv


def _attn_blocks(n):
    cur = pl.multiple_of(n * ATTN_BLOCK, ATTN_BLOCK)
    prev = pl.multiple_of(jnp.maximum(n - 1, 0) * ATTN_BLOCK, ATTN_BLOCK)
    return cur, prev


def _kv_split(kv, g):
    k = kv[:, g * HEAD_DIM:(g + 1) * HEAD_DIM]
    v = kv[:, (N_KV_HEADS + g) * HEAD_DIM:(N_KV_HEADS + g + 1) * HEAD_DIM]
    return k, v


def _attn_fwd(q, kv, qw, kw, sink_rows):
    S, W = q.shape
    nb = S // ATTN_BLOCK

    def body(q_ref, kv_ref, qw_ref, kw_ref, sk_ref, o_ref, o_scr):
        n = pl.program_id(0)
        qf = q_ref[...].astype(F32)
        cur, prev = _attn_blocks(n)
        kvc = kv_ref[pl.ds(cur, ATTN_BLOCK), :].astype(F32)
        kvp = kv_ref[pl.ds(prev, ATTN_BLOCK), :].astype(F32)
        for g in range(N_KV_HEADS):
            qs = _stack_heads(qf, g)
            qn = (qs * _rinv_lanes(qs) * qw_ref[...] * SCALE).astype(BF16)
            kc, vc = _kv_split(kvc, g)
            kp, vp = _kv_split(kvp, g)
            kc = (kc * _rinv_lanes(kc) * kw_ref[...]).astype(BF16)
            kp = (kp * _rinv_lanes(kp) * kw_ref[...]).astype(BF16)
            e_cb, e_pb, _, _, _, inv = _attn_exp(qn, kc, kp, sk_ref[g], n)
            o = (_dot(e_cb, vc.astype(BF16)) + _dot(e_pb, vp.astype(BF16))) * inv[:, :HEAD_DIM]
            _unstack_heads(o_scr, o, g)
        o_ref[...] = o_scr[...].astype(BF16)

    blk = pl.BlockSpec((ATTN_BLOCK, W), lambda n: (n, 0))
    return pl.pallas_call(
        body, name="attn_fwd", grid=(nb,),
        in_specs=[blk, pl.BlockSpec(kv.shape, lambda n: (0, 0)), pl.BlockSpec((1, HEAD_DIM), lambda n: (0, 0)),
                  pl.BlockSpec((1, HEAD_DIM), lambda n: (0, 0)), pl.BlockSpec(sink_rows.shape, lambda n: (0, 0, 0))],
        out_specs=blk, out_shape=jax.ShapeDtypeStruct((S, W), BF16),
        scratch_shapes=[pltpu.VMEM((ATTN_BLOCK, W), F32)],
        compiler_params=_params(dimension_semantics=("arbitrary",)),
    )(q, kv, qw, kw, sink_rows)


def _attn_bwd(q, kv, do, qw, kw, sink_rows):
    S, W = q.shape
    KW = kv.shape[1]
    nb = S // ATTN_BLOCK
    chunk = 512

    def body(q_ref, kv_ref, do_ref, qw_ref, kw_ref, sk_ref, dq_ref, dkv_ref, dqw_ref, dkw_ref, dsk_ref, dq_scr):
        n = pl.program_id(0)

        @pl.when(n == 0)
        def _():
            dkv_ref[...] = jnp.zeros_like(dkv_ref)
            dqw_ref[...] = jnp.zeros_like(dqw_ref)
            dsk_ref[...] = jnp.zeros_like(dsk_ref)

        qf = q_ref[...].astype(F32)
        dof = do_ref[...].astype(F32)
        cur, prev = _attn_blocks(n)
        kvc = kv_ref[pl.ds(cur, ATTN_BLOCK), :].astype(F32)
        kvp = kv_ref[pl.ds(prev, ATTN_BLOCK), :].astype(F32)
        qw_v = qw_ref[...]
        for g in range(N_KV_HEADS):
            qs = _stack_heads(qf, g)
            rq = _rinv_lanes(qs)
            qhat = qs * rq
            qn = (qhat * qw_v * SCALE).astype(BF16)
            kc, vc = _kv_split(kvc, g)
            kp, vp = _kv_split(kvp, g)
            kc = (kc * _rinv_lanes(kc) * kw_ref[...]).astype(BF16)
            kp = (kp * _rinv_lanes(kp) * kw_ref[...]).astype(BF16)
            vc = vc.astype(BF16)
            vp = vp.astype(BF16)
            dos = _stack_heads(dof, g).astype(BF16)
            _, _, e_c, e_p, e_s, inv = _attn_exp(qn, kc, kp, sk_ref[g], n)
            p_c, p_p, p_s = e_c * inv, e_p * inv, e_s * inv
            dp_c = _dot_nt(dos, vc)
            dp_p = _dot_nt(dos, vp)
            drow = _rowsum(jnp.concatenate([(p_c * dp_c).astype(BF16), (p_p * dp_p).astype(BF16)], axis=1), ATTN_BLOCK)
            ds_c = (p_c * (dp_c - drow)).astype(BF16)
            ds_p = (p_p * (dp_p - drow)).astype(BF16)
            dsink = -(p_s * drow)
            for hh in range(GQA_GROUP):
                tot = jnp.sum(dsink[hh * ATTN_BLOCK:(hh + 1) * ATTN_BLOCK, :], axis=0, keepdims=True)
                dsk_ref[g, hh:hh + 1, :] += tot
            dqn = (_dot(ds_c, kc) + _dot(ds_p, kp)) * SCALE
            klo, vlo = g * HEAD_DIM, (N_KV_HEADS + g) * HEAD_DIM
            dkv_ref[pl.ds(cur, ATTN_BLOCK), klo:klo + HEAD_DIM] += _dot_tn(ds_c, qn)
            dkv_ref[pl.ds(prev, ATTN_BLOCK), klo:klo + HEAD_DIM] += _dot_tn(ds_p, qn)
            dkv_ref[pl.ds(cur, ATTN_BLOCK), vlo:vlo + HEAD_DIM] += _dot_tn(p_c.astype(BF16), dos)
            dkv_ref[pl.ds(prev, ATTN_BLOCK), vlo:vlo + HEAD_DIM] += _dot_tn(p_p.astype(BF16), dos)
            dqw_ref[...] += jnp.sum(dqn * qhat, axis=0, keepdims=True)
            z = dqn * qw_v
            dqs = rq * (z - qhat * (_rowsum((z * qhat).astype(BF16), HEAD_DIM) * (1.0 / HEAD_DIM)))
            _unstack_heads(dq_scr, dqs, g)
        dq_ref[...] = dq_scr[...].astype(BF16)

        @pl.when(n == nb - 1)
        def _():
            def one(c, dkw):
                rows = pl.ds(pl.multiple_of(c * chunk, chunk), chunk)
                for g in range(N_KV_HEADS):
                    lo = g * HEAD_DIM
                    k = kv_ref[rows, lo:lo + HEAD_DIM].astype(F32)
                    dx, dg_rows = _rms_bwd(dkv_ref[rows, lo:lo + HEAD_DIM], k, kw_ref[...])
                    dkv_ref[rows, lo:lo + HEAD_DIM] = dx
                    dkw = dkw + jnp.sum(dg_rows, axis=0, keepdims=True)
                return dkw

            dkw_ref[...] = lax.fori_loop(0, S // chunk, one, jnp.zeros((1, HEAD_DIM), F32))

    blk = pl.BlockSpec((ATTN_BLOCK, W), lambda n: (n, 0))
    whole_kv = pl.BlockSpec((S, KW), lambda n: (0, 0))
    vec = pl.BlockSpec((1, HEAD_DIM), lambda n: (0, 0))
    sk = pl.BlockSpec(sink_rows.shape, lambda n: (0, 0, 0))
    dsk = pl.BlockSpec((N_KV_HEADS, GQA_GROUP, 128), lambda n: (0, 0, 0))
    return pl.pallas_call(
        body, name="attn_bwd", grid=(nb,), in_specs=[blk, whole_kv, blk, vec, vec, sk],
        out_specs=[blk, whole_kv, vec, vec, dsk],
        out_shape=[jax.ShapeDtypeStruct((S, W), BF16), jax.ShapeDtypeStruct((S, KW), F32),
                   jax.ShapeDtypeStruct((1, HEAD_DIM), F32), jax.ShapeDtypeStruct((1, HEAD_DIM), F32),
                   jax.ShapeDtypeStruct((N_KV_HEADS, GQA_GROUP, 128), F32)],
        scratch_shapes=[pltpu.VMEM((ATTN_BLOCK, W), F32)],
        compiler_params=_params(dimension_semantics=("arbitrary",)),
    )(q, kv, do, qw, kw, sink_rows)


def _pooled(xc, xprev, i):
    tm = xc.shape[0]
    xh = jnp.concatenate([jnp.where(i > 0, xprev, 0.0), xc], axis=0)
    t = lax.broadcasted_iota(jnp.int32, (tm, 1), 0) + i * tm
    out = []
    for gi, w in enumerate(POOL_WINDOWS):
        acc = xh[:, gi * POOL_GROUP:(gi + 1) * POOL_GROUP]
        sh = 1
        while sh < w:
            acc = acc + pltpu.roll(acc, sh, 0)
            sh *= 2
        cnt = jnp.minimum(t + 1, w).astype(F32)
        out.append(acc[POOL_HALO:, :] / cnt - xc[:, gi * POOL_GROUP:(gi + 1) * POOL_GROUP])
    return jnp.concatenate(out, axis=1)


def _pool_mix(pooled_b, pw_ref):
    return jnp.concatenate([_dot(pooled_b[:, gi * POOL_GROUP:(gi + 1) * POOL_GROUP], pw_ref[gi])
                            for gi in range(len(POOL_WINDOWS))], axis=1)


def _halo_specs(tm, width, S, after):
    per = tm // POOL_HALO
    last = S // POOL_HALO - 1
    if after:
        return pl.BlockSpec((POOL_HALO, width), lambda i: (jnp.minimum((i + 1) * per, last), 0))
    return pl.BlockSpec((POOL_HALO, width), lambda i: (jnp.maximum(i * per - 1, 0), 0))


def _mix_out(xp, attn, gl, bias, pw, pscale, wpot, wao, wo, h, tm=512):
    S, D = h.shape
    nt = S // tm
    PW = xp.shape[1]

    def body(xc_ref, xprev_ref, at_ref, gl_ref, bias_ref, pw_ref, ps_ref, wpo_ref, wao_ref, wo_ref, h_ref,
             ho_ref, bp_ref, ba_ref, ms_ref):
        i = pl.program_id(0)
        pooled = _pooled(xc_ref[...], xprev_ref[...], i).astype(BF16)
        ms = (_pool_mix(pooled, pw_ref) * ps_ref[...]).astype(BF16)
        ms_ref[...] = ms
        bp = _dot_nt(ms, wpo_ref[...])
        ba = _dot(at_ref[...], wao_ref[...])
        bp_ref[...] = bp.astype(BF16)
        ba_ref[...] = ba.astype(BF16)
        gates = jax.nn.sigmoid(gl_ref[...].astype(F32) + bias_ref[...])
        merged = (gates[:, :D] * bp + gates[:, D:] * ba).astype(BF16)
        ho_ref[...] = h_ref[...] + _dot(merged, wo_ref[...])

    def row(width):
        return pl.BlockSpec((tm, width), lambda i: (i, 0))

    def whole(x):
        nd = x.ndim
        return pl.BlockSpec(x.shape, lambda i: (0,) * nd)

    return pl.pallas_call(
        body, name="mix_out", grid=(nt,),
        in_specs=[row(PW), _halo_specs(tm, PW, S, False), row(D), row(2 * D), whole(bias), whole(pw), whole(pscale),
                  whole(wpot), whole(wao), whole(wo), row(D)],
        out_specs=[row(D), row(D), row(D), row(PW)],
        out_shape=[jax.ShapeDtypeStruct((S, D), F32), jax.ShapeDtypeStruct((S, D), BF16),
                   jax.ShapeDtypeStruct((S, D), BF16), jax.ShapeDtypeStruct((S, PW), BF16)],
        compiler_params=_params(dimension_semantics=("arbitrary",)),
    )(xp, xp, attn, gl, bias, pw, pscale, wpot, wao, wo, h)


def _mix_bwd_gate(dh, wo, bp, ba, gl, bias, tm=512):
    S, D = dh.shape
    nt = S // tm

    def body(dh_ref, wo_ref, bp_ref, ba_ref, gl_ref, bias_ref, dgl_ref, dbp_ref, dba_ref, mg_ref, dhb_ref, dbias_ref):
        i = pl.program_id(0)
        dhb = dh_ref[...].astype(BF16)
        dhb_ref[...] = dhb
        dm = _dot_nt(dhb, wo_ref[...])
        gates = jax.nn.sigmoid(gl_ref[...].astype(F32) + bias_ref[...])
        gp, ga = gates[:, :D], gates[:, D:]
        bp_v = bp_ref[...].astype(F32)
        ba_v = ba_ref[...].astype(F32)
        mg_ref[...] = (gp * bp_v + ga * ba_v).astype(BF16)
        dbp_ref[...] = (dm * gp).astype(BF16)
        dba_ref[...] = (dm * ga).astype(BF16)
        dgl = jnp.concatenate([dm * bp_v * gp * (1.0 - gp), dm * ba_v * ga * (1.0 - ga)], axis=1)
        dgl_ref[...] = dgl.astype(BF16)
        _acc(dbias_ref, jnp.sum(dgl, axis=0, keepdims=True), i == 0)

    def row(width):
        return pl.BlockSpec((tm, width), lambda i: (i, 0))

    def whole(shape):
        return pl.BlockSpec(shape, lambda i: (0, 0))

    return pl.pallas_call(
        body, name="mix_bwd_gate", grid=(nt,),
        in_specs=[row(D), whole(wo.shape), row(D), row(D), row(2 * D), whole(bias.shape)],
        out_specs=[row(2 * D), row(D), row(D), row(D), row(D), whole((1, 2 * D))],
        out_shape=[jax.ShapeDtypeStruct((S, 2 * D), BF16)] + [jax.ShapeDtypeStruct((S, D), BF16)] * 4
        + [jax.ShapeDtypeStruct((1, 2 * D), F32)],
        compiler_params=_params(dimension_semantics=("arbitrary",)),
    )(dh, wo, bp, ba, gl, bias)


def _mix_bwd_branch(dbp, dba, xp, pw, pscale, wpot, wao, tm=512):
    S, D = dbp.shape
    nt = S // tm
    PW = xp.shape[1]
    NG = len(POOL_WINDOWS)

    def body(dbp_ref, dba_ref, xc_ref, xprev_ref, pw_ref, ps_ref, wpo_ref, wao_ref, dat_ref, dpl_ref, dpw_ref, dps_ref):
        i = pl.program_id(0)
        dat_ref[...] = _dot_nt(dba_ref[...], wao_ref[...]).astype(BF16)
        pooled = _pooled(xc_ref[...], xprev_ref[...], i).astype(BF16)
        mixed = _pool_mix(pooled, pw_ref)
        ps = ps_ref[...]
        dms = _dot(dbp_ref[...], wpo_ref[...])
        _acc(dps_ref, jnp.sum(dms * mixed, axis=0, keepdims=True), i == 0)
        dmixed = (dms * ps).astype(BF16)
        dpooled = []
        for gi in range(NG):
            cols = slice(gi * POOL_GROUP, (gi + 1) * POOL_GROUP)
            _acc(dpw_ref.at[gi], _dot_tn(pooled[:, cols], dmixed[:, cols]), i == 0)
            dpooled.append(_dot_nt(dmixed[:, cols], pw_ref[gi]))
        dpl_ref[...] = jnp.concatenate(dpooled, axis=1)

    def row(width):
        return pl.BlockSpec((tm, width), lambda i: (i, 0))

    def whole(shape):
        nd = len(shape)
        return pl.BlockSpec(shape, lambda i: (0,) * nd)

    return pl.pallas_call(
        body, name="mix_bwd_branch", grid=(nt,),
        in_specs=[row(D), row(D), row(PW), _halo_specs(tm, PW, S, False), whole(pw.shape), whole(pscale.shape),
                  whole(wpot.shape), whole(wao.shape)],
        out_specs=[row(D), row(PW), whole(pw.shape), whole((1, PW))],
        out_shape=[jax.ShapeDtypeStruct((S, D), BF16), jax.ShapeDtypeStruct((S, PW), F32),
                   jax.ShapeDtypeStruct(pw.shape, F32), jax.ShapeDtypeStruct((1, PW), F32)],
        compiler_params=_params(dimension_semantics=("arbitrary",)),
    )(dbp, dba, xp, xp, pw, pscale, wpot, wao)


def _pool_bwd(dpooled, tm=512):
    S, PW = dpooled.shape
    nt = S // tm

    def body(dc_ref, dnext_ref, dxp_ref):
        i = pl.program_id(0)
        dc = dc_ref[...]
        dh = jnp.concatenate([dc, jnp.where(i < nt - 1, dnext_ref[...], 0.0)], axis=0)
        rows = tm + POOL_HALO
        t = lax.broadcasted_iota(jnp.int32, (rows, 1), 0) + i * tm
        out = []
        for gi, w in enumerate(POOL_WINDOWS):
            cols = slice(gi * POOL_GROUP, (gi + 1) * POOL_GROUP)
            acc = dh[:, cols] / jnp.minimum(t + 1, w).astype(F32)
            sh = 1
            while sh < w:
                acc = acc + pltpu.roll(acc, rows - sh, 0)
                sh *= 2
            out.append(acc[:tm, :] - dc[:, cols])
        dxp_ref[...] = jnp.concatenate(out, axis=1).astype(BF16)

    return pl.pallas_call(
        body, name="pool_bwd", grid=(nt,),
        in_specs=[pl.BlockSpec((tm, PW), lambda i: (i, 0)), _halo_specs(tm, PW, S, True)],
        out_specs=pl.BlockSpec((tm, PW), lambda i: (i, 0)), out_shape=jax.ShapeDtypeStruct((S, PW), BF16),
        compiler_params=_params(dimension_semantics=("arbitrary",)),
    )(dpooled, dpooled)


def _in_bwd_w(pieces, u, tm=1024):
    S, D = u.shape
    nt = S // tm
    NW = sum(width for _, width in SEGMENTS)

    def body(*refs):
        piece_refs, u_ref, dw_hbm, acc, sem = refs[:len(SEGMENTS)], refs[len(SEGMENTS)], refs[len(SEGMENTS) + 1], refs[-2], refs[-1]
        i = pl.program_id(0)
        u_t = u_ref[...]
        for (off, width), ref in zip(SEGMENTS, piece_refs):
            for lo in range(0, width, D):
                hi = min(lo + D, width)
                _acc(acc.at[off + lo:off + hi, :], _dot_tn(ref[:, lo:hi].astype(BF16), u_t), i == 0)

        @pl.when(i == nt - 1)
        def _():
            cp = pltpu.make_async_copy(acc, dw_hbm, sem)
            cp.start()
            cp.wait()

    return pl.pallas_call(
        body, name="in_bwd_w", grid=(nt,),
        in_specs=[pl.BlockSpec((tm, width), lambda i: (i, 0)) for _, width in SEGMENTS] + [pl.BlockSpec((tm, D), lambda i: (i, 0))],
        out_specs=pl.BlockSpec(memory_space=pl.ANY), out_shape=jax.ShapeDtypeStruct((NW, D), F32),
        scratch_shapes=[pltpu.VMEM((NW, D), F32), pltpu.SemaphoreType.DMA],
        compiler_params=_params(dimension_semantics=("arbitrary",)),
    )(*pieces, u)


def _in_bwd_x(pieces, wint, h, dh, gnorm, tm=512):
    S, D = h.shape
    nt = S // tm

    def body(*refs):
        piece_refs = refs[:len(SEGMENTS)]
        w_ref, h_ref, dh_ref, g_ref, dx_ref, dxh_ref, dg_ref = refs[len(SEGMENTS):]
        i = pl.program_id(0)
        du = jnp.zeros((tm, D), F32)
        for (off, width), ref in zip(SEGMENTS, piece_refs):
            du = du + _dot(ref[...].astype(BF16), w_ref[off:off + width, :])
        dx, dg_rows = _rms_bwd(du, h_ref[...], g_ref[...])
        out = dh_ref[...] + dx
        dx_ref[...] = out
        dxh_ref[...] = (0.5 * out).astype(BF16)
        _acc(dg_ref, jnp.sum(dg_rows, axis=0, keepdims=True), i == 0)

    def row(width):
        return pl.BlockSpec((tm, width), lambda i: (i, 0))

    vec = pl.BlockSpec((1, D), lambda i: (0, 0))
    return pl.pallas_call(
        body, name="in_bwd_x", grid=(nt,),
        in_specs=[row(width) for _, width in SEGMENTS] + [pl.BlockSpec(wint.shape, lambda i: (0, 0)), row(D), row(D), vec],
        out_specs=[row(D), row(D), vec],
        out_shape=[jax.ShapeDtypeStruct((S, D), F32), jax.ShapeDtypeStruct((S, D), BF16), jax.ShapeDtypeStruct((1, D), F32)],
        compiler_params=_params(dimension_semantics=("arbitrary",)),
    )(*pieces, wint, h, dh, gnorm)


def _row_tile(rows):
    for t in (512, 480, 352, 256, 128, 64, 32, 16, 8):
        if rows % t == 0:
            return t
    return rows


def _adamw(w, g, m, v, token, name):
    R, C = w.shape
    tr = _row_tile(R)

    def body(w_ref, g_ref, m_ref, v_ref, _, d_ref, mo_ref, vo_ref, go_ref):
        gv = g_ref[...]
        go_ref[...] = gv
        mn = ADAM_B1 * m_ref[...] + (1.0 - ADAM_B1) * gv
        vn = ADAM_B2 * v_ref[...] + (1.0 - ADAM_B2) * (gv * gv)
        m_hat = mn / (1.0 - ADAM_B1 ** ADAM_STEP)
        v_hat = vn / (1.0 - ADAM_B2 ** ADAM_STEP)
        d_ref[...] = -ADAM_LR * (m_hat / (jnp.sqrt(v_hat) + ADAM_EPS) + ADAM_WD * w_ref[...])
        mo_ref[...] = mn
        vo_ref[...] = vn

    blk = pl.BlockSpec((tr, C), lambda i: (i, 0))
    sh = jax.ShapeDtypeStruct((R, C), F32)
    return pl.pallas_call(
        body, name=name, grid=(R // tr,), in_specs=[blk] * 4 + [TOKEN_SPEC], out_specs=[blk] * 4, out_shape=[sh] * 4,
        compiler_params=_params(dimension_semantics=("arbitrary",)),
    )(w, g, m, v, token)


def _cast_place(w, place, name):
    R, C = w.shape
    tr = _row_tile(R)
    per = R // tr

    def body(p_ref, w_ref, o_ref):
        o_ref[...] = w_ref[...].astype(BF16)

    grid_spec = pltpu.PrefetchScalarGridSpec(
        num_scalar_prefetch=1, grid=(per,),
        in_specs=[pl.BlockSpec((tr, C), lambda i, p: (i, 0))],
        out_specs=pl.BlockSpec((tr, C), lambda i, p: (p[0] * per + i, 0)))
    return pl.pallas_call(
        body, name=name, grid_spec=grid_spec, out_shape=jax.ShapeDtypeStruct((N_SHARDS * R, C), BF16),
        compiler_params=_params(dimension_semantics=("arbitrary",)),
    )(place, w)


def _sum_halves(g4, recv, place, name):
    NS, R, C = g4.shape
    hr = R // 2
    tr = _row_tile(hr)
    per = hr // tr

    def body(p_ref, g_ref, r_ref, o_ref):
        o_ref[...] = (g_ref[...] + r_ref[...]).astype(BF16)

    grid_spec = pltpu.PrefetchScalarGridSpec(
        num_scalar_prefetch=1, grid=(NS, per),
        in_specs=[pl.BlockSpec((1, tr, C), lambda s, i, p: (s, p[1] * per + i, 0)),
                  pl.BlockSpec((1, tr, C), lambda s, i, p: (s, i, 0))],
        out_specs=pl.BlockSpec((1, tr, C), lambda s, i, p: (s, i, 0)))
    return pl.pallas_call(
        body, name=name, grid_spec=grid_spec, out_shape=jax.ShapeDtypeStruct((NS, hr, C), BF16),
        compiler_params=_params(dimension_semantics=("arbitrary", "arbitrary")),
    )(place, g4, recv)


def _sum_quarters(h4, recv3, place, name):
    NS, hr, C = h4.shape
    tr = _row_tile(hr)
    per = hr // tr

    def body(p_ref, h_ref, r_ref, o_ref):
        acc = h_ref[0].astype(F32)
        for k in range(N_SHARDS - 1):
            acc = acc + r_ref[k].astype(F32)
        o_ref[...] = acc

    grid_spec = pltpu.PrefetchScalarGridSpec(
        num_scalar_prefetch=1, grid=(per,),
        in_specs=[pl.BlockSpec((1, tr, C), lambda i, p: (p[0], i, 0)),
                  pl.BlockSpec((N_SHARDS - 1, tr, C), lambda i, p: (0, i, 0))],
        out_specs=pl.BlockSpec((tr, C), lambda i, p: (p[1] * per + i, 0)))
    return pl.pallas_call(
        body, name=name, grid_spec=grid_spec, out_shape=jax.ShapeDtypeStruct((2 * hr, C), F32),
        compiler_params=_params(dimension_semantics=("arbitrary",)),
    )(place, h4, recv3)


def _place():
    x, y, c = lax.axis_index("x"), lax.axis_index("y"), lax.axis_index("c")
    chips = [(1 - x, y), (x, 1 - y), (1 - x, 1 - y)]
    return x, y, c, chips


HBM_SPEC = pl.BlockSpec(memory_space=pltpu.HBM)
SEM_SPEC = pl.BlockSpec(memory_space=pltpu.SEMAPHORE)
DATAFLOW = pltpu.SideEffectType.DATAFLOW_SIDE_EFFECTING


def _hbm(a):
    return pltpu.with_memory_space_constraint(a, pltpu.HBM)


class InFlight(NamedTuple):
    send_sem: jax.Array
    recv_sem: jax.Array
    bufs: list
    plan: Callable
    token: jax.Array


def _wait_all(plan, refs, send_ref, recv_ref):
    for k, (src, dst, dev) in enumerate(plan(refs)):
        cp = pltpu.make_async_remote_copy(src_ref=src, dst_ref=dst, send_sem=send_ref.at[k], recv_sem=recv_ref.at[k],
                                          device_id=dev, device_id_type=MESH)
        cp.wait_send()
        cp.wait_recv()


def _start_all(plan, refs, send_ref, recv_ref):
    for k, (src, dst, dev) in enumerate(plan(refs)):
        pltpu.make_async_remote_copy(src_ref=src, dst_ref=dst, send_sem=send_ref.at[k], recv_sem=recv_ref.at[k],
                                     device_id=dev, device_id_type=MESH).start()


def _split_start(name, bufs, plan, n_copies, after, waits=None):
    nb = len(bufs)
    after = list(after) if isinstance(after, (list, tuple)) else [after]

    def body(*refs):
        ins = refs[:nb]
        if waits is not None:
            _wait_all(waits.plan, ins, refs[nb], refs[nb + 1])
        send_sem, recv_sem, token = refs[-nb - 3], refs[-nb - 2], refs[-1]
        _start_all(plan, ins, send_sem, recv_sem)
        token[...] = jnp.zeros_like(token)

    sems = pltpu.SemaphoreType.DMA((n_copies,))
    earlier = [] if waits is None else [waits.send_sem, waits.recv_sem]
    out = pl.pallas_call(
        body, name=name, in_specs=[HBM_SPEC] * nb + [SEM_SPEC] * len(earlier) + [TOKEN_SPEC] * len(after),
        out_shape=(sems, sems, *[pltpu.HBM(b.shape, b.dtype) for b in bufs], jax.ShapeDtypeStruct((8, 128), F32)),
        out_specs=(SEM_SPEC, SEM_SPEC, *[HBM_SPEC] * nb, pl.BlockSpec(memory_space=pltpu.VMEM)),
        input_output_aliases={i: i + 2 for i in range(nb)},
        compiler_params=pltpu.CompilerParams(has_side_effects=DATAFLOW),
    )(*[_hbm(b) for b in bufs], *earlier, *after)
    return InFlight(out[0], out[1], list(out[2:2 + nb]), plan, out[-1])


def _split_wait(name, flight, after):
    nb = len(flight.bufs)
    after = list(after) if isinstance(after, (list, tuple)) else [after]

    def body(*refs):
        _wait_all(flight.plan, refs[:nb], refs[nb], refs[nb + 1])

    out = pl.pallas_call(
        body, name=name, in_specs=[HBM_SPEC] * nb + [SEM_SPEC, SEM_SPEC] + [TOKEN_SPEC] * len(after),
        out_shape=[pltpu.HBM(b.shape, b.dtype) for b in flight.bufs], out_specs=[HBM_SPEC] * nb,
        input_output_aliases={i: i for i in range(nb)},
        compiler_params=pltpu.CompilerParams(has_side_effects=DATAFLOW),
    )(*flight.bufs, flight.send_sem, flight.recv_sem, *after)
    return list(out)


def _half_rows(buf, chip, core):
    hr = buf.shape[0] // (2 * N_SHARDS)
    return buf.at[pl.ds(pl.multiple_of((2 * chip + core) * hr, 16), hr)]


def _gather_ici_plan(bufs):
    x, y, c, chips = _place()
    return [(_half_rows(b, 2 * x + y, c), _half_rows(b, 2 * x + y, c), (px, py, c)) for b in bufs for px, py in chips]


def _gather_d2d_plan(bufs):
    x, y, c, chips = _place()
    return [(_half_rows(b, 2 * px + py, c), _half_rows(b, 2 * px + py, c), (x, y, 1 - c)) for b in bufs for px, py in chips]


def _swap_plan(bufs):
    x, y, c, _ = _place()
    n = len(bufs) // 2
    copies = []
    for g, land in zip(bufs[:n], bufs[n:]):
        hr = g.shape[1] // 2
        copies.append((g.at[:, pl.ds(pl.multiple_of((1 - c) * hr, 8), hr)], land, (x, y, 1 - c)))
    return copies


def _exchange_plan(bufs):
    x, y, c, chips = _place()
    n = len(bufs) // 2
    return [(h.at[2 * px + py], land.at[k], (px, py, c))
            for h, land in zip(bufs[:n], bufs[n:]) for k, (px, py) in enumerate(chips)]


def _share_plan(bufs):
    x, y, c, _ = _place()
    copies = []
    for buf in bufs:
        hr = buf.shape[0] // 2
        mine = buf.at[pl.ds(pl.multiple_of(c * hr, 8), hr)]
        copies.append((mine, mine, (x, y, 1 - c)))
    return copies


N_DEVICES = 8


def _slot_place(vec, place):
    R, C = vec.shape

    def body(p_ref, v_ref, o_ref):
        o_ref[0] = v_ref[...]

    grid_spec = pltpu.PrefetchScalarGridSpec(
        num_scalar_prefetch=1, grid=(1,), in_specs=[pl.BlockSpec((R, C), lambda i, p: (0, 0))],
        out_specs=pl.BlockSpec((1, R, C), lambda i, p: (2 * p[0] + p[1], 0, 0)))
    return pl.pallas_call(
        body, name="slot_place", grid_spec=grid_spec, out_shape=jax.ShapeDtypeStruct((N_DEVICES, R, C), F32),
        compiler_params=_params(dimension_semantics=("arbitrary",)),
    )(place, vec)


def _slots_plan(bufs):
    x, y, c, _ = _place()
    mine = bufs[0].at[4 * x + 2 * y + c]
    return [(mine, mine, (x ^ (r >> 2), y ^ ((r >> 1) & 1), c ^ (r & 1))) for r in range(1, N_DEVICES)]


def _sum_slots(slots):
    _, R, C = slots.shape

    def body(s_ref, o_ref):
        acc = s_ref[0]
        for d in range(1, N_DEVICES):
            acc = acc + s_ref[d]
        o_ref[...] = acc

    return pl.pallas_call(
        body, name="sum_slots", in_specs=[pl.BlockSpec(memory_space=pltpu.VMEM)],
        out_specs=pl.BlockSpec(memory_space=pltpu.VMEM), out_shape=jax.ShapeDtypeStruct((R, C), F32),
        compiler_params=_params(),
    )(slots)


SMALL = ("ffn1_norm", "mix_norm", "pool_w", "pool_scale", "q_norm", "k_norm", "sinks", "gate_bias", "ffn2_norm")
SMALL_COLS = 1024
FFN1 = ("ffn1_w_gate", "ffn1_w_up", "ffn1_w_down")
MIXER = ("w_in", "w_pool_out", "w_attn_out", "w_out")
FFN2 = ("ffn2_w_gate", "ffn2_w_up", "ffn2_w_down")
LARGE = FFN1 + MIXER + FFN2
TRANSPOSED = ("ffn1_w_gate", "ffn1_w_up", "w_in", "w_pool_out", "ffn2_w_gate", "ffn2_w_up")
WEIGHTS = ("ffn1_norm", "ffn1_w_gate", "ffn1_w_up", "ffn1_w_down", "mix_norm", "w_in", "pool_w", "pool_scale",
           "w_pool_out", "q_norm", "k_norm", "sinks", "w_attn_out", "gate_bias", "w_out", "ffn2_norm",
           "ffn2_w_gate", "ffn2_w_up", "ffn2_w_down")


def _pack_small(parts):
    flat = jnp.concatenate([p.reshape(-1) for p in parts])
    rows = -(-flat.shape[0] // (8 * SMALL_COLS)) * 8
    return jnp.pad(flat, (0, rows * SMALL_COLS - flat.shape[0])).reshape(rows, SMALL_COLS)


def _unpack_small(packed, like):
    flat = packed.reshape(-1)
    out, off = [], 0
    for p in like:
        out.append(flat[off:off + p.size].reshape(p.shape))
        off += p.size
    return out, flat[off]


def _tie(small, token):
    return small + token[0, 0]


class Reduction:
    def __init__(self, group, names, grads, place):
        self.group, self.names, self.place = group, names, place
        self.bufs = [g.reshape(N_SHARDS, -1, g.shape[-1]) for g in grads]
        self.flight = None

    def _start(self, stage, bufs, plan, n_copies, after):
        self.flight = _split_start(f"{self.group}_{stage}", bufs, plan, n_copies, after)
        return self.flight.token

    def _landed(self, stage, after):
        n = len(self.names)
        bufs = _split_wait(f"{self.group}_{stage}_wait", self.flight, after)
        return bufs[:n], bufs[n:]

    def start_swap(self, after):
        lands = [lax.empty((g.shape[0], g.shape[1] // 2, g.shape[2]), g.dtype) for g in self.bufs]
        return self._start("swap", self.bufs + lands, _swap_plan, len(self.bufs), after)

    def start_exchange(self, after):
        g4, recv = self._landed("swap", after)
        halves = [_sum_halves(g, r, self.place, f"sum_halves_{n}") for n, g, r in zip(self.names, g4, recv)]
        lands = [lax.empty((N_SHARDS - 1,) + h.shape[1:], h.dtype) for h in halves]
        return self._start("exchange", halves + lands, _exchange_plan, (N_SHARDS - 1) * len(halves), self.place)

    def start_share(self, after):
        halves, recv3 = self._landed("exchange", after)
        reduced = [_sum_quarters(h, r, self.place, f"sum_quarters_{n}") for n, h, r in zip(self.names, halves, recv3)]
        return self._start("share", reduced, _share_plan, len(reduced), self.place)

    def finish(self, after):
        return dict(zip(self.names, _split_wait(f"{self.group}_share_wait", self.flight, after)))


def _row_form(name, a):
    return a.T if name in TRANSPOSED else a


def kernel(x, ffn1_norm, ffn1_w_gate, ffn1_w_up, ffn1_w_down, mix_norm, w_in, pool_w, pool_scale, w_pool_out, q_norm, k_norm, sinks, w_attn_out, gate_bias, w_out, ffn2_norm, ffn2_w_gate, ffn2_w_up, ffn2_w_down, loss_target, m_ffn1_norm, m_ffn1_w_gate, m_ffn1_w_up, m_ffn1_w_down, m_mix_norm, m_w_in, m_pool_w, m_pool_scale, m_w_pool_out, m_q_norm, m_k_norm, m_sinks, m_w_attn_out, m_gate_bias, m_w_out, m_ffn2_norm, m_ffn2_w_gate, m_ffn2_w_up, m_ffn2_w_down, v_ffn1_norm, v_ffn1_w_gate, v_ffn1_w_up, v_ffn1_w_down, v_mix_norm, v_w_in, v_pool_w, v_pool_scale, v_w_pool_out, v_q_norm, v_k_norm, v_sinks, v_w_attn_out, v_gate_bias, v_w_out, v_ffn2_norm, v_ffn2_w_gate, v_ffn2_w_up, v_ffn2_w_down):
    args = dict(locals())
    wts = {n: _row_form(n, args[n]) for n in WEIGHTS}
    mom = {n: _row_form(n, args["m_" + n]) for n in WEIGHTS}
    var = {n: _row_form(n, args["v_" + n]) for n in WEIGHTS}
    shard = 2 * lax.axis_index("x") + lax.axis_index("y")
    place = jnp.stack([shard, lax.axis_index("c")]).astype(jnp.int32)

    xs, target = x[0], loss_target[0]
    D = xs.shape[1]
    g1 = wts["ffn1_norm"].reshape(1, D)
    gm = wts["mix_norm"].reshape(1, D)
    g2 = wts["ffn2_norm"].reshape(1, D)
    qw = wts["q_norm"].reshape(1, HEAD_DIM)
    kw = wts["k_norm"].reshape(1, HEAD_DIM)
    bias = wts["gate_bias"].reshape(1, 2 * D)
    pscale = wts["pool_scale"].reshape(1, -1)
    pw = wts["pool_w"].astype(BF16)
    sink_rows = jnp.broadcast_to(jnp.repeat(wts["sinks"], ATTN_BLOCK).reshape(N_KV_HEADS, GQA_GROUP * ATTN_BLOCK, 1),
                                 (N_KV_HEADS, GQA_GROUP * ATTN_BLOCK, 128))

    first, down1 = FFN1[:2], FFN1[2:]
    placed = {n: _cast_place(wts[n], place, "cast_" + n) for n in FFN1}
    first_ici = _split_start("gather_first_ici", [placed[n] for n in first], _gather_ici_plan, 3 * len(first), place)
    placed.update({n: _cast_place(wts[n], place, "cast_" + n) for n in MIXER + FFN2})
    zero = jnp.zeros((1,), F32)
    packed = [_pack_small([d[n] for n in SMALL] + [zero]) for d in (wts, mom, var)]
    shadow = [placed[n] for n in down1 + MIXER + FFN2] + packed + [sink_rows, pw]
    first_d2d = _split_start("gather_first_d2d", first_ici.bufs, _gather_d2d_plan, 3 * len(first), shadow, waits=first_ici)
    down1_ici = _split_start("gather_down1_ici", [placed[n] for n in down1], _gather_ici_plan, 3 * len(down1), first_d2d.token)
    w = dict(zip(first, _split_wait("gather_first_wait", first_d2d, down1_ici.token)))
    mix_ici = _split_start("gather_mix_ici", [placed[n] for n in MIXER], _gather_ici_plan, 3 * len(MIXER), w[first[0]])
    ffn2_ici = _split_start("gather_ffn2_ici", [placed[n] for n in FFN2], _gather_ici_plan, 3 * len(FFN2), mix_ici.token)

    n1, a1, b1, s1 = _ffn_up(xs, _tie(g1, ffn2_ici.token), w["ffn1_w_gate"], w["ffn1_w_up"], "ffn1_up")
    down1_d2d = _split_start("gather_down1_d2d", down1_ici.bufs, _gather_d2d_plan, 3 * len(down1), s1, waits=down1_ici)
    w.update(zip(down1, _split_wait("gather_down1_wait", down1_d2d, down1_d2d.token)))
    h1 = _ffn_down(xs, s1, w["ffn1_w_down"], "ffn1_down")
    mix_d2d = _split_start("gather_mix_d2d", mix_ici.bufs, _gather_d2d_plan, 3 * len(MIXER), h1, waits=mix_ici)
    w.update(zip(MIXER, _split_wait("gather_mix_wait", mix_d2d, mix_d2d.token)))
    u, xp, q, kv, gl = _mix_proj(h1, gm, w["w_in"])
    attn = _attn_fwd(q, kv, qw, kw, sink_rows)
    ffn2_d2d = _split_start("gather_ffn2_d2d", ffn2_ici.bufs, _gather_d2d_plan, 3 * len(FFN2), attn, waits=ffn2_ici)
    h2, bp, ba, ms = _mix_out(xp, attn, gl, _tie(bias, ffn2_d2d.token), pw, pscale, w["w_pool_out"], w["w_attn_out"], w["w_out"], h1)
    w.update(zip(FFN2, _split_wait("gather_ffn2_wait", ffn2_d2d, h2)))
    n2, a2, b2, s2 = _ffn_up(h2, g2, w["ffn2_w_gate"], w["ffn2_w_up"], "ffn2_up")
    dy, dyh, loss = _ffn_down_loss(h2, s2, w["ffn2_w_down"], target, "ffn2_down_loss")

    gw, gs = {}, {}
    da, db = _ffn_bwd_h(dyh, a2, b2, w["ffn2_w_down"], ffn2_d2d.token, "ffn2_bwd_h")
    gw["ffn2_w_gate"], gw["ffn2_w_up"] = _xty(da, n2, "ffn2_dw_gate"), _xty(db, n2, "ffn2_dw_up")
    gw["ffn2_w_down"] = _xty(s2, dyh, "ffn2_dw_down")
    red2 = Reduction("reduce_ffn2", FFN2, [gw[n] for n in FFN2], place)
    token = red2.start_swap(da)
    dh2, gs["ffn2_norm"] = _ffn_bwd_x(da, db, w["ffn2_w_gate"], w["ffn2_w_up"], h2, dy, _tie(g2, token), "ffn2_bwd_x")
    token = red2.start_exchange(dh2)
    dgl, dbp, dba, merged, dh2b, gs["gate_bias"] = _mix_bwd_gate(dh2, w["w_out"], bp, ba, gl, _tie(bias, token))
    dattn, dpooled, gs["pool_w"], gs["pool_scale"] = _mix_bwd_branch(dbp, dba, xp, pw, pscale, w["w_pool_out"], w["w_attn_out"])
    gw["w_out"] = _xty(merged, dh2b, "dw_out", tf=D)
    gw["w_attn_out"] = _xty(attn, dba, "dw_attn_out", tf=D)
    gw["w_pool_out"] = _xty(dbp, ms, "dw_pool_out", tf=D)
    dq, dkv, gs["q_norm"], gs["k_norm"], dsk = _attn_bwd(q, kv, dattn, qw, kw, sink_rows)
    gs["sinks"] = dsk[:, :, 0]
    red2.start_share(dq)
    pieces = (_pool_bwd(dpooled), dq, dkv, dgl)
    gw["w_in"] = _in_bwd_w(pieces, u)
    grads = red2.finish(gw["w_in"])
    redm = Reduction("reduce_mix", MIXER, [gw[n] for n in MIXER], place)
    token = redm.start_swap(grads[FFN2[0]])
    dh1, dh1h, gs["mix_norm"] = _in_bwd_x(pieces, w["w_in"], h1, dh2, _tie(gm, token))
    token = redm.start_exchange(dh1)
    da, db = _ffn_bwd_h(dh1h, a1, b1, w["ffn1_w_down"], token, "ffn1_bwd_h")
    gw["ffn1_w_gate"], gw["ffn1_w_up"] = _xty(da, n1, "ffn1_dw_gate"), _xty(db, n1, "ffn1_dw_up")
    gw["ffn1_w_down"] = _xty(s1, dh1h, "ffn1_dw_down")
    token = redm.start_share([gw[n] for n in FFN1])
    red1 = Reduction("reduce_ffn1", FFN1, [gw[n] for n in FFN1], place)
    token = red1.start_swap(token)
    grad_x, gs["ffn1_norm"] = _ffn_bwd_x(da, db, w["ffn1_w_gate"], w["ffn1_w_up"], xs, dh1, _tie(g1, token), "ffn1_bwd_x")
    grads.update(redm.finish(grad_x))

    small_parts = [gs[n] for n in SMALL] + [loss[0, 0].reshape(1)]
    slots = _split_start("gather_small", [_slot_place(_pack_small(small_parts), place)], _slots_plan, N_DEVICES - 1,
                         grads[MIXER[0]])
    token = red1.start_exchange(slots.token)
    delta, new_m, new_v = {}, {}, {}
    for n in FFN2 + MIXER:
        delta[n], new_m[n], new_v[n], grads[n] = _adamw(wts[n], grads[n], mom[n], var[n], token, "adamw_" + n)
    summed = _sum_slots(_split_wait("gather_small_wait", slots, delta[MIXER[-1]])[0])
    small_grads, loss_sum = _unpack_small(summed, [wts[n] for n in SMALL])
    grads.update(dict(zip(SMALL, small_grads)))
    ds, ms, vs, _ = _adamw(packed[0], summed, packed[1], packed[2], token, "adamw_small")
    like = [wts[n] for n in SMALL]
    for out, packed_out in ((delta, ds), (new_m, ms), (new_v, vs)):
        out.update(dict(zip(SMALL, _unpack_small(packed_out, like)[0])))
    token = red1.start_share(ds)
    grads.update(red1.finish(token))
    for n in FFN1:
        delta[n], new_m[n], new_v[n], grads[n] = _adamw(wts[n], grads[n], mom[n], var[n], token, "adamw_" + n)

    outs = []
    for d in (grads, delta, new_m, new_v):
        outs += [_row_form(n, d[n]) for n in WEIGHTS]
    return (loss_sum, grad_x[None], *outs)
```

```python
from typing import Callable, NamedTuple

import jax
import jax.numpy as jnp
from jax import lax
from jax.experimental import pallas as pl
from jax.experimental.pallas import tpu as pltpu

F32 = jnp.float32
BF16 = jnp.bfloat16
RMS_EPS = 1e-6
POOL_WINDOWS = (2, 4, 8, 16)
POOL_GROUP = 128
POOL_HALO = 16
HEAD_DIM = 64
GQA_GROUP = 8
N_KV_HEADS = 2
ATTN_BLOCK = 128
SCALE = HEAD_DIM ** -0.5
NEG = -1e30
N_SHARDS = 4
ADAM_LR, ADAM_B1, ADAM_B2, ADAM_EPS, ADAM_WD, ADAM_STEP = 0.001, 0.9, 0.999, 1e-08, 0.01, 10
VMEM_LIMIT = 56 * 1024 * 1024
MESH = pl.DeviceIdType.MESH
SEG_POOL, SEG_Q, SEG_KV, SEG_GATE = (0, 512), (512, 1024), (1536, 256), (1792, 2048)
SEGMENTS = (SEG_POOL, SEG_Q, SEG_KV, SEG_GATE)


def _params(**kw):
    return pltpu.CompilerParams(vmem_limit_bytes=VMEM_LIMIT, **kw)


def _dot(a, b):
    return jnp.dot(a, b, preferred_element_type=F32)


def _dot_nt(a, b):
    return lax.dot_general(a, b, (((1,), (1,)), ((), ())), preferred_element_type=F32)


def _dot_tn(a, b):
    return lax.dot_general(a, b, (((0,), (0,)), ((), ())), preferred_element_type=F32)


def _rinv(x):
    return lax.rsqrt(jnp.mean(x * x, axis=-1, keepdims=True) + RMS_EPS)


def _rms_bwd(dn, x, g):
    r = _rinv(x)
    xr = x * r
    z = dn * g
    dx = r * (z - xr * jnp.mean(z * xr, axis=-1, keepdims=True))
    return dx, dn * xr


def _acc(ref, val, first):
    @pl.when(first)
    def _():
        ref[...] = val

    @pl.when(jnp.logical_not(first))
    def _():
        ref[...] += val


TOKEN_SPEC = pl.BlockSpec(memory_space=pl.ANY)
F_HALF = 1408


def _resident(w):
    return pl.BlockSpec(w.shape, lambda i: (0, 0), pipeline_mode=pl.Buffered(1))


def _ffn_up(h, gnorm, wgt, wut, name, tm=512):
    S, D = h.shape
    F = wgt.shape[0]

    def body(h_ref, g_ref, wg_ref, wu_ref, n_ref, a_ref, b_ref, s_ref):
        x = h_ref[...]
        n = (x * _rinv(x) * g_ref[...]).astype(BF16)
        n_ref[...] = n
        for lo in range(0, F, F_HALF):
            cols = slice(lo, lo + F_HALF)
            a = _dot_nt(n, wg_ref[cols, :])
            b = _dot_nt(n, wu_ref[cols, :])
            a_ref[:, cols] = a.astype(BF16)
            b_ref[:, cols] = b.astype(BF16)
            s_ref[:, cols] = (a * jax.nn.sigmoid(a) * b).astype(BF16)

    act = pl.BlockSpec((tm, F), lambda i: (i, 0))
    hidden = jax.ShapeDtypeStruct((S, F), BF16)
    return pl.pallas_call(
        body, name=name, grid=(S // tm,),
        in_specs=[pl.BlockSpec((tm, D), lambda i: (i, 0)), pl.BlockSpec((1, D), lambda i: (0, 0)), _resident(wgt), _resident(wut)],
        out_specs=[pl.BlockSpec((tm, D), lambda i: (i, 0)), act, act, act],
        out_shape=[jax.ShapeDtypeStruct((S, D), BF16), hidden, hidden, hidden],
        compiler_params=_params(dimension_semantics=("arbitrary",)),
    )(h, gnorm, wgt, wut)


def _ffn_down(h, s, wd, name, tm=1024):
    S, D = h.shape
    F = wd.shape[0]

    def body(h_ref, s_ref, wd_ref, o_ref):
        o_ref[...] = h_ref[...] + 0.5 * _dot(s_ref[...], wd_ref[...])

    row = pl.BlockSpec((tm, D), lambda i: (i, 0))
    return pl.pallas_call(
        body, name=name, grid=(S // tm,), in_specs=[row, pl.BlockSpec((tm, F), lambda i: (i, 0)), _resident(wd)],
        out_specs=row, out_shape=jax.ShapeDtypeStruct((S, D), F32),
        compiler_params=_params(dimension_semantics=("arbitrary",)),
    )(h, s, wd)


def _ffn_down_loss(h, s, wd, target, name, tm=512):
    S, D = h.shape
    F = wd.shape[0]

    def body(h_ref, s_ref, wd_ref, t_ref, dy_ref, dyh_ref, loss_ref):
        e = h_ref[...] + 0.5 * _dot(s_ref[...], wd_ref[...]) - t_ref[...]
        dy = e * (1.0 / D)
        dy_ref[...] = dy
        dyh_ref[...] = (0.5 * dy).astype(BF16)
        tot = jnp.sum(jnp.sum(e * e, axis=1, keepdims=True), axis=0, keepdims=True) * (0.5 / D)
        _acc(loss_ref, jnp.broadcast_to(tot, loss_ref.shape), pl.program_id(0) == 0)

    row = pl.BlockSpec((tm, D), lambda i: (i, 0))
    return pl.pallas_call(
        body, name=name, grid=(S // tm,), in_specs=[row, pl.BlockSpec((tm, F), lambda i: (i, 0)), _resident(wd), row],
        out_specs=[row, row, pl.BlockSpec((8, 128), lambda i: (0, 0))],
        out_shape=[jax.ShapeDtypeStruct((S, D), F32), jax.ShapeDtypeStruct((S, D), BF16), jax.ShapeDtypeStruct((8, 128), F32)],
        compiler_params=_params(dimension_semantics=("arbitrary",)),
    )(h, s, wd, target)


def _ffn_bwd_h(dyh, a, b, wd, token, name, tm=512):
    S, D = dyh.shape
    F = wd.shape[0]

    def body(dyh_ref, a_ref, b_ref, wd_ref, _, da_ref, db_ref):
        dyh_t = dyh_ref[...]
        for lo in range(0, F, F_HALF):
            cols = slice(lo, lo + F_HALF)
            av = a_ref[:, cols].astype(F32)
            bv = b_ref[:, cols].astype(F32)
            ds = _dot_nt(dyh_t, wd_ref[cols, :])
            sig = jax.nn.sigmoid(av)
            silu = av * sig
            da_ref[:, cols] = (ds * bv * (sig * (1.0 + av * (1.0 - sig)))).astype(BF16)
            db_ref[:, cols] = (ds * silu).astype(BF16)

    act = pl.BlockSpec((tm, F), lambda i: (i, 0))
    hidden = jax.ShapeDtypeStruct((S, F), BF16)
    return pl.pallas_call(
        body, name=name, grid=(S // tm,),
        in_specs=[pl.BlockSpec((tm, D), lambda i: (i, 0)), act, act, _resident(wd), TOKEN_SPEC],
        out_specs=[act, act], out_shape=[hidden, hidden],
        compiler_params=_params(dimension_semantics=("arbitrary",)),
    )(dyh, a, b, wd, token)


def _xty(x, y, name, tk=2048, tf=1408):
    S, F = x.shape
    D = y.shape[1]

    def body(x_ref, y_ref, o_ref):
        _acc(o_ref, _dot_tn(x_ref[...], y_ref[...]), pl.program_id(1) == 0)

    return pl.pallas_call(
        body, name=name, grid=(F // tf, S // tk),
        in_specs=[pl.BlockSpec((tk, tf), lambda j, k: (k, j)), pl.BlockSpec((tk, D), lambda j, k: (k, 0))],
        out_specs=pl.BlockSpec((tf, D), lambda j, k: (j, 0)), out_shape=jax.ShapeDtypeStruct((F, D), F32),
        compiler_params=_params(dimension_semantics=("arbitrary", "arbitrary")),
    )(x, y)


def _ffn_bwd_x(da, db, wgt, wut, x, dh, gnorm, name, tm=512):
    S, D = x.shape
    F = wgt.shape[0]

    def body(da_ref, db_ref, wg_ref, wu_ref, x_ref, dh_ref, g_ref, dx_ref, dg_ref):
        dn = _dot(da_ref[...], wg_ref[...]) + _dot(db_ref[...], wu_ref[...])
        dx, dg_rows = _rms_bwd(dn, x_ref[...], g_ref[...])
        dx_ref[...] = dh_ref[...] + dx
        _acc(dg_ref, jnp.sum(dg_rows, axis=0, keepdims=True), pl.program_id(0) == 0)

    row = pl.BlockSpec((tm, D), lambda i: (i, 0))
    act = pl.BlockSpec((tm, F), lambda i: (i, 0))
    vec = pl.BlockSpec((1, D), lambda i: (0, 0))
    return pl.pallas_call(
        body, name=name, grid=(S // tm,), in_specs=[act, act, _resident(wgt), _resident(wut), row, row, vec],
        out_specs=[row, vec],
        out_shape=[jax.ShapeDtypeStruct((S, D), F32), jax.ShapeDtypeStruct((1, D), F32)],
        compiler_params=_params(dimension_semantics=("arbitrary",)),
    )(da, db, wgt, wut, x, dh, gnorm)


def _mix_proj(h, gnorm, wint, tm=512):
    S, D = h.shape
    nt = S // tm

    def body(h_ref, g_ref, w_ref, u_ref, xp_ref, q_ref, kv_ref, gl_ref):
        x = h_ref[...]
        u = (x * _rinv(x) * g_ref[...]).astype(BF16)
        u_ref[...] = u
        for (off, width), ref in zip(SEGMENTS, (xp_ref, q_ref, kv_ref, gl_ref)):
            ref[...] = _dot_nt(u, w_ref[off:off + width, :]).astype(ref.dtype)

    def row(width):
        return pl.BlockSpec((tm, width), lambda i: (i, 0))

    return pl.pallas_call(
        body, name="mix_proj", grid=(nt,),
        in_specs=[row(D), pl.BlockSpec((1, D), lambda i: (0, 0)), pl.BlockSpec(wint.shape, lambda i: (0, 0))],
        out_specs=[row(D), row(SEG_POOL[1]), row(SEG_Q[1]), row(SEG_KV[1]), row(SEG_GATE[1])],
        out_shape=[jax.ShapeDtypeStruct((S, D), BF16), jax.ShapeDtypeStruct((S, SEG_POOL[1]), F32),
                   jax.ShapeDtypeStruct((S, SEG_Q[1]), BF16), jax.ShapeDtypeStruct((S, SEG_KV[1]), BF16),
                   jax.ShapeDtypeStruct((S, SEG_GATE[1]), BF16)],
        compiler_params=_params(dimension_semantics=("arbitrary",)),
    )(h, gnorm, wint)


def _stack_heads(x, g):
    return jnp.concatenate([x[:, (GQA_GROUP * g + hh) * HEAD_DIM:(GQA_GROUP * g + hh + 1) * HEAD_DIM]
                            for hh in range(GQA_GROUP)], axis=0)


def _unstack_heads(ref, val, g):
    for hh in range(GQA_GROUP):
        lo = (GQA_GROUP * g + hh) * HEAD_DIM
        ref[:, lo:lo + HEAD_DIM] = val[hh * ATTN_BLOCK:(hh + 1) * ATTN_BLOCK, :]


def _rowsum(xb, width):
    return _dot(xb, jnp.ones((xb.shape[1], width), BF16))


def _rinv_lanes(x):
    return lax.rsqrt(_rowsum((x * x).astype(BF16), x.shape[1]) * (1.0 / x.shape[1]) + RMS_EPS)


def _attn_exp(qn, kc, kp, sink, n):
    rows = GQA_GROUP * ATTN_BLOCK
    qi = lax.broadcasted_iota(jnp.int32, (rows, ATTN_BLOCK), 0) % ATTN_BLOCK
    kj = lax.broadcasted_iota(jnp.int32, (rows, ATTN_BLOCK), 1)
    s_c = jnp.where(kj <= qi, _dot_nt(qn, kc), NEG)
    s_p = jnp.where(jnp.logical_and(kj > qi, n > 0), _dot_nt(qn, kp), NEG)
    m = jnp.maximum(jnp.max(s_c, axis=-1, keepdims=True), jnp.max(s_p, axis=-1, keepdims=True))
    m = jnp.maximum(jnp.broadcast_to(m, sink.shape), sink)
    e_c = jnp.exp(s_c - m)
    e_p = jnp.exp(s_p - m)
    e_s = jnp.exp(sink - m)
    e_cb, e_pb = e_c.astype(BF16), e_p.astype(BF16)
    inv = 1.0 / (_rowsum(jnp.concatenate([e_cb, e_pb], axis=1), ATTN_BLOCK) + e_s)
    return e_cb, e_pb, e_c, e_p, e_s, inv


def _attn_blocks(n):
    cur = pl.multiple_of(n * ATTN_BLOCK, ATTN_BLOCK)
    prev = pl.multiple_of(jnp.maximum(n - 1, 0) * ATTN_BLOCK, ATTN_BLOCK)
    return cur, prev


def _kv_split(kv, g):
    k = kv[:, g * HEAD_DIM:(g + 1) * HEAD_DIM]
    v = kv[:, (N_KV_HEADS + g) * HEAD_DIM:(N_KV_HEADS + g + 1) * HEAD_DIM]
    return k, v


def _attn_fwd(q, kv, qw, kw, sink_rows):
    S, W = q.shape
    nb = S // ATTN_BLOCK

    def body(q_ref, kv_ref, qw_ref, kw_ref, sk_ref, o_ref, o_scr):
        n = pl.program_id(0)
        qf = q_ref[...].astype(F32)
        cur, prev = _attn_blocks(n)
        kvc = kv_ref[pl.ds(cur, ATTN_BLOCK), :].astype(F32)
        kvp = kv_ref[pl.ds(prev, ATTN_BLOCK), :].astype(F32)
        for g in range(N_KV_HEADS):
            qs = _stack_heads(qf, g)
            qn = (qs * _rinv_lanes(qs) * qw_ref[...] * SCALE).astype(BF16)
            kc, vc = _kv_split(kvc, g)
            kp, vp = _kv_split(kvp, g)
            kc = (kc * _rinv_lanes(kc) * kw_ref[...]).astype(BF16)
            kp = (kp * _rinv_lanes(kp) * kw_ref[...]).astype(BF16)
            e_cb, e_pb, _, _, _, inv = _attn_exp(qn, kc, kp, sk_ref[g], n)
            o = (_dot(e_cb, vc.astype(BF16)) + _dot(e_pb, vp.astype(BF16))) * inv[:, :HEAD_DIM]
            _unstack_heads(o_scr, o, g)
        o_ref[...] = o_scr[...].astype(BF16)

    blk = pl.BlockSpec((ATTN_BLOCK, W), lambda n: (n, 0))
    return pl.pallas_call(
        body, name="attn_fwd", grid=(nb,),
        in_specs=[blk, pl.BlockSpec(kv.shape, lambda n: (0, 0)), pl.BlockSpec((1, HEAD_DIM), lambda n: (0, 0)),
                  pl.BlockSpec((1, HEAD_DIM), lambda n: (0, 0)), pl.BlockSpec(sink_rows.shape, lambda n: (0, 0, 0))],
        out_specs=blk, out_shape=jax.ShapeDtypeStruct((S, W), BF16),
        scratch_shapes=[pltpu.VMEM((ATTN_BLOCK, W), F32)],
        compiler_params=_params(dimension_semantics=("arbitrary",)),
    )(q, kv, qw, kw, sink_rows)


def _attn_bwd(q, kv, do, qw, kw, sink_rows):
    S, W = q.shape
    KW = kv.shape[1]
    nb = S // ATTN_BLOCK
    chunk = 512

    def body(q_ref, kv_ref, do_ref, qw_ref, kw_ref, sk_ref, dq_ref, dkv_ref, dqw_ref, dkw_ref, dsk_ref, dq_scr):
        n = pl.program_id(0)

        @pl.when(n == 0)
        def _():
            dkv_ref[...] = jnp.zeros_like(dkv_ref)
            dqw_ref[...] = jnp.zeros_like(dqw_ref)
            dsk_ref[...] = jnp.zeros_like(dsk_ref)

        qf = q_ref[...].astype(F32)
        dof = do_ref[...].astype(F32)
        cur, prev = _attn_blocks(n)
        kvc = kv_ref[pl.ds(cur, ATTN_BLOCK), :].astype(F32)
        kvp = kv_ref[pl.ds(prev, ATTN_BLOCK), :].astype(F32)
        qw_v = qw_ref[...]
        for g in range(N_KV_HEADS):
            qs = _stack_heads(qf, g)
            rq = _rinv_lanes(qs)
            qhat = qs * rq
            qn = (qhat * qw_v * SCALE).astype(BF16)
            kc, vc = _kv_split(kvc, g)
            kp, vp = _kv_split(kvp, g)
            kc = (kc * _rinv_lanes(kc) * kw_ref[...]).astype(BF16)
            kp = (kp * _rinv_lanes(kp) * kw_ref[...]).astype(BF16)
            vc = vc.astype(BF16)
            vp = vp.astype(BF16)
            dos = _stack_heads(dof, g).astype(BF16)
            _, _, e_c, e_p, e_s, inv = _attn_exp(qn, kc, kp, sk_ref[g], n)
            p_c, p_p, p_s = e_c * inv, e_p * inv, e_s * inv
            dp_c = _dot_nt(dos, vc)
            dp_p = _dot_nt(dos, vp)
            drow = _rowsum(jnp.concatenate([(p_c * dp_c).astype(BF16), (p_p * dp_p).astype(BF16)], axis=1), ATTN_BLOCK)
            ds_c = (p_c * (dp_c - drow)).astype(BF16)
            ds_p = (p_p * (dp_p - drow)).astype(BF16)
            dsink = -(p_s * drow)
            for hh in range(GQA_GROUP):
                tot = jnp.sum(dsink[hh * ATTN_BLOCK:(hh + 1) * ATTN_BLOCK, :], axis=0, keepdims=True)
                dsk_ref[g, hh:hh + 1, :] += tot
            dqn = (_dot(ds_c, kc) + _dot(ds_p, kp)) * SCALE
            klo, vlo = g * HEAD_DIM, (N_KV_HEADS + g) * HEAD_DIM
            dkv_ref[pl.ds(cur, ATTN_BLOCK), klo:klo + HEAD_DIM] += _dot_tn(ds_c, qn)
            dkv_ref[pl.ds(prev, ATTN_BLOCK), klo:klo + HEAD_DIM] += _dot_tn(ds_p, qn)
            dkv_ref[pl.ds(cur, ATTN_BLOCK), vlo:vlo + HEAD_DIM] += _dot_tn(p_c.astype(BF16), dos)
            dkv_ref[pl.ds(prev, ATTN_BLOCK), vlo:vlo + HEAD_DIM] += _dot_tn(p_p.astype(BF16), dos)
            dqw_ref[...] += jnp.sum(dqn * qhat, axis=0, keepdims=True)
            z = dqn * qw_v
            dqs = rq * (z - qhat * (_rowsum((z * qhat).astype(BF16), HEAD_DIM) * (1.0 / HEAD_DIM)))
            _unstack_heads(dq_scr, dqs, g)
        dq_ref[...] = dq_scr[...].astype(BF16)

        @pl.when(n == nb - 1)
        def _():
            def one(c, dkw):
                rows = pl.ds(pl.multiple_of(c * chunk, chunk), chunk)
                for g in range(N_KV_HEADS):
                    lo = g * HEAD_DIM
                    k = kv_ref[rows, lo:lo + HEAD_DIM].astype(F32)
                    dx, dg_rows = _rms_bwd(dkv_ref[rows, lo:lo + HEAD_DIM], k, kw_ref[...])
                    dkv_ref[rows, lo:lo + HEAD_DIM] = dx
                    dkw = dkw + jnp.sum(dg_rows, axis=0, keepdims=True)
                return dkw

            dkw_ref[...] = lax.fori_loop(0, S // chunk, one, jnp.zeros((1, HEAD_DIM), F32))

    blk = pl.BlockSpec((ATTN_BLOCK, W), lambda n: (n, 0))
    whole_kv = pl.BlockSpec((S, KW), lambda n: (0, 0))
    vec = pl.BlockSpec((1, HEAD_DIM), lambda n: (0, 0))
    sk = pl.BlockSpec(sink_rows.shape, lambda n: (0, 0, 0))
    dsk = pl.BlockSpec((N_KV_HEADS, GQA_GROUP, 128), lambda n: (0, 0, 0))
    return pl.pallas_call(
        body, name="attn_bwd", grid=(nb,), in_specs=[blk, whole_kv, blk, vec, vec, sk],
        out_specs=[blk, whole_kv, vec, vec, dsk],
        out_shape=[jax.ShapeDtypeStruct((S, W), BF16), jax.ShapeDtypeStruct((S, KW), F32),
                   jax.ShapeDtypeStruct((1, HEAD_DIM), F32), jax.ShapeDtypeStruct((1, HEAD_DIM), F32),
                   jax.ShapeDtypeStruct((N_KV_HEADS, GQA_GROUP, 128), F32)],
        scratch_shapes=[pltpu.VMEM((ATTN_BLOCK, W), F32)],
        compiler_params=_params(dimension_semantics=("arbitrary",)),
    )(q, kv, do, qw, kw, sink_rows)


def _pooled(xc, xprev, i):
    tm = xc.shape[0]
    xh = jnp.concatenate([jnp.where(i > 0, xprev, 0.0), xc], axis=0)
    t = lax.broadcasted_iota(jnp.int32, (tm, 1), 0) + i * tm
    out = []
    for gi, w in enumerate(POOL_WINDOWS):
        acc = xh[:, gi * POOL_GROUP:(gi + 1) * POOL_GROUP]
        sh = 1
        while sh < w:
            acc = acc + pltpu.roll(acc, sh, 0)
            sh *= 2
        cnt = jnp.minimum(t + 1, w).astype(F32)
        out.append(acc[POOL_HALO:, :] / cnt - xc[:, gi * POOL_GROUP:(gi + 1) * POOL_GROUP])
    return jnp.concatenate(out, axis=1)


def _pool_mix(pooled_b, pw_ref):
    return jnp.concatenate([_dot(pooled_b[:, gi * POOL_GROUP:(gi + 1) * POOL_GROUP], pw_ref[gi])
                            for gi in range(len(POOL_WINDOWS))], axis=1)


def _halo_specs(tm, width, S, after):
    per = tm // POOL_HALO
    last = S // POOL_HALO - 1
    if after:
        return pl.BlockSpec((POOL_HALO, width), lambda i: (jnp.minimum((i + 1) * per, last), 0))
    return pl.BlockSpec((POOL_HALO, width), lambda i: (jnp.maximum(i * per - 1, 0), 0))


def _mix_out(xp, attn, gl, bias, pw, pscale, wpot, wao, wo, h, tm=512):
    S, D = h.shape
    nt = S // tm
    PW = xp.shape[1]

    def body(xc_ref, xprev_ref, at_ref, gl_ref, bias_ref, pw_ref, ps_ref, wpo_ref, wao_ref, wo_ref, h_ref,
             ho_ref, bp_ref, ba_ref):
        i = pl.program_id(0)
        pooled = _pooled(xc_ref[...], xprev_ref[...], i).astype(BF16)
        ms = (_pool_mix(pooled, pw_ref) * ps_ref[...]).astype(BF16)
        bp = _dot_nt(ms, wpo_ref[...])
        ba = _dot(at_ref[...], wao_ref[...])
        bp_ref[...] = bp.astype(BF16)
        ba_ref[...] = ba.astype(BF16)
        gates = jax.nn.sigmoid(gl_ref[...].astype(F32) + bias_ref[...])
        merged = (gates[:, :D] * bp + gates[:, D:] * ba).astype(BF16)
        ho_ref[...] = h_ref[...] + _dot(merged, wo_ref[...])

    def row(width):
        return pl.BlockSpec((tm, width), lambda i: (i, 0))

    def whole(x):
        nd = x.ndim
        return pl.BlockSpec(x.shape, lambda i: (0,) * nd)

    return pl.pallas_call(
        body, name="mix_out", grid=(nt,),
        in_specs=[row(PW), _halo_specs(tm, PW, S, False), row(D), row(2 * D), whole(bias), whole(pw), whole(pscale),
                  whole(wpot), whole(wao), whole(wo), row(D)],
        out_specs=[row(D), row(D), row(D)],
        out_shape=[jax.ShapeDtypeStruct((S, D), F32), jax.ShapeDtypeStruct((S, D), BF16),
                   jax.ShapeDtypeStruct((S, D), BF16)],
        compiler_params=_params(dimension_semantics=("arbitrary",)),
    )(xp, xp, attn, gl, bias, pw, pscale, wpot, wao, wo, h)


def _mix_bwd_gate(dh, wo, bp, ba, gl, bias, tm=512):
    S, D = dh.shape
    nt = S // tm

    def body(dh_ref, wo_ref, bp_ref, ba_ref, gl_ref, bias_ref, dgl_ref, dbp_ref, dba_ref, dwo_ref, dbias_ref):
        i = pl.program_id(0)
        dhb = dh_ref[...].astype(BF16)
        dm = _dot_nt(dhb, wo_ref[...])
        gates = jax.nn.sigmoid(gl_ref[...].astype(F32) + bias_ref[...])
        gp, ga = gates[:, :D], gates[:, D:]
        bp_v = bp_ref[...].astype(F32)
        ba_v = ba_ref[...].astype(F32)
        merged = (gp * bp_v + ga * ba_v).astype(BF16)
        _acc(dwo_ref, _dot_tn(merged, dhb), i == 0)
        dbp_ref[...] = (dm * gp).astype(BF16)
        dba_ref[...] = (dm * ga).astype(BF16)
        dgl = jnp.concatenate([dm * bp_v * gp * (1.0 - gp), dm * ba_v * ga * (1.0 - ga)], axis=1)
        dgl_ref[...] = dgl.astype(BF16)
        _acc(dbias_ref, jnp.sum(dgl, axis=0, keepdims=True), i == 0)

    def row(width):
        return pl.BlockSpec((tm, width), lambda i: (i, 0))

    def whole(shape):
        return pl.BlockSpec(shape, lambda i: (0, 0))

    return pl.pallas_call(
        body, name="mix_bwd_gate", grid=(nt,),
        in_specs=[row(D), whole(wo.shape), row(D), row(D), row(2 * D), whole(bias.shape)],
        out_specs=[row(2 * D), row(D), row(D), whole((D, D)), whole((1, 2 * D))],
        out_shape=[jax.ShapeDtypeStruct((S, 2 * D), BF16), jax.ShapeDtypeStruct((S, D), BF16),
                   jax.ShapeDtypeStruct((S, D), BF16), jax.ShapeDtypeStruct((D, D), F32),
                   jax.ShapeDtypeStruct((1, 2 * D), F32)],
        compiler_params=_params(dimension_semantics=("arbitrary",)),
    )(dh, wo, bp, ba, gl, bias)


def _mix_bwd_branch(dbp, dba, attn, xp, pw, pscale, wpot, wao, tm=512):
    S, D = dbp.shape
    nt = S // tm
    PW = xp.shape[1]
    NG = len(POOL_WINDOWS)

    def body(dbp_ref, dba_ref, at_ref, xc_ref, xprev_ref, pw_ref, ps_ref, wpo_ref, wao_ref,
             dat_ref, dpl_ref, dwao_ref, dwpo_ref, dpw_ref, dps_ref):
        i = pl.program_id(0)
        dba_v = dba_ref[...]
        dbp_v = dbp_ref[...]
        _acc(dwao_ref, _dot_tn(at_ref[...], dba_v), i == 0)
        dat_ref[...] = _dot_nt(dba_v, wao_ref[...]).astype(BF16)
        pooled = _pooled(xc_ref[...], xprev_ref[...], i).astype(BF16)
        mixed = _pool_mix(pooled, pw_ref)
        ps = ps_ref[...]
        _acc(dwpo_ref, _dot_tn(dbp_v, (mixed * ps).astype(BF16)), i == 0)
        dms = _dot(dbp_v, wpo_ref[...])
        _acc(dps_ref, jnp.sum(dms * mixed, axis=0, keepdims=True), i == 0)
        dmixed = (dms * ps).astype(BF16)
        dpooled = []
        for gi in range(NG):
            cols = slice(gi * POOL_GROUP, (gi + 1) * POOL_GROUP)
            _acc(dpw_ref.at[gi], _dot_tn(pooled[:, cols], dmixed[:, cols]), i == 0)
            dpooled.append(_dot_nt(dmixed[:, cols], pw_ref[gi]))
        dpl_ref[...] = jnp.concatenate(dpooled, axis=1)

    def row(width):
        return pl.BlockSpec((tm, width), lambda i: (i, 0))

    def whole(shape):
        nd = len(shape)
        return pl.BlockSpec(shape, lambda i: (0,) * nd)

    return pl.pallas_call(
        body, name="mix_bwd_branch", grid=(nt,),
        in_specs=[row(D), row(D), row(D), row(PW), _halo_specs(tm, PW, S, False), whole(pw.shape), whole(pscale.shape),
                  whole(wpot.shape), whole(wao.shape)],
        out_specs=[row(D), row(PW), whole((D, D)), whole((D, PW)), whole(pw.shape), whole((1, PW))],
        out_shape=[jax.ShapeDtypeStruct((S, D), BF16), jax.ShapeDtypeStruct((S, PW), F32),
                   jax.ShapeDtypeStruct((D, D), F32), jax.ShapeDtypeStruct((D, PW), F32),
                   jax.ShapeDtypeStruct(pw.shape, F32), jax.ShapeDtypeStruct((1, PW), F32)],
        compiler_params=_params(dimension_semantics=("arbitrary",)),
    )(dbp, dba, attn, xp, xp, pw, pscale, wpot, wao)


def _pool_bwd(dpooled, tm=512):
    S, PW = dpooled.shape
    nt = S // tm

    def body(dc_ref, dnext_ref, dxp_ref):
        i = pl.program_id(0)
        dc = dc_ref[...]
        dh = jnp.concatenate([dc, jnp.where(i < nt - 1, dnext_ref[...], 0.0)], axis=0)
        rows = tm + POOL_HALO
        t = lax.broadcasted_iota(jnp.int32, (rows, 1), 0) + i * tm
        out = []
        for gi, w in enumerate(POOL_WINDOWS):
            cols = slice(gi * POOL_GROUP, (gi + 1) * POOL_GROUP)
            acc = dh[:, cols] / jnp.minimum(t + 1, w).astype(F32)
            sh = 1
            while sh < w:
                acc = acc + pltpu.roll(acc, rows - sh, 0)
                sh *= 2
            out.append(acc[:tm, :] - dc[:, cols])
        dxp_ref[...] = jnp.concatenate(out, axis=1).astype(BF16)

    return pl.pallas_call(
        body, name="pool_bwd", grid=(nt,),
        in_specs=[pl.BlockSpec((tm, PW), lambda i: (i, 0)), _halo_specs(tm, PW, S, True)],
        out_specs=pl.BlockSpec((tm, PW), lambda i: (i, 0)), out_shape=jax.ShapeDtypeStruct((S, PW), BF16),
        compiler_params=_params(dimension_semantics=("arbitrary",)),
    )(dpooled, dpooled)


def _in_bwd_w(pieces, u, tm=1024):
    S, D = u.shape
    nt = S // tm
    NW = sum(width for _, width in SEGMENTS)

    def body(*refs):
        piece_refs, u_ref, dw_hbm, acc, sem = refs[:len(SEGMENTS)], refs[len(SEGMENTS)], refs[len(SEGMENTS) + 1], refs[-2], refs[-1]
        i = pl.program_id(0)
        u_t = u_ref[...]
        for (off, width), ref in zip(SEGMENTS, piece_refs):
            for lo in range(0, width, D):
                hi = min(lo + D, width)
                _acc(acc.at[off + lo:off + hi, :], _dot_tn(ref[:, lo:hi].astype(BF16), u_t), i == 0)

        @pl.when(i == nt - 1)
        def _():
            cp = pltpu.make_async_copy(acc, dw_hbm, sem)
            cp.start()
            cp.wait()

    return pl.pallas_call(
        body, name="in_bwd_w", grid=(nt,),
        in_specs=[pl.BlockSpec((tm, width), lambda i: (i, 0)) for _, width in SEGMENTS] + [pl.BlockSpec((tm, D), lambda i: (i, 0))],
        out_specs=pl.BlockSpec(memory_space=pl.ANY), out_shape=jax.ShapeDtypeStruct((NW, D), F32),
        scratch_shapes=[pltpu.VMEM((NW, D), F32), pltpu.SemaphoreType.DMA],
        compiler_params=_params(dimension_semantics=("arbitrary",)),
    )(*pieces, u)


def _in_bwd_x(pieces, wint, h, dh, gnorm, tm=512):
    S, D = h.shape
    nt = S // tm

    def body(*refs):
        piece_refs = refs[:len(SEGMENTS)]
        w_ref, h_ref, dh_ref, g_ref, dx_ref, dxh_ref, dg_ref = refs[len(SEGMENTS):]
        i = pl.program_id(0)
        du = jnp.zeros((tm, D), F32)
        for (off, width), ref in zip(SEGMENTS, piece_refs):
            du = du + _dot(ref[...].astype(BF16), w_ref[off:off + width, :])
        dx, dg_rows = _rms_bwd(du, h_ref[...], g_ref[...])
        out = dh_ref[...] + dx
        dx_ref[...] = out
        dxh_ref[...] = (0.5 * out).astype(BF16)
        _acc(dg_ref, jnp.sum(dg_rows, axis=0, keepdims=True), i == 0)

    def row(width):
        return pl.BlockSpec((tm, width), lambda i: (i, 0))

    vec = pl.BlockSpec((1, D), lambda i: (0, 0))
    return pl.pallas_call(
        body, name="in_bwd_x", grid=(nt,),
        in_specs=[row(width) for _, width in SEGMENTS] + [pl.BlockSpec(wint.shape, lambda i: (0, 0)), row(D), row(D), vec],
        out_specs=[row(D), row(D), vec],
        out_shape=[jax.ShapeDtypeStruct((S, D), F32), jax.ShapeDtypeStruct((S, D), BF16), jax.ShapeDtypeStruct((1, D), F32)],
        compiler_params=_params(dimension_semantics=("arbitrary",)),
    )(*pieces, wint, h, dh, gnorm)


def _row_tile(rows):
    for t in (512, 480, 352, 256, 128, 64, 32, 16, 8):
        if rows % t == 0:
            return t
    return rows


def _adamw(w, g, m, v, token, name):
    R, C = w.shape
    tr = _row_tile(R)

    def body(w_ref, g_ref, m_ref, v_ref, _, d_ref, mo_ref, vo_ref, go_ref):
        gv = g_ref[...]
        go_ref[...] = gv
        mn = ADAM_B1 * m_ref[...] + (1.0 - ADAM_B1) * gv
        vn = ADAM_B2 * v_ref[...] + (1.0 - ADAM_B2) * (gv * gv)
        m_hat = mn / (1.0 - ADAM_B1 ** ADAM_STEP)
        v_hat = vn / (1.0 - ADAM_B2 ** ADAM_STEP)
        d_ref[...] = -ADAM_LR * (m_hat / (jnp.sqrt(v_hat) + ADAM_EPS) + ADAM_WD * w_ref[...])
        mo_ref[...] = mn
        vo_ref[...] = vn

    blk = pl.BlockSpec((tr, C), lambda i: (i, 0))
    sh = jax.ShapeDtypeStruct((R, C), F32)
    return pl.pallas_call(
        body, name=name, grid=(R // tr,), in_specs=[blk] * 4 + [TOKEN_SPEC], out_specs=[blk] * 4, out_shape=[sh] * 4,
        compiler_params=_params(dimension_semantics=("arbitrary",)),
    )(w, g, m, v, token)


def _cast_place(w, place, name):
    R, C = w.shape
    tr = _row_tile(R)
    per = R // tr

    def body(p_ref, w_ref, o_ref):
        o_ref[...] = w_ref[...].astype(BF16)

    grid_spec = pltpu.PrefetchScalarGridSpec(
        num_scalar_prefetch=1, grid=(per,),
        in_specs=[pl.BlockSpec((tr, C), lambda i, p: (i, 0))],
        out_specs=pl.BlockSpec((tr, C), lambda i, p: (p[0] * per + i, 0)))
    return pl.pallas_call(
        body, name=name, grid_spec=grid_spec, out_shape=jax.ShapeDtypeStruct((N_SHARDS * R, C), BF16),
        compiler_params=_params(dimension_semantics=("arbitrary",)),
    )(place, w)


def _sum_halves(g4, recv, place, name):
    NS, R, C = g4.shape
    hr = R // 2
    tr = _row_tile(hr)
    per = hr // tr

    def body(p_ref, g_ref, r_ref, o_ref):
        o_ref[...] = (g_ref[...] + r_ref[...]).astype(BF16)

    grid_spec = pltpu.PrefetchScalarGridSpec(
        num_scalar_prefetch=1, grid=(NS, per),
        in_specs=[pl.BlockSpec((1, tr, C), lambda s, i, p: (s, p[1] * per + i, 0)),
                  pl.BlockSpec((1, tr, C), lambda s, i, p: (s, i, 0))],
        out_specs=pl.BlockSpec((1, tr, C), lambda s, i, p: (s, i, 0)))
    return pl.pallas_call(
        body, name=name, grid_spec=grid_spec, out_shape=jax.ShapeDtypeStruct((NS, hr, C), BF16),
        compiler_params=_params(dimension_semantics=("arbitrary", "arbitrary")),
    )(place, g4, recv)


def _sum_quarters(h4, recv3, place, name):
    NS, hr, C = h4.shape
    tr = _row_tile(hr)
    per = hr // tr

    def body(p_ref, h_ref, r_ref, o_ref):
        acc = h_ref[0].astype(F32)
        for k in range(N_SHARDS - 1):
            acc = acc + r_ref[k].astype(F32)
        o_ref[...] = acc

    grid_spec = pltpu.PrefetchScalarGridSpec(
        num_scalar_prefetch=1, grid=(per,),
        in_specs=[pl.BlockSpec((1, tr, C), lambda i, p: (p[0], i, 0)),
                  pl.BlockSpec((N_SHARDS - 1, tr, C), lambda i, p: (0, i, 0))],
        out_specs=pl.BlockSpec((tr, C), lambda i, p: (p[1] * per + i, 0)))
    return pl.pallas_call(
        body, name=name, grid_spec=grid_spec, out_shape=jax.ShapeDtypeStruct((2 * hr, C), F32),
        compiler_params=_params(dimension_semantics=("arbitrary",)),
    )(place, h4, recv3)


def _place():
    x, y, c = lax.axis_index("x"), lax.axis_index("y"), lax.axis_index("c")
    chips = [(1 - x, y), (x, 1 - y), (1 - x, 1 - y)]
    return x, y, c, chips


HBM_SPEC = pl.BlockSpec(memory_space=pltpu.HBM)
SEM_SPEC = pl.BlockSpec(memory_space=pltpu.SEMAPHORE)
DATAFLOW = pltpu.SideEffectType.DATAFLOW_SIDE_EFFECTING


def _hbm(a):
    return pltpu.with_memory_space_constraint(a, pltpu.HBM)


class InFlight(NamedTuple):
    send_sem: jax.Array
    recv_sem: jax.Array
    bufs: list
    plan: Callable
    token: jax.Array


def _wait_all(plan, refs, send_ref, recv_ref):
    for k, (src, dst, dev) in enumerate(plan(refs)):
        cp = pltpu.make_async_remote_copy(src_ref=src, dst_ref=dst, send_sem=send_ref.at[k], recv_sem=recv_ref.at[k],
                                          device_id=dev, device_id_type=MESH)
        cp.wait_send()
        cp.wait_recv()


def _start_all(plan, refs, send_ref, recv_ref):
    for k, (src, dst, dev) in enumerate(plan(refs)):
        pltpu.make_async_remote_copy(src_ref=src, dst_ref=dst, send_sem=send_ref.at[k], recv_sem=recv_ref.at[k],
                                     device_id=dev, device_id_type=MESH).start()


def _split_start(name, bufs, plan, n_copies, after, waits=None):
    nb = len(bufs)
    after = list(after) if isinstance(after, (list, tuple)) else [after]

    def body(*refs):
        ins = refs[:nb]
        if waits is not None:
            _wait_all(waits.plan, ins, refs[nb], refs[nb + 1])
        send_sem, recv_sem, token = refs[-nb - 3], refs[-nb - 2], refs[-1]
        _start_all(plan, ins, send_sem, recv_sem)
        token[...] = jnp.zeros_like(token)

    sems = pltpu.SemaphoreType.DMA((n_copies,))
    earlier = [] if waits is None else [waits.send_sem, waits.recv_sem]
    out = pl.pallas_call(
        body, name=name, in_specs=[HBM_SPEC] * nb + [SEM_SPEC] * len(earlier) + [TOKEN_SPEC] * len(after),
        out_shape=(sems, sems, *[pltpu.HBM(b.shape, b.dtype) for b in bufs], jax.ShapeDtypeStruct((8, 128), F32)),
        out_specs=(SEM_SPEC, SEM_SPEC, *[HBM_SPEC] * nb, pl.BlockSpec(memory_space=pltpu.VMEM)),
        input_output_aliases={i: i + 2 for i in range(nb)},
        compiler_params=pltpu.CompilerParams(has_side_effects=DATAFLOW),
    )(*[_hbm(b) for b in bufs], *earlier, *after)
    return InFlight(out[0], out[1], list(out[2:2 + nb]), plan, out[-1])


def _split_wait(name, flight, after):
    nb = len(flight.bufs)
    after = list(after) if isinstance(after, (list, tuple)) else [after]

    def body(*refs):
        _wait_all(flight.plan, refs[:nb], refs[nb], refs[nb + 1])

    out = pl.pallas_call(
        body, name=name, in_specs=[HBM_SPEC] * nb + [SEM_SPEC, SEM_SPEC] + [TOKEN_SPEC] * len(after),
        out_shape=[pltpu.HBM(b.shape, b.dtype) for b in flight.bufs], out_specs=[HBM_SPEC] * nb,
        input_output_aliases={i: i for i in range(nb)},
        compiler_params=pltpu.CompilerParams(has_side_effects=DATAFLOW),
    )(*flight.bufs, flight.send_sem, flight.recv_sem, *after)
    return list(out)


def _half_rows(buf, chip, core):
    hr = buf.shape[0] // (2 * N_SHARDS)
    return buf.at[pl.ds(pl.multiple_of((2 * chip + core) * hr, 16), hr)]


def _gather_ici_plan(bufs):
    x, y, c, chips = _place()
    return [(_half_rows(b, 2 * x + y, c), _half_rows(b, 2 * x + y, c), (px, py, c)) for b in bufs for px, py in chips]


def _gather_d2d_plan(bufs):
    x, y, c, chips = _place()
    return [(_half_rows(b, 2 * px + py, c), _half_rows(b, 2 * px + py, c), (x, y, 1 - c)) for b in bufs for px, py in chips]


def _swap_plan(bufs):
    x, y, c, _ = _place()
    n = len(bufs) // 2
    copies = []
    for g, land in zip(bufs[:n], bufs[n:]):
        hr = g.shape[1] // 2
        copies.append((g.at[:, pl.ds(pl.multiple_of((1 - c) * hr, 8), hr)], land, (x, y, 1 - c)))
    return copies


def _exchange_plan(bufs):
    x, y, c, chips = _place()
    n = len(bufs) // 2
    return [(h.at[2 * px + py], land.at[k], (px, py, c))
            for h, land in zip(bufs[:n], bufs[n:]) for k, (px, py) in enumerate(chips)]


def _share_plan(bufs):
    x, y, c, _ = _place()
    copies = []
    for buf in bufs:
        hr = buf.shape[0] // 2
        mine = buf.at[pl.ds(pl.multiple_of(c * hr, 8), hr)]
        copies.append((mine, mine, (x, y, 1 - c)))
    return copies


N_DEVICES = 8


def _slot_place(vec, place):
    R, C = vec.shape

    def body(p_ref, v_ref, o_ref):
        o_ref[0] = v_ref[...]

    grid_spec = pltpu.PrefetchScalarGridSpec(
        num_scalar_prefetch=1, grid=(1,), in_specs=[pl.BlockSpec((R, C), lambda i, p: (0, 0))],
        out_specs=pl.BlockSpec((1, R, C), lambda i, p: (2 * p[0] + p[1], 0, 0)))
    return pl.pallas_call(
        body, name="slot_place", grid_spec=grid_spec, out_shape=jax.ShapeDtypeStruct((N_DEVICES, R, C), F32),
        compiler_params=_params(dimension_semantics=("arbitrary",)),
    )(place, vec)


def _slots_plan(bufs):
    x, y, c, _ = _place()
    mine = bufs[0].at[4 * x + 2 * y + c]
    return [(mine, mine, (x ^ (r >> 2), y ^ ((r >> 1) & 1), c ^ (r & 1))) for r in range(1, N_DEVICES)]


def _sum_slots(slots):
    _, R, C = slots.shape

    def body(s_ref, o_ref):
        acc = s_ref[0]
        for d in range(1, N_DEVICES):
            acc = acc + s_ref[d]
        o_ref[...] = acc

    return pl.pallas_call(
        body, name="sum_slots", in_specs=[pl.BlockSpec(memory_space=pltpu.VMEM)],
        out_specs=pl.BlockSpec(memory_space=pltpu.VMEM), out_shape=jax.ShapeDtypeStruct((R, C), F32),
        compiler_params=_params(),
    )(slots)


SMALL = ("ffn1_norm", "mix_norm", "pool_w", "pool_scale", "q_norm", "k_norm", "sinks", "gate_bias", "ffn2_norm")
SMALL_COLS = 1024
FFN1 = ("ffn1_w_gate", "ffn1_w_up", "ffn1_w_down")
MIXER = ("w_in", "w_pool_out", "w_attn_out", "w_out")
FFN2 = ("ffn2_w_gate", "ffn2_w_up", "ffn2_w_down")
LARGE = FFN1 + MIXER + FFN2
TRANSPOSED = ("ffn1_w_gate", "ffn1_w_up", "w_in", "w_pool_out", "ffn2_w_gate", "ffn2_w_up")
WEIGHTS = ("ffn1_norm", "ffn1_w_gate", "ffn1_w_up", "ffn1_w_down", "mix_norm", "w_in", "pool_w", "pool_scale",
           "w_pool_out", "q_norm", "k_norm", "sinks", "w_attn_out", "gate_bias", "w_out", "ffn2_norm",
           "ffn2_w_gate", "ffn2_w_up", "ffn2_w_down")


def _pack_small(parts):
    flat = jnp.concatenate([p.reshape(-1) for p in parts])
    rows = -(-flat.shape[0] // (8 * SMALL_COLS)) * 8
    return jnp.pad(flat, (0, rows * SMALL_COLS - flat.shape[0])).reshape(rows, SMALL_COLS)


def _unpack_small(packed, like):
    flat = packed.reshape(-1)
    out, off = [], 0
    for p in like:
        out.append(flat[off:off + p.size].reshape(p.shape))
        off += p.size
    return out, flat[off]


def _tie(small, token):
    return small + token[0, 0]


class Reduction:
    def __init__(self, group, names, grads, place):
        self.group, self.names, self.place = group, names, place
        self.bufs = [g.reshape(N_SHARDS, -1, g.shape[-1]) for g in grads]
        self.flight = None

    def _start(self, stage, bufs, plan, n_copies, after):
        self.flight = _split_start(f"{self.group}_{stage}", bufs, plan, n_copies, after)
        return self.flight.token

    def _landed(self, stage, after):
        n = len(self.names)
        bufs = _split_wait(f"{self.group}_{stage}_wait", self.flight, after)
        return bufs[:n], bufs[n:]

    def start_swap(self, after):
        lands = [lax.empty((g.shape[0], g.shape[1] // 2, g.shape[2]), g.dtype) for g in self.bufs]
        return self._start("swap", self.bufs + lands, _swap_plan, len(self.bufs), after)

    def start_exchange(self, after):
        g4, recv = self._landed("swap", after)
        halves = [_sum_halves(g, r, self.place, f"sum_halves_{n}") for n, g, r in zip(self.names, g4, recv)]
        lands = [lax.empty((N_SHARDS - 1,) + h.shape[1:], h.dtype) for h in halves]
        return self._start("exchange", halves + lands, _exchange_plan, (N_SHARDS - 1) * len(halves), self.place)

    def start_share(self, after):
        halves, recv3 = self._landed("exchange", after)
        reduced = [_sum_quarters(h, r, self.place, f"sum_quarters_{n}") for n, h, r in zip(self.names, halves, recv3)]
        return self._start("share", reduced, _share_plan, len(reduced), self.place)

    def finish(self, after):
        return dict(zip(self.names, _split_wait(f"{self.group}_share_wait", self.flight, after)))


def _row_form(name, a):
    return a.T if name in TRANSPOSED else a


def kernel(x, ffn1_norm, ffn1_w_gate, ffn1_w_up, ffn1_w_down, mix_norm, w_in, pool_w, pool_scale, w_pool_out, q_norm, k_norm, sinks, w_attn_out, gate_bias, w_out, ffn2_norm, ffn2_w_gate, ffn2_w_up, ffn2_w_down, loss_target, m_ffn1_norm, m_ffn1_w_gate, m_ffn1_w_up, m_ffn1_w_down, m_mix_norm, m_w_in, m_pool_w, m_pool_scale, m_w_pool_out, m_q_norm, m_k_norm, m_sinks, m_w_attn_out, m_gate_bias, m_w_out, m_ffn2_norm, m_ffn2_w_gate, m_ffn2_w_up, m_ffn2_w_down, v_ffn1_norm, v_ffn1_w_gate, v_ffn1_w_up, v_ffn1_w_down, v_mix_norm, v_w_in, v_pool_w, v_pool_scale, v_w_pool_out, v_q_norm, v_k_norm, v_sinks, v_w_attn_out, v_gate_bias, v_w_out, v_ffn2_norm, v_ffn2_w_gate, v_ffn2_w_up, v_ffn2_w_down):
    args = dict(locals())
    wts = {n: _row_form(n, args[n]) for n in WEIGHTS}
    mom = {n: _row_form(n, args["m_" + n]) for n in WEIGHTS}
    var = {n: _row_form(n, args["v_" + n]) for n in WEIGHTS}
    shard = 2 * lax.axis_index("x") + lax.axis_index("y")
    place = jnp.stack([shard, lax.axis_index("c")]).astype(jnp.int32)

    xs, target = x[0], loss_target[0]
    D = xs.shape[1]
    g1 = wts["ffn1_norm"].reshape(1, D)
    gm = wts["mix_norm"].reshape(1, D)
    g2 = wts["ffn2_norm"].reshape(1, D)
    qw = wts["q_norm"].reshape(1, HEAD_DIM)
    kw = wts["k_norm"].reshape(1, HEAD_DIM)
    bias = wts["gate_bias"].reshape(1, 2 * D)
    pscale = wts["pool_scale"].reshape(1, -1)
    pw = wts["pool_w"].astype(BF16)
    sink_rows = jnp.broadcast_to(jnp.repeat(wts["sinks"], ATTN_BLOCK).reshape(N_KV_HEADS, GQA_GROUP * ATTN_BLOCK, 1),
                                 (N_KV_HEADS, GQA_GROUP * ATTN_BLOCK, 128))

    first, down1 = FFN1[:2], FFN1[2:]
    placed = {n: _cast_place(wts[n], place, "cast_" + n) for n in FFN1}
    first_ici = _split_start("gather_first_ici", [placed[n] for n in first], _gather_ici_plan, 3 * len(first), place)
    placed.update({n: _cast_place(wts[n], place, "cast_" + n) for n in MIXER + FFN2})
    zero = jnp.zeros((1,), F32)
    packed = [_pack_small([d[n] for n in SMALL] + [zero]) for d in (wts, mom, var)]
    shadow = [placed[n] for n in down1 + MIXER + FFN2] + packed + [sink_rows, pw]
    first_d2d = _split_start("gather_first_d2d", first_ici.bufs, _gather_d2d_plan, 3 * len(first), shadow, waits=first_ici)
    down1_ici = _split_start("gather_down1_ici", [placed[n] for n in down1], _gather_ici_plan, 3 * len(down1), first_d2d.token)
    w = dict(zip(first, _split_wait("gather_first_wait", first_d2d, down1_ici.token)))
    mix_ici = _split_start("gather_mix_ici", [placed[n] for n in MIXER], _gather_ici_plan, 3 * len(MIXER), w[first[0]])
    ffn2_ici = _split_start("gather_ffn2_ici", [placed[n] for n in FFN2], _gather_ici_plan, 3 * len(FFN2), mix_ici.token)

    n1, a1, b1, s1 = _ffn_up(xs, _tie(g1, ffn2_ici.token), w["ffn1_w_gate"], w["ffn1_w_up"], "ffn1_up")
    down1_d2d = _split_start("gather_down1_d2d", down1_ici.bufs, _gather_d2d_plan, 3 * len(down1), s1, waits=down1_ici)
    w.update(zip(down1, _split_wait("gather_down1_wait", down1_d2d, down1_d2d.token)))
    h1 = _ffn_down(xs, s1, w["ffn1_w_down"], "ffn1_down")
    mix_d2d = _split_start("gather_mix_d2d", mix_ici.bufs, _gather_d2d_plan, 3 * len(MIXER), h1, waits=mix_ici)
    w.update(zip(MIXER, _split_wait("gather_mix_wait", mix_d2d, mix_d2d.token)))
    u, xp, q, kv, gl = _mix_proj(h1, gm, w["w_in"])
    attn = _attn_fwd(q, kv, qw, kw, sink_rows)
    ffn2_d2d = _split_start("gather_ffn2_d2d", ffn2_ici.bufs, _gather_d2d_plan, 3 * len(FFN2), attn, waits=ffn2_ici)
    h2, bp, ba = _mix_out(xp, attn, gl, _tie(bias, ffn2_d2d.token), pw, pscale, w["w_pool_out"], w["w_attn_out"], w["w_out"], h1)
    w.update(zip(FFN2, _split_wait("gather_ffn2_wait", ffn2_d2d, h2)))
    n2, a2, b2, s2 = _ffn_up(h2, g2, w["ffn2_w_gate"], w["ffn2_w_up"], "ffn2_up")
    dy, dyh, loss = _ffn_down_loss(h2, s2, w["ffn2_w_down"], target, "ffn2_down_loss")

    gw, gs = {}, {}
    da, db = _ffn_bwd_h(dyh, a2, b2, w["ffn2_w_down"], ffn2_d2d.token, "ffn2_bwd_h")
    gw["ffn2_w_gate"], gw["ffn2_w_up"] = _xty(da, n2, "ffn2_dw_gate"), _xty(db, n2, "ffn2_dw_up")
    gw["ffn2_w_down"] = _xty(s2, dyh, "ffn2_dw_down")
    red2 = Reduction("reduce_ffn2", FFN2, [gw[n] for n in FFN2], place)
    token = red2.start_swap(da)
    dh2, gs["ffn2_norm"] = _ffn_bwd_x(da, db, w["ffn2_w_gate"], w["ffn2_w_up"], h2, dy, _tie(g2, token), "ffn2_bwd_x")
    token = red2.start_exchange(dh2)
    dgl, dbp, dba, gw["w_out"], gs["gate_bias"] = _mix_bwd_gate(dh2, w["w_out"], bp, ba, gl, _tie(bias, token))
    dattn, dpooled, gw["w_attn_out"], gw["w_pool_out"], gs["pool_w"], gs["pool_scale"] = _mix_bwd_branch(
        dbp, dba, attn, xp, pw, pscale, w["w_pool_out"], w["w_attn_out"])
    dq, dkv, gs["q_norm"], gs["k_norm"], dsk = _attn_bwd(q, kv, dattn, qw, kw, sink_rows)
    gs["sinks"] = dsk[:, :, 0]
    red2.start_share(dq)
    pieces = (_pool_bwd(dpooled), dq, dkv, dgl)
    gw["w_in"] = _in_bwd_w(pieces, u)
    grads = red2.finish(gw["w_in"])
    redm = Reduction("reduce_mix", MIXER, [gw[n] for n in MIXER], place)
    token = redm.start_swap(grads[FFN2[0]])
    dh1, dh1h, gs["mix_norm"] = _in_bwd_x(pieces, w["w_in"], h1, dh2, _tie(gm, token))
    token = redm.start_exchange(dh1)
    da, db = _ffn_bwd_h(dh1h, a1, b1, w["ffn1_w_down"], token, "ffn1_bwd_h")
    gw["ffn1_w_gate"], gw["ffn1_w_up"] = _xty(da, n1, "ffn1_dw_gate"), _xty(db, n1, "ffn1_dw_up")
    gw["ffn1_w_down"] = _xty(s1, dh1h, "ffn1_dw_down")
    token = redm.start_share([gw[n] for n in FFN1])
    red1 = Reduction("reduce_ffn1", FFN1, [gw[n] for n in FFN1], place)
    token = red1.start_swap(token)
    grad_x, gs["ffn1_norm"] = _ffn_bwd_x(da, db, w["ffn1_w_gate"], w["ffn1_w_up"], xs, dh1, _tie(g1, token), "ffn1_bwd_x")
    grads.update(redm.finish(grad_x))

    small_parts = [gs[n] for n in SMALL] + [loss[0, 0].reshape(1)]
    slots = _split_start("gather_small", [_slot_place(_pack_small(small_parts), place)], _slots_plan, N_DEVICES - 1,
                         grads[MIXER[0]])
    token = red1.start_exchange(slots.token)
    delta, new_m, new_v = {}, {}, {}
    for n in FFN2 + MIXER:
        delta[n], new_m[n], new_v[n], grads[n] = _adamw(wts[n], grads[n], mom[n], var[n], token, "adamw_" + n)
    summed = _sum_slots(_split_wait("gather_small_wait", slots, delta[MIXER[-1]])[0])
    small_grads, loss_sum = _unpack_small(summed, [wts[n] for n in SMALL])
    grads.update(dict(zip(SMALL, small_grads)))
    ds, ms, vs, _ = _adamw(packed[0], summed, packed[1], packed[2], token, "adamw_small")
    like = [wts[n] for n in SMALL]
    for out, packed_out in ((delta, ds), (new_m, ms), (new_v, vs)):
        out.update(dict(zip(SMALL, _unpack_small(packed_out, like)[0])))
    results = {(k, n): _row_form(n, d[n]) for k, d in enumerate((grads, delta, new_m, new_v)) for n in SMALL + FFN2 + MIXER}
    token = red1.start_share([results[k, n] for k in range(4) for n in SMALL + ("w_pool_out",)])
    grads.update(red1.finish(token))
    for n in FFN1:
        delta[n], new_m[n], new_v[n], grads[n] = _adamw(wts[n], grads[n], mom[n], var[n], token, "adamw_" + n)
    results.update({(k, n): _row_form(n, d[n]) for k, d in enumerate((grads, delta, new_m, new_v)) for n in FFN1})
    return (loss_sum, grad_x[None], *[results[k, n] for k in range(4) for n in WEIGHTS])
```

```python
from typing import Callable, NamedTuple

import jax
import jax.numpy as jnp
from jax import lax
from jax.experimental import pallas as pl
from jax.experimental.pallas import tpu as pltpu

F32 = jnp.float32
BF16 = jnp.bfloat16
RMS_EPS = 1e-6
POOL_WINDOWS = (2, 4, 8, 16)
POOL_GROUP = 128
POOL_HALO = 16
HEAD_DIM = 64
GQA_GROUP = 8
N_KV_HEADS = 2
ATTN_BLOCK = 128
SCALE = HEAD_DIM ** -0.5
NEG = -1e30
N_SHARDS = 4
ADAM_LR, ADAM_B1, ADAM_B2, ADAM_EPS, ADAM_WD, ADAM_STEP = 0.001, 0.9, 0.999, 1e-08, 0.01, 10
VMEM_LIMIT = 56 * 1024 * 1024
MESH = pl.DeviceIdType.MESH
SEG_POOL, SEG_Q, SEG_KV, SEG_GATE = (0, 512), (512, 1024), (1536, 256), (1792, 2048)
SEGMENTS = (SEG_POOL, SEG_Q, SEG_KV, SEG_GATE)


def _params(**kw):
    return pltpu.CompilerParams(vmem_limit_bytes=VMEM_LIMIT, **kw)


def _dot(a, b):
    return jnp.dot(a, b, preferred_element_type=F32)


def _dot_nt(a, b):
    return lax.dot_general(a, b, (((1,), (1,)), ((), ())), preferred_element_type=F32)


def _dot_tn(a, b):
    return lax.dot_general(a, b, (((0,), (0,)), ((), ())), preferred_element_type=F32)


def _rinv(x):
    return lax.rsqrt(jnp.mean(x * x, axis=-1, keepdims=True) + RMS_EPS)


def _rms_bwd(dn, x, g):
    r = _rinv(x)
    xr = x * r
    z = dn * g
    dx = r * (z - xr * jnp.mean(z * xr, axis=-1, keepdims=True))
    return dx, dn * xr


def _acc(ref, val, first):
    @pl.when(first)
    def _():
        ref[...] = val

    @pl.when(jnp.logical_not(first))
    def _():
        ref[...] += val


TOKEN_SPEC = pl.BlockSpec(memory_space=pl.ANY)
F_HALF = 1408


def _resident(w):
    return pl.BlockSpec(w.shape, lambda i: (0, 0), pipeline_mode=pl.Buffered(1))


def _ffn_up(h, gnorm, wgt, wut, name, tm=512):
    S, D = h.shape
    F = wgt.shape[0]

    def body(h_ref, g_ref, wg_ref, wu_ref, n_ref, a_ref, b_ref, s_ref):
        x = h_ref[...]
        n = (x * _rinv(x) * g_ref[...]).astype(BF16)
        n_ref[...] = n
        for lo in range(0, F, F_HALF):
            cols = slice(lo, lo + F_HALF)
            a = _dot_nt(n, wg_ref[cols, :])
            b = _dot_nt(n, wu_ref[cols, :])
            a_ref[:, cols] = a.astype(BF16)
            b_ref[:, cols] = b.astype(BF16)
            s_ref[:, cols] = (a * jax.nn.sigmoid(a) * b).astype(BF16)

    act = pl.BlockSpec((tm, F), lambda i: (i, 0))
    hidden = jax.ShapeDtypeStruct((S, F), BF16)
    return pl.pallas_call(
        body, name=name, grid=(S // tm,),
        in_specs=[pl.BlockSpec((tm, D), lambda i: (i, 0)), pl.BlockSpec((1, D), lambda i: (0, 0)), _resident(wgt), _resident(wut)],
        out_specs=[pl.BlockSpec((tm, D), lambda i: (i, 0)), act, act, act],
        out_shape=[jax.ShapeDtypeStruct((S, D), BF16), hidden, hidden, hidden],
        compiler_params=_params(dimension_semantics=("arbitrary",)),
    )(h, gnorm, wgt, wut)


def _ffn_down(h, s, wd, name, tm=1024):
    S, D = h.shape
    F = wd.shape[0]

    def body(h_ref, s_ref, wd_ref, o_ref):
        o_ref[...] = h_ref[...] + 0.5 * _dot(s_ref[...], wd_ref[...])

    row = pl.BlockSpec((tm, D), lambda i: (i, 0))
    return pl.pallas_call(
        body, name=name, grid=(S // tm,), in_specs=[row, pl.BlockSpec((tm, F), lambda i: (i, 0)), _resident(wd)],
        out_specs=row, out_shape=jax.ShapeDtypeStruct((S, D), F32),
        compiler_params=_params(dimension_semantics=("arbitrary",)),
    )(h, s, wd)


def _ffn_down_loss(h, s, wd, target, name, tm=512):
    S, D = h.shape
    F = wd.shape[0]

    def body(h_ref, s_ref, wd_ref, t_ref, dy_ref, dyh_ref, loss_ref):
        e = h_ref[...] + 0.5 * _dot(s_ref[...], wd_ref[...]) - t_ref[...]
        dy = e * (1.0 / D)
        dy_ref[...] = dy
        dyh_ref[...] = (0.5 * dy).astype(BF16)
        tot = jnp.sum(jnp.sum(e * e, axis=1, keepdims=True), axis=0, keepdims=True) * (0.5 / D)
        _acc(loss_ref, jnp.broadcast_to(tot, loss_ref.shape), pl.program_id(0) == 0)

    row = pl.BlockSpec((tm, D), lambda i: (i, 0))
    return pl.pallas_call(
        body, name=name, grid=(S // tm,), in_specs=[row, pl.BlockSpec((tm, F), lambda i: (i, 0)), _resident(wd), row],
        out_specs=[row, row, pl.BlockSpec((8, 128), lambda i: (0, 0))],
        out_shape=[jax.ShapeDtypeStruct((S, D), F32), jax.ShapeDtypeStruct((S, D), BF16), jax.ShapeDtypeStruct((8, 128), F32)],
        compiler_params=_params(dimension_semantics=("arbitrary",)),
    )(h, s, wd, target)


def _ffn_bwd_h(dyh, a, b, wd, token, name, tm=512):
    S, D = dyh.shape
    F = wd.shape[0]

    def body(dyh_ref, a_ref, b_ref, wd_ref, _, da_ref, db_ref):
        dyh_t = dyh_ref[...]
        for lo in range(0, F, F_HALF):
            cols = slice(lo, lo + F_HALF)
            av = a_ref[:, cols].astype(F32)
            bv = b_ref[:, cols].astype(F32)
            ds = _dot_nt(dyh_t, wd_ref[cols, :])
            sig = jax.nn.sigmoid(av)
            silu = av * sig
            da_ref[:, cols] = (ds * bv * (sig * (1.0 + av * (1.0 - sig)))).astype(BF16)
            db_ref[:, cols] = (ds * silu).astype(BF16)

    act = pl.BlockSpec((tm, F), lambda i: (i, 0))
    hidden = jax.ShapeDtypeStruct((S, F), BF16)
    return pl.pallas_call(
        body, name=name, grid=(S // tm,),
        in_specs=[pl.BlockSpec((tm, D), lambda i: (i, 0)), act, act, _resident(wd), TOKEN_SPEC],
        out_specs=[act, act], out_shape=[hidden, hidden],
        compiler_params=_params(dimension_semantics=("arbitrary",)),
    )(dyh, a, b, wd, token)


def _xty(x, y, name, tk=2048, tf=1408):
    S, F = x.shape
    D = y.shape[1]

    def body(x_ref, y_ref, o_ref):
        _acc(o_ref, _dot_tn(x_ref[...], y_ref[...]), pl.program_id(1) == 0)

    return pl.pallas_call(
        body, name=name, grid=(F // tf, S // tk),
        in_specs=[pl.BlockSpec((tk, tf), lambda j, k: (k, j)), pl.BlockSpec((tk, D), lambda j, k: (k, 0))],
        out_specs=pl.BlockSpec((tf, D), lambda j, k: (j, 0)), out_shape=jax.ShapeDtypeStruct((F, D), F32),
        compiler_params=_params(dimension_semantics=("arbitrary", "arbitrary")),
    )(x, y)


def _ffn_bwd_x(da, db, wgt, wut, x, dh, gnorm, name, tm=512):
    S, D = x.shape
    F = wgt.shape[0]

    def body(da_ref, db_ref, wg_ref, wu_ref, x_ref, dh_ref, g_ref, dx_ref, dg_ref):
        dn = _dot(da_ref[...], wg_ref[...]) + _dot(db_ref[...], wu_ref[...])
        dx, dg_rows = _rms_bwd(dn, x_ref[...], g_ref[...])
        dx_ref[...] = dh_ref[...] + dx
        _acc(dg_ref, jnp.sum(dg_rows, axis=0, keepdims=True), pl.program_id(0) == 0)

    row = pl.BlockSpec((tm, D), lambda i: (i, 0))
    act = pl.BlockSpec((tm, F), lambda i: (i, 0))
    vec = pl.BlockSpec((1, D), lambda i: (0, 0))
    return pl.pallas_call(
        body, name=name, grid=(S // tm,), in_specs=[act, act, _resident(wgt), _resident(wut), row, row, vec],
        out_specs=[row, vec],
        out_shape=[jax.ShapeDtypeStruct((S, D), F32), jax.ShapeDtypeStruct((1, D), F32)],
        compiler_params=_params(dimension_semantics=("arbitrary",)),
    )(da, db, wgt, wut, x, dh, gnorm)


def _mix_proj(h, gnorm, wint, tm=512):
    S, D = h.shape
    nt = S // tm

    def body(h_ref, g_ref, w_ref, u_ref, xp_ref, q_ref, kv_ref, gl_ref):
        x = h_ref[...]
        u = (x * _rinv(x) * g_ref[...]).astype(BF16)
        u_ref[...] = u
        for (off, width), ref in zip(SEGMENTS, (xp_ref, q_ref, kv_ref, gl_ref)):
            ref[...] = _dot_nt(u, w_ref[off:off + width, :]).astype(ref.dtype)

    def row(width):
        return pl.BlockSpec((tm, width), lambda i: (i, 0))

    return pl.pallas_call(
        body, name="mix_proj", grid=(nt,),
        in_specs=[row(D), pl.BlockSpec((1, D), lambda i: (0, 0)), pl.BlockSpec(wint.shape, lambda i: (0, 0))],
        out_specs=[row(D), row(SEG_POOL[1]), row(SEG_Q[1]), row(SEG_KV[1]), row(SEG_GATE[1])],
        out_shape=[jax.ShapeDtypeStruct((S, D), BF16), jax.ShapeDtypeStruct((S, SEG_POOL[1]), F32),
                   jax.ShapeDtypeStruct((S, SEG_Q[1]), BF16), jax.ShapeDtypeStruct((S, SEG_KV[1]), BF16),
                   jax.ShapeDtypeStruct((S, SEG_GATE[1]), BF16)],
        compiler_params=_params(dimension_semantics=("arbitrary",)),
    )(h, gnorm, wint)


def _stack_heads(x, g):
    return jnp.concatenate([x[:, (GQA_GROUP * g + hh) * HEAD_DIM:(GQA_GROUP * g + hh + 1) * HEAD_DIM]
                            for hh in range(GQA_GROUP)], axis=0)


def _unstack_heads(ref, val, g):
    for hh in range(GQA_GROUP):
        lo = (GQA_GROUP * g + hh) * HEAD_DIM
        ref[:, lo:lo + HEAD_DIM] = val[hh * ATTN_BLOCK:(hh + 1) * ATTN_BLOCK, :]


def _rowsum(xb, width):
    return _dot(xb, jnp.ones((xb.shape[1], width), BF16))


def _rinv_lanes(x):
    return lax.rsqrt(_rowsum((x * x).astype(BF16), x.shape[1]) * (1.0 / x.shape[1]) + RMS_EPS)


def _attn_exp(qn, kc, kp, sink, n):
    rows = GQA_GROUP * ATTN_BLOCK
    qi = lax.broadcasted_iota(jnp.int32, (rows, ATTN_BLOCK), 0) % ATTN_BLOCK
    kj = lax.broadcasted_iota(jnp.int32, (rows, ATTN_BLOCK), 1)
    s_c = jnp.where(kj <= qi, _dot_nt(qn, kc), NEG)
    s_p = jnp.where(jnp.logical_and(kj > qi, n > 0), _dot_nt(qn, kp), NEG)
    m = jnp.maximum(jnp.max(s_c, axis=-1, keepdims=True), jnp.max(s_p, axis=-1, keepdims=True))
    m = jnp.maximum(jnp.broadcast_to(m, sink.shape), sink)
    e_c = jnp.exp(s_c - m)
    e_p = jnp.exp(s_p - m)
    e_s = jnp.exp(sink - m)
    e_cb, e_pb = e_c.astype(BF16), e_p.astype(BF16)
    inv = 1.0 / (_rowsum(jnp.concatenate([e_cb, e_pb], axis=1), ATTN_BLOCK) + e_s)
    return e_cb, e_pb, e_c, e_p, e_s, inv


def _attn_blocks(n):
    cur = pl.multiple_of(n * ATTN_BLOCK, ATTN_BLOCK)
    prev = pl.multiple_of(jnp.maximum(n - 1, 0) * ATTN_BLOCK, ATTN_BLOCK)
    return cur, prev


def _kv_split(kv, g):
    k = kv[:, g * HEAD_DIM:(g + 1) * HEAD_DIM]
    v = kv[:, (N_KV_HEADS + g) * HEAD_DIM:(N_KV_HEADS + g + 1) * HEAD_DIM]
    return k, v


def _attn_fwd(q, kv, qw, kw, sink_rows):
    S, W = q.shape
    nb = S // ATTN_BLOCK

    def body(q_ref, kv_ref, qw_ref, kw_ref, sk_ref, o_ref, o_scr):
        n = pl.program_id(0)
        qf = q_ref[...].astype(F32)
        cur, prev = _attn_blocks(n)
        kvc = kv_ref[pl.ds(cur, ATTN_BLOCK), :].astype(F32)
        kvp = kv_ref[pl.ds(prev, ATTN_BLOCK), :].astype(F32)
        for g in range(N_KV_HEADS):
            qs = _stack_heads(qf, g)
            qn = (qs * _rinv_lanes(qs) * qw_ref[...] * SCALE).astype(BF16)
            kc, vc = _kv_split(kvc, g)
            kp, vp = _kv_split(kvp, g)
            kc = (kc * _rinv_lanes(kc) * kw_ref[...]).astype(BF16)
            kp = (kp * _rinv_lanes(kp) * kw_ref[...]).astype(BF16)
            e_cb, e_pb, _, _, _, inv = _attn_exp(qn, kc, kp, sk_ref[g], n)
            o = (_dot(e_cb, vc.astype(BF16)) + _dot(e_pb, vp.astype(BF16))) * inv[:, :HEAD_DIM]
            _unstack_heads(o_scr, o, g)
        o_ref[...] = o_scr[...].astype(BF16)

    blk = pl.BlockSpec((ATTN_BLOCK, W), lambda n: (n, 0))
    return pl.pallas_call(
        body, name="attn_fwd", grid=(nb,),
        in_specs=[blk, pl.BlockSpec(kv.shape, lambda n: (0, 0)), pl.BlockSpec((1, HEAD_DIM), lambda n: (0, 0)),
                  pl.BlockSpec((1, HEAD_DIM), lambda n: (0, 0)), pl.BlockSpec(sink_rows.shape, lambda n: (0, 0, 0))],
        out_specs=blk, out_shape=jax.ShapeDtypeStruct((S, W), BF16),
        scratch_shapes=[pltpu.VMEM((ATTN_BLOCK, W), F32)],
        compiler_params=_params(dimension_semantics=("arbitrary",)),
    )(q, kv, qw, kw, sink_rows)


def _attn_bwd(q, kv, do, qw, kw, sink_rows):
    S, W = q.shape
    KW = kv.shape[1]
    nb = S // ATTN_BLOCK
    chunk = 512

    def body(q_ref, kv_ref, do_ref, qw_ref, kw_ref, sk_ref, dq_ref, dkv_ref, dqw_ref, dkw_ref, dsk_ref, dq_scr):
        n = pl.program_id(0)

        @pl.when(n == 0)
        def _():
            dkv_ref[...] = jnp.zeros_like(dkv_ref)
            dqw_ref[...] = jnp.zeros_like(dqw_ref)
            dsk_ref[...] = jnp.zeros_like(dsk_ref)

        qf = q_ref[...].astype(F32)
        dof = do_ref[...].astype(F32)
        cur, prev = _attn_blocks(n)
        kvc = kv_ref[pl.ds(cur, ATTN_BLOCK), :].astype(F32)
        kvp = kv_ref[pl.ds(prev, ATTN_BLOCK), :].astype(F32)
        qw_v = qw_ref[...]
        for g in range(N_KV_HEADS):
            qs = _stack_heads(qf, g)
            rq = _rinv_lanes(qs)
            qhat = qs * rq
            qn = (qhat * qw_v * SCALE).astype(BF16)
            kc, vc = _kv_split(kvc, g)
            kp, vp = _kv_split(kvp, g)
            kc = (kc * _rinv_lanes(kc) * kw_ref[...]).astype(BF16)
            kp = (kp * _rinv_lanes(kp) * kw_ref[...]).astype(BF16)
            vc = vc.astype(BF16)
            vp = vp.astype(BF16)
            dos = _stack_heads(dof, g).astype(BF16)
            _, _, e_c, e_p, e_s, inv = _attn_exp(qn, kc, kp, sk_ref[g], n)
            p_c, p_p, p_s = e_c * inv, e_p * inv, e_s * inv
            dp_c = _dot_nt(dos, vc)
            dp_p = _dot_nt(dos, vp)
            drow = _rowsum(jnp.concatenate([(p_c * dp_c).astype(BF16), (p_p * dp_p).astype(BF16)], axis=1), ATTN_BLOCK)
            ds_c = (p_c * (dp_c - drow)).astype(BF16)
            ds_p = (p_p * (dp_p - drow)).astype(BF16)
            dsink = -(p_s * drow)
            for hh in range(GQA_GROUP):
                tot = jnp.sum(dsink[hh * ATTN_BLOCK:(hh + 1) * ATTN_BLOCK, :], axis=0, keepdims=True)
                dsk_ref[g, hh:hh + 1, :] += tot
            dqn = (_dot(ds_c, kc) + _dot(ds_p, kp)) * SCALE
            klo, vlo = g * HEAD_DIM, (N_KV_HEADS + g) * HEAD_DIM
            dkv_ref[pl.ds(cur, ATTN_BLOCK), klo:klo + HEAD_DIM] += _dot_tn(ds_c, qn)
            dkv_ref[pl.ds(prev, ATTN_BLOCK), klo:klo + HEAD_DIM] += _dot_tn(ds_p, qn)
            dkv_ref[pl.ds(cur, ATTN_BLOCK), vlo:vlo + HEAD_DIM] += _dot_tn(p_c.astype(BF16), dos)
            dkv_ref[pl.ds(prev, ATTN_BLOCK), vlo:vlo + HEAD_DIM] += _dot_tn(p_p.astype(BF16), dos)
            dqw_ref[...] += jnp.sum(dqn * qhat, axis=0, keepdims=True)
            z = dqn * qw_v
            dqs = rq * (z - qhat * (_rowsum((z * qhat).astype(BF16), HEAD_DIM) * (1.0 / HEAD_DIM)))
            _unstack_heads(dq_scr, dqs, g)
        dq_ref[...] = dq_scr[...].astype(BF16)

        @pl.when(n == nb - 1)
        def _():
            def one(c, dkw):
                rows = pl.ds(pl.multiple_of(c * chunk, chunk), chunk)
                for g in range(N_KV_HEADS):
                    lo = g * HEAD_DIM
                    k = kv_ref[rows, lo:lo + HEAD_DIM].astype(F32)
                    dx, dg_rows = _rms_bwd(dkv_ref[rows, lo:lo + HEAD_DIM], k, kw_ref[...])
                    dkv_ref[rows, lo:lo + HEAD_DIM] = dx
                    dkw = dkw + jnp.sum(dg_rows, axis=0, keepdims=True)
                return dkw

            dkw_ref[...] = lax.fori_loop(0, S // chunk, one, jnp.zeros((1, HEAD_DIM), F32))

    blk = pl.BlockSpec((ATTN_BLOCK, W), lambda n: (n, 0))
    whole_kv = pl.BlockSpec((S, KW), lambda n: (0, 0))
    vec = pl.BlockSpec((1, HEAD_DIM), lambda n: (0, 0))
    sk = pl.BlockSpec(sink_rows.shape, lambda n: (0, 0, 0))
    dsk = pl.BlockSpec((N_KV_HEADS, GQA_GROUP, 128), lambda n: (0, 0, 0))
    return pl.pallas_call(
        body, name="attn_bwd", grid=(nb,), in_specs=[blk, whole_kv, blk, vec, vec, sk],
        out_specs=[blk, whole_kv, vec, vec, dsk],
        out_shape=[jax.ShapeDtypeStruct((S, W), BF16), jax.ShapeDtypeStruct((S, KW), F32),
                   jax.ShapeDtypeStruct((1, HEAD_DIM), F32), jax.ShapeDtypeStruct((1, HEAD_DIM), F32),
                   jax.ShapeDtypeStruct((N_KV_HEADS, GQA_GROUP, 128), F32)],
        scratch_shapes=[pltpu.VMEM((ATTN_BLOCK, W), F32)],
        compiler_params=_params(dimension_semantics=("arbitrary",)),
    )(q, kv, do, qw, kw, sink_rows)


def _pooled(xc, xprev, i):
    tm = xc.shape[0]
    xh = jnp.concatenate([jnp.where(i > 0, xprev, 0.0), xc], axis=0)
    t = lax.broadcasted_iota(jnp.int32, (tm, 1), 0) + i * tm
    out = []
    for gi, w in enumerate(POOL_WINDOWS):
        acc = xh[:, gi * POOL_GROUP:(gi + 1) * POOL_GROUP]
        sh = 1
        while sh < w:
            acc = acc + pltpu.roll(acc, sh, 0)
            sh *= 2
        cnt = jnp.minimum(t + 1, w).astype(F32)
        out.append(acc[POOL_HALO:, :] / cnt - xc[:, gi * POOL_GROUP:(gi + 1) * POOL_GROUP])
    return jnp.concatenate(out, axis=1)


def _pool_mix(pooled_b, pw_ref):
    return jnp.concatenate([_dot(pooled_b[:, gi * POOL_GROUP:(gi + 1) * POOL_GROUP], pw_ref[gi])
                            for gi in range(len(POOL_WINDOWS))], axis=1)


def _halo_specs(tm, width, S, after):
    per = tm // POOL_HALO
    last = S // POOL_HALO - 1
    if after:
        return pl.BlockSpec((POOL_HALO, width), lambda i: (jnp.minimum((i + 1) * per, last), 0))
    return pl.BlockSpec((POOL_HALO, width), lambda i: (jnp.maximum(i * per - 1, 0), 0))


def _mix_out(xp, attn, gl, bias, pw, pscale, wpot, wao, wo, h, tm=512):
    S, D = h.shape
    nt = S // tm
    PW = xp.shape[1]

    def body(xc_ref, xprev_ref, at_ref, gl_ref, bias_ref, pw_ref, ps_ref, wpo_ref, wao_ref, wo_ref, h_ref,
             ho_ref, bp_ref, ba_ref):
        i = pl.program_id(0)
        pooled = _pooled(xc_ref[...], xprev_ref[...], i).astype(BF16)
        ms = (_pool_mix(pooled, pw_ref) * ps_ref[...]).astype(BF16)
        bp = _dot_nt(ms, wpo_ref[...])
        ba = _dot(at_ref[...], wao_ref[...])
        bp_ref[...] = bp.astype(BF16)
        ba_ref[...] = ba.astype(BF16)
        gates = jax.nn.sigmoid(gl_ref[...].astype(F32) + bias_ref[...])
        merged = (gates[:, :D] * bp + gates[:, D:] * ba).astype(BF16)
        ho_ref[...] = h_ref[...] + _dot(merged, wo_ref[...])

    def row(width):
        return pl.BlockSpec((tm, width), lambda i: (i, 0))

    def whole(x):
        nd = x.ndim
        return pl.BlockSpec(x.shape, lambda i: (0,) * nd)

    return pl.pallas_call(
        body, name="mix_out", grid=(nt,),
        in_specs=[row(PW), _halo_specs(tm, PW, S, False), row(D), row(2 * D), whole(bias), whole(pw), whole(pscale),
                  whole(wpot), whole(wao), whole(wo), row(D)],
        out_specs=[row(D), row(D), row(D)],
        out_shape=[jax.ShapeDtypeStruct((S, D), F32), jax.ShapeDtypeStruct((S, D), BF16),
                   jax.ShapeDtypeStruct((S, D), BF16)],
        compiler_params=_params(dimension_semantics=("arbitrary",)),
    )(xp, xp, attn, gl, bias, pw, pscale, wpot, wao, wo, h)


def _mix_bwd_gate(dh, wo, bp, ba, gl, bias, tm=512):
    S, D = dh.shape
    nt = S // tm

    def body(dh_ref, wo_ref, bp_ref, ba_ref, gl_ref, bias_ref, dgl_ref, dbp_ref, dba_ref, dwo_ref, dbias_ref):
        i = pl.program_id(0)
        dhb = dh_ref[...].astype(BF16)
        dm = _dot_nt(dhb, wo_ref[...])
        gates = jax.nn.sigmoid(gl_ref[...].astype(F32) + bias_ref[...])
        gp, ga = gates[:, :D], gates[:, D:]
        bp_v = bp_ref[...].astype(F32)
        ba_v = ba_ref[...].astype(F32)
        merged = (gp * bp_v + ga * ba_v).astype(BF16)
        _acc(dwo_ref, _dot_tn(merged, dhb), i == 0)
        dbp_ref[...] = (dm * gp).astype(BF16)
        dba_ref[...] = (dm * ga).astype(BF16)
        dgl = jnp.concatenate([dm * bp_v * gp * (1.0 - gp), dm * ba_v * ga * (1.0 - ga)], axis=1)
        dgl_ref[...] = dgl.astype(BF16)
        _acc(dbias_ref, jnp.sum(dgl, axis=0, keepdims=True), i == 0)

    def row(width):
        return pl.BlockSpec((tm, width), lambda i: (i, 0))

    def whole(shape):
        return pl.BlockSpec(shape, lambda i: (0, 0))

    return pl.pallas_call(
        body, name="mix_bwd_gate", grid=(nt,),
        in_specs=[row(D), whole(wo.shape), row(D), row(D), row(2 * D), whole(bias.shape)],
        out_specs=[row(2 * D), row(D), row(D), whole((D, D)), whole((1, 2 * D))],
        out_shape=[jax.ShapeDtypeStruct((S, 2 * D), BF16), jax.ShapeDtypeStruct((S, D), BF16),
                   jax.ShapeDtypeStruct((S, D), BF16), jax.ShapeDtypeStruct((D, D), F32),
                   jax.ShapeDtypeStruct((1, 2 * D), F32)],
        compiler_params=_params(dimension_semantics=("arbitrary",)),
    )(dh, wo, bp, ba, gl, bias)


def _mix_bwd_branch(dbp, dba, attn, xp, pw, pscale, wpot, wao, tm=1024):
    S, D = dbp.shape
    nt = S // tm
    PW = xp.shape[1]
    NG = len(POOL_WINDOWS)

    def body(dbp_ref, dba_ref, at_ref, xc_ref, xprev_ref, pw_ref, ps_ref, wpo_ref, wao_ref,
             dat_ref, dpl_ref, dwao_ref, dwpo_ref, dpw_ref, dps_ref):
        i = pl.program_id(0)
        dba_v = dba_ref[...]
        dbp_v = dbp_ref[...]
        _acc(dwao_ref, _dot_tn(at_ref[...], dba_v), i == 0)
        dat_ref[...] = _dot_nt(dba_v, wao_ref[...]).astype(BF16)
        pooled = _pooled(xc_ref[...], xprev_ref[...], i).astype(BF16)
        mixed = _pool_mix(pooled, pw_ref)
        ps = ps_ref[...]
        _acc(dwpo_ref, _dot_tn(dbp_v, (mixed * ps).astype(BF16)), i == 0)
        dms = _dot(dbp_v, wpo_ref[...])
        _acc(dps_ref, jnp.sum(dms * mixed, axis=0, keepdims=True), i == 0)
        dmixed = (dms * ps).astype(BF16)
        dpooled = []
        for gi in range(NG):
            cols = slice(gi * POOL_GROUP, (gi + 1) * POOL_GROUP)
            _acc(dpw_ref.at[gi], _dot_tn(pooled[:, cols], dmixed[:, cols]), i == 0)
            dpooled.append(_dot_nt(dmixed[:, cols], pw_ref[gi]))
        dpl_ref[...] = jnp.concatenate(dpooled, axis=1)

    def row(width):
        return pl.BlockSpec((tm, width), lambda i: (i, 0))

    def whole(shape):
        nd = len(shape)
        return pl.BlockSpec(shape, lambda i: (0,) * nd)

    return pl.pallas_call(
        body, name="mix_bwd_branch", grid=(nt,),
        in_specs=[row(D), row(D), row(D), row(PW), _halo_specs(tm, PW, S, False), whole(pw.shape), whole(pscale.shape),
                  whole(wpot.shape), whole(wao.shape)],
        out_specs=[row(D), row(PW), whole((D, D)), whole((D, PW)), whole(pw.shape), whole((1, PW))],
        out_shape=[jax.ShapeDtypeStruct((S, D), BF16), jax.ShapeDtypeStruct((S, PW), F32),
                   jax.ShapeDtypeStruct((D, D), F32), jax.ShapeDtypeStruct((D, PW), F32),
                   jax.ShapeDtypeStruct(pw.shape, F32), jax.ShapeDtypeStruct((1, PW), F32)],
        compiler_params=_params(dimension_semantics=("arbitrary",)),
    )(dbp, dba, attn, xp, xp, pw, pscale, wpot, wao)


def _pool_bwd(dpooled, tm=512):
    S, PW = dpooled.shape
    nt = S // tm

    def body(dc_ref, dnext_ref, dxp_ref):
        i = pl.program_id(0)
        dc = dc_ref[...]
        dh = jnp.concatenate([dc, jnp.where(i < nt - 1, dnext_ref[...], 0.0)], axis=0)
        rows = tm + POOL_HALO
        t = lax.broadcasted_iota(jnp.int32, (rows, 1), 0) + i * tm
        out = []
        for gi, w in enumerate(POOL_WINDOWS):
            cols = slice(gi * POOL_GROUP, (gi + 1) * POOL_GROUP)
            acc = dh[:, cols] / jnp.minimum(t + 1, w).astype(F32)
            sh = 1
            while sh < w:
                acc = acc + pltpu.roll(acc, rows - sh, 0)
                sh *= 2
            out.append(acc[:tm, :] - dc[:, cols])
        dxp_ref[...] = jnp.concatenate(out, axis=1).astype(BF16)

    return pl.pallas_call(
        body, name="pool_bwd", grid=(nt,),
        in_specs=[pl.BlockSpec((tm, PW), lambda i: (i, 0)), _halo_specs(tm, PW, S, True)],
        out_specs=pl.BlockSpec((tm, PW), lambda i: (i, 0)), out_shape=jax.ShapeDtypeStruct((S, PW), BF16),
        compiler_params=_params(dimension_semantics=("arbitrary",)),
    )(dpooled, dpooled)


def _in_bwd_w(pieces, u, tm=1024):
    S, D = u.shape
    nt = S // tm
    NW = sum(width for _, width in SEGMENTS)

    def body(*refs):
        piece_refs, u_ref, dw_hbm, acc, sem = refs[:len(SEGMENTS)], refs[len(SEGMENTS)], refs[len(SEGMENTS) + 1], refs[-2], refs[-1]
        i = pl.program_id(0)
        u_t = u_ref[...]
        for (off, width), ref in zip(SEGMENTS, piece_refs):
            for lo in range(0, width, D):
                hi = min(lo + D, width)
                _acc(acc.at[off + lo:off + hi, :], _dot_tn(ref[:, lo:hi].astype(BF16), u_t), i == 0)

        @pl.when(i == nt - 1)
        def _():
            cp = pltpu.make_async_copy(acc, dw_hbm, sem)
            cp.start()
            cp.wait()

    return pl.pallas_call(
        body, name="in_bwd_w", grid=(nt,),
        in_specs=[pl.BlockSpec((tm, width), lambda i: (i, 0)) for _, width in SEGMENTS] + [pl.BlockSpec((tm, D), lambda i: (i, 0))],
        out_specs=pl.BlockSpec(memory_space=pl.ANY), out_shape=jax.ShapeDtypeStruct((NW, D), F32),
        scratch_shapes=[pltpu.VMEM((NW, D), F32), pltpu.SemaphoreType.DMA],
        compiler_params=_params(dimension_semantics=("arbitrary",)),
    )(*pieces, u)


def _in_bwd_x(pieces, wint, h, dh, gnorm, tm=512):
    S, D = h.shape
    nt = S // tm

    def body(*refs):
        piece_refs = refs[:len(SEGMENTS)]
        w_ref, h_ref, dh_ref, g_ref, dx_ref, dxh_ref, dg_ref = refs[len(SEGMENTS):]
        i = pl.program_id(0)
        du = jnp.zeros((tm, D), F32)
        for (off, width), ref in zip(SEGMENTS, piece_refs):
            du = du + _dot(ref[...].astype(BF16), w_ref[off:off + width, :])
        dx, dg_rows = _rms_bwd(du, h_ref[...], g_ref[...])
        out = dh_ref[...] + dx
        dx_ref[...] = out
        dxh_ref[...] = (0.5 * out).astype(BF16)
        _acc(dg_ref, jnp.sum(dg_rows, axis=0, keepdims=True), i == 0)

    def row(width):
        return pl.BlockSpec((tm, width), lambda i: (i, 0))

    vec = pl.BlockSpec((1, D), lambda i: (0, 0))
    return pl.pallas_call(
        body, name="in_bwd_x", grid=(nt,),
        in_specs=[row(width) for _, width in SEGMENTS] + [pl.BlockSpec(wint.shape, lambda i: (0, 0)), row(D), row(D), vec],
        out_specs=[row(D), row(D), vec],
        out_shape=[jax.ShapeDtypeStruct((S, D), F32), jax.ShapeDtypeStruct((S, D), BF16), jax.ShapeDtypeStruct((1, D), F32)],
        compiler_params=_params(dimension_semantics=("arbitrary",)),
    )(*pieces, wint, h, dh, gnorm)


def _row_tile(rows):
    for t in (512, 480, 352, 256, 128, 64, 32, 16, 8):
        if rows % t == 0:
            return t
    return rows


def _adamw(w, g, m, v, token, name):
    R, C = w.shape
    tr = _row_tile(R)

    def body(w_ref, g_ref, m_ref, v_ref, _, d_ref, mo_ref, vo_ref, go_ref):
        gv = g_ref[...]
        go_ref[...] = gv
        mn = ADAM_B1 * m_ref[...] + (1.0 - ADAM_B1) * gv
        vn = ADAM_B2 * v_ref[...] + (1.0 - ADAM_B2) * (gv * gv)
        m_hat = mn / (1.0 - ADAM_B1 ** ADAM_STEP)
        v_hat = vn / (1.0 - ADAM_B2 ** ADAM_STEP)
        d_ref[...] = -ADAM_LR * (m_hat / (jnp.sqrt(v_hat) + ADAM_EPS) + ADAM_WD * w_ref[...])
        mo_ref[...] = mn
        vo_ref[...] = vn

    blk = pl.BlockSpec((tr, C), lambda i: (i, 0))
    sh = jax.ShapeDtypeStruct((R, C), F32)
    return pl.pallas_call(
        body, name=name, grid=(R // tr,), in_specs=[blk] * 4 + [TOKEN_SPEC], out_specs=[blk] * 4, out_shape=[sh] * 4,
        compiler_params=_params(dimension_semantics=("arbitrary",)),
    )(w, g, m, v, token)


def _cast_place(w, place, name):
    R, C = w.shape
    tr = _row_tile(R)
    per = R // tr

    def body(p_ref, w_ref, o_ref):
        o_ref[...] = w_ref[...].astype(BF16)

    grid_spec = pltpu.PrefetchScalarGridSpec(
        num_scalar_prefetch=1, grid=(per,),
        in_specs=[pl.BlockSpec((tr, C), lambda i, p: (i, 0))],
        out_specs=pl.BlockSpec((tr, C), lambda i, p: (p[0] * per + i, 0)))
    return pl.pallas_call(
        body, name=name, grid_spec=grid_spec, out_shape=jax.ShapeDtypeStruct((N_SHARDS * R, C), BF16),
        compiler_params=_params(dimension_semantics=("arbitrary",)),
    )(place, w)


def _parked(w, m, idx, first, last):
    return jnp.where(m == w, idx, jnp.where(m < w, first, last))


def _sum_halves(g4s, recvs, place, name):
    M = len(g4s)
    NS, R, C = g4s[0].shape
    hr = R // 2
    tr = _row_tile(hr)
    per = hr // tr

    def body(p_ref, *refs):
        m = pl.program_id(0)
        for w in range(M):
            @pl.when(m == w)
            def _(w=w):
                refs[2 * M + w][...] = (refs[w][...] + refs[M + w][...]).astype(BF16)

    def spec(w, mine):
        def index(m, s, i, p):
            row = _parked(w, m, i, 0, per - 1)
            return _parked(w, m, s, 0, NS - 1), (p[1] * per + row if mine else row), 0

        return pl.BlockSpec((1, tr, C), index)

    grid_spec = pltpu.PrefetchScalarGridSpec(
        num_scalar_prefetch=1, grid=(M, NS, per),
        in_specs=[spec(w, True) for w in range(M)] + [spec(w, False) for w in range(M)],
        out_specs=[spec(w, False) for w in range(M)])
    return pl.pallas_call(
        body, name=name, grid_spec=grid_spec, out_shape=[jax.ShapeDtypeStruct((NS, hr, C), BF16)] * M,
        compiler_params=_params(dimension_semantics=("arbitrary",) * 3),
    )(place, *g4s, *recvs)


def _sum_quarters(h4s, recv3s, place, name):
    M = len(h4s)
    NS, hr, C = h4s[0].shape
    tr = _row_tile(hr)
    per = hr // tr

    def body(p_ref, *refs):
        m = pl.program_id(0)
        for w in range(M):
            @pl.when(m == w)
            def _(w=w):
                acc = refs[w][0].astype(F32)
                for k in range(N_SHARDS - 1):
                    acc = acc + refs[M + w][k].astype(F32)
                refs[2 * M + w][...] = acc

    def row(w, m, i):
        return _parked(w, m, i, 0, per - 1)

    grid_spec = pltpu.PrefetchScalarGridSpec(
        num_scalar_prefetch=1, grid=(M, per),
        in_specs=[pl.BlockSpec((1, tr, C), lambda m, i, p, w=w: (p[0], row(w, m, i), 0)) for w in range(M)]
        + [pl.BlockSpec((N_SHARDS - 1, tr, C), lambda m, i, p, w=w: (0, row(w, m, i), 0)) for w in range(M)],
        out_specs=[pl.BlockSpec((tr, C), lambda m, i, p, w=w: (p[1] * per + row(w, m, i), 0)) for w in range(M)])
    return pl.pallas_call(
        body, name=name, grid_spec=grid_spec, out_shape=[jax.ShapeDtypeStruct((2 * hr, C), F32)] * M,
        compiler_params=_params(dimension_semantics=("arbitrary",) * 2),
    )(place, *h4s, *recv3s)


def _by_shape(arrays):
    groups = {}
    for k, a in enumerate(arrays):
        groups.setdefault(a.shape, []).append(k)
    return list(groups.values())


def _place():
    x, y, c = lax.axis_index("x"), lax.axis_index("y"), lax.axis_index("c")
    chips = [(1 - x, y), (x, 1 - y), (1 - x, 1 - y)]
    return x, y, c, chips


HBM_SPEC = pl.BlockSpec(memory_space=pltpu.HBM)
SEM_SPEC = pl.BlockSpec(memory_space=pltpu.SEMAPHORE)
DATAFLOW = pltpu.SideEffectType.DATAFLOW_SIDE_EFFECTING


def _hbm(a):
    return pltpu.with_memory_space_constraint(a, pltpu.HBM)


class InFlight(NamedTuple):
    send_sem: jax.Array
    recv_sem: jax.Array
    bufs: list
    plan: Callable
    token: jax.Array


def _wait_all(plan, refs, send_ref, recv_ref):
    for k, (src, dst, dev) in enumerate(plan(refs)):
        cp = pltpu.make_async_remote_copy(src_ref=src, dst_ref=dst, send_sem=send_ref.at[k], recv_sem=recv_ref.at[k],
                                          device_id=dev, device_id_type=MESH)
        cp.wait_send()
        cp.wait_recv()


def _start_all(plan, refs, send_ref, recv_ref):
    for k, (src, dst, dev) in enumerate(plan(refs)):
        pltpu.make_async_remote_copy(src_ref=src, dst_ref=dst, send_sem=send_ref.at[k], recv_sem=recv_ref.at[k],
                                     device_id=dev, device_id_type=MESH).start()


def _split_start(name, bufs, plan, n_copies, after, waits=None):
    nb = len(bufs)
    after = list(after) if isinstance(after, (list, tuple)) else [after]

    def body(*refs):
        ins = refs[:nb]
        if waits is not None:
            _wait_all(waits.plan, ins, refs[nb], refs[nb + 1])
        send_sem, recv_sem, token = refs[-nb - 3], refs[-nb - 2], refs[-1]
        _start_all(plan, ins, send_sem, recv_sem)
        token[...] = jnp.zeros_like(token)

    sems = pltpu.SemaphoreType.DMA((n_copies,))
    earlier = [] if waits is None else [waits.send_sem, waits.recv_sem]
    out = pl.pallas_call(
        body, name=name, in_specs=[HBM_SPEC] * nb + [SEM_SPEC] * len(earlier) + [TOKEN_SPEC] * len(after),
        out_shape=(sems, sems, *[pltpu.HBM(b.shape, b.dtype) for b in bufs], jax.ShapeDtypeStruct((8, 128), F32)),
        out_specs=(SEM_SPEC, SEM_SPEC, *[HBM_SPEC] * nb, pl.BlockSpec(memory_space=pltpu.VMEM)),
        input_output_aliases={i: i + 2 for i in range(nb)},
        compiler_params=pltpu.CompilerParams(has_side_effects=DATAFLOW),
    )(*[_hbm(b) for b in bufs], *earlier, *after)
    return InFlight(out[0], out[1], list(out[2:2 + nb]), plan, out[-1])


def _split_wait(name, flight, after):
    nb = len(flight.bufs)
    after = list(after) if isinstance(after, (list, tuple)) else [after]

    def body(*refs):
        _wait_all(flight.plan, refs[:nb], refs[nb], refs[nb + 1])

    out = pl.pallas_call(
        body, name=name, in_specs=[HBM_SPEC] * nb + [SEM_SPEC, SEM_SPEC] + [TOKEN_SPEC] * len(after),
        out_shape=[pltpu.HBM(b.shape, b.dtype) for b in flight.bufs], out_specs=[HBM_SPEC] * nb,
        input_output_aliases={i: i for i in range(nb)},
        compiler_params=pltpu.CompilerParams(has_side_effects=DATAFLOW),
    )(*flight.bufs, flight.send_sem, flight.recv_sem, *after)
    return list(out)


def _half_rows(buf, chip, core):
    hr = buf.shape[0] // (2 * N_SHARDS)
    return buf.at[pl.ds(pl.multiple_of((2 * chip + core) * hr, 16), hr)]


def _gather_ici_plan(bufs):
    x, y, c, chips = _place()
    return [(_half_rows(b, 2 * x + y, c), _half_rows(b, 2 * x + y, c), (px, py, c)) for b in bufs for px, py in chips]


def _gather_d2d_plan(bufs):
    x, y, c, chips = _place()
    return [(_half_rows(b, 2 * px + py, c), _half_rows(b, 2 * px + py, c), (x, y, 1 - c)) for b in bufs for px, py in chips]


def _swap_plan(bufs):
    x, y, c, _ = _place()
    n = len(bufs) // 2
    copies = []
    for g, land in zip(bufs[:n], bufs[n:]):
        hr = g.shape[1] // 2
        copies.append((g.at[:, pl.ds(pl.multiple_of((1 - c) * hr, 8), hr)], land, (x, y, 1 - c)))
    return copies


def _exchange_plan(bufs):
    x, y, c, chips = _place()
    n = len(bufs) // 2
    return [(h.at[2 * px + py], land.at[k], (px, py, c))
            for h, land in zip(bufs[:n], bufs[n:]) for k, (px, py) in enumerate(chips)]


def _share_plan(bufs):
    x, y, c, _ = _place()
    copies = []
    for buf in bufs:
        hr = buf.shape[0] // 2
        mine = buf.at[pl.ds(pl.multiple_of(c * hr, 8), hr)]
        copies.append((mine, mine, (x, y, 1 - c)))
    return copies


N_DEVICES = 8


def _slot_place(vec, place):
    R, C = vec.shape

    def body(p_ref, v_ref, o_ref):
        o_ref[0] = v_ref[...]

    grid_spec = pltpu.PrefetchScalarGridSpec(
        num_scalar_prefetch=1, grid=(1,), in_specs=[pl.BlockSpec((R, C), lambda i, p: (0, 0))],
        out_specs=pl.BlockSpec((1, R, C), lambda i, p: (2 * p[0] + p[1], 0, 0)))
    return pl.pallas_call(
        body, name="slot_place", grid_spec=grid_spec, out_shape=jax.ShapeDtypeStruct((N_DEVICES, R, C), F32),
        compiler_params=_params(dimension_semantics=("arbitrary",)),
    )(place, vec)


def _slots_plan(bufs):
    x, y, c, _ = _place()
    mine = bufs[0].at[4 * x + 2 * y + c]
    return [(mine, mine, (x ^ (r >> 2), y ^ ((r >> 1) & 1), c ^ (r & 1))) for r in range(1, N_DEVICES)]


def _sum_slots(slots):
    _, R, C = slots.shape

    def body(s_ref, o_ref):
        acc = s_ref[0]
        for d in range(1, N_DEVICES):
            acc = acc + s_ref[d]
        o_ref[...] = acc

    return pl.pallas_call(
        body, name="sum_slots", in_specs=[pl.BlockSpec(memory_space=pltpu.VMEM)],
        out_specs=pl.BlockSpec(memory_space=pltpu.VMEM), out_shape=jax.ShapeDtypeStruct((R, C), F32),
        compiler_params=_params(),
    )(slots)


SMALL = ("ffn1_norm", "mix_norm", "pool_w", "pool_scale", "q_norm", "k_norm", "sinks", "gate_bias", "ffn2_norm")
SMALL_COLS = 1024
FFN1 = ("ffn1_w_gate", "ffn1_w_up", "ffn1_w_down")
MIXER = ("w_in", "w_pool_out", "w_attn_out", "w_out")
FFN2 = ("ffn2_w_gate", "ffn2_w_up", "ffn2_w_down")
LARGE = FFN1 + MIXER + FFN2
TRANSPOSED = ("ffn1_w_gate", "ffn1_w_up", "w_in", "w_pool_out", "ffn2_w_gate", "ffn2_w_up")
WEIGHTS = ("ffn1_norm", "ffn1_w_gate", "ffn1_w_up", "ffn1_w_down", "mix_norm", "w_in", "pool_w", "pool_scale",
           "w_pool_out", "q_norm", "k_norm", "sinks", "w_attn_out", "gate_bias", "w_out", "ffn2_norm",
           "ffn2_w_gate", "ffn2_w_up", "ffn2_w_down")


def _pack_small(parts):
    flat = jnp.concatenate([p.reshape(-1) for p in parts])
    rows = -(-flat.shape[0] // (8 * SMALL_COLS)) * 8
    return jnp.pad(flat, (0, rows * SMALL_COLS - flat.shape[0])).reshape(rows, SMALL_COLS)


def _unpack_small(packed, like):
    flat = packed.reshape(-1)
    out, off = [], 0
    for p in like:
        out.append(flat[off:off + p.size].reshape(p.shape))
        off += p.size
    return out, flat[off]


def _tie(small, token):
    return small + token[0, 0]


class Reduction:
    def __init__(self, group, names, grads, place):
        self.group, self.names, self.place = group, names, place
        self.bufs = [g.reshape(N_SHARDS, -1, g.shape[-1]) for g in grads]
        self.flight = None

    def _start(self, stage, bufs, plan, n_copies, after):
        self.flight = _split_start(f"{self.group}_{stage}", bufs, plan, n_copies, after)
        return self.flight.token

    def _landed(self, stage, after):
        n = len(self.names)
        bufs = _split_wait(f"{self.group}_{stage}_wait", self.flight, after)
        return bufs[:n], bufs[n:]

    def start_swap(self, after):
        lands = [lax.empty((g.shape[0], g.shape[1] // 2, g.shape[2]), g.dtype) for g in self.bufs]
        return self._start("swap", self.bufs + lands, _swap_plan, len(self.bufs), after)

    def start_exchange(self, after):
        g4, recv = self._landed("swap", after)
        halves = self._per_shape(_sum_halves, "sum_halves", g4, recv)
        lands = [lax.empty((N_SHARDS - 1,) + h.shape[1:], h.dtype) for h in halves]
        return self._start("exchange", halves + lands, _exchange_plan, (N_SHARDS - 1) * len(halves), self.place)

    def start_share(self, after):
        halves, recv3 = self._landed("exchange", after)
        reduced = self._per_shape(_sum_quarters, "sum_quarters", halves, recv3)
        return self._start("share", reduced, _share_plan, len(reduced), self.place)

    def _per_shape(self, add, stage, mine, received):
        out = [None] * len(mine)
        for idx in _by_shape(mine):
            sums = add([mine[k] for k in idx], [received[k] for k in idx], self.place, f"{stage}_{self.names[idx[0]]}")
            for k, v in zip(idx, sums):
                out[k] = v
        return out

    def finish(self, after):
        return dict(zip(self.names, _split_wait(f"{self.group}_share_wait", self.flight, after)))


def _row_form(name, a):
    return a.T if name in TRANSPOSED else a


def kernel(x, ffn1_norm, ffn1_w_gate, ffn1_w_up, ffn1_w_down, mix_norm, w_in, pool_w, pool_scale, w_pool_out, q_norm, k_norm, sinks, w_attn_out, gate_bias, w_out, ffn2_norm, ffn2_w_gate, ffn2_w_up, ffn2_w_down, loss_target, m_ffn1_norm, m_ffn1_w_gate, m_ffn1_w_up, m_ffn1_w_down, m_mix_norm, m_w_in, m_pool_w, m_pool_scale, m_w_pool_out, m_q_norm, m_k_norm, m_sinks, m_w_attn_out, m_gate_bias, m_w_out, m_ffn2_norm, m_ffn2_w_gate, m_ffn2_w_up, m_ffn2_w_down, v_ffn1_norm, v_ffn1_w_gate, v_ffn1_w_up, v_ffn1_w_down, v_mix_norm, v_w_in, v_pool_w, v_pool_scale, v_w_pool_out, v_q_norm, v_k_norm, v_sinks, v_w_attn_out, v_gate_bias, v_w_out, v_ffn2_norm, v_ffn2_w_gate, v_ffn2_w_up, v_ffn2_w_down):
    args = dict(locals())
    wts = {n: _row_form(n, args[n]) for n in WEIGHTS}
    mom = {n: _row_form(n, args["m_" + n]) for n in WEIGHTS}
    var = {n: _row_form(n, args["v_" + n]) for n in WEIGHTS}
    shard = 2 * lax.axis_index("x") + lax.axis_index("y")
    place = jnp.stack([shard, lax.axis_index("c")]).astype(jnp.int32)

    xs, target = x[0], loss_target[0]
    D = xs.shape[1]
    g1 = wts["ffn1_norm"].reshape(1, D)
    gm = wts["mix_norm"].reshape(1, D)
    g2 = wts["ffn2_norm"].reshape(1, D)
    qw = wts["q_norm"].reshape(1, HEAD_DIM)
    kw = wts["k_norm"].reshape(1, HEAD_DIM)
    bias = wts["gate_bias"].reshape(1, 2 * D)
    pscale = wts["pool_scale"].reshape(1, -1)
    pw = wts["pool_w"].astype(BF16)
    sink_rows = jnp.broadcast_to(jnp.repeat(wts["sinks"], ATTN_BLOCK).reshape(N_KV_HEADS, GQA_GROUP * ATTN_BLOCK, 1),
                                 (N_KV_HEADS, GQA_GROUP * ATTN_BLOCK, 128))

    first, down1 = FFN1[:2], FFN1[2:]
    placed = {n: _cast_place(wts[n], place, "cast_" + n) for n in FFN1}
    first_ici = _split_start("gather_first_ici", [placed[n] for n in first], _gather_ici_plan, 3 * len(first), place)
    placed.update({n: _cast_place(wts[n], place, "cast_" + n) for n in MIXER + FFN2})
    zero = jnp.zeros((1,), F32)
    packed = [_pack_small([d[n] for n in SMALL] + [zero]) for d in (wts, mom, var)]
    shadow = [placed[n] for n in down1 + MIXER + FFN2] + packed + [sink_rows, pw]
    first_d2d = _split_start("gather_first_d2d", first_ici.bufs, _gather_d2d_plan, 3 * len(first), shadow, waits=first_ici)
    down1_ici = _split_start("gather_down1_ici", [placed[n] for n in down1], _gather_ici_plan, 3 * len(down1), first_d2d.token)
    w = dict(zip(first, _split_wait("gather_first_wait", first_d2d, down1_ici.token)))
    mix_ici = _split_start("gather_mix_ici", [placed[n] for n in MIXER], _gather_ici_plan, 3 * len(MIXER), w[first[0]])
    ffn2_ici = _split_start("gather_ffn2_ici", [placed[n] for n in FFN2], _gather_ici_plan, 3 * len(FFN2), mix_ici.token)

    n1, a1, b1, s1 = _ffn_up(xs, _tie(g1, ffn2_ici.token), w["ffn1_w_gate"], w["ffn1_w_up"], "ffn1_up")
    down1_d2d = _split_start("gather_down1_d2d", down1_ici.bufs, _gather_d2d_plan, 3 * len(down1), s1, waits=down1_ici)
    w.update(zip(down1, _split_wait("gather_down1_wait", down1_d2d, down1_d2d.token)))
    h1 = _ffn_down(xs, s1, w["ffn1_w_down"], "ffn1_down")
    mix_d2d = _split_start("gather_mix_d2d", mix_ici.bufs, _gather_d2d_plan, 3 * len(MIXER), h1, waits=mix_ici)
    w.update(zip(MIXER, _split_wait("gather_mix_wait", mix_d2d, mix_d2d.token)))
    u, xp, q, kv, gl = _mix_proj(h1, gm, w["w_in"])
    attn = _attn_fwd(q, kv, qw, kw, sink_rows)
    ffn2_d2d = _split_start("gather_ffn2_d2d", ffn2_ici.bufs, _gather_d2d_plan, 3 * len(FFN2), attn, waits=ffn2_ici)
    h2, bp, ba = _mix_out(xp, attn, gl, _tie(bias, ffn2_d2d.token), pw, pscale, w["w_pool_out"], w["w_attn_out"], w["w_out"], h1)
    w.update(zip(FFN2, _split_wait("gather_ffn2_wait", ffn2_d2d, h2)))
    n2, a2, b2, s2 = _ffn_up(h2, g2, w["ffn2_w_gate"], w["ffn2_w_up"], "ffn2_up")
    dy, dyh, loss = _ffn_down_loss(h2, s2, w["ffn2_w_down"], target, "ffn2_down_loss")

    gw, gs = {}, {}
    da, db = _ffn_bwd_h(dyh, a2, b2, w["ffn2_w_down"], ffn2_d2d.token, "ffn2_bwd_h")
    gw["ffn2_w_gate"], gw["ffn2_w_up"] = _xty(da, n2, "ffn2_dw_gate"), _xty(db, n2, "ffn2_dw_up")
    gw["ffn2_w_down"] = _xty(s2, dyh, "ffn2_dw_down")
    red2 = Reduction("reduce_ffn2", FFN2, [gw[n] for n in FFN2], place)
    token = red2.start_swap(da)
    dh2, gs["ffn2_norm"] = _ffn_bwd_x(da, db, w["ffn2_w_gate"], w["ffn2_w_up"], h2, dy, _tie(g2, token), "ffn2_bwd_x")
    token = red2.start_exchange(dh2)
    dgl, dbp, dba, gw["w_out"], gs["gate_bias"] = _mix_bwd_gate(dh2, w["w_out"], bp, ba, gl, _tie(bias, token))
    dattn, dpooled, gw["w_attn_out"], gw["w_pool_out"], gs["pool_w"], gs["pool_scale"] = _mix_bwd_branch(
        dbp, dba, attn, xp, pw, pscale, w["w_pool_out"], w["w_attn_out"])
    dq, dkv, gs["q_norm"], gs["k_norm"], dsk = _attn_bwd(q, kv, dattn, qw, kw, sink_rows)
    gs["sinks"] = dsk[:, :, 0]
    red2.start_share(dq)
    pieces = (_pool_bwd(dpooled), dq, dkv, dgl)
    gw["w_in"] = _in_bwd_w(pieces, u)
    grads = red2.finish(gw["w_in"])
    redm = Reduction("reduce_mix", MIXER, [gw[n] for n in MIXER], place)
    token = redm.start_swap(grads[FFN2[0]])
    dh1, dh1h, gs["mix_norm"] = _in_bwd_x(pieces, w["w_in"], h1, dh2, _tie(gm, token))
    token = redm.start_exchange(dh1)
    da, db = _ffn_bwd_h(dh1h, a1, b1, w["ffn1_w_down"], token, "ffn1_bwd_h")
    gw["ffn1_w_gate"], gw["ffn1_w_up"] = _xty(da, n1, "ffn1_dw_gate"), _xty(db, n1, "ffn1_dw_up")
    gw["ffn1_w_down"] = _xty(s1, dh1h, "ffn1_dw_down")
    token = redm.start_share([gw[n] for n in FFN1])
    red1 = Reduction("reduce_ffn1", FFN1, [gw[n] for n in FFN1], place)
    token = red1.start_swap(token)
    grad_x, gs["ffn1_norm"] = _ffn_bwd_x(da, db, w["ffn1_w_gate"], w["ffn1_w_up"], xs, dh1, _tie(g1, token), "ffn1_bwd_x")
    grads.update(redm.finish(grad_x))

    small_parts = [gs[n] for n in SMALL] + [loss[0, 0].reshape(1)]
    slots = _split_start("gather_small", [_slot_place(_pack_small(small_parts), place)], _slots_plan, N_DEVICES - 1,
                         grads[MIXER[0]])
    token = red1.start_exchange(slots.token)
    delta, new_m, new_v = {}, {}, {}
    for n in FFN2 + MIXER:
        delta[n], new_m[n], new_v[n], grads[n] = _adamw(wts[n], grads[n], mom[n], var[n], token, "adamw_" + n)
    summed = _sum_slots(_split_wait("gather_small_wait", slots, delta[MIXER[-1]])[0])
    small_grads, loss_sum = _unpack_small(summed, [wts[n] for n in SMALL])
    grads.update(dict(zip(SMALL, small_grads)))
    ds, ms, vs, _ = _adamw(packed[0], summed, packed[1], packed[2], token, "adamw_small")
    like = [wts[n] for n in SMALL]
    for out, packed_out in ((delta, ds), (new_m, ms), (new_v, vs)):
        out.update(dict(zip(SMALL, _unpack_small(packed_out, like)[0])))
    results = {(k, n): _row_form(n, d[n]) for k, d in enumerate((grads, delta, new_m, new_v)) for n in SMALL + FFN2 + MIXER}
    token = red1.start_share([results[k, n] for k in range(4) for n in SMALL + ("w_pool_out",)])
    grads.update(red1.finish(token))
    for n in FFN1:
        delta[n], new_m[n], new_v[n], grads[n] = _adamw(wts[n], grads[n], mom[n], var[n], token, "adamw_" + n)
    results.update({(k, n): _row_form(n, d[n]) for k, d in enumerate((grads, delta, new_m, new_v)) for n in FFN1})
    return (loss_sum, grad_x[None], *[results[k, n] for k in range(4) for n in WEIGHTS])
```

```python
from typing import Callable, NamedTuple

import jax
import jax.numpy as jnp
from jax import lax
from jax.experimental import pallas as pl
from jax.experimental.pallas import tpu as pltpu

F32 = jnp.float32
BF16 = jnp.bfloat16
RMS_EPS = 1e-6
POOL_WINDOWS = (2, 4, 8, 16)
POOL_GROUP = 128
POOL_HALO = 16
HEAD_DIM = 64
GQA_GROUP = 8
N_KV_HEADS = 2
ATTN_BLOCK = 128
SCALE = HEAD_DIM ** -0.5
NEG = -1e30
N_SHARDS = 4
ADAM_LR, ADAM_B1, ADAM_B2, ADAM_EPS, ADAM_WD, ADAM_STEP = 0.001, 0.9, 0.999, 1e-08, 0.01, 10
VMEM_LIMIT = 56 * 1024 * 1024
MESH = pl.DeviceIdType.MESH
SEG_POOL, SEG_Q, SEG_KV, SEG_GATE = (0, 512), (512, 1024), (1536, 256), (1792, 2048)
SEGMENTS = (SEG_POOL, SEG_Q, SEG_KV, SEG_GATE)


def _params(**kw):
    return pltpu.CompilerParams(vmem_limit_bytes=VMEM_LIMIT, **kw)


def _dot(a, b):
    return jnp.dot(a, b, preferred_element_type=F32)


def _dot_nt(a, b):
    return lax.dot_general(a, b, (((1,), (1,)), ((), ())), preferred_element_type=F32)


def _dot_tn(a, b):
    return lax.dot_general(a, b, (((0,), (0,)), ((), ())), preferred_element_type=F32)


def _rinv(x):
    return lax.rsqrt(jnp.mean(x * x, axis=-1, keepdims=True) + RMS_EPS)


def _rms_bwd(dn, x, g):
    r = _rinv(x)
    xr = x * r
    z = dn * g
    dx = r * (z - xr * jnp.mean(z * xr, axis=-1, keepdims=True))
    return dx, dn * xr


def _acc(ref, val, first):
    @pl.when(first)
    def _():
        ref[...] = val

    @pl.when(jnp.logical_not(first))
    def _():
        ref[...] += val


TOKEN_SPEC = pl.BlockSpec(memory_space=pl.ANY)
F_HALF = 1408


def _resident(w):
    return pl.BlockSpec(w.shape, lambda i: (0, 0), pipeline_mode=pl.Buffered(1))


def _ffn_up(h, gnorm, wgt, wut, name, tm=512):
    S, D = h.shape
    F = wgt.shape[0]

    def body(h_ref, g_ref, wg_ref, wu_ref, n_ref, a_ref, b_ref, s_ref):
        x = h_ref[...]
        n = (x * _rinv(x) * g_ref[...]).astype(BF16)
        n_ref[...] = n
        for lo in range(0, F, F_HALF):
            cols = slice(lo, lo + F_HALF)
            a = _dot_nt(n, wg_ref[cols, :])
            b = _dot_nt(n, wu_ref[cols, :])
            a_ref[:, cols] = a.astype(BF16)
            b_ref[:, cols] = b.astype(BF16)
            s_ref[:, cols] = (a * jax.nn.sigmoid(a) * b).astype(BF16)

    act = pl.BlockSpec((tm, F), lambda i: (i, 0))
    hidden = jax.ShapeDtypeStruct((S, F), BF16)
    return pl.pallas_call(
        body, name=name, grid=(S // tm,),
        in_specs=[pl.BlockSpec((tm, D), lambda i: (i, 0)), pl.BlockSpec((1, D), lambda i: (0, 0)), _resident(wgt), _resident(wut)],
        out_specs=[pl.BlockSpec((tm, D), lambda i: (i, 0)), act, act, act],
        out_shape=[jax.ShapeDtypeStruct((S, D), BF16), hidden, hidden, hidden],
        compiler_params=_params(dimension_semantics=("arbitrary",)),
    )(h, gnorm, wgt, wut)


def _ffn_down(h, s, wd, name, tm=1024):
    S, D = h.shape
    F = wd.shape[0]

    def body(h_ref, s_ref, wd_ref, o_ref):
        o_ref[...] = h_ref[...] + 0.5 * _dot(s_ref[...], wd_ref[...])

    row = pl.BlockSpec((tm, D), lambda i: (i, 0))
    return pl.pallas_call(
        body, name=name, grid=(S // tm,), in_specs=[row, pl.BlockSpec((tm, F), lambda i: (i, 0)), _resident(wd)],
        out_specs=row, out_shape=jax.ShapeDtypeStruct((S, D), F32),
        compiler_params=_params(dimension_semantics=("arbitrary",)),
    )(h, s, wd)


def _ffn_down_loss(h, s, wd, target, name, tm=512):
    S, D = h.shape
    F = wd.shape[0]

    def body(h_ref, s_ref, wd_ref, t_ref, dy_ref, dyh_ref, loss_ref):
        e = h_ref[...] + 0.5 * _dot(s_ref[...], wd_ref[...]) - t_ref[...]
        dy = e * (1.0 / D)
        dy_ref[...] = dy
        dyh_ref[...] = (0.5 * dy).astype(BF16)
        tot = jnp.sum(jnp.sum(e * e, axis=1, keepdims=True), axis=0, keepdims=True) * (0.5 / D)
        _acc(loss_ref, jnp.broadcast_to(tot, loss_ref.shape), pl.program_id(0) == 0)

    row = pl.BlockSpec((tm, D), lambda i: (i, 0))
    return pl.pallas_call(
        body, name=name, grid=(S // tm,), in_specs=[row, pl.BlockSpec((tm, F), lambda i: (i, 0)), _resident(wd), row],
        out_specs=[row, row, pl.BlockSpec((8, 128), lambda i: (0, 0))],
        out_shape=[jax.ShapeDtypeStruct((S, D), F32), jax.ShapeDtypeStruct((S, D), BF16), jax.ShapeDtypeStruct((8, 128), F32)],
        compiler_params=_params(dimension_semantics=("arbitrary",)),
    )(h, s, wd, target)


def _ffn_bwd_h(dyh, a, b, wd, token, name, tm=512):
    S, D = dyh.shape
    F = wd.shape[0]

    def body(dyh_ref, a_ref, b_ref, wd_ref, _, da_ref, db_ref):
        dyh_t = dyh_ref[...]
        for lo in range(0, F, F_HALF):
            cols = slice(lo, lo + F_HALF)
            av = a_ref[:, cols].astype(F32)
            bv = b_ref[:, cols].astype(F32)
            ds = _dot_nt(dyh_t, wd_ref[cols, :])
            sig = jax.nn.sigmoid(av)
            silu = av * sig
            da_ref[:, cols] = (ds * bv * (sig * (1.0 + av * (1.0 - sig)))).astype(BF16)
            db_ref[:, cols] = (ds * silu).astype(BF16)

    act = pl.BlockSpec((tm, F), lambda i: (i, 0))
    hidden = jax.ShapeDtypeStruct((S, F), BF16)
    return pl.pallas_call(
        body, name=name, grid=(S // tm,),
        in_specs=[pl.BlockSpec((tm, D), lambda i: (i, 0)), act, act, _resident(wd), TOKEN_SPEC],
        out_specs=[act, act], out_shape=[hidden, hidden],
        compiler_params=_params(dimension_semantics=("arbitrary",)),
    )(dyh, a, b, wd, token)


def _xty(x, y, name, tk=2048, tf=1408):
    S, F = x.shape
    D = y.shape[1]
    nk = S // tk

    def body(x_ref, y_ref, o_ref, acc):
        k = pl.program_id(1)
        _acc(acc, _dot_tn(x_ref[...], y_ref[...]), k == 0)

        @pl.when(k == nk - 1)
        def _():
            o_ref[...] = acc[...].astype(BF16)

    return pl.pallas_call(
        body, name=name, grid=(F // tf, nk),
        in_specs=[pl.BlockSpec((tk, tf), lambda j, k: (k, j)), pl.BlockSpec((tk, D), lambda j, k: (k, 0))],
        out_specs=pl.BlockSpec((tf, D), lambda j, k: (j, 0)), out_shape=jax.ShapeDtypeStruct((F, D), BF16),
        scratch_shapes=[pltpu.VMEM((tf, D), F32)],
        compiler_params=_params(dimension_semantics=("arbitrary", "arbitrary")),
    )(x, y)


def _ffn_bwd_x(da, db, wgt, wut, x, dh, gnorm, name, tm=512):
    S, D = x.shape
    F = wgt.shape[0]

    def body(da_ref, db_ref, wg_ref, wu_ref, x_ref, dh_ref, g_ref, dx_ref, dg_ref):
        dn = _dot(da_ref[...], wg_ref[...]) + _dot(db_ref[...], wu_ref[...])
        dx, dg_rows = _rms_bwd(dn, x_ref[...], g_ref[...])
        dx_ref[...] = dh_ref[...] + dx
        _acc(dg_ref, jnp.sum(dg_rows, axis=0, keepdims=True), pl.program_id(0) == 0)

    row = pl.BlockSpec((tm, D), lambda i: (i, 0))
    act = pl.BlockSpec((tm, F), lambda i: (i, 0))
    vec = pl.BlockSpec((1, D), lambda i: (0, 0))
    return pl.pallas_call(
        body, name=name, grid=(S // tm,), in_specs=[act, act, _resident(wgt), _resident(wut), row, row, vec],
        out_specs=[row, vec],
        out_shape=[jax.ShapeDtypeStruct((S, D), F32), jax.ShapeDtypeStruct((1, D), F32)],
        compiler_params=_params(dimension_semantics=("arbitrary",)),
    )(da, db, wgt, wut, x, dh, gnorm)


def _mix_proj(h, gnorm, wint, tm=512):
    S, D = h.shape
    nt = S // tm

    def body(h_ref, g_ref, w_ref, u_ref, xp_ref, q_ref, kv_ref, gl_ref):
        x = h_ref[...]
        u = (x * _rinv(x) * g_ref[...]).astype(BF16)
        u_ref[...] = u
        for (off, width), ref in zip(SEGMENTS, (xp_ref, q_ref, kv_ref, gl_ref)):
            ref[...] = _dot_nt(u, w_ref[off:off + width, :]).astype(ref.dtype)

    def row(width):
        return pl.BlockSpec((tm, width), lambda i: (i, 0))

    return pl.pallas_call(
        body, name="mix_proj", grid=(nt,),
        in_specs=[row(D), pl.BlockSpec((1, D), lambda i: (0, 0)), pl.BlockSpec(wint.shape, lambda i: (0, 0))],
        out_specs=[row(D), row(SEG_POOL[1]), row(SEG_Q[1]), row(SEG_KV[1]), row(SEG_GATE[1])],
        out_shape=[jax.ShapeDtypeStruct((S, D), BF16), jax.ShapeDtypeStruct((S, SEG_POOL[1]), F32),
                   jax.ShapeDtypeStruct((S, SEG_Q[1]), BF16), jax.ShapeDtypeStruct((S, SEG_KV[1]), BF16),
                   jax.ShapeDtypeStruct((S, SEG_GATE[1]), BF16)],
        compiler_params=_params(dimension_semantics=("arbitrary",)),
    )(h, gnorm, wint)


def _stack_heads(x, g):
    return jnp.concatenate([x[:, (GQA_GROUP * g + hh) * HEAD_DIM:(GQA_GROUP * g + hh + 1) * HEAD_DIM]
                            for hh in range(GQA_GROUP)], axis=0)


def _unstack_heads(ref, val, g):
    for hh in range(GQA_GROUP):
        lo = (GQA_GROUP * g + hh) * HEAD_DIM
        ref[:, lo:lo + HEAD_DIM] = val[hh * ATTN_BLOCK:(hh + 1) * ATTN_BLOCK, :]


def _rowsum(xb, width):
    return _dot(xb, jnp.ones((xb.shape[1], width), BF16))


def _rinv_lanes(x):
    return lax.rsqrt(_rowsum((x * x).astype(BF16), x.shape[1]) * (1.0 / x.shape[1]) + RMS_EPS)


def _attn_exp(qn, kc, kp, sink, n):
    rows = GQA_GROUP * ATTN_BLOCK
    qi = lax.broadcasted_iota(jnp.int32, (rows, ATTN_BLOCK), 0) % ATTN_BLOCK
    kj = lax.broadcasted_iota(jnp.int32, (rows, ATTN_BLOCK), 1)
    s_c = jnp.where(kj <= qi, _dot_nt(qn, kc), NEG)
    s_p = jnp.where(jnp.logical_and(kj > qi, n > 0), _dot_nt(qn, kp), NEG)
    m = jnp.maximum(jnp.max(s_c, axis=-1, keepdims=True), jnp.max(s_p, axis=-1, keepdims=True))
    m = jnp.maximum(jnp.broadcast_to(m, sink.shape), sink)
    e_c = jnp.exp(s_c - m)
    e_p = jnp.exp(s_p - m)
    e_s = jnp.exp(sink - m)
    e_cb, e_pb = e_c.astype(BF16), e_p.astype(BF16)
    inv = 1.0 / (_rowsum(jnp.concatenate([e_cb, e_pb], axis=1), ATTN_BLOCK) + e_s)
    return e_cb, e_pb, e_c, e_p, e_s, inv


def _attn_blocks(n):
    cur = pl.multiple_of(n * ATTN_BLOCK, ATTN_BLOCK)
    prev = pl.multiple_of(jnp.maximum(n - 1, 0) * ATTN_BLOCK, ATTN_BLOCK)
    return cur, prev


def _kv_split(kv, g):
    k = kv[:, g * HEAD_DIM:(g + 1) * HEAD_DIM]
    v = kv[:, (N_KV_HEADS + g) * HEAD_DIM:(N_KV_HEADS + g + 1) * HEAD_DIM]
    return k, v


def _attn_fwd(q, kv, qw, kw, sink_rows):
    S, W = q.shape
    nb = S // ATTN_BLOCK

    def body(q_ref, kv_ref, qw_ref, kw_ref, sk_ref, o_ref, o_scr):
        n = pl.program_id(0)
        qf = q_ref[...].astype(F32)
        cur, prev = _attn_blocks(n)
        kvc = kv_ref[pl.ds(cur, ATTN_BLOCK), :].astype(F32)
        kvp = kv_ref[pl.ds(prev, ATTN_BLOCK), :].astype(F32)
        for g in range(N_KV_HEADS):
            qs = _stack_heads(qf, g)
            qn = (qs * _rinv_lanes(qs) * qw_ref[...] * SCALE).astype(BF16)
            kc, vc = _kv_split(kvc, g)
            kp, vp = _kv_split(kvp, g)
            kc = (kc * _rinv_lanes(kc) * kw_ref[...]).astype(BF16)
            kp = (kp * _rinv_lanes(kp) * kw_ref[...]).astype(BF16)
            e_cb, e_pb, _, _, _, inv = _attn_exp(qn, kc, kp, sk_ref[g], n)
            o = (_dot(e_cb, vc.astype(BF16)) + _dot(e_pb, vp.astype(BF16))) * inv[:, :HEAD_DIM]
            _unstack_heads(o_scr, o, g)
        o_ref[...] = o_scr[...].astype(BF16)

    blk = pl.BlockSpec((ATTN_BLOCK, W), lambda n: (n, 0))
    return pl.pallas_call(
        body, name="attn_fwd", grid=(nb,),
        in_specs=[blk, pl.BlockSpec(kv.shape, lambda n: (0, 0)), pl.BlockSpec((1, HEAD_DIM), lambda n: (0, 0)),
                  pl.BlockSpec((1, HEAD_DIM), lambda n: (0, 0)), pl.BlockSpec(sink_rows.shape, lambda n: (0, 0, 0))],
        out_specs=blk, out_shape=jax.ShapeDtypeStruct((S, W), BF16),
        scratch_shapes=[pltpu.VMEM((ATTN_BLOCK, W), F32)],
        compiler_params=_params(dimension_semantics=("arbitrary",)),
    )(q, kv, qw, kw, sink_rows)


def _attn_bwd(q, kv, do, qw, kw, sink_rows):
    S, W = q.shape
    KW = kv.shape[1]
    nb = S // ATTN_BLOCK
    chunk = 512

    def body(q_ref, kv_ref, do_ref, qw_ref, kw_ref, sk_ref, dq_ref, dkv_ref, dqw_ref, dkw_ref, dsk_ref, dq_scr):
        n = pl.program_id(0)

        @pl.when(n == 0)
        def _():
            dkv_ref[...] = jnp.zeros_like(dkv_ref)
            dqw_ref[...] = jnp.zeros_like(dqw_ref)
            dsk_ref[...] = jnp.zeros_like(dsk_ref)

        qf = q_ref[...].astype(F32)
        dof = do_ref[...].astype(F32)
        cur, prev = _attn_blocks(n)
        kvc = kv_ref[pl.ds(cur, ATTN_BLOCK), :].astype(F32)
        kvp = kv_ref[pl.ds(prev, ATTN_BLOCK), :].astype(F32)
        qw_v = qw_ref[...]
        for g in range(N_KV_HEADS):
            qs = _stack_heads(qf, g)
            rq = _rinv_lanes(qs)
            qhat = qs * rq
            qn = (qhat * qw_v * SCALE).astype(BF16)
            kc, vc = _kv_split(kvc, g)
            kp, vp = _kv_split(kvp, g)
            kc = (kc * _rinv_lanes(kc) * kw_ref[...]).astype(BF16)
            kp = (kp * _rinv_lanes(kp) * kw_ref[...]).astype(BF16)
            vc = vc.astype(BF16)
            vp = vp.astype(BF16)
            dos = _stack_heads(dof, g).astype(BF16)
            _, _, e_c, e_p, e_s, inv = _attn_exp(qn, kc, kp, sk_ref[g], n)
            p_c, p_p, p_s = e_c * inv, e_p * inv, e_s * inv
            dp_c = _dot_nt(dos, vc)
            dp_p = _dot_nt(dos, vp)
            drow = _rowsum(jnp.concatenate([(p_c * dp_c).astype(BF16), (p_p * dp_p).astype(BF16)], axis=1), ATTN_BLOCK)
            ds_c = (p_c * (dp_c - drow)).astype(BF16)
            ds_p = (p_p * (dp_p - drow)).astype(BF16)
            dsink = -(p_s * drow)
            for hh in range(GQA_GROUP):
                tot = jnp.sum(dsink[hh * ATTN_BLOCK:(hh + 1) * ATTN_BLOCK, :], axis=0, keepdims=True)
                dsk_ref[g, hh:hh + 1, :] += tot
            dqn = (_dot(ds_c, kc) + _dot(ds_p, kp)) * SCALE
            klo, vlo = g * HEAD_DIM, (N_KV_HEADS + g) * HEAD_DIM
            dkv_ref[pl.ds(cur, ATTN_BLOCK), klo:klo + HEAD_DIM] += _dot_tn(ds_c, qn)
            dkv_ref[pl.ds(prev, ATTN_BLOCK), klo:klo + HEAD_DIM] += _dot_tn(ds_p, qn)
            dkv_ref[pl.ds(cur, ATTN_BLOCK), vlo:vlo + HEAD_DIM] += _dot_tn(p_c.astype(BF16), dos)
            dkv_ref[pl.ds(prev, ATTN_BLOCK), vlo:vlo + HEAD_DIM] += _dot_tn(p_p.astype(BF16), dos)
            dqw_ref[...] += jnp.sum(dqn * qhat, axis=0, keepdims=True)
            z = dqn * qw_v
            dqs = rq * (z - qhat * (_rowsum((z * qhat).astype(BF16), HEAD_DIM) * (1.0 / HEAD_DIM)))
            _unstack_heads(dq_scr, dqs, g)
        dq_ref[...] = dq_scr[...].astype(BF16)

        @pl.when(n == nb - 1)
        def _():
            def one(c, dkw):
                rows = pl.ds(pl.multiple_of(c * chunk, chunk), chunk)
                for g in range(N_KV_HEADS):
                    lo = g * HEAD_DIM
                    k = kv_ref[rows, lo:lo + HEAD_DIM].astype(F32)
                    dx, dg_rows = _rms_bwd(dkv_ref[rows, lo:lo + HEAD_DIM], k, kw_ref[...])
                    dkv_ref[rows, lo:lo + HEAD_DIM] = dx
                    dkw = dkw + jnp.sum(dg_rows, axis=0, keepdims=True)
                return dkw

            dkw_ref[...] = lax.fori_loop(0, S // chunk, one, jnp.zeros((1, HEAD_DIM), F32))

    blk = pl.BlockSpec((ATTN_BLOCK, W), lambda n: (n, 0))
    whole_kv = pl.BlockSpec((S, KW), lambda n: (0, 0))
    vec = pl.BlockSpec((1, HEAD_DIM), lambda n: (0, 0))
    sk = pl.BlockSpec(sink_rows.shape, lambda n: (0, 0, 0))
    dsk = pl.BlockSpec((N_KV_HEADS, GQA_GROUP, 128), lambda n: (0, 0, 0))
    return pl.pallas_call(
        body, name="attn_bwd", grid=(nb,), in_specs=[blk, whole_kv, blk, vec, vec, sk],
        out_specs=[blk, whole_kv, vec, vec, dsk],
        out_shape=[jax.ShapeDtypeStruct((S, W), BF16), jax.ShapeDtypeStruct((S, KW), F32),
                   jax.ShapeDtypeStruct((1, HEAD_DIM), F32), jax.ShapeDtypeStruct((1, HEAD_DIM), F32),
                   jax.ShapeDtypeStruct((N_KV_HEADS, GQA_GROUP, 128), F32)],
        scratch_shapes=[pltpu.VMEM((ATTN_BLOCK, W), F32)],
        compiler_params=_params(dimension_semantics=("arbitrary",)),
    )(q, kv, do, qw, kw, sink_rows)


def _pooled(xc, xprev, i):
    tm = xc.shape[0]
    xh = jnp.concatenate([jnp.where(i > 0, xprev, 0.0), xc], axis=0)
    t = lax.broadcasted_iota(jnp.int32, (tm, 1), 0) + i * tm
    out = []
    for gi, w in enumerate(POOL_WINDOWS):
        acc = xh[:, gi * POOL_GROUP:(gi + 1) * POOL_GROUP]
        sh = 1
        while sh < w:
            acc = acc + pltpu.roll(acc, sh, 0)
            sh *= 2
        cnt = jnp.minimum(t + 1, w).astype(F32)
        out.append(acc[POOL_HALO:, :] / cnt - xc[:, gi * POOL_GROUP:(gi + 1) * POOL_GROUP])
    return jnp.concatenate(out, axis=1)


def _pool_mix(pooled_b, pw_ref):
    return jnp.concatenate([_dot(pooled_b[:, gi * POOL_GROUP:(gi + 1) * POOL_GROUP], pw_ref[gi])
                            for gi in range(len(POOL_WINDOWS))], axis=1)


def _halo_specs(tm, width, S, after):
    per = tm // POOL_HALO
    last = S // POOL_HALO - 1
    if after:
        return pl.BlockSpec((POOL_HALO, width), lambda i: (jnp.minimum((i + 1) * per, last), 0))
    return pl.BlockSpec((POOL_HALO, width), lambda i: (jnp.maximum(i * per - 1, 0), 0))


def _mix_out(xp, attn, gl, bias, pw, pscale, wpot, wao, wo, h, tm=512):
    S, D = h.shape
    nt = S // tm
    PW = xp.shape[1]

    def body(xc_ref, xprev_ref, at_ref, gl_ref, bias_ref, pw_ref, ps_ref, wpo_ref, wao_ref, wo_ref, h_ref,
             ho_ref, bp_ref, ba_ref):
        i = pl.program_id(0)
        pooled = _pooled(xc_ref[...], xprev_ref[...], i).astype(BF16)
        ms = (_pool_mix(pooled, pw_ref) * ps_ref[...]).astype(BF16)
        bp = _dot_nt(ms, wpo_ref[...])
        ba = _dot(at_ref[...], wao_ref[...])
        bp_ref[...] = bp.astype(BF16)
        ba_ref[...] = ba.astype(BF16)
        gates = jax.nn.sigmoid(gl_ref[...].astype(F32) + bias_ref[...])
        merged = (gates[:, :D] * bp + gates[:, D:] * ba).astype(BF16)
        ho_ref[...] = h_ref[...] + _dot(merged, wo_ref[...])

    def row(width):
        return pl.BlockSpec((tm, width), lambda i: (i, 0))

    def whole(x):
        nd = x.ndim
        return pl.BlockSpec(x.shape, lambda i: (0,) * nd)

    return pl.pallas_call(
        body, name="mix_out", grid=(nt,),
        in_specs=[row(PW), _halo_specs(tm, PW, S, False), row(D), row(2 * D), whole(bias), whole(pw), whole(pscale),
                  whole(wpot), whole(wao), whole(wo), row(D)],
        out_specs=[row(D), row(D), row(D)],
        out_shape=[jax.ShapeDtypeStruct((S, D), F32), jax.ShapeDtypeStruct((S, D), BF16),
                   jax.ShapeDtypeStruct((S, D), BF16)],
        compiler_params=_params(dimension_semantics=("arbitrary",)),
    )(xp, xp, attn, gl, bias, pw, pscale, wpot, wao, wo, h)


def _mix_bwd_gate(dh, wo, bp, ba, gl, bias, tm=512):
    S, D = dh.shape
    nt = S // tm

    def body(dh_ref, wo_ref, bp_ref, ba_ref, gl_ref, bias_ref, dgl_ref, dbp_ref, dba_ref, dwo_ref, dbias_ref):
        i = pl.program_id(0)
        dhb = dh_ref[...].astype(BF16)
        dm = _dot_nt(dhb, wo_ref[...])
        gates = jax.nn.sigmoid(gl_ref[...].astype(F32) + bias_ref[...])
        gp, ga = gates[:, :D], gates[:, D:]
        bp_v = bp_ref[...].astype(F32)
        ba_v = ba_ref[...].astype(F32)
        merged = (gp * bp_v + ga * ba_v).astype(BF16)
        _acc(dwo_ref, _dot_tn(merged, dhb), i == 0)
        dbp_ref[...] = (dm * gp).astype(BF16)
        dba_ref[...] = (dm * ga).astype(BF16)
        dgl = jnp.concatenate([dm * bp_v * gp * (1.0 - gp), dm * ba_v * ga * (1.0 - ga)], axis=1)
        dgl_ref[...] = dgl.astype(BF16)
        _acc(dbias_ref, jnp.sum(dgl, axis=0, keepdims=True), i == 0)

    def row(width):
        return pl.BlockSpec((tm, width), lambda i: (i, 0))

    def whole(shape):
        return pl.BlockSpec(shape, lambda i: (0, 0))

    return pl.pallas_call(
        body, name="mix_bwd_gate", grid=(nt,),
        in_specs=[row(D), whole(wo.shape), row(D), row(D), row(2 * D), whole(bias.shape)],
        out_specs=[row(2 * D), row(D), row(D), whole((D, D)), whole((1, 2 * D))],
        out_shape=[jax.ShapeDtypeStruct((S, 2 * D), BF16), jax.ShapeDtypeStruct((S, D), BF16),
                   jax.ShapeDtypeStruct((S, D), BF16), jax.ShapeDtypeStruct((D, D), F32),
                   jax.ShapeDtypeStruct((1, 2 * D), F32)],
        compiler_params=_params(dimension_semantics=("arbitrary",)),
    )(dh, wo, bp, ba, gl, bias)


def _mix_bwd_branch(dbp, dba, attn, xp, pw, pscale, wpot, wao, tm=1024):
    S, D = dbp.shape
    nt = S // tm
    PW = xp.shape[1]
    NG = len(POOL_WINDOWS)

    def body(dbp_ref, dba_ref, at_ref, xc_ref, xprev_ref, pw_ref, ps_ref, wpo_ref, wao_ref,
             dat_ref, dpl_ref, dwao_ref, dwpo_ref, dpw_ref, dps_ref):
        i = pl.program_id(0)
        dba_v = dba_ref[...]
        dbp_v = dbp_ref[...]
        _acc(dwao_ref, _dot_tn(at_ref[...], dba_v), i == 0)
        dat_ref[...] = _dot_nt(dba_v, wao_ref[...]).astype(BF16)
        pooled = _pooled(xc_ref[...], xprev_ref[...], i).astype(BF16)
        mixed = _pool_mix(pooled, pw_ref)
        ps = ps_ref[...]
        _acc(dwpo_ref, _dot_tn(dbp_v, (mixed * ps).astype(BF16)), i == 0)
        dms = _dot(dbp_v, wpo_ref[...])
        _acc(dps_ref, jnp.sum(dms * mixed, axis=0, keepdims=True), i == 0)
        dmixed = (dms * ps).astype(BF16)
        dpooled = []
        for gi in range(NG):
            cols = slice(gi * POOL_GROUP, (gi + 1) * POOL_GROUP)
            _acc(dpw_ref.at[gi], _dot_tn(pooled[:, cols], dmixed[:, cols]), i == 0)
            dpooled.append(_dot_nt(dmixed[:, cols], pw_ref[gi]))
        dpl_ref[...] = jnp.concatenate(dpooled, axis=1)

    def row(width):
        return pl.BlockSpec((tm, width), lambda i: (i, 0))

    def whole(shape):
        nd = len(shape)
        return pl.BlockSpec(shape, lambda i: (0,) * nd)

    return pl.pallas_call(
        body, name="mix_bwd_branch", grid=(nt,),
        in_specs=[row(D), row(D), row(D), row(PW), _halo_specs(tm, PW, S, False), whole(pw.shape), whole(pscale.shape),
                  whole(wpot.shape), whole(wao.shape)],
        out_specs=[row(D), row(PW), whole((D, D)), whole((D, PW)), whole(pw.shape), whole((1, PW))],
        out_shape=[jax.ShapeDtypeStruct((S, D), BF16), jax.ShapeDtypeStruct((S, PW), F32),
                   jax.ShapeDtypeStruct((D, D), F32), jax.ShapeDtypeStruct((D, PW), F32),
                   jax.ShapeDtypeStruct(pw.shape, F32), jax.ShapeDtypeStruct((1, PW), F32)],
        compiler_params=_params(dimension_semantics=("arbitrary",)),
    )(dbp, dba, attn, xp, xp, pw, pscale, wpot, wao)


def _pool_bwd(dpooled, tm=512):
    S, PW = dpooled.shape
    nt = S // tm

    def body(dc_ref, dnext_ref, dxp_ref):
        i = pl.program_id(0)
        dc = dc_ref[...]
        dh = jnp.concatenate([dc, jnp.where(i < nt - 1, dnext_ref[...], 0.0)], axis=0)
        rows = tm + POOL_HALO
        t = lax.broadcasted_iota(jnp.int32, (rows, 1), 0) + i * tm
        out = []
        for gi, w in enumerate(POOL_WINDOWS):
            cols = slice(gi * POOL_GROUP, (gi + 1) * POOL_GROUP)
            acc = dh[:, cols] / jnp.minimum(t + 1, w).astype(F32)
            sh = 1
            while sh < w:
                acc = acc + pltpu.roll(acc, rows - sh, 0)
                sh *= 2
            out.append(acc[:tm, :] - dc[:, cols])
        dxp_ref[...] = jnp.concatenate(out, axis=1).astype(BF16)

    return pl.pallas_call(
        body, name="pool_bwd", grid=(nt,),
        in_specs=[pl.BlockSpec((tm, PW), lambda i: (i, 0)), _halo_specs(tm, PW, S, True)],
        out_specs=pl.BlockSpec((tm, PW), lambda i: (i, 0)), out_shape=jax.ShapeDtypeStruct((S, PW), BF16),
        compiler_params=_params(dimension_semantics=("arbitrary",)),
    )(dpooled, dpooled)


def _in_bwd_w(pieces, u, tm=1024):
    S, D = u.shape
    nt = S // tm
    NW = sum(width for _, width in SEGMENTS)

    def body(*refs):
        piece_refs, u_ref, dw_hbm, acc, sem = refs[:len(SEGMENTS)], refs[len(SEGMENTS)], refs[len(SEGMENTS) + 1], refs[-2], refs[-1]
        i = pl.program_id(0)
        u_t = u_ref[...]
        for (off, width), ref in zip(SEGMENTS, piece_refs):
            for lo in range(0, width, D):
                hi = min(lo + D, width)
                _acc(acc.at[off + lo:off + hi, :], _dot_tn(ref[:, lo:hi].astype(BF16), u_t), i == 0)

        @pl.when(i == nt - 1)
        def _():
            cp = pltpu.make_async_copy(acc, dw_hbm, sem)
            cp.start()
            cp.wait()

    return pl.pallas_call(
        body, name="in_bwd_w", grid=(nt,),
        in_specs=[pl.BlockSpec((tm, width), lambda i: (i, 0)) for _, width in SEGMENTS] + [pl.BlockSpec((tm, D), lambda i: (i, 0))],
        out_specs=pl.BlockSpec(memory_space=pl.ANY), out_shape=jax.ShapeDtypeStruct((NW, D), F32),
        scratch_shapes=[pltpu.VMEM((NW, D), F32), pltpu.SemaphoreType.DMA],
        compiler_params=_params(dimension_semantics=("arbitrary",)),
    )(*pieces, u)


def _in_bwd_x(pieces, wint, h, dh, gnorm, tm=512):
    S, D = h.shape
    nt = S // tm

    def body(*refs):
        piece_refs = refs[:len(SEGMENTS)]
        w_ref, h_ref, dh_ref, g_ref, dx_ref, dxh_ref, dg_ref = refs[len(SEGMENTS):]
        i = pl.program_id(0)
        du = jnp.zeros((tm, D), F32)
        for (off, width), ref in zip(SEGMENTS, piece_refs):
            du = du + _dot(ref[...].astype(BF16), w_ref[off:off + width, :])
        dx, dg_rows = _rms_bwd(du, h_ref[...], g_ref[...])
        out = dh_ref[...] + dx
        dx_ref[...] = out
        dxh_ref[...] = (0.5 * out).astype(BF16)
        _acc(dg_ref, jnp.sum(dg_rows, axis=0, keepdims=True), i == 0)

    def row(width):
        return pl.BlockSpec((tm, width), lambda i: (i, 0))

    vec = pl.BlockSpec((1, D), lambda i: (0, 0))
    return pl.pallas_call(
        body, name="in_bwd_x", grid=(nt,),
        in_specs=[row(width) for _, width in SEGMENTS] + [pl.BlockSpec(wint.shape, lambda i: (0, 0)), row(D), row(D), vec],
        out_specs=[row(D), row(D), vec],
        out_shape=[jax.ShapeDtypeStruct((S, D), F32), jax.ShapeDtypeStruct((S, D), BF16), jax.ShapeDtypeStruct((1, D), F32)],
        compiler_params=_params(dimension_semantics=("arbitrary",)),
    )(*pieces, wint, h, dh, gnorm)


def _row_tile(rows):
    for t in (512, 480, 352, 256, 128, 64, 32, 16, 8):
        if rows % t == 0:
            return t
    return rows


def _adamw(w, g, m, v, token, name):
    R, C = w.shape
    tr = _row_tile(R)

    def body(w_ref, g_ref, m_ref, v_ref, _, d_ref, mo_ref, vo_ref, go_ref):
        gv = g_ref[...]
        go_ref[...] = gv
        mn = ADAM_B1 * m_ref[...] + (1.0 - ADAM_B1) * gv
        vn = ADAM_B2 * v_ref[...] + (1.0 - ADAM_B2) * (gv * gv)
        m_hat = mn / (1.0 - ADAM_B1 ** ADAM_STEP)
        v_hat = vn / (1.0 - ADAM_B2 ** ADAM_STEP)
        d_ref[...] = -ADAM_LR * (m_hat / (jnp.sqrt(v_hat) + ADAM_EPS) + ADAM_WD * w_ref[...])
        mo_ref[...] = mn
        vo_ref[...] = vn

    blk = pl.BlockSpec((tr, C), lambda i: (i, 0))
    sh = jax.ShapeDtypeStruct((R, C), F32)
    return pl.pallas_call(
        body, name=name, grid=(R // tr,), in_specs=[blk] * 4 + [TOKEN_SPEC], out_specs=[blk] * 4, out_shape=[sh] * 4,
        compiler_params=_params(dimension_semantics=("arbitrary",)),
    )(w, g, m, v, token)


def _cast_place(w, place, name):
    R, C = w.shape
    tr = _row_tile(R)
    per = R // tr

    def body(p_ref, w_ref, o_ref):
        o_ref[...] = w_ref[...].astype(BF16)

    grid_spec = pltpu.PrefetchScalarGridSpec(
        num_scalar_prefetch=1, grid=(per,),
        in_specs=[pl.BlockSpec((tr, C), lambda i, p: (i, 0))],
        out_specs=pl.BlockSpec((tr, C), lambda i, p: (p[0] * per + i, 0)))
    return pl.pallas_call(
        body, name=name, grid_spec=grid_spec, out_shape=jax.ShapeDtypeStruct((N_SHARDS * R, C), BF16),
        compiler_params=_params(dimension_semantics=("arbitrary",)),
    )(place, w)


def _parked(w, m, idx, first, last):
    return jnp.where(m == w, idx, jnp.where(m < w, first, last))


def _sum_halves(g4s, recvs, place, name):
    M = len(g4s)
    NS, R, C = g4s[0].shape
    hr = R // 2
    tr = _row_tile(hr)
    per = hr // tr

    def body(p_ref, *refs):
        m = pl.program_id(0)
        for w in range(M):
            @pl.when(m == w)
            def _(w=w):
                refs[2 * M + w][...] = (refs[w][...].astype(F32) + refs[M + w][...].astype(F32)).astype(BF16)

    def spec(w, mine):
        def index(m, s, i, p):
            row = _parked(w, m, i, 0, per - 1)
            return _parked(w, m, s, 0, NS - 1), (p[1] * per + row if mine else row), 0

        return pl.BlockSpec((1, tr, C), index)

    grid_spec = pltpu.PrefetchScalarGridSpec(
        num_scalar_prefetch=1, grid=(M, NS, per),
        in_specs=[spec(w, True) for w in range(M)] + [spec(w, False) for w in range(M)],
        out_specs=[spec(w, False) for w in range(M)])
    return pl.pallas_call(
        body, name=name, grid_spec=grid_spec, out_shape=[jax.ShapeDtypeStruct((NS, hr, C), BF16)] * M,
        compiler_params=_params(dimension_semantics=("arbitrary",) * 3),
    )(place, *g4s, *recvs)


def _sum_quarters(h4s, recv3s, place, name):
    M = len(h4s)
    NS, hr, C = h4s[0].shape
    tr = _row_tile(hr)
    per = hr // tr

    def body(p_ref, *refs):
        m = pl.program_id(0)
        for w in range(M):
            @pl.when(m == w)
            def _(w=w):
                acc = refs[w][0].astype(F32)
                for k in range(N_SHARDS - 1):
                    acc = acc + refs[M + w][k].astype(F32)
                refs[2 * M + w][...] = acc

    def row(w, m, i):
        return _parked(w, m, i, 0, per - 1)

    grid_spec = pltpu.PrefetchScalarGridSpec(
        num_scalar_prefetch=1, grid=(M, per),
        in_specs=[pl.BlockSpec((1, tr, C), lambda m, i, p, w=w: (p[0], row(w, m, i), 0)) for w in range(M)]
        + [pl.BlockSpec((N_SHARDS - 1, tr, C), lambda m, i, p, w=w: (0, row(w, m, i), 0)) for w in range(M)],
        out_specs=[pl.BlockSpec((tr, C), lambda m, i, p, w=w: (p[1] * per + row(w, m, i), 0)) for w in range(M)])
    return pl.pallas_call(
        body, name=name, grid_spec=grid_spec, out_shape=[jax.ShapeDtypeStruct((2 * hr, C), F32)] * M,
        compiler_params=_params(dimension_semantics=("arbitrary",) * 2),
    )(place, *h4s, *recv3s)


def _by_shape(arrays):
    groups = {}
    for k, a in enumerate(arrays):
        groups.setdefault((a.shape, a.dtype), []).append(k)
    return list(groups.values())


def _place():
    x, y, c = lax.axis_index("x"), lax.axis_index("y"), lax.axis_index("c")
    chips = [(1 - x, y), (x, 1 - y), (1 - x, 1 - y)]
    return x, y, c, chips


HBM_SPEC = pl.BlockSpec(memory_space=pltpu.HBM)
SEM_SPEC = pl.BlockSpec(memory_space=pltpu.SEMAPHORE)
DATAFLOW = pltpu.SideEffectType.DATAFLOW_SIDE_EFFECTING


def _hbm(a):
    return pltpu.with_memory_space_constraint(a, pltpu.HBM)


class InFlight(NamedTuple):
    send_sem: jax.Array
    recv_sem: jax.Array
    bufs: list
    plan: Callable
    token: jax.Array


def _wait_all(plan, refs, send_ref, recv_ref):
    for k, (src, dst, dev) in enumerate(plan(refs)):
        cp = pltpu.make_async_remote_copy(src_ref=src, dst_ref=dst, send_sem=send_ref.at[k], recv_sem=recv_ref.at[k],
                                          device_id=dev, device_id_type=MESH)
        cp.wait_send()
        cp.wait_recv()


def _start_all(plan, refs, send_ref, recv_ref):
    for k, (src, dst, dev) in enumerate(plan(refs)):
        pltpu.make_async_remote_copy(src_ref=src, dst_ref=dst, send_sem=send_ref.at[k], recv_sem=recv_ref.at[k],
                                     device_id=dev, device_id_type=MESH).start()


def _split_start(name, bufs, plan, n_copies, after, waits=None):
    nb = len(bufs)
    after = list(after) if isinstance(after, (list, tuple)) else [after]

    def body(*refs):
        ins = refs[:nb]
        if waits is not None:
            _wait_all(waits.plan, ins, refs[nb], refs[nb + 1])
        send_sem, recv_sem, token = refs[-nb - 3], refs[-nb - 2], refs[-1]
        _start_all(plan, ins, send_sem, recv_sem)
        token[...] = jnp.zeros_like(token)

    sems = pltpu.SemaphoreType.DMA((n_copies,))
    earlier = [] if waits is None else [waits.send_sem, waits.recv_sem]
    out = pl.pallas_call(
        body, name=name, in_specs=[HBM_SPEC] * nb + [SEM_SPEC] * len(earlier) + [TOKEN_SPEC] * len(after),
        out_shape=(sems, sems, *[pltpu.HBM(b.shape, b.dtype) for b in bufs], jax.ShapeDtypeStruct((8, 128), F32)),
        out_specs=(SEM_SPEC, SEM_SPEC, *[HBM_SPEC] * nb, pl.BlockSpec(memory_space=pltpu.VMEM)),
        input_output_aliases={i: i + 2 for i in range(nb)},
        compiler_params=pltpu.CompilerParams(has_side_effects=DATAFLOW),
    )(*[_hbm(b) for b in bufs], *earlier, *after)
    return InFlight(out[0], out[1], list(out[2:2 + nb]), plan, out[-1])


def _split_wait(name, flight, after):
    nb = len(flight.bufs)
    after = list(after) if isinstance(after, (list, tuple)) else [after]

    def body(*refs):
        _wait_all(flight.plan, refs[:nb], refs[nb], refs[nb + 1])

    out = pl.pallas_call(
        body, name=name, in_specs=[HBM_SPEC] * nb + [SEM_SPEC, SEM_SPEC] + [TOKEN_SPEC] * len(after),
        out_shape=[pltpu.HBM(b.shape, b.dtype) for b in flight.bufs], out_specs=[HBM_SPEC] * nb,
        input_output_aliases={i: i for i in range(nb)},
        compiler_params=pltpu.CompilerParams(has_side_effects=DATAFLOW),
    )(*flight.bufs, flight.send_sem, flight.recv_sem, *after)
    return list(out)


def _half_rows(buf, chip, core):
    hr = buf.shape[0] // (2 * N_SHARDS)
    return buf.at[pl.ds(pl.multiple_of((2 * chip + core) * hr, 16), hr)]


def _gather_ici_plan(bufs):
    x, y, c, chips = _place()
    return [(_half_rows(b, 2 * x + y, c), _half_rows(b, 2 * x + y, c), (px, py, c)) for b in bufs for px, py in chips]


def _gather_d2d_plan(bufs):
    x, y, c, chips = _place()
    return [(_half_rows(b, 2 * px + py, c), _half_rows(b, 2 * px + py, c), (x, y, 1 - c)) for b in bufs for px, py in chips]


def _swap_plan(bufs):
    x, y, c, _ = _place()
    n = len(bufs) // 2
    copies = []
    for g, land in zip(bufs[:n], bufs[n:]):
        hr = g.shape[1] // 2
        copies.append((g.at[:, pl.ds(pl.multiple_of((1 - c) * hr, 8), hr)], land, (x, y, 1 - c)))
    return copies


def _exchange_plan(bufs):
    x, y, c, chips = _place()
    n = len(bufs) // 2
    return [(h.at[2 * px + py], land.at[k], (px, py, c))
            for h, land in zip(bufs[:n], bufs[n:]) for k, (px, py) in enumerate(chips)]


def _share_plan(bufs):
    x, y, c, _ = _place()
    copies = []
    for buf in bufs:
        hr = buf.shape[0] // 2
        mine = buf.at[pl.ds(pl.multiple_of(c * hr, 8), hr)]
        copies.append((mine, mine, (x, y, 1 - c)))
    return copies


N_DEVICES = 8


def _slot_place(vec, place):
    R, C = vec.shape

    def body(p_ref, v_ref, o_ref):
        o_ref[0] = v_ref[...]

    grid_spec = pltpu.PrefetchScalarGridSpec(
        num_scalar_prefetch=1, grid=(1,), in_specs=[pl.BlockSpec((R, C), lambda i, p: (0, 0))],
        out_specs=pl.BlockSpec((1, R, C), lambda i, p: (2 * p[0] + p[1], 0, 0)))
    return pl.pallas_call(
        body, name="slot_place", grid_spec=grid_spec, out_shape=jax.ShapeDtypeStruct((N_DEVICES, R, C), F32),
        compiler_params=_params(dimension_semantics=("arbitrary",)),
    )(place, vec)


def _slots_plan(bufs):
    x, y, c, _ = _place()
    mine = bufs[0].at[4 * x + 2 * y + c]
    return [(mine, mine, (x ^ (r >> 2), y ^ ((r >> 1) & 1), c ^ (r & 1))) for r in range(1, N_DEVICES)]


def _sum_slots(slots):
    _, R, C = slots.shape

    def body(s_ref, o_ref):
        acc = s_ref[0]
        for d in range(1, N_DEVICES):
            acc = acc + s_ref[d]
        o_ref[...] = acc

    return pl.pallas_call(
        body, name="sum_slots", in_specs=[pl.BlockSpec(memory_space=pltpu.VMEM)],
        out_specs=pl.BlockSpec(memory_space=pltpu.VMEM), out_shape=jax.ShapeDtypeStruct((R, C), F32),
        compiler_params=_params(),
    )(slots)


SMALL = ("ffn1_norm", "mix_norm", "pool_w", "pool_scale", "q_norm", "k_norm", "sinks", "gate_bias", "ffn2_norm")
SMALL_COLS = 1024
FFN1 = ("ffn1_w_gate", "ffn1_w_up", "ffn1_w_down")
MIXER = ("w_in", "w_pool_out", "w_attn_out", "w_out")
FFN2 = ("ffn2_w_gate", "ffn2_w_up", "ffn2_w_down")
LARGE = FFN1 + MIXER + FFN2
TRANSPOSED = ("ffn1_w_gate", "ffn1_w_up", "w_in", "w_pool_out", "ffn2_w_gate", "ffn2_w_up")
WEIGHTS = ("ffn1_norm", "ffn1_w_gate", "ffn1_w_up", "ffn1_w_down", "mix_norm", "w_in", "pool_w", "pool_scale",
           "w_pool_out", "q_norm", "k_norm", "sinks", "w_attn_out", "gate_bias", "w_out", "ffn2_norm",
           "ffn2_w_gate", "ffn2_w_up", "ffn2_w_down")


def _pack_small(parts):
    flat = jnp.concatenate([p.reshape(-1) for p in parts])
    rows = -(-flat.shape[0] // (8 * SMALL_COLS)) * 8
    return jnp.pad(flat, (0, rows * SMALL_COLS - flat.shape[0])).reshape(rows, SMALL_COLS)


def _unpack_small(packed, like):
    flat = packed.reshape(-1)
    out, off = [], 0
    for p in like:
        out.append(flat[off:off + p.size].reshape(p.shape))
        off += p.size
    return out, flat[off]


def _tie(small, token):
    return small + token[0, 0]


class Reduction:
    def __init__(self, group, names, grads, place):
        self.group, self.names, self.place = group, names, place
        self.bufs = [g.reshape(N_SHARDS, -1, g.shape[-1]) for g in grads]
        self.flight = None

    def _start(self, stage, bufs, plan, n_copies, after):
        self.flight = _split_start(f"{self.group}_{stage}", bufs, plan, n_copies, after)
        return self.flight.token

    def _landed(self, stage, after):
        n = len(self.names)
        bufs = _split_wait(f"{self.group}_{stage}_wait", self.flight, after)
        return bufs[:n], bufs[n:]

    def start_swap(self, after):
        lands = [lax.empty((g.shape[0], g.shape[1] // 2, g.shape[2]), g.dtype) for g in self.bufs]
        return self._start("swap", self.bufs + lands, _swap_plan, len(self.bufs), after)

    def start_exchange(self, after):
        g4, recv = self._landed("swap", after)
        halves = self._per_shape(_sum_halves, "sum_halves", g4, recv)
        lands = [lax.empty((N_SHARDS - 1,) + h.shape[1:], h.dtype) for h in halves]
        return self._start("exchange", halves + lands, _exchange_plan, (N_SHARDS - 1) * len(halves), self.place)

    def start_share(self, after):
        halves, recv3 = self._landed("exchange", after)
        reduced = self._per_shape(_sum_quarters, "sum_quarters", halves, recv3)
        return self._start("share", reduced, _share_plan, len(reduced), self.place)

    def _per_shape(self, add, stage, mine, received):
        out = [None] * len(mine)
        for idx in _by_shape(mine):
            sums = add([mine[k] for k in idx], [received[k] for k in idx], self.place, f"{stage}_{self.names[idx[0]]}")
            for k, v in zip(idx, sums):
                out[k] = v
        return out

    def finish(self, after):
        return dict(zip(self.names, _split_wait(f"{self.group}_share_wait", self.flight, after)))


def _row_form(name, a):
    return a.T if name in TRANSPOSED else a


def kernel(x, ffn1_norm, ffn1_w_gate, ffn1_w_up, ffn1_w_down, mix_norm, w_in, pool_w, pool_scale, w_pool_out, q_norm, k_norm, sinks, w_attn_out, gate_bias, w_out, ffn2_norm, ffn2_w_gate, ffn2_w_up, ffn2_w_down, loss_target, m_ffn1_norm, m_ffn1_w_gate, m_ffn1_w_up, m_ffn1_w_down, m_mix_norm, m_w_in, m_pool_w, m_pool_scale, m_w_pool_out, m_q_norm, m_k_norm, m_sinks, m_w_attn_out, m_gate_bias, m_w_out, m_ffn2_norm, m_ffn2_w_gate, m_ffn2_w_up, m_ffn2_w_down, v_ffn1_norm, v_ffn1_w_gate, v_ffn1_w_up, v_ffn1_w_down, v_mix_norm, v_w_in, v_pool_w, v_pool_scale, v_w_pool_out, v_q_norm, v_k_norm, v_sinks, v_w_attn_out, v_gate_bias, v_w_out, v_ffn2_norm, v_ffn2_w_gate, v_ffn2_w_up, v_ffn2_w_down):
    args = dict(locals())
    wts = {n: _row_form(n, args[n]) for n in WEIGHTS}
    mom = {n: _row_form(n, args["m_" + n]) for n in WEIGHTS}
    var = {n: _row_form(n, args["v_" + n]) for n in WEIGHTS}
    shard = 2 * lax.axis_index("x") + lax.axis_index("y")
    place = jnp.stack([shard, lax.axis_index("c")]).astype(jnp.int32)

    xs, target = x[0], loss_target[0]
    D = xs.shape[1]
    g1 = wts["ffn1_norm"].reshape(1, D)
    gm = wts["mix_norm"].reshape(1, D)
    g2 = wts["ffn2_norm"].reshape(1, D)
    qw = wts["q_norm"].reshape(1, HEAD_DIM)
    kw = wts["k_norm"].reshape(1, HEAD_DIM)
    bias = wts["gate_bias"].reshape(1, 2 * D)
    pscale = wts["pool_scale"].reshape(1, -1)
    pw = wts["pool_w"].astype(BF16)
    sink_rows = jnp.broadcast_to(jnp.repeat(wts["sinks"], ATTN_BLOCK).reshape(N_KV_HEADS, GQA_GROUP * ATTN_BLOCK, 1),
                                 (N_KV_HEADS, GQA_GROUP * ATTN_BLOCK, 128))

    first, down1 = FFN1[:2], FFN1[2:]
    placed = {n: _cast_place(wts[n], place, "cast_" + n) for n in FFN1}
    first_ici = _split_start("gather_first_ici", [placed[n] for n in first], _gather_ici_plan, 3 * len(first), place)
    placed.update({n: _cast_place(wts[n], place, "cast_" + n) for n in MIXER + FFN2})
    zero = jnp.zeros((1,), F32)
    packed = [_pack_small([d[n] for n in SMALL] + [zero]) for d in (wts, mom, var)]
    shadow = [placed[n] for n in down1 + MIXER + FFN2] + packed + [sink_rows, pw]
    first_d2d = _split_start("gather_first_d2d", first_ici.bufs, _gather_d2d_plan, 3 * len(first), shadow, waits=first_ici)
    down1_ici = _split_start("gather_down1_ici", [placed[n] for n in down1], _gather_ici_plan, 3 * len(down1), first_d2d.token)
    w = dict(zip(first, _split_wait("gather_first_wait", first_d2d, down1_ici.token)))
    mix_ici = _split_start("gather_mix_ici", [placed[n] for n in MIXER], _gather_ici_plan, 3 * len(MIXER), w[first[0]])
    ffn2_ici = _split_start("gather_ffn2_ici", [placed[n] for n in FFN2], _gather_ici_plan, 3 * len(FFN2), mix_ici.token)

    n1, a1, b1, s1 = _ffn_up(xs, _tie(g1, ffn2_ici.token), w["ffn1_w_gate"], w["ffn1_w_up"], "ffn1_up")
    down1_d2d = _split_start("gather_down1_d2d", down1_ici.bufs, _gather_d2d_plan, 3 * len(down1), s1, waits=down1_ici)
    w.update(zip(down1, _split_wait("gather_down1_wait", down1_d2d, down1_d2d.token)))
    h1 = _ffn_down(xs, s1, w["ffn1_w_down"], "ffn1_down")
    mix_d2d = _split_start("gather_mix_d2d", mix_ici.bufs, _gather_d2d_plan, 3 * len(MIXER), h1, waits=mix_ici)
    w.update(zip(MIXER, _split_wait("gather_mix_wait", mix_d2d, mix_d2d.token)))
    u, xp, q, kv, gl = _mix_proj(h1, gm, w["w_in"])
    attn = _attn_fwd(q, kv, qw, kw, sink_rows)
    ffn2_d2d = _split_start("gather_ffn2_d2d", ffn2_ici.bufs, _gather_d2d_plan, 3 * len(FFN2), attn, waits=ffn2_ici)
    h2, bp, ba = _mix_out(xp, attn, gl, _tie(bias, ffn2_d2d.token), pw, pscale, w["w_pool_out"], w["w_attn_out"], w["w_out"], h1)
    w.update(zip(FFN2, _split_wait("gather_ffn2_wait", ffn2_d2d, h2)))
    n2, a2, b2, s2 = _ffn_up(h2, g2, w["ffn2_w_gate"], w["ffn2_w_up"], "ffn2_up")
    dy, dyh, loss = _ffn_down_loss(h2, s2, w["ffn2_w_down"], target, "ffn2_down_loss")

    gw, gs = {}, {}
    da, db = _ffn_bwd_h(dyh, a2, b2, w["ffn2_w_down"], ffn2_d2d.token, "ffn2_bwd_h")
    gw["ffn2_w_gate"], gw["ffn2_w_up"] = _xty(da, n2, "ffn2_dw_gate"), _xty(db, n2, "ffn2_dw_up")
    gw["ffn2_w_down"] = _xty(s2, dyh, "ffn2_dw_down")
    red2 = Reduction("reduce_ffn2", FFN2, [gw[n] for n in FFN2], place)
    token = red2.start_swap(da)
    dh2, gs["ffn2_norm"] = _ffn_bwd_x(da, db, w["ffn2_w_gate"], w["ffn2_w_up"], h2, dy, _tie(g2, token), "ffn2_bwd_x")
    token = red2.start_exchange(dh2)
    dgl, dbp, dba, gw["w_out"], gs["gate_bias"] = _mix_bwd_gate(dh2, w["w_out"], bp, ba, gl, _tie(bias, token))
    dattn, dpooled, gw["w_attn_out"], gw["w_pool_out"], gs["pool_w"], gs["pool_scale"] = _mix_bwd_branch(
        dbp, dba, attn, xp, pw, pscale, w["w_pool_out"], w["w_attn_out"])
    dq, dkv, gs["q_norm"], gs["k_norm"], dsk = _attn_bwd(q, kv, dattn, qw, kw, sink_rows)
    gs["sinks"] = dsk[:, :, 0]
    red2.start_share(dq)
    pieces = (_pool_bwd(dpooled), dq, dkv, dgl)
    gw["w_in"] = _in_bwd_w(pieces, u)
    grads = red2.finish(gw["w_in"])
    redm = Reduction("reduce_mix", MIXER, [gw[n] for n in MIXER], place)
    token = redm.start_swap(grads[FFN2[0]])
    dh1, dh1h, gs["mix_norm"] = _in_bwd_x(pieces, w["w_in"], h1, dh2, _tie(gm, token))
    token = redm.start_exchange(dh1)
    da, db = _ffn_bwd_h(dh1h, a1, b1, w["ffn1_w_down"], token, "ffn1_bwd_h")
    gw["ffn1_w_gate"], gw["ffn1_w_up"] = _xty(da, n1, "ffn1_dw_gate"), _xty(db, n1, "ffn1_dw_up")
    gw["ffn1_w_down"] = _xty(s1, dh1h, "ffn1_dw_down")
    token = redm.start_share([gw[n] for n in FFN1])
    red1 = Reduction("reduce_ffn1", FFN1, [gw[n] for n in FFN1], place)
    token = red1.start_swap(token)
    grad_x, gs["ffn1_norm"] = _ffn_bwd_x(da, db, w["ffn1_w_gate"], w["ffn1_w_up"], xs, dh1, _tie(g1, token), "ffn1_bwd_x")
    grads.update(redm.finish(grad_x))

    small_parts = [gs[n] for n in SMALL] + [loss[0, 0].reshape(1)]
    slots = _split_start("gather_small", [_slot_place(_pack_small(small_parts), place)], _slots_plan, N_DEVICES - 1,
                         grads[MIXER[0]])
    token = red1.start_exchange(slots.token)
    delta, new_m, new_v = {}, {}, {}
    for n in FFN2 + MIXER:
        delta[n], new_m[n], new_v[n], grads[n] = _adamw(wts[n], grads[n], mom[n], var[n], token, "adamw_" + n)
    summed = _sum_slots(_split_wait("gather_small_wait", slots, delta[MIXER[-1]])[0])
    small_grads, loss_sum = _unpack_small(summed, [wts[n] for n in SMALL])
    grads.update(dict(zip(SMALL, small_grads)))
    ds, ms, vs, _ = _adamw(packed[0], summed, packed[1], packed[2], token, "adamw_small")
    like = [wts[n] for n in SMALL]
    for out, packed_out in ((delta, ds), (new_m, ms), (new_v, vs)):
        out.update(dict(zip(SMALL, _unpack_small(packed_out, like)[0])))
    results = {(k, n): _row_form(n, d[n]) for k, d in enumerate((grads, delta, new_m, new_v)) for n in SMALL + FFN2 + MIXER}
    token = red1.start_share([results[k, n] for k in range(4) for n in SMALL + ("w_pool_out",)])
    grads.update(red1.finish(token))
    for n in FFN1:
        delta[n], new_m[n], new_v[n], grads[n] = _adamw(wts[n], grads[n], mom[n], var[n], token, "adamw_" + n)
    results.update({(k, n): _row_form(n, d[n]) for k, d in enumerate((grads, delta, new_m, new_v)) for n in FFN1})
    return (loss_sum, grad_x[None], *[results[k, n] for k in range(4) for n in WEIGHTS])
```

```python
from typing import Callable, NamedTuple

import jax
import jax.numpy as jnp
from jax import lax
from jax.experimental import pallas as pl
from jax.experimental.pallas import tpu as pltpu

F32 = jnp.float32
BF16 = jnp.bfloat16
RMS_EPS = 1e-6
POOL_WINDOWS = (2, 4, 8, 16)
POOL_GROUP = 128
POOL_HALO = 16
HEAD_DIM = 64
GQA_GROUP = 8
N_KV_HEADS = 2
ATTN_BLOCK = 128
SCALE = HEAD_DIM ** -0.5
NEG = -1e30
N_SHARDS = 4
ADAM_LR, ADAM_B1, ADAM_B2, ADAM_EPS, ADAM_WD, ADAM_STEP = 0.001, 0.9, 0.999, 1e-08, 0.01, 10
VMEM_LIMIT = 56 * 1024 * 1024
MESH = pl.DeviceIdType.MESH
SEG_POOL, SEG_Q, SEG_KV, SEG_GATE = (0, 512), (512, 1024), (1536, 256), (1792, 2048)
SEGMENTS = (SEG_POOL, SEG_Q, SEG_KV, SEG_GATE)


def _params(**kw):
    return pltpu.CompilerParams(vmem_limit_bytes=VMEM_LIMIT, **kw)


def _dot(a, b):
    return jnp.dot(a, b, preferred_element_type=F32)


def _dot_nt(a, b):
    return lax.dot_general(a, b, (((1,), (1,)), ((), ())), preferred_element_type=F32)


def _dot_tn(a, b):
    return lax.dot_general(a, b, (((0,), (0,)), ((), ())), preferred_element_type=F32)


def _rinv(x):
    return lax.rsqrt(jnp.mean(x * x, axis=-1, keepdims=True) + RMS_EPS)


def _rms_bwd(dn, x, g):
    r = _rinv(x)
    xr = x * r
    z = dn * g
    dx = r * (z - xr * jnp.mean(z * xr, axis=-1, keepdims=True))
    return dx, dn * xr


def _acc(ref, val, first):
    @pl.when(first)
    def _():
        ref[...] = val

    @pl.when(jnp.logical_not(first))
    def _():
        ref[...] += val


TOKEN_SPEC = pl.BlockSpec(memory_space=pl.ANY)
F_HALF = 1408


def _resident(w):
    return pl.BlockSpec(w.shape, lambda i: (0, 0), pipeline_mode=pl.Buffered(1))


def _ffn_up(h, gnorm, wgt, wut, name, tm=512):
    S, D = h.shape
    F = wgt.shape[0]

    def body(h_ref, g_ref, wg_ref, wu_ref, n_ref, a_ref, b_ref, s_ref):
        x = h_ref[...]
        n = (x * _rinv(x) * g_ref[...]).astype(BF16)
        n_ref[...] = n
        for lo in range(0, F, F_HALF):
            cols = slice(lo, lo + F_HALF)
            a = _dot_nt(n, wg_ref[cols, :])
            b = _dot_nt(n, wu_ref[cols, :])
            a_ref[:, cols] = a.astype(BF16)
            b_ref[:, cols] = b.astype(BF16)
            s_ref[:, cols] = (a * jax.nn.sigmoid(a) * b).astype(BF16)

    act = pl.BlockSpec((tm, F), lambda i: (i, 0))
    hidden = jax.ShapeDtypeStruct((S, F), BF16)
    return pl.pallas_call(
        body, name=name, grid=(S // tm,),
        in_specs=[pl.BlockSpec((tm, D), lambda i: (i, 0)), pl.BlockSpec((1, D), lambda i: (0, 0)), _resident(wgt), _resident(wut)],
        out_specs=[pl.BlockSpec((tm, D), lambda i: (i, 0)), act, act, act],
        out_shape=[jax.ShapeDtypeStruct((S, D), BF16), hidden, hidden, hidden],
        compiler_params=_params(dimension_semantics=("arbitrary",)),
    )(h, gnorm, wgt, wut)


def _ffn_down(h, s, wd, name, tm=1024):
    S, D = h.shape
    F = wd.shape[0]

    def body(h_ref, s_ref, wd_ref, o_ref):
        o_ref[...] = h_ref[...] + 0.5 * _dot(s_ref[...], wd_ref[...])

    row = pl.BlockSpec((tm, D), lambda i: (i, 0))
    return pl.pallas_call(
        body, name=name, grid=(S // tm,), in_specs=[row, pl.BlockSpec((tm, F), lambda i: (i, 0)), _resident(wd)],
        out_specs=row, out_shape=jax.ShapeDtypeStruct((S, D), F32),
        compiler_params=_params(dimension_semantics=("arbitrary",)),
    )(h, s, wd)


def _ffn_down_loss(h, s, wd, target, name, tm=512):
    S, D = h.shape
    F = wd.shape[0]

    def body(h_ref, s_ref, wd_ref, t_ref, dy_ref, dyh_ref, loss_ref):
        e = h_ref[...] + 0.5 * _dot(s_ref[...], wd_ref[...]) - t_ref[...]
        dy = e * (1.0 / D)
        dy_ref[...] = dy
        dyh_ref[...] = (0.5 * dy).astype(BF16)
        tot = jnp.sum(jnp.sum(e * e, axis=1, keepdims=True), axis=0, keepdims=True) * (0.5 / D)
        _acc(loss_ref, jnp.broadcast_to(tot, loss_ref.shape), pl.program_id(0) == 0)

    row = pl.BlockSpec((tm, D), lambda i: (i, 0))
    return pl.pallas_call(
        body, name=name, grid=(S // tm,), in_specs=[row, pl.BlockSpec((tm, F), lambda i: (i, 0)), _resident(wd), row],
        out_specs=[row, row, pl.BlockSpec((8, 128), lambda i: (0, 0))],
        out_shape=[jax.ShapeDtypeStruct((S, D), F32), jax.ShapeDtypeStruct((S, D), BF16), jax.ShapeDtypeStruct((8, 128), F32)],
        compiler_params=_params(dimension_semantics=("arbitrary",)),
    )(h, s, wd, target)


def _ffn_bwd_h(dyh, a, b, wd, token, name, tm=512):
    S, D = dyh.shape
    F = wd.shape[0]

    def body(dyh_ref, a_ref, b_ref, wd_ref, _, da_ref, db_ref):
        dyh_t = dyh_ref[...]
        for lo in range(0, F, F_HALF):
            cols = slice(lo, lo + F_HALF)
            av = a_ref[:, cols].astype(F32)
            bv = b_ref[:, cols].astype(F32)
            ds = _dot_nt(dyh_t, wd_ref[cols, :])
            sig = jax.nn.sigmoid(av)
            silu = av * sig
            da_ref[:, cols] = (ds * bv * (sig * (1.0 + av * (1.0 - sig)))).astype(BF16)
            db_ref[:, cols] = (ds * silu).astype(BF16)

    act = pl.BlockSpec((tm, F), lambda i: (i, 0))
    hidden = jax.ShapeDtypeStruct((S, F), BF16)
    return pl.pallas_call(
        body, name=name, grid=(S // tm,),
        in_specs=[pl.BlockSpec((tm, D), lambda i: (i, 0)), act, act, _resident(wd), TOKEN_SPEC],
        out_specs=[act, act], out_shape=[hidden, hidden],
        compiler_params=_params(dimension_semantics=("arbitrary",)),
    )(dyh, a, b, wd, token)


def _xty(x, y, name, tk=2048, tf=1408):
    S, F = x.shape
    D = y.shape[1]

    def body(x_ref, y_ref, o_ref):
        _acc(o_ref, _dot_tn(x_ref[...], y_ref[...]), pl.program_id(1) == 0)

    return pl.pallas_call(
        body, name=name, grid=(F // tf, S // tk),
        in_specs=[pl.BlockSpec((tk, tf), lambda j, k: (k, j)), pl.BlockSpec((tk, D), lambda j, k: (k, 0))],
        out_specs=pl.BlockSpec((tf, D), lambda j, k: (j, 0)), out_shape=jax.ShapeDtypeStruct((F, D), F32),
        compiler_params=_params(dimension_semantics=("arbitrary", "arbitrary")),
    )(x, y)


def _ffn_bwd_x(da, db, wgt, wut, x, dh, gnorm, name, tm=512):
    S, D = x.shape
    F = wgt.shape[0]

    def body(da_ref, db_ref, wg_ref, wu_ref, x_ref, dh_ref, g_ref, dx_ref, dg_ref):
        dn = _dot(da_ref[...], wg_ref[...]) + _dot(db_ref[...], wu_ref[...])
        dx, dg_rows = _rms_bwd(dn, x_ref[...], g_ref[...])
        dx_ref[...] = dh_ref[...] + dx
        _acc(dg_ref, jnp.sum(dg_rows, axis=0, keepdims=True), pl.program_id(0) == 0)

    row = pl.BlockSpec((tm, D), lambda i: (i, 0))
    act = pl.BlockSpec((tm, F), lambda i: (i, 0))
    vec = pl.BlockSpec((1, D), lambda i: (0, 0))
    return pl.pallas_call(
        body, name=name, grid=(S // tm,), in_specs=[act, act, _resident(wgt), _resident(wut), row, row, vec],
        out_specs=[row, vec],
        out_shape=[jax.ShapeDtypeStruct((S, D), F32), jax.ShapeDtypeStruct((1, D), F32)],
        compiler_params=_params(dimension_semantics=("arbitrary",)),
    )(da, db, wgt, wut, x, dh, gnorm)


def _mix_proj(h, gnorm, wint, tm=512):
    S, D = h.shape
    nt = S // tm

    def body(h_ref, g_ref, w_ref, u_ref, xp_ref, q_ref, kv_ref, gl_ref):
        x = h_ref[...]
        u = (x * _rinv(x) * g_ref[...]).astype(BF16)
        u_ref[...] = u
        for (off, width), ref in zip(SEGMENTS, (xp_ref, q_ref, kv_ref, gl_ref)):
            ref[...] = _dot_nt(u, w_ref[off:off + width, :]).astype(ref.dtype)

    def row(width):
        return pl.BlockSpec((tm, width), lambda i: (i, 0))

    return pl.pallas_call(
        body, name="mix_proj", grid=(nt,),
        in_specs=[row(D), pl.BlockSpec((1, D), lambda i: (0, 0)), pl.BlockSpec(wint.shape, lambda i: (0, 0))],
        out_specs=[row(D), row(SEG_POOL[1]), row(SEG_Q[1]), row(SEG_KV[1]), row(SEG_GATE[1])],
        out_shape=[jax.ShapeDtypeStruct((S, D), BF16), jax.ShapeDtypeStruct((S, SEG_POOL[1]), F32),
                   jax.ShapeDtypeStruct((S, SEG_Q[1]), BF16), jax.ShapeDtypeStruct((S, SEG_KV[1]), BF16),
                   jax.ShapeDtypeStruct((S, SEG_GATE[1]), BF16)],
        compiler_params=_params(dimension_semantics=("arbitrary",)),
    )(h, gnorm, wint)


def _stack_heads(x, g):
    return jnp.concatenate([x[:, (GQA_GROUP * g + hh) * HEAD_DIM:(GQA_GROUP * g + hh + 1) * HEAD_DIM]
                            for hh in range(GQA_GROUP)], axis=0)


def _unstack_heads(ref, val, g):
    for hh in range(GQA_GROUP):
        lo = (GQA_GROUP * g + hh) * HEAD_DIM
        ref[:, lo:lo + HEAD_DIM] = val[hh * ATTN_BLOCK:(hh + 1) * ATTN_BLOCK, :]


def _rowsum(xb, width):
    return _dot(xb, jnp.ones((xb.shape[1], width), BF16))


def _rinv_lanes(x):
    return lax.rsqrt(_rowsum((x * x).astype(BF16), x.shape[1]) * (1.0 / x.shape[1]) + RMS_EPS)


def _twice(x):
    return jnp.concatenate([x, x], axis=1)


def _band_bias():
    qi = (jnp.arange(GQA_GROUP * ATTN_BLOCK) % ATTN_BLOCK)[:, None]
    kj = jnp.arange(2 * ATTN_BLOCK)[None, :]
    return jnp.where(jnp.logical_and(kj > qi, kj <= qi + ATTN_BLOCK), 0.0, NEG).astype(F32)


def _attn_exp(qn, kk, sink, band, n):
    kj = lax.broadcasted_iota(jnp.int32, (1, 2 * ATTN_BLOCK), 1)
    before_sequence = jnp.where(jnp.logical_and(kj < ATTN_BLOCK, n == 0), NEG, 0.0)
    s = _dot_nt(qn, kk) + band + before_sequence
    m = jnp.maximum(jnp.broadcast_to(jnp.max(s, axis=-1, keepdims=True), sink.shape), sink)
    e = jnp.exp(s - _twice(m))
    e_s = jnp.exp(sink - m)
    e_b = e.astype(BF16)
    inv = 1.0 / (_rowsum(e_b, ATTN_BLOCK) + e_s)
    return e_b, e, e_s, inv


def _attn_blocks(n):
    cur = pl.multiple_of(n * ATTN_BLOCK, ATTN_BLOCK)
    prev = pl.multiple_of(jnp.maximum(n - 1, 0) * ATTN_BLOCK, ATTN_BLOCK)
    return cur, prev


def _kv_window(kv_ref, n):
    cur, prev = _attn_blocks(n)
    return jnp.concatenate([kv_ref[pl.ds(prev, ATTN_BLOCK), :], kv_ref[pl.ds(cur, ATTN_BLOCK), :]], axis=0).astype(F32)


def _kv_split(kv, g):
    k = kv[:, g * HEAD_DIM:(g + 1) * HEAD_DIM]
    v = kv[:, (N_KV_HEADS + g) * HEAD_DIM:(N_KV_HEADS + g + 1) * HEAD_DIM]
    return k, v


def _attn_fwd(q, kv, qw, kw, sink_rows, band):
    S, W = q.shape
    nb = S // ATTN_BLOCK

    def body(q_ref, kv_ref, qw_ref, kw_ref, sk_ref, band_ref, o_ref, o_scr):
        n = pl.program_id(0)
        qf = q_ref[...].astype(F32)
        kvw = _kv_window(kv_ref, n)
        for g in range(N_KV_HEADS):
            qs = _stack_heads(qf, g)
            qn = (qs * _rinv_lanes(qs) * qw_ref[...] * SCALE).astype(BF16)
            k, v = _kv_split(kvw, g)
            kk = (k * _rinv_lanes(k) * kw_ref[...]).astype(BF16)
            e_b, _, _, inv = _attn_exp(qn, kk, sk_ref[g], band_ref[...], n)
            _unstack_heads(o_scr, _dot(e_b, v.astype(BF16)) * inv[:, :HEAD_DIM], g)
        o_ref[...] = o_scr[...].astype(BF16)

    blk = pl.BlockSpec((ATTN_BLOCK, W), lambda n: (n, 0))
    return pl.pallas_call(
        body, name="attn_fwd", grid=(nb,),
        in_specs=[blk, pl.BlockSpec(kv.shape, lambda n: (0, 0)), pl.BlockSpec((1, HEAD_DIM), lambda n: (0, 0)),
                  pl.BlockSpec((1, HEAD_DIM), lambda n: (0, 0)), pl.BlockSpec(sink_rows.shape, lambda n: (0, 0, 0)),
                  pl.BlockSpec(band.shape, lambda n: (0, 0))],
        out_specs=blk, out_shape=jax.ShapeDtypeStruct((S, W), BF16),
        scratch_shapes=[pltpu.VMEM((ATTN_BLOCK, W), F32)],
        compiler_params=_params(dimension_semantics=("arbitrary",)),
    )(q, kv, qw, kw, sink_rows, band)


def _attn_bwd(q, kv, do, qw, kw, sink_rows, band):
    S, W = q.shape
    KW = kv.shape[1]
    nb = S // ATTN_BLOCK
    chunk = 512

    def body(q_ref, kv_ref, do_ref, qw_ref, kw_ref, sk_ref, band_ref, dq_ref, dkv_ref, dqw_ref, dkw_ref, dsk_ref, dq_scr):
        n = pl.program_id(0)

        @pl.when(n == 0)
        def _():
            dkv_ref[...] = jnp.zeros_like(dkv_ref)
            dqw_ref[...] = jnp.zeros_like(dqw_ref)
            dsk_ref[...] = jnp.zeros_like(dsk_ref)

        qf = q_ref[...].astype(F32)
        dof = do_ref[...].astype(F32)
        cur, prev = _attn_blocks(n)
        kvw = _kv_window(kv_ref, n)
        qw_v = qw_ref[...]
        for g in range(N_KV_HEADS):
            qs = _stack_heads(qf, g)
            rq = _rinv_lanes(qs)
            qhat = qs * rq
            qn = (qhat * qw_v * SCALE).astype(BF16)
            k, v = _kv_split(kvw, g)
            kk = (k * _rinv_lanes(k) * kw_ref[...]).astype(BF16)
            vv = v.astype(BF16)
            dos = _stack_heads(dof, g).astype(BF16)
            _, e, e_s, inv = _attn_exp(qn, kk, sk_ref[g], band_ref[...], n)
            p = e * _twice(inv)
            dp = _dot_nt(dos, vv)
            drow = _rowsum((p * dp).astype(BF16), ATTN_BLOCK)
            ds = (p * (dp - _twice(drow))).astype(BF16)
            dsink = -(e_s * inv * drow)
            for hh in range(GQA_GROUP):
                tot = jnp.sum(dsink[hh * ATTN_BLOCK:(hh + 1) * ATTN_BLOCK, :], axis=0, keepdims=True)
                dsk_ref[g, hh:hh + 1, :] += tot
            dqn = _dot(ds, kk) * SCALE
            dkk = _dot_tn(ds, qn)
            dvv = _dot_tn(p.astype(BF16), dos)
            klo, vlo = g * HEAD_DIM, (N_KV_HEADS + g) * HEAD_DIM
            for start, rows in ((prev, slice(0, ATTN_BLOCK)), (cur, slice(ATTN_BLOCK, 2 * ATTN_BLOCK))):
                dkv_ref[pl.ds(start, ATTN_BLOCK), klo:klo + HEAD_DIM] += dkk[rows]
                dkv_ref[pl.ds(start, ATTN_BLOCK), vlo:vlo + HEAD_DIM] += dvv[rows]
            dqw_ref[...] += jnp.sum(dqn * qhat, axis=0, keepdims=True)
            z = dqn * qw_v
            dqs = rq * (z - qhat * (_rowsum((z * qhat).astype(BF16), HEAD_DIM) * (1.0 / HEAD_DIM)))
            _unstack_heads(dq_scr, dqs, g)
        dq_ref[...] = dq_scr[...].astype(BF16)

        @pl.when(n == nb - 1)
        def _():
            def one(c, dkw):
                rows = pl.ds(pl.multiple_of(c * chunk, chunk), chunk)
                for g in range(N_KV_HEADS):
                    lo = g * HEAD_DIM
                    k = kv_ref[rows, lo:lo + HEAD_DIM].astype(F32)
                    dx, dg_rows = _rms_bwd(dkv_ref[rows, lo:lo + HEAD_DIM], k, kw_ref[...])
                    dkv_ref[rows, lo:lo + HEAD_DIM] = dx
                    dkw = dkw + jnp.sum(dg_rows, axis=0, keepdims=True)
                return dkw

            dkw_ref[...] = lax.fori_loop(0, S // chunk, one, jnp.zeros((1, HEAD_DIM), F32))

    blk = pl.BlockSpec((ATTN_BLOCK, W), lambda n: (n, 0))
    whole_kv = pl.BlockSpec((S, KW), lambda n: (0, 0))
    vec = pl.BlockSpec((1, HEAD_DIM), lambda n: (0, 0))
    sk = pl.BlockSpec(sink_rows.shape, lambda n: (0, 0, 0))
    dsk = pl.BlockSpec((N_KV_HEADS, GQA_GROUP, 128), lambda n: (0, 0, 0))
    return pl.pallas_call(
        body, name="attn_bwd", grid=(nb,), in_specs=[blk, whole_kv, blk, vec, vec, sk, pl.BlockSpec(band.shape, lambda n: (0, 0))],
        out_specs=[blk, whole_kv, vec, vec, dsk],
        out_shape=[jax.ShapeDtypeStruct((S, W), BF16), jax.ShapeDtypeStruct((S, KW), F32),
                   jax.ShapeDtypeStruct((1, HEAD_DIM), F32), jax.ShapeDtypeStruct((1, HEAD_DIM), F32),
                   jax.ShapeDtypeStruct((N_KV_HEADS, GQA_GROUP, 128), F32)],
        scratch_shapes=[pltpu.VMEM((ATTN_BLOCK, W), F32)],
        compiler_params=_params(dimension_semantics=("arbitrary",)),
    )(q, kv, do, qw, kw, sink_rows, band)


def _pooled(xc, xprev, i):
    tm = xc.shape[0]
    xh = jnp.concatenate([jnp.where(i > 0, xprev, 0.0), xc], axis=0)
    t = lax.broadcasted_iota(jnp.int32, (tm, 1), 0) + i * tm
    out = []
    for gi, w in enumerate(POOL_WINDOWS):
        acc = xh[:, gi * POOL_GROUP:(gi + 1) * POOL_GROUP]
        sh = 1
        while sh < w:
            acc = acc + pltpu.roll(acc, sh, 0)
            sh *= 2
        cnt = jnp.minimum(t + 1, w).astype(F32)
        out.append(acc[POOL_HALO:, :] / cnt - xc[:, gi * POOL_GROUP:(gi + 1) * POOL_GROUP])
    return jnp.concatenate(out, axis=1)


def _pool_mix(pooled_b, pw_ref):
    return jnp.concatenate([_dot(pooled_b[:, gi * POOL_GROUP:(gi + 1) * POOL_GROUP], pw_ref[gi])
                            for gi in range(len(POOL_WINDOWS))], axis=1)


def _halo_specs(tm, width, S, after):
    per = tm // POOL_HALO
    last = S // POOL_HALO - 1
    if after:
        return pl.BlockSpec((POOL_HALO, width), lambda i: (jnp.minimum((i + 1) * per, last), 0))
    return pl.BlockSpec((POOL_HALO, width), lambda i: (jnp.maximum(i * per - 1, 0), 0))


def _mix_out(xp, attn, gl, bias, pw, pscale, wpot, wao, wo, h, tm=512):
    S, D = h.shape
    nt = S // tm
    PW = xp.shape[1]

    def body(xc_ref, xprev_ref, at_ref, gl_ref, bias_ref, pw_ref, ps_ref, wpo_ref, wao_ref, wo_ref, h_ref,
             ho_ref, bp_ref, ba_ref):
        i = pl.program_id(0)
        pooled = _pooled(xc_ref[...], xprev_ref[...], i).astype(BF16)
        ms = (_pool_mix(pooled, pw_ref) * ps_ref[...]).astype(BF16)
        bp = _dot_nt(ms, wpo_ref[...])
        ba = _dot(at_ref[...], wao_ref[...])
        bp_ref[...] = bp.astype(BF16)
        ba_ref[...] = ba.astype(BF16)
        gates = jax.nn.sigmoid(gl_ref[...].astype(F32) + bias_ref[...])
        merged = (gates[:, :D] * bp + gates[:, D:] * ba).astype(BF16)
        ho_ref[...] = h_ref[...] + _dot(merged, wo_ref[...])

    def row(width):
        return pl.BlockSpec((tm, width), lambda i: (i, 0))

    def whole(x):
        nd = x.ndim
        return pl.BlockSpec(x.shape, lambda i: (0,) * nd)

    return pl.pallas_call(
        body, name="mix_out", grid=(nt,),
        in_specs=[row(PW), _halo_specs(tm, PW, S, False), row(D), row(2 * D), whole(bias), whole(pw), whole(pscale),
                  whole(wpot), whole(wao), whole(wo), row(D)],
        out_specs=[row(D), row(D), row(D)],
        out_shape=[jax.ShapeDtypeStruct((S, D), F32), jax.ShapeDtypeStruct((S, D), BF16),
                   jax.ShapeDtypeStruct((S, D), BF16)],
        compiler_params=_params(dimension_semantics=("arbitrary",)),
    )(xp, xp, attn, gl, bias, pw, pscale, wpot, wao, wo, h)


def _mix_bwd_gate(dh, wo, bp, ba, gl, bias, tm=512):
    S, D = dh.shape
    nt = S // tm

    def body(dh_ref, wo_ref, bp_ref, ba_ref, gl_ref, bias_ref, dgl_ref, dbp_ref, dba_ref, dwo_ref, dbias_ref):
        i = pl.program_id(0)
        dhb = dh_ref[...].astype(BF16)
        dm = _dot_nt(dhb, wo_ref[...])
        gates = jax.nn.sigmoid(gl_ref[...].astype(F32) + bias_ref[...])
        gp, ga = gates[:, :D], gates[:, D:]
        bp_v = bp_ref[...].astype(F32)
        ba_v = ba_ref[...].astype(F32)
        merged = (gp * bp_v + ga * ba_v).astype(BF16)
        _acc(dwo_ref, _dot_tn(merged, dhb), i == 0)
        dbp_ref[...] = (dm * gp).astype(BF16)
        dba_ref[...] = (dm * ga).astype(BF16)
        dgl = jnp.concatenate([dm * bp_v * gp * (1.0 - gp), dm * ba_v * ga * (1.0 - ga)], axis=1)
        dgl_ref[...] = dgl.astype(BF16)
        _acc(dbias_ref, jnp.sum(dgl, axis=0, keepdims=True), i == 0)

    def row(width):
        return pl.BlockSpec((tm, width), lambda i: (i, 0))

    def whole(shape):
        return pl.BlockSpec(shape, lambda i: (0, 0))

    return pl.pallas_call(
        body, name="mix_bwd_gate", grid=(nt,),
        in_specs=[row(D), whole(wo.shape), row(D), row(D), row(2 * D), whole(bias.shape)],
        out_specs=[row(2 * D), row(D), row(D), whole((D, D)), whole((1, 2 * D))],
        out_shape=[jax.ShapeDtypeStruct((S, 2 * D), BF16), jax.ShapeDtypeStruct((S, D), BF16),
                   jax.ShapeDtypeStruct((S, D), BF16), jax.ShapeDtypeStruct((D, D), F32),
                   jax.ShapeDtypeStruct((1, 2 * D), F32)],
        compiler_params=_params(dimension_semantics=("arbitrary",)),
    )(dh, wo, bp, ba, gl, bias)


def _mix_bwd_branch(dbp, dba, attn, xp, pw, pscale, wpot, wao, tm=1024):
    S, D = dbp.shape
    nt = S // tm
    PW = xp.shape[1]
    NG = len(POOL_WINDOWS)

    def body(dbp_ref, dba_ref, at_ref, xc_ref, xprev_ref, pw_ref, ps_ref, wpo_ref, wao_ref,
             dat_ref, dpl_ref, dwao_ref, dwpo_ref, dpw_ref, dps_ref):
        i = pl.program_id(0)
        dba_v = dba_ref[...]
        dbp_v = dbp_ref[...]
        _acc(dwao_ref, _dot_tn(at_ref[...], dba_v), i == 0)
        dat_ref[...] = _dot_nt(dba_v, wao_ref[...]).astype(BF16)
        pooled = _pooled(xc_ref[...], xprev_ref[...], i).astype(BF16)
        mixed = _pool_mix(pooled, pw_ref)
        ps = ps_ref[...]
        _acc(dwpo_ref, _dot_tn(dbp_v, (mixed * ps).astype(BF16)), i == 0)
        dms = _dot(dbp_v, wpo_ref[...])
        _acc(dps_ref, jnp.sum(dms * mixed, axis=0, keepdims=True), i == 0)
        dmixed = (dms * ps).astype(BF16)
        dpooled = []
        for gi in range(NG):
            cols = slice(gi * POOL_GROUP, (gi + 1) * POOL_GROUP)
            _acc(dpw_ref.at[gi], _dot_tn(pooled[:, cols], dmixed[:, cols]), i == 0)
            dpooled.append(_dot_nt(dmixed[:, cols], pw_ref[gi]))
        dpl_ref[...] = jnp.concatenate(dpooled, axis=1)

    def row(width):
        return pl.BlockSpec((tm, width), lambda i: (i, 0))

    def whole(shape):
        nd = len(shape)
        return pl.BlockSpec(shape, lambda i: (0,) * nd)

    return pl.pallas_call(
        body, name="mix_bwd_branch", grid=(nt,),
        in_specs=[row(D), row(D), row(D), row(PW), _halo_specs(tm, PW, S, False), whole(pw.shape), whole(pscale.shape),
                  whole(wpot.shape), whole(wao.shape)],
        out_specs=[row(D), row(PW), whole((D, D)), whole((D, PW)), whole(pw.shape), whole((1, PW))],
        out_shape=[jax.ShapeDtypeStruct((S, D), BF16), jax.ShapeDtypeStruct((S, PW), F32),
                   jax.ShapeDtypeStruct((D, D), F32), jax.ShapeDtypeStruct((D, PW), F32),
                   jax.ShapeDtypeStruct(pw.shape, F32), jax.ShapeDtypeStruct((1, PW), F32)],
        compiler_params=_params(dimension_semantics=("arbitrary",)),
    )(dbp, dba, attn, xp, xp, pw, pscale, wpot, wao)


def _pool_bwd(dpooled, tm=512):
    S, PW = dpooled.shape
    nt = S // tm

    def body(dc_ref, dnext_ref, dxp_ref):
        i = pl.program_id(0)
        dc = dc_ref[...]
        dh = jnp.concatenate([dc, jnp.where(i < nt - 1, dnext_ref[...], 0.0)], axis=0)
        rows = tm + POOL_HALO
        t = lax.broadcasted_iota(jnp.int32, (rows, 1), 0) + i * tm
        out = []
        for gi, w in enumerate(POOL_WINDOWS):
            cols = slice(gi * POOL_GROUP, (gi + 1) * POOL_GROUP)
            acc = dh[:, cols] / jnp.minimum(t + 1, w).astype(F32)
            sh = 1
            while sh < w:
                acc = acc + pltpu.roll(acc, rows - sh, 0)
                sh *= 2
            out.append(acc[:tm, :] - dc[:, cols])
        dxp_ref[...] = jnp.concatenate(out, axis=1).astype(BF16)

    return pl.pallas_call(
        body, name="pool_bwd", grid=(nt,),
        in_specs=[pl.BlockSpec((tm, PW), lambda i: (i, 0)), _halo_specs(tm, PW, S, True)],
        out_specs=pl.BlockSpec((tm, PW), lambda i: (i, 0)), out_shape=jax.ShapeDtypeStruct((S, PW), BF16),
        compiler_params=_params(dimension_semantics=("arbitrary",)),
    )(dpooled, dpooled)


def _in_bwd_w(pieces, u, tm=1024):
    S, D = u.shape
    nt = S // tm
    NW = sum(width for _, width in SEGMENTS)

    def body(*refs):
        piece_refs, u_ref, dw_hbm, acc, sem = refs[:len(SEGMENTS)], refs[len(SEGMENTS)], refs[len(SEGMENTS) + 1], refs[-2], refs[-1]
        i = pl.program_id(0)
        u_t = u_ref[...]
        for (off, width), ref in zip(SEGMENTS, piece_refs):
            for lo in range(0, width, D):
                hi = min(lo + D, width)
                _acc(acc.at[off + lo:off + hi, :], _dot_tn(ref[:, lo:hi].astype(BF16), u_t), i == 0)

        @pl.when(i == nt - 1)
        def _():
            cp = pltpu.make_async_copy(acc, dw_hbm, sem)
            cp.start()
            cp.wait()

    return pl.pallas_call(
        body, name="in_bwd_w", grid=(nt,),
        in_specs=[pl.BlockSpec((tm, width), lambda i: (i, 0)) for _, width in SEGMENTS] + [pl.BlockSpec((tm, D), lambda i: (i, 0))],
        out_specs=pl.BlockSpec(memory_space=pl.ANY), out_shape=jax.ShapeDtypeStruct((NW, D), F32),
        scratch_shapes=[pltpu.VMEM((NW, D), F32), pltpu.SemaphoreType.DMA],
        compiler_params=_params(dimension_semantics=("arbitrary",)),
    )(*pieces, u)


def _in_bwd_x(pieces, wint, h, dh, gnorm, tm=512):
    S, D = h.shape
    nt = S // tm

    def body(*refs):
        piece_refs = refs[:len(SEGMENTS)]
        w_ref, h_ref, dh_ref, g_ref, dx_ref, dxh_ref, dg_ref = refs[len(SEGMENTS):]
        i = pl.program_id(0)
        du = jnp.zeros((tm, D), F32)
        for (off, width), ref in zip(SEGMENTS, piece_refs):
            du = du + _dot(ref[...].astype(BF16), w_ref[off:off + width, :])
        dx, dg_rows = _rms_bwd(du, h_ref[...], g_ref[...])
        out = dh_ref[...] + dx
        dx_ref[...] = out
        dxh_ref[...] = (0.5 * out).astype(BF16)
        _acc(dg_ref, jnp.sum(dg_rows, axis=0, keepdims=True), i == 0)

    def row(width):
        return pl.BlockSpec((tm, width), lambda i: (i, 0))

    vec = pl.BlockSpec((1, D), lambda i: (0, 0))
    return pl.pallas_call(
        body, name="in_bwd_x", grid=(nt,),
        in_specs=[row(width) for _, width in SEGMENTS] + [pl.BlockSpec(wint.shape, lambda i: (0, 0)), row(D), row(D), vec],
        out_specs=[row(D), row(D), vec],
        out_shape=[jax.ShapeDtypeStruct((S, D), F32), jax.ShapeDtypeStruct((S, D), BF16), jax.ShapeDtypeStruct((1, D), F32)],
        compiler_params=_params(dimension_semantics=("arbitrary",)),
    )(*pieces, wint, h, dh, gnorm)


def _row_tile(rows):
    for t in (512, 480, 352, 256, 128, 64, 32, 16, 8):
        if rows % t == 0:
            return t
    return rows


def _adamw(w, g, m, v, token, name):
    R, C = w.shape
    tr = _row_tile(R)

    def body(w_ref, g_ref, m_ref, v_ref, _, d_ref, mo_ref, vo_ref, go_ref):
        gv = g_ref[...]
        go_ref[...] = gv
        mn = ADAM_B1 * m_ref[...] + (1.0 - ADAM_B1) * gv
        vn = ADAM_B2 * v_ref[...] + (1.0 - ADAM_B2) * (gv * gv)
        m_hat = mn / (1.0 - ADAM_B1 ** ADAM_STEP)
        v_hat = vn / (1.0 - ADAM_B2 ** ADAM_STEP)
        d_ref[...] = -ADAM_LR * (m_hat / (jnp.sqrt(v_hat) + ADAM_EPS) + ADAM_WD * w_ref[...])
        mo_ref[...] = mn
        vo_ref[...] = vn

    blk = pl.BlockSpec((tr, C), lambda i: (i, 0))
    sh = jax.ShapeDtypeStruct((R, C), F32)
    return pl.pallas_call(
        body, name=name, grid=(R // tr,), in_specs=[blk] * 4 + [TOKEN_SPEC], out_specs=[blk] * 4, out_shape=[sh] * 4,
        compiler_params=_params(dimension_semantics=("arbitrary",)),
    )(w, g, m, v, token)


def _cast_place(w, place, name):
    R, C = w.shape
    tr = _row_tile(R)
    per = R // tr

    def body(p_ref, w_ref, o_ref):
        o_ref[...] = w_ref[...].astype(BF16)

    grid_spec = pltpu.PrefetchScalarGridSpec(
        num_scalar_prefetch=1, grid=(per,),
        in_specs=[pl.BlockSpec((tr, C), lambda i, p: (i, 0))],
        out_specs=pl.BlockSpec((tr, C), lambda i, p: (p[0] * per + i, 0)))
    return pl.pallas_call(
        body, name=name, grid_spec=grid_spec, out_shape=jax.ShapeDtypeStruct((N_SHARDS * R, C), BF16),
        compiler_params=_params(dimension_semantics=("arbitrary",)),
    )(place, w)


def _parked(w, m, idx, first, last):
    return jnp.where(m == w, idx, jnp.where(m < w, first, last))


def _sum_halves(g4s, recvs, place, name):
    M = len(g4s)
    NS, R, C = g4s[0].shape
    hr = R // 2
    tr = _row_tile(hr)
    per = hr // tr

    def body(p_ref, *refs):
        m = pl.program_id(0)
        for w in range(M):
            @pl.when(m == w)
            def _(w=w):
                refs[2 * M + w][...] = (refs[w][...] + refs[M + w][...]).astype(BF16)

    def spec(w, mine):
        def index(m, s, i, p):
            row = _parked(w, m, i, 0, per - 1)
            return _parked(w, m, s, 0, NS - 1), (p[1] * per + row if mine else row), 0

        return pl.BlockSpec((1, tr, C), index)

    grid_spec = pltpu.PrefetchScalarGridSpec(
        num_scalar_prefetch=1, grid=(M, NS, per),
        in_specs=[spec(w, True) for w in range(M)] + [spec(w, False) for w in range(M)],
        out_specs=[spec(w, False) for w in range(M)])
    return pl.pallas_call(
        body, name=name, grid_spec=grid_spec, out_shape=[jax.ShapeDtypeStruct((NS, hr, C), BF16)] * M,
        compiler_params=_params(dimension_semantics=("arbitrary",) * 3),
    )(place, *g4s, *recvs)


def _sum_quarters(h4s, recv3s, place, name):
    M = len(h4s)
    NS, hr, C = h4s[0].shape
    tr = _row_tile(hr)
    per = hr // tr

    def body(p_ref, *refs):
        m = pl.program_id(0)
        for w in range(M):
            @pl.when(m == w)
            def _(w=w):
                acc = refs[w][0].astype(F32)
                for k in range(N_SHARDS - 1):
                    acc = acc + refs[M + w][k].astype(F32)
                refs[2 * M + w][...] = acc

    def row(w, m, i):
        return _parked(w, m, i, 0, per - 1)

    grid_spec = pltpu.PrefetchScalarGridSpec(
        num_scalar_prefetch=1, grid=(M, per),
        in_specs=[pl.BlockSpec((1, tr, C), lambda m, i, p, w=w: (p[0], row(w, m, i), 0)) for w in range(M)]
        + [pl.BlockSpec((N_SHARDS - 1, tr, C), lambda m, i, p, w=w: (0, row(w, m, i), 0)) for w in range(M)],
        out_specs=[pl.BlockSpec((tr, C), lambda m, i, p, w=w: (p[1] * per + row(w, m, i), 0)) for w in range(M)])
    return pl.pallas_call(
        body, name=name, grid_spec=grid_spec, out_shape=[jax.ShapeDtypeStruct((2 * hr, C), F32)] * M,
        compiler_params=_params(dimension_semantics=("arbitrary",) * 2),
    )(place, *h4s, *recv3s)


def _by_shape(arrays):
    groups = {}
    for k, a in enumerate(arrays):
        groups.setdefault(a.shape, []).append(k)
    return list(groups.values())


def _place():
    x, y, c = lax.axis_index("x"), lax.axis_index("y"), lax.axis_index("c")
    chips = [(1 - x, y), (x, 1 - y), (1 - x, 1 - y)]
    return x, y, c, chips


HBM_SPEC = pl.BlockSpec(memory_space=pltpu.HBM)
SEM_SPEC = pl.BlockSpec(memory_space=pltpu.SEMAPHORE)
DATAFLOW = pltpu.SideEffectType.DATAFLOW_SIDE_EFFECTING


def _hbm(a):
    return pltpu.with_memory_space_constraint(a, pltpu.HBM)


class InFlight(NamedTuple):
    send_sem: jax.Array
    recv_sem: jax.Array
    bufs: list
    plan: Callable
    token: jax.Array


def _wait_all(plan, refs, send_ref, recv_ref):
    for k, (src, dst, dev) in enumerate(plan(refs)):
        cp = pltpu.make_async_remote_copy(src_ref=src, dst_ref=dst, send_sem=send_ref.at[k], recv_sem=recv_ref.at[k],
                                          device_id=dev, device_id_type=MESH)
        cp.wait_send()
        cp.wait_recv()


def _start_all(plan, refs, send_ref, recv_ref):
    for k, (src, dst, dev) in enumerate(plan(refs)):
        pltpu.make_async_remote_copy(src_ref=src, dst_ref=dst, send_sem=send_ref.at[k], recv_sem=recv_ref.at[k],
                                     device_id=dev, device_id_type=MESH).start()


def _split_start(name, bufs, plan, n_copies, after, waits=None):
    nb = len(bufs)
    after = list(after) if isinstance(after, (list, tuple)) else [after]

    def body(*refs):
        ins = refs[:nb]
        if waits is not None:
            _wait_all(waits.plan, ins, refs[nb], refs[nb + 1])
        send_sem, recv_sem, token = refs[-nb - 3], refs[-nb - 2], refs[-1]
        _start_all(plan, ins, send_sem, recv_sem)
        token[...] = jnp.zeros_like(token)

    sems = pltpu.SemaphoreType.DMA((n_copies,))
    earlier = [] if waits is None else [waits.send_sem, waits.recv_sem]
    out = pl.pallas_call(
        body, name=name, in_specs=[HBM_SPEC] * nb + [SEM_SPEC] * len(earlier) + [TOKEN_SPEC] * len(after),
        out_shape=(sems, sems, *[pltpu.HBM(b.shape, b.dtype) for b in bufs], jax.ShapeDtypeStruct((8, 128), F32)),
        out_specs=(SEM_SPEC, SEM_SPEC, *[HBM_SPEC] * nb, pl.BlockSpec(memory_space=pltpu.VMEM)),
        input_output_aliases={i: i + 2 for i in range(nb)},
        compiler_params=pltpu.CompilerParams(has_side_effects=DATAFLOW),
    )(*[_hbm(b) for b in bufs], *earlier, *after)
    return InFlight(out[0], out[1], list(out[2:2 + nb]), plan, out[-1])


def _split_wait(name, flight, after):
    nb = len(flight.bufs)
    after = list(after) if isinstance(after, (list, tuple)) else [after]

    def body(*refs):
        _wait_all(flight.plan, refs[:nb], refs[nb], refs[nb + 1])

    out = pl.pallas_call(
        body, name=name, in_specs=[HBM_SPEC] * nb + [SEM_SPEC, SEM_SPEC] + [TOKEN_SPEC] * len(after),
        out_shape=[pltpu.HBM(b.shape, b.dtype) for b in flight.bufs], out_specs=[HBM_SPEC] * nb,
        input_output_aliases={i: i for i in range(nb)},
        compiler_params=pltpu.CompilerParams(has_side_effects=DATAFLOW),
    )(*flight.bufs, flight.send_sem, flight.recv_sem, *after)
    return list(out)


def _half_rows(buf, chip, core):
    hr = buf.shape[0] // (2 * N_SHARDS)
    return buf.at[pl.ds(pl.multiple_of((2 * chip + core) * hr, 16), hr)]


def _gather_ici_plan(bufs):
    x, y, c, chips = _place()
    return [(_half_rows(b, 2 * x + y, c), _half_rows(b, 2 * x + y, c), (px, py, c)) for b in bufs for px, py in chips]


def _gather_d2d_plan(bufs):
    x, y, c, chips = _place()
    return [(_half_rows(b, 2 * px + py, c), _half_rows(b, 2 * px + py, c), (x, y, 1 - c)) for b in bufs for px, py in chips]


def _swap_plan(bufs):
    x, y, c, _ = _place()
    n = len(bufs) // 2
    copies = []
    for g, land in zip(bufs[:n], bufs[n:]):
        hr = g.shape[1] // 2
        copies.append((g.at[:, pl.ds(pl.multiple_of((1 - c) * hr, 8), hr)], land, (x, y, 1 - c)))
    return copies


def _exchange_plan(bufs):
    x, y, c, chips = _place()
    n = len(bufs) // 2
    return [(h.at[2 * px + py], land.at[k], (px, py, c))
            for h, land in zip(bufs[:n], bufs[n:]) for k, (px, py) in enumerate(chips)]


def _share_plan(bufs):
    x, y, c, _ = _place()
    copies = []
    for buf in bufs:
        hr = buf.shape[0] // 2
        mine = buf.at[pl.ds(pl.multiple_of(c * hr, 8), hr)]
        copies.append((mine, mine, (x, y, 1 - c)))
    return copies


N_DEVICES = 8


def _slot_place(vec, place):
    R, C = vec.shape

    def body(p_ref, v_ref, o_ref):
        o_ref[0] = v_ref[...]

    grid_spec = pltpu.PrefetchScalarGridSpec(
        num_scalar_prefetch=1, grid=(1,), in_specs=[pl.BlockSpec((R, C), lambda i, p: (0, 0))],
        out_specs=pl.BlockSpec((1, R, C), lambda i, p: (2 * p[0] + p[1], 0, 0)))
    return pl.pallas_call(
        body, name="slot_place", grid_spec=grid_spec, out_shape=jax.ShapeDtypeStruct((N_DEVICES, R, C), F32),
        compiler_params=_params(dimension_semantics=("arbitrary",)),
    )(place, vec)


def _slots_plan(bufs):
    x, y, c, _ = _place()
    mine = bufs[0].at[4 * x + 2 * y + c]
    return [(mine, mine, (x ^ (r >> 2), y ^ ((r >> 1) & 1), c ^ (r & 1))) for r in range(1, N_DEVICES)]


def _sum_slots(slots):
    _, R, C = slots.shape

    def body(s_ref, o_ref):
        acc = s_ref[0]
        for d in range(1, N_DEVICES):
            acc = acc + s_ref[d]
        o_ref[...] = acc

    return pl.pallas_call(
        body, name="sum_slots", in_specs=[pl.BlockSpec(memory_space=pltpu.VMEM)],
        out_specs=pl.BlockSpec(memory_space=pltpu.VMEM), out_shape=jax.ShapeDtypeStruct((R, C), F32),
        compiler_params=_params(),
    )(slots)


SMALL = ("ffn1_norm", "mix_norm", "pool_w", "pool_scale", "q_norm", "k_norm", "sinks", "gate_bias", "ffn2_norm")
SMALL_COLS = 1024
FFN1 = ("ffn1_w_gate", "ffn1_w_up", "ffn1_w_down")
MIXER = ("w_in", "w_pool_out", "w_attn_out", "w_out")
FFN2 = ("ffn2_w_gate", "ffn2_w_up", "ffn2_w_down")
LARGE = FFN1 + MIXER + FFN2
TRANSPOSED = ("ffn1_w_gate", "ffn1_w_up", "w_in", "w_pool_out", "ffn2_w_gate", "ffn2_w_up")
WEIGHTS = ("ffn1_norm", "ffn1_w_gate", "ffn1_w_up", "ffn1_w_down", "mix_norm", "w_in", "pool_w", "pool_scale",
           "w_pool_out", "q_norm", "k_norm", "sinks", "w_attn_out", "gate_bias", "w_out", "ffn2_norm",
           "ffn2_w_gate", "ffn2_w_up", "ffn2_w_down")


def _pack_small(parts):
    flat = jnp.concatenate([p.reshape(-1) for p in parts])
    rows = -(-flat.shape[0] // (8 * SMALL_COLS)) * 8
    return jnp.pad(flat, (0, rows * SMALL_COLS - flat.shape[0])).reshape(rows, SMALL_COLS)


def _unpack_small(packed, like):
    flat = packed.reshape(-1)
    out, off = [], 0
    for p in like:
        out.append(flat[off:off + p.size].reshape(p.shape))
        off += p.size
    return out, flat[off]


def _tie(small, token):
    return small + token[0, 0]


class Reduction:
    def __init__(self, group, names, grads, place):
        self.group, self.names, self.place = group, names, place
        self.bufs = [g.reshape(N_SHARDS, -1, g.shape[-1]) for g in grads]
        self.flight = None

    def _start(self, stage, bufs, plan, n_copies, after):
        self.flight = _split_start(f"{self.group}_{stage}", bufs, plan, n_copies, after)
        return self.flight.token

    def _landed(self, stage, after):
        n = len(self.names)
        bufs = _split_wait(f"{self.group}_{stage}_wait", self.flight, after)
        return bufs[:n], bufs[n:]

    def start_swap(self, after):
        lands = [lax.empty((g.shape[0], g.shape[1] // 2, g.shape[2]), g.dtype) for g in self.bufs]
        return self._start("swap", self.bufs + lands, _swap_plan, len(self.bufs), after)

    def start_exchange(self, after):
        g4, recv = self._landed("swap", after)
        halves = self._per_shape(_sum_halves, "sum_halves", g4, recv)
        lands = [lax.empty((N_SHARDS - 1,) + h.shape[1:], h.dtype) for h in halves]
        return self._start("exchange", halves + lands, _exchange_plan, (N_SHARDS - 1) * len(halves), self.place)

    def start_share(self, after):
        halves, recv3 = self._landed("exchange", after)
        reduced = self._per_shape(_sum_quarters, "sum_quarters", halves, recv3)
        return self._start("share", reduced, _share_plan, len(reduced), self.place)

    def _per_shape(self, add, stage, mine, received):
        out = [None] * len(mine)
        for idx in _by_shape(mine):
            sums = add([mine[k] for k in idx], [received[k] for k in idx], self.place, f"{stage}_{self.names[idx[0]]}")
            for k, v in zip(idx, sums):
                out[k] = v
        return out

    def finish(self, after):
        return dict(zip(self.names, _split_wait(f"{self.group}_share_wait", self.flight, after)))


def _row_form(name, a):
    return a.T if name in TRANSPOSED else a


def kernel(x, ffn1_norm, ffn1_w_gate, ffn1_w_up, ffn1_w_down, mix_norm, w_in, pool_w, pool_scale, w_pool_out, q_norm, k_norm, sinks, w_attn_out, gate_bias, w_out, ffn2_norm, ffn2_w_gate, ffn2_w_up, ffn2_w_down, loss_target, m_ffn1_norm, m_ffn1_w_gate, m_ffn1_w_up, m_ffn1_w_down, m_mix_norm, m_w_in, m_pool_w, m_pool_scale, m_w_pool_out, m_q_norm, m_k_norm, m_sinks, m_w_attn_out, m_gate_bias, m_w_out, m_ffn2_norm, m_ffn2_w_gate, m_ffn2_w_up, m_ffn2_w_down, v_ffn1_norm, v_ffn1_w_gate, v_ffn1_w_up, v_ffn1_w_down, v_mix_norm, v_w_in, v_pool_w, v_pool_scale, v_w_pool_out, v_q_norm, v_k_norm, v_sinks, v_w_attn_out, v_gate_bias, v_w_out, v_ffn2_norm, v_ffn2_w_gate, v_ffn2_w_up, v_ffn2_w_down):
    args = dict(locals())
    wts = {n: _row_form(n, args[n]) for n in WEIGHTS}
    mom = {n: _row_form(n, args["m_" + n]) for n in WEIGHTS}
    var = {n: _row_form(n, args["v_" + n]) for n in WEIGHTS}
    shard = 2 * lax.axis_index("x") + lax.axis_index("y")
    place = jnp.stack([shard, lax.axis_index("c")]).astype(jnp.int32)

    xs, target = x[0], loss_target[0]
    D = xs.shape[1]
    g1 = wts["ffn1_norm"].reshape(1, D)
    gm = wts["mix_norm"].reshape(1, D)
    g2 = wts["ffn2_norm"].reshape(1, D)
    qw = wts["q_norm"].reshape(1, HEAD_DIM)
    kw = wts["k_norm"].reshape(1, HEAD_DIM)
    bias = wts["gate_bias"].reshape(1, 2 * D)
    pscale = wts["pool_scale"].reshape(1, -1)
    pw = wts["pool_w"].astype(BF16)
    sink_rows = jnp.broadcast_to(jnp.repeat(wts["sinks"], ATTN_BLOCK).reshape(N_KV_HEADS, GQA_GROUP * ATTN_BLOCK, 1),
                                 (N_KV_HEADS, GQA_GROUP * ATTN_BLOCK, 128))

    first, down1 = FFN1[:2], FFN1[2:]
    placed = {n: _cast_place(wts[n], place, "cast_" + n) for n in FFN1}
    first_ici = _split_start("gather_first_ici", [placed[n] for n in first], _gather_ici_plan, 3 * len(first), place)
    placed.update({n: _cast_place(wts[n], place, "cast_" + n) for n in MIXER + FFN2})
    zero = jnp.zeros((1,), F32)
    packed = [_pack_small([d[n] for n in SMALL] + [zero]) for d in (wts, mom, var)]
    band = _band_bias()
    shadow = [placed[n] for n in down1 + MIXER + FFN2] + packed + [sink_rows, pw, band]
    first_d2d = _split_start("gather_first_d2d", first_ici.bufs, _gather_d2d_plan, 3 * len(first), shadow, waits=first_ici)
    down1_ici = _split_start("gather_down1_ici", [placed[n] for n in down1], _gather_ici_plan, 3 * len(down1), first_d2d.token)
    w = dict(zip(first, _split_wait("gather_first_wait", first_d2d, down1_ici.token)))
    mix_ici = _split_start("gather_mix_ici", [placed[n] for n in MIXER], _gather_ici_plan, 3 * len(MIXER), w[first[0]])
    ffn2_ici = _split_start("gather_ffn2_ici", [placed[n] for n in FFN2], _gather_ici_plan, 3 * len(FFN2), mix_ici.token)

    n1, a1, b1, s1 = _ffn_up(xs, _tie(g1, ffn2_ici.token), w["ffn1_w_gate"], w["ffn1_w_up"], "ffn1_up")
    down1_d2d = _split_start("gather_down1_d2d", down1_ici.bufs, _gather_d2d_plan, 3 * len(down1), s1, waits=down1_ici)
    w.update(zip(down1, _split_wait("gather_down1_wait", down1_d2d, down1_d2d.token)))
    h1 = _ffn_down(xs, s1, w["ffn1_w_down"], "ffn1_down")
    mix_d2d = _split_start("gather_mix_d2d", mix_ici.bufs, _gather_d2d_plan, 3 * len(MIXER), h1, waits=mix_ici)
    w.update(zip(MIXER, _split_wait("gather_mix_wait", mix_d2d, mix_d2d.token)))
    u, xp, q, kv, gl = _mix_proj(h1, gm, w["w_in"])
    attn = _attn_fwd(q, kv, qw, kw, sink_rows, band)
    ffn2_d2d = _split_start("gather_ffn2_d2d", ffn2_ici.bufs, _gather_d2d_plan, 3 * len(FFN2), attn, waits=ffn2_ici)
    h2, bp, ba = _mix_out(xp, attn, gl, _tie(bias, ffn2_d2d.token), pw, pscale, w["w_pool_out"], w["w_attn_out"], w["w_out"], h1)
    w.update(zip(FFN2, _split_wait("gather_ffn2_wait", ffn2_d2d, h2)))
    n2, a2, b2, s2 = _ffn_up(h2, g2, w["ffn2_w_gate"], w["ffn2_w_up"], "ffn2_up")
    dy, dyh, loss = _ffn_down_loss(h2, s2, w["ffn2_w_down"], target, "ffn2_down_loss")

    gw, gs = {}, {}
    da, db = _ffn_bwd_h(dyh, a2, b2, w["ffn2_w_down"], ffn2_d2d.token, "ffn2_bwd_h")
    gw["ffn2_w_gate"], gw["ffn2_w_up"] = _xty(da, n2, "ffn2_dw_gate"), _xty(db, n2, "ffn2_dw_up")
    gw["ffn2_w_down"] = _xty(s2, dyh, "ffn2_dw_down")
    red2 = Reduction("reduce_ffn2", FFN2, [gw[n] for n in FFN2], place)
    token = red2.start_swap(da)
    dh2, gs["ffn2_norm"] = _ffn_bwd_x(da, db, w["ffn2_w_gate"], w["ffn2_w_up"], h2, dy, _tie(g2, token), "ffn2_bwd_x")
    token = red2.start_exchange(dh2)
    dgl, dbp, dba, gw["w_out"], gs["gate_bias"] = _mix_bwd_gate(dh2, w["w_out"], bp, ba, gl, _tie(bias, token))
    dattn, dpooled, gw["w_attn_out"], gw["w_pool_out"], gs["pool_w"], gs["pool_scale"] = _mix_bwd_branch(
        dbp, dba, attn, xp, pw, pscale, w["w_pool_out"], w["w_attn_out"])
    dq, dkv, gs["q_norm"], gs["k_norm"], dsk = _attn_bwd(q, kv, dattn, qw, kw, sink_rows, band)
    gs["sinks"] = dsk[:, :, 0]
    red2.start_share(dq)
    pieces = (_pool_bwd(dpooled), dq, dkv, dgl)
    gw["w_in"] = _in_bwd_w(pieces, u)
    grads = red2.finish(gw["w_in"])
    redm = Reduction("reduce_mix", MIXER, [gw[n] for n in MIXER], place)
    token = redm.start_swap(grads[FFN2[0]])
    dh1, dh1h, gs["mix_norm"] = _in_bwd_x(pieces, w["w_in"], h1, dh2, _tie(gm, token))
    token = redm.start_exchange(dh1)
    da, db = _ffn_bwd_h(dh1h, a1, b1, w["ffn1_w_down"], token, "ffn1_bwd_h")
    gw["ffn1_w_gate"], gw["ffn1_w_up"] = _xty(da, n1, "ffn1_dw_gate"), _xty(db, n1, "ffn1_dw_up")
    gw["ffn1_w_down"] = _xty(s1, dh1h, "ffn1_dw_down")
    token = redm.start_share([gw[n] for n in FFN1])
    red1 = Reduction("reduce_ffn1", FFN1, [gw[n] for n in FFN1], place)
    token = red1.start_swap(token)
    grad_x, gs["ffn1_norm"] = _ffn_bwd_x(da, db, w["ffn1_w_gate"], w["ffn1_w_up"], xs, dh1, _tie(g1, token), "ffn1_bwd_x")
    grads.update(redm.finish(grad_x))

    small_parts = [gs[n] for n in SMALL] + [loss[0, 0].reshape(1)]
    slots = _split_start("gather_small", [_slot_place(_pack_small(small_parts), place)], _slots_plan, N_DEVICES - 1,
                         grads[MIXER[0]])
    token = red1.start_exchange(slots.token)
    delta, new_m, new_v = {}, {}, {}
    for n in FFN2 + MIXER:
        delta[n], new_m[n], new_v[n], grads[n] = _adamw(wts[n], grads[n], mom[n], var[n], token, "adamw_" + n)
    summed = _sum_slots(_split_wait("gather_small_wait", slots, delta[MIXER[-1]])[0])
    small_grads, loss_sum = _unpack_small(summed, [wts[n] for n in SMALL])
    grads.update(dict(zip(SMALL, small_grads)))
    ds, ms, vs, _ = _adamw(packed[0], summed, packed[1], packed[2], token, "adamw_small")
    like = [wts[n] for n in SMALL]
    for out, packed_out in ((delta, ds), (new_m, ms), (new_v, vs)):
        out.update(dict(zip(SMALL, _unpack_small(packed_out, like)[0])))
    results = {(k, n): _row_form(n, d[n]) for k, d in enumerate((grads, delta, new_m, new_v)) for n in SMALL + FFN2 + MIXER}
    token = red1.start_share([results[k, n] for k in range(4) for n in SMALL + ("w_pool_out",)])
    grads.update(red1.finish(token))
    for n in FFN1:
        delta[n], new_m[n], new_v[n], grads[n] = _adamw(wts[n], grads[n], mom[n], var[n], token, "adamw_" + n)
    results.update({(k, n): _row_form(n, d[n]) for k, d in enumerate((grads, delta, new_m, new_v)) for n in FFN1})
    return (loss_sum, grad_x[None], *[results[k, n] for k in range(4) for n in WEIGHTS])
```

```python
from typing import Callable, NamedTuple

import jax
import jax.numpy as jnp
from jax import lax
from jax.experimental import pallas as pl
from jax.experimental.pallas import tpu as pltpu

F32 = jnp.float32
BF16 = jnp.bfloat16
RMS_EPS = 1e-6
POOL_WINDOWS = (2, 4, 8, 16)
POOL_GROUP = 128
POOL_HALO = 16
HEAD_DIM = 64
GQA_GROUP = 8
N_KV_HEADS = 2
ATTN_BLOCK = 128
SCALE = HEAD_DIM ** -0.5
NEG = -1e30
N_SHARDS = 4
ADAM_LR, ADAM_B1, ADAM_B2, ADAM_EPS, ADAM_WD, ADAM_STEP = 0.001, 0.9, 0.999, 1e-08, 0.01, 10
VMEM_LIMIT = 56 * 1024 * 1024
MESH = pl.DeviceIdType.MESH
SEG_POOL, SEG_Q, SEG_KV, SEG_GATE = (0, 512), (512, 1024), (1536, 256), (1792, 2048)
SEGMENTS = (SEG_POOL, SEG_Q, SEG_KV, SEG_GATE)


def _params(**kw):
    return pltpu.CompilerParams(vmem_limit_bytes=VMEM_LIMIT, **kw)


def _dot(a, b):
    return jnp.dot(a, b, preferred_element_type=F32)


def _dot_nt(a, b):
    return lax.dot_general(a, b, (((1,), (1,)), ((), ())), preferred_element_type=F32)


def _dot_tn(a, b):
    return lax.dot_general(a, b, (((0,), (0,)), ((), ())), preferred_element_type=F32)


def _rinv(x):
    return lax.rsqrt(jnp.mean(x * x, axis=-1, keepdims=True) + RMS_EPS)


def _rms_bwd(dn, x, g):
    r = _rinv(x)
    xr = x * r
    z = dn * g
    dx = r * (z - xr * jnp.mean(z * xr, axis=-1, keepdims=True))
    return dx, dn * xr


def _acc(ref, val, first):
    @pl.when(first)
    def _():
        ref[...] = val

    @pl.when(jnp.logical_not(first))
    def _():
        ref[...] += val


TOKEN_SPEC = pl.BlockSpec(memory_space=pl.ANY)
F_HALF = 1408


def _resident(w):
    return pl.BlockSpec(w.shape, lambda i: (0, 0), pipeline_mode=pl.Buffered(1))


def _ffn_up(h, gnorm, wgt, wut, name, tm=512):
    S, D = h.shape
    F = wgt.shape[0]

    def body(h_ref, g_ref, wg_ref, wu_ref, n_ref, a_ref, b_ref, s_ref):
        x = h_ref[...]
        n = (x * _rinv(x) * g_ref[...]).astype(BF16)
        n_ref[...] = n
        for lo in range(0, F, F_HALF):
            cols = slice(lo, lo + F_HALF)
            a = _dot_nt(n, wg_ref[cols, :])
            b = _dot_nt(n, wu_ref[cols, :])
            a_ref[:, cols] = a.astype(BF16)
            b_ref[:, cols] = b.astype(BF16)
            s_ref[:, cols] = (a * jax.nn.sigmoid(a) * b).astype(BF16)

    act = pl.BlockSpec((tm, F), lambda i: (i, 0))
    hidden = jax.ShapeDtypeStruct((S, F), BF16)
    return pl.pallas_call(
        body, name=name, grid=(S // tm,),
        in_specs=[pl.BlockSpec((tm, D), lambda i: (i, 0)), pl.BlockSpec((1, D), lambda i: (0, 0)), _resident(wgt), _resident(wut)],
        out_specs=[pl.BlockSpec((tm, D), lambda i: (i, 0)), act, act, act],
        out_shape=[jax.ShapeDtypeStruct((S, D), BF16), hidden, hidden, hidden],
        compiler_params=_params(dimension_semantics=("arbitrary",)),
    )(h, gnorm, wgt, wut)


def _ffn_gate(h, gnorm, wgt, name, tm=512):
    S, D = h.shape
    F = wgt.shape[0]

    def body(h_ref, g_ref, wg_ref, n_ref, a_ref):
        x = h_ref[...]
        n = (x * _rinv(x) * g_ref[...]).astype(BF16)
        n_ref[...] = n
        for lo in range(0, F, F_HALF):
            a_ref[:, lo:lo + F_HALF] = _dot_nt(n, wg_ref[lo:lo + F_HALF, :]).astype(BF16)

    row = pl.BlockSpec((tm, D), lambda i: (i, 0))
    return pl.pallas_call(
        body, name=name, grid=(S // tm,), in_specs=[row, pl.BlockSpec((1, D), lambda i: (0, 0)), _resident(wgt)],
        out_specs=[row, pl.BlockSpec((tm, F), lambda i: (i, 0))],
        out_shape=[jax.ShapeDtypeStruct((S, D), BF16), jax.ShapeDtypeStruct((S, F), BF16)],
        compiler_params=_params(dimension_semantics=("arbitrary",)),
    )(h, gnorm, wgt)


def _ffn_hidden(n, a, wut, name, tm=512):
    S, D = n.shape
    F = wut.shape[0]

    def body(n_ref, a_ref, wu_ref, b_ref, s_ref):
        n_t = n_ref[...]
        for lo in range(0, F, F_HALF):
            cols = slice(lo, lo + F_HALF)
            a = a_ref[:, cols].astype(F32)
            b = _dot_nt(n_t, wu_ref[cols, :])
            b_ref[:, cols] = b.astype(BF16)
            s_ref[:, cols] = (a * jax.nn.sigmoid(a) * b).astype(BF16)

    act = pl.BlockSpec((tm, F), lambda i: (i, 0))
    hidden = jax.ShapeDtypeStruct((S, F), BF16)
    return pl.pallas_call(
        body, name=name, grid=(S // tm,), in_specs=[pl.BlockSpec((tm, D), lambda i: (i, 0)), act, _resident(wut)],
        out_specs=[act, act], out_shape=[hidden, hidden],
        compiler_params=_params(dimension_semantics=("arbitrary",)),
    )(n, a, wut)


def _ffn_down(h, s, wd, name, tm=1024):
    S, D = h.shape
    F = wd.shape[0]

    def body(h_ref, s_ref, wd_ref, o_ref):
        o_ref[...] = h_ref[...] + 0.5 * _dot(s_ref[...], wd_ref[...])

    row = pl.BlockSpec((tm, D), lambda i: (i, 0))
    return pl.pallas_call(
        body, name=name, grid=(S // tm,), in_specs=[row, pl.BlockSpec((tm, F), lambda i: (i, 0)), _resident(wd)],
        out_specs=row, out_shape=jax.ShapeDtypeStruct((S, D), F32),
        compiler_params=_params(dimension_semantics=("arbitrary",)),
    )(h, s, wd)


def _ffn_down_loss(h, s, wd, target, name, tm=512):
    S, D = h.shape
    F = wd.shape[0]

    def body(h_ref, s_ref, wd_ref, t_ref, dy_ref, dyh_ref, loss_ref):
        e = h_ref[...] + 0.5 * _dot(s_ref[...], wd_ref[...]) - t_ref[...]
        dy = e * (1.0 / D)
        dy_ref[...] = dy
        dyh_ref[...] = (0.5 * dy).astype(BF16)
        tot = jnp.sum(jnp.sum(e * e, axis=1, keepdims=True), axis=0, keepdims=True) * (0.5 / D)
        _acc(loss_ref, jnp.broadcast_to(tot, loss_ref.shape), pl.program_id(0) == 0)

    row = pl.BlockSpec((tm, D), lambda i: (i, 0))
    return pl.pallas_call(
        body, name=name, grid=(S // tm,), in_specs=[row, pl.BlockSpec((tm, F), lambda i: (i, 0)), _resident(wd), row],
        out_specs=[row, row, pl.BlockSpec((8, 128), lambda i: (0, 0))],
        out_shape=[jax.ShapeDtypeStruct((S, D), F32), jax.ShapeDtypeStruct((S, D), BF16), jax.ShapeDtypeStruct((8, 128), F32)],
        compiler_params=_params(dimension_semantics=("arbitrary",)),
    )(h, s, wd, target)


def _ffn_bwd_h(dyh, a, b, wd, token, name, tm=512):
    S, D = dyh.shape
    F = wd.shape[0]

    def body(dyh_ref, a_ref, b_ref, wd_ref, _, da_ref, db_ref):
        dyh_t = dyh_ref[...]
        for lo in range(0, F, F_HALF):
            cols = slice(lo, lo + F_HALF)
            av = a_ref[:, cols].astype(F32)
            bv = b_ref[:, cols].astype(F32)
            ds = _dot_nt(dyh_t, wd_ref[cols, :])
            sig = jax.nn.sigmoid(av)
            silu = av * sig
            da_ref[:, cols] = (ds * bv * (sig * (1.0 + av * (1.0 - sig)))).astype(BF16)
            db_ref[:, cols] = (ds * silu).astype(BF16)

    act = pl.BlockSpec((tm, F), lambda i: (i, 0))
    hidden = jax.ShapeDtypeStruct((S, F), BF16)
    return pl.pallas_call(
        body, name=name, grid=(S // tm,),
        in_specs=[pl.BlockSpec((tm, D), lambda i: (i, 0)), act, act, _resident(wd), TOKEN_SPEC],
        out_specs=[act, act], out_shape=[hidden, hidden],
        compiler_params=_params(dimension_semantics=("arbitrary",)),
    )(dyh, a, b, wd, token)


def _xty(x, y, name, tk=2048, tf=1408):
    S, F = x.shape
    D = y.shape[1]

    def body(x_ref, y_ref, o_ref):
        _acc(o_ref, _dot_tn(x_ref[...], y_ref[...]), pl.program_id(1) == 0)

    return pl.pallas_call(
        body, name=name, grid=(F // tf, S // tk),
        in_specs=[pl.BlockSpec((tk, tf), lambda j, k: (k, j)), pl.BlockSpec((tk, D), lambda j, k: (k, 0))],
        out_specs=pl.BlockSpec((tf, D), lambda j, k: (j, 0)), out_shape=jax.ShapeDtypeStruct((F, D), F32),
        compiler_params=_params(dimension_semantics=("arbitrary", "arbitrary")),
    )(x, y)


def _ffn_bwd_x(da, db, wgt, wut, x, dh, gnorm, name, tm=512):
    S, D = x.shape
    F = wgt.shape[0]

    def body(da_ref, db_ref, wg_ref, wu_ref, x_ref, dh_ref, g_ref, dx_ref, dg_ref):
        dn = _dot(da_ref[...], wg_ref[...]) + _dot(db_ref[...], wu_ref[...])
        dx, dg_rows = _rms_bwd(dn, x_ref[...], g_ref[...])
        dx_ref[...] = dh_ref[...] + dx
        _acc(dg_ref, jnp.sum(dg_rows, axis=0, keepdims=True), pl.program_id(0) == 0)

    row = pl.BlockSpec((tm, D), lambda i: (i, 0))
    act = pl.BlockSpec((tm, F), lambda i: (i, 0))
    vec = pl.BlockSpec((1, D), lambda i: (0, 0))
    return pl.pallas_call(
        body, name=name, grid=(S // tm,), in_specs=[act, act, _resident(wgt), _resident(wut), row, row, vec],
        out_specs=[row, vec],
        out_shape=[jax.ShapeDtypeStruct((S, D), F32), jax.ShapeDtypeStruct((1, D), F32)],
        compiler_params=_params(dimension_semantics=("arbitrary",)),
    )(da, db, wgt, wut, x, dh, gnorm)


def _mix_proj(h, gnorm, wint, tm=512):
    S, D = h.shape
    nt = S // tm

    def body(h_ref, g_ref, w_ref, u_ref, xp_ref, q_ref, kv_ref, gl_ref):
        x = h_ref[...]
        u = (x * _rinv(x) * g_ref[...]).astype(BF16)
        u_ref[...] = u
        for (off, width), ref in zip(SEGMENTS, (xp_ref, q_ref, kv_ref, gl_ref)):
            ref[...] = _dot_nt(u, w_ref[off:off + width, :]).astype(ref.dtype)

    def row(width):
        return pl.BlockSpec((tm, width), lambda i: (i, 0))

    return pl.pallas_call(
        body, name="mix_proj", grid=(nt,),
        in_specs=[row(D), pl.BlockSpec((1, D), lambda i: (0, 0)), pl.BlockSpec(wint.shape, lambda i: (0, 0))],
        out_specs=[row(D), row(SEG_POOL[1]), row(SEG_Q[1]), row(SEG_KV[1]), row(SEG_GATE[1])],
        out_shape=[jax.ShapeDtypeStruct((S, D), BF16), jax.ShapeDtypeStruct((S, SEG_POOL[1]), F32),
                   jax.ShapeDtypeStruct((S, SEG_Q[1]), BF16), jax.ShapeDtypeStruct((S, SEG_KV[1]), BF16),
                   jax.ShapeDtypeStruct((S, SEG_GATE[1]), BF16)],
        compiler_params=_params(dimension_semantics=("arbitrary",)),
    )(h, gnorm, wint)


def _stack_heads(x, g):
    return jnp.concatenate([x[:, (GQA_GROUP * g + hh) * HEAD_DIM:(GQA_GROUP * g + hh + 1) * HEAD_DIM]
                            for hh in range(GQA_GROUP)], axis=0)


def _unstack_heads(ref, val, g):
    for hh in range(GQA_GROUP):
        lo = (GQA_GROUP * g + hh) * HEAD_DIM
        ref[:, lo:lo + HEAD_DIM] = val[hh * ATTN_BLOCK:(hh + 1) * ATTN_BLOCK, :]


def _rowsum(xb, width):
    return _dot(xb, jnp.ones((xb.shape[1], width), BF16))


def _rinv_lanes(x):
    return lax.rsqrt(_rowsum((x * x).astype(BF16), x.shape[1]) * (1.0 / x.shape[1]) + RMS_EPS)


def _twice(x):
    return jnp.concatenate([x, x], axis=1)


def _band_bias():
    qi = (jnp.arange(GQA_GROUP * ATTN_BLOCK) % ATTN_BLOCK)[:, None]
    kj = jnp.arange(2 * ATTN_BLOCK)[None, :]
    return jnp.where(jnp.logical_and(kj > qi, kj <= qi + ATTN_BLOCK), 0.0, NEG).astype(F32)


def _attn_exp(qn, kk, sink, band, n):
    kj = lax.broadcasted_iota(jnp.int32, (1, 2 * ATTN_BLOCK), 1)
    before_sequence = jnp.where(jnp.logical_and(kj < ATTN_BLOCK, n == 0), NEG, 0.0)
    s = _dot_nt(qn, kk) + band + before_sequence
    m = jnp.maximum(jnp.broadcast_to(jnp.max(s, axis=-1, keepdims=True), sink.shape), sink)
    e = jnp.exp(s - _twice(m))
    e_s = jnp.exp(sink - m)
    e_b = e.astype(BF16)
    inv = 1.0 / (_rowsum(e_b, ATTN_BLOCK) + e_s)
    return e_b, e, e_s, inv


def _attn_blocks(n):
    cur = pl.multiple_of(n * ATTN_BLOCK, ATTN_BLOCK)
    prev = pl.multiple_of(jnp.maximum(n - 1, 0) * ATTN_BLOCK, ATTN_BLOCK)
    return cur, prev


def _kv_window(kv_ref, n):
    cur, prev = _attn_blocks(n)
    return jnp.concatenate([kv_ref[pl.ds(prev, ATTN_BLOCK), :], kv_ref[pl.ds(cur, ATTN_BLOCK), :]], axis=0).astype(F32)


def _kv_split(kv, g):
    k = kv[:, g * HEAD_DIM:(g + 1) * HEAD_DIM]
    v = kv[:, (N_KV_HEADS + g) * HEAD_DIM:(N_KV_HEADS + g + 1) * HEAD_DIM]
    return k, v


def _attn_fwd(q, kv, qw, kw, sink_rows, band):
    S, W = q.shape
    nb = S // ATTN_BLOCK

    def body(q_ref, kv_ref, qw_ref, kw_ref, sk_ref, band_ref, o_ref, o_scr):
        n = pl.program_id(0)
        qf = q_ref[...].astype(F32)
        kvw = _kv_window(kv_ref, n)
        for g in range(N_KV_HEADS):
            qs = _stack_heads(qf, g)
            qn = (qs * _rinv_lanes(qs) * qw_ref[...] * SCALE).astype(BF16)
            k, v = _kv_split(kvw, g)
            kk = (k * _rinv_lanes(k) * kw_ref[...]).astype(BF16)
            e_b, _, _, inv = _attn_exp(qn, kk, sk_ref[g], band_ref[...], n)
            _unstack_heads(o_scr, _dot(e_b, v.astype(BF16)) * inv[:, :HEAD_DIM], g)
        o_ref[...] = o_scr[...].astype(BF16)

    blk = pl.BlockSpec((ATTN_BLOCK, W), lambda n: (n, 0))
    return pl.pallas_call(
        body, name="attn_fwd", grid=(nb,),
        in_specs=[blk, pl.BlockSpec(kv.shape, lambda n: (0, 0)), pl.BlockSpec((1, HEAD_DIM), lambda n: (0, 0)),
                  pl.BlockSpec((1, HEAD_DIM), lambda n: (0, 0)), pl.BlockSpec(sink_rows.shape, lambda n: (0, 0, 0)),
                  pl.BlockSpec(band.shape, lambda n: (0, 0))],
        out_specs=blk, out_shape=jax.ShapeDtypeStruct((S, W), BF16),
        scratch_shapes=[pltpu.VMEM((ATTN_BLOCK, W), F32)],
        compiler_params=_params(dimension_semantics=("arbitrary",)),
    )(q, kv, qw, kw, sink_rows, band)


def _attn_bwd(q, kv, do, qw, kw, sink_rows, band):
    S, W = q.shape
    KW = kv.shape[1]
    nb = S // ATTN_BLOCK
    chunk = 512

    def body(q_ref, kv_ref, do_ref, qw_ref, kw_ref, sk_ref, band_ref, dq_ref, dkv_ref, dqw_ref, dkw_ref, dsk_ref, dq_scr):
        n = pl.program_id(0)

        @pl.when(n == 0)
        def _():
            dkv_ref[...] = jnp.zeros_like(dkv_ref)
            dqw_ref[...] = jnp.zeros_like(dqw_ref)
            dsk_ref[...] = jnp.zeros_like(dsk_ref)

        qf = q_ref[...].astype(F32)
        dof = do_ref[...].astype(F32)
        cur, prev = _attn_blocks(n)
        kvw = _kv_window(kv_ref, n)
        qw_v = qw_ref[...]
        for g in range(N_KV_HEADS):
            qs = _stack_heads(qf, g)
            rq = _rinv_lanes(qs)
            qhat = qs * rq
            qn = (qhat * qw_v * SCALE).astype(BF16)
            k, v = _kv_split(kvw, g)
            kk = (k * _rinv_lanes(k) * kw_ref[...]).astype(BF16)
            vv = v.astype(BF16)
            dos = _stack_heads(dof, g).astype(BF16)
            _, e, e_s, inv = _attn_exp(qn, kk, sk_ref[g], band_ref[...], n)
            p = e * _twice(inv)
            dp = _dot_nt(dos, vv)
            drow = _rowsum((p * dp).astype(BF16), ATTN_BLOCK)
            ds = (p * (dp - _twice(drow))).astype(BF16)
            dsink = -(e_s * inv * drow)
            for hh in range(GQA_GROUP):
                tot = jnp.sum(dsink[hh * ATTN_BLOCK:(hh + 1) * ATTN_BLOCK, :], axis=0, keepdims=True)
                dsk_ref[g, hh:hh + 1, :] += tot
            dqn = _dot(ds, kk) * SCALE
            dkk = _dot_tn(ds, qn)
            dvv = _dot_tn(p.astype(BF16), dos)
            klo, vlo = g * HEAD_DIM, (N_KV_HEADS + g) * HEAD_DIM
            for start, rows in ((prev, slice(0, ATTN_BLOCK)), (cur, slice(ATTN_BLOCK, 2 * ATTN_BLOCK))):
                dkv_ref[pl.ds(start, ATTN_BLOCK), klo:klo + HEAD_DIM] += dkk[rows]
                dkv_ref[pl.ds(start, ATTN_BLOCK), vlo:vlo + HEAD_DIM] += dvv[rows]
            dqw_ref[...] += jnp.sum(dqn * qhat, axis=0, keepdims=True)
            z = dqn * qw_v
            dqs = rq * (z - qhat * (_rowsum((z * qhat).astype(BF16), HEAD_DIM) * (1.0 / HEAD_DIM)))
            _unstack_heads(dq_scr, dqs, g)
        dq_ref[...] = dq_scr[...].astype(BF16)

        @pl.when(n == nb - 1)
        def _():
            def one(c, dkw):
                rows = pl.ds(pl.multiple_of(c * chunk, chunk), chunk)
                for g in range(N_KV_HEADS):
                    lo = g * HEAD_DIM
                    k = kv_ref[rows, lo:lo + HEAD_DIM].astype(F32)
                    dx, dg_rows = _rms_bwd(dkv_ref[rows, lo:lo + HEAD_DIM], k, kw_ref[...])
                    dkv_ref[rows, lo:lo + HEAD_DIM] = dx
                    dkw = dkw + jnp.sum(dg_rows, axis=0, keepdims=True)
                return dkw

            dkw_ref[...] = lax.fori_loop(0, S // chunk, one, jnp.zeros((1, HEAD_DIM), F32))

    blk = pl.BlockSpec((ATTN_BLOCK, W), lambda n: (n, 0))
    whole_kv = pl.BlockSpec((S, KW), lambda n: (0, 0))
    vec = pl.BlockSpec((1, HEAD_DIM), lambda n: (0, 0))
    sk = pl.BlockSpec(sink_rows.shape, lambda n: (0, 0, 0))
    dsk = pl.BlockSpec((N_KV_HEADS, GQA_GROUP, 128), lambda n: (0, 0, 0))
    return pl.pallas_call(
        body, name="attn_bwd", grid=(nb,), in_specs=[blk, whole_kv, blk, vec, vec, sk, pl.BlockSpec(band.shape, lambda n: (0, 0))],
        out_specs=[blk, whole_kv, vec, vec, dsk],
        out_shape=[jax.ShapeDtypeStruct((S, W), BF16), jax.ShapeDtypeStruct((S, KW), F32),
                   jax.ShapeDtypeStruct((1, HEAD_DIM), F32), jax.ShapeDtypeStruct((1, HEAD_DIM), F32),
                   jax.ShapeDtypeStruct((N_KV_HEADS, GQA_GROUP, 128), F32)],
        scratch_shapes=[pltpu.VMEM((ATTN_BLOCK, W), F32)],
        compiler_params=_params(dimension_semantics=("arbitrary",)),
    )(q, kv, do, qw, kw, sink_rows, band)


def _pooled(xc, xprev, i):
    tm = xc.shape[0]
    xh = jnp.concatenate([jnp.where(i > 0, xprev, 0.0), xc], axis=0)
    t = lax.broadcasted_iota(jnp.int32, (tm, 1), 0) + i * tm
    out = []
    for gi, w in enumerate(POOL_WINDOWS):
        acc = xh[:, gi * POOL_GROUP:(gi + 1) * POOL_GROUP]
        sh = 1
        while sh < w:
            acc = acc + pltpu.roll(acc, sh, 0)
            sh *= 2
        cnt = jnp.minimum(t + 1, w).astype(F32)
        out.append(acc[POOL_HALO:, :] / cnt - xc[:, gi * POOL_GROUP:(gi + 1) * POOL_GROUP])
    return jnp.concatenate(out, axis=1)


def _pool_mix(pooled_b, pw_ref):
    return jnp.concatenate([_dot(pooled_b[:, gi * POOL_GROUP:(gi + 1) * POOL_GROUP], pw_ref[gi])
                            for gi in range(len(POOL_WINDOWS))], axis=1)


def _halo_specs(tm, width, S, after):
    per = tm // POOL_HALO
    last = S // POOL_HALO - 1
    if after:
        return pl.BlockSpec((POOL_HALO, width), lambda i: (jnp.minimum((i + 1) * per, last), 0))
    return pl.BlockSpec((POOL_HALO, width), lambda i: (jnp.maximum(i * per - 1, 0), 0))


def _mix_out(xp, attn, gl, bias, pw, pscale, wpot, wao, wo, h, tm=512):
    S, D = h.shape
    nt = S // tm
    PW = xp.shape[1]

    def body(xc_ref, xprev_ref, at_ref, gl_ref, bias_ref, pw_ref, ps_ref, wpo_ref, wao_ref, wo_ref, h_ref,
             ho_ref, bp_ref, ba_ref):
        i = pl.program_id(0)
        pooled = _pooled(xc_ref[...], xprev_ref[...], i).astype(BF16)
        ms = (_pool_mix(pooled, pw_ref) * ps_ref[...]).astype(BF16)
        bp = _dot_nt(ms, wpo_ref[...])
        ba = _dot(at_ref[...], wao_ref[...])
        bp_ref[...] = bp.astype(BF16)
        ba_ref[...] = ba.astype(BF16)
        gates = jax.nn.sigmoid(gl_ref[...].astype(F32) + bias_ref[...])
        merged = (gates[:, :D] * bp + gates[:, D:] * ba).astype(BF16)
        ho_ref[...] = h_ref[...] + _dot(merged, wo_ref[...])

    def row(width):
        return pl.BlockSpec((tm, width), lambda i: (i, 0))

    def whole(x):
        nd = x.ndim
        return pl.BlockSpec(x.shape, lambda i: (0,) * nd)

    return pl.pallas_call(
        body, name="mix_out", grid=(nt,),
        in_specs=[row(PW), _halo_specs(tm, PW, S, False), row(D), row(2 * D), whole(bias), whole(pw), whole(pscale),
                  whole(wpot), whole(wao), whole(wo), row(D)],
        out_specs=[row(D), row(D), row(D)],
        out_shape=[jax.ShapeDtypeStruct((S, D), F32), jax.ShapeDtypeStruct((S, D), BF16),
                   jax.ShapeDtypeStruct((S, D), BF16)],
        compiler_params=_params(dimension_semantics=("arbitrary",)),
    )(xp, xp, attn, gl, bias, pw, pscale, wpot, wao, wo, h)


def _mix_bwd_gate(dh, wo, bp, ba, gl, bias, tm=512):
    S, D = dh.shape
    nt = S // tm

    def body(dh_ref, wo_ref, bp_ref, ba_ref, gl_ref, bias_ref, dgl_ref, dbp_ref, dba_ref, dwo_ref, dbias_ref):
        i = pl.program_id(0)
        dhb = dh_ref[...].astype(BF16)
        dm = _dot_nt(dhb, wo_ref[...])
        gates = jax.nn.sigmoid(gl_ref[...].astype(F32) + bias_ref[...])
        gp, ga = gates[:, :D], gates[:, D:]
        bp_v = bp_ref[...].astype(F32)
        ba_v = ba_ref[...].astype(F32)
        merged = (gp * bp_v + ga * ba_v).astype(BF16)
        _acc(dwo_ref, _dot_tn(merged, dhb), i == 0)
        dbp_ref[...] = (dm * gp).astype(BF16)
        dba_ref[...] = (dm * ga).astype(BF16)
        dgl = jnp.concatenate([dm * bp_v * gp * (1.0 - gp), dm * ba_v * ga * (1.0 - ga)], axis=1)
        dgl_ref[...] = dgl.astype(BF16)
        _acc(dbias_ref, jnp.sum(dgl, axis=0, keepdims=True), i == 0)

    def row(width):
        return pl.BlockSpec((tm, width), lambda i: (i, 0))

    def whole(shape):
        return pl.BlockSpec(shape, lambda i: (0, 0))

    return pl.pallas_call(
        body, name="mix_bwd_gate", grid=(nt,),
        in_specs=[row(D), whole(wo.shape), row(D), row(D), row(2 * D), whole(bias.shape)],
        out_specs=[row(2 * D), row(D), row(D), whole((D, D)), whole((1, 2 * D))],
        out_shape=[jax.ShapeDtypeStruct((S, 2 * D), BF16), jax.ShapeDtypeStruct((S, D), BF16),
                   jax.ShapeDtypeStruct((S, D), BF16), jax.ShapeDtypeStruct((D, D), F32),
                   jax.ShapeDtypeStruct((1, 2 * D), F32)],
        compiler_params=_params(dimension_semantics=("arbitrary",)),
    )(dh, wo, bp, ba, gl, bias)


def _mix_bwd_branch(dbp, dba, attn, xp, pw, pscale, wpot, wao, tm=1024):
    S, D = dbp.shape
    nt = S // tm
    PW = xp.shape[1]
    NG = len(POOL_WINDOWS)

    def body(dbp_ref, dba_ref, at_ref, xc_ref, xprev_ref, pw_ref, ps_ref, wpo_ref, wao_ref,
             dat_ref, dpl_ref, dwao_ref, dwpo_ref, dpw_ref, dps_ref):
        i = pl.program_id(0)
        dba_v = dba_ref[...]
        dbp_v = dbp_ref[...]
        _acc(dwao_ref, _dot_tn(at_ref[...], dba_v), i == 0)
        dat_ref[...] = _dot_nt(dba_v, wao_ref[...]).astype(BF16)
        pooled = _pooled(xc_ref[...], xprev_ref[...], i).astype(BF16)
        mixed = _pool_mix(pooled, pw_ref)
        ps = ps_ref[...]
        _acc(dwpo_ref, _dot_tn(dbp_v, (mixed * ps).astype(BF16)), i == 0)
        dms = _dot(dbp_v, wpo_ref[...])
        _acc(dps_ref, jnp.sum(dms * mixed, axis=0, keepdims=True), i == 0)
        dmixed = (dms * ps).astype(BF16)
        dpooled = []
        for gi in range(NG):
            cols = slice(gi * POOL_GROUP, (gi + 1) * POOL_GROUP)
            _acc(dpw_ref.at[gi], _dot_tn(pooled[:, cols], dmixed[:, cols]), i == 0)
            dpooled.append(_dot_nt(dmixed[:, cols], pw_ref[gi]))
        dpl_ref[...] = jnp.concatenate(dpooled, axis=1)

    def row(width):
        return pl.BlockSpec((tm, width), lambda i: (i, 0))

    def whole(shape):
        nd = len(shape)
        return pl.BlockSpec(shape, lambda i: (0,) * nd)

    return pl.pallas_call(
        body, name="mix_bwd_branch", grid=(nt,),
        in_specs=[row(D), row(D), row(D), row(PW), _halo_specs(tm, PW, S, False), whole(pw.shape), whole(pscale.shape),
                  whole(wpot.shape), whole(wao.shape)],
        out_specs=[row(D), row(PW), whole((D, D)), whole((D, PW)), whole(pw.shape), whole((1, PW))],
        out_shape=[jax.ShapeDtypeStruct((S, D), BF16), jax.ShapeDtypeStruct((S, PW), F32),
                   jax.ShapeDtypeStruct((D, D), F32), jax.ShapeDtypeStruct((D, PW), F32),
                   jax.ShapeDtypeStruct(pw.shape, F32), jax.ShapeDtypeStruct((1, PW), F32)],
        compiler_params=_params(dimension_semantics=("arbitrary",)),
    )(dbp, dba, attn, xp, xp, pw, pscale, wpot, wao)


def _pool_bwd(dpooled, tm=512):
    S, PW = dpooled.shape
    nt = S // tm

    def body(dc_ref, dnext_ref, dxp_ref):
        i = pl.program_id(0)
        dc = dc_ref[...]
        dh = jnp.concatenate([dc, jnp.where(i < nt - 1, dnext_ref[...], 0.0)], axis=0)
        rows = tm + POOL_HALO
        t = lax.broadcasted_iota(jnp.int32, (rows, 1), 0) + i * tm
        out = []
        for gi, w in enumerate(POOL_WINDOWS):
            cols = slice(gi * POOL_GROUP, (gi + 1) * POOL_GROUP)
            acc = dh[:, cols] / jnp.minimum(t + 1, w).astype(F32)
            sh = 1
            while sh < w:
                acc = acc + pltpu.roll(acc, rows - sh, 0)
                sh *= 2
            out.append(acc[:tm, :] - dc[:, cols])
        dxp_ref[...] = jnp.concatenate(out, axis=1).astype(BF16)

    return pl.pallas_call(
        body, name="pool_bwd", grid=(nt,),
        in_specs=[pl.BlockSpec((tm, PW), lambda i: (i, 0)), _halo_specs(tm, PW, S, True)],
        out_specs=pl.BlockSpec((tm, PW), lambda i: (i, 0)), out_shape=jax.ShapeDtypeStruct((S, PW), BF16),
        compiler_params=_params(dimension_semantics=("arbitrary",)),
    )(dpooled, dpooled)


def _in_bwd_w(pieces, u, tm=1024):
    S, D = u.shape
    nt = S // tm
    NW = sum(width for _, width in SEGMENTS)

    def body(*refs):
        piece_refs, u_ref, dw_hbm, acc, sem = refs[:len(SEGMENTS)], refs[len(SEGMENTS)], refs[len(SEGMENTS) + 1], refs[-2], refs[-1]
        i = pl.program_id(0)
        u_t = u_ref[...]
        for (off, width), ref in zip(SEGMENTS, piece_refs):
            for lo in range(0, width, D):
                hi = min(lo + D, width)
                _acc(acc.at[off + lo:off + hi, :], _dot_tn(ref[:, lo:hi].astype(BF16), u_t), i == 0)

        @pl.when(i == nt - 1)
        def _():
            cp = pltpu.make_async_copy(acc, dw_hbm, sem)
            cp.start()
            cp.wait()

    return pl.pallas_call(
        body, name="in_bwd_w", grid=(nt,),
        in_specs=[pl.BlockSpec((tm, width), lambda i: (i, 0)) for _, width in SEGMENTS] + [pl.BlockSpec((tm, D), lambda i: (i, 0))],
        out_specs=pl.BlockSpec(memory_space=pl.ANY), out_shape=jax.ShapeDtypeStruct((NW, D), F32),
        scratch_shapes=[pltpu.VMEM((NW, D), F32), pltpu.SemaphoreType.DMA],
        compiler_params=_params(dimension_semantics=("arbitrary",)),
    )(*pieces, u)


def _in_bwd_x(pieces, wint, h, dh, gnorm, tm=512):
    S, D = h.shape
    nt = S // tm

    def body(*refs):
        piece_refs = refs[:len(SEGMENTS)]
        w_ref, h_ref, dh_ref, g_ref, dx_ref, dxh_ref, dg_ref = refs[len(SEGMENTS):]
        i = pl.program_id(0)
        du = jnp.zeros((tm, D), F32)
        for (off, width), ref in zip(SEGMENTS, piece_refs):
            du = du + _dot(ref[...].astype(BF16), w_ref[off:off + width, :])
        dx, dg_rows = _rms_bwd(du, h_ref[...], g_ref[...])
        out = dh_ref[...] + dx
        dx_ref[...] = out
        dxh_ref[...] = (0.5 * out).astype(BF16)
        _acc(dg_ref, jnp.sum(dg_rows, axis=0, keepdims=True), i == 0)

    def row(width):
        return pl.BlockSpec((tm, width), lambda i: (i, 0))

    vec = pl.BlockSpec((1, D), lambda i: (0, 0))
    return pl.pallas_call(
        body, name="in_bwd_x", grid=(nt,),
        in_specs=[row(width) for _, width in SEGMENTS] + [pl.BlockSpec(wint.shape, lambda i: (0, 0)), row(D), row(D), vec],
        out_specs=[row(D), row(D), vec],
        out_shape=[jax.ShapeDtypeStruct((S, D), F32), jax.ShapeDtypeStruct((S, D), BF16), jax.ShapeDtypeStruct((1, D), F32)],
        compiler_params=_params(dimension_semantics=("arbitrary",)),
    )(*pieces, wint, h, dh, gnorm)


def _row_tile(rows):
    for t in (512, 480, 352, 256, 128, 64, 32, 16, 8):
        if rows % t == 0:
            return t
    return rows


def _adamw(w, g, m, v, token, name):
    R, C = w.shape
    tr = _row_tile(R)

    def body(w_ref, g_ref, m_ref, v_ref, _, d_ref, mo_ref, vo_ref, go_ref):
        gv = g_ref[...]
        go_ref[...] = gv
        mn = ADAM_B1 * m_ref[...] + (1.0 - ADAM_B1) * gv
        vn = ADAM_B2 * v_ref[...] + (1.0 - ADAM_B2) * (gv * gv)
        m_hat = mn / (1.0 - ADAM_B1 ** ADAM_STEP)
        v_hat = vn / (1.0 - ADAM_B2 ** ADAM_STEP)
        d_ref[...] = -ADAM_LR * (m_hat / (jnp.sqrt(v_hat) + ADAM_EPS) + ADAM_WD * w_ref[...])
        mo_ref[...] = mn
        vo_ref[...] = vn

    blk = pl.BlockSpec((tr, C), lambda i: (i, 0))
    sh = jax.ShapeDtypeStruct((R, C), F32)
    return pl.pallas_call(
        body, name=name, grid=(R // tr,), in_specs=[blk] * 4 + [TOKEN_SPEC], out_specs=[blk] * 4, out_shape=[sh] * 4,
        compiler_params=_params(dimension_semantics=("arbitrary",)),
    )(w, g, m, v, token)


def _cast_place(w, place, name):
    R, C = w.shape
    tr = _row_tile(R)
    per = R // tr

    def body(p_ref, w_ref, o_ref):
        o_ref[...] = w_ref[...].astype(BF16)

    grid_spec = pltpu.PrefetchScalarGridSpec(
        num_scalar_prefetch=1, grid=(per,),
        in_specs=[pl.BlockSpec((tr, C), lambda i, p: (i, 0))],
        out_specs=pl.BlockSpec((tr, C), lambda i, p: (p[0] * per + i, 0)))
    return pl.pallas_call(
        body, name=name, grid_spec=grid_spec, out_shape=jax.ShapeDtypeStruct((N_SHARDS * R, C), BF16),
        compiler_params=_params(dimension_semantics=("arbitrary",)),
    )(place, w)


def _parked(w, m, idx, first, last):
    return jnp.where(m == w, idx, jnp.where(m < w, first, last))


def _sum_halves(g4s, recvs, place, name):
    M = len(g4s)
    NS, R, C = g4s[0].shape
    hr = R // 2
    tr = _row_tile(hr)
    per = hr // tr

    def body(p_ref, *refs):
        m = pl.program_id(0)
        for w in range(M):
            @pl.when(m == w)
            def _(w=w):
                refs[2 * M + w][...] = (refs[w][...] + refs[M + w][...]).astype(BF16)

    def spec(w, mine):
        def index(m, s, i, p):
            row = _parked(w, m, i, 0, per - 1)
            return _parked(w, m, s, 0, NS - 1), (p[1] * per + row if mine else row), 0

        return pl.BlockSpec((1, tr, C), index)

    grid_spec = pltpu.PrefetchScalarGridSpec(
        num_scalar_prefetch=1, grid=(M, NS, per),
        in_specs=[spec(w, True) for w in range(M)] + [spec(w, False) for w in range(M)],
        out_specs=[spec(w, False) for w in range(M)])
    return pl.pallas_call(
        body, name=name, grid_spec=grid_spec, out_shape=[jax.ShapeDtypeStruct((NS, hr, C), BF16)] * M,
        compiler_params=_params(dimension_semantics=("arbitrary",) * 3),
    )(place, *g4s, *recvs)


def _sum_quarters(h4s, recv3s, place, name):
    M = len(h4s)
    NS, hr, C = h4s[0].shape
    tr = _row_tile(hr)
    per = hr // tr

    def body(p_ref, *refs):
        m = pl.program_id(0)
        for w in range(M):
            @pl.when(m == w)
            def _(w=w):
                acc = refs[w][0].astype(F32)
                for k in range(N_SHARDS - 1):
                    acc = acc + refs[M + w][k].astype(F32)
                refs[2 * M + w][...] = acc

    def row(w, m, i):
        return _parked(w, m, i, 0, per - 1)

    grid_spec = pltpu.PrefetchScalarGridSpec(
        num_scalar_prefetch=1, grid=(M, per),
        in_specs=[pl.BlockSpec((1, tr, C), lambda m, i, p, w=w: (p[0], row(w, m, i), 0)) for w in range(M)]
        + [pl.BlockSpec((N_SHARDS - 1, tr, C), lambda m, i, p, w=w: (0, row(w, m, i), 0)) for w in range(M)],
        out_specs=[pl.BlockSpec((tr, C), lambda m, i, p, w=w: (p[1] * per + row(w, m, i), 0)) for w in range(M)])
    return pl.pallas_call(
        body, name=name, grid_spec=grid_spec, out_shape=[jax.ShapeDtypeStruct((2 * hr, C), F32)] * M,
        compiler_params=_params(dimension_semantics=("arbitrary",) * 2),
    )(place, *h4s, *recv3s)


def _by_shape(arrays):
    groups = {}
    for k, a in enumerate(arrays):
        groups.setdefault(a.shape, []).append(k)
    return list(groups.values())


def _place():
    x, y, c = lax.axis_index("x"), lax.axis_index("y"), lax.axis_index("c")
    chips = [(1 - x, y), (x, 1 - y), (1 - x, 1 - y)]
    return x, y, c, chips


HBM_SPEC = pl.BlockSpec(memory_space=pltpu.HBM)
SEM_SPEC = pl.BlockSpec(memory_space=pltpu.SEMAPHORE)
DATAFLOW = pltpu.SideEffectType.DATAFLOW_SIDE_EFFECTING


def _hbm(a):
    return pltpu.with_memory_space_constraint(a, pltpu.HBM)


class InFlight(NamedTuple):
    send_sem: jax.Array
    recv_sem: jax.Array
    bufs: list
    plan: Callable
    token: jax.Array


def _wait_all(plan, refs, send_ref, recv_ref):
    for k, (src, dst, dev) in enumerate(plan(refs)):
        cp = pltpu.make_async_remote_copy(src_ref=src, dst_ref=dst, send_sem=send_ref.at[k], recv_sem=recv_ref.at[k],
                                          device_id=dev, device_id_type=MESH)
        cp.wait_send()
        cp.wait_recv()


def _start_all(plan, refs, send_ref, recv_ref):
    for k, (src, dst, dev) in enumerate(plan(refs)):
        pltpu.make_async_remote_copy(src_ref=src, dst_ref=dst, send_sem=send_ref.at[k], recv_sem=recv_ref.at[k],
                                     device_id=dev, device_id_type=MESH).start()


def _split_start(name, bufs, plan, n_copies, after, waits=None):
    nb = len(bufs)
    after = list(after) if isinstance(after, (list, tuple)) else [after]

    def body(*refs):
        ins = refs[:nb]
        if waits is not None:
            _wait_all(waits.plan, ins, refs[nb], refs[nb + 1])
        send_sem, recv_sem, token = refs[-nb - 3], refs[-nb - 2], refs[-1]
        _start_all(plan, ins, send_sem, recv_sem)
        token[...] = jnp.zeros_like(token)

    sems = pltpu.SemaphoreType.DMA((n_copies,))
    earlier = [] if waits is None else [waits.send_sem, waits.recv_sem]
    out = pl.pallas_call(
        body, name=name, in_specs=[HBM_SPEC] * nb + [SEM_SPEC] * len(earlier) + [TOKEN_SPEC] * len(after),
        out_shape=(sems, sems, *[pltpu.HBM(b.shape, b.dtype) for b in bufs], jax.ShapeDtypeStruct((8, 128), F32)),
        out_specs=(SEM_SPEC, SEM_SPEC, *[HBM_SPEC] * nb, pl.BlockSpec(memory_space=pltpu.VMEM)),
        input_output_aliases={i: i + 2 for i in range(nb)},
        compiler_params=pltpu.CompilerParams(has_side_effects=DATAFLOW),
    )(*[_hbm(b) for b in bufs], *earlier, *after)
    return InFlight(out[0], out[1], list(out[2:2 + nb]), plan, out[-1])


def _split_wait(name, flight, after):
    nb = len(flight.bufs)
    after = list(after) if isinstance(after, (list, tuple)) else [after]

    def body(*refs):
        _wait_all(flight.plan, refs[:nb], refs[nb], refs[nb + 1])

    out = pl.pallas_call(
        body, name=name, in_specs=[HBM_SPEC] * nb + [SEM_SPEC, SEM_SPEC] + [TOKEN_SPEC] * len(after),
        out_shape=[pltpu.HBM(b.shape, b.dtype) for b in flight.bufs], out_specs=[HBM_SPEC] * nb,
        input_output_aliases={i: i for i in range(nb)},
        compiler_params=pltpu.CompilerParams(has_side_effects=DATAFLOW),
    )(*flight.bufs, flight.send_sem, flight.recv_sem, *after)
    return list(out)


def _half_rows(buf, chip, core):
    hr = buf.shape[0] // (2 * N_SHARDS)
    return buf.at[pl.ds(pl.multiple_of((2 * chip + core) * hr, 16), hr)]


def _gather_ici_plan(bufs):
    x, y, c, chips = _place()
    return [(_half_rows(b, 2 * x + y, c), _half_rows(b, 2 * x + y, c), (px, py, c)) for b in bufs for px, py in chips]


def _gather_d2d_plan(bufs):
    x, y, c, chips = _place()
    return [(_half_rows(b, 2 * px + py, c), _half_rows(b, 2 * px + py, c), (x, y, 1 - c)) for b in bufs for px, py in chips]


def _swap_plan(bufs):
    x, y, c, _ = _place()
    n = len(bufs) // 2
    copies = []
    for g, land in zip(bufs[:n], bufs[n:]):
        hr = g.shape[1] // 2
        copies.append((g.at[:, pl.ds(pl.multiple_of((1 - c) * hr, 8), hr)], land, (x, y, 1 - c)))
    return copies


def _exchange_plan(bufs):
    x, y, c, chips = _place()
    n = len(bufs) // 2
    return [(h.at[2 * px + py], land.at[k], (px, py, c))
            for h, land in zip(bufs[:n], bufs[n:]) for k, (px, py) in enumerate(chips)]


def _share_plan(bufs):
    x, y, c, _ = _place()
    copies = []
    for buf in bufs:
        hr = buf.shape[0] // 2
        mine = buf.at[pl.ds(pl.multiple_of(c * hr, 8), hr)]
        copies.append((mine, mine, (x, y, 1 - c)))
    return copies


N_DEVICES = 8


def _slot_place(vec, place):
    R, C = vec.shape

    def body(p_ref, v_ref, o_ref):
        o_ref[0] = v_ref[...]

    grid_spec = pltpu.PrefetchScalarGridSpec(
        num_scalar_prefetch=1, grid=(1,), in_specs=[pl.BlockSpec((R, C), lambda i, p: (0, 0))],
        out_specs=pl.BlockSpec((1, R, C), lambda i, p: (2 * p[0] + p[1], 0, 0)))
    return pl.pallas_call(
        body, name="slot_place", grid_spec=grid_spec, out_shape=jax.ShapeDtypeStruct((N_DEVICES, R, C), F32),
        compiler_params=_params(dimension_semantics=("arbitrary",)),
    )(place, vec)


def _slots_plan(bufs):
    x, y, c, _ = _place()
    mine = bufs[0].at[4 * x + 2 * y + c]
    return [(mine, mine, (x ^ (r >> 2), y ^ ((r >> 1) & 1), c ^ (r & 1))) for r in range(1, N_DEVICES)]


def _sum_slots(slots):
    _, R, C = slots.shape

    def body(s_ref, o_ref):
        acc = s_ref[0]
        for d in range(1, N_DEVICES):
            acc = acc + s_ref[d]
        o_ref[...] = acc

    return pl.pallas_call(
        body, name="sum_slots", in_specs=[pl.BlockSpec(memory_space=pltpu.VMEM)],
        out_specs=pl.BlockSpec(memory_space=pltpu.VMEM), out_shape=jax.ShapeDtypeStruct((R, C), F32),
        compiler_params=_params(),
    )(slots)


SMALL = ("ffn1_norm", "mix_norm", "pool_w", "pool_scale", "q_norm", "k_norm", "sinks", "gate_bias", "ffn2_norm")
SMALL_COLS = 1024
FFN1 = ("ffn1_w_gate", "ffn1_w_up", "ffn1_w_down")
MIXER = ("w_in", "w_pool_out", "w_attn_out", "w_out")
FFN2 = ("ffn2_w_gate", "ffn2_w_up", "ffn2_w_down")
LARGE = FFN1 + MIXER + FFN2
TRANSPOSED = ("ffn1_w_gate", "ffn1_w_up", "w_in", "w_pool_out", "ffn2_w_gate", "ffn2_w_up")
WEIGHTS = ("ffn1_norm", "ffn1_w_gate", "ffn1_w_up", "ffn1_w_down", "mix_norm", "w_in", "pool_w", "pool_scale",
           "w_pool_out", "q_norm", "k_norm", "sinks", "w_attn_out", "gate_bias", "w_out", "ffn2_norm",
           "ffn2_w_gate", "ffn2_w_up", "ffn2_w_down")


def _pack_small(parts):
    flat = jnp.concatenate([p.reshape(-1) for p in parts])
    rows = -(-flat.shape[0] // (8 * SMALL_COLS)) * 8
    return jnp.pad(flat, (0, rows * SMALL_COLS - flat.shape[0])).reshape(rows, SMALL_COLS)


def _unpack_small(packed, like):
    flat = packed.reshape(-1)
    out, off = [], 0
    for p in like:
        out.append(flat[off:off + p.size].reshape(p.shape))
        off += p.size
    return out, flat[off]


def _tie(small, token):
    return small + token[0, 0]


class Reduction:
    def __init__(self, group, names, grads, place):
        self.group, self.names, self.place = group, names, place
        self.bufs = [g.reshape(N_SHARDS, -1, g.shape[-1]) for g in grads]
        self.flight = None

    def _start(self, stage, bufs, plan, n_copies, after):
        self.flight = _split_start(f"{self.group}_{stage}", bufs, plan, n_copies, after)
        return self.flight.token

    def _landed(self, stage, after):
        n = len(self.names)
        bufs = _split_wait(f"{self.group}_{stage}_wait", self.flight, after)
        return bufs[:n], bufs[n:]

    def start_swap(self, after):
        lands = [lax.empty((g.shape[0], g.shape[1] // 2, g.shape[2]), g.dtype) for g in self.bufs]
        return self._start("swap", self.bufs + lands, _swap_plan, len(self.bufs), after)

    def start_exchange(self, after):
        g4, recv = self._landed("swap", after)
        halves = self._per_shape(_sum_halves, "sum_halves", g4, recv)
        lands = [lax.empty((N_SHARDS - 1,) + h.shape[1:], h.dtype) for h in halves]
        return self._start("exchange", halves + lands, _exchange_plan, (N_SHARDS - 1) * len(halves), self.place)

    def start_share(self, after):
        halves, recv3 = self._landed("exchange", after)
        reduced = self._per_shape(_sum_quarters, "sum_quarters", halves, recv3)
        return self._start("share", reduced, _share_plan, len(reduced), self.place)

    def _per_shape(self, add, stage, mine, received):
        out = [None] * len(mine)
        for idx in _by_shape(mine):
            sums = add([mine[k] for k in idx], [received[k] for k in idx], self.place, f"{stage}_{self.names[idx[0]]}")
            for k, v in zip(idx, sums):
                out[k] = v
        return out

    def finish(self, after):
        return dict(zip(self.names, _split_wait(f"{self.group}_share_wait", self.flight, after)))


def _row_form(name, a):
    return a.T if name in TRANSPOSED else a


def kernel(x, ffn1_norm, ffn1_w_gate, ffn1_w_up, ffn1_w_down, mix_norm, w_in, pool_w, pool_scale, w_pool_out, q_norm, k_norm, sinks, w_attn_out, gate_bias, w_out, ffn2_norm, ffn2_w_gate, ffn2_w_up, ffn2_w_down, loss_target, m_ffn1_norm, m_ffn1_w_gate, m_ffn1_w_up, m_ffn1_w_down, m_mix_norm, m_w_in, m_pool_w, m_pool_scale, m_w_pool_out, m_q_norm, m_k_norm, m_sinks, m_w_attn_out, m_gate_bias, m_w_out, m_ffn2_norm, m_ffn2_w_gate, m_ffn2_w_up, m_ffn2_w_down, v_ffn1_norm, v_ffn1_w_gate, v_ffn1_w_up, v_ffn1_w_down, v_mix_norm, v_w_in, v_pool_w, v_pool_scale, v_w_pool_out, v_q_norm, v_k_norm, v_sinks, v_w_attn_out, v_gate_bias, v_w_out, v_ffn2_norm, v_ffn2_w_gate, v_ffn2_w_up, v_ffn2_w_down):
    args = dict(locals())
    wts = {n: _row_form(n, args[n]) for n in WEIGHTS}
    mom = {n: _row_form(n, args["m_" + n]) for n in WEIGHTS}
    var = {n: _row_form(n, args["v_" + n]) for n in WEIGHTS}
    shard = 2 * lax.axis_index("x") + lax.axis_index("y")
    place = jnp.stack([shard, lax.axis_index("c")]).astype(jnp.int32)

    xs, target = x[0], loss_target[0]
    D = xs.shape[1]
    g1 = wts["ffn1_norm"].reshape(1, D)
    gm = wts["mix_norm"].reshape(1, D)
    g2 = wts["ffn2_norm"].reshape(1, D)
    qw = wts["q_norm"].reshape(1, HEAD_DIM)
    kw = wts["k_norm"].reshape(1, HEAD_DIM)
    bias = wts["gate_bias"].reshape(1, 2 * D)
    pscale = wts["pool_scale"].reshape(1, -1)
    pw = wts["pool_w"].astype(BF16)
    sink_rows = jnp.broadcast_to(jnp.repeat(wts["sinks"], ATTN_BLOCK).reshape(N_KV_HEADS, GQA_GROUP * ATTN_BLOCK, 1),
                                 (N_KV_HEADS, GQA_GROUP * ATTN_BLOCK, 128))

    gate1, up1, down1 = FFN1[:1], FFN1[1:2], FFN1[2:]
    placed = {n: _cast_place(wts[n], place, "cast_" + n) for n in gate1}
    gate1_ici = _split_start("gather_gate1_ici", [placed[n] for n in gate1], _gather_ici_plan, 3 * len(gate1), place)
    placed.update({n: _cast_place(wts[n], place, "cast_" + n) for n in up1 + down1 + MIXER + FFN2})
    zero = jnp.zeros((1,), F32)
    packed = [_pack_small([d[n] for n in SMALL] + [zero]) for d in (wts, mom, var)]
    band = _band_bias()
    shadow = [placed[n] for n in up1 + down1 + MIXER + FFN2] + packed + [sink_rows, pw, band]
    up1_ici = _split_start("gather_up1_ici", [placed[n] for n in up1], _gather_ici_plan, 3 * len(up1), gate1_ici.token)
    down1_ici = _split_start("gather_down1_ici", [placed[n] for n in down1], _gather_ici_plan, 3 * len(down1), up1_ici.token)
    gate1_d2d = _split_start("gather_gate1_d2d", gate1_ici.bufs, _gather_d2d_plan, 3 * len(gate1), shadow + [down1_ici.token],
                             waits=gate1_ici)
    w = dict(zip(gate1, _split_wait("gather_gate1_wait", gate1_d2d, gate1_d2d.token)))
    mix_ici = _split_start("gather_mix_ici", [placed[n] for n in MIXER], _gather_ici_plan, 3 * len(MIXER), w[gate1[0]])
    ffn2_ici = _split_start("gather_ffn2_ici", [placed[n] for n in FFN2], _gather_ici_plan, 3 * len(FFN2), mix_ici.token)

    n1, a1 = _ffn_gate(xs, _tie(g1, ffn2_ici.token), w["ffn1_w_gate"], "ffn1_gate")
    up1_d2d = _split_start("gather_up1_d2d", up1_ici.bufs, _gather_d2d_plan, 3 * len(up1), a1, waits=up1_ici)
    w.update(zip(up1, _split_wait("gather_up1_wait", up1_d2d, up1_d2d.token)))
    b1, s1 = _ffn_hidden(n1, a1, w["ffn1_w_up"], "ffn1_hidden")
    down1_d2d = _split_start("gather_down1_d2d", down1_ici.bufs, _gather_d2d_plan, 3 * len(down1), s1, waits=down1_ici)
    w.update(zip(down1, _split_wait("gather_down1_wait", down1_d2d, down1_d2d.token)))
    h1 = _ffn_down(xs, s1, w["ffn1_w_down"], "ffn1_down")
    mix_d2d = _split_start("gather_mix_d2d", mix_ici.bufs, _gather_d2d_plan, 3 * len(MIXER), h1, waits=mix_ici)
    w.update(zip(MIXER, _split_wait("gather_mix_wait", mix_d2d, mix_d2d.token)))
    u, xp, q, kv, gl = _mix_proj(h1, gm, w["w_in"])
    attn = _attn_fwd(q, kv, qw, kw, sink_rows, band)
    ffn2_d2d = _split_start("gather_ffn2_d2d", ffn2_ici.bufs, _gather_d2d_plan, 3 * len(FFN2), attn, waits=ffn2_ici)
    h2, bp, ba = _mix_out(xp, attn, gl, _tie(bias, ffn2_d2d.token), pw, pscale, w["w_pool_out"], w["w_attn_out"], w["w_out"], h1)
    w.update(zip(FFN2, _split_wait("gather_ffn2_wait", ffn2_d2d, h2)))
    n2, a2, b2, s2 = _ffn_up(h2, g2, w["ffn2_w_gate"], w["ffn2_w_up"], "ffn2_up")
    dy, dyh, loss = _ffn_down_loss(h2, s2, w["ffn2_w_down"], target, "ffn2_down_loss")

    gw, gs = {}, {}
    da, db = _ffn_bwd_h(dyh, a2, b2, w["ffn2_w_down"], ffn2_d2d.token, "ffn2_bwd_h")
    gw["ffn2_w_gate"], gw["ffn2_w_up"] = _xty(da, n2, "ffn2_dw_gate"), _xty(db, n2, "ffn2_dw_up")
    gw["ffn2_w_down"] = _xty(s2, dyh, "ffn2_dw_down")
    red2 = Reduction("reduce_ffn2", FFN2, [gw[n] for n in FFN2], place)
    token = red2.start_swap(da)
    dh2, gs["ffn2_norm"] = _ffn_bwd_x(da, db, w["ffn2_w_gate"], w["ffn2_w_up"], h2, dy, _tie(g2, token), "ffn2_bwd_x")
    token = red2.start_exchange(dh2)
    dgl, dbp, dba, gw["w_out"], gs["gate_bias"] = _mix_bwd_gate(dh2, w["w_out"], bp, ba, gl, _tie(bias, token))
    dattn, dpooled, gw["w_attn_out"], gw["w_pool_out"], gs["pool_w"], gs["pool_scale"] = _mix_bwd_branch(
        dbp, dba, attn, xp, pw, pscale, w["w_pool_out"], w["w_attn_out"])
    dq, dkv, gs["q_norm"], gs["k_norm"], dsk = _attn_bwd(q, kv, dattn, qw, kw, sink_rows, band)
    gs["sinks"] = dsk[:, :, 0]
    red2.start_share(dq)
    pieces = (_pool_bwd(dpooled), dq, dkv, dgl)
    gw["w_in"] = _in_bwd_w(pieces, u)
    grads = red2.finish(gw["w_in"])
    redm = Reduction("reduce_mix", MIXER, [gw[n] for n in MIXER], place)
    token = redm.start_swap(grads[FFN2[0]])
    dh1, dh1h, gs["mix_norm"] = _in_bwd_x(pieces, w["w_in"], h1, dh2, _tie(gm, token))
    token = redm.start_exchange(dh1)
    da, db = _ffn_bwd_h(dh1h, a1, b1, w["ffn1_w_down"], token, "ffn1_bwd_h")
    gw["ffn1_w_gate"], gw["ffn1_w_up"] = _xty(da, n1, "ffn1_dw_gate"), _xty(db, n1, "ffn1_dw_up")
    gw["ffn1_w_down"] = _xty(s1, dh1h, "ffn1_dw_down")
    token = redm.start_share([gw[n] for n in FFN1])
    red1 = Reduction("reduce_ffn1", FFN1, [gw[n] for n in FFN1], place)
    token = red1.start_swap(token)
    grad_x, gs["ffn1_norm"] = _ffn_bwd_x(da, db, w["ffn1_w_gate"], w["ffn1_w_up"], xs, dh1, _tie(g1, token), "ffn1_bwd_x")
    grads.update(redm.finish(grad_x))

    small_parts = [gs[n] for n in SMALL] + [loss[0, 0].reshape(1)]
    slots = _split_start("gather_small", [_slot_place(_pack_small(small_parts), place)], _slots_plan, N_DEVICES - 1,
                         grads[MIXER[0]])
    token = red1.start_exchange(slots.token)
    delta, new_m, new_v = {}, {}, {}
    for n in FFN2 + MIXER:
        delta[n], new_m[n], new_v[n], grads[n] = _adamw(wts[n], grads[n], mom[n], var[n], token, "adamw_" + n)
    summed = _sum_slots(_split_wait("gather_small_wait", slots, delta[MIXER[-1]])[0])
    small_grads, loss_sum = _unpack_small(summed, [wts[n] for n in SMALL])
    grads.update(dict(zip(SMALL, small_grads)))
    ds, ms, vs, _ = _adamw(packed[0], summed, packed[1], packed[2], token, "adamw_small")
    like = [wts[n] for n in SMALL]
    for out, packed_out in ((delta, ds), (new_m, ms), (new_v, vs)):
        out.update(dict(zip(SMALL, _unpack_small(packed_out, like)[0])))
    results = {(k, n): _row_form(n, d[n]) for k, d in enumerate((grads, delta, new_m, new_v)) for n in SMALL + FFN2 + MIXER}
    token = red1.start_share([results[k, n] for k in range(4) for n in SMALL + ("w_pool_out",)])
    grads.update(red1.finish(token))
    for n in FFN1:
        delta[n], new_m[n], new_v[n], grads[n] = _adamw(wts[n], grads[n], mom[n], var[n], token, "adamw_" + n)
    results.update({(k, n): _row_form(n, d[n]) for k, d in enumerate((grads, delta, new_m, new_v)) for n in FFN1})
    return (loss_sum, grad_x[None], *[results[k, n] for k in range(4) for n in WEIGHTS])
```

```python
from typing import Callable, NamedTuple

import jax
import jax.numpy as jnp
from jax import lax
from jax.experimental import pallas as pl
from jax.experimental.pallas import tpu as pltpu

F32 = jnp.float32
BF16 = jnp.bfloat16
RMS_EPS = 1e-6
POOL_WINDOWS = (2, 4, 8, 16)
POOL_GROUP = 128
POOL_HALO = 16
HEAD_DIM = 64
GQA_GROUP = 8
N_KV_HEADS = 2
ATTN_BLOCK = 128
SCALE = HEAD_DIM ** -0.5
NEG = -1e30
N_SHARDS = 4
ADAM_LR, ADAM_B1, ADAM_B2, ADAM_EPS, ADAM_WD, ADAM_STEP = 0.001, 0.9, 0.999, 1e-08, 0.01, 10
VMEM_LIMIT = 56 * 1024 * 1024
MESH = pl.DeviceIdType.MESH
SEG_POOL, SEG_Q, SEG_KV, SEG_GATE = (0, 512), (512, 1024), (1536, 256), (1792, 2048)
SEGMENTS = (SEG_POOL, SEG_Q, SEG_KV, SEG_GATE)


def _params(**kw):
    return pltpu.CompilerParams(vmem_limit_bytes=VMEM_LIMIT, **kw)


def _dot(a, b):
    return jnp.dot(a, b, preferred_element_type=F32)


def _dot_nt(a, b):
    return lax.dot_general(a, b, (((1,), (1,)), ((), ())), preferred_element_type=F32)


def _dot_tn(a, b):
    return lax.dot_general(a, b, (((0,), (0,)), ((), ())), preferred_element_type=F32)


def _rinv(x):
    return lax.rsqrt(jnp.mean(x * x, axis=-1, keepdims=True) + RMS_EPS)


def _rms_bwd(dn, x, g):
    r = _rinv(x)
    xr = x * r
    z = dn * g
    dx = r * (z - xr * jnp.mean(z * xr, axis=-1, keepdims=True))
    return dx, dn * xr


def _acc(ref, val, first):
    @pl.when(first)
    def _():
        ref[...] = val

    @pl.when(jnp.logical_not(first))
    def _():
        ref[...] += val


TOKEN_SPEC = pl.BlockSpec(memory_space=pl.ANY)
F_HALF = 1408


def _resident(w):
    return pl.BlockSpec(w.shape, lambda i: (0, 0), pipeline_mode=pl.Buffered(1))


def _ffn_up(h, gnorm, wgt, wut, name, tm=512):
    S, D = h.shape
    F = wgt.shape[0]

    def body(h_ref, g_ref, wg_ref, wu_ref, n_ref, a_ref, b_ref, s_ref):
        x = h_ref[...]
        n = (x * _rinv(x) * g_ref[...]).astype(BF16)
        n_ref[...] = n
        for lo in range(0, F, F_HALF):
            cols = slice(lo, lo + F_HALF)
            a = _dot_nt(n, wg_ref[cols, :])
            b = _dot_nt(n, wu_ref[cols, :])
            a_ref[:, cols] = a.astype(BF16)
            b_ref[:, cols] = b.astype(BF16)
            s_ref[:, cols] = (a * jax.nn.sigmoid(a) * b).astype(BF16)

    act = pl.BlockSpec((tm, F), lambda i: (i, 0))
    hidden = jax.ShapeDtypeStruct((S, F), BF16)
    return pl.pallas_call(
        body, name=name, grid=(S // tm,),
        in_specs=[pl.BlockSpec((tm, D), lambda i: (i, 0)), pl.BlockSpec((1, D), lambda i: (0, 0)), _resident(wgt), _resident(wut)],
        out_specs=[pl.BlockSpec((tm, D), lambda i: (i, 0)), act, act, act],
        out_shape=[jax.ShapeDtypeStruct((S, D), BF16), hidden, hidden, hidden],
        compiler_params=_params(dimension_semantics=("arbitrary",)),
    )(h, gnorm, wgt, wut)


def _ffn_gate(h, gnorm, wgt, name, tm=512):
    S, D = h.shape
    F = wgt.shape[0]

    def body(h_ref, g_ref, wg_ref, n_ref, a_ref):
        x = h_ref[...]
        n = (x * _rinv(x) * g_ref[...]).astype(BF16)
        n_ref[...] = n
        for lo in range(0, F, F_HALF):
            a_ref[:, lo:lo + F_HALF] = _dot_nt(n, wg_ref[lo:lo + F_HALF, :]).astype(BF16)

    row = pl.BlockSpec((tm, D), lambda i: (i, 0))
    return pl.pallas_call(
        body, name=name, grid=(S // tm,), in_specs=[row, pl.BlockSpec((1, D), lambda i: (0, 0)), _resident(wgt)],
        out_specs=[row, pl.BlockSpec((tm, F), lambda i: (i, 0))],
        out_shape=[jax.ShapeDtypeStruct((S, D), BF16), jax.ShapeDtypeStruct((S, F), BF16)],
        compiler_params=_params(dimension_semantics=("arbitrary",)),
    )(h, gnorm, wgt)


def _ffn_hidden(n, a, wut, name, tm=512):
    S, D = n.shape
    F = wut.shape[0]

    def body(n_ref, a_ref, wu_ref, b_ref, s_ref):
        n_t = n_ref[...]
        for lo in range(0, F, F_HALF):
            cols = slice(lo, lo + F_HALF)
            a = a_ref[:, cols].astype(F32)
            b = _dot_nt(n_t, wu_ref[cols, :])
            b_ref[:, cols] = b.astype(BF16)
            s_ref[:, cols] = (a * jax.nn.sigmoid(a) * b).astype(BF16)

    act = pl.BlockSpec((tm, F), lambda i: (i, 0))
    hidden = jax.ShapeDtypeStruct((S, F), BF16)
    return pl.pallas_call(
        body, name=name, grid=(S // tm,), in_specs=[pl.BlockSpec((tm, D), lambda i: (i, 0)), act, _resident(wut)],
        out_specs=[act, act], out_shape=[hidden, hidden],
        compiler_params=_params(dimension_semantics=("arbitrary",)),
    )(n, a, wut)


def _ffn_down(h, s, wd, name, tm=1024):
    S, D = h.shape
    F = wd.shape[0]

    def body(h_ref, s_ref, wd_ref, o_ref):
        o_ref[...] = h_ref[...] + 0.5 * _dot(s_ref[...], wd_ref[...])

    row = pl.BlockSpec((tm, D), lambda i: (i, 0))
    return pl.pallas_call(
        body, name=name, grid=(S // tm,), in_specs=[row, pl.BlockSpec((tm, F), lambda i: (i, 0)), _resident(wd)],
        out_specs=row, out_shape=jax.ShapeDtypeStruct((S, D), F32),
        compiler_params=_params(dimension_semantics=("arbitrary",)),
    )(h, s, wd)


def _ffn_down_loss(h, s, wd, target, name, tm=512):
    S, D = h.shape
    F = wd.shape[0]

    def body(h_ref, s_ref, wd_ref, t_ref, dy_ref, dyh_ref, loss_ref):
        e = h_ref[...] + 0.5 * _dot(s_ref[...], wd_ref[...]) - t_ref[...]
        dy = e * (1.0 / D)
        dy_ref[...] = dy
        dyh_ref[...] = (0.5 * dy).astype(BF16)
        tot = jnp.sum(jnp.sum(e * e, axis=1, keepdims=True), axis=0, keepdims=True) * (0.5 / D)
        _acc(loss_ref, jnp.broadcast_to(tot, loss_ref.shape), pl.program_id(0) == 0)

    row = pl.BlockSpec((tm, D), lambda i: (i, 0))
    return pl.pallas_call(
        body, name=name, grid=(S // tm,), in_specs=[row, pl.BlockSpec((tm, F), lambda i: (i, 0)), _resident(wd), row],
        out_specs=[row, row, pl.BlockSpec((8, 128), lambda i: (0, 0))],
        out_shape=[jax.ShapeDtypeStruct((S, D), F32), jax.ShapeDtypeStruct((S, D), BF16), jax.ShapeDtypeStruct((8, 128), F32)],
        compiler_params=_params(dimension_semantics=("arbitrary",)),
    )(h, s, wd, target)


def _ffn_bwd_h(dyh, a, b, wd, token, name, tm=512):
    S, D = dyh.shape
    F = wd.shape[0]

    def body(dyh_ref, a_ref, b_ref, wd_ref, _, da_ref, db_ref):
        dyh_t = dyh_ref[...]
        for lo in range(0, F, F_HALF):
            cols = slice(lo, lo + F_HALF)
            av = a_ref[:, cols].astype(F32)
            bv = b_ref[:, cols].astype(F32)
            ds = _dot_nt(dyh_t, wd_ref[cols, :])
            sig = jax.nn.sigmoid(av)
            silu = av * sig
            da_ref[:, cols] = (ds * bv * (sig * (1.0 + av * (1.0 - sig)))).astype(BF16)
            db_ref[:, cols] = (ds * silu).astype(BF16)

    act = pl.BlockSpec((tm, F), lambda i: (i, 0))
    hidden = jax.ShapeDtypeStruct((S, F), BF16)
    return pl.pallas_call(
        body, name=name, grid=(S // tm,),
        in_specs=[pl.BlockSpec((tm, D), lambda i: (i, 0)), act, act, _resident(wd), TOKEN_SPEC],
        out_specs=[act, act], out_shape=[hidden, hidden],
        compiler_params=_params(dimension_semantics=("arbitrary",)),
    )(dyh, a, b, wd, token)


def _xty(x, y, name, tk=2048, tf=1408):
    S, F = x.shape
    D = y.shape[1]

    def body(x_ref, y_ref, o_ref):
        _acc(o_ref, _dot_tn(x_ref[...], y_ref[...]), pl.program_id(1) == 0)

    return pl.pallas_call(
        body, name=name, grid=(F // tf, S // tk),
        in_specs=[pl.BlockSpec((tk, tf), lambda j, k: (k, j)), pl.BlockSpec((tk, D), lambda j, k: (k, 0))],
        out_specs=pl.BlockSpec((tf, D), lambda j, k: (j, 0)), out_shape=jax.ShapeDtypeStruct((F, D), F32),
        compiler_params=_params(dimension_semantics=("arbitrary", "arbitrary")),
    )(x, y)


def _ffn_bwd_x(da, db, wgt, wut, x, dh, gnorm, name, tm=512):
    S, D = x.shape
    F = wgt.shape[0]

    def body(da_ref, db_ref, wg_ref, wu_ref, x_ref, dh_ref, g_ref, dx_ref, dg_ref):
        dn = _dot(da_ref[...], wg_ref[...]) + _dot(db_ref[...], wu_ref[...])
        dx, dg_rows = _rms_bwd(dn, x_ref[...], g_ref[...])
        dx_ref[...] = dh_ref[...] + dx
        _acc(dg_ref, jnp.sum(dg_rows, axis=0, keepdims=True), pl.program_id(0) == 0)

    row = pl.BlockSpec((tm, D), lambda i: (i, 0))
    act = pl.BlockSpec((tm, F), lambda i: (i, 0))
    vec = pl.BlockSpec((1, D), lambda i: (0, 0))
    return pl.pallas_call(
        body, name=name, grid=(S // tm,), in_specs=[act, act, _resident(wgt), _resident(wut), row, row, vec],
        out_specs=[row, vec],
        out_shape=[jax.ShapeDtypeStruct((S, D), F32), jax.ShapeDtypeStruct((1, D), F32)],
        compiler_params=_params(dimension_semantics=("arbitrary",)),
    )(da, db, wgt, wut, x, dh, gnorm)


def _mix_proj(h, gnorm, wint, tm=512):
    S, D = h.shape
    nt = S // tm

    def body(h_ref, g_ref, w_ref, u_ref, xp_ref, q_ref, kv_ref, gl_ref):
        x = h_ref[...]
        u = (x * _rinv(x) * g_ref[...]).astype(BF16)
        u_ref[...] = u
        for (off, width), ref in zip(SEGMENTS, (xp_ref, q_ref, kv_ref, gl_ref)):
            ref[...] = _dot_nt(u, w_ref[off:off + width, :]).astype(ref.dtype)

    def row(width):
        return pl.BlockSpec((tm, width), lambda i: (i, 0))

    return pl.pallas_call(
        body, name="mix_proj", grid=(nt,),
        in_specs=[row(D), pl.BlockSpec((1, D), lambda i: (0, 0)), pl.BlockSpec(wint.shape, lambda i: (0, 0))],
        out_specs=[row(D), row(SEG_POOL[1]), row(SEG_Q[1]), row(SEG_KV[1]), row(SEG_GATE[1])],
        out_shape=[jax.ShapeDtypeStruct((S, D), BF16), jax.ShapeDtypeStruct((S, SEG_POOL[1]), F32),
                   jax.ShapeDtypeStruct((S, SEG_Q[1]), BF16), jax.ShapeDtypeStruct((S, SEG_KV[1]), BF16),
                   jax.ShapeDtypeStruct((S, SEG_GATE[1]), BF16)],
        compiler_params=_params(dimension_semantics=("arbitrary",)),
    )(h, gnorm, wint)


def _stack_heads(x, g):
    return jnp.concatenate([x[:, (GQA_GROUP * g + hh) * HEAD_DIM:(GQA_GROUP * g + hh + 1) * HEAD_DIM]
                            for hh in range(GQA_GROUP)], axis=0)


def _unstack_heads(ref, val, g):
    for hh in range(GQA_GROUP):
        lo = (GQA_GROUP * g + hh) * HEAD_DIM
        ref[:, lo:lo + HEAD_DIM] = val[hh * ATTN_BLOCK:(hh + 1) * ATTN_BLOCK, :]


def _rowsum(xb, width):
    return _dot(xb, jnp.ones((xb.shape[1], width), BF16))


def _rinv_lanes(x):
    return lax.rsqrt(_rowsum((x * x).astype(BF16), x.shape[1]) * (1.0 / x.shape[1]) + RMS_EPS)


def _twice(x):
    return jnp.concatenate([x, x], axis=1)


def _band_bias():
    qi = (jnp.arange(GQA_GROUP * ATTN_BLOCK) % ATTN_BLOCK)[:, None]
    kj = jnp.arange(2 * ATTN_BLOCK)[None, :]
    return jnp.where(jnp.logical_and(kj > qi, kj <= qi + ATTN_BLOCK), 0.0, NEG).astype(F32)


def _attn_exp(qn, kk, sink, band, n):
    kj = lax.broadcasted_iota(jnp.int32, (1, 2 * ATTN_BLOCK), 1)
    before_sequence = jnp.where(jnp.logical_and(kj < ATTN_BLOCK, n == 0), NEG, 0.0)
    s = _dot_nt(qn, kk) + band + before_sequence
    m = jnp.maximum(jnp.broadcast_to(jnp.max(s, axis=-1, keepdims=True), sink.shape), sink)
    e = jnp.exp(s - _twice(m))
    e_s = jnp.exp(sink - m)
    e_b = e.astype(BF16)
    inv = 1.0 / (_rowsum(e_b, ATTN_BLOCK) + e_s)
    return e_b, e, e_s, inv


def _attn_blocks(n):
    cur = pl.multiple_of(n * ATTN_BLOCK, ATTN_BLOCK)
    prev = pl.multiple_of(jnp.maximum(n - 1, 0) * ATTN_BLOCK, ATTN_BLOCK)
    return cur, prev


def _kv_window(kv_ref, n):
    cur, prev = _attn_blocks(n)
    return jnp.concatenate([kv_ref[pl.ds(prev, ATTN_BLOCK), :], kv_ref[pl.ds(cur, ATTN_BLOCK), :]], axis=0).astype(F32)


def _kv_split(kv, g):
    k = kv[:, g * HEAD_DIM:(g + 1) * HEAD_DIM]
    v = kv[:, (N_KV_HEADS + g) * HEAD_DIM:(N_KV_HEADS + g + 1) * HEAD_DIM]
    return k, v


def _attn_fwd(q, kv, qw, kw, sink_rows, band):
    S, W = q.shape
    nb = S // ATTN_BLOCK

    def body(q_ref, kv_ref, qw_ref, kw_ref, sk_ref, band_ref, o_ref, o_scr):
        n = pl.program_id(0)
        qf = q_ref[...].astype(F32)
        kvw = _kv_window(kv_ref, n)
        for g in range(N_KV_HEADS):
            qs = _stack_heads(qf, g)
            qn = (qs * _rinv_lanes(qs) * qw_ref[...] * SCALE).astype(BF16)
            k, v = _kv_split(kvw, g)
            kk = (k * _rinv_lanes(k) * kw_ref[...]).astype(BF16)
            e_b, _, _, inv = _attn_exp(qn, kk, sk_ref[g], band_ref[...], n)
            _unstack_heads(o_scr, _dot(e_b, v.astype(BF16)) * inv[:, :HEAD_DIM], g)
        o_ref[...] = o_scr[...].astype(BF16)

    blk = pl.BlockSpec((ATTN_BLOCK, W), lambda n: (n, 0))
    return pl.pallas_call(
        body, name="attn_fwd", grid=(nb,),
        in_specs=[blk, pl.BlockSpec(kv.shape, lambda n: (0, 0)), pl.BlockSpec((1, HEAD_DIM), lambda n: (0, 0)),
                  pl.BlockSpec((1, HEAD_DIM), lambda n: (0, 0)), pl.BlockSpec(sink_rows.shape, lambda n: (0, 0, 0)),
                  pl.BlockSpec(band.shape, lambda n: (0, 0))],
        out_specs=blk, out_shape=jax.ShapeDtypeStruct((S, W), BF16),
        scratch_shapes=[pltpu.VMEM((ATTN_BLOCK, W), F32)],
        compiler_params=_params(dimension_semantics=("arbitrary",)),
    )(q, kv, qw, kw, sink_rows, band)


def _attn_bwd(q, kv, do, qw, kw, sink_rows, band):
    S, W = q.shape
    KW = kv.shape[1]
    nb = S // ATTN_BLOCK
    chunk = 512

    def body(q_ref, kv_ref, do_ref, qw_ref, kw_ref, sk_ref, band_ref, dq_ref, dkv_ref, dqw_ref, dkw_ref, dsk_ref, dq_scr):
        n = pl.program_id(0)

        @pl.when(n == 0)
        def _():
            dkv_ref[...] = jnp.zeros_like(dkv_ref)
            dqw_ref[...] = jnp.zeros_like(dqw_ref)
            dsk_ref[...] = jnp.zeros_like(dsk_ref)

        qf = q_ref[...].astype(F32)
        dof = do_ref[...].astype(F32)
        cur, prev = _attn_blocks(n)
        kvw = _kv_window(kv_ref, n)
        qw_v = qw_ref[...]
        for g in range(N_KV_HEADS):
            qs = _stack_heads(qf, g)
            rq = _rinv_lanes(qs)
            qhat = qs * rq
            qn = (qhat * qw_v * SCALE).astype(BF16)
            k, v = _kv_split(kvw, g)
            kk = (k * _rinv_lanes(k) * kw_ref[...]).astype(BF16)
            vv = v.astype(BF16)
            dos = _stack_heads(dof, g).astype(BF16)
            _, e, e_s, inv = _attn_exp(qn, kk, sk_ref[g], band_ref[...], n)
            p = e * _twice(inv)
            dp = _dot_nt(dos, vv)
            drow = _rowsum((p * dp).astype(BF16), ATTN_BLOCK)
            ds = (p * (dp - _twice(drow))).astype(BF16)
            dsink = -(e_s * inv * drow)
            for hh in range(GQA_GROUP):
                tot = jnp.sum(dsink[hh * ATTN_BLOCK:(hh + 1) * ATTN_BLOCK, :], axis=0, keepdims=True)
                dsk_ref[g, hh:hh + 1, :] += tot
            dqn = _dot(ds, kk) * SCALE
            dkk = _dot_tn(ds, qn)
            dvv = _dot_tn(p.astype(BF16), dos)
            klo, vlo = g * HEAD_DIM, (N_KV_HEADS + g) * HEAD_DIM
            for start, rows in ((prev, slice(0, ATTN_BLOCK)), (cur, slice(ATTN_BLOCK, 2 * ATTN_BLOCK))):
                dkv_ref[pl.ds(start, ATTN_BLOCK), klo:klo + HEAD_DIM] += dkk[rows]
                dkv_ref[pl.ds(start, ATTN_BLOCK), vlo:vlo + HEAD_DIM] += dvv[rows]
            dqw_ref[...] += jnp.sum(dqn * qhat, axis=0, keepdims=True)
            z = dqn * qw_v
            dqs = rq * (z - qhat * (_rowsum((z * qhat).astype(BF16), HEAD_DIM) * (1.0 / HEAD_DIM)))
            _unstack_heads(dq_scr, dqs, g)
        dq_ref[...] = dq_scr[...].astype(BF16)

        @pl.when(n == nb - 1)
        def _():
            def one(c, dkw):
                rows = pl.ds(pl.multiple_of(c * chunk, chunk), chunk)
                for g in range(N_KV_HEADS):
                    lo = g * HEAD_DIM
                    k = kv_ref[rows, lo:lo + HEAD_DIM].astype(F32)
                    dx, dg_rows = _rms_bwd(dkv_ref[rows, lo:lo + HEAD_DIM], k, kw_ref[...])
                    dkv_ref[rows, lo:lo + HEAD_DIM] = dx
                    dkw = dkw + jnp.sum(dg_rows, axis=0, keepdims=True)
                return dkw

            dkw_ref[...] = lax.fori_loop(0, S // chunk, one, jnp.zeros((1, HEAD_DIM), F32))

    blk = pl.BlockSpec((ATTN_BLOCK, W), lambda n: (n, 0))
    whole_kv = pl.BlockSpec((S, KW), lambda n: (0, 0))
    vec = pl.BlockSpec((1, HEAD_DIM), lambda n: (0, 0))
    sk = pl.BlockSpec(sink_rows.shape, lambda n: (0, 0, 0))
    dsk = pl.BlockSpec((N_KV_HEADS, GQA_GROUP, 128), lambda n: (0, 0, 0))
    return pl.pallas_call(
        body, name="attn_bwd", grid=(nb,), in_specs=[blk, whole_kv, blk, vec, vec, sk, pl.BlockSpec(band.shape, lambda n: (0, 0))],
        out_specs=[blk, whole_kv, vec, vec, dsk],
        out_shape=[jax.ShapeDtypeStruct((S, W), BF16), jax.ShapeDtypeStruct((S, KW), F32),
                   jax.ShapeDtypeStruct((1, HEAD_DIM), F32), jax.ShapeDtypeStruct((1, HEAD_DIM), F32),
                   jax.ShapeDtypeStruct((N_KV_HEADS, GQA_GROUP, 128), F32)],
        scratch_shapes=[pltpu.VMEM((ATTN_BLOCK, W), F32)],
        compiler_params=_params(dimension_semantics=("arbitrary",)),
    )(q, kv, do, qw, kw, sink_rows, band)


def _pooled(xc, xprev, i):
    tm = xc.shape[0]
    xh = jnp.concatenate([jnp.where(i > 0, xprev, 0.0), xc], axis=0)
    t = lax.broadcasted_iota(jnp.int32, (tm, 1), 0) + i * tm
    out = []
    for gi, w in enumerate(POOL_WINDOWS):
        acc = xh[:, gi * POOL_GROUP:(gi + 1) * POOL_GROUP]
        sh = 1
        while sh < w:
            acc = acc + pltpu.roll(acc, sh, 0)
            sh *= 2
        cnt = jnp.minimum(t + 1, w).astype(F32)
        out.append(acc[POOL_HALO:, :] / cnt - xc[:, gi * POOL_GROUP:(gi + 1) * POOL_GROUP])
    return jnp.concatenate(out, axis=1)


def _pool_mix(pooled_b, pw_ref):
    return jnp.concatenate([_dot(pooled_b[:, gi * POOL_GROUP:(gi + 1) * POOL_GROUP], pw_ref[gi])
                            for gi in range(len(POOL_WINDOWS))], axis=1)


def _halo_specs(tm, width, S, after):
    per = tm // POOL_HALO
    last = S // POOL_HALO - 1
    if after:
        return pl.BlockSpec((POOL_HALO, width), lambda i: (jnp.minimum((i + 1) * per, last), 0))
    return pl.BlockSpec((POOL_HALO, width), lambda i: (jnp.maximum(i * per - 1, 0), 0))


def _mix_out(xp, attn, gl, bias, pw, pscale, wpot, wao, wo, h, tm=512):
    S, D = h.shape
    nt = S // tm
    PW = xp.shape[1]

    def body(xc_ref, xprev_ref, at_ref, gl_ref, bias_ref, pw_ref, ps_ref, wpo_ref, wao_ref, wo_ref, h_ref,
             ho_ref, bp_ref, ba_ref):
        i = pl.program_id(0)
        pooled = _pooled(xc_ref[...], xprev_ref[...], i).astype(BF16)
        ms = (_pool_mix(pooled, pw_ref) * ps_ref[...]).astype(BF16)
        bp = _dot_nt(ms, wpo_ref[...])
        ba = _dot(at_ref[...], wao_ref[...])
        bp_ref[...] = bp.astype(BF16)
        ba_ref[...] = ba.astype(BF16)
        gates = jax.nn.sigmoid(gl_ref[...].astype(F32) + bias_ref[...])
        merged = (gates[:, :D] * bp + gates[:, D:] * ba).astype(BF16)
        ho_ref[...] = h_ref[...] + _dot(merged, wo_ref[...])

    def row(width):
        return pl.BlockSpec((tm, width), lambda i: (i, 0))

    def whole(x):
        nd = x.ndim
        return pl.BlockSpec(x.shape, lambda i: (0,) * nd)

    return pl.pallas_call(
        body, name="mix_out", grid=(nt,),
        in_specs=[row(PW), _halo_specs(tm, PW, S, False), row(D), row(2 * D), whole(bias), whole(pw), whole(pscale),
                  whole(wpot), whole(wao), whole(wo), row(D)],
        out_specs=[row(D), row(D), row(D)],
        out_shape=[jax.ShapeDtypeStruct((S, D), F32), jax.ShapeDtypeStruct((S, D), BF16),
                   jax.ShapeDtypeStruct((S, D), BF16)],
        compiler_params=_params(dimension_semantics=("arbitrary",)),
    )(xp, xp, attn, gl, bias, pw, pscale, wpot, wao, wo, h)


def _mix_bwd_gate(dh, wo, bp, ba, gl, bias, tm=512):
    S, D = dh.shape
    nt = S // tm

    def body(dh_ref, wo_ref, bp_ref, ba_ref, gl_ref, bias_ref, dgl_ref, dbp_ref, dba_ref, dwo_ref, dbias_ref):
        i = pl.program_id(0)
        dhb = dh_ref[...].astype(BF16)
        dm = _dot_nt(dhb, wo_ref[...])
        gates = jax.nn.sigmoid(gl_ref[...].astype(F32) + bias_ref[...])
        gp, ga = gates[:, :D], gates[:, D:]
        bp_v = bp_ref[...].astype(F32)
        ba_v = ba_ref[...].astype(F32)
        merged = (gp * bp_v + ga * ba_v).astype(BF16)
        _acc(dwo_ref, _dot_tn(merged, dhb), i == 0)
        dbp_ref[...] = (dm * gp).astype(BF16)
        dba_ref[...] = (dm * ga).astype(BF16)
        dgl = jnp.concatenate([dm * bp_v * gp * (1.0 - gp), dm * ba_v * ga * (1.0 - ga)], axis=1)
        dgl_ref[...] = dgl.astype(BF16)
        _acc(dbias_ref, jnp.sum(dgl, axis=0, keepdims=True), i == 0)

    def row(width):
        return pl.BlockSpec((tm, width), lambda i: (i, 0))

    def whole(shape):
        return pl.BlockSpec(shape, lambda i: (0, 0))

    return pl.pallas_call(
        body, name="mix_bwd_gate", grid=(nt,),
        in_specs=[row(D), whole(wo.shape), row(D), row(D), row(2 * D), whole(bias.shape)],
        out_specs=[row(2 * D), row(D), row(D), whole((D, D)), whole((1, 2 * D))],
        out_shape=[jax.ShapeDtypeStruct((S, 2 * D), BF16), jax.ShapeDtypeStruct((S, D), BF16),
                   jax.ShapeDtypeStruct((S, D), BF16), jax.ShapeDtypeStruct((D, D), F32),
                   jax.ShapeDtypeStruct((1, 2 * D), F32)],
        compiler_params=_params(dimension_semantics=("arbitrary",)),
    )(dh, wo, bp, ba, gl, bias)


def _mix_bwd_branch(dbp, dba, attn, xp, pw, pscale, wpot, wao, tm=1024):
    S, D = dbp.shape
    nt = S // tm
    PW = xp.shape[1]
    NG = len(POOL_WINDOWS)

    def body(dbp_ref, dba_ref, at_ref, xc_ref, xprev_ref, pw_ref, ps_ref, wpo_ref, wao_ref,
             dat_ref, dpl_ref, dwao_ref, dwpo_ref, dpw_ref, dps_ref):
        i = pl.program_id(0)
        dba_v = dba_ref[...]
        dbp_v = dbp_ref[...]
        _acc(dwao_ref, _dot_tn(at_ref[...], dba_v), i == 0)
        dat_ref[...] = _dot_nt(dba_v, wao_ref[...]).astype(BF16)
        pooled = _pooled(xc_ref[...], xprev_ref[...], i).astype(BF16)
        mixed = _pool_mix(pooled, pw_ref)
        ps = ps_ref[...]
        _acc(dwpo_ref, _dot_tn(dbp_v, (mixed * ps).astype(BF16)), i == 0)
        dms = _dot(dbp_v, wpo_ref[...])
        _acc(dps_ref, jnp.sum(dms * mixed, axis=0, keepdims=True), i == 0)
        dmixed = (dms * ps).astype(BF16)
        dpooled = []
        for gi in range(NG):
            cols = slice(gi * POOL_GROUP, (gi + 1) * POOL_GROUP)
            _acc(dpw_ref.at[gi], _dot_tn(pooled[:, cols], dmixed[:, cols]), i == 0)
            dpooled.append(_dot_nt(dmixed[:, cols], pw_ref[gi]))
        dpl_ref[...] = jnp.concatenate(dpooled, axis=1)

    def row(width):
        return pl.BlockSpec((tm, width), lambda i: (i, 0))

    def whole(shape):
        nd = len(shape)
        return pl.BlockSpec(shape, lambda i: (0,) * nd)

    return pl.pallas_call(
        body, name="mix_bwd_branch", grid=(nt,),
        in_specs=[row(D), row(D), row(D), row(PW), _halo_specs(tm, PW, S, False), whole(pw.shape), whole(pscale.shape),
                  whole(wpot.shape), whole(wao.shape)],
        out_specs=[row(D), row(PW), whole((D, D)), whole((D, PW)), whole(pw.shape), whole((1, PW))],
        out_shape=[jax.ShapeDtypeStruct((S, D), BF16), jax.ShapeDtypeStruct((S, PW), F32),
                   jax.ShapeDtypeStruct((D, D), F32), jax.ShapeDtypeStruct((D, PW), F32),
                   jax.ShapeDtypeStruct(pw.shape, F32), jax.ShapeDtypeStruct((1, PW), F32)],
        compiler_params=_params(dimension_semantics=("arbitrary",)),
    )(dbp, dba, attn, xp, xp, pw, pscale, wpot, wao)


def _pool_bwd(dpooled, tm=512):
    S, PW = dpooled.shape
    nt = S // tm

    def body(dc_ref, dnext_ref, dxp_ref):
        i = pl.program_id(0)
        dc = dc_ref[...]
        dh = jnp.concatenate([dc, jnp.where(i < nt - 1, dnext_ref[...], 0.0)], axis=0)
        rows = tm + POOL_HALO
        t = lax.broadcasted_iota(jnp.int32, (rows, 1), 0) + i * tm
        out = []
        for gi, w in enumerate(POOL_WINDOWS):
            cols = slice(gi * POOL_GROUP, (gi + 1) * POOL_GROUP)
            acc = dh[:, cols] / jnp.minimum(t + 1, w).astype(F32)
            sh = 1
            while sh < w:
                acc = acc + pltpu.roll(acc, rows - sh, 0)
                sh *= 2
            out.append(acc[:tm, :] - dc[:, cols])
        dxp_ref[...] = jnp.concatenate(out, axis=1).astype(BF16)

    return pl.pallas_call(
        body, name="pool_bwd", grid=(nt,),
        in_specs=[pl.BlockSpec((tm, PW), lambda i: (i, 0)), _halo_specs(tm, PW, S, True)],
        out_specs=pl.BlockSpec((tm, PW), lambda i: (i, 0)), out_shape=jax.ShapeDtypeStruct((S, PW), BF16),
        compiler_params=_params(dimension_semantics=("arbitrary",)),
    )(dpooled, dpooled)


def _in_bwd_w(pieces, u, tm=1024):
    S, D = u.shape
    nt = S // tm
    NW = sum(width for _, width in SEGMENTS)

    def body(*refs):
        piece_refs, u_ref, dw_hbm, acc, sem = refs[:len(SEGMENTS)], refs[len(SEGMENTS)], refs[len(SEGMENTS) + 1], refs[-2], refs[-1]
        i = pl.program_id(0)
        u_t = u_ref[...]
        for (off, width), ref in zip(SEGMENTS, piece_refs):
            for lo in range(0, width, D):
                hi = min(lo + D, width)
                _acc(acc.at[off + lo:off + hi, :], _dot_tn(ref[:, lo:hi].astype(BF16), u_t), i == 0)

        @pl.when(i == nt - 1)
        def _():
            cp = pltpu.make_async_copy(acc, dw_hbm, sem)
            cp.start()
            cp.wait()

    return pl.pallas_call(
        body, name="in_bwd_w", grid=(nt,),
        in_specs=[pl.BlockSpec((tm, width), lambda i: (i, 0)) for _, width in SEGMENTS] + [pl.BlockSpec((tm, D), lambda i: (i, 0))],
        out_specs=pl.BlockSpec(memory_space=pl.ANY), out_shape=jax.ShapeDtypeStruct((NW, D), F32),
        scratch_shapes=[pltpu.VMEM((NW, D), F32), pltpu.SemaphoreType.DMA],
        compiler_params=_params(dimension_semantics=("arbitrary",)),
    )(*pieces, u)


def _in_bwd_x(pieces, wint, h, dh, gnorm, tm=512):
    S, D = h.shape
    nt = S // tm

    def body(*refs):
        piece_refs = refs[:len(SEGMENTS)]
        w_ref, h_ref, dh_ref, g_ref, dx_ref, dxh_ref, dg_ref = refs[len(SEGMENTS):]
        i = pl.program_id(0)
        du = jnp.zeros((tm, D), F32)
        for (off, width), ref in zip(SEGMENTS, piece_refs):
            du = du + _dot(ref[...].astype(BF16), w_ref[off:off + width, :])
        dx, dg_rows = _rms_bwd(du, h_ref[...], g_ref[...])
        out = dh_ref[...] + dx
        dx_ref[...] = out
        dxh_ref[...] = (0.5 * out).astype(BF16)
        _acc(dg_ref, jnp.sum(dg_rows, axis=0, keepdims=True), i == 0)

    def row(width):
        return pl.BlockSpec((tm, width), lambda i: (i, 0))

    vec = pl.BlockSpec((1, D), lambda i: (0, 0))
    return pl.pallas_call(
        body, name="in_bwd_x", grid=(nt,),
        in_specs=[row(width) for _, width in SEGMENTS] + [pl.BlockSpec(wint.shape, lambda i: (0, 0)), row(D), row(D), vec],
        out_specs=[row(D), row(D), vec],
        out_shape=[jax.ShapeDtypeStruct((S, D), F32), jax.ShapeDtypeStruct((S, D), BF16), jax.ShapeDtypeStruct((1, D), F32)],
        compiler_params=_params(dimension_semantics=("arbitrary",)),
    )(*pieces, wint, h, dh, gnorm)


def _row_tile(rows):
    for t in (512, 480, 352, 256, 128, 64, 32, 16, 8):
        if rows % t == 0:
            return t
    return rows


def _adamw(w, g, m, v, token, name):
    R, C = w.shape
    tr = _row_tile(R)

    def body(w_ref, g_ref, m_ref, v_ref, _, d_ref, mo_ref, vo_ref, go_ref):
        gv = g_ref[...]
        go_ref[...] = gv
        mn = ADAM_B1 * m_ref[...] + (1.0 - ADAM_B1) * gv
        vn = ADAM_B2 * v_ref[...] + (1.0 - ADAM_B2) * (gv * gv)
        m_hat = mn / (1.0 - ADAM_B1 ** ADAM_STEP)
        v_hat = vn / (1.0 - ADAM_B2 ** ADAM_STEP)
        d_ref[...] = -ADAM_LR * (m_hat / (jnp.sqrt(v_hat) + ADAM_EPS) + ADAM_WD * w_ref[...])
        mo_ref[...] = mn
        vo_ref[...] = vn

    blk = pl.BlockSpec((tr, C), lambda i: (i, 0))
    sh = jax.ShapeDtypeStruct((R, C), F32)
    return pl.pallas_call(
        body, name=name, grid=(R // tr,), in_specs=[blk] * 4 + [TOKEN_SPEC], out_specs=[blk] * 4, out_shape=[sh] * 4,
        compiler_params=_params(dimension_semantics=("arbitrary",)),
    )(w, g, m, v, token)


def _cast_place(w, place, name):
    R, C = w.shape
    tr = _row_tile(R)
    per = R // tr

    def body(p_ref, w_ref, o_ref):
        o_ref[...] = w_ref[...].astype(BF16)

    grid_spec = pltpu.PrefetchScalarGridSpec(
        num_scalar_prefetch=1, grid=(per,),
        in_specs=[pl.BlockSpec((tr, C), lambda i, p: (i, 0))],
        out_specs=pl.BlockSpec((tr, C), lambda i, p: (p[0] * per + i, 0)))
    return pl.pallas_call(
        body, name=name, grid_spec=grid_spec, out_shape=jax.ShapeDtypeStruct((N_SHARDS * R, C), BF16),
        compiler_params=_params(dimension_semantics=("arbitrary",)),
    )(place, w)


def _parked(w, m, idx, first, last):
    return jnp.where(m == w, idx, jnp.where(m < w, first, last))


def _sum_halves(g4s, recvs, place, name):
    M = len(g4s)
    NS, R, C = g4s[0].shape
    hr = R // 2
    tr = _row_tile(hr)
    per = hr // tr

    def body(p_ref, *refs):
        m = pl.program_id(0)
        for w in range(M):
            @pl.when(m == w)
            def _(w=w):
                refs[2 * M + w][...] = (refs[w][...] + refs[M + w][...]).astype(BF16)

    def spec(w, mine):
        def index(m, s, i, p):
            row = _parked(w, m, i, 0, per - 1)
            return _parked(w, m, s, 0, NS - 1), (p[1] * per + row if mine else row), 0

        return pl.BlockSpec((1, tr, C), index)

    grid_spec = pltpu.PrefetchScalarGridSpec(
        num_scalar_prefetch=1, grid=(M, NS, per),
        in_specs=[spec(w, True) for w in range(M)] + [spec(w, False) for w in range(M)],
        out_specs=[spec(w, False) for w in range(M)])
    return pl.pallas_call(
        body, name=name, grid_spec=grid_spec, out_shape=[jax.ShapeDtypeStruct((NS, hr, C), BF16)] * M,
        compiler_params=_params(dimension_semantics=("arbitrary",) * 3),
    )(place, *g4s, *recvs)


def _sum_quarters(h4s, recv3s, place, name):
    M = len(h4s)
    NS, hr, C = h4s[0].shape
    tr = _row_tile(hr)
    per = hr // tr

    def body(p_ref, *refs):
        m = pl.program_id(0)
        for w in range(M):
            @pl.when(m == w)
            def _(w=w):
                acc = refs[w][0].astype(F32)
                for k in range(N_SHARDS - 1):
                    acc = acc + refs[M + w][k].astype(F32)
                refs[2 * M + w][...] = acc

    def row(w, m, i):
        return _parked(w, m, i, 0, per - 1)

    grid_spec = pltpu.PrefetchScalarGridSpec(
        num_scalar_prefetch=1, grid=(M, per),
        in_specs=[pl.BlockSpec((1, tr, C), lambda m, i, p, w=w: (p[0], row(w, m, i), 0)) for w in range(M)]
        + [pl.BlockSpec((N_SHARDS - 1, tr, C), lambda m, i, p, w=w: (0, row(w, m, i), 0)) for w in range(M)],
        out_specs=[pl.BlockSpec((tr, C), lambda m, i, p, w=w: (p[1] * per + row(w, m, i), 0)) for w in range(M)])
    return pl.pallas_call(
        body, name=name, grid_spec=grid_spec, out_shape=[jax.ShapeDtypeStruct((2 * hr, C), F32)] * M,
        compiler_params=_params(dimension_semantics=("arbitrary",) * 2),
    )(place, *h4s, *recv3s)


def _by_shape(arrays):
    groups = {}
    for k, a in enumerate(arrays):
        groups.setdefault(a.shape, []).append(k)
    return list(groups.values())


def _place():
    x, y, c = lax.axis_index("x"), lax.axis_index("y"), lax.axis_index("c")
    chips = [(1 - x, y), (x, 1 - y), (1 - x, 1 - y)]
    return x, y, c, chips


HBM_SPEC = pl.BlockSpec(memory_space=pltpu.HBM)
SEM_SPEC = pl.BlockSpec(memory_space=pltpu.SEMAPHORE)
DATAFLOW = pltpu.SideEffectType.DATAFLOW_SIDE_EFFECTING


def _hbm(a):
    return pltpu.with_memory_space_constraint(a, pltpu.HBM)


class InFlight(NamedTuple):
    send_sem: jax.Array
    recv_sem: jax.Array
    bufs: list
    plan: Callable
    token: jax.Array


def _wait_all(plan, refs, send_ref, recv_ref):
    for k, (src, dst, dev) in enumerate(plan(refs)):
        cp = pltpu.make_async_remote_copy(src_ref=src, dst_ref=dst, send_sem=send_ref.at[k], recv_sem=recv_ref.at[k],
                                          device_id=dev, device_id_type=MESH)
        cp.wait_send()
        cp.wait_recv()


def _start_all(plan, refs, send_ref, recv_ref):
    for k, (src, dst, dev) in enumerate(plan(refs)):
        pltpu.make_async_remote_copy(src_ref=src, dst_ref=dst, send_sem=send_ref.at[k], recv_sem=recv_ref.at[k],
                                     device_id=dev, device_id_type=MESH).start()


def _split_start(name, bufs, plan, n_copies, after, waits=None):
    nb = len(bufs)
    after = list(after) if isinstance(after, (list, tuple)) else [after]

    def body(*refs):
        ins = refs[:nb]
        if waits is not None:
            _wait_all(waits.plan, ins, refs[nb], refs[nb + 1])
        send_sem, recv_sem, token = refs[-nb - 3], refs[-nb - 2], refs[-1]
        _start_all(plan, ins, send_sem, recv_sem)
        token[...] = jnp.zeros_like(token)

    sems = pltpu.SemaphoreType.DMA((n_copies,))
    earlier = [] if waits is None else [waits.send_sem, waits.recv_sem]
    out = pl.pallas_call(
        body, name=name, in_specs=[HBM_SPEC] * nb + [SEM_SPEC] * len(earlier) + [TOKEN_SPEC] * len(after),
        out_shape=(sems, sems, *[pltpu.HBM(b.shape, b.dtype) for b in bufs], jax.ShapeDtypeStruct((8, 128), F32)),
        out_specs=(SEM_SPEC, SEM_SPEC, *[HBM_SPEC] * nb, pl.BlockSpec(memory_space=pltpu.VMEM)),
        input_output_aliases={i: i + 2 for i in range(nb)},
        compiler_params=pltpu.CompilerParams(has_side_effects=DATAFLOW),
    )(*[_hbm(b) for b in bufs], *earlier, *after)
    return InFlight(out[0], out[1], list(out[2:2 + nb]), plan, out[-1])


def _split_wait(name, flight, after):
    nb = len(flight.bufs)
    after = list(after) if isinstance(after, (list, tuple)) else [after]

    def body(*refs):
        _wait_all(flight.plan, refs[:nb], refs[nb], refs[nb + 1])

    out = pl.pallas_call(
        body, name=name, in_specs=[HBM_SPEC] * nb + [SEM_SPEC, SEM_SPEC] + [TOKEN_SPEC] * len(after),
        out_shape=[pltpu.HBM(b.shape, b.dtype) for b in flight.bufs], out_specs=[HBM_SPEC] * nb,
        input_output_aliases={i: i for i in range(nb)},
        compiler_params=pltpu.CompilerParams(has_side_effects=DATAFLOW),
    )(*flight.bufs, flight.send_sem, flight.recv_sem, *after)
    return list(out)


def _half_rows(buf, chip, core):
    hr = buf.shape[0] // (2 * N_SHARDS)
    return buf.at[pl.ds(pl.multiple_of((2 * chip + core) * hr, 16), hr)]


def _gather_ici_plan(bufs):
    x, y, c, chips = _place()
    return [(_half_rows(b, 2 * x + y, c), _half_rows(b, 2 * x + y, c), (px, py, c)) for b in bufs for px, py in chips]


def _gather_d2d_plan(bufs):
    x, y, c, chips = _place()
    return [(_half_rows(b, 2 * px + py, c), _half_rows(b, 2 * px + py, c), (x, y, 1 - c)) for b in bufs for px, py in chips]


def _swap_plan(bufs):
    x, y, c, _ = _place()
    n = len(bufs) // 2
    copies = []
    for g, land in zip(bufs[:n], bufs[n:]):
        hr = g.shape[1] // 2
        copies.append((g.at[:, pl.ds(pl.multiple_of((1 - c) * hr, 8), hr)], land, (x, y, 1 - c)))
    return copies


def _exchange_plan(bufs):
    x, y, c, chips = _place()
    n = len(bufs) // 2
    return [(h.at[2 * px + py], land.at[k], (px, py, c))
            for h, land in zip(bufs[:n], bufs[n:]) for k, (px, py) in enumerate(chips)]


def _share_plan(bufs):
    x, y, c, _ = _place()
    copies = []
    for buf in bufs:
        hr = buf.shape[0] // 2
        mine = buf.at[pl.ds(pl.multiple_of(c * hr, 8), hr)]
        copies.append((mine, mine, (x, y, 1 - c)))
    return copies


N_DEVICES = 8


def _slot_place(vec, place):
    R, C = vec.shape

    def body(p_ref, v_ref, o_ref):
        o_ref[0] = v_ref[...]

    grid_spec = pltpu.PrefetchScalarGridSpec(
        num_scalar_prefetch=1, grid=(1,), in_specs=[pl.BlockSpec((R, C), lambda i, p: (0, 0))],
        out_specs=pl.BlockSpec((1, R, C), lambda i, p: (2 * p[0] + p[1], 0, 0)))
    return pl.pallas_call(
        body, name="slot_place", grid_spec=grid_spec, out_shape=jax.ShapeDtypeStruct((N_DEVICES, R, C), F32),
        compiler_params=_params(dimension_semantics=("arbitrary",)),
    )(place, vec)


def _slots_plan(bufs):
    x, y, c, _ = _place()
    mine = bufs[0].at[4 * x + 2 * y + c]
    return [(mine, mine, (x ^ (r >> 2), y ^ ((r >> 1) & 1), c ^ (r & 1))) for r in range(1, N_DEVICES)]


def _sum_slots(slots):
    _, R, C = slots.shape

    def body(s_ref, o_ref):
        acc = s_ref[0]
        for d in range(1, N_DEVICES):
            acc = acc + s_ref[d]
        o_ref[...] = acc

    return pl.pallas_call(
        body, name="sum_slots", in_specs=[pl.BlockSpec(memory_space=pltpu.VMEM)],
        out_specs=pl.BlockSpec(memory_space=pltpu.VMEM), out_shape=jax.ShapeDtypeStruct((R, C), F32),
        compiler_params=_params(),
    )(slots)


SMALL = ("ffn1_norm", "mix_norm", "pool_w", "pool_scale", "q_norm", "k_norm", "sinks", "gate_bias", "ffn2_norm")
SMALL_COLS = 1024
FFN1 = ("ffn1_w_gate", "ffn1_w_up", "ffn1_w_down")
MIXER = ("w_in", "w_pool_out", "w_attn_out", "w_out")
FFN2 = ("ffn2_w_gate", "ffn2_w_up", "ffn2_w_down")
LARGE = FFN1 + MIXER + FFN2
TRANSPOSED = ("ffn1_w_gate", "ffn1_w_up", "w_in", "w_pool_out", "ffn2_w_gate", "ffn2_w_up")
WEIGHTS = ("ffn1_norm", "ffn1_w_gate", "ffn1_w_up", "ffn1_w_down", "mix_norm", "w_in", "pool_w", "pool_scale",
           "w_pool_out", "q_norm", "k_norm", "sinks", "w_attn_out", "gate_bias", "w_out", "ffn2_norm",
           "ffn2_w_gate", "ffn2_w_up", "ffn2_w_down")


def _pack_small(parts):
    flat = jnp.concatenate([p.reshape(-1) for p in parts])
    rows = -(-flat.shape[0] // (8 * SMALL_COLS)) * 8
    return jnp.pad(flat, (0, rows * SMALL_COLS - flat.shape[0])).reshape(rows, SMALL_COLS)


def _unpack_small(packed, like):
    flat = packed.reshape(-1)
    out, off = [], 0
    for p in like:
        out.append(flat[off:off + p.size].reshape(p.shape))
        off += p.size
    return out, flat[off]


def _tie(small, token):
    return small + token[0, 0]


class Reduction:
    def __init__(self, group, names, grads, place):
        self.group, self.names, self.place = group, names, place
        self.bufs = [g.reshape(N_SHARDS, -1, g.shape[-1]) for g in grads]
        self.flight = None

    def _start(self, stage, bufs, plan, n_copies, after):
        self.flight = _split_start(f"{self.group}_{stage}", bufs, plan, n_copies, after)
        return self.flight.token

    def _landed(self, stage, after):
        n = len(self.names)
        bufs = _split_wait(f"{self.group}_{stage}_wait", self.flight, after)
        return bufs[:n], bufs[n:]

    def start_swap(self, after):
        lands = [lax.empty((g.shape[0], g.shape[1] // 2, g.shape[2]), g.dtype) for g in self.bufs]
        return self._start("swap", self.bufs + lands, _swap_plan, len(self.bufs), after)

    def start_exchange(self, after):
        g4, recv = self._landed("swap", after)
        halves = self._per_shape(_sum_halves, "sum_halves", g4, recv)
        lands = [lax.empty((N_SHARDS - 1,) + h.shape[1:], h.dtype) for h in halves]
        return self._start("exchange", halves + lands, _exchange_plan, (N_SHARDS - 1) * len(halves), self.place)

    def start_share(self, after):
        halves, recv3 = self._landed("exchange", after)
        reduced = self._per_shape(_sum_quarters, "sum_quarters", halves, recv3)
        return self._start("share", reduced, _share_plan, len(reduced), self.place)

    def _per_shape(self, add, stage, mine, received):
        out = [None] * len(mine)
        for idx in _by_shape(mine):
            sums = add([mine[k] for k in idx], [received[k] for k in idx], self.place, f"{stage}_{self.names[idx[0]]}")
            for k, v in zip(idx, sums):
                out[k] = v
        return out

    def finish(self, after):
        return dict(zip(self.names, _split_wait(f"{self.group}_share_wait", self.flight, after)))


def _row_form(name, a):
    return a.T if name in TRANSPOSED else a


def kernel(x, ffn1_norm, ffn1_w_gate, ffn1_w_up, ffn1_w_down, mix_norm, w_in, pool_w, pool_scale, w_pool_out, q_norm, k_norm, sinks, w_attn_out, gate_bias, w_out, ffn2_norm, ffn2_w_gate, ffn2_w_up, ffn2_w_down, loss_target, m_ffn1_norm, m_ffn1_w_gate, m_ffn1_w_up, m_ffn1_w_down, m_mix_norm, m_w_in, m_pool_w, m_pool_scale, m_w_pool_out, m_q_norm, m_k_norm, m_sinks, m_w_attn_out, m_gate_bias, m_w_out, m_ffn2_norm, m_ffn2_w_gate, m_ffn2_w_up, m_ffn2_w_down, v_ffn1_norm, v_ffn1_w_gate, v_ffn1_w_up, v_ffn1_w_down, v_mix_norm, v_w_in, v_pool_w, v_pool_scale, v_w_pool_out, v_q_norm, v_k_norm, v_sinks, v_w_attn_out, v_gate_bias, v_w_out, v_ffn2_norm, v_ffn2_w_gate, v_ffn2_w_up, v_ffn2_w_down):
    args = dict(locals())
    wts = {n: _row_form(n, args[n]) for n in WEIGHTS}
    mom = {n: _row_form(n, args["m_" + n]) for n in WEIGHTS}
    var = {n: _row_form(n, args["v_" + n]) for n in WEIGHTS}
    shard = 2 * lax.axis_index("x") + lax.axis_index("y")
    place = jnp.stack([shard, lax.axis_index("c")]).astype(jnp.int32)

    xs, target = x[0], loss_target[0]
    D = xs.shape[1]
    g1 = wts["ffn1_norm"].reshape(1, D)
    gm = wts["mix_norm"].reshape(1, D)
    g2 = wts["ffn2_norm"].reshape(1, D)
    qw = wts["q_norm"].reshape(1, HEAD_DIM)
    kw = wts["k_norm"].reshape(1, HEAD_DIM)
    bias = wts["gate_bias"].reshape(1, 2 * D)
    pscale = wts["pool_scale"].reshape(1, -1)
    pw = wts["pool_w"].astype(BF16)
    sink_rows = jnp.broadcast_to(jnp.repeat(wts["sinks"], ATTN_BLOCK).reshape(N_KV_HEADS, GQA_GROUP * ATTN_BLOCK, 1),
                                 (N_KV_HEADS, GQA_GROUP * ATTN_BLOCK, 128))

    gate1, up1, down1 = FFN1[:1], FFN1[1:2], FFN1[2:]
    placed = {n: _cast_place(wts[n], place, "cast_" + n) for n in gate1}
    gate1_ici = _split_start("gather_gate1_ici", [placed[n] for n in gate1], _gather_ici_plan, 3 * len(gate1), place)
    placed.update({n: _cast_place(wts[n], place, "cast_" + n) for n in up1 + down1 + MIXER + FFN2})
    zero = jnp.zeros((1,), F32)
    packed = [_pack_small([d[n] for n in SMALL] + [zero]) for d in (wts, mom, var)]
    band = _band_bias()
    shadow = [placed[n] for n in MIXER + FFN2] + packed + [sink_rows, pw, band]
    up1_ici = _split_start("gather_up1_ici", [placed[n] for n in up1], _gather_ici_plan, 3 * len(up1), gate1_ici.token)
    down1_ici = _split_start("gather_down1_ici", [placed[n] for n in down1], _gather_ici_plan, 3 * len(down1), up1_ici.token)
    gate1_d2d = _split_start("gather_gate1_d2d", gate1_ici.bufs, _gather_d2d_plan, 3 * len(gate1), shadow + [down1_ici.token],
                             waits=gate1_ici)
    w = dict(zip(gate1, _split_wait("gather_gate1_wait", gate1_d2d, gate1_d2d.token)))
    mix_ici = _split_start("gather_mix_ici", [placed[n] for n in MIXER], _gather_ici_plan, 3 * len(MIXER), w[gate1[0]])
    ffn2_ici = _split_start("gather_ffn2_ici", [placed[n] for n in FFN2], _gather_ici_plan, 3 * len(FFN2), mix_ici.token)

    n1, a1 = _ffn_gate(xs, _tie(g1, ffn2_ici.token), w["ffn1_w_gate"], "ffn1_gate")
    up1_d2d = _split_start("gather_up1_d2d", up1_ici.bufs, _gather_d2d_plan, 3 * len(up1), a1, waits=up1_ici)
    w.update(zip(up1, _split_wait("gather_up1_wait", up1_d2d, up1_d2d.token)))
    b1, s1 = _ffn_hidden(n1, a1, w["ffn1_w_up"], "ffn1_hidden")
    down1_d2d = _split_start("gather_down1_d2d", down1_ici.bufs, _gather_d2d_plan, 3 * len(down1), s1, waits=down1_ici)
    w.update(zip(down1, _split_wait("gather_down1_wait", down1_d2d, down1_d2d.token)))
    h1 = _ffn_down(xs, s1, w["ffn1_w_down"], "ffn1_down")
    mix_d2d = _split_start("gather_mix_d2d", mix_ici.bufs, _gather_d2d_plan, 3 * len(MIXER), h1, waits=mix_ici)
    w.update(zip(MIXER, _split_wait("gather_mix_wait", mix_d2d, mix_d2d.token)))
    u, xp, q, kv, gl = _mix_proj(h1, gm, w["w_in"])
    attn = _attn_fwd(q, kv, qw, kw, sink_rows, band)
    ffn2_d2d = _split_start("gather_ffn2_d2d", ffn2_ici.bufs, _gather_d2d_plan, 3 * len(FFN2), attn, waits=ffn2_ici)
    h2, bp, ba = _mix_out(xp, attn, gl, _tie(bias, ffn2_d2d.token), pw, pscale, w["w_pool_out"], w["w_attn_out"], w["w_out"], h1)
    w.update(zip(FFN2, _split_wait("gather_ffn2_wait", ffn2_d2d, h2)))
    n2, a2, b2, s2 = _ffn_up(h2, g2, w["ffn2_w_gate"], w["ffn2_w_up"], "ffn2_up")
    dy, dyh, loss = _ffn_down_loss(h2, s2, w["ffn2_w_down"], target, "ffn2_down_loss")

    gw, gs = {}, {}
    da, db = _ffn_bwd_h(dyh, a2, b2, w["ffn2_w_down"], ffn2_d2d.token, "ffn2_bwd_h")
    gw["ffn2_w_gate"], gw["ffn2_w_up"] = _xty(da, n2, "ffn2_dw_gate"), _xty(db, n2, "ffn2_dw_up")
    gw["ffn2_w_down"] = _xty(s2, dyh, "ffn2_dw_down")
    red2 = Reduction("reduce_ffn2", FFN2, [gw[n] for n in FFN2], place)
    token = red2.start_swap(da)
    dh2, gs["ffn2_norm"] = _ffn_bwd_x(da, db, w["ffn2_w_gate"], w["ffn2_w_up"], h2, dy, _tie(g2, token), "ffn2_bwd_x")
    token = red2.start_exchange(dh2)
    dgl, dbp, dba, gw["w_out"], gs["gate_bias"] = _mix_bwd_gate(dh2, w["w_out"], bp, ba, gl, _tie(bias, token))
    dattn, dpooled, gw["w_attn_out"], gw["w_pool_out"], gs["pool_w"], gs["pool_scale"] = _mix_bwd_branch(
        dbp, dba, attn, xp, pw, pscale, w["w_pool_out"], w["w_attn_out"])
    dq, dkv, gs["q_norm"], gs["k_norm"], dsk = _attn_bwd(q, kv, dattn, qw, kw, sink_rows, band)
    gs["sinks"] = dsk[:, :, 0]
    red2.start_share(dq)
    pieces = (_pool_bwd(dpooled), dq, dkv, dgl)
    gw["w_in"] = _in_bwd_w(pieces, u)
    grads = red2.finish(gw["w_in"])
    redm = Reduction("reduce_mix", MIXER, [gw[n] for n in MIXER], place)
    token = redm.start_swap(grads[FFN2[0]])
    dh1, dh1h, gs["mix_norm"] = _in_bwd_x(pieces, w["w_in"], h1, dh2, _tie(gm, token))
    token = redm.start_exchange(dh1)
    da, db = _ffn_bwd_h(dh1h, a1, b1, w["ffn1_w_down"], token, "ffn1_bwd_h")
    gw["ffn1_w_gate"], gw["ffn1_w_up"] = _xty(da, n1, "ffn1_dw_gate"), _xty(db, n1, "ffn1_dw_up")
    gw["ffn1_w_down"] = _xty(s1, dh1h, "ffn1_dw_down")
    token = redm.start_share([gw[n] for n in FFN1])
    red1 = Reduction("reduce_ffn1", FFN1, [gw[n] for n in FFN1], place)
    token = red1.start_swap(token)
    grad_x, gs["ffn1_norm"] = _ffn_bwd_x(da, db, w["ffn1_w_gate"], w["ffn1_w_up"], xs, dh1, _tie(g1, token), "ffn1_bwd_x")
    grads.update(redm.finish(grad_x))

    small_parts = [gs[n] for n in SMALL] + [loss[0, 0].reshape(1)]
    slots = _split_start("gather_small", [_slot_place(_pack_small(small_parts), place)], _slots_plan, N_DEVICES - 1,
                         grads[MIXER[0]])
    token = red1.start_exchange(slots.token)
    delta, new_m, new_v = {}, {}, {}
    for n in FFN2 + MIXER:
        delta[n], new_m[n], new_v[n], grads[n] = _adamw(wts[n], grads[n], mom[n], var[n], token, "adamw_" + n)
    summed = _sum_slots(_split_wait("gather_small_wait", slots, delta[MIXER[-1]])[0])
    small_grads, loss_sum = _unpack_small(summed, [wts[n] for n in SMALL])
    grads.update(dict(zip(SMALL, small_grads)))
    ds, ms, vs, _ = _adamw(packed[0], summed, packed[1], packed[2], token, "adamw_small")
    like = [wts[n] for n in SMALL]
    for out, packed_out in ((delta, ds), (new_m, ms), (new_v, vs)):
        out.update(dict(zip(SMALL, _unpack_small(packed_out, like)[0])))
    results = {(k, n): _row_form(n, d[n]) for k, d in enumerate((grads, delta, new_m, new_v)) for n in SMALL + FFN2 + MIXER}
    token = red1.start_share([results[k, n] for k in range(4) for n in SMALL + ("w_pool_out",)])
    grads.update(red1.finish(token))
    for n in FFN1:
        delta[n], new_m[n], new_v[n], grads[n] = _adamw(wts[n], grads[n], mom[n], var[n], token, "adamw_" + n)
    results.update({(k, n): _row_form(n, d[n]) for k, d in enumerate((grads, delta, new_m, new_v)) for n in FFN1})
    return (loss_sum, grad_x[None], *[results[k, n] for k in range(4) for n in WEIGHTS])
```

```python
from typing import Callable, NamedTuple, Optional

import jax
import jax.numpy as jnp
from jax import lax
from jax.experimental import pallas as pl
from jax.experimental.pallas import tpu as pltpu

F32 = jnp.float32
BF16 = jnp.bfloat16
RMS_EPS = 1e-6
POOL_WINDOWS = (2, 4, 8, 16)
POOL_GROUP = 128
POOL_HALO = 16
HEAD_DIM = 64
GQA_GROUP = 8
N_KV_HEADS = 2
ATTN_BLOCK = 128
SCALE = HEAD_DIM ** -0.5
NEG = -1e30
N_SHARDS = 4
ADAM_LR, ADAM_B1, ADAM_B2, ADAM_EPS, ADAM_WD, ADAM_STEP = 0.001, 0.9, 0.999, 1e-08, 0.01, 10
VMEM_LIMIT = 56 * 1024 * 1024
MESH = pl.DeviceIdType.MESH
SEG_POOL, SEG_Q, SEG_KV, SEG_GATE = (0, 512), (512, 1024), (1536, 256), (1792, 2048)
SEGMENTS = (SEG_POOL, SEG_Q, SEG_KV, SEG_GATE)


def _params(**kw):
    return pltpu.CompilerParams(vmem_limit_bytes=VMEM_LIMIT, **kw)


def _dot(a, b):
    return jnp.dot(a, b, preferred_element_type=F32)


def _dot_nt(a, b):
    return lax.dot_general(a, b, (((1,), (1,)), ((), ())), preferred_element_type=F32)


def _dot_tn(a, b):
    return lax.dot_general(a, b, (((0,), (0,)), ((), ())), preferred_element_type=F32)


def _rinv(x):
    return lax.rsqrt(jnp.mean(x * x, axis=-1, keepdims=True) + RMS_EPS)


def _rms_bwd(dn, x, g):
    r = _rinv(x)
    xr = x * r
    z = dn * g
    dx = r * (z - xr * jnp.mean(z * xr, axis=-1, keepdims=True))
    return dx, dn * xr


def _acc(ref, val, first):
    @pl.when(first)
    def _():
        ref[...] = val

    @pl.when(jnp.logical_not(first))
    def _():
        ref[...] += val


TOKEN_SPEC = pl.BlockSpec(memory_space=pl.ANY)
F_HALF = 1408


def _resident(w):
    return pl.BlockSpec(w.shape, lambda i: (0, 0), pipeline_mode=pl.Buffered(1))


def _ffn_up(h, gnorm, wgt, wut, name, tm=512):
    S, D = h.shape
    F = wgt.shape[0]

    def body(h_ref, g_ref, wg_ref, wu_ref, n_ref, a_ref, b_ref, s_ref):
        x = h_ref[...]
        n = (x * _rinv(x) * g_ref[...]).astype(BF16)
        n_ref[...] = n
        for lo in range(0, F, F_HALF):
            cols = slice(lo, lo + F_HALF)
            a = _dot_nt(n, wg_ref[cols, :])
            b = _dot_nt(n, wu_ref[cols, :])
            a_ref[:, cols] = a.astype(BF16)
            b_ref[:, cols] = b.astype(BF16)
            s_ref[:, cols] = (a * jax.nn.sigmoid(a) * b).astype(BF16)

    act = pl.BlockSpec((tm, F), lambda i: (i, 0))
    hidden = jax.ShapeDtypeStruct((S, F), BF16)
    return pl.pallas_call(
        body, name=name, grid=(S // tm,),
        in_specs=[pl.BlockSpec((tm, D), lambda i: (i, 0)), pl.BlockSpec((1, D), lambda i: (0, 0)), _resident(wgt), _resident(wut)],
        out_specs=[pl.BlockSpec((tm, D), lambda i: (i, 0)), act, act, act],
        out_shape=[jax.ShapeDtypeStruct((S, D), BF16), hidden, hidden, hidden],
        compiler_params=_params(dimension_semantics=("arbitrary",)),
    )(h, gnorm, wgt, wut)


def _ffn_gate(h, gnorm, wgt, name, tm=512):
    S, D = h.shape
    F = wgt.shape[0]

    def body(h_ref, g_ref, wg_ref, n_ref, a_ref):
        x = h_ref[...]
        n = (x * _rinv(x) * g_ref[...]).astype(BF16)
        n_ref[...] = n
        for lo in range(0, F, F_HALF):
            a_ref[:, lo:lo + F_HALF] = _dot_nt(n, wg_ref[lo:lo + F_HALF, :]).astype(BF16)

    row = pl.BlockSpec((tm, D), lambda i: (i, 0))
    return pl.pallas_call(
        body, name=name, grid=(S // tm,), in_specs=[row, pl.BlockSpec((1, D), lambda i: (0, 0)), _resident(wgt)],
        out_specs=[row, pl.BlockSpec((tm, F), lambda i: (i, 0))],
        out_shape=[jax.ShapeDtypeStruct((S, D), BF16), jax.ShapeDtypeStruct((S, F), BF16)],
        compiler_params=_params(dimension_semantics=("arbitrary",)),
    )(h, gnorm, wgt)


def _ffn_hidden(n, a, wut, name, tm=512):
    S, D = n.shape
    F = wut.shape[0]

    def body(n_ref, a_ref, wu_ref, b_ref, s_ref):
        n_t = n_ref[...]
        for lo in range(0, F, F_HALF):
            cols = slice(lo, lo + F_HALF)
            a = a_ref[:, cols].astype(F32)
            b = _dot_nt(n_t, wu_ref[cols, :])
            b_ref[:, cols] = b.astype(BF16)
            s_ref[:, cols] = (a * jax.nn.sigmoid(a) * b).astype(BF16)

    act = pl.BlockSpec((tm, F), lambda i: (i, 0))
    hidden = jax.ShapeDtypeStruct((S, F), BF16)
    return pl.pallas_call(
        body, name=name, grid=(S // tm,), in_specs=[pl.BlockSpec((tm, D), lambda i: (i, 0)), act, _resident(wut)],
        out_specs=[act, act], out_shape=[hidden, hidden],
        compiler_params=_params(dimension_semantics=("arbitrary",)),
    )(n, a, wut)


def _ffn_down(h, s, wd, name, tm=1024):
    S, D = h.shape
    F = wd.shape[0]

    def body(h_ref, s_ref, wd_ref, o_ref):
        o_ref[...] = h_ref[...] + 0.5 * _dot(s_ref[...], wd_ref[...])

    row = pl.BlockSpec((tm, D), lambda i: (i, 0))
    return pl.pallas_call(
        body, name=name, grid=(S // tm,), in_specs=[row, pl.BlockSpec((tm, F), lambda i: (i, 0)), _resident(wd)],
        out_specs=row, out_shape=jax.ShapeDtypeStruct((S, D), F32),
        compiler_params=_params(dimension_semantics=("arbitrary",)),
    )(h, s, wd)


def _ffn_down_loss(h, s, wd, target, name, tm=512):
    S, D = h.shape
    F = wd.shape[0]

    def body(h_ref, s_ref, wd_ref, t_ref, dy_ref, dyh_ref, loss_ref):
        e = h_ref[...] + 0.5 * _dot(s_ref[...], wd_ref[...]) - t_ref[...]
        dy = e * (1.0 / D)
        dy_ref[...] = dy
        dyh_ref[...] = (0.5 * dy).astype(BF16)
        tot = jnp.sum(jnp.sum(e * e, axis=1, keepdims=True), axis=0, keepdims=True) * (0.5 / D)
        _acc(loss_ref, jnp.broadcast_to(tot, loss_ref.shape), pl.program_id(0) == 0)

    row = pl.BlockSpec((tm, D), lambda i: (i, 0))
    return pl.pallas_call(
        body, name=name, grid=(S // tm,), in_specs=[row, pl.BlockSpec((tm, F), lambda i: (i, 0)), _resident(wd), row],
        out_specs=[row, row, pl.BlockSpec((8, 128), lambda i: (0, 0))],
        out_shape=[jax.ShapeDtypeStruct((S, D), F32), jax.ShapeDtypeStruct((S, D), BF16), jax.ShapeDtypeStruct((8, 128), F32)],
        compiler_params=_params(dimension_semantics=("arbitrary",)),
    )(h, s, wd, target)


def _ffn_bwd_h(dyh, a, b, wd, token, name, tm=512):
    S, D = dyh.shape
    F = wd.shape[0]

    def body(dyh_ref, a_ref, b_ref, wd_ref, _, da_ref, db_ref):
        dyh_t = dyh_ref[...]
        for lo in range(0, F, F_HALF):
            cols = slice(lo, lo + F_HALF)
            av = a_ref[:, cols].astype(F32)
            bv = b_ref[:, cols].astype(F32)
            ds = _dot_nt(dyh_t, wd_ref[cols, :])
            sig = jax.nn.sigmoid(av)
            silu = av * sig
            da_ref[:, cols] = (ds * bv * (sig * (1.0 + av * (1.0 - sig)))).astype(BF16)
            db_ref[:, cols] = (ds * silu).astype(BF16)

    act = pl.BlockSpec((tm, F), lambda i: (i, 0))
    hidden = jax.ShapeDtypeStruct((S, F), BF16)
    return pl.pallas_call(
        body, name=name, grid=(S // tm,),
        in_specs=[pl.BlockSpec((tm, D), lambda i: (i, 0)), act, act, _resident(wd), TOKEN_SPEC],
        out_specs=[act, act], out_shape=[hidden, hidden],
        compiler_params=_params(dimension_semantics=("arbitrary",)),
    )(dyh, a, b, wd, token)


def _xty(x, y, name, tk=2048, tf=1408):
    S, F = x.shape
    D = y.shape[1]

    def body(x_ref, y_ref, o_ref):
        _acc(o_ref, _dot_tn(x_ref[...], y_ref[...]), pl.program_id(1) == 0)

    return pl.pallas_call(
        body, name=name, grid=(F // tf, S // tk),
        in_specs=[pl.BlockSpec((tk, tf), lambda j, k: (k, j)), pl.BlockSpec((tk, D), lambda j, k: (k, 0))],
        out_specs=pl.BlockSpec((tf, D), lambda j, k: (j, 0)), out_shape=jax.ShapeDtypeStruct((F, D), F32),
        compiler_params=_params(dimension_semantics=("arbitrary", "arbitrary")),
    )(x, y)


def _ffn_bwd_x(da, db, wgt, wut, x, dh, gnorm, name, tm=512):
    S, D = x.shape
    F = wgt.shape[0]

    def body(da_ref, db_ref, wg_ref, wu_ref, x_ref, dh_ref, g_ref, dx_ref, dg_ref):
        dn = _dot(da_ref[...], wg_ref[...]) + _dot(db_ref[...], wu_ref[...])
        dx, dg_rows = _rms_bwd(dn, x_ref[...], g_ref[...])
        dx_ref[...] = dh_ref[...] + dx
        _acc(dg_ref, jnp.sum(dg_rows, axis=0, keepdims=True), pl.program_id(0) == 0)

    row = pl.BlockSpec((tm, D), lambda i: (i, 0))
    act = pl.BlockSpec((tm, F), lambda i: (i, 0))
    vec = pl.BlockSpec((1, D), lambda i: (0, 0))
    return pl.pallas_call(
        body, name=name, grid=(S // tm,), in_specs=[act, act, _resident(wgt), _resident(wut), row, row, vec],
        out_specs=[row, vec],
        out_shape=[jax.ShapeDtypeStruct((S, D), F32), jax.ShapeDtypeStruct((1, D), F32)],
        compiler_params=_params(dimension_semantics=("arbitrary",)),
    )(da, db, wgt, wut, x, dh, gnorm)


def _mix_proj(h, gnorm, wint, tm=512):
    S, D = h.shape
    nt = S // tm

    def body(h_ref, g_ref, w_ref, u_ref, xp_ref, q_ref, kv_ref, gl_ref):
        x = h_ref[...]
        u = (x * _rinv(x) * g_ref[...]).astype(BF16)
        u_ref[...] = u
        for (off, width), ref in zip(SEGMENTS, (xp_ref, q_ref, kv_ref, gl_ref)):
            ref[...] = _dot_nt(u, w_ref[off:off + width, :]).astype(ref.dtype)

    def row(width):
        return pl.BlockSpec((tm, width), lambda i: (i, 0))

    return pl.pallas_call(
        body, name="mix_proj", grid=(nt,),
        in_specs=[row(D), pl.BlockSpec((1, D), lambda i: (0, 0)), pl.BlockSpec(wint.shape, lambda i: (0, 0))],
        out_specs=[row(D), row(SEG_POOL[1]), row(SEG_Q[1]), row(SEG_KV[1]), row(SEG_GATE[1])],
        out_shape=[jax.ShapeDtypeStruct((S, D), BF16), jax.ShapeDtypeStruct((S, SEG_POOL[1]), F32),
                   jax.ShapeDtypeStruct((S, SEG_Q[1]), BF16), jax.ShapeDtypeStruct((S, SEG_KV[1]), BF16),
                   jax.ShapeDtypeStruct((S, SEG_GATE[1]), BF16)],
        compiler_params=_params(dimension_semantics=("arbitrary",)),
    )(h, gnorm, wint)


def _stack_heads(x, g):
    return jnp.concatenate([x[:, (GQA_GROUP * g + hh) * HEAD_DIM:(GQA_GROUP * g + hh + 1) * HEAD_DIM]
                            for hh in range(GQA_GROUP)], axis=0)


def _unstack_heads(ref, val, g):
    for hh in range(GQA_GROUP):
        lo = (GQA_GROUP * g + hh) * HEAD_DIM
        ref[:, lo:lo + HEAD_DIM] = val[hh * ATTN_BLOCK:(hh + 1) * ATTN_BLOCK, :]


def _rowsum(xb, width):
    return _dot(xb, jnp.ones((xb.shape[1], width), BF16))


def _rinv_lanes(x):
    return lax.rsqrt(_rowsum((x * x).astype(BF16), x.shape[1]) * (1.0 / x.shape[1]) + RMS_EPS)


def _twice(x):
    return jnp.concatenate([x, x], axis=1)


def _band_bias():
    qi = (jnp.arange(GQA_GROUP * ATTN_BLOCK) % ATTN_BLOCK)[:, None]
    kj = jnp.arange(2 * ATTN_BLOCK)[None, :]
    return jnp.where(jnp.logical_and(kj > qi, kj <= qi + ATTN_BLOCK), 0.0, NEG).astype(F32)


def _attn_exp(qn, kk, sink, band, n):
    kj = lax.broadcasted_iota(jnp.int32, (1, 2 * ATTN_BLOCK), 1)
    before_sequence = jnp.where(jnp.logical_and(kj < ATTN_BLOCK, n == 0), NEG, 0.0)
    s = _dot_nt(qn, kk) + band + before_sequence
    m = jnp.maximum(jnp.broadcast_to(jnp.max(s, axis=-1, keepdims=True), sink.shape), sink)
    e = jnp.exp(s - _twice(m))
    e_s = jnp.exp(sink - m)
    e_b = e.astype(BF16)
    inv = 1.0 / (_rowsum(e_b, ATTN_BLOCK) + e_s)
    return e_b, e, e_s, inv


def _attn_blocks(n):
    cur = pl.multiple_of(n * ATTN_BLOCK, ATTN_BLOCK)
    prev = pl.multiple_of(jnp.maximum(n - 1, 0) * ATTN_BLOCK, ATTN_BLOCK)
    return cur, prev


def _kv_window(kv_ref, n):
    cur, prev = _attn_blocks(n)
    return jnp.concatenate([kv_ref[pl.ds(prev, ATTN_BLOCK), :], kv_ref[pl.ds(cur, ATTN_BLOCK), :]], axis=0).astype(F32)


def _kv_split(kv, g):
    k = kv[:, g * HEAD_DIM:(g + 1) * HEAD_DIM]
    v = kv[:, (N_KV_HEADS + g) * HEAD_DIM:(N_KV_HEADS + g + 1) * HEAD_DIM]
    return k, v


def _attn_fwd(q, kv, qw, kw, sink_rows, band):
    S, W = q.shape
    nb = S // ATTN_BLOCK

    def body(q_ref, kv_ref, qw_ref, kw_ref, sk_ref, band_ref, o_ref, o_scr):
        n = pl.program_id(0)
        qf = q_ref[...].astype(F32)
        kvw = _kv_window(kv_ref, n)
        for g in range(N_KV_HEADS):
            qs = _stack_heads(qf, g)
            qn = (qs * _rinv_lanes(qs) * qw_ref[...] * SCALE).astype(BF16)
            k, v = _kv_split(kvw, g)
            kk = (k * _rinv_lanes(k) * kw_ref[...]).astype(BF16)
            e_b, _, _, inv = _attn_exp(qn, kk, sk_ref[g], band_ref[...], n)
            _unstack_heads(o_scr, _dot(e_b, v.astype(BF16)) * inv[:, :HEAD_DIM], g)
        o_ref[...] = o_scr[...].astype(BF16)

    blk = pl.BlockSpec((ATTN_BLOCK, W), lambda n: (n, 0))
    return pl.pallas_call(
        body, name="attn_fwd", grid=(nb,),
        in_specs=[blk, pl.BlockSpec(kv.shape, lambda n: (0, 0)), pl.BlockSpec((1, HEAD_DIM), lambda n: (0, 0)),
                  pl.BlockSpec((1, HEAD_DIM), lambda n: (0, 0)), pl.BlockSpec(sink_rows.shape, lambda n: (0, 0, 0)),
                  pl.BlockSpec(band.shape, lambda n: (0, 0))],
        out_specs=blk, out_shape=jax.ShapeDtypeStruct((S, W), BF16),
        scratch_shapes=[pltpu.VMEM((ATTN_BLOCK, W), F32)],
        compiler_params=_params(dimension_semantics=("arbitrary",)),
    )(q, kv, qw, kw, sink_rows, band)


def _attn_bwd(q, kv, do, qw, kw, sink_rows, band):
    S, W = q.shape
    KW = kv.shape[1]
    nb = S // ATTN_BLOCK
    chunk = 512

    def body(q_ref, kv_ref, do_ref, qw_ref, kw_ref, sk_ref, band_ref, dq_ref, dkv_ref, dqw_ref, dkw_ref, dsk_ref, dq_scr):
        n = pl.program_id(0)

        @pl.when(n == 0)
        def _():
            dkv_ref[...] = jnp.zeros_like(dkv_ref)
            dqw_ref[...] = jnp.zeros_like(dqw_ref)
            dsk_ref[...] = jnp.zeros_like(dsk_ref)

        qf = q_ref[...].astype(F32)
        dof = do_ref[...].astype(F32)
        cur, prev = _attn_blocks(n)
        kvw = _kv_window(kv_ref, n)
        qw_v = qw_ref[...]
        for g in range(N_KV_HEADS):
            qs = _stack_heads(qf, g)
            rq = _rinv_lanes(qs)
            qhat = qs * rq
            qn = (qhat * qw_v * SCALE).astype(BF16)
            k, v = _kv_split(kvw, g)
            kk = (k * _rinv_lanes(k) * kw_ref[...]).astype(BF16)
            vv = v.astype(BF16)
            dos = _stack_heads(dof, g).astype(BF16)
            _, e, e_s, inv = _attn_exp(qn, kk, sk_ref[g], band_ref[...], n)
            p = e * _twice(inv)
            dp = _dot_nt(dos, vv)
            drow = _rowsum((p * dp).astype(BF16), ATTN_BLOCK)
            ds = (p * (dp - _twice(drow))).astype(BF16)
            dsink = -(e_s * inv * drow)
            for hh in range(GQA_GROUP):
                tot = jnp.sum(dsink[hh * ATTN_BLOCK:(hh + 1) * ATTN_BLOCK, :], axis=0, keepdims=True)
                dsk_ref[g, hh:hh + 1, :] += tot
            dqn = _dot(ds, kk) * SCALE
            dkk = _dot_tn(ds, qn)
            dvv = _dot_tn(p.astype(BF16), dos)
            klo, vlo = g * HEAD_DIM, (N_KV_HEADS + g) * HEAD_DIM
            for start, rows in ((prev, slice(0, ATTN_BLOCK)), (cur, slice(ATTN_BLOCK, 2 * ATTN_BLOCK))):
                dkv_ref[pl.ds(start, ATTN_BLOCK), klo:klo + HEAD_DIM] += dkk[rows]
                dkv_ref[pl.ds(start, ATTN_BLOCK), vlo:vlo + HEAD_DIM] += dvv[rows]
            dqw_ref[...] += jnp.sum(dqn * qhat, axis=0, keepdims=True)
            z = dqn * qw_v
            dqs = rq * (z - qhat * (_rowsum((z * qhat).astype(BF16), HEAD_DIM) * (1.0 / HEAD_DIM)))
            _unstack_heads(dq_scr, dqs, g)
        dq_ref[...] = dq_scr[...].astype(BF16)

        @pl.when(n == nb - 1)
        def _():
            def one(c, dkw):
                rows = pl.ds(pl.multiple_of(c * chunk, chunk), chunk)
                for g in range(N_KV_HEADS):
                    lo = g * HEAD_DIM
                    k = kv_ref[rows, lo:lo + HEAD_DIM].astype(F32)
                    dx, dg_rows = _rms_bwd(dkv_ref[rows, lo:lo + HEAD_DIM], k, kw_ref[...])
                    dkv_ref[rows, lo:lo + HEAD_DIM] = dx
                    dkw = dkw + jnp.sum(dg_rows, axis=0, keepdims=True)
                return dkw

            dkw_ref[...] = lax.fori_loop(0, S // chunk, one, jnp.zeros((1, HEAD_DIM), F32))

    blk = pl.BlockSpec((ATTN_BLOCK, W), lambda n: (n, 0))
    whole_kv = pl.BlockSpec((S, KW), lambda n: (0, 0))
    vec = pl.BlockSpec((1, HEAD_DIM), lambda n: (0, 0))
    sk = pl.BlockSpec(sink_rows.shape, lambda n: (0, 0, 0))
    dsk = pl.BlockSpec((N_KV_HEADS, GQA_GROUP, 128), lambda n: (0, 0, 0))
    return pl.pallas_call(
        body, name="attn_bwd", grid=(nb,), in_specs=[blk, whole_kv, blk, vec, vec, sk, pl.BlockSpec(band.shape, lambda n: (0, 0))],
        out_specs=[blk, whole_kv, vec, vec, dsk],
        out_shape=[jax.ShapeDtypeStruct((S, W), BF16), jax.ShapeDtypeStruct((S, KW), F32),
                   jax.ShapeDtypeStruct((1, HEAD_DIM), F32), jax.ShapeDtypeStruct((1, HEAD_DIM), F32),
                   jax.ShapeDtypeStruct((N_KV_HEADS, GQA_GROUP, 128), F32)],
        scratch_shapes=[pltpu.VMEM((ATTN_BLOCK, W), F32)],
        compiler_params=_params(dimension_semantics=("arbitrary",)),
    )(q, kv, do, qw, kw, sink_rows, band)


def _pooled(xc, xprev, i):
    tm = xc.shape[0]
    xh = jnp.concatenate([jnp.where(i > 0, xprev, 0.0), xc], axis=0)
    t = lax.broadcasted_iota(jnp.int32, (tm, 1), 0) + i * tm
    out = []
    for gi, w in enumerate(POOL_WINDOWS):
        acc = xh[:, gi * POOL_GROUP:(gi + 1) * POOL_GROUP]
        sh = 1
        while sh < w:
            acc = acc + pltpu.roll(acc, sh, 0)
            sh *= 2
        cnt = jnp.minimum(t + 1, w).astype(F32)
        out.append(acc[POOL_HALO:, :] / cnt - xc[:, gi * POOL_GROUP:(gi + 1) * POOL_GROUP])
    return jnp.concatenate(out, axis=1)


def _pool_mix(pooled_b, pw_ref):
    return jnp.concatenate([_dot(pooled_b[:, gi * POOL_GROUP:(gi + 1) * POOL_GROUP], pw_ref[gi])
                            for gi in range(len(POOL_WINDOWS))], axis=1)


def _halo_specs(tm, width, S, after):
    per = tm // POOL_HALO
    last = S // POOL_HALO - 1
    if after:
        return pl.BlockSpec((POOL_HALO, width), lambda i: (jnp.minimum((i + 1) * per, last), 0))
    return pl.BlockSpec((POOL_HALO, width), lambda i: (jnp.maximum(i * per - 1, 0), 0))


def _mix_out(xp, attn, gl, bias, pw, pscale, wpot, wao, wo, h, tm=512):
    S, D = h.shape
    nt = S // tm
    PW = xp.shape[1]

    def body(xc_ref, xprev_ref, at_ref, gl_ref, bias_ref, pw_ref, ps_ref, wpo_ref, wao_ref, wo_ref, h_ref,
             ho_ref, bp_ref, ba_ref):
        i = pl.program_id(0)
        pooled = _pooled(xc_ref[...], xprev_ref[...], i).astype(BF16)
        ms = (_pool_mix(pooled, pw_ref) * ps_ref[...]).astype(BF16)
        bp = _dot_nt(ms, wpo_ref[...])
        ba = _dot(at_ref[...], wao_ref[...])
        bp_ref[...] = bp.astype(BF16)
        ba_ref[...] = ba.astype(BF16)
        gates = jax.nn.sigmoid(gl_ref[...].astype(F32) + bias_ref[...])
        merged = (gates[:, :D] * bp + gates[:, D:] * ba).astype(BF16)
        ho_ref[...] = h_ref[...] + _dot(merged, wo_ref[...])

    def row(width):
        return pl.BlockSpec((tm, width), lambda i: (i, 0))

    def whole(x):
        nd = x.ndim
        return pl.BlockSpec(x.shape, lambda i: (0,) * nd)

    return pl.pallas_call(
        body, name="mix_out", grid=(nt,),
        in_specs=[row(PW), _halo_specs(tm, PW, S, False), row(D), row(2 * D), whole(bias), whole(pw), whole(pscale),
                  whole(wpot), whole(wao), whole(wo), row(D)],
        out_specs=[row(D), row(D), row(D)],
        out_shape=[jax.ShapeDtypeStruct((S, D), F32), jax.ShapeDtypeStruct((S, D), BF16),
                   jax.ShapeDtypeStruct((S, D), BF16)],
        compiler_params=_params(dimension_semantics=("arbitrary",)),
    )(xp, xp, attn, gl, bias, pw, pscale, wpot, wao, wo, h)


def _mix_bwd_gate(dh, wo, bp, ba, gl, bias, tm=512):
    S, D = dh.shape
    nt = S // tm

    def body(dh_ref, wo_ref, bp_ref, ba_ref, gl_ref, bias_ref, dgl_ref, dbp_ref, dba_ref, dwo_ref, dbias_ref):
        i = pl.program_id(0)
        dhb = dh_ref[...].astype(BF16)
        dm = _dot_nt(dhb, wo_ref[...])
        gates = jax.nn.sigmoid(gl_ref[...].astype(F32) + bias_ref[...])
        gp, ga = gates[:, :D], gates[:, D:]
        bp_v = bp_ref[...].astype(F32)
        ba_v = ba_ref[...].astype(F32)
        merged = (gp * bp_v + ga * ba_v).astype(BF16)
        _acc(dwo_ref, _dot_tn(merged, dhb), i == 0)
        dbp_ref[...] = (dm * gp).astype(BF16)
        dba_ref[...] = (dm * ga).astype(BF16)
        dgl = jnp.concatenate([dm * bp_v * gp * (1.0 - gp), dm * ba_v * ga * (1.0 - ga)], axis=1)
        dgl_ref[...] = dgl.astype(BF16)
        _acc(dbias_ref, jnp.sum(dgl, axis=0, keepdims=True), i == 0)

    def row(width):
        return pl.BlockSpec((tm, width), lambda i: (i, 0))

    def whole(shape):
        return pl.BlockSpec(shape, lambda i: (0, 0))

    return pl.pallas_call(
        body, name="mix_bwd_gate", grid=(nt,),
        in_specs=[row(D), whole(wo.shape), row(D), row(D), row(2 * D), whole(bias.shape)],
        out_specs=[row(2 * D), row(D), row(D), whole((D, D)), whole((1, 2 * D))],
        out_shape=[jax.ShapeDtypeStruct((S, 2 * D), BF16), jax.ShapeDtypeStruct((S, D), BF16),
                   jax.ShapeDtypeStruct((S, D), BF16), jax.ShapeDtypeStruct((D, D), F32),
                   jax.ShapeDtypeStruct((1, 2 * D), F32)],
        compiler_params=_params(dimension_semantics=("arbitrary",)),
    )(dh, wo, bp, ba, gl, bias)


def _mix_bwd_branch(dbp, dba, attn, xp, pw, pscale, wpot, wao, tm=1024):
    S, D = dbp.shape
    nt = S // tm
    PW = xp.shape[1]
    NG = len(POOL_WINDOWS)

    def body(dbp_ref, dba_ref, at_ref, xc_ref, xprev_ref, pw_ref, ps_ref, wpo_ref, wao_ref,
             dat_ref, dpl_ref, dwao_ref, dwpo_ref, dpw_ref, dps_ref):
        i = pl.program_id(0)
        dba_v = dba_ref[...]
        dbp_v = dbp_ref[...]
        _acc(dwao_ref, _dot_tn(at_ref[...], dba_v), i == 0)
        dat_ref[...] = _dot_nt(dba_v, wao_ref[...]).astype(BF16)
        pooled = _pooled(xc_ref[...], xprev_ref[...], i).astype(BF16)
        mixed = _pool_mix(pooled, pw_ref)
        ps = ps_ref[...]
        _acc(dwpo_ref, _dot_tn(dbp_v, (mixed * ps).astype(BF16)), i == 0)
        dms = _dot(dbp_v, wpo_ref[...])
        _acc(dps_ref, jnp.sum(dms * mixed, axis=0, keepdims=True), i == 0)
        dmixed = (dms * ps).astype(BF16)
        dpooled = []
        for gi in range(NG):
            cols = slice(gi * POOL_GROUP, (gi + 1) * POOL_GROUP)
            _acc(dpw_ref.at[gi], _dot_tn(pooled[:, cols], dmixed[:, cols]), i == 0)
            dpooled.append(_dot_nt(dmixed[:, cols], pw_ref[gi]))
        dpl_ref[...] = jnp.concatenate(dpooled, axis=1)

    def row(width):
        return pl.BlockSpec((tm, width), lambda i: (i, 0))

    def whole(shape):
        nd = len(shape)
        return pl.BlockSpec(shape, lambda i: (0,) * nd)

    return pl.pallas_call(
        body, name="mix_bwd_branch", grid=(nt,),
        in_specs=[row(D), row(D), row(D), row(PW), _halo_specs(tm, PW, S, False), whole(pw.shape), whole(pscale.shape),
                  whole(wpot.shape), whole(wao.shape)],
        out_specs=[row(D), row(PW), whole((D, D)), whole((D, PW)), whole(pw.shape), whole((1, PW))],
        out_shape=[jax.ShapeDtypeStruct((S, D), BF16), jax.ShapeDtypeStruct((S, PW), F32),
                   jax.ShapeDtypeStruct((D, D), F32), jax.ShapeDtypeStruct((D, PW), F32),
                   jax.ShapeDtypeStruct(pw.shape, F32), jax.ShapeDtypeStruct((1, PW), F32)],
        compiler_params=_params(dimension_semantics=("arbitrary",)),
    )(dbp, dba, attn, xp, xp, pw, pscale, wpot, wao)


def _pool_bwd(dpooled, tm=512):
    S, PW = dpooled.shape
    nt = S // tm

    def body(dc_ref, dnext_ref, dxp_ref):
        i = pl.program_id(0)
        dc = dc_ref[...]
        dh = jnp.concatenate([dc, jnp.where(i < nt - 1, dnext_ref[...], 0.0)], axis=0)
        rows = tm + POOL_HALO
        t = lax.broadcasted_iota(jnp.int32, (rows, 1), 0) + i * tm
        out = []
        for gi, w in enumerate(POOL_WINDOWS):
            cols = slice(gi * POOL_GROUP, (gi + 1) * POOL_GROUP)
            acc = dh[:, cols] / jnp.minimum(t + 1, w).astype(F32)
            sh = 1
            while sh < w:
                acc = acc + pltpu.roll(acc, rows - sh, 0)
                sh *= 2
            out.append(acc[:tm, :] - dc[:, cols])
        dxp_ref[...] = jnp.concatenate(out, axis=1).astype(BF16)

    return pl.pallas_call(
        body, name="pool_bwd", grid=(nt,),
        in_specs=[pl.BlockSpec((tm, PW), lambda i: (i, 0)), _halo_specs(tm, PW, S, True)],
        out_specs=pl.BlockSpec((tm, PW), lambda i: (i, 0)), out_shape=jax.ShapeDtypeStruct((S, PW), BF16),
        compiler_params=_params(dimension_semantics=("arbitrary",)),
    )(dpooled, dpooled)


def _in_bwd_w(pieces, u, tm=1024):
    S, D = u.shape
    nt = S // tm
    NW = sum(width for _, width in SEGMENTS)

    def body(*refs):
        piece_refs, u_ref, dw_hbm, acc, sem = refs[:len(SEGMENTS)], refs[len(SEGMENTS)], refs[len(SEGMENTS) + 1], refs[-2], refs[-1]
        i = pl.program_id(0)
        u_t = u_ref[...]
        for (off, width), ref in zip(SEGMENTS, piece_refs):
            for lo in range(0, width, D):
                hi = min(lo + D, width)
                _acc(acc.at[off + lo:off + hi, :], _dot_tn(ref[:, lo:hi].astype(BF16), u_t), i == 0)

        @pl.when(i == nt - 1)
        def _():
            cp = pltpu.make_async_copy(acc, dw_hbm, sem)
            cp.start()
            cp.wait()

    return pl.pallas_call(
        body, name="in_bwd_w", grid=(nt,),
        in_specs=[pl.BlockSpec((tm, width), lambda i: (i, 0)) for _, width in SEGMENTS] + [pl.BlockSpec((tm, D), lambda i: (i, 0))],
        out_specs=pl.BlockSpec(memory_space=pl.ANY), out_shape=jax.ShapeDtypeStruct((NW, D), F32),
        scratch_shapes=[pltpu.VMEM((NW, D), F32), pltpu.SemaphoreType.DMA],
        compiler_params=_params(dimension_semantics=("arbitrary",)),
    )(*pieces, u)


def _in_bwd_x(pieces, wint, h, dh, gnorm, tm=512):
    S, D = h.shape
    nt = S // tm

    def body(*refs):
        piece_refs = refs[:len(SEGMENTS)]
        w_ref, h_ref, dh_ref, g_ref, dx_ref, dxh_ref, dg_ref = refs[len(SEGMENTS):]
        i = pl.program_id(0)
        du = jnp.zeros((tm, D), F32)
        for (off, width), ref in zip(SEGMENTS, piece_refs):
            du = du + _dot(ref[...].astype(BF16), w_ref[off:off + width, :])
        dx, dg_rows = _rms_bwd(du, h_ref[...], g_ref[...])
        out = dh_ref[...] + dx
        dx_ref[...] = out
        dxh_ref[...] = (0.5 * out).astype(BF16)
        _acc(dg_ref, jnp.sum(dg_rows, axis=0, keepdims=True), i == 0)

    def row(width):
        return pl.BlockSpec((tm, width), lambda i: (i, 0))

    vec = pl.BlockSpec((1, D), lambda i: (0, 0))
    return pl.pallas_call(
        body, name="in_bwd_x", grid=(nt,),
        in_specs=[row(width) for _, width in SEGMENTS] + [pl.BlockSpec(wint.shape, lambda i: (0, 0)), row(D), row(D), vec],
        out_specs=[row(D), row(D), vec],
        out_shape=[jax.ShapeDtypeStruct((S, D), F32), jax.ShapeDtypeStruct((S, D), BF16), jax.ShapeDtypeStruct((1, D), F32)],
        compiler_params=_params(dimension_semantics=("arbitrary",)),
    )(*pieces, wint, h, dh, gnorm)


def _row_tile(rows):
    for t in (512, 480, 352, 256, 128, 64, 32, 16, 8):
        if rows % t == 0:
            return t
    return rows


def _adamw(w, g, m, v, token, name):
    R, C = w.shape
    tr = _row_tile(R)

    def body(w_ref, g_ref, m_ref, v_ref, _, d_ref, mo_ref, vo_ref, go_ref):
        gv = g_ref[...]
        go_ref[...] = gv
        mn = ADAM_B1 * m_ref[...] + (1.0 - ADAM_B1) * gv
        vn = ADAM_B2 * v_ref[...] + (1.0 - ADAM_B2) * (gv * gv)
        m_hat = mn / (1.0 - ADAM_B1 ** ADAM_STEP)
        v_hat = vn / (1.0 - ADAM_B2 ** ADAM_STEP)
        d_ref[...] = -ADAM_LR * (m_hat / (jnp.sqrt(v_hat) + ADAM_EPS) + ADAM_WD * w_ref[...])
        mo_ref[...] = mn
        vo_ref[...] = vn

    blk = pl.BlockSpec((tr, C), lambda i: (i, 0))
    sh = jax.ShapeDtypeStruct((R, C), F32)
    return pl.pallas_call(
        body, name=name, grid=(R // tr,), in_specs=[blk] * 4 + [TOKEN_SPEC], out_specs=[blk] * 4, out_shape=[sh] * 4,
        compiler_params=_params(dimension_semantics=("arbitrary",)),
    )(w, g, m, v, token)


def _cast_place(w, place, name):
    R, C = w.shape
    tr = _row_tile(R)
    per = R // tr

    def body(p_ref, w_ref, o_ref):
        o_ref[...] = w_ref[...].astype(BF16)

    grid_spec = pltpu.PrefetchScalarGridSpec(
        num_scalar_prefetch=1, grid=(per,),
        in_specs=[pl.BlockSpec((tr, C), lambda i, p: (i, 0))],
        out_specs=pl.BlockSpec((tr, C), lambda i, p: (p[0] * per + i, 0)))
    return pl.pallas_call(
        body, name=name, grid_spec=grid_spec, out_shape=jax.ShapeDtypeStruct((N_SHARDS * R, C), BF16),
        compiler_params=_params(dimension_semantics=("arbitrary",)),
    )(place, w)


def _parked(w, m, idx, first, last):
    return jnp.where(m == w, idx, jnp.where(m < w, first, last))


def _sum_halves(g4s, recvs, place, name):
    M = len(g4s)
    NS, R, C = g4s[0].shape
    hr = R // 2
    tr = _row_tile(hr)
    per = hr // tr

    def body(p_ref, *refs):
        m = pl.program_id(0)
        for w in range(M):
            @pl.when(m == w)
            def _(w=w):
                refs[2 * M + w][...] = (refs[w][...] + refs[M + w][...]).astype(BF16)

    def spec(w, mine):
        def index(m, s, i, p):
            row = _parked(w, m, i, 0, per - 1)
            return _parked(w, m, s, 0, NS - 1), (p[1] * per + row if mine else row), 0

        return pl.BlockSpec((1, tr, C), index)

    grid_spec = pltpu.PrefetchScalarGridSpec(
        num_scalar_prefetch=1, grid=(M, NS, per),
        in_specs=[spec(w, True) for w in range(M)] + [spec(w, False) for w in range(M)],
        out_specs=[spec(w, False) for w in range(M)])
    return pl.pallas_call(
        body, name=name, grid_spec=grid_spec, out_shape=[jax.ShapeDtypeStruct((NS, hr, C), BF16)] * M,
        compiler_params=_params(dimension_semantics=("arbitrary",) * 3),
    )(place, *g4s, *recvs)


def _sum_quarters(h4s, recv3s, place, name):
    M = len(h4s)
    NS, hr, C = h4s[0].shape
    tr = _row_tile(hr)
    per = hr // tr

    def body(p_ref, *refs):
        m = pl.program_id(0)
        for w in range(M):
            @pl.when(m == w)
            def _(w=w):
                acc = refs[w][0].astype(F32)
                for k in range(N_SHARDS - 1):
                    acc = acc + refs[M + w][k].astype(F32)
                refs[2 * M + w][...] = acc

    def row(w, m, i):
        return _parked(w, m, i, 0, per - 1)

    grid_spec = pltpu.PrefetchScalarGridSpec(
        num_scalar_prefetch=1, grid=(M, per),
        in_specs=[pl.BlockSpec((1, tr, C), lambda m, i, p, w=w: (p[0], row(w, m, i), 0)) for w in range(M)]
        + [pl.BlockSpec((N_SHARDS - 1, tr, C), lambda m, i, p, w=w: (0, row(w, m, i), 0)) for w in range(M)],
        out_specs=[pl.BlockSpec((tr, C), lambda m, i, p, w=w: (p[1] * per + row(w, m, i), 0)) for w in range(M)])
    return pl.pallas_call(
        body, name=name, grid_spec=grid_spec, out_shape=[jax.ShapeDtypeStruct((2 * hr, C), F32)] * M,
        compiler_params=_params(dimension_semantics=("arbitrary",) * 2),
    )(place, *h4s, *recv3s)


def _by_shape(arrays):
    groups = {}
    for k, a in enumerate(arrays):
        groups.setdefault(a.shape, []).append(k)
    return list(groups.values())


def _place():
    x, y, c = lax.axis_index("x"), lax.axis_index("y"), lax.axis_index("c")
    chips = [(1 - x, y), (x, 1 - y), (1 - x, 1 - y)]
    return x, y, c, chips


HBM_SPEC = pl.BlockSpec(memory_space=pltpu.HBM)
SEM_SPEC = pl.BlockSpec(memory_space=pltpu.SEMAPHORE)
DATAFLOW = pltpu.SideEffectType.DATAFLOW_SIDE_EFFECTING


def _hbm(a):
    return pltpu.with_memory_space_constraint(a, pltpu.HBM)


class InFlight(NamedTuple):
    send_sem: jax.Array
    recv_sem: jax.Array
    bufs: list
    plan: Callable
    token: jax.Array


class Leg(NamedTuple):
    bufs: list
    landing: Optional[InFlight]
    plan: Optional[Callable]
    n_copies: int


def _send(bufs, plan, n_copies):
    return Leg([_hbm(b) for b in bufs], None, plan, n_copies)


def _land(flight):
    return Leg(flight.bufs, flight, None, 0)


def _forward(flight, plan, n_copies):
    return Leg(flight.bufs, flight, plan, n_copies)


def _wait_all(plan, refs, send_ref, recv_ref):
    for k, (src, dst, dev) in enumerate(plan(refs)):
        cp = pltpu.make_async_remote_copy(src_ref=src, dst_ref=dst, send_sem=send_ref.at[k], recv_sem=recv_ref.at[k],
                                          device_id=dev, device_id_type=MESH)
        cp.wait_send()
        cp.wait_recv()


def _start_all(plan, refs, send_ref, recv_ref):
    for k, (src, dst, dev) in enumerate(plan(refs)):
        pltpu.make_async_remote_copy(src_ref=src, dst_ref=dst, send_sem=send_ref.at[k], recv_sem=recv_ref.at[k],
                                     device_id=dev, device_id_type=MESH).start()


def _comm(name, legs, after):
    after = list(after) if isinstance(after, (list, tuple)) else [after]
    ins, in_specs, out_shape, out_specs, aliases, first = [], [], [], [], {}, []
    for leg in legs:
        first.append((len(ins), len(out_shape)))
        for b in leg.bufs:
            aliases[len(ins)] = len(out_shape)
            ins.append(b)
            in_specs.append(HBM_SPEC)
            out_shape.append(pltpu.HBM(b.shape, b.dtype))
            out_specs.append(HBM_SPEC)
        if leg.landing is not None:
            ins += [leg.landing.send_sem, leg.landing.recv_sem]
            in_specs += [SEM_SPEC, SEM_SPEC]
        if leg.plan is not None:
            out_shape += [pltpu.SemaphoreType.DMA((leg.n_copies,))] * 2
            out_specs += [SEM_SPEC, SEM_SPEC]
    starts = any(leg.plan is not None for leg in legs)
    if starts:
        out_shape.append(jax.ShapeDtypeStruct((8, 128), F32))
        out_specs.append(pl.BlockSpec(memory_space=pltpu.VMEM))
    n_in = len(ins) + len(after)

    def body(*refs):
        outs = refs[n_in:]
        for leg, (i, o) in zip(legs, first):
            nb = len(leg.bufs)
            bufs = refs[i:i + nb]
            if leg.landing is not None:
                _wait_all(leg.landing.plan, bufs, refs[i + nb], refs[i + nb + 1])
            if leg.plan is not None:
                _start_all(leg.plan, bufs, outs[o + nb], outs[o + nb + 1])
        if starts:
            outs[-1][...] = jnp.zeros_like(outs[-1])

    out = pl.pallas_call(
        body, name=name, in_specs=in_specs + [TOKEN_SPEC] * len(after), out_shape=out_shape, out_specs=out_specs,
        input_output_aliases=aliases, compiler_params=pltpu.CompilerParams(has_side_effects=DATAFLOW),
    )(*ins, *after)
    results = []
    for leg, (_, o) in zip(legs, first):
        nb = len(leg.bufs)
        bufs = list(out[o:o + nb])
        results.append(bufs if leg.plan is None else InFlight(out[o + nb], out[o + nb + 1], bufs, leg.plan, out[-1]))
    return results


def _half_rows(buf, chip, core):
    hr = buf.shape[0] // (2 * N_SHARDS)
    return buf.at[pl.ds(pl.multiple_of((2 * chip + core) * hr, 16), hr)]


def _gather_ici_plan(bufs):
    x, y, c, chips = _place()
    return [(_half_rows(b, 2 * x + y, c), _half_rows(b, 2 * x + y, c), (px, py, c)) for b in bufs for px, py in chips]


def _gather_d2d_plan(bufs):
    x, y, c, chips = _place()
    return [(_half_rows(b, 2 * px + py, c), _half_rows(b, 2 * px + py, c), (x, y, 1 - c)) for b in bufs for px, py in chips]


def _swap_plan(bufs):
    x, y, c, _ = _place()
    n = len(bufs) // 2
    copies = []
    for g, land in zip(bufs[:n], bufs[n:]):
        hr = g.shape[1] // 2
        copies.append((g.at[:, pl.ds(pl.multiple_of((1 - c) * hr, 8), hr)], land, (x, y, 1 - c)))
    return copies


def _exchange_plan(bufs):
    x, y, c, chips = _place()
    n = len(bufs) // 2
    return [(h.at[2 * px + py], land.at[k], (px, py, c))
            for h, land in zip(bufs[:n], bufs[n:]) for k, (px, py) in enumerate(chips)]


def _share_plan(bufs):
    x, y, c, _ = _place()
    copies = []
    for buf in bufs:
        hr = buf.shape[0] // 2
        mine = buf.at[pl.ds(pl.multiple_of(c * hr, 8), hr)]
        copies.append((mine, mine, (x, y, 1 - c)))
    return copies


N_DEVICES = 8


def _slot_place(vec, place):
    R, C = vec.shape

    def body(p_ref, v_ref, o_ref):
        o_ref[0] = v_ref[...]

    grid_spec = pltpu.PrefetchScalarGridSpec(
        num_scalar_prefetch=1, grid=(1,), in_specs=[pl.BlockSpec((R, C), lambda i, p: (0, 0))],
        out_specs=pl.BlockSpec((1, R, C), lambda i, p: (2 * p[0] + p[1], 0, 0)))
    return pl.pallas_call(
        body, name="slot_place", grid_spec=grid_spec, out_shape=jax.ShapeDtypeStruct((N_DEVICES, R, C), F32),
        compiler_params=_params(dimension_semantics=("arbitrary",)),
    )(place, vec)


def _slots_plan(bufs):
    x, y, c, _ = _place()
    mine = bufs[0].at[4 * x + 2 * y + c]
    return [(mine, mine, (x ^ (r >> 2), y ^ ((r >> 1) & 1), c ^ (r & 1))) for r in range(1, N_DEVICES)]


def _sum_slots(slots):
    _, R, C = slots.shape

    def body(s_ref, o_ref):
        acc = s_ref[0]
        for d in range(1, N_DEVICES):
            acc = acc + s_ref[d]
        o_ref[...] = acc

    return pl.pallas_call(
        body, name="sum_slots", in_specs=[pl.BlockSpec(memory_space=pltpu.VMEM)],
        out_specs=pl.BlockSpec(memory_space=pltpu.VMEM), out_shape=jax.ShapeDtypeStruct((R, C), F32),
        compiler_params=_params(),
    )(slots)


SMALL = ("ffn1_norm", "mix_norm", "pool_w", "pool_scale", "q_norm", "k_norm", "sinks", "gate_bias", "ffn2_norm")
SMALL_COLS = 1024
FFN1 = ("ffn1_w_gate", "ffn1_w_up", "ffn1_w_down")
MIXER = ("w_in", "w_pool_out", "w_attn_out", "w_out")
FFN2 = ("ffn2_w_gate", "ffn2_w_up", "ffn2_w_down")
LARGE = FFN1 + MIXER + FFN2
TRANSPOSED = ("ffn1_w_gate", "ffn1_w_up", "w_in", "w_pool_out", "ffn2_w_gate", "ffn2_w_up")
WEIGHTS = ("ffn1_norm", "ffn1_w_gate", "ffn1_w_up", "ffn1_w_down", "mix_norm", "w_in", "pool_w", "pool_scale",
           "w_pool_out", "q_norm", "k_norm", "sinks", "w_attn_out", "gate_bias", "w_out", "ffn2_norm",
           "ffn2_w_gate", "ffn2_w_up", "ffn2_w_down")


def _pack_small(parts):
    flat = jnp.concatenate([p.reshape(-1) for p in parts])
    rows = -(-flat.shape[0] // (8 * SMALL_COLS)) * 8
    return jnp.pad(flat, (0, rows * SMALL_COLS - flat.shape[0])).reshape(rows, SMALL_COLS)


def _unpack_small(packed, like):
    flat = packed.reshape(-1)
    out, off = [], 0
    for p in like:
        out.append(flat[off:off + p.size].reshape(p.shape))
        off += p.size
    return out, flat[off]


def _tie(small, token):
    return small + token[0, 0]


class Reduction:
    def __init__(self, names, grads, place):
        self.names, self.place = names, place
        self.grads = [g.reshape(N_SHARDS, -1, g.shape[-1]) for g in grads]

    def swap(self):
        lands = [lax.empty((g.shape[0], g.shape[1] // 2, g.shape[2]), g.dtype) for g in self.grads]
        return _send(self.grads + lands, _swap_plan, len(self.grads))

    def exchange(self, swapped):
        n = len(self.names)
        halves = self._per_shape(_sum_halves, "sum_halves", swapped[:n], swapped[n:])
        lands = [lax.empty((N_SHARDS - 1,) + h.shape[1:], h.dtype) for h in halves]
        return _send(halves + lands, _exchange_plan, (N_SHARDS - 1) * n)

    def share(self, exchanged):
        n = len(self.names)
        return _send(self._per_shape(_sum_quarters, "sum_quarters", exchanged[:n], exchanged[n:]), _share_plan, n)

    def named(self, shared):
        return dict(zip(self.names, shared))

    def _per_shape(self, add, stage, mine, received):
        out = [None] * len(mine)
        for idx in _by_shape(mine):
            sums = add([mine[k] for k in idx], [received[k] for k in idx], self.place, f"{stage}_{self.names[idx[0]]}")
            for k, v in zip(idx, sums):
                out[k] = v
        return out


def _row_form(name, a):
    return a.T if name in TRANSPOSED else a


def kernel(x, ffn1_norm, ffn1_w_gate, ffn1_w_up, ffn1_w_down, mix_norm, w_in, pool_w, pool_scale, w_pool_out, q_norm, k_norm, sinks, w_attn_out, gate_bias, w_out, ffn2_norm, ffn2_w_gate, ffn2_w_up, ffn2_w_down, loss_target, m_ffn1_norm, m_ffn1_w_gate, m_ffn1_w_up, m_ffn1_w_down, m_mix_norm, m_w_in, m_pool_w, m_pool_scale, m_w_pool_out, m_q_norm, m_k_norm, m_sinks, m_w_attn_out, m_gate_bias, m_w_out, m_ffn2_norm, m_ffn2_w_gate, m_ffn2_w_up, m_ffn2_w_down, v_ffn1_norm, v_ffn1_w_gate, v_ffn1_w_up, v_ffn1_w_down, v_mix_norm, v_w_in, v_pool_w, v_pool_scale, v_w_pool_out, v_q_norm, v_k_norm, v_sinks, v_w_attn_out, v_gate_bias, v_w_out, v_ffn2_norm, v_ffn2_w_gate, v_ffn2_w_up, v_ffn2_w_down):
    args = dict(locals())
    wts = {n: _row_form(n, args[n]) for n in WEIGHTS}
    mom = {n: _row_form(n, args["m_" + n]) for n in WEIGHTS}
    var = {n: _row_form(n, args["v_" + n]) for n in WEIGHTS}
    shard = 2 * lax.axis_index("x") + lax.axis_index("y")
    place = jnp.stack([shard, lax.axis_index("c")]).astype(jnp.int32)

    xs, target = x[0], loss_target[0]
    D = xs.shape[1]
    g1 = wts["ffn1_norm"].reshape(1, D)
    gm = wts["mix_norm"].reshape(1, D)
    g2 = wts["ffn2_norm"].reshape(1, D)
    qw = wts["q_norm"].reshape(1, HEAD_DIM)
    kw = wts["k_norm"].reshape(1, HEAD_DIM)
    bias = wts["gate_bias"].reshape(1, 2 * D)
    pscale = wts["pool_scale"].reshape(1, -1)
    pw = wts["pool_w"].astype(BF16)
    sink_rows = jnp.broadcast_to(jnp.repeat(wts["sinks"], ATTN_BLOCK).reshape(N_KV_HEADS, GQA_GROUP * ATTN_BLOCK, 1),
                                 (N_KV_HEADS, GQA_GROUP * ATTN_BLOCK, 128))

    gate1, up1, down1 = FFN1[:1], FFN1[1:2], FFN1[2:]

    def over_ici(names):
        return _send([placed[n] for n in names], _gather_ici_plan, 3 * len(names))

    def to_sibling(flight):
        return _forward(flight, _gather_d2d_plan, len(flight.bufs) * 3)

    placed = {n: _cast_place(wts[n], place, "cast_" + n) for n in gate1}
    gate1_ici, = _comm("gather_gate1_ici", [over_ici(gate1)], place)
    placed.update({n: _cast_place(wts[n], place, "cast_" + n) for n in up1 + down1})
    up1_ici, down1_ici = _comm("gather_ffn1_ici", [over_ici(up1), over_ici(down1)], gate1_ici.token)
    placed.update({n: _cast_place(wts[n], place, "cast_" + n) for n in MIXER + FFN2})
    zero = jnp.zeros((1,), F32)
    packed = [_pack_small([d[n] for n in SMALL] + [zero]) for d in (wts, mom, var)]
    band = _band_bias()
    shadow = packed + [sink_rows, pw, band, down1_ici.token]
    gate1_d2d, mix_ici, ffn2_ici = _comm("gather_rest_ici", [to_sibling(gate1_ici), over_ici(MIXER), over_ici(FFN2)], shadow)
    w = dict(zip(gate1, _comm("gather_gate1_wait", [_land(gate1_d2d)], gate1_d2d.token)[0]))

    n1, a1 = _ffn_gate(xs, g1, w["ffn1_w_gate"], "ffn1_gate")
    up1_d2d, = _comm("gather_up1_d2d", [to_sibling(up1_ici)], a1)
    down1_d2d, landed = _comm("gather_down1_d2d", [to_sibling(down1_ici), _land(up1_d2d)], up1_d2d.token)
    w.update(zip(up1, landed))
    b1, s1 = _ffn_hidden(n1, a1, w["ffn1_w_up"], "ffn1_hidden")
    mix_d2d, landed = _comm("gather_mix_d2d", [to_sibling(mix_ici), _land(down1_d2d)], s1)
    w.update(zip(down1, landed))
    h1 = _ffn_down(xs, s1, w["ffn1_w_down"], "ffn1_down")
    w.update(zip(MIXER, _comm("gather_mix_wait", [_land(mix_d2d)], h1)[0]))
    u, xp, q, kv, gl = _mix_proj(h1, gm, w["w_in"])
    attn = _attn_fwd(q, kv, qw, kw, sink_rows, band)
    ffn2_d2d, = _comm("gather_ffn2_d2d", [to_sibling(ffn2_ici)], attn)
    h2, bp, ba = _mix_out(xp, attn, gl, _tie(bias, ffn2_d2d.token), pw, pscale, w["w_pool_out"], w["w_attn_out"], w["w_out"], h1)
    w.update(zip(FFN2, _comm("gather_ffn2_wait", [_land(ffn2_d2d)], h2)[0]))
    n2, a2, b2, s2 = _ffn_up(h2, g2, w["ffn2_w_gate"], w["ffn2_w_up"], "ffn2_up")
    dy, dyh, loss = _ffn_down_loss(h2, s2, w["ffn2_w_down"], target, "ffn2_down_loss")

    gw, gs = {}, {}
    da, db = _ffn_bwd_h(dyh, a2, b2, w["ffn2_w_down"], ffn2_d2d.token, "ffn2_bwd_h")
    gw["ffn2_w_gate"], gw["ffn2_w_up"] = _xty(da, n2, "ffn2_dw_gate"), _xty(db, n2, "ffn2_dw_up")
    gw["ffn2_w_down"] = _xty(s2, dyh, "ffn2_dw_down")
    red2 = Reduction(FFN2, [gw[n] for n in FFN2], place)
    swap, = _comm("reduce_ffn2_swap", [red2.swap()], da)
    dh2, gs["ffn2_norm"] = _ffn_bwd_x(da, db, w["ffn2_w_gate"], w["ffn2_w_up"], h2, dy, _tie(g2, swap.token), "ffn2_bwd_x")
    swapped, = _comm("reduce_ffn2_swap_wait", [_land(swap)], dh2)
    exchange, = _comm("reduce_ffn2_exchange", [red2.exchange(swapped)], place)
    dgl, dbp, dba, gw["w_out"], gs["gate_bias"] = _mix_bwd_gate(dh2, w["w_out"], bp, ba, gl, _tie(bias, exchange.token))
    dattn, dpooled, gw["w_attn_out"], gw["w_pool_out"], gs["pool_w"], gs["pool_scale"] = _mix_bwd_branch(
        dbp, dba, attn, xp, pw, pscale, w["w_pool_out"], w["w_attn_out"])
    dq, dkv, gs["q_norm"], gs["k_norm"], dsk = _attn_bwd(q, kv, dattn, qw, kw, sink_rows, band)
    gs["sinks"] = dsk[:, :, 0]
    exchanged, = _comm("reduce_ffn2_exchange_wait", [_land(exchange)], dq)
    share2, = _comm("reduce_ffn2_share", [red2.share(exchanged)], place)
    pieces = (_pool_bwd(dpooled), dq, dkv, dgl)
    gw["w_in"] = _in_bwd_w(pieces, u)
    redm = Reduction(MIXER, [gw[n] for n in MIXER], place)
    swap, shared = _comm("reduce_mix_swap", [redm.swap(), _land(share2)], share2.token)
    grads = red2.named(shared)
    dh1, dh1h, gs["mix_norm"] = _in_bwd_x(pieces, w["w_in"], h1, dh2, _tie(gm, swap.token))
    swapped, = _comm("reduce_mix_swap_wait", [_land(swap)], dh1)
    exchange, = _comm("reduce_mix_exchange", [redm.exchange(swapped)], place)
    da, db = _ffn_bwd_h(dh1h, a1, b1, w["ffn1_w_down"], exchange.token, "ffn1_bwd_h")
    gw["ffn1_w_gate"], gw["ffn1_w_up"] = _xty(da, n1, "ffn1_dw_gate"), _xty(db, n1, "ffn1_dw_up")
    gw["ffn1_w_down"] = _xty(s1, dh1h, "ffn1_dw_down")
    exchanged, = _comm("reduce_mix_exchange_wait", [_land(exchange)], [gw[n] for n in FFN1])
    red1 = Reduction(FFN1, [gw[n] for n in FFN1], place)
    sharem, swap = _comm("reduce_mix_share", [redm.share(exchanged), red1.swap()], place)
    grad_x, gs["ffn1_norm"] = _ffn_bwd_x(da, db, w["ffn1_w_gate"], w["ffn1_w_up"], xs, dh1, _tie(g1, swap.token), "ffn1_bwd_x")

    small_parts = [gs[n] for n in SMALL] + [loss[0, 0].reshape(1)]
    slots, shared, swapped = _comm(
        "gather_small", [_send([_slot_place(_pack_small(small_parts), place)], _slots_plan, N_DEVICES - 1), _land(sharem),
                         _land(swap)], grad_x)
    grads.update(redm.named(shared))
    exchange, = _comm("reduce_ffn1_exchange", [red1.exchange(swapped)], slots.token)
    token = exchange.token
    delta, new_m, new_v = {}, {}, {}
    for n in FFN2 + MIXER:
        delta[n], new_m[n], new_v[n], grads[n] = _adamw(wts[n], grads[n], mom[n], var[n], token, "adamw_" + n)
    summed = _sum_slots(_comm("gather_small_wait", [_land(slots)], delta[MIXER[-1]])[0][0])
    small_grads, loss_sum = _unpack_small(summed, [wts[n] for n in SMALL])
    grads.update(dict(zip(SMALL, small_grads)))
    ds, ms, vs, _ = _adamw(packed[0], summed, packed[1], packed[2], token, "adamw_small")
    like = [wts[n] for n in SMALL]
    for out, packed_out in ((delta, ds), (new_m, ms), (new_v, vs)):
        out.update(dict(zip(SMALL, _unpack_small(packed_out, like)[0])))
    results = {(k, n): _row_form(n, d[n]) for k, d in enumerate((grads, delta, new_m, new_v)) for n in SMALL + FFN2 + MIXER}
    exchanged, = _comm("reduce_ffn1_exchange_wait", [_land(exchange)],
                       [results[k, n] for k in range(4) for n in SMALL + ("w_pool_out",)])
    share1, = _comm("reduce_ffn1_share", [red1.share(exchanged)], place)
    shared, = _comm("reduce_ffn1_share_wait", [_land(share1)], share1.token)
    grads.update(red1.named(shared))
    for n in FFN1:
        delta[n], new_m[n], new_v[n], grads[n] = _adamw(wts[n], grads[n], mom[n], var[n], token, "adamw_" + n)
    results.update({(k, n): _row_form(n, d[n]) for k, d in enumerate((grads, delta, new_m, new_v)) for n in FFN1})
    return (loss_sum, grad_x[None], *[results[k, n] for k in range(4) for n in WEIGHTS])
```

```python
from typing import Callable, NamedTuple, Optional

import jax
import jax.numpy as jnp
from jax import lax
from jax.experimental import pallas as pl
from jax.experimental.pallas import tpu as pltpu

F32 = jnp.float32
BF16 = jnp.bfloat16
RMS_EPS = 1e-6
POOL_WINDOWS = (2, 4, 8, 16)
POOL_GROUP = 128
POOL_HALO = 16
HEAD_DIM = 64
GQA_GROUP = 8
N_KV_HEADS = 2
ATTN_BLOCK = 128
SCALE = HEAD_DIM ** -0.5
NEG = -1e30
N_SHARDS = 4
ADAM_LR, ADAM_B1, ADAM_B2, ADAM_EPS, ADAM_WD, ADAM_STEP = 0.001, 0.9, 0.999, 1e-08, 0.01, 10
VMEM_LIMIT = 56 * 1024 * 1024
MESH = pl.DeviceIdType.MESH
SEG_POOL, SEG_Q, SEG_KV, SEG_GATE = (0, 512), (512, 1024), (1536, 256), (1792, 2048)
SEGMENTS = (SEG_POOL, SEG_Q, SEG_KV, SEG_GATE)


def _params(**kw):
    return pltpu.CompilerParams(vmem_limit_bytes=VMEM_LIMIT, **kw)


def _dot(a, b):
    return jnp.dot(a, b, preferred_element_type=F32)


def _dot_nt(a, b):
    return lax.dot_general(a, b, (((1,), (1,)), ((), ())), preferred_element_type=F32)


def _dot_tn(a, b):
    return lax.dot_general(a, b, (((0,), (0,)), ((), ())), preferred_element_type=F32)


def _rinv(x):
    return lax.rsqrt(jnp.mean(x * x, axis=-1, keepdims=True) + RMS_EPS)


def _rms_bwd(dn, x, g):
    r = _rinv(x)
    xr = x * r
    z = dn * g
    dx = r * (z - xr * jnp.mean(z * xr, axis=-1, keepdims=True))
    return dx, dn * xr


def _acc(ref, val, first):
    @pl.when(first)
    def _():
        ref[...] = val

    @pl.when(jnp.logical_not(first))
    def _():
        ref[...] += val


TOKEN_SPEC = pl.BlockSpec(memory_space=pl.ANY)
F_HALF = 1408


def _resident(w):
    return pl.BlockSpec(w.shape, lambda i: (0, 0), pipeline_mode=pl.Buffered(1))


def _ffn_up(h, gnorm, wgt, wut, name, tm=512):
    S, D = h.shape
    F = wgt.shape[0]

    def body(h_ref, g_ref, wg_ref, wu_ref, n_ref, a_ref, b_ref, s_ref):
        x = h_ref[...]
        n = (x * _rinv(x) * g_ref[...]).astype(BF16)
        n_ref[...] = n
        for lo in range(0, F, F_HALF):
            cols = slice(lo, lo + F_HALF)
            a = _dot_nt(n, wg_ref[cols, :])
            b = _dot_nt(n, wu_ref[cols, :])
            a_ref[:, cols] = a.astype(BF16)
            b_ref[:, cols] = b.astype(BF16)
            s_ref[:, cols] = (a * jax.nn.sigmoid(a) * b).astype(BF16)

    act = pl.BlockSpec((tm, F), lambda i: (i, 0))
    hidden = jax.ShapeDtypeStruct((S, F), BF16)
    return pl.pallas_call(
        body, name=name, grid=(S // tm,),
        in_specs=[pl.BlockSpec((tm, D), lambda i: (i, 0)), pl.BlockSpec((1, D), lambda i: (0, 0)), _resident(wgt), _resident(wut)],
        out_specs=[pl.BlockSpec((tm, D), lambda i: (i, 0)), act, act, act],
        out_shape=[jax.ShapeDtypeStruct((S, D), BF16), hidden, hidden, hidden],
        compiler_params=_params(dimension_semantics=("arbitrary",)),
    )(h, gnorm, wgt, wut)


def _ffn_gate(h, gnorm, wgt, name, tm=512):
    S, D = h.shape
    F = wgt.shape[0]

    def body(h_ref, g_ref, wg_ref, n_ref, a_ref):
        x = h_ref[...]
        n = (x * _rinv(x) * g_ref[...]).astype(BF16)
        n_ref[...] = n
        for lo in range(0, F, F_HALF):
            a_ref[:, lo:lo + F_HALF] = _dot_nt(n, wg_ref[lo:lo + F_HALF, :]).astype(BF16)

    row = pl.BlockSpec((tm, D), lambda i: (i, 0))
    return pl.pallas_call(
        body, name=name, grid=(S // tm,), in_specs=[row, pl.BlockSpec((1, D), lambda i: (0, 0)), _resident(wgt)],
        out_specs=[row, pl.BlockSpec((tm, F), lambda i: (i, 0))],
        out_shape=[jax.ShapeDtypeStruct((S, D), BF16), jax.ShapeDtypeStruct((S, F), BF16)],
        compiler_params=_params(dimension_semantics=("arbitrary",)),
    )(h, gnorm, wgt)


def _ffn_hidden(n, a, wut, name, tm=512):
    S, D = n.shape
    F = wut.shape[0]

    def body(n_ref, a_ref, wu_ref, b_ref, s_ref):
        n_t = n_ref[...]
        for lo in range(0, F, F_HALF):
            cols = slice(lo, lo + F_HALF)
            a = a_ref[:, cols].astype(F32)
            b = _dot_nt(n_t, wu_ref[cols, :])
            b_ref[:, cols] = b.astype(BF16)
            s_ref[:, cols] = (a * jax.nn.sigmoid(a) * b).astype(BF16)

    act = pl.BlockSpec((tm, F), lambda i: (i, 0))
    hidden = jax.ShapeDtypeStruct((S, F), BF16)
    return pl.pallas_call(
        body, name=name, grid=(S // tm,), in_specs=[pl.BlockSpec((tm, D), lambda i: (i, 0)), act, _resident(wut)],
        out_specs=[act, act], out_shape=[hidden, hidden],
        compiler_params=_params(dimension_semantics=("arbitrary",)),
    )(n, a, wut)


def _ffn_down(h, s, wd, name, tm=1024):
    S, D = h.shape
    F = wd.shape[0]

    def body(h_ref, s_ref, wd_ref, o_ref):
        o_ref[...] = h_ref[...] + 0.5 * _dot(s_ref[...], wd_ref[...])

    row = pl.BlockSpec((tm, D), lambda i: (i, 0))
    return pl.pallas_call(
        body, name=name, grid=(S // tm,), in_specs=[row, pl.BlockSpec((tm, F), lambda i: (i, 0)), _resident(wd)],
        out_specs=row, out_shape=jax.ShapeDtypeStruct((S, D), F32),
        compiler_params=_params(dimension_semantics=("arbitrary",)),
    )(h, s, wd)


def _ffn_down_loss(h, s, wd, target, name, tm=512):
    S, D = h.shape
    F = wd.shape[0]

    def body(h_ref, s_ref, wd_ref, t_ref, dy_ref, dyh_ref, loss_ref):
        e = h_ref[...] + 0.5 * _dot(s_ref[...], wd_ref[...]) - t_ref[...]
        dy = e * (1.0 / D)
        dy_ref[...] = dy
        dyh_ref[...] = (0.5 * dy).astype(BF16)
        tot = jnp.sum(jnp.sum(e * e, axis=1, keepdims=True), axis=0, keepdims=True) * (0.5 / D)
        _acc(loss_ref, jnp.broadcast_to(tot, loss_ref.shape), pl.program_id(0) == 0)

    row = pl.BlockSpec((tm, D), lambda i: (i, 0))
    return pl.pallas_call(
        body, name=name, grid=(S // tm,), in_specs=[row, pl.BlockSpec((tm, F), lambda i: (i, 0)), _resident(wd), row],
        out_specs=[row, row, pl.BlockSpec((8, 128), lambda i: (0, 0))],
        out_shape=[jax.ShapeDtypeStruct((S, D), F32), jax.ShapeDtypeStruct((S, D), BF16), jax.ShapeDtypeStruct((8, 128), F32)],
        compiler_params=_params(dimension_semantics=("arbitrary",)),
    )(h, s, wd, target)


def _ffn_bwd_h(dyh, a, b, wd, token, name, tm=512):
    S, D = dyh.shape
    F = wd.shape[0]

    def body(dyh_ref, a_ref, b_ref, wd_ref, _, da_ref, db_ref):
        dyh_t = dyh_ref[...]
        for lo in range(0, F, F_HALF):
            cols = slice(lo, lo + F_HALF)
            av = a_ref[:, cols].astype(F32)
            bv = b_ref[:, cols].astype(F32)
            ds = _dot_nt(dyh_t, wd_ref[cols, :])
            sig = jax.nn.sigmoid(av)
            silu = av * sig
            da_ref[:, cols] = (ds * bv * (sig * (1.0 + av * (1.0 - sig)))).astype(BF16)
            db_ref[:, cols] = (ds * silu).astype(BF16)

    act = pl.BlockSpec((tm, F), lambda i: (i, 0))
    hidden = jax.ShapeDtypeStruct((S, F), BF16)
    return pl.pallas_call(
        body, name=name, grid=(S // tm,),
        in_specs=[pl.BlockSpec((tm, D), lambda i: (i, 0)), act, act, _resident(wd), TOKEN_SPEC],
        out_specs=[act, act], out_shape=[hidden, hidden],
        compiler_params=_params(dimension_semantics=("arbitrary",)),
    )(dyh, a, b, wd, token)


def _xty(x, y, name, tk=2048, tf=1408):
    S, F = x.shape
    D = y.shape[1]

    def body(x_ref, y_ref, o_ref):
        _acc(o_ref, _dot_tn(x_ref[...], y_ref[...]), pl.program_id(1) == 0)

    return pl.pallas_call(
        body, name=name, grid=(F // tf, S // tk),
        in_specs=[pl.BlockSpec((tk, tf), lambda j, k: (k, j)), pl.BlockSpec((tk, D), lambda j, k: (k, 0))],
        out_specs=pl.BlockSpec((tf, D), lambda j, k: (j, 0)), out_shape=jax.ShapeDtypeStruct((F, D), F32),
        compiler_params=_params(dimension_semantics=("arbitrary", "arbitrary")),
    )(x, y)


def _ffn_bwd_x(da, db, wgt, wut, x, dh, gnorm, name, tm=512):
    S, D = x.shape
    F = wgt.shape[0]

    def body(da_ref, db_ref, wg_ref, wu_ref, x_ref, dh_ref, g_ref, dx_ref, dg_ref):
        dn = _dot(da_ref[...], wg_ref[...]) + _dot(db_ref[...], wu_ref[...])
        dx, dg_rows = _rms_bwd(dn, x_ref[...], g_ref[...])
        dx_ref[...] = dh_ref[...] + dx
        _acc(dg_ref, jnp.sum(dg_rows, axis=0, keepdims=True), pl.program_id(0) == 0)

    row = pl.BlockSpec((tm, D), lambda i: (i, 0))
    act = pl.BlockSpec((tm, F), lambda i: (i, 0))
    vec = pl.BlockSpec((1, D), lambda i: (0, 0))
    return pl.pallas_call(
        body, name=name, grid=(S // tm,), in_specs=[act, act, _resident(wgt), _resident(wut), row, row, vec],
        out_specs=[row, vec],
        out_shape=[jax.ShapeDtypeStruct((S, D), F32), jax.ShapeDtypeStruct((1, D), F32)],
        compiler_params=_params(dimension_semantics=("arbitrary",)),
    )(da, db, wgt, wut, x, dh, gnorm)


def _mix_proj(h, gnorm, wint, tm=512):
    S, D = h.shape
    nt = S // tm

    def body(h_ref, g_ref, w_ref, u_ref, xp_ref, q_ref, kv_ref, gl_ref):
        x = h_ref[...]
        u = (x * _rinv(x) * g_ref[...]).astype(BF16)
        u_ref[...] = u
        for (off, width), ref in zip(SEGMENTS, (xp_ref, q_ref, kv_ref, gl_ref)):
            ref[...] = _dot_nt(u, w_ref[off:off + width, :]).astype(ref.dtype)

    def row(width):
        return pl.BlockSpec((tm, width), lambda i: (i, 0))

    return pl.pallas_call(
        body, name="mix_proj", grid=(nt,),
        in_specs=[row(D), pl.BlockSpec((1, D), lambda i: (0, 0)), pl.BlockSpec(wint.shape, lambda i: (0, 0))],
        out_specs=[row(D), row(SEG_POOL[1]), row(SEG_Q[1]), row(SEG_KV[1]), row(SEG_GATE[1])],
        out_shape=[jax.ShapeDtypeStruct((S, D), BF16), jax.ShapeDtypeStruct((S, SEG_POOL[1]), F32),
                   jax.ShapeDtypeStruct((S, SEG_Q[1]), BF16), jax.ShapeDtypeStruct((S, SEG_KV[1]), BF16),
                   jax.ShapeDtypeStruct((S, SEG_GATE[1]), BF16)],
        compiler_params=_params(dimension_semantics=("arbitrary",)),
    )(h, gnorm, wint)


def _stack_heads(x, g):
    return jnp.concatenate([x[:, (GQA_GROUP * g + hh) * HEAD_DIM:(GQA_GROUP * g + hh + 1) * HEAD_DIM]
                            for hh in range(GQA_GROUP)], axis=0)


def _unstack_heads(ref, val, g):
    for hh in range(GQA_GROUP):
        lo = (GQA_GROUP * g + hh) * HEAD_DIM
        ref[:, lo:lo + HEAD_DIM] = val[hh * ATTN_BLOCK:(hh + 1) * ATTN_BLOCK, :]


def _rowsum(xb, width):
    return _dot(xb, jnp.ones((xb.shape[1], width), BF16))


def _rinv_lanes(x):
    return lax.rsqrt(_rowsum((x * x).astype(BF16), x.shape[1]) * (1.0 / x.shape[1]) + RMS_EPS)


def _twice(x):
    return jnp.concatenate([x, x], axis=1)


def _band_bias():
    qi = (jnp.arange(GQA_GROUP * ATTN_BLOCK) % ATTN_BLOCK)[:, None]
    kj = jnp.arange(2 * ATTN_BLOCK)[None, :]
    return jnp.where(jnp.logical_and(kj > qi, kj <= qi + ATTN_BLOCK), 0.0, NEG).astype(F32)


def _attn_exp(qn, kk, sink, band, n):
    kj = lax.broadcasted_iota(jnp.int32, (1, 2 * ATTN_BLOCK), 1)
    before_sequence = jnp.where(jnp.logical_and(kj < ATTN_BLOCK, n == 0), NEG, 0.0)
    s = _dot_nt(qn, kk) + band + before_sequence
    m = jnp.maximum(jnp.broadcast_to(jnp.max(s, axis=-1, keepdims=True), sink.shape), sink)
    e = jnp.exp(s - _twice(m))
    e_s = jnp.exp(sink - m)
    e_b = e.astype(BF16)
    inv = 1.0 / (_rowsum(e_b, ATTN_BLOCK) + e_s)
    return e_b, e, e_s, inv


def _attn_blocks(n):
    cur = pl.multiple_of(n * ATTN_BLOCK, ATTN_BLOCK)
    prev = pl.multiple_of(jnp.maximum(n - 1, 0) * ATTN_BLOCK, ATTN_BLOCK)
    return cur, prev


def _kv_window(kv_ref, n):
    cur, prev = _attn_blocks(n)
    return jnp.concatenate([kv_ref[pl.ds(prev, ATTN_BLOCK), :], kv_ref[pl.ds(cur, ATTN_BLOCK), :]], axis=0).astype(F32)


def _kv_split(kv, g):
    k = kv[:, g * HEAD_DIM:(g + 1) * HEAD_DIM]
    v = kv[:, (N_KV_HEADS + g) * HEAD_DIM:(N_KV_HEADS + g + 1) * HEAD_DIM]
    return k, v


def _attn_fwd(q, kv, qw, kw, sink_rows, band):
    S, W = q.shape
    nb = S // ATTN_BLOCK

    def body(q_ref, kv_ref, qw_ref, kw_ref, sk_ref, band_ref, o_ref, o_scr):
        n = pl.program_id(0)
        qf = q_ref[...].astype(F32)
        kvw = _kv_window(kv_ref, n)
        for g in range(N_KV_HEADS):
            qs = _stack_heads(qf, g)
            qn = (qs * _rinv_lanes(qs) * qw_ref[...] * SCALE).astype(BF16)
            k, v = _kv_split(kvw, g)
            kk = (k * _rinv_lanes(k) * kw_ref[...]).astype(BF16)
            e_b, _, _, inv = _attn_exp(qn, kk, sk_ref[g], band_ref[...], n)
            _unstack_heads(o_scr, _dot(e_b, v.astype(BF16)) * inv[:, :HEAD_DIM], g)
        o_ref[...] = o_scr[...].astype(BF16)

    blk = pl.BlockSpec((ATTN_BLOCK, W), lambda n: (n, 0))
    return pl.pallas_call(
        body, name="attn_fwd", grid=(nb,),
        in_specs=[blk, pl.BlockSpec(kv.shape, lambda n: (0, 0)), pl.BlockSpec((1, HEAD_DIM), lambda n: (0, 0)),
                  pl.BlockSpec((1, HEAD_DIM), lambda n: (0, 0)), pl.BlockSpec(sink_rows.shape, lambda n: (0, 0, 0)),
                  pl.BlockSpec(band.shape, lambda n: (0, 0))],
        out_specs=blk, out_shape=jax.ShapeDtypeStruct((S, W), BF16),
        scratch_shapes=[pltpu.VMEM((ATTN_BLOCK, W), F32)],
        compiler_params=_params(dimension_semantics=("arbitrary",)),
    )(q, kv, qw, kw, sink_rows, band)


def _attn_bwd(q, kv, do, qw, kw, sink_rows, band):
    S, W = q.shape
    KW = kv.shape[1]
    nb = S // ATTN_BLOCK
    chunk = 512

    def body(q_ref, kv_ref, do_ref, qw_ref, kw_ref, sk_ref, band_ref, dq_ref, dkv_ref, dqw_ref, dkw_ref, dsk_ref, dq_scr):
        n = pl.program_id(0)

        @pl.when(n == 0)
        def _():
            dkv_ref[...] = jnp.zeros_like(dkv_ref)
            dqw_ref[...] = jnp.zeros_like(dqw_ref)
            dsk_ref[...] = jnp.zeros_like(dsk_ref)

        qf = q_ref[...].astype(F32)
        dof = do_ref[...].astype(F32)
        cur, prev = _attn_blocks(n)
        kvw = _kv_window(kv_ref, n)
        qw_v = qw_ref[...]
        for g in range(N_KV_HEADS):
            qs = _stack_heads(qf, g)
            rq = _rinv_lanes(qs)
            qhat = qs * rq
            qn = (qhat * qw_v * SCALE).astype(BF16)
            k, v = _kv_split(kvw, g)
            kk = (k * _rinv_lanes(k) * kw_ref[...]).astype(BF16)
            vv = v.astype(BF16)
            dos = _stack_heads(dof, g).astype(BF16)
            _, e, e_s, inv = _attn_exp(qn, kk, sk_ref[g], band_ref[...], n)
            p = e * _twice(inv)
            dp = _dot_nt(dos, vv)
            drow = _rowsum((p * dp).astype(BF16), ATTN_BLOCK)
            ds = (p * (dp - _twice(drow))).astype(BF16)
            dsink = -(e_s * inv * drow)
            for hh in range(GQA_GROUP):
                tot = jnp.sum(dsink[hh * ATTN_BLOCK:(hh + 1) * ATTN_BLOCK, :], axis=0, keepdims=True)
                dsk_ref[g, hh:hh + 1, :] += tot
            dqn = _dot(ds, kk) * SCALE
            dkk = _dot_tn(ds, qn)
            dvv = _dot_tn(p.astype(BF16), dos)
            klo, vlo = g * HEAD_DIM, (N_KV_HEADS + g) * HEAD_DIM
            for start, rows in ((prev, slice(0, ATTN_BLOCK)), (cur, slice(ATTN_BLOCK, 2 * ATTN_BLOCK))):
                dkv_ref[pl.ds(start, ATTN_BLOCK), klo:klo + HEAD_DIM] += dkk[rows]
                dkv_ref[pl.ds(start, ATTN_BLOCK), vlo:vlo + HEAD_DIM] += dvv[rows]
            dqw_ref[...] += jnp.sum(dqn * qhat, axis=0, keepdims=True)
            z = dqn * qw_v
            dqs = rq * (z - qhat * (_rowsum((z * qhat).astype(BF16), HEAD_DIM) * (1.0 / HEAD_DIM)))
            _unstack_heads(dq_scr, dqs, g)
        dq_ref[...] = dq_scr[...].astype(BF16)

        @pl.when(n == nb - 1)
        def _():
            def one(c, dkw):
                rows = pl.ds(pl.multiple_of(c * chunk, chunk), chunk)
                for g in range(N_KV_HEADS):
                    lo = g * HEAD_DIM
                    k = kv_ref[rows, lo:lo + HEAD_DIM].astype(F32)
                    dx, dg_rows = _rms_bwd(dkv_ref[rows, lo:lo + HEAD_DIM], k, kw_ref[...])
                    dkv_ref[rows, lo:lo + HEAD_DIM] = dx
                    dkw = dkw + jnp.sum(dg_rows, axis=0, keepdims=True)
                return dkw

            dkw_ref[...] = lax.fori_loop(0, S // chunk, one, jnp.zeros((1, HEAD_DIM), F32))

    blk = pl.BlockSpec((ATTN_BLOCK, W), lambda n: (n, 0))
    whole_kv = pl.BlockSpec((S, KW), lambda n: (0, 0))
    vec = pl.BlockSpec((1, HEAD_DIM), lambda n: (0, 0))
    sk = pl.BlockSpec(sink_rows.shape, lambda n: (0, 0, 0))
    dsk = pl.BlockSpec((N_KV_HEADS, GQA_GROUP, 128), lambda n: (0, 0, 0))
    return pl.pallas_call(
        body, name="attn_bwd", grid=(nb,), in_specs=[blk, whole_kv, blk, vec, vec, sk, pl.BlockSpec(band.shape, lambda n: (0, 0))],
        out_specs=[blk, whole_kv, vec, vec, dsk],
        out_shape=[jax.ShapeDtypeStruct((S, W), BF16), jax.ShapeDtypeStruct((S, KW), F32),
                   jax.ShapeDtypeStruct((1, HEAD_DIM), F32), jax.ShapeDtypeStruct((1, HEAD_DIM), F32),
                   jax.ShapeDtypeStruct((N_KV_HEADS, GQA_GROUP, 128), F32)],
        scratch_shapes=[pltpu.VMEM((ATTN_BLOCK, W), F32)],
        compiler_params=_params(dimension_semantics=("arbitrary",)),
    )(q, kv, do, qw, kw, sink_rows, band)


def _pooled(xc, xprev, i):
    tm = xc.shape[0]
    xh = jnp.concatenate([jnp.where(i > 0, xprev, 0.0), xc], axis=0)
    t = lax.broadcasted_iota(jnp.int32, (tm, 1), 0) + i * tm
    out = []
    for gi, w in enumerate(POOL_WINDOWS):
        acc = xh[:, gi * POOL_GROUP:(gi + 1) * POOL_GROUP]
        sh = 1
        while sh < w:
            acc = acc + pltpu.roll(acc, sh, 0)
            sh *= 2
        cnt = jnp.minimum(t + 1, w).astype(F32)
        out.append(acc[POOL_HALO:, :] / cnt - xc[:, gi * POOL_GROUP:(gi + 1) * POOL_GROUP])
    return jnp.concatenate(out, axis=1)


def _pool_mix(pooled_b, pw_ref):
    return jnp.concatenate([_dot(pooled_b[:, gi * POOL_GROUP:(gi + 1) * POOL_GROUP], pw_ref[gi])
                            for gi in range(len(POOL_WINDOWS))], axis=1)


def _halo_specs(tm, width, S, after):
    per = tm // POOL_HALO
    last = S // POOL_HALO - 1
    if after:
        return pl.BlockSpec((POOL_HALO, width), lambda i: (jnp.minimum((i + 1) * per, last), 0))
    return pl.BlockSpec((POOL_HALO, width), lambda i: (jnp.maximum(i * per - 1, 0), 0))


def _mix_out(xp, attn, gl, bias, pw, pscale, wpot, wao, wo, h, tm=512):
    S, D = h.shape
    nt = S // tm
    PW = xp.shape[1]

    def body(xc_ref, xprev_ref, at_ref, gl_ref, bias_ref, pw_ref, ps_ref, wpo_ref, wao_ref, wo_ref, h_ref,
             ho_ref, bp_ref, ba_ref):
        i = pl.program_id(0)
        pooled = _pooled(xc_ref[...], xprev_ref[...], i).astype(BF16)
        ms = (_pool_mix(pooled, pw_ref) * ps_ref[...]).astype(BF16)
        bp = _dot_nt(ms, wpo_ref[...])
        ba = _dot(at_ref[...], wao_ref[...])
        bp_ref[...] = bp.astype(BF16)
        ba_ref[...] = ba.astype(BF16)
        gates = jax.nn.sigmoid(gl_ref[...].astype(F32) + bias_ref[...])
        merged = (gates[:, :D] * bp + gates[:, D:] * ba).astype(BF16)
        ho_ref[...] = h_ref[...] + _dot(merged, wo_ref[...])

    def row(width):
        return pl.BlockSpec((tm, width), lambda i: (i, 0))

    def whole(x):
        nd = x.ndim
        return pl.BlockSpec(x.shape, lambda i: (0,) * nd)

    return pl.pallas_call(
        body, name="mix_out", grid=(nt,),
        in_specs=[row(PW), _halo_specs(tm, PW, S, False), row(D), row(2 * D), whole(bias), whole(pw), whole(pscale),
                  whole(wpot), whole(wao), whole(wo), row(D)],
        out_specs=[row(D), row(D), row(D)],
        out_shape=[jax.ShapeDtypeStruct((S, D), F32), jax.ShapeDtypeStruct((S, D), BF16),
                   jax.ShapeDtypeStruct((S, D), BF16)],
        compiler_params=_params(dimension_semantics=("arbitrary",)),
    )(xp, xp, attn, gl, bias, pw, pscale, wpot, wao, wo, h)


def _mix_bwd_gate(dh, wo, bp, ba, gl, bias, tm=512):
    S, D = dh.shape
    nt = S // tm

    def body(dh_ref, wo_ref, bp_ref, ba_ref, gl_ref, bias_ref, dgl_ref, dbp_ref, dba_ref, dwo_ref, dbias_ref):
        i = pl.program_id(0)
        dhb = dh_ref[...].astype(BF16)
        dm = _dot_nt(dhb, wo_ref[...])
        gates = jax.nn.sigmoid(gl_ref[...].astype(F32) + bias_ref[...])
        gp, ga = gates[:, :D], gates[:, D:]
        bp_v = bp_ref[...].astype(F32)
        ba_v = ba_ref[...].astype(F32)
        merged = (gp * bp_v + ga * ba_v).astype(BF16)
        _acc(dwo_ref, _dot_tn(merged, dhb), i == 0)
        dbp_ref[...] = (dm * gp).astype(BF16)
        dba_ref[...] = (dm * ga).astype(BF16)
        dgl = jnp.concatenate([dm * bp_v * gp * (1.0 - gp), dm * ba_v * ga * (1.0 - ga)], axis=1)
        dgl_ref[...] = dgl.astype(BF16)
        _acc(dbias_ref, jnp.sum(dgl, axis=0, keepdims=True), i == 0)

    def row(width):
        return pl.BlockSpec((tm, width), lambda i: (i, 0))

    def whole(shape):
        return pl.BlockSpec(shape, lambda i: (0, 0))

    return pl.pallas_call(
        body, name="mix_bwd_gate", grid=(nt,),
        in_specs=[row(D), whole(wo.shape), row(D), row(D), row(2 * D), whole(bias.shape)],
        out_specs=[row(2 * D), row(D), row(D), whole((D, D)), whole((1, 2 * D))],
        out_shape=[jax.ShapeDtypeStruct((S, 2 * D), BF16), jax.ShapeDtypeStruct((S, D), BF16),
                   jax.ShapeDtypeStruct((S, D), BF16), jax.ShapeDtypeStruct((D, D), F32),
                   jax.ShapeDtypeStruct((1, 2 * D), F32)],
        compiler_params=_params(dimension_semantics=("arbitrary",)),
    )(dh, wo, bp, ba, gl, bias)


def _mix_bwd_branch(dbp, dba, attn, xp, pw, pscale, wpot, wao, tm=1024):
    S, D = dbp.shape
    nt = S // tm
    PW = xp.shape[1]
    NG = len(POOL_WINDOWS)

    def body(dbp_ref, dba_ref, at_ref, xc_ref, xprev_ref, pw_ref, ps_ref, wpo_ref, wao_ref,
             dat_ref, dpl_ref, dwao_ref, dwpo_ref, dpw_ref, dps_ref):
        i = pl.program_id(0)
        dba_v = dba_ref[...]
        dbp_v = dbp_ref[...]
        _acc(dwao_ref, _dot_tn(at_ref[...], dba_v), i == 0)
        dat_ref[...] = _dot_nt(dba_v, wao_ref[...]).astype(BF16)
        pooled = _pooled(xc_ref[...], xprev_ref[...], i).astype(BF16)
        mixed = _pool_mix(pooled, pw_ref)
        ps = ps_ref[...]
        _acc(dwpo_ref, _dot_tn(dbp_v, (mixed * ps).astype(BF16)), i == 0)
        dms = _dot(dbp_v, wpo_ref[...])
        _acc(dps_ref, jnp.sum(dms * mixed, axis=0, keepdims=True), i == 0)
        dmixed = (dms * ps).astype(BF16)
        dpooled = []
        for gi in range(NG):
            cols = slice(gi * POOL_GROUP, (gi + 1) * POOL_GROUP)
            _acc(dpw_ref.at[gi], _dot_tn(pooled[:, cols], dmixed[:, cols]), i == 0)
            dpooled.append(_dot_nt(dmixed[:, cols], pw_ref[gi]))
        dpl_ref[...] = jnp.concatenate(dpooled, axis=1)

    def row(width):
        return pl.BlockSpec((tm, width), lambda i: (i, 0))

    def whole(shape):
        nd = len(shape)
        return pl.BlockSpec(shape, lambda i: (0,) * nd)

    return pl.pallas_call(
        body, name="mix_bwd_branch", grid=(nt,),
        in_specs=[row(D), row(D), row(D), row(PW), _halo_specs(tm, PW, S, False), whole(pw.shape), whole(pscale.shape),
                  whole(wpot.shape), whole(wao.shape)],
        out_specs=[row(D), row(PW), whole((D, D)), whole((D, PW)), whole(pw.shape), whole((1, PW))],
        out_shape=[jax.ShapeDtypeStruct((S, D), BF16), jax.ShapeDtypeStruct((S, PW), F32),
                   jax.ShapeDtypeStruct((D, D), F32), jax.ShapeDtypeStruct((D, PW), F32),
                   jax.ShapeDtypeStruct(pw.shape, F32), jax.ShapeDtypeStruct((1, PW), F32)],
        compiler_params=_params(dimension_semantics=("arbitrary",)),
    )(dbp, dba, attn, xp, xp, pw, pscale, wpot, wao)


def _pool_bwd(dpooled, tm=512):
    S, PW = dpooled.shape
    nt = S // tm

    def body(dc_ref, dnext_ref, dxp_ref):
        i = pl.program_id(0)
        dc = dc_ref[...]
        dh = jnp.concatenate([dc, jnp.where(i < nt - 1, dnext_ref[...], 0.0)], axis=0)
        rows = tm + POOL_HALO
        t = lax.broadcasted_iota(jnp.int32, (rows, 1), 0) + i * tm
        out = []
        for gi, w in enumerate(POOL_WINDOWS):
            cols = slice(gi * POOL_GROUP, (gi + 1) * POOL_GROUP)
            acc = dh[:, cols] / jnp.minimum(t + 1, w).astype(F32)
            sh = 1
            while sh < w:
                acc = acc + pltpu.roll(acc, rows - sh, 0)
                sh *= 2
            out.append(acc[:tm, :] - dc[:, cols])
        dxp_ref[...] = jnp.concatenate(out, axis=1).astype(BF16)

    return pl.pallas_call(
        body, name="pool_bwd", grid=(nt,),
        in_specs=[pl.BlockSpec((tm, PW), lambda i: (i, 0)), _halo_specs(tm, PW, S, True)],
        out_specs=pl.BlockSpec((tm, PW), lambda i: (i, 0)), out_shape=jax.ShapeDtypeStruct((S, PW), BF16),
        compiler_params=_params(dimension_semantics=("arbitrary",)),
    )(dpooled, dpooled)


def _in_bwd_w(pieces, u, tm=1024):
    S, D = u.shape
    nt = S // tm
    NW = sum(width for _, width in SEGMENTS)

    def body(*refs):
        piece_refs, u_ref, dw_hbm, acc, sem = refs[:len(SEGMENTS)], refs[len(SEGMENTS)], refs[len(SEGMENTS) + 1], refs[-2], refs[-1]
        i = pl.program_id(0)
        u_t = u_ref[...]
        for (off, width), ref in zip(SEGMENTS, piece_refs):
            for lo in range(0, width, D):
                hi = min(lo + D, width)
                _acc(acc.at[off + lo:off + hi, :], _dot_tn(ref[:, lo:hi].astype(BF16), u_t), i == 0)

        @pl.when(i == nt - 1)
        def _():
            cp = pltpu.make_async_copy(acc, dw_hbm, sem)
            cp.start()
            cp.wait()

    return pl.pallas_call(
        body, name="in_bwd_w", grid=(nt,),
        in_specs=[pl.BlockSpec((tm, width), lambda i: (i, 0)) for _, width in SEGMENTS] + [pl.BlockSpec((tm, D), lambda i: (i, 0))],
        out_specs=pl.BlockSpec(memory_space=pl.ANY), out_shape=jax.ShapeDtypeStruct((NW, D), F32),
        scratch_shapes=[pltpu.VMEM((NW, D), F32), pltpu.SemaphoreType.DMA],
        compiler_params=_params(dimension_semantics=("arbitrary",)),
    )(*pieces, u)


def _in_bwd_x(pieces, wint, h, dh, gnorm, tm=512):
    S, D = h.shape
    nt = S // tm

    def body(*refs):
        piece_refs = refs[:len(SEGMENTS)]
        w_ref, h_ref, dh_ref, g_ref, dx_ref, dxh_ref, dg_ref = refs[len(SEGMENTS):]
        i = pl.program_id(0)
        du = jnp.zeros((tm, D), F32)
        for (off, width), ref in zip(SEGMENTS, piece_refs):
            du = du + _dot(ref[...].astype(BF16), w_ref[off:off + width, :])
        dx, dg_rows = _rms_bwd(du, h_ref[...], g_ref[...])
        out = dh_ref[...] + dx
        dx_ref[...] = out
        dxh_ref[...] = (0.5 * out).astype(BF16)
        _acc(dg_ref, jnp.sum(dg_rows, axis=0, keepdims=True), i == 0)

    def row(width):
        return pl.BlockSpec((tm, width), lambda i: (i, 0))

    vec = pl.BlockSpec((1, D), lambda i: (0, 0))
    return pl.pallas_call(
        body, name="in_bwd_x", grid=(nt,),
        in_specs=[row(width) for _, width in SEGMENTS] + [pl.BlockSpec(wint.shape, lambda i: (0, 0)), row(D), row(D), vec],
        out_specs=[row(D), row(D), vec],
        out_shape=[jax.ShapeDtypeStruct((S, D), F32), jax.ShapeDtypeStruct((S, D), BF16), jax.ShapeDtypeStruct((1, D), F32)],
        compiler_params=_params(dimension_semantics=("arbitrary",)),
    )(*pieces, wint, h, dh, gnorm)


def _row_tile(rows):
    for t in (512, 480, 352, 256, 128, 64, 32, 16, 8):
        if rows % t == 0:
            return t
    return rows


def _adamw(w, g, m, v, token, name):
    R, C = w.shape
    tr = _row_tile(R)

    def body(w_ref, g_ref, m_ref, v_ref, _, d_ref, mo_ref, vo_ref, go_ref):
        gv = g_ref[...]
        go_ref[...] = gv
        mn = ADAM_B1 * m_ref[...] + (1.0 - ADAM_B1) * gv
        vn = ADAM_B2 * v_ref[...] + (1.0 - ADAM_B2) * (gv * gv)
        m_hat = mn / (1.0 - ADAM_B1 ** ADAM_STEP)
        v_hat = vn / (1.0 - ADAM_B2 ** ADAM_STEP)
        d_ref[...] = -ADAM_LR * (m_hat / (jnp.sqrt(v_hat) + ADAM_EPS) + ADAM_WD * w_ref[...])
        mo_ref[...] = mn
        vo_ref[...] = vn

    blk = pl.BlockSpec((tr, C), lambda i: (i, 0))
    sh = jax.ShapeDtypeStruct((R, C), F32)
    return pl.pallas_call(
        body, name=name, grid=(R // tr,), in_specs=[blk] * 4 + [TOKEN_SPEC], out_specs=[blk] * 4, out_shape=[sh] * 4,
        compiler_params=_params(dimension_semantics=("arbitrary",)),
    )(w, g, m, v, token)


def _cast_place(w, place, name):
    R, C = w.shape
    tr = _row_tile(R)
    per = R // tr

    def body(p_ref, w_ref, o_ref):
        o_ref[...] = w_ref[...].astype(BF16)

    grid_spec = pltpu.PrefetchScalarGridSpec(
        num_scalar_prefetch=1, grid=(per,),
        in_specs=[pl.BlockSpec((tr, C), lambda i, p: (i, 0))],
        out_specs=pl.BlockSpec((tr, C), lambda i, p: (p[0] * per + i, 0)))
    return pl.pallas_call(
        body, name=name, grid_spec=grid_spec, out_shape=jax.ShapeDtypeStruct((N_SHARDS * R, C), BF16),
        compiler_params=_params(dimension_semantics=("arbitrary",)),
    )(place, w)


def _parked(w, m, idx, first, last):
    return jnp.where(m == w, idx, jnp.where(m < w, first, last))


def _sum_halves(g4s, recvs, place, name):
    M = len(g4s)
    NS, R, C = g4s[0].shape
    hr = R // 2
    tr = _row_tile(hr)
    per = hr // tr

    def body(p_ref, *refs):
        m = pl.program_id(0)
        for w in range(M):
            @pl.when(m == w)
            def _(w=w):
                refs[2 * M + w][...] = (refs[w][...] + refs[M + w][...]).astype(BF16)

    def spec(w, mine):
        def index(m, s, i, p):
            row = _parked(w, m, i, 0, per - 1)
            return _parked(w, m, s, 0, NS - 1), (p[1] * per + row if mine else row), 0

        return pl.BlockSpec((1, tr, C), index)

    grid_spec = pltpu.PrefetchScalarGridSpec(
        num_scalar_prefetch=1, grid=(M, NS, per),
        in_specs=[spec(w, True) for w in range(M)] + [spec(w, False) for w in range(M)],
        out_specs=[spec(w, False) for w in range(M)])
    return pl.pallas_call(
        body, name=name, grid_spec=grid_spec, out_shape=[jax.ShapeDtypeStruct((NS, hr, C), BF16)] * M,
        compiler_params=_params(dimension_semantics=("arbitrary",) * 3),
    )(place, *g4s, *recvs)


def _sum_quarters(h4s, recv3s, place, name):
    M = len(h4s)
    NS, hr, C = h4s[0].shape
    tr = _row_tile(hr)
    per = hr // tr

    def body(p_ref, *refs):
        m = pl.program_id(0)
        for w in range(M):
            @pl.when(m == w)
            def _(w=w):
                acc = refs[w][0].astype(F32)
                for k in range(N_SHARDS - 1):
                    acc = acc + refs[M + w][k].astype(F32)
                refs[2 * M + w][...] = acc

    def row(w, m, i):
        return _parked(w, m, i, 0, per - 1)

    grid_spec = pltpu.PrefetchScalarGridSpec(
        num_scalar_prefetch=1, grid=(M, per),
        in_specs=[pl.BlockSpec((1, tr, C), lambda m, i, p, w=w: (p[0], row(w, m, i), 0)) for w in range(M)]
        + [pl.BlockSpec((N_SHARDS - 1, tr, C), lambda m, i, p, w=w: (0, row(w, m, i), 0)) for w in range(M)],
        out_specs=[pl.BlockSpec((tr, C), lambda m, i, p, w=w: (p[1] * per + row(w, m, i), 0)) for w in range(M)])
    return pl.pallas_call(
        body, name=name, grid_spec=grid_spec, out_shape=[jax.ShapeDtypeStruct((2 * hr, C), F32)] * M,
        compiler_params=_params(dimension_semantics=("arbitrary",) * 2),
    )(place, *h4s, *recv3s)


def _by_shape(arrays):
    groups = {}
    for k, a in enumerate(arrays):
        groups.setdefault(a.shape, []).append(k)
    return list(groups.values())


def _place():
    x, y, c = lax.axis_index("x"), lax.axis_index("y"), lax.axis_index("c")
    chips = [(1 - x, y), (x, 1 - y), (1 - x, 1 - y)]
    return x, y, c, chips


HBM_SPEC = pl.BlockSpec(memory_space=pltpu.HBM)
SEM_SPEC = pl.BlockSpec(memory_space=pltpu.SEMAPHORE)
DATAFLOW = pltpu.SideEffectType.DATAFLOW_SIDE_EFFECTING


def _hbm(a):
    return pltpu.with_memory_space_constraint(a, pltpu.HBM)


class InFlight(NamedTuple):
    send_sem: jax.Array
    recv_sem: jax.Array
    bufs: list
    plan: Callable
    token: jax.Array


class Leg(NamedTuple):
    bufs: list
    landing: Optional[InFlight]
    plan: Optional[Callable]
    n_copies: int


def _send(bufs, plan, n_copies):
    return Leg([_hbm(b) for b in bufs], None, plan, n_copies)


def _land(flight):
    return Leg(flight.bufs, flight, None, 0)


def _forward(flight, plan, n_copies):
    return Leg(flight.bufs, flight, plan, n_copies)


def _wait_all(plan, refs, send_ref, recv_ref):
    for k, (src, dst, dev) in enumerate(plan(refs)):
        cp = pltpu.make_async_remote_copy(src_ref=src, dst_ref=dst, send_sem=send_ref.at[k], recv_sem=recv_ref.at[k],
                                          device_id=dev, device_id_type=MESH)
        cp.wait_send()
        cp.wait_recv()


def _start_all(plan, refs, send_ref, recv_ref):
    for k, (src, dst, dev) in enumerate(plan(refs)):
        pltpu.make_async_remote_copy(src_ref=src, dst_ref=dst, send_sem=send_ref.at[k], recv_sem=recv_ref.at[k],
                                     device_id=dev, device_id_type=MESH).start()


def _comm(name, legs, after):
    after = list(after) if isinstance(after, (list, tuple)) else [after]
    ins, in_specs, out_shape, out_specs, aliases, first = [], [], [], [], {}, []
    for leg in legs:
        first.append((len(ins), len(out_shape)))
        for b in leg.bufs:
            aliases[len(ins)] = len(out_shape)
            ins.append(b)
            in_specs.append(HBM_SPEC)
            out_shape.append(pltpu.HBM(b.shape, b.dtype))
            out_specs.append(HBM_SPEC)
        if leg.landing is not None:
            ins += [leg.landing.send_sem, leg.landing.recv_sem]
            in_specs += [SEM_SPEC, SEM_SPEC]
        if leg.plan is not None:
            out_shape += [pltpu.SemaphoreType.DMA((leg.n_copies,))] * 2
            out_specs += [SEM_SPEC, SEM_SPEC]
    starts = any(leg.plan is not None for leg in legs)
    if starts:
        out_shape.append(jax.ShapeDtypeStruct((8, 128), F32))
        out_specs.append(pl.BlockSpec(memory_space=pltpu.VMEM))
    n_in = len(ins) + len(after)

    def body(*refs):
        outs = refs[n_in:]
        for leg, (i, o) in zip(legs, first):
            nb = len(leg.bufs)
            bufs = refs[i:i + nb]
            if leg.landing is not None:
                _wait_all(leg.landing.plan, bufs, refs[i + nb], refs[i + nb + 1])
            if leg.plan is not None:
                _start_all(leg.plan, bufs, outs[o + nb], outs[o + nb + 1])
        if starts:
            outs[-1][...] = jnp.zeros_like(outs[-1])

    out = pl.pallas_call(
        body, name=name, in_specs=in_specs + [TOKEN_SPEC] * len(after), out_shape=out_shape, out_specs=out_specs,
        input_output_aliases=aliases, compiler_params=pltpu.CompilerParams(has_side_effects=DATAFLOW),
    )(*ins, *after)
    results = []
    for leg, (_, o) in zip(legs, first):
        nb = len(leg.bufs)
        bufs = list(out[o:o + nb])
        results.append(bufs if leg.plan is None else InFlight(out[o + nb], out[o + nb + 1], bufs, leg.plan, out[-1]))
    return results


def _half_rows(buf, chip, core):
    hr = buf.shape[0] // (2 * N_SHARDS)
    return buf.at[pl.ds(pl.multiple_of((2 * chip + core) * hr, 16), hr)]


def _gather_ici_plan(bufs):
    x, y, c, chips = _place()
    return [(_half_rows(b, 2 * x + y, c), _half_rows(b, 2 * x + y, c), (px, py, c)) for b in bufs for px, py in chips]


def _gather_d2d_plan(bufs):
    x, y, c, chips = _place()
    return [(_half_rows(b, 2 * px + py, c), _half_rows(b, 2 * px + py, c), (x, y, 1 - c)) for b in bufs for px, py in chips]


def _swap_plan(bufs):
    x, y, c, _ = _place()
    n = len(bufs) // 2
    copies = []
    for g, land in zip(bufs[:n], bufs[n:]):
        hr = g.shape[1] // 2
        copies.append((g.at[:, pl.ds(pl.multiple_of((1 - c) * hr, 8), hr)], land, (x, y, 1 - c)))
    return copies


def _exchange_plan(bufs):
    x, y, c, chips = _place()
    n = len(bufs) // 2
    return [(h.at[2 * px + py], land.at[k], (px, py, c))
            for h, land in zip(bufs[:n], bufs[n:]) for k, (px, py) in enumerate(chips)]


def _share_plan(bufs):
    x, y, c, _ = _place()
    copies = []
    for buf in bufs:
        hr = buf.shape[0] // 2
        mine = buf.at[pl.ds(pl.multiple_of(c * hr, 8), hr)]
        copies.append((mine, mine, (x, y, 1 - c)))
    return copies


N_DEVICES = 8


def _slot_place(vec, place):
    R, C = vec.shape

    def body(p_ref, v_ref, o_ref):
        o_ref[0] = v_ref[...]

    grid_spec = pltpu.PrefetchScalarGridSpec(
        num_scalar_prefetch=1, grid=(1,), in_specs=[pl.BlockSpec((R, C), lambda i, p: (0, 0))],
        out_specs=pl.BlockSpec((1, R, C), lambda i, p: (2 * p[0] + p[1], 0, 0)))
    return pl.pallas_call(
        body, name="slot_place", grid_spec=grid_spec, out_shape=jax.ShapeDtypeStruct((N_DEVICES, R, C), F32),
        compiler_params=_params(dimension_semantics=("arbitrary",)),
    )(place, vec)


def _slots_plan(bufs):
    x, y, c, _ = _place()
    mine = bufs[0].at[4 * x + 2 * y + c]
    return [(mine, mine, (x ^ (r >> 2), y ^ ((r >> 1) & 1), c ^ (r & 1))) for r in range(1, N_DEVICES)]


def _sum_slots(slots):
    _, R, C = slots.shape

    def body(s_ref, o_ref):
        acc = s_ref[0]
        for d in range(1, N_DEVICES):
            acc = acc + s_ref[d]
        o_ref[...] = acc

    return pl.pallas_call(
        body, name="sum_slots", in_specs=[pl.BlockSpec(memory_space=pltpu.VMEM)],
        out_specs=pl.BlockSpec(memory_space=pltpu.VMEM), out_shape=jax.ShapeDtypeStruct((R, C), F32),
        compiler_params=_params(),
    )(slots)


SMALL = ("ffn1_norm", "mix_norm", "pool_w", "pool_scale", "q_norm", "k_norm", "sinks", "gate_bias", "ffn2_norm")
SMALL_COLS = 1024
FFN1 = ("ffn1_w_gate", "ffn1_w_up", "ffn1_w_down")
MIXER = ("w_in", "w_pool_out", "w_attn_out", "w_out")
FFN2 = ("ffn2_w_gate", "ffn2_w_up", "ffn2_w_down")
LARGE = FFN1 + MIXER + FFN2
TRANSPOSED = ("ffn1_w_gate", "ffn1_w_up", "w_in", "w_pool_out", "ffn2_w_gate", "ffn2_w_up")
WEIGHTS = ("ffn1_norm", "ffn1_w_gate", "ffn1_w_up", "ffn1_w_down", "mix_norm", "w_in", "pool_w", "pool_scale",
           "w_pool_out", "q_norm", "k_norm", "sinks", "w_attn_out", "gate_bias", "w_out", "ffn2_norm",
           "ffn2_w_gate", "ffn2_w_up", "ffn2_w_down")


def _pack_small(parts):
    flat = jnp.concatenate([p.reshape(-1) for p in parts])
    rows = -(-flat.shape[0] // (8 * SMALL_COLS)) * 8
    return jnp.pad(flat, (0, rows * SMALL_COLS - flat.shape[0])).reshape(rows, SMALL_COLS)


def _unpack_small(packed, like):
    flat = packed.reshape(-1)
    out, off = [], 0
    for p in like:
        out.append(flat[off:off + p.size].reshape(p.shape))
        off += p.size
    return out, flat[off]


def _tie(small, token):
    return small + token[0, 0]


class Reduction:
    def __init__(self, names, grads, place):
        self.names, self.place = names, place
        self.grads = [g.reshape(N_SHARDS, -1, g.shape[-1]) for g in grads]

    def swap(self):
        lands = [lax.empty((g.shape[0], g.shape[1] // 2, g.shape[2]), g.dtype) for g in self.grads]
        return _send(self.grads + lands, _swap_plan, len(self.grads))

    def exchange(self, swapped):
        n = len(self.names)
        halves = self._per_shape(_sum_halves, "sum_halves", swapped[:n], swapped[n:])
        lands = [lax.empty((N_SHARDS - 1,) + h.shape[1:], h.dtype) for h in halves]
        return _send(halves + lands, _exchange_plan, (N_SHARDS - 1) * n)

    def share(self, exchanged):
        n = len(self.names)
        return _send(self._per_shape(_sum_quarters, "sum_quarters", exchanged[:n], exchanged[n:]), _share_plan, n)

    def named(self, shared):
        return dict(zip(self.names, shared))

    def _per_shape(self, add, stage, mine, received):
        out = [None] * len(mine)
        for idx in _by_shape(mine):
            sums = add([mine[k] for k in idx], [received[k] for k in idx], self.place, f"{stage}_{self.names[idx[0]]}")
            for k, v in zip(idx, sums):
                out[k] = v
        return out


def _row_form(name, a):
    return a.T if name in TRANSPOSED else a


def kernel(x, ffn1_norm, ffn1_w_gate, ffn1_w_up, ffn1_w_down, mix_norm, w_in, pool_w, pool_scale, w_pool_out, q_norm, k_norm, sinks, w_attn_out, gate_bias, w_out, ffn2_norm, ffn2_w_gate, ffn2_w_up, ffn2_w_down, loss_target, m_ffn1_norm, m_ffn1_w_gate, m_ffn1_w_up, m_ffn1_w_down, m_mix_norm, m_w_in, m_pool_w, m_pool_scale, m_w_pool_out, m_q_norm, m_k_norm, m_sinks, m_w_attn_out, m_gate_bias, m_w_out, m_ffn2_norm, m_ffn2_w_gate, m_ffn2_w_up, m_ffn2_w_down, v_ffn1_norm, v_ffn1_w_gate, v_ffn1_w_up, v_ffn1_w_down, v_mix_norm, v_w_in, v_pool_w, v_pool_scale, v_w_pool_out, v_q_norm, v_k_norm, v_sinks, v_w_attn_out, v_gate_bias, v_w_out, v_ffn2_norm, v_ffn2_w_gate, v_ffn2_w_up, v_ffn2_w_down):
    args = dict(locals())
    wts = {n: _row_form(n, args[n]) for n in WEIGHTS}
    mom = {n: _row_form(n, args["m_" + n]) for n in WEIGHTS}
    var = {n: _row_form(n, args["v_" + n]) for n in WEIGHTS}
    shard = 2 * lax.axis_index("x") + lax.axis_index("y")
    place = jnp.stack([shard, lax.axis_index("c")]).astype(jnp.int32)

    xs, target = x[0], loss_target[0]
    D = xs.shape[1]
    g1 = wts["ffn1_norm"].reshape(1, D)
    gm = wts["mix_norm"].reshape(1, D)
    g2 = wts["ffn2_norm"].reshape(1, D)
    qw = wts["q_norm"].reshape(1, HEAD_DIM)
    kw = wts["k_norm"].reshape(1, HEAD_DIM)
    bias = wts["gate_bias"].reshape(1, 2 * D)
    pscale = wts["pool_scale"].reshape(1, -1)
    pw = wts["pool_w"].astype(BF16)
    sink_rows = jnp.broadcast_to(jnp.repeat(wts["sinks"], ATTN_BLOCK).reshape(N_KV_HEADS, GQA_GROUP * ATTN_BLOCK, 1),
                                 (N_KV_HEADS, GQA_GROUP * ATTN_BLOCK, 128))

    gate1, up1, down1 = FFN1[:1], FFN1[1:2], FFN1[2:]
    mix_in, mix_out = MIXER[:1], MIXER[1:]

    def over_ici(names):
        return _send([placed[n] for n in names], _gather_ici_plan, 3 * len(names))

    def to_sibling(flight):
        return _forward(flight, _gather_d2d_plan, len(flight.bufs) * 3)

    placed = {n: _cast_place(wts[n], place, "cast_" + n) for n in gate1}
    gate1_ici, = _comm("gather_gate1_ici", [over_ici(gate1)], place)
    placed.update({n: _cast_place(wts[n], place, "cast_" + n) for n in up1 + down1})
    up1_ici, down1_ici = _comm("gather_ffn1_ici", [over_ici(up1), over_ici(down1)], gate1_ici.token)
    placed.update({n: _cast_place(wts[n], place, "cast_" + n) for n in MIXER + FFN2})
    zero = jnp.zeros((1,), F32)
    packed = [_pack_small([d[n] for n in SMALL] + [zero]) for d in (wts, mom, var)]
    band = _band_bias()
    shadow = packed + [sink_rows, pw, band, down1_ici.token]
    gate1_d2d, in_ici, out_ici, ffn2_ici = _comm(
        "gather_rest_ici", [to_sibling(gate1_ici), over_ici(mix_in), over_ici(mix_out), over_ici(FFN2)], shadow)
    w = dict(zip(gate1, _comm("gather_gate1_wait", [_land(gate1_d2d)], gate1_d2d.token)[0]))

    n1, a1 = _ffn_gate(xs, g1, w["ffn1_w_gate"], "ffn1_gate")
    up1_d2d, down1_d2d = _comm("gather_ffn1_d2d", [to_sibling(up1_ici), to_sibling(down1_ici)], a1)
    w.update(zip(up1, _comm("gather_up1_wait", [_land(up1_d2d)], up1_d2d.token)[0]))
    b1, s1 = _ffn_hidden(n1, a1, w["ffn1_w_up"], "ffn1_hidden")
    w.update(zip(down1, _comm("gather_down1_wait", [_land(down1_d2d)], s1)[0]))
    h1 = _ffn_down(xs, s1, w["ffn1_w_down"], "ffn1_down")
    in_d2d, out_d2d = _comm("gather_mix_d2d", [to_sibling(in_ici), to_sibling(out_ici)], h1)
    w.update(zip(mix_in, _comm("gather_in_wait", [_land(in_d2d)], in_d2d.token)[0]))
    u, xp, q, kv, gl = _mix_proj(h1, gm, w["w_in"])
    attn = _attn_fwd(q, kv, qw, kw, sink_rows, band)
    ffn2_d2d, landed = _comm("gather_ffn2_d2d", [to_sibling(ffn2_ici), _land(out_d2d)], attn)
    w.update(zip(mix_out, landed))
    h2, bp, ba = _mix_out(xp, attn, gl, bias, pw, pscale, w["w_pool_out"], w["w_attn_out"], w["w_out"], h1)
    w.update(zip(FFN2, _comm("gather_ffn2_wait", [_land(ffn2_d2d)], h2)[0]))
    n2, a2, b2, s2 = _ffn_up(h2, g2, w["ffn2_w_gate"], w["ffn2_w_up"], "ffn2_up")
    dy, dyh, loss = _ffn_down_loss(h2, s2, w["ffn2_w_down"], target, "ffn2_down_loss")

    gw, gs = {}, {}
    da, db = _ffn_bwd_h(dyh, a2, b2, w["ffn2_w_down"], ffn2_d2d.token, "ffn2_bwd_h")
    gw["ffn2_w_gate"], gw["ffn2_w_up"] = _xty(da, n2, "ffn2_dw_gate"), _xty(db, n2, "ffn2_dw_up")
    gw["ffn2_w_down"] = _xty(s2, dyh, "ffn2_dw_down")
    red2 = Reduction(FFN2, [gw[n] for n in FFN2], place)
    swap, = _comm("reduce_ffn2_swap", [red2.swap()], da)
    dh2, gs["ffn2_norm"] = _ffn_bwd_x(da, db, w["ffn2_w_gate"], w["ffn2_w_up"], h2, dy, _tie(g2, swap.token), "ffn2_bwd_x")
    swapped, = _comm("reduce_ffn2_swap_wait", [_land(swap)], dh2)
    exchange, = _comm("reduce_ffn2_exchange", [red2.exchange(swapped)], place)
    dgl, dbp, dba, gw["w_out"], gs["gate_bias"] = _mix_bwd_gate(dh2, w["w_out"], bp, ba, gl, _tie(bias, exchange.token))
    dattn, dpooled, gw["w_attn_out"], gw["w_pool_out"], gs["pool_w"], gs["pool_scale"] = _mix_bwd_branch(
        dbp, dba, attn, xp, pw, pscale, w["w_pool_out"], w["w_attn_out"])
    dq, dkv, gs["q_norm"], gs["k_norm"], dsk = _attn_bwd(q, kv, dattn, qw, kw, sink_rows, band)
    gs["sinks"] = dsk[:, :, 0]
    exchanged, = _comm("reduce_ffn2_exchange_wait", [_land(exchange)], dq)
    share2, = _comm("reduce_ffn2_share", [red2.share(exchanged)], place)
    pieces = (_pool_bwd(dpooled), dq, dkv, dgl)
    gw["w_in"] = _in_bwd_w(pieces, u)
    redm = Reduction(MIXER, [gw[n] for n in MIXER], place)
    swap, shared = _comm("reduce_mix_swap", [redm.swap(), _land(share2)], share2.token)
    grads = red2.named(shared)
    dh1, dh1h, gs["mix_norm"] = _in_bwd_x(pieces, w["w_in"], h1, dh2, _tie(gm, swap.token))
    swapped, = _comm("reduce_mix_swap_wait", [_land(swap)], dh1)
    exchange, = _comm("reduce_mix_exchange", [redm.exchange(swapped)], place)
    da, db = _ffn_bwd_h(dh1h, a1, b1, w["ffn1_w_down"], exchange.token, "ffn1_bwd_h")
    gw["ffn1_w_gate"], gw["ffn1_w_up"] = _xty(da, n1, "ffn1_dw_gate"), _xty(db, n1, "ffn1_dw_up")
    gw["ffn1_w_down"] = _xty(s1, dh1h, "ffn1_dw_down")
    exchanged, = _comm("reduce_mix_exchange_wait", [_land(exchange)], [gw[n] for n in FFN1])
    red1 = Reduction(FFN1, [gw[n] for n in FFN1], place)
    sharem, swap = _comm("reduce_mix_share", [redm.share(exchanged), red1.swap()], place)
    grad_x, gs["ffn1_norm"] = _ffn_bwd_x(da, db, w["ffn1_w_gate"], w["ffn1_w_up"], xs, dh1, _tie(g1, swap.token), "ffn1_bwd_x")

    small_parts = [gs[n] for n in SMALL] + [loss[0, 0].reshape(1)]
    slots, shared, swapped = _comm(
        "gather_small", [_send([_slot_place(_pack_small(small_parts), place)], _slots_plan, N_DEVICES - 1), _land(sharem),
                         _land(swap)], grad_x)
    grads.update(redm.named(shared))
    exchange, = _comm("reduce_ffn1_exchange", [red1.exchange(swapped)], slots.token)
    token = exchange.token
    delta, new_m, new_v = {}, {}, {}
    for n in FFN2 + MIXER:
        delta[n], new_m[n], new_v[n], grads[n] = _adamw(wts[n], grads[n], mom[n], var[n], token, "adamw_" + n)
    summed = _sum_slots(_comm("gather_small_wait", [_land(slots)], delta[MIXER[-1]])[0][0])
    small_grads, loss_sum = _unpack_small(summed, [wts[n] for n in SMALL])
    grads.update(dict(zip(SMALL, small_grads)))
    ds, ms, vs, _ = _adamw(packed[0], summed, packed[1], packed[2], token, "adamw_small")
    like = [wts[n] for n in SMALL]
    for out, packed_out in ((delta, ds), (new_m, ms), (new_v, vs)):
        out.update(dict(zip(SMALL, _unpack_small(packed_out, like)[0])))
    results = {(k, n): _row_form(n, d[n]) for k, d in enumerate((grads, delta, new_m, new_v)) for n in SMALL + FFN2 + MIXER}
    exchanged, = _comm("reduce_ffn1_exchange_wait", [_land(exchange)],
                       [results[k, n] for k in range(4) for n in SMALL + ("w_pool_out",)])
    share1, = _comm("reduce_ffn1_share", [red1.share(exchanged)], place)
    shared, = _comm("reduce_ffn1_share_wait", [_land(share1)], share1.token)
    grads.update(red1.named(shared))
    for n in FFN1:
        delta[n], new_m[n], new_v[n], grads[n] = _adamw(wts[n], grads[n], mom[n], var[n], token, "adamw_" + n)
    results.update({(k, n): _row_form(n, d[n]) for k, d in enumerate((grads, delta, new_m, new_v)) for n in FFN1})
    return (loss_sum, grad_x[None], *[results[k, n] for k in range(4) for n in WEIGHTS])
```

```python
from typing import Callable, NamedTuple, Optional

import jax
import jax.numpy as jnp
from jax import lax
from jax.experimental import pallas as pl
from jax.experimental.pallas import tpu as pltpu

F32 = jnp.float32
BF16 = jnp.bfloat16
RMS_EPS = 1e-6
POOL_WINDOWS = (2, 4, 8, 16)
POOL_GROUP = 128
POOL_HALO = 16
HEAD_DIM = 64
GQA_GROUP = 8
N_KV_HEADS = 2
ATTN_BLOCK = 128
SCALE = HEAD_DIM ** -0.5
NEG = -1e30
N_SHARDS = 4
ADAM_LR, ADAM_B1, ADAM_B2, ADAM_EPS, ADAM_WD, ADAM_STEP = 0.001, 0.9, 0.999, 1e-08, 0.01, 10
VMEM_LIMIT = 56 * 1024 * 1024
MESH = pl.DeviceIdType.MESH
SEG_POOL, SEG_Q, SEG_KV, SEG_GATE = (0, 512), (512, 1024), (1536, 256), (1792, 2048)
SEGMENTS = (SEG_POOL, SEG_Q, SEG_KV, SEG_GATE)


def _params(**kw):
    return pltpu.CompilerParams(vmem_limit_bytes=VMEM_LIMIT, **kw)


def _dot(a, b):
    return jnp.dot(a, b, preferred_element_type=F32)


def _dot_nt(a, b):
    return lax.dot_general(a, b, (((1,), (1,)), ((), ())), preferred_element_type=F32)


def _dot_tn(a, b):
    return lax.dot_general(a, b, (((0,), (0,)), ((), ())), preferred_element_type=F32)


def _rinv(x):
    return lax.rsqrt(jnp.mean(x * x, axis=-1, keepdims=True) + RMS_EPS)


def _rms_bwd(dn, x, g):
    r = _rinv(x)
    xr = x * r
    z = dn * g
    dx = r * (z - xr * jnp.mean(z * xr, axis=-1, keepdims=True))
    return dx, dn * xr


def _acc(ref, val, first):
    @pl.when(first)
    def _():
        ref[...] = val

    @pl.when(jnp.logical_not(first))
    def _():
        ref[...] += val


TOKEN_SPEC = pl.BlockSpec(memory_space=pl.ANY)
F_HALF = 1408


def _resident(w):
    return pl.BlockSpec(w.shape, lambda i: (0, 0), pipeline_mode=pl.Buffered(1))


def _ffn_up(h, gnorm, wgt, wut, name, tm=512):
    S, D = h.shape
    F = wgt.shape[0]

    def body(h_ref, g_ref, wg_ref, wu_ref, n_ref, a_ref, b_ref, s_ref):
        x = h_ref[...]
        n = (x * _rinv(x) * g_ref[...]).astype(BF16)
        n_ref[...] = n
        for lo in range(0, F, F_HALF):
            cols = slice(lo, lo + F_HALF)
            a = _dot_nt(n, wg_ref[cols, :])
            b = _dot_nt(n, wu_ref[cols, :])
            a_ref[:, cols] = a.astype(BF16)
            b_ref[:, cols] = b.astype(BF16)
            s_ref[:, cols] = (a * jax.nn.sigmoid(a) * b).astype(BF16)

    act = pl.BlockSpec((tm, F), lambda i: (i, 0))
    hidden = jax.ShapeDtypeStruct((S, F), BF16)
    return pl.pallas_call(
        body, name=name, grid=(S // tm,),
        in_specs=[pl.BlockSpec((tm, D), lambda i: (i, 0)), pl.BlockSpec((1, D), lambda i: (0, 0)), _resident(wgt), _resident(wut)],
        out_specs=[pl.BlockSpec((tm, D), lambda i: (i, 0)), act, act, act],
        out_shape=[jax.ShapeDtypeStruct((S, D), BF16), hidden, hidden, hidden],
        compiler_params=_params(dimension_semantics=("arbitrary",)),
    )(h, gnorm, wgt, wut)


def _ffn_gate(h, gnorm, wgt, name, tm=512):
    S, D = h.shape
    F = wgt.shape[0]

    def body(h_ref, g_ref, wg_ref, n_ref, a_ref):
        x = h_ref[...]
        n = (x * _rinv(x) * g_ref[...]).astype(BF16)
        n_ref[...] = n
        for lo in range(0, F, F_HALF):
            a_ref[:, lo:lo + F_HALF] = _dot_nt(n, wg_ref[lo:lo + F_HALF, :]).astype(BF16)

    row = pl.BlockSpec((tm, D), lambda i: (i, 0))
    return pl.pallas_call(
        body, name=name, grid=(S // tm,), in_specs=[row, pl.BlockSpec((1, D), lambda i: (0, 0)), _resident(wgt)],
        out_specs=[row, pl.BlockSpec((tm, F), lambda i: (i, 0))],
        out_shape=[jax.ShapeDtypeStruct((S, D), BF16), jax.ShapeDtypeStruct((S, F), BF16)],
        compiler_params=_params(dimension_semantics=("arbitrary",)),
    )(h, gnorm, wgt)


def _ffn_hidden(n, a, wut, name, tm=512):
    S, D = n.shape
    F = wut.shape[0]

    def body(n_ref, a_ref, wu_ref, b_ref, s_ref):
        n_t = n_ref[...]
        for lo in range(0, F, F_HALF):
            cols = slice(lo, lo + F_HALF)
            a = a_ref[:, cols].astype(F32)
            b = _dot_nt(n_t, wu_ref[cols, :])
            b_ref[:, cols] = b.astype(BF16)
            s_ref[:, cols] = (a * jax.nn.sigmoid(a) * b).astype(BF16)

    act = pl.BlockSpec((tm, F), lambda i: (i, 0))
    hidden = jax.ShapeDtypeStruct((S, F), BF16)
    return pl.pallas_call(
        body, name=name, grid=(S // tm,), in_specs=[pl.BlockSpec((tm, D), lambda i: (i, 0)), act, _resident(wut)],
        out_specs=[act, act], out_shape=[hidden, hidden],
        compiler_params=_params(dimension_semantics=("arbitrary",)),
    )(n, a, wut)


def _ffn_down(h, s, wd, name, tm=1024):
    S, D = h.shape
    F = wd.shape[0]

    def body(h_ref, s_ref, wd_ref, o_ref):
        o_ref[...] = h_ref[...] + 0.5 * _dot(s_ref[...], wd_ref[...])

    row = pl.BlockSpec((tm, D), lambda i: (i, 0))
    return pl.pallas_call(
        body, name=name, grid=(S // tm,), in_specs=[row, pl.BlockSpec((tm, F), lambda i: (i, 0)), _resident(wd)],
        out_specs=row, out_shape=jax.ShapeDtypeStruct((S, D), F32),
        compiler_params=_params(dimension_semantics=("arbitrary",)),
    )(h, s, wd)


def _ffn_down_loss(h, s, wd, target, name, tm=512):
    S, D = h.shape
    F = wd.shape[0]

    def body(h_ref, s_ref, wd_ref, t_ref, dy_ref, dyh_ref, loss_ref):
        e = h_ref[...] + 0.5 * _dot(s_ref[...], wd_ref[...]) - t_ref[...]
        dy = e * (1.0 / D)
        dy_ref[...] = dy
        dyh_ref[...] = (0.5 * dy).astype(BF16)
        tot = jnp.sum(jnp.sum(e * e, axis=1, keepdims=True), axis=0, keepdims=True) * (0.5 / D)
        _acc(loss_ref, jnp.broadcast_to(tot, loss_ref.shape), pl.program_id(0) == 0)

    row = pl.BlockSpec((tm, D), lambda i: (i, 0))
    return pl.pallas_call(
        body, name=name, grid=(S // tm,), in_specs=[row, pl.BlockSpec((tm, F), lambda i: (i, 0)), _resident(wd), row],
        out_specs=[row, row, pl.BlockSpec((8, 128), lambda i: (0, 0))],
        out_shape=[jax.ShapeDtypeStruct((S, D), F32), jax.ShapeDtypeStruct((S, D), BF16), jax.ShapeDtypeStruct((8, 128), F32)],
        compiler_params=_params(dimension_semantics=("arbitrary",)),
    )(h, s, wd, target)


def _ffn_bwd_h(dyh, a, b, wd, token, name, tm=512):
    S, D = dyh.shape
    F = wd.shape[0]

    def body(dyh_ref, a_ref, b_ref, wd_ref, _, da_ref, db_ref):
        dyh_t = dyh_ref[...]
        for lo in range(0, F, F_HALF):
            cols = slice(lo, lo + F_HALF)
            av = a_ref[:, cols].astype(F32)
            bv = b_ref[:, cols].astype(F32)
            ds = _dot_nt(dyh_t, wd_ref[cols, :])
            sig = jax.nn.sigmoid(av)
            silu = av * sig
            da_ref[:, cols] = (ds * bv * (sig * (1.0 + av * (1.0 - sig)))).astype(BF16)
            db_ref[:, cols] = (ds * silu).astype(BF16)

    act = pl.BlockSpec((tm, F), lambda i: (i, 0))
    hidden = jax.ShapeDtypeStruct((S, F), BF16)
    return pl.pallas_call(
        body, name=name, grid=(S // tm,),
        in_specs=[pl.BlockSpec((tm, D), lambda i: (i, 0)), act, act, _resident(wd), TOKEN_SPEC],
        out_specs=[act, act], out_shape=[hidden, hidden],
        compiler_params=_params(dimension_semantics=("arbitrary",)),
    )(dyh, a, b, wd, token)


def _xty(x, y, name, tk=2048, tf=1408):
    S, F = x.shape
    D = y.shape[1]

    def body(x_ref, y_ref, o_ref):
        _acc(o_ref, _dot_tn(x_ref[...], y_ref[...]), pl.program_id(1) == 0)

    return pl.pallas_call(
        body, name=name, grid=(F // tf, S // tk),
        in_specs=[pl.BlockSpec((tk, tf), lambda j, k: (k, j)), pl.BlockSpec((tk, D), lambda j, k: (k, 0))],
        out_specs=pl.BlockSpec((tf, D), lambda j, k: (j, 0)), out_shape=jax.ShapeDtypeStruct((F, D), F32),
        compiler_params=_params(dimension_semantics=("arbitrary", "arbitrary")),
    )(x, y)


def _ffn_bwd_x(da, db, wgt, wut, x, dh, gnorm, name, tm=512):
    S, D = x.shape
    F = wgt.shape[0]

    def body(da_ref, db_ref, wg_ref, wu_ref, x_ref, dh_ref, g_ref, dx_ref, dg_ref):
        dn = _dot(da_ref[...], wg_ref[...]) + _dot(db_ref[...], wu_ref[...])
        dx, dg_rows = _rms_bwd(dn, x_ref[...], g_ref[...])
        dx_ref[...] = dh_ref[...] + dx
        _acc(dg_ref, jnp.sum(dg_rows, axis=0, keepdims=True), pl.program_id(0) == 0)

    row = pl.BlockSpec((tm, D), lambda i: (i, 0))
    act = pl.BlockSpec((tm, F), lambda i: (i, 0))
    vec = pl.BlockSpec((1, D), lambda i: (0, 0))
    return pl.pallas_call(
        body, name=name, grid=(S // tm,), in_specs=[act, act, _resident(wgt), _resident(wut), row, row, vec],
        out_specs=[row, vec],
        out_shape=[jax.ShapeDtypeStruct((S, D), F32), jax.ShapeDtypeStruct((1, D), F32)],
        compiler_params=_params(dimension_semantics=("arbitrary",)),
    )(da, db, wgt, wut, x, dh, gnorm)


def _mix_proj(h, gnorm, wint, tm=512):
    S, D = h.shape
    nt = S // tm

    def body(h_ref, g_ref, w_ref, u_ref, xp_ref, q_ref, kv_ref, gl_ref):
        x = h_ref[...]
        u = (x * _rinv(x) * g_ref[...]).astype(BF16)
        u_ref[...] = u
        for (off, width), ref in zip(SEGMENTS, (xp_ref, q_ref, kv_ref, gl_ref)):
            ref[...] = _dot_nt(u, w_ref[off:off + width, :]).astype(ref.dtype)

    def row(width):
        return pl.BlockSpec((tm, width), lambda i: (i, 0))

    return pl.pallas_call(
        body, name="mix_proj", grid=(nt,),
        in_specs=[row(D), pl.BlockSpec((1, D), lambda i: (0, 0)), pl.BlockSpec(wint.shape, lambda i: (0, 0))],
        out_specs=[row(D), row(SEG_POOL[1]), row(SEG_Q[1]), row(SEG_KV[1]), row(SEG_GATE[1])],
        out_shape=[jax.ShapeDtypeStruct((S, D), BF16), jax.ShapeDtypeStruct((S, SEG_POOL[1]), F32),
                   jax.ShapeDtypeStruct((S, SEG_Q[1]), BF16), jax.ShapeDtypeStruct((S, SEG_KV[1]), BF16),
                   jax.ShapeDtypeStruct((S, SEG_GATE[1]), BF16)],
        compiler_params=_params(dimension_semantics=("arbitrary",)),
    )(h, gnorm, wint)


def _stack_heads(x, g):
    return jnp.concatenate([x[:, (GQA_GROUP * g + hh) * HEAD_DIM:(GQA_GROUP * g + hh + 1) * HEAD_DIM]
                            for hh in range(GQA_GROUP)], axis=0)


def _unstack_heads(ref, val, g):
    for hh in range(GQA_GROUP):
        lo = (GQA_GROUP * g + hh) * HEAD_DIM
        ref[:, lo:lo + HEAD_DIM] = val[hh * ATTN_BLOCK:(hh + 1) * ATTN_BLOCK, :]


def _rowsum(xb, width):
    return _dot(xb, jnp.ones((xb.shape[1], width), BF16))


def _rinv_lanes(x):
    return lax.rsqrt(_rowsum((x * x).astype(BF16), x.shape[1]) * (1.0 / x.shape[1]) + RMS_EPS)


def _twice(x):
    return jnp.concatenate([x, x], axis=1)


def _band_bias():
    qi = (jnp.arange(GQA_GROUP * ATTN_BLOCK) % ATTN_BLOCK)[:, None]
    kj = jnp.arange(2 * ATTN_BLOCK)[None, :]
    return jnp.where(jnp.logical_and(kj > qi, kj <= qi + ATTN_BLOCK), 0.0, NEG).astype(F32)


def _attn_exp(qn, kk, sink, band, n):
    kj = lax.broadcasted_iota(jnp.int32, (1, 2 * ATTN_BLOCK), 1)
    before_sequence = jnp.where(jnp.logical_and(kj < ATTN_BLOCK, n == 0), NEG, 0.0)
    s = _dot_nt(qn, kk) + band + before_sequence
    m = jnp.maximum(jnp.broadcast_to(jnp.max(s, axis=-1, keepdims=True), sink.shape), sink)
    e = jnp.exp(s - _twice(m))
    e_s = jnp.exp(sink - m)
    e_b = e.astype(BF16)
    inv = 1.0 / (_rowsum(e_b, ATTN_BLOCK) + e_s)
    return e_b, e, e_s, inv


def _attn_blocks(n):
    cur = pl.multiple_of(n * ATTN_BLOCK, ATTN_BLOCK)
    prev = pl.multiple_of(jnp.maximum(n - 1, 0) * ATTN_BLOCK, ATTN_BLOCK)
    return cur, prev


def _kv_window(kv_ref, n):
    cur, prev = _attn_blocks(n)
    return jnp.concatenate([kv_ref[pl.ds(prev, ATTN_BLOCK), :], kv_ref[pl.ds(cur, ATTN_BLOCK), :]], axis=0).astype(F32)


def _kv_split(kv, g):
    k = kv[:, g * HEAD_DIM:(g + 1) * HEAD_DIM]
    v = kv[:, (N_KV_HEADS + g) * HEAD_DIM:(N_KV_HEADS + g + 1) * HEAD_DIM]
    return k, v


def _attn_fwd(q, kv, qw, kw, sink_rows, band):
    S, W = q.shape
    nb = S // ATTN_BLOCK

    def body(q_ref, kv_ref, qw_ref, kw_ref, sk_ref, band_ref, o_ref, o_scr):
        n = pl.program_id(0)
        qf = q_ref[...].astype(F32)
        kvw = _kv_window(kv_ref, n)
        for g in range(N_KV_HEADS):
            qs = _stack_heads(qf, g)
            qn = (qs * _rinv_lanes(qs) * qw_ref[...] * SCALE).astype(BF16)
            k, v = _kv_split(kvw, g)
            kk = (k * _rinv_lanes(k) * kw_ref[...]).astype(BF16)
            e_b, _, _, inv = _attn_exp(qn, kk, sk_ref[g], band_ref[...], n)
            _unstack_heads(o_scr, _dot(e_b, v.astype(BF16)) * inv[:, :HEAD_DIM], g)
        o_ref[...] = o_scr[...].astype(BF16)

    blk = pl.BlockSpec((ATTN_BLOCK, W), lambda n: (n, 0))
    return pl.pallas_call(
        body, name="attn_fwd", grid=(nb,),
        in_specs=[blk, pl.BlockSpec(kv.shape, lambda n: (0, 0)), pl.BlockSpec((1, HEAD_DIM), lambda n: (0, 0)),
                  pl.BlockSpec((1, HEAD_DIM), lambda n: (0, 0)), pl.BlockSpec(sink_rows.shape, lambda n: (0, 0, 0)),
                  pl.BlockSpec(band.shape, lambda n: (0, 0))],
        out_specs=blk, out_shape=jax.ShapeDtypeStruct((S, W), BF16),
        scratch_shapes=[pltpu.VMEM((ATTN_BLOCK, W), F32)],
        compiler_params=_params(dimension_semantics=("arbitrary",)),
    )(q, kv, qw, kw, sink_rows, band)


def _attn_bwd(q, kv, do, qw, kw, sink_rows, band):
    S, W = q.shape
    KW = kv.shape[1]
    nb = S // ATTN_BLOCK
    chunk = 512

    def body(q_ref, kv_ref, do_ref, qw_ref, kw_ref, sk_ref, band_ref, dq_ref, dkv_ref, dqw_ref, dkw_ref, dsk_ref, dq_scr):
        n = pl.program_id(0)

        @pl.when(n == 0)
        def _():
            dkv_ref[...] = jnp.zeros_like(dkv_ref)
            dqw_ref[...] = jnp.zeros_like(dqw_ref)
            dsk_ref[...] = jnp.zeros_like(dsk_ref)

        qf = q_ref[...].astype(F32)
        dof = do_ref[...].astype(F32)
        cur, prev = _attn_blocks(n)
        kvw = _kv_window(kv_ref, n)
        qw_v = qw_ref[...]
        for g in range(N_KV_HEADS):
            qs = _stack_heads(qf, g)
            rq = _rinv_lanes(qs)
            qhat = qs * rq
            qn = (qhat * qw_v * SCALE).astype(BF16)
            k, v = _kv_split(kvw, g)
            kk = (k * _rinv_lanes(k) * kw_ref[...]).astype(BF16)
            vv = v.astype(BF16)
            dos = _stack_heads(dof, g).astype(BF16)
            _, e, e_s, inv = _attn_exp(qn, kk, sk_ref[g], band_ref[...], n)
            p = e * _twice(inv)
            dp = _dot_nt(dos, vv)
            drow = _rowsum((p * dp).astype(BF16), ATTN_BLOCK)
            ds = (p * (dp - _twice(drow))).astype(BF16)
            dsink = -(e_s * inv * drow)
            for hh in range(GQA_GROUP):
                tot = jnp.sum(dsink[hh * ATTN_BLOCK:(hh + 1) * ATTN_BLOCK, :], axis=0, keepdims=True)
                dsk_ref[g, hh:hh + 1, :] += tot
            dqn = _dot(ds, kk) * SCALE
            dkk = _dot_tn(ds, qn)
            dvv = _dot_tn(p.astype(BF16), dos)
            klo, vlo = g * HEAD_DIM, (N_KV_HEADS + g) * HEAD_DIM
            for start, rows in ((prev, slice(0, ATTN_BLOCK)), (cur, slice(ATTN_BLOCK, 2 * ATTN_BLOCK))):
                dkv_ref[pl.ds(start, ATTN_BLOCK), klo:klo + HEAD_DIM] += dkk[rows]
                dkv_ref[pl.ds(start, ATTN_BLOCK), vlo:vlo + HEAD_DIM] += dvv[rows]
            dqw_ref[...] += jnp.sum(dqn * qhat, axis=0, keepdims=True)
            z = dqn * qw_v
            dqs = rq * (z - qhat * (_rowsum((z * qhat).astype(BF16), HEAD_DIM) * (1.0 / HEAD_DIM)))
            _unstack_heads(dq_scr, dqs, g)
        dq_ref[...] = dq_scr[...].astype(BF16)

        @pl.when(n == nb - 1)
        def _():
            def one(c, dkw):
                rows = pl.ds(pl.multiple_of(c * chunk, chunk), chunk)
                for g in range(N_KV_HEADS):
                    lo = g * HEAD_DIM
                    k = kv_ref[rows, lo:lo + HEAD_DIM].astype(F32)
                    dx, dg_rows = _rms_bwd(dkv_ref[rows, lo:lo + HEAD_DIM], k, kw_ref[...])
                    dkv_ref[rows, lo:lo + HEAD_DIM] = dx
                    dkw = dkw + jnp.sum(dg_rows, axis=0, keepdims=True)
                return dkw

            dkw_ref[...] = lax.fori_loop(0, S // chunk, one, jnp.zeros((1, HEAD_DIM), F32))

    blk = pl.BlockSpec((ATTN_BLOCK, W), lambda n: (n, 0))
    whole_kv = pl.BlockSpec((S, KW), lambda n: (0, 0))
    vec = pl.BlockSpec((1, HEAD_DIM), lambda n: (0, 0))
    sk = pl.BlockSpec(sink_rows.shape, lambda n: (0, 0, 0))
    dsk = pl.BlockSpec((N_KV_HEADS, GQA_GROUP, 128), lambda n: (0, 0, 0))
    return pl.pallas_call(
        body, name="attn_bwd", grid=(nb,), in_specs=[blk, whole_kv, blk, vec, vec, sk, pl.BlockSpec(band.shape, lambda n: (0, 0))],
        out_specs=[blk, whole_kv, vec, vec, dsk],
        out_shape=[jax.ShapeDtypeStruct((S, W), BF16), jax.ShapeDtypeStruct((S, KW), F32),
                   jax.ShapeDtypeStruct((1, HEAD_DIM), F32), jax.ShapeDtypeStruct((1, HEAD_DIM), F32),
                   jax.ShapeDtypeStruct((N_KV_HEADS, GQA_GROUP, 128), F32)],
        scratch_shapes=[pltpu.VMEM((ATTN_BLOCK, W), F32)],
        compiler_params=_params(dimension_semantics=("arbitrary",)),
    )(q, kv, do, qw, kw, sink_rows, band)


def _pooled(xc, xprev, i):
    tm = xc.shape[0]
    xh = jnp.concatenate([jnp.where(i > 0, xprev, 0.0), xc], axis=0)
    t = lax.broadcasted_iota(jnp.int32, (tm, 1), 0) + i * tm
    out = []
    for gi, w in enumerate(POOL_WINDOWS):
        acc = xh[:, gi * POOL_GROUP:(gi + 1) * POOL_GROUP]
        sh = 1
        while sh < w:
            acc = acc + pltpu.roll(acc, sh, 0)
            sh *= 2
        cnt = jnp.minimum(t + 1, w).astype(F32)
        out.append(acc[POOL_HALO:, :] / cnt - xc[:, gi * POOL_GROUP:(gi + 1) * POOL_GROUP])
    return jnp.concatenate(out, axis=1)


def _pool_mix(pooled_b, pw_ref):
    return jnp.concatenate([_dot(pooled_b[:, gi * POOL_GROUP:(gi + 1) * POOL_GROUP], pw_ref[gi])
                            for gi in range(len(POOL_WINDOWS))], axis=1)


def _halo_specs(tm, width, S, after):
    per = tm // POOL_HALO
    last = S // POOL_HALO - 1
    if after:
        return pl.BlockSpec((POOL_HALO, width), lambda i: (jnp.minimum((i + 1) * per, last), 0))
    return pl.BlockSpec((POOL_HALO, width), lambda i: (jnp.maximum(i * per - 1, 0), 0))


def _mix_out(xp, attn, gl, bias, pw, pscale, wpot, wao, wo, h, tm=512):
    S, D = h.shape
    nt = S // tm
    PW = xp.shape[1]

    def body(xc_ref, xprev_ref, at_ref, gl_ref, bias_ref, pw_ref, ps_ref, wpo_ref, wao_ref, wo_ref, h_ref,
             ho_ref, bp_ref, ba_ref):
        i = pl.program_id(0)
        pooled = _pooled(xc_ref[...], xprev_ref[...], i).astype(BF16)
        ms = (_pool_mix(pooled, pw_ref) * ps_ref[...]).astype(BF16)
        bp = _dot_nt(ms, wpo_ref[...])
        ba = _dot(at_ref[...], wao_ref[...])
        bp_ref[...] = bp.astype(BF16)
        ba_ref[...] = ba.astype(BF16)
        gates = jax.nn.sigmoid(gl_ref[...].astype(F32) + bias_ref[...])
        merged = (gates[:, :D] * bp + gates[:, D:] * ba).astype(BF16)
        ho_ref[...] = h_ref[...] + _dot(merged, wo_ref[...])

    def row(width):
        return pl.BlockSpec((tm, width), lambda i: (i, 0))

    def whole(x):
        nd = x.ndim
        return pl.BlockSpec(x.shape, lambda i: (0,) * nd)

    return pl.pallas_call(
        body, name="mix_out", grid=(nt,),
        in_specs=[row(PW), _halo_specs(tm, PW, S, False), row(D), row(2 * D), whole(bias), whole(pw), whole(pscale),
                  whole(wpot), whole(wao), whole(wo), row(D)],
        out_specs=[row(D), row(D), row(D)],
        out_shape=[jax.ShapeDtypeStruct((S, D), F32), jax.ShapeDtypeStruct((S, D), BF16),
                   jax.ShapeDtypeStruct((S, D), BF16)],
        compiler_params=_params(dimension_semantics=("arbitrary",)),
    )(xp, xp, attn, gl, bias, pw, pscale, wpot, wao, wo, h)


def _mix_bwd_gate(dh, wo, bp, ba, gl, bias, tm=512):
    S, D = dh.shape
    nt = S // tm

    def body(dh_ref, wo_ref, bp_ref, ba_ref, gl_ref, bias_ref, dgl_ref, dbp_ref, dba_ref, dwo_ref, dbias_ref):
        i = pl.program_id(0)
        dhb = dh_ref[...].astype(BF16)
        dm = _dot_nt(dhb, wo_ref[...])
        gates = jax.nn.sigmoid(gl_ref[...].astype(F32) + bias_ref[...])
        gp, ga = gates[:, :D], gates[:, D:]
        bp_v = bp_ref[...].astype(F32)
        ba_v = ba_ref[...].astype(F32)
        merged = (gp * bp_v + ga * ba_v).astype(BF16)
        _acc(dwo_ref, _dot_tn(merged, dhb), i == 0)
        dbp_ref[...] = (dm * gp).astype(BF16)
        dba_ref[...] = (dm * ga).astype(BF16)
        dgl = jnp.concatenate([dm * bp_v * gp * (1.0 - gp), dm * ba_v * ga * (1.0 - ga)], axis=1)
        dgl_ref[...] = dgl.astype(BF16)
        _acc(dbias_ref, jnp.sum(dgl, axis=0, keepdims=True), i == 0)

    def row(width):
        return pl.BlockSpec((tm, width), lambda i: (i, 0))

    def whole(shape):
        return pl.BlockSpec(shape, lambda i: (0, 0))

    return pl.pallas_call(
        body, name="mix_bwd_gate", grid=(nt,),
        in_specs=[row(D), whole(wo.shape), row(D), row(D), row(2 * D), whole(bias.shape)],
        out_specs=[row(2 * D), row(D), row(D), whole((D, D)), whole((1, 2 * D))],
        out_shape=[jax.ShapeDtypeStruct((S, 2 * D), BF16), jax.ShapeDtypeStruct((S, D), BF16),
                   jax.ShapeDtypeStruct((S, D), BF16), jax.ShapeDtypeStruct((D, D), F32),
                   jax.ShapeDtypeStruct((1, 2 * D), F32)],
        compiler_params=_params(dimension_semantics=("arbitrary",)),
    )(dh, wo, bp, ba, gl, bias)


def _mix_bwd_branch(dbp, dba, attn, xp, pw, pscale, wpot, wao, tm=1024):
    S, D = dbp.shape
    nt = S // tm
    PW = xp.shape[1]
    NG = len(POOL_WINDOWS)

    def body(dbp_ref, dba_ref, at_ref, xc_ref, xprev_ref, pw_ref, ps_ref, wpo_ref, wao_ref,
             dat_ref, dpl_ref, dwao_ref, dwpo_ref, dpw_ref, dps_ref):
        i = pl.program_id(0)
        dba_v = dba_ref[...]
        dbp_v = dbp_ref[...]
        _acc(dwao_ref, _dot_tn(at_ref[...], dba_v), i == 0)
        dat_ref[...] = _dot_nt(dba_v, wao_ref[...]).astype(BF16)
        pooled = _pooled(xc_ref[...], xprev_ref[...], i).astype(BF16)
        mixed = _pool_mix(pooled, pw_ref)
        ps = ps_ref[...]
        _acc(dwpo_ref, _dot_tn(dbp_v, (mixed * ps).astype(BF16)), i == 0)
        dms = _dot(dbp_v, wpo_ref[...])
        _acc(dps_ref, jnp.sum(dms * mixed, axis=0, keepdims=True), i == 0)
        dmixed = (dms * ps).astype(BF16)
        dpooled = []
        for gi in range(NG):
            cols = slice(gi * POOL_GROUP, (gi + 1) * POOL_GROUP)
            _acc(dpw_ref.at[gi], _dot_tn(pooled[:, cols], dmixed[:, cols]), i == 0)
            dpooled.append(_dot_nt(dmixed[:, cols], pw_ref[gi]))
        dpl_ref[...] = jnp.concatenate(dpooled, axis=1)

    def row(width):
        return pl.BlockSpec((tm, width), lambda i: (i, 0))

    def whole(shape):
        nd = len(shape)
        return pl.BlockSpec(shape, lambda i: (0,) * nd)

    return pl.pallas_call(
        body, name="mix_bwd_branch", grid=(nt,),
        in_specs=[row(D), row(D), row(D), row(PW), _halo_specs(tm, PW, S, False), whole(pw.shape), whole(pscale.shape),
                  whole(wpot.shape), whole(wao.shape)],
        out_specs=[row(D), row(PW), whole((D, D)), whole((D, PW)), whole(pw.shape), whole((1, PW))],
        out_shape=[jax.ShapeDtypeStruct((S, D), BF16), jax.ShapeDtypeStruct((S, PW), F32),
                   jax.ShapeDtypeStruct((D, D), F32), jax.ShapeDtypeStruct((D, PW), F32),
                   jax.ShapeDtypeStruct(pw.shape, F32), jax.ShapeDtypeStruct((1, PW), F32)],
        compiler_params=_params(dimension_semantics=("arbitrary",)),
    )(dbp, dba, attn, xp, xp, pw, pscale, wpot, wao)


def _pool_bwd(dpooled, tm=512):
    S, PW = dpooled.shape
    nt = S // tm

    def body(dc_ref, dnext_ref, dxp_ref):
        i = pl.program_id(0)
        dc = dc_ref[...]
        dh = jnp.concatenate([dc, jnp.where(i < nt - 1, dnext_ref[...], 0.0)], axis=0)
        rows = tm + POOL_HALO
        t = lax.broadcasted_iota(jnp.int32, (rows, 1), 0) + i * tm
        out = []
        for gi, w in enumerate(POOL_WINDOWS):
            cols = slice(gi * POOL_GROUP, (gi + 1) * POOL_GROUP)
            acc = dh[:, cols] / jnp.minimum(t + 1, w).astype(F32)
            sh = 1
            while sh < w:
                acc = acc + pltpu.roll(acc, rows - sh, 0)
                sh *= 2
            out.append(acc[:tm, :] - dc[:, cols])
        dxp_ref[...] = jnp.concatenate(out, axis=1).astype(BF16)

    return pl.pallas_call(
        body, name="pool_bwd", grid=(nt,),
        in_specs=[pl.BlockSpec((tm, PW), lambda i: (i, 0)), _halo_specs(tm, PW, S, True)],
        out_specs=pl.BlockSpec((tm, PW), lambda i: (i, 0)), out_shape=jax.ShapeDtypeStruct((S, PW), BF16),
        compiler_params=_params(dimension_semantics=("arbitrary",)),
    )(dpooled, dpooled)


def _in_bwd_w(pieces, u, tm=1024):
    S, D = u.shape
    nt = S // tm
    NW = sum(width for _, width in SEGMENTS)

    def body(*refs):
        piece_refs, u_ref, dw_hbm, acc, sem = refs[:len(SEGMENTS)], refs[len(SEGMENTS)], refs[len(SEGMENTS) + 1], refs[-2], refs[-1]
        i = pl.program_id(0)
        u_t = u_ref[...]
        for (off, width), ref in zip(SEGMENTS, piece_refs):
            for lo in range(0, width, D):
                hi = min(lo + D, width)
                _acc(acc.at[off + lo:off + hi, :], _dot_tn(ref[:, lo:hi].astype(BF16), u_t), i == 0)

        @pl.when(i == nt - 1)
        def _():
            cp = pltpu.make_async_copy(acc, dw_hbm, sem)
            cp.start()
            cp.wait()

    return pl.pallas_call(
        body, name="in_bwd_w", grid=(nt,),
        in_specs=[pl.BlockSpec((tm, width), lambda i: (i, 0)) for _, width in SEGMENTS] + [pl.BlockSpec((tm, D), lambda i: (i, 0))],
        out_specs=pl.BlockSpec(memory_space=pl.ANY), out_shape=jax.ShapeDtypeStruct((NW, D), F32),
        scratch_shapes=[pltpu.VMEM((NW, D), F32), pltpu.SemaphoreType.DMA],
        compiler_params=_params(dimension_semantics=("arbitrary",)),
    )(*pieces, u)


def _in_bwd_x(pieces, wint, h, dh, gnorm, tm=512):
    S, D = h.shape
    nt = S // tm

    def body(*refs):
        piece_refs = refs[:len(SEGMENTS)]
        w_ref, h_ref, dh_ref, g_ref, dx_ref, dxh_ref, dg_ref = refs[len(SEGMENTS):]
        i = pl.program_id(0)
        du = jnp.zeros((tm, D), F32)
        for (off, width), ref in zip(SEGMENTS, piece_refs):
            du = du + _dot(ref[...].astype(BF16), w_ref[off:off + width, :])
        dx, dg_rows = _rms_bwd(du, h_ref[...], g_ref[...])
        out = dh_ref[...] + dx
        dx_ref[...] = out
        dxh_ref[...] = (0.5 * out).astype(BF16)
        _acc(dg_ref, jnp.sum(dg_rows, axis=0, keepdims=True), i == 0)

    def row(width):
        return pl.BlockSpec((tm, width), lambda i: (i, 0))

    vec = pl.BlockSpec((1, D), lambda i: (0, 0))
    return pl.pallas_call(
        body, name="in_bwd_x", grid=(nt,),
        in_specs=[row(width) for _, width in SEGMENTS] + [pl.BlockSpec(wint.shape, lambda i: (0, 0)), row(D), row(D), vec],
        out_specs=[row(D), row(D), vec],
        out_shape=[jax.ShapeDtypeStruct((S, D), F32), jax.ShapeDtypeStruct((S, D), BF16), jax.ShapeDtypeStruct((1, D), F32)],
        compiler_params=_params(dimension_semantics=("arbitrary",)),
    )(*pieces, wint, h, dh, gnorm)


def _row_tile(rows):
    for t in (512, 480, 352, 256, 128, 64, 32, 16, 8):
        if rows % t == 0:
            return t
    return rows


def _adamw_update(w, g, m, v):
    mn = ADAM_B1 * m + (1.0 - ADAM_B1) * g
    vn = ADAM_B2 * v + (1.0 - ADAM_B2) * (g * g)
    m_hat = mn / (1.0 - ADAM_B1 ** ADAM_STEP)
    v_hat = vn / (1.0 - ADAM_B2 ** ADAM_STEP)
    return -ADAM_LR * (m_hat / (jnp.sqrt(v_hat) + ADAM_EPS) + ADAM_WD * w), mn, vn


def _adamw_small(ws, gs, ms, vs):
    n = len(ws)

    def rows(a):
        return a.reshape(-1, a.shape[-1]) if a.ndim > 1 else a.reshape(1, -1)

    def body(*refs):
        for k in range(n):
            w_ref, g_ref, m_ref, v_ref, d_ref, mo_ref, vo_ref = refs[k::n]
            d_ref[...], mo_ref[...], vo_ref[...] = _adamw_update(w_ref[...], g_ref[...], m_ref[...], v_ref[...])

    vmem = pl.BlockSpec(memory_space=pltpu.VMEM)
    out = pl.pallas_call(
        body, name="adamw_small", in_specs=[vmem] * (4 * n), out_specs=[vmem] * (3 * n),
        out_shape=[jax.ShapeDtypeStruct(rows(w).shape, F32) for _ in range(3) for w in ws], compiler_params=_params(),
    )(*[rows(a) for a in ws + gs + ms + vs])
    return [[o.reshape(w.shape) for o, w in zip(out[k * n:(k + 1) * n], ws)] for k in range(3)]


def _adamw(w, g, m, v, token, name, turned=False):
    R, C = w.shape
    tr = R if turned else _row_tile(R)

    def body(w_ref, g_ref, m_ref, v_ref, _, d_ref, mo_ref, vo_ref, go_ref):
        gv = g_ref[...].T if turned else g_ref[...]
        go_ref[...] = gv
        d_ref[...], mo_ref[...], vo_ref[...] = _adamw_update(w_ref[...], gv, m_ref[...], v_ref[...])

    blk = pl.BlockSpec((tr, C), lambda i: (i, 0))
    g_blk = pl.BlockSpec((C, R), lambda i: (0, 0)) if turned else blk
    sh = jax.ShapeDtypeStruct((R, C), F32)
    return pl.pallas_call(
        body, name=name, grid=(R // tr,), in_specs=[blk, g_blk, blk, blk, TOKEN_SPEC], out_specs=[blk] * 4, out_shape=[sh] * 4,
        compiler_params=_params(dimension_semantics=("arbitrary",)),
    )(w, g, m, v, token)


def _cast_place(w, place, name, turned=False):
    R, C = w.shape[::-1] if turned else w.shape
    tr = R if turned else _row_tile(R)
    per = R // tr

    def body(p_ref, w_ref, o_ref):
        o_ref[...] = (w_ref[...].T if turned else w_ref[...]).astype(BF16)

    grid_spec = pltpu.PrefetchScalarGridSpec(
        num_scalar_prefetch=1, grid=(per,),
        in_specs=[pl.BlockSpec((C, R), lambda i, p: (0, 0)) if turned else pl.BlockSpec((tr, C), lambda i, p: (i, 0))],
        out_specs=pl.BlockSpec((tr, C), lambda i, p: (p[0] * per + i, 0)))
    return pl.pallas_call(
        body, name=name, grid_spec=grid_spec, out_shape=jax.ShapeDtypeStruct((N_SHARDS * R, C), BF16),
        compiler_params=_params(dimension_semantics=("arbitrary",)),
    )(place, w)


def _parked(w, m, idx, first, last):
    return jnp.where(m == w, idx, jnp.where(m < w, first, last))


def _sum_halves(g4s, recvs, place, name):
    M = len(g4s)
    NS, R, C = g4s[0].shape
    hr = R // 2
    tr = _row_tile(hr)
    per = hr // tr

    def body(p_ref, *refs):
        m = pl.program_id(0)
        for w in range(M):
            @pl.when(m == w)
            def _(w=w):
                refs[2 * M + w][...] = (refs[w][...] + refs[M + w][...]).astype(BF16)

    def spec(w, mine):
        def index(m, s, i, p):
            row = _parked(w, m, i, 0, per - 1)
            return _parked(w, m, s, 0, NS - 1), (p[1] * per + row if mine else row), 0

        return pl.BlockSpec((1, tr, C), index)

    grid_spec = pltpu.PrefetchScalarGridSpec(
        num_scalar_prefetch=1, grid=(M, NS, per),
        in_specs=[spec(w, True) for w in range(M)] + [spec(w, False) for w in range(M)],
        out_specs=[spec(w, False) for w in range(M)])
    return pl.pallas_call(
        body, name=name, grid_spec=grid_spec, out_shape=[jax.ShapeDtypeStruct((NS, hr, C), BF16)] * M,
        compiler_params=_params(dimension_semantics=("arbitrary",) * 3),
    )(place, *g4s, *recvs)


def _sum_quarters(h4s, recv3s, place, name):
    M = len(h4s)
    NS, hr, C = h4s[0].shape
    tr = _row_tile(hr)
    per = hr // tr

    def body(p_ref, *refs):
        m = pl.program_id(0)
        for w in range(M):
            @pl.when(m == w)
            def _(w=w):
                acc = refs[w][0].astype(F32)
                for k in range(N_SHARDS - 1):
                    acc = acc + refs[M + w][k].astype(F32)
                refs[2 * M + w][...] = acc

    def row(w, m, i):
        return _parked(w, m, i, 0, per - 1)

    grid_spec = pltpu.PrefetchScalarGridSpec(
        num_scalar_prefetch=1, grid=(M, per),
        in_specs=[pl.BlockSpec((1, tr, C), lambda m, i, p, w=w: (p[0], row(w, m, i), 0)) for w in range(M)]
        + [pl.BlockSpec((N_SHARDS - 1, tr, C), lambda m, i, p, w=w: (0, row(w, m, i), 0)) for w in range(M)],
        out_specs=[pl.BlockSpec((tr, C), lambda m, i, p, w=w: (p[1] * per + row(w, m, i), 0)) for w in range(M)])
    return pl.pallas_call(
        body, name=name, grid_spec=grid_spec, out_shape=[jax.ShapeDtypeStruct((2 * hr, C), F32)] * M,
        compiler_params=_params(dimension_semantics=("arbitrary",) * 2),
    )(place, *h4s, *recv3s)


def _by_shape(arrays):
    groups = {}
    for k, a in enumerate(arrays):
        groups.setdefault(a.shape, []).append(k)
    return list(groups.values())


def _place():
    x, y, c = lax.axis_index("x"), lax.axis_index("y"), lax.axis_index("c")
    chips = [(1 - x, y), (x, 1 - y), (1 - x, 1 - y)]
    return x, y, c, chips


HBM_SPEC = pl.BlockSpec(memory_space=pltpu.HBM)
SEM_SPEC = pl.BlockSpec(memory_space=pltpu.SEMAPHORE)
DATAFLOW = pltpu.SideEffectType.DATAFLOW_SIDE_EFFECTING


def _hbm(a):
    return pltpu.with_memory_space_constraint(a, pltpu.HBM)


class InFlight(NamedTuple):
    send_sem: jax.Array
    recv_sem: jax.Array
    bufs: list
    plan: Callable
    token: jax.Array


class Leg(NamedTuple):
    bufs: list
    landing: Optional[InFlight]
    plan: Optional[Callable]
    n_copies: int


def _send(bufs, plan, n_copies):
    return Leg([_hbm(b) for b in bufs], None, plan, n_copies)


def _land(flight):
    return Leg(flight.bufs, flight, None, 0)


def _forward(flight, plan, n_copies):
    return Leg(flight.bufs, flight, plan, n_copies)


def _wait_all(plan, refs, send_ref, recv_ref):
    for k, (src, dst, dev) in enumerate(plan(refs)):
        cp = pltpu.make_async_remote_copy(src_ref=src, dst_ref=dst, send_sem=send_ref.at[k], recv_sem=recv_ref.at[k],
                                          device_id=dev, device_id_type=MESH)
        cp.wait_send()
        cp.wait_recv()


def _start_all(plan, refs, send_ref, recv_ref):
    for k, (src, dst, dev) in enumerate(plan(refs)):
        pltpu.make_async_remote_copy(src_ref=src, dst_ref=dst, send_sem=send_ref.at[k], recv_sem=recv_ref.at[k],
                                     device_id=dev, device_id_type=MESH).start()


def _comm(name, legs, after):
    after = list(after) if isinstance(after, (list, tuple)) else [after]
    ins, in_specs, out_shape, out_specs, aliases, first = [], [], [], [], {}, []
    for leg in legs:
        first.append((len(ins), len(out_shape)))
        for b in leg.bufs:
            aliases[len(ins)] = len(out_shape)
            ins.append(b)
            in_specs.append(HBM_SPEC)
            out_shape.append(pltpu.HBM(b.shape, b.dtype))
            out_specs.append(HBM_SPEC)
        if leg.landing is not None:
            ins += [leg.landing.send_sem, leg.landing.recv_sem]
            in_specs += [SEM_SPEC, SEM_SPEC]
        if leg.plan is not None:
            out_shape += [pltpu.SemaphoreType.DMA((leg.n_copies,))] * 2
            out_specs += [SEM_SPEC, SEM_SPEC]
    starts = any(leg.plan is not None for leg in legs)
    if starts:
        out_shape.append(jax.ShapeDtypeStruct((8, 128), F32))
        out_specs.append(pl.BlockSpec(memory_space=pltpu.VMEM))
    n_in = len(ins) + len(after)

    def body(*refs):
        outs = refs[n_in:]
        for leg, (i, o) in zip(legs, first):
            nb = len(leg.bufs)
            bufs = refs[i:i + nb]
            if leg.landing is not None:
                _wait_all(leg.landing.plan, bufs, refs[i + nb], refs[i + nb + 1])
            if leg.plan is not None:
                _start_all(leg.plan, bufs, outs[o + nb], outs[o + nb + 1])
        if starts:
            outs[-1][...] = jnp.zeros_like(outs[-1])

    out = pl.pallas_call(
        body, name=name, in_specs=in_specs + [TOKEN_SPEC] * len(after), out_shape=out_shape, out_specs=out_specs,
        input_output_aliases=aliases, compiler_params=pltpu.CompilerParams(has_side_effects=DATAFLOW),
    )(*ins, *after)
    results = []
    for leg, (_, o) in zip(legs, first):
        nb = len(leg.bufs)
        bufs = list(out[o:o + nb])
        results.append(bufs if leg.plan is None else InFlight(out[o + nb], out[o + nb + 1], bufs, leg.plan, out[-1]))
    return results


def _half_rows(buf, chip, core):
    hr = buf.shape[0] // (2 * N_SHARDS)
    return buf.at[pl.ds(pl.multiple_of((2 * chip + core) * hr, 16), hr)]


def _gather_ici_plan(bufs):
    x, y, c, chips = _place()
    return [(_half_rows(b, 2 * x + y, c), _half_rows(b, 2 * x + y, c), (px, py, c)) for b in bufs for px, py in chips]


def _gather_d2d_plan(bufs):
    x, y, c, chips = _place()
    return [(_half_rows(b, 2 * px + py, c), _half_rows(b, 2 * px + py, c), (x, y, 1 - c)) for b in bufs for px, py in chips]


def _swap_plan(bufs):
    x, y, c, _ = _place()
    n = len(bufs) // 2
    copies = []
    for g, land in zip(bufs[:n], bufs[n:]):
        hr = g.shape[1] // 2
        copies.append((g.at[:, pl.ds(pl.multiple_of((1 - c) * hr, 8), hr)], land, (x, y, 1 - c)))
    return copies


def _exchange_plan(bufs):
    x, y, c, chips = _place()
    n = len(bufs) // 2
    return [(h.at[2 * px + py], land.at[k], (px, py, c))
            for h, land in zip(bufs[:n], bufs[n:]) for k, (px, py) in enumerate(chips)]


def _share_plan(bufs):
    x, y, c, _ = _place()
    copies = []
    for buf in bufs:
        hr = buf.shape[0] // 2
        mine = buf.at[pl.ds(pl.multiple_of(c * hr, 8), hr)]
        copies.append((mine, mine, (x, y, 1 - c)))
    return copies


N_DEVICES = 8


def _slot_place(vec, place):
    R, C = vec.shape

    def body(p_ref, v_ref, o_ref):
        o_ref[0] = v_ref[...]

    grid_spec = pltpu.PrefetchScalarGridSpec(
        num_scalar_prefetch=1, grid=(1,), in_specs=[pl.BlockSpec((R, C), lambda i, p: (0, 0))],
        out_specs=pl.BlockSpec((1, R, C), lambda i, p: (2 * p[0] + p[1], 0, 0)))
    return pl.pallas_call(
        body, name="slot_place", grid_spec=grid_spec, out_shape=jax.ShapeDtypeStruct((N_DEVICES, R, C), F32),
        compiler_params=_params(dimension_semantics=("arbitrary",)),
    )(place, vec)


def _slots_plan(bufs):
    x, y, c, _ = _place()
    mine = bufs[0].at[4 * x + 2 * y + c]
    return [(mine, mine, (x ^ (r >> 2), y ^ ((r >> 1) & 1), c ^ (r & 1))) for r in range(1, N_DEVICES)]


def _sum_slots(slots):
    _, R, C = slots.shape

    def body(s_ref, o_ref):
        acc = s_ref[0]
        for d in range(1, N_DEVICES):
            acc = acc + s_ref[d]
        o_ref[...] = acc

    return pl.pallas_call(
        body, name="sum_slots", in_specs=[pl.BlockSpec(memory_space=pltpu.VMEM)],
        out_specs=pl.BlockSpec(memory_space=pltpu.VMEM), out_shape=jax.ShapeDtypeStruct((R, C), F32),
        compiler_params=_params(),
    )(slots)


SMALL = ("ffn1_norm", "mix_norm", "pool_w", "pool_scale", "q_norm", "k_norm", "sinks", "gate_bias", "ffn2_norm")
SMALL_COLS = 1024
FFN1 = ("ffn1_w_gate", "ffn1_w_up", "ffn1_w_down")
MIXER = ("w_in", "w_pool_out", "w_attn_out", "w_out")
FFN2 = ("ffn2_w_gate", "ffn2_w_up", "ffn2_w_down")
LARGE = FFN1 + MIXER + FFN2
TRANSPOSED = ("ffn1_w_gate", "ffn1_w_up", "w_in", "w_pool_out", "ffn2_w_gate", "ffn2_w_up")
TURNED = ("w_pool_out",)
WEIGHTS = ("ffn1_norm", "ffn1_w_gate", "ffn1_w_up", "ffn1_w_down", "mix_norm", "w_in", "pool_w", "pool_scale",
           "w_pool_out", "q_norm", "k_norm", "sinks", "w_attn_out", "gate_bias", "w_out", "ffn2_norm",
           "ffn2_w_gate", "ffn2_w_up", "ffn2_w_down")


def _pack_small(parts):
    flat = jnp.concatenate([p.reshape(-1) for p in parts])
    rows = -(-flat.shape[0] // (8 * SMALL_COLS)) * 8
    return jnp.pad(flat, (0, rows * SMALL_COLS - flat.shape[0])).reshape(rows, SMALL_COLS)


def _unpack_small(packed, like):
    flat = packed.reshape(-1)
    out, off = [], 0
    for p in like:
        out.append(flat[off:off + p.size].reshape(p.shape))
        off += p.size
    return out, flat[off]


def _tie(small, token):
    return small + token[0, 0]


class Reduction:
    def __init__(self, names, grads, place):
        self.names, self.place = names, place
        self.grads = [g.reshape(N_SHARDS, -1, g.shape[-1]) for g in grads]

    def swap(self):
        lands = [lax.empty((g.shape[0], g.shape[1] // 2, g.shape[2]), g.dtype) for g in self.grads]
        return _send(self.grads + lands, _swap_plan, len(self.grads))

    def exchange(self, swapped):
        n = len(self.names)
        halves = self._per_shape(_sum_halves, "sum_halves", swapped[:n], swapped[n:])
        lands = [lax.empty((N_SHARDS - 1,) + h.shape[1:], h.dtype) for h in halves]
        return _send(halves + lands, _exchange_plan, (N_SHARDS - 1) * n)

    def share(self, exchanged):
        n = len(self.names)
        return _send(self._per_shape(_sum_quarters, "sum_quarters", exchanged[:n], exchanged[n:]), _share_plan, n)

    def named(self, shared):
        return dict(zip(self.names, shared))

    def _per_shape(self, add, stage, mine, received):
        out = [None] * len(mine)
        for idx in _by_shape(mine):
            sums = add([mine[k] for k in idx], [received[k] for k in idx], self.place, f"{stage}_{self.names[idx[0]]}")
            for k, v in zip(idx, sums):
                out[k] = v
        return out


def _row_form(name, a):
    return a.T if name in TRANSPOSED else a


def _shard_form(name, a):
    return a if name in TURNED else _row_form(name, a)


def kernel(x, ffn1_norm, ffn1_w_gate, ffn1_w_up, ffn1_w_down, mix_norm, w_in, pool_w, pool_scale, w_pool_out, q_norm, k_norm, sinks, w_attn_out, gate_bias, w_out, ffn2_norm, ffn2_w_gate, ffn2_w_up, ffn2_w_down, loss_target, m_ffn1_norm, m_ffn1_w_gate, m_ffn1_w_up, m_ffn1_w_down, m_mix_norm, m_w_in, m_pool_w, m_pool_scale, m_w_pool_out, m_q_norm, m_k_norm, m_sinks, m_w_attn_out, m_gate_bias, m_w_out, m_ffn2_norm, m_ffn2_w_gate, m_ffn2_w_up, m_ffn2_w_down, v_ffn1_norm, v_ffn1_w_gate, v_ffn1_w_up, v_ffn1_w_down, v_mix_norm, v_w_in, v_pool_w, v_pool_scale, v_w_pool_out, v_q_norm, v_k_norm, v_sinks, v_w_attn_out, v_gate_bias, v_w_out, v_ffn2_norm, v_ffn2_w_gate, v_ffn2_w_up, v_ffn2_w_down):
    args = dict(locals())
    wts = {n: _shard_form(n, args[n]) for n in WEIGHTS}
    mom = {n: _shard_form(n, args["m_" + n]) for n in WEIGHTS}
    var = {n: _shard_form(n, args["v_" + n]) for n in WEIGHTS}
    shard = 2 * lax.axis_index("x") + lax.axis_index("y")
    place = jnp.stack([shard, lax.axis_index("c")]).astype(jnp.int32)

    xs, target = x[0], loss_target[0]
    D = xs.shape[1]
    g1 = wts["ffn1_norm"].reshape(1, D)
    gm = wts["mix_norm"].reshape(1, D)
    g2 = wts["ffn2_norm"].reshape(1, D)
    qw = wts["q_norm"].reshape(1, HEAD_DIM)
    kw = wts["k_norm"].reshape(1, HEAD_DIM)
    bias = wts["gate_bias"].reshape(1, 2 * D)
    pscale = wts["pool_scale"].reshape(1, -1)
    pw = wts["pool_w"].astype(BF16)
    sink_rows = jnp.broadcast_to(jnp.repeat(wts["sinks"], ATTN_BLOCK).reshape(N_KV_HEADS, GQA_GROUP * ATTN_BLOCK, 1),
                                 (N_KV_HEADS, GQA_GROUP * ATTN_BLOCK, 128))

    gate1, up1, down1 = FFN1[:1], FFN1[1:2], FFN1[2:]
    mix_in, mix_out = MIXER[:1], MIXER[1:]

    def over_ici(names):
        return _send([placed[n] for n in names], _gather_ici_plan, 3 * len(names))

    def to_sibling(flight):
        return _forward(flight, _gather_d2d_plan, len(flight.bufs) * 3)

    placed = {n: _cast_place(wts[n], place, "cast_" + n) for n in gate1}
    gate1_ici, = _comm("gather_gate1_ici", [over_ici(gate1)], place)
    placed.update({n: _cast_place(wts[n], place, "cast_" + n) for n in up1 + down1})
    up1_ici, down1_ici = _comm("gather_ffn1_ici", [over_ici(up1), over_ici(down1)], gate1_ici.token)
    placed.update({n: _cast_place(wts[n], place, "cast_" + n, turned=n in TURNED) for n in MIXER + FFN2})
    band = _band_bias()
    shadow = [sink_rows, pw, band, down1_ici.token]
    gate1_d2d, in_ici, out_ici, ffn2_ici = _comm(
        "gather_rest_ici", [to_sibling(gate1_ici), over_ici(mix_in), over_ici(mix_out), over_ici(FFN2)], shadow)
    w = dict(zip(gate1, _comm("gather_gate1_wait", [_land(gate1_d2d)], gate1_d2d.token)[0]))

    n1, a1 = _ffn_gate(xs, g1, w["ffn1_w_gate"], "ffn1_gate")
    up1_d2d, down1_d2d = _comm("gather_ffn1_d2d", [to_sibling(up1_ici), to_sibling(down1_ici)], a1)
    w.update(zip(up1, _comm("gather_up1_wait", [_land(up1_d2d)], up1_d2d.token)[0]))
    b1, s1 = _ffn_hidden(n1, a1, w["ffn1_w_up"], "ffn1_hidden")
    w.update(zip(down1, _comm("gather_down1_wait", [_land(down1_d2d)], s1)[0]))
    h1 = _ffn_down(xs, s1, w["ffn1_w_down"], "ffn1_down")
    in_d2d, out_d2d = _comm("gather_mix_d2d", [to_sibling(in_ici), to_sibling(out_ici)], h1)
    w.update(zip(mix_in, _comm("gather_in_wait", [_land(in_d2d)], in_d2d.token)[0]))
    u, xp, q, kv, gl = _mix_proj(h1, gm, w["w_in"])
    attn = _attn_fwd(q, kv, qw, kw, sink_rows, band)
    ffn2_d2d, landed = _comm("gather_ffn2_d2d", [to_sibling(ffn2_ici), _land(out_d2d)], attn)
    w.update(zip(mix_out, landed))
    h2, bp, ba = _mix_out(xp, attn, gl, bias, pw, pscale, w["w_pool_out"], w["w_attn_out"], w["w_out"], h1)
    w.update(zip(FFN2, _comm("gather_ffn2_wait", [_land(ffn2_d2d)], h2)[0]))
    n2, a2, b2, s2 = _ffn_up(h2, g2, w["ffn2_w_gate"], w["ffn2_w_up"], "ffn2_up")
    dy, dyh, loss = _ffn_down_loss(h2, s2, w["ffn2_w_down"], target, "ffn2_down_loss")

    gw, gs = {}, {}
    da, db = _ffn_bwd_h(dyh, a2, b2, w["ffn2_w_down"], ffn2_d2d.token, "ffn2_bwd_h")
    gw["ffn2_w_gate"], gw["ffn2_w_up"] = _xty(da, n2, "ffn2_dw_gate"), _xty(db, n2, "ffn2_dw_up")
    gw["ffn2_w_down"] = _xty(s2, dyh, "ffn2_dw_down")
    red2 = Reduction(FFN2, [gw[n] for n in FFN2], place)
    swap, = _comm("reduce_ffn2_swap", [red2.swap()], da)
    dh2, gs["ffn2_norm"] = _ffn_bwd_x(da, db, w["ffn2_w_gate"], w["ffn2_w_up"], h2, dy, _tie(g2, swap.token), "ffn2_bwd_x")
    swapped, = _comm("reduce_ffn2_swap_wait", [_land(swap)], dh2)
    exchange, = _comm("reduce_ffn2_exchange", [red2.exchange(swapped)], place)
    dgl, dbp, dba, gw["w_out"], gs["gate_bias"] = _mix_bwd_gate(dh2, w["w_out"], bp, ba, gl, _tie(bias, exchange.token))
    dattn, dpooled, gw["w_attn_out"], gw["w_pool_out"], gs["pool_w"], gs["pool_scale"] = _mix_bwd_branch(
        dbp, dba, attn, xp, pw, pscale, w["w_pool_out"], w["w_attn_out"])
    dq, dkv, gs["q_norm"], gs["k_norm"], dsk = _attn_bwd(q, kv, dattn, qw, kw, sink_rows, band)
    gs["sinks"] = dsk[:, :, 0]
    exchanged, = _comm("reduce_ffn2_exchange_wait", [_land(exchange)], dq)
    share2, = _comm("reduce_ffn2_share", [red2.share(exchanged)], place)
    pieces = (_pool_bwd(dpooled), dq, dkv, dgl)
    gw["w_in"] = _in_bwd_w(pieces, u)
    redm = Reduction(MIXER, [gw[n] for n in MIXER], place)
    swap, shared = _comm("reduce_mix_swap", [redm.swap(), _land(share2)], share2.token)
    grads = red2.named(shared)
    dh1, dh1h, gs["mix_norm"] = _in_bwd_x(pieces, w["w_in"], h1, dh2, _tie(gm, swap.token))
    swapped, = _comm("reduce_mix_swap_wait", [_land(swap)], dh1)
    exchange, = _comm("reduce_mix_exchange", [redm.exchange(swapped)], place)
    da, db = _ffn_bwd_h(dh1h, a1, b1, w["ffn1_w_down"], exchange.token, "ffn1_bwd_h")
    gw["ffn1_w_gate"], gw["ffn1_w_up"] = _xty(da, n1, "ffn1_dw_gate"), _xty(db, n1, "ffn1_dw_up")
    gw["ffn1_w_down"] = _xty(s1, dh1h, "ffn1_dw_down")
    exchanged, = _comm("reduce_mix_exchange_wait", [_land(exchange)], [gw[n] for n in FFN1])
    red1 = Reduction(FFN1, [gw[n] for n in FFN1], place)
    sharem, swap = _comm("reduce_mix_share", [redm.share(exchanged), red1.swap()], place)
    grad_x, gs["ffn1_norm"] = _ffn_bwd_x(da, db, w["ffn1_w_gate"], w["ffn1_w_up"], xs, dh1, _tie(g1, swap.token), "ffn1_bwd_x")

    small_parts = [gs[n] for n in SMALL] + [loss[0, 0].reshape(1)]
    slots, shared, swapped = _comm(
        "gather_small", [_send([_slot_place(_pack_small(small_parts), place)], _slots_plan, N_DEVICES - 1), _land(sharem),
                         _land(swap)], grad_x)
    grads.update(redm.named(shared))
    exchange, = _comm("reduce_ffn1_exchange", [red1.exchange(swapped)], slots.token)
    token = exchange.token
    delta, new_m, new_v = {}, {}, {}
    for n in FFN2 + MIXER:
        delta[n], new_m[n], new_v[n], grads[n] = _adamw(wts[n], grads[n], mom[n], var[n], token, "adamw_" + n, n in TURNED)
    summed = _sum_slots(_comm("gather_small_wait", [_land(slots)], delta[MIXER[-1]])[0][0])
    small_grads, loss_sum = _unpack_small(summed, [wts[n] for n in SMALL])
    grads.update(dict(zip(SMALL, small_grads)))
    for out, vals in zip((delta, new_m, new_v), _adamw_small(*[[d[n] for n in SMALL] for d in (wts, grads, mom, var)])):
        out.update(dict(zip(SMALL, vals)))
    results = {(k, n): _shard_form(n, d[n]) for k, d in enumerate((grads, delta, new_m, new_v)) for n in SMALL + FFN2 + MIXER}
    exchanged, = _comm("reduce_ffn1_exchange_wait", [_land(exchange)],
                       [results[k, n] for k in range(4) for n in SMALL])
    share1, = _comm("reduce_ffn1_share", [red1.share(exchanged)], place)
    shared, = _comm("reduce_ffn1_share_wait", [_land(share1)], share1.token)
    grads.update(red1.named(shared))
    for n in FFN1:
        delta[n], new_m[n], new_v[n], grads[n] = _adamw(wts[n], grads[n], mom[n], var[n], token, "adamw_" + n, n in TURNED)
    results.update({(k, n): _shard_form(n, d[n]) for k, d in enumerate((grads, delta, new_m, new_v)) for n in FFN1})
    return (loss_sum, grad_x[None], *[results[k, n] for k in range(4) for n in WEIGHTS])
```

```python
from typing import Callable, NamedTuple, Optional

import jax
import jax.numpy as jnp
from jax import lax
from jax.experimental import pallas as pl
from jax.experimental.pallas import tpu as pltpu

F32 = jnp.float32
BF16 = jnp.bfloat16
RMS_EPS = 1e-6
POOL_WINDOWS = (2, 4, 8, 16)
POOL_GROUP = 128
POOL_HALO = 16
HEAD_DIM = 64
GQA_GROUP = 8
N_KV_HEADS = 2
ATTN_BLOCK = 128
SCALE = HEAD_DIM ** -0.5
NEG = -1e30
N_SHARDS = 4
ADAM_LR, ADAM_B1, ADAM_B2, ADAM_EPS, ADAM_WD, ADAM_STEP = 0.001, 0.9, 0.999, 1e-08, 0.01, 10
VMEM_LIMIT = 56 * 1024 * 1024
MESH = pl.DeviceIdType.MESH
SEG_POOL, SEG_Q, SEG_KV, SEG_GATE = (0, 512), (512, 1024), (1536, 256), (1792, 2048)
SEGMENTS = (SEG_POOL, SEG_Q, SEG_KV, SEG_GATE)


def _params(**kw):
    return pltpu.CompilerParams(vmem_limit_bytes=VMEM_LIMIT, **kw)


def _dot(a, b):
    return jnp.dot(a, b, preferred_element_type=F32)


def _dot_nt(a, b):
    return lax.dot_general(a, b, (((1,), (1,)), ((), ())), preferred_element_type=F32)


def _dot_tn(a, b):
    return lax.dot_general(a, b, (((0,), (0,)), ((), ())), preferred_element_type=F32)


def _rinv(x):
    return lax.rsqrt(jnp.mean(x * x, axis=-1, keepdims=True) + RMS_EPS)


def _rms_bwd(dn, x, g):
    r = _rinv(x)
    xr = x * r
    z = dn * g
    dx = r * (z - xr * jnp.mean(z * xr, axis=-1, keepdims=True))
    return dx, dn * xr


def _acc(ref, val, first):
    @pl.when(first)
    def _():
        ref[...] = val

    @pl.when(jnp.logical_not(first))
    def _():
        ref[...] += val


TOKEN_SPEC = pl.BlockSpec(memory_space=pl.ANY)
F_HALF = 1408


def _resident(w):
    return pl.BlockSpec(w.shape, lambda i: (0, 0), pipeline_mode=pl.Buffered(1))


def _ffn_up(h, gnorm, wgt, wut, name, tm=512):
    S, D = h.shape
    F = wgt.shape[0]

    def body(h_ref, g_ref, wg_ref, wu_ref, n_ref, a_ref, b_ref, s_ref):
        x = h_ref[...]
        n = (x * _rinv(x) * g_ref[...]).astype(BF16)
        n_ref[...] = n
        for lo in range(0, F, F_HALF):
            cols = slice(lo, lo + F_HALF)
            a = _dot_nt(n, wg_ref[cols, :])
            b = _dot_nt(n, wu_ref[cols, :])
            a_ref[:, cols] = a.astype(BF16)
            b_ref[:, cols] = b.astype(BF16)
            s_ref[:, cols] = (a * jax.nn.sigmoid(a) * b).astype(BF16)

    act = pl.BlockSpec((tm, F), lambda i: (i, 0))
    hidden = jax.ShapeDtypeStruct((S, F), BF16)
    return pl.pallas_call(
        body, name=name, grid=(S // tm,),
        in_specs=[pl.BlockSpec((tm, D), lambda i: (i, 0)), pl.BlockSpec((1, D), lambda i: (0, 0)), _resident(wgt), _resident(wut)],
        out_specs=[pl.BlockSpec((tm, D), lambda i: (i, 0)), act, act, act],
        out_shape=[jax.ShapeDtypeStruct((S, D), BF16), hidden, hidden, hidden],
        compiler_params=_params(dimension_semantics=("arbitrary",)),
    )(h, gnorm, wgt, wut)


def _ffn_gate(h, gnorm, wgt, name, tm=512):
    S, D = h.shape
    F = wgt.shape[0]

    def body(h_ref, g_ref, wg_ref, n_ref, a_ref):
        x = h_ref[...]
        n = (x * _rinv(x) * g_ref[...]).astype(BF16)
        n_ref[...] = n
        for lo in range(0, F, F_HALF):
            a_ref[:, lo:lo + F_HALF] = _dot_nt(n, wg_ref[lo:lo + F_HALF, :]).astype(BF16)

    row = pl.BlockSpec((tm, D), lambda i: (i, 0))
    return pl.pallas_call(
        body, name=name, grid=(S // tm,), in_specs=[row, pl.BlockSpec((1, D), lambda i: (0, 0)), _resident(wgt)],
        out_specs=[row, pl.BlockSpec((tm, F), lambda i: (i, 0))],
        out_shape=[jax.ShapeDtypeStruct((S, D), BF16), jax.ShapeDtypeStruct((S, F), BF16)],
        compiler_params=_params(dimension_semantics=("arbitrary",)),
    )(h, gnorm, wgt)


def _ffn_hidden(n, a, wut, name, tm=512):
    S, D = n.shape
    F = wut.shape[0]

    def body(n_ref, a_ref, wu_ref, b_ref, s_ref):
        n_t = n_ref[...]
        for lo in range(0, F, F_HALF):
            cols = slice(lo, lo + F_HALF)
            a = a_ref[:, cols].astype(F32)
            b = _dot_nt(n_t, wu_ref[cols, :])
            b_ref[:, cols] = b.astype(BF16)
            s_ref[:, cols] = (a * jax.nn.sigmoid(a) * b).astype(BF16)

    act = pl.BlockSpec((tm, F), lambda i: (i, 0))
    hidden = jax.ShapeDtypeStruct((S, F), BF16)
    return pl.pallas_call(
        body, name=name, grid=(S // tm,), in_specs=[pl.BlockSpec((tm, D), lambda i: (i, 0)), act, _resident(wut)],
        out_specs=[act, act], out_shape=[hidden, hidden],
        compiler_params=_params(dimension_semantics=("arbitrary",)),
    )(n, a, wut)


def _ffn_down(h, s, wd, name, tm=1024):
    S, D = h.shape
    F = wd.shape[0]

    def body(h_ref, s_ref, wd_ref, o_ref):
        o_ref[...] = h_ref[...] + 0.5 * _dot(s_ref[...], wd_ref[...])

    row = pl.BlockSpec((tm, D), lambda i: (i, 0))
    return pl.pallas_call(
        body, name=name, grid=(S // tm,), in_specs=[row, pl.BlockSpec((tm, F), lambda i: (i, 0)), _resident(wd)],
        out_specs=row, out_shape=jax.ShapeDtypeStruct((S, D), F32),
        compiler_params=_params(dimension_semantics=("arbitrary",)),
    )(h, s, wd)


def _ffn_down_loss(h, s, wd, target, name, tm=512):
    S, D = h.shape
    F = wd.shape[0]

    def body(h_ref, s_ref, wd_ref, t_ref, dy_ref, dyh_ref, loss_ref):
        e = h_ref[...] + 0.5 * _dot(s_ref[...], wd_ref[...]) - t_ref[...]
        dy = e * (1.0 / D)
        dy_ref[...] = dy
        dyh_ref[...] = (0.5 * dy).astype(BF16)
        tot = jnp.sum(jnp.sum(e * e, axis=1, keepdims=True), axis=0, keepdims=True) * (0.5 / D)
        _acc(loss_ref, jnp.broadcast_to(tot, loss_ref.shape), pl.program_id(0) == 0)

    row = pl.BlockSpec((tm, D), lambda i: (i, 0))
    return pl.pallas_call(
        body, name=name, grid=(S // tm,), in_specs=[row, pl.BlockSpec((tm, F), lambda i: (i, 0)), _resident(wd), row],
        out_specs=[row, row, pl.BlockSpec((8, 128), lambda i: (0, 0))],
        out_shape=[jax.ShapeDtypeStruct((S, D), F32), jax.ShapeDtypeStruct((S, D), BF16), jax.ShapeDtypeStruct((8, 128), F32)],
        compiler_params=_params(dimension_semantics=("arbitrary",)),
    )(h, s, wd, target)


def _ffn_bwd_h(dyh, a, b, wd, token, name, tm=512):
    S, D = dyh.shape
    F = wd.shape[0]

    def body(dyh_ref, a_ref, b_ref, wd_ref, _, da_ref, db_ref):
        dyh_t = dyh_ref[...]
        for lo in range(0, F, F_HALF):
            cols = slice(lo, lo + F_HALF)
            av = a_ref[:, cols].astype(F32)
            bv = b_ref[:, cols].astype(F32)
            ds = _dot_nt(dyh_t, wd_ref[cols, :])
            sig = jax.nn.sigmoid(av)
            silu = av * sig
            da_ref[:, cols] = (ds * bv * (sig * (1.0 + av * (1.0 - sig)))).astype(BF16)
            db_ref[:, cols] = (ds * silu).astype(BF16)

    act = pl.BlockSpec((tm, F), lambda i: (i, 0))
    hidden = jax.ShapeDtypeStruct((S, F), BF16)
    return pl.pallas_call(
        body, name=name, grid=(S // tm,),
        in_specs=[pl.BlockSpec((tm, D), lambda i: (i, 0)), act, act, _resident(wd), TOKEN_SPEC],
        out_specs=[act, act], out_shape=[hidden, hidden],
        compiler_params=_params(dimension_semantics=("arbitrary",)),
    )(dyh, a, b, wd, token)


def _xty(x, y, name, tk=2048, tf=1408, after=()):
    S, F = x.shape
    D = y.shape[1]

    def body(x_ref, y_ref, *rest):
        _acc(rest[-1], _dot_tn(x_ref[...], y_ref[...]), pl.program_id(1) == 0)

    return pl.pallas_call(
        body, name=name, grid=(F // tf, S // tk),
        in_specs=[pl.BlockSpec((tk, tf), lambda j, k: (k, j)), pl.BlockSpec((tk, D), lambda j, k: (k, 0))]
        + [TOKEN_SPEC] * len(after),
        out_specs=pl.BlockSpec((tf, D), lambda j, k: (j, 0)), out_shape=jax.ShapeDtypeStruct((F, D), F32),
        compiler_params=_params(dimension_semantics=("arbitrary", "arbitrary")),
    )(x, y, *after)


def _ffn_bwd_x(da, db, wgt, wut, x, dh, gnorm, name, tm=512):
    S, D = x.shape
    F = wgt.shape[0]

    def body(da_ref, db_ref, wg_ref, wu_ref, x_ref, dh_ref, g_ref, dx_ref, dg_ref):
        dn = _dot(da_ref[...], wg_ref[...]) + _dot(db_ref[...], wu_ref[...])
        dx, dg_rows = _rms_bwd(dn, x_ref[...], g_ref[...])
        dx_ref[...] = dh_ref[...] + dx
        _acc(dg_ref, jnp.sum(dg_rows, axis=0, keepdims=True), pl.program_id(0) == 0)

    row = pl.BlockSpec((tm, D), lambda i: (i, 0))
    act = pl.BlockSpec((tm, F), lambda i: (i, 0))
    vec = pl.BlockSpec((1, D), lambda i: (0, 0))
    return pl.pallas_call(
        body, name=name, grid=(S // tm,), in_specs=[act, act, _resident(wgt), _resident(wut), row, row, vec],
        out_specs=[row, vec],
        out_shape=[jax.ShapeDtypeStruct((S, D), F32), jax.ShapeDtypeStruct((1, D), F32)],
        compiler_params=_params(dimension_semantics=("arbitrary",)),
    )(da, db, wgt, wut, x, dh, gnorm)


def _mix_proj(h, gnorm, wint, tm=512):
    S, D = h.shape
    nt = S // tm

    def body(h_ref, g_ref, w_ref, u_ref, xp_ref, q_ref, kv_ref, gl_ref):
        x = h_ref[...]
        u = (x * _rinv(x) * g_ref[...]).astype(BF16)
        u_ref[...] = u
        for (off, width), ref in zip(SEGMENTS, (xp_ref, q_ref, kv_ref, gl_ref)):
            ref[...] = _dot_nt(u, w_ref[off:off + width, :]).astype(ref.dtype)

    def row(width):
        return pl.BlockSpec((tm, width), lambda i: (i, 0))

    return pl.pallas_call(
        body, name="mix_proj", grid=(nt,),
        in_specs=[row(D), pl.BlockSpec((1, D), lambda i: (0, 0)), pl.BlockSpec(wint.shape, lambda i: (0, 0))],
        out_specs=[row(D), row(SEG_POOL[1]), row(SEG_Q[1]), row(SEG_KV[1]), row(SEG_GATE[1])],
        out_shape=[jax.ShapeDtypeStruct((S, D), BF16), jax.ShapeDtypeStruct((S, SEG_POOL[1]), F32),
                   jax.ShapeDtypeStruct((S, SEG_Q[1]), BF16), jax.ShapeDtypeStruct((S, SEG_KV[1]), BF16),
                   jax.ShapeDtypeStruct((S, SEG_GATE[1]), BF16)],
        compiler_params=_params(dimension_semantics=("arbitrary",)),
    )(h, gnorm, wint)


def _stack_heads(x, g):
    return jnp.concatenate([x[:, (GQA_GROUP * g + hh) * HEAD_DIM:(GQA_GROUP * g + hh + 1) * HEAD_DIM]
                            for hh in range(GQA_GROUP)], axis=0)


def _unstack_heads(ref, val, g):
    for hh in range(GQA_GROUP):
        lo = (GQA_GROUP * g + hh) * HEAD_DIM
        ref[:, lo:lo + HEAD_DIM] = val[hh * ATTN_BLOCK:(hh + 1) * ATTN_BLOCK, :]


def _rowsum(xb, width):
    return _dot(xb, jnp.ones((xb.shape[1], width), BF16))


def _rinv_lanes(x):
    return lax.rsqrt(_rowsum((x * x).astype(BF16), x.shape[1]) * (1.0 / x.shape[1]) + RMS_EPS)


def _twice(x):
    return jnp.concatenate([x, x], axis=1)


def _band_bias():
    qi = (jnp.arange(GQA_GROUP * ATTN_BLOCK) % ATTN_BLOCK)[:, None]
    kj = jnp.arange(2 * ATTN_BLOCK)[None, :]
    return jnp.where(jnp.logical_and(kj > qi, kj <= qi + ATTN_BLOCK), 0.0, NEG).astype(F32)


def _attn_exp(qn, kk, sink, band, n):
    kj = lax.broadcasted_iota(jnp.int32, (1, 2 * ATTN_BLOCK), 1)
    before_sequence = jnp.where(jnp.logical_and(kj < ATTN_BLOCK, n == 0), NEG, 0.0)
    s = _dot_nt(qn, kk) + band + before_sequence
    m = jnp.maximum(jnp.broadcast_to(jnp.max(s, axis=-1, keepdims=True), sink.shape), sink)
    e = jnp.exp(s - _twice(m))
    e_s = jnp.exp(sink - m)
    e_b = e.astype(BF16)
    inv = 1.0 / (_rowsum(e_b, ATTN_BLOCK) + e_s)
    return e_b, e, e_s, inv


def _attn_blocks(n):
    cur = pl.multiple_of(n * ATTN_BLOCK, ATTN_BLOCK)
    prev = pl.multiple_of(jnp.maximum(n - 1, 0) * ATTN_BLOCK, ATTN_BLOCK)
    return cur, prev


def _kv_window(kv_ref, n):
    cur, prev = _attn_blocks(n)
    return jnp.concatenate([kv_ref[pl.ds(prev, ATTN_BLOCK), :], kv_ref[pl.ds(cur, ATTN_BLOCK), :]], axis=0).astype(F32)


def _kv_split(kv, g):
    k = kv[:, g * HEAD_DIM:(g + 1) * HEAD_DIM]
    v = kv[:, (N_KV_HEADS + g) * HEAD_DIM:(N_KV_HEADS + g + 1) * HEAD_DIM]
    return k, v


def _attn_fwd(q, kv, qw, kw, sink_rows, band):
    S, W = q.shape
    nb = S // ATTN_BLOCK

    def body(q_ref, kv_ref, qw_ref, kw_ref, sk_ref, band_ref, o_ref, o_scr):
        n = pl.program_id(0)
        qf = q_ref[...].astype(F32)
        kvw = _kv_window(kv_ref, n)
        for g in range(N_KV_HEADS):
            qs = _stack_heads(qf, g)
            qn = (qs * _rinv_lanes(qs) * qw_ref[...] * SCALE).astype(BF16)
            k, v = _kv_split(kvw, g)
            kk = (k * _rinv_lanes(k) * kw_ref[...]).astype(BF16)
            e_b, _, _, inv = _attn_exp(qn, kk, sk_ref[g], band_ref[...], n)
            _unstack_heads(o_scr, _dot(e_b, v.astype(BF16)) * inv[:, :HEAD_DIM], g)
        o_ref[...] = o_scr[...].astype(BF16)

    blk = pl.BlockSpec((ATTN_BLOCK, W), lambda n: (n, 0))
    return pl.pallas_call(
        body, name="attn_fwd", grid=(nb,),
        in_specs=[blk, pl.BlockSpec(kv.shape, lambda n: (0, 0)), pl.BlockSpec((1, HEAD_DIM), lambda n: (0, 0)),
                  pl.BlockSpec((1, HEAD_DIM), lambda n: (0, 0)), pl.BlockSpec(sink_rows.shape, lambda n: (0, 0, 0)),
                  pl.BlockSpec(band.shape, lambda n: (0, 0))],
        out_specs=blk, out_shape=jax.ShapeDtypeStruct((S, W), BF16),
        scratch_shapes=[pltpu.VMEM((ATTN_BLOCK, W), F32)],
        compiler_params=_params(dimension_semantics=("arbitrary",)),
    )(q, kv, qw, kw, sink_rows, band)


def _attn_bwd(q, kv, do, qw, kw, sink_rows, band):
    S, W = q.shape
    KW = kv.shape[1]
    nb = S // ATTN_BLOCK
    chunk = 512

    def body(q_ref, kv_ref, do_ref, qw_ref, kw_ref, sk_ref, band_ref, dq_ref, dkv_ref, dqw_ref, dkw_ref, dsk_ref, dq_scr):
        n = pl.program_id(0)

        @pl.when(n == 0)
        def _():
            dkv_ref[...] = jnp.zeros_like(dkv_ref)
            dqw_ref[...] = jnp.zeros_like(dqw_ref)
            dsk_ref[...] = jnp.zeros_like(dsk_ref)

        qf = q_ref[...].astype(F32)
        dof = do_ref[...].astype(F32)
        cur, prev = _attn_blocks(n)
        kvw = _kv_window(kv_ref, n)
        qw_v = qw_ref[...]
        for g in range(N_KV_HEADS):
            qs = _stack_heads(qf, g)
            rq = _rinv_lanes(qs)
            qhat = qs * rq
            qn = (qhat * qw_v * SCALE).astype(BF16)
            k, v = _kv_split(kvw, g)
            kk = (k * _rinv_lanes(k) * kw_ref[...]).astype(BF16)
            vv = v.astype(BF16)
            dos = _stack_heads(dof, g).astype(BF16)
            _, e, e_s, inv = _attn_exp(qn, kk, sk_ref[g], band_ref[...], n)
            p = e * _twice(inv)
            dp = _dot_nt(dos, vv)
            drow = _rowsum((p * dp).astype(BF16), ATTN_BLOCK)
            ds = (p * (dp - _twice(drow))).astype(BF16)
            dsink = -(e_s * inv * drow)
            for hh in range(GQA_GROUP):
                tot = jnp.sum(dsink[hh * ATTN_BLOCK:(hh + 1) * ATTN_BLOCK, :], axis=0, keepdims=True)
                dsk_ref[g, hh:hh + 1, :] += tot
            dqn = _dot(ds, kk) * SCALE
            dkk = _dot_tn(ds, qn)
            dvv = _dot_tn(p.astype(BF16), dos)
            klo, vlo = g * HEAD_DIM, (N_KV_HEADS + g) * HEAD_DIM
            for start, rows in ((prev, slice(0, ATTN_BLOCK)), (cur, slice(ATTN_BLOCK, 2 * ATTN_BLOCK))):
                dkv_ref[pl.ds(start, ATTN_BLOCK), klo:klo + HEAD_DIM] += dkk[rows]
                dkv_ref[pl.ds(start, ATTN_BLOCK), vlo:vlo + HEAD_DIM] += dvv[rows]
            dqw_ref[...] += jnp.sum(dqn * qhat, axis=0, keepdims=True)
            z = dqn * qw_v
            dqs = rq * (z - qhat * (_rowsum((z * qhat).astype(BF16), HEAD_DIM) * (1.0 / HEAD_DIM)))
            _unstack_heads(dq_scr, dqs, g)
        dq_ref[...] = dq_scr[...].astype(BF16)

        @pl.when(n == nb - 1)
        def _():
            def one(c, dkw):
                rows = pl.ds(pl.multiple_of(c * chunk, chunk), chunk)
                for g in range(N_KV_HEADS):
                    lo = g * HEAD_DIM
                    k = kv_ref[rows, lo:lo + HEAD_DIM].astype(F32)
                    dx, dg_rows = _rms_bwd(dkv_ref[rows, lo:lo + HEAD_DIM], k, kw_ref[...])
                    dkv_ref[rows, lo:lo + HEAD_DIM] = dx
                    dkw = dkw + jnp.sum(dg_rows, axis=0, keepdims=True)
                return dkw

            dkw_ref[...] = lax.fori_loop(0, S // chunk, one, jnp.zeros((1, HEAD_DIM), F32))

    blk = pl.BlockSpec((ATTN_BLOCK, W), lambda n: (n, 0))
    whole_kv = pl.BlockSpec((S, KW), lambda n: (0, 0))
    vec = pl.BlockSpec((1, HEAD_DIM), lambda n: (0, 0))
    sk = pl.BlockSpec(sink_rows.shape, lambda n: (0, 0, 0))
    dsk = pl.BlockSpec((N_KV_HEADS, GQA_GROUP, 128), lambda n: (0, 0, 0))
    return pl.pallas_call(
        body, name="attn_bwd", grid=(nb,), in_specs=[blk, whole_kv, blk, vec, vec, sk, pl.BlockSpec(band.shape, lambda n: (0, 0))],
        out_specs=[blk, whole_kv, vec, vec, dsk],
        out_shape=[jax.ShapeDtypeStruct((S, W), BF16), jax.ShapeDtypeStruct((S, KW), F32),
                   jax.ShapeDtypeStruct((1, HEAD_DIM), F32), jax.ShapeDtypeStruct((1, HEAD_DIM), F32),
                   jax.ShapeDtypeStruct((N_KV_HEADS, GQA_GROUP, 128), F32)],
        scratch_shapes=[pltpu.VMEM((ATTN_BLOCK, W), F32)],
        compiler_params=_params(dimension_semantics=("arbitrary",)),
    )(q, kv, do, qw, kw, sink_rows, band)


def _pooled(xc, xprev, i):
    tm = xc.shape[0]
    xh = jnp.concatenate([jnp.where(i > 0, xprev, 0.0), xc], axis=0)
    t = lax.broadcasted_iota(jnp.int32, (tm, 1), 0) + i * tm
    out = []
    for gi, w in enumerate(POOL_WINDOWS):
        acc = xh[:, gi * POOL_GROUP:(gi + 1) * POOL_GROUP]
        sh = 1
        while sh < w:
            acc = acc + pltpu.roll(acc, sh, 0)
            sh *= 2
        cnt = jnp.minimum(t + 1, w).astype(F32)
        out.append(acc[POOL_HALO:, :] / cnt - xc[:, gi * POOL_GROUP:(gi + 1) * POOL_GROUP])
    return jnp.concatenate(out, axis=1)


def _pool_mix(pooled_b, pw_ref):
    return jnp.concatenate([_dot(pooled_b[:, gi * POOL_GROUP:(gi + 1) * POOL_GROUP], pw_ref[gi])
                            for gi in range(len(POOL_WINDOWS))], axis=1)


def _halo_specs(tm, width, S, after):
    per = tm // POOL_HALO
    last = S // POOL_HALO - 1
    if after:
        return pl.BlockSpec((POOL_HALO, width), lambda i: (jnp.minimum((i + 1) * per, last), 0))
    return pl.BlockSpec((POOL_HALO, width), lambda i: (jnp.maximum(i * per - 1, 0), 0))


def _mix_out(xp, attn, gl, bias, pw, pscale, wpot, wao, wo, h, tm=512):
    S, D = h.shape
    nt = S // tm
    PW = xp.shape[1]

    def body(xc_ref, xprev_ref, at_ref, gl_ref, bias_ref, pw_ref, ps_ref, wpo_ref, wao_ref, wo_ref, h_ref,
             ho_ref, bp_ref, ba_ref):
        i = pl.program_id(0)
        pooled = _pooled(xc_ref[...], xprev_ref[...], i).astype(BF16)
        ms = (_pool_mix(pooled, pw_ref) * ps_ref[...]).astype(BF16)
        bp = _dot_nt(ms, wpo_ref[...])
        ba = _dot(at_ref[...], wao_ref[...])
        bp_ref[...] = bp.astype(BF16)
        ba_ref[...] = ba.astype(BF16)
        gates = jax.nn.sigmoid(gl_ref[...].astype(F32) + bias_ref[...])
        merged = (gates[:, :D] * bp + gates[:, D:] * ba).astype(BF16)
        ho_ref[...] = h_ref[...] + _dot(merged, wo_ref[...])

    def row(width):
        return pl.BlockSpec((tm, width), lambda i: (i, 0))

    def whole(x):
        nd = x.ndim
        return pl.BlockSpec(x.shape, lambda i: (0,) * nd)

    return pl.pallas_call(
        body, name="mix_out", grid=(nt,),
        in_specs=[row(PW), _halo_specs(tm, PW, S, False), row(D), row(2 * D), whole(bias), whole(pw), whole(pscale),
                  whole(wpot), whole(wao), whole(wo), row(D)],
        out_specs=[row(D), row(D), row(D)],
        out_shape=[jax.ShapeDtypeStruct((S, D), F32), jax.ShapeDtypeStruct((S, D), BF16),
                   jax.ShapeDtypeStruct((S, D), BF16)],
        compiler_params=_params(dimension_semantics=("arbitrary",)),
    )(xp, xp, attn, gl, bias, pw, pscale, wpot, wao, wo, h)


def _mix_bwd_gate(dh, wo, bp, ba, gl, bias, tm=512):
    S, D = dh.shape
    nt = S // tm

    def body(dh_ref, wo_ref, bp_ref, ba_ref, gl_ref, bias_ref, dgl_ref, dbp_ref, dba_ref, dwo_ref, dbias_ref):
        i = pl.program_id(0)
        dhb = dh_ref[...].astype(BF16)
        dm = _dot_nt(dhb, wo_ref[...])
        gates = jax.nn.sigmoid(gl_ref[...].astype(F32) + bias_ref[...])
        gp, ga = gates[:, :D], gates[:, D:]
        bp_v = bp_ref[...].astype(F32)
        ba_v = ba_ref[...].astype(F32)
        merged = (gp * bp_v + ga * ba_v).astype(BF16)
        _acc(dwo_ref, _dot_tn(merged, dhb), i == 0)
        dbp_ref[...] = (dm * gp).astype(BF16)
        dba_ref[...] = (dm * ga).astype(BF16)
        dgl = jnp.concatenate([dm * bp_v * gp * (1.0 - gp), dm * ba_v * ga * (1.0 - ga)], axis=1)
        dgl_ref[...] = dgl.astype(BF16)
        _acc(dbias_ref, jnp.sum(dgl, axis=0, keepdims=True), i == 0)

    def row(width):
        return pl.BlockSpec((tm, width), lambda i: (i, 0))

    def whole(shape):
        return pl.BlockSpec(shape, lambda i: (0, 0))

    return pl.pallas_call(
        body, name="mix_bwd_gate", grid=(nt,),
        in_specs=[row(D), whole(wo.shape), row(D), row(D), row(2 * D), whole(bias.shape)],
        out_specs=[row(2 * D), row(D), row(D), whole((D, D)), whole((1, 2 * D))],
        out_shape=[jax.ShapeDtypeStruct((S, 2 * D), BF16), jax.ShapeDtypeStruct((S, D), BF16),
                   jax.ShapeDtypeStruct((S, D), BF16), jax.ShapeDtypeStruct((D, D), F32),
                   jax.ShapeDtypeStruct((1, 2 * D), F32)],
        compiler_params=_params(dimension_semantics=("arbitrary",)),
    )(dh, wo, bp, ba, gl, bias)


def _mix_bwd_branch(dbp, dba, attn, xp, pw, pscale, wpot, wao, tm=1024):
    S, D = dbp.shape
    nt = S // tm
    PW = xp.shape[1]
    NG = len(POOL_WINDOWS)

    def body(dbp_ref, dba_ref, at_ref, xc_ref, xprev_ref, pw_ref, ps_ref, wpo_ref, wao_ref,
             dat_ref, dpl_ref, dwao_ref, dwpo_ref, dpw_ref, dps_ref):
        i = pl.program_id(0)
        dba_v = dba_ref[...]
        dbp_v = dbp_ref[...]
        _acc(dwao_ref, _dot_tn(at_ref[...], dba_v), i == 0)
        dat_ref[...] = _dot_nt(dba_v, wao_ref[...]).astype(BF16)
        pooled = _pooled(xc_ref[...], xprev_ref[...], i).astype(BF16)
        mixed = _pool_mix(pooled, pw_ref)
        ps = ps_ref[...]
        _acc(dwpo_ref, _dot_tn(dbp_v, (mixed * ps).astype(BF16)), i == 0)
        dms = _dot(dbp_v, wpo_ref[...])
        _acc(dps_ref, jnp.sum(dms * mixed, axis=0, keepdims=True), i == 0)
        dmixed = (dms * ps).astype(BF16)
        dpooled = []
        for gi in range(NG):
            cols = slice(gi * POOL_GROUP, (gi + 1) * POOL_GROUP)
            _acc(dpw_ref.at[gi], _dot_tn(pooled[:, cols], dmixed[:, cols]), i == 0)
            dpooled.append(_dot_nt(dmixed[:, cols], pw_ref[gi]))
        dpl_ref[...] = jnp.concatenate(dpooled, axis=1)

    def row(width):
        return pl.BlockSpec((tm, width), lambda i: (i, 0))

    def whole(shape):
        nd = len(shape)
        return pl.BlockSpec(shape, lambda i: (0,) * nd)

    return pl.pallas_call(
        body, name="mix_bwd_branch", grid=(nt,),
        in_specs=[row(D), row(D), row(D), row(PW), _halo_specs(tm, PW, S, False), whole(pw.shape), whole(pscale.shape),
                  whole(wpot.shape), whole(wao.shape)],
        out_specs=[row(D), row(PW), whole((D, D)), whole((D, PW)), whole(pw.shape), whole((1, PW))],
        out_shape=[jax.ShapeDtypeStruct((S, D), BF16), jax.ShapeDtypeStruct((S, PW), F32),
                   jax.ShapeDtypeStruct((D, D), F32), jax.ShapeDtypeStruct((D, PW), F32),
                   jax.ShapeDtypeStruct(pw.shape, F32), jax.ShapeDtypeStruct((1, PW), F32)],
        compiler_params=_params(dimension_semantics=("arbitrary",)),
    )(dbp, dba, attn, xp, xp, pw, pscale, wpot, wao)


def _pool_bwd(dpooled, tm=512):
    S, PW = dpooled.shape
    nt = S // tm

    def body(dc_ref, dnext_ref, dxp_ref):
        i = pl.program_id(0)
        dc = dc_ref[...]
        dh = jnp.concatenate([dc, jnp.where(i < nt - 1, dnext_ref[...], 0.0)], axis=0)
        rows = tm + POOL_HALO
        t = lax.broadcasted_iota(jnp.int32, (rows, 1), 0) + i * tm
        out = []
        for gi, w in enumerate(POOL_WINDOWS):
            cols = slice(gi * POOL_GROUP, (gi + 1) * POOL_GROUP)
            acc = dh[:, cols] / jnp.minimum(t + 1, w).astype(F32)
            sh = 1
            while sh < w:
                acc = acc + pltpu.roll(acc, rows - sh, 0)
                sh *= 2
            out.append(acc[:tm, :] - dc[:, cols])
        dxp_ref[...] = jnp.concatenate(out, axis=1).astype(BF16)

    return pl.pallas_call(
        body, name="pool_bwd", grid=(nt,),
        in_specs=[pl.BlockSpec((tm, PW), lambda i: (i, 0)), _halo_specs(tm, PW, S, True)],
        out_specs=pl.BlockSpec((tm, PW), lambda i: (i, 0)), out_shape=jax.ShapeDtypeStruct((S, PW), BF16),
        compiler_params=_params(dimension_semantics=("arbitrary",)),
    )(dpooled, dpooled)


def _in_bwd_w(pieces, u, tm=1024):
    S, D = u.shape
    nt = S // tm
    NW = sum(width for _, width in SEGMENTS)

    def body(*refs):
        piece_refs, u_ref, dw_hbm, acc, sem = refs[:len(SEGMENTS)], refs[len(SEGMENTS)], refs[len(SEGMENTS) + 1], refs[-2], refs[-1]
        i = pl.program_id(0)
        u_t = u_ref[...]
        for (off, width), ref in zip(SEGMENTS, piece_refs):
            for lo in range(0, width, D):
                hi = min(lo + D, width)
                _acc(acc.at[off + lo:off + hi, :], _dot_tn(ref[:, lo:hi].astype(BF16), u_t), i == 0)

        @pl.when(i == nt - 1)
        def _():
            cp = pltpu.make_async_copy(acc, dw_hbm, sem)
            cp.start()
            cp.wait()

    return pl.pallas_call(
        body, name="in_bwd_w", grid=(nt,),
        in_specs=[pl.BlockSpec((tm, width), lambda i: (i, 0)) for _, width in SEGMENTS] + [pl.BlockSpec((tm, D), lambda i: (i, 0))],
        out_specs=pl.BlockSpec(memory_space=pl.ANY), out_shape=jax.ShapeDtypeStruct((NW, D), F32),
        scratch_shapes=[pltpu.VMEM((NW, D), F32), pltpu.SemaphoreType.DMA],
        compiler_params=_params(dimension_semantics=("arbitrary",)),
    )(*pieces, u)


def _in_bwd_x(pieces, wint, h, dh, gnorm, tm=512):
    S, D = h.shape
    nt = S // tm

    def body(*refs):
        piece_refs = refs[:len(SEGMENTS)]
        w_ref, h_ref, dh_ref, g_ref, dx_ref, dxh_ref, dg_ref = refs[len(SEGMENTS):]
        i = pl.program_id(0)
        du = jnp.zeros((tm, D), F32)
        for (off, width), ref in zip(SEGMENTS, piece_refs):
            du = du + _dot(ref[...].astype(BF16), w_ref[off:off + width, :])
        dx, dg_rows = _rms_bwd(du, h_ref[...], g_ref[...])
        out = dh_ref[...] + dx
        dx_ref[...] = out
        dxh_ref[...] = (0.5 * out).astype(BF16)
        _acc(dg_ref, jnp.sum(dg_rows, axis=0, keepdims=True), i == 0)

    def row(width):
        return pl.BlockSpec((tm, width), lambda i: (i, 0))

    vec = pl.BlockSpec((1, D), lambda i: (0, 0))
    return pl.pallas_call(
        body, name="in_bwd_x", grid=(nt,),
        in_specs=[row(width) for _, width in SEGMENTS] + [pl.BlockSpec(wint.shape, lambda i: (0, 0)), row(D), row(D), vec],
        out_specs=[row(D), row(D), vec],
        out_shape=[jax.ShapeDtypeStruct((S, D), F32), jax.ShapeDtypeStruct((S, D), BF16), jax.ShapeDtypeStruct((1, D), F32)],
        compiler_params=_params(dimension_semantics=("arbitrary",)),
    )(*pieces, wint, h, dh, gnorm)


def _row_tile(rows):
    for t in (512, 480, 352, 256, 128, 64, 32, 16, 8):
        if rows % t == 0:
            return t
    return rows


def _adamw_update(w, g, m, v):
    mn = ADAM_B1 * m + (1.0 - ADAM_B1) * g
    vn = ADAM_B2 * v + (1.0 - ADAM_B2) * (g * g)
    m_hat = mn / (1.0 - ADAM_B1 ** ADAM_STEP)
    v_hat = vn / (1.0 - ADAM_B2 ** ADAM_STEP)
    return -ADAM_LR * (m_hat / (jnp.sqrt(v_hat) + ADAM_EPS) + ADAM_WD * w), mn, vn


def _adamw_small(ws, gs, ms, vs):
    n = len(ws)

    def rows(a):
        return a.reshape(-1, a.shape[-1]) if a.ndim > 1 else a.reshape(1, -1)

    def body(*refs):
        for k in range(n):
            w_ref, g_ref, m_ref, v_ref, d_ref, mo_ref, vo_ref = refs[k::n]
            d_ref[...], mo_ref[...], vo_ref[...] = _adamw_update(w_ref[...], g_ref[...], m_ref[...], v_ref[...])

    vmem = pl.BlockSpec(memory_space=pltpu.VMEM)
    out = pl.pallas_call(
        body, name="adamw_small", in_specs=[vmem] * (4 * n), out_specs=[vmem] * (3 * n),
        out_shape=[jax.ShapeDtypeStruct(rows(w).shape, F32) for _ in range(3) for w in ws], compiler_params=_params(),
    )(*[rows(a) for a in ws + gs + ms + vs])
    return [[o.reshape(w.shape) for o, w in zip(out[k * n:(k + 1) * n], ws)] for k in range(3)]


def _adamw(w, g, m, v, token, name, turned=False):
    R, C = w.shape
    tr = R if turned else _row_tile(R)

    def body(w_ref, g_ref, m_ref, v_ref, _, d_ref, mo_ref, vo_ref, go_ref):
        gv = g_ref[...].T if turned else g_ref[...]
        go_ref[...] = gv
        d_ref[...], mo_ref[...], vo_ref[...] = _adamw_update(w_ref[...], gv, m_ref[...], v_ref[...])

    blk = pl.BlockSpec((tr, C), lambda i: (i, 0))
    g_blk = pl.BlockSpec((C, R), lambda i: (0, 0)) if turned else blk
    sh = jax.ShapeDtypeStruct((R, C), F32)
    return pl.pallas_call(
        body, name=name, grid=(R // tr,), in_specs=[blk, g_blk, blk, blk, TOKEN_SPEC], out_specs=[blk] * 4, out_shape=[sh] * 4,
        compiler_params=_params(dimension_semantics=("arbitrary",)),
    )(w, g, m, v, token)


def _cast_place(w, place, name, turned=False):
    R, C = w.shape[::-1] if turned else w.shape
    tr = R if turned else _row_tile(R)
    per = R // tr

    def body(p_ref, w_ref, o_ref):
        o_ref[...] = (w_ref[...].T if turned else w_ref[...]).astype(BF16)

    grid_spec = pltpu.PrefetchScalarGridSpec(
        num_scalar_prefetch=1, grid=(per,),
        in_specs=[pl.BlockSpec((C, R), lambda i, p: (0, 0)) if turned else pl.BlockSpec((tr, C), lambda i, p: (i, 0))],
        out_specs=pl.BlockSpec((tr, C), lambda i, p: (p[0] * per + i, 0)))
    return pl.pallas_call(
        body, name=name, grid_spec=grid_spec, out_shape=jax.ShapeDtypeStruct((N_SHARDS * R, C), BF16),
        compiler_params=_params(dimension_semantics=("arbitrary",)),
    )(place, w)


def _parked(w, m, idx, first, last):
    return jnp.where(m == w, idx, jnp.where(m < w, first, last))


def _sum_halves(g4s, recvs, place, name):
    M = len(g4s)
    NS, R, C = g4s[0].shape
    hr = R // 2
    tr = _row_tile(hr)
    per = hr // tr

    def body(p_ref, *refs):
        m = pl.program_id(0)
        for w in range(M):
            @pl.when(m == w)
            def _(w=w):
                refs[2 * M + w][...] = (refs[w][...] + refs[M + w][...]).astype(BF16)

    def spec(w, mine):
        def index(m, s, i, p):
            row = _parked(w, m, i, 0, per - 1)
            return _parked(w, m, s, 0, NS - 1), (p[1] * per + row if mine else row), 0

        return pl.BlockSpec((1, tr, C), index)

    grid_spec = pltpu.PrefetchScalarGridSpec(
        num_scalar_prefetch=1, grid=(M, NS, per),
        in_specs=[spec(w, True) for w in range(M)] + [spec(w, False) for w in range(M)],
        out_specs=[spec(w, False) for w in range(M)])
    return pl.pallas_call(
        body, name=name, grid_spec=grid_spec, out_shape=[jax.ShapeDtypeStruct((NS, hr, C), BF16)] * M,
        compiler_params=_params(dimension_semantics=("arbitrary",) * 3),
    )(place, *g4s, *recvs)


def _sum_quarters(h4s, recv3s, place, name):
    M = len(h4s)
    NS, hr, C = h4s[0].shape
    tr = _row_tile(hr)
    per = hr // tr

    def body(p_ref, *refs):
        m = pl.program_id(0)
        for w in range(M):
            @pl.when(m == w)
            def _(w=w):
                acc = refs[w][0].astype(F32)
                for k in range(N_SHARDS - 1):
                    acc = acc + refs[M + w][k].astype(F32)
                refs[2 * M + w][...] = acc

    def row(w, m, i):
        return _parked(w, m, i, 0, per - 1)

    grid_spec = pltpu.PrefetchScalarGridSpec(
        num_scalar_prefetch=1, grid=(M, per),
        in_specs=[pl.BlockSpec((1, tr, C), lambda m, i, p, w=w: (p[0], row(w, m, i), 0)) for w in range(M)]
        + [pl.BlockSpec((N_SHARDS - 1, tr, C), lambda m, i, p, w=w: (0, row(w, m, i), 0)) for w in range(M)],
        out_specs=[pl.BlockSpec((tr, C), lambda m, i, p, w=w: (p[1] * per + row(w, m, i), 0)) for w in range(M)])
    return pl.pallas_call(
        body, name=name, grid_spec=grid_spec, out_shape=[jax.ShapeDtypeStruct((2 * hr, C), F32)] * M,
        compiler_params=_params(dimension_semantics=("arbitrary",) * 2),
    )(place, *h4s, *recv3s)


def _by_shape(arrays):
    groups = {}
    for k, a in enumerate(arrays):
        groups.setdefault(a.shape, []).append(k)
    return list(groups.values())


def _place():
    x, y, c = lax.axis_index("x"), lax.axis_index("y"), lax.axis_index("c")
    chips = [(1 - x, y), (x, 1 - y), (1 - x, 1 - y)]
    return x, y, c, chips


HBM_SPEC = pl.BlockSpec(memory_space=pltpu.HBM)
SEM_SPEC = pl.BlockSpec(memory_space=pltpu.SEMAPHORE)
DATAFLOW = pltpu.SideEffectType.DATAFLOW_SIDE_EFFECTING


def _hbm(a):
    return pltpu.with_memory_space_constraint(a, pltpu.HBM)


class InFlight(NamedTuple):
    send_sem: jax.Array
    recv_sem: jax.Array
    bufs: list
    plan: Callable
    token: jax.Array


class Leg(NamedTuple):
    bufs: list
    landing: Optional[InFlight]
    plan: Optional[Callable]
    n_copies: int


def _send(bufs, plan, n_copies):
    return Leg([_hbm(b) for b in bufs], None, plan, n_copies)


def _land(flight):
    return Leg(flight.bufs, flight, None, 0)


def _forward(flight, plan, n_copies):
    return Leg(flight.bufs, flight, plan, n_copies)


def _wait_all(plan, refs, send_ref, recv_ref):
    for k, (src, dst, dev) in enumerate(plan(refs)):
        cp = pltpu.make_async_remote_copy(src_ref=src, dst_ref=dst, send_sem=send_ref.at[k], recv_sem=recv_ref.at[k],
                                          device_id=dev, device_id_type=MESH)
        cp.wait_send()
        cp.wait_recv()


def _start_all(plan, refs, send_ref, recv_ref):
    for k, (src, dst, dev) in enumerate(plan(refs)):
        pltpu.make_async_remote_copy(src_ref=src, dst_ref=dst, send_sem=send_ref.at[k], recv_sem=recv_ref.at[k],
                                     device_id=dev, device_id_type=MESH).start()


def _comm(name, legs, after):
    after = list(after) if isinstance(after, (list, tuple)) else [after]
    ins, in_specs, out_shape, out_specs, aliases, first = [], [], [], [], {}, []
    for leg in legs:
        first.append((len(ins), len(out_shape)))
        for b in leg.bufs:
            aliases[len(ins)] = len(out_shape)
            ins.append(b)
            in_specs.append(HBM_SPEC)
            out_shape.append(pltpu.HBM(b.shape, b.dtype))
            out_specs.append(HBM_SPEC)
        if leg.landing is not None:
            ins += [leg.landing.send_sem, leg.landing.recv_sem]
            in_specs += [SEM_SPEC, SEM_SPEC]
        if leg.plan is not None:
            out_shape += [pltpu.SemaphoreType.DMA((leg.n_copies,))] * 2
            out_specs += [SEM_SPEC, SEM_SPEC]
    starts = any(leg.plan is not None for leg in legs)
    if starts:
        out_shape.append(jax.ShapeDtypeStruct((8, 128), F32))
        out_specs.append(pl.BlockSpec(memory_space=pltpu.VMEM))
    n_in = len(ins) + len(after)

    def body(*refs):
        outs = refs[n_in:]
        for leg, (i, o) in zip(legs, first):
            nb = len(leg.bufs)
            bufs = refs[i:i + nb]
            if leg.landing is not None:
                _wait_all(leg.landing.plan, bufs, refs[i + nb], refs[i + nb + 1])
            if leg.plan is not None:
                _start_all(leg.plan, bufs, outs[o + nb], outs[o + nb + 1])
        if starts:
            outs[-1][...] = jnp.zeros_like(outs[-1])

    out = pl.pallas_call(
        body, name=name, in_specs=in_specs + [TOKEN_SPEC] * len(after), out_shape=out_shape, out_specs=out_specs,
        input_output_aliases=aliases, compiler_params=pltpu.CompilerParams(has_side_effects=DATAFLOW),
    )(*ins, *after)
    results = []
    for leg, (_, o) in zip(legs, first):
        nb = len(leg.bufs)
        bufs = list(out[o:o + nb])
        results.append(bufs if leg.plan is None else InFlight(out[o + nb], out[o + nb + 1], bufs, leg.plan, out[-1]))
    return results


def _half_rows(buf, chip, core):
    hr = buf.shape[0] // (2 * N_SHARDS)
    return buf.at[pl.ds(pl.multiple_of((2 * chip + core) * hr, 16), hr)]


def _gather_ici_plan(bufs):
    x, y, c, chips = _place()
    return [(_half_rows(b, 2 * x + y, c), _half_rows(b, 2 * x + y, c), (px, py, c)) for b in bufs for px, py in chips]


def _gather_d2d_plan(bufs):
    x, y, c, chips = _place()
    return [(_half_rows(b, 2 * px + py, c), _half_rows(b, 2 * px + py, c), (x, y, 1 - c)) for b in bufs for px, py in chips]


def _swap_plan(bufs):
    x, y, c, _ = _place()
    n = len(bufs) // 2
    copies = []
    for g, land in zip(bufs[:n], bufs[n:]):
        hr = g.shape[1] // 2
        copies.append((g.at[:, pl.ds(pl.multiple_of((1 - c) * hr, 8), hr)], land, (x, y, 1 - c)))
    return copies


def _exchange_plan(bufs):
    x, y, c, chips = _place()
    n = len(bufs) // 2
    return [(h.at[2 * px + py], land.at[k], (px, py, c))
            for h, land in zip(bufs[:n], bufs[n:]) for k, (px, py) in enumerate(chips)]


def _share_plan(bufs):
    x, y, c, _ = _place()
    copies = []
    for buf in bufs:
        hr = buf.shape[0] // 2
        mine = buf.at[pl.ds(pl.multiple_of(c * hr, 8), hr)]
        copies.append((mine, mine, (x, y, 1 - c)))
    return copies


N_DEVICES = 8


def _slot_place(vec, place):
    R, C = vec.shape

    def body(p_ref, v_ref, o_ref):
        o_ref[0] = v_ref[...]

    grid_spec = pltpu.PrefetchScalarGridSpec(
        num_scalar_prefetch=1, grid=(1,), in_specs=[pl.BlockSpec((R, C), lambda i, p: (0, 0))],
        out_specs=pl.BlockSpec((1, R, C), lambda i, p: (2 * p[0] + p[1], 0, 0)))
    return pl.pallas_call(
        body, name="slot_place", grid_spec=grid_spec, out_shape=jax.ShapeDtypeStruct((N_DEVICES, R, C), F32),
        compiler_params=_params(dimension_semantics=("arbitrary",)),
    )(place, vec)


def _slots_plan(bufs):
    x, y, c, _ = _place()
    mine = bufs[0].at[4 * x + 2 * y + c]
    return [(mine, mine, (x ^ (r >> 2), y ^ ((r >> 1) & 1), c ^ (r & 1))) for r in range(1, N_DEVICES)]


def _sum_slots(slots):
    _, R, C = slots.shape

    def body(s_ref, o_ref):
        acc = s_ref[0]
        for d in range(1, N_DEVICES):
            acc = acc + s_ref[d]
        o_ref[...] = acc

    return pl.pallas_call(
        body, name="sum_slots", in_specs=[pl.BlockSpec(memory_space=pltpu.VMEM)],
        out_specs=pl.BlockSpec(memory_space=pltpu.VMEM), out_shape=jax.ShapeDtypeStruct((R, C), F32),
        compiler_params=_params(),
    )(slots)


SMALL = ("ffn1_norm", "mix_norm", "pool_w", "pool_scale", "q_norm", "k_norm", "sinks", "gate_bias", "ffn2_norm")
SMALL_COLS = 1024
FFN1 = ("ffn1_w_gate", "ffn1_w_up", "ffn1_w_down")
MIXER = ("w_in", "w_pool_out", "w_attn_out", "w_out")
FFN2 = ("ffn2_w_gate", "ffn2_w_up", "ffn2_w_down")
LARGE = FFN1 + MIXER + FFN2
TRANSPOSED = ("ffn1_w_gate", "ffn1_w_up", "w_in", "w_pool_out", "ffn2_w_gate", "ffn2_w_up")
TURNED = ("w_pool_out",)
WEIGHTS = ("ffn1_norm", "ffn1_w_gate", "ffn1_w_up", "ffn1_w_down", "mix_norm", "w_in", "pool_w", "pool_scale",
           "w_pool_out", "q_norm", "k_norm", "sinks", "w_attn_out", "gate_bias", "w_out", "ffn2_norm",
           "ffn2_w_gate", "ffn2_w_up", "ffn2_w_down")


def _pack_small(parts):
    flat = jnp.concatenate([p.reshape(-1) for p in parts])
    rows = -(-flat.shape[0] // (8 * SMALL_COLS)) * 8
    return jnp.pad(flat, (0, rows * SMALL_COLS - flat.shape[0])).reshape(rows, SMALL_COLS)


def _unpack_small(packed, like):
    flat = packed.reshape(-1)
    out, off = [], 0
    for p in like:
        out.append(flat[off:off + p.size].reshape(p.shape))
        off += p.size
    return out, flat[off]


def _tie(small, token):
    return small + token[0, 0]


class Reduction:
    def __init__(self, names, place):
        self.names, self.place = names, place

    def swap(self, grads):
        grads = [g.reshape(N_SHARDS, -1, g.shape[-1]) for g in grads]
        lands = [lax.empty((g.shape[0], g.shape[1] // 2, g.shape[2]), g.dtype) for g in grads]
        return _send(grads + lands, _swap_plan, len(grads))

    def exchange(self, swapped):
        n = len(self.names)
        halves = self._per_shape(_sum_halves, "sum_halves", swapped[:n], swapped[n:])
        lands = [lax.empty((N_SHARDS - 1,) + h.shape[1:], h.dtype) for h in halves]
        return _send(halves + lands, _exchange_plan, (N_SHARDS - 1) * n)

    def share(self, exchanged):
        n = len(self.names)
        return _send(self._per_shape(_sum_quarters, "sum_quarters", exchanged[:n], exchanged[n:]), _share_plan, n)

    def named(self, shared):
        return dict(zip(self.names, shared))

    def _per_shape(self, add, stage, mine, received):
        out = [None] * len(mine)
        for idx in _by_shape(mine):
            sums = add([mine[k] for k in idx], [received[k] for k in idx], self.place, f"{stage}_{self.names[idx[0]]}")
            for k, v in zip(idx, sums):
                out[k] = v
        return out


def _row_form(name, a):
    return a.T if name in TRANSPOSED else a


def _shard_form(name, a):
    return a if name in TURNED else _row_form(name, a)


def kernel(x, ffn1_norm, ffn1_w_gate, ffn1_w_up, ffn1_w_down, mix_norm, w_in, pool_w, pool_scale, w_pool_out, q_norm, k_norm, sinks, w_attn_out, gate_bias, w_out, ffn2_norm, ffn2_w_gate, ffn2_w_up, ffn2_w_down, loss_target, m_ffn1_norm, m_ffn1_w_gate, m_ffn1_w_up, m_ffn1_w_down, m_mix_norm, m_w_in, m_pool_w, m_pool_scale, m_w_pool_out, m_q_norm, m_k_norm, m_sinks, m_w_attn_out, m_gate_bias, m_w_out, m_ffn2_norm, m_ffn2_w_gate, m_ffn2_w_up, m_ffn2_w_down, v_ffn1_norm, v_ffn1_w_gate, v_ffn1_w_up, v_ffn1_w_down, v_mix_norm, v_w_in, v_pool_w, v_pool_scale, v_w_pool_out, v_q_norm, v_k_norm, v_sinks, v_w_attn_out, v_gate_bias, v_w_out, v_ffn2_norm, v_ffn2_w_gate, v_ffn2_w_up, v_ffn2_w_down):
    args = dict(locals())
    wts = {n: _shard_form(n, args[n]) for n in WEIGHTS}
    mom = {n: _shard_form(n, args["m_" + n]) for n in WEIGHTS}
    var = {n: _shard_form(n, args["v_" + n]) for n in WEIGHTS}
    shard = 2 * lax.axis_index("x") + lax.axis_index("y")
    place = jnp.stack([shard, lax.axis_index("c")]).astype(jnp.int32)

    xs, target = x[0], loss_target[0]
    D = xs.shape[1]
    g1 = wts["ffn1_norm"].reshape(1, D)
    gm = wts["mix_norm"].reshape(1, D)
    g2 = wts["ffn2_norm"].reshape(1, D)
    qw = wts["q_norm"].reshape(1, HEAD_DIM)
    kw = wts["k_norm"].reshape(1, HEAD_DIM)
    bias = wts["gate_bias"].reshape(1, 2 * D)
    pscale = wts["pool_scale"].reshape(1, -1)
    pw = wts["pool_w"].astype(BF16)
    sink_rows = jnp.broadcast_to(jnp.repeat(wts["sinks"], ATTN_BLOCK).reshape(N_KV_HEADS, GQA_GROUP * ATTN_BLOCK, 1),
                                 (N_KV_HEADS, GQA_GROUP * ATTN_BLOCK, 128))

    gate1, up1, down1 = FFN1[:1], FFN1[1:2], FFN1[2:]
    mix_in, mix_out = MIXER[:1], MIXER[1:]

    def over_ici(names):
        return _send([placed[n] for n in names], _gather_ici_plan, 3 * len(names))

    def to_sibling(flight):
        return _forward(flight, _gather_d2d_plan, len(flight.bufs) * 3)

    placed = {n: _cast_place(wts[n], place, "cast_" + n) for n in gate1}
    gate1_ici, = _comm("gather_gate1_ici", [over_ici(gate1)], place)
    placed.update({n: _cast_place(wts[n], place, "cast_" + n) for n in up1 + down1})
    up1_ici, down1_ici = _comm("gather_ffn1_ici", [over_ici(up1), over_ici(down1)], gate1_ici.token)
    placed.update({n: _cast_place(wts[n], place, "cast_" + n, turned=n in TURNED) for n in MIXER + FFN2})
    band = _band_bias()
    shadow = [sink_rows, pw, band, down1_ici.token]
    gate1_d2d, in_ici, out_ici, ffn2_ici = _comm(
        "gather_rest_ici", [to_sibling(gate1_ici), over_ici(mix_in), over_ici(mix_out), over_ici(FFN2)], shadow)
    w = dict(zip(gate1, _comm("gather_gate1_wait", [_land(gate1_d2d)], gate1_d2d.token)[0]))

    n1, a1 = _ffn_gate(xs, g1, w["ffn1_w_gate"], "ffn1_gate")
    up1_d2d, down1_d2d = _comm("gather_ffn1_d2d", [to_sibling(up1_ici), to_sibling(down1_ici)], a1)
    w.update(zip(up1, _comm("gather_up1_wait", [_land(up1_d2d)], up1_d2d.token)[0]))
    b1, s1 = _ffn_hidden(n1, a1, w["ffn1_w_up"], "ffn1_hidden")
    w.update(zip(down1, _comm("gather_down1_wait", [_land(down1_d2d)], s1)[0]))
    h1 = _ffn_down(xs, s1, w["ffn1_w_down"], "ffn1_down")
    in_d2d, out_d2d = _comm("gather_mix_d2d", [to_sibling(in_ici), to_sibling(out_ici)], h1)
    w.update(zip(mix_in, _comm("gather_in_wait", [_land(in_d2d)], in_d2d.token)[0]))
    u, xp, q, kv, gl = _mix_proj(h1, gm, w["w_in"])
    attn = _attn_fwd(q, kv, qw, kw, sink_rows, band)
    ffn2_d2d, landed = _comm("gather_ffn2_d2d", [to_sibling(ffn2_ici), _land(out_d2d)], attn)
    w.update(zip(mix_out, landed))
    h2, bp, ba = _mix_out(xp, attn, gl, bias, pw, pscale, w["w_pool_out"], w["w_attn_out"], w["w_out"], h1)
    w.update(zip(FFN2, _comm("gather_ffn2_wait", [_land(ffn2_d2d)], h2)[0]))
    n2, a2, b2, s2 = _ffn_up(h2, g2, w["ffn2_w_gate"], w["ffn2_w_up"], "ffn2_up")
    dy, dyh, loss = _ffn_down_loss(h2, s2, w["ffn2_w_down"], target, "ffn2_down_loss")

    gw, gs = {}, {}
    da, db = _ffn_bwd_h(dyh, a2, b2, w["ffn2_w_down"], ffn2_d2d.token, "ffn2_bwd_h")
    gw["ffn2_w_gate"], gw["ffn2_w_up"] = _xty(da, n2, "ffn2_dw_gate"), _xty(db, n2, "ffn2_dw_up")
    gw["ffn2_w_down"] = _xty(s2, dyh, "ffn2_dw_down")
    red2 = Reduction(FFN2, place)
    swap, = _comm("reduce_ffn2_swap", [red2.swap([gw[n] for n in FFN2])], da)
    dh2, gs["ffn2_norm"] = _ffn_bwd_x(da, db, w["ffn2_w_gate"], w["ffn2_w_up"], h2, dy, _tie(g2, swap.token), "ffn2_bwd_x")
    swapped, = _comm("reduce_ffn2_swap_wait", [_land(swap)], dh2)
    exchange, = _comm("reduce_ffn2_exchange", [red2.exchange(swapped)], place)
    dgl, dbp, dba, gw["w_out"], gs["gate_bias"] = _mix_bwd_gate(dh2, w["w_out"], bp, ba, gl, _tie(bias, exchange.token))
    dattn, dpooled, gw["w_attn_out"], gw["w_pool_out"], gs["pool_w"], gs["pool_scale"] = _mix_bwd_branch(
        dbp, dba, attn, xp, pw, pscale, w["w_pool_out"], w["w_attn_out"])
    dq, dkv, gs["q_norm"], gs["k_norm"], dsk = _attn_bwd(q, kv, dattn, qw, kw, sink_rows, band)
    gs["sinks"] = dsk[:, :, 0]
    exchanged, = _comm("reduce_ffn2_exchange_wait", [_land(exchange)], dq)
    share2, = _comm("reduce_ffn2_share", [red2.share(exchanged)], place)
    pieces = (_pool_bwd(dpooled), dq, dkv, dgl)
    gw["w_in"] = _in_bwd_w(pieces, u)
    redm = Reduction(MIXER, place)
    swap, shared = _comm("reduce_mix_swap", [redm.swap([gw[n] for n in MIXER]), _land(share2)], share2.token)
    grads = red2.named(shared)
    dh1, dh1h, gs["mix_norm"] = _in_bwd_x(pieces, w["w_in"], h1, dh2, _tie(gm, swap.token))
    red1, swaps = Reduction(FFN1, place), {}
    dw_down = _xty(s1, dh1h, "ffn1_dw_down")
    swaps["ffn1_w_down"], swapped = _comm("reduce_mix_swap_wait", [red1.swap([dw_down]), _land(swap)], dh1)
    exchange, = _comm("reduce_mix_exchange", [redm.exchange(swapped)], place)
    da, db = _ffn_bwd_h(dh1h, a1, b1, w["ffn1_w_down"], exchange.token, "ffn1_bwd_h")
    dw_up = _xty(db, n1, "ffn1_dw_up")
    swaps["ffn1_w_up"], = _comm("reduce_ffn1_swap_up", [red1.swap([dw_up])], place)
    dw_gate = _xty(da, n1, "ffn1_dw_gate", after=[swaps["ffn1_w_up"].token])
    swaps["ffn1_w_gate"], exchanged = _comm("reduce_mix_exchange_wait", [red1.swap([dw_gate]), _land(exchange)], place)
    sharem, *landed = _comm("reduce_mix_share", [redm.share(exchanged)] + [_land(swaps[n]) for n in FFN1], place)
    exchange, = _comm("reduce_ffn1_exchange", [red1.exchange([l[0] for l in landed] + [l[1] for l in landed])], place)
    grad_x, gs["ffn1_norm"] = _ffn_bwd_x(da, db, w["ffn1_w_gate"], w["ffn1_w_up"], xs, dh1, _tie(g1, exchange.token), "ffn1_bwd_x")

    small_parts = [gs[n] for n in SMALL] + [loss[0, 0].reshape(1)]
    slots, shared = _comm(
        "gather_small", [_send([_slot_place(_pack_small(small_parts), place)], _slots_plan, N_DEVICES - 1), _land(sharem)],
        grad_x)
    grads.update(redm.named(shared))
    token = slots.token
    delta, new_m, new_v = {}, {}, {}
    for n in FFN2 + MIXER:
        delta[n], new_m[n], new_v[n], grads[n] = _adamw(wts[n], grads[n], mom[n], var[n], token, "adamw_" + n, n in TURNED)
    summed = _sum_slots(_comm("gather_small_wait", [_land(slots)], delta[MIXER[-1]])[0][0])
    small_grads, loss_sum = _unpack_small(summed, [wts[n] for n in SMALL])
    grads.update(dict(zip(SMALL, small_grads)))
    for out, vals in zip((delta, new_m, new_v), _adamw_small(*[[d[n] for n in SMALL] for d in (wts, grads, mom, var)])):
        out.update(dict(zip(SMALL, vals)))
    results = {(k, n): _shard_form(n, d[n]) for k, d in enumerate((grads, delta, new_m, new_v)) for n in SMALL + FFN2 + MIXER}
    exchanged, = _comm("reduce_ffn1_exchange_wait", [_land(exchange)],
                       [results[k, n] for k in range(4) for n in SMALL])
    share1, = _comm("reduce_ffn1_share", [red1.share(exchanged)], place)
    shared, = _comm("reduce_ffn1_share_wait", [_land(share1)], share1.token)
    grads.update(red1.named(shared))
    for n in FFN1:
        delta[n], new_m[n], new_v[n], grads[n] = _adamw(wts[n], grads[n], mom[n], var[n], token, "adamw_" + n, n in TURNED)
    results.update({(k, n): _shard_form(n, d[n]) for k, d in enumerate((grads, delta, new_m, new_v)) for n in FFN1})
    return (loss_sum, grad_x[None], *[results[k, n] for k in range(4) for n in WEIGHTS])
```

```python
from typing import Callable, NamedTuple, Optional

import jax
import jax.numpy as jnp
from jax import lax
from jax.experimental import pallas as pl
from jax.experimental.pallas import tpu as pltpu

F32 = jnp.float32
BF16 = jnp.bfloat16
RMS_EPS = 1e-6
POOL_WINDOWS = (2, 4, 8, 16)
POOL_GROUP = 128
POOL_HALO = 16
HEAD_DIM = 64
GQA_GROUP = 8
N_KV_HEADS = 2
ATTN_BLOCK = 128
SCALE = HEAD_DIM ** -0.5
NEG = -1e30
N_SHARDS = 4
ADAM_LR, ADAM_B1, ADAM_B2, ADAM_EPS, ADAM_WD, ADAM_STEP = 0.001, 0.9, 0.999, 1e-08, 0.01, 10
VMEM_LIMIT = 56 * 1024 * 1024
MESH = pl.DeviceIdType.MESH
SEG_POOL, SEG_Q, SEG_KV, SEG_GATE = (0, 512), (512, 1024), (1536, 256), (1792, 2048)
SEGMENTS = (SEG_POOL, SEG_Q, SEG_KV, SEG_GATE)


def _params(**kw):
    return pltpu.CompilerParams(vmem_limit_bytes=VMEM_LIMIT, **kw)


def _dot(a, b):
    return jnp.dot(a, b, preferred_element_type=F32)


def _dot_nt(a, b):
    return lax.dot_general(a, b, (((1,), (1,)), ((), ())), preferred_element_type=F32)


def _dot_tn(a, b):
    return lax.dot_general(a, b, (((0,), (0,)), ((), ())), preferred_element_type=F32)


def _rinv(x):
    return lax.rsqrt(jnp.mean(x * x, axis=-1, keepdims=True) + RMS_EPS)


def _rms_bwd(dn, x, g):
    r = _rinv(x)
    xr = x * r
    z = dn * g
    dx = r * (z - xr * jnp.mean(z * xr, axis=-1, keepdims=True))
    return dx, dn * xr


def _acc(ref, val, first):
    @pl.when(first)
    def _():
        ref[...] = val

    @pl.when(jnp.logical_not(first))
    def _():
        ref[...] += val


TOKEN_SPEC = pl.BlockSpec(memory_space=pl.ANY)
F_HALF = 1408


def _resident(w):
    return pl.BlockSpec(w.shape, lambda i: (0, 0), pipeline_mode=pl.Buffered(1))


def _swiglu(a, b):
    sig = jax.nn.sigmoid(a)
    q = a * sig
    return b * (sig + q * (1.0 - sig)), q, q * b


def _ffn_up(h, gnorm, wgt, wut, name, tm=512):
    S, D = h.shape
    F = wgt.shape[0]

    def body(h_ref, g_ref, wg_ref, wu_ref, n_ref, p_ref, q_ref, s_ref):
        x = h_ref[...]
        n = (x * _rinv(x) * g_ref[...]).astype(BF16)
        n_ref[...] = n
        for lo in range(0, F, F_HALF):
            cols = slice(lo, lo + F_HALF)
            p, q, s = _swiglu(_dot_nt(n, wg_ref[cols, :]), _dot_nt(n, wu_ref[cols, :]))
            p_ref[:, cols] = p.astype(BF16)
            q_ref[:, cols] = q.astype(BF16)
            s_ref[:, cols] = s.astype(BF16)

    act = pl.BlockSpec((tm, F), lambda i: (i, 0))
    hidden = jax.ShapeDtypeStruct((S, F), BF16)
    return pl.pallas_call(
        body, name=name, grid=(S // tm,),
        in_specs=[pl.BlockSpec((tm, D), lambda i: (i, 0)), pl.BlockSpec((1, D), lambda i: (0, 0)), _resident(wgt), _resident(wut)],
        out_specs=[pl.BlockSpec((tm, D), lambda i: (i, 0)), act, act, act],
        out_shape=[jax.ShapeDtypeStruct((S, D), BF16), hidden, hidden, hidden],
        compiler_params=_params(dimension_semantics=("arbitrary",)),
    )(h, gnorm, wgt, wut)


def _ffn_gate(h, gnorm, wgt, name, tm=512):
    S, D = h.shape
    F = wgt.shape[0]

    def body(h_ref, g_ref, wg_ref, n_ref, a_ref):
        x = h_ref[...]
        n = (x * _rinv(x) * g_ref[...]).astype(BF16)
        n_ref[...] = n
        for lo in range(0, F, F_HALF):
            a_ref[:, lo:lo + F_HALF] = _dot_nt(n, wg_ref[lo:lo + F_HALF, :]).astype(BF16)

    row = pl.BlockSpec((tm, D), lambda i: (i, 0))
    return pl.pallas_call(
        body, name=name, grid=(S // tm,), in_specs=[row, pl.BlockSpec((1, D), lambda i: (0, 0)), _resident(wgt)],
        out_specs=[row, pl.BlockSpec((tm, F), lambda i: (i, 0))],
        out_shape=[jax.ShapeDtypeStruct((S, D), BF16), jax.ShapeDtypeStruct((S, F), BF16)],
        compiler_params=_params(dimension_semantics=("arbitrary",)),
    )(h, gnorm, wgt)


def _ffn_hidden(n, a, wut, name, tm=512):
    S, D = n.shape
    F = wut.shape[0]

    def body(n_ref, a_ref, wu_ref, p_ref, q_ref, s_ref):
        n_t = n_ref[...]
        for lo in range(0, F, F_HALF):
            cols = slice(lo, lo + F_HALF)
            p, q, s = _swiglu(a_ref[:, cols].astype(F32), _dot_nt(n_t, wu_ref[cols, :]))
            p_ref[:, cols] = p.astype(BF16)
            q_ref[:, cols] = q.astype(BF16)
            s_ref[:, cols] = s.astype(BF16)

    act = pl.BlockSpec((tm, F), lambda i: (i, 0))
    hidden = jax.ShapeDtypeStruct((S, F), BF16)
    return pl.pallas_call(
        body, name=name, grid=(S // tm,), in_specs=[pl.BlockSpec((tm, D), lambda i: (i, 0)), act, _resident(wut)],
        out_specs=[act, act, act], out_shape=[hidden, hidden, hidden],
        compiler_params=_params(dimension_semantics=("arbitrary",)),
    )(n, a, wut)


def _ffn_down(h, s, wd, name, tm=1024):
    S, D = h.shape
    F = wd.shape[0]

    def body(h_ref, s_ref, wd_ref, o_ref):
        o_ref[...] = h_ref[...] + 0.5 * _dot(s_ref[...], wd_ref[...])

    row = pl.BlockSpec((tm, D), lambda i: (i, 0))
    return pl.pallas_call(
        body, name=name, grid=(S // tm,), in_specs=[row, pl.BlockSpec((tm, F), lambda i: (i, 0)), _resident(wd)],
        out_specs=row, out_shape=jax.ShapeDtypeStruct((S, D), F32),
        compiler_params=_params(dimension_semantics=("arbitrary",)),
    )(h, s, wd)


def _ffn_down_loss(h, s, wd, target, name, tm=512):
    S, D = h.shape
    F = wd.shape[0]

    def body(h_ref, s_ref, wd_ref, t_ref, dy_ref, dyh_ref, loss_ref):
        e = h_ref[...] + 0.5 * _dot(s_ref[...], wd_ref[...]) - t_ref[...]
        dy = e * (1.0 / D)
        dy_ref[...] = dy
        dyh_ref[...] = (0.5 * dy).astype(BF16)
        tot = jnp.sum(jnp.sum(e * e, axis=1, keepdims=True), axis=0, keepdims=True) * (0.5 / D)
        _acc(loss_ref, jnp.broadcast_to(tot, loss_ref.shape), pl.program_id(0) == 0)

    row = pl.BlockSpec((tm, D), lambda i: (i, 0))
    return pl.pallas_call(
        body, name=name, grid=(S // tm,), in_specs=[row, pl.BlockSpec((tm, F), lambda i: (i, 0)), _resident(wd), row],
        out_specs=[row, row, pl.BlockSpec((8, 128), lambda i: (0, 0))],
        out_shape=[jax.ShapeDtypeStruct((S, D), F32), jax.ShapeDtypeStruct((S, D), BF16), jax.ShapeDtypeStruct((8, 128), F32)],
        compiler_params=_params(dimension_semantics=("arbitrary",)),
    )(h, s, wd, target)


def _ffn_bwd_h(dyh, p, q, wd, token, name, tm=512):
    S, D = dyh.shape
    F = wd.shape[0]

    def body(dyh_ref, p_ref, q_ref, wd_ref, _, da_ref, db_ref):
        dyh_t = dyh_ref[...]
        for lo in range(0, F, F_HALF):
            cols = slice(lo, lo + F_HALF)
            ds = _dot_nt(dyh_t, wd_ref[cols, :])
            da_ref[:, cols] = (ds * p_ref[:, cols].astype(F32)).astype(BF16)
            db_ref[:, cols] = (ds * q_ref[:, cols].astype(F32)).astype(BF16)

    act = pl.BlockSpec((tm, F), lambda i: (i, 0))
    hidden = jax.ShapeDtypeStruct((S, F), BF16)
    return pl.pallas_call(
        body, name=name, grid=(S // tm,),
        in_specs=[pl.BlockSpec((tm, D), lambda i: (i, 0)), act, act, _resident(wd), TOKEN_SPEC],
        out_specs=[act, act], out_shape=[hidden, hidden],
        compiler_params=_params(dimension_semantics=("arbitrary",)),
    )(dyh, p, q, wd, token)


def _xty(x, y, name, tk=2048, tf=1408, after=()):
    S, F = x.shape
    D = y.shape[1]

    def body(x_ref, y_ref, *rest):
        _acc(rest[-1], _dot_tn(x_ref[...], y_ref[...]), pl.program_id(1) == 0)

    return pl.pallas_call(
        body, name=name, grid=(F // tf, S // tk),
        in_specs=[pl.BlockSpec((tk, tf), lambda j, k: (k, j)), pl.BlockSpec((tk, D), lambda j, k: (k, 0))]
        + [TOKEN_SPEC] * len(after),
        out_specs=pl.BlockSpec((tf, D), lambda j, k: (j, 0)), out_shape=jax.ShapeDtypeStruct((F, D), F32),
        compiler_params=_params(dimension_semantics=("arbitrary", "arbitrary")),
    )(x, y, *after)


def _ffn_bwd_x(da, db, wgt, wut, x, dh, gnorm, name, tm=512):
    S, D = x.shape
    F = wgt.shape[0]

    def body(da_ref, db_ref, wg_ref, wu_ref, x_ref, dh_ref, g_ref, dx_ref, dg_ref):
        dn = _dot(da_ref[...], wg_ref[...]) + _dot(db_ref[...], wu_ref[...])
        dx, dg_rows = _rms_bwd(dn, x_ref[...], g_ref[...])
        dx_ref[...] = dh_ref[...] + dx
        _acc(dg_ref, jnp.sum(dg_rows, axis=0, keepdims=True), pl.program_id(0) == 0)

    row = pl.BlockSpec((tm, D), lambda i: (i, 0))
    act = pl.BlockSpec((tm, F), lambda i: (i, 0))
    vec = pl.BlockSpec((1, D), lambda i: (0, 0))
    return pl.pallas_call(
        body, name=name, grid=(S // tm,), in_specs=[act, act, _resident(wgt), _resident(wut), row, row, vec],
        out_specs=[row, vec],
        out_shape=[jax.ShapeDtypeStruct((S, D), F32), jax.ShapeDtypeStruct((1, D), F32)],
        compiler_params=_params(dimension_semantics=("arbitrary",)),
    )(da, db, wgt, wut, x, dh, gnorm)


def _mix_proj(h, gnorm, wint, tm=512):
    S, D = h.shape
    nt = S // tm

    def body(h_ref, g_ref, w_ref, u_ref, xp_ref, q_ref, kv_ref, gl_ref):
        x = h_ref[...]
        u = (x * _rinv(x) * g_ref[...]).astype(BF16)
        u_ref[...] = u
        for (off, width), ref in zip(SEGMENTS, (xp_ref, q_ref, kv_ref, gl_ref)):
            ref[...] = _dot_nt(u, w_ref[off:off + width, :]).astype(ref.dtype)

    def row(width):
        return pl.BlockSpec((tm, width), lambda i: (i, 0))

    return pl.pallas_call(
        body, name="mix_proj", grid=(nt,),
        in_specs=[row(D), pl.BlockSpec((1, D), lambda i: (0, 0)), pl.BlockSpec(wint.shape, lambda i: (0, 0))],
        out_specs=[row(D), row(SEG_POOL[1]), row(SEG_Q[1]), row(SEG_KV[1]), row(SEG_GATE[1])],
        out_shape=[jax.ShapeDtypeStruct((S, D), BF16), jax.ShapeDtypeStruct((S, SEG_POOL[1]), F32),
                   jax.ShapeDtypeStruct((S, SEG_Q[1]), BF16), jax.ShapeDtypeStruct((S, SEG_KV[1]), BF16),
                   jax.ShapeDtypeStruct((S, SEG_GATE[1]), BF16)],
        compiler_params=_params(dimension_semantics=("arbitrary",)),
    )(h, gnorm, wint)


def _stack_heads(x, g):
    return jnp.concatenate([x[:, (GQA_GROUP * g + hh) * HEAD_DIM:(GQA_GROUP * g + hh + 1) * HEAD_DIM]
                            for hh in range(GQA_GROUP)], axis=0)


def _unstack_heads(ref, val, g):
    for hh in range(GQA_GROUP):
        lo = (GQA_GROUP * g + hh) * HEAD_DIM
        ref[:, lo:lo + HEAD_DIM] = val[hh * ATTN_BLOCK:(hh + 1) * ATTN_BLOCK, :]


def _rowsum(xb, width):
    return _dot(xb, jnp.ones((xb.shape[1], width), BF16))


def _rinv_lanes(x):
    return lax.rsqrt(_rowsum((x * x).astype(BF16), x.shape[1]) * (1.0 / x.shape[1]) + RMS_EPS)


def _twice(x):
    return jnp.concatenate([x, x], axis=1)


def _band_bias():
    qi = (jnp.arange(GQA_GROUP * ATTN_BLOCK) % ATTN_BLOCK)[:, None]
    kj = jnp.arange(2 * ATTN_BLOCK)[None, :]
    return jnp.where(jnp.logical_and(kj > qi, kj <= qi + ATTN_BLOCK), 0.0, NEG).astype(F32)


def _attn_exp(qn, kk, sink, band, n):
    kj = lax.broadcasted_iota(jnp.int32, (1, 2 * ATTN_BLOCK), 1)
    before_sequence = jnp.where(jnp.logical_and(kj < ATTN_BLOCK, n == 0), NEG, 0.0)
    s = _dot_nt(qn, kk) + band + before_sequence
    m = jnp.maximum(jnp.broadcast_to(jnp.max(s, axis=-1, keepdims=True), sink.shape), sink)
    e = jnp.exp(s - _twice(m))
    e_s = jnp.exp(sink - m)
    e_b = e.astype(BF16)
    inv = 1.0 / (_rowsum(e_b, ATTN_BLOCK) + e_s)
    return e_b, e, e_s, inv


def _attn_blocks(n):
    cur = pl.multiple_of(n * ATTN_BLOCK, ATTN_BLOCK)
    prev = pl.multiple_of(jnp.maximum(n - 1, 0) * ATTN_BLOCK, ATTN_BLOCK)
    return cur, prev


def _kv_window(kv_ref, n):
    cur, prev = _attn_blocks(n)
    return jnp.concatenate([kv_ref[pl.ds(prev, ATTN_BLOCK), :], kv_ref[pl.ds(cur, ATTN_BLOCK), :]], axis=0).astype(F32)


def _kv_split(kv, g):
    k = kv[:, g * HEAD_DIM:(g + 1) * HEAD_DIM]
    v = kv[:, (N_KV_HEADS + g) * HEAD_DIM:(N_KV_HEADS + g + 1) * HEAD_DIM]
    return k, v


def _attn_fwd(q, kv, qw, kw, sink_rows, band):
    S, W = q.shape
    nb = S // ATTN_BLOCK

    def body(q_ref, kv_ref, qw_ref, kw_ref, sk_ref, band_ref, o_ref, o_scr):
        n = pl.program_id(0)
        qf = q_ref[...].astype(F32)
        kvw = _kv_window(kv_ref, n)
        for g in range(N_KV_HEADS):
            qs = _stack_heads(qf, g)
            qn = (qs * _rinv_lanes(qs) * qw_ref[...] * SCALE).astype(BF16)
            k, v = _kv_split(kvw, g)
            kk = (k * _rinv_lanes(k) * kw_ref[...]).astype(BF16)
            e_b, _, _, inv = _attn_exp(qn, kk, sk_ref[g], band_ref[...], n)
            _unstack_heads(o_scr, _dot(e_b, v.astype(BF16)) * inv[:, :HEAD_DIM], g)
        o_ref[...] = o_scr[...].astype(BF16)

    blk = pl.BlockSpec((ATTN_BLOCK, W), lambda n: (n, 0))
    return pl.pallas_call(
        body, name="attn_fwd", grid=(nb,),
        in_specs=[blk, pl.BlockSpec(kv.shape, lambda n: (0, 0)), pl.BlockSpec((1, HEAD_DIM), lambda n: (0, 0)),
                  pl.BlockSpec((1, HEAD_DIM), lambda n: (0, 0)), pl.BlockSpec(sink_rows.shape, lambda n: (0, 0, 0)),
                  pl.BlockSpec(band.shape, lambda n: (0, 0))],
        out_specs=blk, out_shape=jax.ShapeDtypeStruct((S, W), BF16),
        scratch_shapes=[pltpu.VMEM((ATTN_BLOCK, W), F32)],
        compiler_params=_params(dimension_semantics=("arbitrary",)),
    )(q, kv, qw, kw, sink_rows, band)


def _attn_bwd(q, kv, do, qw, kw, sink_rows, band):
    S, W = q.shape
    KW = kv.shape[1]
    nb = S // ATTN_BLOCK
    chunk = 512

    def body(q_ref, kv_ref, do_ref, qw_ref, kw_ref, sk_ref, band_ref, dq_ref, dkv_ref, dqw_ref, dkw_ref, dsk_ref, dq_scr):
        n = pl.program_id(0)

        @pl.when(n == 0)
        def _():
            dkv_ref[...] = jnp.zeros_like(dkv_ref)
            dqw_ref[...] = jnp.zeros_like(dqw_ref)
            dsk_ref[...] = jnp.zeros_like(dsk_ref)

        qf = q_ref[...].astype(F32)
        dof = do_ref[...].astype(F32)
        cur, prev = _attn_blocks(n)
        kvw = _kv_window(kv_ref, n)
        qw_v = qw_ref[...]
        for g in range(N_KV_HEADS):
            qs = _stack_heads(qf, g)
            rq = _rinv_lanes(qs)
            qhat = qs * rq
            qn = (qhat * qw_v * SCALE).astype(BF16)
            k, v = _kv_split(kvw, g)
            kk = (k * _rinv_lanes(k) * kw_ref[...]).astype(BF16)
            vv = v.astype(BF16)
            dos = _stack_heads(dof, g).astype(BF16)
            _, e, e_s, inv = _attn_exp(qn, kk, sk_ref[g], band_ref[...], n)
            p = e * _twice(inv)
            dp = _dot_nt(dos, vv)
            drow = _rowsum((p * dp).astype(BF16), ATTN_BLOCK)
            ds = (p * (dp - _twice(drow))).astype(BF16)
            dsink = -(e_s * inv * drow)
            for hh in range(GQA_GROUP):
                tot = jnp.sum(dsink[hh * ATTN_BLOCK:(hh + 1) * ATTN_BLOCK, :], axis=0, keepdims=True)
                dsk_ref[g, hh:hh + 1, :] += tot
            dqn = _dot(ds, kk) * SCALE
            dkk = _dot_tn(ds, qn)
            dvv = _dot_tn(p.astype(BF16), dos)
            klo, vlo = g * HEAD_DIM, (N_KV_HEADS + g) * HEAD_DIM
            for start, rows in ((prev, slice(0, ATTN_BLOCK)), (cur, slice(ATTN_BLOCK, 2 * ATTN_BLOCK))):
                dkv_ref[pl.ds(start, ATTN_BLOCK), klo:klo + HEAD_DIM] += dkk[rows]
                dkv_ref[pl.ds(start, ATTN_BLOCK), vlo:vlo + HEAD_DIM] += dvv[rows]
            dqw_ref[...] += jnp.sum(dqn * qhat, axis=0, keepdims=True)
            z = dqn * qw_v
            dqs = rq * (z - qhat * (_rowsum((z * qhat).astype(BF16), HEAD_DIM) * (1.0 / HEAD_DIM)))
            _unstack_heads(dq_scr, dqs, g)
        dq_ref[...] = dq_scr[...].astype(BF16)

        @pl.when(n == nb - 1)
        def _():
            def one(c, dkw):
                rows = pl.ds(pl.multiple_of(c * chunk, chunk), chunk)
                for g in range(N_KV_HEADS):
                    lo = g * HEAD_DIM
                    k = kv_ref[rows, lo:lo + HEAD_DIM].astype(F32)
                    dx, dg_rows = _rms_bwd(dkv_ref[rows, lo:lo + HEAD_DIM], k, kw_ref[...])
                    dkv_ref[rows, lo:lo + HEAD_DIM] = dx
                    dkw = dkw + jnp.sum(dg_rows, axis=0, keepdims=True)
                return dkw

            dkw_ref[...] = lax.fori_loop(0, S // chunk, one, jnp.zeros((1, HEAD_DIM), F32))

    blk = pl.BlockSpec((ATTN_BLOCK, W), lambda n: (n, 0))
    whole_kv = pl.BlockSpec((S, KW), lambda n: (0, 0))
    vec = pl.BlockSpec((1, HEAD_DIM), lambda n: (0, 0))
    sk = pl.BlockSpec(sink_rows.shape, lambda n: (0, 0, 0))
    dsk = pl.BlockSpec((N_KV_HEADS, GQA_GROUP, 128), lambda n: (0, 0, 0))
    return pl.pallas_call(
        body, name="attn_bwd", grid=(nb,), in_specs=[blk, whole_kv, blk, vec, vec, sk, pl.BlockSpec(band.shape, lambda n: (0, 0))],
        out_specs=[blk, whole_kv, vec, vec, dsk],
        out_shape=[jax.ShapeDtypeStruct((S, W), BF16), jax.ShapeDtypeStruct((S, KW), F32),
                   jax.ShapeDtypeStruct((1, HEAD_DIM), F32), jax.ShapeDtypeStruct((1, HEAD_DIM), F32),
                   jax.ShapeDtypeStruct((N_KV_HEADS, GQA_GROUP, 128), F32)],
        scratch_shapes=[pltpu.VMEM((ATTN_BLOCK, W), F32)],
        compiler_params=_params(dimension_semantics=("arbitrary",)),
    )(q, kv, do, qw, kw, sink_rows, band)


def _pooled(xc, xprev, i):
    tm = xc.shape[0]
    xh = jnp.concatenate([jnp.where(i > 0, xprev, 0.0), xc], axis=0)
    t = lax.broadcasted_iota(jnp.int32, (tm, 1), 0) + i * tm
    out = []
    for gi, w in enumerate(POOL_WINDOWS):
        acc = xh[:, gi * POOL_GROUP:(gi + 1) * POOL_GROUP]
        sh = 1
        while sh < w:
            acc = acc + pltpu.roll(acc, sh, 0)
            sh *= 2
        cnt = jnp.minimum(t + 1, w).astype(F32)
        out.append(acc[POOL_HALO:, :] / cnt - xc[:, gi * POOL_GROUP:(gi + 1) * POOL_GROUP])
    return jnp.concatenate(out, axis=1)


def _pool_mix(pooled_b, pw_ref):
    return jnp.concatenate([_dot(pooled_b[:, gi * POOL_GROUP:(gi + 1) * POOL_GROUP], pw_ref[gi])
                            for gi in range(len(POOL_WINDOWS))], axis=1)


def _halo_specs(tm, width, S, after):
    per = tm // POOL_HALO
    last = S // POOL_HALO - 1
    if after:
        return pl.BlockSpec((POOL_HALO, width), lambda i: (jnp.minimum((i + 1) * per, last), 0))
    return pl.BlockSpec((POOL_HALO, width), lambda i: (jnp.maximum(i * per - 1, 0), 0))


def _mix_out(xp, attn, gl, bias, pw, pscale, wpot, wao, wo, h, tm=512):
    S, D = h.shape
    nt = S // tm
    PW = xp.shape[1]

    def body(xc_ref, xprev_ref, at_ref, gl_ref, bias_ref, pw_ref, ps_ref, wpo_ref, wao_ref, wo_ref, h_ref,
             ho_ref, bp_ref, ba_ref):
        i = pl.program_id(0)
        pooled = _pooled(xc_ref[...], xprev_ref[...], i).astype(BF16)
        ms = (_pool_mix(pooled, pw_ref) * ps_ref[...]).astype(BF16)
        bp = _dot_nt(ms, wpo_ref[...])
        ba = _dot(at_ref[...], wao_ref[...])
        bp_ref[...] = bp.astype(BF16)
        ba_ref[...] = ba.astype(BF16)
        gates = jax.nn.sigmoid(gl_ref[...].astype(F32) + bias_ref[...])
        merged = (gates[:, :D] * bp + gates[:, D:] * ba).astype(BF16)
        ho_ref[...] = h_ref[...] + _dot(merged, wo_ref[...])

    def row(width):
        return pl.BlockSpec((tm, width), lambda i: (i, 0))

    def whole(x):
        nd = x.ndim
        return pl.BlockSpec(x.shape, lambda i: (0,) * nd)

    return pl.pallas_call(
        body, name="mix_out", grid=(nt,),
        in_specs=[row(PW), _halo_specs(tm, PW, S, False), row(D), row(2 * D), whole(bias), whole(pw), whole(pscale),
                  whole(wpot), whole(wao), whole(wo), row(D)],
        out_specs=[row(D), row(D), row(D)],
        out_shape=[jax.ShapeDtypeStruct((S, D), F32), jax.ShapeDtypeStruct((S, D), BF16),
                   jax.ShapeDtypeStruct((S, D), BF16)],
        compiler_params=_params(dimension_semantics=("arbitrary",)),
    )(xp, xp, attn, gl, bias, pw, pscale, wpot, wao, wo, h)


def _mix_bwd_gate(dh, wo, bp, ba, gl, bias, tm=512):
    S, D = dh.shape
    nt = S // tm

    def body(dh_ref, wo_ref, bp_ref, ba_ref, gl_ref, bias_ref, dgl_ref, dbp_ref, dba_ref, dwo_ref, dbias_ref):
        i = pl.program_id(0)
        dhb = dh_ref[...].astype(BF16)
        dm = _dot_nt(dhb, wo_ref[...])
        gates = jax.nn.sigmoid(gl_ref[...].astype(F32) + bias_ref[...])
        gp, ga = gates[:, :D], gates[:, D:]
        bp_v = bp_ref[...].astype(F32)
        ba_v = ba_ref[...].astype(F32)
        merged = (gp * bp_v + ga * ba_v).astype(BF16)
        _acc(dwo_ref, _dot_tn(merged, dhb), i == 0)
        dbp_ref[...] = (dm * gp).astype(BF16)
        dba_ref[...] = (dm * ga).astype(BF16)
        dgl = jnp.concatenate([dm * bp_v * gp * (1.0 - gp), dm * ba_v * ga * (1.0 - ga)], axis=1)
        dgl_ref[...] = dgl.astype(BF16)
        _acc(dbias_ref, jnp.sum(dgl, axis=0, keepdims=True), i == 0)

    def row(width):
        return pl.BlockSpec((tm, width), lambda i: (i, 0))

    def whole(shape):
        return pl.BlockSpec(shape, lambda i: (0, 0))

    return pl.pallas_call(
        body, name="mix_bwd_gate", grid=(nt,),
        in_specs=[row(D), whole(wo.shape), row(D), row(D), row(2 * D), whole(bias.shape)],
        out_specs=[row(2 * D), row(D), row(D), whole((D, D)), whole((1, 2 * D))],
        out_shape=[jax.ShapeDtypeStruct((S, 2 * D), BF16), jax.ShapeDtypeStruct((S, D), BF16),
                   jax.ShapeDtypeStruct((S, D), BF16), jax.ShapeDtypeStruct((D, D), F32),
                   jax.ShapeDtypeStruct((1, 2 * D), F32)],
        compiler_params=_params(dimension_semantics=("arbitrary",)),
    )(dh, wo, bp, ba, gl, bias)


def _mix_bwd_branch(dbp, dba, attn, xp, pw, pscale, wpot, wao, tm=1024):
    S, D = dbp.shape
    nt = S // tm
    PW = xp.shape[1]
    NG = len(POOL_WINDOWS)

    def body(dbp_ref, dba_ref, at_ref, xc_ref, xprev_ref, pw_ref, ps_ref, wpo_ref, wao_ref,
             dat_ref, dpl_ref, dwao_ref, dwpo_ref, dpw_ref, dps_ref):
        i = pl.program_id(0)
        dba_v = dba_ref[...]
        dbp_v = dbp_ref[...]
        _acc(dwao_ref, _dot_tn(at_ref[...], dba_v), i == 0)
        dat_ref[...] = _dot_nt(dba_v, wao_ref[...]).astype(BF16)
        pooled = _pooled(xc_ref[...], xprev_ref[...], i).astype(BF16)
        mixed = _pool_mix(pooled, pw_ref)
        ps = ps_ref[...]
        _acc(dwpo_ref, _dot_tn(dbp_v, (mixed * ps).astype(BF16)), i == 0)
        dms = _dot(dbp_v, wpo_ref[...])
        _acc(dps_ref, jnp.sum(dms * mixed, axis=0, keepdims=True), i == 0)
        dmixed = (dms * ps).astype(BF16)
        dpooled = []
        for gi in range(NG):
            cols = slice(gi * POOL_GROUP, (gi + 1) * POOL_GROUP)
            _acc(dpw_ref.at[gi], _dot_tn(pooled[:, cols], dmixed[:, cols]), i == 0)
            dpooled.append(_dot_nt(dmixed[:, cols], pw_ref[gi]))
        dpl_ref[...] = jnp.concatenate(dpooled, axis=1)

    def row(width):
        return pl.BlockSpec((tm, width), lambda i: (i, 0))

    def whole(shape):
        nd = len(shape)
        return pl.BlockSpec(shape, lambda i: (0,) * nd)

    return pl.pallas_call(
        body, name="mix_bwd_branch", grid=(nt,),
        in_specs=[row(D), row(D), row(D), row(PW), _halo_specs(tm, PW, S, False), whole(pw.shape), whole(pscale.shape),
                  whole(wpot.shape), whole(wao.shape)],
        out_specs=[row(D), row(PW), whole((D, D)), whole((D, PW)), whole(pw.shape), whole((1, PW))],
        out_shape=[jax.ShapeDtypeStruct((S, D), BF16), jax.ShapeDtypeStruct((S, PW), F32),
                   jax.ShapeDtypeStruct((D, D), F32), jax.ShapeDtypeStruct((D, PW), F32),
                   jax.ShapeDtypeStruct(pw.shape, F32), jax.ShapeDtypeStruct((1, PW), F32)],
        compiler_params=_params(dimension_semantics=("arbitrary",)),
    )(dbp, dba, attn, xp, xp, pw, pscale, wpot, wao)


def _pool_bwd(dpooled, tm=512):
    S, PW = dpooled.shape
    nt = S // tm

    def body(dc_ref, dnext_ref, dxp_ref):
        i = pl.program_id(0)
        dc = dc_ref[...]
        dh = jnp.concatenate([dc, jnp.where(i < nt - 1, dnext_ref[...], 0.0)], axis=0)
        rows = tm + POOL_HALO
        t = lax.broadcasted_iota(jnp.int32, (rows, 1), 0) + i * tm
        out = []
        for gi, w in enumerate(POOL_WINDOWS):
            cols = slice(gi * POOL_GROUP, (gi + 1) * POOL_GROUP)
            acc = dh[:, cols] / jnp.minimum(t + 1, w).astype(F32)
            sh = 1
            while sh < w:
                acc = acc + pltpu.roll(acc, rows - sh, 0)
                sh *= 2
            out.append(acc[:tm, :] - dc[:, cols])
        dxp_ref[...] = jnp.concatenate(out, axis=1).astype(BF16)

    return pl.pallas_call(
        body, name="pool_bwd", grid=(nt,),
        in_specs=[pl.BlockSpec((tm, PW), lambda i: (i, 0)), _halo_specs(tm, PW, S, True)],
        out_specs=pl.BlockSpec((tm, PW), lambda i: (i, 0)), out_shape=jax.ShapeDtypeStruct((S, PW), BF16),
        compiler_params=_params(dimension_semantics=("arbitrary",)),
    )(dpooled, dpooled)


def _in_bwd_w(pieces, u, tm=1024):
    S, D = u.shape
    nt = S // tm
    NW = sum(width for _, width in SEGMENTS)

    def body(*refs):
        piece_refs, u_ref, dw_hbm, acc, sem = refs[:len(SEGMENTS)], refs[len(SEGMENTS)], refs[len(SEGMENTS) + 1], refs[-2], refs[-1]
        i = pl.program_id(0)
        u_t = u_ref[...]
        for (off, width), ref in zip(SEGMENTS, piece_refs):
            for lo in range(0, width, D):
                hi = min(lo + D, width)
                _acc(acc.at[off + lo:off + hi, :], _dot_tn(ref[:, lo:hi].astype(BF16), u_t), i == 0)

        @pl.when(i == nt - 1)
        def _():
            cp = pltpu.make_async_copy(acc, dw_hbm, sem)
            cp.start()
            cp.wait()

    return pl.pallas_call(
        body, name="in_bwd_w", grid=(nt,),
        in_specs=[pl.BlockSpec((tm, width), lambda i: (i, 0)) for _, width in SEGMENTS] + [pl.BlockSpec((tm, D), lambda i: (i, 0))],
        out_specs=pl.BlockSpec(memory_space=pl.ANY), out_shape=jax.ShapeDtypeStruct((NW, D), F32),
        scratch_shapes=[pltpu.VMEM((NW, D), F32), pltpu.SemaphoreType.DMA],
        compiler_params=_params(dimension_semantics=("arbitrary",)),
    )(*pieces, u)


def _in_bwd_x(pieces, wint, h, dh, gnorm, tm=512):
    S, D = h.shape
    nt = S // tm

    def body(*refs):
        piece_refs = refs[:len(SEGMENTS)]
        w_ref, h_ref, dh_ref, g_ref, dx_ref, dxh_ref, dg_ref = refs[len(SEGMENTS):]
        i = pl.program_id(0)
        du = jnp.zeros((tm, D), F32)
        for (off, width), ref in zip(SEGMENTS, piece_refs):
            du = du + _dot(ref[...].astype(BF16), w_ref[off:off + width, :])
        dx, dg_rows = _rms_bwd(du, h_ref[...], g_ref[...])
        out = dh_ref[...] + dx
        dx_ref[...] = out
        dxh_ref[...] = (0.5 * out).astype(BF16)
        _acc(dg_ref, jnp.sum(dg_rows, axis=0, keepdims=True), i == 0)

    def row(width):
        return pl.BlockSpec((tm, width), lambda i: (i, 0))

    vec = pl.BlockSpec((1, D), lambda i: (0, 0))
    return pl.pallas_call(
        body, name="in_bwd_x", grid=(nt,),
        in_specs=[row(width) for _, width in SEGMENTS] + [pl.BlockSpec(wint.shape, lambda i: (0, 0)), row(D), row(D), vec],
        out_specs=[row(D), row(D), vec],
        out_shape=[jax.ShapeDtypeStruct((S, D), F32), jax.ShapeDtypeStruct((S, D), BF16), jax.ShapeDtypeStruct((1, D), F32)],
        compiler_params=_params(dimension_semantics=("arbitrary",)),
    )(*pieces, wint, h, dh, gnorm)


def _row_tile(rows):
    for t in (512, 480, 352, 256, 128, 64, 32, 16, 8):
        if rows % t == 0:
            return t
    return rows


def _adamw_update(w, g, m, v):
    mn = ADAM_B1 * m + (1.0 - ADAM_B1) * g
    vn = ADAM_B2 * v + (1.0 - ADAM_B2) * (g * g)
    m_hat = mn / (1.0 - ADAM_B1 ** ADAM_STEP)
    v_hat = vn / (1.0 - ADAM_B2 ** ADAM_STEP)
    return -ADAM_LR * (m_hat / (jnp.sqrt(v_hat) + ADAM_EPS) + ADAM_WD * w), mn, vn


def _adamw_small(ws, gs, ms, vs):
    n = len(ws)

    def rows(a):
        return a.reshape(-1, a.shape[-1]) if a.ndim > 1 else a.reshape(1, -1)

    def body(*refs):
        for k in range(n):
            w_ref, g_ref, m_ref, v_ref, d_ref, mo_ref, vo_ref = refs[k::n]
            d_ref[...], mo_ref[...], vo_ref[...] = _adamw_update(w_ref[...], g_ref[...], m_ref[...], v_ref[...])

    vmem = pl.BlockSpec(memory_space=pltpu.VMEM)
    out = pl.pallas_call(
        body, name="adamw_small", in_specs=[vmem] * (4 * n), out_specs=[vmem] * (3 * n),
        out_shape=[jax.ShapeDtypeStruct(rows(w).shape, F32) for _ in range(3) for w in ws], compiler_params=_params(),
    )(*[rows(a) for a in ws + gs + ms + vs])
    return [[o.reshape(w.shape) for o, w in zip(out[k * n:(k + 1) * n], ws)] for k in range(3)]


def _adamw(w, g, m, v, token, name, turned=False):
    R, C = w.shape
    tr = R if turned else _row_tile(R)

    def body(w_ref, g_ref, m_ref, v_ref, _, d_ref, mo_ref, vo_ref, go_ref):
        gv = g_ref[...].T if turned else g_ref[...]
        go_ref[...] = gv
        d_ref[...], mo_ref[...], vo_ref[...] = _adamw_update(w_ref[...], gv, m_ref[...], v_ref[...])

    blk = pl.BlockSpec((tr, C), lambda i: (i, 0))
    g_blk = pl.BlockSpec((C, R), lambda i: (0, 0)) if turned else blk
    sh = jax.ShapeDtypeStruct((R, C), F32)
    return pl.pallas_call(
        body, name=name, grid=(R // tr,), in_specs=[blk, g_blk, blk, blk, TOKEN_SPEC], out_specs=[blk] * 4, out_shape=[sh] * 4,
        compiler_params=_params(dimension_semantics=("arbitrary",)),
    )(w, g, m, v, token)


def _cast_place(w, place, name, turned=False):
    R, C = w.shape[::-1] if turned else w.shape
    tr = R if turned else _row_tile(R)
    per = R // tr

    def body(p_ref, w_ref, o_ref):
        o_ref[...] = (w_ref[...].T if turned else w_ref[...]).astype(BF16)

    grid_spec = pltpu.PrefetchScalarGridSpec(
        num_scalar_prefetch=1, grid=(per,),
        in_specs=[pl.BlockSpec((C, R), lambda i, p: (0, 0)) if turned else pl.BlockSpec((tr, C), lambda i, p: (i, 0))],
        out_specs=pl.BlockSpec((tr, C), lambda i, p: (p[0] * per + i, 0)))
    return pl.pallas_call(
        body, name=name, grid_spec=grid_spec, out_shape=jax.ShapeDtypeStruct((N_SHARDS * R, C), BF16),
        compiler_params=_params(dimension_semantics=("arbitrary",)),
    )(place, w)


def _parked(w, m, idx, first, last):
    return jnp.where(m == w, idx, jnp.where(m < w, first, last))


def _sum_halves(g4s, recvs, place, name):
    M = len(g4s)
    NS, R, C = g4s[0].shape
    hr = R // 2
    tr = _row_tile(hr)
    per = hr // tr

    def body(p_ref, *refs):
        m = pl.program_id(0)
        for w in range(M):
            @pl.when(m == w)
            def _(w=w):
                refs[2 * M + w][...] = (refs[w][...] + refs[M + w][...]).astype(BF16)

    def spec(w, mine):
        def index(m, s, i, p):
            row = _parked(w, m, i, 0, per - 1)
            return _parked(w, m, s, 0, NS - 1), (p[1] * per + row if mine else row), 0

        return pl.BlockSpec((1, tr, C), index)

    grid_spec = pltpu.PrefetchScalarGridSpec(
        num_scalar_prefetch=1, grid=(M, NS, per),
        in_specs=[spec(w, True) for w in range(M)] + [spec(w, False) for w in range(M)],
        out_specs=[spec(w, False) for w in range(M)])
    return pl.pallas_call(
        body, name=name, grid_spec=grid_spec, out_shape=[jax.ShapeDtypeStruct((NS, hr, C), BF16)] * M,
        compiler_params=_params(dimension_semantics=("arbitrary",) * 3),
    )(place, *g4s, *recvs)


def _sum_quarters(h4s, recv3s, place, name):
    M = len(h4s)
    NS, hr, C = h4s[0].shape
    tr = _row_tile(hr)
    per = hr // tr

    def body(p_ref, *refs):
        m = pl.program_id(0)
        for w in range(M):
            @pl.when(m == w)
            def _(w=w):
                acc = refs[w][0].astype(F32)
                for k in range(N_SHARDS - 1):
                    acc = acc + refs[M + w][k].astype(F32)
                refs[2 * M + w][...] = acc

    def row(w, m, i):
        return _parked(w, m, i, 0, per - 1)

    grid_spec = pltpu.PrefetchScalarGridSpec(
        num_scalar_prefetch=1, grid=(M, per),
        in_specs=[pl.BlockSpec((1, tr, C), lambda m, i, p, w=w: (p[0], row(w, m, i), 0)) for w in range(M)]
        + [pl.BlockSpec((N_SHARDS - 1, tr, C), lambda m, i, p, w=w: (0, row(w, m, i), 0)) for w in range(M)],
        out_specs=[pl.BlockSpec((tr, C), lambda m, i, p, w=w: (p[1] * per + row(w, m, i), 0)) for w in range(M)])
    return pl.pallas_call(
        body, name=name, grid_spec=grid_spec, out_shape=[jax.ShapeDtypeStruct((2 * hr, C), F32)] * M,
        compiler_params=_params(dimension_semantics=("arbitrary",) * 2),
    )(place, *h4s, *recv3s)


def _by_shape(arrays):
    groups = {}
    for k, a in enumerate(arrays):
        groups.setdefault(a.shape, []).append(k)
    return list(groups.values())


def _place():
    x, y, c = lax.axis_index("x"), lax.axis_index("y"), lax.axis_index("c")
    chips = [(1 - x, y), (x, 1 - y), (1 - x, 1 - y)]
    return x, y, c, chips


HBM_SPEC = pl.BlockSpec(memory_space=pltpu.HBM)
SEM_SPEC = pl.BlockSpec(memory_space=pltpu.SEMAPHORE)
DATAFLOW = pltpu.SideEffectType.DATAFLOW_SIDE_EFFECTING


def _hbm(a):
    return pltpu.with_memory_space_constraint(a, pltpu.HBM)


class InFlight(NamedTuple):
    send_sem: jax.Array
    recv_sem: jax.Array
    bufs: list
    plan: Callable
    token: jax.Array


class Leg(NamedTuple):
    bufs: list
    landing: Optional[InFlight]
    plan: Optional[Callable]
    n_copies: int


def _send(bufs, plan, n_copies):
    return Leg([_hbm(b) for b in bufs], None, plan, n_copies)


def _land(flight):
    return Leg(flight.bufs, flight, None, 0)


def _forward(flight, plan, n_copies):
    return Leg(flight.bufs, flight, plan, n_copies)


def _wait_all(plan, refs, send_ref, recv_ref):
    for k, (src, dst, dev) in enumerate(plan(refs)):
        cp = pltpu.make_async_remote_copy(src_ref=src, dst_ref=dst, send_sem=send_ref.at[k], recv_sem=recv_ref.at[k],
                                          device_id=dev, device_id_type=MESH)
        cp.wait_send()
        cp.wait_recv()


def _start_all(plan, refs, send_ref, recv_ref):
    for k, (src, dst, dev) in enumerate(plan(refs)):
        pltpu.make_async_remote_copy(src_ref=src, dst_ref=dst, send_sem=send_ref.at[k], recv_sem=recv_ref.at[k],
                                     device_id=dev, device_id_type=MESH).start()


def _comm(name, legs, after):
    after = list(after) if isinstance(after, (list, tuple)) else [after]
    ins, in_specs, out_shape, out_specs, aliases, first = [], [], [], [], {}, []
    for leg in legs:
        first.append((len(ins), len(out_shape)))
        for b in leg.bufs:
            aliases[len(ins)] = len(out_shape)
            ins.append(b)
            in_specs.append(HBM_SPEC)
            out_shape.append(pltpu.HBM(b.shape, b.dtype))
            out_specs.append(HBM_SPEC)
        if leg.landing is not None:
            ins += [leg.landing.send_sem, leg.landing.recv_sem]
            in_specs += [SEM_SPEC, SEM_SPEC]
        if leg.plan is not None:
            out_shape += [pltpu.SemaphoreType.DMA((leg.n_copies,))] * 2
            out_specs += [SEM_SPEC, SEM_SPEC]
    starts = any(leg.plan is not None for leg in legs)
    if starts:
        out_shape.append(jax.ShapeDtypeStruct((8, 128), F32))
        out_specs.append(pl.BlockSpec(memory_space=pltpu.VMEM))
    n_in = len(ins) + len(after)

    def body(*refs):
        outs = refs[n_in:]
        for leg, (i, o) in zip(legs, first):
            nb = len(leg.bufs)
            bufs = refs[i:i + nb]
            if leg.landing is not None:
                _wait_all(leg.landing.plan, bufs, refs[i + nb], refs[i + nb + 1])
            if leg.plan is not None:
                _start_all(leg.plan, bufs, outs[o + nb], outs[o + nb + 1])
        if starts:
            outs[-1][...] = jnp.zeros_like(outs[-1])

    out = pl.pallas_call(
        body, name=name, in_specs=in_specs + [TOKEN_SPEC] * len(after), out_shape=out_shape, out_specs=out_specs,
        input_output_aliases=aliases, compiler_params=pltpu.CompilerParams(has_side_effects=DATAFLOW),
    )(*ins, *after)
    results = []
    for leg, (_, o) in zip(legs, first):
        nb = len(leg.bufs)
        bufs = list(out[o:o + nb])
        results.append(bufs if leg.plan is None else InFlight(out[o + nb], out[o + nb + 1], bufs, leg.plan, out[-1]))
    return results


def _half_rows(buf, chip, core):
    hr = buf.shape[0] // (2 * N_SHARDS)
    return buf.at[pl.ds(pl.multiple_of((2 * chip + core) * hr, 16), hr)]


def _gather_ici_plan(bufs):
    x, y, c, chips = _place()
    return [(_half_rows(b, 2 * x + y, c), _half_rows(b, 2 * x + y, c), (px, py, c)) for b in bufs for px, py in chips]


def _gather_d2d_plan(bufs):
    x, y, c, chips = _place()
    return [(_half_rows(b, 2 * px + py, c), _half_rows(b, 2 * px + py, c), (x, y, 1 - c)) for b in bufs for px, py in chips]


def _swap_plan(bufs):
    x, y, c, _ = _place()
    n = len(bufs) // 2
    copies = []
    for g, land in zip(bufs[:n], bufs[n:]):
        hr = g.shape[1] // 2
        copies.append((g.at[:, pl.ds(pl.multiple_of((1 - c) * hr, 8), hr)], land, (x, y, 1 - c)))
    return copies


def _exchange_plan(bufs):
    x, y, c, chips = _place()
    n = len(bufs) // 2
    return [(h.at[2 * px + py], land.at[k], (px, py, c))
            for h, land in zip(bufs[:n], bufs[n:]) for k, (px, py) in enumerate(chips)]


def _share_plan(bufs):
    x, y, c, _ = _place()
    copies = []
    for buf in bufs:
        hr = buf.shape[0] // 2
        mine = buf.at[pl.ds(pl.multiple_of(c * hr, 8), hr)]
        copies.append((mine, mine, (x, y, 1 - c)))
    return copies


N_DEVICES = 8


def _slot_place(vec, place):
    R, C = vec.shape

    def body(p_ref, v_ref, o_ref):
        o_ref[0] = v_ref[...]

    grid_spec = pltpu.PrefetchScalarGridSpec(
        num_scalar_prefetch=1, grid=(1,), in_specs=[pl.BlockSpec((R, C), lambda i, p: (0, 0))],
        out_specs=pl.BlockSpec((1, R, C), lambda i, p: (2 * p[0] + p[1], 0, 0)))
    return pl.pallas_call(
        body, name="slot_place", grid_spec=grid_spec, out_shape=jax.ShapeDtypeStruct((N_DEVICES, R, C), F32),
        compiler_params=_params(dimension_semantics=("arbitrary",)),
    )(place, vec)


def _slots_plan(bufs):
    x, y, c, _ = _place()
    mine = bufs[0].at[4 * x + 2 * y + c]
    return [(mine, mine, (x ^ (r >> 2), y ^ ((r >> 1) & 1), c ^ (r & 1))) for r in range(1, N_DEVICES)]


def _sum_slots(slots):
    _, R, C = slots.shape

    def body(s_ref, o_ref):
        acc = s_ref[0]
        for d in range(1, N_DEVICES):
            acc = acc + s_ref[d]
        o_ref[...] = acc

    return pl.pallas_call(
        body, name="sum_slots", in_specs=[pl.BlockSpec(memory_space=pltpu.VMEM)],
        out_specs=pl.BlockSpec(memory_space=pltpu.VMEM), out_shape=jax.ShapeDtypeStruct((R, C), F32),
        compiler_params=_params(),
    )(slots)


SMALL = ("ffn1_norm", "mix_norm", "pool_w", "pool_scale", "q_norm", "k_norm", "sinks", "gate_bias", "ffn2_norm")
SMALL_COLS = 1024
FFN1 = ("ffn1_w_gate", "ffn1_w_up", "ffn1_w_down")
MIXER = ("w_in", "w_pool_out", "w_attn_out", "w_out")
FFN2 = ("ffn2_w_gate", "ffn2_w_up", "ffn2_w_down")
LARGE = FFN1 + MIXER + FFN2
TRANSPOSED = ("ffn1_w_gate", "ffn1_w_up", "w_in", "w_pool_out", "ffn2_w_gate", "ffn2_w_up")
TURNED = ("w_pool_out",)
WEIGHTS = ("ffn1_norm", "ffn1_w_gate", "ffn1_w_up", "ffn1_w_down", "mix_norm", "w_in", "pool_w", "pool_scale",
           "w_pool_out", "q_norm", "k_norm", "sinks", "w_attn_out", "gate_bias", "w_out", "ffn2_norm",
           "ffn2_w_gate", "ffn2_w_up", "ffn2_w_down")


def _pack_small(parts):
    flat = jnp.concatenate([p.reshape(-1) for p in parts])
    rows = -(-flat.shape[0] // (8 * SMALL_COLS)) * 8
    return jnp.pad(flat, (0, rows * SMALL_COLS - flat.shape[0])).reshape(rows, SMALL_COLS)


def _unpack_small(packed, like):
    flat = packed.reshape(-1)
    out, off = [], 0
    for p in like:
        out.append(flat[off:off + p.size].reshape(p.shape))
        off += p.size
    return out, flat[off]


def _tie(small, token):
    return small + token[0, 0]


class Reduction:
    def __init__(self, names, place):
        self.names, self.place = names, place

    def swap(self, grads):
        grads = [g.reshape(N_SHARDS, -1, g.shape[-1]) for g in grads]
        lands = [lax.empty((g.shape[0], g.shape[1] // 2, g.shape[2]), g.dtype) for g in grads]
        return _send(grads + lands, _swap_plan, len(grads))

    def exchange(self, swapped):
        n = len(self.names)
        halves = self._per_shape(_sum_halves, "sum_halves", swapped[:n], swapped[n:])
        lands = [lax.empty((N_SHARDS - 1,) + h.shape[1:], h.dtype) for h in halves]
        return _send(halves + lands, _exchange_plan, (N_SHARDS - 1) * n)

    def share(self, exchanged):
        n = len(self.names)
        return _send(self._per_shape(_sum_quarters, "sum_quarters", exchanged[:n], exchanged[n:]), _share_plan, n)

    def named(self, shared):
        return dict(zip(self.names, shared))

    def _per_shape(self, add, stage, mine, received):
        out = [None] * len(mine)
        for idx in _by_shape(mine):
            sums = add([mine[k] for k in idx], [received[k] for k in idx], self.place, f"{stage}_{self.names[idx[0]]}")
            for k, v in zip(idx, sums):
                out[k] = v
        return out


def _row_form(name, a):
    return a.T if name in TRANSPOSED else a


def _shard_form(name, a):
    return a if name in TURNED else _row_form(name, a)


def kernel(x, ffn1_norm, ffn1_w_gate, ffn1_w_up, ffn1_w_down, mix_norm, w_in, pool_w, pool_scale, w_pool_out, q_norm, k_norm, sinks, w_attn_out, gate_bias, w_out, ffn2_norm, ffn2_w_gate, ffn2_w_up, ffn2_w_down, loss_target, m_ffn1_norm, m_ffn1_w_gate, m_ffn1_w_up, m_ffn1_w_down, m_mix_norm, m_w_in, m_pool_w, m_pool_scale, m_w_pool_out, m_q_norm, m_k_norm, m_sinks, m_w_attn_out, m_gate_bias, m_w_out, m_ffn2_norm, m_ffn2_w_gate, m_ffn2_w_up, m_ffn2_w_down, v_ffn1_norm, v_ffn1_w_gate, v_ffn1_w_up, v_ffn1_w_down, v_mix_norm, v_w_in, v_pool_w, v_pool_scale, v_w_pool_out, v_q_norm, v_k_norm, v_sinks, v_w_attn_out, v_gate_bias, v_w_out, v_ffn2_norm, v_ffn2_w_gate, v_ffn2_w_up, v_ffn2_w_down):
    args = dict(locals())
    wts = {n: _shard_form(n, args[n]) for n in WEIGHTS}
    mom = {n: _shard_form(n, args["m_" + n]) for n in WEIGHTS}
    var = {n: _shard_form(n, args["v_" + n]) for n in WEIGHTS}
    shard = 2 * lax.axis_index("x") + lax.axis_index("y")
    place = jnp.stack([shard, lax.axis_index("c")]).astype(jnp.int32)

    xs, target = x[0], loss_target[0]
    D = xs.shape[1]
    g1 = wts["ffn1_norm"].reshape(1, D)
    gm = wts["mix_norm"].reshape(1, D)
    g2 = wts["ffn2_norm"].reshape(1, D)
    qw = wts["q_norm"].reshape(1, HEAD_DIM)
    kw = wts["k_norm"].reshape(1, HEAD_DIM)
    bias = wts["gate_bias"].reshape(1, 2 * D)
    pscale = wts["pool_scale"].reshape(1, -1)
    pw = wts["pool_w"].astype(BF16)
    sink_rows = jnp.broadcast_to(jnp.repeat(wts["sinks"], ATTN_BLOCK).reshape(N_KV_HEADS, GQA_GROUP * ATTN_BLOCK, 1),
                                 (N_KV_HEADS, GQA_GROUP * ATTN_BLOCK, 128))

    gate1, up1, down1 = FFN1[:1], FFN1[1:2], FFN1[2:]
    mix_in, mix_out = MIXER[:1], MIXER[1:]

    def over_ici(names):
        return _send([placed[n] for n in names], _gather_ici_plan, 3 * len(names))

    def to_sibling(flight):
        return _forward(flight, _gather_d2d_plan, len(flight.bufs) * 3)

    placed = {n: _cast_place(wts[n], place, "cast_" + n) for n in gate1}
    gate1_ici, = _comm("gather_gate1_ici", [over_ici(gate1)], place)
    placed.update({n: _cast_place(wts[n], place, "cast_" + n) for n in up1 + down1})
    up1_ici, down1_ici = _comm("gather_ffn1_ici", [over_ici(up1), over_ici(down1)], gate1_ici.token)
    placed.update({n: _cast_place(wts[n], place, "cast_" + n, turned=n in TURNED) for n in MIXER + FFN2})
    band = _band_bias()
    shadow = [sink_rows, pw, band, down1_ici.token]
    gate1_d2d, in_ici, out_ici, ffn2_ici = _comm(
        "gather_rest_ici", [to_sibling(gate1_ici), over_ici(mix_in), over_ici(mix_out), over_ici(FFN2)], shadow)
    w = dict(zip(gate1, _comm("gather_gate1_wait", [_land(gate1_d2d)], gate1_d2d.token)[0]))

    n1, a1 = _ffn_gate(xs, g1, w["ffn1_w_gate"], "ffn1_gate")
    up1_d2d, down1_d2d = _comm("gather_ffn1_d2d", [to_sibling(up1_ici), to_sibling(down1_ici)], a1)
    w.update(zip(up1, _comm("gather_up1_wait", [_land(up1_d2d)], up1_d2d.token)[0]))
    p1, q1, s1 = _ffn_hidden(n1, a1, w["ffn1_w_up"], "ffn1_hidden")
    w.update(zip(down1, _comm("gather_down1_wait", [_land(down1_d2d)], s1)[0]))
    h1 = _ffn_down(xs, s1, w["ffn1_w_down"], "ffn1_down")
    in_d2d, out_d2d = _comm("gather_mix_d2d", [to_sibling(in_ici), to_sibling(out_ici)], h1)
    w.update(zip(mix_in, _comm("gather_in_wait", [_land(in_d2d)], in_d2d.token)[0]))
    u, xp, q, kv, gl = _mix_proj(h1, gm, w["w_in"])
    attn = _attn_fwd(q, kv, qw, kw, sink_rows, band)
    ffn2_d2d, landed = _comm("gather_ffn2_d2d", [to_sibling(ffn2_ici), _land(out_d2d)], attn)
    w.update(zip(mix_out, landed))
    h2, bp, ba = _mix_out(xp, attn, gl, bias, pw, pscale, w["w_pool_out"], w["w_attn_out"], w["w_out"], h1)
    w.update(zip(FFN2, _comm("gather_ffn2_wait", [_land(ffn2_d2d)], h2)[0]))
    n2, p2, q2, s2 = _ffn_up(h2, g2, w["ffn2_w_gate"], w["ffn2_w_up"], "ffn2_up")
    dy, dyh, loss = _ffn_down_loss(h2, s2, w["ffn2_w_down"], target, "ffn2_down_loss")

    gw, gs = {}, {}
    da, db = _ffn_bwd_h(dyh, p2, q2, w["ffn2_w_down"], ffn2_d2d.token, "ffn2_bwd_h")
    gw["ffn2_w_gate"], gw["ffn2_w_up"] = _xty(da, n2, "ffn2_dw_gate"), _xty(db, n2, "ffn2_dw_up")
    gw["ffn2_w_down"] = _xty(s2, dyh, "ffn2_dw_down")
    red2 = Reduction(FFN2, place)
    swap, = _comm("reduce_ffn2_swap", [red2.swap([gw[n] for n in FFN2])], da)
    dh2, gs["ffn2_norm"] = _ffn_bwd_x(da, db, w["ffn2_w_gate"], w["ffn2_w_up"], h2, dy, _tie(g2, swap.token), "ffn2_bwd_x")
    swapped, = _comm("reduce_ffn2_swap_wait", [_land(swap)], dh2)
    exchange, = _comm("reduce_ffn2_exchange", [red2.exchange(swapped)], place)
    dgl, dbp, dba, gw["w_out"], gs["gate_bias"] = _mix_bwd_gate(dh2, w["w_out"], bp, ba, gl, _tie(bias, exchange.token))
    dattn, dpooled, gw["w_attn_out"], gw["w_pool_out"], gs["pool_w"], gs["pool_scale"] = _mix_bwd_branch(
        dbp, dba, attn, xp, pw, pscale, w["w_pool_out"], w["w_attn_out"])
    dq, dkv, gs["q_norm"], gs["k_norm"], dsk = _attn_bwd(q, kv, dattn, qw, kw, sink_rows, band)
    gs["sinks"] = dsk[:, :, 0]
    exchanged, = _comm("reduce_ffn2_exchange_wait", [_land(exchange)], dq)
    share2, = _comm("reduce_ffn2_share", [red2.share(exchanged)], place)
    pieces = (_pool_bwd(dpooled), dq, dkv, dgl)
    gw["w_in"] = _in_bwd_w(pieces, u)
    redm = Reduction(MIXER, place)
    swap, shared = _comm("reduce_mix_swap", [redm.swap([gw[n] for n in MIXER]), _land(share2)], share2.token)
    grads = red2.named(shared)
    dh1, dh1h, gs["mix_norm"] = _in_bwd_x(pieces, w["w_in"], h1, dh2, _tie(gm, swap.token))
    red1, swaps = Reduction(FFN1, place), {}
    dw_down = _xty(s1, dh1h, "ffn1_dw_down")
    swaps["ffn1_w_down"], swapped = _comm("reduce_mix_swap_wait", [red1.swap([dw_down]), _land(swap)], dh1)
    exchange, = _comm("reduce_mix_exchange", [redm.exchange(swapped)], place)
    da, db = _ffn_bwd_h(dh1h, p1, q1, w["ffn1_w_down"], exchange.token, "ffn1_bwd_h")
    dw_up = _xty(db, n1, "ffn1_dw_up")
    swaps["ffn1_w_up"], = _comm("reduce_ffn1_swap_up", [red1.swap([dw_up])], place)
    dw_gate = _xty(da, n1, "ffn1_dw_gate", after=[swaps["ffn1_w_up"].token])
    swaps["ffn1_w_gate"], exchanged = _comm("reduce_mix_exchange_wait", [red1.swap([dw_gate]), _land(exchange)], place)
    sharem, *landed = _comm("reduce_mix_share", [redm.share(exchanged)] + [_land(swaps[n]) for n in FFN1], place)
    exchange, = _comm("reduce_ffn1_exchange", [red1.exchange([l[0] for l in landed] + [l[1] for l in landed])], place)
    grad_x, gs["ffn1_norm"] = _ffn_bwd_x(da, db, w["ffn1_w_gate"], w["ffn1_w_up"], xs, dh1, _tie(g1, exchange.token), "ffn1_bwd_x")

    small_parts = [gs[n] for n in SMALL] + [loss[0, 0].reshape(1)]
    slots, shared = _comm(
        "gather_small", [_send([_slot_place(_pack_small(small_parts), place)], _slots_plan, N_DEVICES - 1), _land(sharem)],
        grad_x)
    grads.update(redm.named(shared))
    delta, new_m, new_v = {}, {}, {}

    def adamw(names, token):
        for n in names:
            delta[n], new_m[n], new_v[n], grads[n] = _adamw(wts[n], grads[n], mom[n], var[n], token, "adamw_" + n, n in TURNED)

    early, late = FFN2 + MIXER[:-1], MIXER[-1:]
    adamw(early, slots.token)
    exchanged, = _comm("reduce_ffn1_exchange_wait", [_land(exchange)], [delta[n] for n in early])
    share1, landed = _comm("reduce_ffn1_share", [red1.share(exchanged), _land(slots)], place)
    summed = _sum_slots(landed[0])
    small_grads, loss_sum = _unpack_small(summed, [wts[n] for n in SMALL])
    grads.update(dict(zip(SMALL, small_grads)))
    for out, vals in zip((delta, new_m, new_v), _adamw_small(*[[d[n] for n in SMALL] for d in (wts, grads, mom, var)])):
        out.update(dict(zip(SMALL, vals)))
    adamw(late, share1.token)
    results = {(k, n): _shard_form(n, d[n]) for k, d in enumerate((grads, delta, new_m, new_v)) for n in SMALL + FFN2 + MIXER}
    shared, = _comm("reduce_ffn1_share_wait", [_land(share1)], [results[k, n] for k in range(4) for n in SMALL + late])
    grads.update(red1.named(shared))
    adamw(FFN1, share1.token)
    results.update({(k, n): _shard_form(n, d[n]) for k, d in enumerate((grads, delta, new_m, new_v)) for n in FFN1})
    return (loss_sum, grad_x[None], *[results[k, n] for k in range(4) for n in WEIGHTS])
```

```python
from typing import Callable, NamedTuple, Optional

import jax
import jax.numpy as jnp
from jax import lax
from jax.experimental import pallas as pl
from jax.experimental.pallas import tpu as pltpu

F32 = jnp.float32
BF16 = jnp.bfloat16
RMS_EPS = 1e-6
POOL_WINDOWS = (2, 4, 8, 16)
POOL_GROUP = 128
POOL_HALO = 16
HEAD_DIM = 64
GQA_GROUP = 8
N_KV_HEADS = 2
ATTN_BLOCK = 128
SCALE = HEAD_DIM ** -0.5
NEG = -1e30
N_SHARDS = 4
ADAM_LR, ADAM_B1, ADAM_B2, ADAM_EPS, ADAM_WD, ADAM_STEP = 0.001, 0.9, 0.999, 1e-08, 0.01, 10
VMEM_LIMIT = 56 * 1024 * 1024
MESH = pl.DeviceIdType.MESH
SEG_POOL, SEG_Q, SEG_KV, SEG_GATE = (0, 512), (512, 1024), (1536, 256), (1792, 2048)
SEGMENTS = (SEG_POOL, SEG_Q, SEG_KV, SEG_GATE)


def _params(**kw):
    return pltpu.CompilerParams(vmem_limit_bytes=VMEM_LIMIT, **kw)


def _dot(a, b):
    return jnp.dot(a, b, preferred_element_type=F32)


def _dot_nt(a, b):
    return lax.dot_general(a, b, (((1,), (1,)), ((), ())), preferred_element_type=F32)


def _dot_tn(a, b):
    return lax.dot_general(a, b, (((0,), (0,)), ((), ())), preferred_element_type=F32)


def _rinv(x):
    return lax.rsqrt(jnp.mean(x * x, axis=-1, keepdims=True) + RMS_EPS)


def _rms_bwd(dn, x, g):
    r = _rinv(x)
    xr = x * r
    z = dn * g
    dx = r * (z - xr * jnp.mean(z * xr, axis=-1, keepdims=True))
    return dx, dn * xr


def _acc(ref, val, first):
    @pl.when(first)
    def _():
        ref[...] = val

    @pl.when(jnp.logical_not(first))
    def _():
        ref[...] += val


TOKEN_SPEC = pl.BlockSpec(memory_space=pl.ANY)
F_HALF = 1408


def _resident(w):
    return pl.BlockSpec(w.shape, lambda i: (0, 0), pipeline_mode=pl.Buffered(1))


def _swiglu(a, b):
    sig = jax.nn.sigmoid(a)
    q = a * sig
    return b * (sig + q * (1.0 - sig)), q, q * b


def _ffn_up(h, gnorm, wgt, wut, name, tm=512):
    S, D = h.shape
    F = wgt.shape[0]

    def body(h_ref, g_ref, wg_ref, wu_ref, n_ref, p_ref, q_ref, s_ref):
        x = h_ref[...]
        n = (x * _rinv(x) * g_ref[...]).astype(BF16)
        n_ref[...] = n
        for lo in range(0, F, F_HALF):
            cols = slice(lo, lo + F_HALF)
            p, q, s = _swiglu(_dot_nt(n, wg_ref[cols, :]), _dot_nt(n, wu_ref[cols, :]))
            p_ref[:, cols] = p.astype(BF16)
            q_ref[:, cols] = q.astype(BF16)
            s_ref[:, cols] = s.astype(BF16)

    act = pl.BlockSpec((tm, F), lambda i: (i, 0))
    hidden = jax.ShapeDtypeStruct((S, F), BF16)
    return pl.pallas_call(
        body, name=name, grid=(S // tm,),
        in_specs=[pl.BlockSpec((tm, D), lambda i: (i, 0)), pl.BlockSpec((1, D), lambda i: (0, 0)), _resident(wgt), _resident(wut)],
        out_specs=[pl.BlockSpec((tm, D), lambda i: (i, 0)), act, act, act],
        out_shape=[jax.ShapeDtypeStruct((S, D), BF16), hidden, hidden, hidden],
        compiler_params=_params(dimension_semantics=("arbitrary",)),
    )(h, gnorm, wgt, wut)


def _ffn_gate(h, gnorm, wgt, name, tm=512):
    S, D = h.shape
    F = wgt.shape[0]

    def body(h_ref, g_ref, wg_ref, n_ref, a_ref):
        x = h_ref[...]
        n = (x * _rinv(x) * g_ref[...]).astype(BF16)
        n_ref[...] = n
        for lo in range(0, F, F_HALF):
            a_ref[:, lo:lo + F_HALF] = _dot_nt(n, wg_ref[lo:lo + F_HALF, :]).astype(BF16)

    row = pl.BlockSpec((tm, D), lambda i: (i, 0))
    return pl.pallas_call(
        body, name=name, grid=(S // tm,), in_specs=[row, pl.BlockSpec((1, D), lambda i: (0, 0)), _resident(wgt)],
        out_specs=[row, pl.BlockSpec((tm, F), lambda i: (i, 0))],
        out_shape=[jax.ShapeDtypeStruct((S, D), BF16), jax.ShapeDtypeStruct((S, F), BF16)],
        compiler_params=_params(dimension_semantics=("arbitrary",)),
    )(h, gnorm, wgt)


def _ffn_hidden(n, a, wut, name, tm=512):
    S, D = n.shape
    F = wut.shape[0]

    def body(n_ref, a_ref, wu_ref, p_ref, q_ref, s_ref):
        n_t = n_ref[...]
        for lo in range(0, F, F_HALF):
            cols = slice(lo, lo + F_HALF)
            p, q, s = _swiglu(a_ref[:, cols].astype(F32), _dot_nt(n_t, wu_ref[cols, :]))
            p_ref[:, cols] = p.astype(BF16)
            q_ref[:, cols] = q.astype(BF16)
            s_ref[:, cols] = s.astype(BF16)

    act = pl.BlockSpec((tm, F), lambda i: (i, 0))
    hidden = jax.ShapeDtypeStruct((S, F), BF16)
    return pl.pallas_call(
        body, name=name, grid=(S // tm,), in_specs=[pl.BlockSpec((tm, D), lambda i: (i, 0)), act, _resident(wut)],
        out_specs=[act, act, act], out_shape=[hidden, hidden, hidden],
        compiler_params=_params(dimension_semantics=("arbitrary",)),
    )(n, a, wut)


def _ffn_down(h, s, wd, name, tm=1024):
    S, D = h.shape
    F = wd.shape[0]

    def body(h_ref, s_ref, wd_ref, o_ref):
        o_ref[...] = h_ref[...] + 0.5 * _dot(s_ref[...], wd_ref[...])

    row = pl.BlockSpec((tm, D), lambda i: (i, 0))
    return pl.pallas_call(
        body, name=name, grid=(S // tm,), in_specs=[row, pl.BlockSpec((tm, F), lambda i: (i, 0)), _resident(wd)],
        out_specs=row, out_shape=jax.ShapeDtypeStruct((S, D), F32),
        compiler_params=_params(dimension_semantics=("arbitrary",)),
    )(h, s, wd)


def _ffn_down_loss(h, s, wd, target, name, tm=512):
    S, D = h.shape
    F = wd.shape[0]

    def body(h_ref, s_ref, wd_ref, t_ref, dy_ref, dyh_ref, loss_ref):
        e = h_ref[...] + 0.5 * _dot(s_ref[...], wd_ref[...]) - t_ref[...]
        dy = e * (1.0 / D)
        dy_ref[...] = dy
        dyh_ref[...] = (0.5 * dy).astype(BF16)
        tot = jnp.sum(jnp.sum(e * e, axis=1, keepdims=True), axis=0, keepdims=True) * (0.5 / D)
        _acc(loss_ref, jnp.broadcast_to(tot, loss_ref.shape), pl.program_id(0) == 0)

    row = pl.BlockSpec((tm, D), lambda i: (i, 0))
    return pl.pallas_call(
        body, name=name, grid=(S // tm,), in_specs=[row, pl.BlockSpec((tm, F), lambda i: (i, 0)), _resident(wd), row],
        out_specs=[row, row, pl.BlockSpec((8, 128), lambda i: (0, 0))],
        out_shape=[jax.ShapeDtypeStruct((S, D), F32), jax.ShapeDtypeStruct((S, D), BF16), jax.ShapeDtypeStruct((8, 128), F32)],
        compiler_params=_params(dimension_semantics=("arbitrary",)),
    )(h, s, wd, target)


def _ffn_bwd_h(dyh, p, q, wd, token, name, tm=512):
    S, D = dyh.shape
    F = wd.shape[0]

    def body(dyh_ref, p_ref, q_ref, wd_ref, _, da_ref, db_ref):
        dyh_t = dyh_ref[...]
        for lo in range(0, F, F_HALF):
            cols = slice(lo, lo + F_HALF)
            ds = _dot_nt(dyh_t, wd_ref[cols, :])
            da_ref[:, cols] = (ds * p_ref[:, cols].astype(F32)).astype(BF16)
            db_ref[:, cols] = (ds * q_ref[:, cols].astype(F32)).astype(BF16)

    act = pl.BlockSpec((tm, F), lambda i: (i, 0))
    hidden = jax.ShapeDtypeStruct((S, F), BF16)
    return pl.pallas_call(
        body, name=name, grid=(S // tm,),
        in_specs=[pl.BlockSpec((tm, D), lambda i: (i, 0)), act, act, _resident(wd), TOKEN_SPEC],
        out_specs=[act, act], out_shape=[hidden, hidden],
        compiler_params=_params(dimension_semantics=("arbitrary",)),
    )(dyh, p, q, wd, token)


def _xty(x, y, name, tk=2048, tf=1408, after=()):
    S, F = x.shape
    D = y.shape[1]

    def body(x_ref, y_ref, *rest):
        _acc(rest[-1], _dot_tn(x_ref[...], y_ref[...]), pl.program_id(1) == 0)

    return pl.pallas_call(
        body, name=name, grid=(F // tf, S // tk),
        in_specs=[pl.BlockSpec((tk, tf), lambda j, k: (k, j)), pl.BlockSpec((tk, D), lambda j, k: (k, 0))]
        + [TOKEN_SPEC] * len(after),
        out_specs=pl.BlockSpec((tf, D), lambda j, k: (j, 0)), out_shape=jax.ShapeDtypeStruct((F, D), F32),
        compiler_params=_params(dimension_semantics=("arbitrary", "arbitrary")),
    )(x, y, *after)


def _ffn_bwd_x(da, db, wgt, wut, x, dh, gnorm, name, tm=512):
    S, D = x.shape
    F = wgt.shape[0]

    def body(da_ref, db_ref, wg_ref, wu_ref, x_ref, dh_ref, g_ref, dx_ref, dg_ref):
        dn = _dot(da_ref[...], wg_ref[...]) + _dot(db_ref[...], wu_ref[...])
        dx, dg_rows = _rms_bwd(dn, x_ref[...], g_ref[...])
        dx_ref[...] = dh_ref[...] + dx
        _acc(dg_ref, jnp.sum(dg_rows, axis=0, keepdims=True), pl.program_id(0) == 0)

    row = pl.BlockSpec((tm, D), lambda i: (i, 0))
    act = pl.BlockSpec((tm, F), lambda i: (i, 0))
    vec = pl.BlockSpec((1, D), lambda i: (0, 0))
    return pl.pallas_call(
        body, name=name, grid=(S // tm,), in_specs=[act, act, _resident(wgt), _resident(wut), row, row, vec],
        out_specs=[row, vec],
        out_shape=[jax.ShapeDtypeStruct((S, D), F32), jax.ShapeDtypeStruct((1, D), F32)],
        compiler_params=_params(dimension_semantics=("arbitrary",)),
    )(da, db, wgt, wut, x, dh, gnorm)


def _mix_proj(h, gnorm, wint, tm=512):
    S, D = h.shape
    nt = S // tm

    def body(h_ref, g_ref, w_ref, u_ref, xp_ref, q_ref, kv_ref, gl_ref):
        x = h_ref[...]
        u = (x * _rinv(x) * g_ref[...]).astype(BF16)
        u_ref[...] = u
        for (off, width), ref in zip(SEGMENTS, (xp_ref, q_ref, kv_ref, gl_ref)):
            ref[...] = _dot_nt(u, w_ref[off:off + width, :]).astype(ref.dtype)

    def row(width):
        return pl.BlockSpec((tm, width), lambda i: (i, 0))

    return pl.pallas_call(
        body, name="mix_proj", grid=(nt,),
        in_specs=[row(D), pl.BlockSpec((1, D), lambda i: (0, 0)), pl.BlockSpec(wint.shape, lambda i: (0, 0))],
        out_specs=[row(D), row(SEG_POOL[1]), row(SEG_Q[1]), row(SEG_KV[1]), row(SEG_GATE[1])],
        out_shape=[jax.ShapeDtypeStruct((S, D), BF16), jax.ShapeDtypeStruct((S, SEG_POOL[1]), F32),
                   jax.ShapeDtypeStruct((S, SEG_Q[1]), BF16), jax.ShapeDtypeStruct((S, SEG_KV[1]), BF16),
                   jax.ShapeDtypeStruct((S, SEG_GATE[1]), BF16)],
        compiler_params=_params(dimension_semantics=("arbitrary",)),
    )(h, gnorm, wint)


def _stack_heads(x, g):
    return jnp.concatenate([x[:, (GQA_GROUP * g + hh) * HEAD_DIM:(GQA_GROUP * g + hh + 1) * HEAD_DIM]
                            for hh in range(GQA_GROUP)], axis=0)


def _unstack_heads(ref, val, g):
    for hh in range(GQA_GROUP):
        lo = (GQA_GROUP * g + hh) * HEAD_DIM
        ref[:, lo:lo + HEAD_DIM] = val[hh * ATTN_BLOCK:(hh + 1) * ATTN_BLOCK, :]


def _rowsum(xb, width):
    return _dot(xb, jnp.ones((xb.shape[1], width), BF16))


def _rinv_lanes(x):
    return lax.rsqrt(_rowsum((x * x).astype(BF16), x.shape[1]) * (1.0 / x.shape[1]) + RMS_EPS)


def _twice(x):
    return jnp.concatenate([x, x], axis=1)


def _band_bias():
    qi = (jnp.arange(GQA_GROUP * ATTN_BLOCK) % ATTN_BLOCK)[:, None]
    kj = jnp.arange(2 * ATTN_BLOCK)[None, :]
    return jnp.where(jnp.logical_and(kj > qi, kj <= qi + ATTN_BLOCK), 0.0, NEG).astype(F32)


def _attn_exp(qn, kk, sink, band, n):
    kj = lax.broadcasted_iota(jnp.int32, (1, 2 * ATTN_BLOCK), 1)
    before_sequence = jnp.where(jnp.logical_and(kj < ATTN_BLOCK, n == 0), NEG, 0.0)
    s = _dot_nt(qn, kk) + band + before_sequence
    m = jnp.maximum(jnp.broadcast_to(jnp.max(s, axis=-1, keepdims=True), sink.shape), sink)
    e = jnp.exp(s - _twice(m))
    e_s = jnp.exp(sink - m)
    e_b = e.astype(BF16)
    inv = 1.0 / (_rowsum(e_b, ATTN_BLOCK) + e_s)
    return e_b, e, e_s, inv


def _attn_blocks(n):
    cur = pl.multiple_of(n * ATTN_BLOCK, ATTN_BLOCK)
    prev = pl.multiple_of(jnp.maximum(n - 1, 0) * ATTN_BLOCK, ATTN_BLOCK)
    return cur, prev


def _kv_window(kv_ref, n):
    cur, prev = _attn_blocks(n)
    return jnp.concatenate([kv_ref[pl.ds(prev, ATTN_BLOCK), :], kv_ref[pl.ds(cur, ATTN_BLOCK), :]], axis=0).astype(F32)


def _kv_split(kv, g):
    k = kv[:, g * HEAD_DIM:(g + 1) * HEAD_DIM]
    v = kv[:, (N_KV_HEADS + g) * HEAD_DIM:(N_KV_HEADS + g + 1) * HEAD_DIM]
    return k, v


def _attn_fwd(q, kv, qw, kw, sink_rows, band):
    S, W = q.shape
    nb = S // ATTN_BLOCK

    def body(q_ref, kv_ref, qw_ref, kw_ref, sk_ref, band_ref, o_ref, o_scr):
        n = pl.program_id(0)
        qf = q_ref[...].astype(F32)
        kvw = _kv_window(kv_ref, n)
        for g in range(N_KV_HEADS):
            qs = _stack_heads(qf, g)
            qn = (qs * _rinv_lanes(qs) * qw_ref[...] * SCALE).astype(BF16)
            k, v = _kv_split(kvw, g)
            kk = (k * _rinv_lanes(k) * kw_ref[...]).astype(BF16)
            e_b, _, _, inv = _attn_exp(qn, kk, sk_ref[g], band_ref[...], n)
            _unstack_heads(o_scr, _dot(e_b, v.astype(BF16)) * inv[:, :HEAD_DIM], g)
        o_ref[...] = o_scr[...].astype(BF16)

    blk = pl.BlockSpec((ATTN_BLOCK, W), lambda n: (n, 0))
    return pl.pallas_call(
        body, name="attn_fwd", grid=(nb,),
        in_specs=[blk, pl.BlockSpec(kv.shape, lambda n: (0, 0)), pl.BlockSpec((1, HEAD_DIM), lambda n: (0, 0)),
                  pl.BlockSpec((1, HEAD_DIM), lambda n: (0, 0)), pl.BlockSpec(sink_rows.shape, lambda n: (0, 0, 0)),
                  pl.BlockSpec(band.shape, lambda n: (0, 0))],
        out_specs=blk, out_shape=jax.ShapeDtypeStruct((S, W), BF16),
        scratch_shapes=[pltpu.VMEM((ATTN_BLOCK, W), F32)],
        compiler_params=_params(dimension_semantics=("arbitrary",)),
    )(q, kv, qw, kw, sink_rows, band)


def _attn_bwd(q, kv, do, qw, kw, sink_rows, band):
    S, W = q.shape
    KW = kv.shape[1]
    nb = S // ATTN_BLOCK
    chunk = 512

    def body(q_ref, kv_ref, do_ref, qw_ref, kw_ref, sk_ref, band_ref, dq_ref, dkv_ref, dqw_ref, dkw_ref, dsk_ref, dq_scr):
        n = pl.program_id(0)

        @pl.when(n == 0)
        def _():
            dkv_ref[...] = jnp.zeros_like(dkv_ref)
            dqw_ref[...] = jnp.zeros_like(dqw_ref)
            dsk_ref[...] = jnp.zeros_like(dsk_ref)

        qf = q_ref[...].astype(F32)
        dof = do_ref[...].astype(F32)
        cur, prev = _attn_blocks(n)
        kvw = _kv_window(kv_ref, n)
        qw_v = qw_ref[...]
        for g in range(N_KV_HEADS):
            qs = _stack_heads(qf, g)
            rq = _rinv_lanes(qs)
            qhat = qs * rq
            qn = (qhat * qw_v * SCALE).astype(BF16)
            k, v = _kv_split(kvw, g)
            kk = (k * _rinv_lanes(k) * kw_ref[...]).astype(BF16)
            vv = v.astype(BF16)
            dos = _stack_heads(dof, g).astype(BF16)
            _, e, e_s, inv = _attn_exp(qn, kk, sk_ref[g], band_ref[...], n)
            p = e * _twice(inv)
            dp = _dot_nt(dos, vv)
            drow = _rowsum((p * dp).astype(BF16), ATTN_BLOCK)
            ds = (p * (dp - _twice(drow))).astype(BF16)
            dsink = -(e_s * inv * drow)
            for hh in range(GQA_GROUP):
                tot = jnp.sum(dsink[hh * ATTN_BLOCK:(hh + 1) * ATTN_BLOCK, :], axis=0, keepdims=True)
                dsk_ref[g, hh:hh + 1, :] += tot
            dqn = _dot(ds, kk) * SCALE
            dkk = _dot_tn(ds, qn)
            dvv = _dot_tn(p.astype(BF16), dos)
            klo, vlo = g * HEAD_DIM, (N_KV_HEADS + g) * HEAD_DIM
            for start, rows in ((prev, slice(0, ATTN_BLOCK)), (cur, slice(ATTN_BLOCK, 2 * ATTN_BLOCK))):
                dkv_ref[pl.ds(start, ATTN_BLOCK), klo:klo + HEAD_DIM] += dkk[rows]
                dkv_ref[pl.ds(start, ATTN_BLOCK), vlo:vlo + HEAD_DIM] += dvv[rows]
            dqw_ref[...] += jnp.sum(dqn * qhat, axis=0, keepdims=True)
            z = dqn * qw_v
            dqs = rq * (z - qhat * (_rowsum((z * qhat).astype(BF16), HEAD_DIM) * (1.0 / HEAD_DIM)))
            _unstack_heads(dq_scr, dqs, g)
        dq_ref[...] = dq_scr[...].astype(BF16)

        @pl.when(n == nb - 1)
        def _():
            def one(c, dkw):
                rows = pl.ds(pl.multiple_of(c * chunk, chunk), chunk)
                for g in range(N_KV_HEADS):
                    lo = g * HEAD_DIM
                    k = kv_ref[rows, lo:lo + HEAD_DIM].astype(F32)
                    dx, dg_rows = _rms_bwd(dkv_ref[rows, lo:lo + HEAD_DIM], k, kw_ref[...])
                    dkv_ref[rows, lo:lo + HEAD_DIM] = dx
                    dkw = dkw + jnp.sum(dg_rows, axis=0, keepdims=True)
                return dkw

            dkw_ref[...] = lax.fori_loop(0, S // chunk, one, jnp.zeros((1, HEAD_DIM), F32))

    blk = pl.BlockSpec((ATTN_BLOCK, W), lambda n: (n, 0))
    whole_kv = pl.BlockSpec((S, KW), lambda n: (0, 0))
    vec = pl.BlockSpec((1, HEAD_DIM), lambda n: (0, 0))
    sk = pl.BlockSpec(sink_rows.shape, lambda n: (0, 0, 0))
    dsk = pl.BlockSpec((N_KV_HEADS, GQA_GROUP, 128), lambda n: (0, 0, 0))
    return pl.pallas_call(
        body, name="attn_bwd", grid=(nb,), in_specs=[blk, whole_kv, blk, vec, vec, sk, pl.BlockSpec(band.shape, lambda n: (0, 0))],
        out_specs=[blk, whole_kv, vec, vec, dsk],
        out_shape=[jax.ShapeDtypeStruct((S, W), BF16), jax.ShapeDtypeStruct((S, KW), F32),
                   jax.ShapeDtypeStruct((1, HEAD_DIM), F32), jax.ShapeDtypeStruct((1, HEAD_DIM), F32),
                   jax.ShapeDtypeStruct((N_KV_HEADS, GQA_GROUP, 128), F32)],
        scratch_shapes=[pltpu.VMEM((ATTN_BLOCK, W), F32)],
        compiler_params=_params(dimension_semantics=("arbitrary",)),
    )(q, kv, do, qw, kw, sink_rows, band)


def _pooled(xc, xprev, i):
    tm = xc.shape[0]
    xh = jnp.concatenate([jnp.where(i > 0, xprev, 0.0), xc], axis=0)
    t = lax.broadcasted_iota(jnp.int32, (tm, 1), 0) + i * tm
    out = []
    for gi, w in enumerate(POOL_WINDOWS):
        acc = xh[:, gi * POOL_GROUP:(gi + 1) * POOL_GROUP]
        sh = 1
        while sh < w:
            acc = acc + pltpu.roll(acc, sh, 0)
            sh *= 2
        cnt = jnp.minimum(t + 1, w).astype(F32)
        out.append(acc[POOL_HALO:, :] / cnt - xc[:, gi * POOL_GROUP:(gi + 1) * POOL_GROUP])
    return jnp.concatenate(out, axis=1)


def _pool_mix(pooled_b, pw_ref):
    return jnp.concatenate([_dot(pooled_b[:, gi * POOL_GROUP:(gi + 1) * POOL_GROUP], pw_ref[gi])
                            for gi in range(len(POOL_WINDOWS))], axis=1)


def _halo_specs(tm, width, S, after):
    per = tm // POOL_HALO
    last = S // POOL_HALO - 1
    if after:
        return pl.BlockSpec((POOL_HALO, width), lambda i: (jnp.minimum((i + 1) * per, last), 0))
    return pl.BlockSpec((POOL_HALO, width), lambda i: (jnp.maximum(i * per - 1, 0), 0))


def _mix_out(xp, attn, gl, bias, pw, pscale, wpot, wao, wo, h, tm=512):
    S, D = h.shape
    nt = S // tm
    PW = xp.shape[1]

    def body(xc_ref, xprev_ref, at_ref, gl_ref, bias_ref, pw_ref, ps_ref, wpo_ref, wao_ref, wo_ref, h_ref,
             ho_ref, gates_ref, fac_ref, merged_ref):
        i = pl.program_id(0)
        pooled = _pooled(xc_ref[...], xprev_ref[...], i).astype(BF16)
        ms = (_pool_mix(pooled, pw_ref) * ps_ref[...]).astype(BF16)
        gates = jax.nn.sigmoid(gl_ref[...].astype(F32) + bias_ref[...])
        gates_ref[...] = gates.astype(BF16)
        gated = gates * jnp.concatenate([_dot_nt(ms, wpo_ref[...]), _dot(at_ref[...], wao_ref[...])], axis=1)
        fac_ref[...] = (gated * (1.0 - gates)).astype(BF16)
        merged = (gated[:, :D] + gated[:, D:]).astype(BF16)
        merged_ref[...] = merged
        ho_ref[...] = h_ref[...] + _dot(merged, wo_ref[...])

    def row(width):
        return pl.BlockSpec((tm, width), lambda i: (i, 0))

    def whole(x):
        nd = x.ndim
        return pl.BlockSpec(x.shape, lambda i: (0,) * nd)

    return pl.pallas_call(
        body, name="mix_out", grid=(nt,),
        in_specs=[row(PW), _halo_specs(tm, PW, S, False), row(D), row(2 * D), whole(bias), whole(pw), whole(pscale),
                  whole(wpot), whole(wao), whole(wo), row(D)],
        out_specs=[row(D), row(2 * D), row(2 * D), row(D)],
        out_shape=[jax.ShapeDtypeStruct((S, D), F32), jax.ShapeDtypeStruct((S, 2 * D), BF16),
                   jax.ShapeDtypeStruct((S, 2 * D), BF16), jax.ShapeDtypeStruct((S, D), BF16)],
        compiler_params=_params(dimension_semantics=("arbitrary",)),
    )(xp, xp, attn, gl, bias, pw, pscale, wpot, wao, wo, h)


def _mix_bwd_gate(dh, wo, gates, fac, merged, token, tm=512):
    S, D = dh.shape
    nt = S // tm

    def body(dh_ref, wo_ref, gates_ref, fac_ref, merged_ref, _, dgl_ref, dbp_ref, dba_ref, dwo_ref, dbias_ref):
        i = pl.program_id(0)
        dhb = dh_ref[...].astype(BF16)
        dm = _dot_nt(dhb, wo_ref[...])
        _acc(dwo_ref, _dot_tn(merged_ref[...], dhb), i == 0)
        dbp_ref[...] = (dm * gates_ref[:, :D].astype(F32)).astype(BF16)
        dba_ref[...] = (dm * gates_ref[:, D:].astype(F32)).astype(BF16)
        dgl = jnp.concatenate([dm, dm], axis=1) * fac_ref[...].astype(F32)
        dgl_ref[...] = dgl.astype(BF16)
        _acc(dbias_ref, jnp.sum(dgl, axis=0, keepdims=True), i == 0)

    def row(width):
        return pl.BlockSpec((tm, width), lambda i: (i, 0))

    def whole(shape):
        return pl.BlockSpec(shape, lambda i: (0, 0))

    return pl.pallas_call(
        body, name="mix_bwd_gate", grid=(nt,),
        in_specs=[row(D), whole(wo.shape), row(2 * D), row(2 * D), row(D), TOKEN_SPEC],
        out_specs=[row(2 * D), row(D), row(D), whole((D, D)), whole((1, 2 * D))],
        out_shape=[jax.ShapeDtypeStruct((S, 2 * D), BF16), jax.ShapeDtypeStruct((S, D), BF16),
                   jax.ShapeDtypeStruct((S, D), BF16), jax.ShapeDtypeStruct((D, D), F32),
                   jax.ShapeDtypeStruct((1, 2 * D), F32)],
        compiler_params=_params(dimension_semantics=("arbitrary",)),
    )(dh, wo, gates, fac, merged, token)


def _mix_bwd_branch(dbp, dba, attn, xp, pw, pscale, wpot, wao, tm=1024):
    S, D = dbp.shape
    nt = S // tm
    PW = xp.shape[1]
    NG = len(POOL_WINDOWS)

    def body(dbp_ref, dba_ref, at_ref, xc_ref, xprev_ref, pw_ref, ps_ref, wpo_ref, wao_ref,
             dat_ref, dpl_ref, dwao_ref, dwpo_ref, dpw_ref, dps_ref):
        i = pl.program_id(0)
        dba_v = dba_ref[...]
        dbp_v = dbp_ref[...]
        _acc(dwao_ref, _dot_tn(at_ref[...], dba_v), i == 0)
        dat_ref[...] = _dot_nt(dba_v, wao_ref[...]).astype(BF16)
        pooled = _pooled(xc_ref[...], xprev_ref[...], i).astype(BF16)
        mixed = _pool_mix(pooled, pw_ref)
        ps = ps_ref[...]
        _acc(dwpo_ref, _dot_tn(dbp_v, (mixed * ps).astype(BF16)), i == 0)
        dms = _dot(dbp_v, wpo_ref[...])
        _acc(dps_ref, jnp.sum(dms * mixed, axis=0, keepdims=True), i == 0)
        dmixed = (dms * ps).astype(BF16)
        dpooled = []
        for gi in range(NG):
            cols = slice(gi * POOL_GROUP, (gi + 1) * POOL_GROUP)
            _acc(dpw_ref.at[gi], _dot_tn(pooled[:, cols], dmixed[:, cols]), i == 0)
            dpooled.append(_dot_nt(dmixed[:, cols], pw_ref[gi]))
        dpl_ref[...] = jnp.concatenate(dpooled, axis=1)

    def row(width):
        return pl.BlockSpec((tm, width), lambda i: (i, 0))

    def whole(shape):
        nd = len(shape)
        return pl.BlockSpec(shape, lambda i: (0,) * nd)

    return pl.pallas_call(
        body, name="mix_bwd_branch", grid=(nt,),
        in_specs=[row(D), row(D), row(D), row(PW), _halo_specs(tm, PW, S, False), whole(pw.shape), whole(pscale.shape),
                  whole(wpot.shape), whole(wao.shape)],
        out_specs=[row(D), row(PW), whole((D, D)), whole((D, PW)), whole(pw.shape), whole((1, PW))],
        out_shape=[jax.ShapeDtypeStruct((S, D), BF16), jax.ShapeDtypeStruct((S, PW), F32),
                   jax.ShapeDtypeStruct((D, D), F32), jax.ShapeDtypeStruct((D, PW), F32),
                   jax.ShapeDtypeStruct(pw.shape, F32), jax.ShapeDtypeStruct((1, PW), F32)],
        compiler_params=_params(dimension_semantics=("arbitrary",)),
    )(dbp, dba, attn, xp, xp, pw, pscale, wpot, wao)


def _pool_bwd(dpooled, tm=512):
    S, PW = dpooled.shape
    nt = S // tm

    def body(dc_ref, dnext_ref, dxp_ref):
        i = pl.program_id(0)
        dc = dc_ref[...]
        dh = jnp.concatenate([dc, jnp.where(i < nt - 1, dnext_ref[...], 0.0)], axis=0)
        rows = tm + POOL_HALO
        t = lax.broadcasted_iota(jnp.int32, (rows, 1), 0) + i * tm
        out = []
        for gi, w in enumerate(POOL_WINDOWS):
            cols = slice(gi * POOL_GROUP, (gi + 1) * POOL_GROUP)
            acc = dh[:, cols] / jnp.minimum(t + 1, w).astype(F32)
            sh = 1
            while sh < w:
                acc = acc + pltpu.roll(acc, rows - sh, 0)
                sh *= 2
            out.append(acc[:tm, :] - dc[:, cols])
        dxp_ref[...] = jnp.concatenate(out, axis=1).astype(BF16)

    return pl.pallas_call(
        body, name="pool_bwd", grid=(nt,),
        in_specs=[pl.BlockSpec((tm, PW), lambda i: (i, 0)), _halo_specs(tm, PW, S, True)],
        out_specs=pl.BlockSpec((tm, PW), lambda i: (i, 0)), out_shape=jax.ShapeDtypeStruct((S, PW), BF16),
        compiler_params=_params(dimension_semantics=("arbitrary",)),
    )(dpooled, dpooled)


def _in_bwd_w(pieces, u, tm=1024):
    S, D = u.shape
    nt = S // tm
    NW = sum(width for _, width in SEGMENTS)

    def body(*refs):
        piece_refs, u_ref, dw_hbm, acc, sem = refs[:len(SEGMENTS)], refs[len(SEGMENTS)], refs[len(SEGMENTS) + 1], refs[-2], refs[-1]
        i = pl.program_id(0)
        u_t = u_ref[...]
        for (off, width), ref in zip(SEGMENTS, piece_refs):
            for lo in range(0, width, D):
                hi = min(lo + D, width)
                _acc(acc.at[off + lo:off + hi, :], _dot_tn(ref[:, lo:hi].astype(BF16), u_t), i == 0)

        @pl.when(i == nt - 1)
        def _():
            cp = pltpu.make_async_copy(acc, dw_hbm, sem)
            cp.start()
            cp.wait()

    return pl.pallas_call(
        body, name="in_bwd_w", grid=(nt,),
        in_specs=[pl.BlockSpec((tm, width), lambda i: (i, 0)) for _, width in SEGMENTS] + [pl.BlockSpec((tm, D), lambda i: (i, 0))],
        out_specs=pl.BlockSpec(memory_space=pl.ANY), out_shape=jax.ShapeDtypeStruct((NW, D), F32),
        scratch_shapes=[pltpu.VMEM((NW, D), F32), pltpu.SemaphoreType.DMA],
        compiler_params=_params(dimension_semantics=("arbitrary",)),
    )(*pieces, u)


def _in_bwd_x(pieces, wint, h, dh, gnorm, tm=512):
    S, D = h.shape
    nt = S // tm

    def body(*refs):
        piece_refs = refs[:len(SEGMENTS)]
        w_ref, h_ref, dh_ref, g_ref, dx_ref, dxh_ref, dg_ref = refs[len(SEGMENTS):]
        i = pl.program_id(0)
        du = jnp.zeros((tm, D), F32)
        for (off, width), ref in zip(SEGMENTS, piece_refs):
            du = du + _dot(ref[...].astype(BF16), w_ref[off:off + width, :])
        dx, dg_rows = _rms_bwd(du, h_ref[...], g_ref[...])
        out = dh_ref[...] + dx
        dx_ref[...] = out
        dxh_ref[...] = (0.5 * out).astype(BF16)
        _acc(dg_ref, jnp.sum(dg_rows, axis=0, keepdims=True), i == 0)

    def row(width):
        return pl.BlockSpec((tm, width), lambda i: (i, 0))

    vec = pl.BlockSpec((1, D), lambda i: (0, 0))
    return pl.pallas_call(
        body, name="in_bwd_x", grid=(nt,),
        in_specs=[row(width) for _, width in SEGMENTS] + [pl.BlockSpec(wint.shape, lambda i: (0, 0)), row(D), row(D), vec],
        out_specs=[row(D), row(D), vec],
        out_shape=[jax.ShapeDtypeStruct((S, D), F32), jax.ShapeDtypeStruct((S, D), BF16), jax.ShapeDtypeStruct((1, D), F32)],
        compiler_params=_params(dimension_semantics=("arbitrary",)),
    )(*pieces, wint, h, dh, gnorm)


def _row_tile(rows):
    for t in (512, 480, 352, 256, 128, 64, 32, 16, 8):
        if rows % t == 0:
            return t
    return rows


def _adamw_update(w, g, m, v):
    mn = ADAM_B1 * m + (1.0 - ADAM_B1) * g
    vn = ADAM_B2 * v + (1.0 - ADAM_B2) * (g * g)
    m_hat = mn / (1.0 - ADAM_B1 ** ADAM_STEP)
    v_hat = vn / (1.0 - ADAM_B2 ** ADAM_STEP)
    return -ADAM_LR * (m_hat / (jnp.sqrt(v_hat) + ADAM_EPS) + ADAM_WD * w), mn, vn


def _adamw_small(ws, gs, ms, vs):
    n = len(ws)

    def rows(a):
        return a.reshape(-1, a.shape[-1]) if a.ndim > 1 else a.reshape(1, -1)

    def body(*refs):
        for k in range(n):
            w_ref, g_ref, m_ref, v_ref, d_ref, mo_ref, vo_ref = refs[k::n]
            d_ref[...], mo_ref[...], vo_ref[...] = _adamw_update(w_ref[...], g_ref[...], m_ref[...], v_ref[...])

    vmem = pl.BlockSpec(memory_space=pltpu.VMEM)
    out = pl.pallas_call(
        body, name="adamw_small", in_specs=[vmem] * (4 * n), out_specs=[vmem] * (3 * n),
        out_shape=[jax.ShapeDtypeStruct(rows(w).shape, F32) for _ in range(3) for w in ws], compiler_params=_params(),
    )(*[rows(a) for a in ws + gs + ms + vs])
    return [[o.reshape(w.shape) for o, w in zip(out[k * n:(k + 1) * n], ws)] for k in range(3)]


def _adamw(w, g, m, v, token, name, turned=False):
    R, C = w.shape
    tr = R if turned else _row_tile(R)

    def body(w_ref, g_ref, m_ref, v_ref, _, d_ref, mo_ref, vo_ref, go_ref):
        gv = g_ref[...].T if turned else g_ref[...]
        go_ref[...] = gv
        d_ref[...], mo_ref[...], vo_ref[...] = _adamw_update(w_ref[...], gv, m_ref[...], v_ref[...])

    blk = pl.BlockSpec((tr, C), lambda i: (i, 0))
    g_blk = pl.BlockSpec((C, R), lambda i: (0, 0)) if turned else blk
    sh = jax.ShapeDtypeStruct((R, C), F32)
    return pl.pallas_call(
        body, name=name, grid=(R // tr,), in_specs=[blk, g_blk, blk, blk, TOKEN_SPEC], out_specs=[blk] * 4, out_shape=[sh] * 4,
        compiler_params=_params(dimension_semantics=("arbitrary",)),
    )(w, g, m, v, token)


def _cast_place(w, place, name, turned=False):
    R, C = w.shape[::-1] if turned else w.shape
    tr = R if turned else _row_tile(R)
    per = R // tr

    def body(p_ref, w_ref, o_ref):
        o_ref[...] = (w_ref[...].T if turned else w_ref[...]).astype(BF16)

    grid_spec = pltpu.PrefetchScalarGridSpec(
        num_scalar_prefetch=1, grid=(per,),
        in_specs=[pl.BlockSpec((C, R), lambda i, p: (0, 0)) if turned else pl.BlockSpec((tr, C), lambda i, p: (i, 0))],
        out_specs=pl.BlockSpec((tr, C), lambda i, p: (p[0] * per + i, 0)))
    return pl.pallas_call(
        body, name=name, grid_spec=grid_spec, out_shape=jax.ShapeDtypeStruct((N_SHARDS * R, C), BF16),
        compiler_params=_params(dimension_semantics=("arbitrary",)),
    )(place, w)


def _parked(w, m, idx, first, last):
    return jnp.where(m == w, idx, jnp.where(m < w, first, last))


def _sum_halves(g4s, recvs, place, name):
    M = len(g4s)
    NS, R, C = g4s[0].shape
    hr = R // 2
    tr = _row_tile(hr)
    per = hr // tr

    def body(p_ref, *refs):
        m = pl.program_id(0)
        for w in range(M):
            @pl.when(m == w)
            def _(w=w):
                refs[2 * M + w][...] = (refs[w][...] + refs[M + w][...]).astype(BF16)

    def spec(w, mine):
        def index(m, s, i, p):
            row = _parked(w, m, i, 0, per - 1)
            return _parked(w, m, s, 0, NS - 1), (p[1] * per + row if mine else row), 0

        return pl.BlockSpec((1, tr, C), index)

    grid_spec = pltpu.PrefetchScalarGridSpec(
        num_scalar_prefetch=1, grid=(M, NS, per),
        in_specs=[spec(w, True) for w in range(M)] + [spec(w, False) for w in range(M)],
        out_specs=[spec(w, False) for w in range(M)])
    return pl.pallas_call(
        body, name=name, grid_spec=grid_spec, out_shape=[jax.ShapeDtypeStruct((NS, hr, C), BF16)] * M,
        compiler_params=_params(dimension_semantics=("arbitrary",) * 3),
    )(place, *g4s, *recvs)


def _sum_quarters(h4s, recv3s, place, name):
    M = len(h4s)
    NS, hr, C = h4s[0].shape
    tr = _row_tile(hr)
    per = hr // tr

    def body(p_ref, *refs):
        m = pl.program_id(0)
        for w in range(M):
            @pl.when(m == w)
            def _(w=w):
                acc = refs[w][0].astype(F32)
                for k in range(N_SHARDS - 1):
                    acc = acc + refs[M + w][k].astype(F32)
                refs[2 * M + w][...] = acc

    def row(w, m, i):
        return _parked(w, m, i, 0, per - 1)

    grid_spec = pltpu.PrefetchScalarGridSpec(
        num_scalar_prefetch=1, grid=(M, per),
        in_specs=[pl.BlockSpec((1, tr, C), lambda m, i, p, w=w: (p[0], row(w, m, i), 0)) for w in range(M)]
        + [pl.BlockSpec((N_SHARDS - 1, tr, C), lambda m, i, p, w=w: (0, row(w, m, i), 0)) for w in range(M)],
        out_specs=[pl.BlockSpec((tr, C), lambda m, i, p, w=w: (p[1] * per + row(w, m, i), 0)) for w in range(M)])
    return pl.pallas_call(
        body, name=name, grid_spec=grid_spec, out_shape=[jax.ShapeDtypeStruct((2 * hr, C), F32)] * M,
        compiler_params=_params(dimension_semantics=("arbitrary",) * 2),
    )(place, *h4s, *recv3s)


def _by_shape(arrays):
    groups = {}
    for k, a in enumerate(arrays):
        groups.setdefault(a.shape, []).append(k)
    return list(groups.values())


def _place():
    x, y, c = lax.axis_index("x"), lax.axis_index("y"), lax.axis_index("c")
    chips = [(1 - x, y), (x, 1 - y), (1 - x, 1 - y)]
    return x, y, c, chips


HBM_SPEC = pl.BlockSpec(memory_space=pltpu.HBM)
SEM_SPEC = pl.BlockSpec(memory_space=pltpu.SEMAPHORE)
DATAFLOW = pltpu.SideEffectType.DATAFLOW_SIDE_EFFECTING


def _hbm(a):
    return pltpu.with_memory_space_constraint(a, pltpu.HBM)


class InFlight(NamedTuple):
    send_sem: jax.Array
    recv_sem: jax.Array
    bufs: list
    plan: Callable
    token: jax.Array


class Leg(NamedTuple):
    bufs: list
    landing: Optional[InFlight]
    plan: Optional[Callable]
    n_copies: int


def _send(bufs, plan, n_copies):
    return Leg([_hbm(b) for b in bufs], None, plan, n_copies)


def _land(flight):
    return Leg(flight.bufs, flight, None, 0)


def _forward(flight, plan, n_copies):
    return Leg(flight.bufs, flight, plan, n_copies)


def _wait_all(plan, refs, send_ref, recv_ref):
    for k, (src, dst, dev) in enumerate(plan(refs)):
        cp = pltpu.make_async_remote_copy(src_ref=src, dst_ref=dst, send_sem=send_ref.at[k], recv_sem=recv_ref.at[k],
                                          device_id=dev, device_id_type=MESH)
        cp.wait_send()
        cp.wait_recv()


def _start_all(plan, refs, send_ref, recv_ref):
    for k, (src, dst, dev) in enumerate(plan(refs)):
        pltpu.make_async_remote_copy(src_ref=src, dst_ref=dst, send_sem=send_ref.at[k], recv_sem=recv_ref.at[k],
                                     device_id=dev, device_id_type=MESH).start()


def _comm(name, legs, after):
    after = list(after) if isinstance(after, (list, tuple)) else [after]
    ins, in_specs, out_shape, out_specs, aliases, first = [], [], [], [], {}, []
    for leg in legs:
        first.append((len(ins), len(out_shape)))
        for b in leg.bufs:
            aliases[len(ins)] = len(out_shape)
            ins.append(b)
            in_specs.append(HBM_SPEC)
            out_shape.append(pltpu.HBM(b.shape, b.dtype))
            out_specs.append(HBM_SPEC)
        if leg.landing is not None:
            ins += [leg.landing.send_sem, leg.landing.recv_sem]
            in_specs += [SEM_SPEC, SEM_SPEC]
        if leg.plan is not None:
            out_shape += [pltpu.SemaphoreType.DMA((leg.n_copies,))] * 2
            out_specs += [SEM_SPEC, SEM_SPEC]
    starts = any(leg.plan is not None for leg in legs)
    if starts:
        out_shape.append(jax.ShapeDtypeStruct((8, 128), F32))
        out_specs.append(pl.BlockSpec(memory_space=pltpu.VMEM))
    n_in = len(ins) + len(after)

    def body(*refs):
        outs = refs[n_in:]
        for leg, (i, o) in zip(legs, first):
            nb = len(leg.bufs)
            bufs = refs[i:i + nb]
            if leg.landing is not None:
                _wait_all(leg.landing.plan, bufs, refs[i + nb], refs[i + nb + 1])
            if leg.plan is not None:
                _start_all(leg.plan, bufs, outs[o + nb], outs[o + nb + 1])
        if starts:
            outs[-1][...] = jnp.zeros_like(outs[-1])

    out = pl.pallas_call(
        body, name=name, in_specs=in_specs + [TOKEN_SPEC] * len(after), out_shape=out_shape, out_specs=out_specs,
        input_output_aliases=aliases, compiler_params=pltpu.CompilerParams(has_side_effects=DATAFLOW),
    )(*ins, *after)
    results = []
    for leg, (_, o) in zip(legs, first):
        nb = len(leg.bufs)
        bufs = list(out[o:o + nb])
        results.append(bufs if leg.plan is None else InFlight(out[o + nb], out[o + nb + 1], bufs, leg.plan, out[-1]))
    return results


def _half_rows(buf, chip, core):
    hr = buf.shape[0] // (2 * N_SHARDS)
    return buf.at[pl.ds(pl.multiple_of((2 * chip + core) * hr, 16), hr)]


def _gather_ici_plan(bufs):
    x, y, c, chips = _place()
    return [(_half_rows(b, 2 * x + y, c), _half_rows(b, 2 * x + y, c), (px, py, c)) for b in bufs for px, py in chips]


def _gather_d2d_plan(bufs):
    x, y, c, chips = _place()
    return [(_half_rows(b, 2 * px + py, c), _half_rows(b, 2 * px + py, c), (x, y, 1 - c)) for b in bufs for px, py in chips]


def _swap_plan(bufs):
    x, y, c, _ = _place()
    n = len(bufs) // 2
    copies = []
    for g, land in zip(bufs[:n], bufs[n:]):
        hr = g.shape[1] // 2
        copies.append((g.at[:, pl.ds(pl.multiple_of((1 - c) * hr, 8), hr)], land, (x, y, 1 - c)))
    return copies


def _exchange_plan(bufs):
    x, y, c, chips = _place()
    n = len(bufs) // 2
    return [(h.at[2 * px + py], land.at[k], (px, py, c))
            for h, land in zip(bufs[:n], bufs[n:]) for k, (px, py) in enumerate(chips)]


def _share_plan(bufs):
    x, y, c, _ = _place()
    copies = []
    for buf in bufs:
        hr = buf.shape[0] // 2
        mine = buf.at[pl.ds(pl.multiple_of(c * hr, 8), hr)]
        copies.append((mine, mine, (x, y, 1 - c)))
    return copies


N_DEVICES = 8


def _slot_place(vec, place):
    R, C = vec.shape

    def body(p_ref, v_ref, o_ref):
        o_ref[0] = v_ref[...]

    grid_spec = pltpu.PrefetchScalarGridSpec(
        num_scalar_prefetch=1, grid=(1,), in_specs=[pl.BlockSpec((R, C), lambda i, p: (0, 0))],
        out_specs=pl.BlockSpec((1, R, C), lambda i, p: (2 * p[0] + p[1], 0, 0)))
    return pl.pallas_call(
        body, name="slot_place", grid_spec=grid_spec, out_shape=jax.ShapeDtypeStruct((N_DEVICES, R, C), F32),
        compiler_params=_params(dimension_semantics=("arbitrary",)),
    )(place, vec)


def _slots_plan(bufs):
    x, y, c, _ = _place()
    mine = bufs[0].at[4 * x + 2 * y + c]
    return [(mine, mine, (x ^ (r >> 2), y ^ ((r >> 1) & 1), c ^ (r & 1))) for r in range(1, N_DEVICES)]


def _sum_slots(slots):
    _, R, C = slots.shape

    def body(s_ref, o_ref):
        acc = s_ref[0]
        for d in range(1, N_DEVICES):
            acc = acc + s_ref[d]
        o_ref[...] = acc

    return pl.pallas_call(
        body, name="sum_slots", in_specs=[pl.BlockSpec(memory_space=pltpu.VMEM)],
        out_specs=pl.BlockSpec(memory_space=pltpu.VMEM), out_shape=jax.ShapeDtypeStruct((R, C), F32),
        compiler_params=_params(),
    )(slots)


SMALL = ("ffn1_norm", "mix_norm", "pool_w", "pool_scale", "q_norm", "k_norm", "sinks", "gate_bias", "ffn2_norm")
SMALL_COLS = 1024
FFN1 = ("ffn1_w_gate", "ffn1_w_up", "ffn1_w_down")
MIXER = ("w_in", "w_pool_out", "w_attn_out", "w_out")
FFN2 = ("ffn2_w_gate", "ffn2_w_up", "ffn2_w_down")
LARGE = FFN1 + MIXER + FFN2
TRANSPOSED = ("ffn1_w_gate", "ffn1_w_up", "w_in", "w_pool_out", "ffn2_w_gate", "ffn2_w_up")
TURNED = ("w_pool_out",)
WEIGHTS = ("ffn1_norm", "ffn1_w_gate", "ffn1_w_up", "ffn1_w_down", "mix_norm", "w_in", "pool_w", "pool_scale",
           "w_pool_out", "q_norm", "k_norm", "sinks", "w_attn_out", "gate_bias", "w_out", "ffn2_norm",
           "ffn2_w_gate", "ffn2_w_up", "ffn2_w_down")


def _pack_small(parts):
    flat = jnp.concatenate([p.reshape(-1) for p in parts])
    rows = -(-flat.shape[0] // (8 * SMALL_COLS)) * 8
    return jnp.pad(flat, (0, rows * SMALL_COLS - flat.shape[0])).reshape(rows, SMALL_COLS)


def _unpack_small(packed, like):
    flat = packed.reshape(-1)
    out, off = [], 0
    for p in like:
        out.append(flat[off:off + p.size].reshape(p.shape))
        off += p.size
    return out, flat[off]


def _tie(small, token):
    return small + token[0, 0]


class Reduction:
    def __init__(self, names, place):
        self.names, self.place = names, place

    def swap(self, grads):
        grads = [g.reshape(N_SHARDS, -1, g.shape[-1]) for g in grads]
        lands = [lax.empty((g.shape[0], g.shape[1] // 2, g.shape[2]), g.dtype) for g in grads]
        return _send(grads + lands, _swap_plan, len(grads))

    def exchange(self, swapped):
        n = len(self.names)
        halves = self._per_shape(_sum_halves, "sum_halves", swapped[:n], swapped[n:])
        lands = [lax.empty((N_SHARDS - 1,) + h.shape[1:], h.dtype) for h in halves]
        return _send(halves + lands, _exchange_plan, (N_SHARDS - 1) * n)

    def share(self, exchanged):
        n = len(self.names)
        return _send(self._per_shape(_sum_quarters, "sum_quarters", exchanged[:n], exchanged[n:]), _share_plan, n)

    def named(self, shared):
        return dict(zip(self.names, shared))

    def _per_shape(self, add, stage, mine, received):
        out = [None] * len(mine)
        for idx in _by_shape(mine):
            sums = add([mine[k] for k in idx], [received[k] for k in idx], self.place, f"{stage}_{self.names[idx[0]]}")
            for k, v in zip(idx, sums):
                out[k] = v
        return out


def _row_form(name, a):
    return a.T if name in TRANSPOSED else a


def _shard_form(name, a):
    return a if name in TURNED else _row_form(name, a)


def kernel(x, ffn1_norm, ffn1_w_gate, ffn1_w_up, ffn1_w_down, mix_norm, w_in, pool_w, pool_scale, w_pool_out, q_norm, k_norm, sinks, w_attn_out, gate_bias, w_out, ffn2_norm, ffn2_w_gate, ffn2_w_up, ffn2_w_down, loss_target, m_ffn1_norm, m_ffn1_w_gate, m_ffn1_w_up, m_ffn1_w_down, m_mix_norm, m_w_in, m_pool_w, m_pool_scale, m_w_pool_out, m_q_norm, m_k_norm, m_sinks, m_w_attn_out, m_gate_bias, m_w_out, m_ffn2_norm, m_ffn2_w_gate, m_ffn2_w_up, m_ffn2_w_down, v_ffn1_norm, v_ffn1_w_gate, v_ffn1_w_up, v_ffn1_w_down, v_mix_norm, v_w_in, v_pool_w, v_pool_scale, v_w_pool_out, v_q_norm, v_k_norm, v_sinks, v_w_attn_out, v_gate_bias, v_w_out, v_ffn2_norm, v_ffn2_w_gate, v_ffn2_w_up, v_ffn2_w_down):
    args = dict(locals())
    wts = {n: _shard_form(n, args[n]) for n in WEIGHTS}
    mom = {n: _shard_form(n, args["m_" + n]) for n in WEIGHTS}
    var = {n: _shard_form(n, args["v_" + n]) for n in WEIGHTS}
    shard = 2 * lax.axis_index("x") + lax.axis_index("y")
    place = jnp.stack([shard, lax.axis_index("c")]).astype(jnp.int32)

    xs, target = x[0], loss_target[0]
    D = xs.shape[1]
    g1 = wts["ffn1_norm"].reshape(1, D)
    gm = wts["mix_norm"].reshape(1, D)
    g2 = wts["ffn2_norm"].reshape(1, D)
    qw = wts["q_norm"].reshape(1, HEAD_DIM)
    kw = wts["k_norm"].reshape(1, HEAD_DIM)
    bias = wts["gate_bias"].reshape(1, 2 * D)
    pscale = wts["pool_scale"].reshape(1, -1)
    pw = wts["pool_w"].astype(BF16)
    sink_rows = jnp.broadcast_to(jnp.repeat(wts["sinks"], ATTN_BLOCK).reshape(N_KV_HEADS, GQA_GROUP * ATTN_BLOCK, 1),
                                 (N_KV_HEADS, GQA_GROUP * ATTN_BLOCK, 128))

    gate1, up1, down1 = FFN1[:1], FFN1[1:2], FFN1[2:]
    mix_in, mix_out = MIXER[:1], MIXER[1:]

    def over_ici(names):
        return _send([placed[n] for n in names], _gather_ici_plan, 3 * len(names))

    def to_sibling(flight):
        return _forward(flight, _gather_d2d_plan, len(flight.bufs) * 3)

    placed = {n: _cast_place(wts[n], place, "cast_" + n) for n in gate1}
    gate1_ici, = _comm("gather_gate1_ici", [over_ici(gate1)], place)
    placed.update({n: _cast_place(wts[n], place, "cast_" + n) for n in up1 + down1})
    up1_ici, down1_ici = _comm("gather_ffn1_ici", [over_ici(up1), over_ici(down1)], gate1_ici.token)
    placed.update({n: _cast_place(wts[n], place, "cast_" + n, turned=n in TURNED) for n in MIXER + FFN2})
    band = _band_bias()
    shadow = [sink_rows, pw, band, down1_ici.token]
    gate1_d2d, in_ici, out_ici, ffn2_ici = _comm(
        "gather_rest_ici", [to_sibling(gate1_ici), over_ici(mix_in), over_ici(mix_out), over_ici(FFN2)], shadow)
    w = dict(zip(gate1, _comm("gather_gate1_wait", [_land(gate1_d2d)], gate1_d2d.token)[0]))

    n1, a1 = _ffn_gate(xs, g1, w["ffn1_w_gate"], "ffn1_gate")
    up1_d2d, down1_d2d = _comm("gather_ffn1_d2d", [to_sibling(up1_ici), to_sibling(down1_ici)], a1)
    w.update(zip(up1, _comm("gather_up1_wait", [_land(up1_d2d)], up1_d2d.token)[0]))
    p1, q1, s1 = _ffn_hidden(n1, a1, w["ffn1_w_up"], "ffn1_hidden")
    w.update(zip(down1, _comm("gather_down1_wait", [_land(down1_d2d)], s1)[0]))
    h1 = _ffn_down(xs, s1, w["ffn1_w_down"], "ffn1_down")
    in_d2d, out_d2d = _comm("gather_mix_d2d", [to_sibling(in_ici), to_sibling(out_ici)], h1)
    w.update(zip(mix_in, _comm("gather_in_wait", [_land(in_d2d)], in_d2d.token)[0]))
    u, xp, q, kv, gl = _mix_proj(h1, gm, w["w_in"])
    attn = _attn_fwd(q, kv, qw, kw, sink_rows, band)
    ffn2_d2d, landed = _comm("gather_ffn2_d2d", [to_sibling(ffn2_ici), _land(out_d2d)], attn)
    w.update(zip(mix_out, landed))
    h2, gates, fac, merged = _mix_out(xp, attn, gl, bias, pw, pscale, w["w_pool_out"], w["w_attn_out"], w["w_out"], h1)
    w.update(zip(FFN2, _comm("gather_ffn2_wait", [_land(ffn2_d2d)], h2)[0]))
    n2, p2, q2, s2 = _ffn_up(h2, g2, w["ffn2_w_gate"], w["ffn2_w_up"], "ffn2_up")
    dy, dyh, loss = _ffn_down_loss(h2, s2, w["ffn2_w_down"], target, "ffn2_down_loss")

    gw, gs = {}, {}
    da, db = _ffn_bwd_h(dyh, p2, q2, w["ffn2_w_down"], ffn2_d2d.token, "ffn2_bwd_h")
    gw["ffn2_w_gate"], gw["ffn2_w_up"] = _xty(da, n2, "ffn2_dw_gate"), _xty(db, n2, "ffn2_dw_up")
    gw["ffn2_w_down"] = _xty(s2, dyh, "ffn2_dw_down")
    red2 = Reduction(FFN2, place)
    swap, = _comm("reduce_ffn2_swap", [red2.swap([gw[n] for n in FFN2])], da)
    dh2, gs["ffn2_norm"] = _ffn_bwd_x(da, db, w["ffn2_w_gate"], w["ffn2_w_up"], h2, dy, _tie(g2, swap.token), "ffn2_bwd_x")
    swapped, = _comm("reduce_ffn2_swap_wait", [_land(swap)], dh2)
    exchange, = _comm("reduce_ffn2_exchange", [red2.exchange(swapped)], place)
    dgl, dbp, dba, gw["w_out"], gs["gate_bias"] = _mix_bwd_gate(dh2, w["w_out"], gates, fac, merged, exchange.token)
    dattn, dpooled, gw["w_attn_out"], gw["w_pool_out"], gs["pool_w"], gs["pool_scale"] = _mix_bwd_branch(
        dbp, dba, attn, xp, pw, pscale, w["w_pool_out"], w["w_attn_out"])
    dq, dkv, gs["q_norm"], gs["k_norm"], dsk = _attn_bwd(q, kv, dattn, qw, kw, sink_rows, band)
    gs["sinks"] = dsk[:, :, 0]
    exchanged, = _comm("reduce_ffn2_exchange_wait", [_land(exchange)], dq)
    share2, = _comm("reduce_ffn2_share", [red2.share(exchanged)], place)
    pieces = (_pool_bwd(dpooled), dq, dkv, dgl)
    gw["w_in"] = _in_bwd_w(pieces, u)
    redm = Reduction(MIXER, place)
    swap, shared = _comm("reduce_mix_swap", [redm.swap([gw[n] for n in MIXER]), _land(share2)], share2.token)
    grads = red2.named(shared)
    dh1, dh1h, gs["mix_norm"] = _in_bwd_x(pieces, w["w_in"], h1, dh2, _tie(gm, swap.token))
    red1, swaps = Reduction(FFN1, place), {}
    dw_down = _xty(s1, dh1h, "ffn1_dw_down")
    swaps["ffn1_w_down"], swapped = _comm("reduce_mix_swap_wait", [red1.swap([dw_down]), _land(swap)], dh1)
    exchange, = _comm("reduce_mix_exchange", [redm.exchange(swapped)], place)
    da, db = _ffn_bwd_h(dh1h, p1, q1, w["ffn1_w_down"], exchange.token, "ffn1_bwd_h")
    dw_up = _xty(db, n1, "ffn1_dw_up")
    swaps["ffn1_w_up"], = _comm("reduce_ffn1_swap_up", [red1.swap([dw_up])], place)
    dw_gate = _xty(da, n1, "ffn1_dw_gate", after=[swaps["ffn1_w_up"].token])
    swaps["ffn1_w_gate"], exchanged = _comm("reduce_mix_exchange_wait", [red1.swap([dw_gate]), _land(exchange)], place)
    sharem, *landed = _comm("reduce_mix_share", [redm.share(exchanged)] + [_land(swaps[n]) for n in FFN1], place)
    exchange, = _comm("reduce_ffn1_exchange", [red1.exchange([l[0] for l in landed] + [l[1] for l in landed])], place)
    grad_x, gs["ffn1_norm"] = _ffn_bwd_x(da, db, w["ffn1_w_gate"], w["ffn1_w_up"], xs, dh1, _tie(g1, exchange.token), "ffn1_bwd_x")

    small_parts = [gs[n] for n in SMALL] + [loss[0, 0].reshape(1)]
    slots, shared = _comm(
        "gather_small", [_send([_slot_place(_pack_small(small_parts), place)], _slots_plan, N_DEVICES - 1), _land(sharem)],
        grad_x)
    grads.update(redm.named(shared))
    delta, new_m, new_v = {}, {}, {}

    def adamw(names, token):
        for n in names:
            delta[n], new_m[n], new_v[n], grads[n] = _adamw(wts[n], grads[n], mom[n], var[n], token, "adamw_" + n, n in TURNED)

    early, late = FFN2 + MIXER[:-1], MIXER[-1:]
    adamw(early, slots.token)
    exchanged, = _comm("reduce_ffn1_exchange_wait", [_land(exchange)], [delta[n] for n in early])
    share1, landed = _comm("reduce_ffn1_share", [red1.share(exchanged), _land(slots)], place)
    summed = _sum_slots(landed[0])
    small_grads, loss_sum = _unpack_small(summed, [wts[n] for n in SMALL])
    grads.update(dict(zip(SMALL, small_grads)))
    for out, vals in zip((delta, new_m, new_v), _adamw_small(*[[d[n] for n in SMALL] for d in (wts, grads, mom, var)])):
        out.update(dict(zip(SMALL, vals)))
    adamw(late, share1.token)
    results = {(k, n): _shard_form(n, d[n]) for k, d in enumerate((grads, delta, new_m, new_v)) for n in SMALL + FFN2 + MIXER}
    shared, = _comm("reduce_ffn1_share_wait", [_land(share1)], [results[k, n] for k in range(4) for n in SMALL + late])
    grads.update(red1.named(shared))
    adamw(FFN1, share1.token)
    results.update({(k, n): _shard_form(n, d[n]) for k, d in enumerate((grads, delta, new_m, new_v)) for n in FFN1})
    return (loss_sum, grad_x[None], *[results[k, n] for k in range(4) for n in WEIGHTS])
```

```python
from typing import Callable, NamedTuple, Optional

import jax
import jax.numpy as jnp
from jax import lax
from jax.experimental import pallas as pl
from jax.experimental.pallas import tpu as pltpu

F32 = jnp.float32
BF16 = jnp.bfloat16
RMS_EPS = 1e-6
POOL_WINDOWS = (2, 4, 8, 16)
POOL_GROUP = 128
POOL_HALO = 16
HEAD_DIM = 64
GQA_GROUP = 8
N_KV_HEADS = 2
ATTN_BLOCK = 128
SCALE = HEAD_DIM ** -0.5
NEG = -1e30
N_SHARDS = 4
ADAM_LR, ADAM_B1, ADAM_B2, ADAM_EPS, ADAM_WD, ADAM_STEP = 0.001, 0.9, 0.999, 1e-08, 0.01, 10
VMEM_LIMIT = 56 * 1024 * 1024
MESH = pl.DeviceIdType.MESH
SEG_POOL, SEG_Q, SEG_KV, SEG_GATE = (0, 512), (512, 1024), (1536, 256), (1792, 2048)
SEGMENTS = (SEG_POOL, SEG_Q, SEG_KV, SEG_GATE)


def _params(**kw):
    return pltpu.CompilerParams(vmem_limit_bytes=VMEM_LIMIT, **kw)


def _dot(a, b):
    return jnp.dot(a, b, preferred_element_type=F32)


def _dot_nt(a, b):
    return lax.dot_general(a, b, (((1,), (1,)), ((), ())), preferred_element_type=F32)


def _dot_tn(a, b):
    return lax.dot_general(a, b, (((0,), (0,)), ((), ())), preferred_element_type=F32)


def _rinv(x):
    return lax.rsqrt(jnp.mean(x * x, axis=-1, keepdims=True) + RMS_EPS)


def _rms_bwd(dn, x, g):
    r = _rinv(x)
    xr = x * r
    z = dn * g
    dx = r * (z - xr * jnp.mean(z * xr, axis=-1, keepdims=True))
    return dx, dn * xr


def _acc(ref, val, first):
    @pl.when(first)
    def _():
        ref[...] = val

    @pl.when(jnp.logical_not(first))
    def _():
        ref[...] += val


TOKEN_SPEC = pl.BlockSpec(memory_space=pl.ANY)
F_HALF = 1408


def _resident(w):
    return pl.BlockSpec(w.shape, lambda i: (0, 0), pipeline_mode=pl.Buffered(1))


def _swiglu(a, b):
    sig = jax.nn.sigmoid(a)
    q = a * sig
    return b * (sig + q * (1.0 - sig)), q, q * b


def _ffn_up(h, gnorm, wgt, wut, name, tm=512):
    S, D = h.shape
    F = wgt.shape[0]

    def body(h_ref, g_ref, wg_ref, wu_ref, n_ref, p_ref, q_ref, s_ref):
        x = h_ref[...]
        n = (x * _rinv(x) * g_ref[...]).astype(BF16)
        n_ref[...] = n
        for lo in range(0, F, F_HALF):
            cols = slice(lo, lo + F_HALF)
            p, q, s = _swiglu(_dot_nt(n, wg_ref[cols, :]), _dot_nt(n, wu_ref[cols, :]))
            p_ref[:, cols] = p.astype(BF16)
            q_ref[:, cols] = q.astype(BF16)
            s_ref[:, cols] = s.astype(BF16)

    act = pl.BlockSpec((tm, F), lambda i: (i, 0))
    hidden = jax.ShapeDtypeStruct((S, F), BF16)
    return pl.pallas_call(
        body, name=name, grid=(S // tm,),
        in_specs=[pl.BlockSpec((tm, D), lambda i: (i, 0)), pl.BlockSpec((1, D), lambda i: (0, 0)), _resident(wgt), _resident(wut)],
        out_specs=[pl.BlockSpec((tm, D), lambda i: (i, 0)), act, act, act],
        out_shape=[jax.ShapeDtypeStruct((S, D), BF16), hidden, hidden, hidden],
        compiler_params=_params(dimension_semantics=("arbitrary",)),
    )(h, gnorm, wgt, wut)


def _ffn_gate(h, gnorm, wgt, name, tm=512):
    S, D = h.shape
    F = wgt.shape[0]

    def body(h_ref, g_ref, wg_ref, n_ref, a_ref):
        x = h_ref[...]
        n = (x * _rinv(x) * g_ref[...]).astype(BF16)
        n_ref[...] = n
        for lo in range(0, F, F_HALF):
            a_ref[:, lo:lo + F_HALF] = _dot_nt(n, wg_ref[lo:lo + F_HALF, :]).astype(BF16)

    row = pl.BlockSpec((tm, D), lambda i: (i, 0))
    return pl.pallas_call(
        body, name=name, grid=(S // tm,), in_specs=[row, pl.BlockSpec((1, D), lambda i: (0, 0)), _resident(wgt)],
        out_specs=[row, pl.BlockSpec((tm, F), lambda i: (i, 0))],
        out_shape=[jax.ShapeDtypeStruct((S, D), BF16), jax.ShapeDtypeStruct((S, F), BF16)],
        compiler_params=_params(dimension_semantics=("arbitrary",)),
    )(h, gnorm, wgt)


def _ffn_hidden(n, a, wut, name, tm=512):
    S, D = n.shape
    F = wut.shape[0]

    def body(n_ref, a_ref, wu_ref, p_ref, q_ref, s_ref):
        n_t = n_ref[...]
        for lo in range(0, F, F_HALF):
            cols = slice(lo, lo + F_HALF)
            p, q, s = _swiglu(a_ref[:, cols].astype(F32), _dot_nt(n_t, wu_ref[cols, :]))
            p_ref[:, cols] = p.astype(BF16)
            q_ref[:, cols] = q.astype(BF16)
            s_ref[:, cols] = s.astype(BF16)

    act = pl.BlockSpec((tm, F), lambda i: (i, 0))
    hidden = jax.ShapeDtypeStruct((S, F), BF16)
    return pl.pallas_call(
        body, name=name, grid=(S // tm,), in_specs=[pl.BlockSpec((tm, D), lambda i: (i, 0)), act, _resident(wut)],
        out_specs=[act, act, act], out_shape=[hidden, hidden, hidden],
        compiler_params=_params(dimension_semantics=("arbitrary",)),
    )(n, a, wut)


def _ffn_down(h, s, wd, name, tm=1024):
    S, D = h.shape
    F = wd.shape[0]

    def body(h_ref, s_ref, wd_ref, o_ref):
        o_ref[...] = h_ref[...] + 0.5 * _dot(s_ref[...], wd_ref[...])

    row = pl.BlockSpec((tm, D), lambda i: (i, 0))
    return pl.pallas_call(
        body, name=name, grid=(S // tm,), in_specs=[row, pl.BlockSpec((tm, F), lambda i: (i, 0)), _resident(wd)],
        out_specs=row, out_shape=jax.ShapeDtypeStruct((S, D), F32),
        compiler_params=_params(dimension_semantics=("arbitrary",)),
    )(h, s, wd)


def _ffn_down_loss(h, s, wd, target, name, tm=512):
    S, D = h.shape
    F = wd.shape[0]

    def body(h_ref, s_ref, wd_ref, t_ref, dy_ref, dyh_ref, loss_ref):
        e = h_ref[...] + 0.5 * _dot(s_ref[...], wd_ref[...]) - t_ref[...]
        dy = e * (1.0 / D)
        dy_ref[...] = dy
        dyh_ref[...] = (0.5 * dy).astype(BF16)
        tot = jnp.sum(jnp.sum(e * e, axis=1, keepdims=True), axis=0, keepdims=True) * (0.5 / D)
        _acc(loss_ref, jnp.broadcast_to(tot, loss_ref.shape), pl.program_id(0) == 0)

    row = pl.BlockSpec((tm, D), lambda i: (i, 0))
    return pl.pallas_call(
        body, name=name, grid=(S // tm,), in_specs=[row, pl.BlockSpec((tm, F), lambda i: (i, 0)), _resident(wd), row],
        out_specs=[row, row, pl.BlockSpec((8, 128), lambda i: (0, 0))],
        out_shape=[jax.ShapeDtypeStruct((S, D), F32), jax.ShapeDtypeStruct((S, D), BF16), jax.ShapeDtypeStruct((8, 128), F32)],
        compiler_params=_params(dimension_semantics=("arbitrary",)),
    )(h, s, wd, target)


def _ffn_bwd_h(dyh, p, q, wd, token, name, tm=512):
    S, D = dyh.shape
    F = wd.shape[0]

    def body(dyh_ref, p_ref, q_ref, wd_ref, _, da_ref, db_ref):
        dyh_t = dyh_ref[...]
        for lo in range(0, F, F_HALF):
            cols = slice(lo, lo + F_HALF)
            ds = _dot_nt(dyh_t, wd_ref[cols, :])
            da_ref[:, cols] = (ds * p_ref[:, cols].astype(F32)).astype(BF16)
            db_ref[:, cols] = (ds * q_ref[:, cols].astype(F32)).astype(BF16)

    act = pl.BlockSpec((tm, F), lambda i: (i, 0))
    hidden = jax.ShapeDtypeStruct((S, F), BF16)
    return pl.pallas_call(
        body, name=name, grid=(S // tm,),
        in_specs=[pl.BlockSpec((tm, D), lambda i: (i, 0)), act, act, _resident(wd), TOKEN_SPEC],
        out_specs=[act, act], out_shape=[hidden, hidden],
        compiler_params=_params(dimension_semantics=("arbitrary",)),
    )(dyh, p, q, wd, token)


def _xty(x, y, name, tk=2048, tf=1408, after=()):
    S, F = x.shape
    D = y.shape[1]

    def body(x_ref, y_ref, *rest):
        _acc(rest[-1], _dot_tn(x_ref[...], y_ref[...]), pl.program_id(1) == 0)

    return pl.pallas_call(
        body, name=name, grid=(F // tf, S // tk),
        in_specs=[pl.BlockSpec((tk, tf), lambda j, k: (k, j)), pl.BlockSpec((tk, D), lambda j, k: (k, 0))]
        + [TOKEN_SPEC] * len(after),
        out_specs=pl.BlockSpec((tf, D), lambda j, k: (j, 0)), out_shape=jax.ShapeDtypeStruct((F, D), F32),
        compiler_params=_params(dimension_semantics=("arbitrary", "arbitrary")),
    )(x, y, *after)


def _ffn_bwd_x(da, db, wgt, wut, x, dh, gnorm, name, tm=512):
    S, D = x.shape
    F = wgt.shape[0]

    def body(da_ref, db_ref, wg_ref, wu_ref, x_ref, dh_ref, g_ref, dx_ref, dg_ref):
        dn = _dot(da_ref[...], wg_ref[...]) + _dot(db_ref[...], wu_ref[...])
        dx, dg_rows = _rms_bwd(dn, x_ref[...], g_ref[...])
        dx_ref[...] = dh_ref[...] + dx
        _acc(dg_ref, jnp.sum(dg_rows, axis=0, keepdims=True), pl.program_id(0) == 0)

    row = pl.BlockSpec((tm, D), lambda i: (i, 0))
    act = pl.BlockSpec((tm, F), lambda i: (i, 0))
    vec = pl.BlockSpec((1, D), lambda i: (0, 0))
    return pl.pallas_call(
        body, name=name, grid=(S // tm,), in_specs=[act, act, _resident(wgt), _resident(wut), row, row, vec],
        out_specs=[row, vec],
        out_shape=[jax.ShapeDtypeStruct((S, D), F32), jax.ShapeDtypeStruct((1, D), F32)],
        compiler_params=_params(dimension_semantics=("arbitrary",)),
    )(da, db, wgt, wut, x, dh, gnorm)


def _mix_proj(h, gnorm, wint, tm=512):
    S, D = h.shape
    nt = S // tm

    def body(h_ref, g_ref, w_ref, u_ref, xp_ref, q_ref, kv_ref, gl_ref):
        x = h_ref[...]
        u = (x * _rinv(x) * g_ref[...]).astype(BF16)
        u_ref[...] = u
        for (off, width), ref in zip(SEGMENTS, (xp_ref, q_ref, kv_ref, gl_ref)):
            ref[...] = _dot_nt(u, w_ref[off:off + width, :]).astype(ref.dtype)

    def row(width):
        return pl.BlockSpec((tm, width), lambda i: (i, 0))

    return pl.pallas_call(
        body, name="mix_proj", grid=(nt,),
        in_specs=[row(D), pl.BlockSpec((1, D), lambda i: (0, 0)), pl.BlockSpec(wint.shape, lambda i: (0, 0))],
        out_specs=[row(D), row(SEG_POOL[1]), row(SEG_Q[1]), row(SEG_KV[1]), row(SEG_GATE[1])],
        out_shape=[jax.ShapeDtypeStruct((S, D), BF16), jax.ShapeDtypeStruct((S, SEG_POOL[1]), F32),
                   jax.ShapeDtypeStruct((S, SEG_Q[1]), BF16), jax.ShapeDtypeStruct((S, SEG_KV[1]), BF16),
                   jax.ShapeDtypeStruct((S, SEG_GATE[1]), BF16)],
        compiler_params=_params(dimension_semantics=("arbitrary",)),
    )(h, gnorm, wint)


def _stack_heads(x, g):
    return jnp.concatenate([x[:, (GQA_GROUP * g + hh) * HEAD_DIM:(GQA_GROUP * g + hh + 1) * HEAD_DIM]
                            for hh in range(GQA_GROUP)], axis=0)


def _unstack_heads(ref, val, g):
    for hh in range(GQA_GROUP):
        lo = (GQA_GROUP * g + hh) * HEAD_DIM
        ref[:, lo:lo + HEAD_DIM] = val[hh * ATTN_BLOCK:(hh + 1) * ATTN_BLOCK, :]


def _rowsum(xb, width):
    return _dot(xb, jnp.ones((xb.shape[1], width), BF16))


def _rinv_lanes(x):
    return lax.rsqrt(_rowsum((x * x).astype(BF16), x.shape[1]) * (1.0 / x.shape[1]) + RMS_EPS)


def _twice(x):
    return jnp.concatenate([x, x], axis=1)


def _band_bias():
    qi = (jnp.arange(GQA_GROUP * ATTN_BLOCK) % ATTN_BLOCK)[:, None]
    kj = jnp.arange(2 * ATTN_BLOCK)[None, :]
    return jnp.where(jnp.logical_and(kj > qi, kj <= qi + ATTN_BLOCK), 0.0, NEG).astype(F32)


def _attn_exp(qn, kk, sink, band, n):
    kj = lax.broadcasted_iota(jnp.int32, (1, 2 * ATTN_BLOCK), 1)
    before_sequence = jnp.where(jnp.logical_and(kj < ATTN_BLOCK, n == 0), NEG, 0.0)
    s = _dot_nt(qn, kk) + band + before_sequence
    m = jnp.maximum(jnp.broadcast_to(jnp.max(s, axis=-1, keepdims=True), sink.shape), sink)
    e = jnp.exp(s - _twice(m))
    e_s = jnp.exp(sink - m)
    e_b = e.astype(BF16)
    inv = 1.0 / (_rowsum(e_b, ATTN_BLOCK) + e_s)
    return e_b, e, e_s, inv


def _attn_blocks(n):
    cur = pl.multiple_of(n * ATTN_BLOCK, ATTN_BLOCK)
    prev = pl.multiple_of(jnp.maximum(n - 1, 0) * ATTN_BLOCK, ATTN_BLOCK)
    return cur, prev


def _kv_window(kv_ref, n):
    cur, prev = _attn_blocks(n)
    return jnp.concatenate([kv_ref[pl.ds(prev, ATTN_BLOCK), :], kv_ref[pl.ds(cur, ATTN_BLOCK), :]], axis=0).astype(F32)


def _kv_split(kv, g):
    k = kv[:, g * HEAD_DIM:(g + 1) * HEAD_DIM]
    v = kv[:, (N_KV_HEADS + g) * HEAD_DIM:(N_KV_HEADS + g + 1) * HEAD_DIM]
    return k, v


def _attn_fwd(q, kv, qw, kw, sink_rows, band):
    S, W = q.shape
    nb = S // ATTN_BLOCK

    def body(q_ref, kv_ref, qw_ref, kw_ref, sk_ref, band_ref, o_ref, o_scr):
        n = pl.program_id(0)
        qf = q_ref[...].astype(F32)
        kvw = _kv_window(kv_ref, n)
        for g in range(N_KV_HEADS):
            qs = _stack_heads(qf, g)
            qn = (qs * _rinv_lanes(qs) * qw_ref[...] * SCALE).astype(BF16)
            k, v = _kv_split(kvw, g)
            kk = (k * _rinv_lanes(k) * kw_ref[...]).astype(BF16)
            e_b, _, _, inv = _attn_exp(qn, kk, sk_ref[g], band_ref[...], n)
            _unstack_heads(o_scr, _dot(e_b, v.astype(BF16)) * inv[:, :HEAD_DIM], g)
        o_ref[...] = o_scr[...].astype(BF16)

    blk = pl.BlockSpec((ATTN_BLOCK, W), lambda n: (n, 0))
    return pl.pallas_call(
        body, name="attn_fwd", grid=(nb,),
        in_specs=[blk, pl.BlockSpec(kv.shape, lambda n: (0, 0)), pl.BlockSpec((1, HEAD_DIM), lambda n: (0, 0)),
                  pl.BlockSpec((1, HEAD_DIM), lambda n: (0, 0)), pl.BlockSpec(sink_rows.shape, lambda n: (0, 0, 0)),
                  pl.BlockSpec(band.shape, lambda n: (0, 0))],
        out_specs=blk, out_shape=jax.ShapeDtypeStruct((S, W), BF16),
        scratch_shapes=[pltpu.VMEM((ATTN_BLOCK, W), F32)],
        compiler_params=_params(dimension_semantics=("arbitrary",)),
    )(q, kv, qw, kw, sink_rows, band)


def _attn_bwd(q, kv, do, qw, kw, sink_rows, band):
    S, W = q.shape
    KW = kv.shape[1]
    nb = S // ATTN_BLOCK
    chunk = 512

    def body(q_ref, kv_ref, do_ref, qw_ref, kw_ref, sk_ref, band_ref, dq_ref, dkv_ref, dqw_ref, dkw_ref, dsk_ref, dq_scr):
        n = pl.program_id(0)

        @pl.when(n == 0)
        def _():
            dkv_ref[...] = jnp.zeros_like(dkv_ref)
            dqw_ref[...] = jnp.zeros_like(dqw_ref)
            dsk_ref[...] = jnp.zeros_like(dsk_ref)

        qf = q_ref[...].astype(F32)
        dof = do_ref[...].astype(F32)
        cur, prev = _attn_blocks(n)
        kvw = _kv_window(kv_ref, n)
        qw_v = qw_ref[...]
        for g in range(N_KV_HEADS):
            qs = _stack_heads(qf, g)
            rq = _rinv_lanes(qs)
            qhat = qs * rq
            qn = (qhat * qw_v * SCALE).astype(BF16)
            k, v = _kv_split(kvw, g)
            kk = (k * _rinv_lanes(k) * kw_ref[...]).astype(BF16)
            vv = v.astype(BF16)
            dos = _stack_heads(dof, g).astype(BF16)
            _, e, e_s, inv = _attn_exp(qn, kk, sk_ref[g], band_ref[...], n)
            p = e * _twice(inv)
            dp = _dot_nt(dos, vv)
            drow = _rowsum((p * dp).astype(BF16), ATTN_BLOCK)
            ds = (p * (dp - _twice(drow))).astype(BF16)
            dsink = -(e_s * inv * drow)
            for hh in range(GQA_GROUP):
                tot = jnp.sum(dsink[hh * ATTN_BLOCK:(hh + 1) * ATTN_BLOCK, :], axis=0, keepdims=True)
                dsk_ref[g, hh:hh + 1, :] += tot
            dqn = _dot(ds, kk) * SCALE
            dkk = _dot_tn(ds, qn)
            dvv = _dot_tn(p.astype(BF16), dos)
            klo, vlo = g * HEAD_DIM, (N_KV_HEADS + g) * HEAD_DIM
            for start, rows in ((prev, slice(0, ATTN_BLOCK)), (cur, slice(ATTN_BLOCK, 2 * ATTN_BLOCK))):
                dkv_ref[pl.ds(start, ATTN_BLOCK), klo:klo + HEAD_DIM] += dkk[rows]
                dkv_ref[pl.ds(start, ATTN_BLOCK), vlo:vlo + HEAD_DIM] += dvv[rows]
            dqw_ref[...] += jnp.sum(dqn * qhat, axis=0, keepdims=True)
            z = dqn * qw_v
            dqs = rq * (z - qhat * (_rowsum((z * qhat).astype(BF16), HEAD_DIM) * (1.0 / HEAD_DIM)))
            _unstack_heads(dq_scr, dqs, g)
        dq_ref[...] = dq_scr[...].astype(BF16)

        @pl.when(n == nb - 1)
        def _():
            def one(c, dkw):
                rows = pl.ds(pl.multiple_of(c * chunk, chunk), chunk)
                for g in range(N_KV_HEADS):
                    lo = g * HEAD_DIM
                    k = kv_ref[rows, lo:lo + HEAD_DIM].astype(F32)
                    dx, dg_rows = _rms_bwd(dkv_ref[rows, lo:lo + HEAD_DIM], k, kw_ref[...])
                    dkv_ref[rows, lo:lo + HEAD_DIM] = dx
                    dkw = dkw + jnp.sum(dg_rows, axis=0, keepdims=True)
                return dkw

            dkw_ref[...] = lax.fori_loop(0, S // chunk, one, jnp.zeros((1, HEAD_DIM), F32))

    blk = pl.BlockSpec((ATTN_BLOCK, W), lambda n: (n, 0))
    whole_kv = pl.BlockSpec((S, KW), lambda n: (0, 0))
    vec = pl.BlockSpec((1, HEAD_DIM), lambda n: (0, 0))
    sk = pl.BlockSpec(sink_rows.shape, lambda n: (0, 0, 0))
    dsk = pl.BlockSpec((N_KV_HEADS, GQA_GROUP, 128), lambda n: (0, 0, 0))
    return pl.pallas_call(
        body, name="attn_bwd", grid=(nb,), in_specs=[blk, whole_kv, blk, vec, vec, sk, pl.BlockSpec(band.shape, lambda n: (0, 0))],
        out_specs=[blk, whole_kv, vec, vec, dsk],
        out_shape=[jax.ShapeDtypeStruct((S, W), BF16), jax.ShapeDtypeStruct((S, KW), F32),
                   jax.ShapeDtypeStruct((1, HEAD_DIM), F32), jax.ShapeDtypeStruct((1, HEAD_DIM), F32),
                   jax.ShapeDtypeStruct((N_KV_HEADS, GQA_GROUP, 128), F32)],
        scratch_shapes=[pltpu.VMEM((ATTN_BLOCK, W), F32)],
        compiler_params=_params(dimension_semantics=("arbitrary",)),
    )(q, kv, do, qw, kw, sink_rows, band)


def _pooled(xc, xprev, i):
    tm = xc.shape[0]
    xh = jnp.concatenate([jnp.where(i > 0, xprev, 0.0), xc], axis=0)
    t = lax.broadcasted_iota(jnp.int32, (tm, 1), 0) + i * tm
    out = []
    for gi, w in enumerate(POOL_WINDOWS):
        acc = xh[:, gi * POOL_GROUP:(gi + 1) * POOL_GROUP]
        sh = 1
        while sh < w:
            acc = acc + pltpu.roll(acc, sh, 0)
            sh *= 2
        cnt = jnp.minimum(t + 1, w).astype(F32)
        out.append(acc[POOL_HALO:, :] / cnt - xc[:, gi * POOL_GROUP:(gi + 1) * POOL_GROUP])
    return jnp.concatenate(out, axis=1)


def _pool_mix(pooled_b, pw_ref):
    return jnp.concatenate([_dot(pooled_b[:, gi * POOL_GROUP:(gi + 1) * POOL_GROUP], pw_ref[gi])
                            for gi in range(len(POOL_WINDOWS))], axis=1)


def _halo_specs(tm, width, S, after):
    per = tm // POOL_HALO
    last = S // POOL_HALO - 1
    if after:
        return pl.BlockSpec((POOL_HALO, width), lambda i: (jnp.minimum((i + 1) * per, last), 0))
    return pl.BlockSpec((POOL_HALO, width), lambda i: (jnp.maximum(i * per - 1, 0), 0))


def _mix_out(xp, attn, gl, bias, pw, pscale, wpot, wao, wo, h, tm=512):
    S, D = h.shape
    nt = S // tm
    PW = xp.shape[1]

    def body(xc_ref, xprev_ref, at_ref, gl_ref, bias_ref, pw_ref, ps_ref, wpo_ref, wao_ref, wo_ref, h_ref,
             ho_ref, bp_ref, ba_ref):
        i = pl.program_id(0)
        pooled = _pooled(xc_ref[...], xprev_ref[...], i).astype(BF16)
        ms = (_pool_mix(pooled, pw_ref) * ps_ref[...]).astype(BF16)
        bp = _dot_nt(ms, wpo_ref[...])
        ba = _dot(at_ref[...], wao_ref[...])
        bp_ref[...] = bp.astype(BF16)
        ba_ref[...] = ba.astype(BF16)
        gates = jax.nn.sigmoid(gl_ref[...].astype(F32) + bias_ref[...])
        merged = (gates[:, :D] * bp + gates[:, D:] * ba).astype(BF16)
        ho_ref[...] = h_ref[...] + _dot(merged, wo_ref[...])

    def row(width):
        return pl.BlockSpec((tm, width), lambda i: (i, 0))

    def whole(x):
        nd = x.ndim
        return pl.BlockSpec(x.shape, lambda i: (0,) * nd)

    return pl.pallas_call(
        body, name="mix_out", grid=(nt,),
        in_specs=[row(PW), _halo_specs(tm, PW, S, False), row(D), row(2 * D), whole(bias), whole(pw), whole(pscale),
                  whole(wpot), whole(wao), whole(wo), row(D)],
        out_specs=[row(D), row(D), row(D)],
        out_shape=[jax.ShapeDtypeStruct((S, D), F32), jax.ShapeDtypeStruct((S, D), BF16),
                   jax.ShapeDtypeStruct((S, D), BF16)],
        compiler_params=_params(dimension_semantics=("arbitrary",)),
    )(xp, xp, attn, gl, bias, pw, pscale, wpot, wao, wo, h)


def _mix_bwd_gate(dh, wo, bp, ba, gl, bias, tm=512):
    S, D = dh.shape
    nt = S // tm

    def body(dh_ref, wo_ref, bp_ref, ba_ref, gl_ref, bias_ref, dgl_ref, dbp_ref, dba_ref, dwo_ref, dbias_ref):
        i = pl.program_id(0)
        dhb = dh_ref[...].astype(BF16)
        dm = _dot_nt(dhb, wo_ref[...])
        gates = jax.nn.sigmoid(gl_ref[...].astype(F32) + bias_ref[...])
        gp, ga = gates[:, :D], gates[:, D:]
        bp_v = bp_ref[...].astype(F32)
        ba_v = ba_ref[...].astype(F32)
        merged = (gp * bp_v + ga * ba_v).astype(BF16)
        _acc(dwo_ref, _dot_tn(merged, dhb), i == 0)
        dbp_ref[...] = (dm * gp).astype(BF16)
        dba_ref[...] = (dm * ga).astype(BF16)
        dgl = jnp.concatenate([dm * bp_v * gp * (1.0 - gp), dm * ba_v * ga * (1.0 - ga)], axis=1)
        dgl_ref[...] = dgl.astype(BF16)
        _acc(dbias_ref, jnp.sum(dgl, axis=0, keepdims=True), i == 0)

    def row(width):
        return pl.BlockSpec((tm, width), lambda i: (i, 0))

    def whole(shape):
        return pl.BlockSpec(shape, lambda i: (0, 0))

    return pl.pallas_call(
        body, name="mix_bwd_gate", grid=(nt,),
        in_specs=[row(D), whole(wo.shape), row(D), row(D), row(2 * D), whole(bias.shape)],
        out_specs=[row(2 * D), row(D), row(D), whole((D, D)), whole((1, 2 * D))],
        out_shape=[jax.ShapeDtypeStruct((S, 2 * D), BF16), jax.ShapeDtypeStruct((S, D), BF16),
                   jax.ShapeDtypeStruct((S, D), BF16), jax.ShapeDtypeStruct((D, D), F32),
                   jax.ShapeDtypeStruct((1, 2 * D), F32)],
        compiler_params=_params(dimension_semantics=("arbitrary",)),
    )(dh, wo, bp, ba, gl, bias)


def _mix_bwd_branch(dbp, dba, attn, xp, pw, pscale, wpot, wao, tm=1024):
    S, D = dbp.shape
    nt = S // tm
    PW = xp.shape[1]
    NG = len(POOL_WINDOWS)

    def body(dbp_ref, dba_ref, at_ref, xc_ref, xprev_ref, pw_ref, ps_ref, wpo_ref, wao_ref,
             dat_ref, dpl_ref, dwao_ref, dwpo_ref, dpw_ref, dps_ref):
        i = pl.program_id(0)
        dba_v = dba_ref[...]
        dbp_v = dbp_ref[...]
        _acc(dwao_ref, _dot_tn(at_ref[...], dba_v), i == 0)
        dat_ref[...] = _dot_nt(dba_v, wao_ref[...]).astype(BF16)
        pooled = _pooled(xc_ref[...], xprev_ref[...], i).astype(BF16)
        mixed = _pool_mix(pooled, pw_ref)
        ps = ps_ref[...]
        _acc(dwpo_ref, _dot_tn(dbp_v, (mixed * ps).astype(BF16)), i == 0)
        dms = _dot(dbp_v, wpo_ref[...])
        _acc(dps_ref, jnp.sum(dms * mixed, axis=0, keepdims=True), i == 0)
        dmixed = (dms * ps).astype(BF16)
        dpooled = []
        for gi in range(NG):
            cols = slice(gi * POOL_GROUP, (gi + 1) * POOL_GROUP)
            _acc(dpw_ref.at[gi], _dot_tn(pooled[:, cols], dmixed[:, cols]), i == 0)
            dpooled.append(_dot_nt(dmixed[:, cols], pw_ref[gi]))
        dpl_ref[...] = jnp.concatenate(dpooled, axis=1)

    def row(width):
        return pl.BlockSpec((tm, width), lambda i: (i, 0))

    def whole(shape):
        nd = len(shape)
        return pl.BlockSpec(shape, lambda i: (0,) * nd)

    return pl.pallas_call(
        body, name="mix_bwd_branch", grid=(nt,),
        in_specs=[row(D), row(D), row(D), row(PW), _halo_specs(tm, PW, S, False), whole(pw.shape), whole(pscale.shape),
                  whole(wpot.shape), whole(wao.shape)],
        out_specs=[row(D), row(PW), whole((D, D)), whole((D, PW)), whole(pw.shape), whole((1, PW))],
        out_shape=[jax.ShapeDtypeStruct((S, D), BF16), jax.ShapeDtypeStruct((S, PW), F32),
                   jax.ShapeDtypeStruct((D, D), F32), jax.ShapeDtypeStruct((D, PW), F32),
                   jax.ShapeDtypeStruct(pw.shape, F32), jax.ShapeDtypeStruct((1, PW), F32)],
        compiler_params=_params(dimension_semantics=("arbitrary",)),
    )(dbp, dba, attn, xp, xp, pw, pscale, wpot, wao)


def _pool_bwd(dpooled, tm=512):
    S, PW = dpooled.shape
    nt = S // tm

    def body(dc_ref, dnext_ref, dxp_ref):
        i = pl.program_id(0)
        dc = dc_ref[...]
        dh = jnp.concatenate([dc, jnp.where(i < nt - 1, dnext_ref[...], 0.0)], axis=0)
        rows = tm + POOL_HALO
        t = lax.broadcasted_iota(jnp.int32, (rows, 1), 0) + i * tm
        out = []
        for gi, w in enumerate(POOL_WINDOWS):
            cols = slice(gi * POOL_GROUP, (gi + 1) * POOL_GROUP)
            acc = dh[:, cols] / jnp.minimum(t + 1, w).astype(F32)
            sh = 1
            while sh < w:
                acc = acc + pltpu.roll(acc, rows - sh, 0)
                sh *= 2
            out.append(acc[:tm, :] - dc[:, cols])
        dxp_ref[...] = jnp.concatenate(out, axis=1).astype(BF16)

    return pl.pallas_call(
        body, name="pool_bwd", grid=(nt,),
        in_specs=[pl.BlockSpec((tm, PW), lambda i: (i, 0)), _halo_specs(tm, PW, S, True)],
        out_specs=pl.BlockSpec((tm, PW), lambda i: (i, 0)), out_shape=jax.ShapeDtypeStruct((S, PW), BF16),
        compiler_params=_params(dimension_semantics=("arbitrary",)),
    )(dpooled, dpooled)


def _in_bwd_w(pieces, u, tm=1024):
    S, D = u.shape
    nt = S // tm
    NW = sum(width for _, width in SEGMENTS)

    def body(*refs):
        piece_refs, u_ref, dw_hbm, acc, sem = refs[:len(SEGMENTS)], refs[len(SEGMENTS)], refs[len(SEGMENTS) + 1], refs[-2], refs[-1]
        i = pl.program_id(0)
        u_t = u_ref[...]
        for (off, width), ref in zip(SEGMENTS, piece_refs):
            for lo in range(0, width, D):
                hi = min(lo + D, width)
                _acc(acc.at[off + lo:off + hi, :], _dot_tn(ref[:, lo:hi].astype(BF16), u_t), i == 0)

        @pl.when(i == nt - 1)
        def _():
            cp = pltpu.make_async_copy(acc, dw_hbm, sem)
            cp.start()
            cp.wait()

    return pl.pallas_call(
        body, name="in_bwd_w", grid=(nt,),
        in_specs=[pl.BlockSpec((tm, width), lambda i: (i, 0)) for _, width in SEGMENTS] + [pl.BlockSpec((tm, D), lambda i: (i, 0))],
        out_specs=pl.BlockSpec(memory_space=pl.ANY), out_shape=jax.ShapeDtypeStruct((NW, D), F32),
        scratch_shapes=[pltpu.VMEM((NW, D), F32), pltpu.SemaphoreType.DMA],
        compiler_params=_params(dimension_semantics=("arbitrary",)),
    )(*pieces, u)


def _in_bwd_x(pieces, wint, h, dh, gnorm, tm=512):
    S, D = h.shape
    nt = S // tm

    def body(*refs):
        piece_refs = refs[:len(SEGMENTS)]
        w_ref, h_ref, dh_ref, g_ref, dx_ref, dxh_ref, dg_ref = refs[len(SEGMENTS):]
        i = pl.program_id(0)
        du = jnp.zeros((tm, D), F32)
        for (off, width), ref in zip(SEGMENTS, piece_refs):
            du = du + _dot(ref[...].astype(BF16), w_ref[off:off + width, :])
        dx, dg_rows = _rms_bwd(du, h_ref[...], g_ref[...])
        out = dh_ref[...] + dx
        dx_ref[...] = out
        dxh_ref[...] = (0.5 * out).astype(BF16)
        _acc(dg_ref, jnp.sum(dg_rows, axis=0, keepdims=True), i == 0)

    def row(width):
        return pl.BlockSpec((tm, width), lambda i: (i, 0))

    vec = pl.BlockSpec((1, D), lambda i: (0, 0))
    return pl.pallas_call(
        body, name="in_bwd_x", grid=(nt,),
        in_specs=[row(width) for _, width in SEGMENTS] + [pl.BlockSpec(wint.shape, lambda i: (0, 0)), row(D), row(D), vec],
        out_specs=[row(D), row(D), vec],
        out_shape=[jax.ShapeDtypeStruct((S, D), F32), jax.ShapeDtypeStruct((S, D), BF16), jax.ShapeDtypeStruct((1, D), F32)],
        compiler_params=_params(dimension_semantics=("arbitrary",)),
    )(*pieces, wint, h, dh, gnorm)


def _row_tile(rows):
    for t in (512, 480, 352, 256, 128, 64, 32, 16, 8):
        if rows % t == 0:
            return t
    return rows


def _adamw_update(w, g, m, v):
    mn = ADAM_B1 * m + (1.0 - ADAM_B1) * g
    vn = ADAM_B2 * v + (1.0 - ADAM_B2) * (g * g)
    m_hat = mn / (1.0 - ADAM_B1 ** ADAM_STEP)
    v_hat = vn / (1.0 - ADAM_B2 ** ADAM_STEP)
    return -ADAM_LR * (m_hat / (jnp.sqrt(v_hat) + ADAM_EPS) + ADAM_WD * w), mn, vn


def _adamw_small(ws, gs, ms, vs):
    n = len(ws)

    def rows(a):
        return a.reshape(-1, a.shape[-1]) if a.ndim > 1 else a.reshape(1, -1)

    def body(*refs):
        for k in range(n):
            w_ref, g_ref, m_ref, v_ref, d_ref, mo_ref, vo_ref = refs[k::n]
            d_ref[...], mo_ref[...], vo_ref[...] = _adamw_update(w_ref[...], g_ref[...], m_ref[...], v_ref[...])

    vmem = pl.BlockSpec(memory_space=pltpu.VMEM)
    out = pl.pallas_call(
        body, name="adamw_small", in_specs=[vmem] * (4 * n), out_specs=[vmem] * (3 * n),
        out_shape=[jax.ShapeDtypeStruct(rows(w).shape, F32) for _ in range(3) for w in ws], compiler_params=_params(),
    )(*[rows(a) for a in ws + gs + ms + vs])
    return [[o.reshape(w.shape) for o, w in zip(out[k * n:(k + 1) * n], ws)] for k in range(3)]


def _adamw(w, g, m, v, token, name, turned=False):
    R, C = w.shape
    tr = R if turned else _row_tile(R)

    def body(w_ref, g_ref, m_ref, v_ref, _, d_ref, mo_ref, vo_ref, go_ref):
        gv = g_ref[...].T if turned else g_ref[...]
        go_ref[...] = gv
        d_ref[...], mo_ref[...], vo_ref[...] = _adamw_update(w_ref[...], gv, m_ref[...], v_ref[...])

    blk = pl.BlockSpec((tr, C), lambda i: (i, 0))
    g_blk = pl.BlockSpec((C, R), lambda i: (0, 0)) if turned else blk
    sh = jax.ShapeDtypeStruct((R, C), F32)
    return pl.pallas_call(
        body, name=name, grid=(R // tr,), in_specs=[blk, g_blk, blk, blk, TOKEN_SPEC], out_specs=[blk] * 4, out_shape=[sh] * 4,
        compiler_params=_params(dimension_semantics=("arbitrary",)),
    )(w, g, m, v, token)


def _cast_place(w, place, name, turned=False, after=()):
    R, C = w.shape[::-1] if turned else w.shape
    tr = R if turned else _row_tile(R)
    per = R // tr

    def body(p_ref, w_ref, *rest):
        rest[-1][...] = (w_ref[...].T if turned else w_ref[...]).astype(BF16)

    grid_spec = pltpu.PrefetchScalarGridSpec(
        num_scalar_prefetch=1, grid=(per,),
        in_specs=[pl.BlockSpec((C, R), lambda i, p: (0, 0)) if turned else pl.BlockSpec((tr, C), lambda i, p: (i, 0))]
        + [TOKEN_SPEC] * len(after),
        out_specs=pl.BlockSpec((tr, C), lambda i, p: (p[0] * per + i, 0)))
    return pl.pallas_call(
        body, name=name, grid_spec=grid_spec, out_shape=jax.ShapeDtypeStruct((N_SHARDS * R, C), BF16),
        compiler_params=_params(dimension_semantics=("arbitrary",)),
    )(place, w, *after)


def _parked(w, m, idx, first, last):
    return jnp.where(m == w, idx, jnp.where(m < w, first, last))


def _sum_halves(g4s, recvs, place, name):
    M = len(g4s)
    NS, R, C = g4s[0].shape
    hr = R // 2
    tr = _row_tile(hr)
    per = hr // tr

    def body(p_ref, *refs):
        m = pl.program_id(0)
        for w in range(M):
            @pl.when(m == w)
            def _(w=w):
                refs[2 * M + w][...] = (refs[w][...] + refs[M + w][...]).astype(BF16)

    def spec(w, mine):
        def index(m, s, i, p):
            row = _parked(w, m, i, 0, per - 1)
            return _parked(w, m, s, 0, NS - 1), (p[1] * per + row if mine else row), 0

        return pl.BlockSpec((1, tr, C), index)

    grid_spec = pltpu.PrefetchScalarGridSpec(
        num_scalar_prefetch=1, grid=(M, NS, per),
        in_specs=[spec(w, True) for w in range(M)] + [spec(w, False) for w in range(M)],
        out_specs=[spec(w, False) for w in range(M)])
    return pl.pallas_call(
        body, name=name, grid_spec=grid_spec, out_shape=[jax.ShapeDtypeStruct((NS, hr, C), BF16)] * M,
        compiler_params=_params(dimension_semantics=("arbitrary",) * 3),
    )(place, *g4s, *recvs)


def _sum_quarters(h4s, recv3s, place, name):
    M = len(h4s)
    NS, hr, C = h4s[0].shape
    tr = _row_tile(hr)
    per = hr // tr

    def body(p_ref, *refs):
        m = pl.program_id(0)
        for w in range(M):
            @pl.when(m == w)
            def _(w=w):
                acc = refs[w][0].astype(F32)
                for k in range(N_SHARDS - 1):
                    acc = acc + refs[M + w][k].astype(F32)
                refs[2 * M + w][...] = acc

    def row(w, m, i):
        return _parked(w, m, i, 0, per - 1)

    grid_spec = pltpu.PrefetchScalarGridSpec(
        num_scalar_prefetch=1, grid=(M, per),
        in_specs=[pl.BlockSpec((1, tr, C), lambda m, i, p, w=w: (p[0], row(w, m, i), 0)) for w in range(M)]
        + [pl.BlockSpec((N_SHARDS - 1, tr, C), lambda m, i, p, w=w: (0, row(w, m, i), 0)) for w in range(M)],
        out_specs=[pl.BlockSpec((tr, C), lambda m, i, p, w=w: (p[1] * per + row(w, m, i), 0)) for w in range(M)])
    return pl.pallas_call(
        body, name=name, grid_spec=grid_spec, out_shape=[jax.ShapeDtypeStruct((2 * hr, C), F32)] * M,
        compiler_params=_params(dimension_semantics=("arbitrary",) * 2),
    )(place, *h4s, *recv3s)


def _by_shape(arrays):
    groups = {}
    for k, a in enumerate(arrays):
        groups.setdefault(a.shape, []).append(k)
    return list(groups.values())


def _place():
    x, y, c = lax.axis_index("x"), lax.axis_index("y"), lax.axis_index("c")
    chips = [(1 - x, y), (x, 1 - y), (1 - x, 1 - y)]
    return x, y, c, chips


HBM_SPEC = pl.BlockSpec(memory_space=pltpu.HBM)
SEM_SPEC = pl.BlockSpec(memory_space=pltpu.SEMAPHORE)
DATAFLOW = pltpu.SideEffectType.DATAFLOW_SIDE_EFFECTING


def _hbm(a):
    return pltpu.with_memory_space_constraint(a, pltpu.HBM)


class InFlight(NamedTuple):
    send_sem: jax.Array
    recv_sem: jax.Array
    bufs: list
    plan: Callable
    token: jax.Array


class Leg(NamedTuple):
    bufs: list
    landing: Optional[InFlight]
    plan: Optional[Callable]
    n_copies: int


def _send(bufs, plan, n_copies):
    return Leg([_hbm(b) for b in bufs], None, plan, n_copies)


def _land(flight):
    return Leg(flight.bufs, flight, None, 0)


def _forward(flight, plan, n_copies):
    return Leg(flight.bufs, flight, plan, n_copies)


def _wait_all(plan, refs, send_ref, recv_ref):
    for k, (src, dst, dev) in enumerate(plan(refs)):
        cp = pltpu.make_async_remote_copy(src_ref=src, dst_ref=dst, send_sem=send_ref.at[k], recv_sem=recv_ref.at[k],
                                          device_id=dev, device_id_type=MESH)
        cp.wait_send()
        cp.wait_recv()


def _start_all(plan, refs, send_ref, recv_ref):
    for k, (src, dst, dev) in enumerate(plan(refs)):
        pltpu.make_async_remote_copy(src_ref=src, dst_ref=dst, send_sem=send_ref.at[k], recv_sem=recv_ref.at[k],
                                     device_id=dev, device_id_type=MESH).start()


def _comm(name, legs, after):
    after = list(after) if isinstance(after, (list, tuple)) else [after]
    ins, in_specs, out_shape, out_specs, aliases, first = [], [], [], [], {}, []
    for leg in legs:
        first.append((len(ins), len(out_shape)))
        for b in leg.bufs:
            aliases[len(ins)] = len(out_shape)
            ins.append(b)
            in_specs.append(HBM_SPEC)
            out_shape.append(pltpu.HBM(b.shape, b.dtype))
            out_specs.append(HBM_SPEC)
        if leg.landing is not None:
            ins += [leg.landing.send_sem, leg.landing.recv_sem]
            in_specs += [SEM_SPEC, SEM_SPEC]
        if leg.plan is not None:
            out_shape += [pltpu.SemaphoreType.DMA((leg.n_copies,))] * 2
            out_specs += [SEM_SPEC, SEM_SPEC]
    starts = any(leg.plan is not None for leg in legs)
    if starts:
        out_shape.append(jax.ShapeDtypeStruct((8, 128), F32))
        out_specs.append(pl.BlockSpec(memory_space=pltpu.VMEM))
    n_in = len(ins) + len(after)

    def body(*refs):
        outs = refs[n_in:]
        for leg, (i, o) in zip(legs, first):
            nb = len(leg.bufs)
            bufs = refs[i:i + nb]
            if leg.landing is not None:
                _wait_all(leg.landing.plan, bufs, refs[i + nb], refs[i + nb + 1])
            if leg.plan is not None:
                _start_all(leg.plan, bufs, outs[o + nb], outs[o + nb + 1])
        if starts:
            outs[-1][...] = jnp.zeros_like(outs[-1])

    out = pl.pallas_call(
        body, name=name, in_specs=in_specs + [TOKEN_SPEC] * len(after), out_shape=out_shape, out_specs=out_specs,
        input_output_aliases=aliases, compiler_params=pltpu.CompilerParams(has_side_effects=DATAFLOW),
    )(*ins, *after)
    results = []
    for leg, (_, o) in zip(legs, first):
        nb = len(leg.bufs)
        bufs = list(out[o:o + nb])
        results.append(bufs if leg.plan is None else InFlight(out[o + nb], out[o + nb + 1], bufs, leg.plan, out[-1]))
    return results


def _half_rows(buf, chip, core):
    hr = buf.shape[0] // (2 * N_SHARDS)
    return buf.at[pl.ds(pl.multiple_of((2 * chip + core) * hr, 16), hr)]


def _gather_ici_plan(bufs):
    x, y, c, chips = _place()
    return [(_half_rows(b, 2 * x + y, c), _half_rows(b, 2 * x + y, c), (px, py, c)) for b in bufs for px, py in chips]


def _gather_d2d_plan(bufs):
    x, y, c, chips = _place()
    return [(_half_rows(b, 2 * px + py, c), _half_rows(b, 2 * px + py, c), (x, y, 1 - c)) for b in bufs for px, py in chips]


def _swap_plan(bufs):
    x, y, c, _ = _place()
    n = len(bufs) // 2
    copies = []
    for g, land in zip(bufs[:n], bufs[n:]):
        hr = g.shape[1] // 2
        copies.append((g.at[:, pl.ds(pl.multiple_of((1 - c) * hr, 8), hr)], land, (x, y, 1 - c)))
    return copies


def _exchange_plan(bufs):
    x, y, c, chips = _place()
    n = len(bufs) // 2
    return [(h.at[2 * px + py], land.at[k], (px, py, c))
            for h, land in zip(bufs[:n], bufs[n:]) for k, (px, py) in enumerate(chips)]


def _share_plan(bufs):
    x, y, c, _ = _place()
    copies = []
    for buf in bufs:
        hr = buf.shape[0] // 2
        mine = buf.at[pl.ds(pl.multiple_of(c * hr, 8), hr)]
        copies.append((mine, mine, (x, y, 1 - c)))
    return copies


N_DEVICES = 8


def _slot_place(vec, place):
    R, C = vec.shape

    def body(p_ref, v_ref, o_ref):
        o_ref[0] = v_ref[...]

    grid_spec = pltpu.PrefetchScalarGridSpec(
        num_scalar_prefetch=1, grid=(1,), in_specs=[pl.BlockSpec((R, C), lambda i, p: (0, 0))],
        out_specs=pl.BlockSpec((1, R, C), lambda i, p: (2 * p[0] + p[1], 0, 0)))
    return pl.pallas_call(
        body, name="slot_place", grid_spec=grid_spec, out_shape=jax.ShapeDtypeStruct((N_DEVICES, R, C), F32),
        compiler_params=_params(dimension_semantics=("arbitrary",)),
    )(place, vec)


def _slots_plan(bufs):
    x, y, c, _ = _place()
    mine = bufs[0].at[4 * x + 2 * y + c]
    return [(mine, mine, (x ^ (r >> 2), y ^ ((r >> 1) & 1), c ^ (r & 1))) for r in range(1, N_DEVICES)]


def _sum_slots(slots):
    _, R, C = slots.shape

    def body(s_ref, o_ref):
        acc = s_ref[0]
        for d in range(1, N_DEVICES):
            acc = acc + s_ref[d]
        o_ref[...] = acc

    return pl.pallas_call(
        body, name="sum_slots", in_specs=[pl.BlockSpec(memory_space=pltpu.VMEM)],
        out_specs=pl.BlockSpec(memory_space=pltpu.VMEM), out_shape=jax.ShapeDtypeStruct((R, C), F32),
        compiler_params=_params(),
    )(slots)


SMALL = ("ffn1_norm", "mix_norm", "pool_w", "pool_scale", "q_norm", "k_norm", "sinks", "gate_bias", "ffn2_norm")
SMALL_COLS = 1024
FFN1 = ("ffn1_w_gate", "ffn1_w_up", "ffn1_w_down")
MIXER = ("w_in", "w_pool_out", "w_attn_out", "w_out")
FFN2 = ("ffn2_w_gate", "ffn2_w_up", "ffn2_w_down")
LARGE = FFN1 + MIXER + FFN2
TRANSPOSED = ("ffn1_w_gate", "ffn1_w_up", "w_in", "w_pool_out", "ffn2_w_gate", "ffn2_w_up")
TURNED = ("w_pool_out",)
WEIGHTS = ("ffn1_norm", "ffn1_w_gate", "ffn1_w_up", "ffn1_w_down", "mix_norm", "w_in", "pool_w", "pool_scale",
           "w_pool_out", "q_norm", "k_norm", "sinks", "w_attn_out", "gate_bias", "w_out", "ffn2_norm",
           "ffn2_w_gate", "ffn2_w_up", "ffn2_w_down")


def _pack_small(parts):
    flat = jnp.concatenate([p.reshape(-1) for p in parts])
    rows = -(-flat.shape[0] // (8 * SMALL_COLS)) * 8
    return jnp.pad(flat, (0, rows * SMALL_COLS - flat.shape[0])).reshape(rows, SMALL_COLS)


def _unpack_small(packed, like):
    flat = packed.reshape(-1)
    out, off = [], 0
    for p in like:
        out.append(flat[off:off + p.size].reshape(p.shape))
        off += p.size
    return out, flat[off]


def _tie(small, token):
    return small + token[0, 0]


class Reduction:
    def __init__(self, names, place):
        self.names, self.place = names, place

    def swap(self, grads):
        grads = [g.reshape(N_SHARDS, -1, g.shape[-1]) for g in grads]
        lands = [lax.empty((g.shape[0], g.shape[1] // 2, g.shape[2]), g.dtype) for g in grads]
        return _send(grads + lands, _swap_plan, len(grads))

    def exchange(self, swapped):
        n = len(self.names)
        halves = self._per_shape(_sum_halves, "sum_halves", swapped[:n], swapped[n:])
        lands = [lax.empty((N_SHARDS - 1,) + h.shape[1:], h.dtype) for h in halves]
        return _send(halves + lands, _exchange_plan, (N_SHARDS - 1) * n)

    def share(self, exchanged):
        n = len(self.names)
        return _send(self._per_shape(_sum_quarters, "sum_quarters", exchanged[:n], exchanged[n:]), _share_plan, n)

    def named(self, shared):
        return dict(zip(self.names, shared))

    def _per_shape(self, add, stage, mine, received):
        out = [None] * len(mine)
        for idx in _by_shape(mine):
            sums = add([mine[k] for k in idx], [received[k] for k in idx], self.place, f"{stage}_{self.names[idx[0]]}")
            for k, v in zip(idx, sums):
                out[k] = v
        return out


def _row_form(name, a):
    return a.T if name in TRANSPOSED else a


def _shard_form(name, a):
    return a if name in TURNED else _row_form(name, a)


def kernel(x, ffn1_norm, ffn1_w_gate, ffn1_w_up, ffn1_w_down, mix_norm, w_in, pool_w, pool_scale, w_pool_out, q_norm, k_norm, sinks, w_attn_out, gate_bias, w_out, ffn2_norm, ffn2_w_gate, ffn2_w_up, ffn2_w_down, loss_target, m_ffn1_norm, m_ffn1_w_gate, m_ffn1_w_up, m_ffn1_w_down, m_mix_norm, m_w_in, m_pool_w, m_pool_scale, m_w_pool_out, m_q_norm, m_k_norm, m_sinks, m_w_attn_out, m_gate_bias, m_w_out, m_ffn2_norm, m_ffn2_w_gate, m_ffn2_w_up, m_ffn2_w_down, v_ffn1_norm, v_ffn1_w_gate, v_ffn1_w_up, v_ffn1_w_down, v_mix_norm, v_w_in, v_pool_w, v_pool_scale, v_w_pool_out, v_q_norm, v_k_norm, v_sinks, v_w_attn_out, v_gate_bias, v_w_out, v_ffn2_norm, v_ffn2_w_gate, v_ffn2_w_up, v_ffn2_w_down):
    args = dict(locals())
    wts = {n: _shard_form(n, args[n]) for n in WEIGHTS}
    mom = {n: _shard_form(n, args["m_" + n]) for n in WEIGHTS}
    var = {n: _shard_form(n, args["v_" + n]) for n in WEIGHTS}
    shard = 2 * lax.axis_index("x") + lax.axis_index("y")
    place = jnp.stack([shard, lax.axis_index("c")]).astype(jnp.int32)

    xs, target = x[0], loss_target[0]
    D = xs.shape[1]
    g1 = wts["ffn1_norm"].reshape(1, D)
    gm = wts["mix_norm"].reshape(1, D)
    g2 = wts["ffn2_norm"].reshape(1, D)
    qw = wts["q_norm"].reshape(1, HEAD_DIM)
    kw = wts["k_norm"].reshape(1, HEAD_DIM)
    bias = wts["gate_bias"].reshape(1, 2 * D)
    pscale = wts["pool_scale"].reshape(1, -1)
    pw = wts["pool_w"].astype(BF16)
    sink_rows = jnp.broadcast_to(jnp.repeat(wts["sinks"], ATTN_BLOCK).reshape(N_KV_HEADS, GQA_GROUP * ATTN_BLOCK, 1),
                                 (N_KV_HEADS, GQA_GROUP * ATTN_BLOCK, 128))

    gate1, up1, down1 = FFN1[:1], FFN1[1:2], FFN1[2:]
    mix_in, mix_out = MIXER[:1], MIXER[1:]

    def over_ici(names):
        return _send([placed[n] for n in names], _gather_ici_plan, 3 * len(names))

    def to_sibling(flight):
        return _forward(flight, _gather_d2d_plan, len(flight.bufs) * 3)

    placed = {n: _cast_place(wts[n], place, "cast_" + n) for n in gate1}
    gate1_ici, = _comm("gather_gate1_ici", [over_ici(gate1)], place)
    placed.update({n: _cast_place(wts[n], place, "cast_" + n) for n in up1 + down1})
    up1_ici, down1_ici = _comm("gather_ffn1_ici", [over_ici(up1), over_ici(down1)], gate1_ici.token)
    placed.update({n: _cast_place(wts[n], place, "cast_" + n, turned=n in TURNED) for n in MIXER})
    band = _band_bias()
    shadow = [sink_rows, pw, band, down1_ici.token]
    gate1_d2d, in_ici, out_ici = _comm(
        "gather_mix_ici", [to_sibling(gate1_ici), over_ici(mix_in), over_ici(mix_out)], shadow)
    w = dict(zip(gate1, _comm("gather_gate1_wait", [_land(gate1_d2d)], gate1_d2d.token)[0]))

    n1, a1 = _ffn_gate(xs, g1, w["ffn1_w_gate"], "ffn1_gate")
    placed.update({n: _cast_place(wts[n], place, "cast_" + n, after=[a1]) for n in FFN2})
    up1_d2d, ffn2_ici, down1_d2d = _comm(
        "gather_ffn1_d2d", [to_sibling(up1_ici), over_ici(FFN2), to_sibling(down1_ici)], a1)
    w.update(zip(up1, _comm("gather_up1_wait", [_land(up1_d2d)], up1_d2d.token)[0]))
    p1, q1, s1 = _ffn_hidden(n1, a1, w["ffn1_w_up"], "ffn1_hidden")
    w.update(zip(down1, _comm("gather_down1_wait", [_land(down1_d2d)], s1)[0]))
    h1 = _ffn_down(xs, s1, w["ffn1_w_down"], "ffn1_down")
    in_d2d, out_d2d = _comm("gather_mix_d2d", [to_sibling(in_ici), to_sibling(out_ici)], h1)
    w.update(zip(mix_in, _comm("gather_in_wait", [_land(in_d2d)], in_d2d.token)[0]))
    u, xp, q, kv, gl = _mix_proj(h1, gm, w["w_in"])
    attn = _attn_fwd(q, kv, qw, kw, sink_rows, band)
    ffn2_d2d, landed = _comm("gather_ffn2_d2d", [to_sibling(ffn2_ici), _land(out_d2d)], attn)
    w.update(zip(mix_out, landed))
    h2, bp, ba = _mix_out(xp, attn, gl, bias, pw, pscale, w["w_pool_out"], w["w_attn_out"], w["w_out"], h1)
    w.update(zip(FFN2, _comm("gather_ffn2_wait", [_land(ffn2_d2d)], h2)[0]))
    n2, p2, q2, s2 = _ffn_up(h2, g2, w["ffn2_w_gate"], w["ffn2_w_up"], "ffn2_up")
    dy, dyh, loss = _ffn_down_loss(h2, s2, w["ffn2_w_down"], target, "ffn2_down_loss")

    gw, gs = {}, {}
    da, db = _ffn_bwd_h(dyh, p2, q2, w["ffn2_w_down"], ffn2_d2d.token, "ffn2_bwd_h")
    gw["ffn2_w_gate"], gw["ffn2_w_up"] = _xty(da, n2, "ffn2_dw_gate"), _xty(db, n2, "ffn2_dw_up")
    gw["ffn2_w_down"] = _xty(s2, dyh, "ffn2_dw_down")
    red2 = Reduction(FFN2, place)
    swap, = _comm("reduce_ffn2_swap", [red2.swap([gw[n] for n in FFN2])], da)
    dh2, gs["ffn2_norm"] = _ffn_bwd_x(da, db, w["ffn2_w_gate"], w["ffn2_w_up"], h2, dy, _tie(g2, swap.token), "ffn2_bwd_x")
    swapped, = _comm("reduce_ffn2_swap_wait", [_land(swap)], dh2)
    exchange, = _comm("reduce_ffn2_exchange", [red2.exchange(swapped)], place)
    dgl, dbp, dba, gw["w_out"], gs["gate_bias"] = _mix_bwd_gate(dh2, w["w_out"], bp, ba, gl, _tie(bias, exchange.token))
    dattn, dpooled, gw["w_attn_out"], gw["w_pool_out"], gs["pool_w"], gs["pool_scale"] = _mix_bwd_branch(
        dbp, dba, attn, xp, pw, pscale, w["w_pool_out"], w["w_attn_out"])
    dq, dkv, gs["q_norm"], gs["k_norm"], dsk = _attn_bwd(q, kv, dattn, qw, kw, sink_rows, band)
    gs["sinks"] = dsk[:, :, 0]
    exchanged, = _comm("reduce_ffn2_exchange_wait", [_land(exchange)], dq)
    share2, = _comm("reduce_ffn2_share", [red2.share(exchanged)], place)
    pieces = (_pool_bwd(dpooled), dq, dkv, dgl)
    gw["w_in"] = _in_bwd_w(pieces, u)
    redm = Reduction(MIXER, place)
    swap, shared = _comm("reduce_mix_swap", [redm.swap([gw[n] for n in MIXER]), _land(share2)], share2.token)
    grads = red2.named(shared)
    dh1, dh1h, gs["mix_norm"] = _in_bwd_x(pieces, w["w_in"], h1, dh2, _tie(gm, swap.token))
    red1, swaps = Reduction(FFN1, place), {}
    dw_down = _xty(s1, dh1h, "ffn1_dw_down")
    swaps["ffn1_w_down"], swapped = _comm("reduce_mix_swap_wait", [red1.swap([dw_down]), _land(swap)], dh1)
    exchange, = _comm("reduce_mix_exchange", [redm.exchange(swapped)], place)
    da, db = _ffn_bwd_h(dh1h, p1, q1, w["ffn1_w_down"], exchange.token, "ffn1_bwd_h")
    dw_up = _xty(db, n1, "ffn1_dw_up")
    swaps["ffn1_w_up"], = _comm("reduce_ffn1_swap_up", [red1.swap([dw_up])], place)
    dw_gate = _xty(da, n1, "ffn1_dw_gate", after=[swaps["ffn1_w_up"].token])
    swaps["ffn1_w_gate"], exchanged = _comm("reduce_mix_exchange_wait", [red1.swap([dw_gate]), _land(exchange)], place)
    sharem, *landed = _comm("reduce_mix_share", [redm.share(exchanged)] + [_land(swaps[n]) for n in FFN1], place)
    exchange, = _comm("reduce_ffn1_exchange", [red1.exchange([l[0] for l in landed] + [l[1] for l in landed])], place)
    grad_x, gs["ffn1_norm"] = _ffn_bwd_x(da, db, w["ffn1_w_gate"], w["ffn1_w_up"], xs, dh1, _tie(g1, exchange.token), "ffn1_bwd_x")

    small_parts = [gs[n] for n in SMALL] + [loss[0, 0].reshape(1)]
    slots, shared = _comm(
        "gather_small", [_send([_slot_place(_pack_small(small_parts), place)], _slots_plan, N_DEVICES - 1), _land(sharem)],
        grad_x)
    grads.update(redm.named(shared))
    delta, new_m, new_v = {}, {}, {}

    def adamw(names, token):
        for n in names:
            delta[n], new_m[n], new_v[n], grads[n] = _adamw(wts[n], grads[n], mom[n], var[n], token, "adamw_" + n, n in TURNED)

    early, late = FFN2 + MIXER[:-1], MIXER[-1:]
    adamw(early, slots.token)
    exchanged, = _comm("reduce_ffn1_exchange_wait", [_land(exchange)], [delta[n] for n in early])
    share1, landed = _comm("reduce_ffn1_share", [red1.share(exchanged), _land(slots)], place)
    summed = _sum_slots(landed[0])
    small_grads, loss_sum = _unpack_small(summed, [wts[n] for n in SMALL])
    grads.update(dict(zip(SMALL, small_grads)))
    for out, vals in zip((delta, new_m, new_v), _adamw_small(*[[d[n] for n in SMALL] for d in (wts, grads, mom, var)])):
        out.update(dict(zip(SMALL, vals)))
    adamw(late, share1.token)
    results = {(k, n): _shard_form(n, d[n]) for k, d in enumerate((grads, delta, new_m, new_v)) for n in SMALL + FFN2 + MIXER}
    shared, = _comm("reduce_ffn1_share_wait", [_land(share1)], [results[k, n] for k in range(4) for n in SMALL + late])
    grads.update(red1.named(shared))
    adamw(FFN1, share1.token)
    results.update({(k, n): _shard_form(n, d[n]) for k, d in enumerate((grads, delta, new_m, new_v)) for n in FFN1})
    return (loss_sum, grad_x[None], *[results[k, n] for k in range(4) for n in WEIGHTS])
```

```python
from typing import Callable, NamedTuple, Optional

import jax
import jax.numpy as jnp
from jax import lax
from jax.experimental import pallas as pl
from jax.experimental.pallas import tpu as pltpu

F32 = jnp.float32
BF16 = jnp.bfloat16
RMS_EPS = 1e-6
POOL_WINDOWS = (2, 4, 8, 16)
POOL_GROUP = 128
POOL_HALO = 16
HEAD_DIM = 64
GQA_GROUP = 8
N_KV_HEADS = 2
ATTN_BLOCK = 128
SCALE = HEAD_DIM ** -0.5
NEG = -1e30
N_SHARDS = 4
ADAM_LR, ADAM_B1, ADAM_B2, ADAM_EPS, ADAM_WD, ADAM_STEP = 0.001, 0.9, 0.999, 1e-08, 0.01, 10
VMEM_LIMIT = 56 * 1024 * 1024
MESH = pl.DeviceIdType.MESH
SEG_POOL, SEG_Q, SEG_KV, SEG_GATE = (0, 512), (512, 1024), (1536, 256), (1792, 2048)
SEGMENTS = (SEG_POOL, SEG_Q, SEG_KV, SEG_GATE)


def _params(**kw):
    return pltpu.CompilerParams(vmem_limit_bytes=VMEM_LIMIT, **kw)


def _dot(a, b):
    return jnp.dot(a, b, preferred_element_type=F32)


def _dot_nt(a, b):
    return lax.dot_general(a, b, (((1,), (1,)), ((), ())), preferred_element_type=F32)


def _dot_tn(a, b):
    return lax.dot_general(a, b, (((0,), (0,)), ((), ())), preferred_element_type=F32)


def _rinv(x):
    return lax.rsqrt(jnp.mean(x * x, axis=-1, keepdims=True) + RMS_EPS)


def _rms_bwd(dn, x, g):
    r = _rinv(x)
    xr = x * r
    z = dn * g
    dx = r * (z - xr * jnp.mean(z * xr, axis=-1, keepdims=True))
    return dx, dn * xr


def _acc(ref, val, first):
    @pl.when(first)
    def _():
        ref[...] = val

    @pl.when(jnp.logical_not(first))
    def _():
        ref[...] += val


TOKEN_SPEC = pl.BlockSpec(memory_space=pl.ANY)
F_HALF = 1408


def _resident(w):
    return pl.BlockSpec(w.shape, lambda i: (0, 0), pipeline_mode=pl.Buffered(1))


def _swiglu(a, b):
    sig = jax.nn.sigmoid(a)
    q = a * sig
    return b * (sig + q * (1.0 - sig)), q, q * b


def _ffn_up(h, gnorm, wgt, wut, name, tm=512):
    S, D = h.shape
    F = wgt.shape[0]

    def body(h_ref, g_ref, wg_ref, wu_ref, n_ref, p_ref, q_ref, s_ref):
        x = h_ref[...]
        n = (x * _rinv(x) * g_ref[...]).astype(BF16)
        n_ref[...] = n
        for lo in range(0, F, F_HALF):
            cols = slice(lo, lo + F_HALF)
            p, q, s = _swiglu(_dot_nt(n, wg_ref[cols, :]), _dot_nt(n, wu_ref[cols, :]))
            p_ref[:, cols] = p.astype(BF16)
            q_ref[:, cols] = q.astype(BF16)
            s_ref[:, cols] = s.astype(BF16)

    act = pl.BlockSpec((tm, F), lambda i: (i, 0))
    hidden = jax.ShapeDtypeStruct((S, F), BF16)
    return pl.pallas_call(
        body, name=name, grid=(S // tm,),
        in_specs=[pl.BlockSpec((tm, D), lambda i: (i, 0)), pl.BlockSpec((1, D), lambda i: (0, 0)), _resident(wgt), _resident(wut)],
        out_specs=[pl.BlockSpec((tm, D), lambda i: (i, 0)), act, act, act],
        out_shape=[jax.ShapeDtypeStruct((S, D), BF16), hidden, hidden, hidden],
        compiler_params=_params(dimension_semantics=("arbitrary",)),
    )(h, gnorm, wgt, wut)


def _ffn_gate(h, gnorm, wgt, name, tm=512):
    S, D = h.shape
    F = wgt.shape[0]

    def body(h_ref, g_ref, wg_ref, n_ref, a_ref):
        x = h_ref[...]
        n = (x * _rinv(x) * g_ref[...]).astype(BF16)
        n_ref[...] = n
        for lo in range(0, F, F_HALF):
            a_ref[:, lo:lo + F_HALF] = _dot_nt(n, wg_ref[lo:lo + F_HALF, :]).astype(BF16)

    row = pl.BlockSpec((tm, D), lambda i: (i, 0))
    return pl.pallas_call(
        body, name=name, grid=(S // tm,), in_specs=[row, pl.BlockSpec((1, D), lambda i: (0, 0)), _resident(wgt)],
        out_specs=[row, pl.BlockSpec((tm, F), lambda i: (i, 0))],
        out_shape=[jax.ShapeDtypeStruct((S, D), BF16), jax.ShapeDtypeStruct((S, F), BF16)],
        compiler_params=_params(dimension_semantics=("arbitrary",)),
    )(h, gnorm, wgt)


def _ffn_hidden(n, a, wut, name, tm=512):
    S, D = n.shape
    F = wut.shape[0]

    def body(n_ref, a_ref, wu_ref, p_ref, q_ref, s_ref):
        n_t = n_ref[...]
        for lo in range(0, F, F_HALF):
            cols = slice(lo, lo + F_HALF)
            p, q, s = _swiglu(a_ref[:, cols].astype(F32), _dot_nt(n_t, wu_ref[cols, :]))
            p_ref[:, cols] = p.astype(BF16)
            q_ref[:, cols] = q.astype(BF16)
            s_ref[:, cols] = s.astype(BF16)

    act = pl.BlockSpec((tm, F), lambda i: (i, 0))
    hidden = jax.ShapeDtypeStruct((S, F), BF16)
    return pl.pallas_call(
        body, name=name, grid=(S // tm,), in_specs=[pl.BlockSpec((tm, D), lambda i: (i, 0)), act, _resident(wut)],
        out_specs=[act, act, act], out_shape=[hidden, hidden, hidden],
        compiler_params=_params(dimension_semantics=("arbitrary",)),
    )(n, a, wut)


def _ffn_down(h, s, wd, name, tm=1024):
    S, D = h.shape
    F = wd.shape[0]

    def body(h_ref, s_ref, wd_ref, o_ref):
        o_ref[...] = h_ref[...] + 0.5 * _dot(s_ref[...], wd_ref[...])

    row = pl.BlockSpec((tm, D), lambda i: (i, 0))
    return pl.pallas_call(
        body, name=name, grid=(S // tm,), in_specs=[row, pl.BlockSpec((tm, F), lambda i: (i, 0)), _resident(wd)],
        out_specs=row, out_shape=jax.ShapeDtypeStruct((S, D), F32),
        compiler_params=_params(dimension_semantics=("arbitrary",)),
    )(h, s, wd)


def _ffn_down_loss(h, s, wd, target, name, tm=512):
    S, D = h.shape
    F = wd.shape[0]

    def body(h_ref, s_ref, wd_ref, t_ref, dy_ref, dyh_ref, loss_ref):
        e = h_ref[...] + 0.5 * _dot(s_ref[...], wd_ref[...]) - t_ref[...]
        dy = e * (1.0 / D)
        dy_ref[...] = dy
        dyh_ref[...] = (0.5 * dy).astype(BF16)
        tot = jnp.sum(jnp.sum(e * e, axis=1, keepdims=True), axis=0, keepdims=True) * (0.5 / D)
        _acc(loss_ref, jnp.broadcast_to(tot, loss_ref.shape), pl.program_id(0) == 0)

    row = pl.BlockSpec((tm, D), lambda i: (i, 0))
    return pl.pallas_call(
        body, name=name, grid=(S // tm,), in_specs=[row, pl.BlockSpec((tm, F), lambda i: (i, 0)), _resident(wd), row],
        out_specs=[row, row, pl.BlockSpec((8, 128), lambda i: (0, 0))],
        out_shape=[jax.ShapeDtypeStruct((S, D), F32), jax.ShapeDtypeStruct((S, D), BF16), jax.ShapeDtypeStruct((8, 128), F32)],
        compiler_params=_params(dimension_semantics=("arbitrary",)),
    )(h, s, wd, target)


def _ffn_bwd_h(dyh, p, q, wd, token, name, tm=512):
    S, D = dyh.shape
    F = wd.shape[0]

    def body(dyh_ref, p_ref, q_ref, wd_ref, _, da_ref, db_ref):
        dyh_t = dyh_ref[...]
        for lo in range(0, F, F_HALF):
            cols = slice(lo, lo + F_HALF)
            ds = _dot_nt(dyh_t, wd_ref[cols, :])
            da_ref[:, cols] = (ds * p_ref[:, cols].astype(F32)).astype(BF16)
            db_ref[:, cols] = (ds * q_ref[:, cols].astype(F32)).astype(BF16)

    act = pl.BlockSpec((tm, F), lambda i: (i, 0))
    hidden = jax.ShapeDtypeStruct((S, F), BF16)
    return pl.pallas_call(
        body, name=name, grid=(S // tm,),
        in_specs=[pl.BlockSpec((tm, D), lambda i: (i, 0)), act, act, _resident(wd), TOKEN_SPEC],
        out_specs=[act, act], out_shape=[hidden, hidden],
        compiler_params=_params(dimension_semantics=("arbitrary",)),
    )(dyh, p, q, wd, token)


def _xty(x, y, name, tk=2048, tf=1408, after=()):
    S, F = x.shape
    D = y.shape[1]

    def body(x_ref, y_ref, *rest):
        _acc(rest[-1], _dot_tn(x_ref[...], y_ref[...]), pl.program_id(1) == 0)

    return pl.pallas_call(
        body, name=name, grid=(F // tf, S // tk),
        in_specs=[pl.BlockSpec((tk, tf), lambda j, k: (k, j)), pl.BlockSpec((tk, D), lambda j, k: (k, 0))]
        + [TOKEN_SPEC] * len(after),
        out_specs=pl.BlockSpec((tf, D), lambda j, k: (j, 0)), out_shape=jax.ShapeDtypeStruct((F, D), F32),
        compiler_params=_params(dimension_semantics=("arbitrary", "arbitrary")),
    )(x, y, *after)


def _ffn_bwd_x(da, db, wgt, wut, x, dh, gnorm, name, tm=512):
    S, D = x.shape
    F = wgt.shape[0]

    def body(da_ref, db_ref, wg_ref, wu_ref, x_ref, dh_ref, g_ref, dx_ref, dg_ref):
        dn = _dot(da_ref[...], wg_ref[...]) + _dot(db_ref[...], wu_ref[...])
        dx, dg_rows = _rms_bwd(dn, x_ref[...], g_ref[...])
        dx_ref[...] = dh_ref[...] + dx
        _acc(dg_ref, jnp.sum(dg_rows, axis=0, keepdims=True), pl.program_id(0) == 0)

    row = pl.BlockSpec((tm, D), lambda i: (i, 0))
    act = pl.BlockSpec((tm, F), lambda i: (i, 0))
    vec = pl.BlockSpec((1, D), lambda i: (0, 0))
    return pl.pallas_call(
        body, name=name, grid=(S // tm,), in_specs=[act, act, _resident(wgt), _resident(wut), row, row, vec],
        out_specs=[row, vec],
        out_shape=[jax.ShapeDtypeStruct((S, D), F32), jax.ShapeDtypeStruct((1, D), F32)],
        compiler_params=_params(dimension_semantics=("arbitrary",)),
    )(da, db, wgt, wut, x, dh, gnorm)


def _mix_proj(h, gnorm, wint, tm=512):
    S, D = h.shape
    nt = S // tm

    def body(h_ref, g_ref, w_ref, u_ref, xp_ref, q_ref, kv_ref, gl_ref):
        x = h_ref[...]
        u = (x * _rinv(x) * g_ref[...]).astype(BF16)
        u_ref[...] = u
        for (off, width), ref in zip(SEGMENTS, (xp_ref, q_ref, kv_ref, gl_ref)):
            ref[...] = _dot_nt(u, w_ref[off:off + width, :]).astype(ref.dtype)

    def row(width):
        return pl.BlockSpec((tm, width), lambda i: (i, 0))

    return pl.pallas_call(
        body, name="mix_proj", grid=(nt,),
        in_specs=[row(D), pl.BlockSpec((1, D), lambda i: (0, 0)), pl.BlockSpec(wint.shape, lambda i: (0, 0))],
        out_specs=[row(D), row(SEG_POOL[1]), row(SEG_Q[1]), row(SEG_KV[1]), row(SEG_GATE[1])],
        out_shape=[jax.ShapeDtypeStruct((S, D), BF16), jax.ShapeDtypeStruct((S, SEG_POOL[1]), F32),
                   jax.ShapeDtypeStruct((S, SEG_Q[1]), BF16), jax.ShapeDtypeStruct((S, SEG_KV[1]), BF16),
                   jax.ShapeDtypeStruct((S, SEG_GATE[1]), BF16)],
        compiler_params=_params(dimension_semantics=("arbitrary",)),
    )(h, gnorm, wint)


def _stack_heads(x, g):
    return jnp.concatenate([x[:, (GQA_GROUP * g + hh) * HEAD_DIM:(GQA_GROUP * g + hh + 1) * HEAD_DIM]
                            for hh in range(GQA_GROUP)], axis=0)


def _unstack_heads(ref, val, g):
    for hh in range(GQA_GROUP):
        lo = (GQA_GROUP * g + hh) * HEAD_DIM
        ref[:, lo:lo + HEAD_DIM] = val[hh * ATTN_BLOCK:(hh + 1) * ATTN_BLOCK, :]


def _rowsum(xb, width):
    return _dot(xb, jnp.ones((xb.shape[1], width), BF16))


def _rinv_lanes(x):
    return lax.rsqrt(_rowsum((x * x).astype(BF16), x.shape[1]) * (1.0 / x.shape[1]) + RMS_EPS)


def _twice(x):
    return jnp.concatenate([x, x], axis=1)


def _band_bias():
    qi = (jnp.arange(GQA_GROUP * ATTN_BLOCK) % ATTN_BLOCK)[:, None]
    kj = jnp.arange(2 * ATTN_BLOCK)[None, :]
    return jnp.where(jnp.logical_and(kj > qi, kj <= qi + ATTN_BLOCK), 0.0, NEG).astype(F32)


def _attn_exp(qn, kk, sink, band, n):
    kj = lax.broadcasted_iota(jnp.int32, (1, 2 * ATTN_BLOCK), 1)
    before_sequence = jnp.where(jnp.logical_and(kj < ATTN_BLOCK, n == 0), NEG, 0.0)
    s = _dot_nt(qn, kk) + band + before_sequence
    m = jnp.maximum(jnp.broadcast_to(jnp.max(s, axis=-1, keepdims=True), sink.shape), sink)
    e = jnp.exp(s - _twice(m))
    e_s = jnp.exp(sink - m)
    e_b = e.astype(BF16)
    inv = 1.0 / (_rowsum(e_b, ATTN_BLOCK) + e_s)
    return e_b, e, e_s, inv


def _attn_blocks(n):
    cur = pl.multiple_of(n * ATTN_BLOCK, ATTN_BLOCK)
    prev = pl.multiple_of(jnp.maximum(n - 1, 0) * ATTN_BLOCK, ATTN_BLOCK)
    return cur, prev


def _kv_window(kv_ref, n):
    cur, prev = _attn_blocks(n)
    return jnp.concatenate([kv_ref[pl.ds(prev, ATTN_BLOCK), :], kv_ref[pl.ds(cur, ATTN_BLOCK), :]], axis=0).astype(F32)


def _kv_split(kv, g):
    k = kv[:, g * HEAD_DIM:(g + 1) * HEAD_DIM]
    v = kv[:, (N_KV_HEADS + g) * HEAD_DIM:(N_KV_HEADS + g + 1) * HEAD_DIM]
    return k, v


def _attn_fwd(q, kv, qw, kw, sink_rows, band):
    S, W = q.shape
    nb = S // ATTN_BLOCK

    def body(q_ref, kv_ref, qw_ref, kw_ref, sk_ref, band_ref, o_ref, o_scr):
        n = pl.program_id(0)
        qf = q_ref[...].astype(F32)
        kvw = _kv_window(kv_ref, n)
        groups = range(N_KV_HEADS)
        qs = [_stack_heads(qf, g) for g in groups]
        kvs = [_kv_split(kvw, g) for g in groups]
        rq = [_rinv_lanes(qs[g]) for g in groups]
        rk = [_rinv_lanes(kvs[g][0]) for g in groups]
        qn = [(qs[g] * rq[g] * qw_ref[...] * SCALE).astype(BF16) for g in groups]
        kk = [(kvs[g][0] * rk[g] * kw_ref[...]).astype(BF16) for g in groups]
        ex = [_attn_exp(qn[g], kk[g], sk_ref[g], band_ref[...], n) for g in groups]
        pv = [_dot(ex[g][0], kvs[g][1].astype(BF16)) for g in groups]
        for g in groups:
            _unstack_heads(o_scr, pv[g] * ex[g][3][:, :HEAD_DIM], g)
        o_ref[...] = o_scr[...].astype(BF16)

    blk = pl.BlockSpec((ATTN_BLOCK, W), lambda n: (n, 0))
    return pl.pallas_call(
        body, name="attn_fwd", grid=(nb,),
        in_specs=[blk, pl.BlockSpec(kv.shape, lambda n: (0, 0)), pl.BlockSpec((1, HEAD_DIM), lambda n: (0, 0)),
                  pl.BlockSpec((1, HEAD_DIM), lambda n: (0, 0)), pl.BlockSpec(sink_rows.shape, lambda n: (0, 0, 0)),
                  pl.BlockSpec(band.shape, lambda n: (0, 0))],
        out_specs=blk, out_shape=jax.ShapeDtypeStruct((S, W), BF16),
        scratch_shapes=[pltpu.VMEM((ATTN_BLOCK, W), F32)],
        compiler_params=_params(dimension_semantics=("arbitrary",)),
    )(q, kv, qw, kw, sink_rows, band)


def _attn_bwd(q, kv, do, qw, kw, sink_rows, band):
    S, W = q.shape
    KW = kv.shape[1]
    nb = S // ATTN_BLOCK
    chunk = 512

    def body(q_ref, kv_ref, do_ref, qw_ref, kw_ref, sk_ref, band_ref, dq_ref, dkv_ref, dqw_ref, dkw_ref, dsk_ref, dq_scr):
        n = pl.program_id(0)

        @pl.when(n == 0)
        def _():
            dkv_ref[...] = jnp.zeros_like(dkv_ref)
            dqw_ref[...] = jnp.zeros_like(dqw_ref)
            dsk_ref[...] = jnp.zeros_like(dsk_ref)

        qf = q_ref[...].astype(F32)
        dof = do_ref[...].astype(F32)
        cur, prev = _attn_blocks(n)
        kvw = _kv_window(kv_ref, n)
        qw_v = qw_ref[...]
        for g in range(N_KV_HEADS):
            qs = _stack_heads(qf, g)
            rq = _rinv_lanes(qs)
            qhat = qs * rq
            qn = (qhat * qw_v * SCALE).astype(BF16)
            k, v = _kv_split(kvw, g)
            kk = (k * _rinv_lanes(k) * kw_ref[...]).astype(BF16)
            vv = v.astype(BF16)
            dos = _stack_heads(dof, g).astype(BF16)
            _, e, e_s, inv = _attn_exp(qn, kk, sk_ref[g], band_ref[...], n)
            p = e * _twice(inv)
            dp = _dot_nt(dos, vv)
            drow = _rowsum((p * dp).astype(BF16), ATTN_BLOCK)
            ds = (p * (dp - _twice(drow))).astype(BF16)
            dsink = -(e_s * inv * drow)
            for hh in range(GQA_GROUP):
                tot = jnp.sum(dsink[hh * ATTN_BLOCK:(hh + 1) * ATTN_BLOCK, :], axis=0, keepdims=True)
                dsk_ref[g, hh:hh + 1, :] += tot
            dqn = _dot(ds, kk) * SCALE
            dkk = _dot_tn(ds, qn)
            dvv = _dot_tn(p.astype(BF16), dos)
            klo, vlo = g * HEAD_DIM, (N_KV_HEADS + g) * HEAD_DIM
            for start, rows in ((prev, slice(0, ATTN_BLOCK)), (cur, slice(ATTN_BLOCK, 2 * ATTN_BLOCK))):
                dkv_ref[pl.ds(start, ATTN_BLOCK), klo:klo + HEAD_DIM] += dkk[rows]
                dkv_ref[pl.ds(start, ATTN_BLOCK), vlo:vlo + HEAD_DIM] += dvv[rows]
            dqw_ref[...] += jnp.sum(dqn * qhat, axis=0, keepdims=True)
            z = dqn * qw_v
            dqs = rq * (z - qhat * (_rowsum((z * qhat).astype(BF16), HEAD_DIM) * (1.0 / HEAD_DIM)))
            _unstack_heads(dq_scr, dqs, g)
        dq_ref[...] = dq_scr[...].astype(BF16)

        @pl.when(n == nb - 1)
        def _():
            def one(c, dkw):
                rows = pl.ds(pl.multiple_of(c * chunk, chunk), chunk)
                for g in range(N_KV_HEADS):
                    lo = g * HEAD_DIM
                    k = kv_ref[rows, lo:lo + HEAD_DIM].astype(F32)
                    dx, dg_rows = _rms_bwd(dkv_ref[rows, lo:lo + HEAD_DIM], k, kw_ref[...])
                    dkv_ref[rows, lo:lo + HEAD_DIM] = dx
                    dkw = dkw + jnp.sum(dg_rows, axis=0, keepdims=True)
                return dkw

            dkw_ref[...] = lax.fori_loop(0, S // chunk, one, jnp.zeros((1, HEAD_DIM), F32))

    blk = pl.BlockSpec((ATTN_BLOCK, W), lambda n: (n, 0))
    whole_kv = pl.BlockSpec((S, KW), lambda n: (0, 0))
    vec = pl.BlockSpec((1, HEAD_DIM), lambda n: (0, 0))
    sk = pl.BlockSpec(sink_rows.shape, lambda n: (0, 0, 0))
    dsk = pl.BlockSpec((N_KV_HEADS, GQA_GROUP, 128), lambda n: (0, 0, 0))
    return pl.pallas_call(
        body, name="attn_bwd", grid=(nb,), in_specs=[blk, whole_kv, blk, vec, vec, sk, pl.BlockSpec(band.shape, lambda n: (0, 0))],
        out_specs=[blk, whole_kv, vec, vec, dsk],
        out_shape=[jax.ShapeDtypeStruct((S, W), BF16), jax.ShapeDtypeStruct((S, KW), F32),
                   jax.ShapeDtypeStruct((1, HEAD_DIM), F32), jax.ShapeDtypeStruct((1, HEAD_DIM), F32),
                   jax.ShapeDtypeStruct((N_KV_HEADS, GQA_GROUP, 128), F32)],
        scratch_shapes=[pltpu.VMEM((ATTN_BLOCK, W), F32)],
        compiler_params=_params(dimension_semantics=("arbitrary",)),
    )(q, kv, do, qw, kw, sink_rows, band)


def _pooled(xc, xprev, i):
    tm = xc.shape[0]
    xh = jnp.concatenate([jnp.where(i > 0, xprev, 0.0), xc], axis=0)
    t = lax.broadcasted_iota(jnp.int32, (tm, 1), 0) + i * tm
    out = []
    for gi, w in enumerate(POOL_WINDOWS):
        acc = xh[:, gi * POOL_GROUP:(gi + 1) * POOL_GROUP]
        sh = 1
        while sh < w:
            acc = acc + pltpu.roll(acc, sh, 0)
            sh *= 2
        cnt = jnp.minimum(t + 1, w).astype(F32)
        out.append(acc[POOL_HALO:, :] / cnt - xc[:, gi * POOL_GROUP:(gi + 1) * POOL_GROUP])
    return jnp.concatenate(out, axis=1)


def _pool_mix(pooled_b, pw_ref):
    return jnp.concatenate([_dot(pooled_b[:, gi * POOL_GROUP:(gi + 1) * POOL_GROUP], pw_ref[gi])
                            for gi in range(len(POOL_WINDOWS))], axis=1)


def _halo_specs(tm, width, S, after):
    per = tm // POOL_HALO
    last = S // POOL_HALO - 1
    if after:
        return pl.BlockSpec((POOL_HALO, width), lambda i: (jnp.minimum((i + 1) * per, last), 0))
    return pl.BlockSpec((POOL_HALO, width), lambda i: (jnp.maximum(i * per - 1, 0), 0))


def _mix_out(xp, attn, gl, bias, pw, pscale, wpot, wao, wo, h, tm=512):
    S, D = h.shape
    nt = S // tm
    PW = xp.shape[1]

    def body(xc_ref, xprev_ref, at_ref, gl_ref, bias_ref, pw_ref, ps_ref, wpo_ref, wao_ref, wo_ref, h_ref,
             ho_ref, bp_ref, ba_ref):
        i = pl.program_id(0)
        pooled = _pooled(xc_ref[...], xprev_ref[...], i).astype(BF16)
        ms = (_pool_mix(pooled, pw_ref) * ps_ref[...]).astype(BF16)
        bp = _dot_nt(ms, wpo_ref[...])
        ba = _dot(at_ref[...], wao_ref[...])
        bp_ref[...] = bp.astype(BF16)
        ba_ref[...] = ba.astype(BF16)
        gates = jax.nn.sigmoid(gl_ref[...].astype(F32) + bias_ref[...])
        merged = (gates[:, :D] * bp + gates[:, D:] * ba).astype(BF16)
        ho_ref[...] = h_ref[...] + _dot(merged, wo_ref[...])

    def row(width):
        return pl.BlockSpec((tm, width), lambda i: (i, 0))

    def whole(x):
        nd = x.ndim
        return pl.BlockSpec(x.shape, lambda i: (0,) * nd)

    return pl.pallas_call(
        body, name="mix_out", grid=(nt,),
        in_specs=[row(PW), _halo_specs(tm, PW, S, False), row(D), row(2 * D), whole(bias), whole(pw), whole(pscale),
                  whole(wpot), whole(wao), whole(wo), row(D)],
        out_specs=[row(D), row(D), row(D)],
        out_shape=[jax.ShapeDtypeStruct((S, D), F32), jax.ShapeDtypeStruct((S, D), BF16),
                   jax.ShapeDtypeStruct((S, D), BF16)],
        compiler_params=_params(dimension_semantics=("arbitrary",)),
    )(xp, xp, attn, gl, bias, pw, pscale, wpot, wao, wo, h)


def _mix_bwd_gate(dh, wo, bp, ba, gl, bias, tm=512):
    S, D = dh.shape
    nt = S // tm

    def body(dh_ref, wo_ref, bp_ref, ba_ref, gl_ref, bias_ref, dgl_ref, dbp_ref, dba_ref, dwo_ref, dbias_ref):
        i = pl.program_id(0)
        dhb = dh_ref[...].astype(BF16)
        dm = _dot_nt(dhb, wo_ref[...])
        gates = jax.nn.sigmoid(gl_ref[...].astype(F32) + bias_ref[...])
        gp, ga = gates[:, :D], gates[:, D:]
        bp_v = bp_ref[...].astype(F32)
        ba_v = ba_ref[...].astype(F32)
        merged = (gp * bp_v + ga * ba_v).astype(BF16)
        _acc(dwo_ref, _dot_tn(merged, dhb), i == 0)
        dbp_ref[...] = (dm * gp).astype(BF16)
        dba_ref[...] = (dm * ga).astype(BF16)
        dgl = jnp.concatenate([dm * bp_v * gp * (1.0 - gp), dm * ba_v * ga * (1.0 - ga)], axis=1)
        dgl_ref[...] = dgl.astype(BF16)
        _acc(dbias_ref, jnp.sum(dgl, axis=0, keepdims=True), i == 0)

    def row(width):
        return pl.BlockSpec((tm, width), lambda i: (i, 0))

    def whole(shape):
        return pl.BlockSpec(shape, lambda i: (0, 0))

    return pl.pallas_call(
        body, name="mix_bwd_gate", grid=(nt,),
        in_specs=[row(D), whole(wo.shape), row(D), row(D), row(2 * D), whole(bias.shape)],
        out_specs=[row(2 * D), row(D), row(D), whole((D, D)), whole((1, 2 * D))],
        out_shape=[jax.ShapeDtypeStruct((S, 2 * D), BF16), jax.ShapeDtypeStruct((S, D), BF16),
                   jax.ShapeDtypeStruct((S, D), BF16), jax.ShapeDtypeStruct((D, D), F32),
                   jax.ShapeDtypeStruct((1, 2 * D), F32)],
        compiler_params=_params(dimension_semantics=("arbitrary",)),
    )(dh, wo, bp, ba, gl, bias)


def _mix_bwd_branch(dbp, dba, attn, xp, pw, pscale, wpot, wao, tm=1024):
    S, D = dbp.shape
    nt = S // tm
    PW = xp.shape[1]
    NG = len(POOL_WINDOWS)

    def body(dbp_ref, dba_ref, at_ref, xc_ref, xprev_ref, pw_ref, ps_ref, wpo_ref, wao_ref,
             dat_ref, dpl_ref, dwao_ref, dwpo_ref, dpw_ref, dps_ref):
        i = pl.program_id(0)
        dba_v = dba_ref[...]
        dbp_v = dbp_ref[...]
        _acc(dwao_ref, _dot_tn(at_ref[...], dba_v), i == 0)
        dat_ref[...] = _dot_nt(dba_v, wao_ref[...]).astype(BF16)
        pooled = _pooled(xc_ref[...], xprev_ref[...], i).astype(BF16)
        mixed = _pool_mix(pooled, pw_ref)
        ps = ps_ref[...]
        _acc(dwpo_ref, _dot_tn(dbp_v, (mixed * ps).astype(BF16)), i == 0)
        dms = _dot(dbp_v, wpo_ref[...])
        _acc(dps_ref, jnp.sum(dms * mixed, axis=0, keepdims=True), i == 0)
        dmixed = (dms * ps).astype(BF16)
        dpooled = []
        for gi in range(NG):
            cols = slice(gi * POOL_GROUP, (gi + 1) * POOL_GROUP)
            _acc(dpw_ref.at[gi], _dot_tn(pooled[:, cols], dmixed[:, cols]), i == 0)
            dpooled.append(_dot_nt(dmixed[:, cols], pw_ref[gi]))
        dpl_ref[...] = jnp.concatenate(dpooled, axis=1)

    def row(width):
        return pl.BlockSpec((tm, width), lambda i: (i, 0))

    def whole(shape):
        nd = len(shape)
        return pl.BlockSpec(shape, lambda i: (0,) * nd)

    return pl.pallas_call(
        body, name="mix_bwd_branch", grid=(nt,),
        in_specs=[row(D), row(D), row(D), row(PW), _halo_specs(tm, PW, S, False), whole(pw.shape), whole(pscale.shape),
                  whole(wpot.shape), whole(wao.shape)],
        out_specs=[row(D), row(PW), whole((D, D)), whole((D, PW)), whole(pw.shape), whole((1, PW))],
        out_shape=[jax.ShapeDtypeStruct((S, D), BF16), jax.ShapeDtypeStruct((S, PW), F32),
                   jax.ShapeDtypeStruct((D, D), F32), jax.ShapeDtypeStruct((D, PW), F32),
                   jax.ShapeDtypeStruct(pw.shape, F32), jax.ShapeDtypeStruct((1, PW), F32)],
        compiler_params=_params(dimension_semantics=("arbitrary",)),
    )(dbp, dba, attn, xp, xp, pw, pscale, wpot, wao)


def _pool_bwd(dpooled, tm=512):
    S, PW = dpooled.shape
    nt = S // tm

    def body(dc_ref, dnext_ref, dxp_ref):
        i = pl.program_id(0)
        dc = dc_ref[...]
        dh = jnp.concatenate([dc, jnp.where(i < nt - 1, dnext_ref[...], 0.0)], axis=0)
        rows = tm + POOL_HALO
        t = lax.broadcasted_iota(jnp.int32, (rows, 1), 0) + i * tm
        out = []
        for gi, w in enumerate(POOL_WINDOWS):
            cols = slice(gi * POOL_GROUP, (gi + 1) * POOL_GROUP)
            acc = dh[:, cols] / jnp.minimum(t + 1, w).astype(F32)
            sh = 1
            while sh < w:
                acc = acc + pltpu.roll(acc, rows - sh, 0)
                sh *= 2
            out.append(acc[:tm, :] - dc[:, cols])
        dxp_ref[...] = jnp.concatenate(out, axis=1).astype(BF16)

    return pl.pallas_call(
        body, name="pool_bwd", grid=(nt,),
        in_specs=[pl.BlockSpec((tm, PW), lambda i: (i, 0)), _halo_specs(tm, PW, S, True)],
        out_specs=pl.BlockSpec((tm, PW), lambda i: (i, 0)), out_shape=jax.ShapeDtypeStruct((S, PW), BF16),
        compiler_params=_params(dimension_semantics=("arbitrary",)),
    )(dpooled, dpooled)


def _in_bwd_w(pieces, u, tm=1024):
    S, D = u.shape
    nt = S // tm
    NW = sum(width for _, width in SEGMENTS)

    def body(*refs):
        piece_refs, u_ref, dw_hbm, acc, sem = refs[:len(SEGMENTS)], refs[len(SEGMENTS)], refs[len(SEGMENTS) + 1], refs[-2], refs[-1]
        i = pl.program_id(0)
        u_t = u_ref[...]
        for (off, width), ref in zip(SEGMENTS, piece_refs):
            for lo in range(0, width, D):
                hi = min(lo + D, width)
                _acc(acc.at[off + lo:off + hi, :], _dot_tn(ref[:, lo:hi].astype(BF16), u_t), i == 0)

        @pl.when(i == nt - 1)
        def _():
            cp = pltpu.make_async_copy(acc, dw_hbm, sem)
            cp.start()
            cp.wait()

    return pl.pallas_call(
        body, name="in_bwd_w", grid=(nt,),
        in_specs=[pl.BlockSpec((tm, width), lambda i: (i, 0)) for _, width in SEGMENTS] + [pl.BlockSpec((tm, D), lambda i: (i, 0))],
        out_specs=pl.BlockSpec(memory_space=pl.ANY), out_shape=jax.ShapeDtypeStruct((NW, D), F32),
        scratch_shapes=[pltpu.VMEM((NW, D), F32), pltpu.SemaphoreType.DMA],
        compiler_params=_params(dimension_semantics=("arbitrary",)),
    )(*pieces, u)


def _in_bwd_x(pieces, wint, h, dh, gnorm, tm=512):
    S, D = h.shape
    nt = S // tm

    def body(*refs):
        piece_refs = refs[:len(SEGMENTS)]
        w_ref, h_ref, dh_ref, g_ref, dx_ref, dxh_ref, dg_ref = refs[len(SEGMENTS):]
        i = pl.program_id(0)
        du = jnp.zeros((tm, D), F32)
        for (off, width), ref in zip(SEGMENTS, piece_refs):
            du = du + _dot(ref[...].astype(BF16), w_ref[off:off + width, :])
        dx, dg_rows = _rms_bwd(du, h_ref[...], g_ref[...])
        out = dh_ref[...] + dx
        dx_ref[...] = out
        dxh_ref[...] = (0.5 * out).astype(BF16)
        _acc(dg_ref, jnp.sum(dg_rows, axis=0, keepdims=True), i == 0)

    def row(width):
        return pl.BlockSpec((tm, width), lambda i: (i, 0))

    vec = pl.BlockSpec((1, D), lambda i: (0, 0))
    return pl.pallas_call(
        body, name="in_bwd_x", grid=(nt,),
        in_specs=[row(width) for _, width in SEGMENTS] + [pl.BlockSpec(wint.shape, lambda i: (0, 0)), row(D), row(D), vec],
        out_specs=[row(D), row(D), vec],
        out_shape=[jax.ShapeDtypeStruct((S, D), F32), jax.ShapeDtypeStruct((S, D), BF16), jax.ShapeDtypeStruct((1, D), F32)],
        compiler_params=_params(dimension_semantics=("arbitrary",)),
    )(*pieces, wint, h, dh, gnorm)


def _row_tile(rows):
    for t in (512, 480, 352, 256, 128, 64, 32, 16, 8):
        if rows % t == 0:
            return t
    return rows


def _adamw_update(w, g, m, v):
    mn = ADAM_B1 * m + (1.0 - ADAM_B1) * g
    vn = ADAM_B2 * v + (1.0 - ADAM_B2) * (g * g)
    m_hat = mn / (1.0 - ADAM_B1 ** ADAM_STEP)
    v_hat = vn / (1.0 - ADAM_B2 ** ADAM_STEP)
    return -ADAM_LR * (m_hat / (jnp.sqrt(v_hat) + ADAM_EPS) + ADAM_WD * w), mn, vn


def _adamw_small(ws, gs, ms, vs):
    n = len(ws)

    def rows(a):
        return a.reshape(-1, a.shape[-1]) if a.ndim > 1 else a.reshape(1, -1)

    def body(*refs):
        for k in range(n):
            w_ref, g_ref, m_ref, v_ref, d_ref, mo_ref, vo_ref = refs[k::n]
            d_ref[...], mo_ref[...], vo_ref[...] = _adamw_update(w_ref[...], g_ref[...], m_ref[...], v_ref[...])

    vmem = pl.BlockSpec(memory_space=pltpu.VMEM)
    out = pl.pallas_call(
        body, name="adamw_small", in_specs=[vmem] * (4 * n), out_specs=[vmem] * (3 * n),
        out_shape=[jax.ShapeDtypeStruct(rows(w).shape, F32) for _ in range(3) for w in ws], compiler_params=_params(),
    )(*[rows(a) for a in ws + gs + ms + vs])
    return [[o.reshape(w.shape) for o, w in zip(out[k * n:(k + 1) * n], ws)] for k in range(3)]


def _adamw(w, g, m, v, token, name, turned=False):
    R, C = w.shape
    tr = R if turned else _row_tile(R)

    def body(w_ref, g_ref, m_ref, v_ref, _, d_ref, mo_ref, vo_ref, go_ref):
        gv = g_ref[...].T if turned else g_ref[...]
        go_ref[...] = gv
        d_ref[...], mo_ref[...], vo_ref[...] = _adamw_update(w_ref[...], gv, m_ref[...], v_ref[...])

    blk = pl.BlockSpec((tr, C), lambda i: (i, 0))
    g_blk = pl.BlockSpec((C, R), lambda i: (0, 0)) if turned else blk
    sh = jax.ShapeDtypeStruct((R, C), F32)
    return pl.pallas_call(
        body, name=name, grid=(R // tr,), in_specs=[blk, g_blk, blk, blk, TOKEN_SPEC], out_specs=[blk] * 4, out_shape=[sh] * 4,
        compiler_params=_params(dimension_semantics=("arbitrary",)),
    )(w, g, m, v, token)


def _cast_place(w, place, name, turned=False, after=()):
    R, C = w.shape[::-1] if turned else w.shape
    tr = R if turned else _row_tile(R)
    per = R // tr

    def body(p_ref, w_ref, *rest):
        rest[-1][...] = (w_ref[...].T if turned else w_ref[...]).astype(BF16)

    grid_spec = pltpu.PrefetchScalarGridSpec(
        num_scalar_prefetch=1, grid=(per,),
        in_specs=[pl.BlockSpec((C, R), lambda i, p: (0, 0)) if turned else pl.BlockSpec((tr, C), lambda i, p: (i, 0))]
        + [TOKEN_SPEC] * len(after),
        out_specs=pl.BlockSpec((tr, C), lambda i, p: (p[0] * per + i, 0)))
    return pl.pallas_call(
        body, name=name, grid_spec=grid_spec, out_shape=jax.ShapeDtypeStruct((N_SHARDS * R, C), BF16),
        compiler_params=_params(dimension_semantics=("arbitrary",)),
    )(place, w, *after)


def _parked(w, m, idx, first, last):
    return jnp.where(m == w, idx, jnp.where(m < w, first, last))


def _sum_halves(g4s, recvs, place, name):
    M = len(g4s)
    NS, R, C = g4s[0].shape
    hr = R // 2
    tr = _row_tile(hr)
    per = hr // tr

    def body(p_ref, *refs):
        m = pl.program_id(0)
        for w in range(M):
            @pl.when(m == w)
            def _(w=w):
                refs[2 * M + w][...] = (refs[w][...] + refs[M + w][...]).astype(BF16)

    def spec(w, mine):
        def index(m, s, i, p):
            row = _parked(w, m, i, 0, per - 1)
            return _parked(w, m, s, 0, NS - 1), (p[1] * per + row if mine else row), 0

        return pl.BlockSpec((1, tr, C), index)

    grid_spec = pltpu.PrefetchScalarGridSpec(
        num_scalar_prefetch=1, grid=(M, NS, per),
        in_specs=[spec(w, True) for w in range(M)] + [spec(w, False) for w in range(M)],
        out_specs=[spec(w, False) for w in range(M)])
    return pl.pallas_call(
        body, name=name, grid_spec=grid_spec, out_shape=[jax.ShapeDtypeStruct((NS, hr, C), BF16)] * M,
        compiler_params=_params(dimension_semantics=("arbitrary",) * 3),
    )(place, *g4s, *recvs)


def _sum_quarters(h4s, recv3s, place, name):
    M = len(h4s)
    NS, hr, C = h4s[0].shape
    tr = _row_tile(hr)
    per = hr // tr

    def body(p_ref, *refs):
        m = pl.program_id(0)
        for w in range(M):
            @pl.when(m == w)
            def _(w=w):
                acc = refs[w][0].astype(F32)
                for k in range(N_SHARDS - 1):
                    acc = acc + refs[M + w][k].astype(F32)
                refs[2 * M + w][...] = acc

    def row(w, m, i):
        return _parked(w, m, i, 0, per - 1)

    grid_spec = pltpu.PrefetchScalarGridSpec(
        num_scalar_prefetch=1, grid=(M, per),
        in_specs=[pl.BlockSpec((1, tr, C), lambda m, i, p, w=w: (p[0], row(w, m, i), 0)) for w in range(M)]
        + [pl.BlockSpec((N_SHARDS - 1, tr, C), lambda m, i, p, w=w: (0, row(w, m, i), 0)) for w in range(M)],
        out_specs=[pl.BlockSpec((tr, C), lambda m, i, p, w=w: (p[1] * per + row(w, m, i), 0)) for w in range(M)])
    return pl.pallas_call(
        body, name=name, grid_spec=grid_spec, out_shape=[jax.ShapeDtypeStruct((2 * hr, C), F32)] * M,
        compiler_params=_params(dimension_semantics=("arbitrary",) * 2),
    )(place, *h4s, *recv3s)


def _by_shape(arrays):
    groups = {}
    for k, a in enumerate(arrays):
        groups.setdefault(a.shape, []).append(k)
    return list(groups.values())


def _place():
    x, y, c = lax.axis_index("x"), lax.axis_index("y"), lax.axis_index("c")
    chips = [(1 - x, y), (x, 1 - y), (1 - x, 1 - y)]
    return x, y, c, chips


HBM_SPEC = pl.BlockSpec(memory_space=pltpu.HBM)
SEM_SPEC = pl.BlockSpec(memory_space=pltpu.SEMAPHORE)
DATAFLOW = pltpu.SideEffectType.DATAFLOW_SIDE_EFFECTING


def _hbm(a):
    return pltpu.with_memory_space_constraint(a, pltpu.HBM)


class InFlight(NamedTuple):
    send_sem: jax.Array
    recv_sem: jax.Array
    bufs: list
    plan: Callable
    token: jax.Array


class Leg(NamedTuple):
    bufs: list
    landing: Optional[InFlight]
    plan: Optional[Callable]
    n_copies: int


def _send(bufs, plan, n_copies):
    return Leg([_hbm(b) for b in bufs], None, plan, n_copies)


def _land(flight):
    return Leg(flight.bufs, flight, None, 0)


def _forward(flight, plan, n_copies):
    return Leg(flight.bufs, flight, plan, n_copies)


def _wait_all(plan, refs, send_ref, recv_ref):
    for k, (src, dst, dev) in enumerate(plan(refs)):
        cp = pltpu.make_async_remote_copy(src_ref=src, dst_ref=dst, send_sem=send_ref.at[k], recv_sem=recv_ref.at[k],
                                          device_id=dev, device_id_type=MESH)
        cp.wait_send()
        cp.wait_recv()


def _start_all(plan, refs, send_ref, recv_ref):
    for k, (src, dst, dev) in enumerate(plan(refs)):
        pltpu.make_async_remote_copy(src_ref=src, dst_ref=dst, send_sem=send_ref.at[k], recv_sem=recv_ref.at[k],
                                     device_id=dev, device_id_type=MESH).start()


def _comm(name, legs, after):
    after = list(after) if isinstance(after, (list, tuple)) else [after]
    ins, in_specs, out_shape, out_specs, aliases, first = [], [], [], [], {}, []
    for leg in legs:
        first.append((len(ins), len(out_shape)))
        for b in leg.bufs:
            aliases[len(ins)] = len(out_shape)
            ins.append(b)
            in_specs.append(HBM_SPEC)
            out_shape.append(pltpu.HBM(b.shape, b.dtype))
            out_specs.append(HBM_SPEC)
        if leg.landing is not None:
            ins += [leg.landing.send_sem, leg.landing.recv_sem]
            in_specs += [SEM_SPEC, SEM_SPEC]
        if leg.plan is not None:
            out_shape += [pltpu.SemaphoreType.DMA((leg.n_copies,))] * 2
            out_specs += [SEM_SPEC, SEM_SPEC]
    starts = any(leg.plan is not None for leg in legs)
    if starts:
        out_shape.append(jax.ShapeDtypeStruct((8, 128), F32))
        out_specs.append(pl.BlockSpec(memory_space=pltpu.VMEM))
    n_in = len(ins) + len(after)

    def body(*refs):
        outs = refs[n_in:]
        for leg, (i, o) in zip(legs, first):
            nb = len(leg.bufs)
            bufs = refs[i:i + nb]
            if leg.landing is not None:
                _wait_all(leg.landing.plan, bufs, refs[i + nb], refs[i + nb + 1])
            if leg.plan is not None:
                _start_all(leg.plan, bufs, outs[o + nb], outs[o + nb + 1])
        if starts:
            outs[-1][...] = jnp.zeros_like(outs[-1])

    out = pl.pallas_call(
        body, name=name, in_specs=in_specs + [TOKEN_SPEC] * len(after), out_shape=out_shape, out_specs=out_specs,
        input_output_aliases=aliases, compiler_params=pltpu.CompilerParams(has_side_effects=DATAFLOW),
    )(*ins, *after)
    results = []
    for leg, (_, o) in zip(legs, first):
        nb = len(leg.bufs)
        bufs = list(out[o:o + nb])
        results.append(bufs if leg.plan is None else InFlight(out[o + nb], out[o + nb + 1], bufs, leg.plan, out[-1]))
    return results


def _half_rows(buf, chip, core):
    hr = buf.shape[0] // (2 * N_SHARDS)
    return buf.at[pl.ds(pl.multiple_of((2 * chip + core) * hr, 16), hr)]


def _gather_ici_plan(bufs):
    x, y, c, chips = _place()
    return [(_half_rows(b, 2 * x + y, c), _half_rows(b, 2 * x + y, c), (px, py, c)) for b in bufs for px, py in chips]


def _gather_d2d_plan(bufs):
    x, y, c, chips = _place()
    return [(_half_rows(b, 2 * px + py, c), _half_rows(b, 2 * px + py, c), (x, y, 1 - c)) for b in bufs for px, py in chips]


def _swap_plan(bufs):
    x, y, c, _ = _place()
    n = len(bufs) // 2
    copies = []
    for g, land in zip(bufs[:n], bufs[n:]):
        hr = g.shape[1] // 2
        copies.append((g.at[:, pl.ds(pl.multiple_of((1 - c) * hr, 8), hr)], land, (x, y, 1 - c)))
    return copies


def _exchange_plan(bufs):
    x, y, c, chips = _place()
    n = len(bufs) // 2
    return [(h.at[2 * px + py], land.at[k], (px, py, c))
            for h, land in zip(bufs[:n], bufs[n:]) for k, (px, py) in enumerate(chips)]


def _share_plan(bufs):
    x, y, c, _ = _place()
    copies = []
    for buf in bufs:
        hr = buf.shape[0] // 2
        mine = buf.at[pl.ds(pl.multiple_of(c * hr, 8), hr)]
        copies.append((mine, mine, (x, y, 1 - c)))
    return copies


N_DEVICES = 8


def _slot_place(vec, place):
    R, C = vec.shape

    def body(p_ref, v_ref, o_ref):
        o_ref[0] = v_ref[...]

    grid_spec = pltpu.PrefetchScalarGridSpec(
        num_scalar_prefetch=1, grid=(1,), in_specs=[pl.BlockSpec((R, C), lambda i, p: (0, 0))],
        out_specs=pl.BlockSpec((1, R, C), lambda i, p: (2 * p[0] + p[1], 0, 0)))
    return pl.pallas_call(
        body, name="slot_place", grid_spec=grid_spec, out_shape=jax.ShapeDtypeStruct((N_DEVICES, R, C), F32),
        compiler_params=_params(dimension_semantics=("arbitrary",)),
    )(place, vec)


def _slots_plan(bufs):
    x, y, c, _ = _place()
    mine = bufs[0].at[4 * x + 2 * y + c]
    return [(mine, mine, (x ^ (r >> 2), y ^ ((r >> 1) & 1), c ^ (r & 1))) for r in range(1, N_DEVICES)]


def _sum_slots(slots):
    _, R, C = slots.shape

    def body(s_ref, o_ref):
        acc = s_ref[0]
        for d in range(1, N_DEVICES):
            acc = acc + s_ref[d]
        o_ref[...] = acc

    return pl.pallas_call(
        body, name="sum_slots", in_specs=[pl.BlockSpec(memory_space=pltpu.VMEM)],
        out_specs=pl.BlockSpec(memory_space=pltpu.VMEM), out_shape=jax.ShapeDtypeStruct((R, C), F32),
        compiler_params=_params(),
    )(slots)


SMALL = ("ffn1_norm", "mix_norm", "pool_w", "pool_scale", "q_norm", "k_norm", "sinks", "gate_bias", "ffn2_norm")
SMALL_COLS = 1024
FFN1 = ("ffn1_w_gate", "ffn1_w_up", "ffn1_w_down")
MIXER = ("w_in", "w_pool_out", "w_attn_out", "w_out")
FFN2 = ("ffn2_w_gate", "ffn2_w_up", "ffn2_w_down")
LARGE = FFN1 + MIXER + FFN2
TRANSPOSED = ("ffn1_w_gate", "ffn1_w_up", "w_in", "w_pool_out", "ffn2_w_gate", "ffn2_w_up")
TURNED = ("w_pool_out",)
WEIGHTS = ("ffn1_norm", "ffn1_w_gate", "ffn1_w_up", "ffn1_w_down", "mix_norm", "w_in", "pool_w", "pool_scale",
           "w_pool_out", "q_norm", "k_norm", "sinks", "w_attn_out", "gate_bias", "w_out", "ffn2_norm",
           "ffn2_w_gate", "ffn2_w_up", "ffn2_w_down")


def _pack_small(parts):
    flat = jnp.concatenate([p.reshape(-1) for p in parts])
    rows = -(-flat.shape[0] // (8 * SMALL_COLS)) * 8
    return jnp.pad(flat, (0, rows * SMALL_COLS - flat.shape[0])).reshape(rows, SMALL_COLS)


def _unpack_small(packed, like):
    flat = packed.reshape(-1)
    out, off = [], 0
    for p in like:
        out.append(flat[off:off + p.size].reshape(p.shape))
        off += p.size
    return out, flat[off]


def _tie(small, token):
    return small + token[0, 0]


class Reduction:
    def __init__(self, names, place):
        self.names, self.place = names, place

    def swap(self, grads):
        grads = [g.reshape(N_SHARDS, -1, g.shape[-1]) for g in grads]
        lands = [lax.empty((g.shape[0], g.shape[1] // 2, g.shape[2]), g.dtype) for g in grads]
        return _send(grads + lands, _swap_plan, len(grads))

    def exchange(self, swapped):
        n = len(self.names)
        halves = self._per_shape(_sum_halves, "sum_halves", swapped[:n], swapped[n:])
        lands = [lax.empty((N_SHARDS - 1,) + h.shape[1:], h.dtype) for h in halves]
        return _send(halves + lands, _exchange_plan, (N_SHARDS - 1) * n)

    def share(self, exchanged):
        n = len(self.names)
        return _send(self._per_shape(_sum_quarters, "sum_quarters", exchanged[:n], exchanged[n:]), _share_plan, n)

    def named(self, shared):
        return dict(zip(self.names, shared))

    def _per_shape(self, add, stage, mine, received):
        out = [None] * len(mine)
        for idx in _by_shape(mine):
            sums = add([mine[k] for k in idx], [received[k] for k in idx], self.place, f"{stage}_{self.names[idx[0]]}")
            for k, v in zip(idx, sums):
                out[k] = v
        return out


def _row_form(name, a):
    return a.T if name in TRANSPOSED else a


def _shard_form(name, a):
    return a if name in TURNED else _row_form(name, a)


def kernel(x, ffn1_norm, ffn1_w_gate, ffn1_w_up, ffn1_w_down, mix_norm, w_in, pool_w, pool_scale, w_pool_out, q_norm, k_norm, sinks, w_attn_out, gate_bias, w_out, ffn2_norm, ffn2_w_gate, ffn2_w_up, ffn2_w_down, loss_target, m_ffn1_norm, m_ffn1_w_gate, m_ffn1_w_up, m_ffn1_w_down, m_mix_norm, m_w_in, m_pool_w, m_pool_scale, m_w_pool_out, m_q_norm, m_k_norm, m_sinks, m_w_attn_out, m_gate_bias, m_w_out, m_ffn2_norm, m_ffn2_w_gate, m_ffn2_w_up, m_ffn2_w_down, v_ffn1_norm, v_ffn1_w_gate, v_ffn1_w_up, v_ffn1_w_down, v_mix_norm, v_w_in, v_pool_w, v_pool_scale, v_w_pool_out, v_q_norm, v_k_norm, v_sinks, v_w_attn_out, v_gate_bias, v_w_out, v_ffn2_norm, v_ffn2_w_gate, v_ffn2_w_up, v_ffn2_w_down):
    args = dict(locals())
    wts = {n: _shard_form(n, args[n]) for n in WEIGHTS}
    mom = {n: _shard_form(n, args["m_" + n]) for n in WEIGHTS}
    var = {n: _shard_form(n, args["v_" + n]) for n in WEIGHTS}
    shard = 2 * lax.axis_index("x") + lax.axis_index("y")
    place = jnp.stack([shard, lax.axis_index("c")]).astype(jnp.int32)

    xs, target = x[0], loss_target[0]
    D = xs.shape[1]
    g1 = wts["ffn1_norm"].reshape(1, D)
    gm = wts["mix_norm"].reshape(1, D)
    g2 = wts["ffn2_norm"].reshape(1, D)
    qw = wts["q_norm"].reshape(1, HEAD_DIM)
    kw = wts["k_norm"].reshape(1, HEAD_DIM)
    bias = wts["gate_bias"].reshape(1, 2 * D)
    pscale = wts["pool_scale"].reshape(1, -1)
    pw = wts["pool_w"].astype(BF16)
    sink_rows = jnp.broadcast_to(jnp.repeat(wts["sinks"], ATTN_BLOCK).reshape(N_KV_HEADS, GQA_GROUP * ATTN_BLOCK, 1),
                                 (N_KV_HEADS, GQA_GROUP * ATTN_BLOCK, 128))

    gate1, up1, down1 = FFN1[:1], FFN1[1:2], FFN1[2:]
    mix_in, mix_out = MIXER[:1], MIXER[1:]

    def over_ici(names):
        return _send([placed[n] for n in names], _gather_ici_plan, 3 * len(names))

    def to_sibling(flight):
        return _forward(flight, _gather_d2d_plan, len(flight.bufs) * 3)

    placed = {n: _cast_place(wts[n], place, "cast_" + n) for n in gate1}
    gate1_ici, = _comm("gather_gate1_ici", [over_ici(gate1)], place)
    placed.update({n: _cast_place(wts[n], place, "cast_" + n) for n in up1 + down1})
    up1_ici, down1_ici = _comm("gather_ffn1_ici", [over_ici(up1), over_ici(down1)], gate1_ici.token)
    placed.update({n: _cast_place(wts[n], place, "cast_" + n, turned=n in TURNED) for n in MIXER})
    band = _band_bias()
    shadow = [sink_rows, pw, band, down1_ici.token]
    gate1_d2d, in_ici, out_ici = _comm(
        "gather_mix_ici", [to_sibling(gate1_ici), over_ici(mix_in), over_ici(mix_out)], shadow)
    w = dict(zip(gate1, _comm("gather_gate1_wait", [_land(gate1_d2d)], gate1_d2d.token)[0]))

    n1, a1 = _ffn_gate(xs, g1, w["ffn1_w_gate"], "ffn1_gate")
    placed.update({n: _cast_place(wts[n], place, "cast_" + n, after=[a1]) for n in FFN2})
    up1_d2d, ffn2_ici, down1_d2d = _comm(
        "gather_ffn1_d2d", [to_sibling(up1_ici), over_ici(FFN2), to_sibling(down1_ici)], a1)
    w.update(zip(up1, _comm("gather_up1_wait", [_land(up1_d2d)], up1_d2d.token)[0]))
    p1, q1, s1 = _ffn_hidden(n1, a1, w["ffn1_w_up"], "ffn1_hidden")
    w.update(zip(down1, _comm("gather_down1_wait", [_land(down1_d2d)], s1)[0]))
    h1 = _ffn_down(xs, s1, w["ffn1_w_down"], "ffn1_down")
    in_d2d, out_d2d = _comm("gather_mix_d2d", [to_sibling(in_ici), to_sibling(out_ici)], h1)
    w.update(zip(mix_in, _comm("gather_in_wait", [_land(in_d2d)], in_d2d.token)[0]))
    u, xp, q, kv, gl = _mix_proj(h1, gm, w["w_in"])
    attn = _attn_fwd(q, kv, qw, kw, sink_rows, band)
    ffn2_d2d, landed = _comm("gather_ffn2_d2d", [to_sibling(ffn2_ici), _land(out_d2d)], attn)
    w.update(zip(mix_out, landed))
    h2, bp, ba = _mix_out(xp, attn, gl, bias, pw, pscale, w["w_pool_out"], w["w_attn_out"], w["w_out"], h1)
    w.update(zip(FFN2, _comm("gather_ffn2_wait", [_land(ffn2_d2d)], h2)[0]))
    n2, p2, q2, s2 = _ffn_up(h2, g2, w["ffn2_w_gate"], w["ffn2_w_up"], "ffn2_up")
    dy, dyh, loss = _ffn_down_loss(h2, s2, w["ffn2_w_down"], target, "ffn2_down_loss")

    gw, gs = {}, {}
    da, db = _ffn_bwd_h(dyh, p2, q2, w["ffn2_w_down"], ffn2_d2d.token, "ffn2_bwd_h")
    gw["ffn2_w_gate"], gw["ffn2_w_up"] = _xty(da, n2, "ffn2_dw_gate"), _xty(db, n2, "ffn2_dw_up")
    gw["ffn2_w_down"] = _xty(s2, dyh, "ffn2_dw_down")
    red2 = Reduction(FFN2, place)
    swap, = _comm("reduce_ffn2_swap", [red2.swap([gw[n] for n in FFN2])], da)
    dh2, gs["ffn2_norm"] = _ffn_bwd_x(da, db, w["ffn2_w_gate"], w["ffn2_w_up"], h2, dy, _tie(g2, swap.token), "ffn2_bwd_x")
    swapped, = _comm("reduce_ffn2_swap_wait", [_land(swap)], dh2)
    exchange, = _comm("reduce_ffn2_exchange", [red2.exchange(swapped)], place)
    dgl, dbp, dba, gw["w_out"], gs["gate_bias"] = _mix_bwd_gate(dh2, w["w_out"], bp, ba, gl, _tie(bias, exchange.token))
    dattn, dpooled, gw["w_attn_out"], gw["w_pool_out"], gs["pool_w"], gs["pool_scale"] = _mix_bwd_branch(
        dbp, dba, attn, xp, pw, pscale, w["w_pool_out"], w["w_attn_out"])
    dq, dkv, gs["q_norm"], gs["k_norm"], dsk = _attn_bwd(q, kv, dattn, qw, kw, sink_rows, band)
    gs["sinks"] = dsk[:, :, 0]
    exchanged, = _comm("reduce_ffn2_exchange_wait", [_land(exchange)], dq)
    share2, = _comm("reduce_ffn2_share", [red2.share(exchanged)], place)
    pieces = (_pool_bwd(dpooled), dq, dkv, dgl)
    gw["w_in"] = _in_bwd_w(pieces, u)
    redm = Reduction(MIXER, place)
    swap, shared = _comm("reduce_mix_swap", [redm.swap([gw[n] for n in MIXER]), _land(share2)], share2.token)
    grads = red2.named(shared)
    dh1, dh1h, gs["mix_norm"] = _in_bwd_x(pieces, w["w_in"], h1, dh2, _tie(gm, swap.token))
    red1, swaps = Reduction(FFN1, place), {}
    dw_down = _xty(s1, dh1h, "ffn1_dw_down")
    swaps["ffn1_w_down"], swapped = _comm("reduce_mix_swap_wait", [red1.swap([dw_down]), _land(swap)], dh1)
    exchange, = _comm("reduce_mix_exchange", [redm.exchange(swapped)], place)
    da, db = _ffn_bwd_h(dh1h, p1, q1, w["ffn1_w_down"], exchange.token, "ffn1_bwd_h")
    dw_up = _xty(db, n1, "ffn1_dw_up")
    swaps["ffn1_w_up"], = _comm("reduce_ffn1_swap_up", [red1.swap([dw_up])], place)
    dw_gate = _xty(da, n1, "ffn1_dw_gate", after=[swaps["ffn1_w_up"].token])
    swaps["ffn1_w_gate"], exchanged = _comm("reduce_mix_exchange_wait", [red1.swap([dw_gate]), _land(exchange)], place)
    sharem, *landed = _comm("reduce_mix_share", [redm.share(exchanged)] + [_land(swaps[n]) for n in FFN1], place)
    exchange, = _comm("reduce_ffn1_exchange", [red1.exchange([l[0] for l in landed] + [l[1] for l in landed])], place)
    grad_x, gs["ffn1_norm"] = _ffn_bwd_x(da, db, w["ffn1_w_gate"], w["ffn1_w_up"], xs, dh1, _tie(g1, exchange.token), "ffn1_bwd_x")

    small_parts = [gs[n] for n in SMALL] + [loss[0, 0].reshape(1)]
    slots, shared = _comm(
        "gather_small", [_send([_slot_place(_pack_small(small_parts), place)], _slots_plan, N_DEVICES - 1), _land(sharem)],
        grad_x)
    grads.update(redm.named(shared))
    delta, new_m, new_v = {}, {}, {}

    def adamw(names, token):
        for n in names:
            delta[n], new_m[n], new_v[n], grads[n] = _adamw(wts[n], grads[n], mom[n], var[n], token, "adamw_" + n, n in TURNED)

    early, late = FFN2 + MIXER[:-1], MIXER[-1:]
    adamw(early, slots.token)
    exchanged, = _comm("reduce_ffn1_exchange_wait", [_land(exchange)], [delta[n] for n in early])
    share1, landed = _comm("reduce_ffn1_share", [red1.share(exchanged), _land(slots)], place)
    summed = _sum_slots(landed[0])
    small_grads, loss_sum = _unpack_small(summed, [wts[n] for n in SMALL])
    grads.update(dict(zip(SMALL, small_grads)))
    for out, vals in zip((delta, new_m, new_v), _adamw_small(*[[d[n] for n in SMALL] for d in (wts, grads, mom, var)])):
        out.update(dict(zip(SMALL, vals)))
    adamw(late, share1.token)
    results = {(k, n): _shard_form(n, d[n]) for k, d in enumerate((grads, delta, new_m, new_v)) for n in SMALL + FFN2 + MIXER}
    shared, = _comm("reduce_ffn1_share_wait", [_land(share1)], [results[k, n] for k in range(4) for n in SMALL + late])
    grads.update(red1.named(shared))
    adamw(FFN1, share1.token)
    results.update({(k, n): _shard_form(n, d[n]) for k, d in enumerate((grads, delta, new_m, new_v)) for n in FFN1})
    return (loss_sum, grad_x[None], *[results[k, n] for k in range(4) for n in WEIGHTS])
```

```python
from typing import Callable, NamedTuple, Optional

import jax
import jax.numpy as jnp
from jax import lax
from jax.experimental import pallas as pl
from jax.experimental.pallas import tpu as pltpu

F32 = jnp.float32
BF16 = jnp.bfloat16
RMS_EPS = 1e-6
POOL_WINDOWS = (2, 4, 8, 16)
POOL_GROUP = 128
POOL_HALO = 16
HEAD_DIM = 64
GQA_GROUP = 8
N_KV_HEADS = 2
ATTN_BLOCK = 128
SCALE = HEAD_DIM ** -0.5
NEG = -1e30
N_SHARDS = 4
ADAM_LR, ADAM_B1, ADAM_B2, ADAM_EPS, ADAM_WD, ADAM_STEP = 0.001, 0.9, 0.999, 1e-08, 0.01, 10
VMEM_LIMIT = 56 * 1024 * 1024
MESH = pl.DeviceIdType.MESH
SEG_POOL, SEG_Q, SEG_KV, SEG_GATE = (0, 512), (512, 1024), (1536, 256), (1792, 2048)
SEGMENTS = (SEG_POOL, SEG_Q, SEG_KV, SEG_GATE)


def _params(**kw):
    return pltpu.CompilerParams(vmem_limit_bytes=VMEM_LIMIT, **kw)


def _dot(a, b):
    return jnp.dot(a, b, preferred_element_type=F32)


def _dot_nt(a, b):
    return lax.dot_general(a, b, (((1,), (1,)), ((), ())), preferred_element_type=F32)


def _dot_tn(a, b):
    return lax.dot_general(a, b, (((0,), (0,)), ((), ())), preferred_element_type=F32)


def _rinv(x):
    return lax.rsqrt(jnp.mean(x * x, axis=-1, keepdims=True) + RMS_EPS)


def _rms_bwd(dn, x, g):
    r = _rinv(x)
    xr = x * r
    z = dn * g
    dx = r * (z - xr * jnp.mean(z * xr, axis=-1, keepdims=True))
    return dx, dn * xr


def _acc(ref, val, first):
    @pl.when(first)
    def _():
        ref[...] = val

    @pl.when(jnp.logical_not(first))
    def _():
        ref[...] += val


TOKEN_SPEC = pl.BlockSpec(memory_space=pl.ANY)
F_HALF = 1408


def _resident(w):
    return pl.BlockSpec(w.shape, lambda i: (0, 0), pipeline_mode=pl.Buffered(1))


def _swiglu(a, b):
    sig = jax.nn.sigmoid(a)
    q = a * sig
    return b * (sig + q * (1.0 - sig)), q, q * b


def _ffn_up(h, gnorm, wgt, wut, name, tm=512):
    S, D = h.shape
    F = wgt.shape[0]

    def body(h_ref, g_ref, wg_ref, wu_ref, n_ref, p_ref, q_ref, s_ref):
        x = h_ref[...]
        n = (x * _rinv(x) * g_ref[...]).astype(BF16)
        n_ref[...] = n
        for lo in range(0, F, F_HALF):
            cols = slice(lo, lo + F_HALF)
            p, q, s = _swiglu(_dot_nt(n, wg_ref[cols, :]), _dot_nt(n, wu_ref[cols, :]))
            p_ref[:, cols] = p.astype(BF16)
            q_ref[:, cols] = q.astype(BF16)
            s_ref[:, cols] = s.astype(BF16)

    act = pl.BlockSpec((tm, F), lambda i: (i, 0))
    hidden = jax.ShapeDtypeStruct((S, F), BF16)
    return pl.pallas_call(
        body, name=name, grid=(S // tm,),
        in_specs=[pl.BlockSpec((tm, D), lambda i: (i, 0)), pl.BlockSpec((1, D), lambda i: (0, 0)), _resident(wgt), _resident(wut)],
        out_specs=[pl.BlockSpec((tm, D), lambda i: (i, 0)), act, act, act],
        out_shape=[jax.ShapeDtypeStruct((S, D), BF16), hidden, hidden, hidden],
        compiler_params=_params(dimension_semantics=("arbitrary",)),
    )(h, gnorm, wgt, wut)


def _ffn_gate(h, gnorm, wgt, name, tm=512):
    S, D = h.shape
    F = wgt.shape[0]

    def body(h_ref, g_ref, wg_ref, n_ref, a_ref):
        x = h_ref[...]
        n = (x * _rinv(x) * g_ref[...]).astype(BF16)
        n_ref[...] = n
        for lo in range(0, F, F_HALF):
            a_ref[:, lo:lo + F_HALF] = _dot_nt(n, wg_ref[lo:lo + F_HALF, :]).astype(BF16)

    row = pl.BlockSpec((tm, D), lambda i: (i, 0))
    return pl.pallas_call(
        body, name=name, grid=(S // tm,), in_specs=[row, pl.BlockSpec((1, D), lambda i: (0, 0)), _resident(wgt)],
        out_specs=[row, pl.BlockSpec((tm, F), lambda i: (i, 0))],
        out_shape=[jax.ShapeDtypeStruct((S, D), BF16), jax.ShapeDtypeStruct((S, F), BF16)],
        compiler_params=_params(dimension_semantics=("arbitrary",)),
    )(h, gnorm, wgt)


def _ffn_hidden(n, a, wut, name, tm=512):
    S, D = n.shape
    F = wut.shape[0]

    def body(n_ref, a_ref, wu_ref, p_ref, q_ref, s_ref):
        n_t = n_ref[...]
        for lo in range(0, F, F_HALF):
            cols = slice(lo, lo + F_HALF)
            p, q, s = _swiglu(a_ref[:, cols].astype(F32), _dot_nt(n_t, wu_ref[cols, :]))
            p_ref[:, cols] = p.astype(BF16)
            q_ref[:, cols] = q.astype(BF16)
            s_ref[:, cols] = s.astype(BF16)

    act = pl.BlockSpec((tm, F), lambda i: (i, 0))
    hidden = jax.ShapeDtypeStruct((S, F), BF16)
    return pl.pallas_call(
        body, name=name, grid=(S // tm,), in_specs=[pl.BlockSpec((tm, D), lambda i: (i, 0)), act, _resident(wut)],
        out_specs=[act, act, act], out_shape=[hidden, hidden, hidden],
        compiler_params=_params(dimension_semantics=("arbitrary",)),
    )(n, a, wut)


def _ffn_down(h, s, wd, name, tm=1024):
    S, D = h.shape
    F = wd.shape[0]

    def body(h_ref, s_ref, wd_ref, o_ref):
        o_ref[...] = h_ref[...] + 0.5 * _dot(s_ref[...], wd_ref[...])

    row = pl.BlockSpec((tm, D), lambda i: (i, 0))
    return pl.pallas_call(
        body, name=name, grid=(S // tm,), in_specs=[row, pl.BlockSpec((tm, F), lambda i: (i, 0)), _resident(wd)],
        out_specs=row, out_shape=jax.ShapeDtypeStruct((S, D), F32),
        compiler_params=_params(dimension_semantics=("arbitrary",)),
    )(h, s, wd)


def _ffn_down_loss(h, s, wd, target, name, tm=512):
    S, D = h.shape
    F = wd.shape[0]

    def body(h_ref, s_ref, wd_ref, t_ref, dy_ref, dyh_ref, loss_ref):
        e = h_ref[...] + 0.5 * _dot(s_ref[...], wd_ref[...]) - t_ref[...]
        dy = e * (1.0 / D)
        dy_ref[...] = dy
        dyh_ref[...] = (0.5 * dy).astype(BF16)
        tot = jnp.sum(jnp.sum(e * e, axis=1, keepdims=True), axis=0, keepdims=True) * (0.5 / D)
        _acc(loss_ref, jnp.broadcast_to(tot, loss_ref.shape), pl.program_id(0) == 0)

    row = pl.BlockSpec((tm, D), lambda i: (i, 0))
    return pl.pallas_call(
        body, name=name, grid=(S // tm,), in_specs=[row, pl.BlockSpec((tm, F), lambda i: (i, 0)), _resident(wd), row],
        out_specs=[row, row, pl.BlockSpec((8, 128), lambda i: (0, 0))],
        out_shape=[jax.ShapeDtypeStruct((S, D), F32), jax.ShapeDtypeStruct((S, D), BF16), jax.ShapeDtypeStruct((8, 128), F32)],
        compiler_params=_params(dimension_semantics=("arbitrary",)),
    )(h, s, wd, target)


def _ffn_bwd_h(dyh, p, q, wd, token, name, tm=512):
    S, D = dyh.shape
    F = wd.shape[0]

    def body(dyh_ref, p_ref, q_ref, wd_ref, _, da_ref, db_ref):
        dyh_t = dyh_ref[...]
        for lo in range(0, F, F_HALF):
            cols = slice(lo, lo + F_HALF)
            ds = _dot_nt(dyh_t, wd_ref[cols, :])
            da_ref[:, cols] = (ds * p_ref[:, cols].astype(F32)).astype(BF16)
            db_ref[:, cols] = (ds * q_ref[:, cols].astype(F32)).astype(BF16)

    act = pl.BlockSpec((tm, F), lambda i: (i, 0))
    hidden = jax.ShapeDtypeStruct((S, F), BF16)
    return pl.pallas_call(
        body, name=name, grid=(S // tm,),
        in_specs=[pl.BlockSpec((tm, D), lambda i: (i, 0)), act, act, _resident(wd), TOKEN_SPEC],
        out_specs=[act, act], out_shape=[hidden, hidden],
        compiler_params=_params(dimension_semantics=("arbitrary",)),
    )(dyh, p, q, wd, token)


def _xty(x, y, name, tk=2048, tf=1408, after=()):
    S, F = x.shape
    D = y.shape[1]

    def body(x_ref, y_ref, *rest):
        _acc(rest[-1], _dot_tn(x_ref[...], y_ref[...]), pl.program_id(1) == 0)

    return pl.pallas_call(
        body, name=name, grid=(F // tf, S // tk),
        in_specs=[pl.BlockSpec((tk, tf), lambda j, k: (k, j)), pl.BlockSpec((tk, D), lambda j, k: (k, 0))]
        + [TOKEN_SPEC] * len(after),
        out_specs=pl.BlockSpec((tf, D), lambda j, k: (j, 0)), out_shape=jax.ShapeDtypeStruct((F, D), F32),
        compiler_params=_params(dimension_semantics=("arbitrary", "arbitrary")),
    )(x, y, *after)


def _ffn_bwd_x(da, db, wgt, wut, x, dh, gnorm, name, tm=512):
    S, D = x.shape
    F = wgt.shape[0]

    def body(da_ref, db_ref, wg_ref, wu_ref, x_ref, dh_ref, g_ref, dx_ref, dg_ref):
        dn = _dot(da_ref[...], wg_ref[...]) + _dot(db_ref[...], wu_ref[...])
        dx, dg_rows = _rms_bwd(dn, x_ref[...], g_ref[...])
        dx_ref[...] = dh_ref[...] + dx
        _acc(dg_ref, jnp.sum(dg_rows, axis=0, keepdims=True), pl.program_id(0) == 0)

    row = pl.BlockSpec((tm, D), lambda i: (i, 0))
    act = pl.BlockSpec((tm, F), lambda i: (i, 0))
    vec = pl.BlockSpec((1, D), lambda i: (0, 0))
    return pl.pallas_call(
        body, name=name, grid=(S // tm,), in_specs=[act, act, _resident(wgt), _resident(wut), row, row, vec],
        out_specs=[row, vec],
        out_shape=[jax.ShapeDtypeStruct((S, D), F32), jax.ShapeDtypeStruct((1, D), F32)],
        compiler_params=_params(dimension_semantics=("arbitrary",)),
    )(da, db, wgt, wut, x, dh, gnorm)


def _mix_proj(h, gnorm, wint, tm=512):
    S, D = h.shape
    nt = S // tm

    def body(h_ref, g_ref, w_ref, u_ref, xp_ref, q_ref, kv_ref, gl_ref):
        x = h_ref[...]
        u = (x * _rinv(x) * g_ref[...]).astype(BF16)
        u_ref[...] = u
        for (off, width), ref in zip(SEGMENTS, (xp_ref, q_ref, kv_ref, gl_ref)):
            ref[...] = _dot_nt(u, w_ref[off:off + width, :]).astype(ref.dtype)

    def row(width):
        return pl.BlockSpec((tm, width), lambda i: (i, 0))

    return pl.pallas_call(
        body, name="mix_proj", grid=(nt,),
        in_specs=[row(D), pl.BlockSpec((1, D), lambda i: (0, 0)), pl.BlockSpec(wint.shape, lambda i: (0, 0))],
        out_specs=[row(D), row(SEG_POOL[1]), row(SEG_Q[1]), row(SEG_KV[1]), row(SEG_GATE[1])],
        out_shape=[jax.ShapeDtypeStruct((S, D), BF16), jax.ShapeDtypeStruct((S, SEG_POOL[1]), F32),
                   jax.ShapeDtypeStruct((S, SEG_Q[1]), BF16), jax.ShapeDtypeStruct((S, SEG_KV[1]), BF16),
                   jax.ShapeDtypeStruct((S, SEG_GATE[1]), BF16)],
        compiler_params=_params(dimension_semantics=("arbitrary",)),
    )(h, gnorm, wint)


def _stack_heads(x, g):
    return jnp.concatenate([x[:, (GQA_GROUP * g + hh) * HEAD_DIM:(GQA_GROUP * g + hh + 1) * HEAD_DIM]
                            for hh in range(GQA_GROUP)], axis=0)


def _unstack_heads(ref, val, g):
    for hh in range(GQA_GROUP):
        lo = (GQA_GROUP * g + hh) * HEAD_DIM
        ref[:, lo:lo + HEAD_DIM] = val[hh * ATTN_BLOCK:(hh + 1) * ATTN_BLOCK, :]


def _rowsum(xb, width):
    return _dot(xb, jnp.ones((xb.shape[1], width), BF16))


def _rinv_lanes(x):
    return lax.rsqrt(_rowsum((x * x).astype(BF16), x.shape[1]) * (1.0 / x.shape[1]) + RMS_EPS)


def _twice(x):
    return jnp.concatenate([x, x], axis=1)


def _band_bias():
    qi = (jnp.arange(GQA_GROUP * ATTN_BLOCK) % ATTN_BLOCK)[:, None]
    kj = jnp.arange(2 * ATTN_BLOCK)[None, :]
    return jnp.where(jnp.logical_and(kj > qi, kj <= qi + ATTN_BLOCK), 0.0, NEG).astype(F32)


def _attn_exp(qn, kk, sink, band, n):
    kj = lax.broadcasted_iota(jnp.int32, (1, 2 * ATTN_BLOCK), 1)
    before_sequence = jnp.where(jnp.logical_and(kj < ATTN_BLOCK, n == 0), NEG, 0.0)
    s = _dot_nt(qn, kk) + band + before_sequence
    m = jnp.maximum(jnp.broadcast_to(jnp.max(s, axis=-1, keepdims=True), sink.shape), sink)
    e = jnp.exp(s - _twice(m))
    e_s = jnp.exp(sink - m)
    e_b = e.astype(BF16)
    inv = 1.0 / (_rowsum(e_b, ATTN_BLOCK) + e_s)
    return e_b, e, e_s, inv


def _attn_blocks(n):
    cur = pl.multiple_of(n * ATTN_BLOCK, ATTN_BLOCK)
    prev = pl.multiple_of(jnp.maximum(n - 1, 0) * ATTN_BLOCK, ATTN_BLOCK)
    return cur, prev


def _kv_window(kv_ref, n):
    cur, prev = _attn_blocks(n)
    return jnp.concatenate([kv_ref[pl.ds(prev, ATTN_BLOCK), :], kv_ref[pl.ds(cur, ATTN_BLOCK), :]], axis=0).astype(F32)


def _kv_split(kv, g):
    k = kv[:, g * HEAD_DIM:(g + 1) * HEAD_DIM]
    v = kv[:, (N_KV_HEADS + g) * HEAD_DIM:(N_KV_HEADS + g + 1) * HEAD_DIM]
    return k, v


def _attn_fwd(q, kv, qw, kw, sink_rows, band):
    S, W = q.shape
    per = 2
    rows = per * ATTN_BLOCK

    def body(q_ref, kv_ref, qw_ref, kw_ref, sk_ref, band_ref, o_ref, o_scr):
        qf = q_ref[...].astype(F32)
        pairs = [(b, g) for b in range(per) for g in range(N_KV_HEADS)]
        ns = [per * pl.program_id(0) + b for b in range(per)]
        kvw = [_kv_window(kv_ref, n) for n in ns]
        qs = [_stack_heads(qf[b * ATTN_BLOCK:(b + 1) * ATTN_BLOCK], g) for b, g in pairs]
        kvs = [_kv_split(kvw[b], g) for b, g in pairs]
        rq = [_rinv_lanes(x) for x in qs]
        rk = [_rinv_lanes(k) for k, _ in kvs]
        qn = [(x * r * qw_ref[...] * SCALE).astype(BF16) for x, r in zip(qs, rq)]
        kk = [(k * r * kw_ref[...]).astype(BF16) for (k, _), r in zip(kvs, rk)]
        ex = [_attn_exp(qn[i], kk[i], sk_ref[g], band_ref[...], ns[b]) for i, (b, g) in enumerate(pairs)]
        pv = [_dot(ex[i][0], kvs[i][1].astype(BF16)) for i in range(len(pairs))]
        for i, (b, g) in enumerate(pairs):
            _unstack_heads(o_scr.at[b * ATTN_BLOCK:(b + 1) * ATTN_BLOCK, :], pv[i] * ex[i][3][:, :HEAD_DIM], g)
        o_ref[...] = o_scr[...].astype(BF16)

    blk = pl.BlockSpec((rows, W), lambda n: (n, 0))
    return pl.pallas_call(
        body, name="attn_fwd", grid=(S // rows,),
        in_specs=[blk, pl.BlockSpec(kv.shape, lambda n: (0, 0)), pl.BlockSpec((1, HEAD_DIM), lambda n: (0, 0)),
                  pl.BlockSpec((1, HEAD_DIM), lambda n: (0, 0)), pl.BlockSpec(sink_rows.shape, lambda n: (0, 0, 0)),
                  pl.BlockSpec(band.shape, lambda n: (0, 0))],
        out_specs=blk, out_shape=jax.ShapeDtypeStruct((S, W), BF16),
        scratch_shapes=[pltpu.VMEM((rows, W), F32)],
        compiler_params=_params(dimension_semantics=("arbitrary",)),
    )(q, kv, qw, kw, sink_rows, band)


def _attn_bwd(q, kv, do, qw, kw, sink_rows, band):
    S, W = q.shape
    KW = kv.shape[1]
    nb = S // ATTN_BLOCK
    chunk = 512

    def body(q_ref, kv_ref, do_ref, qw_ref, kw_ref, sk_ref, band_ref, dq_ref, dkv_ref, dqw_ref, dkw_ref, dsk_ref, dq_scr):
        n = pl.program_id(0)

        @pl.when(n == 0)
        def _():
            dkv_ref[...] = jnp.zeros_like(dkv_ref)
            dqw_ref[...] = jnp.zeros_like(dqw_ref)
            dsk_ref[...] = jnp.zeros_like(dsk_ref)

        qf = q_ref[...].astype(F32)
        dof = do_ref[...].astype(F32)
        cur, prev = _attn_blocks(n)
        kvw = _kv_window(kv_ref, n)
        qw_v = qw_ref[...]
        for g in range(N_KV_HEADS):
            qs = _stack_heads(qf, g)
            rq = _rinv_lanes(qs)
            qhat = qs * rq
            qn = (qhat * qw_v * SCALE).astype(BF16)
            k, v = _kv_split(kvw, g)
            kk = (k * _rinv_lanes(k) * kw_ref[...]).astype(BF16)
            vv = v.astype(BF16)
            dos = _stack_heads(dof, g).astype(BF16)
            _, e, e_s, inv = _attn_exp(qn, kk, sk_ref[g], band_ref[...], n)
            p = e * _twice(inv)
            dp = _dot_nt(dos, vv)
            drow = _rowsum((p * dp).astype(BF16), ATTN_BLOCK)
            ds = (p * (dp - _twice(drow))).astype(BF16)
            dsink = -(e_s * inv * drow)
            for hh in range(GQA_GROUP):
                tot = jnp.sum(dsink[hh * ATTN_BLOCK:(hh + 1) * ATTN_BLOCK, :], axis=0, keepdims=True)
                dsk_ref[g, hh:hh + 1, :] += tot
            dqn = _dot(ds, kk) * SCALE
            dkk = _dot_tn(ds, qn)
            dvv = _dot_tn(p.astype(BF16), dos)
            klo, vlo = g * HEAD_DIM, (N_KV_HEADS + g) * HEAD_DIM
            for start, rows in ((prev, slice(0, ATTN_BLOCK)), (cur, slice(ATTN_BLOCK, 2 * ATTN_BLOCK))):
                dkv_ref[pl.ds(start, ATTN_BLOCK), klo:klo + HEAD_DIM] += dkk[rows]
                dkv_ref[pl.ds(start, ATTN_BLOCK), vlo:vlo + HEAD_DIM] += dvv[rows]
            dqw_ref[...] += jnp.sum(dqn * qhat, axis=0, keepdims=True)
            z = dqn * qw_v
            dqs = rq * (z - qhat * (_rowsum((z * qhat).astype(BF16), HEAD_DIM) * (1.0 / HEAD_DIM)))
            _unstack_heads(dq_scr, dqs, g)
        dq_ref[...] = dq_scr[...].astype(BF16)

        @pl.when(n == nb - 1)
        def _():
            def one(c, dkw):
                rows = pl.ds(pl.multiple_of(c * chunk, chunk), chunk)
                for g in range(N_KV_HEADS):
                    lo = g * HEAD_DIM
                    k = kv_ref[rows, lo:lo + HEAD_DIM].astype(F32)
                    dx, dg_rows = _rms_bwd(dkv_ref[rows, lo:lo + HEAD_DIM], k, kw_ref[...])
                    dkv_ref[rows, lo:lo + HEAD_DIM] = dx
                    dkw = dkw + jnp.sum(dg_rows, axis=0, keepdims=True)
                return dkw

            dkw_ref[...] = lax.fori_loop(0, S // chunk, one, jnp.zeros((1, HEAD_DIM), F32))

    blk = pl.BlockSpec((ATTN_BLOCK, W), lambda n: (n, 0))
    whole_kv = pl.BlockSpec((S, KW), lambda n: (0, 0))
    vec = pl.BlockSpec((1, HEAD_DIM), lambda n: (0, 0))
    sk = pl.BlockSpec(sink_rows.shape, lambda n: (0, 0, 0))
    dsk = pl.BlockSpec((N_KV_HEADS, GQA_GROUP, 128), lambda n: (0, 0, 0))
    return pl.pallas_call(
        body, name="attn_bwd", grid=(nb,), in_specs=[blk, whole_kv, blk, vec, vec, sk, pl.BlockSpec(band.shape, lambda n: (0, 0))],
        out_specs=[blk, whole_kv, vec, vec, dsk],
        out_shape=[jax.ShapeDtypeStruct((S, W), BF16), jax.ShapeDtypeStruct((S, KW), F32),
                   jax.ShapeDtypeStruct((1, HEAD_DIM), F32), jax.ShapeDtypeStruct((1, HEAD_DIM), F32),
                   jax.ShapeDtypeStruct((N_KV_HEADS, GQA_GROUP, 128), F32)],
        scratch_shapes=[pltpu.VMEM((ATTN_BLOCK, W), F32)],
        compiler_params=_params(dimension_semantics=("arbitrary",)),
    )(q, kv, do, qw, kw, sink_rows, band)


def _pooled(xc, xprev, i):
    tm = xc.shape[0]
    xh = jnp.concatenate([jnp.where(i > 0, xprev, 0.0), xc], axis=0)
    t = lax.broadcasted_iota(jnp.int32, (tm, 1), 0) + i * tm
    out = []
    for gi, w in enumerate(POOL_WINDOWS):
        acc = xh[:, gi * POOL_GROUP:(gi + 1) * POOL_GROUP]
        sh = 1
        while sh < w:
            acc = acc + pltpu.roll(acc, sh, 0)
            sh *= 2
        cnt = jnp.minimum(t + 1, w).astype(F32)
        out.append(acc[POOL_HALO:, :] / cnt - xc[:, gi * POOL_GROUP:(gi + 1) * POOL_GROUP])
    return jnp.concatenate(out, axis=1)


def _pool_mix(pooled_b, pw_ref):
    return jnp.concatenate([_dot(pooled_b[:, gi * POOL_GROUP:(gi + 1) * POOL_GROUP], pw_ref[gi])
                            for gi in range(len(POOL_WINDOWS))], axis=1)


def _halo_specs(tm, width, S, after):
    per = tm // POOL_HALO
    last = S // POOL_HALO - 1
    if after:
        return pl.BlockSpec((POOL_HALO, width), lambda i: (jnp.minimum((i + 1) * per, last), 0))
    return pl.BlockSpec((POOL_HALO, width), lambda i: (jnp.maximum(i * per - 1, 0), 0))


def _mix_out(xp, attn, gl, bias, pw, pscale, wpot, wao, wo, h, tm=512):
    S, D = h.shape
    nt = S // tm
    PW = xp.shape[1]

    def body(xc_ref, xprev_ref, at_ref, gl_ref, bias_ref, pw_ref, ps_ref, wpo_ref, wao_ref, wo_ref, h_ref,
             ho_ref, bp_ref, ba_ref):
        i = pl.program_id(0)
        pooled = _pooled(xc_ref[...], xprev_ref[...], i).astype(BF16)
        ms = (_pool_mix(pooled, pw_ref) * ps_ref[...]).astype(BF16)
        bp = _dot_nt(ms, wpo_ref[...])
        ba = _dot(at_ref[...], wao_ref[...])
        bp_ref[...] = bp.astype(BF16)
        ba_ref[...] = ba.astype(BF16)
        gates = jax.nn.sigmoid(gl_ref[...].astype(F32) + bias_ref[...])
        merged = (gates[:, :D] * bp + gates[:, D:] * ba).astype(BF16)
        ho_ref[...] = h_ref[...] + _dot(merged, wo_ref[...])

    def row(width):
        return pl.BlockSpec((tm, width), lambda i: (i, 0))

    def whole(x):
        nd = x.ndim
        return pl.BlockSpec(x.shape, lambda i: (0,) * nd)

    return pl.pallas_call(
        body, name="mix_out", grid=(nt,),
        in_specs=[row(PW), _halo_specs(tm, PW, S, False), row(D), row(2 * D), whole(bias), whole(pw), whole(pscale),
                  whole(wpot), whole(wao), whole(wo), row(D)],
        out_specs=[row(D), row(D), row(D)],
        out_shape=[jax.ShapeDtypeStruct((S, D), F32), jax.ShapeDtypeStruct((S, D), BF16),
                   jax.ShapeDtypeStruct((S, D), BF16)],
        compiler_params=_params(dimension_semantics=("arbitrary",)),
    )(xp, xp, attn, gl, bias, pw, pscale, wpot, wao, wo, h)


def _mix_bwd_gate(dh, wo, bp, ba, gl, bias, tm=512):
    S, D = dh.shape
    nt = S // tm

    def body(dh_ref, wo_ref, bp_ref, ba_ref, gl_ref, bias_ref, dgl_ref, dbp_ref, dba_ref, dwo_ref, dbias_ref):
        i = pl.program_id(0)
        dhb = dh_ref[...].astype(BF16)
        dm = _dot_nt(dhb, wo_ref[...])
        gates = jax.nn.sigmoid(gl_ref[...].astype(F32) + bias_ref[...])
        gp, ga = gates[:, :D], gates[:, D:]
        bp_v = bp_ref[...].astype(F32)
        ba_v = ba_ref[...].astype(F32)
        merged = (gp * bp_v + ga * ba_v).astype(BF16)
        _acc(dwo_ref, _dot_tn(merged, dhb), i == 0)
        dbp_ref[...] = (dm * gp).astype(BF16)
        dba_ref[...] = (dm * ga).astype(BF16)
        dgl = jnp.concatenate([dm * bp_v * gp * (1.0 - gp), dm * ba_v * ga * (1.0 - ga)], axis=1)
        dgl_ref[...] = dgl.astype(BF16)
        _acc(dbias_ref, jnp.sum(dgl, axis=0, keepdims=True), i == 0)

    def row(width):
        return pl.BlockSpec((tm, width), lambda i: (i, 0))

    def whole(shape):
        return pl.BlockSpec(shape, lambda i: (0, 0))

    return pl.pallas_call(
        body, name="mix_bwd_gate", grid=(nt,),
        in_specs=[row(D), whole(wo.shape), row(D), row(D), row(2 * D), whole(bias.shape)],
        out_specs=[row(2 * D), row(D), row(D), whole((D, D)), whole((1, 2 * D))],
        out_shape=[jax.ShapeDtypeStruct((S, 2 * D), BF16), jax.ShapeDtypeStruct((S, D), BF16),
                   jax.ShapeDtypeStruct((S, D), BF16), jax.ShapeDtypeStruct((D, D), F32),
                   jax.ShapeDtypeStruct((1, 2 * D), F32)],
        compiler_params=_params(dimension_semantics=("arbitrary",)),
    )(dh, wo, bp, ba, gl, bias)


def _mix_bwd_branch(dbp, dba, attn, xp, pw, pscale, wpot, wao, tm=1024):
    S, D = dbp.shape
    nt = S // tm
    PW = xp.shape[1]
    NG = len(POOL_WINDOWS)

    def body(dbp_ref, dba_ref, at_ref, xc_ref, xprev_ref, pw_ref, ps_ref, wpo_ref, wao_ref,
             dat_ref, dpl_ref, dwao_ref, dwpo_ref, dpw_ref, dps_ref):
        i = pl.program_id(0)
        dba_v = dba_ref[...]
        dbp_v = dbp_ref[...]
        _acc(dwao_ref, _dot_tn(at_ref[...], dba_v), i == 0)
        dat_ref[...] = _dot_nt(dba_v, wao_ref[...]).astype(BF16)
        pooled = _pooled(xc_ref[...], xprev_ref[...], i).astype(BF16)
        mixed = _pool_mix(pooled, pw_ref)
        ps = ps_ref[...]
        _acc(dwpo_ref, _dot_tn(dbp_v, (mixed * ps).astype(BF16)), i == 0)
        dms = _dot(dbp_v, wpo_ref[...])
        _acc(dps_ref, jnp.sum(dms * mixed, axis=0, keepdims=True), i == 0)
        dmixed = (dms * ps).astype(BF16)
        dpooled = []
        for gi in range(NG):
            cols = slice(gi * POOL_GROUP, (gi + 1) * POOL_GROUP)
            _acc(dpw_ref.at[gi], _dot_tn(pooled[:, cols], dmixed[:, cols]), i == 0)
            dpooled.append(_dot_nt(dmixed[:, cols], pw_ref[gi]))
        dpl_ref[...] = jnp.concatenate(dpooled, axis=1)

    def row(width):
        return pl.BlockSpec((tm, width), lambda i: (i, 0))

    def whole(shape):
        nd = len(shape)
        return pl.BlockSpec(shape, lambda i: (0,) * nd)

    return pl.pallas_call(
        body, name="mix_bwd_branch", grid=(nt,),
        in_specs=[row(D), row(D), row(D), row(PW), _halo_specs(tm, PW, S, False), whole(pw.shape), whole(pscale.shape),
                  whole(wpot.shape), whole(wao.shape)],
        out_specs=[row(D), row(PW), whole((D, D)), whole((D, PW)), whole(pw.shape), whole((1, PW))],
        out_shape=[jax.ShapeDtypeStruct((S, D), BF16), jax.ShapeDtypeStruct((S, PW), F32),
                   jax.ShapeDtypeStruct((D, D), F32), jax.ShapeDtypeStruct((D, PW), F32),
                   jax.ShapeDtypeStruct(pw.shape, F32), jax.ShapeDtypeStruct((1, PW), F32)],
        compiler_params=_params(dimension_semantics=("arbitrary",)),
    )(dbp, dba, attn, xp, xp, pw, pscale, wpot, wao)


def _pool_bwd(dpooled, tm=512):
    S, PW = dpooled.shape
    nt = S // tm

    def body(dc_ref, dnext_ref, dxp_ref):
        i = pl.program_id(0)
        dc = dc_ref[...]
        dh = jnp.concatenate([dc, jnp.where(i < nt - 1, dnext_ref[...], 0.0)], axis=0)
        rows = tm + POOL_HALO
        t = lax.broadcasted_iota(jnp.int32, (rows, 1), 0) + i * tm
        out = []
        for gi, w in enumerate(POOL_WINDOWS):
            cols = slice(gi * POOL_GROUP, (gi + 1) * POOL_GROUP)
            acc = dh[:, cols] / jnp.minimum(t + 1, w).astype(F32)
            sh = 1
            while sh < w:
                acc = acc + pltpu.roll(acc, rows - sh, 0)
                sh *= 2
            out.append(acc[:tm, :] - dc[:, cols])
        dxp_ref[...] = jnp.concatenate(out, axis=1).astype(BF16)

    return pl.pallas_call(
        body, name="pool_bwd", grid=(nt,),
        in_specs=[pl.BlockSpec((tm, PW), lambda i: (i, 0)), _halo_specs(tm, PW, S, True)],
        out_specs=pl.BlockSpec((tm, PW), lambda i: (i, 0)), out_shape=jax.ShapeDtypeStruct((S, PW), BF16),
        compiler_params=_params(dimension_semantics=("arbitrary",)),
    )(dpooled, dpooled)


def _in_bwd_w(pieces, u, tm=1024):
    S, D = u.shape
    nt = S // tm
    NW = sum(width for _, width in SEGMENTS)

    def body(*refs):
        piece_refs, u_ref, dw_hbm, acc, sem = refs[:len(SEGMENTS)], refs[len(SEGMENTS)], refs[len(SEGMENTS) + 1], refs[-2], refs[-1]
        i = pl.program_id(0)
        u_t = u_ref[...]
        for (off, width), ref in zip(SEGMENTS, piece_refs):
            for lo in range(0, width, D):
                hi = min(lo + D, width)
                _acc(acc.at[off + lo:off + hi, :], _dot_tn(ref[:, lo:hi].astype(BF16), u_t), i == 0)

        @pl.when(i == nt - 1)
        def _():
            cp = pltpu.make_async_copy(acc, dw_hbm, sem)
            cp.start()
            cp.wait()

    return pl.pallas_call(
        body, name="in_bwd_w", grid=(nt,),
        in_specs=[pl.BlockSpec((tm, width), lambda i: (i, 0)) for _, width in SEGMENTS] + [pl.BlockSpec((tm, D), lambda i: (i, 0))],
        out_specs=pl.BlockSpec(memory_space=pl.ANY), out_shape=jax.ShapeDtypeStruct((NW, D), F32),
        scratch_shapes=[pltpu.VMEM((NW, D), F32), pltpu.SemaphoreType.DMA],
        compiler_params=_params(dimension_semantics=("arbitrary",)),
    )(*pieces, u)


def _in_bwd_x(pieces, wint, h, dh, gnorm, tm=512):
    S, D = h.shape
    nt = S // tm

    def body(*refs):
        piece_refs = refs[:len(SEGMENTS)]
        w_ref, h_ref, dh_ref, g_ref, dx_ref, dxh_ref, dg_ref = refs[len(SEGMENTS):]
        i = pl.program_id(0)
        du = jnp.zeros((tm, D), F32)
        for (off, width), ref in zip(SEGMENTS, piece_refs):
            du = du + _dot(ref[...].astype(BF16), w_ref[off:off + width, :])
        dx, dg_rows = _rms_bwd(du, h_ref[...], g_ref[...])
        out = dh_ref[...] + dx
        dx_ref[...] = out
        dxh_ref[...] = (0.5 * out).astype(BF16)
        _acc(dg_ref, jnp.sum(dg_rows, axis=0, keepdims=True), i == 0)

    def row(width):
        return pl.BlockSpec((tm, width), lambda i: (i, 0))

    vec = pl.BlockSpec((1, D), lambda i: (0, 0))
    return pl.pallas_call(
        body, name="in_bwd_x", grid=(nt,),
        in_specs=[row(width) for _, width in SEGMENTS] + [pl.BlockSpec(wint.shape, lambda i: (0, 0)), row(D), row(D), vec],
        out_specs=[row(D), row(D), vec],
        out_shape=[jax.ShapeDtypeStruct((S, D), F32), jax.ShapeDtypeStruct((S, D), BF16), jax.ShapeDtypeStruct((1, D), F32)],
        compiler_params=_params(dimension_semantics=("arbitrary",)),
    )(*pieces, wint, h, dh, gnorm)


def _row_tile(rows):
    for t in (512, 480, 352, 256, 128, 64, 32, 16, 8):
        if rows % t == 0:
            return t
    return rows


def _adamw_update(w, g, m, v):
    mn = ADAM_B1 * m + (1.0 - ADAM_B1) * g
    vn = ADAM_B2 * v + (1.0 - ADAM_B2) * (g * g)
    m_hat = mn / (1.0 - ADAM_B1 ** ADAM_STEP)
    v_hat = vn / (1.0 - ADAM_B2 ** ADAM_STEP)
    return -ADAM_LR * (m_hat / (jnp.sqrt(v_hat) + ADAM_EPS) + ADAM_WD * w), mn, vn


def _adamw_small(ws, gs, ms, vs):
    n = len(ws)

    def rows(a):
        return a.reshape(-1, a.shape[-1]) if a.ndim > 1 else a.reshape(1, -1)

    def body(*refs):
        for k in range(n):
            w_ref, g_ref, m_ref, v_ref, d_ref, mo_ref, vo_ref = refs[k::n]
            d_ref[...], mo_ref[...], vo_ref[...] = _adamw_update(w_ref[...], g_ref[...], m_ref[...], v_ref[...])

    vmem = pl.BlockSpec(memory_space=pltpu.VMEM)
    out = pl.pallas_call(
        body, name="adamw_small", in_specs=[vmem] * (4 * n), out_specs=[vmem] * (3 * n),
        out_shape=[jax.ShapeDtypeStruct(rows(w).shape, F32) for _ in range(3) for w in ws], compiler_params=_params(),
    )(*[rows(a) for a in ws + gs + ms + vs])
    return [[o.reshape(w.shape) for o, w in zip(out[k * n:(k + 1) * n], ws)] for k in range(3)]


def _adamw(w, g, m, v, token, name, turned=False):
    R, C = w.shape
    tr = R if turned else _row_tile(R)

    def body(w_ref, g_ref, m_ref, v_ref, _, d_ref, mo_ref, vo_ref, go_ref):
        gv = g_ref[...].T if turned else g_ref[...]
        go_ref[...] = gv
        d_ref[...], mo_ref[...], vo_ref[...] = _adamw_update(w_ref[...], gv, m_ref[...], v_ref[...])

    blk = pl.BlockSpec((tr, C), lambda i: (i, 0))
    g_blk = pl.BlockSpec((C, R), lambda i: (0, 0)) if turned else blk
    sh = jax.ShapeDtypeStruct((R, C), F32)
    return pl.pallas_call(
        body, name=name, grid=(R // tr,), in_specs=[blk, g_blk, blk, blk, TOKEN_SPEC], out_specs=[blk] * 4, out_shape=[sh] * 4,
        compiler_params=_params(dimension_semantics=("arbitrary",)),
    )(w, g, m, v, token)


def _cast_place(w, place, name, turned=False, after=()):
    R, C = w.shape[::-1] if turned else w.shape
    tr = R if turned else _row_tile(R)
    per = R // tr

    def body(p_ref, w_ref, *rest):
        rest[-1][...] = (w_ref[...].T if turned else w_ref[...]).astype(BF16)

    grid_spec = pltpu.PrefetchScalarGridSpec(
        num_scalar_prefetch=1, grid=(per,),
        in_specs=[pl.BlockSpec((C, R), lambda i, p: (0, 0)) if turned else pl.BlockSpec((tr, C), lambda i, p: (i, 0))]
        + [TOKEN_SPEC] * len(after),
        out_specs=pl.BlockSpec((tr, C), lambda i, p: (p[0] * per + i, 0)))
    return pl.pallas_call(
        body, name=name, grid_spec=grid_spec, out_shape=jax.ShapeDtypeStruct((N_SHARDS * R, C), BF16),
        compiler_params=_params(dimension_semantics=("arbitrary",)),
    )(place, w, *after)


def _parked(w, m, idx, first, last):
    return jnp.where(m == w, idx, jnp.where(m < w, first, last))


def _sum_halves(g4s, recvs, place, name):
    M = len(g4s)
    NS, R, C = g4s[0].shape
    hr = R // 2
    tr = _row_tile(hr)
    per = hr // tr

    def body(p_ref, *refs):
        m = pl.program_id(0)
        for w in range(M):
            @pl.when(m == w)
            def _(w=w):
                refs[2 * M + w][...] = (refs[w][...] + refs[M + w][...]).astype(BF16)

    def spec(w, mine):
        def index(m, s, i, p):
            row = _parked(w, m, i, 0, per - 1)
            return _parked(w, m, s, 0, NS - 1), (p[1] * per + row if mine else row), 0

        return pl.BlockSpec((1, tr, C), index)

    grid_spec = pltpu.PrefetchScalarGridSpec(
        num_scalar_prefetch=1, grid=(M, NS, per),
        in_specs=[spec(w, True) for w in range(M)] + [spec(w, False) for w in range(M)],
        out_specs=[spec(w, False) for w in range(M)])
    return pl.pallas_call(
        body, name=name, grid_spec=grid_spec, out_shape=[jax.ShapeDtypeStruct((NS, hr, C), BF16)] * M,
        compiler_params=_params(dimension_semantics=("arbitrary",) * 3),
    )(place, *g4s, *recvs)


def _sum_quarters(h4s, recv3s, place, name):
    M = len(h4s)
    NS, hr, C = h4s[0].shape
    tr = _row_tile(hr)
    per = hr // tr

    def body(p_ref, *refs):
        m = pl.program_id(0)
        for w in range(M):
            @pl.when(m == w)
            def _(w=w):
                acc = refs[w][0].astype(F32)
                for k in range(N_SHARDS - 1):
                    acc = acc + refs[M + w][k].astype(F32)
                refs[2 * M + w][...] = acc

    def row(w, m, i):
        return _parked(w, m, i, 0, per - 1)

    grid_spec = pltpu.PrefetchScalarGridSpec(
        num_scalar_prefetch=1, grid=(M, per),
        in_specs=[pl.BlockSpec((1, tr, C), lambda m, i, p, w=w: (p[0], row(w, m, i), 0)) for w in range(M)]
        + [pl.BlockSpec((N_SHARDS - 1, tr, C), lambda m, i, p, w=w: (0, row(w, m, i), 0)) for w in range(M)],
        out_specs=[pl.BlockSpec((tr, C), lambda m, i, p, w=w: (p[1] * per + row(w, m, i), 0)) for w in range(M)])
    return pl.pallas_call(
        body, name=name, grid_spec=grid_spec, out_shape=[jax.ShapeDtypeStruct((2 * hr, C), F32)] * M,
        compiler_params=_params(dimension_semantics=("arbitrary",) * 2),
    )(place, *h4s, *recv3s)


def _by_shape(arrays):
    groups = {}
    for k, a in enumerate(arrays):
        groups.setdefault(a.shape, []).append(k)
    return list(groups.values())


def _place():
    x, y, c = lax.axis_index("x"), lax.axis_index("y"), lax.axis_index("c")
    chips = [(1 - x, y), (x, 1 - y), (1 - x, 1 - y)]
    return x, y, c, chips


HBM_SPEC = pl.BlockSpec(memory_space=pltpu.HBM)
SEM_SPEC = pl.BlockSpec(memory_space=pltpu.SEMAPHORE)
DATAFLOW = pltpu.SideEffectType.DATAFLOW_SIDE_EFFECTING


def _hbm(a):
    return pltpu.with_memory_space_constraint(a, pltpu.HBM)


class InFlight(NamedTuple):
    send_sem: jax.Array
    recv_sem: jax.Array
    bufs: list
    plan: Callable
    token: jax.Array


class Leg(NamedTuple):
    bufs: list
    landing: Optional[InFlight]
    plan: Optional[Callable]
    n_copies: int


def _send(bufs, plan, n_copies):
    return Leg([_hbm(b) for b in bufs], None, plan, n_copies)


def _land(flight):
    return Leg(flight.bufs, flight, None, 0)


def _forward(flight, plan, n_copies):
    return Leg(flight.bufs, flight, plan, n_copies)


def _wait_all(plan, refs, send_ref, recv_ref):
    for k, (src, dst, dev) in enumerate(plan(refs)):
        cp = pltpu.make_async_remote_copy(src_ref=src, dst_ref=dst, send_sem=send_ref.at[k], recv_sem=recv_ref.at[k],
                                          device_id=dev, device_id_type=MESH)
        cp.wait_send()
        cp.wait_recv()


def _start_all(plan, refs, send_ref, recv_ref):
    for k, (src, dst, dev) in enumerate(plan(refs)):
        pltpu.make_async_remote_copy(src_ref=src, dst_ref=dst, send_sem=send_ref.at[k], recv_sem=recv_ref.at[k],
                                     device_id=dev, device_id_type=MESH).start()


def _comm(name, legs, after):
    after = list(after) if isinstance(after, (list, tuple)) else [after]
    ins, in_specs, out_shape, out_specs, aliases, first = [], [], [], [], {}, []
    for leg in legs:
        first.append((len(ins), len(out_shape)))
        for b in leg.bufs:
            aliases[len(ins)] = len(out_shape)
            ins.append(b)
            in_specs.append(HBM_SPEC)
            out_shape.append(pltpu.HBM(b.shape, b.dtype))
            out_specs.append(HBM_SPEC)
        if leg.landing is not None:
            ins += [leg.landing.send_sem, leg.landing.recv_sem]
            in_specs += [SEM_SPEC, SEM_SPEC]
        if leg.plan is not None:
            out_shape += [pltpu.SemaphoreType.DMA((leg.n_copies,))] * 2
            out_specs += [SEM_SPEC, SEM_SPEC]
    starts = any(leg.plan is not None for leg in legs)
    if starts:
        out_shape.append(jax.ShapeDtypeStruct((8, 128), F32))
        out_specs.append(pl.BlockSpec(memory_space=pltpu.VMEM))
    n_in = len(ins) + len(after)

    def body(*refs):
        outs = refs[n_in:]
        for leg, (i, o) in zip(legs, first):
            nb = len(leg.bufs)
            bufs = refs[i:i + nb]
            if leg.landing is not None:
                _wait_all(leg.landing.plan, bufs, refs[i + nb], refs[i + nb + 1])
            if leg.plan is not None:
                _start_all(leg.plan, bufs, outs[o + nb], outs[o + nb + 1])
        if starts:
            outs[-1][...] = jnp.zeros_like(outs[-1])

    out = pl.pallas_call(
        body, name=name, in_specs=in_specs + [TOKEN_SPEC] * len(after), out_shape=out_shape, out_specs=out_specs,
        input_output_aliases=aliases, compiler_params=pltpu.CompilerParams(has_side_effects=DATAFLOW),
    )(*ins, *after)
    results = []
    for leg, (_, o) in zip(legs, first):
        nb = len(leg.bufs)
        bufs = list(out[o:o + nb])
        results.append(bufs if leg.plan is None else InFlight(out[o + nb], out[o + nb + 1], bufs, leg.plan, out[-1]))
    return results


def _half_rows(buf, chip, core):
    hr = buf.shape[0] // (2 * N_SHARDS)
    return buf.at[pl.ds(pl.multiple_of((2 * chip + core) * hr, 16), hr)]


def _gather_ici_plan(bufs):
    x, y, c, chips = _place()
    return [(_half_rows(b, 2 * x + y, c), _half_rows(b, 2 * x + y, c), (px, py, c)) for b in bufs for px, py in chips]


def _gather_d2d_plan(bufs):
    x, y, c, chips = _place()
    return [(_half_rows(b, 2 * px + py, c), _half_rows(b, 2 * px + py, c), (x, y, 1 - c)) for b in bufs for px, py in chips]


def _swap_plan(bufs):
    x, y, c, _ = _place()
    n = len(bufs) // 2
    copies = []
    for g, land in zip(bufs[:n], bufs[n:]):
        hr = g.shape[1] // 2
        copies.append((g.at[:, pl.ds(pl.multiple_of((1 - c) * hr, 8), hr)], land, (x, y, 1 - c)))
    return copies


def _exchange_plan(bufs):
    x, y, c, chips = _place()
    n = len(bufs) // 2
    return [(h.at[2 * px + py], land.at[k], (px, py, c))
            for h, land in zip(bufs[:n], bufs[n:]) for k, (px, py) in enumerate(chips)]


def _share_plan(bufs):
    x, y, c, _ = _place()
    copies = []
    for buf in bufs:
        hr = buf.shape[0] // 2
        mine = buf.at[pl.ds(pl.multiple_of(c * hr, 8), hr)]
        copies.append((mine, mine, (x, y, 1 - c)))
    return copies


N_DEVICES = 8


def _slot_place(vec, place):
    R, C = vec.shape

    def body(p_ref, v_ref, o_ref):
        o_ref[0] = v_ref[...]

    grid_spec = pltpu.PrefetchScalarGridSpec(
        num_scalar_prefetch=1, grid=(1,), in_specs=[pl.BlockSpec((R, C), lambda i, p: (0, 0))],
        out_specs=pl.BlockSpec((1, R, C), lambda i, p: (2 * p[0] + p[1], 0, 0)))
    return pl.pallas_call(
        body, name="slot_place", grid_spec=grid_spec, out_shape=jax.ShapeDtypeStruct((N_DEVICES, R, C), F32),
        compiler_params=_params(dimension_semantics=("arbitrary",)),
    )(place, vec)


def _slots_plan(bufs):
    x, y, c, _ = _place()
    mine = bufs[0].at[4 * x + 2 * y + c]
    return [(mine, mine, (x ^ (r >> 2), y ^ ((r >> 1) & 1), c ^ (r & 1))) for r in range(1, N_DEVICES)]


def _sum_slots(slots):
    _, R, C = slots.shape

    def body(s_ref, o_ref):
        acc = s_ref[0]
        for d in range(1, N_DEVICES):
            acc = acc + s_ref[d]
        o_ref[...] = acc

    return pl.pallas_call(
        body, name="sum_slots", in_specs=[pl.BlockSpec(memory_space=pltpu.VMEM)],
        out_specs=pl.BlockSpec(memory_space=pltpu.VMEM), out_shape=jax.ShapeDtypeStruct((R, C), F32),
        compiler_params=_params(),
    )(slots)


SMALL = ("ffn1_norm", "mix_norm", "pool_w", "pool_scale", "q_norm", "k_norm", "sinks", "gate_bias", "ffn2_norm")
SMALL_COLS = 1024
FFN1 = ("ffn1_w_gate", "ffn1_w_up", "ffn1_w_down")
MIXER = ("w_in", "w_pool_out", "w_attn_out", "w_out")
FFN2 = ("ffn2_w_gate", "ffn2_w_up", "ffn2_w_down")
LARGE = FFN1 + MIXER + FFN2
TRANSPOSED = ("ffn1_w_gate", "ffn1_w_up", "w_in", "w_pool_out", "ffn2_w_gate", "ffn2_w_up")
TURNED = ("w_pool_out",)
WEIGHTS = ("ffn1_norm", "ffn1_w_gate", "ffn1_w_up", "ffn1_w_down", "mix_norm", "w_in", "pool_w", "pool_scale",
           "w_pool_out", "q_norm", "k_norm", "sinks", "w_attn_out", "gate_bias", "w_out", "ffn2_norm",
           "ffn2_w_gate", "ffn2_w_up", "ffn2_w_down")


def _pack_small(parts):
    flat = jnp.concatenate([p.reshape(-1) for p in parts])
    rows = -(-flat.shape[0] // (8 * SMALL_COLS)) * 8
    return jnp.pad(flat, (0, rows * SMALL_COLS - flat.shape[0])).reshape(rows, SMALL_COLS)


def _unpack_small(packed, like):
    flat = packed.reshape(-1)
    out, off = [], 0
    for p in like:
        out.append(flat[off:off + p.size].reshape(p.shape))
        off += p.size
    return out, flat[off]


def _tie(small, token):
    return small + token[0, 0]


class Reduction:
    def __init__(self, names, place):
        self.names, self.place = names, place

    def swap(self, grads):
        grads = [g.reshape(N_SHARDS, -1, g.shape[-1]) for g in grads]
        lands = [lax.empty((g.shape[0], g.shape[1] // 2, g.shape[2]), g.dtype) for g in grads]
        return _send(grads + lands, _swap_plan, len(grads))

    def exchange(self, swapped):
        n = len(self.names)
        halves = self._per_shape(_sum_halves, "sum_halves", swapped[:n], swapped[n:])
        lands = [lax.empty((N_SHARDS - 1,) + h.shape[1:], h.dtype) for h in halves]
        return _send(halves + lands, _exchange_plan, (N_SHARDS - 1) * n)

    def share(self, exchanged):
        n = len(self.names)
        return _send(self._per_shape(_sum_quarters, "sum_quarters", exchanged[:n], exchanged[n:]), _share_plan, n)

    def named(self, shared):
        return dict(zip(self.names, shared))

    def _per_shape(self, add, stage, mine, received):
        out = [None] * len(mine)
        for idx in _by_shape(mine):
            sums = add([mine[k] for k in idx], [received[k] for k in idx], self.place, f"{stage}_{self.names[idx[0]]}")
            for k, v in zip(idx, sums):
                out[k] = v
        return out


def _row_form(name, a):
    return a.T if name in TRANSPOSED else a


def _shard_form(name, a):
    return a if name in TURNED else _row_form(name, a)


def kernel(x, ffn1_norm, ffn1_w_gate, ffn1_w_up, ffn1_w_down, mix_norm, w_in, pool_w, pool_scale, w_pool_out, q_norm, k_norm, sinks, w_attn_out, gate_bias, w_out, ffn2_norm, ffn2_w_gate, ffn2_w_up, ffn2_w_down, loss_target, m_ffn1_norm, m_ffn1_w_gate, m_ffn1_w_up, m_ffn1_w_down, m_mix_norm, m_w_in, m_pool_w, m_pool_scale, m_w_pool_out, m_q_norm, m_k_norm, m_sinks, m_w_attn_out, m_gate_bias, m_w_out, m_ffn2_norm, m_ffn2_w_gate, m_ffn2_w_up, m_ffn2_w_down, v_ffn1_norm, v_ffn1_w_gate, v_ffn1_w_up, v_ffn1_w_down, v_mix_norm, v_w_in, v_pool_w, v_pool_scale, v_w_pool_out, v_q_norm, v_k_norm, v_sinks, v_w_attn_out, v_gate_bias, v_w_out, v_ffn2_norm, v_ffn2_w_gate, v_ffn2_w_up, v_ffn2_w_down):
    args = dict(locals())
    wts = {n: _shard_form(n, args[n]) for n in WEIGHTS}
    mom = {n: _shard_form(n, args["m_" + n]) for n in WEIGHTS}
    var = {n: _shard_form(n, args["v_" + n]) for n in WEIGHTS}
    shard = 2 * lax.axis_index("x") + lax.axis_index("y")
    place = jnp.stack([shard, lax.axis_index("c")]).astype(jnp.int32)

    xs, target = x[0], loss_target[0]
    D = xs.shape[1]
    g1 = wts["ffn1_norm"].reshape(1, D)
    gm = wts["mix_norm"].reshape(1, D)
    g2 = wts["ffn2_norm"].reshape(1, D)
    qw = wts["q_norm"].reshape(1, HEAD_DIM)
    kw = wts["k_norm"].reshape(1, HEAD_DIM)
    bias = wts["gate_bias"].reshape(1, 2 * D)
    pscale = wts["pool_scale"].reshape(1, -1)
    pw = wts["pool_w"].astype(BF16)
    sink_rows = jnp.broadcast_to(jnp.repeat(wts["sinks"], ATTN_BLOCK).reshape(N_KV_HEADS, GQA_GROUP * ATTN_BLOCK, 1),
                                 (N_KV_HEADS, GQA_GROUP * ATTN_BLOCK, 128))

    gate1, up1, down1 = FFN1[:1], FFN1[1:2], FFN1[2:]
    mix_in, mix_out = MIXER[:1], MIXER[1:]

    def over_ici(names):
        return _send([placed[n] for n in names], _gather_ici_plan, 3 * len(names))

    def to_sibling(flight):
        return _forward(flight, _gather_d2d_plan, len(flight.bufs) * 3)

    placed = {n: _cast_place(wts[n], place, "cast_" + n) for n in gate1}
    gate1_ici, = _comm("gather_gate1_ici", [over_ici(gate1)], place)
    placed.update({n: _cast_place(wts[n], place, "cast_" + n) for n in up1 + down1})
    up1_ici, down1_ici = _comm("gather_ffn1_ici", [over_ici(up1), over_ici(down1)], gate1_ici.token)
    placed.update({n: _cast_place(wts[n], place, "cast_" + n, turned=n in TURNED) for n in MIXER})
    band = _band_bias()
    shadow = [sink_rows, pw, band, down1_ici.token]
    gate1_d2d, in_ici, out_ici = _comm(
        "gather_mix_ici", [to_sibling(gate1_ici), over_ici(mix_in), over_ici(mix_out)], shadow)
    w = dict(zip(gate1, _comm("gather_gate1_wait", [_land(gate1_d2d)], gate1_d2d.token)[0]))

    n1, a1 = _ffn_gate(xs, g1, w["ffn1_w_gate"], "ffn1_gate")
    placed.update({n: _cast_place(wts[n], place, "cast_" + n, after=[a1]) for n in FFN2})
    up1_d2d, ffn2_ici, down1_d2d = _comm(
        "gather_ffn1_d2d", [to_sibling(up1_ici), over_ici(FFN2), to_sibling(down1_ici)], a1)
    w.update(zip(up1, _comm("gather_up1_wait", [_land(up1_d2d)], up1_d2d.token)[0]))
    p1, q1, s1 = _ffn_hidden(n1, a1, w["ffn1_w_up"], "ffn1_hidden")
    w.update(zip(down1, _comm("gather_down1_wait", [_land(down1_d2d)], s1)[0]))
    h1 = _ffn_down(xs, s1, w["ffn1_w_down"], "ffn1_down")
    in_d2d, out_d2d = _comm("gather_mix_d2d", [to_sibling(in_ici), to_sibling(out_ici)], h1)
    w.update(zip(mix_in, _comm("gather_in_wait", [_land(in_d2d)], in_d2d.token)[0]))
    u, xp, q, kv, gl = _mix_proj(h1, gm, w["w_in"])
    attn = _attn_fwd(q, kv, qw, kw, sink_rows, band)
    ffn2_d2d, landed = _comm("gather_ffn2_d2d", [to_sibling(ffn2_ici), _land(out_d2d)], attn)
    w.update(zip(mix_out, landed))
    h2, bp, ba = _mix_out(xp, attn, gl, bias, pw, pscale, w["w_pool_out"], w["w_attn_out"], w["w_out"], h1)
    w.update(zip(FFN2, _comm("gather_ffn2_wait", [_land(ffn2_d2d)], h2)[0]))
    n2, p2, q2, s2 = _ffn_up(h2, g2, w["ffn2_w_gate"], w["ffn2_w_up"], "ffn2_up")
    dy, dyh, loss = _ffn_down_loss(h2, s2, w["ffn2_w_down"], target, "ffn2_down_loss")

    gw, gs = {}, {}
    da, db = _ffn_bwd_h(dyh, p2, q2, w["ffn2_w_down"], ffn2_d2d.token, "ffn2_bwd_h")
    gw["ffn2_w_gate"], gw["ffn2_w_up"] = _xty(da, n2, "ffn2_dw_gate"), _xty(db, n2, "ffn2_dw_up")
    gw["ffn2_w_down"] = _xty(s2, dyh, "ffn2_dw_down")
    red2 = Reduction(FFN2, place)
    swap, = _comm("reduce_ffn2_swap", [red2.swap([gw[n] for n in FFN2])], da)
    dh2, gs["ffn2_norm"] = _ffn_bwd_x(da, db, w["ffn2_w_gate"], w["ffn2_w_up"], h2, dy, _tie(g2, swap.token), "ffn2_bwd_x")
    swapped, = _comm("reduce_ffn2_swap_wait", [_land(swap)], dh2)
    exchange, = _comm("reduce_ffn2_exchange", [red2.exchange(swapped)], place)
    dgl, dbp, dba, gw["w_out"], gs["gate_bias"] = _mix_bwd_gate(dh2, w["w_out"], bp, ba, gl, _tie(bias, exchange.token))
    dattn, dpooled, gw["w_attn_out"], gw["w_pool_out"], gs["pool_w"], gs["pool_scale"] = _mix_bwd_branch(
        dbp, dba, attn, xp, pw, pscale, w["w_pool_out"], w["w_attn_out"])
    dq, dkv, gs["q_norm"], gs["k_norm"], dsk = _attn_bwd(q, kv, dattn, qw, kw, sink_rows, band)
    gs["sinks"] = dsk[:, :, 0]
    exchanged, = _comm("reduce_ffn2_exchange_wait", [_land(exchange)], dq)
    share2, = _comm("reduce_ffn2_share", [red2.share(exchanged)], place)
    pieces = (_pool_bwd(dpooled), dq, dkv, dgl)
    gw["w_in"] = _in_bwd_w(pieces, u)
    redm = Reduction(MIXER, place)
    swap, shared = _comm("reduce_mix_swap", [redm.swap([gw[n] for n in MIXER]), _land(share2)], share2.token)
    grads = red2.named(shared)
    dh1, dh1h, gs["mix_norm"] = _in_bwd_x(pieces, w["w_in"], h1, dh2, _tie(gm, swap.token))
    red1, swaps = Reduction(FFN1, place), {}
    dw_down = _xty(s1, dh1h, "ffn1_dw_down")
    swaps["ffn1_w_down"], swapped = _comm("reduce_mix_swap_wait", [red1.swap([dw_down]), _land(swap)], dh1)
    exchange, = _comm("reduce_mix_exchange", [redm.exchange(swapped)], place)
    da, db = _ffn_bwd_h(dh1h, p1, q1, w["ffn1_w_down"], exchange.token, "ffn1_bwd_h")
    dw_up = _xty(db, n1, "ffn1_dw_up")
    swaps["ffn1_w_up"], = _comm("reduce_ffn1_swap_up", [red1.swap([dw_up])], place)
    dw_gate = _xty(da, n1, "ffn1_dw_gate", after=[swaps["ffn1_w_up"].token])
    swaps["ffn1_w_gate"], exchanged = _comm("reduce_mix_exchange_wait", [red1.swap([dw_gate]), _land(exchange)], place)
    sharem, *landed = _comm("reduce_mix_share", [redm.share(exchanged)] + [_land(swaps[n]) for n in FFN1], place)
    exchange, = _comm("reduce_ffn1_exchange", [red1.exchange([l[0] for l in landed] + [l[1] for l in landed])], place)
    grad_x, gs["ffn1_norm"] = _ffn_bwd_x(da, db, w["ffn1_w_gate"], w["ffn1_w_up"], xs, dh1, _tie(g1, exchange.token), "ffn1_bwd_x")

    small_parts = [gs[n] for n in SMALL] + [loss[0, 0].reshape(1)]
    slots, shared = _comm(
        "gather_small", [_send([_slot_place(_pack_small(small_parts), place)], _slots_plan, N_DEVICES - 1), _land(sharem)],
        grad_x)
    grads.update(redm.named(shared))
    delta, new_m, new_v = {}, {}, {}

    def adamw(names, token):
        for n in names:
            delta[n], new_m[n], new_v[n], grads[n] = _adamw(wts[n], grads[n], mom[n], var[n], token, "adamw_" + n, n in TURNED)

    early, late = FFN2 + MIXER[:-1], MIXER[-1:]
    adamw(early, slots.token)
    exchanged, = _comm("reduce_ffn1_exchange_wait", [_land(exchange)], [delta[n] for n in early])
    share1, landed = _comm("reduce_ffn1_share", [red1.share(exchanged), _land(slots)], place)
    summed = _sum_slots(landed[0])
    small_grads, loss_sum = _unpack_small(summed, [wts[n] for n in SMALL])
    grads.update(dict(zip(SMALL, small_grads)))
    for out, vals in zip((delta, new_m, new_v), _adamw_small(*[[d[n] for n in SMALL] for d in (wts, grads, mom, var)])):
        out.update(dict(zip(SMALL, vals)))
    adamw(late, share1.token)
    results = {(k, n): _shard_form(n, d[n]) for k, d in enumerate((grads, delta, new_m, new_v)) for n in SMALL + FFN2 + MIXER}
    shared, = _comm("reduce_ffn1_share_wait", [_land(share1)], [results[k, n] for k in range(4) for n in SMALL + late])
    grads.update(red1.named(shared))
    adamw(FFN1, share1.token)
    results.update({(k, n): _shard_form(n, d[n]) for k, d in enumerate((grads, delta, new_m, new_v)) for n in FFN1})
    return (loss_sum, grad_x[None], *[results[k, n] for k in range(4) for n in WEIGHTS])
```

```python
from typing import Callable, NamedTuple, Optional

import jax
import jax.numpy as jnp
from jax import lax
from jax.experimental import pallas as pl
from jax.experimental.pallas import tpu as pltpu

F32 = jnp.float32
BF16 = jnp.bfloat16
RMS_EPS = 1e-6
POOL_WINDOWS = (2, 4, 8, 16)
POOL_GROUP = 128
POOL_HALO = 16
HEAD_DIM = 64
GQA_GROUP = 8
N_KV_HEADS = 2
ATTN_BLOCK = 128
SCALE = HEAD_DIM ** -0.5
NEG = -1e30
N_SHARDS = 4
ADAM_LR, ADAM_B1, ADAM_B2, ADAM_EPS, ADAM_WD, ADAM_STEP = 0.001, 0.9, 0.999, 1e-08, 0.01, 10
VMEM_LIMIT = 56 * 1024 * 1024
MESH = pl.DeviceIdType.MESH
SEG_POOL, SEG_Q, SEG_KV, SEG_GATE = (0, 512), (512, 1024), (1536, 256), (1792, 2048)
SEGMENTS = (SEG_POOL, SEG_Q, SEG_KV, SEG_GATE)


def _params(**kw):
    return pltpu.CompilerParams(vmem_limit_bytes=VMEM_LIMIT, **kw)


def _dot(a, b):
    return jnp.dot(a, b, preferred_element_type=F32)


def _dot_nt(a, b):
    return lax.dot_general(a, b, (((1,), (1,)), ((), ())), preferred_element_type=F32)


def _dot_tn(a, b):
    return lax.dot_general(a, b, (((0,), (0,)), ((), ())), preferred_element_type=F32)


def _rinv(x):
    return lax.rsqrt(jnp.mean(x * x, axis=-1, keepdims=True) + RMS_EPS)


def _rms_bwd(dn, x, g):
    r = _rinv(x)
    xr = x * r
    z = dn * g
    dx = r * (z - xr * jnp.mean(z * xr, axis=-1, keepdims=True))
    return dx, dn * xr


def _acc(ref, val, first):
    @pl.when(first)
    def _():
        ref[...] = val

    @pl.when(jnp.logical_not(first))
    def _():
        ref[...] += val


TOKEN_SPEC = pl.BlockSpec(memory_space=pl.ANY)
F_HALF = 1408


def _resident(w):
    return pl.BlockSpec(w.shape, lambda i: (0, 0), pipeline_mode=pl.Buffered(1))


def _swiglu(a, b):
    sig = jax.nn.sigmoid(a)
    q = a * sig
    return b * (sig + q * (1.0 - sig)), q, q * b


def _ffn_up(h, gnorm, wgt, wut, name, tm=512):
    S, D = h.shape
    F = wgt.shape[0]

    def body(h_ref, g_ref, wg_ref, wu_ref, n_ref, p_ref, q_ref, s_ref):
        x = h_ref[...]
        n = (x * _rinv(x) * g_ref[...]).astype(BF16)
        n_ref[...] = n
        for lo in range(0, F, F_HALF):
            cols = slice(lo, lo + F_HALF)
            p, q, s = _swiglu(_dot_nt(n, wg_ref[cols, :]), _dot_nt(n, wu_ref[cols, :]))
            p_ref[:, cols] = p.astype(BF16)
            q_ref[:, cols] = q.astype(BF16)
            s_ref[:, cols] = s.astype(BF16)

    act = pl.BlockSpec((tm, F), lambda i: (i, 0))
    hidden = jax.ShapeDtypeStruct((S, F), BF16)
    return pl.pallas_call(
        body, name=name, grid=(S // tm,),
        in_specs=[pl.BlockSpec((tm, D), lambda i: (i, 0)), pl.BlockSpec((1, D), lambda i: (0, 0)), _resident(wgt), _resident(wut)],
        out_specs=[pl.BlockSpec((tm, D), lambda i: (i, 0)), act, act, act],
        out_shape=[jax.ShapeDtypeStruct((S, D), BF16), hidden, hidden, hidden],
        compiler_params=_params(dimension_semantics=("arbitrary",)),
    )(h, gnorm, wgt, wut)


def _ffn_gate(h, gnorm, wgt, name, tm=512):
    S, D = h.shape
    F = wgt.shape[0]

    def body(h_ref, g_ref, wg_ref, n_ref, a_ref):
        x = h_ref[...]
        n = (x * _rinv(x) * g_ref[...]).astype(BF16)
        n_ref[...] = n
        for lo in range(0, F, F_HALF):
            a_ref[:, lo:lo + F_HALF] = _dot_nt(n, wg_ref[lo:lo + F_HALF, :]).astype(BF16)

    row = pl.BlockSpec((tm, D), lambda i: (i, 0))
    return pl.pallas_call(
        body, name=name, grid=(S // tm,), in_specs=[row, pl.BlockSpec((1, D), lambda i: (0, 0)), _resident(wgt)],
        out_specs=[row, pl.BlockSpec((tm, F), lambda i: (i, 0))],
        out_shape=[jax.ShapeDtypeStruct((S, D), BF16), jax.ShapeDtypeStruct((S, F), BF16)],
        compiler_params=_params(dimension_semantics=("arbitrary",)),
    )(h, gnorm, wgt)


def _ffn_hidden(n, a, wut, name, tm=512):
    S, D = n.shape
    F = wut.shape[0]

    def body(n_ref, a_ref, wu_ref, p_ref, q_ref, s_ref):
        n_t = n_ref[...]
        for lo in range(0, F, F_HALF):
            cols = slice(lo, lo + F_HALF)
            p, q, s = _swiglu(a_ref[:, cols].astype(F32), _dot_nt(n_t, wu_ref[cols, :]))
            p_ref[:, cols] = p.astype(BF16)
            q_ref[:, cols] = q.astype(BF16)
            s_ref[:, cols] = s.astype(BF16)

    act = pl.BlockSpec((tm, F), lambda i: (i, 0))
    hidden = jax.ShapeDtypeStruct((S, F), BF16)
    return pl.pallas_call(
        body, name=name, grid=(S // tm,), in_specs=[pl.BlockSpec((tm, D), lambda i: (i, 0)), act, _resident(wut)],
        out_specs=[act, act, act], out_shape=[hidden, hidden, hidden],
        compiler_params=_params(dimension_semantics=("arbitrary",)),
    )(n, a, wut)


def _ffn_down(h, s, wd, name, tm=1024):
    S, D = h.shape
    F = wd.shape[0]

    def body(h_ref, s_ref, wd_ref, o_ref):
        o_ref[...] = h_ref[...] + 0.5 * _dot(s_ref[...], wd_ref[...])

    row = pl.BlockSpec((tm, D), lambda i: (i, 0))
    return pl.pallas_call(
        body, name=name, grid=(S // tm,), in_specs=[row, pl.BlockSpec((tm, F), lambda i: (i, 0)), _resident(wd)],
        out_specs=row, out_shape=jax.ShapeDtypeStruct((S, D), F32),
        compiler_params=_params(dimension_semantics=("arbitrary",)),
    )(h, s, wd)


def _ffn_down_loss(h, s, wd, target, name, tm=512):
    S, D = h.shape
    F = wd.shape[0]

    def body(h_ref, s_ref, wd_ref, t_ref, dy_ref, dyh_ref, loss_ref):
        e = h_ref[...] + 0.5 * _dot(s_ref[...], wd_ref[...]) - t_ref[...]
        dy = e * (1.0 / D)
        dy_ref[...] = dy
        dyh_ref[...] = (0.5 * dy).astype(BF16)
        tot = jnp.sum(jnp.sum(e * e, axis=1, keepdims=True), axis=0, keepdims=True) * (0.5 / D)
        _acc(loss_ref, jnp.broadcast_to(tot, loss_ref.shape), pl.program_id(0) == 0)

    row = pl.BlockSpec((tm, D), lambda i: (i, 0))
    return pl.pallas_call(
        body, name=name, grid=(S // tm,), in_specs=[row, pl.BlockSpec((tm, F), lambda i: (i, 0)), _resident(wd), row],
        out_specs=[row, row, pl.BlockSpec((8, 128), lambda i: (0, 0))],
        out_shape=[jax.ShapeDtypeStruct((S, D), F32), jax.ShapeDtypeStruct((S, D), BF16), jax.ShapeDtypeStruct((8, 128), F32)],
        compiler_params=_params(dimension_semantics=("arbitrary",)),
    )(h, s, wd, target)


def _ffn_bwd_h(dyh, p, q, wd, token, name, tm=512):
    S, D = dyh.shape
    F = wd.shape[0]

    def body(dyh_ref, p_ref, q_ref, wd_ref, _, da_ref, db_ref):
        dyh_t = dyh_ref[...]
        for lo in range(0, F, F_HALF):
            cols = slice(lo, lo + F_HALF)
            ds = _dot_nt(dyh_t, wd_ref[cols, :])
            da_ref[:, cols] = (ds * p_ref[:, cols].astype(F32)).astype(BF16)
            db_ref[:, cols] = (ds * q_ref[:, cols].astype(F32)).astype(BF16)

    act = pl.BlockSpec((tm, F), lambda i: (i, 0))
    hidden = jax.ShapeDtypeStruct((S, F), BF16)
    return pl.pallas_call(
        body, name=name, grid=(S // tm,),
        in_specs=[pl.BlockSpec((tm, D), lambda i: (i, 0)), act, act, _resident(wd), TOKEN_SPEC],
        out_specs=[act, act], out_shape=[hidden, hidden],
        compiler_params=_params(dimension_semantics=("arbitrary",)),
    )(dyh, p, q, wd, token)


def _xty(x, y, name, tk=2048, tf=1408, after=()):
    S, F = x.shape
    D = y.shape[1]

    def body(x_ref, y_ref, *rest):
        _acc(rest[-1], _dot_tn(x_ref[...], y_ref[...]), pl.program_id(1) == 0)

    return pl.pallas_call(
        body, name=name, grid=(F // tf, S // tk),
        in_specs=[pl.BlockSpec((tk, tf), lambda j, k: (k, j)), pl.BlockSpec((tk, D), lambda j, k: (k, 0))]
        + [TOKEN_SPEC] * len(after),
        out_specs=pl.BlockSpec((tf, D), lambda j, k: (j, 0)), out_shape=jax.ShapeDtypeStruct((F, D), F32),
        compiler_params=_params(dimension_semantics=("arbitrary", "arbitrary")),
    )(x, y, *after)


def _ffn_bwd_x(da, db, wgt, wut, x, dh, gnorm, name, tm=512):
    S, D = x.shape
    F = wgt.shape[0]

    def body(da_ref, db_ref, wg_ref, wu_ref, x_ref, dh_ref, g_ref, dx_ref, dg_ref):
        dn = _dot(da_ref[...], wg_ref[...]) + _dot(db_ref[...], wu_ref[...])
        dx, dg_rows = _rms_bwd(dn, x_ref[...], g_ref[...])
        dx_ref[...] = dh_ref[...] + dx
        _acc(dg_ref, jnp.sum(dg_rows, axis=0, keepdims=True), pl.program_id(0) == 0)

    row = pl.BlockSpec((tm, D), lambda i: (i, 0))
    act = pl.BlockSpec((tm, F), lambda i: (i, 0))
    vec = pl.BlockSpec((1, D), lambda i: (0, 0))
    return pl.pallas_call(
        body, name=name, grid=(S // tm,), in_specs=[act, act, _resident(wgt), _resident(wut), row, row, vec],
        out_specs=[row, vec],
        out_shape=[jax.ShapeDtypeStruct((S, D), F32), jax.ShapeDtypeStruct((1, D), F32)],
        compiler_params=_params(dimension_semantics=("arbitrary",)),
    )(da, db, wgt, wut, x, dh, gnorm)


def _mix_proj(h, gnorm, wint, tm=512):
    S, D = h.shape
    nt = S // tm

    def body(h_ref, g_ref, w_ref, u_ref, xp_ref, q_ref, kv_ref, gl_ref):
        x = h_ref[...]
        u = (x * _rinv(x) * g_ref[...]).astype(BF16)
        u_ref[...] = u
        for (off, width), ref in zip(SEGMENTS, (xp_ref, q_ref, kv_ref, gl_ref)):
            ref[...] = _dot_nt(u, w_ref[off:off + width, :]).astype(ref.dtype)

    def row(width):
        return pl.BlockSpec((tm, width), lambda i: (i, 0))

    return pl.pallas_call(
        body, name="mix_proj", grid=(nt,),
        in_specs=[row(D), pl.BlockSpec((1, D), lambda i: (0, 0)), pl.BlockSpec(wint.shape, lambda i: (0, 0))],
        out_specs=[row(D), row(SEG_POOL[1]), row(SEG_Q[1]), row(SEG_KV[1]), row(SEG_GATE[1])],
        out_shape=[jax.ShapeDtypeStruct((S, D), BF16), jax.ShapeDtypeStruct((S, SEG_POOL[1]), F32),
                   jax.ShapeDtypeStruct((S, SEG_Q[1]), BF16), jax.ShapeDtypeStruct((S, SEG_KV[1]), BF16),
                   jax.ShapeDtypeStruct((S, SEG_GATE[1]), BF16)],
        compiler_params=_params(dimension_semantics=("arbitrary",)),
    )(h, gnorm, wint)


def _stack_heads(x, g):
    return jnp.concatenate([x[:, (GQA_GROUP * g + hh) * HEAD_DIM:(GQA_GROUP * g + hh + 1) * HEAD_DIM]
                            for hh in range(GQA_GROUP)], axis=0)


def _unstack_heads(ref, val, g):
    for hh in range(GQA_GROUP):
        lo = (GQA_GROUP * g + hh) * HEAD_DIM
        ref[:, lo:lo + HEAD_DIM] = val[hh * ATTN_BLOCK:(hh + 1) * ATTN_BLOCK, :]


def _rowsum(xb, width):
    return _dot(xb, jnp.ones((xb.shape[1], width), BF16))


def _rinv_lanes(x):
    return lax.rsqrt(_rowsum((x * x).astype(BF16), x.shape[1]) * (1.0 / x.shape[1]) + RMS_EPS)


def _twice(x):
    return jnp.concatenate([x, x], axis=1)


def _band_bias():
    qi = (jnp.arange(GQA_GROUP * ATTN_BLOCK) % ATTN_BLOCK)[:, None]
    kj = jnp.arange(2 * ATTN_BLOCK)[None, :]
    return jnp.where(jnp.logical_and(kj > qi, kj <= qi + ATTN_BLOCK), 0.0, NEG).astype(F32)


def _attn_exp(qn, kk, sink, band, n):
    kj = lax.broadcasted_iota(jnp.int32, (1, 2 * ATTN_BLOCK), 1)
    before_sequence = jnp.where(jnp.logical_and(kj < ATTN_BLOCK, n == 0), NEG, 0.0)
    s = _dot_nt(qn, kk) + band + before_sequence
    m = jnp.maximum(jnp.broadcast_to(jnp.max(s, axis=-1, keepdims=True), sink.shape), sink)
    e = jnp.exp(s - _twice(m))
    e_s = jnp.exp(sink - m)
    e_b = e.astype(BF16)
    inv = 1.0 / (_rowsum(e_b, ATTN_BLOCK) + e_s)
    return e_b, e, e_s, inv


def _attn_blocks(n):
    cur = pl.multiple_of(n * ATTN_BLOCK, ATTN_BLOCK)
    prev = pl.multiple_of(jnp.maximum(n - 1, 0) * ATTN_BLOCK, ATTN_BLOCK)
    return cur, prev


def _kv_window(kv_ref, n):
    cur, prev = _attn_blocks(n)
    return jnp.concatenate([kv_ref[pl.ds(prev, ATTN_BLOCK), :], kv_ref[pl.ds(cur, ATTN_BLOCK), :]], axis=0).astype(F32)


def _kv_split(kv, g):
    k = kv[:, g * HEAD_DIM:(g + 1) * HEAD_DIM]
    v = kv[:, (N_KV_HEADS + g) * HEAD_DIM:(N_KV_HEADS + g + 1) * HEAD_DIM]
    return k, v


def _attn_fwd(q, kv, qw, kw, sink_rows, band):
    S, W = q.shape
    per = 2
    rows = per * ATTN_BLOCK

    def body(q_ref, kv_ref, qw_ref, kw_ref, sk_ref, band_ref, o_ref, o_scr):
        qf = q_ref[...].astype(F32)
        pairs = [(b, g) for b in range(per) for g in range(N_KV_HEADS)]
        ns = [per * pl.program_id(0) + b for b in range(per)]
        kvw = [_kv_window(kv_ref, n) for n in ns]
        qs = [_stack_heads(qf[b * ATTN_BLOCK:(b + 1) * ATTN_BLOCK], g) for b, g in pairs]
        kvs = [_kv_split(kvw[b], g) for b, g in pairs]
        rq = [_rinv_lanes(x) for x in qs]
        rk = [_rinv_lanes(k) for k, _ in kvs]
        qn = [(x * r * qw_ref[...] * SCALE).astype(BF16) for x, r in zip(qs, rq)]
        kk = [(k * r * kw_ref[...]).astype(BF16) for (k, _), r in zip(kvs, rk)]
        ex = [_attn_exp(qn[i], kk[i], sk_ref[g], band_ref[...], ns[b]) for i, (b, g) in enumerate(pairs)]
        pv = [_dot(ex[i][0], kvs[i][1].astype(BF16)) for i in range(len(pairs))]
        for i, (b, g) in enumerate(pairs):
            _unstack_heads(o_scr.at[b * ATTN_BLOCK:(b + 1) * ATTN_BLOCK, :], pv[i] * ex[i][3][:, :HEAD_DIM], g)
        o_ref[...] = o_scr[...].astype(BF16)

    blk = pl.BlockSpec((rows, W), lambda n: (n, 0))
    return pl.pallas_call(
        body, name="attn_fwd", grid=(S // rows,),
        in_specs=[blk, pl.BlockSpec(kv.shape, lambda n: (0, 0)), pl.BlockSpec((1, HEAD_DIM), lambda n: (0, 0)),
                  pl.BlockSpec((1, HEAD_DIM), lambda n: (0, 0)), pl.BlockSpec(sink_rows.shape, lambda n: (0, 0, 0)),
                  pl.BlockSpec(band.shape, lambda n: (0, 0))],
        out_specs=blk, out_shape=jax.ShapeDtypeStruct((S, W), BF16),
        scratch_shapes=[pltpu.VMEM((rows, W), F32)],
        compiler_params=_params(dimension_semantics=("arbitrary",)),
    )(q, kv, qw, kw, sink_rows, band)


def _attn_bwd(q, kv, do, qw, kw, sink_rows, band):
    S, W = q.shape
    KW = kv.shape[1]
    nb = S // ATTN_BLOCK
    per = 2
    chunk = 512

    def body(q_ref, kv_ref, do_ref, qw_ref, kw_ref, sk_ref, band_ref, dq_ref, dkv_ref, dqw_ref, dkw_ref, dsk_ref, dq_scr):
        step = pl.program_id(0)

        @pl.when(step == 0)
        def _():
            dkv_ref[...] = jnp.zeros_like(dkv_ref)
            dqw_ref[...] = jnp.zeros_like(dqw_ref)
            dsk_ref[...] = jnp.zeros_like(dsk_ref)

        qw_v = qw_ref[...]
        for b in range(per):
            n = per * step + b
            mine = slice(b * ATTN_BLOCK, (b + 1) * ATTN_BLOCK)
            qf = q_ref[mine, :].astype(F32)
            dof = do_ref[mine, :].astype(F32)
            cur, prev = _attn_blocks(n)
            kvw = _kv_window(kv_ref, n)
            for g in range(N_KV_HEADS):
                qs = _stack_heads(qf, g)
                rq = _rinv_lanes(qs)
                qhat = qs * rq
                qn = (qhat * qw_v * SCALE).astype(BF16)
                k, v = _kv_split(kvw, g)
                kk = (k * _rinv_lanes(k) * kw_ref[...]).astype(BF16)
                vv = v.astype(BF16)
                dos = _stack_heads(dof, g).astype(BF16)
                _, e, e_s, inv = _attn_exp(qn, kk, sk_ref[g], band_ref[...], n)
                p = e * _twice(inv)
                dp = _dot_nt(dos, vv)
                drow = _rowsum((p * dp).astype(BF16), ATTN_BLOCK)
                ds = (p * (dp - _twice(drow))).astype(BF16)
                dsink = -(e_s * inv * drow)
                for hh in range(GQA_GROUP):
                    tot = jnp.sum(dsink[hh * ATTN_BLOCK:(hh + 1) * ATTN_BLOCK, :], axis=0, keepdims=True)
                    dsk_ref[g, hh:hh + 1, :] += tot
                dqn = _dot(ds, kk) * SCALE
                dkk = _dot_tn(ds, qn)
                dvv = _dot_tn(p.astype(BF16), dos)
                klo, vlo = g * HEAD_DIM, (N_KV_HEADS + g) * HEAD_DIM
                for start, rows in ((prev, slice(0, ATTN_BLOCK)), (cur, slice(ATTN_BLOCK, 2 * ATTN_BLOCK))):
                    dkv_ref[pl.ds(start, ATTN_BLOCK), klo:klo + HEAD_DIM] += dkk[rows]
                    dkv_ref[pl.ds(start, ATTN_BLOCK), vlo:vlo + HEAD_DIM] += dvv[rows]
                dqw_ref[...] += jnp.sum(dqn * qhat, axis=0, keepdims=True)
                z = dqn * qw_v
                dqs = rq * (z - qhat * (_rowsum((z * qhat).astype(BF16), HEAD_DIM) * (1.0 / HEAD_DIM)))
                _unstack_heads(dq_scr.at[mine, :], dqs, g)
        dq_ref[...] = dq_scr[...].astype(BF16)

        @pl.when(step == nb // per - 1)
        def _():
            def one(c, dkw):
                rows = pl.ds(pl.multiple_of(c * chunk, chunk), chunk)
                for g in range(N_KV_HEADS):
                    lo = g * HEAD_DIM
                    k = kv_ref[rows, lo:lo + HEAD_DIM].astype(F32)
                    dx, dg_rows = _rms_bwd(dkv_ref[rows, lo:lo + HEAD_DIM], k, kw_ref[...])
                    dkv_ref[rows, lo:lo + HEAD_DIM] = dx
                    dkw = dkw + jnp.sum(dg_rows, axis=0, keepdims=True)
                return dkw

            dkw_ref[...] = lax.fori_loop(0, S // chunk, one, jnp.zeros((1, HEAD_DIM), F32))

    blk = pl.BlockSpec((per * ATTN_BLOCK, W), lambda n: (n, 0))
    whole_kv = pl.BlockSpec((S, KW), lambda n: (0, 0))
    vec = pl.BlockSpec((1, HEAD_DIM), lambda n: (0, 0))
    sk = pl.BlockSpec(sink_rows.shape, lambda n: (0, 0, 0))
    dsk = pl.BlockSpec((N_KV_HEADS, GQA_GROUP, 128), lambda n: (0, 0, 0))
    return pl.pallas_call(
        body, name="attn_bwd", grid=(nb // per,),
        in_specs=[blk, whole_kv, blk, vec, vec, sk, pl.BlockSpec(band.shape, lambda n: (0, 0))],
        out_specs=[blk, whole_kv, vec, vec, dsk],
        out_shape=[jax.ShapeDtypeStruct((S, W), BF16), jax.ShapeDtypeStruct((S, KW), F32),
                   jax.ShapeDtypeStruct((1, HEAD_DIM), F32), jax.ShapeDtypeStruct((1, HEAD_DIM), F32),
                   jax.ShapeDtypeStruct((N_KV_HEADS, GQA_GROUP, 128), F32)],
        scratch_shapes=[pltpu.VMEM((per * ATTN_BLOCK, W), F32)],
        compiler_params=_params(dimension_semantics=("arbitrary",)),
    )(q, kv, do, qw, kw, sink_rows, band)


def _pooled(xc, xprev, i):
    tm = xc.shape[0]
    xh = jnp.concatenate([jnp.where(i > 0, xprev, 0.0), xc], axis=0)
    t = lax.broadcasted_iota(jnp.int32, (tm, 1), 0) + i * tm
    out = []
    for gi, w in enumerate(POOL_WINDOWS):
        acc = xh[:, gi * POOL_GROUP:(gi + 1) * POOL_GROUP]
        sh = 1
        while sh < w:
            acc = acc + pltpu.roll(acc, sh, 0)
            sh *= 2
        cnt = jnp.minimum(t + 1, w).astype(F32)
        out.append(acc[POOL_HALO:, :] / cnt - xc[:, gi * POOL_GROUP:(gi + 1) * POOL_GROUP])
    return jnp.concatenate(out, axis=1)


def _pool_mix(pooled_b, pw_ref):
    return jnp.concatenate([_dot(pooled_b[:, gi * POOL_GROUP:(gi + 1) * POOL_GROUP], pw_ref[gi])
                            for gi in range(len(POOL_WINDOWS))], axis=1)


def _halo_specs(tm, width, S, after):
    per = tm // POOL_HALO
    last = S // POOL_HALO - 1
    if after:
        return pl.BlockSpec((POOL_HALO, width), lambda i: (jnp.minimum((i + 1) * per, last), 0))
    return pl.BlockSpec((POOL_HALO, width), lambda i: (jnp.maximum(i * per - 1, 0), 0))


def _mix_out(xp, attn, gl, bias, pw, pscale, wpot, wao, wo, h, tm=512):
    S, D = h.shape
    nt = S // tm
    PW = xp.shape[1]

    def body(xc_ref, xprev_ref, at_ref, gl_ref, bias_ref, pw_ref, ps_ref, wpo_ref, wao_ref, wo_ref, h_ref,
             ho_ref, bp_ref, ba_ref):
        i = pl.program_id(0)
        pooled = _pooled(xc_ref[...], xprev_ref[...], i).astype(BF16)
        ms = (_pool_mix(pooled, pw_ref) * ps_ref[...]).astype(BF16)
        bp = _dot_nt(ms, wpo_ref[...])
        ba = _dot(at_ref[...], wao_ref[...])
        bp_ref[...] = bp.astype(BF16)
        ba_ref[...] = ba.astype(BF16)
        gates = jax.nn.sigmoid(gl_ref[...].astype(F32) + bias_ref[...])
        merged = (gates[:, :D] * bp + gates[:, D:] * ba).astype(BF16)
        ho_ref[...] = h_ref[...] + _dot(merged, wo_ref[...])

    def row(width):
        return pl.BlockSpec((tm, width), lambda i: (i, 0))

    def whole(x):
        nd = x.ndim
        return pl.BlockSpec(x.shape, lambda i: (0,) * nd)

    return pl.pallas_call(
        body, name="mix_out", grid=(nt,),
        in_specs=[row(PW), _halo_specs(tm, PW, S, False), row(D), row(2 * D), whole(bias), whole(pw), whole(pscale),
                  whole(wpot), whole(wao), whole(wo), row(D)],
        out_specs=[row(D), row(D), row(D)],
        out_shape=[jax.ShapeDtypeStruct((S, D), F32), jax.ShapeDtypeStruct((S, D), BF16),
                   jax.ShapeDtypeStruct((S, D), BF16)],
        compiler_params=_params(dimension_semantics=("arbitrary",)),
    )(xp, xp, attn, gl, bias, pw, pscale, wpot, wao, wo, h)


def _mix_bwd_gate(dh, wo, bp, ba, gl, bias, tm=512):
    S, D = dh.shape
    nt = S // tm

    def body(dh_ref, wo_ref, bp_ref, ba_ref, gl_ref, bias_ref, dgl_ref, dbp_ref, dba_ref, dwo_ref, dbias_ref):
        i = pl.program_id(0)
        dhb = dh_ref[...].astype(BF16)
        dm = _dot_nt(dhb, wo_ref[...])
        gates = jax.nn.sigmoid(gl_ref[...].astype(F32) + bias_ref[...])
        gp, ga = gates[:, :D], gates[:, D:]
        bp_v = bp_ref[...].astype(F32)
        ba_v = ba_ref[...].astype(F32)
        merged = (gp * bp_v + ga * ba_v).astype(BF16)
        _acc(dwo_ref, _dot_tn(merged, dhb), i == 0)
        dbp_ref[...] = (dm * gp).astype(BF16)
        dba_ref[...] = (dm * ga).astype(BF16)
        dgl = jnp.concatenate([dm * bp_v * gp * (1.0 - gp), dm * ba_v * ga * (1.0 - ga)], axis=1)
        dgl_ref[...] = dgl.astype(BF16)
        _acc(dbias_ref, jnp.sum(dgl, axis=0, keepdims=True), i == 0)

    def row(width):
        return pl.BlockSpec((tm, width), lambda i: (i, 0))

    def whole(shape):
        return pl.BlockSpec(shape, lambda i: (0, 0))

    return pl.pallas_call(
        body, name="mix_bwd_gate", grid=(nt,),
        in_specs=[row(D), whole(wo.shape), row(D), row(D), row(2 * D), whole(bias.shape)],
        out_specs=[row(2 * D), row(D), row(D), whole((D, D)), whole((1, 2 * D))],
        out_shape=[jax.ShapeDtypeStruct((S, 2 * D), BF16), jax.ShapeDtypeStruct((S, D), BF16),
                   jax.ShapeDtypeStruct((S, D), BF16), jax.ShapeDtypeStruct((D, D), F32),
                   jax.ShapeDtypeStruct((1, 2 * D), F32)],
        compiler_params=_params(dimension_semantics=("arbitrary",)),
    )(dh, wo, bp, ba, gl, bias)


def _mix_bwd_branch(dbp, dba, attn, xp, pw, pscale, wpot, wao, tm=1024):
    S, D = dbp.shape
    nt = S // tm
    PW = xp.shape[1]
    NG = len(POOL_WINDOWS)

    def body(dbp_ref, dba_ref, at_ref, xc_ref, xprev_ref, pw_ref, ps_ref, wpo_ref, wao_ref,
             dat_ref, dpl_ref, dwao_ref, dwpo_ref, dpw_ref, dps_ref):
        i = pl.program_id(0)
        dba_v = dba_ref[...]
        dbp_v = dbp_ref[...]
        _acc(dwao_ref, _dot_tn(at_ref[...], dba_v), i == 0)
        dat_ref[...] = _dot_nt(dba_v, wao_ref[...]).astype(BF16)
        pooled = _pooled(xc_ref[...], xprev_ref[...], i).astype(BF16)
        mixed = _pool_mix(pooled, pw_ref)
        ps = ps_ref[...]
        _acc(dwpo_ref, _dot_tn(dbp_v, (mixed * ps).astype(BF16)), i == 0)
        dms = _dot(dbp_v, wpo_ref[...])
        _acc(dps_ref, jnp.sum(dms * mixed, axis=0, keepdims=True), i == 0)
        dmixed = (dms * ps).astype(BF16)
        dpooled = []
        for gi in range(NG):
            cols = slice(gi * POOL_GROUP, (gi + 1) * POOL_GROUP)
            _acc(dpw_ref.at[gi], _dot_tn(pooled[:, cols], dmixed[:, cols]), i == 0)
            dpooled.append(_dot_nt(dmixed[:, cols], pw_ref[gi]))
        dpl_ref[...] = jnp.concatenate(dpooled, axis=1)

    def row(width):
        return pl.BlockSpec((tm, width), lambda i: (i, 0))

    def whole(shape):
        nd = len(shape)
        return pl.BlockSpec(shape, lambda i: (0,) * nd)

    return pl.pallas_call(
        body, name="mix_bwd_branch", grid=(nt,),
        in_specs=[row(D), row(D), row(D), row(PW), _halo_specs(tm, PW, S, False), whole(pw.shape), whole(pscale.shape),
                  whole(wpot.shape), whole(wao.shape)],
        out_specs=[row(D), row(PW), whole((D, D)), whole((D, PW)), whole(pw.shape), whole((1, PW))],
        out_shape=[jax.ShapeDtypeStruct((S, D), BF16), jax.ShapeDtypeStruct((S, PW), F32),
                   jax.ShapeDtypeStruct((D, D), F32), jax.ShapeDtypeStruct((D, PW), F32),
                   jax.ShapeDtypeStruct(pw.shape, F32), jax.ShapeDtypeStruct((1, PW), F32)],
        compiler_params=_params(dimension_semantics=("arbitrary",)),
    )(dbp, dba, attn, xp, xp, pw, pscale, wpot, wao)


def _pool_bwd(dpooled, tm=512):
    S, PW = dpooled.shape
    nt = S // tm

    def body(dc_ref, dnext_ref, dxp_ref):
        i = pl.program_id(0)
        dc = dc_ref[...]
        dh = jnp.concatenate([dc, jnp.where(i < nt - 1, dnext_ref[...], 0.0)], axis=0)
        rows = tm + POOL_HALO
        t = lax.broadcasted_iota(jnp.int32, (rows, 1), 0) + i * tm
        out = []
        for gi, w in enumerate(POOL_WINDOWS):
            cols = slice(gi * POOL_GROUP, (gi + 1) * POOL_GROUP)
            acc = dh[:, cols] / jnp.minimum(t + 1, w).astype(F32)
            sh = 1
            while sh < w:
                acc = acc + pltpu.roll(acc, rows - sh, 0)
                sh *= 2
            out.append(acc[:tm, :] - dc[:, cols])
        dxp_ref[...] = jnp.concatenate(out, axis=1).astype(BF16)

    return pl.pallas_call(
        body, name="pool_bwd", grid=(nt,),
        in_specs=[pl.BlockSpec((tm, PW), lambda i: (i, 0)), _halo_specs(tm, PW, S, True)],
        out_specs=pl.BlockSpec((tm, PW), lambda i: (i, 0)), out_shape=jax.ShapeDtypeStruct((S, PW), BF16),
        compiler_params=_params(dimension_semantics=("arbitrary",)),
    )(dpooled, dpooled)


def _in_bwd_w(pieces, u, tm=1024):
    S, D = u.shape
    nt = S // tm
    NW = sum(width for _, width in SEGMENTS)

    def body(*refs):
        piece_refs, u_ref, dw_hbm, acc, sem = refs[:len(SEGMENTS)], refs[len(SEGMENTS)], refs[len(SEGMENTS) + 1], refs[-2], refs[-1]
        i = pl.program_id(0)
        u_t = u_ref[...]
        for (off, width), ref in zip(SEGMENTS, piece_refs):
            for lo in range(0, width, D):
                hi = min(lo + D, width)
                _acc(acc.at[off + lo:off + hi, :], _dot_tn(ref[:, lo:hi].astype(BF16), u_t), i == 0)

        @pl.when(i == nt - 1)
        def _():
            cp = pltpu.make_async_copy(acc, dw_hbm, sem)
            cp.start()
            cp.wait()

    return pl.pallas_call(
        body, name="in_bwd_w", grid=(nt,),
        in_specs=[pl.BlockSpec((tm, width), lambda i: (i, 0)) for _, width in SEGMENTS] + [pl.BlockSpec((tm, D), lambda i: (i, 0))],
        out_specs=pl.BlockSpec(memory_space=pl.ANY), out_shape=jax.ShapeDtypeStruct((NW, D), F32),
        scratch_shapes=[pltpu.VMEM((NW, D), F32), pltpu.SemaphoreType.DMA],
        compiler_params=_params(dimension_semantics=("arbitrary",)),
    )(*pieces, u)


def _in_bwd_x(pieces, wint, h, dh, gnorm, tm=512):
    S, D = h.shape
    nt = S // tm

    def body(*refs):
        piece_refs = refs[:len(SEGMENTS)]
        w_ref, h_ref, dh_ref, g_ref, dx_ref, dxh_ref, dg_ref = refs[len(SEGMENTS):]
        i = pl.program_id(0)
        du = jnp.zeros((tm, D), F32)
        for (off, width), ref in zip(SEGMENTS, piece_refs):
            du = du + _dot(ref[...].astype(BF16), w_ref[off:off + width, :])
        dx, dg_rows = _rms_bwd(du, h_ref[...], g_ref[...])
        out = dh_ref[...] + dx
        dx_ref[...] = out
        dxh_ref[...] = (0.5 * out).astype(BF16)
        _acc(dg_ref, jnp.sum(dg_rows, axis=0, keepdims=True), i == 0)

    def row(width):
        return pl.BlockSpec((tm, width), lambda i: (i, 0))

    vec = pl.BlockSpec((1, D), lambda i: (0, 0))
    return pl.pallas_call(
        body, name="in_bwd_x", grid=(nt,),
        in_specs=[row(width) for _, width in SEGMENTS] + [pl.BlockSpec(wint.shape, lambda i: (0, 0)), row(D), row(D), vec],
        out_specs=[row(D), row(D), vec],
        out_shape=[jax.ShapeDtypeStruct((S, D), F32), jax.ShapeDtypeStruct((S, D), BF16), jax.ShapeDtypeStruct((1, D), F32)],
        compiler_params=_params(dimension_semantics=("arbitrary",)),
    )(*pieces, wint, h, dh, gnorm)


def _row_tile(rows):
    for t in (512, 480, 352, 256, 128, 64, 32, 16, 8):
        if rows % t == 0:
            return t
    return rows


def _adamw_update(w, g, m, v):
    mn = ADAM_B1 * m + (1.0 - ADAM_B1) * g
    vn = ADAM_B2 * v + (1.0 - ADAM_B2) * (g * g)
    m_hat = mn / (1.0 - ADAM_B1 ** ADAM_STEP)
    v_hat = vn / (1.0 - ADAM_B2 ** ADAM_STEP)
    return -ADAM_LR * (m_hat / (jnp.sqrt(v_hat) + ADAM_EPS) + ADAM_WD * w), mn, vn


def _adamw_small(ws, gs, ms, vs):
    n = len(ws)

    def rows(a):
        return a.reshape(-1, a.shape[-1]) if a.ndim > 1 else a.reshape(1, -1)

    def body(*refs):
        for k in range(n):
            w_ref, g_ref, m_ref, v_ref, d_ref, mo_ref, vo_ref = refs[k::n]
            d_ref[...], mo_ref[...], vo_ref[...] = _adamw_update(w_ref[...], g_ref[...], m_ref[...], v_ref[...])

    vmem = pl.BlockSpec(memory_space=pltpu.VMEM)
    out = pl.pallas_call(
        body, name="adamw_small", in_specs=[vmem] * (4 * n), out_specs=[vmem] * (3 * n),
        out_shape=[jax.ShapeDtypeStruct(rows(w).shape, F32) for _ in range(3) for w in ws], compiler_params=_params(),
    )(*[rows(a) for a in ws + gs + ms + vs])
    return [[o.reshape(w.shape) for o, w in zip(out[k * n:(k + 1) * n], ws)] for k in range(3)]


def _adamw(w, g, m, v, token, name, turned=False):
    R, C = w.shape
    tr = R if turned else _row_tile(R)

    def body(w_ref, g_ref, m_ref, v_ref, _, d_ref, mo_ref, vo_ref, go_ref):
        gv = g_ref[...].T if turned else g_ref[...]
        go_ref[...] = gv
        d_ref[...], mo_ref[...], vo_ref[...] = _adamw_update(w_ref[...], gv, m_ref[...], v_ref[...])

    blk = pl.BlockSpec((tr, C), lambda i: (i, 0))
    g_blk = pl.BlockSpec((C, R), lambda i: (0, 0)) if turned else blk
    sh = jax.ShapeDtypeStruct((R, C), F32)
    return pl.pallas_call(
        body, name=name, grid=(R // tr,), in_specs=[blk, g_blk, blk, blk, TOKEN_SPEC], out_specs=[blk] * 4, out_shape=[sh] * 4,
        compiler_params=_params(dimension_semantics=("arbitrary",)),
    )(w, g, m, v, token)


def _cast_place(w, place, name, turned=False, after=()):
    R, C = w.shape[::-1] if turned else w.shape
    tr = R if turned else _row_tile(R)
    per = R // tr

    def body(p_ref, w_ref, *rest):
        rest[-1][...] = (w_ref[...].T if turned else w_ref[...]).astype(BF16)

    grid_spec = pltpu.PrefetchScalarGridSpec(
        num_scalar_prefetch=1, grid=(per,),
        in_specs=[pl.BlockSpec((C, R), lambda i, p: (0, 0)) if turned else pl.BlockSpec((tr, C), lambda i, p: (i, 0))]
        + [TOKEN_SPEC] * len(after),
        out_specs=pl.BlockSpec((tr, C), lambda i, p: (p[0] * per + i, 0)))
    return pl.pallas_call(
        body, name=name, grid_spec=grid_spec, out_shape=jax.ShapeDtypeStruct((N_SHARDS * R, C), BF16),
        compiler_params=_params(dimension_semantics=("arbitrary",)),
    )(place, w, *after)


def _parked(w, m, idx, first, last):
    return jnp.where(m == w, idx, jnp.where(m < w, first, last))


def _sum_halves(g4s, recvs, place, name):
    M = len(g4s)
    NS, R, C = g4s[0].shape
    hr = R // 2
    tr = _row_tile(hr)
    per = hr // tr

    def body(p_ref, *refs):
        m = pl.program_id(0)
        for w in range(M):
            @pl.when(m == w)
            def _(w=w):
                refs[2 * M + w][...] = (refs[w][...] + refs[M + w][...]).astype(BF16)

    def spec(w, mine):
        def index(m, s, i, p):
            row = _parked(w, m, i, 0, per - 1)
            return _parked(w, m, s, 0, NS - 1), (p[1] * per + row if mine else row), 0

        return pl.BlockSpec((1, tr, C), index)

    grid_spec = pltpu.PrefetchScalarGridSpec(
        num_scalar_prefetch=1, grid=(M, NS, per),
        in_specs=[spec(w, True) for w in range(M)] + [spec(w, False) for w in range(M)],
        out_specs=[spec(w, False) for w in range(M)])
    return pl.pallas_call(
        body, name=name, grid_spec=grid_spec, out_shape=[jax.ShapeDtypeStruct((NS, hr, C), BF16)] * M,
        compiler_params=_params(dimension_semantics=("arbitrary",) * 3),
    )(place, *g4s, *recvs)


def _sum_quarters(h4s, recv3s, place, name):
    M = len(h4s)
    NS, hr, C = h4s[0].shape
    tr = _row_tile(hr)
    per = hr // tr

    def body(p_ref, *refs):
        m = pl.program_id(0)
        for w in range(M):
            @pl.when(m == w)
            def _(w=w):
                acc = refs[w][0].astype(F32)
                for k in range(N_SHARDS - 1):
                    acc = acc + refs[M + w][k].astype(F32)
                refs[2 * M + w][...] = acc

    def row(w, m, i):
        return _parked(w, m, i, 0, per - 1)

    grid_spec = pltpu.PrefetchScalarGridSpec(
        num_scalar_prefetch=1, grid=(M, per),
        in_specs=[pl.BlockSpec((1, tr, C), lambda m, i, p, w=w: (p[0], row(w, m, i), 0)) for w in range(M)]
        + [pl.BlockSpec((N_SHARDS - 1, tr, C), lambda m, i, p, w=w: (0, row(w, m, i), 0)) for w in range(M)],
        out_specs=[pl.BlockSpec((tr, C), lambda m, i, p, w=w: (p[1] * per + row(w, m, i), 0)) for w in range(M)])
    return pl.pallas_call(
        body, name=name, grid_spec=grid_spec, out_shape=[jax.ShapeDtypeStruct((2 * hr, C), F32)] * M,
        compiler_params=_params(dimension_semantics=("arbitrary",) * 2),
    )(place, *h4s, *recv3s)


def _by_shape(arrays):
    groups = {}
    for k, a in enumerate(arrays):
        groups.setdefault(a.shape, []).append(k)
    return list(groups.values())


def _place():
    x, y, c = lax.axis_index("x"), lax.axis_index("y"), lax.axis_index("c")
    chips = [(1 - x, y), (x, 1 - y), (1 - x, 1 - y)]
    return x, y, c, chips


HBM_SPEC = pl.BlockSpec(memory_space=pltpu.HBM)
SEM_SPEC = pl.BlockSpec(memory_space=pltpu.SEMAPHORE)
DATAFLOW = pltpu.SideEffectType.DATAFLOW_SIDE_EFFECTING


def _hbm(a):
    return pltpu.with_memory_space_constraint(a, pltpu.HBM)


class InFlight(NamedTuple):
    send_sem: jax.Array
    recv_sem: jax.Array
    bufs: list
    plan: Callable
    token: jax.Array


class Leg(NamedTuple):
    bufs: list
    landing: Optional[InFlight]
    plan: Optional[Callable]
    n_copies: int


def _send(bufs, plan, n_copies):
    return Leg([_hbm(b) for b in bufs], None, plan, n_copies)


def _land(flight):
    return Leg(flight.bufs, flight, None, 0)


def _forward(flight, plan, n_copies):
    return Leg(flight.bufs, flight, plan, n_copies)


def _wait_all(plan, refs, send_ref, recv_ref):
    for k, (src, dst, dev) in enumerate(plan(refs)):
        cp = pltpu.make_async_remote_copy(src_ref=src, dst_ref=dst, send_sem=send_ref.at[k], recv_sem=recv_ref.at[k],
                                          device_id=dev, device_id_type=MESH)
        cp.wait_send()
        cp.wait_recv()


def _start_all(plan, refs, send_ref, recv_ref):
    for k, (src, dst, dev) in enumerate(plan(refs)):
        pltpu.make_async_remote_copy(src_ref=src, dst_ref=dst, send_sem=send_ref.at[k], recv_sem=recv_ref.at[k],
                                     device_id=dev, device_id_type=MESH).start()


def _comm(name, legs, after):
    after = list(after) if isinstance(after, (list, tuple)) else [after]
    ins, in_specs, out_shape, out_specs, aliases, first = [], [], [], [], {}, []
    for leg in legs:
        first.append((len(ins), len(out_shape)))
        for b in leg.bufs:
            aliases[len(ins)] = len(out_shape)
            ins.append(b)
            in_specs.append(HBM_SPEC)
            out_shape.append(pltpu.HBM(b.shape, b.dtype))
            out_specs.append(HBM_SPEC)
        if leg.landing is not None:
            ins += [leg.landing.send_sem, leg.landing.recv_sem]
            in_specs += [SEM_SPEC, SEM_SPEC]
        if leg.plan is not None:
            out_shape += [pltpu.SemaphoreType.DMA((leg.n_copies,))] * 2
            out_specs += [SEM_SPEC, SEM_SPEC]
    starts = any(leg.plan is not None for leg in legs)
    if starts:
        out_shape.append(jax.ShapeDtypeStruct((8, 128), F32))
        out_specs.append(pl.BlockSpec(memory_space=pltpu.VMEM))
    n_in = len(ins) + len(after)

    def body(*refs):
        outs = refs[n_in:]
        for leg, (i, o) in zip(legs, first):
            nb = len(leg.bufs)
            bufs = refs[i:i + nb]
            if leg.landing is not None:
                _wait_all(leg.landing.plan, bufs, refs[i + nb], refs[i + nb + 1])
            if leg.plan is not None:
                _start_all(leg.plan, bufs, outs[o + nb], outs[o + nb + 1])
        if starts:
            outs[-1][...] = jnp.zeros_like(outs[-1])

    out = pl.pallas_call(
        body, name=name, in_specs=in_specs + [TOKEN_SPEC] * len(after), out_shape=out_shape, out_specs=out_specs,
        input_output_aliases=aliases, compiler_params=pltpu.CompilerParams(has_side_effects=DATAFLOW),
    )(*ins, *after)
    results = []
    for leg, (_, o) in zip(legs, first):
        nb = len(leg.bufs)
        bufs = list(out[o:o + nb])
        results.append(bufs if leg.plan is None else InFlight(out[o + nb], out[o + nb + 1], bufs, leg.plan, out[-1]))
    return results


def _half_rows(buf, chip, core):
    hr = buf.shape[0] // (2 * N_SHARDS)
    return buf.at[pl.ds(pl.multiple_of((2 * chip + core) * hr, 16), hr)]


def _gather_ici_plan(bufs):
    x, y, c, chips = _place()
    return [(_half_rows(b, 2 * x + y, c), _half_rows(b, 2 * x + y, c), (px, py, c)) for b in bufs for px, py in chips]


def _gather_d2d_plan(bufs):
    x, y, c, chips = _place()
    return [(_half_rows(b, 2 * px + py, c), _half_rows(b, 2 * px + py, c), (x, y, 1 - c)) for b in bufs for px, py in chips]


def _swap_plan(bufs):
    x, y, c, _ = _place()
    n = len(bufs) // 2
    copies = []
    for g, land in zip(bufs[:n], bufs[n:]):
        hr = g.shape[1] // 2
        copies.append((g.at[:, pl.ds(pl.multiple_of((1 - c) * hr, 8), hr)], land, (x, y, 1 - c)))
    return copies


def _exchange_plan(bufs):
    x, y, c, chips = _place()
    n = len(bufs) // 2
    return [(h.at[2 * px + py], land.at[k], (px, py, c))
            for h, land in zip(bufs[:n], bufs[n:]) for k, (px, py) in enumerate(chips)]


def _share_plan(bufs):
    x, y, c, _ = _place()
    copies = []
    for buf in bufs:
        hr = buf.shape[0] // 2
        mine = buf.at[pl.ds(pl.multiple_of(c * hr, 8), hr)]
        copies.append((mine, mine, (x, y, 1 - c)))
    return copies


N_DEVICES = 8


def _slot_place(vec, place):
    R, C = vec.shape

    def body(p_ref, v_ref, o_ref):
        o_ref[0] = v_ref[...]

    grid_spec = pltpu.PrefetchScalarGridSpec(
        num_scalar_prefetch=1, grid=(1,), in_specs=[pl.BlockSpec((R, C), lambda i, p: (0, 0))],
        out_specs=pl.BlockSpec((1, R, C), lambda i, p: (2 * p[0] + p[1], 0, 0)))
    return pl.pallas_call(
        body, name="slot_place", grid_spec=grid_spec, out_shape=jax.ShapeDtypeStruct((N_DEVICES, R, C), F32),
        compiler_params=_params(dimension_semantics=("arbitrary",)),
    )(place, vec)


def _slots_plan(bufs):
    x, y, c, _ = _place()
    mine = bufs[0].at[4 * x + 2 * y + c]
    return [(mine, mine, (x ^ (r >> 2), y ^ ((r >> 1) & 1), c ^ (r & 1))) for r in range(1, N_DEVICES)]


def _sum_slots(slots):
    _, R, C = slots.shape

    def body(s_ref, o_ref):
        acc = s_ref[0]
        for d in range(1, N_DEVICES):
            acc = acc + s_ref[d]
        o_ref[...] = acc

    return pl.pallas_call(
        body, name="sum_slots", in_specs=[pl.BlockSpec(memory_space=pltpu.VMEM)],
        out_specs=pl.BlockSpec(memory_space=pltpu.VMEM), out_shape=jax.ShapeDtypeStruct((R, C), F32),
        compiler_params=_params(),
    )(slots)


SMALL = ("ffn1_norm", "mix_norm", "pool_w", "pool_scale", "q_norm", "k_norm", "sinks", "gate_bias", "ffn2_norm")
SMALL_COLS = 1024
FFN1 = ("ffn1_w_gate", "ffn1_w_up", "ffn1_w_down")
MIXER = ("w_in", "w_pool_out", "w_attn_out", "w_out")
FFN2 = ("ffn2_w_gate", "ffn2_w_up", "ffn2_w_down")
LARGE = FFN1 + MIXER + FFN2
TRANSPOSED = ("ffn1_w_gate", "ffn1_w_up", "w_in", "w_pool_out", "ffn2_w_gate", "ffn2_w_up")
TURNED = ("w_pool_out",)
WEIGHTS = ("ffn1_norm", "ffn1_w_gate", "ffn1_w_up", "ffn1_w_down", "mix_norm", "w_in", "pool_w", "pool_scale",
           "w_pool_out", "q_norm", "k_norm", "sinks", "w_attn_out", "gate_bias", "w_out", "ffn2_norm",
           "ffn2_w_gate", "ffn2_w_up", "ffn2_w_down")


def _pack_small(parts):
    flat = jnp.concatenate([p.reshape(-1) for p in parts])
    rows = -(-flat.shape[0] // (8 * SMALL_COLS)) * 8
    return jnp.pad(flat, (0, rows * SMALL_COLS - flat.shape[0])).reshape(rows, SMALL_COLS)


def _unpack_small(packed, like):
    flat = packed.reshape(-1)
    out, off = [], 0
    for p in like:
        out.append(flat[off:off + p.size].reshape(p.shape))
        off += p.size
    return out, flat[off]


def _tie(small, token):
    return small + token[0, 0]


class Reduction:
    def __init__(self, names, place):
        self.names, self.place = names, place

    def swap(self, grads):
        grads = [g.reshape(N_SHARDS, -1, g.shape[-1]) for g in grads]
        lands = [lax.empty((g.shape[0], g.shape[1] // 2, g.shape[2]), g.dtype) for g in grads]
        return _send(grads + lands, _swap_plan, len(grads))

    def exchange(self, swapped):
        n = len(self.names)
        halves = self._per_shape(_sum_halves, "sum_halves", swapped[:n], swapped[n:])
        lands = [lax.empty((N_SHARDS - 1,) + h.shape[1:], h.dtype) for h in halves]
        return _send(halves + lands, _exchange_plan, (N_SHARDS - 1) * n)

    def share(self, exchanged):
        n = len(self.names)
        return _send(self._per_shape(_sum_quarters, "sum_quarters", exchanged[:n], exchanged[n:]), _share_plan, n)

    def named(self, shared):
        return dict(zip(self.names, shared))

    def _per_shape(self, add, stage, mine, received):
        out = [None] * len(mine)
        for idx in _by_shape(mine):
            sums = add([mine[k] for k in idx], [received[k] for k in idx], self.place, f"{stage}_{self.names[idx[0]]}")
            for k, v in zip(idx, sums):
                out[k] = v
        return out


def _row_form(name, a):
    return a.T if name in TRANSPOSED else a


def _shard_form(name, a):
    return a if name in TURNED else _row_form(name, a)


def kernel(x, ffn1_norm, ffn1_w_gate, ffn1_w_up, ffn1_w_down, mix_norm, w_in, pool_w, pool_scale, w_pool_out, q_norm, k_norm, sinks, w_attn_out, gate_bias, w_out, ffn2_norm, ffn2_w_gate, ffn2_w_up, ffn2_w_down, loss_target, m_ffn1_norm, m_ffn1_w_gate, m_ffn1_w_up, m_ffn1_w_down, m_mix_norm, m_w_in, m_pool_w, m_pool_scale, m_w_pool_out, m_q_norm, m_k_norm, m_sinks, m_w_attn_out, m_gate_bias, m_w_out, m_ffn2_norm, m_ffn2_w_gate, m_ffn2_w_up, m_ffn2_w_down, v_ffn1_norm, v_ffn1_w_gate, v_ffn1_w_up, v_ffn1_w_down, v_mix_norm, v_w_in, v_pool_w, v_pool_scale, v_w_pool_out, v_q_norm, v_k_norm, v_sinks, v_w_attn_out, v_gate_bias, v_w_out, v_ffn2_norm, v_ffn2_w_gate, v_ffn2_w_up, v_ffn2_w_down):
    args = dict(locals())
    wts = {n: _shard_form(n, args[n]) for n in WEIGHTS}
    mom = {n: _shard_form(n, args["m_" + n]) for n in WEIGHTS}
    var = {n: _shard_form(n, args["v_" + n]) for n in WEIGHTS}
    shard = 2 * lax.axis_index("x") + lax.axis_index("y")
    place = jnp.stack([shard, lax.axis_index("c")]).astype(jnp.int32)

    xs, target = x[0], loss_target[0]
    D = xs.shape[1]
    g1 = wts["ffn1_norm"].reshape(1, D)
    gm = wts["mix_norm"].reshape(1, D)
    g2 = wts["ffn2_norm"].reshape(1, D)
    qw = wts["q_norm"].reshape(1, HEAD_DIM)
    kw = wts["k_norm"].reshape(1, HEAD_DIM)
    bias = wts["gate_bias"].reshape(1, 2 * D)
    pscale = wts["pool_scale"].reshape(1, -1)
    pw = wts["pool_w"].astype(BF16)
    sink_rows = jnp.broadcast_to(jnp.repeat(wts["sinks"], ATTN_BLOCK).reshape(N_KV_HEADS, GQA_GROUP * ATTN_BLOCK, 1),
                                 (N_KV_HEADS, GQA_GROUP * ATTN_BLOCK, 128))

    gate1, up1, down1 = FFN1[:1], FFN1[1:2], FFN1[2:]
    mix_in, mix_out = MIXER[:1], MIXER[1:]

    def over_ici(names):
        return _send([placed[n] for n in names], _gather_ici_plan, 3 * len(names))

    def to_sibling(flight):
        return _forward(flight, _gather_d2d_plan, len(flight.bufs) * 3)

    placed = {n: _cast_place(wts[n], place, "cast_" + n) for n in gate1}
    gate1_ici, = _comm("gather_gate1_ici", [over_ici(gate1)], place)
    placed.update({n: _cast_place(wts[n], place, "cast_" + n) for n in up1 + down1})
    up1_ici, down1_ici = _comm("gather_ffn1_ici", [over_ici(up1), over_ici(down1)], gate1_ici.token)
    placed.update({n: _cast_place(wts[n], place, "cast_" + n, turned=n in TURNED) for n in MIXER})
    band = _band_bias()
    shadow = [sink_rows, pw, band, down1_ici.token]
    gate1_d2d, in_ici, out_ici = _comm(
        "gather_mix_ici", [to_sibling(gate1_ici), over_ici(mix_in), over_ici(mix_out)], shadow)
    w = dict(zip(gate1, _comm("gather_gate1_wait", [_land(gate1_d2d)], gate1_d2d.token)[0]))

    n1, a1 = _ffn_gate(xs, g1, w["ffn1_w_gate"], "ffn1_gate")
    placed.update({n: _cast_place(wts[n], place, "cast_" + n, after=[a1]) for n in FFN2})
    up1_d2d, ffn2_ici, down1_d2d = _comm(
        "gather_ffn1_d2d", [to_sibling(up1_ici), over_ici(FFN2), to_sibling(down1_ici)], a1)
    w.update(zip(up1, _comm("gather_up1_wait", [_land(up1_d2d)], up1_d2d.token)[0]))
    p1, q1, s1 = _ffn_hidden(n1, a1, w["ffn1_w_up"], "ffn1_hidden")
    w.update(zip(down1, _comm("gather_down1_wait", [_land(down1_d2d)], s1)[0]))
    h1 = _ffn_down(xs, s1, w["ffn1_w_down"], "ffn1_down")
    in_d2d, out_d2d = _comm("gather_mix_d2d", [to_sibling(in_ici), to_sibling(out_ici)], h1)
    w.update(zip(mix_in, _comm("gather_in_wait", [_land(in_d2d)], in_d2d.token)[0]))
    u, xp, q, kv, gl = _mix_proj(h1, gm, w["w_in"])
    attn = _attn_fwd(q, kv, qw, kw, sink_rows, band)
    ffn2_d2d, landed = _comm("gather_ffn2_d2d", [to_sibling(ffn2_ici), _land(out_d2d)], attn)
    w.update(zip(mix_out, landed))
    h2, bp, ba = _mix_out(xp, attn, gl, bias, pw, pscale, w["w_pool_out"], w["w_attn_out"], w["w_out"], h1)
    w.update(zip(FFN2, _comm("gather_ffn2_wait", [_land(ffn2_d2d)], h2)[0]))
    n2, p2, q2, s2 = _ffn_up(h2, g2, w["ffn2_w_gate"], w["ffn2_w_up"], "ffn2_up")
    dy, dyh, loss = _ffn_down_loss(h2, s2, w["ffn2_w_down"], target, "ffn2_down_loss")

    gw, gs = {}, {}
    da, db = _ffn_bwd_h(dyh, p2, q2, w["ffn2_w_down"], ffn2_d2d.token, "ffn2_bwd_h")
    gw["ffn2_w_gate"], gw["ffn2_w_up"] = _xty(da, n2, "ffn2_dw_gate"), _xty(db, n2, "ffn2_dw_up")
    gw["ffn2_w_down"] = _xty(s2, dyh, "ffn2_dw_down")
    red2 = Reduction(FFN2, place)
    swap, = _comm("reduce_ffn2_swap", [red2.swap([gw[n] for n in FFN2])], da)
    dh2, gs["ffn2_norm"] = _ffn_bwd_x(da, db, w["ffn2_w_gate"], w["ffn2_w_up"], h2, dy, _tie(g2, swap.token), "ffn2_bwd_x")
    swapped, = _comm("reduce_ffn2_swap_wait", [_land(swap)], dh2)
    exchange, = _comm("reduce_ffn2_exchange", [red2.exchange(swapped)], place)
    dgl, dbp, dba, gw["w_out"], gs["gate_bias"] = _mix_bwd_gate(dh2, w["w_out"], bp, ba, gl, _tie(bias, exchange.token))
    dattn, dpooled, gw["w_attn_out"], gw["w_pool_out"], gs["pool_w"], gs["pool_scale"] = _mix_bwd_branch(
        dbp, dba, attn, xp, pw, pscale, w["w_pool_out"], w["w_attn_out"])
    dq, dkv, gs["q_norm"], gs["k_norm"], dsk = _attn_bwd(q, kv, dattn, qw, kw, sink_rows, band)
    gs["sinks"] = dsk[:, :, 0]
    exchanged, = _comm("reduce_ffn2_exchange_wait", [_land(exchange)], dq)
    share2, = _comm("reduce_ffn2_share", [red2.share(exchanged)], place)
    pieces = (_pool_bwd(dpooled), dq, dkv, dgl)
    gw["w_in"] = _in_bwd_w(pieces, u)
    redm = Reduction(MIXER, place)
    swap, shared = _comm("reduce_mix_swap", [redm.swap([gw[n] for n in MIXER]), _land(share2)], share2.token)
    grads = red2.named(shared)
    dh1, dh1h, gs["mix_norm"] = _in_bwd_x(pieces, w["w_in"], h1, dh2, _tie(gm, swap.token))
    red1, swaps = Reduction(FFN1, place), {}
    dw_down = _xty(s1, dh1h, "ffn1_dw_down")
    swaps["ffn1_w_down"], swapped = _comm("reduce_mix_swap_wait", [red1.swap([dw_down]), _land(swap)], dh1)
    exchange, = _comm("reduce_mix_exchange", [redm.exchange(swapped)], place)
    da, db = _ffn_bwd_h(dh1h, p1, q1, w["ffn1_w_down"], exchange.token, "ffn1_bwd_h")
    dw_up = _xty(db, n1, "ffn1_dw_up")
    swaps["ffn1_w_up"], = _comm("reduce_ffn1_swap_up", [red1.swap([dw_up])], place)
    dw_gate = _xty(da, n1, "ffn1_dw_gate", after=[swaps["ffn1_w_up"].token])
    swaps["ffn1_w_gate"], exchanged = _comm("reduce_mix_exchange_wait", [red1.swap([dw_gate]), _land(exchange)], place)
    sharem, *landed = _comm("reduce_mix_share", [redm.share(exchanged)] + [_land(swaps[n]) for n in FFN1], place)
    exchange, = _comm("reduce_ffn1_exchange", [red1.exchange([l[0] for l in landed] + [l[1] for l in landed])], place)
    grad_x, gs["ffn1_norm"] = _ffn_bwd_x(da, db, w["ffn1_w_gate"], w["ffn1_w_up"], xs, dh1, _tie(g1, exchange.token), "ffn1_bwd_x")

    small_parts = [gs[n] for n in SMALL] + [loss[0, 0].reshape(1)]
    slots, shared = _comm(
        "gather_small", [_send([_slot_place(_pack_small(small_parts), place)], _slots_plan, N_DEVICES - 1), _land(sharem)],
        grad_x)
    grads.update(redm.named(shared))
    delta, new_m, new_v = {}, {}, {}

    def adamw(names, token):
        for n in names:
            delta[n], new_m[n], new_v[n], grads[n] = _adamw(wts[n], grads[n], mom[n], var[n], token, "adamw_" + n, n in TURNED)

    early, late = FFN2 + MIXER[:-1], MIXER[-1:]
    adamw(early, slots.token)
    exchanged, = _comm("reduce_ffn1_exchange_wait", [_land(exchange)], [delta[n] for n in early])
    share1, landed = _comm("reduce_ffn1_share", [red1.share(exchanged), _land(slots)], place)
    summed = _sum_slots(landed[0])
    small_grads, loss_sum = _unpack_small(summed, [wts[n] for n in SMALL])
    grads.update(dict(zip(SMALL, small_grads)))
    for out, vals in zip((delta, new_m, new_v), _adamw_small(*[[d[n] for n in SMALL] for d in (wts, grads, mom, var)])):
        out.update(dict(zip(SMALL, vals)))
    adamw(late, share1.token)
    results = {(k, n): _shard_form(n, d[n]) for k, d in enumerate((grads, delta, new_m, new_v)) for n in SMALL + FFN2 + MIXER}
    shared, = _comm("reduce_ffn1_share_wait", [_land(share1)], [results[k, n] for k in range(4) for n in SMALL + late])
    grads.update(red1.named(shared))
    adamw(FFN1, share1.token)
    results.update({(k, n): _shard_form(n, d[n]) for k, d in enumerate((grads, delta, new_m, new_v)) for n in FFN1})
    return (loss_sum, grad_x[None], *[results[k, n] for k in range(4) for n in WEIGHTS])
```

```python
from typing import Callable, NamedTuple, Optional

import jax
import jax.numpy as jnp
from jax import lax
from jax.experimental import pallas as pl
from jax.experimental.pallas import tpu as pltpu

F32 = jnp.float32
BF16 = jnp.bfloat16
RMS_EPS = 1e-6
POOL_WINDOWS = (2, 4, 8, 16)
POOL_GROUP = 128
POOL_HALO = 16
HEAD_DIM = 64
GQA_GROUP = 8
N_KV_HEADS = 2
ATTN_BLOCK = 128
SCALE = HEAD_DIM ** -0.5
NEG = -1e30
N_SHARDS = 4
ADAM_LR, ADAM_B1, ADAM_B2, ADAM_EPS, ADAM_WD, ADAM_STEP = 0.001, 0.9, 0.999, 1e-08, 0.01, 10
VMEM_LIMIT = 56 * 1024 * 1024
MESH = pl.DeviceIdType.MESH
SEG_POOL, SEG_Q, SEG_KV, SEG_GATE = (0, 512), (512, 1024), (1536, 256), (1792, 2048)
SEGMENTS = (SEG_POOL, SEG_Q, SEG_KV, SEG_GATE)


def _params(**kw):
    return pltpu.CompilerParams(vmem_limit_bytes=VMEM_LIMIT, **kw)


def _dot(a, b):
    return jnp.dot(a, b, preferred_element_type=F32)


def _dot_nt(a, b):
    return lax.dot_general(a, b, (((1,), (1,)), ((), ())), preferred_element_type=F32)


def _dot_tn(a, b):
    return lax.dot_general(a, b, (((0,), (0,)), ((), ())), preferred_element_type=F32)


def _rinv(x):
    return lax.rsqrt(jnp.mean(x * x, axis=-1, keepdims=True) + RMS_EPS)


def _rms_bwd(dn, x, g):
    r = _rinv(x)
    xr = x * r
    z = dn * g
    dx = r * (z - xr * jnp.mean(z * xr, axis=-1, keepdims=True))
    return dx, dn * xr


def _acc(ref, val, first):
    @pl.when(first)
    def _():
        ref[...] = val

    @pl.when(jnp.logical_not(first))
    def _():
        ref[...] += val


TOKEN_SPEC = pl.BlockSpec(memory_space=pl.ANY)
F_HALF = 1408


def _resident(w):
    return pl.BlockSpec(w.shape, lambda i: (0, 0), pipeline_mode=pl.Buffered(1))


def _swiglu(a, b):
    sig = jax.nn.sigmoid(a)
    q = a * sig
    return b * (sig + q * (1.0 - sig)), q, q * b


def _ffn_up(h, gnorm, wgt, wut, name, tm=512):
    S, D = h.shape
    F = wgt.shape[0]

    def body(h_ref, g_ref, wg_ref, wu_ref, n_ref, p_ref, q_ref, s_ref):
        x = h_ref[...]
        n = (x * _rinv(x) * g_ref[...]).astype(BF16)
        n_ref[...] = n
        for lo in range(0, F, F_HALF):
            cols = slice(lo, lo + F_HALF)
            p, q, s = _swiglu(_dot_nt(n, wg_ref[cols, :]), _dot_nt(n, wu_ref[cols, :]))
            p_ref[:, cols] = p.astype(BF16)
            q_ref[:, cols] = q.astype(BF16)
            s_ref[:, cols] = s.astype(BF16)

    act = pl.BlockSpec((tm, F), lambda i: (i, 0))
    hidden = jax.ShapeDtypeStruct((S, F), BF16)
    return pl.pallas_call(
        body, name=name, grid=(S // tm,),
        in_specs=[pl.BlockSpec((tm, D), lambda i: (i, 0)), pl.BlockSpec((1, D), lambda i: (0, 0)), _resident(wgt), _resident(wut)],
        out_specs=[pl.BlockSpec((tm, D), lambda i: (i, 0)), act, act, act],
        out_shape=[jax.ShapeDtypeStruct((S, D), BF16), hidden, hidden, hidden],
        compiler_params=_params(dimension_semantics=("arbitrary",)),
    )(h, gnorm, wgt, wut)


def _ffn_gate(h, gnorm, wgt, name, tm=512):
    S, D = h.shape
    F = wgt.shape[0]

    def body(h_ref, g_ref, wg_ref, n_ref, a_ref):
        x = h_ref[...]
        n = (x * _rinv(x) * g_ref[...]).astype(BF16)
        n_ref[...] = n
        for lo in range(0, F, F_HALF):
            a_ref[:, lo:lo + F_HALF] = _dot_nt(n, wg_ref[lo:lo + F_HALF, :]).astype(BF16)

    row = pl.BlockSpec((tm, D), lambda i: (i, 0))
    return pl.pallas_call(
        body, name=name, grid=(S // tm,), in_specs=[row, pl.BlockSpec((1, D), lambda i: (0, 0)), _resident(wgt)],
        out_specs=[row, pl.BlockSpec((tm, F), lambda i: (i, 0))],
        out_shape=[jax.ShapeDtypeStruct((S, D), BF16), jax.ShapeDtypeStruct((S, F), BF16)],
        compiler_params=_params(dimension_semantics=("arbitrary",)),
    )(h, gnorm, wgt)


def _ffn_hidden(n, a, wut, name, tm=512):
    S, D = n.shape
    F = wut.shape[0]

    def body(n_ref, a_ref, wu_ref, p_ref, q_ref, s_ref):
        n_t = n_ref[...]
        for lo in range(0, F, F_HALF):
            cols = slice(lo, lo + F_HALF)
            p, q, s = _swiglu(a_ref[:, cols].astype(F32), _dot_nt(n_t, wu_ref[cols, :]))
            p_ref[:, cols] = p.astype(BF16)
            q_ref[:, cols] = q.astype(BF16)
            s_ref[:, cols] = s.astype(BF16)

    act = pl.BlockSpec((tm, F), lambda i: (i, 0))
    hidden = jax.ShapeDtypeStruct((S, F), BF16)
    return pl.pallas_call(
        body, name=name, grid=(S // tm,), in_specs=[pl.BlockSpec((tm, D), lambda i: (i, 0)), act, _resident(wut)],
        out_specs=[act, act, act], out_shape=[hidden, hidden, hidden],
        compiler_params=_params(dimension_semantics=("arbitrary",)),
    )(n, a, wut)


def _ffn_down(h, s, wd, name, tm=1024):
    S, D = h.shape
    F = wd.shape[0]

    def body(h_ref, s_ref, wd_ref, o_ref):
        o_ref[...] = h_ref[...] + 0.5 * _dot(s_ref[...], wd_ref[...])

    row = pl.BlockSpec((tm, D), lambda i: (i, 0))
    return pl.pallas_call(
        body, name=name, grid=(S // tm,), in_specs=[row, pl.BlockSpec((tm, F), lambda i: (i, 0)), _resident(wd)],
        out_specs=row, out_shape=jax.ShapeDtypeStruct((S, D), F32),
        compiler_params=_params(dimension_semantics=("arbitrary",)),
    )(h, s, wd)


def _ffn_down_loss(h, s, wd, target, name, tm=512):
    S, D = h.shape
    F = wd.shape[0]

    def body(h_ref, s_ref, wd_ref, t_ref, dy_ref, dyh_ref, loss_ref):
        e = h_ref[...] + 0.5 * _dot(s_ref[...], wd_ref[...]) - t_ref[...]
        dy = e * (1.0 / D)
        dy_ref[...] = dy
        dyh_ref[...] = (0.5 * dy).astype(BF16)
        tot = jnp.sum(jnp.sum(e * e, axis=1, keepdims=True), axis=0, keepdims=True) * (0.5 / D)
        _acc(loss_ref, jnp.broadcast_to(tot, loss_ref.shape), pl.program_id(0) == 0)

    row = pl.BlockSpec((tm, D), lambda i: (i, 0))
    return pl.pallas_call(
        body, name=name, grid=(S // tm,), in_specs=[row, pl.BlockSpec((tm, F), lambda i: (i, 0)), _resident(wd), row],
        out_specs=[row, row, pl.BlockSpec((8, 128), lambda i: (0, 0))],
        out_shape=[jax.ShapeDtypeStruct((S, D), F32), jax.ShapeDtypeStruct((S, D), BF16), jax.ShapeDtypeStruct((8, 128), F32)],
        compiler_params=_params(dimension_semantics=("arbitrary",)),
    )(h, s, wd, target)


def _ffn_bwd_h(dyh, p, q, wd, token, name, tm=512):
    S, D = dyh.shape
    F = wd.shape[0]

    def body(dyh_ref, p_ref, q_ref, wd_ref, _, da_ref, db_ref):
        dyh_t = dyh_ref[...]
        for lo in range(0, F, F_HALF):
            cols = slice(lo, lo + F_HALF)
            ds = _dot_nt(dyh_t, wd_ref[cols, :])
            da_ref[:, cols] = (ds * p_ref[:, cols].astype(F32)).astype(BF16)
            db_ref[:, cols] = (ds * q_ref[:, cols].astype(F32)).astype(BF16)

    act = pl.BlockSpec((tm, F), lambda i: (i, 0))
    hidden = jax.ShapeDtypeStruct((S, F), BF16)
    return pl.pallas_call(
        body, name=name, grid=(S // tm,),
        in_specs=[pl.BlockSpec((tm, D), lambda i: (i, 0)), act, act, _resident(wd), TOKEN_SPEC],
        out_specs=[act, act], out_shape=[hidden, hidden],
        compiler_params=_params(dimension_semantics=("arbitrary",)),
    )(dyh, p, q, wd, token)


def _xty(x, y, name, tk=2048, tf=1408, after=()):
    S, F = x.shape
    D = y.shape[1]

    def body(x_ref, y_ref, *rest):
        _acc(rest[-1], _dot_tn(x_ref[...], y_ref[...]), pl.program_id(1) == 0)

    return pl.pallas_call(
        body, name=name, grid=(F // tf, S // tk),
        in_specs=[pl.BlockSpec((tk, tf), lambda j, k: (k, j)), pl.BlockSpec((tk, D), lambda j, k: (k, 0))]
        + [TOKEN_SPEC] * len(after),
        out_specs=pl.BlockSpec((tf, D), lambda j, k: (j, 0)), out_shape=jax.ShapeDtypeStruct((F, D), F32),
        compiler_params=_params(dimension_semantics=("arbitrary", "arbitrary")),
    )(x, y, *after)


def _ffn_bwd_x(da, db, wgt, wut, x, dh, gnorm, name, tm=512):
    S, D = x.shape
    F = wgt.shape[0]

    def body(da_ref, db_ref, wg_ref, wu_ref, x_ref, dh_ref, g_ref, dx_ref, dg_ref):
        dn = _dot(da_ref[...], wg_ref[...]) + _dot(db_ref[...], wu_ref[...])
        dx, dg_rows = _rms_bwd(dn, x_ref[...], g_ref[...])
        dx_ref[...] = dh_ref[...] + dx
        _acc(dg_ref, jnp.sum(dg_rows, axis=0, keepdims=True), pl.program_id(0) == 0)

    row = pl.BlockSpec((tm, D), lambda i: (i, 0))
    act = pl.BlockSpec((tm, F), lambda i: (i, 0))
    vec = pl.BlockSpec((1, D), lambda i: (0, 0))
    return pl.pallas_call(
        body, name=name, grid=(S // tm,), in_specs=[act, act, _resident(wgt), _resident(wut), row, row, vec],
        out_specs=[row, vec],
        out_shape=[jax.ShapeDtypeStruct((S, D), F32), jax.ShapeDtypeStruct((1, D), F32)],
        compiler_params=_params(dimension_semantics=("arbitrary",)),
    )(da, db, wgt, wut, x, dh, gnorm)


def _mix_proj(h, gnorm, wint, tm=512):
    S, D = h.shape
    nt = S // tm

    def body(h_ref, g_ref, w_ref, u_ref, xp_ref, q_ref, kv_ref, gl_ref):
        x = h_ref[...]
        u = (x * _rinv(x) * g_ref[...]).astype(BF16)
        u_ref[...] = u
        for (off, width), ref in zip(SEGMENTS, (xp_ref, q_ref, kv_ref, gl_ref)):
            ref[...] = _dot_nt(u, w_ref[off:off + width, :]).astype(ref.dtype)

    def row(width):
        return pl.BlockSpec((tm, width), lambda i: (i, 0))

    return pl.pallas_call(
        body, name="mix_proj", grid=(nt,),
        in_specs=[row(D), pl.BlockSpec((1, D), lambda i: (0, 0)), pl.BlockSpec(wint.shape, lambda i: (0, 0))],
        out_specs=[row(D), row(SEG_POOL[1]), row(SEG_Q[1]), row(SEG_KV[1]), row(SEG_GATE[1])],
        out_shape=[jax.ShapeDtypeStruct((S, D), BF16), jax.ShapeDtypeStruct((S, SEG_POOL[1]), F32),
                   jax.ShapeDtypeStruct((S, SEG_Q[1]), BF16), jax.ShapeDtypeStruct((S, SEG_KV[1]), BF16),
                   jax.ShapeDtypeStruct((S, SEG_GATE[1]), BF16)],
        compiler_params=_params(dimension_semantics=("arbitrary",)),
    )(h, gnorm, wint)


def _stack_heads(x, g):
    return jnp.concatenate([x[:, (GQA_GROUP * g + hh) * HEAD_DIM:(GQA_GROUP * g + hh + 1) * HEAD_DIM]
                            for hh in range(GQA_GROUP)], axis=0)


def _unstack_heads(ref, val, g):
    for hh in range(GQA_GROUP):
        lo = (GQA_GROUP * g + hh) * HEAD_DIM
        ref[:, lo:lo + HEAD_DIM] = val[hh * ATTN_BLOCK:(hh + 1) * ATTN_BLOCK, :]


def _rowsum(xb, width):
    return _dot(xb, jnp.ones((xb.shape[1], width), BF16))


def _rinv_lanes(x):
    return lax.rsqrt(_rowsum((x * x).astype(BF16), x.shape[1]) * (1.0 / x.shape[1]) + RMS_EPS)


def _twice(x):
    return jnp.concatenate([x, x], axis=1)


def _band_bias():
    qi = (jnp.arange(GQA_GROUP * ATTN_BLOCK) % ATTN_BLOCK)[:, None]
    kj = jnp.arange(2 * ATTN_BLOCK)[None, :]
    return jnp.where(jnp.logical_and(kj > qi, kj <= qi + ATTN_BLOCK), 0.0, NEG).astype(F32)


def _attn_exp(qn, kk, sink, band, n):
    kj = lax.broadcasted_iota(jnp.int32, (1, 2 * ATTN_BLOCK), 1)
    before_sequence = jnp.where(jnp.logical_and(kj < ATTN_BLOCK, n == 0), NEG, 0.0)
    s = _dot_nt(qn, kk) + band + before_sequence
    m = jnp.maximum(jnp.broadcast_to(jnp.max(s, axis=-1, keepdims=True), sink.shape), sink)
    e = jnp.exp(s - _twice(m))
    e_s = jnp.exp(sink - m)
    e_b = e.astype(BF16)
    inv = 1.0 / (_rowsum(e_b, ATTN_BLOCK) + e_s)
    return e_b, e, e_s, inv


def _attn_blocks(n):
    cur = pl.multiple_of(n * ATTN_BLOCK, ATTN_BLOCK)
    prev = pl.multiple_of(jnp.maximum(n - 1, 0) * ATTN_BLOCK, ATTN_BLOCK)
    return cur, prev


def _kv_window(kv_ref, n):
    cur, prev = _attn_blocks(n)
    return jnp.concatenate([kv_ref[pl.ds(prev, ATTN_BLOCK), :], kv_ref[pl.ds(cur, ATTN_BLOCK), :]], axis=0).astype(F32)


def _kv_split(kv, g):
    k = kv[:, g * HEAD_DIM:(g + 1) * HEAD_DIM]
    v = kv[:, (N_KV_HEADS + g) * HEAD_DIM:(N_KV_HEADS + g + 1) * HEAD_DIM]
    return k, v


def _attn_fwd(q, kv, qw, kw, sink_rows, band):
    S, W = q.shape
    per = 4
    rows = per * ATTN_BLOCK

    def body(q_ref, kv_ref, qw_ref, kw_ref, sk_ref, band_ref, o_ref, o_scr):
        qf = q_ref[...].astype(F32)
        pairs = [(b, g) for b in range(per) for g in range(N_KV_HEADS)]
        ns = [per * pl.program_id(0) + b for b in range(per)]
        kvw = [_kv_window(kv_ref, n) for n in ns]
        qs = [_stack_heads(qf[b * ATTN_BLOCK:(b + 1) * ATTN_BLOCK], g) for b, g in pairs]
        kvs = [_kv_split(kvw[b], g) for b, g in pairs]
        rq = [_rinv_lanes(x) for x in qs]
        rk = [_rinv_lanes(k) for k, _ in kvs]
        qn = [(x * r * qw_ref[...] * SCALE).astype(BF16) for x, r in zip(qs, rq)]
        kk = [(k * r * kw_ref[...]).astype(BF16) for (k, _), r in zip(kvs, rk)]
        ex = [_attn_exp(qn[i], kk[i], sk_ref[g], band_ref[...], ns[b]) for i, (b, g) in enumerate(pairs)]
        pv = [_dot(ex[i][0], kvs[i][1].astype(BF16)) for i in range(len(pairs))]
        for i, (b, g) in enumerate(pairs):
            _unstack_heads(o_scr.at[b * ATTN_BLOCK:(b + 1) * ATTN_BLOCK, :], pv[i] * ex[i][3][:, :HEAD_DIM], g)
        o_ref[...] = o_scr[...].astype(BF16)

    blk = pl.BlockSpec((rows, W), lambda n: (n, 0))
    return pl.pallas_call(
        body, name="attn_fwd", grid=(S // rows,),
        in_specs=[blk, pl.BlockSpec(kv.shape, lambda n: (0, 0)), pl.BlockSpec((1, HEAD_DIM), lambda n: (0, 0)),
                  pl.BlockSpec((1, HEAD_DIM), lambda n: (0, 0)), pl.BlockSpec(sink_rows.shape, lambda n: (0, 0, 0)),
                  pl.BlockSpec(band.shape, lambda n: (0, 0))],
        out_specs=blk, out_shape=jax.ShapeDtypeStruct((S, W), BF16),
        scratch_shapes=[pltpu.VMEM((rows, W), F32)],
        compiler_params=_params(dimension_semantics=("arbitrary",)),
    )(q, kv, qw, kw, sink_rows, band)


def _attn_bwd(q, kv, do, qw, kw, sink_rows, band):
    S, W = q.shape
    KW = kv.shape[1]
    nb = S // ATTN_BLOCK
    per = 4
    chunk = 512

    def body(q_ref, kv_ref, do_ref, qw_ref, kw_ref, sk_ref, band_ref, dq_ref, dkv_ref, dqw_ref, dkw_ref, dsk_ref, dq_scr):
        step = pl.program_id(0)

        @pl.when(step == 0)
        def _():
            dkv_ref[...] = jnp.zeros_like(dkv_ref)
            dqw_ref[...] = jnp.zeros_like(dqw_ref)
            dsk_ref[...] = jnp.zeros_like(dsk_ref)

        qw_v = qw_ref[...]
        for b in range(per):
            n = per * step + b
            mine = slice(b * ATTN_BLOCK, (b + 1) * ATTN_BLOCK)
            qf = q_ref[mine, :].astype(F32)
            dof = do_ref[mine, :].astype(F32)
            cur, prev = _attn_blocks(n)
            kvw = _kv_window(kv_ref, n)
            for g in range(N_KV_HEADS):
                qs = _stack_heads(qf, g)
                rq = _rinv_lanes(qs)
                qhat = qs * rq
                qn = (qhat * qw_v * SCALE).astype(BF16)
                k, v = _kv_split(kvw, g)
                kk = (k * _rinv_lanes(k) * kw_ref[...]).astype(BF16)
                vv = v.astype(BF16)
                dos = _stack_heads(dof, g).astype(BF16)
                _, e, e_s, inv = _attn_exp(qn, kk, sk_ref[g], band_ref[...], n)
                p = e * _twice(inv)
                dp = _dot_nt(dos, vv)
                drow = _rowsum((p * dp).astype(BF16), ATTN_BLOCK)
                ds = (p * (dp - _twice(drow))).astype(BF16)
                dsink = -(e_s * inv * drow)
                for hh in range(GQA_GROUP):
                    tot = jnp.sum(dsink[hh * ATTN_BLOCK:(hh + 1) * ATTN_BLOCK, :], axis=0, keepdims=True)
                    dsk_ref[g, hh:hh + 1, :] += tot
                dqn = _dot(ds, kk) * SCALE
                dkk = _dot_tn(ds, qn)
                dvv = _dot_tn(p.astype(BF16), dos)
                klo, vlo = g * HEAD_DIM, (N_KV_HEADS + g) * HEAD_DIM
                for start, rows in ((prev, slice(0, ATTN_BLOCK)), (cur, slice(ATTN_BLOCK, 2 * ATTN_BLOCK))):
                    dkv_ref[pl.ds(start, ATTN_BLOCK), klo:klo + HEAD_DIM] += dkk[rows]
                    dkv_ref[pl.ds(start, ATTN_BLOCK), vlo:vlo + HEAD_DIM] += dvv[rows]
                dqw_ref[...] += jnp.sum(dqn * qhat, axis=0, keepdims=True)
                z = dqn * qw_v
                dqs = rq * (z - qhat * (_rowsum((z * qhat).astype(BF16), HEAD_DIM) * (1.0 / HEAD_DIM)))
                _unstack_heads(dq_scr.at[mine, :], dqs, g)
        dq_ref[...] = dq_scr[...].astype(BF16)

        @pl.when(step == nb // per - 1)
        def _():
            def one(c, dkw):
                rows = pl.ds(pl.multiple_of(c * chunk, chunk), chunk)
                for g in range(N_KV_HEADS):
                    lo = g * HEAD_DIM
                    k = kv_ref[rows, lo:lo + HEAD_DIM].astype(F32)
                    dx, dg_rows = _rms_bwd(dkv_ref[rows, lo:lo + HEAD_DIM], k, kw_ref[...])
                    dkv_ref[rows, lo:lo + HEAD_DIM] = dx
                    dkw = dkw + jnp.sum(dg_rows, axis=0, keepdims=True)
                return dkw

            dkw_ref[...] = lax.fori_loop(0, S // chunk, one, jnp.zeros((1, HEAD_DIM), F32))

    blk = pl.BlockSpec((per * ATTN_BLOCK, W), lambda n: (n, 0))
    whole_kv = pl.BlockSpec((S, KW), lambda n: (0, 0))
    vec = pl.BlockSpec((1, HEAD_DIM), lambda n: (0, 0))
    sk = pl.BlockSpec(sink_rows.shape, lambda n: (0, 0, 0))
    dsk = pl.BlockSpec((N_KV_HEADS, GQA_GROUP, 128), lambda n: (0, 0, 0))
    return pl.pallas_call(
        body, name="attn_bwd", grid=(nb // per,),
        in_specs=[blk, whole_kv, blk, vec, vec, sk, pl.BlockSpec(band.shape, lambda n: (0, 0))],
        out_specs=[blk, whole_kv, vec, vec, dsk],
        out_shape=[jax.ShapeDtypeStruct((S, W), BF16), jax.ShapeDtypeStruct((S, KW), F32),
                   jax.ShapeDtypeStruct((1, HEAD_DIM), F32), jax.ShapeDtypeStruct((1, HEAD_DIM), F32),
                   jax.ShapeDtypeStruct((N_KV_HEADS, GQA_GROUP, 128), F32)],
        scratch_shapes=[pltpu.VMEM((per * ATTN_BLOCK, W), F32)],
        compiler_params=_params(dimension_semantics=("arbitrary",)),
    )(q, kv, do, qw, kw, sink_rows, band)


def _pooled(xc, xprev, i):
    tm = xc.shape[0]
    xh = jnp.concatenate([jnp.where(i > 0, xprev, 0.0), xc], axis=0)
    t = lax.broadcasted_iota(jnp.int32, (tm, 1), 0) + i * tm
    out = []
    for gi, w in enumerate(POOL_WINDOWS):
        acc = xh[:, gi * POOL_GROUP:(gi + 1) * POOL_GROUP]
        sh = 1
        while sh < w:
            acc = acc + pltpu.roll(acc, sh, 0)
            sh *= 2
        cnt = jnp.minimum(t + 1, w).astype(F32)
        out.append(acc[POOL_HALO:, :] / cnt - xc[:, gi * POOL_GROUP:(gi + 1) * POOL_GROUP])
    return jnp.concatenate(out, axis=1)


def _pool_mix(pooled_b, pw_ref):
    return jnp.concatenate([_dot(pooled_b[:, gi * POOL_GROUP:(gi + 1) * POOL_GROUP], pw_ref[gi])
                            for gi in range(len(POOL_WINDOWS))], axis=1)


def _halo_specs(tm, width, S, after):
    per = tm // POOL_HALO
    last = S // POOL_HALO - 1
    if after:
        return pl.BlockSpec((POOL_HALO, width), lambda i: (jnp.minimum((i + 1) * per, last), 0))
    return pl.BlockSpec((POOL_HALO, width), lambda i: (jnp.maximum(i * per - 1, 0), 0))


def _mix_out(xp, attn, gl, bias, pw, pscale, wpot, wao, wo, h, tm=512):
    S, D = h.shape
    nt = S // tm
    PW = xp.shape[1]

    def body(xc_ref, xprev_ref, at_ref, gl_ref, bias_ref, pw_ref, ps_ref, wpo_ref, wao_ref, wo_ref, h_ref,
             ho_ref, bp_ref, ba_ref):
        i = pl.program_id(0)
        pooled = _pooled(xc_ref[...], xprev_ref[...], i).astype(BF16)
        ms = (_pool_mix(pooled, pw_ref) * ps_ref[...]).astype(BF16)
        bp = _dot_nt(ms, wpo_ref[...])
        ba = _dot(at_ref[...], wao_ref[...])
        bp_ref[...] = bp.astype(BF16)
        ba_ref[...] = ba.astype(BF16)
        gates = jax.nn.sigmoid(gl_ref[...].astype(F32) + bias_ref[...])
        merged = (gates[:, :D] * bp + gates[:, D:] * ba).astype(BF16)
        ho_ref[...] = h_ref[...] + _dot(merged, wo_ref[...])

    def row(width):
        return pl.BlockSpec((tm, width), lambda i: (i, 0))

    def whole(x):
        nd = x.ndim
        return pl.BlockSpec(x.shape, lambda i: (0,) * nd)

    return pl.pallas_call(
        body, name="mix_out", grid=(nt,),
        in_specs=[row(PW), _halo_specs(tm, PW, S, False), row(D), row(2 * D), whole(bias), whole(pw), whole(pscale),
                  whole(wpot), whole(wao), whole(wo), row(D)],
        out_specs=[row(D), row(D), row(D)],
        out_shape=[jax.ShapeDtypeStruct((S, D), F32), jax.ShapeDtypeStruct((S, D), BF16),
                   jax.ShapeDtypeStruct((S, D), BF16)],
        compiler_params=_params(dimension_semantics=("arbitrary",)),
    )(xp, xp, attn, gl, bias, pw, pscale, wpot, wao, wo, h)


def _mix_bwd_gate(dh, wo, bp, ba, gl, bias, tm=512):
    S, D = dh.shape
    nt = S // tm

    def body(dh_ref, wo_ref, bp_ref, ba_ref, gl_ref, bias_ref, dgl_ref, dbp_ref, dba_ref, dwo_ref, dbias_ref):
        i = pl.program_id(0)
        dhb = dh_ref[...].astype(BF16)
        dm = _dot_nt(dhb, wo_ref[...])
        gates = jax.nn.sigmoid(gl_ref[...].astype(F32) + bias_ref[...])
        gp, ga = gates[:, :D], gates[:, D:]
        bp_v = bp_ref[...].astype(F32)
        ba_v = ba_ref[...].astype(F32)
        merged = (gp * bp_v + ga * ba_v).astype(BF16)
        _acc(dwo_ref, _dot_tn(merged, dhb), i == 0)
        dbp_ref[...] = (dm * gp).astype(BF16)
        dba_ref[...] = (dm * ga).astype(BF16)
        dgl = jnp.concatenate([dm * bp_v * gp * (1.0 - gp), dm * ba_v * ga * (1.0 - ga)], axis=1)
        dgl_ref[...] = dgl.astype(BF16)
        _acc(dbias_ref, jnp.sum(dgl, axis=0, keepdims=True), i == 0)

    def row(width):
        return pl.BlockSpec((tm, width), lambda i: (i, 0))

    def whole(shape):
        return pl.BlockSpec(shape, lambda i: (0, 0))

    return pl.pallas_call(
        body, name="mix_bwd_gate", grid=(nt,),
        in_specs=[row(D), whole(wo.shape), row(D), row(D), row(2 * D), whole(bias.shape)],
        out_specs=[row(2 * D), row(D), row(D), whole((D, D)), whole((1, 2 * D))],
        out_shape=[jax.ShapeDtypeStruct((S, 2 * D), BF16), jax.ShapeDtypeStruct((S, D), BF16),
                   jax.ShapeDtypeStruct((S, D), BF16), jax.ShapeDtypeStruct((D, D), F32),
                   jax.ShapeDtypeStruct((1, 2 * D), F32)],
        compiler_params=_params(dimension_semantics=("arbitrary",)),
    )(dh, wo, bp, ba, gl, bias)


def _mix_bwd_branch(dbp, dba, attn, xp, pw, pscale, wpot, wao, tm=1024):
    S, D = dbp.shape
    nt = S // tm
    PW = xp.shape[1]
    NG = len(POOL_WINDOWS)

    def body(dbp_ref, dba_ref, at_ref, xc_ref, xprev_ref, pw_ref, ps_ref, wpo_ref, wao_ref,
             dat_ref, dpl_ref, dwao_ref, dwpo_ref, dpw_ref, dps_ref):
        i = pl.program_id(0)
        dba_v = dba_ref[...]
        dbp_v = dbp_ref[...]
        _acc(dwao_ref, _dot_tn(at_ref[...], dba_v), i == 0)
        dat_ref[...] = _dot_nt(dba_v, wao_ref[...]).astype(BF16)
        pooled = _pooled(xc_ref[...], xprev_ref[...], i).astype(BF16)
        mixed = _pool_mix(pooled, pw_ref)
        ps = ps_ref[...]
        _acc(dwpo_ref, _dot_tn(dbp_v, (mixed * ps).astype(BF16)), i == 0)
        dms = _dot(dbp_v, wpo_ref[...])
        _acc(dps_ref, jnp.sum(dms * mixed, axis=0, keepdims=True), i == 0)
        dmixed = (dms * ps).astype(BF16)
        dpooled = []
        for gi in range(NG):
            cols = slice(gi * POOL_GROUP, (gi + 1) * POOL_GROUP)
            _acc(dpw_ref.at[gi], _dot_tn(pooled[:, cols], dmixed[:, cols]), i == 0)
            dpooled.append(_dot_nt(dmixed[:, cols], pw_ref[gi]))
        dpl_ref[...] = jnp.concatenate(dpooled, axis=1)

    def row(width):
        return pl.BlockSpec((tm, width), lambda i: (i, 0))

    def whole(shape):
        nd = len(shape)
        return pl.BlockSpec(shape, lambda i: (0,) * nd)

    return pl.pallas_call(
        body, name="mix_bwd_branch", grid=(nt,),
        in_specs=[row(D), row(D), row(D), row(PW), _halo_specs(tm, PW, S, False), whole(pw.shape), whole(pscale.shape),
                  whole(wpot.shape), whole(wao.shape)],
        out_specs=[row(D), row(PW), whole((D, D)), whole((D, PW)), whole(pw.shape), whole((1, PW))],
        out_shape=[jax.ShapeDtypeStruct((S, D), BF16), jax.ShapeDtypeStruct((S, PW), F32),
                   jax.ShapeDtypeStruct((D, D), F32), jax.ShapeDtypeStruct((D, PW), F32),
                   jax.ShapeDtypeStruct(pw.shape, F32), jax.ShapeDtypeStruct((1, PW), F32)],
        compiler_params=_params(dimension_semantics=("arbitrary",)),
    )(dbp, dba, attn, xp, xp, pw, pscale, wpot, wao)


def _pool_bwd(dpooled, tm=512):
    S, PW = dpooled.shape
    nt = S // tm

    def body(dc_ref, dnext_ref, dxp_ref):
        i = pl.program_id(0)
        dc = dc_ref[...]
        dh = jnp.concatenate([dc, jnp.where(i < nt - 1, dnext_ref[...], 0.0)], axis=0)
        rows = tm + POOL_HALO
        t = lax.broadcasted_iota(jnp.int32, (rows, 1), 0) + i * tm
        out = []
        for gi, w in enumerate(POOL_WINDOWS):
            cols = slice(gi * POOL_GROUP, (gi + 1) * POOL_GROUP)
            acc = dh[:, cols] / jnp.minimum(t + 1, w).astype(F32)
            sh = 1
            while sh < w:
                acc = acc + pltpu.roll(acc, rows - sh, 0)
                sh *= 2
            out.append(acc[:tm, :] - dc[:, cols])
        dxp_ref[...] = jnp.concatenate(out, axis=1).astype(BF16)

    return pl.pallas_call(
        body, name="pool_bwd", grid=(nt,),
        in_specs=[pl.BlockSpec((tm, PW), lambda i: (i, 0)), _halo_specs(tm, PW, S, True)],
        out_specs=pl.BlockSpec((tm, PW), lambda i: (i, 0)), out_shape=jax.ShapeDtypeStruct((S, PW), BF16),
        compiler_params=_params(dimension_semantics=("arbitrary",)),
    )(dpooled, dpooled)


def _in_bwd_w(pieces, u, tm=1024):
    S, D = u.shape
    nt = S // tm
    NW = sum(width for _, width in SEGMENTS)

    def body(*refs):
        piece_refs, u_ref, dw_hbm, acc, sem = refs[:len(SEGMENTS)], refs[len(SEGMENTS)], refs[len(SEGMENTS) + 1], refs[-2], refs[-1]
        i = pl.program_id(0)
        u_t = u_ref[...]
        for (off, width), ref in zip(SEGMENTS, piece_refs):
            for lo in range(0, width, D):
                hi = min(lo + D, width)
                _acc(acc.at[off + lo:off + hi, :], _dot_tn(ref[:, lo:hi].astype(BF16), u_t), i == 0)

        @pl.when(i == nt - 1)
        def _():
            cp = pltpu.make_async_copy(acc, dw_hbm, sem)
            cp.start()
            cp.wait()

    return pl.pallas_call(
        body, name="in_bwd_w", grid=(nt,),
        in_specs=[pl.BlockSpec((tm, width), lambda i: (i, 0)) for _, width in SEGMENTS] + [pl.BlockSpec((tm, D), lambda i: (i, 0))],
        out_specs=pl.BlockSpec(memory_space=pl.ANY), out_shape=jax.ShapeDtypeStruct((NW, D), F32),
        scratch_shapes=[pltpu.VMEM((NW, D), F32), pltpu.SemaphoreType.DMA],
        compiler_params=_params(dimension_semantics=("arbitrary",)),
    )(*pieces, u)


def _in_bwd_x(pieces, wint, h, dh, gnorm, tm=512):
    S, D = h.shape
    nt = S // tm

    def body(*refs):
        piece_refs = refs[:len(SEGMENTS)]
        w_ref, h_ref, dh_ref, g_ref, dx_ref, dxh_ref, dg_ref = refs[len(SEGMENTS):]
        i = pl.program_id(0)
        du = jnp.zeros((tm, D), F32)
        for (off, width), ref in zip(SEGMENTS, piece_refs):
            du = du + _dot(ref[...].astype(BF16), w_ref[off:off + width, :])
        dx, dg_rows = _rms_bwd(du, h_ref[...], g_ref[...])
        out = dh_ref[...] + dx
        dx_ref[...] = out
        dxh_ref[...] = (0.5 * out).astype(BF16)
        _acc(dg_ref, jnp.sum(dg_rows, axis=0, keepdims=True), i == 0)

    def row(width):
        return pl.BlockSpec((tm, width), lambda i: (i, 0))

    vec = pl.BlockSpec((1, D), lambda i: (0, 0))
    return pl.pallas_call(
        body, name="in_bwd_x", grid=(nt,),
        in_specs=[row(width) for _, width in SEGMENTS] + [pl.BlockSpec(wint.shape, lambda i: (0, 0)), row(D), row(D), vec],
        out_specs=[row(D), row(D), vec],
        out_shape=[jax.ShapeDtypeStruct((S, D), F32), jax.ShapeDtypeStruct((S, D), BF16), jax.ShapeDtypeStruct((1, D), F32)],
        compiler_params=_params(dimension_semantics=("arbitrary",)),
    )(*pieces, wint, h, dh, gnorm)


def _row_tile(rows):
    for t in (512, 480, 352, 256, 128, 64, 32, 16, 8):
        if rows % t == 0:
            return t
    return rows


def _adamw_update(w, g, m, v):
    mn = ADAM_B1 * m + (1.0 - ADAM_B1) * g
    vn = ADAM_B2 * v + (1.0 - ADAM_B2) * (g * g)
    m_hat = mn / (1.0 - ADAM_B1 ** ADAM_STEP)
    v_hat = vn / (1.0 - ADAM_B2 ** ADAM_STEP)
    return -ADAM_LR * (m_hat / (jnp.sqrt(v_hat) + ADAM_EPS) + ADAM_WD * w), mn, vn


def _adamw_small(ws, gs, ms, vs):
    n = len(ws)

    def rows(a):
        return a.reshape(-1, a.shape[-1]) if a.ndim > 1 else a.reshape(1, -1)

    def body(*refs):
        for k in range(n):
            w_ref, g_ref, m_ref, v_ref, d_ref, mo_ref, vo_ref = refs[k::n]
            d_ref[...], mo_ref[...], vo_ref[...] = _adamw_update(w_ref[...], g_ref[...], m_ref[...], v_ref[...])

    vmem = pl.BlockSpec(memory_space=pltpu.VMEM)
    out = pl.pallas_call(
        body, name="adamw_small", in_specs=[vmem] * (4 * n), out_specs=[vmem] * (3 * n),
        out_shape=[jax.ShapeDtypeStruct(rows(w).shape, F32) for _ in range(3) for w in ws], compiler_params=_params(),
    )(*[rows(a) for a in ws + gs + ms + vs])
    return [[o.reshape(w.shape) for o, w in zip(out[k * n:(k + 1) * n], ws)] for k in range(3)]


def _adamw(w, g, m, v, token, name, turned=False):
    R, C = w.shape
    tr = R if turned else _row_tile(R)

    def body(w_ref, g_ref, m_ref, v_ref, _, d_ref, mo_ref, vo_ref, go_ref):
        gv = g_ref[...].T if turned else g_ref[...]
        go_ref[...] = gv
        d_ref[...], mo_ref[...], vo_ref[...] = _adamw_update(w_ref[...], gv, m_ref[...], v_ref[...])

    blk = pl.BlockSpec((tr, C), lambda i: (i, 0))
    g_blk = pl.BlockSpec((C, R), lambda i: (0, 0)) if turned else blk
    sh = jax.ShapeDtypeStruct((R, C), F32)
    return pl.pallas_call(
        body, name=name, grid=(R // tr,), in_specs=[blk, g_blk, blk, blk, TOKEN_SPEC], out_specs=[blk] * 4, out_shape=[sh] * 4,
        compiler_params=_params(dimension_semantics=("arbitrary",)),
    )(w, g, m, v, token)


def _cast_place(w, place, name, turned=False, after=()):
    R, C = w.shape[::-1] if turned else w.shape
    tr = R if turned else _row_tile(R)
    per = R // tr

    def body(p_ref, w_ref, *rest):
        rest[-1][...] = (w_ref[...].T if turned else w_ref[...]).astype(BF16)

    grid_spec = pltpu.PrefetchScalarGridSpec(
        num_scalar_prefetch=1, grid=(per,),
        in_specs=[pl.BlockSpec((C, R), lambda i, p: (0, 0)) if turned else pl.BlockSpec((tr, C), lambda i, p: (i, 0))]
        + [TOKEN_SPEC] * len(after),
        out_specs=pl.BlockSpec((tr, C), lambda i, p: (p[0] * per + i, 0)))
    return pl.pallas_call(
        body, name=name, grid_spec=grid_spec, out_shape=jax.ShapeDtypeStruct((N_SHARDS * R, C), BF16),
        compiler_params=_params(dimension_semantics=("arbitrary",)),
    )(place, w, *after)


def _parked(w, m, idx, first, last):
    return jnp.where(m == w, idx, jnp.where(m < w, first, last))


def _sum_halves(g4s, recvs, place, name):
    M = len(g4s)
    NS, R, C = g4s[0].shape
    hr = R // 2
    tr = _row_tile(hr)
    per = hr // tr

    def body(p_ref, *refs):
        m = pl.program_id(0)
        for w in range(M):
            @pl.when(m == w)
            def _(w=w):
                refs[2 * M + w][...] = (refs[w][...] + refs[M + w][...]).astype(BF16)

    def spec(w, mine):
        def index(m, s, i, p):
            row = _parked(w, m, i, 0, per - 1)
            return _parked(w, m, s, 0, NS - 1), (p[1] * per + row if mine else row), 0

        return pl.BlockSpec((1, tr, C), index)

    grid_spec = pltpu.PrefetchScalarGridSpec(
        num_scalar_prefetch=1, grid=(M, NS, per),
        in_specs=[spec(w, True) for w in range(M)] + [spec(w, False) for w in range(M)],
        out_specs=[spec(w, False) for w in range(M)])
    return pl.pallas_call(
        body, name=name, grid_spec=grid_spec, out_shape=[jax.ShapeDtypeStruct((NS, hr, C), BF16)] * M,
        compiler_params=_params(dimension_semantics=("arbitrary",) * 3),
    )(place, *g4s, *recvs)


def _sum_quarters(h4s, recv3s, place, name):
    M = len(h4s)
    NS, hr, C = h4s[0].shape
    tr = _row_tile(hr)
    per = hr // tr

    def body(p_ref, *refs):
        m = pl.program_id(0)
        for w in range(M):
            @pl.when(m == w)
            def _(w=w):
                acc = refs[w][0].astype(F32)
                for k in range(N_SHARDS - 1):
                    acc = acc + refs[M + w][k].astype(F32)
                refs[2 * M + w][...] = acc

    def row(w, m, i):
        return _parked(w, m, i, 0, per - 1)

    grid_spec = pltpu.PrefetchScalarGridSpec(
        num_scalar_prefetch=1, grid=(M, per),
        in_specs=[pl.BlockSpec((1, tr, C), lambda m, i, p, w=w: (p[0], row(w, m, i), 0)) for w in range(M)]
        + [pl.BlockSpec((N_SHARDS - 1, tr, C), lambda m, i, p, w=w: (0, row(w, m, i), 0)) for w in range(M)],
        out_specs=[pl.BlockSpec((tr, C), lambda m, i, p, w=w: (p[1] * per + row(w, m, i), 0)) for w in range(M)])
    return pl.pallas_call(
        body, name=name, grid_spec=grid_spec, out_shape=[jax.ShapeDtypeStruct((2 * hr, C), F32)] * M,
        compiler_params=_params(dimension_semantics=("arbitrary",) * 2),
    )(place, *h4s, *recv3s)


def _by_shape(arrays):
    groups = {}
    for k, a in enumerate(arrays):
        groups.setdefault(a.shape, []).append(k)
    return list(groups.values())


def _place():
    x, y, c = lax.axis_index("x"), lax.axis_index("y"), lax.axis_index("c")
    chips = [(1 - x, y), (x, 1 - y), (1 - x, 1 - y)]
    return x, y, c, chips


HBM_SPEC = pl.BlockSpec(memory_space=pltpu.HBM)
SEM_SPEC = pl.BlockSpec(memory_space=pltpu.SEMAPHORE)
DATAFLOW = pltpu.SideEffectType.DATAFLOW_SIDE_EFFECTING


def _hbm(a):
    return pltpu.with_memory_space_constraint(a, pltpu.HBM)


class InFlight(NamedTuple):
    send_sem: jax.Array
    recv_sem: jax.Array
    bufs: list
    plan: Callable
    token: jax.Array


class Leg(NamedTuple):
    bufs: list
    landing: Optional[InFlight]
    plan: Optional[Callable]
    n_copies: int


def _send(bufs, plan, n_copies):
    return Leg([_hbm(b) for b in bufs], None, plan, n_copies)


def _land(flight):
    return Leg(flight.bufs, flight, None, 0)


def _forward(flight, plan, n_copies):
    return Leg(flight.bufs, flight, plan, n_copies)


def _wait_all(plan, refs, send_ref, recv_ref):
    for k, (src, dst, dev) in enumerate(plan(refs)):
        cp = pltpu.make_async_remote_copy(src_ref=src, dst_ref=dst, send_sem=send_ref.at[k], recv_sem=recv_ref.at[k],
                                          device_id=dev, device_id_type=MESH)
        cp.wait_send()
        cp.wait_recv()


def _start_all(plan, refs, send_ref, recv_ref):
    for k, (src, dst, dev) in enumerate(plan(refs)):
        pltpu.make_async_remote_copy(src_ref=src, dst_ref=dst, send_sem=send_ref.at[k], recv_sem=recv_ref.at[k],
                                     device_id=dev, device_id_type=MESH).start()


def _comm(name, legs, after):
    after = list(after) if isinstance(after, (list, tuple)) else [after]
    ins, in_specs, out_shape, out_specs, aliases, first = [], [], [], [], {}, []
    for leg in legs:
        first.append((len(ins), len(out_shape)))
        for b in leg.bufs:
            aliases[len(ins)] = len(out_shape)
            ins.append(b)
            in_specs.append(HBM_SPEC)
            out_shape.append(pltpu.HBM(b.shape, b.dtype))
            out_specs.append(HBM_SPEC)
        if leg.landing is not None:
            ins += [leg.landing.send_sem, leg.landing.recv_sem]
            in_specs += [SEM_SPEC, SEM_SPEC]
        if leg.plan is not None:
            out_shape += [pltpu.SemaphoreType.DMA((leg.n_copies,))] * 2
            out_specs += [SEM_SPEC, SEM_SPEC]
    starts = any(leg.plan is not None for leg in legs)
    if starts:
        out_shape.append(jax.ShapeDtypeStruct((8, 128), F32))
        out_specs.append(pl.BlockSpec(memory_space=pltpu.VMEM))
    n_in = len(ins) + len(after)

    def body(*refs):
        outs = refs[n_in:]
        for leg, (i, o) in zip(legs, first):
            nb = len(leg.bufs)
            bufs = refs[i:i + nb]
            if leg.landing is not None:
                _wait_all(leg.landing.plan, bufs, refs[i + nb], refs[i + nb + 1])
            if leg.plan is not None:
                _start_all(leg.plan, bufs, outs[o + nb], outs[o + nb + 1])
        if starts:
            outs[-1][...] = jnp.zeros_like(outs[-1])

    out = pl.pallas_call(
        body, name=name, in_specs=in_specs + [TOKEN_SPEC] * len(after), out_shape=out_shape, out_specs=out_specs,
        input_output_aliases=aliases, compiler_params=pltpu.CompilerParams(has_side_effects=DATAFLOW),
    )(*ins, *after)
    results = []
    for leg, (_, o) in zip(legs, first):
        nb = len(leg.bufs)
        bufs = list(out[o:o + nb])
        results.append(bufs if leg.plan is None else InFlight(out[o + nb], out[o + nb + 1], bufs, leg.plan, out[-1]))
    return results


def _half_rows(buf, chip, core):
    hr = buf.shape[0] // (2 * N_SHARDS)
    return buf.at[pl.ds(pl.multiple_of((2 * chip + core) * hr, 16), hr)]


def _gather_ici_plan(bufs):
    x, y, c, chips = _place()
    return [(_half_rows(b, 2 * x + y, c), _half_rows(b, 2 * x + y, c), (px, py, c)) for b in bufs for px, py in chips]


def _gather_d2d_plan(bufs):
    x, y, c, chips = _place()
    return [(_half_rows(b, 2 * px + py, c), _half_rows(b, 2 * px + py, c), (x, y, 1 - c)) for b in bufs for px, py in chips]


def _swap_plan(bufs):
    x, y, c, _ = _place()
    n = len(bufs) // 2
    copies = []
    for g, land in zip(bufs[:n], bufs[n:]):
        hr = g.shape[1] // 2
        copies.append((g.at[:, pl.ds(pl.multiple_of((1 - c) * hr, 8), hr)], land, (x, y, 1 - c)))
    return copies


def _exchange_plan(bufs):
    x, y, c, chips = _place()
    n = len(bufs) // 2
    return [(h.at[2 * px + py], land.at[k], (px, py, c))
            for h, land in zip(bufs[:n], bufs[n:]) for k, (px, py) in enumerate(chips)]


def _share_plan(bufs):
    x, y, c, _ = _place()
    copies = []
    for buf in bufs:
        hr = buf.shape[0] // 2
        mine = buf.at[pl.ds(pl.multiple_of(c * hr, 8), hr)]
        copies.append((mine, mine, (x, y, 1 - c)))
    return copies


N_DEVICES = 8


def _slot_place(vec, place):
    R, C = vec.shape

    def body(p_ref, v_ref, o_ref):
        o_ref[0] = v_ref[...]

    grid_spec = pltpu.PrefetchScalarGridSpec(
        num_scalar_prefetch=1, grid=(1,), in_specs=[pl.BlockSpec((R, C), lambda i, p: (0, 0))],
        out_specs=pl.BlockSpec((1, R, C), lambda i, p: (2 * p[0] + p[1], 0, 0)))
    return pl.pallas_call(
        body, name="slot_place", grid_spec=grid_spec, out_shape=jax.ShapeDtypeStruct((N_DEVICES, R, C), F32),
        compiler_params=_params(dimension_semantics=("arbitrary",)),
    )(place, vec)


def _slots_plan(bufs):
    x, y, c, _ = _place()
    mine = bufs[0].at[4 * x + 2 * y + c]
    return [(mine, mine, (x ^ (r >> 2), y ^ ((r >> 1) & 1), c ^ (r & 1))) for r in range(1, N_DEVICES)]


def _sum_slots(slots):
    _, R, C = slots.shape

    def body(s_ref, o_ref):
        acc = s_ref[0]
        for d in range(1, N_DEVICES):
            acc = acc + s_ref[d]
        o_ref[...] = acc

    return pl.pallas_call(
        body, name="sum_slots", in_specs=[pl.BlockSpec(memory_space=pltpu.VMEM)],
        out_specs=pl.BlockSpec(memory_space=pltpu.VMEM), out_shape=jax.ShapeDtypeStruct((R, C), F32),
        compiler_params=_params(),
    )(slots)


SMALL = ("ffn1_norm", "mix_norm", "pool_w", "pool_scale", "q_norm", "k_norm", "sinks", "gate_bias", "ffn2_norm")
SMALL_COLS = 1024
FFN1 = ("ffn1_w_gate", "ffn1_w_up", "ffn1_w_down")
MIXER = ("w_in", "w_pool_out", "w_attn_out", "w_out")
FFN2 = ("ffn2_w_gate", "ffn2_w_up", "ffn2_w_down")
LARGE = FFN1 + MIXER + FFN2
TRANSPOSED = ("ffn1_w_gate", "ffn1_w_up", "w_in", "w_pool_out", "ffn2_w_gate", "ffn2_w_up")
TURNED = ("w_pool_out",)
WEIGHTS = ("ffn1_norm", "ffn1_w_gate", "ffn1_w_up", "ffn1_w_down", "mix_norm", "w_in", "pool_w", "pool_scale",
           "w_pool_out", "q_norm", "k_norm", "sinks", "w_attn_out", "gate_bias", "w_out", "ffn2_norm",
           "ffn2_w_gate", "ffn2_w_up", "ffn2_w_down")


def _pack_small(parts):
    flat = jnp.concatenate([p.reshape(-1) for p in parts])
    rows = -(-flat.shape[0] // (8 * SMALL_COLS)) * 8
    return jnp.pad(flat, (0, rows * SMALL_COLS - flat.shape[0])).reshape(rows, SMALL_COLS)


def _unpack_small(packed, like):
    flat = packed.reshape(-1)
    out, off = [], 0
    for p in like:
        out.append(flat[off:off + p.size].reshape(p.shape))
        off += p.size
    return out, flat[off]


def _tie(small, token):
    return small + token[0, 0]


class Reduction:
    def __init__(self, names, place):
        self.names, self.place = names, place

    def swap(self, grads):
        grads = [g.reshape(N_SHARDS, -1, g.shape[-1]) for g in grads]
        lands = [lax.empty((g.shape[0], g.shape[1] // 2, g.shape[2]), g.dtype) for g in grads]
        return _send(grads + lands, _swap_plan, len(grads))

    def exchange(self, swapped):
        n = len(self.names)
        halves = self._per_shape(_sum_halves, "sum_halves", swapped[:n], swapped[n:])
        lands = [lax.empty((N_SHARDS - 1,) + h.shape[1:], h.dtype) for h in halves]
        return _send(halves + lands, _exchange_plan, (N_SHARDS - 1) * n)

    def share(self, exchanged):
        n = len(self.names)
        return _send(self._per_shape(_sum_quarters, "sum_quarters", exchanged[:n], exchanged[n:]), _share_plan, n)

    def named(self, shared):
        return dict(zip(self.names, shared))

    def _per_shape(self, add, stage, mine, received):
        out = [None] * len(mine)
        for idx in _by_shape(mine):
            sums = add([mine[k] for k in idx], [received[k] for k in idx], self.place, f"{stage}_{self.names[idx[0]]}")
            for k, v in zip(idx, sums):
                out[k] = v
        return out


def _row_form(name, a):
    return a.T if name in TRANSPOSED else a


def _shard_form(name, a):
    return a if name in TURNED else _row_form(name, a)


def kernel(x, ffn1_norm, ffn1_w_gate, ffn1_w_up, ffn1_w_down, mix_norm, w_in, pool_w, pool_scale, w_pool_out, q_norm, k_norm, sinks, w_attn_out, gate_bias, w_out, ffn2_norm, ffn2_w_gate, ffn2_w_up, ffn2_w_down, loss_target, m_ffn1_norm, m_ffn1_w_gate, m_ffn1_w_up, m_ffn1_w_down, m_mix_norm, m_w_in, m_pool_w, m_pool_scale, m_w_pool_out, m_q_norm, m_k_norm, m_sinks, m_w_attn_out, m_gate_bias, m_w_out, m_ffn2_norm, m_ffn2_w_gate, m_ffn2_w_up, m_ffn2_w_down, v_ffn1_norm, v_ffn1_w_gate, v_ffn1_w_up, v_ffn1_w_down, v_mix_norm, v_w_in, v_pool_w, v_pool_scale, v_w_pool_out, v_q_norm, v_k_norm, v_sinks, v_w_attn_out, v_gate_bias, v_w_out, v_ffn2_norm, v_ffn2_w_gate, v_ffn2_w_up, v_ffn2_w_down):
    args = dict(locals())
    wts = {n: _shard_form(n, args[n]) for n in WEIGHTS}
    mom = {n: _shard_form(n, args["m_" + n]) for n in WEIGHTS}
    var = {n: _shard_form(n, args["v_" + n]) for n in WEIGHTS}
    shard = 2 * lax.axis_index("x") + lax.axis_index("y")
    place = jnp.stack([shard, lax.axis_index("c")]).astype(jnp.int32)

    xs, target = x[0], loss_target[0]
    D = xs.shape[1]
    g1 = wts["ffn1_norm"].reshape(1, D)
    gm = wts["mix_norm"].reshape(1, D)
    g2 = wts["ffn2_norm"].reshape(1, D)
    qw = wts["q_norm"].reshape(1, HEAD_DIM)
    kw = wts["k_norm"].reshape(1, HEAD_DIM)
    bias = wts["gate_bias"].reshape(1, 2 * D)
    pscale = wts["pool_scale"].reshape(1, -1)
    pw = wts["pool_w"].astype(BF16)
    sink_rows = jnp.broadcast_to(jnp.repeat(wts["sinks"], ATTN_BLOCK).reshape(N_KV_HEADS, GQA_GROUP * ATTN_BLOCK, 1),
                                 (N_KV_HEADS, GQA_GROUP * ATTN_BLOCK, 128))

    gate1, up1, down1 = FFN1[:1], FFN1[1:2], FFN1[2:]
    mix_in, mix_out = MIXER[:1], MIXER[1:]

    def over_ici(names):
        return _send([placed[n] for n in names], _gather_ici_plan, 3 * len(names))

    def to_sibling(flight):
        return _forward(flight, _gather_d2d_plan, len(flight.bufs) * 3)

    placed = {n: _cast_place(wts[n], place, "cast_" + n) for n in gate1}
    gate1_ici, = _comm("gather_gate1_ici", [over_ici(gate1)], place)
    placed.update({n: _cast_place(wts[n], place, "cast_" + n) for n in up1 + down1})
    up1_ici, down1_ici = _comm("gather_ffn1_ici", [over_ici(up1), over_ici(down1)], gate1_ici.token)
    placed.update({n: _cast_place(wts[n], place, "cast_" + n, turned=n in TURNED) for n in MIXER})
    band = _band_bias()
    shadow = [sink_rows, pw, band, down1_ici.token]
    gate1_d2d, in_ici, out_ici = _comm(
        "gather_mix_ici", [to_sibling(gate1_ici), over_ici(mix_in), over_ici(mix_out)], shadow)
    w = dict(zip(gate1, _comm("gather_gate1_wait", [_land(gate1_d2d)], gate1_d2d.token)[0]))

    n1, a1 = _ffn_gate(xs, g1, w["ffn1_w_gate"], "ffn1_gate")
    placed.update({n: _cast_place(wts[n], place, "cast_" + n, after=[a1]) for n in FFN2})
    up1_d2d, ffn2_ici, down1_d2d = _comm(
        "gather_ffn1_d2d", [to_sibling(up1_ici), over_ici(FFN2), to_sibling(down1_ici)], a1)
    w.update(zip(up1, _comm("gather_up1_wait", [_land(up1_d2d)], up1_d2d.token)[0]))
    p1, q1, s1 = _ffn_hidden(n1, a1, w["ffn1_w_up"], "ffn1_hidden")
    w.update(zip(down1, _comm("gather_down1_wait", [_land(down1_d2d)], s1)[0]))
    h1 = _ffn_down(xs, s1, w["ffn1_w_down"], "ffn1_down")
    in_d2d, out_d2d = _comm("gather_mix_d2d", [to_sibling(in_ici), to_sibling(out_ici)], h1)
    w.update(zip(mix_in, _comm("gather_in_wait", [_land(in_d2d)], in_d2d.token)[0]))
    u, xp, q, kv, gl = _mix_proj(h1, gm, w["w_in"])
    attn = _attn_fwd(q, kv, qw, kw, sink_rows, band)
    ffn2_d2d, landed = _comm("gather_ffn2_d2d", [to_sibling(ffn2_ici), _land(out_d2d)], attn)
    w.update(zip(mix_out, landed))
    h2, bp, ba = _mix_out(xp, attn, gl, bias, pw, pscale, w["w_pool_out"], w["w_attn_out"], w["w_out"], h1)
    w.update(zip(FFN2, _comm("gather_ffn2_wait", [_land(ffn2_d2d)], h2)[0]))
    n2, p2, q2, s2 = _ffn_up(h2, g2, w["ffn2_w_gate"], w["ffn2_w_up"], "ffn2_up")
    dy, dyh, loss = _ffn_down_loss(h2, s2, w["ffn2_w_down"], target, "ffn2_down_loss")

    gw, gs = {}, {}
    da, db = _ffn_bwd_h(dyh, p2, q2, w["ffn2_w_down"], ffn2_d2d.token, "ffn2_bwd_h")
    gw["ffn2_w_gate"], gw["ffn2_w_up"] = _xty(da, n2, "ffn2_dw_gate"), _xty(db, n2, "ffn2_dw_up")
    gw["ffn2_w_down"] = _xty(s2, dyh, "ffn2_dw_down")
    red2 = Reduction(FFN2, place)
    swap, = _comm("reduce_ffn2_swap", [red2.swap([gw[n] for n in FFN2])], da)
    dh2, gs["ffn2_norm"] = _ffn_bwd_x(da, db, w["ffn2_w_gate"], w["ffn2_w_up"], h2, dy, _tie(g2, swap.token), "ffn2_bwd_x")
    swapped, = _comm("reduce_ffn2_swap_wait", [_land(swap)], dh2)
    exchange, = _comm("reduce_ffn2_exchange", [red2.exchange(swapped)], place)
    dgl, dbp, dba, gw["w_out"], gs["gate_bias"] = _mix_bwd_gate(dh2, w["w_out"], bp, ba, gl, _tie(bias, exchange.token))
    dattn, dpooled, gw["w_attn_out"], gw["w_pool_out"], gs["pool_w"], gs["pool_scale"] = _mix_bwd_branch(
        dbp, dba, attn, xp, pw, pscale, w["w_pool_out"], w["w_attn_out"])
    dq, dkv, gs["q_norm"], gs["k_norm"], dsk = _attn_bwd(q, kv, dattn, qw, kw, sink_rows, band)
    gs["sinks"] = dsk[:, :, 0]
    exchanged, = _comm("reduce_ffn2_exchange_wait", [_land(exchange)], dq)
    share2, = _comm("reduce_ffn2_share", [red2.share(exchanged)], place)
    pieces = (_pool_bwd(dpooled), dq, dkv, dgl)
    gw["w_in"] = _in_bwd_w(pieces, u)
    redm = Reduction(MIXER, place)
    swap, shared = _comm("reduce_mix_swap", [redm.swap([gw[n] for n in MIXER]), _land(share2)], share2.token)
    grads = red2.named(shared)
    dh1, dh1h, gs["mix_norm"] = _in_bwd_x(pieces, w["w_in"], h1, dh2, _tie(gm, swap.token))
    red1, swaps = Reduction(FFN1, place), {}
    dw_down = _xty(s1, dh1h, "ffn1_dw_down")
    swaps["ffn1_w_down"], swapped = _comm("reduce_mix_swap_wait", [red1.swap([dw_down]), _land(swap)], dh1)
    exchange, = _comm("reduce_mix_exchange", [redm.exchange(swapped)], place)
    da, db = _ffn_bwd_h(dh1h, p1, q1, w["ffn1_w_down"], exchange.token, "ffn1_bwd_h")
    dw_up = _xty(db, n1, "ffn1_dw_up")
    swaps["ffn1_w_up"], = _comm("reduce_ffn1_swap_up", [red1.swap([dw_up])], place)
    dw_gate = _xty(da, n1, "ffn1_dw_gate", after=[swaps["ffn1_w_up"].token])
    swaps["ffn1_w_gate"], exchanged = _comm("reduce_mix_exchange_wait", [red1.swap([dw_gate]), _land(exchange)], place)
    sharem, *landed = _comm("reduce_mix_share", [redm.share(exchanged)] + [_land(swaps[n]) for n in FFN1], place)
    exchange, = _comm("reduce_ffn1_exchange", [red1.exchange([l[0] for l in landed] + [l[1] for l in landed])], place)
    grad_x, gs["ffn1_norm"] = _ffn_bwd_x(da, db, w["ffn1_w_gate"], w["ffn1_w_up"], xs, dh1, _tie(g1, exchange.token), "ffn1_bwd_x")

    small_parts = [gs[n] for n in SMALL] + [loss[0, 0].reshape(1)]
    slots, shared = _comm(
        "gather_small", [_send([_slot_place(_pack_small(small_parts), place)], _slots_plan, N_DEVICES - 1), _land(sharem)],
        grad_x)
    grads.update(redm.named(shared))
    delta, new_m, new_v = {}, {}, {}

    def adamw(names, token):
        for n in names:
            delta[n], new_m[n], new_v[n], grads[n] = _adamw(wts[n], grads[n], mom[n], var[n], token, "adamw_" + n, n in TURNED)

    early, late = FFN2 + MIXER[:-1], MIXER[-1:]
    adamw(early, slots.token)
    exchanged, = _comm("reduce_ffn1_exchange_wait", [_land(exchange)], [delta[n] for n in early])
    share1, landed = _comm("reduce_ffn1_share", [red1.share(exchanged), _land(slots)], place)
    summed = _sum_slots(landed[0])
    small_grads, loss_sum = _unpack_small(summed, [wts[n] for n in SMALL])
    grads.update(dict(zip(SMALL, small_grads)))
    for out, vals in zip((delta, new_m, new_v), _adamw_small(*[[d[n] for n in SMALL] for d in (wts, grads, mom, var)])):
        out.update(dict(zip(SMALL, vals)))
    adamw(late, share1.token)
    results = {(k, n): _shard_form(n, d[n]) for k, d in enumerate((grads, delta, new_m, new_v)) for n in SMALL + FFN2 + MIXER}
    shared, = _comm("reduce_ffn1_share_wait", [_land(share1)], [results[k, n] for k in range(4) for n in SMALL + late])
    grads.update(red1.named(shared))
    adamw(FFN1, share1.token)
    results.update({(k, n): _shard_form(n, d[n]) for k, d in enumerate((grads, delta, new_m, new_v)) for n in FFN1})
    return (loss_sum, grad_x[None], *[results[k, n] for k in range(4) for n in WEIGHTS])
```

```python
from typing import Callable, NamedTuple, Optional

import jax
import jax.numpy as jnp
from jax import lax
from jax.experimental import pallas as pl
from jax.experimental.pallas import tpu as pltpu

F32 = jnp.float32
BF16 = jnp.bfloat16
RMS_EPS = 1e-6
POOL_WINDOWS = (2, 4, 8, 16)
POOL_GROUP = 128
POOL_HALO = 16
HEAD_DIM = 64
GQA_GROUP = 8
N_KV_HEADS = 2
ATTN_BLOCK = 128
SCALE = HEAD_DIM ** -0.5
NEG = -1e30
N_SHARDS = 4
ADAM_LR, ADAM_B1, ADAM_B2, ADAM_EPS, ADAM_WD, ADAM_STEP = 0.001, 0.9, 0.999, 1e-08, 0.01, 10
VMEM_LIMIT = 56 * 1024 * 1024
MESH = pl.DeviceIdType.MESH
SEG_POOL, SEG_Q, SEG_KV, SEG_GATE = (0, 512), (512, 1024), (1536, 256), (1792, 2048)
SEGMENTS = (SEG_POOL, SEG_Q, SEG_KV, SEG_GATE)


def _params(**kw):
    return pltpu.CompilerParams(vmem_limit_bytes=VMEM_LIMIT, **kw)


def _dot(a, b):
    return jnp.dot(a, b, preferred_element_type=F32)


def _dot_nt(a, b):
    return lax.dot_general(a, b, (((1,), (1,)), ((), ())), preferred_element_type=F32)


def _dot_tn(a, b):
    return lax.dot_general(a, b, (((0,), (0,)), ((), ())), preferred_element_type=F32)


def _rinv(x):
    return lax.rsqrt(jnp.mean(x * x, axis=-1, keepdims=True) + RMS_EPS)


def _rms_bwd(dn, x, g):
    r = _rinv(x)
    xr = x * r
    z = dn * g
    dx = r * (z - xr * jnp.mean(z * xr, axis=-1, keepdims=True))
    return dx, dn * xr


def _acc(ref, val, first):
    @pl.when(first)
    def _():
        ref[...] = val

    @pl.when(jnp.logical_not(first))
    def _():
        ref[...] += val


TOKEN_SPEC = pl.BlockSpec(memory_space=pl.ANY)
F_HALF = 1408


def _resident(w):
    return pl.BlockSpec(w.shape, lambda i: (0, 0), pipeline_mode=pl.Buffered(1))


def _swiglu(a, b):
    sig = jax.nn.sigmoid(a)
    q = a * sig
    return b * (sig + q * (1.0 - sig)), q, q * b


def _ffn_up(h, gnorm, wgt, wut, name, tm=512):
    S, D = h.shape
    F = wgt.shape[0]

    def body(h_ref, g_ref, wg_ref, wu_ref, n_ref, p_ref, q_ref, s_ref):
        x = h_ref[...]
        n = (x * _rinv(x) * g_ref[...]).astype(BF16)
        n_ref[...] = n
        for lo in range(0, F, F_HALF):
            cols = slice(lo, lo + F_HALF)
            p, q, s = _swiglu(_dot_nt(n, wg_ref[cols, :]), _dot_nt(n, wu_ref[cols, :]))
            p_ref[:, cols] = p.astype(BF16)
            q_ref[:, cols] = q.astype(BF16)
            s_ref[:, cols] = s.astype(BF16)

    act = pl.BlockSpec((tm, F), lambda i: (i, 0))
    hidden = jax.ShapeDtypeStruct((S, F), BF16)
    return pl.pallas_call(
        body, name=name, grid=(S // tm,),
        in_specs=[pl.BlockSpec((tm, D), lambda i: (i, 0)), pl.BlockSpec((1, D), lambda i: (0, 0)), _resident(wgt), _resident(wut)],
        out_specs=[pl.BlockSpec((tm, D), lambda i: (i, 0)), act, act, act],
        out_shape=[jax.ShapeDtypeStruct((S, D), BF16), hidden, hidden, hidden],
        compiler_params=_params(dimension_semantics=("arbitrary",)),
    )(h, gnorm, wgt, wut)


def _ffn_gate(h, gnorm, wgt, name, tm=512):
    S, D = h.shape
    F = wgt.shape[0]

    def body(h_ref, g_ref, wg_ref, n_ref, a_ref):
        x = h_ref[...]
        n = (x * _rinv(x) * g_ref[...]).astype(BF16)
        n_ref[...] = n
        for lo in range(0, F, F_HALF):
            a_ref[:, lo:lo + F_HALF] = _dot_nt(n, wg_ref[lo:lo + F_HALF, :]).astype(BF16)

    row = pl.BlockSpec((tm, D), lambda i: (i, 0))
    return pl.pallas_call(
        body, name=name, grid=(S // tm,), in_specs=[row, pl.BlockSpec((1, D), lambda i: (0, 0)), _resident(wgt)],
        out_specs=[row, pl.BlockSpec((tm, F), lambda i: (i, 0))],
        out_shape=[jax.ShapeDtypeStruct((S, D), BF16), jax.ShapeDtypeStruct((S, F), BF16)],
        compiler_params=_params(dimension_semantics=("arbitrary",)),
    )(h, gnorm, wgt)


def _ffn_hidden(n, a, wut, name, tm=512):
    S, D = n.shape
    F = wut.shape[0]

    def body(n_ref, a_ref, wu_ref, p_ref, q_ref, s_ref):
        n_t = n_ref[...]
        for lo in range(0, F, F_HALF):
            cols = slice(lo, lo + F_HALF)
            p, q, s = _swiglu(a_ref[:, cols].astype(F32), _dot_nt(n_t, wu_ref[cols, :]))
            p_ref[:, cols] = p.astype(BF16)
            q_ref[:, cols] = q.astype(BF16)
            s_ref[:, cols] = s.astype(BF16)

    act = pl.BlockSpec((tm, F), lambda i: (i, 0))
    hidden = jax.ShapeDtypeStruct((S, F), BF16)
    return pl.pallas_call(
        body, name=name, grid=(S // tm,), in_specs=[pl.BlockSpec((tm, D), lambda i: (i, 0)), act, _resident(wut)],
        out_specs=[act, act, act], out_shape=[hidden, hidden, hidden],
        compiler_params=_params(dimension_semantics=("arbitrary",)),
    )(n, a, wut)


def _ffn_down(h, s, wd, name, tm=1024):
    S, D = h.shape
    F = wd.shape[0]

    def body(h_ref, s_ref, wd_ref, o_ref):
        o_ref[...] = h_ref[...] + 0.5 * _dot(s_ref[...], wd_ref[...])

    row = pl.BlockSpec((tm, D), lambda i: (i, 0))
    return pl.pallas_call(
        body, name=name, grid=(S // tm,), in_specs=[row, pl.BlockSpec((tm, F), lambda i: (i, 0)), _resident(wd)],
        out_specs=row, out_shape=jax.ShapeDtypeStruct((S, D), F32),
        compiler_params=_params(dimension_semantics=("arbitrary",)),
    )(h, s, wd)


def _ffn_down_loss(h, s, wd, target, name, tm=512):
    S, D = h.shape
    F = wd.shape[0]

    def body(h_ref, s_ref, wd_ref, t_ref, dy_ref, dyh_ref, loss_ref):
        e = h_ref[...] + 0.5 * _dot(s_ref[...], wd_ref[...]) - t_ref[...]
        dy = e * (1.0 / D)
        dy_ref[...] = dy
        dyh_ref[...] = (0.5 * dy).astype(BF16)
        tot = jnp.sum(jnp.sum(e * e, axis=1, keepdims=True), axis=0, keepdims=True) * (0.5 / D)
        _acc(loss_ref, jnp.broadcast_to(tot, loss_ref.shape), pl.program_id(0) == 0)

    row = pl.BlockSpec((tm, D), lambda i: (i, 0))
    return pl.pallas_call(
        body, name=name, grid=(S // tm,), in_specs=[row, pl.BlockSpec((tm, F), lambda i: (i, 0)), _resident(wd), row],
        out_specs=[row, row, pl.BlockSpec((8, 128), lambda i: (0, 0))],
        out_shape=[jax.ShapeDtypeStruct((S, D), F32), jax.ShapeDtypeStruct((S, D), BF16), jax.ShapeDtypeStruct((8, 128), F32)],
        compiler_params=_params(dimension_semantics=("arbitrary",)),
    )(h, s, wd, target)


def _ffn_bwd_h(dyh, p, q, wd, token, name, tm=512):
    S, D = dyh.shape
    F = wd.shape[0]

    def body(dyh_ref, p_ref, q_ref, wd_ref, _, da_ref, db_ref):
        dyh_t = dyh_ref[...]
        for lo in range(0, F, F_HALF):
            cols = slice(lo, lo + F_HALF)
            ds = _dot_nt(dyh_t, wd_ref[cols, :])
            da_ref[:, cols] = (ds * p_ref[:, cols].astype(F32)).astype(BF16)
            db_ref[:, cols] = (ds * q_ref[:, cols].astype(F32)).astype(BF16)

    act = pl.BlockSpec((tm, F), lambda i: (i, 0))
    hidden = jax.ShapeDtypeStruct((S, F), BF16)
    return pl.pallas_call(
        body, name=name, grid=(S // tm,),
        in_specs=[pl.BlockSpec((tm, D), lambda i: (i, 0)), act, act, _resident(wd), TOKEN_SPEC],
        out_specs=[act, act], out_shape=[hidden, hidden],
        compiler_params=_params(dimension_semantics=("arbitrary",)),
    )(dyh, p, q, wd, token)


def _xty(x, y, name, tk=4096, tf=1408, after=()):
    S, F = x.shape
    D = y.shape[1]

    def body(x_ref, y_ref, *rest):
        _acc(rest[-1], _dot_tn(x_ref[...], y_ref[...]), pl.program_id(1) == 0)

    return pl.pallas_call(
        body, name=name, grid=(F // tf, S // tk),
        in_specs=[pl.BlockSpec((tk, tf), lambda j, k: (k, j)), pl.BlockSpec((tk, D), lambda j, k: (k, 0))]
        + [TOKEN_SPEC] * len(after),
        out_specs=pl.BlockSpec((tf, D), lambda j, k: (j, 0)), out_shape=jax.ShapeDtypeStruct((F, D), F32),
        compiler_params=_params(dimension_semantics=("arbitrary", "arbitrary")),
    )(x, y, *after)


def _ffn_bwd_x(da, db, wgt, wut, x, dh, gnorm, name, tm=512):
    S, D = x.shape
    F = wgt.shape[0]

    def body(da_ref, db_ref, wg_ref, wu_ref, x_ref, dh_ref, g_ref, dx_ref, dg_ref):
        dn = _dot(da_ref[...], wg_ref[...]) + _dot(db_ref[...], wu_ref[...])
        dx, dg_rows = _rms_bwd(dn, x_ref[...], g_ref[...])
        dx_ref[...] = dh_ref[...] + dx
        _acc(dg_ref, jnp.sum(dg_rows, axis=0, keepdims=True), pl.program_id(0) == 0)

    row = pl.BlockSpec((tm, D), lambda i: (i, 0))
    act = pl.BlockSpec((tm, F), lambda i: (i, 0))
    vec = pl.BlockSpec((1, D), lambda i: (0, 0))
    return pl.pallas_call(
        body, name=name, grid=(S // tm,), in_specs=[act, act, _resident(wgt), _resident(wut), row, row, vec],
        out_specs=[row, vec],
        out_shape=[jax.ShapeDtypeStruct((S, D), F32), jax.ShapeDtypeStruct((1, D), F32)],
        compiler_params=_params(dimension_semantics=("arbitrary",)),
    )(da, db, wgt, wut, x, dh, gnorm)


def _mix_proj(h, gnorm, wint, tm=512):
    S, D = h.shape
    nt = S // tm

    def body(h_ref, g_ref, w_ref, u_ref, xp_ref, q_ref, kv_ref, gl_ref):
        x = h_ref[...]
        u = (x * _rinv(x) * g_ref[...]).astype(BF16)
        u_ref[...] = u
        for (off, width), ref in zip(SEGMENTS, (xp_ref, q_ref, kv_ref, gl_ref)):
            ref[...] = _dot_nt(u, w_ref[off:off + width, :]).astype(ref.dtype)

    def row(width):
        return pl.BlockSpec((tm, width), lambda i: (i, 0))

    return pl.pallas_call(
        body, name="mix_proj", grid=(nt,),
        in_specs=[row(D), pl.BlockSpec((1, D), lambda i: (0, 0)), pl.BlockSpec(wint.shape, lambda i: (0, 0))],
        out_specs=[row(D), row(SEG_POOL[1]), row(SEG_Q[1]), row(SEG_KV[1]), row(SEG_GATE[1])],
        out_shape=[jax.ShapeDtypeStruct((S, D), BF16), jax.ShapeDtypeStruct((S, SEG_POOL[1]), F32),
                   jax.ShapeDtypeStruct((S, SEG_Q[1]), BF16), jax.ShapeDtypeStruct((S, SEG_KV[1]), BF16),
                   jax.ShapeDtypeStruct((S, SEG_GATE[1]), BF16)],
        compiler_params=_params(dimension_semantics=("arbitrary",)),
    )(h, gnorm, wint)


def _stack_heads(x, g):
    return jnp.concatenate([x[:, (GQA_GROUP * g + hh) * HEAD_DIM:(GQA_GROUP * g + hh + 1) * HEAD_DIM]
                            for hh in range(GQA_GROUP)], axis=0)


def _unstack_heads(ref, val, g):
    for hh in range(GQA_GROUP):
        lo = (GQA_GROUP * g + hh) * HEAD_DIM
        ref[:, lo:lo + HEAD_DIM] = val[hh * ATTN_BLOCK:(hh + 1) * ATTN_BLOCK, :]


def _rowsum(xb, width):
    return _dot(xb, jnp.ones((xb.shape[1], width), BF16))


def _rinv_lanes(x):
    return lax.rsqrt(_rowsum((x * x).astype(BF16), x.shape[1]) * (1.0 / x.shape[1]) + RMS_EPS)


def _twice(x):
    return jnp.concatenate([x, x], axis=1)


def _band_bias():
    qi = (jnp.arange(GQA_GROUP * ATTN_BLOCK) % ATTN_BLOCK)[:, None]
    kj = jnp.arange(2 * ATTN_BLOCK)[None, :]
    return jnp.where(jnp.logical_and(kj > qi, kj <= qi + ATTN_BLOCK), 0.0, NEG).astype(F32)


def _attn_exp(qn, kk, sink, band, n):
    kj = lax.broadcasted_iota(jnp.int32, (1, 2 * ATTN_BLOCK), 1)
    before_sequence = jnp.where(jnp.logical_and(kj < ATTN_BLOCK, n == 0), NEG, 0.0)
    s = _dot_nt(qn, kk) + band + before_sequence
    m = jnp.maximum(jnp.broadcast_to(jnp.max(s, axis=-1, keepdims=True), sink.shape), sink)
    e = jnp.exp(s - _twice(m))
    e_s = jnp.exp(sink - m)
    e_b = e.astype(BF16)
    inv = 1.0 / (_rowsum(e_b, ATTN_BLOCK) + e_s)
    return e_b, e, e_s, inv


def _attn_blocks(n):
    cur = pl.multiple_of(n * ATTN_BLOCK, ATTN_BLOCK)
    prev = pl.multiple_of(jnp.maximum(n - 1, 0) * ATTN_BLOCK, ATTN_BLOCK)
    return cur, prev


def _kv_window(kv_ref, n):
    cur, prev = _attn_blocks(n)
    return jnp.concatenate([kv_ref[pl.ds(prev, ATTN_BLOCK), :], kv_ref[pl.ds(cur, ATTN_BLOCK), :]], axis=0).astype(F32)


def _kv_split(kv, g):
    k = kv[:, g * HEAD_DIM:(g + 1) * HEAD_DIM]
    v = kv[:, (N_KV_HEADS + g) * HEAD_DIM:(N_KV_HEADS + g + 1) * HEAD_DIM]
    return k, v


def _attn_fwd(q, kv, qw, kw, sink_rows, band):
    S, W = q.shape
    per = 2
    rows = per * ATTN_BLOCK

    def body(q_ref, kv_ref, qw_ref, kw_ref, sk_ref, band_ref, o_ref, o_scr):
        qf = q_ref[...].astype(F32)
        pairs = [(b, g) for b in range(per) for g in range(N_KV_HEADS)]
        ns = [per * pl.program_id(0) + b for b in range(per)]
        kvw = [_kv_window(kv_ref, n) for n in ns]
        qs = [_stack_heads(qf[b * ATTN_BLOCK:(b + 1) * ATTN_BLOCK], g) for b, g in pairs]
        kvs = [_kv_split(kvw[b], g) for b, g in pairs]
        rq = [_rinv_lanes(x) for x in qs]
        rk = [_rinv_lanes(k) for k, _ in kvs]
        qn = [(x * r * qw_ref[...] * SCALE).astype(BF16) for x, r in zip(qs, rq)]
        kk = [(k * r * kw_ref[...]).astype(BF16) for (k, _), r in zip(kvs, rk)]
        ex = [_attn_exp(qn[i], kk[i], sk_ref[g], band_ref[...], ns[b]) for i, (b, g) in enumerate(pairs)]
        pv = [_dot(ex[i][0], kvs[i][1].astype(BF16)) for i in range(len(pairs))]
        for i, (b, g) in enumerate(pairs):
            _unstack_heads(o_scr.at[b * ATTN_BLOCK:(b + 1) * ATTN_BLOCK, :], pv[i] * ex[i][3][:, :HEAD_DIM], g)
        o_ref[...] = o_scr[...].astype(BF16)

    blk = pl.BlockSpec((rows, W), lambda n: (n, 0))
    return pl.pallas_call(
        body, name="attn_fwd", grid=(S // rows,),
        in_specs=[blk, pl.BlockSpec(kv.shape, lambda n: (0, 0)), pl.BlockSpec((1, HEAD_DIM), lambda n: (0, 0)),
                  pl.BlockSpec((1, HEAD_DIM), lambda n: (0, 0)), pl.BlockSpec(sink_rows.shape, lambda n: (0, 0, 0)),
                  pl.BlockSpec(band.shape, lambda n: (0, 0))],
        out_specs=blk, out_shape=jax.ShapeDtypeStruct((S, W), BF16),
        scratch_shapes=[pltpu.VMEM((rows, W), F32)],
        compiler_params=_params(dimension_semantics=("arbitrary",)),
    )(q, kv, qw, kw, sink_rows, band)


def _attn_bwd(q, kv, do, qw, kw, sink_rows, band):
    S, W = q.shape
    KW = kv.shape[1]
    nb = S // ATTN_BLOCK
    per = 2
    chunk = 512

    def body(q_ref, kv_ref, do_ref, qw_ref, kw_ref, sk_ref, band_ref, dq_ref, dkv_ref, dqw_ref, dkw_ref, dsk_ref, dq_scr):
        step = pl.program_id(0)

        @pl.when(step == 0)
        def _():
            dkv_ref[...] = jnp.zeros_like(dkv_ref)
            dqw_ref[...] = jnp.zeros_like(dqw_ref)
            dsk_ref[...] = jnp.zeros_like(dsk_ref)

        qw_v = qw_ref[...]
        for b in range(per):
            n = per * step + b
            mine = slice(b * ATTN_BLOCK, (b + 1) * ATTN_BLOCK)
            qf = q_ref[mine, :].astype(F32)
            dof = do_ref[mine, :].astype(F32)
            cur, prev = _attn_blocks(n)
            kvw = _kv_window(kv_ref, n)
            for g in range(N_KV_HEADS):
                qs = _stack_heads(qf, g)
                rq = _rinv_lanes(qs)
                qhat = qs * rq
                qn = (qhat * qw_v * SCALE).astype(BF16)
                k, v = _kv_split(kvw, g)
                kk = (k * _rinv_lanes(k) * kw_ref[...]).astype(BF16)
                vv = v.astype(BF16)
                dos = _stack_heads(dof, g).astype(BF16)
                _, e, e_s, inv = _attn_exp(qn, kk, sk_ref[g], band_ref[...], n)
                p = e * _twice(inv)
                dp = _dot_nt(dos, vv)
                drow = _rowsum((p * dp).astype(BF16), ATTN_BLOCK)
                ds = (p * (dp - _twice(drow))).astype(BF16)
                dsink = -(e_s * inv * drow)
                for hh in range(GQA_GROUP):
                    tot = jnp.sum(dsink[hh * ATTN_BLOCK:(hh + 1) * ATTN_BLOCK, :], axis=0, keepdims=True)
                    dsk_ref[g, hh:hh + 1, :] += tot
                dqn = _dot(ds, kk) * SCALE
                dkk = _dot_tn(ds, qn)
                dvv = _dot_tn(p.astype(BF16), dos)
                klo, vlo = g * HEAD_DIM, (N_KV_HEADS + g) * HEAD_DIM
                for start, rows in ((prev, slice(0, ATTN_BLOCK)), (cur, slice(ATTN_BLOCK, 2 * ATTN_BLOCK))):
                    dkv_ref[pl.ds(start, ATTN_BLOCK), klo:klo + HEAD_DIM] += dkk[rows]
                    dkv_ref[pl.ds(start, ATTN_BLOCK), vlo:vlo + HEAD_DIM] += dvv[rows]
                dqw_ref[...] += jnp.sum(dqn * qhat, axis=0, keepdims=True)
                z = dqn * qw_v
                dqs = rq * (z - qhat * (_rowsum((z * qhat).astype(BF16), HEAD_DIM) * (1.0 / HEAD_DIM)))
                _unstack_heads(dq_scr.at[mine, :], dqs, g)
        dq_ref[...] = dq_scr[...].astype(BF16)

        @pl.when(step == nb // per - 1)
        def _():
            def one(c, dkw):
                rows = pl.ds(pl.multiple_of(c * chunk, chunk), chunk)
                for g in range(N_KV_HEADS):
                    lo = g * HEAD_DIM
                    k = kv_ref[rows, lo:lo + HEAD_DIM].astype(F32)
                    dx, dg_rows = _rms_bwd(dkv_ref[rows, lo:lo + HEAD_DIM], k, kw_ref[...])
                    dkv_ref[rows, lo:lo + HEAD_DIM] = dx
                    dkw = dkw + jnp.sum(dg_rows, axis=0, keepdims=True)
                return dkw

            dkw_ref[...] = lax.fori_loop(0, S // chunk, one, jnp.zeros((1, HEAD_DIM), F32))

    blk = pl.BlockSpec((per * ATTN_BLOCK, W), lambda n: (n, 0))
    whole_kv = pl.BlockSpec((S, KW), lambda n: (0, 0))
    vec = pl.BlockSpec((1, HEAD_DIM), lambda n: (0, 0))
    sk = pl.BlockSpec(sink_rows.shape, lambda n: (0, 0, 0))
    dsk = pl.BlockSpec((N_KV_HEADS, GQA_GROUP, 128), lambda n: (0, 0, 0))
    return pl.pallas_call(
        body, name="attn_bwd", grid=(nb // per,),
        in_specs=[blk, whole_kv, blk, vec, vec, sk, pl.BlockSpec(band.shape, lambda n: (0, 0))],
        out_specs=[blk, whole_kv, vec, vec, dsk],
        out_shape=[jax.ShapeDtypeStruct((S, W), BF16), jax.ShapeDtypeStruct((S, KW), F32),
                   jax.ShapeDtypeStruct((1, HEAD_DIM), F32), jax.ShapeDtypeStruct((1, HEAD_DIM), F32),
                   jax.ShapeDtypeStruct((N_KV_HEADS, GQA_GROUP, 128), F32)],
        scratch_shapes=[pltpu.VMEM((per * ATTN_BLOCK, W), F32)],
        compiler_params=_params(dimension_semantics=("arbitrary",)),
    )(q, kv, do, qw, kw, sink_rows, band)


def _pooled(xc, xprev, i):
    tm = xc.shape[0]
    xh = jnp.concatenate([jnp.where(i > 0, xprev, 0.0), xc], axis=0)
    t = lax.broadcasted_iota(jnp.int32, (tm, 1), 0) + i * tm
    out = []
    for gi, w in enumerate(POOL_WINDOWS):
        acc = xh[:, gi * POOL_GROUP:(gi + 1) * POOL_GROUP]
        sh = 1
        while sh < w:
            acc = acc + pltpu.roll(acc, sh, 0)
            sh *= 2
        cnt = jnp.minimum(t + 1, w).astype(F32)
        out.append(acc[POOL_HALO:, :] / cnt - xc[:, gi * POOL_GROUP:(gi + 1) * POOL_GROUP])
    return jnp.concatenate(out, axis=1)


def _pool_mix(pooled_b, pw_ref):
    return jnp.concatenate([_dot(pooled_b[:, gi * POOL_GROUP:(gi + 1) * POOL_GROUP], pw_ref[gi])
                            for gi in range(len(POOL_WINDOWS))], axis=1)


def _halo_specs(tm, width, S, after):
    per = tm // POOL_HALO
    last = S // POOL_HALO - 1
    if after:
        return pl.BlockSpec((POOL_HALO, width), lambda i: (jnp.minimum((i + 1) * per, last), 0))
    return pl.BlockSpec((POOL_HALO, width), lambda i: (jnp.maximum(i * per - 1, 0), 0))


def _mix_out(xp, attn, gl, bias, pw, pscale, wpot, wao, wo, h, tm=512):
    S, D = h.shape
    nt = S // tm
    PW = xp.shape[1]

    def body(xc_ref, xprev_ref, at_ref, gl_ref, bias_ref, pw_ref, ps_ref, wpo_ref, wao_ref, wo_ref, h_ref,
             ho_ref, bp_ref, ba_ref):
        i = pl.program_id(0)
        pooled = _pooled(xc_ref[...], xprev_ref[...], i).astype(BF16)
        ms = (_pool_mix(pooled, pw_ref) * ps_ref[...]).astype(BF16)
        bp = _dot_nt(ms, wpo_ref[...])
        ba = _dot(at_ref[...], wao_ref[...])
        bp_ref[...] = bp.astype(BF16)
        ba_ref[...] = ba.astype(BF16)
        gates = jax.nn.sigmoid(gl_ref[...].astype(F32) + bias_ref[...])
        merged = (gates[:, :D] * bp + gates[:, D:] * ba).astype(BF16)
        ho_ref[...] = h_ref[...] + _dot(merged, wo_ref[...])

    def row(width):
        return pl.BlockSpec((tm, width), lambda i: (i, 0))

    def whole(x):
        nd = x.ndim
        return pl.BlockSpec(x.shape, lambda i: (0,) * nd)

    return pl.pallas_call(
        body, name="mix_out", grid=(nt,),
        in_specs=[row(PW), _halo_specs(tm, PW, S, False), row(D), row(2 * D), whole(bias), whole(pw), whole(pscale),
                  whole(wpot), whole(wao), whole(wo), row(D)],
        out_specs=[row(D), row(D), row(D)],
        out_shape=[jax.ShapeDtypeStruct((S, D), F32), jax.ShapeDtypeStruct((S, D), BF16),
                   jax.ShapeDtypeStruct((S, D), BF16)],
        compiler_params=_params(dimension_semantics=("arbitrary",)),
    )(xp, xp, attn, gl, bias, pw, pscale, wpot, wao, wo, h)


def _mix_bwd_gate(dh, wo, bp, ba, gl, bias, tm=512):
    S, D = dh.shape
    nt = S // tm

    def body(dh_ref, wo_ref, bp_ref, ba_ref, gl_ref, bias_ref, dgl_ref, dbp_ref, dba_ref, dwo_ref, dbias_ref):
        i = pl.program_id(0)
        dhb = dh_ref[...].astype(BF16)
        dm = _dot_nt(dhb, wo_ref[...])
        gates = jax.nn.sigmoid(gl_ref[...].astype(F32) + bias_ref[...])
        gp, ga = gates[:, :D], gates[:, D:]
        bp_v = bp_ref[...].astype(F32)
        ba_v = ba_ref[...].astype(F32)
        merged = (gp * bp_v + ga * ba_v).astype(BF16)
        _acc(dwo_ref, _dot_tn(merged, dhb), i == 0)
        dbp_ref[...] = (dm * gp).astype(BF16)
        dba_ref[...] = (dm * ga).astype(BF16)
        dgl = jnp.concatenate([dm * bp_v * gp * (1.0 - gp), dm * ba_v * ga * (1.0 - ga)], axis=1)
        dgl_ref[...] = dgl.astype(BF16)
        _acc(dbias_ref, jnp.sum(dgl, axis=0, keepdims=True), i == 0)

    def row(width):
        return pl.BlockSpec((tm, width), lambda i: (i, 0))

    def whole(shape):
        return pl.BlockSpec(shape, lambda i: (0, 0))

    return pl.pallas_call(
        body, name="mix_bwd_gate", grid=(nt,),
        in_specs=[row(D), whole(wo.shape), row(D), row(D), row(2 * D), whole(bias.shape)],
        out_specs=[row(2 * D), row(D), row(D), whole((D, D)), whole((1, 2 * D))],
        out_shape=[jax.ShapeDtypeStruct((S, 2 * D), BF16), jax.ShapeDtypeStruct((S, D), BF16),
                   jax.ShapeDtypeStruct((S, D), BF16), jax.ShapeDtypeStruct((D, D), F32),
                   jax.ShapeDtypeStruct((1, 2 * D), F32)],
        compiler_params=_params(dimension_semantics=("arbitrary",)),
    )(dh, wo, bp, ba, gl, bias)


def _mix_bwd_branch(dbp, dba, attn, xp, pw, pscale, wpot, wao, tm=1024):
    S, D = dbp.shape
    nt = S // tm
    PW = xp.shape[1]
    NG = len(POOL_WINDOWS)

    def body(dbp_ref, dba_ref, at_ref, xc_ref, xprev_ref, pw_ref, ps_ref, wpo_ref, wao_ref,
             dat_ref, dpl_ref, dwao_ref, dwpo_ref, dpw_ref, dps_ref):
        i = pl.program_id(0)
        dba_v = dba_ref[...]
        dbp_v = dbp_ref[...]
        _acc(dwao_ref, _dot_tn(at_ref[...], dba_v), i == 0)
        dat_ref[...] = _dot_nt(dba_v, wao_ref[...]).astype(BF16)
        pooled = _pooled(xc_ref[...], xprev_ref[...], i).astype(BF16)
        mixed = _pool_mix(pooled, pw_ref)
        ps = ps_ref[...]
        _acc(dwpo_ref, _dot_tn(dbp_v, (mixed * ps).astype(BF16)), i == 0)
        dms = _dot(dbp_v, wpo_ref[...])
        _acc(dps_ref, jnp.sum(dms * mixed, axis=0, keepdims=True), i == 0)
        dmixed = (dms * ps).astype(BF16)
        dpooled = []
        for gi in range(NG):
            cols = slice(gi * POOL_GROUP, (gi + 1) * POOL_GROUP)
            _acc(dpw_ref.at[gi], _dot_tn(pooled[:, cols], dmixed[:, cols]), i == 0)
            dpooled.append(_dot_nt(dmixed[:, cols], pw_ref[gi]))
        dpl_ref[...] = jnp.concatenate(dpooled, axis=1)

    def row(width):
        return pl.BlockSpec((tm, width), lambda i: (i, 0))

    def whole(shape):
        nd = len(shape)
        return pl.BlockSpec(shape, lambda i: (0,) * nd)

    return pl.pallas_call(
        body, name="mix_bwd_branch", grid=(nt,),
        in_specs=[row(D), row(D), row(D), row(PW), _halo_specs(tm, PW, S, False), whole(pw.shape), whole(pscale.shape),
                  whole(wpot.shape), whole(wao.shape)],
        out_specs=[row(D), row(PW), whole((D, D)), whole((D, PW)), whole(pw.shape), whole((1, PW))],
        out_shape=[jax.ShapeDtypeStruct((S, D), BF16), jax.ShapeDtypeStruct((S, PW), F32),
                   jax.ShapeDtypeStruct((D, D), F32), jax.ShapeDtypeStruct((D, PW), F32),
                   jax.ShapeDtypeStruct(pw.shape, F32), jax.ShapeDtypeStruct((1, PW), F32)],
        compiler_params=_params(dimension_semantics=("arbitrary",)),
    )(dbp, dba, attn, xp, xp, pw, pscale, wpot, wao)


def _pool_bwd(dpooled, tm=512):
    S, PW = dpooled.shape
    nt = S // tm

    def body(dc_ref, dnext_ref, dxp_ref):
        i = pl.program_id(0)
        dc = dc_ref[...]
        dh = jnp.concatenate([dc, jnp.where(i < nt - 1, dnext_ref[...], 0.0)], axis=0)
        rows = tm + POOL_HALO
        t = lax.broadcasted_iota(jnp.int32, (rows, 1), 0) + i * tm
        out = []
        for gi, w in enumerate(POOL_WINDOWS):
            cols = slice(gi * POOL_GROUP, (gi + 1) * POOL_GROUP)
            acc = dh[:, cols] / jnp.minimum(t + 1, w).astype(F32)
            sh = 1
            while sh < w:
                acc = acc + pltpu.roll(acc, rows - sh, 0)
                sh *= 2
            out.append(acc[:tm, :] - dc[:, cols])
        dxp_ref[...] = jnp.concatenate(out, axis=1).astype(BF16)

    return pl.pallas_call(
        body, name="pool_bwd", grid=(nt,),
        in_specs=[pl.BlockSpec((tm, PW), lambda i: (i, 0)), _halo_specs(tm, PW, S, True)],
        out_specs=pl.BlockSpec((tm, PW), lambda i: (i, 0)), out_shape=jax.ShapeDtypeStruct((S, PW), BF16),
        compiler_params=_params(dimension_semantics=("arbitrary",)),
    )(dpooled, dpooled)


def _in_bwd_w(pieces, u, tm=1024):
    S, D = u.shape
    nt = S // tm
    NW = sum(width for _, width in SEGMENTS)

    def body(*refs):
        piece_refs, u_ref, dw_hbm, acc, sem = refs[:len(SEGMENTS)], refs[len(SEGMENTS)], refs[len(SEGMENTS) + 1], refs[-2], refs[-1]
        i = pl.program_id(0)
        u_t = u_ref[...]
        for (off, width), ref in zip(SEGMENTS, piece_refs):
            for lo in range(0, width, D):
                hi = min(lo + D, width)
                _acc(acc.at[off + lo:off + hi, :], _dot_tn(ref[:, lo:hi].astype(BF16), u_t), i == 0)

        @pl.when(i == nt - 1)
        def _():
            cp = pltpu.make_async_copy(acc, dw_hbm, sem)
            cp.start()
            cp.wait()

    return pl.pallas_call(
        body, name="in_bwd_w", grid=(nt,),
        in_specs=[pl.BlockSpec((tm, width), lambda i: (i, 0)) for _, width in SEGMENTS] + [pl.BlockSpec((tm, D), lambda i: (i, 0))],
        out_specs=pl.BlockSpec(memory_space=pl.ANY), out_shape=jax.ShapeDtypeStruct((NW, D), F32),
        scratch_shapes=[pltpu.VMEM((NW, D), F32), pltpu.SemaphoreType.DMA],
        compiler_params=_params(dimension_semantics=("arbitrary",)),
    )(*pieces, u)


def _in_bwd_x(pieces, wint, h, dh, gnorm, tm=512):
    S, D = h.shape
    nt = S // tm

    def body(*refs):
        piece_refs = refs[:len(SEGMENTS)]
        w_ref, h_ref, dh_ref, g_ref, dx_ref, dxh_ref, dg_ref = refs[len(SEGMENTS):]
        i = pl.program_id(0)
        du = jnp.zeros((tm, D), F32)
        for (off, width), ref in zip(SEGMENTS, piece_refs):
            du = du + _dot(ref[...].astype(BF16), w_ref[off:off + width, :])
        dx, dg_rows = _rms_bwd(du, h_ref[...], g_ref[...])
        out = dh_ref[...] + dx
        dx_ref[...] = out
        dxh_ref[...] = (0.5 * out).astype(BF16)
        _acc(dg_ref, jnp.sum(dg_rows, axis=0, keepdims=True), i == 0)

    def row(width):
        return pl.BlockSpec((tm, width), lambda i: (i, 0))

    vec = pl.BlockSpec((1, D), lambda i: (0, 0))
    return pl.pallas_call(
        body, name="in_bwd_x", grid=(nt,),
        in_specs=[row(width) for _, width in SEGMENTS] + [pl.BlockSpec(wint.shape, lambda i: (0, 0)), row(D), row(D), vec],
        out_specs=[row(D), row(D), vec],
        out_shape=[jax.ShapeDtypeStruct((S, D), F32), jax.ShapeDtypeStruct((S, D), BF16), jax.ShapeDtypeStruct((1, D), F32)],
        compiler_params=_params(dimension_semantics=("arbitrary",)),
    )(*pieces, wint, h, dh, gnorm)


def _row_tile(rows):
    for t in (512, 480, 352, 256, 128, 64, 32, 16, 8):
        if rows % t == 0:
            return t
    return rows


def _adamw_update(w, g, m, v):
    mn = ADAM_B1 * m + (1.0 - ADAM_B1) * g
    vn = ADAM_B2 * v + (1.0 - ADAM_B2) * (g * g)
    m_hat = mn / (1.0 - ADAM_B1 ** ADAM_STEP)
    v_hat = vn / (1.0 - ADAM_B2 ** ADAM_STEP)
    return -ADAM_LR * (m_hat / (jnp.sqrt(v_hat) + ADAM_EPS) + ADAM_WD * w), mn, vn


def _adamw_small(ws, gs, ms, vs):
    n = len(ws)

    def rows(a):
        return a.reshape(-1, a.shape[-1]) if a.ndim > 1 else a.reshape(1, -1)

    def body(*refs):
        for k in range(n):
            w_ref, g_ref, m_ref, v_ref, d_ref, mo_ref, vo_ref = refs[k::n]
            d_ref[...], mo_ref[...], vo_ref[...] = _adamw_update(w_ref[...], g_ref[...], m_ref[...], v_ref[...])

    vmem = pl.BlockSpec(memory_space=pltpu.VMEM)
    out = pl.pallas_call(
        body, name="adamw_small", in_specs=[vmem] * (4 * n), out_specs=[vmem] * (3 * n),
        out_shape=[jax.ShapeDtypeStruct(rows(w).shape, F32) for _ in range(3) for w in ws], compiler_params=_params(),
    )(*[rows(a) for a in ws + gs + ms + vs])
    return [[o.reshape(w.shape) for o, w in zip(out[k * n:(k + 1) * n], ws)] for k in range(3)]


def _adamw(w, g, m, v, token, name, turned=False):
    R, C = w.shape
    tr = R if turned else _row_tile(R)

    def body(w_ref, g_ref, m_ref, v_ref, _, d_ref, mo_ref, vo_ref, go_ref):
        gv = g_ref[...].T if turned else g_ref[...]
        go_ref[...] = gv
        d_ref[...], mo_ref[...], vo_ref[...] = _adamw_update(w_ref[...], gv, m_ref[...], v_ref[...])

    blk = pl.BlockSpec((tr, C), lambda i: (i, 0))
    g_blk = pl.BlockSpec((C, R), lambda i: (0, 0)) if turned else blk
    sh = jax.ShapeDtypeStruct((R, C), F32)
    return pl.pallas_call(
        body, name=name, grid=(R // tr,), in_specs=[blk, g_blk, blk, blk, TOKEN_SPEC], out_specs=[blk] * 4, out_shape=[sh] * 4,
        compiler_params=_params(dimension_semantics=("arbitrary",)),
    )(w, g, m, v, token)


def _cast_place(w, place, name, turned=False, after=()):
    R, C = w.shape[::-1] if turned else w.shape
    tr = R if turned else _row_tile(R)
    per = R // tr

    def body(p_ref, w_ref, *rest):
        rest[-1][...] = (w_ref[...].T if turned else w_ref[...]).astype(BF16)

    grid_spec = pltpu.PrefetchScalarGridSpec(
        num_scalar_prefetch=1, grid=(per,),
        in_specs=[pl.BlockSpec((C, R), lambda i, p: (0, 0)) if turned else pl.BlockSpec((tr, C), lambda i, p: (i, 0))]
        + [TOKEN_SPEC] * len(after),
        out_specs=pl.BlockSpec((tr, C), lambda i, p: (p[0] * per + i, 0)))
    return pl.pallas_call(
        body, name=name, grid_spec=grid_spec, out_shape=jax.ShapeDtypeStruct((N_SHARDS * R, C), BF16),
        compiler_params=_params(dimension_semantics=("arbitrary",)),
    )(place, w, *after)


def _parked(w, m, idx, first, last):
    return jnp.where(m == w, idx, jnp.where(m < w, first, last))


def _sum_halves(g4s, recvs, place, name):
    M = len(g4s)
    NS, R, C = g4s[0].shape
    hr = R // 2
    tr = _row_tile(hr)
    per = hr // tr

    def body(p_ref, *refs):
        m = pl.program_id(0)
        for w in range(M):
            @pl.when(m == w)
            def _(w=w):
                refs[2 * M + w][...] = (refs[w][...] + refs[M + w][...]).astype(BF16)

    def spec(w, mine):
        def index(m, s, i, p):
            row = _parked(w, m, i, 0, per - 1)
            return _parked(w, m, s, 0, NS - 1), (p[1] * per + row if mine else row), 0

        return pl.BlockSpec((1, tr, C), index)

    grid_spec = pltpu.PrefetchScalarGridSpec(
        num_scalar_prefetch=1, grid=(M, NS, per),
        in_specs=[spec(w, True) for w in range(M)] + [spec(w, False) for w in range(M)],
        out_specs=[spec(w, False) for w in range(M)])
    return pl.pallas_call(
        body, name=name, grid_spec=grid_spec, out_shape=[jax.ShapeDtypeStruct((NS, hr, C), BF16)] * M,
        compiler_params=_params(dimension_semantics=("arbitrary",) * 3),
    )(place, *g4s, *recvs)


def _sum_quarters(h4s, recv3s, place, name):
    M = len(h4s)
    NS, hr, C = h4s[0].shape
    tr = _row_tile(hr)
    per = hr // tr

    def body(p_ref, *refs):
        m = pl.program_id(0)
        for w in range(M):
            @pl.when(m == w)
            def _(w=w):
                acc = refs[w][0].astype(F32)
                for k in range(N_SHARDS - 1):
                    acc = acc + refs[M + w][k].astype(F32)
                refs[2 * M + w][...] = acc

    def row(w, m, i):
        return _parked(w, m, i, 0, per - 1)

    grid_spec = pltpu.PrefetchScalarGridSpec(
        num_scalar_prefetch=1, grid=(M, per),
        in_specs=[pl.BlockSpec((1, tr, C), lambda m, i, p, w=w: (p[0], row(w, m, i), 0)) for w in range(M)]
        + [pl.BlockSpec((N_SHARDS - 1, tr, C), lambda m, i, p, w=w: (0, row(w, m, i), 0)) for w in range(M)],
        out_specs=[pl.BlockSpec((tr, C), lambda m, i, p, w=w: (p[1] * per + row(w, m, i), 0)) for w in range(M)])
    return pl.pallas_call(
        body, name=name, grid_spec=grid_spec, out_shape=[jax.ShapeDtypeStruct((2 * hr, C), F32)] * M,
        compiler_params=_params(dimension_semantics=("arbitrary",) * 2),
    )(place, *h4s, *recv3s)


def _by_shape(arrays):
    groups = {}
    for k, a in enumerate(arrays):
        groups.setdefault(a.shape, []).append(k)
    return list(groups.values())


def _place():
    x, y, c = lax.axis_index("x"), lax.axis_index("y"), lax.axis_index("c")
    chips = [(1 - x, y), (x, 1 - y), (1 - x, 1 - y)]
    return x, y, c, chips


HBM_SPEC = pl.BlockSpec(memory_space=pltpu.HBM)
SEM_SPEC = pl.BlockSpec(memory_space=pltpu.SEMAPHORE)
DATAFLOW = pltpu.SideEffectType.DATAFLOW_SIDE_EFFECTING


def _hbm(a):
    return pltpu.with_memory_space_constraint(a, pltpu.HBM)


class InFlight(NamedTuple):
    send_sem: jax.Array
    recv_sem: jax.Array
    bufs: list
    plan: Callable
    token: jax.Array


class Leg(NamedTuple):
    bufs: list
    landing: Optional[InFlight]
    plan: Optional[Callable]
    n_copies: int


def _send(bufs, plan, n_copies):
    return Leg([_hbm(b) for b in bufs], None, plan, n_copies)


def _land(flight):
    return Leg(flight.bufs, flight, None, 0)


def _forward(flight, plan, n_copies):
    return Leg(flight.bufs, flight, plan, n_copies)


def _wait_all(plan, refs, send_ref, recv_ref):
    for k, (src, dst, dev) in enumerate(plan(refs)):
        cp = pltpu.make_async_remote_copy(src_ref=src, dst_ref=dst, send_sem=send_ref.at[k], recv_sem=recv_ref.at[k],
                                          device_id=dev, device_id_type=MESH)
        cp.wait_send()
        cp.wait_recv()


def _start_all(plan, refs, send_ref, recv_ref):
    for k, (src, dst, dev) in enumerate(plan(refs)):
        pltpu.make_async_remote_copy(src_ref=src, dst_ref=dst, send_sem=send_ref.at[k], recv_sem=recv_ref.at[k],
                                     device_id=dev, device_id_type=MESH).start()


def _comm(name, legs, after):
    after = list(after) if isinstance(after, (list, tuple)) else [after]
    ins, in_specs, out_shape, out_specs, aliases, first = [], [], [], [], {}, []
    for leg in legs:
        first.append((len(ins), len(out_shape)))
        for b in leg.bufs:
            aliases[len(ins)] = len(out_shape)
            ins.append(b)
            in_specs.append(HBM_SPEC)
            out_shape.append(pltpu.HBM(b.shape, b.dtype))
            out_specs.append(HBM_SPEC)
        if leg.landing is not None:
            ins += [leg.landing.send_sem, leg.landing.recv_sem]
            in_specs += [SEM_SPEC, SEM_SPEC]
        if leg.plan is not None:
            out_shape += [pltpu.SemaphoreType.DMA((leg.n_copies,))] * 2
            out_specs += [SEM_SPEC, SEM_SPEC]
    starts = any(leg.plan is not None for leg in legs)
    if starts:
        out_shape.append(jax.ShapeDtypeStruct((8, 128), F32))
        out_specs.append(pl.BlockSpec(memory_space=pltpu.VMEM))
    n_in = len(ins) + len(after)

    def body(*refs):
        outs = refs[n_in:]
        for leg, (i, o) in zip(legs, first):
            nb = len(leg.bufs)
            bufs = refs[i:i + nb]
            if leg.landing is not None:
                _wait_all(leg.landing.plan, bufs, refs[i + nb], refs[i + nb + 1])
            if leg.plan is not None:
                _start_all(leg.plan, bufs, outs[o + nb], outs[o + nb + 1])
        if starts:
            outs[-1][...] = jnp.zeros_like(outs[-1])

    out = pl.pallas_call(
        body, name=name, in_specs=in_specs + [TOKEN_SPEC] * len(after), out_shape=out_shape, out_specs=out_specs,
        input_output_aliases=aliases, compiler_params=pltpu.CompilerParams(has_side_effects=DATAFLOW),
    )(*ins, *after)
    results = []
    for leg, (_, o) in zip(legs, first):
        nb = len(leg.bufs)
        bufs = list(out[o:o + nb])
        results.append(bufs if leg.plan is None else InFlight(out[o + nb], out[o + nb + 1], bufs, leg.plan, out[-1]))
    return results


def _half_rows(buf, chip, core):
    hr = buf.shape[0] // (2 * N_SHARDS)
    return buf.at[pl.ds(pl.multiple_of((2 * chip + core) * hr, 16), hr)]


def _gather_ici_plan(bufs):
    x, y, c, chips = _place()
    return [(_half_rows(b, 2 * x + y, c), _half_rows(b, 2 * x + y, c), (px, py, c)) for b in bufs for px, py in chips]


def _gather_d2d_plan(bufs):
    x, y, c, chips = _place()
    return [(_half_rows(b, 2 * px + py, c), _half_rows(b, 2 * px + py, c), (x, y, 1 - c)) for b in bufs for px, py in chips]


def _swap_plan(bufs):
    x, y, c, _ = _place()
    n = len(bufs) // 2
    copies = []
    for g, land in zip(bufs[:n], bufs[n:]):
        hr = g.shape[1] // 2
        copies.append((g.at[:, pl.ds(pl.multiple_of((1 - c) * hr, 8), hr)], land, (x, y, 1 - c)))
    return copies


def _exchange_plan(bufs):
    x, y, c, chips = _place()
    n = len(bufs) // 2
    return [(h.at[2 * px + py], land.at[k], (px, py, c))
            for h, land in zip(bufs[:n], bufs[n:]) for k, (px, py) in enumerate(chips)]


def _share_plan(bufs):
    x, y, c, _ = _place()
    copies = []
    for buf in bufs:
        hr = buf.shape[0] // 2
        mine = buf.at[pl.ds(pl.multiple_of(c * hr, 8), hr)]
        copies.append((mine, mine, (x, y, 1 - c)))
    return copies


N_DEVICES = 8


def _slot_place(vec, place):
    R, C = vec.shape

    def body(p_ref, v_ref, o_ref):
        o_ref[0] = v_ref[...]

    grid_spec = pltpu.PrefetchScalarGridSpec(
        num_scalar_prefetch=1, grid=(1,), in_specs=[pl.BlockSpec((R, C), lambda i, p: (0, 0))],
        out_specs=pl.BlockSpec((1, R, C), lambda i, p: (2 * p[0] + p[1], 0, 0)))
    return pl.pallas_call(
        body, name="slot_place", grid_spec=grid_spec, out_shape=jax.ShapeDtypeStruct((N_DEVICES, R, C), F32),
        compiler_params=_params(dimension_semantics=("arbitrary",)),
    )(place, vec)


def _slots_plan(bufs):
    x, y, c, _ = _place()
    mine = bufs[0].at[4 * x + 2 * y + c]
    return [(mine, mine, (x ^ (r >> 2), y ^ ((r >> 1) & 1), c ^ (r & 1))) for r in range(1, N_DEVICES)]


def _sum_slots(slots):
    _, R, C = slots.shape

    def body(s_ref, o_ref):
        acc = s_ref[0]
        for d in range(1, N_DEVICES):
            acc = acc + s_ref[d]
        o_ref[...] = acc

    return pl.pallas_call(
        body, name="sum_slots", in_specs=[pl.BlockSpec(memory_space=pltpu.VMEM)],
        out_specs=pl.BlockSpec(memory_space=pltpu.VMEM), out_shape=jax.ShapeDtypeStruct((R, C), F32),
        compiler_params=_params(),
    )(slots)


SMALL = ("ffn1_norm", "mix_norm", "pool_w", "pool_scale", "q_norm", "k_norm", "sinks", "gate_bias", "ffn2_norm")
SMALL_COLS = 1024
FFN1 = ("ffn1_w_gate", "ffn1_w_up", "ffn1_w_down")
MIXER = ("w_in", "w_pool_out", "w_attn_out", "w_out")
FFN2 = ("ffn2_w_gate", "ffn2_w_up", "ffn2_w_down")
LARGE = FFN1 + MIXER + FFN2
TRANSPOSED = ("ffn1_w_gate", "ffn1_w_up", "w_in", "w_pool_out", "ffn2_w_gate", "ffn2_w_up")
TURNED = ("w_pool_out",)
WEIGHTS = ("ffn1_norm", "ffn1_w_gate", "ffn1_w_up", "ffn1_w_down", "mix_norm", "w_in", "pool_w", "pool_scale",
           "w_pool_out", "q_norm", "k_norm", "sinks", "w_attn_out", "gate_bias", "w_out", "ffn2_norm",
           "ffn2_w_gate", "ffn2_w_up", "ffn2_w_down")


def _pack_small(parts):
    flat = jnp.concatenate([p.reshape(-1) for p in parts])
    rows = -(-flat.shape[0] // (8 * SMALL_COLS)) * 8
    return jnp.pad(flat, (0, rows * SMALL_COLS - flat.shape[0])).reshape(rows, SMALL_COLS)


def _unpack_small(packed, like):
    flat = packed.reshape(-1)
    out, off = [], 0
    for p in like:
        out.append(flat[off:off + p.size].reshape(p.shape))
        off += p.size
    return out, flat[off]


def _tie(small, token):
    return small + token[0, 0]


class Reduction:
    def __init__(self, names, place):
        self.names, self.place = names, place

    def swap(self, grads):
        grads = [g.reshape(N_SHARDS, -1, g.shape[-1]) for g in grads]
        lands = [lax.empty((g.shape[0], g.shape[1] // 2, g.shape[2]), g.dtype) for g in grads]
        return _send(grads + lands, _swap_plan, len(grads))

    def exchange(self, swapped):
        n = len(self.names)
        halves = self._per_shape(_sum_halves, "sum_halves", swapped[:n], swapped[n:])
        lands = [lax.empty((N_SHARDS - 1,) + h.shape[1:], h.dtype) for h in halves]
        return _send(halves + lands, _exchange_plan, (N_SHARDS - 1) * n)

    def share(self, exchanged):
        n = len(self.names)
        return _send(self._per_shape(_sum_quarters, "sum_quarters", exchanged[:n], exchanged[n:]), _share_plan, n)

    def named(self, shared):
        return dict(zip(self.names, shared))

    def _per_shape(self, add, stage, mine, received):
        out = [None] * len(mine)
        for idx in _by_shape(mine):
            sums = add([mine[k] for k in idx], [received[k] for k in idx], self.place, f"{stage}_{self.names[idx[0]]}")
            for k, v in zip(idx, sums):
                out[k] = v
        return out


def _row_form(name, a):
    return a.T if name in TRANSPOSED else a


def _shard_form(name, a):
    return a if name in TURNED else _row_form(name, a)


def kernel(x, ffn1_norm, ffn1_w_gate, ffn1_w_up, ffn1_w_down, mix_norm, w_in, pool_w, pool_scale, w_pool_out, q_norm, k_norm, sinks, w_attn_out, gate_bias, w_out, ffn2_norm, ffn2_w_gate, ffn2_w_up, ffn2_w_down, loss_target, m_ffn1_norm, m_ffn1_w_gate, m_ffn1_w_up, m_ffn1_w_down, m_mix_norm, m_w_in, m_pool_w, m_pool_scale, m_w_pool_out, m_q_norm, m_k_norm, m_sinks, m_w_attn_out, m_gate_bias, m_w_out, m_ffn2_norm, m_ffn2_w_gate, m_ffn2_w_up, m_ffn2_w_down, v_ffn1_norm, v_ffn1_w_gate, v_ffn1_w_up, v_ffn1_w_down, v_mix_norm, v_w_in, v_pool_w, v_pool_scale, v_w_pool_out, v_q_norm, v_k_norm, v_sinks, v_w_attn_out, v_gate_bias, v_w_out, v_ffn2_norm, v_ffn2_w_gate, v_ffn2_w_up, v_ffn2_w_down):
    args = dict(locals())
    wts = {n: _shard_form(n, args[n]) for n in WEIGHTS}
    mom = {n: _shard_form(n, args["m_" + n]) for n in WEIGHTS}
    var = {n: _shard_form(n, args["v_" + n]) for n in WEIGHTS}
    shard = 2 * lax.axis_index("x") + lax.axis_index("y")
    place = jnp.stack([shard, lax.axis_index("c")]).astype(jnp.int32)

    xs, target = x[0], loss_target[0]
    D = xs.shape[1]
    g1 = wts["ffn1_norm"].reshape(1, D)
    gm = wts["mix_norm"].reshape(1, D)
    g2 = wts["ffn2_norm"].reshape(1, D)
    qw = wts["q_norm"].reshape(1, HEAD_DIM)
    kw = wts["k_norm"].reshape(1, HEAD_DIM)
    bias = wts["gate_bias"].reshape(1, 2 * D)
    pscale = wts["pool_scale"].reshape(1, -1)
    pw = wts["pool_w"].astype(BF16)
    sink_rows = jnp.broadcast_to(jnp.repeat(wts["sinks"], ATTN_BLOCK).reshape(N_KV_HEADS, GQA_GROUP * ATTN_BLOCK, 1),
                                 (N_KV_HEADS, GQA_GROUP * ATTN_BLOCK, 128))

    gate1, up1, down1 = FFN1[:1], FFN1[1:2], FFN1[2:]
    mix_in, mix_out = MIXER[:1], MIXER[1:]

    def over_ici(names):
        return _send([placed[n] for n in names], _gather_ici_plan, 3 * len(names))

    def to_sibling(flight):
        return _forward(flight, _gather_d2d_plan, len(flight.bufs) * 3)

    placed = {n: _cast_place(wts[n], place, "cast_" + n) for n in gate1}
    gate1_ici, = _comm("gather_gate1_ici", [over_ici(gate1)], place)
    placed.update({n: _cast_place(wts[n], place, "cast_" + n) for n in up1 + down1})
    up1_ici, down1_ici = _comm("gather_ffn1_ici", [over_ici(up1), over_ici(down1)], gate1_ici.token)
    placed.update({n: _cast_place(wts[n], place, "cast_" + n, turned=n in TURNED) for n in MIXER})
    band = _band_bias()
    shadow = [sink_rows, pw, band, down1_ici.token]
    gate1_d2d, in_ici, out_ici = _comm(
        "gather_mix_ici", [to_sibling(gate1_ici), over_ici(mix_in), over_ici(mix_out)], shadow)
    w = dict(zip(gate1, _comm("gather_gate1_wait", [_land(gate1_d2d)], gate1_d2d.token)[0]))

    n1, a1 = _ffn_gate(xs, g1, w["ffn1_w_gate"], "ffn1_gate")
    placed.update({n: _cast_place(wts[n], place, "cast_" + n, after=[a1]) for n in FFN2})
    up1_d2d, ffn2_ici, down1_d2d = _comm(
        "gather_ffn1_d2d", [to_sibling(up1_ici), over_ici(FFN2), to_sibling(down1_ici)], a1)
    w.update(zip(up1, _comm("gather_up1_wait", [_land(up1_d2d)], up1_d2d.token)[0]))
    p1, q1, s1 = _ffn_hidden(n1, a1, w["ffn1_w_up"], "ffn1_hidden")
    w.update(zip(down1, _comm("gather_down1_wait", [_land(down1_d2d)], s1)[0]))
    h1 = _ffn_down(xs, s1, w["ffn1_w_down"], "ffn1_down")
    in_d2d, out_d2d = _comm("gather_mix_d2d", [to_sibling(in_ici), to_sibling(out_ici)], h1)
    w.update(zip(mix_in, _comm("gather_in_wait", [_land(in_d2d)], in_d2d.token)[0]))
    u, xp, q, kv, gl = _mix_proj(h1, gm, w["w_in"])
    attn = _attn_fwd(q, kv, qw, kw, sink_rows, band)
    ffn2_d2d, landed = _comm("gather_ffn2_d2d", [to_sibling(ffn2_ici), _land(out_d2d)], attn)
    w.update(zip(mix_out, landed))
    h2, bp, ba = _mix_out(xp, attn, gl, bias, pw, pscale, w["w_pool_out"], w["w_attn_out"], w["w_out"], h1)
    w.update(zip(FFN2, _comm("gather_ffn2_wait", [_land(ffn2_d2d)], h2)[0]))
    n2, p2, q2, s2 = _ffn_up(h2, g2, w["ffn2_w_gate"], w["ffn2_w_up"], "ffn2_up")
    dy, dyh, loss = _ffn_down_loss(h2, s2, w["ffn2_w_down"], target, "ffn2_down_loss")

    gw, gs = {}, {}
    da, db = _ffn_bwd_h(dyh, p2, q2, w["ffn2_w_down"], ffn2_d2d.token, "ffn2_bwd_h")
    gw["ffn2_w_gate"], gw["ffn2_w_up"] = _xty(da, n2, "ffn2_dw_gate"), _xty(db, n2, "ffn2_dw_up")
    gw["ffn2_w_down"] = _xty(s2, dyh, "ffn2_dw_down")
    red2 = Reduction(FFN2, place)
    swap, = _comm("reduce_ffn2_swap", [red2.swap([gw[n] for n in FFN2])], da)
    dh2, gs["ffn2_norm"] = _ffn_bwd_x(da, db, w["ffn2_w_gate"], w["ffn2_w_up"], h2, dy, _tie(g2, swap.token), "ffn2_bwd_x")
    swapped, = _comm("reduce_ffn2_swap_wait", [_land(swap)], dh2)
    exchange, = _comm("reduce_ffn2_exchange", [red2.exchange(swapped)], place)
    dgl, dbp, dba, gw["w_out"], gs["gate_bias"] = _mix_bwd_gate(dh2, w["w_out"], bp, ba, gl, _tie(bias, exchange.token))
    dattn, dpooled, gw["w_attn_out"], gw["w_pool_out"], gs["pool_w"], gs["pool_scale"] = _mix_bwd_branch(
        dbp, dba, attn, xp, pw, pscale, w["w_pool_out"], w["w_attn_out"])
    dq, dkv, gs["q_norm"], gs["k_norm"], dsk = _attn_bwd(q, kv, dattn, qw, kw, sink_rows, band)
    gs["sinks"] = dsk[:, :, 0]
    exchanged, = _comm("reduce_ffn2_exchange_wait", [_land(exchange)], dq)
    share2, = _comm("reduce_ffn2_share", [red2.share(exchanged)], place)
    pieces = (_pool_bwd(dpooled), dq, dkv, dgl)
    gw["w_in"] = _in_bwd_w(pieces, u)
    redm = Reduction(MIXER, place)
    swap, shared = _comm("reduce_mix_swap", [redm.swap([gw[n] for n in MIXER]), _land(share2)], share2.token)
    grads = red2.named(shared)
    dh1, dh1h, gs["mix_norm"] = _in_bwd_x(pieces, w["w_in"], h1, dh2, _tie(gm, swap.token))
    red1, swaps = Reduction(FFN1, place), {}
    dw_down = _xty(s1, dh1h, "ffn1_dw_down")
    swaps["ffn1_w_down"], swapped = _comm("reduce_mix_swap_wait", [red1.swap([dw_down]), _land(swap)], dh1)
    exchange, = _comm("reduce_mix_exchange", [redm.exchange(swapped)], place)
    da, db = _ffn_bwd_h(dh1h, p1, q1, w["ffn1_w_down"], exchange.token, "ffn1_bwd_h")
    dw_up = _xty(db, n1, "ffn1_dw_up")
    swaps["ffn1_w_up"], = _comm("reduce_ffn1_swap_up", [red1.swap([dw_up])], place)
    dw_gate = _xty(da, n1, "ffn1_dw_gate", after=[swaps["ffn1_w_up"].token])
    swaps["ffn1_w_gate"], exchanged = _comm("reduce_mix_exchange_wait", [red1.swap([dw_gate]), _land(exchange)], place)
    sharem, *landed = _comm("reduce_mix_share", [redm.share(exchanged)] + [_land(swaps[n]) for n in FFN1], place)
    exchange, = _comm("reduce_ffn1_exchange", [red1.exchange([l[0] for l in landed] + [l[1] for l in landed])], place)
    grad_x, gs["ffn1_norm"] = _ffn_bwd_x(da, db, w["ffn1_w_gate"], w["ffn1_w_up"], xs, dh1, _tie(g1, exchange.token), "ffn1_bwd_x")

    small_parts = [gs[n] for n in SMALL] + [loss[0, 0].reshape(1)]
    slots, shared = _comm(
        "gather_small", [_send([_slot_place(_pack_small(small_parts), place)], _slots_plan, N_DEVICES - 1), _land(sharem)],
        grad_x)
    grads.update(redm.named(shared))
    delta, new_m, new_v = {}, {}, {}

    def adamw(names, token):
        for n in names:
            delta[n], new_m[n], new_v[n], grads[n] = _adamw(wts[n], grads[n], mom[n], var[n], token, "adamw_" + n, n in TURNED)

    early, late = FFN2 + MIXER[:-1], MIXER[-1:]
    adamw(early, slots.token)
    exchanged, = _comm("reduce_ffn1_exchange_wait", [_land(exchange)], [delta[n] for n in early])
    share1, landed = _comm("reduce_ffn1_share", [red1.share(exchanged), _land(slots)], place)
    summed = _sum_slots(landed[0])
    small_grads, loss_sum = _unpack_small(summed, [wts[n] for n in SMALL])
    grads.update(dict(zip(SMALL, small_grads)))
    for out, vals in zip((delta, new_m, new_v), _adamw_small(*[[d[n] for n in SMALL] for d in (wts, grads, mom, var)])):
        out.update(dict(zip(SMALL, vals)))
    adamw(late, share1.token)
    results = {(k, n): _shard_form(n, d[n]) for k, d in enumerate((grads, delta, new_m, new_v)) for n in SMALL + FFN2 + MIXER}
    shared, = _comm("reduce_ffn1_share_wait", [_land(share1)], [results[k, n] for k in range(4) for n in SMALL + late])
    grads.update(red1.named(shared))
    adamw(FFN1, share1.token)
    results.update({(k, n): _shard_form(n, d[n]) for k, d in enumerate((grads, delta, new_m, new_v)) for n in FFN1})
    return (loss_sum, grad_x[None], *[results[k, n] for k in range(4) for n in WEIGHTS])
```

```python
from typing import Callable, NamedTuple, Optional

import jax
import jax.numpy as jnp
from jax import lax
from jax.experimental import pallas as pl
from jax.experimental.pallas import tpu as pltpu

F32 = jnp.float32
BF16 = jnp.bfloat16
RMS_EPS = 1e-6
POOL_WINDOWS = (2, 4, 8, 16)
POOL_GROUP = 128
POOL_HALO = 16
HEAD_DIM = 64
GQA_GROUP = 8
N_KV_HEADS = 2
ATTN_BLOCK = 128
SCALE = HEAD_DIM ** -0.5
NEG = -1e30
N_SHARDS = 4
ADAM_LR, ADAM_B1, ADAM_B2, ADAM_EPS, ADAM_WD, ADAM_STEP = 0.001, 0.9, 0.999, 1e-08, 0.01, 10
VMEM_LIMIT = 56 * 1024 * 1024
MESH = pl.DeviceIdType.MESH
SEG_POOL, SEG_Q, SEG_KV, SEG_GATE = (0, 512), (512, 1024), (1536, 256), (1792, 2048)
SEGMENTS = (SEG_POOL, SEG_Q, SEG_KV, SEG_GATE)


def _params(**kw):
    return pltpu.CompilerParams(vmem_limit_bytes=VMEM_LIMIT, **kw)


def _dot(a, b):
    return jnp.dot(a, b, preferred_element_type=F32)


def _dot_nt(a, b):
    return lax.dot_general(a, b, (((1,), (1,)), ((), ())), preferred_element_type=F32)


def _dot_tn(a, b):
    return lax.dot_general(a, b, (((0,), (0,)), ((), ())), preferred_element_type=F32)


def _rinv(x):
    return lax.rsqrt(jnp.mean(x * x, axis=-1, keepdims=True) + RMS_EPS)


def _rms_bwd(dn, x, g):
    r = _rinv(x)
    xr = x * r
    z = dn * g
    dx = r * (z - xr * jnp.mean(z * xr, axis=-1, keepdims=True))
    return dx, dn * xr


def _acc(ref, val, first):
    @pl.when(first)
    def _():
        ref[...] = val

    @pl.when(jnp.logical_not(first))
    def _():
        ref[...] += val


TOKEN_SPEC = pl.BlockSpec(memory_space=pl.ANY)
F_HALF = 1408


def _resident(w):
    return pl.BlockSpec(w.shape, lambda i: (0, 0), pipeline_mode=pl.Buffered(1))


def _swiglu(a, b):
    sig = jax.nn.sigmoid(a)
    q = a * sig
    return b * (sig + q * (1.0 - sig)), q, q * b


def _ffn_up(h, gnorm, wgt, wut, name, tm=512):
    S, D = h.shape
    F = wgt.shape[0]

    def body(h_ref, g_ref, wg_ref, wu_ref, n_ref, p_ref, q_ref, s_ref):
        x = h_ref[...]
        n = (x * _rinv(x) * g_ref[...]).astype(BF16)
        n_ref[...] = n
        for lo in range(0, F, F_HALF):
            cols = slice(lo, lo + F_HALF)
            p, q, s = _swiglu(_dot_nt(n, wg_ref[cols, :]), _dot_nt(n, wu_ref[cols, :]))
            p_ref[:, cols] = p.astype(BF16)
            q_ref[:, cols] = q.astype(BF16)
            s_ref[:, cols] = s.astype(BF16)

    act = pl.BlockSpec((tm, F), lambda i: (i, 0))
    hidden = jax.ShapeDtypeStruct((S, F), BF16)
    return pl.pallas_call(
        body, name=name, grid=(S // tm,),
        in_specs=[pl.BlockSpec((tm, D), lambda i: (i, 0)), pl.BlockSpec((1, D), lambda i: (0, 0)), _resident(wgt), _resident(wut)],
        out_specs=[pl.BlockSpec((tm, D), lambda i: (i, 0)), act, act, act],
        out_shape=[jax.ShapeDtypeStruct((S, D), BF16), hidden, hidden, hidden],
        compiler_params=_params(dimension_semantics=("arbitrary",)),
    )(h, gnorm, wgt, wut)


def _ffn_gate(h, gnorm, wgt, name, tm=512):
    S, D = h.shape
    F = wgt.shape[0]

    def body(h_ref, g_ref, wg_ref, n_ref, a_ref):
        x = h_ref[...]
        n = (x * _rinv(x) * g_ref[...]).astype(BF16)
        n_ref[...] = n
        for lo in range(0, F, F_HALF):
            a_ref[:, lo:lo + F_HALF] = _dot_nt(n, wg_ref[lo:lo + F_HALF, :]).astype(BF16)

    row = pl.BlockSpec((tm, D), lambda i: (i, 0))
    return pl.pallas_call(
        body, name=name, grid=(S // tm,), in_specs=[row, pl.BlockSpec((1, D), lambda i: (0, 0)), _resident(wgt)],
        out_specs=[row, pl.BlockSpec((tm, F), lambda i: (i, 0))],
        out_shape=[jax.ShapeDtypeStruct((S, D), BF16), jax.ShapeDtypeStruct((S, F), BF16)],
        compiler_params=_params(dimension_semantics=("arbitrary",)),
    )(h, gnorm, wgt)


def _ffn_hidden(n, a, wut, name, tm=512):
    S, D = n.shape
    F = wut.shape[0]

    def body(n_ref, a_ref, wu_ref, p_ref, q_ref, s_ref):
        n_t = n_ref[...]
        for lo in range(0, F, F_HALF):
            cols = slice(lo, lo + F_HALF)
            p, q, s = _swiglu(a_ref[:, cols].astype(F32), _dot_nt(n_t, wu_ref[cols, :]))
            p_ref[:, cols] = p.astype(BF16)
            q_ref[:, cols] = q.astype(BF16)
            s_ref[:, cols] = s.astype(BF16)

    act = pl.BlockSpec((tm, F), lambda i: (i, 0))
    hidden = jax.ShapeDtypeStruct((S, F), BF16)
    return pl.pallas_call(
        body, name=name, grid=(S // tm,), in_specs=[pl.BlockSpec((tm, D), lambda i: (i, 0)), act, _resident(wut)],
        out_specs=[act, act, act], out_shape=[hidden, hidden, hidden],
        compiler_params=_params(dimension_semantics=("arbitrary",)),
    )(n, a, wut)


def _ffn_down(h, s, wd, name, tm=1024):
    S, D = h.shape
    F = wd.shape[0]

    def body(h_ref, s_ref, wd_ref, o_ref):
        o_ref[...] = h_ref[...] + 0.5 * _dot(s_ref[...], wd_ref[...])

    row = pl.BlockSpec((tm, D), lambda i: (i, 0))
    return pl.pallas_call(
        body, name=name, grid=(S // tm,), in_specs=[row, pl.BlockSpec((tm, F), lambda i: (i, 0)), _resident(wd)],
        out_specs=row, out_shape=jax.ShapeDtypeStruct((S, D), F32),
        compiler_params=_params(dimension_semantics=("arbitrary",)),
    )(h, s, wd)


def _ffn_down_loss(h, s, wd, target, name, tm=512):
    S, D = h.shape
    F = wd.shape[0]

    def body(h_ref, s_ref, wd_ref, t_ref, dy_ref, dyh_ref, loss_ref):
        e = h_ref[...] + 0.5 * _dot(s_ref[...], wd_ref[...]) - t_ref[...]
        dy = e * (1.0 / D)
        dy_ref[...] = dy
        dyh_ref[...] = (0.5 * dy).astype(BF16)
        tot = jnp.sum(jnp.sum(e * e, axis=1, keepdims=True), axis=0, keepdims=True) * (0.5 / D)
        _acc(loss_ref, jnp.broadcast_to(tot, loss_ref.shape), pl.program_id(0) == 0)

    row = pl.BlockSpec((tm, D), lambda i: (i, 0))
    return pl.pallas_call(
        body, name=name, grid=(S // tm,), in_specs=[row, pl.BlockSpec((tm, F), lambda i: (i, 0)), _resident(wd), row],
        out_specs=[row, row, pl.BlockSpec((8, 128), lambda i: (0, 0))],
        out_shape=[jax.ShapeDtypeStruct((S, D), F32), jax.ShapeDtypeStruct((S, D), BF16), jax.ShapeDtypeStruct((8, 128), F32)],
        compiler_params=_params(dimension_semantics=("arbitrary",)),
    )(h, s, wd, target)


def _ffn_bwd_h(dyh, p, q, wd, token, name, tm=512):
    S, D = dyh.shape
    F = wd.shape[0]

    def body(dyh_ref, p_ref, q_ref, wd_ref, _, da_ref, db_ref):
        dyh_t = dyh_ref[...]
        for lo in range(0, F, F_HALF):
            cols = slice(lo, lo + F_HALF)
            ds = _dot_nt(dyh_t, wd_ref[cols, :])
            da_ref[:, cols] = (ds * p_ref[:, cols].astype(F32)).astype(BF16)
            db_ref[:, cols] = (ds * q_ref[:, cols].astype(F32)).astype(BF16)

    act = pl.BlockSpec((tm, F), lambda i: (i, 0))
    hidden = jax.ShapeDtypeStruct((S, F), BF16)
    return pl.pallas_call(
        body, name=name, grid=(S // tm,),
        in_specs=[pl.BlockSpec((tm, D), lambda i: (i, 0)), act, act, _resident(wd), TOKEN_SPEC],
        out_specs=[act, act], out_shape=[hidden, hidden],
        compiler_params=_params(dimension_semantics=("arbitrary",)),
    )(dyh, p, q, wd, token)


def _xty(x, y, name, tk=2048, tf=1408, after=()):
    S, F = x.shape
    D = y.shape[1]

    def body(x_ref, y_ref, *rest):
        _acc(rest[-1], _dot_tn(x_ref[...], y_ref[...]), pl.program_id(1) == 0)

    return pl.pallas_call(
        body, name=name, grid=(F // tf, S // tk),
        in_specs=[pl.BlockSpec((tk, tf), lambda j, k: (k, j)), pl.BlockSpec((tk, D), lambda j, k: (k, 0))]
        + [TOKEN_SPEC] * len(after),
        out_specs=pl.BlockSpec((tf, D), lambda j, k: (j, 0)), out_shape=jax.ShapeDtypeStruct((F, D), F32),
        compiler_params=_params(dimension_semantics=("arbitrary", "arbitrary")),
    )(x, y, *after)


def _ffn_bwd_x(da, db, wgt, wut, x, dh, gnorm, name, tm=512):
    S, D = x.shape
    F = wgt.shape[0]

    def body(da_ref, db_ref, wg_ref, wu_ref, x_ref, dh_ref, g_ref, dx_ref, dg_ref):
        dn = _dot(da_ref[...], wg_ref[...]) + _dot(db_ref[...], wu_ref[...])
        dx, dg_rows = _rms_bwd(dn, x_ref[...], g_ref[...])
        dx_ref[...] = dh_ref[...] + dx
        _acc(dg_ref, jnp.sum(dg_rows, axis=0, keepdims=True), pl.program_id(0) == 0)

    row = pl.BlockSpec((tm, D), lambda i: (i, 0))
    act = pl.BlockSpec((tm, F), lambda i: (i, 0))
    vec = pl.BlockSpec((1, D), lambda i: (0, 0))
    return pl.pallas_call(
        body, name=name, grid=(S // tm,), in_specs=[act, act, _resident(wgt), _resident(wut), row, row, vec],
        out_specs=[row, vec],
        out_shape=[jax.ShapeDtypeStruct((S, D), F32), jax.ShapeDtypeStruct((1, D), F32)],
        compiler_params=_params(dimension_semantics=("arbitrary",)),
    )(da, db, wgt, wut, x, dh, gnorm)


def _mix_proj(h, gnorm, wint, tm=512):
    S, D = h.shape
    nt = S // tm

    def body(h_ref, g_ref, w_ref, u_ref, xp_ref, q_ref, kv_ref, gl_ref):
        x = h_ref[...]
        u = (x * _rinv(x) * g_ref[...]).astype(BF16)
        u_ref[...] = u
        for (off, width), ref in zip(SEGMENTS, (xp_ref, q_ref, kv_ref, gl_ref)):
            ref[...] = _dot_nt(u, w_ref[off:off + width, :]).astype(ref.dtype)

    def row(width):
        return pl.BlockSpec((tm, width), lambda i: (i, 0))

    return pl.pallas_call(
        body, name="mix_proj", grid=(nt,),
        in_specs=[row(D), pl.BlockSpec((1, D), lambda i: (0, 0)), pl.BlockSpec(wint.shape, lambda i: (0, 0))],
        out_specs=[row(D), row(SEG_POOL[1]), row(SEG_Q[1]), row(SEG_KV[1]), row(SEG_GATE[1])],
        out_shape=[jax.ShapeDtypeStruct((S, D), BF16), jax.ShapeDtypeStruct((S, SEG_POOL[1]), F32),
                   jax.ShapeDtypeStruct((S, SEG_Q[1]), BF16), jax.ShapeDtypeStruct((S, SEG_KV[1]), BF16),
                   jax.ShapeDtypeStruct((S, SEG_GATE[1]), BF16)],
        compiler_params=_params(dimension_semantics=("arbitrary",)),
    )(h, gnorm, wint)


def _stack_heads(x, g):
    return jnp.concatenate([x[:, (GQA_GROUP * g + hh) * HEAD_DIM:(GQA_GROUP * g + hh + 1) * HEAD_DIM]
                            for hh in range(GQA_GROUP)], axis=0)


def _unstack_heads(ref, val, g):
    for hh in range(GQA_GROUP):
        lo = (GQA_GROUP * g + hh) * HEAD_DIM
        ref[:, lo:lo + HEAD_DIM] = val[hh * ATTN_BLOCK:(hh + 1) * ATTN_BLOCK, :]


def _rowsum(xb, width):
    return _dot(xb, jnp.ones((xb.shape[1], width), BF16))


def _rinv_lanes(x):
    return lax.rsqrt(_rowsum((x * x).astype(BF16), x.shape[1]) * (1.0 / x.shape[1]) + RMS_EPS)


def _twice(x):
    return jnp.concatenate([x, x], axis=1)


def _band_bias():
    qi = (jnp.arange(GQA_GROUP * ATTN_BLOCK) % ATTN_BLOCK)[:, None]
    kj = jnp.arange(2 * ATTN_BLOCK)[None, :]
    return jnp.where(jnp.logical_and(kj > qi, kj <= qi + ATTN_BLOCK), 0.0, NEG).astype(F32)


def _attn_exp(qn, kk, sink, band, n):
    kj = lax.broadcasted_iota(jnp.int32, (1, 2 * ATTN_BLOCK), 1)
    before_sequence = jnp.where(jnp.logical_and(kj < ATTN_BLOCK, n == 0), NEG, 0.0)
    s = _dot_nt(qn, kk) + band + before_sequence
    m = jnp.maximum(jnp.broadcast_to(jnp.max(s, axis=-1, keepdims=True), sink.shape), sink)
    e = jnp.exp(s - _twice(m))
    e_s = jnp.exp(sink - m)
    e_b = e.astype(BF16)
    inv = 1.0 / (_rowsum(e_b, ATTN_BLOCK) + e_s)
    return e_b, e, e_s, inv


def _attn_blocks(n):
    cur = pl.multiple_of(n * ATTN_BLOCK, ATTN_BLOCK)
    prev = pl.multiple_of(jnp.maximum(n - 1, 0) * ATTN_BLOCK, ATTN_BLOCK)
    return cur, prev


def _kv_window(kv_ref, n):
    cur, prev = _attn_blocks(n)
    return jnp.concatenate([kv_ref[pl.ds(prev, ATTN_BLOCK), :], kv_ref[pl.ds(cur, ATTN_BLOCK), :]], axis=0).astype(F32)


def _kv_split(kv, g):
    k = kv[:, g * HEAD_DIM:(g + 1) * HEAD_DIM]
    v = kv[:, (N_KV_HEADS + g) * HEAD_DIM:(N_KV_HEADS + g + 1) * HEAD_DIM]
    return k, v


def _attn_fwd(q, kv, qw, kw, sink_rows, band):
    S, W = q.shape
    per = 2
    rows = per * ATTN_BLOCK

    def body(q_ref, kv_ref, qw_ref, kw_ref, sk_ref, band_ref, o_ref, o_scr):
        qf = q_ref[...].astype(F32)
        pairs = [(b, g) for b in range(per) for g in range(N_KV_HEADS)]
        ns = [per * pl.program_id(0) + b for b in range(per)]
        kvw = [_kv_window(kv_ref, n) for n in ns]
        qs = [_stack_heads(qf[b * ATTN_BLOCK:(b + 1) * ATTN_BLOCK], g) for b, g in pairs]
        kvs = [_kv_split(kvw[b], g) for b, g in pairs]
        rq = [_rinv_lanes(x) for x in qs]
        rk = [_rinv_lanes(k) for k, _ in kvs]
        qn = [(x * r * qw_ref[...] * SCALE).astype(BF16) for x, r in zip(qs, rq)]
        kk = [(k * r * kw_ref[...]).astype(BF16) for (k, _), r in zip(kvs, rk)]
        ex = [_attn_exp(qn[i], kk[i], sk_ref[g], band_ref[...], ns[b]) for i, (b, g) in enumerate(pairs)]
        pv = [_dot(ex[i][0], kvs[i][1].astype(BF16)) for i in range(len(pairs))]
        for i, (b, g) in enumerate(pairs):
            _unstack_heads(o_scr.at[b * ATTN_BLOCK:(b + 1) * ATTN_BLOCK, :], pv[i] * ex[i][3][:, :HEAD_DIM], g)
        o_ref[...] = o_scr[...].astype(BF16)

    blk = pl.BlockSpec((rows, W), lambda n: (n, 0))
    return pl.pallas_call(
        body, name="attn_fwd", grid=(S // rows,),
        in_specs=[blk, pl.BlockSpec(kv.shape, lambda n: (0, 0)), pl.BlockSpec((1, HEAD_DIM), lambda n: (0, 0)),
                  pl.BlockSpec((1, HEAD_DIM), lambda n: (0, 0)), pl.BlockSpec(sink_rows.shape, lambda n: (0, 0, 0)),
                  pl.BlockSpec(band.shape, lambda n: (0, 0))],
        out_specs=blk, out_shape=jax.ShapeDtypeStruct((S, W), BF16),
        scratch_shapes=[pltpu.VMEM((rows, W), F32)],
        compiler_params=_params(dimension_semantics=("arbitrary",)),
    )(q, kv, qw, kw, sink_rows, band)


def _attn_bwd(q, kv, do, qw, kw, sink_rows, band):
    S, W = q.shape
    KW = kv.shape[1]
    nb = S // ATTN_BLOCK
    per = 2
    chunk = 512

    def body(q_ref, kv_ref, do_ref, qw_ref, kw_ref, sk_ref, band_ref, dq_ref, dkv_ref, dqw_ref, dkw_ref, dsk_ref, dq_scr):
        step = pl.program_id(0)

        @pl.when(step == 0)
        def _():
            dkv_ref[...] = jnp.zeros_like(dkv_ref)
            dqw_ref[...] = jnp.zeros_like(dqw_ref)
            dsk_ref[...] = jnp.zeros_like(dsk_ref)

        qw_v = qw_ref[...]
        for b in range(per):
            n = per * step + b
            mine = slice(b * ATTN_BLOCK, (b + 1) * ATTN_BLOCK)
            qf = q_ref[mine, :].astype(F32)
            dof = do_ref[mine, :].astype(F32)
            cur, prev = _attn_blocks(n)
            kvw = _kv_window(kv_ref, n)
            for g in range(N_KV_HEADS):
                qs = _stack_heads(qf, g)
                rq = _rinv_lanes(qs)
                qhat = qs * rq
                qn = (qhat * qw_v * SCALE).astype(BF16)
                k, v = _kv_split(kvw, g)
                kk = (k * _rinv_lanes(k) * kw_ref[...]).astype(BF16)
                vv = v.astype(BF16)
                dos = _stack_heads(dof, g).astype(BF16)
                _, e, e_s, inv = _attn_exp(qn, kk, sk_ref[g], band_ref[...], n)
                p = e * _twice(inv)
                dp = _dot_nt(dos, vv)
                drow = _rowsum((p * dp).astype(BF16), ATTN_BLOCK)
                ds = (p * (dp - _twice(drow))).astype(BF16)
                dsink = -(e_s * inv * drow)
                for hh in range(GQA_GROUP):
                    tot = jnp.sum(dsink[hh * ATTN_BLOCK:(hh + 1) * ATTN_BLOCK, :], axis=0, keepdims=True)
                    dsk_ref[g, hh:hh + 1, :] += tot
                dqn = _dot(ds, kk) * SCALE
                dkk = _dot_tn(ds, qn)
                dvv = _dot_tn(p.astype(BF16), dos)
                klo, vlo = g * HEAD_DIM, (N_KV_HEADS + g) * HEAD_DIM
                for start, rows in ((prev, slice(0, ATTN_BLOCK)), (cur, slice(ATTN_BLOCK, 2 * ATTN_BLOCK))):
                    dkv_ref[pl.ds(start, ATTN_BLOCK), klo:klo + HEAD_DIM] += dkk[rows]
                    dkv_ref[pl.ds(start, ATTN_BLOCK), vlo:vlo + HEAD_DIM] += dvv[rows]
                dqw_ref[...] += jnp.sum(dqn * qhat, axis=0, keepdims=True)
                z = dqn * qw_v
                dqs = rq * (z - qhat * (_rowsum((z * qhat).astype(BF16), HEAD_DIM) * (1.0 / HEAD_DIM)))
                _unstack_heads(dq_scr.at[mine, :], dqs, g)
        dq_ref[...] = dq_scr[...].astype(BF16)

        @pl.when(step == nb // per - 1)
        def _():
            def one(c, dkw):
                rows = pl.ds(pl.multiple_of(c * chunk, chunk), chunk)
                for g in range(N_KV_HEADS):
                    lo = g * HEAD_DIM
                    k = kv_ref[rows, lo:lo + HEAD_DIM].astype(F32)
                    dx, dg_rows = _rms_bwd(dkv_ref[rows, lo:lo + HEAD_DIM], k, kw_ref[...])
                    dkv_ref[rows, lo:lo + HEAD_DIM] = dx
                    dkw = dkw + jnp.sum(dg_rows, axis=0, keepdims=True)
                return dkw

            dkw_ref[...] = lax.fori_loop(0, S // chunk, one, jnp.zeros((1, HEAD_DIM), F32))

    blk = pl.BlockSpec((per * ATTN_BLOCK, W), lambda n: (n, 0))
    whole_kv = pl.BlockSpec((S, KW), lambda n: (0, 0))
    vec = pl.BlockSpec((1, HEAD_DIM), lambda n: (0, 0))
    sk = pl.BlockSpec(sink_rows.shape, lambda n: (0, 0, 0))
    dsk = pl.BlockSpec((N_KV_HEADS, GQA_GROUP, 128), lambda n: (0, 0, 0))
    return pl.pallas_call(
        body, name="attn_bwd", grid=(nb // per,),
        in_specs=[blk, whole_kv, blk, vec, vec, sk, pl.BlockSpec(band.shape, lambda n: (0, 0))],
        out_specs=[blk, whole_kv, vec, vec, dsk],
        out_shape=[jax.ShapeDtypeStruct((S, W), BF16), jax.ShapeDtypeStruct((S, KW), F32),
                   jax.ShapeDtypeStruct((1, HEAD_DIM), F32), jax.ShapeDtypeStruct((1, HEAD_DIM), F32),
                   jax.ShapeDtypeStruct((N_KV_HEADS, GQA_GROUP, 128), F32)],
        scratch_shapes=[pltpu.VMEM((per * ATTN_BLOCK, W), F32)],
        compiler_params=_params(dimension_semantics=("arbitrary",)),
    )(q, kv, do, qw, kw, sink_rows, band)


def _pooled(xc, xprev, i):
    tm = xc.shape[0]
    xh = jnp.concatenate([jnp.where(i > 0, xprev, 0.0), xc], axis=0)
    t = lax.broadcasted_iota(jnp.int32, (tm, 1), 0) + i * tm
    out = []
    for gi, w in enumerate(POOL_WINDOWS):
        acc = xh[:, gi * POOL_GROUP:(gi + 1) * POOL_GROUP]
        sh = 1
        while sh < w:
            acc = acc + pltpu.roll(acc, sh, 0)
            sh *= 2
        cnt = jnp.minimum(t + 1, w).astype(F32)
        out.append(acc[POOL_HALO:, :] / cnt - xc[:, gi * POOL_GROUP:(gi + 1) * POOL_GROUP])
    return jnp.concatenate(out, axis=1)


def _pool_mix(pooled_b, pw_ref):
    return jnp.concatenate([_dot(pooled_b[:, gi * POOL_GROUP:(gi + 1) * POOL_GROUP], pw_ref[gi])
                            for gi in range(len(POOL_WINDOWS))], axis=1)


def _halo_specs(tm, width, S, after):
    per = tm // POOL_HALO
    last = S // POOL_HALO - 1
    if after:
        return pl.BlockSpec((POOL_HALO, width), lambda i: (jnp.minimum((i + 1) * per, last), 0))
    return pl.BlockSpec((POOL_HALO, width), lambda i: (jnp.maximum(i * per - 1, 0), 0))


def _mix_out(xp, attn, gl, bias, pw, pscale, wpot, wao, wo, h, tm=512):
    S, D = h.shape
    nt = S // tm
    PW = xp.shape[1]

    def body(xc_ref, xprev_ref, at_ref, gl_ref, bias_ref, pw_ref, ps_ref, wpo_ref, wao_ref, wo_ref, h_ref,
             ho_ref, bp_ref, ba_ref):
        i = pl.program_id(0)
        pooled = _pooled(xc_ref[...], xprev_ref[...], i).astype(BF16)
        ms = (_pool_mix(pooled, pw_ref) * ps_ref[...]).astype(BF16)
        bp = _dot_nt(ms, wpo_ref[...])
        ba = _dot(at_ref[...], wao_ref[...])
        bp_ref[...] = bp.astype(BF16)
        ba_ref[...] = ba.astype(BF16)
        gates = jax.nn.sigmoid(gl_ref[...].astype(F32) + bias_ref[...])
        merged = (gates[:, :D] * bp + gates[:, D:] * ba).astype(BF16)
        ho_ref[...] = h_ref[...] + _dot(merged, wo_ref[...])

    def row(width):
        return pl.BlockSpec((tm, width), lambda i: (i, 0))

    def whole(x):
        nd = x.ndim
        return pl.BlockSpec(x.shape, lambda i: (0,) * nd)

    return pl.pallas_call(
        body, name="mix_out", grid=(nt,),
        in_specs=[row(PW), _halo_specs(tm, PW, S, False), row(D), row(2 * D), whole(bias), whole(pw), whole(pscale),
                  whole(wpot), whole(wao), whole(wo), row(D)],
        out_specs=[row(D), row(D), row(D)],
        out_shape=[jax.ShapeDtypeStruct((S, D), F32), jax.ShapeDtypeStruct((S, D), BF16),
                   jax.ShapeDtypeStruct((S, D), BF16)],
        compiler_params=_params(dimension_semantics=("arbitrary",)),
    )(xp, xp, attn, gl, bias, pw, pscale, wpot, wao, wo, h)


def _mix_bwd_gate(dh, wo, bp, ba, gl, bias, tm=512):
    S, D = dh.shape
    nt = S // tm

    def body(dh_ref, wo_ref, bp_ref, ba_ref, gl_ref, bias_ref, dgl_ref, dbp_ref, dba_ref, dwo_ref, dbias_ref):
        i = pl.program_id(0)
        dhb = dh_ref[...].astype(BF16)
        dm = _dot_nt(dhb, wo_ref[...])
        gates = jax.nn.sigmoid(gl_ref[...].astype(F32) + bias_ref[...])
        gp, ga = gates[:, :D], gates[:, D:]
        bp_v = bp_ref[...].astype(F32)
        ba_v = ba_ref[...].astype(F32)
        merged = (gp * bp_v + ga * ba_v).astype(BF16)
        _acc(dwo_ref, _dot_tn(merged, dhb), i == 0)
        dbp_ref[...] = (dm * gp).astype(BF16)
        dba_ref[...] = (dm * ga).astype(BF16)
        dgl = jnp.concatenate([dm * bp_v * gp * (1.0 - gp), dm * ba_v * ga * (1.0 - ga)], axis=1)
        dgl_ref[...] = dgl.astype(BF16)
        _acc(dbias_ref, jnp.sum(dgl, axis=0, keepdims=True), i == 0)

    def row(width):
        return pl.BlockSpec((tm, width), lambda i: (i, 0))

    def whole(shape):
        return pl.BlockSpec(shape, lambda i: (0, 0))

    return pl.pallas_call(
        body, name="mix_bwd_gate", grid=(nt,),
        in_specs=[row(D), whole(wo.shape), row(D), row(D), row(2 * D), whole(bias.shape)],
        out_specs=[row(2 * D), row(D), row(D), whole((D, D)), whole((1, 2 * D))],
        out_shape=[jax.ShapeDtypeStruct((S, 2 * D), BF16), jax.ShapeDtypeStruct((S, D), BF16),
                   jax.ShapeDtypeStruct((S, D), BF16), jax.ShapeDtypeStruct((D, D), F32),
                   jax.ShapeDtypeStruct((1, 2 * D), F32)],
        compiler_params=_params(dimension_semantics=("arbitrary",)),
    )(dh, wo, bp, ba, gl, bias)


def _mix_bwd_branch(dbp, dba, attn, xp, pw, pscale, wpot, wao, tm=1024):
    S, D = dbp.shape
    nt = S // tm
    PW = xp.shape[1]
    NG = len(POOL_WINDOWS)

    def body(dbp_ref, dba_ref, at_ref, xc_ref, xprev_ref, pw_ref, ps_ref, wpo_ref, wao_ref,
             dat_ref, dpl_ref, dwao_ref, dwpo_ref, dpw_ref, dps_ref):
        i = pl.program_id(0)
        dba_v = dba_ref[...]
        dbp_v = dbp_ref[...]
        _acc(dwao_ref, _dot_tn(at_ref[...], dba_v), i == 0)
        dat_ref[...] = _dot_nt(dba_v, wao_ref[...]).astype(BF16)
        pooled = _pooled(xc_ref[...], xprev_ref[...], i).astype(BF16)
        mixed = _pool_mix(pooled, pw_ref)
        ps = ps_ref[...]
        _acc(dwpo_ref, _dot_tn(dbp_v, (mixed * ps).astype(BF16)), i == 0)
        dms = _dot(dbp_v, wpo_ref[...])
        _acc(dps_ref, jnp.sum(dms * mixed, axis=0, keepdims=True), i == 0)
        dmixed = (dms * ps).astype(BF16)
        dpooled = []
        for gi in range(NG):
            cols = slice(gi * POOL_GROUP, (gi + 1) * POOL_GROUP)
            _acc(dpw_ref.at[gi], _dot_tn(pooled[:, cols], dmixed[:, cols]), i == 0)
            dpooled.append(_dot_nt(dmixed[:, cols], pw_ref[gi]))
        dpl_ref[...] = jnp.concatenate(dpooled, axis=1)

    def row(width):
        return pl.BlockSpec((tm, width), lambda i: (i, 0))

    def whole(shape):
        nd = len(shape)
        return pl.BlockSpec(shape, lambda i: (0,) * nd)

    return pl.pallas_call(
        body, name="mix_bwd_branch", grid=(nt,),
        in_specs=[row(D), row(D), row(D), row(PW), _halo_specs(tm, PW, S, False), whole(pw.shape), whole(pscale.shape),
                  whole(wpot.shape), whole(wao.shape)],
        out_specs=[row(D), row(PW), whole((D, D)), whole((D, PW)), whole(pw.shape), whole((1, PW))],
        out_shape=[jax.ShapeDtypeStruct((S, D), BF16), jax.ShapeDtypeStruct((S, PW), F32),
                   jax.ShapeDtypeStruct((D, D), F32), jax.ShapeDtypeStruct((D, PW), F32),
                   jax.ShapeDtypeStruct(pw.shape, F32), jax.ShapeDtypeStruct((1, PW), F32)],
        compiler_params=_params(dimension_semantics=("arbitrary",)),
    )(dbp, dba, attn, xp, xp, pw, pscale, wpot, wao)


def _pool_bwd(dpooled, tm=512):
    S, PW = dpooled.shape
    nt = S // tm

    def body(dc_ref, dnext_ref, dxp_ref):
        i = pl.program_id(0)
        dc = dc_ref[...]
        dh = jnp.concatenate([dc, jnp.where(i < nt - 1, dnext_ref[...], 0.0)], axis=0)
        rows = tm + POOL_HALO
        t = lax.broadcasted_iota(jnp.int32, (rows, 1), 0) + i * tm
        out = []
        for gi, w in enumerate(POOL_WINDOWS):
            cols = slice(gi * POOL_GROUP, (gi + 1) * POOL_GROUP)
            acc = dh[:, cols] / jnp.minimum(t + 1, w).astype(F32)
            sh = 1
            while sh < w:
                acc = acc + pltpu.roll(acc, rows - sh, 0)
                sh *= 2
            out.append(acc[:tm, :] - dc[:, cols])
        dxp_ref[...] = jnp.concatenate(out, axis=1).astype(BF16)

    return pl.pallas_call(
        body, name="pool_bwd", grid=(nt,),
        in_specs=[pl.BlockSpec((tm, PW), lambda i: (i, 0)), _halo_specs(tm, PW, S, True)],
        out_specs=pl.BlockSpec((tm, PW), lambda i: (i, 0)), out_shape=jax.ShapeDtypeStruct((S, PW), BF16),
        compiler_params=_params(dimension_semantics=("arbitrary",)),
    )(dpooled, dpooled)


def _in_bwd_w(pieces, u, tm=1024):
    S, D = u.shape
    nt = S // tm
    NW = sum(width for _, width in SEGMENTS)

    def body(*refs):
        piece_refs, u_ref, dw_hbm, acc, sem = refs[:len(SEGMENTS)], refs[len(SEGMENTS)], refs[len(SEGMENTS) + 1], refs[-2], refs[-1]
        i = pl.program_id(0)
        u_t = u_ref[...]
        for (off, width), ref in zip(SEGMENTS, piece_refs):
            for lo in range(0, width, D):
                hi = min(lo + D, width)
                _acc(acc.at[off + lo:off + hi, :], _dot_tn(ref[:, lo:hi].astype(BF16), u_t), i == 0)

        @pl.when(i == nt - 1)
        def _():
            cp = pltpu.make_async_copy(acc, dw_hbm, sem)
            cp.start()
            cp.wait()

    return pl.pallas_call(
        body, name="in_bwd_w", grid=(nt,),
        in_specs=[pl.BlockSpec((tm, width), lambda i: (i, 0)) for _, width in SEGMENTS] + [pl.BlockSpec((tm, D), lambda i: (i, 0))],
        out_specs=pl.BlockSpec(memory_space=pl.ANY), out_shape=jax.ShapeDtypeStruct((NW, D), F32),
        scratch_shapes=[pltpu.VMEM((NW, D), F32), pltpu.SemaphoreType.DMA],
        compiler_params=_params(dimension_semantics=("arbitrary",)),
    )(*pieces, u)


def _in_bwd_x(pieces, wint, h, dh, gnorm, tm=512):
    S, D = h.shape
    nt = S // tm

    def body(*refs):
        piece_refs = refs[:len(SEGMENTS)]
        w_ref, h_ref, dh_ref, g_ref, dx_ref, dxh_ref, dg_ref = refs[len(SEGMENTS):]
        i = pl.program_id(0)
        du = jnp.zeros((tm, D), F32)
        for (off, width), ref in zip(SEGMENTS, piece_refs):
            du = du + _dot(ref[...].astype(BF16), w_ref[off:off + width, :])
        dx, dg_rows = _rms_bwd(du, h_ref[...], g_ref[...])
        out = dh_ref[...] + dx
        dx_ref[...] = out
        dxh_ref[...] = (0.5 * out).astype(BF16)
        _acc(dg_ref, jnp.sum(dg_rows, axis=0, keepdims=True), i == 0)

    def row(width):
        return pl.BlockSpec((tm, width), lambda i: (i, 0))

    vec = pl.BlockSpec((1, D), lambda i: (0, 0))
    return pl.pallas_call(
        body, name="in_bwd_x", grid=(nt,),
        in_specs=[row(width) for _, width in SEGMENTS] + [pl.BlockSpec(wint.shape, lambda i: (0, 0)), row(D), row(D), vec],
        out_specs=[row(D), row(D), vec],
        out_shape=[jax.ShapeDtypeStruct((S, D), F32), jax.ShapeDtypeStruct((S, D), BF16), jax.ShapeDtypeStruct((1, D), F32)],
        compiler_params=_params(dimension_semantics=("arbitrary",)),
    )(*pieces, wint, h, dh, gnorm)


def _row_tile(rows):
    for t in (512, 480, 352, 256, 128, 64, 32, 16, 8):
        if rows % t == 0:
            return t
    return rows


def _adamw_update(w, g, m, v):
    mn = ADAM_B1 * m + (1.0 - ADAM_B1) * g
    vn = ADAM_B2 * v + (1.0 - ADAM_B2) * (g * g)
    m_hat = mn / (1.0 - ADAM_B1 ** ADAM_STEP)
    v_hat = vn / (1.0 - ADAM_B2 ** ADAM_STEP)
    return -ADAM_LR * (m_hat / (jnp.sqrt(v_hat) + ADAM_EPS) + ADAM_WD * w), mn, vn


def _adamw_small(ws, gs, ms, vs):
    n = len(ws)

    def rows(a):
        return a.reshape(-1, a.shape[-1]) if a.ndim > 1 else a.reshape(1, -1)

    def body(*refs):
        for k in range(n):
            w_ref, g_ref, m_ref, v_ref, d_ref, mo_ref, vo_ref = refs[k::n]
            d_ref[...], mo_ref[...], vo_ref[...] = _adamw_update(w_ref[...], g_ref[...], m_ref[...], v_ref[...])

    vmem = pl.BlockSpec(memory_space=pltpu.VMEM)
    out = pl.pallas_call(
        body, name="adamw_small", in_specs=[vmem] * (4 * n), out_specs=[vmem] * (3 * n),
        out_shape=[jax.ShapeDtypeStruct(rows(w).shape, F32) for _ in range(3) for w in ws], compiler_params=_params(),
    )(*[rows(a) for a in ws + gs + ms + vs])
    return [[o.reshape(w.shape) for o, w in zip(out[k * n:(k + 1) * n], ws)] for k in range(3)]


def _adamw(w, g, m, v, token, name, turned=False):
    R, C = w.shape
    tr = R if turned else _row_tile(R)

    def body(w_ref, g_ref, m_ref, v_ref, _, d_ref, mo_ref, vo_ref, go_ref):
        gv = g_ref[...].T if turned else g_ref[...]
        go_ref[...] = gv
        d_ref[...], mo_ref[...], vo_ref[...] = _adamw_update(w_ref[...], gv, m_ref[...], v_ref[...])

    blk = pl.BlockSpec((tr, C), lambda i: (i, 0))
    g_blk = pl.BlockSpec((C, R), lambda i: (0, 0)) if turned else blk
    sh = jax.ShapeDtypeStruct((R, C), F32)
    return pl.pallas_call(
        body, name=name, grid=(R // tr,), in_specs=[blk, g_blk, blk, blk, TOKEN_SPEC], out_specs=[blk] * 4, out_shape=[sh] * 4,
        compiler_params=_params(dimension_semantics=("arbitrary",)),
    )(w, g, m, v, token)


def _cast_place(w, place, name, turned=False, after=()):
    R, C = w.shape[::-1] if turned else w.shape
    tr = R if turned else _row_tile(R)
    per = R // tr

    def body(p_ref, w_ref, *rest):
        rest[-1][...] = (w_ref[...].T if turned else w_ref[...]).astype(BF16)

    grid_spec = pltpu.PrefetchScalarGridSpec(
        num_scalar_prefetch=1, grid=(per,),
        in_specs=[pl.BlockSpec((C, R), lambda i, p: (0, 0)) if turned else pl.BlockSpec((tr, C), lambda i, p: (i, 0))]
        + [TOKEN_SPEC] * len(after),
        out_specs=pl.BlockSpec((tr, C), lambda i, p: (p[0] * per + i, 0)))
    return pl.pallas_call(
        body, name=name, grid_spec=grid_spec, out_shape=jax.ShapeDtypeStruct((N_SHARDS * R, C), BF16),
        compiler_params=_params(dimension_semantics=("arbitrary",)),
    )(place, w, *after)


def _parked(w, m, idx, first, last):
    return jnp.where(m == w, idx, jnp.where(m < w, first, last))


def _sum_halves(g4s, recvs, place, name):
    M = len(g4s)
    NS, R, C = g4s[0].shape
    hr = R // 2
    pieces = [(w, s) for w in range(M) for s in range(NS)]
    n_in, n_out = 3, 2

    def body(p_ref, *refs):
        g_refs, r_refs, o_refs = refs[:M], refs[M:2 * M], refs[2 * M:3 * M]
        gbuf, rbuf, obuf, in_sem, out_sem = refs[3 * M:]
        mine = pl.ds(pl.multiple_of(p_ref[1] * hr, 8), hr)

        def fetch(k):
            w, s = pieces[k]
            return (pltpu.make_async_copy(g_refs[w].at[s, mine, :], gbuf.at[k % n_in], in_sem.at[0, k % n_in]),
                    pltpu.make_async_copy(r_refs[w].at[s], rbuf.at[k % n_in], in_sem.at[1, k % n_in]))

        def put(k):
            w, s = pieces[k]
            return pltpu.make_async_copy(obuf.at[k % n_out], o_refs[w].at[s], out_sem.at[k % n_out])

        for k in range(min(n_in, len(pieces))):
            for copy in fetch(k):
                copy.start()
        for k in range(len(pieces)):
            for copy in fetch(k):
                copy.wait()
            if k >= n_out:
                put(k - n_out).wait()
            obuf[k % n_out] = (gbuf[k % n_in] + rbuf[k % n_in]).astype(BF16)
            put(k).start()
            if k + n_in < len(pieces):
                for copy in fetch(k + n_in):
                    copy.start()
        for k in range(max(0, len(pieces) - n_out), len(pieces)):
            put(k).wait()

    return pl.pallas_call(
        body, name=name, in_specs=[pl.BlockSpec(memory_space=pltpu.SMEM)] + [TOKEN_SPEC] * (2 * M),
        out_specs=[TOKEN_SPEC] * M, out_shape=[jax.ShapeDtypeStruct((NS, hr, C), BF16)] * M,
        scratch_shapes=[pltpu.VMEM((n_in, hr, C), F32), pltpu.VMEM((n_in, hr, C), F32), pltpu.VMEM((n_out, hr, C), BF16),
                        pltpu.SemaphoreType.DMA((2, n_in)), pltpu.SemaphoreType.DMA((n_out,))],
        compiler_params=_params(),
    )(place, *g4s, *recvs)


def _sum_quarters(h4s, recv3s, place, name):
    M = len(h4s)
    NS, hr, C = h4s[0].shape
    tr = _row_tile(hr)
    per = hr // tr

    def body(p_ref, *refs):
        m = pl.program_id(0)
        for w in range(M):
            @pl.when(m == w)
            def _(w=w):
                acc = refs[w][0].astype(F32)
                for k in range(N_SHARDS - 1):
                    acc = acc + refs[M + w][k].astype(F32)
                refs[2 * M + w][...] = acc

    def row(w, m, i):
        return _parked(w, m, i, 0, per - 1)

    grid_spec = pltpu.PrefetchScalarGridSpec(
        num_scalar_prefetch=1, grid=(M, per),
        in_specs=[pl.BlockSpec((1, tr, C), lambda m, i, p, w=w: (p[0], row(w, m, i), 0)) for w in range(M)]
        + [pl.BlockSpec((N_SHARDS - 1, tr, C), lambda m, i, p, w=w: (0, row(w, m, i), 0)) for w in range(M)],
        out_specs=[pl.BlockSpec((tr, C), lambda m, i, p, w=w: (p[1] * per + row(w, m, i), 0)) for w in range(M)])
    return pl.pallas_call(
        body, name=name, grid_spec=grid_spec, out_shape=[jax.ShapeDtypeStruct((2 * hr, C), F32)] * M,
        compiler_params=_params(dimension_semantics=("arbitrary",) * 2),
    )(place, *h4s, *recv3s)


def _by_shape(arrays):
    groups = {}
    for k, a in enumerate(arrays):
        groups.setdefault(a.shape, []).append(k)
    return list(groups.values())


def _place():
    x, y, c = lax.axis_index("x"), lax.axis_index("y"), lax.axis_index("c")
    chips = [(1 - x, y), (x, 1 - y), (1 - x, 1 - y)]
    return x, y, c, chips


HBM_SPEC = pl.BlockSpec(memory_space=pltpu.HBM)
SEM_SPEC = pl.BlockSpec(memory_space=pltpu.SEMAPHORE)
DATAFLOW = pltpu.SideEffectType.DATAFLOW_SIDE_EFFECTING


def _hbm(a):
    return pltpu.with_memory_space_constraint(a, pltpu.HBM)


class InFlight(NamedTuple):
    send_sem: jax.Array
    recv_sem: jax.Array
    bufs: list
    plan: Callable
    token: jax.Array


class Leg(NamedTuple):
    bufs: list
    landing: Optional[InFlight]
    plan: Optional[Callable]
    n_copies: int


def _send(bufs, plan, n_copies):
    return Leg([_hbm(b) for b in bufs], None, plan, n_copies)


def _land(flight):
    return Leg(flight.bufs, flight, None, 0)


def _forward(flight, plan, n_copies):
    return Leg(flight.bufs, flight, plan, n_copies)


def _wait_all(plan, refs, send_ref, recv_ref):
    for k, (src, dst, dev) in enumerate(plan(refs)):
        cp = pltpu.make_async_remote_copy(src_ref=src, dst_ref=dst, send_sem=send_ref.at[k], recv_sem=recv_ref.at[k],
                                          device_id=dev, device_id_type=MESH)
        cp.wait_send()
        cp.wait_recv()


def _start_all(plan, refs, send_ref, recv_ref):
    for k, (src, dst, dev) in enumerate(plan(refs)):
        pltpu.make_async_remote_copy(src_ref=src, dst_ref=dst, send_sem=send_ref.at[k], recv_sem=recv_ref.at[k],
                                     device_id=dev, device_id_type=MESH).start()


def _comm(name, legs, after):
    after = list(after) if isinstance(after, (list, tuple)) else [after]
    ins, in_specs, out_shape, out_specs, aliases, first = [], [], [], [], {}, []
    for leg in legs:
        first.append((len(ins), len(out_shape)))
        for b in leg.bufs:
            aliases[len(ins)] = len(out_shape)
            ins.append(b)
            in_specs.append(HBM_SPEC)
            out_shape.append(pltpu.HBM(b.shape, b.dtype))
            out_specs.append(HBM_SPEC)
        if leg.landing is not None:
            ins += [leg.landing.send_sem, leg.landing.recv_sem]
            in_specs += [SEM_SPEC, SEM_SPEC]
        if leg.plan is not None:
            out_shape += [pltpu.SemaphoreType.DMA((leg.n_copies,))] * 2
            out_specs += [SEM_SPEC, SEM_SPEC]
    starts = any(leg.plan is not None for leg in legs)
    if starts:
        out_shape.append(jax.ShapeDtypeStruct((8, 128), F32))
        out_specs.append(pl.BlockSpec(memory_space=pltpu.VMEM))
    n_in = len(ins) + len(after)

    def body(*refs):
        outs = refs[n_in:]
        for leg, (i, o) in zip(legs, first):
            nb = len(leg.bufs)
            bufs = refs[i:i + nb]
            if leg.landing is not None:
                _wait_all(leg.landing.plan, bufs, refs[i + nb], refs[i + nb + 1])
            if leg.plan is not None:
                _start_all(leg.plan, bufs, outs[o + nb], outs[o + nb + 1])
        if starts:
            outs[-1][...] = jnp.zeros_like(outs[-1])

    out = pl.pallas_call(
        body, name=name, in_specs=in_specs + [TOKEN_SPEC] * len(after), out_shape=out_shape, out_specs=out_specs,
        input_output_aliases=aliases, compiler_params=pltpu.CompilerParams(has_side_effects=DATAFLOW),
    )(*ins, *after)
    results = []
    for leg, (_, o) in zip(legs, first):
        nb = len(leg.bufs)
        bufs = list(out[o:o + nb])
        results.append(bufs if leg.plan is None else InFlight(out[o + nb], out[o + nb + 1], bufs, leg.plan, out[-1]))
    return results


def _half_rows(buf, chip, core):
    hr = buf.shape[0] // (2 * N_SHARDS)
    return buf.at[pl.ds(pl.multiple_of((2 * chip + core) * hr, 16), hr)]


def _gather_ici_plan(bufs):
    x, y, c, chips = _place()
    return [(_half_rows(b, 2 * x + y, c), _half_rows(b, 2 * x + y, c), (px, py, c)) for b in bufs for px, py in chips]


def _gather_d2d_plan(bufs):
    x, y, c, chips = _place()
    return [(_half_rows(b, 2 * px + py, c), _half_rows(b, 2 * px + py, c), (x, y, 1 - c)) for b in bufs for px, py in chips]


def _swap_plan(bufs):
    x, y, c, _ = _place()
    n = len(bufs) // 2
    copies = []
    for g, land in zip(bufs[:n], bufs[n:]):
        hr = g.shape[1] // 2
        copies.append((g.at[:, pl.ds(pl.multiple_of((1 - c) * hr, 8), hr)], land, (x, y, 1 - c)))
    return copies


def _exchange_plan(bufs):
    x, y, c, chips = _place()
    n = len(bufs) // 2
    return [(h.at[2 * px + py], land.at[k], (px, py, c))
            for h, land in zip(bufs[:n], bufs[n:]) for k, (px, py) in enumerate(chips)]


def _share_plan(bufs):
    x, y, c, _ = _place()
    copies = []
    for buf in bufs:
        hr = buf.shape[0] // 2
        mine = buf.at[pl.ds(pl.multiple_of(c * hr, 8), hr)]
        copies.append((mine, mine, (x, y, 1 - c)))
    return copies


N_DEVICES = 8


def _slot_place(vec, place):
    R, C = vec.shape

    def body(p_ref, v_ref, o_ref):
        o_ref[0] = v_ref[...]

    grid_spec = pltpu.PrefetchScalarGridSpec(
        num_scalar_prefetch=1, grid=(1,), in_specs=[pl.BlockSpec((R, C), lambda i, p: (0, 0))],
        out_specs=pl.BlockSpec((1, R, C), lambda i, p: (2 * p[0] + p[1], 0, 0)))
    return pl.pallas_call(
        body, name="slot_place", grid_spec=grid_spec, out_shape=jax.ShapeDtypeStruct((N_DEVICES, R, C), F32),
        compiler_params=_params(dimension_semantics=("arbitrary",)),
    )(place, vec)


def _slots_plan(bufs):
    x, y, c, _ = _place()
    mine = bufs[0].at[4 * x + 2 * y + c]
    return [(mine, mine, (x ^ (r >> 2), y ^ ((r >> 1) & 1), c ^ (r & 1))) for r in range(1, N_DEVICES)]


def _sum_slots(slots):
    _, R, C = slots.shape

    def body(s_ref, o_ref):
        acc = s_ref[0]
        for d in range(1, N_DEVICES):
            acc = acc + s_ref[d]
        o_ref[...] = acc

    return pl.pallas_call(
        body, name="sum_slots", in_specs=[pl.BlockSpec(memory_space=pltpu.VMEM)],
        out_specs=pl.BlockSpec(memory_space=pltpu.VMEM), out_shape=jax.ShapeDtypeStruct((R, C), F32),
        compiler_params=_params(),
    )(slots)


SMALL = ("ffn1_norm", "mix_norm", "pool_w", "pool_scale", "q_norm", "k_norm", "sinks", "gate_bias", "ffn2_norm")
SMALL_COLS = 1024
FFN1 = ("ffn1_w_gate", "ffn1_w_up", "ffn1_w_down")
MIXER = ("w_in", "w_pool_out", "w_attn_out", "w_out")
FFN2 = ("ffn2_w_gate", "ffn2_w_up", "ffn2_w_down")
LARGE = FFN1 + MIXER + FFN2
TRANSPOSED = ("ffn1_w_gate", "ffn1_w_up", "w_in", "w_pool_out", "ffn2_w_gate", "ffn2_w_up")
TURNED = ("w_pool_out",)
WEIGHTS = ("ffn1_norm", "ffn1_w_gate", "ffn1_w_up", "ffn1_w_down", "mix_norm", "w_in", "pool_w", "pool_scale",
           "w_pool_out", "q_norm", "k_norm", "sinks", "w_attn_out", "gate_bias", "w_out", "ffn2_norm",
           "ffn2_w_gate", "ffn2_w_up", "ffn2_w_down")


def _pack_small(parts):
    flat = jnp.concatenate([p.reshape(-1) for p in parts])
    rows = -(-flat.shape[0] // (8 * SMALL_COLS)) * 8
    return jnp.pad(flat, (0, rows * SMALL_COLS - flat.shape[0])).reshape(rows, SMALL_COLS)


def _unpack_small(packed, like):
    flat = packed.reshape(-1)
    out, off = [], 0
    for p in like:
        out.append(flat[off:off + p.size].reshape(p.shape))
        off += p.size
    return out, flat[off]


def _tie(small, token):
    return small + token[0, 0]


class Reduction:
    def __init__(self, names, place):
        self.names, self.place = names, place

    def swap(self, grads):
        grads = [g.reshape(N_SHARDS, -1, g.shape[-1]) for g in grads]
        lands = [lax.empty((g.shape[0], g.shape[1] // 2, g.shape[2]), g.dtype) for g in grads]
        return _send(grads + lands, _swap_plan, len(grads))

    def exchange(self, swapped):
        n = len(self.names)
        halves = self._per_shape(_sum_halves, "sum_halves", swapped[:n], swapped[n:])
        lands = [lax.empty((N_SHARDS - 1,) + h.shape[1:], h.dtype) for h in halves]
        return _send(halves + lands, _exchange_plan, (N_SHARDS - 1) * n)

    def share(self, exchanged):
        n = len(self.names)
        return _send(self._per_shape(_sum_quarters, "sum_quarters", exchanged[:n], exchanged[n:]), _share_plan, n)

    def named(self, shared):
        return dict(zip(self.names, shared))

    def _per_shape(self, add, stage, mine, received):
        out = [None] * len(mine)
        for idx in _by_shape(mine):
            sums = add([mine[k] for k in idx], [received[k] for k in idx], self.place, f"{stage}_{self.names[idx[0]]}")
            for k, v in zip(idx, sums):
                out[k] = v
        return out


def _row_form(name, a):
    return a.T if name in TRANSPOSED else a


def _shard_form(name, a):
    return a if name in TURNED else _row_form(name, a)


def kernel(x, ffn1_norm, ffn1_w_gate, ffn1_w_up, ffn1_w_down, mix_norm, w_in, pool_w, pool_scale, w_pool_out, q_norm, k_norm, sinks, w_attn_out, gate_bias, w_out, ffn2_norm, ffn2_w_gate, ffn2_w_up, ffn2_w_down, loss_target, m_ffn1_norm, m_ffn1_w_gate, m_ffn1_w_up, m_ffn1_w_down, m_mix_norm, m_w_in, m_pool_w, m_pool_scale, m_w_pool_out, m_q_norm, m_k_norm, m_sinks, m_w_attn_out, m_gate_bias, m_w_out, m_ffn2_norm, m_ffn2_w_gate, m_ffn2_w_up, m_ffn2_w_down, v_ffn1_norm, v_ffn1_w_gate, v_ffn1_w_up, v_ffn1_w_down, v_mix_norm, v_w_in, v_pool_w, v_pool_scale, v_w_pool_out, v_q_norm, v_k_norm, v_sinks, v_w_attn_out, v_gate_bias, v_w_out, v_ffn2_norm, v_ffn2_w_gate, v_ffn2_w_up, v_ffn2_w_down):
    args = dict(locals())
    wts = {n: _shard_form(n, args[n]) for n in WEIGHTS}
    mom = {n: _shard_form(n, args["m_" + n]) for n in WEIGHTS}
    var = {n: _shard_form(n, args["v_" + n]) for n in WEIGHTS}
    shard = 2 * lax.axis_index("x") + lax.axis_index("y")
    place = jnp.stack([shard, lax.axis_index("c")]).astype(jnp.int32)

    xs, target = x[0], loss_target[0]
    D = xs.shape[1]
    g1 = wts["ffn1_norm"].reshape(1, D)
    gm = wts["mix_norm"].reshape(1, D)
    g2 = wts["ffn2_norm"].reshape(1, D)
    qw = wts["q_norm"].reshape(1, HEAD_DIM)
    kw = wts["k_norm"].reshape(1, HEAD_DIM)
    bias = wts["gate_bias"].reshape(1, 2 * D)
    pscale = wts["pool_scale"].reshape(1, -1)
    pw = wts["pool_w"].astype(BF16)
    sink_rows = jnp.broadcast_to(jnp.repeat(wts["sinks"], ATTN_BLOCK).reshape(N_KV_HEADS, GQA_GROUP * ATTN_BLOCK, 1),
                                 (N_KV_HEADS, GQA_GROUP * ATTN_BLOCK, 128))

    gate1, up1, down1 = FFN1[:1], FFN1[1:2], FFN1[2:]
    mix_in, mix_out = MIXER[:1], MIXER[1:]

    def over_ici(names):
        return _send([placed[n] for n in names], _gather_ici_plan, 3 * len(names))

    def to_sibling(flight):
        return _forward(flight, _gather_d2d_plan, len(flight.bufs) * 3)

    placed = {n: _cast_place(wts[n], place, "cast_" + n) for n in gate1}
    gate1_ici, = _comm("gather_gate1_ici", [over_ici(gate1)], place)
    placed.update({n: _cast_place(wts[n], place, "cast_" + n) for n in up1 + down1})
    up1_ici, down1_ici = _comm("gather_ffn1_ici", [over_ici(up1), over_ici(down1)], gate1_ici.token)
    placed.update({n: _cast_place(wts[n], place, "cast_" + n, turned=n in TURNED) for n in MIXER})
    band = _band_bias()
    shadow = [sink_rows, pw, band, down1_ici.token]
    gate1_d2d, in_ici, out_ici = _comm(
        "gather_mix_ici", [to_sibling(gate1_ici), over_ici(mix_in), over_ici(mix_out)], shadow)
    w = dict(zip(gate1, _comm("gather_gate1_wait", [_land(gate1_d2d)], gate1_d2d.token)[0]))

    n1, a1 = _ffn_gate(xs, g1, w["ffn1_w_gate"], "ffn1_gate")
    placed.update({n: _cast_place(wts[n], place, "cast_" + n, after=[a1]) for n in FFN2})
    up1_d2d, ffn2_ici, down1_d2d = _comm(
        "gather_ffn1_d2d", [to_sibling(up1_ici), over_ici(FFN2), to_sibling(down1_ici)], a1)
    w.update(zip(up1, _comm("gather_up1_wait", [_land(up1_d2d)], up1_d2d.token)[0]))
    p1, q1, s1 = _ffn_hidden(n1, a1, w["ffn1_w_up"], "ffn1_hidden")
    w.update(zip(down1, _comm("gather_down1_wait", [_land(down1_d2d)], s1)[0]))
    h1 = _ffn_down(xs, s1, w["ffn1_w_down"], "ffn1_down")
    in_d2d, out_d2d = _comm("gather_mix_d2d", [to_sibling(in_ici), to_sibling(out_ici)], h1)
    w.update(zip(mix_in, _comm("gather_in_wait", [_land(in_d2d)], in_d2d.token)[0]))
    u, xp, q, kv, gl = _mix_proj(h1, gm, w["w_in"])
    attn = _attn_fwd(q, kv, qw, kw, sink_rows, band)
    ffn2_d2d, landed = _comm("gather_ffn2_d2d", [to_sibling(ffn2_ici), _land(out_d2d)], attn)
    w.update(zip(mix_out, landed))
    h2, bp, ba = _mix_out(xp, attn, gl, bias, pw, pscale, w["w_pool_out"], w["w_attn_out"], w["w_out"], h1)
    w.update(zip(FFN2, _comm("gather_ffn2_wait", [_land(ffn2_d2d)], h2)[0]))
    n2, p2, q2, s2 = _ffn_up(h2, g2, w["ffn2_w_gate"], w["ffn2_w_up"], "ffn2_up")
    dy, dyh, loss = _ffn_down_loss(h2, s2, w["ffn2_w_down"], target, "ffn2_down_loss")

    gw, gs = {}, {}
    da, db = _ffn_bwd_h(dyh, p2, q2, w["ffn2_w_down"], ffn2_d2d.token, "ffn2_bwd_h")
    gw["ffn2_w_gate"], gw["ffn2_w_up"] = _xty(da, n2, "ffn2_dw_gate"), _xty(db, n2, "ffn2_dw_up")
    gw["ffn2_w_down"] = _xty(s2, dyh, "ffn2_dw_down")
    red2 = Reduction(FFN2, place)
    swap, = _comm("reduce_ffn2_swap", [red2.swap([gw[n] for n in FFN2])], da)
    dh2, gs["ffn2_norm"] = _ffn_bwd_x(da, db, w["ffn2_w_gate"], w["ffn2_w_up"], h2, dy, _tie(g2, swap.token), "ffn2_bwd_x")
    swapped, = _comm("reduce_ffn2_swap_wait", [_land(swap)], dh2)
    exchange, = _comm("reduce_ffn2_exchange", [red2.exchange(swapped)], place)
    dgl, dbp, dba, gw["w_out"], gs["gate_bias"] = _mix_bwd_gate(dh2, w["w_out"], bp, ba, gl, _tie(bias, exchange.token))
    dattn, dpooled, gw["w_attn_out"], gw["w_pool_out"], gs["pool_w"], gs["pool_scale"] = _mix_bwd_branch(
        dbp, dba, attn, xp, pw, pscale, w["w_pool_out"], w["w_attn_out"])
    dq, dkv, gs["q_norm"], gs["k_norm"], dsk = _attn_bwd(q, kv, dattn, qw, kw, sink_rows, band)
    gs["sinks"] = dsk[:, :, 0]
    exchanged, = _comm("reduce_ffn2_exchange_wait", [_land(exchange)], dq)
    share2, = _comm("reduce_ffn2_share", [red2.share(exchanged)], place)
    pieces = (_pool_bwd(dpooled), dq, dkv, dgl)
    gw["w_in"] = _in_bwd_w(pieces, u)
    redm = Reduction(MIXER, place)
    swap, shared = _comm("reduce_mix_swap", [redm.swap([gw[n] for n in MIXER]), _land(share2)], share2.token)
    grads = red2.named(shared)
    dh1, dh1h, gs["mix_norm"] = _in_bwd_x(pieces, w["w_in"], h1, dh2, _tie(gm, swap.token))
    red1, swaps = Reduction(FFN1, place), {}
    dw_down = _xty(s1, dh1h, "ffn1_dw_down")
    swaps["ffn1_w_down"], swapped = _comm("reduce_mix_swap_wait", [red1.swap([dw_down]), _land(swap)], dh1)
    exchange, = _comm("reduce_mix_exchange", [redm.exchange(swapped)], place)
    da, db = _ffn_bwd_h(dh1h, p1, q1, w["ffn1_w_down"], exchange.token, "ffn1_bwd_h")
    dw_up = _xty(db, n1, "ffn1_dw_up")
    swaps["ffn1_w_up"], = _comm("reduce_ffn1_swap_up", [red1.swap([dw_up])], place)
    dw_gate = _xty(da, n1, "ffn1_dw_gate", after=[swaps["ffn1_w_up"].token])
    swaps["ffn1_w_gate"], exchanged = _comm("reduce_mix_exchange_wait", [red1.swap([dw_gate]), _land(exchange)], place)
    sharem, *landed = _comm("reduce_mix_share", [redm.share(exchanged)] + [_land(swaps[n]) for n in FFN1], place)
    exchange, = _comm("reduce_ffn1_exchange", [red1.exchange([l[0] for l in landed] + [l[1] for l in landed])], place)
    grad_x, gs["ffn1_norm"] = _ffn_bwd_x(da, db, w["ffn1_w_gate"], w["ffn1_w_up"], xs, dh1, _tie(g1, exchange.token), "ffn1_bwd_x")

    small_parts = [gs[n] for n in SMALL] + [loss[0, 0].reshape(1)]
    slots, shared = _comm(
        "gather_small", [_send([_slot_place(_pack_small(small_parts), place)], _slots_plan, N_DEVICES - 1), _land(sharem)],
        grad_x)
    grads.update(redm.named(shared))
    delta, new_m, new_v = {}, {}, {}

    def adamw(names, token):
        for n in names:
            delta[n], new_m[n], new_v[n], grads[n] = _adamw(wts[n], grads[n], mom[n], var[n], token, "adamw_" + n, n in TURNED)

    early, late = FFN2 + MIXER[:-1], MIXER[-1:]
    adamw(early, slots.token)
    exchanged, = _comm("reduce_ffn1_exchange_wait", [_land(exchange)], [delta[n] for n in early])
    share1, landed = _comm("reduce_ffn1_share", [red1.share(exchanged), _land(slots)], place)
    summed = _sum_slots(landed[0])
    small_grads, loss_sum = _unpack_small(summed, [wts[n] for n in SMALL])
    grads.update(dict(zip(SMALL, small_grads)))
    for out, vals in zip((delta, new_m, new_v), _adamw_small(*[[d[n] for n in SMALL] for d in (wts, grads, mom, var)])):
        out.update(dict(zip(SMALL, vals)))
    adamw(late, share1.token)
    results = {(k, n): _shard_form(n, d[n]) for k, d in enumerate((grads, delta, new_m, new_v)) for n in SMALL + FFN2 + MIXER}
    shared, = _comm("reduce_ffn1_share_wait", [_land(share1)], [results[k, n] for k in range(4) for n in SMALL + late])
    grads.update(red1.named(shared))
    adamw(FFN1, share1.token)
    results.update({(k, n): _shard_form(n, d[n]) for k, d in enumerate((grads, delta, new_m, new_v)) for n in FFN1})
    return (loss_sum, grad_x[None], *[results[k, n] for k in range(4) for n in WEIGHTS])
```

```python
from typing import Callable, NamedTuple, Optional

import jax
import jax.numpy as jnp
from jax import lax
from jax.experimental import pallas as pl
from jax.experimental.pallas import tpu as pltpu

F32 = jnp.float32
BF16 = jnp.bfloat16
RMS_EPS = 1e-6
POOL_WINDOWS = (2, 4, 8, 16)
POOL_GROUP = 128
POOL_HALO = 16
HEAD_DIM = 64
GQA_GROUP = 8
N_KV_HEADS = 2
ATTN_BLOCK = 128
SCALE = HEAD_DIM ** -0.5
NEG = -1e30
N_SHARDS = 4
ADAM_LR, ADAM_B1, ADAM_B2, ADAM_EPS, ADAM_WD, ADAM_STEP = 0.001, 0.9, 0.999, 1e-08, 0.01, 10
VMEM_LIMIT = 56 * 1024 * 1024
MESH = pl.DeviceIdType.MESH
SEG_POOL, SEG_Q, SEG_KV, SEG_GATE = (0, 512), (512, 1024), (1536, 256), (1792, 2048)
SEGMENTS = (SEG_POOL, SEG_Q, SEG_KV, SEG_GATE)


def _params(**kw):
    return pltpu.CompilerParams(vmem_limit_bytes=VMEM_LIMIT, **kw)


def _dot(a, b):
    return jnp.dot(a, b, preferred_element_type=F32)


def _dot_nt(a, b):
    return lax.dot_general(a, b, (((1,), (1,)), ((), ())), preferred_element_type=F32)


def _dot_tn(a, b):
    return lax.dot_general(a, b, (((0,), (0,)), ((), ())), preferred_element_type=F32)


def _rinv(x):
    return lax.rsqrt(jnp.mean(x * x, axis=-1, keepdims=True) + RMS_EPS)


def _rms_bwd(dn, x, g):
    r = _rinv(x)
    xr = x * r
    z = dn * g
    dx = r * (z - xr * jnp.mean(z * xr, axis=-1, keepdims=True))
    return dx, dn * xr


def _acc(ref, val, first):
    @pl.when(first)
    def _():
        ref[...] = val

    @pl.when(jnp.logical_not(first))
    def _():
        ref[...] += val


TOKEN_SPEC = pl.BlockSpec(memory_space=pl.ANY)
F_HALF = 1408


def _resident(w):
    return pl.BlockSpec(w.shape, lambda i: (0, 0), pipeline_mode=pl.Buffered(1))


def _swiglu(a, b):
    sig = jax.nn.sigmoid(a)
    q = a * sig
    return b * (sig + q * (1.0 - sig)), q, q * b


def _ffn_up(h, gnorm, wgt, wut, name, tm=512):
    S, D = h.shape
    F = wgt.shape[0]

    def body(h_ref, g_ref, wg_ref, wu_ref, n_ref, p_ref, q_ref, s_ref):
        x = h_ref[...]
        n = (x * _rinv(x) * g_ref[...]).astype(BF16)
        n_ref[...] = n
        for lo in range(0, F, F_HALF):
            cols = slice(lo, lo + F_HALF)
            p, q, s = _swiglu(_dot_nt(n, wg_ref[cols, :]), _dot_nt(n, wu_ref[cols, :]))
            p_ref[:, cols] = p.astype(BF16)
            q_ref[:, cols] = q.astype(BF16)
            s_ref[:, cols] = s.astype(BF16)

    act = pl.BlockSpec((tm, F), lambda i: (i, 0))
    hidden = jax.ShapeDtypeStruct((S, F), BF16)
    return pl.pallas_call(
        body, name=name, grid=(S // tm,),
        in_specs=[pl.BlockSpec((tm, D), lambda i: (i, 0)), pl.BlockSpec((1, D), lambda i: (0, 0)), _resident(wgt), _resident(wut)],
        out_specs=[pl.BlockSpec((tm, D), lambda i: (i, 0)), act, act, act],
        out_shape=[jax.ShapeDtypeStruct((S, D), BF16), hidden, hidden, hidden],
        compiler_params=_params(dimension_semantics=("arbitrary",)),
    )(h, gnorm, wgt, wut)


def _ffn_gate(h, gnorm, wgt, name, tm=512):
    S, D = h.shape
    F = wgt.shape[0]

    def body(h_ref, g_ref, wg_ref, n_ref, a_ref):
        x = h_ref[...]
        n = (x * _rinv(x) * g_ref[...]).astype(BF16)
        n_ref[...] = n
        for lo in range(0, F, F_HALF):
            a_ref[:, lo:lo + F_HALF] = _dot_nt(n, wg_ref[lo:lo + F_HALF, :]).astype(BF16)

    row = pl.BlockSpec((tm, D), lambda i: (i, 0))
    return pl.pallas_call(
        body, name=name, grid=(S // tm,), in_specs=[row, pl.BlockSpec((1, D), lambda i: (0, 0)), _resident(wgt)],
        out_specs=[row, pl.BlockSpec((tm, F), lambda i: (i, 0))],
        out_shape=[jax.ShapeDtypeStruct((S, D), BF16), jax.ShapeDtypeStruct((S, F), BF16)],
        compiler_params=_params(dimension_semantics=("arbitrary",)),
    )(h, gnorm, wgt)


def _ffn_hidden(n, a, wut, name, tm=512):
    S, D = n.shape
    F = wut.shape[0]

    def body(n_ref, a_ref, wu_ref, p_ref, q_ref, s_ref):
        n_t = n_ref[...]
        for lo in range(0, F, F_HALF):
            cols = slice(lo, lo + F_HALF)
            p, q, s = _swiglu(a_ref[:, cols].astype(F32), _dot_nt(n_t, wu_ref[cols, :]))
            p_ref[:, cols] = p.astype(BF16)
            q_ref[:, cols] = q.astype(BF16)
            s_ref[:, cols] = s.astype(BF16)

    act = pl.BlockSpec((tm, F), lambda i: (i, 0))
    hidden = jax.ShapeDtypeStruct((S, F), BF16)
    return pl.pallas_call(
        body, name=name, grid=(S // tm,), in_specs=[pl.BlockSpec((tm, D), lambda i: (i, 0)), act, _resident(wut)],
        out_specs=[act, act, act], out_shape=[hidden, hidden, hidden],
        compiler_params=_params(dimension_semantics=("arbitrary",)),
    )(n, a, wut)


def _ffn_down(h, s, wd, name, tm=1024):
    S, D = h.shape
    F = wd.shape[0]

    def body(h_ref, s_ref, wd_ref, o_ref):
        o_ref[...] = h_ref[...] + 0.5 * _dot(s_ref[...], wd_ref[...])

    row = pl.BlockSpec((tm, D), lambda i: (i, 0))
    return pl.pallas_call(
        body, name=name, grid=(S // tm,), in_specs=[row, pl.BlockSpec((tm, F), lambda i: (i, 0)), _resident(wd)],
        out_specs=row, out_shape=jax.ShapeDtypeStruct((S, D), F32),
        compiler_params=_params(dimension_semantics=("arbitrary",)),
    )(h, s, wd)


def _ffn_down_loss(h, s, wd, target, name, tm=512):
    S, D = h.shape
    F = wd.shape[0]

    def body(h_ref, s_ref, wd_ref, t_ref, dy_ref, dyh_ref, loss_ref):
        e = h_ref[...] + 0.5 * _dot(s_ref[...], wd_ref[...]) - t_ref[...]
        dy = e * (1.0 / D)
        dy_ref[...] = dy
        dyh_ref[...] = (0.5 * dy).astype(BF16)
        tot = jnp.sum(jnp.sum(e * e, axis=1, keepdims=True), axis=0, keepdims=True) * (0.5 / D)
        _acc(loss_ref, jnp.broadcast_to(tot, loss_ref.shape), pl.program_id(0) == 0)

    row = pl.BlockSpec((tm, D), lambda i: (i, 0))
    return pl.pallas_call(
        body, name=name, grid=(S // tm,), in_specs=[row, pl.BlockSpec((tm, F), lambda i: (i, 0)), _resident(wd), row],
        out_specs=[row, row, pl.BlockSpec((8, 128), lambda i: (0, 0))],
        out_shape=[jax.ShapeDtypeStruct((S, D), F32), jax.ShapeDtypeStruct((S, D), BF16), jax.ShapeDtypeStruct((8, 128), F32)],
        compiler_params=_params(dimension_semantics=("arbitrary",)),
    )(h, s, wd, target)


def _ffn_bwd_h(dyh, p, q, wd, token, name, tm=512):
    S, D = dyh.shape
    F = wd.shape[0]

    def body(dyh_ref, p_ref, q_ref, wd_ref, _, da_ref, db_ref):
        dyh_t = dyh_ref[...]
        for lo in range(0, F, F_HALF):
            cols = slice(lo, lo + F_HALF)
            ds = _dot_nt(dyh_t, wd_ref[cols, :])
            da_ref[:, cols] = (ds * p_ref[:, cols].astype(F32)).astype(BF16)
            db_ref[:, cols] = (ds * q_ref[:, cols].astype(F32)).astype(BF16)

    act = pl.BlockSpec((tm, F), lambda i: (i, 0))
    hidden = jax.ShapeDtypeStruct((S, F), BF16)
    return pl.pallas_call(
        body, name=name, grid=(S // tm,),
        in_specs=[pl.BlockSpec((tm, D), lambda i: (i, 0)), act, act, _resident(wd), TOKEN_SPEC],
        out_specs=[act, act], out_shape=[hidden, hidden],
        compiler_params=_params(dimension_semantics=("arbitrary",)),
    )(dyh, p, q, wd, token)


def _xty(x, y, name, tk=2048, tf=1408, after=()):
    S, F = x.shape
    D = y.shape[1]

    def body(x_ref, y_ref, *rest):
        _acc(rest[-1], _dot_tn(x_ref[...], y_ref[...]), pl.program_id(1) == 0)

    return pl.pallas_call(
        body, name=name, grid=(F // tf, S // tk),
        in_specs=[pl.BlockSpec((tk, tf), lambda j, k: (k, j)), pl.BlockSpec((tk, D), lambda j, k: (k, 0))]
        + [TOKEN_SPEC] * len(after),
        out_specs=pl.BlockSpec((tf, D), lambda j, k: (j, 0)), out_shape=jax.ShapeDtypeStruct((F, D), F32),
        compiler_params=_params(dimension_semantics=("arbitrary", "arbitrary")),
    )(x, y, *after)


def _ffn_bwd_x(da, db, wgt, wut, x, dh, gnorm, name, tm=512):
    S, D = x.shape
    F = wgt.shape[0]

    def body(da_ref, db_ref, wg_ref, wu_ref, x_ref, dh_ref, g_ref, dx_ref, dg_ref):
        dn = _dot(da_ref[...], wg_ref[...]) + _dot(db_ref[...], wu_ref[...])
        dx, dg_rows = _rms_bwd(dn, x_ref[...], g_ref[...])
        dx_ref[...] = dh_ref[...] + dx
        _acc(dg_ref, jnp.sum(dg_rows, axis=0, keepdims=True), pl.program_id(0) == 0)

    row = pl.BlockSpec((tm, D), lambda i: (i, 0))
    act = pl.BlockSpec((tm, F), lambda i: (i, 0))
    vec = pl.BlockSpec((1, D), lambda i: (0, 0))
    return pl.pallas_call(
        body, name=name, grid=(S // tm,), in_specs=[act, act, _resident(wgt), _resident(wut), row, row, vec],
        out_specs=[row, vec],
        out_shape=[jax.ShapeDtypeStruct((S, D), F32), jax.ShapeDtypeStruct((1, D), F32)],
        compiler_params=_params(dimension_semantics=("arbitrary",)),
    )(da, db, wgt, wut, x, dh, gnorm)


def _mix_proj(h, gnorm, wint, tm=512):
    S, D = h.shape
    nt = S // tm

    def body(h_ref, g_ref, w_ref, u_ref, xp_ref, q_ref, kv_ref, gl_ref):
        x = h_ref[...]
        u = (x * _rinv(x) * g_ref[...]).astype(BF16)
        u_ref[...] = u
        for (off, width), ref in zip(SEGMENTS, (xp_ref, q_ref, kv_ref, gl_ref)):
            ref[...] = _dot_nt(u, w_ref[off:off + width, :]).astype(ref.dtype)

    def row(width):
        return pl.BlockSpec((tm, width), lambda i: (i, 0))

    return pl.pallas_call(
        body, name="mix_proj", grid=(nt,),
        in_specs=[row(D), pl.BlockSpec((1, D), lambda i: (0, 0)), pl.BlockSpec(wint.shape, lambda i: (0, 0))],
        out_specs=[row(D), row(SEG_POOL[1]), row(SEG_Q[1]), row(SEG_KV[1]), row(SEG_GATE[1])],
        out_shape=[jax.ShapeDtypeStruct((S, D), BF16), jax.ShapeDtypeStruct((S, SEG_POOL[1]), F32),
                   jax.ShapeDtypeStruct((S, SEG_Q[1]), BF16), jax.ShapeDtypeStruct((S, SEG_KV[1]), BF16),
                   jax.ShapeDtypeStruct((S, SEG_GATE[1]), BF16)],
        compiler_params=_params(dimension_semantics=("arbitrary",)),
    )(h, gnorm, wint)


def _stack_heads(x, g):
    return jnp.concatenate([x[:, (GQA_GROUP * g + hh) * HEAD_DIM:(GQA_GROUP * g + hh + 1) * HEAD_DIM]
                            for hh in range(GQA_GROUP)], axis=0)


def _unstack_heads(ref, val, g):
    for hh in range(GQA_GROUP):
        lo = (GQA_GROUP * g + hh) * HEAD_DIM
        ref[:, lo:lo + HEAD_DIM] = val[hh * ATTN_BLOCK:(hh + 1) * ATTN_BLOCK, :]


def _rowsum(xb, width):
    return _dot(xb, jnp.ones((xb.shape[1], width), BF16))


def _rinv_lanes(x):
    return lax.rsqrt(_rowsum((x * x).astype(BF16), x.shape[1]) * (1.0 / x.shape[1]) + RMS_EPS)


def _twice(x):
    return jnp.concatenate([x, x], axis=1)


def _band_bias():
    qi = (jnp.arange(GQA_GROUP * ATTN_BLOCK) % ATTN_BLOCK)[:, None]
    kj = jnp.arange(2 * ATTN_BLOCK)[None, :]
    return jnp.where(jnp.logical_and(kj > qi, kj <= qi + ATTN_BLOCK), 0.0, NEG).astype(F32)


def _attn_exp(qn, kk, sink, band, n):
    kj = lax.broadcasted_iota(jnp.int32, (1, 2 * ATTN_BLOCK), 1)
    before_sequence = jnp.where(jnp.logical_and(kj < ATTN_BLOCK, n == 0), NEG, 0.0)
    s = _dot_nt(qn, kk) + band + before_sequence
    m = jnp.maximum(jnp.broadcast_to(jnp.max(s, axis=-1, keepdims=True), sink.shape), sink)
    e = jnp.exp(s - _twice(m))
    e_s = jnp.exp(sink - m)
    e_b = e.astype(BF16)
    inv = 1.0 / (_rowsum(e_b, ATTN_BLOCK) + e_s)
    return e_b, e, e_s, inv


def _attn_blocks(n):
    cur = pl.multiple_of(n * ATTN_BLOCK, ATTN_BLOCK)
    prev = pl.multiple_of(jnp.maximum(n - 1, 0) * ATTN_BLOCK, ATTN_BLOCK)
    return cur, prev


def _kv_window(kv_ref, n):
    cur, prev = _attn_blocks(n)
    return jnp.concatenate([kv_ref[pl.ds(prev, ATTN_BLOCK), :], kv_ref[pl.ds(cur, ATTN_BLOCK), :]], axis=0).astype(F32)


def _kv_split(kv, g):
    k = kv[:, g * HEAD_DIM:(g + 1) * HEAD_DIM]
    v = kv[:, (N_KV_HEADS + g) * HEAD_DIM:(N_KV_HEADS + g + 1) * HEAD_DIM]
    return k, v


def _attn_fwd(q, kv, qw, kw, sink_rows, band):
    S, W = q.shape
    per = 2
    rows = per * ATTN_BLOCK

    def body(q_ref, kv_ref, qw_ref, kw_ref, sk_ref, band_ref, o_ref, o_scr):
        qf = q_ref[...].astype(F32)
        pairs = [(b, g) for b in range(per) for g in range(N_KV_HEADS)]
        ns = [per * pl.program_id(0) + b for b in range(per)]
        kvw = [_kv_window(kv_ref, n) for n in ns]
        qs = [_stack_heads(qf[b * ATTN_BLOCK:(b + 1) * ATTN_BLOCK], g) for b, g in pairs]
        kvs = [_kv_split(kvw[b], g) for b, g in pairs]
        rq = [_rinv_lanes(x) for x in qs]
        rk = [_rinv_lanes(k) for k, _ in kvs]
        qn = [(x * r * qw_ref[...] * SCALE).astype(BF16) for x, r in zip(qs, rq)]
        kk = [(k * r * kw_ref[...]).astype(BF16) for (k, _), r in zip(kvs, rk)]
        ex = [_attn_exp(qn[i], kk[i], sk_ref[g], band_ref[...], ns[b]) for i, (b, g) in enumerate(pairs)]
        pv = [_dot(ex[i][0], kvs[i][1].astype(BF16)) for i in range(len(pairs))]
        for i, (b, g) in enumerate(pairs):
            _unstack_heads(o_scr.at[b * ATTN_BLOCK:(b + 1) * ATTN_BLOCK, :], pv[i] * ex[i][3][:, :HEAD_DIM], g)
        o_ref[...] = o_scr[...].astype(BF16)

    blk = pl.BlockSpec((rows, W), lambda n: (n, 0))
    return pl.pallas_call(
        body, name="attn_fwd", grid=(S // rows,),
        in_specs=[blk, pl.BlockSpec(kv.shape, lambda n: (0, 0)), pl.BlockSpec((1, HEAD_DIM), lambda n: (0, 0)),
                  pl.BlockSpec((1, HEAD_DIM), lambda n: (0, 0)), pl.BlockSpec(sink_rows.shape, lambda n: (0, 0, 0)),
                  pl.BlockSpec(band.shape, lambda n: (0, 0))],
        out_specs=blk, out_shape=jax.ShapeDtypeStruct((S, W), BF16),
        scratch_shapes=[pltpu.VMEM((rows, W), F32)],
        compiler_params=_params(dimension_semantics=("arbitrary",)),
    )(q, kv, qw, kw, sink_rows, band)


def _attn_bwd(q, kv, do, qw, kw, sink_rows, band):
    S, W = q.shape
    KW = kv.shape[1]
    nb = S // ATTN_BLOCK
    per = 2
    chunk = 512

    def body(q_ref, kv_ref, do_ref, qw_ref, kw_ref, sk_ref, band_ref, dq_ref, dkv_ref, dqw_ref, dkw_ref, dsk_ref, dq_scr):
        step = pl.program_id(0)

        @pl.when(step == 0)
        def _():
            dkv_ref[...] = jnp.zeros_like(dkv_ref)
            dqw_ref[...] = jnp.zeros_like(dqw_ref)
            dsk_ref[...] = jnp.zeros_like(dsk_ref)

        qw_v = qw_ref[...]
        for b in range(per):
            n = per * step + b
            mine = slice(b * ATTN_BLOCK, (b + 1) * ATTN_BLOCK)
            qf = q_ref[mine, :].astype(F32)
            dof = do_ref[mine, :].astype(F32)
            cur, prev = _attn_blocks(n)
            kvw = _kv_window(kv_ref, n)
            for g in range(N_KV_HEADS):
                qs = _stack_heads(qf, g)
                rq = _rinv_lanes(qs)
                qhat = qs * rq
                qn = (qhat * qw_v * SCALE).astype(BF16)
                k, v = _kv_split(kvw, g)
                kk = (k * _rinv_lanes(k) * kw_ref[...]).astype(BF16)
                vv = v.astype(BF16)
                dos = _stack_heads(dof, g).astype(BF16)
                _, e, e_s, inv = _attn_exp(qn, kk, sk_ref[g], band_ref[...], n)
                p = e * _twice(inv)
                dp = _dot_nt(dos, vv)
                drow = _rowsum((p * dp).astype(BF16), ATTN_BLOCK)
                ds = (p * (dp - _twice(drow))).astype(BF16)
                dsink = -(e_s * inv * drow)
                for hh in range(GQA_GROUP):
                    tot = jnp.sum(dsink[hh * ATTN_BLOCK:(hh + 1) * ATTN_BLOCK, :], axis=0, keepdims=True)
                    dsk_ref[g, hh:hh + 1, :] += tot
                dqn = _dot(ds, kk) * SCALE
                dkk = _dot_tn(ds, qn)
                dvv = _dot_tn(p.astype(BF16), dos)
                klo, vlo = g * HEAD_DIM, (N_KV_HEADS + g) * HEAD_DIM
                for start, rows in ((prev, slice(0, ATTN_BLOCK)), (cur, slice(ATTN_BLOCK, 2 * ATTN_BLOCK))):
                    dkv_ref[pl.ds(start, ATTN_BLOCK), klo:klo + HEAD_DIM] += dkk[rows]
                    dkv_ref[pl.ds(start, ATTN_BLOCK), vlo:vlo + HEAD_DIM] += dvv[rows]
                dqw_ref[...] += jnp.sum(dqn * qhat, axis=0, keepdims=True)
                z = dqn * qw_v
                dqs = rq * (z - qhat * (_rowsum((z * qhat).astype(BF16), HEAD_DIM) * (1.0 / HEAD_DIM)))
                _unstack_heads(dq_scr.at[mine, :], dqs, g)
        dq_ref[...] = dq_scr[...].astype(BF16)

        @pl.when(step == nb // per - 1)
        def _():
            def one(c, dkw):
                rows = pl.ds(pl.multiple_of(c * chunk, chunk), chunk)
                for g in range(N_KV_HEADS):
                    lo = g * HEAD_DIM
                    k = kv_ref[rows, lo:lo + HEAD_DIM].astype(F32)
                    dx, dg_rows = _rms_bwd(dkv_ref[rows, lo:lo + HEAD_DIM], k, kw_ref[...])
                    dkv_ref[rows, lo:lo + HEAD_DIM] = dx
                    dkw = dkw + jnp.sum(dg_rows, axis=0, keepdims=True)
                return dkw

            dkw_ref[...] = lax.fori_loop(0, S // chunk, one, jnp.zeros((1, HEAD_DIM), F32))

    blk = pl.BlockSpec((per * ATTN_BLOCK, W), lambda n: (n, 0))
    whole_kv = pl.BlockSpec((S, KW), lambda n: (0, 0))
    vec = pl.BlockSpec((1, HEAD_DIM), lambda n: (0, 0))
    sk = pl.BlockSpec(sink_rows.shape, lambda n: (0, 0, 0))
    dsk = pl.BlockSpec((N_KV_HEADS, GQA_GROUP, 128), lambda n: (0, 0, 0))
    return pl.pallas_call(
        body, name="attn_bwd", grid=(nb // per,),
        in_specs=[blk, whole_kv, blk, vec, vec, sk, pl.BlockSpec(band.shape, lambda n: (0, 0))],
        out_specs=[blk, whole_kv, vec, vec, dsk],
        out_shape=[jax.ShapeDtypeStruct((S, W), BF16), jax.ShapeDtypeStruct((S, KW), F32),
                   jax.ShapeDtypeStruct((1, HEAD_DIM), F32), jax.ShapeDtypeStruct((1, HEAD_DIM), F32),
                   jax.ShapeDtypeStruct((N_KV_HEADS, GQA_GROUP, 128), F32)],
        scratch_shapes=[pltpu.VMEM((per * ATTN_BLOCK, W), F32)],
        compiler_params=_params(dimension_semantics=("arbitrary",)),
    )(q, kv, do, qw, kw, sink_rows, band)


def _pooled(xc, xprev, i):
    tm = xc.shape[0]
    xh = jnp.concatenate([jnp.where(i > 0, xprev, 0.0), xc], axis=0)
    t = lax.broadcasted_iota(jnp.int32, (tm, 1), 0) + i * tm
    out = []
    for gi, w in enumerate(POOL_WINDOWS):
        acc = xh[:, gi * POOL_GROUP:(gi + 1) * POOL_GROUP]
        sh = 1
        while sh < w:
            acc = acc + pltpu.roll(acc, sh, 0)
            sh *= 2
        cnt = jnp.minimum(t + 1, w).astype(F32)
        out.append(acc[POOL_HALO:, :] / cnt - xc[:, gi * POOL_GROUP:(gi + 1) * POOL_GROUP])
    return jnp.concatenate(out, axis=1)


def _pool_mix(pooled_b, pw_ref):
    return jnp.concatenate([_dot(pooled_b[:, gi * POOL_GROUP:(gi + 1) * POOL_GROUP], pw_ref[gi])
                            for gi in range(len(POOL_WINDOWS))], axis=1)


def _halo_specs(tm, width, S, after):
    per = tm // POOL_HALO
    last = S // POOL_HALO - 1
    if after:
        return pl.BlockSpec((POOL_HALO, width), lambda i: (jnp.minimum((i + 1) * per, last), 0))
    return pl.BlockSpec((POOL_HALO, width), lambda i: (jnp.maximum(i * per - 1, 0), 0))


def _mix_out(xp, attn, gl, bias, pw, pscale, wpot, wao, wo, h, tm=512):
    S, D = h.shape
    nt = S // tm
    PW = xp.shape[1]

    def body(xc_ref, xprev_ref, at_ref, gl_ref, bias_ref, pw_ref, ps_ref, wpo_ref, wao_ref, wo_ref, h_ref,
             ho_ref, bp_ref, ba_ref):
        i = pl.program_id(0)
        pooled = _pooled(xc_ref[...], xprev_ref[...], i).astype(BF16)
        ms = (_pool_mix(pooled, pw_ref) * ps_ref[...]).astype(BF16)
        bp = _dot_nt(ms, wpo_ref[...])
        ba = _dot(at_ref[...], wao_ref[...])
        bp_ref[...] = bp.astype(BF16)
        ba_ref[...] = ba.astype(BF16)
        gates = jax.nn.sigmoid(gl_ref[...].astype(F32) + bias_ref[...])
        merged = (gates[:, :D] * bp + gates[:, D:] * ba).astype(BF16)
        ho_ref[...] = h_ref[...] + _dot(merged, wo_ref[...])

    def row(width):
        return pl.BlockSpec((tm, width), lambda i: (i, 0))

    def whole(x):
        nd = x.ndim
        return pl.BlockSpec(x.shape, lambda i: (0,) * nd)

    return pl.pallas_call(
        body, name="mix_out", grid=(nt,),
        in_specs=[row(PW), _halo_specs(tm, PW, S, False), row(D), row(2 * D), whole(bias), whole(pw), whole(pscale),
                  whole(wpot), whole(wao), whole(wo), row(D)],
        out_specs=[row(D), row(D), row(D)],
        out_shape=[jax.ShapeDtypeStruct((S, D), F32), jax.ShapeDtypeStruct((S, D), BF16),
                   jax.ShapeDtypeStruct((S, D), BF16)],
        compiler_params=_params(dimension_semantics=("arbitrary",)),
    )(xp, xp, attn, gl, bias, pw, pscale, wpot, wao, wo, h)


def _mix_bwd_gate(dh, wo, bp, ba, gl, bias, tm=512):
    S, D = dh.shape
    nt = S // tm

    def body(dh_ref, wo_ref, bp_ref, ba_ref, gl_ref, bias_ref, dgl_ref, dbp_ref, dba_ref, dwo_ref, dbias_ref):
        i = pl.program_id(0)
        dhb = dh_ref[...].astype(BF16)
        dm = _dot_nt(dhb, wo_ref[...])
        gates = jax.nn.sigmoid(gl_ref[...].astype(F32) + bias_ref[...])
        gp, ga = gates[:, :D], gates[:, D:]
        bp_v = bp_ref[...].astype(F32)
        ba_v = ba_ref[...].astype(F32)
        merged = (gp * bp_v + ga * ba_v).astype(BF16)
        _acc(dwo_ref, _dot_tn(merged, dhb), i == 0)
        dbp_ref[...] = (dm * gp).astype(BF16)
        dba_ref[...] = (dm * ga).astype(BF16)
        dgl = jnp.concatenate([dm * bp_v * gp * (1.0 - gp), dm * ba_v * ga * (1.0 - ga)], axis=1)
        dgl_ref[...] = dgl.astype(BF16)
        _acc(dbias_ref, jnp.sum(dgl, axis=0, keepdims=True), i == 0)

    def row(width):
        return pl.BlockSpec((tm, width), lambda i: (i, 0))

    def whole(shape):
        return pl.BlockSpec(shape, lambda i: (0, 0))

    return pl.pallas_call(
        body, name="mix_bwd_gate", grid=(nt,),
        in_specs=[row(D), whole(wo.shape), row(D), row(D), row(2 * D), whole(bias.shape)],
        out_specs=[row(2 * D), row(D), row(D), whole((D, D)), whole((1, 2 * D))],
        out_shape=[jax.ShapeDtypeStruct((S, 2 * D), BF16), jax.ShapeDtypeStruct((S, D), BF16),
                   jax.ShapeDtypeStruct((S, D), BF16), jax.ShapeDtypeStruct((D, D), F32),
                   jax.ShapeDtypeStruct((1, 2 * D), F32)],
        compiler_params=_params(dimension_semantics=("arbitrary",)),
    )(dh, wo, bp, ba, gl, bias)


def _mix_bwd_branch(dbp, dba, attn, xp, pw, pscale, wpot, wao, tm=1024):
    S, D = dbp.shape
    nt = S // tm
    PW = xp.shape[1]
    NG = len(POOL_WINDOWS)

    def body(dbp_ref, dba_ref, at_ref, xc_ref, xprev_ref, pw_ref, ps_ref, wpo_ref, wao_ref,
             dat_ref, dpl_ref, dwao_ref, dwpo_ref, dpw_ref, dps_ref):
        i = pl.program_id(0)
        dba_v = dba_ref[...]
        dbp_v = dbp_ref[...]
        _acc(dwao_ref, _dot_tn(at_ref[...], dba_v), i == 0)
        dat_ref[...] = _dot_nt(dba_v, wao_ref[...]).astype(BF16)
        pooled = _pooled(xc_ref[...], xprev_ref[...], i).astype(BF16)
        mixed = _pool_mix(pooled, pw_ref)
        ps = ps_ref[...]
        _acc(dwpo_ref, _dot_tn(dbp_v, (mixed * ps).astype(BF16)), i == 0)
        dms = _dot(dbp_v, wpo_ref[...])
        _acc(dps_ref, jnp.sum(dms * mixed, axis=0, keepdims=True), i == 0)
        dmixed = (dms * ps).astype(BF16)
        dpooled = []
        for gi in range(NG):
            cols = slice(gi * POOL_GROUP, (gi + 1) * POOL_GROUP)
            _acc(dpw_ref.at[gi], _dot_tn(pooled[:, cols], dmixed[:, cols]), i == 0)
            dpooled.append(_dot_nt(dmixed[:, cols], pw_ref[gi]))
        dpl_ref[...] = jnp.concatenate(dpooled, axis=1)

    def row(width):
        return pl.BlockSpec((tm, width), lambda i: (i, 0))

    def whole(shape):
        nd = len(shape)
        return pl.BlockSpec(shape, lambda i: (0,) * nd)

    return pl.pallas_call(
        body, name="mix_bwd_branch", grid=(nt,),
        in_specs=[row(D), row(D), row(D), row(PW), _halo_specs(tm, PW, S, False), whole(pw.shape), whole(pscale.shape),
                  whole(wpot.shape), whole(wao.shape)],
        out_specs=[row(D), row(PW), whole((D, D)), whole((D, PW)), whole(pw.shape), whole((1, PW))],
        out_shape=[jax.ShapeDtypeStruct((S, D), BF16), jax.ShapeDtypeStruct((S, PW), F32),
                   jax.ShapeDtypeStruct((D, D), F32), jax.ShapeDtypeStruct((D, PW), F32),
                   jax.ShapeDtypeStruct(pw.shape, F32), jax.ShapeDtypeStruct((1, PW), F32)],
        compiler_params=_params(dimension_semantics=("arbitrary",)),
    )(dbp, dba, attn, xp, xp, pw, pscale, wpot, wao)


def _pool_bwd(dpooled, tm=512):
    S, PW = dpooled.shape
    nt = S // tm

    def body(dc_ref, dnext_ref, dxp_ref):
        i = pl.program_id(0)
        dc = dc_ref[...]
        dh = jnp.concatenate([dc, jnp.where(i < nt - 1, dnext_ref[...], 0.0)], axis=0)
        rows = tm + POOL_HALO
        t = lax.broadcasted_iota(jnp.int32, (rows, 1), 0) + i * tm
        out = []
        for gi, w in enumerate(POOL_WINDOWS):
            cols = slice(gi * POOL_GROUP, (gi + 1) * POOL_GROUP)
            acc = dh[:, cols] / jnp.minimum(t + 1, w).astype(F32)
            sh = 1
            while sh < w:
                acc = acc + pltpu.roll(acc, rows - sh, 0)
                sh *= 2
            out.append(acc[:tm, :] - dc[:, cols])
        dxp_ref[...] = jnp.concatenate(out, axis=1).astype(BF16)

    return pl.pallas_call(
        body, name="pool_bwd", grid=(nt,),
        in_specs=[pl.BlockSpec((tm, PW), lambda i: (i, 0)), _halo_specs(tm, PW, S, True)],
        out_specs=pl.BlockSpec((tm, PW), lambda i: (i, 0)), out_shape=jax.ShapeDtypeStruct((S, PW), BF16),
        compiler_params=_params(dimension_semantics=("arbitrary",)),
    )(dpooled, dpooled)


def _in_bwd_w(pieces, u, tm=1024):
    S, D = u.shape
    nt = S // tm
    NW = sum(width for _, width in SEGMENTS)

    def body(*refs):
        piece_refs, u_ref, dw_hbm, acc, sem = refs[:len(SEGMENTS)], refs[len(SEGMENTS)], refs[len(SEGMENTS) + 1], refs[-2], refs[-1]
        i = pl.program_id(0)
        u_t = u_ref[...]
        for (off, width), ref in zip(SEGMENTS, piece_refs):
            for lo in range(0, width, D):
                hi = min(lo + D, width)
                _acc(acc.at[off + lo:off + hi, :], _dot_tn(ref[:, lo:hi].astype(BF16), u_t), i == 0)

        @pl.when(i == nt - 1)
        def _():
            cp = pltpu.make_async_copy(acc, dw_hbm, sem)
            cp.start()
            cp.wait()

    return pl.pallas_call(
        body, name="in_bwd_w", grid=(nt,),
        in_specs=[pl.BlockSpec((tm, width), lambda i: (i, 0)) for _, width in SEGMENTS] + [pl.BlockSpec((tm, D), lambda i: (i, 0))],
        out_specs=pl.BlockSpec(memory_space=pl.ANY), out_shape=jax.ShapeDtypeStruct((NW, D), F32),
        scratch_shapes=[pltpu.VMEM((NW, D), F32), pltpu.SemaphoreType.DMA],
        compiler_params=_params(dimension_semantics=("arbitrary",)),
    )(*pieces, u)


def _in_bwd_x(pieces, wint, h, dh, gnorm, tm=512):
    S, D = h.shape
    nt = S // tm

    def body(*refs):
        piece_refs = refs[:len(SEGMENTS)]
        w_ref, h_ref, dh_ref, g_ref, dx_ref, dxh_ref, dg_ref = refs[len(SEGMENTS):]
        i = pl.program_id(0)
        du = jnp.zeros((tm, D), F32)
        for (off, width), ref in zip(SEGMENTS, piece_refs):
            du = du + _dot(ref[...].astype(BF16), w_ref[off:off + width, :])
        dx, dg_rows = _rms_bwd(du, h_ref[...], g_ref[...])
        out = dh_ref[...] + dx
        dx_ref[...] = out
        dxh_ref[...] = (0.5 * out).astype(BF16)
        _acc(dg_ref, jnp.sum(dg_rows, axis=0, keepdims=True), i == 0)

    def row(width):
        return pl.BlockSpec((tm, width), lambda i: (i, 0))

    vec = pl.BlockSpec((1, D), lambda i: (0, 0))
    return pl.pallas_call(
        body, name="in_bwd_x", grid=(nt,),
        in_specs=[row(width) for _, width in SEGMENTS] + [pl.BlockSpec(wint.shape, lambda i: (0, 0)), row(D), row(D), vec],
        out_specs=[row(D), row(D), vec],
        out_shape=[jax.ShapeDtypeStruct((S, D), F32), jax.ShapeDtypeStruct((S, D), BF16), jax.ShapeDtypeStruct((1, D), F32)],
        compiler_params=_params(dimension_semantics=("arbitrary",)),
    )(*pieces, wint, h, dh, gnorm)


def _row_tile(rows):
    for t in (512, 480, 352, 256, 128, 64, 32, 16, 8):
        if rows % t == 0:
            return t
    return rows


def _adamw_update(w, g, m, v):
    mn = ADAM_B1 * m + (1.0 - ADAM_B1) * g
    vn = ADAM_B2 * v + (1.0 - ADAM_B2) * (g * g)
    m_hat = mn / (1.0 - ADAM_B1 ** ADAM_STEP)
    v_hat = vn / (1.0 - ADAM_B2 ** ADAM_STEP)
    return -ADAM_LR * (m_hat / (jnp.sqrt(v_hat) + ADAM_EPS) + ADAM_WD * w), mn, vn


def _adamw_small(ws, gs, ms, vs):
    n = len(ws)

    def rows(a):
        return a.reshape(-1, a.shape[-1]) if a.ndim > 1 else a.reshape(1, -1)

    def body(*refs):
        for k in range(n):
            w_ref, g_ref, m_ref, v_ref, d_ref, mo_ref, vo_ref = refs[k::n]
            d_ref[...], mo_ref[...], vo_ref[...] = _adamw_update(w_ref[...], g_ref[...], m_ref[...], v_ref[...])

    vmem = pl.BlockSpec(memory_space=pltpu.VMEM)
    out = pl.pallas_call(
        body, name="adamw_small", in_specs=[vmem] * (4 * n), out_specs=[vmem] * (3 * n),
        out_shape=[jax.ShapeDtypeStruct(rows(w).shape, F32) for _ in range(3) for w in ws], compiler_params=_params(),
    )(*[rows(a) for a in ws + gs + ms + vs])
    return [[o.reshape(w.shape) for o, w in zip(out[k * n:(k + 1) * n], ws)] for k in range(3)]


def _adamw(w, g, m, v, token, name, turned=False):
    R, C = w.shape
    tr = R if turned else _row_tile(R)

    def body(w_ref, g_ref, m_ref, v_ref, _, d_ref, mo_ref, vo_ref, go_ref):
        gv = g_ref[...].T if turned else g_ref[...]
        go_ref[...] = gv
        d_ref[...], mo_ref[...], vo_ref[...] = _adamw_update(w_ref[...], gv, m_ref[...], v_ref[...])

    blk = pl.BlockSpec((tr, C), lambda i: (i, 0))
    g_blk = pl.BlockSpec((C, R), lambda i: (0, 0)) if turned else blk
    sh = jax.ShapeDtypeStruct((R, C), F32)
    return pl.pallas_call(
        body, name=name, grid=(R // tr,), in_specs=[blk, g_blk, blk, blk, TOKEN_SPEC], out_specs=[blk] * 4, out_shape=[sh] * 4,
        compiler_params=_params(dimension_semantics=("arbitrary",)),
    )(w, g, m, v, token)


def _cast_place(w, place, name, turned=False, after=()):
    R, C = w.shape[::-1] if turned else w.shape
    tr = R if turned else _row_tile(R)
    per = R // tr

    def body(p_ref, w_ref, *rest):
        rest[-1][...] = (w_ref[...].T if turned else w_ref[...]).astype(BF16)

    grid_spec = pltpu.PrefetchScalarGridSpec(
        num_scalar_prefetch=1, grid=(per,),
        in_specs=[pl.BlockSpec((C, R), lambda i, p: (0, 0)) if turned else pl.BlockSpec((tr, C), lambda i, p: (i, 0))]
        + [TOKEN_SPEC] * len(after),
        out_specs=pl.BlockSpec((tr, C), lambda i, p: (p[0] * per + i, 0)))
    return pl.pallas_call(
        body, name=name, grid_spec=grid_spec, out_shape=jax.ShapeDtypeStruct((N_SHARDS * R, C), BF16),
        compiler_params=_params(dimension_semantics=("arbitrary",)),
    )(place, w, *after)


def _parked(w, m, idx, first, last):
    return jnp.where(m == w, idx, jnp.where(m < w, first, last))


def _sum_halves(g4s, recvs, place, name):
    M = len(g4s)
    NS, R, C = g4s[0].shape
    hr = R // 2
    pieces = [(w, s) for w in range(M) for s in range(NS)]
    n_in, n_out = 3, 2

    def body(p_ref, *refs):
        g_refs, r_refs, o_refs = refs[:M], refs[M:2 * M], refs[2 * M:3 * M]
        gbuf, rbuf, obuf, in_sem, out_sem = refs[3 * M:]
        mine = pl.ds(pl.multiple_of(p_ref[1] * hr, 8), hr)

        def fetch(k):
            w, s = pieces[k]
            return (pltpu.make_async_copy(g_refs[w].at[s, mine, :], gbuf.at[k % n_in], in_sem.at[0, k % n_in]),
                    pltpu.make_async_copy(r_refs[w].at[s], rbuf.at[k % n_in], in_sem.at[1, k % n_in]))

        def put(k):
            w, s = pieces[k]
            return pltpu.make_async_copy(obuf.at[k % n_out], o_refs[w].at[s], out_sem.at[k % n_out])

        for k in range(min(n_in, len(pieces))):
            for copy in fetch(k):
                copy.start()
        for k in range(len(pieces)):
            for copy in fetch(k):
                copy.wait()
            if k >= n_out:
                put(k - n_out).wait()
            obuf[k % n_out] = (gbuf[k % n_in] + rbuf[k % n_in]).astype(BF16)
            put(k).start()
            if k + n_in < len(pieces):
                for copy in fetch(k + n_in):
                    copy.start()
        for k in range(max(0, len(pieces) - n_out), len(pieces)):
            put(k).wait()

    return pl.pallas_call(
        body, name=name, in_specs=[pl.BlockSpec(memory_space=pltpu.SMEM)] + [TOKEN_SPEC] * (2 * M),
        out_specs=[TOKEN_SPEC] * M, out_shape=[jax.ShapeDtypeStruct((NS, hr, C), BF16)] * M,
        scratch_shapes=[pltpu.VMEM((n_in, hr, C), F32), pltpu.VMEM((n_in, hr, C), F32), pltpu.VMEM((n_out, hr, C), BF16),
                        pltpu.SemaphoreType.DMA((2, n_in)), pltpu.SemaphoreType.DMA((n_out,))],
        compiler_params=_params(),
    )(place, *g4s, *recvs)


def _sum_quarters(h4s, recv3s, place, name):
    M = len(h4s)
    NS, hr, C = h4s[0].shape
    tr = hr // 2
    pieces = [(w, r) for w in range(M) for r in range(0, hr, tr)]
    n_in, n_out = 3, 2

    def body(p_ref, *refs):
        h_refs, r_refs, o_refs = refs[:M], refs[M:2 * M], refs[2 * M:3 * M]
        hbuf, rbuf, obuf, in_sem, out_sem = refs[3 * M:]
        base = pl.multiple_of(p_ref[1] * hr, 8)

        def fetch(k):
            w, r = pieces[k]
            return (pltpu.make_async_copy(h_refs[w].at[p_ref[0], r:r + tr, :], hbuf.at[k % n_in], in_sem.at[0, k % n_in]),
                    pltpu.make_async_copy(r_refs[w].at[:, r:r + tr, :], rbuf.at[k % n_in], in_sem.at[1, k % n_in]))

        def put(k):
            w, r = pieces[k]
            return pltpu.make_async_copy(obuf.at[k % n_out], o_refs[w].at[pl.ds(base + r, tr), :], out_sem.at[k % n_out])

        for k in range(min(n_in, len(pieces))):
            for copy in fetch(k):
                copy.start()
        for k in range(len(pieces)):
            for copy in fetch(k):
                copy.wait()
            if k >= n_out:
                put(k - n_out).wait()
            acc = hbuf[k % n_in].astype(F32)
            for j in range(N_SHARDS - 1):
                acc = acc + rbuf[k % n_in, j].astype(F32)
            obuf[k % n_out] = acc
            put(k).start()
            if k + n_in < len(pieces):
                for copy in fetch(k + n_in):
                    copy.start()
        for k in range(max(0, len(pieces) - n_out), len(pieces)):
            put(k).wait()

    return pl.pallas_call(
        body, name=name, in_specs=[pl.BlockSpec(memory_space=pltpu.SMEM)] + [TOKEN_SPEC] * (2 * M),
        out_specs=[TOKEN_SPEC] * M, out_shape=[jax.ShapeDtypeStruct((2 * hr, C), F32)] * M,
        scratch_shapes=[pltpu.VMEM((n_in, tr, C), BF16), pltpu.VMEM((n_in, N_SHARDS - 1, tr, C), BF16),
                        pltpu.VMEM((n_out, tr, C), F32), pltpu.SemaphoreType.DMA((2, n_in)), pltpu.SemaphoreType.DMA((n_out,))],
        compiler_params=_params(),
    )(place, *h4s, *recv3s)


def _by_shape(arrays):
    groups = {}
    for k, a in enumerate(arrays):
        groups.setdefault(a.shape, []).append(k)
    return list(groups.values())


def _place():
    x, y, c = lax.axis_index("x"), lax.axis_index("y"), lax.axis_index("c")
    chips = [(1 - x, y), (x, 1 - y), (1 - x, 1 - y)]
    return x, y, c, chips


HBM_SPEC = pl.BlockSpec(memory_space=pltpu.HBM)
SEM_SPEC = pl.BlockSpec(memory_space=pltpu.SEMAPHORE)
DATAFLOW = pltpu.SideEffectType.DATAFLOW_SIDE_EFFECTING


def _hbm(a):
    return pltpu.with_memory_space_constraint(a, pltpu.HBM)


class InFlight(NamedTuple):
    send_sem: jax.Array
    recv_sem: jax.Array
    bufs: list
    plan: Callable
    token: jax.Array


class Leg(NamedTuple):
    bufs: list
    landing: Optional[InFlight]
    plan: Optional[Callable]
    n_copies: int


def _send(bufs, plan, n_copies):
    return Leg([_hbm(b) for b in bufs], None, plan, n_copies)


def _land(flight):
    return Leg(flight.bufs, flight, None, 0)


def _forward(flight, plan, n_copies):
    return Leg(flight.bufs, flight, plan, n_copies)


def _wait_all(plan, refs, send_ref, recv_ref):
    for k, (src, dst, dev) in enumerate(plan(refs)):
        cp = pltpu.make_async_remote_copy(src_ref=src, dst_ref=dst, send_sem=send_ref.at[k], recv_sem=recv_ref.at[k],
                                          device_id=dev, device_id_type=MESH)
        cp.wait_send()
        cp.wait_recv()


def _start_all(plan, refs, send_ref, recv_ref):
    for k, (src, dst, dev) in enumerate(plan(refs)):
        pltpu.make_async_remote_copy(src_ref=src, dst_ref=dst, send_sem=send_ref.at[k], recv_sem=recv_ref.at[k],
                                     device_id=dev, device_id_type=MESH).start()


def _comm(name, legs, after):
    after = list(after) if isinstance(after, (list, tuple)) else [after]
    ins, in_specs, out_shape, out_specs, aliases, first = [], [], [], [], {}, []
    for leg in legs:
        first.append((len(ins), len(out_shape)))
        for b in leg.bufs:
            aliases[len(ins)] = len(out_shape)
            ins.append(b)
            in_specs.append(HBM_SPEC)
            out_shape.append(pltpu.HBM(b.shape, b.dtype))
            out_specs.append(HBM_SPEC)
        if leg.landing is not None:
            ins += [leg.landing.send_sem, leg.landing.recv_sem]
            in_specs += [SEM_SPEC, SEM_SPEC]
        if leg.plan is not None:
            out_shape += [pltpu.SemaphoreType.DMA((leg.n_copies,))] * 2
            out_specs += [SEM_SPEC, SEM_SPEC]
    starts = any(leg.plan is not None for leg in legs)
    if starts:
        out_shape.append(jax.ShapeDtypeStruct((8, 128), F32))
        out_specs.append(pl.BlockSpec(memory_space=pltpu.VMEM))
    n_in = len(ins) + len(after)

    def body(*refs):
        outs = refs[n_in:]
        for leg, (i, o) in zip(legs, first):
            nb = len(leg.bufs)
            bufs = refs[i:i + nb]
            if leg.landing is not None:
                _wait_all(leg.landing.plan, bufs, refs[i + nb], refs[i + nb + 1])
            if leg.plan is not None:
                _start_all(leg.plan, bufs, outs[o + nb], outs[o + nb + 1])
        if starts:
            outs[-1][...] = jnp.zeros_like(outs[-1])

    out = pl.pallas_call(
        body, name=name, in_specs=in_specs + [TOKEN_SPEC] * len(after), out_shape=out_shape, out_specs=out_specs,
        input_output_aliases=aliases, compiler_params=pltpu.CompilerParams(has_side_effects=DATAFLOW),
    )(*ins, *after)
    results = []
    for leg, (_, o) in zip(legs, first):
        nb = len(leg.bufs)
        bufs = list(out[o:o + nb])
        results.append(bufs if leg.plan is None else InFlight(out[o + nb], out[o + nb + 1], bufs, leg.plan, out[-1]))
    return results


def _half_rows(buf, chip, core):
    hr = buf.shape[0] // (2 * N_SHARDS)
    return buf.at[pl.ds(pl.multiple_of((2 * chip + core) * hr, 16), hr)]


def _gather_ici_plan(bufs):
    x, y, c, chips = _place()
    return [(_half_rows(b, 2 * x + y, c), _half_rows(b, 2 * x + y, c), (px, py, c)) for b in bufs for px, py in chips]


def _gather_d2d_plan(bufs):
    x, y, c, chips = _place()
    return [(_half_rows(b, 2 * px + py, c), _half_rows(b, 2 * px + py, c), (x, y, 1 - c)) for b in bufs for px, py in chips]


def _swap_plan(bufs):
    x, y, c, _ = _place()
    n = len(bufs) // 2
    copies = []
    for g, land in zip(bufs[:n], bufs[n:]):
        hr = g.shape[1] // 2
        copies.append((g.at[:, pl.ds(pl.multiple_of((1 - c) * hr, 8), hr)], land, (x, y, 1 - c)))
    return copies


def _exchange_plan(bufs):
    x, y, c, chips = _place()
    n = len(bufs) // 2
    return [(h.at[2 * px + py], land.at[k], (px, py, c))
            for h, land in zip(bufs[:n], bufs[n:]) for k, (px, py) in enumerate(chips)]


def _share_plan(bufs):
    x, y, c, _ = _place()
    copies = []
    for buf in bufs:
        hr = buf.shape[0] // 2
        mine = buf.at[pl.ds(pl.multiple_of(c * hr, 8), hr)]
        copies.append((mine, mine, (x, y, 1 - c)))
    return copies


N_DEVICES = 8


def _slot_place(vec, place):
    R, C = vec.shape

    def body(p_ref, v_ref, o_ref):
        o_ref[0] = v_ref[...]

    grid_spec = pltpu.PrefetchScalarGridSpec(
        num_scalar_prefetch=1, grid=(1,), in_specs=[pl.BlockSpec((R, C), lambda i, p: (0, 0))],
        out_specs=pl.BlockSpec((1, R, C), lambda i, p: (2 * p[0] + p[1], 0, 0)))
    return pl.pallas_call(
        body, name="slot_place", grid_spec=grid_spec, out_shape=jax.ShapeDtypeStruct((N_DEVICES, R, C), F32),
        compiler_params=_params(dimension_semantics=("arbitrary",)),
    )(place, vec)


def _slots_plan(bufs):
    x, y, c, _ = _place()
    mine = bufs[0].at[4 * x + 2 * y + c]
    return [(mine, mine, (x ^ (r >> 2), y ^ ((r >> 1) & 1), c ^ (r & 1))) for r in range(1, N_DEVICES)]


def _sum_slots(slots):
    _, R, C = slots.shape

    def body(s_ref, o_ref):
        acc = s_ref[0]
        for d in range(1, N_DEVICES):
            acc = acc + s_ref[d]
        o_ref[...] = acc

    return pl.pallas_call(
        body, name="sum_slots", in_specs=[pl.BlockSpec(memory_space=pltpu.VMEM)],
        out_specs=pl.BlockSpec(memory_space=pltpu.VMEM), out_shape=jax.ShapeDtypeStruct((R, C), F32),
        compiler_params=_params(),
    )(slots)


SMALL = ("ffn1_norm", "mix_norm", "pool_w", "pool_scale", "q_norm", "k_norm", "sinks", "gate_bias", "ffn2_norm")
SMALL_COLS = 1024
FFN1 = ("ffn1_w_gate", "ffn1_w_up", "ffn1_w_down")
MIXER = ("w_in", "w_pool_out", "w_attn_out", "w_out")
FFN2 = ("ffn2_w_gate", "ffn2_w_up", "ffn2_w_down")
LARGE = FFN1 + MIXER + FFN2
TRANSPOSED = ("ffn1_w_gate", "ffn1_w_up", "w_in", "w_pool_out", "ffn2_w_gate", "ffn2_w_up")
TURNED = ("w_pool_out",)
WEIGHTS = ("ffn1_norm", "ffn1_w_gate", "ffn1_w_up", "ffn1_w_down", "mix_norm", "w_in", "pool_w", "pool_scale",
           "w_pool_out", "q_norm", "k_norm", "sinks", "w_attn_out", "gate_bias", "w_out", "ffn2_norm",
           "ffn2_w_gate", "ffn2_w_up", "ffn2_w_down")


def _pack_small(parts):
    flat = jnp.concatenate([p.reshape(-1) for p in parts])
    rows = -(-flat.shape[0] // (8 * SMALL_COLS)) * 8
    return jnp.pad(flat, (0, rows * SMALL_COLS - flat.shape[0])).reshape(rows, SMALL_COLS)


def _unpack_small(packed, like):
    flat = packed.reshape(-1)
    out, off = [], 0
    for p in like:
        out.append(flat[off:off + p.size].reshape(p.shape))
        off += p.size
    return out, flat[off]


def _tie(small, token):
    return small + token[0, 0]


class Reduction:
    def __init__(self, names, place):
        self.names, self.place = names, place

    def swap(self, grads):
        grads = [g.reshape(N_SHARDS, -1, g.shape[-1]) for g in grads]
        lands = [lax.empty((g.shape[0], g.shape[1] // 2, g.shape[2]), g.dtype) for g in grads]
        return _send(grads + lands, _swap_plan, len(grads))

    def exchange(self, swapped):
        n = len(self.names)
        halves = self._per_shape(_sum_halves, "sum_halves", swapped[:n], swapped[n:])
        lands = [lax.empty((N_SHARDS - 1,) + h.shape[1:], h.dtype) for h in halves]
        return _send(halves + lands, _exchange_plan, (N_SHARDS - 1) * n)

    def share(self, exchanged):
        n = len(self.names)
        return _send(self._per_shape(_sum_quarters, "sum_quarters", exchanged[:n], exchanged[n:]), _share_plan, n)

    def named(self, shared):
        return dict(zip(self.names, shared))

    def _per_shape(self, add, stage, mine, received):
        out = [None] * len(mine)
        for idx in _by_shape(mine):
            sums = add([mine[k] for k in idx], [received[k] for k in idx], self.place, f"{stage}_{self.names[idx[0]]}")
            for k, v in zip(idx, sums):
                out[k] = v
        return out


def _row_form(name, a):
    return a.T if name in TRANSPOSED else a


def _shard_form(name, a):
    return a if name in TURNED else _row_form(name, a)


def kernel(x, ffn1_norm, ffn1_w_gate, ffn1_w_up, ffn1_w_down, mix_norm, w_in, pool_w, pool_scale, w_pool_out, q_norm, k_norm, sinks, w_attn_out, gate_bias, w_out, ffn2_norm, ffn2_w_gate, ffn2_w_up, ffn2_w_down, loss_target, m_ffn1_norm, m_ffn1_w_gate, m_ffn1_w_up, m_ffn1_w_down, m_mix_norm, m_w_in, m_pool_w, m_pool_scale, m_w_pool_out, m_q_norm, m_k_norm, m_sinks, m_w_attn_out, m_gate_bias, m_w_out, m_ffn2_norm, m_ffn2_w_gate, m_ffn2_w_up, m_ffn2_w_down, v_ffn1_norm, v_ffn1_w_gate, v_ffn1_w_up, v_ffn1_w_down, v_mix_norm, v_w_in, v_pool_w, v_pool_scale, v_w_pool_out, v_q_norm, v_k_norm, v_sinks, v_w_attn_out, v_gate_bias, v_w_out, v_ffn2_norm, v_ffn2_w_gate, v_ffn2_w_up, v_ffn2_w_down):
    args = dict(locals())
    wts = {n: _shard_form(n, args[n]) for n in WEIGHTS}
    mom = {n: _shard_form(n, args["m_" + n]) for n in WEIGHTS}
    var = {n: _shard_form(n, args["v_" + n]) for n in WEIGHTS}
    shard = 2 * lax.axis_index("x") + lax.axis_index("y")
    place = jnp.stack([shard, lax.axis_index("c")]).astype(jnp.int32)

    xs, target = x[0], loss_target[0]
    D = xs.shape[1]
    g1 = wts["ffn1_norm"].reshape(1, D)
    gm = wts["mix_norm"].reshape(1, D)
    g2 = wts["ffn2_norm"].reshape(1, D)
    qw = wts["q_norm"].reshape(1, HEAD_DIM)
    kw = wts["k_norm"].reshape(1, HEAD_DIM)
    bias = wts["gate_bias"].reshape(1, 2 * D)
    pscale = wts["pool_scale"].reshape(1, -1)
    pw = wts["pool_w"].astype(BF16)
    sink_rows = jnp.broadcast_to(jnp.repeat(wts["sinks"], ATTN_BLOCK).reshape(N_KV_HEADS, GQA_GROUP * ATTN_BLOCK, 1),
                                 (N_KV_HEADS, GQA_GROUP * ATTN_BLOCK, 128))

    gate1, up1, down1 = FFN1[:1], FFN1[1:2], FFN1[2:]
    mix_in, mix_out = MIXER[:1], MIXER[1:]

    def over_ici(names):
        return _send([placed[n] for n in names], _gather_ici_plan, 3 * len(names))

    def to_sibling(flight):
        return _forward(flight, _gather_d2d_plan, len(flight.bufs) * 3)

    placed = {n: _cast_place(wts[n], place, "cast_" + n) for n in gate1}
    gate1_ici, = _comm("gather_gate1_ici", [over_ici(gate1)], place)
    placed.update({n: _cast_place(wts[n], place, "cast_" + n) for n in up1 + down1})
    up1_ici, down1_ici = _comm("gather_ffn1_ici", [over_ici(up1), over_ici(down1)], gate1_ici.token)
    placed.update({n: _cast_place(wts[n], place, "cast_" + n, turned=n in TURNED) for n in MIXER})
    band = _band_bias()
    shadow = [sink_rows, pw, band, down1_ici.token]
    gate1_d2d, in_ici, out_ici = _comm(
        "gather_mix_ici", [to_sibling(gate1_ici), over_ici(mix_in), over_ici(mix_out)], shadow)
    w = dict(zip(gate1, _comm("gather_gate1_wait", [_land(gate1_d2d)], gate1_d2d.token)[0]))

    n1, a1 = _ffn_gate(xs, g1, w["ffn1_w_gate"], "ffn1_gate")
    placed.update({n: _cast_place(wts[n], place, "cast_" + n, after=[a1]) for n in FFN2})
    up1_d2d, ffn2_ici, down1_d2d = _comm(
        "gather_ffn1_d2d", [to_sibling(up1_ici), over_ici(FFN2), to_sibling(down1_ici)], a1)
    w.update(zip(up1, _comm("gather_up1_wait", [_land(up1_d2d)], up1_d2d.token)[0]))
    p1, q1, s1 = _ffn_hidden(n1, a1, w["ffn1_w_up"], "ffn1_hidden")
    w.update(zip(down1, _comm("gather_down1_wait", [_land(down1_d2d)], s1)[0]))
    h1 = _ffn_down(xs, s1, w["ffn1_w_down"], "ffn1_down")
    in_d2d, out_d2d = _comm("gather_mix_d2d", [to_sibling(in_ici), to_sibling(out_ici)], h1)
    w.update(zip(mix_in, _comm("gather_in_wait", [_land(in_d2d)], in_d2d.token)[0]))
    u, xp, q, kv, gl = _mix_proj(h1, gm, w["w_in"])
    attn = _attn_fwd(q, kv, qw, kw, sink_rows, band)
    ffn2_d2d, landed = _comm("gather_ffn2_d2d", [to_sibling(ffn2_ici), _land(out_d2d)], attn)
    w.update(zip(mix_out, landed))
    h2, bp, ba = _mix_out(xp, attn, gl, bias, pw, pscale, w["w_pool_out"], w["w_attn_out"], w["w_out"], h1)
    w.update(zip(FFN2, _comm("gather_ffn2_wait", [_land(ffn2_d2d)], h2)[0]))
    n2, p2, q2, s2 = _ffn_up(h2, g2, w["ffn2_w_gate"], w["ffn2_w_up"], "ffn2_up")
    dy, dyh, loss = _ffn_down_loss(h2, s2, w["ffn2_w_down"], target, "ffn2_down_loss")

    gw, gs = {}, {}
    da, db = _ffn_bwd_h(dyh, p2, q2, w["ffn2_w_down"], ffn2_d2d.token, "ffn2_bwd_h")
    gw["ffn2_w_gate"], gw["ffn2_w_up"] = _xty(da, n2, "ffn2_dw_gate"), _xty(db, n2, "ffn2_dw_up")
    gw["ffn2_w_down"] = _xty(s2, dyh, "ffn2_dw_down")
    red2 = Reduction(FFN2, place)
    swap, = _comm("reduce_ffn2_swap", [red2.swap([gw[n] for n in FFN2])], da)
    dh2, gs["ffn2_norm"] = _ffn_bwd_x(da, db, w["ffn2_w_gate"], w["ffn2_w_up"], h2, dy, _tie(g2, swap.token), "ffn2_bwd_x")
    swapped, = _comm("reduce_ffn2_swap_wait", [_land(swap)], dh2)
    exchange, = _comm("reduce_ffn2_exchange", [red2.exchange(swapped)], place)
    dgl, dbp, dba, gw["w_out"], gs["gate_bias"] = _mix_bwd_gate(dh2, w["w_out"], bp, ba, gl, _tie(bias, exchange.token))
    dattn, dpooled, gw["w_attn_out"], gw["w_pool_out"], gs["pool_w"], gs["pool_scale"] = _mix_bwd_branch(
        dbp, dba, attn, xp, pw, pscale, w["w_pool_out"], w["w_attn_out"])
    dq, dkv, gs["q_norm"], gs["k_norm"], dsk = _attn_bwd(q, kv, dattn, qw, kw, sink_rows, band)
    gs["sinks"] = dsk[:, :, 0]
    exchanged, = _comm("reduce_ffn2_exchange_wait", [_land(exchange)], dq)
    share2, = _comm("reduce_ffn2_share", [red2.share(exchanged)], place)
    pieces = (_pool_bwd(dpooled), dq, dkv, dgl)
    gw["w_in"] = _in_bwd_w(pieces, u)
    redm = Reduction(MIXER, place)
    swap, shared = _comm("reduce_mix_swap", [redm.swap([gw[n] for n in MIXER]), _land(share2)], share2.token)
    grads = red2.named(shared)
    dh1, dh1h, gs["mix_norm"] = _in_bwd_x(pieces, w["w_in"], h1, dh2, _tie(gm, swap.token))
    red1, swaps = Reduction(FFN1, place), {}
    dw_down = _xty(s1, dh1h, "ffn1_dw_down")
    swaps["ffn1_w_down"], swapped = _comm("reduce_mix_swap_wait", [red1.swap([dw_down]), _land(swap)], dh1)
    exchange, = _comm("reduce_mix_exchange", [redm.exchange(swapped)], place)
    da, db = _ffn_bwd_h(dh1h, p1, q1, w["ffn1_w_down"], exchange.token, "ffn1_bwd_h")
    dw_up = _xty(db, n1, "ffn1_dw_up")
    swaps["ffn1_w_up"], = _comm("reduce_ffn1_swap_up", [red1.swap([dw_up])], place)
    dw_gate = _xty(da, n1, "ffn1_dw_gate", after=[swaps["ffn1_w_up"].token])
    swaps["ffn1_w_gate"], exchanged = _comm("reduce_mix_exchange_wait", [red1.swap([dw_gate]), _land(exchange)], place)
    sharem, *landed = _comm("reduce_mix_share", [redm.share(exchanged)] + [_land(swaps[n]) for n in FFN1], place)
    exchange, = _comm("reduce_ffn1_exchange", [red1.exchange([l[0] for l in landed] + [l[1] for l in landed])], place)
    grad_x, gs["ffn1_norm"] = _ffn_bwd_x(da, db, w["ffn1_w_gate"], w["ffn1_w_up"], xs, dh1, _tie(g1, exchange.token), "ffn1_bwd_x")

    small_parts = [gs[n] for n in SMALL] + [loss[0, 0].reshape(1)]
    slots, shared = _comm(
        "gather_small", [_send([_slot_place(_pack_small(small_parts), place)], _slots_plan, N_DEVICES - 1), _land(sharem)],
        grad_x)
    grads.update(redm.named(shared))
    delta, new_m, new_v = {}, {}, {}

    def adamw(names, token):
        for n in names:
            delta[n], new_m[n], new_v[n], grads[n] = _adamw(wts[n], grads[n], mom[n], var[n], token, "adamw_" + n, n in TURNED)

    early, late = FFN2 + MIXER[:-1], MIXER[-1:]
    adamw(early, slots.token)
    exchanged, = _comm("reduce_ffn1_exchange_wait", [_land(exchange)], [delta[n] for n in early])
    share1, landed = _comm("reduce_ffn1_share", [red1.share(exchanged), _land(slots)], place)
    summed = _sum_slots(landed[0])
    small_grads, loss_sum = _unpack_small(summed, [wts[n] for n in SMALL])
    grads.update(dict(zip(SMALL, small_grads)))
    for out, vals in zip((delta, new_m, new_v), _adamw_small(*[[d[n] for n in SMALL] for d in (wts, grads, mom, var)])):
        out.update(dict(zip(SMALL, vals)))
    adamw(late, share1.token)
    results = {(k, n): _shard_form(n, d[n]) for k, d in enumerate((grads, delta, new_m, new_v)) for n in SMALL + FFN2 + MIXER}
    shared, = _comm("reduce_ffn1_share_wait", [_land(share1)], [results[k, n] for k in range(4) for n in SMALL + late])
    grads.update(red1.named(shared))
    adamw(FFN1, share1.token)
    results.update({(k, n): _shard_form(n, d[n]) for k, d in enumerate((grads, delta, new_m, new_v)) for n in FFN1})
    return (loss_sum, grad_x[None], *[results[k, n] for k in range(4) for n in WEIGHTS])
```

```python
from typing import Callable, NamedTuple, Optional

import jax
import jax.numpy as jnp
from jax import lax
from jax.experimental import pallas as pl
from jax.experimental.pallas import tpu as pltpu

F32 = jnp.float32
BF16 = jnp.bfloat16
RMS_EPS = 1e-6
POOL_WINDOWS = (2, 4, 8, 16)
POOL_GROUP = 128
POOL_HALO = 16
HEAD_DIM = 64
GQA_GROUP = 8
N_KV_HEADS = 2
ATTN_BLOCK = 128
SCALE = HEAD_DIM ** -0.5
NEG = -1e30
N_SHARDS = 4
ADAM_LR, ADAM_B1, ADAM_B2, ADAM_EPS, ADAM_WD, ADAM_STEP = 0.001, 0.9, 0.999, 1e-08, 0.01, 10
VMEM_LIMIT = 56 * 1024 * 1024
MESH = pl.DeviceIdType.MESH
SEG_POOL, SEG_Q, SEG_KV, SEG_GATE = (0, 512), (512, 1024), (1536, 256), (1792, 2048)
SEGMENTS = (SEG_POOL, SEG_Q, SEG_KV, SEG_GATE)


def _params(**kw):
    return pltpu.CompilerParams(vmem_limit_bytes=VMEM_LIMIT, **kw)


def _dot(a, b):
    return jnp.dot(a, b, preferred_element_type=F32)


def _dot_nt(a, b):
    return lax.dot_general(a, b, (((1,), (1,)), ((), ())), preferred_element_type=F32)


def _dot_tn(a, b):
    return lax.dot_general(a, b, (((0,), (0,)), ((), ())), preferred_element_type=F32)


def _rinv(x):
    return lax.rsqrt(jnp.mean(x * x, axis=-1, keepdims=True) + RMS_EPS)


def _rms_bwd(dn, x, g):
    r = _rinv(x)
    xr = x * r
    z = dn * g
    dx = r * (z - xr * jnp.mean(z * xr, axis=-1, keepdims=True))
    return dx, dn * xr


def _acc(ref, val, first):
    @pl.when(first)
    def _():
        ref[...] = val

    @pl.when(jnp.logical_not(first))
    def _():
        ref[...] += val


TOKEN_SPEC = pl.BlockSpec(memory_space=pl.ANY)
F_HALF = 1408


def _resident(w):
    return pl.BlockSpec(w.shape, lambda i: (0, 0), pipeline_mode=pl.Buffered(1))


def _swiglu(a, b):
    sig = jax.nn.sigmoid(a)
    q = a * sig
    return b * (sig + q * (1.0 - sig)), q, q * b


def _ffn_up(h, gnorm, wgt, wut, name, tm=512):
    S, D = h.shape
    F = wgt.shape[0]

    def body(h_ref, g_ref, wg_ref, wu_ref, n_ref, p_ref, q_ref, s_ref):
        x = h_ref[...]
        n = (x * _rinv(x) * g_ref[...]).astype(BF16)
        n_ref[...] = n
        for lo in range(0, F, F_HALF):
            cols = slice(lo, lo + F_HALF)
            p, q, s = _swiglu(_dot_nt(n, wg_ref[cols, :]), _dot_nt(n, wu_ref[cols, :]))
            p_ref[:, cols] = p.astype(BF16)
            q_ref[:, cols] = q.astype(BF16)
            s_ref[:, cols] = s.astype(BF16)

    act = pl.BlockSpec((tm, F), lambda i: (i, 0))
    hidden = jax.ShapeDtypeStruct((S, F), BF16)
    return pl.pallas_call(
        body, name=name, grid=(S // tm,),
        in_specs=[pl.BlockSpec((tm, D), lambda i: (i, 0)), pl.BlockSpec((1, D), lambda i: (0, 0)), _resident(wgt), _resident(wut)],
        out_specs=[pl.BlockSpec((tm, D), lambda i: (i, 0)), act, act, act],
        out_shape=[jax.ShapeDtypeStruct((S, D), BF16), hidden, hidden, hidden],
        compiler_params=_params(dimension_semantics=("arbitrary",)),
    )(h, gnorm, wgt, wut)


def _ffn_gate(h, gnorm, wgt, name, tm=512):
    S, D = h.shape
    F = wgt.shape[0]

    def body(h_ref, g_ref, wg_ref, n_ref, a_ref):
        x = h_ref[...]
        n = (x * _rinv(x) * g_ref[...]).astype(BF16)
        n_ref[...] = n
        for lo in range(0, F, F_HALF):
            a_ref[:, lo:lo + F_HALF] = _dot_nt(n, wg_ref[lo:lo + F_HALF, :]).astype(BF16)

    row = pl.BlockSpec((tm, D), lambda i: (i, 0))
    return pl.pallas_call(
        body, name=name, grid=(S // tm,), in_specs=[row, pl.BlockSpec((1, D), lambda i: (0, 0)), _resident(wgt)],
        out_specs=[row, pl.BlockSpec((tm, F), lambda i: (i, 0))],
        out_shape=[jax.ShapeDtypeStruct((S, D), BF16), jax.ShapeDtypeStruct((S, F), BF16)],
        compiler_params=_params(dimension_semantics=("arbitrary",)),
    )(h, gnorm, wgt)


def _ffn_hidden(n, a, wut, name, tm=512):
    S, D = n.shape
    F = wut.shape[0]

    def body(n_ref, a_ref, wu_ref, p_ref, q_ref, s_ref):
        n_t = n_ref[...]
        for lo in range(0, F, F_HALF):
            cols = slice(lo, lo + F_HALF)
            p, q, s = _swiglu(a_ref[:, cols].astype(F32), _dot_nt(n_t, wu_ref[cols, :]))
            p_ref[:, cols] = p.astype(BF16)
            q_ref[:, cols] = q.astype(BF16)
            s_ref[:, cols] = s.astype(BF16)

    act = pl.BlockSpec((tm, F), lambda i: (i, 0))
    hidden = jax.ShapeDtypeStruct((S, F), BF16)
    return pl.pallas_call(
        body, name=name, grid=(S // tm,), in_specs=[pl.BlockSpec((tm, D), lambda i: (i, 0)), act, _resident(wut)],
        out_specs=[act, act, act], out_shape=[hidden, hidden, hidden],
        compiler_params=_params(dimension_semantics=("arbitrary",)),
    )(n, a, wut)


def _ffn_down(h, s, wd, name, tm=1024):
    S, D = h.shape
    F = wd.shape[0]

    def body(h_ref, s_ref, wd_ref, o_ref):
        o_ref[...] = h_ref[...] + 0.5 * _dot(s_ref[...], wd_ref[...])

    row = pl.BlockSpec((tm, D), lambda i: (i, 0))
    return pl.pallas_call(
        body, name=name, grid=(S // tm,), in_specs=[row, pl.BlockSpec((tm, F), lambda i: (i, 0)), _resident(wd)],
        out_specs=row, out_shape=jax.ShapeDtypeStruct((S, D), F32),
        compiler_params=_params(dimension_semantics=("arbitrary",)),
    )(h, s, wd)


def _ffn_down_loss(h, s, wd, target, name, tm=512):
    S, D = h.shape
    F = wd.shape[0]

    def body(h_ref, s_ref, wd_ref, t_ref, dy_ref, dyh_ref, loss_ref):
        e = h_ref[...] + 0.5 * _dot(s_ref[...], wd_ref[...]) - t_ref[...]
        dy = e * (1.0 / D)
        dy_ref[...] = dy
        dyh_ref[...] = (0.5 * dy).astype(BF16)
        tot = jnp.sum(jnp.sum(e * e, axis=1, keepdims=True), axis=0, keepdims=True) * (0.5 / D)
        _acc(loss_ref, jnp.broadcast_to(tot, loss_ref.shape), pl.program_id(0) == 0)

    row = pl.BlockSpec((tm, D), lambda i: (i, 0))
    return pl.pallas_call(
        body, name=name, grid=(S // tm,), in_specs=[row, pl.BlockSpec((tm, F), lambda i: (i, 0)), _resident(wd), row],
        out_specs=[row, row, pl.BlockSpec((8, 128), lambda i: (0, 0))],
        out_shape=[jax.ShapeDtypeStruct((S, D), F32), jax.ShapeDtypeStruct((S, D), BF16), jax.ShapeDtypeStruct((8, 128), F32)],
        compiler_params=_params(dimension_semantics=("arbitrary",)),
    )(h, s, wd, target)


def _ffn_bwd_h(dyh, p, q, wd, token, name, tm=512):
    S, D = dyh.shape
    F = wd.shape[0]

    def body(dyh_ref, p_ref, q_ref, wd_ref, _, da_ref, db_ref):
        dyh_t = dyh_ref[...]
        for lo in range(0, F, F_HALF):
            cols = slice(lo, lo + F_HALF)
            ds = _dot_nt(dyh_t, wd_ref[cols, :])
            da_ref[:, cols] = (ds * p_ref[:, cols].astype(F32)).astype(BF16)
            db_ref[:, cols] = (ds * q_ref[:, cols].astype(F32)).astype(BF16)

    act = pl.BlockSpec((tm, F), lambda i: (i, 0))
    hidden = jax.ShapeDtypeStruct((S, F), BF16)
    return pl.pallas_call(
        body, name=name, grid=(S // tm,),
        in_specs=[pl.BlockSpec((tm, D), lambda i: (i, 0)), act, act, _resident(wd), TOKEN_SPEC],
        out_specs=[act, act], out_shape=[hidden, hidden],
        compiler_params=_params(dimension_semantics=("arbitrary",)),
    )(dyh, p, q, wd, token)


def _xty(x, y, name, tk=2048, tf=1408, after=()):
    S, F = x.shape
    D = y.shape[1]

    def body(x_ref, y_ref, *rest):
        _acc(rest[-1], _dot_tn(x_ref[...], y_ref[...]), pl.program_id(1) == 0)

    return pl.pallas_call(
        body, name=name, grid=(F // tf, S // tk),
        in_specs=[pl.BlockSpec((tk, tf), lambda j, k: (k, j)), pl.BlockSpec((tk, D), lambda j, k: (k, 0))]
        + [TOKEN_SPEC] * len(after),
        out_specs=pl.BlockSpec((tf, D), lambda j, k: (j, 0)), out_shape=jax.ShapeDtypeStruct((F, D), F32),
        compiler_params=_params(dimension_semantics=("arbitrary", "arbitrary")),
    )(x, y, *after)


def _ffn_bwd_x(da, db, wgt, wut, x, dh, gnorm, name, tm=512):
    S, D = x.shape
    F = wgt.shape[0]

    def body(da_ref, db_ref, wg_ref, wu_ref, x_ref, dh_ref, g_ref, dx_ref, dg_ref):
        dn = _dot(da_ref[...], wg_ref[...]) + _dot(db_ref[...], wu_ref[...])
        dx, dg_rows = _rms_bwd(dn, x_ref[...], g_ref[...])
        dx_ref[...] = dh_ref[...] + dx
        _acc(dg_ref, jnp.sum(dg_rows, axis=0, keepdims=True), pl.program_id(0) == 0)

    row = pl.BlockSpec((tm, D), lambda i: (i, 0))
    act = pl.BlockSpec((tm, F), lambda i: (i, 0))
    vec = pl.BlockSpec((1, D), lambda i: (0, 0))
    return pl.pallas_call(
        body, name=name, grid=(S // tm,), in_specs=[act, act, _resident(wgt), _resident(wut), row, row, vec],
        out_specs=[row, vec],
        out_shape=[jax.ShapeDtypeStruct((S, D), F32), jax.ShapeDtypeStruct((1, D), F32)],
        compiler_params=_params(dimension_semantics=("arbitrary",)),
    )(da, db, wgt, wut, x, dh, gnorm)


def _mix_proj(h, gnorm, wint, tm=512):
    S, D = h.shape
    nt = S // tm

    def body(h_ref, g_ref, w_ref, u_ref, xp_ref, q_ref, kv_ref, gl_ref):
        x = h_ref[...]
        u = (x * _rinv(x) * g_ref[...]).astype(BF16)
        u_ref[...] = u
        for (off, width), ref in zip(SEGMENTS, (xp_ref, q_ref, kv_ref, gl_ref)):
            ref[...] = _dot_nt(u, w_ref[off:off + width, :]).astype(ref.dtype)

    def row(width):
        return pl.BlockSpec((tm, width), lambda i: (i, 0))

    return pl.pallas_call(
        body, name="mix_proj", grid=(nt,),
        in_specs=[row(D), pl.BlockSpec((1, D), lambda i: (0, 0)), pl.BlockSpec(wint.shape, lambda i: (0, 0))],
        out_specs=[row(D), row(SEG_POOL[1]), row(SEG_Q[1]), row(SEG_KV[1]), row(SEG_GATE[1])],
        out_shape=[jax.ShapeDtypeStruct((S, D), BF16), jax.ShapeDtypeStruct((S, SEG_POOL[1]), F32),
                   jax.ShapeDtypeStruct((S, SEG_Q[1]), BF16), jax.ShapeDtypeStruct((S, SEG_KV[1]), BF16),
                   jax.ShapeDtypeStruct((S, SEG_GATE[1]), BF16)],
        compiler_params=_params(dimension_semantics=("arbitrary",)),
    )(h, gnorm, wint)


def _stack_heads(x, g):
    return jnp.concatenate([x[:, (GQA_GROUP * g + hh) * HEAD_DIM:(GQA_GROUP * g + hh + 1) * HEAD_DIM]
                            for hh in range(GQA_GROUP)], axis=0)


def _unstack_heads(ref, val, g):
    for hh in range(GQA_GROUP):
        lo = (GQA_GROUP * g + hh) * HEAD_DIM
        ref[:, lo:lo + HEAD_DIM] = val[hh * ATTN_BLOCK:(hh + 1) * ATTN_BLOCK, :]


def _rowsum(xb, width):
    return _dot(xb, jnp.ones((xb.shape[1], width), BF16))


def _rinv_lanes(x):
    return lax.rsqrt(_rowsum((x * x).astype(BF16), x.shape[1]) * (1.0 / x.shape[1]) + RMS_EPS)


def _twice(x):
    return jnp.concatenate([x, x], axis=1)


def _band_bias():
    qi = (jnp.arange(GQA_GROUP * ATTN_BLOCK) % ATTN_BLOCK)[:, None]
    kj = jnp.arange(2 * ATTN_BLOCK)[None, :]
    return jnp.where(jnp.logical_and(kj > qi, kj <= qi + ATTN_BLOCK), 0.0, NEG).astype(F32)


def _attn_exp(qn, kk, sink, band, n):
    kj = lax.broadcasted_iota(jnp.int32, (1, 2 * ATTN_BLOCK), 1)
    before_sequence = jnp.where(jnp.logical_and(kj < ATTN_BLOCK, n == 0), NEG, 0.0)
    s = _dot_nt(qn, kk) + band + before_sequence
    m = jnp.maximum(jnp.broadcast_to(jnp.max(s, axis=-1, keepdims=True), sink.shape), sink)
    e = jnp.exp(s - _twice(m))
    e_s = jnp.exp(sink - m)
    e_b = e.astype(BF16)
    inv = 1.0 / (_rowsum(e_b, ATTN_BLOCK) + e_s)
    return e_b, e, e_s, inv


def _attn_blocks(n):
    cur = pl.multiple_of(n * ATTN_BLOCK, ATTN_BLOCK)
    prev = pl.multiple_of(jnp.maximum(n - 1, 0) * ATTN_BLOCK, ATTN_BLOCK)
    return cur, prev


def _kv_window(kv_ref, n):
    cur, prev = _attn_blocks(n)
    return jnp.concatenate([kv_ref[pl.ds(prev, ATTN_BLOCK), :], kv_ref[pl.ds(cur, ATTN_BLOCK), :]], axis=0).astype(F32)


def _kv_split(kv, g):
    k = kv[:, g * HEAD_DIM:(g + 1) * HEAD_DIM]
    v = kv[:, (N_KV_HEADS + g) * HEAD_DIM:(N_KV_HEADS + g + 1) * HEAD_DIM]
    return k, v


def _attn_fwd(q, kv, qw, kw, sink_rows, band):
    S, W = q.shape
    per = 2
    rows = per * ATTN_BLOCK

    def body(q_ref, kv_ref, qw_ref, kw_ref, sk_ref, band_ref, o_ref, o_scr):
        qf = q_ref[...].astype(F32)
        pairs = [(b, g) for b in range(per) for g in range(N_KV_HEADS)]
        ns = [per * pl.program_id(0) + b for b in range(per)]
        kvw = [_kv_window(kv_ref, n) for n in ns]
        qs = [_stack_heads(qf[b * ATTN_BLOCK:(b + 1) * ATTN_BLOCK], g) for b, g in pairs]
        kvs = [_kv_split(kvw[b], g) for b, g in pairs]
        rq = [_rinv_lanes(x) for x in qs]
        rk = [_rinv_lanes(k) for k, _ in kvs]
        qn = [(x * r * qw_ref[...] * SCALE).astype(BF16) for x, r in zip(qs, rq)]
        kk = [(k * r * kw_ref[...]).astype(BF16) for (k, _), r in zip(kvs, rk)]
        ex = [_attn_exp(qn[i], kk[i], sk_ref[g], band_ref[...], ns[b]) for i, (b, g) in enumerate(pairs)]
        pv = [_dot(ex[i][0], kvs[i][1].astype(BF16)) for i in range(len(pairs))]
        for i, (b, g) in enumerate(pairs):
            _unstack_heads(o_scr.at[b * ATTN_BLOCK:(b + 1) * ATTN_BLOCK, :], pv[i] * ex[i][3][:, :HEAD_DIM], g)
        o_ref[...] = o_scr[...].astype(BF16)

    blk = pl.BlockSpec((rows, W), lambda n: (n, 0))
    return pl.pallas_call(
        body, name="attn_fwd", grid=(S // rows,),
        in_specs=[blk, pl.BlockSpec(kv.shape, lambda n: (0, 0)), pl.BlockSpec((1, HEAD_DIM), lambda n: (0, 0)),
                  pl.BlockSpec((1, HEAD_DIM), lambda n: (0, 0)), pl.BlockSpec(sink_rows.shape, lambda n: (0, 0, 0)),
                  pl.BlockSpec(band.shape, lambda n: (0, 0))],
        out_specs=blk, out_shape=jax.ShapeDtypeStruct((S, W), BF16),
        scratch_shapes=[pltpu.VMEM((rows, W), F32)],
        compiler_params=_params(dimension_semantics=("arbitrary",)),
    )(q, kv, qw, kw, sink_rows, band)


def _attn_bwd(q, kv, do, qw, kw, sink_rows, band):
    S, W = q.shape
    KW = kv.shape[1]
    nb = S // ATTN_BLOCK
    per = 2
    chunk = 512

    def body(q_ref, kv_ref, do_ref, qw_ref, kw_ref, sk_ref, band_ref, dq_ref, dkv_ref, dqw_ref, dkw_ref, dsk_ref, dq_scr):
        step = pl.program_id(0)

        @pl.when(step == 0)
        def _():
            dkv_ref[...] = jnp.zeros_like(dkv_ref)
            dqw_ref[...] = jnp.zeros_like(dqw_ref)
            dsk_ref[...] = jnp.zeros_like(dsk_ref)

        qw_v = qw_ref[...]
        for b in range(per):
            n = per * step + b
            mine = slice(b * ATTN_BLOCK, (b + 1) * ATTN_BLOCK)
            qf = q_ref[mine, :].astype(F32)
            dof = do_ref[mine, :].astype(F32)
            cur, prev = _attn_blocks(n)
            kvw = _kv_window(kv_ref, n)
            for g in range(N_KV_HEADS):
                qs = _stack_heads(qf, g)
                rq = _rinv_lanes(qs)
                qhat = qs * rq
                qn = (qhat * qw_v * SCALE).astype(BF16)
                k, v = _kv_split(kvw, g)
                kk = (k * _rinv_lanes(k) * kw_ref[...]).astype(BF16)
                vv = v.astype(BF16)
                dos = _stack_heads(dof, g).astype(BF16)
                _, e, e_s, inv = _attn_exp(qn, kk, sk_ref[g], band_ref[...], n)
                p = e * _twice(inv)
                dp = _dot_nt(dos, vv)
                drow = _rowsum((p * dp).astype(BF16), ATTN_BLOCK)
                ds = (p * (dp - _twice(drow))).astype(BF16)
                dsink = -(e_s * inv * drow)
                for hh in range(GQA_GROUP):
                    tot = jnp.sum(dsink[hh * ATTN_BLOCK:(hh + 1) * ATTN_BLOCK, :], axis=0, keepdims=True)
                    dsk_ref[g, hh:hh + 1, :] += tot
                dqn = _dot(ds, kk) * SCALE
                dkk = _dot_tn(ds, qn)
                dvv = _dot_tn(p.astype(BF16), dos)
                klo, vlo = g * HEAD_DIM, (N_KV_HEADS + g) * HEAD_DIM
                for start, rows in ((prev, slice(0, ATTN_BLOCK)), (cur, slice(ATTN_BLOCK, 2 * ATTN_BLOCK))):
                    dkv_ref[pl.ds(start, ATTN_BLOCK), klo:klo + HEAD_DIM] += dkk[rows]
                    dkv_ref[pl.ds(start, ATTN_BLOCK), vlo:vlo + HEAD_DIM] += dvv[rows]
                dqw_ref[...] += jnp.sum(dqn * qhat, axis=0, keepdims=True)
                z = dqn * qw_v
                dqs = rq * (z - qhat * (_rowsum((z * qhat).astype(BF16), HEAD_DIM) * (1.0 / HEAD_DIM)))
                _unstack_heads(dq_scr.at[mine, :], dqs, g)
        dq_ref[...] = dq_scr[...].astype(BF16)

        @pl.when(step == nb // per - 1)
        def _():
            def one(c, dkw):
                rows = pl.ds(pl.multiple_of(c * chunk, chunk), chunk)
                for g in range(N_KV_HEADS):
                    lo = g * HEAD_DIM
                    k = kv_ref[rows, lo:lo + HEAD_DIM].astype(F32)
                    dx, dg_rows = _rms_bwd(dkv_ref[rows, lo:lo + HEAD_DIM], k, kw_ref[...])
                    dkv_ref[rows, lo:lo + HEAD_DIM] = dx
                    dkw = dkw + jnp.sum(dg_rows, axis=0, keepdims=True)
                return dkw

            dkw_ref[...] = lax.fori_loop(0, S // chunk, one, jnp.zeros((1, HEAD_DIM), F32))

    blk = pl.BlockSpec((per * ATTN_BLOCK, W), lambda n: (n, 0))
    whole_kv = pl.BlockSpec((S, KW), lambda n: (0, 0))
    vec = pl.BlockSpec((1, HEAD_DIM), lambda n: (0, 0))
    sk = pl.BlockSpec(sink_rows.shape, lambda n: (0, 0, 0))
    dsk = pl.BlockSpec((N_KV_HEADS, GQA_GROUP, 128), lambda n: (0, 0, 0))
    return pl.pallas_call(
        body, name="attn_bwd", grid=(nb // per,),
        in_specs=[blk, whole_kv, blk, vec, vec, sk, pl.BlockSpec(band.shape, lambda n: (0, 0))],
        out_specs=[blk, whole_kv, vec, vec, dsk],
        out_shape=[jax.ShapeDtypeStruct((S, W), BF16), jax.ShapeDtypeStruct((S, KW), F32),
                   jax.ShapeDtypeStruct((1, HEAD_DIM), F32), jax.ShapeDtypeStruct((1, HEAD_DIM), F32),
                   jax.ShapeDtypeStruct((N_KV_HEADS, GQA_GROUP, 128), F32)],
        scratch_shapes=[pltpu.VMEM((per * ATTN_BLOCK, W), F32)],
        compiler_params=_params(dimension_semantics=("arbitrary",)),
    )(q, kv, do, qw, kw, sink_rows, band)


def _pooled(xc, xprev, i):
    tm = xc.shape[0]
    xh = jnp.concatenate([jnp.where(i > 0, xprev, 0.0), xc], axis=0)
    t = lax.broadcasted_iota(jnp.int32, (tm, 1), 0) + i * tm
    out = []
    for gi, w in enumerate(POOL_WINDOWS):
        acc = xh[:, gi * POOL_GROUP:(gi + 1) * POOL_GROUP]
        sh = 1
        while sh < w:
            acc = acc + pltpu.roll(acc, sh, 0)
            sh *= 2
        cnt = jnp.minimum(t + 1, w).astype(F32)
        out.append(acc[POOL_HALO:, :] / cnt - xc[:, gi * POOL_GROUP:(gi + 1) * POOL_GROUP])
    return jnp.concatenate(out, axis=1)


def _pool_mix(pooled_b, pw_ref):
    return jnp.concatenate([_dot(pooled_b[:, gi * POOL_GROUP:(gi + 1) * POOL_GROUP], pw_ref[gi])
                            for gi in range(len(POOL_WINDOWS))], axis=1)


def _halo_specs(tm, width, S, after):
    per = tm // POOL_HALO
    last = S // POOL_HALO - 1
    if after:
        return pl.BlockSpec((POOL_HALO, width), lambda i: (jnp.minimum((i + 1) * per, last), 0))
    return pl.BlockSpec((POOL_HALO, width), lambda i: (jnp.maximum(i * per - 1, 0), 0))


def _mix_out(xp, attn, gl, bias, pw, pscale, wpot, wao, wo, h, tm=512):
    S, D = h.shape
    nt = S // tm
    PW = xp.shape[1]

    def body(xc_ref, xprev_ref, at_ref, gl_ref, bias_ref, pw_ref, ps_ref, wpo_ref, wao_ref, wo_ref, h_ref,
             ho_ref, bp_ref, ba_ref):
        i = pl.program_id(0)
        pooled = _pooled(xc_ref[...], xprev_ref[...], i).astype(BF16)
        ms = (_pool_mix(pooled, pw_ref) * ps_ref[...]).astype(BF16)
        bp = _dot_nt(ms, wpo_ref[...])
        ba = _dot(at_ref[...], wao_ref[...])
        bp_ref[...] = bp.astype(BF16)
        ba_ref[...] = ba.astype(BF16)
        gates = jax.nn.sigmoid(gl_ref[...].astype(F32) + bias_ref[...])
        merged = (gates[:, :D] * bp + gates[:, D:] * ba).astype(BF16)
        ho_ref[...] = h_ref[...] + _dot(merged, wo_ref[...])

    def row(width):
        return pl.BlockSpec((tm, width), lambda i: (i, 0))

    def whole(x):
        nd = x.ndim
        return pl.BlockSpec(x.shape, lambda i: (0,) * nd)

    return pl.pallas_call(
        body, name="mix_out", grid=(nt,),
        in_specs=[row(PW), _halo_specs(tm, PW, S, False), row(D), row(2 * D), whole(bias), whole(pw), whole(pscale),
                  whole(wpot), whole(wao), whole(wo), row(D)],
        out_specs=[row(D), row(D), row(D)],
        out_shape=[jax.ShapeDtypeStruct((S, D), F32), jax.ShapeDtypeStruct((S, D), BF16),
                   jax.ShapeDtypeStruct((S, D), BF16)],
        compiler_params=_params(dimension_semantics=("arbitrary",)),
    )(xp, xp, attn, gl, bias, pw, pscale, wpot, wao, wo, h)


def _mix_bwd_gate(dh, wo, bp, ba, gl, bias, tm=512):
    S, D = dh.shape
    nt = S // tm

    def body(dh_ref, wo_ref, bp_ref, ba_ref, gl_ref, bias_ref, dgl_ref, dbp_ref, dba_ref, dwo_ref, dbias_ref):
        i = pl.program_id(0)
        dhb = dh_ref[...].astype(BF16)
        dm = _dot_nt(dhb, wo_ref[...])
        gates = jax.nn.sigmoid(gl_ref[...].astype(F32) + bias_ref[...])
        gp, ga = gates[:, :D], gates[:, D:]
        bp_v = bp_ref[...].astype(F32)
        ba_v = ba_ref[...].astype(F32)
        merged = (gp * bp_v + ga * ba_v).astype(BF16)
        _acc(dwo_ref, _dot_tn(merged, dhb), i == 0)
        dbp_ref[...] = (dm * gp).astype(BF16)
        dba_ref[...] = (dm * ga).astype(BF16)
        dgl = jnp.concatenate([dm * bp_v * gp * (1.0 - gp), dm * ba_v * ga * (1.0 - ga)], axis=1)
        dgl_ref[...] = dgl.astype(BF16)
        _acc(dbias_ref, jnp.sum(dgl, axis=0, keepdims=True), i == 0)

    def row(width):
        return pl.BlockSpec((tm, width), lambda i: (i, 0))

    def whole(shape):
        return pl.BlockSpec(shape, lambda i: (0, 0))

    return pl.pallas_call(
        body, name="mix_bwd_gate", grid=(nt,),
        in_specs=[row(D), whole(wo.shape), row(D), row(D), row(2 * D), whole(bias.shape)],
        out_specs=[row(2 * D), row(D), row(D), whole((D, D)), whole((1, 2 * D))],
        out_shape=[jax.ShapeDtypeStruct((S, 2 * D), BF16), jax.ShapeDtypeStruct((S, D), BF16),
                   jax.ShapeDtypeStruct((S, D), BF16), jax.ShapeDtypeStruct((D, D), F32),
                   jax.ShapeDtypeStruct((1, 2 * D), F32)],
        compiler_params=_params(dimension_semantics=("arbitrary",)),
    )(dh, wo, bp, ba, gl, bias)


def _mix_bwd_branch(dbp, dba, attn, xp, pw, pscale, wpot, wao, tm=1024):
    S, D = dbp.shape
    nt = S // tm
    PW = xp.shape[1]
    NG = len(POOL_WINDOWS)

    def body(dbp_ref, dba_ref, at_ref, xc_ref, xprev_ref, pw_ref, ps_ref, wpo_ref, wao_ref,
             dat_ref, dpl_ref, dwao_ref, dwpo_ref, dpw_ref, dps_ref):
        i = pl.program_id(0)
        dba_v = dba_ref[...]
        dbp_v = dbp_ref[...]
        _acc(dwao_ref, _dot_tn(at_ref[...], dba_v), i == 0)
        dat_ref[...] = _dot_nt(dba_v, wao_ref[...]).astype(BF16)
        pooled = _pooled(xc_ref[...], xprev_ref[...], i).astype(BF16)
        mixed = _pool_mix(pooled, pw_ref)
        ps = ps_ref[...]
        _acc(dwpo_ref, _dot_tn(dbp_v, (mixed * ps).astype(BF16)), i == 0)
        dms = _dot(dbp_v, wpo_ref[...])
        _acc(dps_ref, jnp.sum(dms * mixed, axis=0, keepdims=True), i == 0)
        dmixed = (dms * ps).astype(BF16)
        dpooled = []
        for gi in range(NG):
            cols = slice(gi * POOL_GROUP, (gi + 1) * POOL_GROUP)
            _acc(dpw_ref.at[gi], _dot_tn(pooled[:, cols], dmixed[:, cols]), i == 0)
            dpooled.append(_dot_nt(dmixed[:, cols], pw_ref[gi]))
        dpl_ref[...] = jnp.concatenate(dpooled, axis=1)

    def row(width):
        return pl.BlockSpec((tm, width), lambda i: (i, 0))

    def whole(shape):
        nd = len(shape)
        return pl.BlockSpec(shape, lambda i: (0,) * nd)

    return pl.pallas_call(
        body, name="mix_bwd_branch", grid=(nt,),
        in_specs=[row(D), row(D), row(D), row(PW), _halo_specs(tm, PW, S, False), whole(pw.shape), whole(pscale.shape),
                  whole(wpot.shape), whole(wao.shape)],
        out_specs=[row(D), row(PW), whole((D, D)), whole((D, PW)), whole(pw.shape), whole((1, PW))],
        out_shape=[jax.ShapeDtypeStruct((S, D), BF16), jax.ShapeDtypeStruct((S, PW), F32),
                   jax.ShapeDtypeStruct((D, D), F32), jax.ShapeDtypeStruct((D, PW), F32),
                   jax.ShapeDtypeStruct(pw.shape, F32), jax.ShapeDtypeStruct((1, PW), F32)],
        compiler_params=_params(dimension_semantics=("arbitrary",)),
    )(dbp, dba, attn, xp, xp, pw, pscale, wpot, wao)


def _pool_bwd(dpooled, tm=512):
    S, PW = dpooled.shape
    nt = S // tm

    def body(dc_ref, dnext_ref, dxp_ref):
        i = pl.program_id(0)
        dc = dc_ref[...]
        dh = jnp.concatenate([dc, jnp.where(i < nt - 1, dnext_ref[...], 0.0)], axis=0)
        rows = tm + POOL_HALO
        t = lax.broadcasted_iota(jnp.int32, (rows, 1), 0) + i * tm
        out = []
        for gi, w in enumerate(POOL_WINDOWS):
            cols = slice(gi * POOL_GROUP, (gi + 1) * POOL_GROUP)
            acc = dh[:, cols] / jnp.minimum(t + 1, w).astype(F32)
            sh = 1
            while sh < w:
                acc = acc + pltpu.roll(acc, rows - sh, 0)
                sh *= 2
            out.append(acc[:tm, :] - dc[:, cols])
        dxp_ref[...] = jnp.concatenate(out, axis=1).astype(BF16)

    return pl.pallas_call(
        body, name="pool_bwd", grid=(nt,),
        in_specs=[pl.BlockSpec((tm, PW), lambda i: (i, 0)), _halo_specs(tm, PW, S, True)],
        out_specs=pl.BlockSpec((tm, PW), lambda i: (i, 0)), out_shape=jax.ShapeDtypeStruct((S, PW), BF16),
        compiler_params=_params(dimension_semantics=("arbitrary",)),
    )(dpooled, dpooled)


def _in_bwd_w(pieces, u, tm=1024):
    S, D = u.shape
    nt = S // tm
    NW = sum(width for _, width in SEGMENTS)

    def body(*refs):
        piece_refs, u_ref, dw_hbm, acc, sem = refs[:len(SEGMENTS)], refs[len(SEGMENTS)], refs[len(SEGMENTS) + 1], refs[-2], refs[-1]
        i = pl.program_id(0)
        u_t = u_ref[...]
        for (off, width), ref in zip(SEGMENTS, piece_refs):
            for lo in range(0, width, D):
                hi = min(lo + D, width)
                _acc(acc.at[off + lo:off + hi, :], _dot_tn(ref[:, lo:hi].astype(BF16), u_t), i == 0)

        @pl.when(i == nt - 1)
        def _():
            cp = pltpu.make_async_copy(acc, dw_hbm, sem)
            cp.start()
            cp.wait()

    return pl.pallas_call(
        body, name="in_bwd_w", grid=(nt,),
        in_specs=[pl.BlockSpec((tm, width), lambda i: (i, 0)) for _, width in SEGMENTS] + [pl.BlockSpec((tm, D), lambda i: (i, 0))],
        out_specs=pl.BlockSpec(memory_space=pl.ANY), out_shape=jax.ShapeDtypeStruct((NW, D), F32),
        scratch_shapes=[pltpu.VMEM((NW, D), F32), pltpu.SemaphoreType.DMA],
        compiler_params=_params(dimension_semantics=("arbitrary",)),
    )(*pieces, u)


def _in_bwd_x(pieces, wint, h, dh, gnorm, tm=512):
    S, D = h.shape
    nt = S // tm

    def body(*refs):
        piece_refs = refs[:len(SEGMENTS)]
        w_ref, h_ref, dh_ref, g_ref, dx_ref, dxh_ref, dg_ref = refs[len(SEGMENTS):]
        i = pl.program_id(0)
        du = jnp.zeros((tm, D), F32)
        for (off, width), ref in zip(SEGMENTS, piece_refs):
            du = du + _dot(ref[...].astype(BF16), w_ref[off:off + width, :])
        dx, dg_rows = _rms_bwd(du, h_ref[...], g_ref[...])
        out = dh_ref[...] + dx
        dx_ref[...] = out
        dxh_ref[...] = (0.5 * out).astype(BF16)
        _acc(dg_ref, jnp.sum(dg_rows, axis=0, keepdims=True), i == 0)

    def row(width):
        return pl.BlockSpec((tm, width), lambda i: (i, 0))

    vec = pl.BlockSpec((1, D), lambda i: (0, 0))
    return pl.pallas_call(
        body, name="in_bwd_x", grid=(nt,),
        in_specs=[row(width) for _, width in SEGMENTS] + [pl.BlockSpec(wint.shape, lambda i: (0, 0)), row(D), row(D), vec],
        out_specs=[row(D), row(D), vec],
        out_shape=[jax.ShapeDtypeStruct((S, D), F32), jax.ShapeDtypeStruct((S, D), BF16), jax.ShapeDtypeStruct((1, D), F32)],
        compiler_params=_params(dimension_semantics=("arbitrary",)),
    )(*pieces, wint, h, dh, gnorm)


def _row_tile(rows):
    for t in (512, 480, 352, 256, 128, 64, 32, 16, 8):
        if rows % t == 0:
            return t
    return rows


def _adamw_update(w, g, m, v):
    mn = ADAM_B1 * m + (1.0 - ADAM_B1) * g
    vn = ADAM_B2 * v + (1.0 - ADAM_B2) * (g * g)
    m_hat = mn / (1.0 - ADAM_B1 ** ADAM_STEP)
    v_hat = vn / (1.0 - ADAM_B2 ** ADAM_STEP)
    return -ADAM_LR * (m_hat / (jnp.sqrt(v_hat) + ADAM_EPS) + ADAM_WD * w), mn, vn


def _adamw_small(ws, gs, ms, vs):
    n = len(ws)

    def rows(a):
        return a.reshape(-1, a.shape[-1]) if a.ndim > 1 else a.reshape(1, -1)

    def body(*refs):
        for k in range(n):
            w_ref, g_ref, m_ref, v_ref, d_ref, mo_ref, vo_ref = refs[k::n]
            d_ref[...], mo_ref[...], vo_ref[...] = _adamw_update(w_ref[...], g_ref[...], m_ref[...], v_ref[...])

    vmem = pl.BlockSpec(memory_space=pltpu.VMEM)
    out = pl.pallas_call(
        body, name="adamw_small", in_specs=[vmem] * (4 * n), out_specs=[vmem] * (3 * n),
        out_shape=[jax.ShapeDtypeStruct(rows(w).shape, F32) for _ in range(3) for w in ws], compiler_params=_params(),
    )(*[rows(a) for a in ws + gs + ms + vs])
    return [[o.reshape(w.shape) for o, w in zip(out[k * n:(k + 1) * n], ws)] for k in range(3)]


def _adamw(w, g, m, v, token, name, turned=False):
    R, C = w.shape
    tr = R if turned else _row_tile(R)

    def body(w_ref, g_ref, m_ref, v_ref, _, d_ref, mo_ref, vo_ref, go_ref):
        gv = g_ref[...].T if turned else g_ref[...]
        go_ref[...] = gv
        d_ref[...], mo_ref[...], vo_ref[...] = _adamw_update(w_ref[...], gv, m_ref[...], v_ref[...])

    blk = pl.BlockSpec((tr, C), lambda i: (i, 0))
    g_blk = pl.BlockSpec((C, R), lambda i: (0, 0)) if turned else blk
    sh = jax.ShapeDtypeStruct((R, C), F32)
    return pl.pallas_call(
        body, name=name, grid=(R // tr,), in_specs=[blk, g_blk, blk, blk, TOKEN_SPEC], out_specs=[blk] * 4, out_shape=[sh] * 4,
        compiler_params=_params(dimension_semantics=("arbitrary",)),
    )(w, g, m, v, token)


def _cast_place(w, place, name, turned=False, after=()):
    R, C = w.shape[::-1] if turned else w.shape
    tr = R if turned else _row_tile(R)
    per = R // tr

    def body(p_ref, w_ref, *rest):
        rest[-1][...] = (w_ref[...].T if turned else w_ref[...]).astype(BF16)

    grid_spec = pltpu.PrefetchScalarGridSpec(
        num_scalar_prefetch=1, grid=(per,),
        in_specs=[pl.BlockSpec((C, R), lambda i, p: (0, 0)) if turned else pl.BlockSpec((tr, C), lambda i, p: (i, 0))]
        + [TOKEN_SPEC] * len(after),
        out_specs=pl.BlockSpec((tr, C), lambda i, p: (p[0] * per + i, 0)))
    return pl.pallas_call(
        body, name=name, grid_spec=grid_spec, out_shape=jax.ShapeDtypeStruct((N_SHARDS * R, C), BF16),
        compiler_params=_params(dimension_semantics=("arbitrary",)),
    )(place, w, *after)


def _sum_halves(g4s, recvs, place, name):
    M = len(g4s)
    NS, R, C = g4s[0].shape
    hr = R // 2
    pieces = [(w, s) for w in range(M) for s in range(NS)]
    n_in, n_out = 4, 2

    def body(p_ref, *refs):
        g_refs, r_refs, o_refs = refs[:M], refs[M:2 * M], refs[2 * M:3 * M]
        gbuf, rbuf, obuf, in_sem, out_sem = refs[3 * M:]
        mine = pl.ds(pl.multiple_of(p_ref[1] * hr, 8), hr)

        def fetch(k):
            w, s = pieces[k]
            return (pltpu.make_async_copy(g_refs[w].at[s, mine, :], gbuf.at[k % n_in], in_sem.at[0, k % n_in]),
                    pltpu.make_async_copy(r_refs[w].at[s], rbuf.at[k % n_in], in_sem.at[1, k % n_in]))

        def put(k):
            w, s = pieces[k]
            return pltpu.make_async_copy(obuf.at[k % n_out], o_refs[w].at[s], out_sem.at[k % n_out])

        for k in range(min(n_in, len(pieces))):
            for copy in fetch(k):
                copy.start()
        for k in range(len(pieces)):
            for copy in fetch(k):
                copy.wait()
            if k >= n_out:
                put(k - n_out).wait()
            obuf[k % n_out] = (gbuf[k % n_in] + rbuf[k % n_in]).astype(BF16)
            put(k).start()
            if k + n_in < len(pieces):
                for copy in fetch(k + n_in):
                    copy.start()
        for k in range(max(0, len(pieces) - n_out), len(pieces)):
            put(k).wait()

    return pl.pallas_call(
        body, name=name, in_specs=[pl.BlockSpec(memory_space=pltpu.SMEM)] + [TOKEN_SPEC] * (2 * M),
        out_specs=[TOKEN_SPEC] * M, out_shape=[jax.ShapeDtypeStruct((NS, hr, C), BF16)] * M,
        scratch_shapes=[pltpu.VMEM((n_in, hr, C), F32), pltpu.VMEM((n_in, hr, C), F32), pltpu.VMEM((n_out, hr, C), BF16),
                        pltpu.SemaphoreType.DMA((2, n_in)), pltpu.SemaphoreType.DMA((n_out,))],
        compiler_params=_params(),
    )(place, *g4s, *recvs)


def _sum_quarters(h4s, recv3s, place, name):
    M = len(h4s)
    NS, hr, C = h4s[0].shape
    tr = hr // 2
    pieces = [(w, r) for w in range(M) for r in range(0, hr, tr)]
    n_in, n_out = 3, 2

    def body(p_ref, *refs):
        h_refs, r_refs, o_refs = refs[:M], refs[M:2 * M], refs[2 * M:3 * M]
        hbuf, rbuf, obuf, in_sem, out_sem = refs[3 * M:]
        base = pl.multiple_of(p_ref[1] * hr, 8)

        def fetch(k):
            w, r = pieces[k]
            return (pltpu.make_async_copy(h_refs[w].at[p_ref[0], r:r + tr, :], hbuf.at[k % n_in], in_sem.at[0, k % n_in]),
                    pltpu.make_async_copy(r_refs[w].at[:, r:r + tr, :], rbuf.at[k % n_in], in_sem.at[1, k % n_in]))

        def put(k):
            w, r = pieces[k]
            return pltpu.make_async_copy(obuf.at[k % n_out], o_refs[w].at[pl.ds(base + r, tr), :], out_sem.at[k % n_out])

        for k in range(min(n_in, len(pieces))):
            for copy in fetch(k):
                copy.start()
        for k in range(len(pieces)):
            for copy in fetch(k):
                copy.wait()
            if k >= n_out:
                put(k - n_out).wait()
            acc = hbuf[k % n_in].astype(F32)
            for j in range(N_SHARDS - 1):
                acc = acc + rbuf[k % n_in, j].astype(F32)
            obuf[k % n_out] = acc
            put(k).start()
            if k + n_in < len(pieces):
                for copy in fetch(k + n_in):
                    copy.start()
        for k in range(max(0, len(pieces) - n_out), len(pieces)):
            put(k).wait()

    return pl.pallas_call(
        body, name=name, in_specs=[pl.BlockSpec(memory_space=pltpu.SMEM)] + [TOKEN_SPEC] * (2 * M),
        out_specs=[TOKEN_SPEC] * M, out_shape=[jax.ShapeDtypeStruct((2 * hr, C), F32)] * M,
        scratch_shapes=[pltpu.VMEM((n_in, tr, C), BF16), pltpu.VMEM((n_in, N_SHARDS - 1, tr, C), BF16),
                        pltpu.VMEM((n_out, tr, C), F32), pltpu.SemaphoreType.DMA((2, n_in)), pltpu.SemaphoreType.DMA((n_out,))],
        compiler_params=_params(),
    )(place, *h4s, *recv3s)


def _by_shape(arrays):
    groups = {}
    for k, a in enumerate(arrays):
        groups.setdefault(a.shape, []).append(k)
    return list(groups.values())


def _place():
    x, y, c = lax.axis_index("x"), lax.axis_index("y"), lax.axis_index("c")
    chips = [(1 - x, y), (x, 1 - y), (1 - x, 1 - y)]
    return x, y, c, chips


HBM_SPEC = pl.BlockSpec(memory_space=pltpu.HBM)
SEM_SPEC = pl.BlockSpec(memory_space=pltpu.SEMAPHORE)
DATAFLOW = pltpu.SideEffectType.DATAFLOW_SIDE_EFFECTING


def _hbm(a):
    return pltpu.with_memory_space_constraint(a, pltpu.HBM)


class InFlight(NamedTuple):
    send_sem: jax.Array
    recv_sem: jax.Array
    bufs: list
    plan: Callable
    token: jax.Array


class Leg(NamedTuple):
    bufs: list
    landing: Optional[InFlight]
    plan: Optional[Callable]
    n_copies: int


def _send(bufs, plan, n_copies):
    return Leg([_hbm(b) for b in bufs], None, plan, n_copies)


def _land(flight):
    return Leg(flight.bufs, flight, None, 0)


def _forward(flight, plan, n_copies):
    return Leg(flight.bufs, flight, plan, n_copies)


def _wait_all(plan, refs, send_ref, recv_ref):
    for k, (src, dst, dev) in enumerate(plan(refs)):
        cp = pltpu.make_async_remote_copy(src_ref=src, dst_ref=dst, send_sem=send_ref.at[k], recv_sem=recv_ref.at[k],
                                          device_id=dev, device_id_type=MESH)
        cp.wait_send()
        cp.wait_recv()


def _start_all(plan, refs, send_ref, recv_ref):
    for k, (src, dst, dev) in enumerate(plan(refs)):
        pltpu.make_async_remote_copy(src_ref=src, dst_ref=dst, send_sem=send_ref.at[k], recv_sem=recv_ref.at[k],
                                     device_id=dev, device_id_type=MESH).start()


def _comm(name, legs, after):
    after = list(after) if isinstance(after, (list, tuple)) else [after]
    ins, in_specs, out_shape, out_specs, aliases, first = [], [], [], [], {}, []
    for leg in legs:
        first.append((len(ins), len(out_shape)))
        for b in leg.bufs:
            aliases[len(ins)] = len(out_shape)
            ins.append(b)
            in_specs.append(HBM_SPEC)
            out_shape.append(pltpu.HBM(b.shape, b.dtype))
            out_specs.append(HBM_SPEC)
        if leg.landing is not None:
            ins += [leg.landing.send_sem, leg.landing.recv_sem]
            in_specs += [SEM_SPEC, SEM_SPEC]
        if leg.plan is not None:
            out_shape += [pltpu.SemaphoreType.DMA((leg.n_copies,))] * 2
            out_specs += [SEM_SPEC, SEM_SPEC]
    starts = any(leg.plan is not None for leg in legs)
    if starts:
        out_shape.append(jax.ShapeDtypeStruct((8, 128), F32))
        out_specs.append(pl.BlockSpec(memory_space=pltpu.VMEM))
    n_in = len(ins) + len(after)

    def body(*refs):
        outs = refs[n_in:]
        for leg, (i, o) in zip(legs, first):
            nb = len(leg.bufs)
            bufs = refs[i:i + nb]
            if leg.landing is not None:
                _wait_all(leg.landing.plan, bufs, refs[i + nb], refs[i + nb + 1])
            if leg.plan is not None:
                _start_all(leg.plan, bufs, outs[o + nb], outs[o + nb + 1])
        if starts:
            outs[-1][...] = jnp.zeros_like(outs[-1])

    out = pl.pallas_call(
        body, name=name, in_specs=in_specs + [TOKEN_SPEC] * len(after), out_shape=out_shape, out_specs=out_specs,
        input_output_aliases=aliases, compiler_params=pltpu.CompilerParams(has_side_effects=DATAFLOW),
    )(*ins, *after)
    results = []
    for leg, (_, o) in zip(legs, first):
        nb = len(leg.bufs)
        bufs = list(out[o:o + nb])
        results.append(bufs if leg.plan is None else InFlight(out[o + nb], out[o + nb + 1], bufs, leg.plan, out[-1]))
    return results


def _half_rows(buf, chip, core):
    hr = buf.shape[0] // (2 * N_SHARDS)
    return buf.at[pl.ds(pl.multiple_of((2 * chip + core) * hr, 16), hr)]


def _gather_ici_plan(bufs):
    x, y, c, chips = _place()
    return [(_half_rows(b, 2 * x + y, c), _half_rows(b, 2 * x + y, c), (px, py, c)) for b in bufs for px, py in chips]


def _gather_d2d_plan(bufs):
    x, y, c, chips = _place()
    return [(_half_rows(b, 2 * px + py, c), _half_rows(b, 2 * px + py, c), (x, y, 1 - c)) for b in bufs for px, py in chips]


def _swap_plan(bufs):
    x, y, c, _ = _place()
    n = len(bufs) // 2
    copies = []
    for g, land in zip(bufs[:n], bufs[n:]):
        hr = g.shape[1] // 2
        copies.append((g.at[:, pl.ds(pl.multiple_of((1 - c) * hr, 8), hr)], land, (x, y, 1 - c)))
    return copies


def _exchange_plan(bufs):
    x, y, c, chips = _place()
    n = len(bufs) // 2
    return [(h.at[2 * px + py], land.at[k], (px, py, c))
            for h, land in zip(bufs[:n], bufs[n:]) for k, (px, py) in enumerate(chips)]


def _share_plan(bufs):
    x, y, c, _ = _place()
    copies = []
    for buf in bufs:
        hr = buf.shape[0] // 2
        mine = buf.at[pl.ds(pl.multiple_of(c * hr, 8), hr)]
        copies.append((mine, mine, (x, y, 1 - c)))
    return copies


N_DEVICES = 8


def _slot_place(vec, place):
    R, C = vec.shape

    def body(p_ref, v_ref, o_ref):
        o_ref[0] = v_ref[...]

    grid_spec = pltpu.PrefetchScalarGridSpec(
        num_scalar_prefetch=1, grid=(1,), in_specs=[pl.BlockSpec((R, C), lambda i, p: (0, 0))],
        out_specs=pl.BlockSpec((1, R, C), lambda i, p: (2 * p[0] + p[1], 0, 0)))
    return pl.pallas_call(
        body, name="slot_place", grid_spec=grid_spec, out_shape=jax.ShapeDtypeStruct((N_DEVICES, R, C), F32),
        compiler_params=_params(dimension_semantics=("arbitrary",)),
    )(place, vec)


def _slots_plan(bufs):
    x, y, c, _ = _place()
    mine = bufs[0].at[4 * x + 2 * y + c]
    return [(mine, mine, (x ^ (r >> 2), y ^ ((r >> 1) & 1), c ^ (r & 1))) for r in range(1, N_DEVICES)]


def _sum_slots(slots):
    _, R, C = slots.shape

    def body(s_ref, o_ref):
        acc = s_ref[0]
        for d in range(1, N_DEVICES):
            acc = acc + s_ref[d]
        o_ref[...] = acc

    return pl.pallas_call(
        body, name="sum_slots", in_specs=[pl.BlockSpec(memory_space=pltpu.VMEM)],
        out_specs=pl.BlockSpec(memory_space=pltpu.VMEM), out_shape=jax.ShapeDtypeStruct((R, C), F32),
        compiler_params=_params(),
    )(slots)


SMALL = ("ffn1_norm", "mix_norm", "pool_w", "pool_scale", "q_norm", "k_norm", "sinks", "gate_bias", "ffn2_norm")
SMALL_COLS = 1024
FFN1 = ("ffn1_w_gate", "ffn1_w_up", "ffn1_w_down")
MIXER = ("w_in", "w_pool_out", "w_attn_out", "w_out")
FFN2 = ("ffn2_w_gate", "ffn2_w_up", "ffn2_w_down")
LARGE = FFN1 + MIXER + FFN2
TRANSPOSED = ("ffn1_w_gate", "ffn1_w_up", "w_in", "w_pool_out", "ffn2_w_gate", "ffn2_w_up")
TURNED = ("w_pool_out",)
WEIGHTS = ("ffn1_norm", "ffn1_w_gate", "ffn1_w_up", "ffn1_w_down", "mix_norm", "w_in", "pool_w", "pool_scale",
           "w_pool_out", "q_norm", "k_norm", "sinks", "w_attn_out", "gate_bias", "w_out", "ffn2_norm",
           "ffn2_w_gate", "ffn2_w_up", "ffn2_w_down")


def _pack_small(parts):
    flat = jnp.concatenate([p.reshape(-1) for p in parts])
    rows = -(-flat.shape[0] // (8 * SMALL_COLS)) * 8
    return jnp.pad(flat, (0, rows * SMALL_COLS - flat.shape[0])).reshape(rows, SMALL_COLS)


def _unpack_small(packed, like):
    flat = packed.reshape(-1)
    out, off = [], 0
    for p in like:
        out.append(flat[off:off + p.size].reshape(p.shape))
        off += p.size
    return out, flat[off]


def _tie(small, token):
    return small + token[0, 0]


class Reduction:
    def __init__(self, names, place):
        self.names, self.place = names, place

    def swap(self, grads):
        grads = [g.reshape(N_SHARDS, -1, g.shape[-1]) for g in grads]
        lands = [lax.empty((g.shape[0], g.shape[1] // 2, g.shape[2]), g.dtype) for g in grads]
        return _send(grads + lands, _swap_plan, len(grads))

    def exchange(self, swapped):
        n = len(self.names)
        halves = self._per_shape(_sum_halves, "sum_halves", swapped[:n], swapped[n:])
        lands = [lax.empty((N_SHARDS - 1,) + h.shape[1:], h.dtype) for h in halves]
        return _send(halves + lands, _exchange_plan, (N_SHARDS - 1) * n)

    def share(self, exchanged):
        n = len(self.names)
        return _send(self._per_shape(_sum_quarters, "sum_quarters", exchanged[:n], exchanged[n:]), _share_plan, n)

    def named(self, shared):
        return dict(zip(self.names, shared))

    def _per_shape(self, add, stage, mine, received):
        out = [None] * len(mine)
        for idx in _by_shape(mine):
            sums = add([mine[k] for k in idx], [received[k] for k in idx], self.place, f"{stage}_{self.names[idx[0]]}")
            for k, v in zip(idx, sums):
                out[k] = v
        return out


def _row_form(name, a):
    return a.T if name in TRANSPOSED else a


def _shard_form(name, a):
    return a if name in TURNED else _row_form(name, a)


def kernel(x, ffn1_norm, ffn1_w_gate, ffn1_w_up, ffn1_w_down, mix_norm, w_in, pool_w, pool_scale, w_pool_out, q_norm, k_norm, sinks, w_attn_out, gate_bias, w_out, ffn2_norm, ffn2_w_gate, ffn2_w_up, ffn2_w_down, loss_target, m_ffn1_norm, m_ffn1_w_gate, m_ffn1_w_up, m_ffn1_w_down, m_mix_norm, m_w_in, m_pool_w, m_pool_scale, m_w_pool_out, m_q_norm, m_k_norm, m_sinks, m_w_attn_out, m_gate_bias, m_w_out, m_ffn2_norm, m_ffn2_w_gate, m_ffn2_w_up, m_ffn2_w_down, v_ffn1_norm, v_ffn1_w_gate, v_ffn1_w_up, v_ffn1_w_down, v_mix_norm, v_w_in, v_pool_w, v_pool_scale, v_w_pool_out, v_q_norm, v_k_norm, v_sinks, v_w_attn_out, v_gate_bias, v_w_out, v_ffn2_norm, v_ffn2_w_gate, v_ffn2_w_up, v_ffn2_w_down):
    args = dict(locals())
    wts = {n: _shard_form(n, args[n]) for n in WEIGHTS}
    mom = {n: _shard_form(n, args["m_" + n]) for n in WEIGHTS}
    var = {n: _shard_form(n, args["v_" + n]) for n in WEIGHTS}
    shard = 2 * lax.axis_index("x") + lax.axis_index("y")
    place = jnp.stack([shard, lax.axis_index("c")]).astype(jnp.int32)

    xs, target = x[0], loss_target[0]
    D = xs.shape[1]
    g1 = wts["ffn1_norm"].reshape(1, D)
    gm = wts["mix_norm"].reshape(1, D)
    g2 = wts["ffn2_norm"].reshape(1, D)
    qw = wts["q_norm"].reshape(1, HEAD_DIM)
    kw = wts["k_norm"].reshape(1, HEAD_DIM)
    bias = wts["gate_bias"].reshape(1, 2 * D)
    pscale = wts["pool_scale"].reshape(1, -1)
    pw = wts["pool_w"].astype(BF16)
    sink_rows = jnp.broadcast_to(jnp.repeat(wts["sinks"], ATTN_BLOCK).reshape(N_KV_HEADS, GQA_GROUP * ATTN_BLOCK, 1),
                                 (N_KV_HEADS, GQA_GROUP * ATTN_BLOCK, 128))

    gate1, up1, down1 = FFN1[:1], FFN1[1:2], FFN1[2:]
    mix_in, mix_out = MIXER[:1], MIXER[1:]

    def over_ici(names):
        return _send([placed[n] for n in names], _gather_ici_plan, 3 * len(names))

    def to_sibling(flight):
        return _forward(flight, _gather_d2d_plan, len(flight.bufs) * 3)

    placed = {n: _cast_place(wts[n], place, "cast_" + n) for n in gate1}
    gate1_ici, = _comm("gather_gate1_ici", [over_ici(gate1)], place)
    placed.update({n: _cast_place(wts[n], place, "cast_" + n) for n in up1 + down1})
    up1_ici, down1_ici = _comm("gather_ffn1_ici", [over_ici(up1), over_ici(down1)], gate1_ici.token)
    placed.update({n: _cast_place(wts[n], place, "cast_" + n, turned=n in TURNED) for n in MIXER})
    band = _band_bias()
    shadow = [sink_rows, pw, band, down1_ici.token]
    gate1_d2d, in_ici, out_ici = _comm(
        "gather_mix_ici", [to_sibling(gate1_ici), over_ici(mix_in), over_ici(mix_out)], shadow)
    w = dict(zip(gate1, _comm("gather_gate1_wait", [_land(gate1_d2d)], gate1_d2d.token)[0]))

    n1, a1 = _ffn_gate(xs, g1, w["ffn1_w_gate"], "ffn1_gate")
    placed.update({n: _cast_place(wts[n], place, "cast_" + n, after=[a1]) for n in FFN2})
    up1_d2d, ffn2_ici, down1_d2d = _comm(
        "gather_ffn1_d2d", [to_sibling(up1_ici), over_ici(FFN2), to_sibling(down1_ici)], a1)
    w.update(zip(up1, _comm("gather_up1_wait", [_land(up1_d2d)], up1_d2d.token)[0]))
    p1, q1, s1 = _ffn_hidden(n1, a1, w["ffn1_w_up"], "ffn1_hidden")
    w.update(zip(down1, _comm("gather_down1_wait", [_land(down1_d2d)], s1)[0]))
    h1 = _ffn_down(xs, s1, w["ffn1_w_down"], "ffn1_down")
    in_d2d, out_d2d = _comm("gather_mix_d2d", [to_sibling(in_ici), to_sibling(out_ici)], h1)
    w.update(zip(mix_in, _comm("gather_in_wait", [_land(in_d2d)], in_d2d.token)[0]))
    u, xp, q, kv, gl = _mix_proj(h1, gm, w["w_in"])
    attn = _attn_fwd(q, kv, qw, kw, sink_rows, band)
    ffn2_d2d, landed = _comm("gather_ffn2_d2d", [to_sibling(ffn2_ici), _land(out_d2d)], attn)
    w.update(zip(mix_out, landed))
    h2, bp, ba = _mix_out(xp, attn, gl, bias, pw, pscale, w["w_pool_out"], w["w_attn_out"], w["w_out"], h1)
    w.update(zip(FFN2, _comm("gather_ffn2_wait", [_land(ffn2_d2d)], h2)[0]))
    n2, p2, q2, s2 = _ffn_up(h2, g2, w["ffn2_w_gate"], w["ffn2_w_up"], "ffn2_up")
    dy, dyh, loss = _ffn_down_loss(h2, s2, w["ffn2_w_down"], target, "ffn2_down_loss")

    gw, gs = {}, {}
    da, db = _ffn_bwd_h(dyh, p2, q2, w["ffn2_w_down"], ffn2_d2d.token, "ffn2_bwd_h")
    gw["ffn2_w_gate"], gw["ffn2_w_up"] = _xty(da, n2, "ffn2_dw_gate"), _xty(db, n2, "ffn2_dw_up")
    gw["ffn2_w_down"] = _xty(s2, dyh, "ffn2_dw_down")
    red2 = Reduction(FFN2, place)
    swap, = _comm("reduce_ffn2_swap", [red2.swap([gw[n] for n in FFN2])], da)
    dh2, gs["ffn2_norm"] = _ffn_bwd_x(da, db, w["ffn2_w_gate"], w["ffn2_w_up"], h2, dy, _tie(g2, swap.token), "ffn2_bwd_x")
    swapped, = _comm("reduce_ffn2_swap_wait", [_land(swap)], dh2)
    exchange, = _comm("reduce_ffn2_exchange", [red2.exchange(swapped)], place)
    dgl, dbp, dba, gw["w_out"], gs["gate_bias"] = _mix_bwd_gate(dh2, w["w_out"], bp, ba, gl, _tie(bias, exchange.token))
    dattn, dpooled, gw["w_attn_out"], gw["w_pool_out"], gs["pool_w"], gs["pool_scale"] = _mix_bwd_branch(
        dbp, dba, attn, xp, pw, pscale, w["w_pool_out"], w["w_attn_out"])
    dq, dkv, gs["q_norm"], gs["k_norm"], dsk = _attn_bwd(q, kv, dattn, qw, kw, sink_rows, band)
    gs["sinks"] = dsk[:, :, 0]
    exchanged, = _comm("reduce_ffn2_exchange_wait", [_land(exchange)], dq)
    share2, = _comm("reduce_ffn2_share", [red2.share(exchanged)], place)
    pieces = (_pool_bwd(dpooled), dq, dkv, dgl)
    gw["w_in"] = _in_bwd_w(pieces, u)
    redm = Reduction(MIXER, place)
    swap, shared = _comm("reduce_mix_swap", [redm.swap([gw[n] for n in MIXER]), _land(share2)], share2.token)
    grads = red2.named(shared)
    dh1, dh1h, gs["mix_norm"] = _in_bwd_x(pieces, w["w_in"], h1, dh2, _tie(gm, swap.token))
    red1, swaps = Reduction(FFN1, place), {}
    dw_down = _xty(s1, dh1h, "ffn1_dw_down")
    swaps["ffn1_w_down"], swapped = _comm("reduce_mix_swap_wait", [red1.swap([dw_down]), _land(swap)], dh1)
    exchange, = _comm("reduce_mix_exchange", [redm.exchange(swapped)], place)
    da, db = _ffn_bwd_h(dh1h, p1, q1, w["ffn1_w_down"], exchange.token, "ffn1_bwd_h")
    dw_up = _xty(db, n1, "ffn1_dw_up")
    swaps["ffn1_w_up"], = _comm("reduce_ffn1_swap_up", [red1.swap([dw_up])], place)
    dw_gate = _xty(da, n1, "ffn1_dw_gate", after=[swaps["ffn1_w_up"].token])
    swaps["ffn1_w_gate"], exchanged = _comm("reduce_mix_exchange_wait", [red1.swap([dw_gate]), _land(exchange)], place)
    sharem, *landed = _comm("reduce_mix_share", [redm.share(exchanged)] + [_land(swaps[n]) for n in FFN1], place)
    exchange, = _comm("reduce_ffn1_exchange", [red1.exchange([l[0] for l in landed] + [l[1] for l in landed])], place)
    grad_x, gs["ffn1_norm"] = _ffn_bwd_x(da, db, w["ffn1_w_gate"], w["ffn1_w_up"], xs, dh1, _tie(g1, exchange.token), "ffn1_bwd_x")

    small_parts = [gs[n] for n in SMALL] + [loss[0, 0].reshape(1)]
    slots, shared = _comm(
        "gather_small", [_send([_slot_place(_pack_small(small_parts), place)], _slots_plan, N_DEVICES - 1), _land(sharem)],
        grad_x)
    grads.update(redm.named(shared))
    delta, new_m, new_v = {}, {}, {}

    def adamw(names, token):
        for n in names:
            delta[n], new_m[n], new_v[n], grads[n] = _adamw(wts[n], grads[n], mom[n], var[n], token, "adamw_" + n, n in TURNED)

    early, late = FFN2 + MIXER[:-1], MIXER[-1:]
    adamw(early, slots.token)
    exchanged, = _comm("reduce_ffn1_exchange_wait", [_land(exchange)], [delta[n] for n in early])
    share1, landed = _comm("reduce_ffn1_share", [red1.share(exchanged), _land(slots)], place)
    summed = _sum_slots(landed[0])
    small_grads, loss_sum = _unpack_small(summed, [wts[n] for n in SMALL])
    grads.update(dict(zip(SMALL, small_grads)))
    for out, vals in zip((delta, new_m, new_v), _adamw_small(*[[d[n] for n in SMALL] for d in (wts, grads, mom, var)])):
        out.update(dict(zip(SMALL, vals)))
    adamw(late, share1.token)
    results = {(k, n): _shard_form(n, d[n]) for k, d in enumerate((grads, delta, new_m, new_v)) for n in SMALL + FFN2 + MIXER}
    shared, = _comm("reduce_ffn1_share_wait", [_land(share1)], [results[k, n] for k in range(4) for n in SMALL + late])
    grads.update(red1.named(shared))
    adamw(FFN1, share1.token)
    results.update({(k, n): _shard_form(n, d[n]) for k, d in enumerate((grads, delta, new_m, new_v)) for n in FFN1})
    return (loss_sum, grad_x[None], *[results[k, n] for k in range(4) for n in WEIGHTS])
```
